```python
import jax, jax.numpy as jnp
from jax import lax
import numpy as np

D_MODEL = 1024
BATCH = 8
SEQ = 2048
DEPTH = 1

N_META = 16
CHUNK = 128
D_MIX = 2 * D_MODEL
RET_WIDTH = D_MIX // 2
RET_HEADS = 4
RET_DV = RET_WIDTH // RET_HEADS
RET_DK = RET_DV // 2
FOX_WIDTH = D_MIX - RET_WIDTH
FOX_HEAD_DIM = 64
FOX_HEADS = FOX_WIDTH // FOX_HEAD_DIM
ROPE_BASE = 10000.0
EPS = 1e-6
NEG_INF = -1e30
SPLIT_SIZES = (RET_HEADS * RET_DK, RET_HEADS * RET_DK, RET_WIDTH, RET_WIDTH,
               FOX_WIDTH, FOX_WIDTH, FOX_WIDTH, FOX_WIDTH, FOX_HEADS)
D_IN_PROJ = 2 * RET_HEADS * RET_DK + 2 * RET_WIDTH + 4 * FOX_WIDTH + FOX_HEADS

kernel_name = "hymba_retention_fox_hybrid"


def rmsnorm(x, g):
    xf = x.astype(jnp.float32)
    y = xf * lax.rsqrt(jnp.mean(xf * xf, axis=-1, keepdims=True) + EPS)
    return (y * g.astype(jnp.float32)).astype(x.dtype)


def head_rmsnorm(x):
    xf = x.astype(jnp.float32)
    y = xf * lax.rsqrt(jnp.mean(xf * xf, axis=-1, keepdims=True) + EPS)
    return y.astype(x.dtype)


def rotary(x, pos):
    d = x.shape[-1]
    inv = ROPE_BASE ** (-jnp.arange(0, d, 2, dtype=jnp.float32) / d)
    ang = pos[:, None] * inv[None, :]
    cos = jnp.cos(ang)[None, :, None, :]
    sin = jnp.sin(ang)[None, :, None, :]
    xf = x.astype(jnp.float32)
    x1, x2 = xf[..., : d // 2], xf[..., d // 2:]
    out = jnp.concatenate([x1 * cos - x2 * sin, x1 * sin + x2 * cos], axis=-1)
    return out.astype(x.dtype)


def retention_chunkwise(q, k, v):
    b, t, h, dk = q.shape
    dv = v.shape[-1]
    nc = t // CHUNK
    log_gamma = jnp.log1p(-jnp.exp2(-5.0 - jnp.arange(h, dtype=jnp.float32)))
    q = q.reshape(b, nc, CHUNK, h, dk)
    k = k.reshape(b, nc, CHUNK, h, dk)
    v = v.reshape(b, nc, CHUNK, h, dv)
    idx = jnp.arange(CHUNK, dtype=jnp.float32)
    diff = idx[:, None] - idx[None, :]
    dmask = jnp.where(diff[None] >= 0,
                      jnp.exp(log_gamma[:, None, None] * jnp.maximum(diff, 0.0)[None]),
                      0.0)
    scores = jnp.einsum('bnchk,bnshk->bnhcs', q, k) * dmask.astype(q.dtype)
    out_intra = jnp.einsum('bnhcs,bnshv->bnchv', scores, v)
    zeta = jnp.exp(log_gamma[:, None] * (CHUNK - 1.0 - idx)[None, :])
    kv = jnp.einsum('bnchk,bnchv,hc->nbhkv', k, v, zeta.astype(k.dtype)).astype(jnp.float32)
    chunk_decay = jnp.exp(log_gamma * CHUNK)[None, :, None, None]

    def step(state, kv_i):
        return state * chunk_decay + kv_i, state

    _, s_prev = lax.scan(step, jnp.zeros((b, h, dk, dv), jnp.float32), kv)
    xi = jnp.exp(log_gamma[:, None] * (idx + 1.0)[None, :])
    out_inter = jnp.einsum('bnchk,nbhkv,hc->bnchv', q, s_prev.astype(q.dtype), xi.astype(q.dtype))
    return (out_intra + out_inter).reshape(b, t, h, dv)


def forgetting_attention(q, k, v, log_f, valid):
    b, t, h, d = q.shape
    c = jnp.cumsum(log_f, axis=1).transpose(0, 2, 1)
    scale = d ** -0.5
    outs = []
    for i in range(t // CHUNK):
        lo, hi = i * CHUNK, (i + 1) * CHUNK
        s = jnp.einsum('bthd,bshd->bhts', q[:, lo:hi], k[:, :hi]).astype(jnp.float32) * scale
        s = s + (c[:, :, lo:hi, None] - c[:, :, None, :hi])
        tpos = jnp.arange(lo, hi)
        spos = jnp.arange(hi)
        mask = (spos[None, :] <= tpos[:, None]) & valid[None, :hi]
        s = jnp.where(mask[None, None], s, NEG_INF)
        p = jax.nn.softmax(s, axis=-1)
        outs.append(jnp.einsum('bhts,bshd->bthd', p.astype(v.dtype), v[:, :hi]))
    return jnp.concatenate(outs, axis=1)


def hybrid_layer(h_res, norm_g, w_in, b_f, w_out):
    b, l, _ = h_res.shape
    pad = CHUNK - N_META
    t = l + pad
    u = rmsnorm(h_res, norm_g)
    u = jnp.pad(u, ((0, 0), (pad, 0), (0, 0)))
    z = u @ w_in
    offsets = np.cumsum(SPLIT_SIZES)[:-1].tolist()
    rq, rk, rv, rg, fq, fk, fv, fg, ff = jnp.split(z, offsets, axis=-1)
    pos = jnp.arange(t, dtype=jnp.float32) - pad
    valid = jnp.arange(t) >= pad
    rq = rotary(rq.reshape(b, t, RET_HEADS, RET_DK), pos)
    rk = rotary(rk.reshape(b, t, RET_HEADS, RET_DK), pos) * (RET_DK ** -0.5)
    y_r = retention_chunkwise(rq, rk, rv.reshape(b, t, RET_HEADS, RET_DV))
    y_r = head_rmsnorm(y_r).reshape(b, t, RET_WIDTH) * jax.nn.silu(rg)
    log_f = jnp.where(valid[None, :, None],
                      jax.nn.log_sigmoid((ff + b_f).astype(jnp.float32)), 0.0)
    y_f = forgetting_attention(fq.reshape(b, t, FOX_HEADS, FOX_HEAD_DIM),
                               fk.reshape(b, t, FOX_HEADS, FOX_HEAD_DIM),
                               fv.reshape(b, t, FOX_HEADS, FOX_HEAD_DIM),
                               log_f, valid)
    y_f = y_f.reshape(b, t, FOX_WIDTH) * jax.nn.silu(fg)
    y = jnp.concatenate([y_r, y_f], axis=-1)[:, pad:]
    return h_res + y @ w_out


def _fwd_setup_inputs(seed: int = 0) -> dict:
    key = jax.random.key(seed)
    k = jax.random.split(key, 7)
    x = jax.random.normal(k[0], (BATCH, SEQ, D_MODEL), jnp.float32)
    meta_tokens = jax.random.normal(k[1], (N_META, D_MODEL), jnp.float32)
    norm_g = 1.0 + 0.02 * jax.random.normal(k[2], (DEPTH, D_MODEL), jnp.float32)
    w_in = jax.random.normal(k[3], (DEPTH, D_MODEL, D_IN_PROJ), jnp.float32) * D_MODEL ** -0.5
    b_f = jax.random.uniform(k[4], (DEPTH, FOX_HEADS), jnp.float32, 1.0, 4.0)
    w_out = jax.random.normal(k[5], (DEPTH, D_MIX, D_MODEL), jnp.float32) * D_MIX ** -0.5
    final_g = 1.0 + 0.02 * jax.random.normal(k[6], (D_MODEL,), jnp.float32)
    return {"x": x, "meta_tokens": meta_tokens, "norm_g": norm_g, "w_in": w_in,
            "b_f": b_f, "w_out": w_out, "final_g": final_g}


def _fwd_reference(x, meta_tokens, norm_g, w_in, b_f, w_out, final_g):
    b = x.shape[0]
    meta = jnp.broadcast_to(meta_tokens[None].astype(x.dtype), (b, N_META, D_MODEL))
    h = jnp.concatenate([meta, x], axis=1)
    for layer in range(DEPTH):
        h = hybrid_layer(h, norm_g[layer], w_in[layer], b_f[layer], w_out[layer])
    return rmsnorm(h[:, N_META:], final_g)


import jax as _jax
import jax.numpy as _jnp

TWIN_FORMAT = 'train_step'
FWD_PARAMS = ['x', 'meta_tokens', 'norm_g', 'w_in', 'b_f', 'w_out', 'final_g']
TWIN_WEIGHTS = ['meta_tokens', 'norm_g', 'w_in', 'b_f', 'w_out', 'final_g']
TWIN_DIFF_INPUT = 'x'
TWIN_INPUTS = ['x', 'meta_tokens', 'norm_g', 'w_in', 'b_f', 'w_out', 'final_g', 'loss_target', 'm_meta_tokens', 'm_norm_g', 'm_w_in', 'm_b_f', 'm_w_out', 'm_final_g', 'v_meta_tokens', 'v_norm_g', 'v_w_in', 'v_b_f', 'v_w_out', 'v_final_g']
TWIN_OUTPUTS = ['loss', 'grad_x', 'grad_meta_tokens', 'grad_norm_g', 'grad_w_in', 'grad_b_f', 'grad_w_out', 'grad_final_g', 'delta_meta_tokens', 'delta_norm_g', 'delta_w_in', 'delta_b_f', 'delta_w_out', 'delta_final_g', 'new_m_meta_tokens', 'new_m_norm_g', 'new_m_w_in', 'new_m_b_f', 'new_m_w_out', 'new_m_final_g', 'new_v_meta_tokens', 'new_v_norm_g', 'new_v_w_in', 'new_v_b_f', 'new_v_w_out', 'new_v_final_g']
TWIN_LEAF_KINDS = {'loss': 'loss', 'grad_x': 'grad_x', 'grad_meta_tokens': 'grad_w', 'grad_norm_g': 'grad_w', 'grad_w_in': 'grad_w', 'grad_b_f': 'grad_w', 'grad_w_out': 'grad_w', 'grad_final_g': 'grad_w', 'delta_meta_tokens': 'delta_w', 'delta_norm_g': 'delta_w', 'delta_w_in': 'delta_w', 'delta_b_f': 'delta_w', 'delta_w_out': 'delta_w', 'delta_final_g': 'delta_w', 'new_m_meta_tokens': 'new_m', 'new_m_norm_g': 'new_m', 'new_m_w_in': 'new_m', 'new_m_b_f': 'new_m', 'new_m_w_out': 'new_m', 'new_m_final_g': 'new_m', 'new_v_meta_tokens': 'new_v', 'new_v_norm_g': 'new_v', 'new_v_w_in': 'new_v', 'new_v_b_f': 'new_v', 'new_v_w_out': 'new_v', 'new_v_final_g': 'new_v'}


def _forward(args):
    return _fwd_reference(*[args[k] for k in FWD_PARAMS])


def _output_shape():
    out = _jax.eval_shape(lambda: _forward(_fwd_setup_inputs(0)))
    return out.shape, out.dtype

N_MICROBATCH = 1
ADAM_LR = 0.001
ADAM_B1 = 0.9
ADAM_B2 = 0.999
ADAM_EPS = 1e-08
ADAM_WD = 0.01
ADAM_STEP = 10
PER_EXAMPLE_BATCH_AXIS = {'x': 0, 'loss_target': 0}
SHARED_INPUTS = []
_WEIGHT_DTYPES = {'meta_tokens': _jnp.float32, 'norm_g': _jnp.float32, 'w_in': _jnp.float32, 'b_f': _jnp.float32, 'w_out': _jnp.float32, 'final_g': _jnp.float32}
MOMENT_SCALE = {'meta_tokens': 5.162122e-03, 'norm_g': 1.055557e-01, 'w_in': 3.900800e-02, 'b_f': 8.723227e-02, 'w_out': 5.149641e-02, 'final_g': 1.598136e+01}


def _to_microbatches(a, axis):
    t = _jnp.moveaxis(a, axis, 0)
    t = t.reshape((N_MICROBATCH, t.shape[0] // N_MICROBATCH) + t.shape[1:])
    return _jnp.moveaxis(t, 1, axis + 1)


def setup_inputs(seed: int = 0) -> dict:
    inp = _fwd_setup_inputs(seed)
    key = _jax.random.fold_in(_jax.random.key(seed), 7919)
    shape, _ = _output_shape()
    out = dict(inp)
    out["loss_target"] = _jax.random.normal(_jax.random.fold_in(key, 0), shape, _jnp.float32)
    for i, name in enumerate(TWIN_WEIGHTS):
        w = inp[name].astype(_jnp.float32)
        if MOMENT_SCALE is None:
            s = _jnp.sqrt(_jnp.mean(_jnp.square(w)) + 1e-30)
        else:
            s = MOMENT_SCALE[name]
        km, kv = _jax.random.split(_jax.random.fold_in(key, i + 1))
        out[name] = w
        out["m_" + name] = s * _jax.random.normal(km, w.shape, _jnp.float32)
        out["v_" + name] = (s * s) * _jax.random.uniform(kv, w.shape, _jnp.float32, 0.5, 1.5)
    if N_MICROBATCH > 1:
        for name, axis in PER_EXAMPLE_BATCH_AXIS.items():
            out[name] = _to_microbatches(out[name], axis)
    return {'x': out['x'], 'meta_tokens': out['meta_tokens'], 'norm_g': out['norm_g'], 'w_in': out['w_in'], 'b_f': out['b_f'], 'w_out': out['w_out'], 'final_g': out['final_g'], 'loss_target': out['loss_target'], 'm_meta_tokens': out['m_meta_tokens'], 'm_norm_g': out['m_norm_g'], 'm_w_in': out['m_w_in'], 'm_b_f': out['m_b_f'], 'm_w_out': out['m_w_out'], 'm_final_g': out['m_final_g'], 'v_meta_tokens': out['v_meta_tokens'], 'v_norm_g': out['v_norm_g'], 'v_w_in': out['v_w_in'], 'v_b_f': out['v_b_f'], 'v_w_out': out['v_w_out'], 'v_final_g': out['v_final_g']}


def _loss(weights, diff, rest, loss_target):
    with _jax.named_scope("forward"):
        args = {**rest, TWIN_DIFF_INPUT: diff, **{k: w.astype(_WEIGHT_DTYPES[k]) for k, w in weights.items()}}
        y = _forward(args)
    with _jax.named_scope("loss_head"):
        err = _jnp.square(y.astype(_jnp.float32) - loss_target)
        return 0.5 * _jnp.sum(_jnp.mean(err, axis=-1)) if err.ndim else 0.5 * err


def _adamw(w, g, m, v):
    m = ADAM_B1 * m + (1.0 - ADAM_B1) * g
    v = ADAM_B2 * v + (1.0 - ADAM_B2) * _jnp.square(g)
    m_hat = m / (1.0 - ADAM_B1 ** ADAM_STEP)
    v_hat = v / (1.0 - ADAM_B2 ** ADAM_STEP)
    delta = -ADAM_LR * (m_hat / (_jnp.sqrt(v_hat) + ADAM_EPS) + ADAM_WD * w)
    return delta, m, v


def reference(x, meta_tokens, norm_g, w_in, b_f, w_out, final_g, loss_target, m_meta_tokens, m_norm_g, m_w_in, m_b_f, m_w_out, m_final_g, v_meta_tokens, v_norm_g, v_w_in, v_b_f, v_w_out, v_final_g):
    given = dict(x=x, meta_tokens=meta_tokens, norm_g=norm_g, w_in=w_in, b_f=b_f, w_out=w_out, final_g=final_g, loss_target=loss_target, m_meta_tokens=m_meta_tokens, m_norm_g=m_norm_g, m_w_in=m_w_in, m_b_f=m_b_f, m_w_out=m_w_out, m_final_g=m_final_g, v_meta_tokens=v_meta_tokens, v_norm_g=v_norm_g, v_w_in=v_w_in, v_b_f=v_b_f, v_w_out=v_w_out, v_final_g=v_final_g)
    weights = {n: given[n] for n in TWIN_WEIGHTS}
    shared = {n: given[n] for n in SHARED_INPUTS}
    per_example = {n: given[n] for n in ['x']}
    grad_fn = _jax.value_and_grad(_loss, argnums=(0, 1))

    def one_microbatch(ex, loss_target):
        ex = dict(ex)
        diff = ex.pop(TWIN_DIFF_INPUT)
        return grad_fn(weights, diff, {**shared, **ex}, loss_target)

    if N_MICROBATCH == 1:
        loss, (grad_w, grad_x) = one_microbatch(per_example, given["loss_target"])
    else:
        def body(carry, xs):
            loss_sum, grad_sum = carry
            l_k, (gw_k, gx_k) = one_microbatch(xs[0], xs[1])
            with _jax.named_scope("update"):
                return (loss_sum + l_k, _jax.tree.map(_jnp.add, grad_sum, gw_k)), gx_k

        init = (_jnp.zeros((), _jnp.float32), _jax.tree.map(_jnp.zeros_like, weights))
        (loss, grad_w), grad_x = _jax.lax.scan(body, init, (per_example, given["loss_target"]))
    with _jax.named_scope("update"):
        delta_w, new_m, new_v = {}, {}, {}
        for n in TWIN_WEIGHTS:
            delta_w[n], new_m[n], new_v[n] = _adamw(weights[n], grad_w[n], given["m_" + n], given["v_" + n])
    return (loss, grad_x, *[grad_w[n] for n in TWIN_WEIGHTS], *[delta_w[n] for n in TWIN_WEIGHTS],
            *[new_m[n] for n in TWIN_WEIGHTS], *[new_v[n] for n in TWIN_WEIGHTS])
```

```python
import functools

import numpy as np
import jax
import jax.numpy as jnp
from jax import lax
from jax.experimental import pallas as pl
from jax.experimental.pallas import tpu as pltpu

D = 1024
SEQ = 2048
NMETA = 16
C = 128
PAD = C - NMETA
T = PAD + NMETA + SEQ
NCH = T // C
RH, RDK, RDV = 4, 128, 256
FH, FD = 16, 64
NPAIR = FH // 2
WMAIN = 7168
NFF = 16
WIN = WMAIN + NFF
WSH = WIN // 4
DMIX = 2048
EPS = 1e-6
NEG = -1e30
RSCALE = RDK ** -0.5
FSCALE = FD ** -0.5
ROPE_BASE = 10000.0
LR, B1, B2, AEPS, WD, STEP = 0.001, 0.9, 0.999, 1e-08, 0.01, 10

BF = jnp.bfloat16
F32 = jnp.float32
NT = (((1,), (1,)), ((), ()))
TN = (((0,), (0,)), ((), ()))
MESH = pl.DeviceIdType.MESH
ANY = pl.BlockSpec(memory_space=pl.ANY)
VMEM_LIMIT = 48 * 1024 * 1024

QB_R, KB_R = 0, 4
VB_R = 4
GB_R, GB_F = 2, 6
QB_F, KB_F, VB_F = 24, 32, 40


def _dot(a, b):
    return jnp.dot(a, b, preferred_element_type=F32)


def _dg(a, b, dims):
    return lax.dot_general(a, b, dims, preferred_element_type=F32)


def _params(sem=None):
    return pltpu.CompilerParams(dimension_semantics=sem, vmem_limit_bytes=VMEM_LIMIT)


def _constants():
    pos = jnp.arange(T, dtype=F32) - PAD
    inv = ROPE_BASE ** (-jnp.arange(0, RDK, 2, dtype=F32) / RDK)
    ang = pos[:, None] * inv[None, :]
    cos, sin = jnp.cos(ang), jnp.sin(ang)
    cos2 = jnp.concatenate([cos, cos], axis=1)
    sin2 = jnp.concatenate([-sin, sin], axis=1)
    log_gamma = jnp.log1p(-jnp.exp2(-5.0 - jnp.arange(RH, dtype=F32)))
    idx = jnp.arange(C, dtype=F32)
    diff = idx[:, None] - idx[None, :]
    dmask = jnp.where(diff[None] >= 0, jnp.exp(log_gamma[:, None, None] * jnp.maximum(diff, 0.0)[None]), 0.0)
    zeta = jnp.exp(log_gamma[:, None] * (C - 1.0 - idx)[None, :])
    xi = jnp.exp(log_gamma[:, None] * (idx + 1.0)[None, :])
    gdec = jnp.exp(log_gamma * C)
    zeta_b = jnp.broadcast_to(zeta[:, :, None], (RH, C, RDK))
    xi_b = jnp.broadcast_to(xi[:, :, None], (RH, C, RDK))
    gdec_b = jnp.broadcast_to(gdec[:, None, None], (RH, RDK, RDV))
    tri = jnp.asarray(np.tril(np.ones((C, C), np.float32)))
    return dict(cos2=cos2, sin2=sin2, dmask=dmask, zeta=zeta_b, xi=xi_b, gdec=gdec_b, tri=tri)


def _norm_in(hpad, g):
    def body(h_ref, g_ref, u_ref, ut_ref):
        h = h_ref[...]
        rs = lax.rsqrt(jnp.mean(h * h, axis=1, keepdims=True) + EPS)
        u = h * rs * g_ref[...]
        u_ref[...] = u.astype(BF)
        ut_ref[...] = u.T.astype(BF)

    return pl.pallas_call(
        body, name="norm_in", grid=(NCH,),
        in_specs=[pl.BlockSpec((C, D), lambda i: (i, 0)), pl.BlockSpec((1, D), lambda i: (0, 0))],
        out_specs=[pl.BlockSpec((C, D), lambda i: (i, 0)), pl.BlockSpec((D, C), lambda i: (0, i))],
        out_shape=[jax.ShapeDtypeStruct((T, D), BF), jax.ShapeDtypeStruct((D, T), BF)],
        compiler_params=_params(("parallel",)),
    )(hpad, g)


def _mm_nn(a, b, tm, tn, name):
    m, k = a.shape
    _, n = b.shape

    def body(a_ref, b_ref, o_ref):
        o_ref[...] = _dot(a_ref[...], b_ref[...])

    return pl.pallas_call(
        body, name=name, grid=(m // tm, n // tn),
        in_specs=[pl.BlockSpec((tm, k), lambda i, j: (i, 0)), pl.BlockSpec((k, tn), lambda i, j: (0, j))],
        out_specs=pl.BlockSpec((tm, tn), lambda i, j: (i, j)),
        out_shape=jax.ShapeDtypeStruct((m, n), F32),
        compiler_params=_params(("parallel", "parallel")),
    )(a, b)


def _rot(x, cos2, sin2):
    return x * cos2 + pltpu.roll(x, 64, 1) * sin2


def _ret_specs():
    def specs(chunk):
        return [
            pl.BlockSpec((C, RDK), lambda h, n: (chunk(n), QB_R + h)),
            pl.BlockSpec((C, RDK), lambda h, n: (chunk(n), KB_R + h)),
            pl.BlockSpec((C, RDV), lambda h, n: (chunk(n), VB_R + h)),
            pl.BlockSpec((C, RDK), lambda h, n: (chunk(n), 0)),
            pl.BlockSpec((C, RDK), lambda h, n: (chunk(n), 0)),
            pl.BlockSpec((1, C, C), lambda h, n: (h, 0, 0)),
            pl.BlockSpec((1, C, RDK), lambda h, n: (h, 0, 0)),
            pl.BlockSpec((1, C, RDK), lambda h, n: (h, 0, 0)),
            pl.BlockSpec((1, RDK, RDV), lambda h, n: (h, 0, 0)),
        ]
    return specs


def _ret_fwd(z, cst):
    def body(q_ref, k_ref, v_ref, cos_ref, sin_ref, dm_ref, xi_ref, zt_ref, gd_ref, r_ref, sp_ref, st):
        n = pl.program_id(1)

        @pl.when(n == 0)
        def _():
            st[...] = jnp.zeros_like(st)

        cos, sin = cos_ref[...], sin_ref[...]
        qr = _rot(q_ref[...], cos, sin)
        kr = _rot(k_ref[...], cos, sin) * RSCALE
        qb, kb, vb = qr.astype(BF), kr.astype(BF), v_ref[...].astype(BF)
        sd = (_dg(qb, kb, NT) * dm_ref[0]).astype(BF)
        state = st[...]
        sp_ref[0, 0] = state
        qx = (qr * xi_ref[0]).astype(BF)
        r_ref[...] = _dot(sd, vb) + _dot(qx, state.astype(BF))
        kz = (kr * zt_ref[0]).astype(BF)
        st[...] = state * gd_ref[0] + _dg(kz, vb, TN)

    return pl.pallas_call(
        body, name="ret_fwd", grid=(RH, NCH),
        in_specs=_ret_specs()(lambda n: n),
        out_specs=[pl.BlockSpec((C, RDV), lambda h, n: (n, h)),
                   pl.BlockSpec((1, 1, RDK, RDV), lambda h, n: (n, h, 0, 0))],
        out_shape=[jax.ShapeDtypeStruct((T, RH * RDV), F32), jax.ShapeDtypeStruct((NCH, RH, RDK, RDV), F32)],
        scratch_shapes=[pltpu.VMEM((RDK, RDV), F32)],
        compiler_params=_params(("parallel", "arbitrary")),
    )(z, z, z, cst["cos2"], cst["sin2"], cst["dmask"], cst["xi"], cst["zeta"], cst["gdec"])


def _ret_bwd(z, cst, sprev, dr):
    def body(q_ref, k_ref, v_ref, cos_ref, sin_ref, dm_ref, xi_ref, zt_ref, gd_ref, sp_ref, dr_ref,
             dq_ref, dk_ref, dv_ref, gst):
        i = pl.program_id(1)

        @pl.when(i == 0)
        def _():
            gst[...] = jnp.zeros_like(gst)

        cos, sin = cos_ref[...], sin_ref[...]
        dm, xi, zt = dm_ref[0], xi_ref[0], zt_ref[0]
        qr = _rot(q_ref[...], cos, sin)
        kr = _rot(k_ref[...], cos, sin) * RSCALE
        qb, kb, vb = qr.astype(BF), kr.astype(BF), v_ref[...].astype(BF)
        sd = (_dg(qb, kb, NT) * dm).astype(BF)
        qx = (qr * xi).astype(BF)
        kz = (kr * zt).astype(BF)
        drb = dr_ref[...]
        sb = sp_ref[0, 0].astype(BF)
        g = gst[...]
        gb = g.astype(BF)
        ds = (_dg(drb, vb, NT) * dm).astype(BF)
        dq = _dot(ds, kb) + _dg(drb, sb, NT) * xi
        dk = _dg(ds, qb, TN) + _dg(vb, gb, NT) * zt
        dv = _dg(sd, drb, TN) + _dot(kz, gb)
        gst[...] = g * gd_ref[0] + _dg(qx, drb, TN)
        dq_ref[...] = (dq * cos + pltpu.roll(dq * sin, 64, 1)).astype(BF)
        dkr = dk * RSCALE
        dk_ref[...] = (dkr * cos + pltpu.roll(dkr * sin, 64, 1)).astype(BF)
        dv_ref[...] = dv.astype(BF)

    rev = lambda n: NCH - 1 - n
    return pl.pallas_call(
        body, name="ret_bwd", grid=(RH, NCH),
        in_specs=_ret_specs()(rev) + [
            pl.BlockSpec((1, 1, RDK, RDV), lambda h, n: (rev(n), h, 0, 0)),
            pl.BlockSpec((C, RDV), lambda h, n: (rev(n), h)),
        ],
        out_specs=[pl.BlockSpec((C, RDK), lambda h, n: (rev(n), h)),
                   pl.BlockSpec((C, RDK), lambda h, n: (rev(n), h)),
                   pl.BlockSpec((C, RDV), lambda h, n: (rev(n), h))],
        out_shape=[jax.ShapeDtypeStruct((T, RH * RDK), BF), jax.ShapeDtypeStruct((T, RH * RDK), BF),
                   jax.ShapeDtypeStruct((T, RH * RDV), BF)],
        scratch_shapes=[pltpu.VMEM((RDK, RDV), F32)],
        compiler_params=_params(("parallel", "arbitrary")),
    )(z, z, z, cst["cos2"], cst["sin2"], cst["dmask"], cst["xi"], cst["zeta"], cst["gdec"], sprev, dr)


def _log_sigmoid(x):
    return -(jnp.maximum(-x, 0.0) + jnp.log1p(jnp.exp(-jnp.abs(x))))


def _fox_prep(zf, bf_pad, tri):
    def body(zf_ref, b_ref, tri_ref, c_ref, carry):
        n = pl.program_id(0)

        @pl.when(n == 0)
        def _():
            carry[...] = jnp.zeros_like(carry)

        ls = _log_sigmoid(zf_ref[...] + b_ref[...])
        row = n * C + lax.broadcasted_iota(jnp.int32, (C, C), 0)
        lf = jnp.where(row >= PAD, ls, 0.0)
        cc = jnp.dot(tri_ref[...], lf, precision=lax.Precision.HIGHEST, preferred_element_type=F32) + carry[0:1, :]
        c_ref[...] = cc
        carry[...] = jnp.broadcast_to(cc[C - 1:C, :], carry.shape)

    return pl.pallas_call(
        body, name="fox_prep", grid=(NCH,),
        in_specs=[pl.BlockSpec((C, C), lambda n: (n, 0)), pl.BlockSpec((1, C), lambda n: (0, 0)),
                  pl.BlockSpec((C, C), lambda n: (0, 0))],
        out_specs=pl.BlockSpec((C, C), lambda n: (n, 0)),
        out_shape=jax.ShapeDtypeStruct((T, C), F32),
        scratch_shapes=[pltpu.VMEM((8, C), F32)],
        compiler_params=_params(("arbitrary",)),
    )(zf, bf_pad, tri)


def _fox_mask(i, j):
    rows = i * C + lax.broadcasted_iota(jnp.int32, (C, C), 0)
    cols = j * C + lax.broadcasted_iota(jnp.int32, (C, C), 1)
    return (cols <= rows) & (cols >= PAD)


def _fox_fwd(z, cpair, ct):
    def body(q_ref, k_ref, v_ref, cp_ref, ct_ref, a_ref, l_ref):
        p, i = pl.program_id(0), pl.program_id(1)
        qf = q_ref[...]
        cpf = cp_ref[0]
        qh = [qf[:, FD * hh:FD * (hh + 1)].astype(BF) for hh in range(2)]
        ci = [cpf[:, FD * hh:FD * hh + 1] for hh in range(2)]

        def step(j, carry):
            ks = pl.multiple_of(j * C, C)
            kf = k_ref[pl.ds(ks, C), :]
            vf = v_ref[pl.ds(ks, C), :]
            mask = _fox_mask(i, j)
            out = []
            for hh in range(2):
                m, l, acc = carry[3 * hh:3 * hh + 3]
                kh = kf[:, FD * hh:FD * (hh + 1)].astype(BF)
                vh = vf[:, FD * hh:FD * (hh + 1)].astype(BF)
                cj = ct_ref[2 * p + hh, pl.ds(j, 1), :]
                s = _dg(qh[hh], kh, NT) * FSCALE + (ci[hh] - cj)
                s = jnp.where(mask, s, NEG)
                m_new = jnp.maximum(m, jnp.max(s, axis=1, keepdims=True))
                alpha = jnp.exp(m - m_new)
                pe = jnp.exp(s - m_new)
                l = alpha * l + jnp.sum(pe, axis=1, keepdims=True)
                acc = alpha * acc + _dot(pe.astype(BF), vh)
                out += [m_new, l, acc]
            return tuple(out)

        init = (jnp.full((C, 1), NEG, F32), jnp.zeros((C, 1), F32), jnp.zeros((C, FD), F32)) * 2
        res = lax.fori_loop(0, i + 1, step, init)
        for hh in range(2):
            m, l, acc = res[3 * hh:3 * hh + 3]
            a_ref[:, FD * hh:FD * (hh + 1)] = acc / l
            l_ref[0, :, FD * hh:FD * (hh + 1)] = jnp.broadcast_to(m + jnp.log(l), (C, FD))

    return pl.pallas_call(
        body, name="fox_fwd", grid=(NPAIR, NCH),
        in_specs=[pl.BlockSpec((C, C), lambda p, i: (i, QB_F + p)),
                  pl.BlockSpec((T, C), lambda p, i: (0, KB_F + p)),
                  pl.BlockSpec((T, C), lambda p, i: (0, VB_F + p)),
                  pl.BlockSpec((1, C, C), lambda p, i: (p, i, 0)),
                  pl.BlockSpec((FH, NCH, C), lambda p, i: (0, 0, 0))],
        out_specs=[pl.BlockSpec((C, C), lambda p, i: (i, p)),
                   pl.BlockSpec((1, C, C), lambda p, i: (p, i, 0))],
        out_shape=[jax.ShapeDtypeStruct((T, FH * FD), F32), jax.ShapeDtypeStruct((NPAIR, T, C), F32)],
        compiler_params=_params(("parallel", "arbitrary")),
    )(z, z, z, cpair, ct)


def _fox_bwd(z, da, a, lse, cpair, ct):
    def body(q_ref, k_ref, v_ref, da_ref, a_ref, l_ref, cp_ref, ct_ref, dq_ref, dk_ref, dv_ref, dc_ref, drow_ref,
             dqacc, dracc):
        p, j = pl.program_id(0), pl.program_id(1)

        @pl.when(j == 0)
        def _():
            dqacc[...] = jnp.zeros_like(dqacc)
            dracc[...] = jnp.zeros_like(dracc)

        kf, vf = k_ref[...], v_ref[...]
        kh = [kf[:, FD * hh:FD * (hh + 1)].astype(BF) for hh in range(2)]
        vh = [vf[:, FD * hh:FD * (hh + 1)].astype(BF) for hh in range(2)]
        cj = [ct_ref[2 * p + hh, pl.ds(j, 1), :] for hh in range(2)]

        def step(i, carry):
            rs = pl.multiple_of(i * C, C)
            qf = q_ref[pl.ds(rs, C), :]
            daf = da_ref[pl.ds(rs, C), :]
            af = a_ref[pl.ds(rs, C), :]
            lf = l_ref[0, pl.ds(rs, C), :]
            cf = cp_ref[0, pl.ds(rs, C), :]
            mask = _fox_mask(i, j)
            out = []
            for hh in range(2):
                dk, dv, dc = carry[3 * hh:3 * hh + 3]
                sl = slice(FD * hh, FD * (hh + 1))
                qh = qf[:, sl].astype(BF)
                doh = daf[:, sl]
                delta = jnp.sum(doh.astype(F32) * af[:, sl], axis=1, keepdims=True)
                s = _dg(qh, kh[hh], NT) * FSCALE + (cf[:, FD * hh:FD * hh + 1] - cj[hh])
                s = jnp.where(mask, s, NEG)
                pm = jnp.exp(s - lf[:, FD * hh:FD * hh + 1])
                ds = pm * (_dg(doh, vh[hh], NT) - delta)
                dsb = (ds * FSCALE).astype(BF)
                dv = dv + _dg(pm.astype(BF), doh, TN)
                dk = dk + _dg(dsb, qh, TN)
                dqacc[pl.ds(rs, C), sl] += _dot(dsb, kh[hh])
                dracc[pl.ds(rs, C), sl] += jnp.broadcast_to(jnp.sum(ds, axis=1, keepdims=True), (C, FD))
                dc = dc + jnp.sum(ds, axis=0, keepdims=True)
                out += [dk, dv, dc]
            return tuple(out)

        init = (jnp.zeros((C, FD), F32), jnp.zeros((C, FD), F32), jnp.zeros((1, C), F32)) * 2
        res = lax.fori_loop(j, NCH, step, init)
        for hh in range(2):
            dk_ref[:, FD * hh:FD * (hh + 1)] = res[3 * hh].astype(BF)
            dv_ref[:, FD * hh:FD * (hh + 1)] = res[3 * hh + 1].astype(BF)
        row = lax.broadcasted_iota(jnp.int32, (8, C), 0)
        dc_ref[0, 0] = jnp.where(row == 0, -res[2], jnp.where(row == 1, -res[5], 0.0))

        @pl.when(j == NCH - 1)
        def _():
            dq_ref[...] = dqacc[...].astype(BF)
            drow_ref[0] = dracc[...]

    return pl.pallas_call(
        body, name="fox_bwd", grid=(NPAIR, NCH),
        in_specs=[pl.BlockSpec((T, C), lambda p, j: (0, QB_F + p)),
                  pl.BlockSpec((C, C), lambda p, j: (j, KB_F + p)),
                  pl.BlockSpec((C, C), lambda p, j: (j, VB_F + p)),
                  pl.BlockSpec((T, C), lambda p, j: (0, p)),
                  pl.BlockSpec((T, C), lambda p, j: (0, p)),
                  pl.BlockSpec((1, T, C), lambda p, j: (p, 0, 0)),
                  pl.BlockSpec((1, T, C), lambda p, j: (p, 0, 0)),
                  pl.BlockSpec((FH, NCH, C), lambda p, j: (0, 0, 0))],
        out_specs=[pl.BlockSpec((T, C), lambda p, j: (0, p)),
                   pl.BlockSpec((C, C), lambda p, j: (j, p)),
                   pl.BlockSpec((C, C), lambda p, j: (j, p)),
                   pl.BlockSpec((1, 1, 8, C), lambda p, j: (p, j, 0, 0)),
                   pl.BlockSpec((1, T, C), lambda p, j: (p, 0, 0))],
        out_shape=[jax.ShapeDtypeStruct((T, FH * FD), BF), jax.ShapeDtypeStruct((T, FH * FD), BF),
                   jax.ShapeDtypeStruct((T, FH * FD), BF), jax.ShapeDtypeStruct((NPAIR, NCH, 8, C), F32),
                   jax.ShapeDtypeStruct((NPAIR, T, C), F32)],
        scratch_shapes=[pltpu.VMEM((T, C), F32), pltpu.VMEM((T, C), F32)],
        compiler_params=_params(("parallel", "arbitrary")),
    )(z, z, z, da, a, lse, cpair, ct)


def _fox_gate_bwd(dc, zf, bf_pad, tri):
    def body(dc_ref, zf_ref, b_ref, tri_ref, dff_ref, db_ref, carry):
        s = pl.program_id(0)
        n = NCH - 1 - s

        @pl.when(s == 0)
        def _():
            carry[...] = jnp.zeros_like(carry)
            db_ref[...] = jnp.zeros_like(db_ref)

        dcb = dc_ref[...]
        suf = lax.dot_general(tri_ref[...], dcb, TN, precision=lax.Precision.HIGHEST,
                              preferred_element_type=F32) + carry[0:1, :]
        carry[...] = jnp.broadcast_to(suf[0:1, :], carry.shape)
        x = zf_ref[...] + b_ref[...]
        row = n * C + lax.broadcasted_iota(jnp.int32, (C, C), 0)
        dff = jnp.where(row >= PAD, suf * (1.0 - jax.nn.sigmoid(x)), 0.0)
        dff_ref[...] = dff.astype(BF)
        db_ref[...] += jnp.sum(dff, axis=0, keepdims=True)

    rev = lambda s: (NCH - 1 - s, 0)
    return pl.pallas_call(
        body, name="fox_gate_bwd", grid=(NCH,),
        in_specs=[pl.BlockSpec((C, C), rev), pl.BlockSpec((C, C), rev),
                  pl.BlockSpec((1, C), lambda s: (0, 0)), pl.BlockSpec((C, C), lambda s: (0, 0))],
        out_specs=[pl.BlockSpec((C, C), rev), pl.BlockSpec((1, C), lambda s: (0, 0))],
        out_shape=[jax.ShapeDtypeStruct((T, C), BF), jax.ShapeDtypeStruct((1, C), F32)],
        scratch_shapes=[pltpu.VMEM((8, C), F32)],
        compiler_params=_params(("arbitrary",)),
    )(dc, zf, bf_pad, tri)


def _gated(r, rg, a, fg):
    rn, rs = [], []
    for h in range(RH):
        rh = r[:, RDV * h:RDV * (h + 1)]
        s = lax.rsqrt(jnp.mean(rh * rh, axis=1, keepdims=True) + EPS)
        rn.append(rh * s)
        rs.append(s)
    rn = jnp.concatenate(rn, axis=1)
    y = jnp.concatenate([rn * (rg * jax.nn.sigmoid(rg)), a * (fg * jax.nn.sigmoid(fg))], axis=1)
    return y, rn, rs


def _out_loss(r, z, a, wout, x, tgt, fgain):
    def body(r_ref, rg_ref, a_ref, fg_ref, w_ref, x_ref, t_ref, g_ref, yt_ref, do_ref, dob_ref, loss_ref, dg_ref):
        i = pl.program_id(0)

        @pl.when(i == 0)
        def _():
            yt_ref[...] = jnp.zeros_like(yt_ref)
            do_ref[...] = jnp.zeros_like(do_ref)
            dob_ref[...] = jnp.zeros_like(dob_ref)
            loss_ref[...] = jnp.zeros_like(loss_ref)
            dg_ref[...] = jnp.zeros_like(dg_ref)

        @pl.when(i > 0)
        def _():
            y, _, _ = _gated(r_ref[...], rg_ref[...], a_ref[...], fg_ref[...])
            yt_ref[...] = y.T.astype(BF)
            o = x_ref[...] + _dot(y.astype(BF), w_ref[...])
            rs = lax.rsqrt(jnp.mean(o * o, axis=1, keepdims=True) + EPS)
            on = o * rs
            g = g_ref[...]
            e = on * g - t_ref[...]
            loss_ref[...] += 0.5 * jnp.sum(jnp.mean(e * e, axis=1, keepdims=True))
            dyh = e * (1.0 / D)
            dg_ref[...] += jnp.sum(dyh * on, axis=0, keepdims=True)
            don = dyh * g
            do = rs * (don - on * jnp.mean(don * on, axis=1, keepdims=True))
            do_ref[...] = do
            dob_ref[...] = do.astype(BF)

    tok = lambda i: (jnp.maximum(i - 1, 0), 0)
    return pl.pallas_call(
        body, name="out_loss", grid=(NCH,),
        in_specs=[pl.BlockSpec((C, D), lambda i: (i, 0)), pl.BlockSpec((C, D), lambda i: (i, GB_R)),
                  pl.BlockSpec((C, D), lambda i: (i, 0)), pl.BlockSpec((C, D), lambda i: (i, GB_F)),
                  pl.BlockSpec((DMIX, D), lambda i: (0, 0)),
                  pl.BlockSpec((C, D), tok), pl.BlockSpec((C, D), tok), pl.BlockSpec((1, D), lambda i: (0, 0))],
        out_specs=[pl.BlockSpec((DMIX, C), lambda i: (0, i)), pl.BlockSpec((C, D), lambda i: (i, 0)),
                   pl.BlockSpec((C, D), lambda i: (i, 0)), pl.BlockSpec((8, C), lambda i: (0, 0)),
                   pl.BlockSpec((1, D), lambda i: (0, 0))],
        out_shape=[jax.ShapeDtypeStruct((DMIX, T), BF), jax.ShapeDtypeStruct((T, D), F32),
                   jax.ShapeDtypeStruct((T, D), BF), jax.ShapeDtypeStruct((8, C), F32),
                   jax.ShapeDtypeStruct((1, D), F32)],
        compiler_params=_params(("arbitrary",)),
    )(r, z, a, z, wout, x, tgt, fgain)


def _dsilu(x):
    s = jax.nn.sigmoid(x)
    return s * (1.0 + x * (1.0 - s))


def _dy_gate_bwd(dob, wout, r, z, a):
    def body(do_ref, w_ref, r_ref, rg_ref, a_ref, fg_ref, dr_ref, da_ref, drg_ref, dfg_ref):
        dy = _dg(do_ref[...], w_ref[...], NT)
        rg, fg, a_ = rg_ref[...], fg_ref[...], a_ref[...]
        _, rn, rs = _gated(r_ref[...], rg, a_, fg)
        dyr, dyf = dy[:, :D], dy[:, D:]
        drn = dyr * (rg * jax.nn.sigmoid(rg))
        drg_ref[...] = (dyr * rn * _dsilu(rg)).astype(BF)
        for h in range(RH):
            sl = slice(RDV * h, RDV * (h + 1))
            dh, nh = drn[:, sl], rn[:, sl]
            dr_ref[:, sl] = (rs[h] * (dh - nh * jnp.mean(dh * nh, axis=1, keepdims=True))).astype(BF)
        da_ref[...] = (dyf * (fg * jax.nn.sigmoid(fg))).astype(BF)
        dfg_ref[...] = (dyf * a_ * _dsilu(fg)).astype(BF)

    row = lambda i: (i, 0)
    return pl.pallas_call(
        body, name="dy_gate_bwd", grid=(NCH,),
        in_specs=[pl.BlockSpec((C, D), row), pl.BlockSpec((DMIX, D), lambda i: (0, 0)),
                  pl.BlockSpec((C, D), row), pl.BlockSpec((C, D), lambda i: (i, GB_R)),
                  pl.BlockSpec((C, D), row), pl.BlockSpec((C, D), lambda i: (i, GB_F))],
        out_specs=[pl.BlockSpec((C, D), row)] * 4,
        out_shape=[jax.ShapeDtypeStruct((T, D), BF)] * 4,
        compiler_params=_params(("parallel",)),
    )(dob, wout, r, z, a, z)


def _du_norm_bwd(dzm, dzf, wm, wf, hpad, g, dopad):
    tm, tk = 544, 1024
    nk = WMAIN // tk

    def body(dzm_ref, dzf_ref, wm_ref, wf_ref, h_ref, g_ref, do_ref, gh_ref, dg_ref, acc):
        i, k = pl.program_id(0), pl.program_id(1)

        @pl.when(k == 0)
        def _():
            acc[...] = _dg(dzf_ref[...], wf_ref[...], NT)

        acc[...] += _dg(dzm_ref[...], wm_ref[...], NT)

        @pl.when(k == nk - 1)
        def _():
            du = acc[...]
            h = h_ref[...]
            gg = g_ref[...]
            rs = lax.rsqrt(jnp.mean(h * h, axis=1, keepdims=True) + EPS)
            hn = h * rs
            part = jnp.sum(du * hn, axis=0, keepdims=True)

            @pl.when(i == 0)
            def _():
                dg_ref[...] = part

            @pl.when(i > 0)
            def _():
                dg_ref[...] += part

            dhn = du * gg
            gh_ref[...] = rs * (dhn - hn * jnp.mean(dhn * hn, axis=1, keepdims=True)) + do_ref[...]

    return pl.pallas_call(
        body, name="du_norm_bwd", grid=(T // tm, nk),
        in_specs=[pl.BlockSpec((tm, tk), lambda i, k: (i, k)), pl.BlockSpec((tm, C), lambda i, k: (i, 0)),
                  pl.BlockSpec((D, tk), lambda i, k: (0, k)), pl.BlockSpec((D, C), lambda i, k: (0, 0)),
                  pl.BlockSpec((tm, D), lambda i, k: (i, 0)), pl.BlockSpec((1, D), lambda i, k: (0, 0)),
                  pl.BlockSpec((tm, D), lambda i, k: (i, 0))],
        out_specs=[pl.BlockSpec((tm, D), lambda i, k: (i, 0)), pl.BlockSpec((1, D), lambda i, k: (0, 0))],
        out_shape=[jax.ShapeDtypeStruct((T, D), F32), jax.ShapeDtypeStruct((1, D), F32)],
        scratch_shapes=[pltpu.VMEM((tm, D), F32)],
        compiler_params=_params(("arbitrary", "arbitrary")),
    )(dzm, dzf, wm, wf, hpad, g, dopad)


def _local_step(x, tgt, meta, norm_g, wm, wf, b_f, wout, final_g):
    cst = _constants()
    hpad = jnp.concatenate([jnp.pad(meta, ((PAD, 0), (0, 0))), x], axis=0)
    bf_pad = jnp.pad(b_f, ((0, 0), (0, C - NFF)))
    u, ut = _norm_in(hpad, norm_g)
    z = _mm_nn(u, wm, T // 2, 512, "in_proj")
    zf = _mm_nn(u, wf, T // 2, C, "in_proj_ff")
    r, sprev = _ret_fwd(z, cst)
    c = _fox_prep(zf, bf_pad, cst["tri"])
    c16 = c[:, :FH]
    cpair = jnp.repeat(c16.T.reshape(NPAIR, 2, T), FD, axis=1).transpose(0, 2, 1)
    ct = c16.T.reshape(FH, NCH, C)
    a, lse = _fox_fwd(z, cpair, ct)
    yt, dopad, dob, loss8, dfg = _out_loss(r, z, a, wout, x, tgt, final_g)
    dr, da, dzrg, dzfg = _dy_gate_bwd(dob, wout, r, z, a)
    dwout = _mm_nn(yt, dob, 512, D, "dw_out")
    dzq_r, dzk_r, dzv_r = _ret_bwd(z, cst, sprev, dr)
    dzq_f, dzk_f, dzv_f, dcol, drow = _fox_bwd(z, da, a, lse, cpair, ct)
    dc = dcol[:, :, :2, :].transpose(1, 3, 0, 2).reshape(T, FH)
    dc = dc + drow[:, :, ::FD].transpose(1, 0, 2).reshape(T, FH)
    dc = jnp.pad(dc, ((0, 0), (0, C - FH)))
    dzf, dbf = _fox_gate_bwd(dc, zf, bf_pad, cst["tri"])
    dzm = jnp.concatenate([dzq_r, dzk_r, dzv_r, dzrg, dzq_f, dzk_f, dzv_f, dzfg], axis=1)
    dwm = _mm_nn(ut, dzm, 512, 1024, "dw_in")
    dwf = _mm_nn(ut, dzf, D, C, "dw_in_ff")
    gh, dng = _du_norm_bwd(dzm, dzf, wm, wf, hpad, norm_g, dopad)
    dwin = jnp.concatenate([dwm, dwf[:, :NFF]], axis=1)
    return (loss8[0, 0], gh[C:], gh[PAD:C], dng, dwin, dbf[:, :NFF], dwout, dfg)


def _place():
    x, y, c = lax.axis_index("x"), lax.axis_index("y"), lax.axis_index("c")
    return x, y, c


def _other_chips(x, y):
    return [(1 - x, y, 2 * (1 - x) + y), (x, 1 - y, 2 * x + (1 - y)), (1 - x, 1 - y, 2 * (1 - x) + (1 - y))]


def _all_gather_shards(shards):
    n = len(shards)

    def body(*refs):
        ins, outs = refs[:n], refs[n:2 * n]
        send_sems, recv_sems, local_sems = refs[2 * n:]
        x, y, c = _place()
        me_s = 2 * x + y
        sib = (x, y, 1 - c)
        chips = _other_chips(x, y)
        local, sends, waits = [], [], []
        for a in range(n):
            rows = ins[a].shape[0] // 2
            half = pl.ds(c * rows, rows)
            other = pl.ds((1 - c) * rows, rows)
            cp = pltpu.make_async_copy(ins[a], outs[a].at[me_s], local_sems.at[a])
            cp.start()
            local.append(cp)
            for k, (cx, cy, cs) in enumerate(chips):
                sends.append(pltpu.make_async_remote_copy(
                    src_ref=ins[a].at[half], dst_ref=outs[a].at[me_s, half],
                    send_sem=send_sems.at[6 * a + k], recv_sem=recv_sems.at[6 * a + k],
                    device_id=(cx, cy, c), device_id_type=MESH))
                sends[-1].start()
        for a in range(n):
            rows = ins[a].shape[0] // 2
            half = pl.ds(c * rows, rows)
            other = pl.ds((1 - c) * rows, rows)
            for k, (cx, cy, cs) in enumerate(chips):
                pltpu.make_async_remote_copy(
                    src_ref=outs[a].at[cs, half], dst_ref=outs[a].at[cs, half],
                    send_sem=send_sems.at[6 * a + k], recv_sem=recv_sems.at[6 * a + k],
                    device_id=(cx, cy, c), device_id_type=MESH).wait_recv()
                fwd = pltpu.make_async_remote_copy(
                    src_ref=outs[a].at[cs, half], dst_ref=outs[a].at[cs, half],
                    send_sem=send_sems.at[6 * a + 3 + k], recv_sem=recv_sems.at[6 * a + 3 + k],
                    device_id=sib, device_id_type=MESH)
                fwd.start()
                sends.append(fwd)
                waits.append(pltpu.make_async_remote_copy(
                    src_ref=outs[a].at[cs, other], dst_ref=outs[a].at[cs, other],
                    send_sem=send_sems.at[6 * a + 3 + k], recv_sem=recv_sems.at[6 * a + 3 + k],
                    device_id=sib, device_id_type=MESH))
        for w in waits:
            w.wait_recv()
        for s in sends:
            s.wait_send()
        for cp in local:
            cp.wait()

    return pl.pallas_call(
        body, name="all_gather_w",
        in_specs=[ANY] * n, out_specs=[ANY] * n,
        out_shape=[jax.ShapeDtypeStruct((4,) + s.shape, s.dtype) for s in shards],
        scratch_shapes=[pltpu.SemaphoreType.DMA((6 * n,)), pltpu.SemaphoreType.DMA((6 * n,)),
                        pltpu.SemaphoreType.DMA((n,))],
    )(*shards)


def _pair_swap(arrs, small):
    n = len(arrs)

    def body(*refs):
        ins, sm = refs[:n], refs[n]
        outs, smo = refs[n + 1:2 * n + 1], refs[2 * n + 1]
        send_sems, recv_sems = refs[2 * n + 2:]
        x, y, c = _place()
        sib = (x, y, 1 - c)
        cps = []
        for a in range(n):
            rows = ins[a].shape[1] // 2
            cps.append(pltpu.make_async_remote_copy(
                src_ref=ins[a].at[:, pl.ds((1 - c) * rows, rows)], dst_ref=outs[a],
                send_sem=send_sems.at[a], recv_sem=recv_sems.at[a], device_id=sib, device_id_type=MESH))
        cps.append(pltpu.make_async_remote_copy(
            src_ref=sm, dst_ref=smo, send_sem=send_sems.at[n], recv_sem=recv_sems.at[n],
            device_id=sib, device_id_type=MESH))
        for cp in cps:
            cp.start()
        for cp in cps:
            cp.wait()

    return pl.pallas_call(
        body, name="rs_pair_swap",
        in_specs=[ANY] * (n + 1), out_specs=[ANY] * (n + 1),
        out_shape=[jax.ShapeDtypeStruct((4, a.shape[1] // 2, a.shape[2]), a.dtype) for a in arrs]
        + [jax.ShapeDtypeStruct(small.shape, small.dtype)],
        scratch_shapes=[pltpu.SemaphoreType.DMA((n + 1,)), pltpu.SemaphoreType.DMA((n + 1,))],
    )(*arrs, small)


def _chip_exchange(parts, small):
    n = len(parts)

    def body(*refs):
        ins, sm = refs[:n], refs[n]
        outs, smo = refs[n + 1:2 * n + 1], refs[2 * n + 1]
        send_sems, recv_sems, local_sems = refs[2 * n + 2:]
        x, y, c = _place()
        me_s = 2 * x + y
        chips = _other_chips(x, y)
        cps, local = [], []
        for a in range(n + 1):
            src = ins[a] if a < n else sm
            dst = outs[a] if a < n else smo
            mine = src.at[me_s] if a < n else src
            cp = pltpu.make_async_copy(mine, dst.at[me_s], local_sems.at[a])
            cp.start()
            local.append(cp)
            for k, (cx, cy, cs) in enumerate(chips):
                cps.append(pltpu.make_async_remote_copy(
                    src_ref=src.at[cs] if a < n else src, dst_ref=dst.at[me_s],
                    send_sem=send_sems.at[3 * a + k], recv_sem=recv_sems.at[3 * a + k],
                    device_id=(cx, cy, c), device_id_type=MESH))
        for cp in cps:
            cp.start()
        for cp in cps:
            cp.wait()
        for cp in local:
            cp.wait()

    return pl.pallas_call(
        body, name="rs_chip_exchange",
        in_specs=[ANY] * (n + 1), out_specs=[ANY] * (n + 1),
        out_shape=[jax.ShapeDtypeStruct(p.shape, p.dtype) for p in parts]
        + [jax.ShapeDtypeStruct((4,) + small.shape, small.dtype)],
        scratch_shapes=[pltpu.SemaphoreType.DMA((3 * (n + 1),)), pltpu.SemaphoreType.DMA((3 * (n + 1),)),
                        pltpu.SemaphoreType.DMA((n + 1,))],
    )(*parts, small)


def _pair_gather(halves):
    n = len(halves)

    def body(*refs):
        ins, outs = refs[:n], refs[n:2 * n]
        send_sems, recv_sems, local_sems = refs[2 * n:]
        x, y, c = _place()
        sib = (x, y, 1 - c)
        cps, local = [], []
        for a in range(n):
            rows = ins[a].shape[0]
            dst = outs[a].at[pl.ds(c * rows, rows)]
            cp = pltpu.make_async_copy(ins[a], dst, local_sems.at[a])
            cp.start()
            local.append(cp)
            cps.append(pltpu.make_async_remote_copy(
                src_ref=ins[a], dst_ref=dst, send_sem=send_sems.at[a], recv_sem=recv_sems.at[a],
                device_id=sib, device_id_type=MESH))
        for cp in cps:
            cp.start()
        for cp in cps:
            cp.wait()
        for cp in local:
            cp.wait()

    return pl.pallas_call(
        body, name="rs_pair_gather",
        in_specs=[ANY] * n, out_specs=[ANY] * n,
        out_shape=[jax.ShapeDtypeStruct((2 * h.shape[0], h.shape[1]), h.dtype) for h in halves],
        scratch_shapes=[pltpu.SemaphoreType.DMA((n,)), pltpu.SemaphoreType.DMA((n,)),
                        pltpu.SemaphoreType.DMA((n,))],
    )(*halves)


def _row_block(rows):
    for tb in (256, 128, 64, 32, 16, 8):
        if rows % tb == 0:
            return tb
    return rows


def _add_halves(full, recv, name):
    _, r2, w = recv.shape
    tb = _row_block(r2)
    nb = r2 // tb
    c = lax.axis_index("c")

    def body(c_ref, a_ref, b_ref, o_ref):
        o_ref[...] = a_ref[...] + b_ref[...]

    return pl.pallas_call(
        body, name=name,
        grid_spec=pltpu.PrefetchScalarGridSpec(
            num_scalar_prefetch=1, grid=(4, nb),
            in_specs=[pl.BlockSpec((1, tb, w), lambda s, i, cr: (s, cr[0] * nb + i, 0)),
                      pl.BlockSpec((1, tb, w), lambda s, i, cr: (s, i, 0))],
            out_specs=pl.BlockSpec((1, tb, w), lambda s, i, cr: (s, i, 0))),
        out_shape=jax.ShapeDtypeStruct(recv.shape, recv.dtype),
        compiler_params=_params(("parallel", "parallel")),
    )(jnp.reshape(c, (1,)).astype(jnp.int32), full, recv)


def _add2(a, b, name):
    def body(a_ref, b_ref, o_ref):
        o_ref[...] = a_ref[...] + b_ref[...]

    return pl.pallas_call(body, name=name, out_shape=jax.ShapeDtypeStruct(a.shape, a.dtype))(a, b)


def _sum4(buf, name):
    _, r, w = buf.shape
    tb = _row_block(r)

    def body(b_ref, o_ref):
        o_ref[...] = ((b_ref[0] + b_ref[1]) + b_ref[2]) + b_ref[3]

    return pl.pallas_call(
        body, name=name, grid=(r // tb,),
        in_specs=[pl.BlockSpec((4, tb, w), lambda i: (0, i, 0))],
        out_specs=pl.BlockSpec((tb, w), lambda i: (i, 0)),
        out_shape=jax.ShapeDtypeStruct((r, w), buf.dtype),
        compiler_params=_params(("parallel",)),
    )(buf)


def _adamw(w, g, m, v, name):
    r, c_ = w.shape
    tb = _row_block(r)

    def body(w_ref, g_ref, m_ref, v_ref, d_ref, mo_ref, vo_ref):
        g_ = g_ref[...]
        mn = B1 * m_ref[...] + (1.0 - B1) * g_
        vn = B2 * v_ref[...] + (1.0 - B2) * (g_ * g_)
        m_hat = mn / (1.0 - B1 ** STEP)
        v_hat = vn / (1.0 - B2 ** STEP)
        d_ref[...] = -LR * (m_hat / (jnp.sqrt(v_hat) + AEPS) + WD * w_ref[...])
        mo_ref[...] = mn
        vo_ref[...] = vn

    spec = pl.BlockSpec((tb, c_), lambda i: (i, 0))
    return pl.pallas_call(
        body, name=name, grid=(r // tb,),
        in_specs=[spec] * 4, out_specs=[spec] * 3,
        out_shape=[jax.ShapeDtypeStruct(w.shape, F32)] * 3,
        compiler_params=_params(("parallel",)),
    )(w, g, m, v)


def kernel(x, meta_tokens, norm_g, w_in, b_f, w_out, final_g, loss_target, m_meta_tokens, m_norm_g, m_w_in, m_b_f, m_w_out, m_final_g, v_meta_tokens, v_norm_g, v_w_in, v_b_f, v_w_out, v_final_g):
    gin, gout, gmeta = _all_gather_shards([w_in[0].astype(BF), w_out[0].astype(BF), meta_tokens])
    wfull = jnp.concatenate([gin[0], gin[1], gin[2], gin[3]], axis=1)
    wm = wfull[:, :WMAIN]
    wf = jnp.pad(wfull[:, WMAIN:], ((0, 0), (0, C - NFF)))
    wout = gout.reshape(DMIX, D)
    meta = jnp.concatenate([gmeta[0], gmeta[1], gmeta[2], gmeta[3]], axis=1)

    loss, gx, dmeta, dng, dwin, dbf, dwout, dfg = _local_step(
        x[0], loss_target[0], meta, norm_g, wm, wf, b_f, wout, final_g.reshape(1, D))

    g_in = jnp.stack([dwin[:, WSH * s:WSH * (s + 1)] for s in range(4)])
    g_out = dwout.reshape(4, DMIX // 4, D)
    g_meta = jnp.stack([dmeta[:, 256 * s:256 * (s + 1)] for s in range(4)])
    small = jnp.concatenate([dng, dfg, jnp.pad(dbf, ((0, 0), (0, D - NFF))),
                             jnp.pad(jnp.reshape(loss, (1, 1)), ((0, 0), (0, D - 1))),
                             jnp.zeros((4, D), F32)], axis=0)
    r_in, r_out, r_meta, r_small = _pair_swap([g_in, g_out, g_meta], small)
    p_in = _add_halves(g_in, r_in, "pair_add_in")
    p_out = _add_halves(g_out, r_out, "pair_add_out")
    p_meta = _add_halves(g_meta, r_meta, "pair_add_meta")
    p_small = _add2(small, r_small, "pair_add_small")
    e_in, e_out, e_meta, e_small = _chip_exchange([p_in, p_out, p_meta], p_small)
    h_in, h_out, h_meta = _sum4(e_in, "sum_in"), _sum4(e_out, "sum_out"), _sum4(e_meta, "sum_meta")
    tot = _sum4(e_small, "sum_small")
    gw_in, gw_out, gw_meta = _pair_gather([h_in, h_out, h_meta])
    g_norm, g_final, g_bf, loss_all = tot[0:1], tot[1], tot[2:3, :NFF], tot[3, 0]

    d_meta, nm_meta, nv_meta = _adamw(meta_tokens, gw_meta, m_meta_tokens, v_meta_tokens, "adamw_meta")
    d_norm, nm_norm, nv_norm = _adamw(norm_g, g_norm, m_norm_g, v_norm_g, "adamw_norm")
    d_in, nm_in, nv_in = _adamw(w_in[0], gw_in, m_w_in[0], v_w_in[0], "adamw_in")
    d_bf, nm_bf, nv_bf = _adamw(b_f, g_bf, m_b_f, v_b_f, "adamw_bf")
    d_out, nm_out, nv_out = _adamw(w_out[0], gw_out, m_w_out[0], v_w_out[0], "adamw_out")
    d_fin, nm_fin, nv_fin = _adamw(final_g.reshape(1, D), g_final.reshape(1, D), m_final_g.reshape(1, D),
                                   v_final_g.reshape(1, D), "adamw_final")
    return (loss_all, gx[None], gw_meta, g_norm, gw_in[None], g_bf, gw_out[None], g_final,
            d_meta, d_norm, d_in[None], d_bf, d_out[None], d_fin.reshape(D),
            nm_meta, nm_norm, nm_in[None], nm_bf, nm_out[None], nm_fin.reshape(D),
            nv_meta, nv_norm, nv_in[None], nv_bf, nv_out[None], nv_fin.reshape(D))
```

```python
import numpy as np
import jax
import jax.numpy as jnp
from jax import lax
from jax.experimental import pallas as pl
from jax.experimental.pallas import tpu as pltpu

D = 1024
SEQ = 2048
NMETA = 16
C = 128
PAD = C - NMETA
T = PAD + NMETA + SEQ
NCH = T // C
RH, RDK, RDV = 4, 128, 256
FH, FD = 16, 64
NPAIR = FH // 2
WMAIN = 7168
NFF = 16
WIN = WMAIN + NFF
WSH = WIN // 4
DMIX = 2048
EPS = 1e-6
NEG = -1e30
RSCALE = RDK ** -0.5
FSCALE = FD ** -0.5
ROPE_BASE = 10000.0
LR, B1, B2, AEPS, WD, STEP = 0.001, 0.9, 0.999, 1e-08, 0.01, 10

BF = jnp.bfloat16
F32 = jnp.float32
NT = (((1,), (1,)), ((), ()))
TN = (((0,), (0,)), ((), ()))
HI = lax.Precision.HIGHEST
MESH = pl.DeviceIdType.MESH
ANY = pl.BlockSpec(memory_space=pl.ANY)
VMEM_LIMIT = 48 * 1024 * 1024

QB_R, KB_R = 0, 4
VB_R = 4
GB_R, GB_F = 2, 6
QB_F, KB_F, VB_F = 24, 32, 40


def _dot(a, b):
    return jnp.dot(a, b, preferred_element_type=F32)


def _dg(a, b, dims):
    return lax.dot_general(a, b, dims, preferred_element_type=F32)


def _params(sem=None):
    return pltpu.CompilerParams(dimension_semantics=sem, vmem_limit_bytes=VMEM_LIMIT)


def _constants():
    pos = jnp.arange(T, dtype=F32) - PAD
    inv = ROPE_BASE ** (-jnp.arange(0, RDK, 2, dtype=F32) / RDK)
    ang = pos[:, None] * inv[None, :]
    cos, sin = jnp.cos(ang), jnp.sin(ang)
    cos2 = jnp.concatenate([cos, cos], axis=1)
    sin2 = jnp.concatenate([-sin, sin], axis=1)
    log_gamma = jnp.log1p(-jnp.exp2(-5.0 - jnp.arange(RH, dtype=F32)))
    idx = jnp.arange(C, dtype=F32)
    diff = idx[:, None] - idx[None, :]
    dmask = jnp.where(diff[None] >= 0, jnp.exp(log_gamma[:, None, None] * jnp.maximum(diff, 0.0)[None]), 0.0)
    zeta = jnp.exp(log_gamma[:, None] * (C - 1.0 - idx)[None, :])
    xi = jnp.exp(log_gamma[:, None] * (idx + 1.0)[None, :])
    gdec = jnp.exp(log_gamma * C)
    zeta_b = jnp.broadcast_to(zeta[:, :, None], (RH, C, RDK))
    xi_b = jnp.broadcast_to(xi[:, :, None], (RH, C, RDK))
    gdec_b = jnp.broadcast_to(gdec[:, None, None], (RH, RDK, RDV))
    tri = jnp.asarray(np.tril(np.ones((C, C), np.float32)))
    head_of_lane = np.arange(FH * FD) // FD
    spread = (np.arange(C)[:, None] == head_of_lane[None, :]).astype(np.float32)
    pick = ((np.arange(FH * FD)[:, None] % FD == 0)
            & (head_of_lane[:, None] == np.arange(C)[None, :])).astype(np.float32)
    seg = (np.arange(C)[:, None] // FD == np.arange(C)[None, :] // FD).astype(np.float32)
    ones_aug = np.concatenate([np.tile((np.arange(C) < FD)[None, :], (C, 1)),
                               np.tile((np.arange(C) >= FD)[None, :], (C, 1))], axis=0).astype(np.float32)
    lane = np.arange(2 * C) % C
    causal = np.where(lane[None, :] <= np.arange(C)[:, None], 0.0, NEG).astype(np.float32)
    mask_bias = np.stack([np.zeros((C, 2 * C), np.float32), causal, np.full((C, 2 * C), NEG, np.float32)])
    return dict(cos2=cos2, sin2=sin2, dmask=dmask, zeta=zeta_b, xi=xi_b, gdec=gdec_b, tri=tri,
                mask_bias=jnp.asarray(mask_bias),
                spread=jnp.asarray(spread), pick=jnp.asarray(pick), seg=jnp.asarray(seg, dtype=BF),
                ones_aug=jnp.asarray(ones_aug, dtype=BF))


def _norm_in(hpad, g):
    def body(h_ref, g_ref, u_ref, ut_ref):
        h = h_ref[...]
        rs = lax.rsqrt(jnp.mean(h * h, axis=1, keepdims=True) + EPS)
        u = h * rs * g_ref[...]
        u_ref[...] = u.astype(BF)
        ut_ref[...] = u.T.astype(BF)

    return pl.pallas_call(
        body, name="norm_in", grid=(NCH,),
        in_specs=[pl.BlockSpec((C, D), lambda i: (i, 0)), pl.BlockSpec((1, D), lambda i: (0, 0))],
        out_specs=[pl.BlockSpec((C, D), lambda i: (i, 0)), pl.BlockSpec((D, C), lambda i: (0, i))],
        out_shape=[jax.ShapeDtypeStruct((T, D), BF), jax.ShapeDtypeStruct((D, T), BF)],
        compiler_params=_params(("parallel",)),
    )(hpad, g)


def _mm_nn(a, b, tm, tn, name):
    m, k = a.shape
    _, n = b.shape

    def body(a_ref, b_ref, o_ref):
        o_ref[...] = _dot(a_ref[...], b_ref[...])

    return pl.pallas_call(
        body, name=name, grid=(m // tm, n // tn),
        in_specs=[pl.BlockSpec((tm, k), lambda i, j: (i, 0)), pl.BlockSpec((k, tn), lambda i, j: (0, j))],
        out_specs=pl.BlockSpec((tm, tn), lambda i, j: (i, j)),
        out_shape=jax.ShapeDtypeStruct((m, n), F32),
        compiler_params=_params(("parallel", "parallel")),
    )(a, b)


def _rot(x, cos2, sin2):
    return x * cos2 + pltpu.roll(x, 64, 1) * sin2


def _ret_specs(chunk):
    return [
        pl.BlockSpec((C, RDK), lambda h, n: (chunk(n), QB_R + h)),
        pl.BlockSpec((C, RDK), lambda h, n: (chunk(n), KB_R + h)),
        pl.BlockSpec((C, RDV), lambda h, n: (chunk(n), VB_R + h)),
        pl.BlockSpec((C, RDK), lambda h, n: (chunk(n), 0)),
        pl.BlockSpec((C, RDK), lambda h, n: (chunk(n), 0)),
        pl.BlockSpec((1, C, C), lambda h, n: (h, 0, 0)),
        pl.BlockSpec((1, C, RDK), lambda h, n: (h, 0, 0)),
        pl.BlockSpec((1, C, RDK), lambda h, n: (h, 0, 0)),
        pl.BlockSpec((1, RDK, RDV), lambda h, n: (h, 0, 0)),
    ]


def _ret_fwd(z, cst):
    def body(q_ref, k_ref, v_ref, cos_ref, sin_ref, dm_ref, xi_ref, zt_ref, gd_ref, r_ref, sp_ref, st):
        n = pl.program_id(1)

        @pl.when(n == 0)
        def _():
            st[...] = jnp.zeros_like(st)

        cos, sin = cos_ref[...], sin_ref[...]
        qr = _rot(q_ref[...], cos, sin)
        kr = _rot(k_ref[...], cos, sin) * RSCALE
        qb, kb, vb = qr.astype(BF), kr.astype(BF), v_ref[...].astype(BF)
        sd = (_dg(qb, kb, NT) * dm_ref[0]).astype(BF)
        state = st[...]
        sp_ref[0, 0] = state
        qx = (qr * xi_ref[0]).astype(BF)
        r_ref[...] = _dot(sd, vb) + _dot(qx, state.astype(BF))
        kz = (kr * zt_ref[0]).astype(BF)
        st[...] = state * gd_ref[0] + _dg(kz, vb, TN)

    return pl.pallas_call(
        body, name="ret_fwd", grid=(RH, NCH),
        in_specs=_ret_specs(lambda n: n),
        out_specs=[pl.BlockSpec((C, RDV), lambda h, n: (n, h)),
                   pl.BlockSpec((1, 1, RDK, RDV), lambda h, n: (n, h, 0, 0))],
        out_shape=[jax.ShapeDtypeStruct((T, RH * RDV), F32), jax.ShapeDtypeStruct((NCH, RH, RDK, RDV), F32)],
        scratch_shapes=[pltpu.VMEM((RDK, RDV), F32)],
        compiler_params=_params(("parallel", "arbitrary")),
    )(z, z, z, cst["cos2"], cst["sin2"], cst["dmask"], cst["xi"], cst["zeta"], cst["gdec"])


def _ret_bwd(z, cst, sprev, dr):
    def body(q_ref, k_ref, v_ref, cos_ref, sin_ref, dm_ref, xi_ref, zt_ref, gd_ref, sp_ref, dr_ref,
             dq_ref, dk_ref, dv_ref, gst):
        i = pl.program_id(1)

        @pl.when(i == 0)
        def _():
            gst[...] = jnp.zeros_like(gst)

        cos, sin = cos_ref[...], sin_ref[...]
        dm, xi, zt = dm_ref[0], xi_ref[0], zt_ref[0]
        qr = _rot(q_ref[...], cos, sin)
        kr = _rot(k_ref[...], cos, sin) * RSCALE
        qb, kb, vb = qr.astype(BF), kr.astype(BF), v_ref[...].astype(BF)
        sd = (_dg(qb, kb, NT) * dm).astype(BF)
        qx = (qr * xi).astype(BF)
        kz = (kr * zt).astype(BF)
        drb = dr_ref[...]
        sb = sp_ref[0, 0].astype(BF)
        g = gst[...]
        gb = g.astype(BF)
        ds = (_dg(drb, vb, NT) * dm).astype(BF)
        dq = _dot(ds, kb) + _dg(drb, sb, NT) * xi
        dk = _dg(ds, qb, TN) + _dg(vb, gb, NT) * zt
        dv = _dg(sd, drb, TN) + _dot(kz, gb)
        gst[...] = g * gd_ref[0] + _dg(qx, drb, TN)
        dq_ref[...] = (dq * cos + pltpu.roll(dq * sin, 64, 1)).astype(BF)
        dkr = dk * RSCALE
        dk_ref[...] = (dkr * cos + pltpu.roll(dkr * sin, 64, 1)).astype(BF)
        dv_ref[...] = dv.astype(BF)

    rev = lambda n: NCH - 1 - n
    return pl.pallas_call(
        body, name="ret_bwd", grid=(RH, NCH),
        in_specs=_ret_specs(rev) + [
            pl.BlockSpec((1, 1, RDK, RDV), lambda h, n: (rev(n), h, 0, 0)),
            pl.BlockSpec((C, RDV), lambda h, n: (rev(n), h)),
        ],
        out_specs=[pl.BlockSpec((C, RDK), lambda h, n: (rev(n), h)),
                   pl.BlockSpec((C, RDK), lambda h, n: (rev(n), h)),
                   pl.BlockSpec((C, RDV), lambda h, n: (rev(n), h))],
        out_shape=[jax.ShapeDtypeStruct((T, RH * RDK), BF), jax.ShapeDtypeStruct((T, RH * RDK), BF),
                   jax.ShapeDtypeStruct((T, RH * RDV), BF)],
        scratch_shapes=[pltpu.VMEM((RDK, RDV), F32)],
        compiler_params=_params(("parallel", "arbitrary")),
    )(z, z, z, cst["cos2"], cst["sin2"], cst["dmask"], cst["xi"], cst["zeta"], cst["gdec"], sprev, dr)


def _log_sigmoid(x):
    return -(jnp.maximum(-x, 0.0) + jnp.log1p(jnp.exp(-jnp.abs(x))))


def _fox_prep(zf, bf_pad, cst):
    def body(zf_ref, b_ref, tri_ref, spread_ref, cb_ref, ct_ref, carry):
        n = pl.program_id(0)

        @pl.when(n == 0)
        def _():
            carry[...] = jnp.zeros_like(carry)

        ls = _log_sigmoid(zf_ref[...] + b_ref[...])
        row = n * C + lax.broadcasted_iota(jnp.int32, (C, C), 0)
        lf = jnp.where(row >= PAD, ls, 0.0)
        cc = jnp.dot(tri_ref[...], lf, precision=HI, preferred_element_type=F32) + carry[0:1, :]
        carry[...] = jnp.broadcast_to(cc[C - 1:C, :], carry.shape)
        cb_ref[...] = jnp.dot(cc, spread_ref[...], precision=HI, preferred_element_type=F32)
        pos = n * C + lax.broadcasted_iota(jnp.int32, (FH, C), 1)
        ct_ref[0] = jnp.where(pos >= PAD, cc.T[:FH, :], -NEG)

    return pl.pallas_call(
        body, name="fox_prep", grid=(NCH,),
        in_specs=[pl.BlockSpec((C, C), lambda n: (n, 0)), pl.BlockSpec((1, C), lambda n: (0, 0)),
                  pl.BlockSpec((C, C), lambda n: (0, 0)), pl.BlockSpec((C, FH * FD), lambda n: (0, 0))],
        out_specs=[pl.BlockSpec((C, FH * FD), lambda n: (n, 0)), pl.BlockSpec((1, FH, C), lambda n: (n, 0, 0))],
        out_shape=[jax.ShapeDtypeStruct((T, FH * FD), F32), jax.ShapeDtypeStruct((NCH, FH, C), F32)],
        scratch_shapes=[pltpu.VMEM((8, C), F32)],
        compiler_params=_params(("arbitrary",)),
    )(zf, bf_pad, cst["tri"], cst["spread"])


def _lo_lanes(shape):
    return lax.broadcasted_iota(jnp.int32, shape, 1) < FD


def _split_heads(x):
    lo = _lo_lanes(x.shape)
    zero = jnp.zeros_like(x)
    return jnp.concatenate([jnp.where(lo, x, zero), jnp.where(lo, zero, x)], axis=0)


def _spread2(x):
    lo = _lo_lanes(x.shape)
    r = pltpu.roll(x, FD, 1)
    return jnp.concatenate([jnp.where(lo, x, r), jnp.where(lo, r, x)], axis=1)


FOX_GROUP = 4


def _fox_tile_ids(i, t):
    out = []
    for u in range(FOX_GROUP):
        j = FOX_GROUP * t + u
        kind = jnp.where(j < i, 0, jnp.where(j == i, 1, 2))
        out.append((jnp.minimum(j, i), kind))
    return out


def _fox_scores(qb, kk, row_bias, ct_ref, mb_ref, p, j, kind):
    cj = jnp.concatenate([ct_ref[j, pl.ds(2 * p, 1), :], ct_ref[j, pl.ds(2 * p + 1, 1), :]], axis=1)
    return _dg(qb, kk, NT) * FSCALE + ((row_bias - cj) + mb_ref[kind])


def _fox_groups(i, group):
    def step(t, carry):
        group(t)
        return carry

    lax.fori_loop(0, (i + FOX_GROUP) // FOX_GROUP, step, 0)


def _fox_fwd(z, cb, ct, cst):
    def body(q_ref, k_ref, v_ref, cb_ref, ct_ref, ones_ref, mb_ref, a_ref, g_ref, mx, acc):
        p, i = pl.program_id(0), pl.program_id(1)
        qb = q_ref[...].astype(BF)
        cbt = cb_ref[...]
        ci = _spread2(cbt)
        ones = ones_ref[...]

        def scores(j, kind):
            ks = pl.multiple_of(j * C, C)
            kk = _split_heads(k_ref[pl.ds(ks, C), :]).astype(BF)
            return _fox_scores(qb, kk, ci, ct_ref, mb_ref, p, j, kind)

        def pass_max(t):
            ss = [scores(j, kind) for j, kind in _fox_tile_ids(i, t)]
            while len(ss) > 1:
                ss = [jnp.maximum(a_, b_) for a_, b_ in zip(ss[::2], ss[1::2])]
            mx[...] = jnp.maximum(mx[...], ss[0])

        mx[...] = jnp.full(mx.shape, NEG, F32)
        _fox_groups(i, pass_max)
        m = jnp.concatenate(
            [jnp.broadcast_to(jnp.max(mx[:, :C], axis=1, keepdims=True), (C, C)),
             jnp.broadcast_to(jnp.max(mx[:, C:], axis=1, keepdims=True), (C, C))], axis=1)

        def pass_sum(t):
            ids = _fox_tile_ids(i, t)
            ss = [scores(j, kind) for j, kind in ids]
            pes = [jnp.exp(s - m).astype(BF) for s in ss]
            vvs = [jnp.concatenate([_split_heads(v_ref[pl.ds(pl.multiple_of(j * C, C), C), :]).astype(BF), ones],
                                   axis=1) for j, _ in ids]
            parts = [_dot(pe, vv) for pe, vv in zip(pes, vvs)]
            while len(parts) > 1:
                parts = [a_ + b_ for a_, b_ in zip(parts[::2], parts[1::2])]
            acc[...] += parts[0]

        acc[...] = jnp.zeros_like(acc)
        _fox_groups(i, pass_sum)
        res = acc[...]
        l = res[:, C:]
        a_ref[...] = res[:, :C] / l
        g_ref[...] = cbt - (jnp.where(_lo_lanes((C, C)), m[:, :C], m[:, C:]) + jnp.log(l))

    return pl.pallas_call(
        body, name="fox_fwd", grid=(NPAIR, NCH),
        in_specs=[pl.BlockSpec((C, C), lambda p, i: (i, QB_F + p)),
                  pl.BlockSpec((T, C), lambda p, i: (0, KB_F + p)),
                  pl.BlockSpec((T, C), lambda p, i: (0, VB_F + p)),
                  pl.BlockSpec((C, C), lambda p, i: (i, p)),
                  pl.BlockSpec((NCH, FH, C), lambda p, i: (0, 0, 0)),
                  pl.BlockSpec((2 * C, C), lambda p, i: (0, 0)),
                  pl.BlockSpec((3, C, 2 * C), lambda p, i: (0, 0, 0))],
        out_specs=[pl.BlockSpec((C, C), lambda p, i: (i, p)), pl.BlockSpec((C, C), lambda p, i: (i, p))],
        out_shape=[jax.ShapeDtypeStruct((T, FH * FD), F32), jax.ShapeDtypeStruct((T, FH * FD), F32)],
        scratch_shapes=[pltpu.VMEM((C, 2 * C), F32), pltpu.VMEM((C, 2 * C), F32)],
        compiler_params=_params(("parallel", "arbitrary")),
    )(z, z, z, cb, ct, cst["ones_aug"], cst["mask_bias"])


def _fox_bwd(z, da, g, delta, ct, cst):
    def body(q_ref, da_ref, g_ref, dl_ref, k_ref, v_ref, ct_ref, ones_ref, mb_ref,
             dq_ref, dr_ref, dk_ref, dv_ref, dcs_ref, dkacc, dvacc, csacc, dqacc):
        p, i = pl.program_id(0), pl.program_id(1)

        @pl.when(i == 0)
        def _():
            dkacc[...] = jnp.zeros_like(dkacc)
            dvacc[...] = jnp.zeros_like(dvacc)
            csacc[...] = jnp.zeros_like(csacc)

        ones = ones_ref[...]
        qf = q_ref[...]
        qb = qf.astype(BF)
        dab = da_ref[...]
        qq = jnp.concatenate([_split_heads(qf).astype(BF), ones], axis=1)
        dd = _split_heads(dab.astype(F32)).astype(BF)
        gi = _spread2(g_ref[...])
        dl = _spread2(dl_ref[...])
        dqacc[...] = jnp.zeros_like(dqacc)

        def group(t):
            ids = _fox_tile_ids(i, t)
            rows = [pl.ds(pl.multiple_of(j * C, C), C) for j, _ in ids]
            kks = [_split_heads(k_ref[r, :]).astype(BF) for r in rows]
            vvs = [_split_heads(v_ref[r, :]).astype(BF) for r in rows]
            ss = [_fox_scores(qb, kk, gi, ct_ref, mb_ref, p, j, kind) for kk, (j, kind) in zip(kks, ids)]
            dps = [_dg(dab, vv, NT) for vv in vvs]
            pes = [jnp.exp(s) for s in ss]
            dss = [pe * (dp - dl) * FSCALE for pe, dp in zip(pes, dps)]
            pts = [jnp.concatenate([pe[:, :C].T, pe[:, C:].T], axis=1).astype(BF) for pe in pes]
            dsts = [jnp.concatenate([ds[:, :C].T, ds[:, C:].T], axis=1).astype(BF) for ds in dss]
            dvs = [_dot(pt, dd) for pt in pts]
            rs = [_dot(dst, qq) for dst in dsts]
            parts = [_dot(ds.astype(BF), jnp.concatenate([kk, ones], axis=1)) for ds, kk in zip(dss, kks)]
            for r, dv, rr in zip(rows, dvs, rs):
                dvacc[r, :] += dv
                dkacc[r, :] += rr[:, :C]
                csacc[r, :] += rr[:, C:]
            while len(parts) > 1:
                parts = [a_ + b_ for a_, b_ in zip(parts[::2], parts[1::2])]
            dqacc[...] += parts[0]

        _fox_groups(i, group)
        res = dqacc[...]
        dq_ref[...] = res[:, :C].astype(BF)
        dr_ref[...] = res[:, C:]

        @pl.when(i == NCH - 1)
        def _():
            dk_ref[...] = dkacc[...].astype(BF)
            dv_ref[...] = dvacc[...].astype(BF)
            dcs_ref[...] = csacc[...]

    blk = pl.BlockSpec((C, C), lambda p, i: (i, p))
    col = pl.BlockSpec((T, C), lambda p, i: (0, p))
    return pl.pallas_call(
        body, name="fox_bwd", grid=(NPAIR, NCH),
        in_specs=[pl.BlockSpec((C, C), lambda p, i: (i, QB_F + p)), blk, blk, blk,
                  pl.BlockSpec((T, C), lambda p, i: (0, KB_F + p)),
                  pl.BlockSpec((T, C), lambda p, i: (0, VB_F + p)),
                  pl.BlockSpec((NCH, FH, C), lambda p, i: (0, 0, 0)),
                  pl.BlockSpec((2 * C, C), lambda p, i: (0, 0)),
                  pl.BlockSpec((3, C, 2 * C), lambda p, i: (0, 0, 0))],
        out_specs=[blk, blk, col, col, col],
        out_shape=[jax.ShapeDtypeStruct((T, FH * FD), BF), jax.ShapeDtypeStruct((T, FH * FD), F32),
                   jax.ShapeDtypeStruct((T, FH * FD), BF), jax.ShapeDtypeStruct((T, FH * FD), BF),
                   jax.ShapeDtypeStruct((T, FH * FD), F32)],
        scratch_shapes=[pltpu.VMEM((T, C), F32), pltpu.VMEM((T, C), F32), pltpu.VMEM((T, C), F32),
                        pltpu.VMEM((C, 2 * C), F32)],
        compiler_params=_params(("parallel", "arbitrary")),
    )(z, da, g, delta, z, z, ct, cst["ones_aug"], cst["mask_bias"])


def _fox_gate_bwd(drow, dcol, zf, bf_pad, cst):
    def body(dr_ref, dc_ref, zf_ref, b_ref, tri_ref, pick_ref, dff_ref, db_ref, carry):
        s = pl.program_id(0)
        n = NCH - 1 - s

        @pl.when(s == 0)
        def _():
            carry[...] = jnp.zeros_like(carry)
            db_ref[...] = jnp.zeros_like(db_ref)

        dcb = jnp.dot((dr_ref[...] - dc_ref[...]) * (1.0 / FSCALE), pick_ref[...], precision=HI,
                      preferred_element_type=F32)
        suf = lax.dot_general(tri_ref[...], dcb, TN, precision=HI, preferred_element_type=F32) + carry[0:1, :]
        carry[...] = jnp.broadcast_to(suf[0:1, :], carry.shape)
        x = zf_ref[...] + b_ref[...]
        row = n * C + lax.broadcasted_iota(jnp.int32, (C, C), 0)
        dff = jnp.where(row >= PAD, suf * (1.0 - jax.nn.sigmoid(x)), 0.0)
        dff_ref[...] = dff.astype(BF)
        db_ref[...] += jnp.sum(dff, axis=0, keepdims=True)

    rev = lambda s: (NCH - 1 - s, 0)
    return pl.pallas_call(
        body, name="fox_gate_bwd", grid=(NCH,),
        in_specs=[pl.BlockSpec((C, FH * FD), rev), pl.BlockSpec((C, FH * FD), rev), pl.BlockSpec((C, C), rev),
                  pl.BlockSpec((1, C), lambda s: (0, 0)), pl.BlockSpec((C, C), lambda s: (0, 0)),
                  pl.BlockSpec((FH * FD, C), lambda s: (0, 0))],
        out_specs=[pl.BlockSpec((C, C), rev), pl.BlockSpec((1, C), lambda s: (0, 0))],
        out_shape=[jax.ShapeDtypeStruct((T, C), BF), jax.ShapeDtypeStruct((1, C), F32)],
        scratch_shapes=[pltpu.VMEM((8, C), F32)],
        compiler_params=_params(("arbitrary",)),
    )(drow, dcol, zf, bf_pad, cst["tri"], cst["pick"])


def _gated(r, rg, a, fg):
    rn, rs = [], []
    for h in range(RH):
        rh = r[:, RDV * h:RDV * (h + 1)]
        s = lax.rsqrt(jnp.mean(rh * rh, axis=1, keepdims=True) + EPS)
        rn.append(rh * s)
        rs.append(s)
    rn = jnp.concatenate(rn, axis=1)
    y = jnp.concatenate([rn * (rg * jax.nn.sigmoid(rg)), a * (fg * jax.nn.sigmoid(fg))], axis=1)
    return y, rn, rs


def _out_loss(r, z, a, wout, x, tgt, fgain):
    def body(r_ref, rg_ref, a_ref, fg_ref, w_ref, x_ref, t_ref, g_ref, yt_ref, do_ref, dob_ref, loss_ref, dg_ref):
        i = pl.program_id(0)

        @pl.when(i == 0)
        def _():
            yt_ref[...] = jnp.zeros_like(yt_ref)
            do_ref[...] = jnp.zeros_like(do_ref)
            dob_ref[...] = jnp.zeros_like(dob_ref)
            loss_ref[...] = jnp.zeros_like(loss_ref)
            dg_ref[...] = jnp.zeros_like(dg_ref)

        @pl.when(i > 0)
        def _():
            y, _, _ = _gated(r_ref[...], rg_ref[...], a_ref[...], fg_ref[...])
            yt_ref[...] = y.T.astype(BF)
            o = x_ref[...] + _dot(y.astype(BF), w_ref[...])
            rs = lax.rsqrt(jnp.mean(o * o, axis=1, keepdims=True) + EPS)
            on = o * rs
            g = g_ref[...]
            e = on * g - t_ref[...]
            loss_ref[...] += 0.5 * jnp.sum(jnp.mean(e * e, axis=1, keepdims=True))
            dyh = e * (1.0 / D)
            dg_ref[...] += jnp.sum(dyh * on, axis=0, keepdims=True)
            don = dyh * g
            do = rs * (don - on * jnp.mean(don * on, axis=1, keepdims=True))
            do_ref[...] = do
            dob_ref[...] = do.astype(BF)

    tok = lambda i: (jnp.maximum(i - 1, 0), 0)
    return pl.pallas_call(
        body, name="out_loss", grid=(NCH,),
        in_specs=[pl.BlockSpec((C, D), lambda i: (i, 0)), pl.BlockSpec((C, D), lambda i: (i, GB_R)),
                  pl.BlockSpec((C, D), lambda i: (i, 0)), pl.BlockSpec((C, D), lambda i: (i, GB_F)),
                  pl.BlockSpec((DMIX, D), lambda i: (0, 0)),
                  pl.BlockSpec((C, D), tok), pl.BlockSpec((C, D), tok), pl.BlockSpec((1, D), lambda i: (0, 0))],
        out_specs=[pl.BlockSpec((DMIX, C), lambda i: (0, i)), pl.BlockSpec((C, D), lambda i: (i, 0)),
                   pl.BlockSpec((C, D), lambda i: (i, 0)), pl.BlockSpec((8, C), lambda i: (0, 0)),
                   pl.BlockSpec((1, D), lambda i: (0, 0))],
        out_shape=[jax.ShapeDtypeStruct((DMIX, T), BF), jax.ShapeDtypeStruct((T, D), F32),
                   jax.ShapeDtypeStruct((T, D), BF), jax.ShapeDtypeStruct((8, C), F32),
                   jax.ShapeDtypeStruct((1, D), F32)],
        compiler_params=_params(("arbitrary",)),
    )(r, z, a, z, wout, x, tgt, fgain)


def _dsilu(x):
    s = jax.nn.sigmoid(x)
    return s * (1.0 + x * (1.0 - s))


def _dy_gate_bwd(dob, wout, r, z, a, seg):
    def body(do_ref, w_ref, r_ref, rg_ref, a_ref, fg_ref, seg_ref, dr_ref, da_ref, drg_ref, dfg_ref, dl_ref):
        dy = _dg(do_ref[...], w_ref[...], NT)
        rg, fg, a_ = rg_ref[...], fg_ref[...], a_ref[...]
        _, rn, rs = _gated(r_ref[...], rg, a_, fg)
        dyr, dyf = dy[:, :D], dy[:, D:]
        drn = dyr * (rg * jax.nn.sigmoid(rg))
        drg_ref[...] = (dyr * rn * _dsilu(rg)).astype(BF)
        for h in range(RH):
            sl = slice(RDV * h, RDV * (h + 1))
            dh, nh = drn[:, sl], rn[:, sl]
            dr_ref[:, sl] = (rs[h] * (dh - nh * jnp.mean(dh * nh, axis=1, keepdims=True))).astype(BF)
        dab = (dyf * (fg * jax.nn.sigmoid(fg))).astype(BF)
        da_ref[...] = dab
        dfg_ref[...] = (dyf * a_ * _dsilu(fg)).astype(BF)
        prod = dab.astype(F32) * a_
        segm = seg_ref[...]
        for p in range(NPAIR):
            sl = slice(C * p, C * (p + 1))
            hi = prod[:, sl].astype(BF)
            lo = (prod[:, sl] - hi.astype(F32)).astype(BF)
            dl_ref[:, sl] = _dot(hi, segm) + _dot(lo, segm)

    row = lambda i: (i, 0)
    return pl.pallas_call(
        body, name="dy_gate_bwd", grid=(NCH,),
        in_specs=[pl.BlockSpec((C, D), row), pl.BlockSpec((DMIX, D), lambda i: (0, 0)),
                  pl.BlockSpec((C, D), row), pl.BlockSpec((C, D), lambda i: (i, GB_R)),
                  pl.BlockSpec((C, D), row), pl.BlockSpec((C, D), lambda i: (i, GB_F)),
                  pl.BlockSpec((C, C), lambda i: (0, 0))],
        out_specs=[pl.BlockSpec((C, D), row)] * 5,
        out_shape=[jax.ShapeDtypeStruct((T, D), BF)] * 4 + [jax.ShapeDtypeStruct((T, D), F32)],
        compiler_params=_params(("parallel",)),
    )(dob, wout, r, z, a, z, seg)


def _du_norm_bwd(dzm, dzf, wm, wf, hpad, g, dopad):
    tm, tk = 544, 1024
    nk = WMAIN // tk

    def body(dzm_ref, dzf_ref, wm_ref, wf_ref, h_ref, g_ref, do_ref, gh_ref, dg_ref, acc):
        i, k = pl.program_id(0), pl.program_id(1)

        @pl.when(k == 0)
        def _():
            acc[...] = _dg(dzf_ref[...], wf_ref[...], NT)

        acc[...] += _dg(dzm_ref[...], wm_ref[...], NT)

        @pl.when(k == nk - 1)
        def _():
            du = acc[...]
            h = h_ref[...]
            gg = g_ref[...]
            rs = lax.rsqrt(jnp.mean(h * h, axis=1, keepdims=True) + EPS)
            hn = h * rs
            part = jnp.sum(du * hn, axis=0, keepdims=True)

            @pl.when(i == 0)
            def _():
                dg_ref[...] = part

            @pl.when(i > 0)
            def _():
                dg_ref[...] += part

            dhn = du * gg
            gh_ref[...] = rs * (dhn - hn * jnp.mean(dhn * hn, axis=1, keepdims=True)) + do_ref[...]

    return pl.pallas_call(
        body, name="du_norm_bwd", grid=(T // tm, nk),
        in_specs=[pl.BlockSpec((tm, tk), lambda i, k: (i, k)), pl.BlockSpec((tm, C), lambda i, k: (i, 0)),
                  pl.BlockSpec((D, tk), lambda i, k: (0, k)), pl.BlockSpec((D, C), lambda i, k: (0, 0)),
                  pl.BlockSpec((tm, D), lambda i, k: (i, 0)), pl.BlockSpec((1, D), lambda i, k: (0, 0)),
                  pl.BlockSpec((tm, D), lambda i, k: (i, 0))],
        out_specs=[pl.BlockSpec((tm, D), lambda i, k: (i, 0)), pl.BlockSpec((1, D), lambda i, k: (0, 0))],
        out_shape=[jax.ShapeDtypeStruct((T, D), F32), jax.ShapeDtypeStruct((1, D), F32)],
        scratch_shapes=[pltpu.VMEM((tm, D), F32)],
        compiler_params=_params(("arbitrary", "arbitrary")),
    )(dzm, dzf, wm, wf, hpad, g, dopad)


def _local_step(x, tgt, meta, norm_g, wm, wf, b_f, wout, final_g):
    cst = _constants()
    hpad = jnp.concatenate([jnp.pad(meta, ((PAD, 0), (0, 0))), x], axis=0)
    bf_pad = jnp.pad(b_f, ((0, 0), (0, C - NFF)))
    u, ut = _norm_in(hpad, norm_g)
    z = _mm_nn(u, wm, T // 2, 512, "in_proj")
    zf = _mm_nn(u, wf, T // 2, C, "in_proj_ff")
    r, sprev = _ret_fwd(z, cst)
    cb, ct = _fox_prep(zf, bf_pad, cst)
    a, g = _fox_fwd(z, cb, ct, cst)
    yt, dopad, dob, loss8, dfg = _out_loss(r, z, a, wout, x, tgt, final_g)
    dr, da, dzrg, dzfg, delta = _dy_gate_bwd(dob, wout, r, z, a, cst["seg"])
    dwout = _mm_nn(yt, dob, 512, D, "dw_out")
    dzq_r, dzk_r, dzv_r = _ret_bwd(z, cst, sprev, dr)
    dzq_f, drow, dzk_f, dzv_f, dcol = _fox_bwd(z, da, g, delta, ct, cst)
    dzf, dbf = _fox_gate_bwd(drow, dcol, zf, bf_pad, cst)
    dzm = jnp.concatenate([dzq_r, dzk_r, dzv_r, dzrg, dzq_f, dzk_f, dzv_f, dzfg], axis=1)
    dwm = _mm_nn(ut, dzm, 512, 1024, "dw_in")
    dwf = _mm_nn(ut, dzf, D, C, "dw_in_ff")
    gh, dng = _du_norm_bwd(dzm, dzf, wm, wf, hpad, norm_g, dopad)
    dwin = jnp.concatenate([dwm, dwf[:, :NFF]], axis=1)
    return (loss8[0, 0], gh[C:], gh[PAD:C], dng, dwin, dbf[:, :NFF], dwout, dfg)


def _place():
    x, y, c = lax.axis_index("x"), lax.axis_index("y"), lax.axis_index("c")
    return x, y, c


def _other_chips(x, y):
    return [(1 - x, y, 2 * (1 - x) + y), (x, 1 - y, 2 * x + (1 - y)), (1 - x, 1 - y, 2 * (1 - x) + (1 - y))]


def _all_gather_shards(shards):
    n = len(shards)

    def body(*refs):
        ins, outs = refs[:n], refs[n:2 * n]
        send_sems, recv_sems, local_sems = refs[2 * n:]
        x, y, c = _place()
        me_s = 2 * x + y
        sib = (x, y, 1 - c)
        chips = _other_chips(x, y)
        local, sends, waits = [], [], []
        for a in range(n):
            rows = ins[a].shape[0] // 2
            half = pl.ds(c * rows, rows)
            cp = pltpu.make_async_copy(ins[a], outs[a].at[me_s], local_sems.at[a])
            cp.start()
            local.append(cp)
            for k, (cx, cy, cs) in enumerate(chips):
                sends.append(pltpu.make_async_remote_copy(
                    src_ref=ins[a].at[half], dst_ref=outs[a].at[me_s, half],
                    send_sem=send_sems.at[6 * a + k], recv_sem=recv_sems.at[6 * a + k],
                    device_id=(cx, cy, c), device_id_type=MESH))
                sends[-1].start()
        for a in range(n):
            rows = ins[a].shape[0] // 2
            half = pl.ds(c * rows, rows)
            other = pl.ds((1 - c) * rows, rows)
            for k, (cx, cy, cs) in enumerate(chips):
                pltpu.make_async_remote_copy(
                    src_ref=outs[a].at[cs, half], dst_ref=outs[a].at[cs, half],
                    send_sem=send_sems.at[6 * a + k], recv_sem=recv_sems.at[6 * a + k],
                    device_id=(cx, cy, c), device_id_type=MESH).wait_recv()
                fwd = pltpu.make_async_remote_copy(
                    src_ref=outs[a].at[cs, half], dst_ref=outs[a].at[cs, half],
                    send_sem=send_sems.at[6 * a + 3 + k], recv_sem=recv_sems.at[6 * a + 3 + k],
                    device_id=sib, device_id_type=MESH)
                fwd.start()
                sends.append(fwd)
                waits.append(pltpu.make_async_remote_copy(
                    src_ref=outs[a].at[cs, other], dst_ref=outs[a].at[cs, other],
                    send_sem=send_sems.at[6 * a + 3 + k], recv_sem=recv_sems.at[6 * a + 3 + k],
                    device_id=sib, device_id_type=MESH))
        for w in waits:
            w.wait_recv()
        for s in sends:
            s.wait_send()
        for cp in local:
            cp.wait()

    return pl.pallas_call(
        body, name="all_gather_w",
        in_specs=[ANY] * n, out_specs=[ANY] * n,
        out_shape=[jax.ShapeDtypeStruct((4,) + s.shape, s.dtype) for s in shards],
        scratch_shapes=[pltpu.SemaphoreType.DMA((6 * n,)), pltpu.SemaphoreType.DMA((6 * n,)),
                        pltpu.SemaphoreType.DMA((n,))],
    )(*shards)


def _pair_swap(arrs, small):
    n = len(arrs)

    def body(*refs):
        ins, sm = refs[:n], refs[n]
        outs, smo = refs[n + 1:2 * n + 1], refs[2 * n + 1]
        send_sems, recv_sems = refs[2 * n + 2:]
        x, y, c = _place()
        sib = (x, y, 1 - c)
        cps = []
        for a in range(n):
            rows = ins[a].shape[1] // 2
            cps.append(pltpu.make_async_remote_copy(
                src_ref=ins[a].at[:, pl.ds((1 - c) * rows, rows)], dst_ref=outs[a],
                send_sem=send_sems.at[a], recv_sem=recv_sems.at[a], device_id=sib, device_id_type=MESH))
        cps.append(pltpu.make_async_remote_copy(
            src_ref=sm, dst_ref=smo, send_sem=send_sems.at[n], recv_sem=recv_sems.at[n],
            device_id=sib, device_id_type=MESH))
        for cp in cps:
            cp.start()
        for cp in cps:
            cp.wait()

    return pl.pallas_call(
        body, name="rs_pair_swap",
        in_specs=[ANY] * (n + 1), out_specs=[ANY] * (n + 1),
        out_shape=[jax.ShapeDtypeStruct((4, a.shape[1] // 2, a.shape[2]), a.dtype) for a in arrs]
        + [jax.ShapeDtypeStruct(small.shape, small.dtype)],
        scratch_shapes=[pltpu.SemaphoreType.DMA((n + 1,)), pltpu.SemaphoreType.DMA((n + 1,))],
    )(*arrs, small)


def _chip_exchange(parts, small):
    n = len(parts)

    def body(*refs):
        ins, sm = refs[:n], refs[n]
        outs, smo = refs[n + 1:2 * n + 1], refs[2 * n + 1]
        send_sems, recv_sems, local_sems = refs[2 * n + 2:]
        x, y, c = _place()
        me_s = 2 * x + y
        chips = _other_chips(x, y)
        cps, local = [], []
        for a in range(n + 1):
            src = ins[a] if a < n else sm
            dst = outs[a] if a < n else smo
            mine = src.at[me_s] if a < n else src
            cp = pltpu.make_async_copy(mine, dst.at[me_s], local_sems.at[a])
            cp.start()
            local.append(cp)
            for k, (cx, cy, cs) in enumerate(chips):
                cps.append(pltpu.make_async_remote_copy(
                    src_ref=src.at[cs] if a < n else src, dst_ref=dst.at[me_s],
                    send_sem=send_sems.at[3 * a + k], recv_sem=recv_sems.at[3 * a + k],
                    device_id=(cx, cy, c), device_id_type=MESH))
        for cp in cps:
            cp.start()
        for cp in cps:
            cp.wait()
        for cp in local:
            cp.wait()

    return pl.pallas_call(
        body, name="rs_chip_exchange",
        in_specs=[ANY] * (n + 1), out_specs=[ANY] * (n + 1),
        out_shape=[jax.ShapeDtypeStruct(p.shape, p.dtype) for p in parts]
        + [jax.ShapeDtypeStruct((4,) + small.shape, small.dtype)],
        scratch_shapes=[pltpu.SemaphoreType.DMA((3 * (n + 1),)), pltpu.SemaphoreType.DMA((3 * (n + 1),)),
                        pltpu.SemaphoreType.DMA((n + 1,))],
    )(*parts, small)


def _pair_gather(halves):
    n = len(halves)

    def body(*refs):
        ins, outs = refs[:n], refs[n:2 * n]
        send_sems, recv_sems, local_sems = refs[2 * n:]
        x, y, c = _place()
        sib = (x, y, 1 - c)
        cps, local = [], []
        for a in range(n):
            rows = ins[a].shape[0]
            dst = outs[a].at[pl.ds(c * rows, rows)]
            cp = pltpu.make_async_copy(ins[a], dst, local_sems.at[a])
            cp.start()
            local.append(cp)
            cps.append(pltpu.make_async_remote_copy(
                src_ref=ins[a], dst_ref=dst, send_sem=send_sems.at[a], recv_sem=recv_sems.at[a],
                device_id=sib, device_id_type=MESH))
        for cp in cps:
            cp.start()
        for cp in cps:
            cp.wait()
        for cp in local:
            cp.wait()

    return pl.pallas_call(
        body, name="rs_pair_gather",
        in_specs=[ANY] * n, out_specs=[ANY] * n,
        out_shape=[jax.ShapeDtypeStruct((2 * h.shape[0], h.shape[1]), h.dtype) for h in halves],
        scratch_shapes=[pltpu.SemaphoreType.DMA((n,)), pltpu.SemaphoreType.DMA((n,)),
                        pltpu.SemaphoreType.DMA((n,))],
    )(*halves)


def _row_block(rows):
    for tb in (256, 128, 64, 32, 16, 8):
        if rows % tb == 0:
            return tb
    return rows


def _add_halves(full, recv, name):
    _, r2, w = recv.shape
    tb = _row_block(r2)
    nb = r2 // tb
    c = lax.axis_index("c")

    def body(c_ref, a_ref, b_ref, o_ref):
        o_ref[...] = a_ref[...] + b_ref[...]

    return pl.pallas_call(
        body, name=name,
        grid_spec=pltpu.PrefetchScalarGridSpec(
            num_scalar_prefetch=1, grid=(4, nb),
            in_specs=[pl.BlockSpec((1, tb, w), lambda s, i, cr: (s, cr[0] * nb + i, 0)),
                      pl.BlockSpec((1, tb, w), lambda s, i, cr: (s, i, 0))],
            out_specs=pl.BlockSpec((1, tb, w), lambda s, i, cr: (s, i, 0))),
        out_shape=jax.ShapeDtypeStruct(recv.shape, recv.dtype),
        compiler_params=_params(("parallel", "parallel")),
    )(jnp.reshape(c, (1,)).astype(jnp.int32), full, recv)


def _add2(a, b, name):
    def body(a_ref, b_ref, o_ref):
        o_ref[...] = a_ref[...] + b_ref[...]

    return pl.pallas_call(body, name=name, out_shape=jax.ShapeDtypeStruct(a.shape, a.dtype))(a, b)


def _sum4(buf, name):
    _, r, w = buf.shape
    tb = _row_block(r)

    def body(b_ref, o_ref):
        o_ref[...] = ((b_ref[0] + b_ref[1]) + b_ref[2]) + b_ref[3]

    return pl.pallas_call(
        body, name=name, grid=(r // tb,),
        in_specs=[pl.BlockSpec((4, tb, w), lambda i: (0, i, 0))],
        out_specs=pl.BlockSpec((tb, w), lambda i: (i, 0)),
        out_shape=jax.ShapeDtypeStruct((r, w), buf.dtype),
        compiler_params=_params(("parallel",)),
    )(buf)


def _adamw(w, g, m, v, name):
    r, c_ = w.shape
    tb = _row_block(r)

    def body(w_ref, g_ref, m_ref, v_ref, d_ref, mo_ref, vo_ref):
        g_ = g_ref[...]
        mn = B1 * m_ref[...] + (1.0 - B1) * g_
        vn = B2 * v_ref[...] + (1.0 - B2) * (g_ * g_)
        m_hat = mn / (1.0 - B1 ** STEP)
        v_hat = vn / (1.0 - B2 ** STEP)
        d_ref[...] = -LR * (m_hat / (jnp.sqrt(v_hat) + AEPS) + WD * w_ref[...])
        mo_ref[...] = mn
        vo_ref[...] = vn

    spec = pl.BlockSpec((tb, c_), lambda i: (i, 0))
    return pl.pallas_call(
        body, name=name, grid=(r // tb,),
        in_specs=[spec] * 4, out_specs=[spec] * 3,
        out_shape=[jax.ShapeDtypeStruct(w.shape, F32)] * 3,
        compiler_params=_params(("parallel",)),
    )(w, g, m, v)


def kernel(x, meta_tokens, norm_g, w_in, b_f, w_out, final_g, loss_target, m_meta_tokens, m_norm_g, m_w_in, m_b_f, m_w_out, m_final_g, v_meta_tokens, v_norm_g, v_w_in, v_b_f, v_w_out, v_final_g):
    gin, gout, gmeta = _all_gather_shards([w_in[0].astype(BF), w_out[0].astype(BF), meta_tokens])
    wfull = jnp.concatenate([gin[0], gin[1], gin[2], gin[3]], axis=1)
    wm = wfull[:, :WMAIN]
    wf = jnp.pad(wfull[:, WMAIN:], ((0, 0), (0, C - NFF)))
    wout = gout.reshape(DMIX, D)
    meta = jnp.concatenate([gmeta[0], gmeta[1], gmeta[2], gmeta[3]], axis=1)

    loss, gx, dmeta, dng, dwin, dbf, dwout, dfg = _local_step(
        x[0], loss_target[0], meta, norm_g, wm, wf, b_f, wout, final_g.reshape(1, D))

    g_in = jnp.stack([dwin[:, WSH * s:WSH * (s + 1)] for s in range(4)])
    g_out = dwout.reshape(4, DMIX // 4, D)
    g_meta = jnp.stack([dmeta[:, 256 * s:256 * (s + 1)] for s in range(4)])
    small = jnp.concatenate([dng, dfg, jnp.pad(dbf, ((0, 0), (0, D - NFF))),
                             jnp.pad(jnp.reshape(loss, (1, 1)), ((0, 0), (0, D - 1))),
                             jnp.zeros((4, D), F32)], axis=0)
    r_in, r_out, r_meta, r_small = _pair_swap([g_in, g_out, g_meta], small)
    p_in = _add_halves(g_in, r_in, "pair_add_in")
    p_out = _add_halves(g_out, r_out, "pair_add_out")
    p_meta = _add_halves(g_meta, r_meta, "pair_add_meta")
    p_small = _add2(small, r_small, "pair_add_small")
    e_in, e_out, e_meta, e_small = _chip_exchange([p_in, p_out, p_meta], p_small)
    h_in, h_out, h_meta = _sum4(e_in, "sum_in"), _sum4(e_out, "sum_out"), _sum4(e_meta, "sum_meta")
    tot = _sum4(e_small, "sum_small")
    gw_in, gw_out, gw_meta = _pair_gather([h_in, h_out, h_meta])
    g_norm, g_final, g_bf, loss_all = tot[0:1], tot[1], tot[2:3, :NFF], tot[3, 0]

    d_meta, nm_meta, nv_meta = _adamw(meta_tokens, gw_meta, m_meta_tokens, v_meta_tokens, "adamw_meta")
    d_norm, nm_norm, nv_norm = _adamw(norm_g, g_norm, m_norm_g, v_norm_g, "adamw_norm")
    d_in, nm_in, nv_in = _adamw(w_in[0], gw_in, m_w_in[0], v_w_in[0], "adamw_in")
    d_bf, nm_bf, nv_bf = _adamw(b_f, g_bf, m_b_f, v_b_f, "adamw_bf")
    d_out, nm_out, nv_out = _adamw(w_out[0], gw_out, m_w_out[0], v_w_out[0], "adamw_out")
    d_fin, nm_fin, nv_fin = _adamw(final_g.reshape(1, D), g_final.reshape(1, D), m_final_g.reshape(1, D),
                                   v_final_g.reshape(1, D), "adamw_final")
    return (loss_all, gx[None], gw_meta, g_norm, gw_in[None], g_bf, gw_out[None], g_final,
            d_meta, d_norm, d_in[None], d_bf, d_out[None], d_fin.reshape(D),
            nm_meta, nm_norm, nm_in[None], nm_bf, nm_out[None], nm_fin.reshape(D),
            nv_meta, nv_norm, nv_in[None], nv_bf, nv_out[None], nv_fin.reshape(D))
```

```python
import numpy as np
import jax
import jax.numpy as jnp
from jax import lax
from jax.experimental import pallas as pl
from jax.experimental.pallas import tpu as pltpu

D = 1024
SEQ = 2048
NMETA = 16
C = 128
PAD = C - NMETA
T = PAD + NMETA + SEQ
NCH = T // C
RH, RDK, RDV = 4, 128, 256
FH, FD = 16, 64
NPAIR = FH // 2
WMAIN = 7168
NFF = 16
WIN = WMAIN + NFF
WSH = WIN // 4
DMIX = 2048
EPS = 1e-6
NEG = -1e30
RSCALE = RDK ** -0.5
FSCALE = FD ** -0.5
ROPE_BASE = 10000.0
LR, B1, B2, AEPS, WD, STEP = 0.001, 0.9, 0.999, 1e-08, 0.01, 10

BF = jnp.bfloat16
F32 = jnp.float32
NT = (((1,), (1,)), ((), ()))
TN = (((0,), (0,)), ((), ()))
HI = lax.Precision.HIGHEST
MESH = pl.DeviceIdType.MESH
ANY = pl.BlockSpec(memory_space=pl.ANY)
VMEM_LIMIT = 48 * 1024 * 1024

QB_R, KB_R = 0, 4
VB_R = 4
GB_R, GB_F = 2, 6
QB_F, KB_F, VB_F = 24, 32, 40


def _dot(a, b):
    return jnp.dot(a, b, preferred_element_type=F32)


def _dg(a, b, dims):
    return lax.dot_general(a, b, dims, preferred_element_type=F32)


def _params(sem=None):
    return pltpu.CompilerParams(dimension_semantics=sem, vmem_limit_bytes=VMEM_LIMIT)


def _constants():
    pos = jnp.arange(T, dtype=F32) - PAD
    inv = ROPE_BASE ** (-jnp.arange(0, RDK, 2, dtype=F32) / RDK)
    ang = pos[:, None] * inv[None, :]
    cos, sin = jnp.cos(ang), jnp.sin(ang)
    cos2 = jnp.concatenate([cos, cos], axis=1)
    sin2 = jnp.concatenate([-sin, sin], axis=1)
    log_gamma = jnp.log1p(-jnp.exp2(-5.0 - jnp.arange(RH, dtype=F32)))
    idx = jnp.arange(C, dtype=F32)
    diff = idx[:, None] - idx[None, :]
    dmask = jnp.where(diff[None] >= 0, jnp.exp(log_gamma[:, None, None] * jnp.maximum(diff, 0.0)[None]), 0.0)
    zeta = jnp.exp(log_gamma[:, None] * (C - 1.0 - idx)[None, :])
    xi = jnp.exp(log_gamma[:, None] * (idx + 1.0)[None, :])
    gdec = jnp.exp(log_gamma * C)
    zeta_b = jnp.broadcast_to(zeta[:, :, None], (RH, C, RDK))
    xi_b = jnp.broadcast_to(xi[:, :, None], (RH, C, RDK))
    gdec_b = jnp.broadcast_to(gdec[:, None, None], (RH, RDK, RDV))
    tri = jnp.asarray(np.tril(np.ones((C, C), np.float32)))
    head_of_lane = np.arange(FH * FD) // FD
    spread = (np.arange(C)[:, None] == head_of_lane[None, :]).astype(np.float32)
    pick = ((np.arange(FH * FD)[:, None] % FD == 0)
            & (head_of_lane[:, None] == np.arange(C)[None, :])).astype(np.float32)
    seg = (np.arange(C)[:, None] // FD == np.arange(C)[None, :] // FD).astype(np.float32)
    ones_aug = np.concatenate([np.tile((np.arange(C) < FD)[None, :], (C, 1)),
                               np.tile((np.arange(C) >= FD)[None, :], (C, 1))], axis=0).astype(np.float32)
    lane = np.arange(2 * C) % C
    causal = np.where(lane[None, :] <= np.arange(C)[:, None], 0.0, NEG).astype(np.float32)
    mask_bias = np.stack([np.zeros((C, 2 * C), np.float32), causal, np.full((C, 2 * C), NEG, np.float32)])
    return dict(cos2=cos2, sin2=sin2, dmask=dmask, zeta=zeta_b, xi=xi_b, gdec=gdec_b, tri=tri,
                mask_bias=jnp.asarray(mask_bias),
                spread=jnp.asarray(spread), pick=jnp.asarray(pick), seg=jnp.asarray(seg, dtype=BF),
                ones_aug=jnp.asarray(ones_aug, dtype=BF))


def _norm_in(hpad, g):
    def body(h_ref, g_ref, u_ref, ut_ref):
        h = h_ref[...]
        rs = lax.rsqrt(jnp.mean(h * h, axis=1, keepdims=True) + EPS)
        u = h * rs * g_ref[...]
        u_ref[...] = u.astype(BF)
        ut_ref[...] = u.T.astype(BF)

    return pl.pallas_call(
        body, name="norm_in", grid=(NCH,),
        in_specs=[pl.BlockSpec((C, D), lambda i: (i, 0)), pl.BlockSpec((1, D), lambda i: (0, 0))],
        out_specs=[pl.BlockSpec((C, D), lambda i: (i, 0)), pl.BlockSpec((D, C), lambda i: (0, i))],
        out_shape=[jax.ShapeDtypeStruct((T, D), BF), jax.ShapeDtypeStruct((D, T), BF)],
        compiler_params=_params(("parallel",)),
    )(hpad, g)


def _mm_nn(a, b, tm, tn, name):
    m, k = a.shape
    _, n = b.shape

    def body(a_ref, b_ref, o_ref):
        o_ref[...] = _dot(a_ref[...], b_ref[...])

    return pl.pallas_call(
        body, name=name, grid=(m // tm, n // tn),
        in_specs=[pl.BlockSpec((tm, k), lambda i, j: (i, 0)), pl.BlockSpec((k, tn), lambda i, j: (0, j))],
        out_specs=pl.BlockSpec((tm, tn), lambda i, j: (i, j)),
        out_shape=jax.ShapeDtypeStruct((m, n), F32),
        compiler_params=_params(("parallel", "parallel")),
    )(a, b)


def _rot(x, cos2, sin2):
    return x * cos2 + pltpu.roll(x, 64, 1) * sin2


def _ret_specs(chunk):
    return [
        pl.BlockSpec((C, RDK), lambda h, n: (chunk(n), QB_R + h)),
        pl.BlockSpec((C, RDK), lambda h, n: (chunk(n), KB_R + h)),
        pl.BlockSpec((C, RDV), lambda h, n: (chunk(n), VB_R + h)),
        pl.BlockSpec((C, RDK), lambda h, n: (chunk(n), 0)),
        pl.BlockSpec((C, RDK), lambda h, n: (chunk(n), 0)),
        pl.BlockSpec((1, C, C), lambda h, n: (h, 0, 0)),
        pl.BlockSpec((1, C, RDK), lambda h, n: (h, 0, 0)),
        pl.BlockSpec((1, C, RDK), lambda h, n: (h, 0, 0)),
        pl.BlockSpec((1, RDK, RDV), lambda h, n: (h, 0, 0)),
    ]


def _ret_fwd(z, cst):
    def body(q_ref, k_ref, v_ref, cos_ref, sin_ref, dm_ref, xi_ref, zt_ref, gd_ref, r_ref, sp_ref, st):
        n = pl.program_id(1)

        @pl.when(n == 0)
        def _():
            st[...] = jnp.zeros_like(st)

        cos, sin = cos_ref[...], sin_ref[...]
        qr = _rot(q_ref[...], cos, sin)
        kr = _rot(k_ref[...], cos, sin) * RSCALE
        qb, kb, vb = qr.astype(BF), kr.astype(BF), v_ref[...].astype(BF)
        sd = (_dg(qb, kb, NT) * dm_ref[0]).astype(BF)
        state = st[...]
        sp_ref[0, 0] = state
        qx = (qr * xi_ref[0]).astype(BF)
        r_ref[...] = _dot(sd, vb) + _dot(qx, state.astype(BF))
        kz = (kr * zt_ref[0]).astype(BF)
        st[...] = state * gd_ref[0] + _dg(kz, vb, TN)

    return pl.pallas_call(
        body, name="ret_fwd", grid=(RH, NCH),
        in_specs=_ret_specs(lambda n: n),
        out_specs=[pl.BlockSpec((C, RDV), lambda h, n: (n, h)),
                   pl.BlockSpec((1, 1, RDK, RDV), lambda h, n: (n, h, 0, 0))],
        out_shape=[jax.ShapeDtypeStruct((T, RH * RDV), F32), jax.ShapeDtypeStruct((NCH, RH, RDK, RDV), F32)],
        scratch_shapes=[pltpu.VMEM((RDK, RDV), F32)],
        compiler_params=_params(("parallel", "arbitrary")),
    )(z, z, z, cst["cos2"], cst["sin2"], cst["dmask"], cst["xi"], cst["zeta"], cst["gdec"])


def _ret_bwd(z, cst, sprev, dr):
    def body(q_ref, k_ref, v_ref, cos_ref, sin_ref, dm_ref, xi_ref, zt_ref, gd_ref, sp_ref, dr_ref,
             dq_ref, dk_ref, dv_ref, gst):
        i = pl.program_id(1)

        @pl.when(i == 0)
        def _():
            gst[...] = jnp.zeros_like(gst)

        cos, sin = cos_ref[...], sin_ref[...]
        dm, xi, zt = dm_ref[0], xi_ref[0], zt_ref[0]
        qr = _rot(q_ref[...], cos, sin)
        kr = _rot(k_ref[...], cos, sin) * RSCALE
        qb, kb, vb = qr.astype(BF), kr.astype(BF), v_ref[...].astype(BF)
        sd = (_dg(qb, kb, NT) * dm).astype(BF)
        qx = (qr * xi).astype(BF)
        kz = (kr * zt).astype(BF)
        drb = dr_ref[...]
        sb = sp_ref[0, 0].astype(BF)
        g = gst[...]
        gb = g.astype(BF)
        ds = (_dg(drb, vb, NT) * dm).astype(BF)
        dq = _dot(ds, kb) + _dg(drb, sb, NT) * xi
        dk = _dg(ds, qb, TN) + _dg(vb, gb, NT) * zt
        dv = _dg(sd, drb, TN) + _dot(kz, gb)
        gst[...] = g * gd_ref[0] + _dg(qx, drb, TN)
        dq_ref[...] = (dq * cos + pltpu.roll(dq * sin, 64, 1)).astype(BF)
        dkr = dk * RSCALE
        dk_ref[...] = (dkr * cos + pltpu.roll(dkr * sin, 64, 1)).astype(BF)
        dv_ref[...] = dv.astype(BF)

    rev = lambda n: NCH - 1 - n
    return pl.pallas_call(
        body, name="ret_bwd", grid=(RH, NCH),
        in_specs=_ret_specs(rev) + [
            pl.BlockSpec((1, 1, RDK, RDV), lambda h, n: (rev(n), h, 0, 0)),
            pl.BlockSpec((C, RDV), lambda h, n: (rev(n), h)),
        ],
        out_specs=[pl.BlockSpec((C, RDK), lambda h, n: (rev(n), h)),
                   pl.BlockSpec((C, RDK), lambda h, n: (rev(n), h)),
                   pl.BlockSpec((C, RDV), lambda h, n: (rev(n), h))],
        out_shape=[jax.ShapeDtypeStruct((T, RH * RDK), BF), jax.ShapeDtypeStruct((T, RH * RDK), BF),
                   jax.ShapeDtypeStruct((T, RH * RDV), BF)],
        scratch_shapes=[pltpu.VMEM((RDK, RDV), F32)],
        compiler_params=_params(("parallel", "arbitrary")),
    )(z, z, z, cst["cos2"], cst["sin2"], cst["dmask"], cst["xi"], cst["zeta"], cst["gdec"], sprev, dr)


def _log_sigmoid(x):
    return -(jnp.maximum(-x, 0.0) + jnp.log1p(jnp.exp(-jnp.abs(x))))


def _fox_prep(zf, bf_pad, cst):
    def body(zf_ref, b_ref, tri_ref, spread_ref, cb_ref, ct_ref, carry):
        n = pl.program_id(0)

        @pl.when(n == 0)
        def _():
            carry[...] = jnp.zeros_like(carry)

        ls = _log_sigmoid(zf_ref[...] + b_ref[...])
        row = n * C + lax.broadcasted_iota(jnp.int32, (C, C), 0)
        lf = jnp.where(row >= PAD, ls, 0.0)
        cc = jnp.dot(tri_ref[...], lf, precision=HI, preferred_element_type=F32) + carry[0:1, :]
        carry[...] = jnp.broadcast_to(cc[C - 1:C, :], carry.shape)
        cb_ref[...] = jnp.dot(cc, spread_ref[...], precision=HI, preferred_element_type=F32)
        pos = n * C + lax.broadcasted_iota(jnp.int32, (FH, C), 1)
        ct_ref[0] = jnp.where(pos >= PAD, cc.T[:FH, :], -NEG)

    return pl.pallas_call(
        body, name="fox_prep", grid=(NCH,),
        in_specs=[pl.BlockSpec((C, C), lambda n: (n, 0)), pl.BlockSpec((1, C), lambda n: (0, 0)),
                  pl.BlockSpec((C, C), lambda n: (0, 0)), pl.BlockSpec((C, FH * FD), lambda n: (0, 0))],
        out_specs=[pl.BlockSpec((C, FH * FD), lambda n: (n, 0)), pl.BlockSpec((1, FH, C), lambda n: (n, 0, 0))],
        out_shape=[jax.ShapeDtypeStruct((T, FH * FD), F32), jax.ShapeDtypeStruct((NCH, FH, C), F32)],
        scratch_shapes=[pltpu.VMEM((8, C), F32)],
        compiler_params=_params(("arbitrary",)),
    )(zf, bf_pad, cst["tri"], cst["spread"])


def _lo_lanes(shape):
    return lax.broadcasted_iota(jnp.int32, shape, 1) < FD


def _split_heads(x):
    lo = _lo_lanes(x.shape)
    zero = jnp.zeros_like(x)
    return jnp.concatenate([jnp.where(lo, x, zero), jnp.where(lo, zero, x)], axis=0)


def _spread2(x):
    lo = _lo_lanes(x.shape)
    r = pltpu.roll(x, FD, 1)
    return jnp.concatenate([jnp.where(lo, x, r), jnp.where(lo, r, x)], axis=1)


FOX_GROUP = 4


def _fox_tile_ids(i, t):
    out = []
    for u in range(FOX_GROUP):
        j = FOX_GROUP * t + u
        kind = jnp.where(j < i, 0, jnp.where(j == i, 1, 2))
        out.append((jnp.minimum(j, i), kind))
    return out


def _fox_scores(qb, kk, row_bias, ct_ref, mb_ref, p, j, kind):
    cj = jnp.concatenate([ct_ref[j, pl.ds(2 * p, 1), :], ct_ref[j, pl.ds(2 * p + 1, 1), :]], axis=1)
    return _dg(qb, kk, NT) * FSCALE + ((row_bias - cj) + mb_ref[kind])


def _fox_groups(i, group):
    def step(t, carry):
        group(t)
        return carry

    lax.fori_loop(0, (i + FOX_GROUP) // FOX_GROUP, step, 0)


def _fox_fwd(z, cb, ct, cst):
    def body(q_ref, k_ref, v_ref, cb_ref, ct_ref, ones_ref, mb_ref, a_ref, g_ref, mx, acc):
        p, i = pl.program_id(0), pl.program_id(1)
        qb = q_ref[...].astype(BF)
        cbt = cb_ref[...]
        ci = _spread2(cbt)
        ones = ones_ref[...]

        def scores(j, kind):
            ks = pl.multiple_of(j * C, C)
            kk = _split_heads(k_ref[pl.ds(ks, C), :]).astype(BF)
            return _fox_scores(qb, kk, ci, ct_ref, mb_ref, p, j, kind)

        def pass_max(t):
            ss = [scores(j, kind) for j, kind in _fox_tile_ids(i, t)]
            while len(ss) > 1:
                ss = [jnp.maximum(a_, b_) for a_, b_ in zip(ss[::2], ss[1::2])]
            mx[...] = jnp.maximum(mx[...], ss[0])

        mx[...] = jnp.full(mx.shape, NEG, F32)
        _fox_groups(i, pass_max)
        m = jnp.concatenate(
            [jnp.broadcast_to(jnp.max(mx[:, :C], axis=1, keepdims=True), (C, C)),
             jnp.broadcast_to(jnp.max(mx[:, C:], axis=1, keepdims=True), (C, C))], axis=1)

        def pass_sum(t):
            ids = _fox_tile_ids(i, t)
            ss = [scores(j, kind) for j, kind in ids]
            pes = [jnp.exp(s - m).astype(BF) for s in ss]
            vvs = [jnp.concatenate([_split_heads(v_ref[pl.ds(pl.multiple_of(j * C, C), C), :]).astype(BF), ones],
                                   axis=1) for j, _ in ids]
            parts = [_dot(pe, vv) for pe, vv in zip(pes, vvs)]
            while len(parts) > 1:
                parts = [a_ + b_ for a_, b_ in zip(parts[::2], parts[1::2])]
            acc[...] += parts[0]

        acc[...] = jnp.zeros_like(acc)
        _fox_groups(i, pass_sum)
        res = acc[...]
        l = res[:, C:]
        a_ref[...] = res[:, :C] / l
        g_ref[...] = cbt - (jnp.where(_lo_lanes((C, C)), m[:, :C], m[:, C:]) + jnp.log(l))

    return pl.pallas_call(
        body, name="fox_fwd", grid=(NPAIR, NCH),
        in_specs=[pl.BlockSpec((C, C), lambda p, i: (i, QB_F + p)),
                  pl.BlockSpec((T, C), lambda p, i: (0, KB_F + p)),
                  pl.BlockSpec((T, C), lambda p, i: (0, VB_F + p)),
                  pl.BlockSpec((C, C), lambda p, i: (i, p)),
                  pl.BlockSpec((NCH, FH, C), lambda p, i: (0, 0, 0)),
                  pl.BlockSpec((2 * C, C), lambda p, i: (0, 0)),
                  pl.BlockSpec((3, C, 2 * C), lambda p, i: (0, 0, 0))],
        out_specs=[pl.BlockSpec((C, C), lambda p, i: (i, p)), pl.BlockSpec((C, C), lambda p, i: (i, p))],
        out_shape=[jax.ShapeDtypeStruct((T, FH * FD), F32), jax.ShapeDtypeStruct((T, FH * FD), F32)],
        scratch_shapes=[pltpu.VMEM((C, 2 * C), F32), pltpu.VMEM((C, 2 * C), F32)],
        compiler_params=_params(("parallel", "arbitrary")),
    )(z, z, z, cb, ct, cst["ones_aug"], cst["mask_bias"])


def _fox_bwd(z, da, g, delta, ct, cst):
    def body(q_ref, da_ref, g_ref, dl_ref, k_ref, v_ref, ct_ref, ones_ref, mb_ref,
             dq_ref, dr_ref, dk_ref, dv_ref, dcs_ref, dkacc, dvacc, csacc, dqacc):
        p, i = pl.program_id(0), pl.program_id(1)

        @pl.when(i == 0)
        def _():
            dkacc[...] = jnp.zeros_like(dkacc)
            dvacc[...] = jnp.zeros_like(dvacc)
            csacc[...] = jnp.zeros_like(csacc)

        ones = ones_ref[...]
        qf = q_ref[...]
        qb = qf.astype(BF)
        dab = da_ref[...]
        qq = jnp.concatenate([_split_heads(qf).astype(BF), ones], axis=1)
        dd = _split_heads(dab.astype(F32)).astype(BF)
        gi = _spread2(g_ref[...])
        dl = _spread2(dl_ref[...])
        dqacc[...] = jnp.zeros_like(dqacc)

        def group(t):
            ids = _fox_tile_ids(i, t)
            rows = [pl.ds(pl.multiple_of(j * C, C), C) for j, _ in ids]
            kks = [_split_heads(k_ref[r, :]).astype(BF) for r in rows]
            vvs = [_split_heads(v_ref[r, :]).astype(BF) for r in rows]
            ss = [_fox_scores(qb, kk, gi, ct_ref, mb_ref, p, j, kind) for kk, (j, kind) in zip(kks, ids)]
            dps = [_dg(dab, vv, NT) for vv in vvs]
            pes = [jnp.exp(s) for s in ss]
            dss = [pe * (dp - dl) * FSCALE for pe, dp in zip(pes, dps)]
            pts = [jnp.concatenate([pe[:, :C].T, pe[:, C:].T], axis=1).astype(BF) for pe in pes]
            dsts = [jnp.concatenate([ds[:, :C].T, ds[:, C:].T], axis=1).astype(BF) for ds in dss]
            dvs = [_dot(pt, dd) for pt in pts]
            rs = [_dot(dst, qq) for dst in dsts]
            parts = [_dot(ds.astype(BF), jnp.concatenate([kk, ones], axis=1)) for ds, kk in zip(dss, kks)]
            for r, dv, rr in zip(rows, dvs, rs):
                dvacc[r, :] += dv
                dkacc[r, :] += rr[:, :C]
                csacc[r, :] += rr[:, C:]
            while len(parts) > 1:
                parts = [a_ + b_ for a_, b_ in zip(parts[::2], parts[1::2])]
            dqacc[...] += parts[0]

        _fox_groups(i, group)
        res = dqacc[...]
        dq_ref[...] = res[:, :C].astype(BF)
        dr_ref[...] = res[:, C:]

        @pl.when(i == NCH - 1)
        def _():
            dk_ref[...] = dkacc[...].astype(BF)
            dv_ref[...] = dvacc[...].astype(BF)
            dcs_ref[...] = csacc[...]

    blk = pl.BlockSpec((C, C), lambda p, i: (i, p))
    col = pl.BlockSpec((T, C), lambda p, i: (0, p))
    return pl.pallas_call(
        body, name="fox_bwd", grid=(NPAIR, NCH),
        in_specs=[pl.BlockSpec((C, C), lambda p, i: (i, QB_F + p)), blk, blk, blk,
                  pl.BlockSpec((T, C), lambda p, i: (0, KB_F + p)),
                  pl.BlockSpec((T, C), lambda p, i: (0, VB_F + p)),
                  pl.BlockSpec((NCH, FH, C), lambda p, i: (0, 0, 0)),
                  pl.BlockSpec((2 * C, C), lambda p, i: (0, 0)),
                  pl.BlockSpec((3, C, 2 * C), lambda p, i: (0, 0, 0))],
        out_specs=[blk, blk, col, col, col],
        out_shape=[jax.ShapeDtypeStruct((T, FH * FD), BF), jax.ShapeDtypeStruct((T, FH * FD), F32),
                   jax.ShapeDtypeStruct((T, FH * FD), BF), jax.ShapeDtypeStruct((T, FH * FD), BF),
                   jax.ShapeDtypeStruct((T, FH * FD), F32)],
        scratch_shapes=[pltpu.VMEM((T, C), F32), pltpu.VMEM((T, C), F32), pltpu.VMEM((T, C), F32),
                        pltpu.VMEM((C, 2 * C), F32)],
        compiler_params=_params(("parallel", "arbitrary")),
    )(z, da, g, delta, z, z, ct, cst["ones_aug"], cst["mask_bias"])


def _fox_gate_bwd(drow, dcol, zf, bf_pad, cst):
    def body(dr_ref, dc_ref, zf_ref, b_ref, tri_ref, pick_ref, dff_ref, db_ref, carry):
        s = pl.program_id(0)
        n = NCH - 1 - s

        @pl.when(s == 0)
        def _():
            carry[...] = jnp.zeros_like(carry)
            db_ref[...] = jnp.zeros_like(db_ref)

        dcb = jnp.dot((dr_ref[...] - dc_ref[...]) * (1.0 / FSCALE), pick_ref[...], precision=HI,
                      preferred_element_type=F32)
        suf = lax.dot_general(tri_ref[...], dcb, TN, precision=HI, preferred_element_type=F32) + carry[0:1, :]
        carry[...] = jnp.broadcast_to(suf[0:1, :], carry.shape)
        x = zf_ref[...] + b_ref[...]
        row = n * C + lax.broadcasted_iota(jnp.int32, (C, C), 0)
        dff = jnp.where(row >= PAD, suf * (1.0 - jax.nn.sigmoid(x)), 0.0)
        dff_ref[...] = dff.astype(BF)
        db_ref[...] += jnp.sum(dff, axis=0, keepdims=True)

    rev = lambda s: (NCH - 1 - s, 0)
    return pl.pallas_call(
        body, name="fox_gate_bwd", grid=(NCH,),
        in_specs=[pl.BlockSpec((C, FH * FD), rev), pl.BlockSpec((C, FH * FD), rev), pl.BlockSpec((C, C), rev),
                  pl.BlockSpec((1, C), lambda s: (0, 0)), pl.BlockSpec((C, C), lambda s: (0, 0)),
                  pl.BlockSpec((FH * FD, C), lambda s: (0, 0))],
        out_specs=[pl.BlockSpec((C, C), rev), pl.BlockSpec((1, C), lambda s: (0, 0))],
        out_shape=[jax.ShapeDtypeStruct((T, C), BF), jax.ShapeDtypeStruct((1, C), F32)],
        scratch_shapes=[pltpu.VMEM((8, C), F32)],
        compiler_params=_params(("arbitrary",)),
    )(drow, dcol, zf, bf_pad, cst["tri"], cst["pick"])


def _gated(r, rg, a, fg):
    rn, rs = [], []
    for h in range(RH):
        rh = r[:, RDV * h:RDV * (h + 1)]
        s = lax.rsqrt(jnp.mean(rh * rh, axis=1, keepdims=True) + EPS)
        rn.append(rh * s)
        rs.append(s)
    rn = jnp.concatenate(rn, axis=1)
    y = jnp.concatenate([rn * (rg * jax.nn.sigmoid(rg)), a * (fg * jax.nn.sigmoid(fg))], axis=1)
    return y, rn, rs


def _out_loss(r, z, a, wout, x, tgt, fgain):
    def body(r_ref, rg_ref, a_ref, fg_ref, w_ref, x_ref, t_ref, g_ref, yt_ref, do_ref, dob_ref, loss_ref, dg_ref):
        i = pl.program_id(0)

        @pl.when(i == 0)
        def _():
            yt_ref[...] = jnp.zeros_like(yt_ref)
            do_ref[...] = jnp.zeros_like(do_ref)
            dob_ref[...] = jnp.zeros_like(dob_ref)
            loss_ref[...] = jnp.zeros_like(loss_ref)
            dg_ref[...] = jnp.zeros_like(dg_ref)

        @pl.when(i > 0)
        def _():
            y, _, _ = _gated(r_ref[...], rg_ref[...], a_ref[...], fg_ref[...])
            yt_ref[...] = y.T.astype(BF)
            o = x_ref[...] + _dot(y.astype(BF), w_ref[...])
            rs = lax.rsqrt(jnp.mean(o * o, axis=1, keepdims=True) + EPS)
            on = o * rs
            g = g_ref[...]
            e = on * g - t_ref[...]
            loss_ref[...] += 0.5 * jnp.sum(jnp.mean(e * e, axis=1, keepdims=True))
            dyh = e * (1.0 / D)
            dg_ref[...] += jnp.sum(dyh * on, axis=0, keepdims=True)
            don = dyh * g
            do = rs * (don - on * jnp.mean(don * on, axis=1, keepdims=True))
            do_ref[...] = do
            dob_ref[...] = do.astype(BF)

    tok = lambda i: (jnp.maximum(i - 1, 0), 0)
    return pl.pallas_call(
        body, name="out_loss", grid=(NCH,),
        in_specs=[pl.BlockSpec((C, D), lambda i: (i, 0)), pl.BlockSpec((C, D), lambda i: (i, GB_R)),
                  pl.BlockSpec((C, D), lambda i: (i, 0)), pl.BlockSpec((C, D), lambda i: (i, GB_F)),
                  pl.BlockSpec((DMIX, D), lambda i: (0, 0)),
                  pl.BlockSpec((C, D), tok), pl.BlockSpec((C, D), tok), pl.BlockSpec((1, D), lambda i: (0, 0))],
        out_specs=[pl.BlockSpec((DMIX, C), lambda i: (0, i)), pl.BlockSpec((C, D), lambda i: (i, 0)),
                   pl.BlockSpec((C, D), lambda i: (i, 0)), pl.BlockSpec((8, C), lambda i: (0, 0)),
                   pl.BlockSpec((1, D), lambda i: (0, 0))],
        out_shape=[jax.ShapeDtypeStruct((DMIX, T), BF), jax.ShapeDtypeStruct((T, D), F32),
                   jax.ShapeDtypeStruct((T, D), BF), jax.ShapeDtypeStruct((8, C), F32),
                   jax.ShapeDtypeStruct((1, D), F32)],
        compiler_params=_params(("arbitrary",)),
    )(r, z, a, z, wout, x, tgt, fgain)


def _dsilu(x):
    s = jax.nn.sigmoid(x)
    return s * (1.0 + x * (1.0 - s))


def _dy_gate_bwd(dob, wout, r, z, a, seg):
    def body(do_ref, w_ref, r_ref, rg_ref, a_ref, fg_ref, seg_ref, dr_ref, da_ref, drg_ref, dfg_ref, dl_ref):
        dy = _dg(do_ref[...], w_ref[...], NT)
        rg, fg, a_ = rg_ref[...], fg_ref[...], a_ref[...]
        _, rn, rs = _gated(r_ref[...], rg, a_, fg)
        dyr, dyf = dy[:, :D], dy[:, D:]
        drn = dyr * (rg * jax.nn.sigmoid(rg))
        drg_ref[...] = (dyr * rn * _dsilu(rg)).astype(BF)
        for h in range(RH):
            sl = slice(RDV * h, RDV * (h + 1))
            dh, nh = drn[:, sl], rn[:, sl]
            dr_ref[:, sl] = (rs[h] * (dh - nh * jnp.mean(dh * nh, axis=1, keepdims=True))).astype(BF)
        dab = (dyf * (fg * jax.nn.sigmoid(fg))).astype(BF)
        da_ref[...] = dab
        dfg_ref[...] = (dyf * a_ * _dsilu(fg)).astype(BF)
        prod = dab.astype(F32) * a_
        segm = seg_ref[...]
        for p in range(NPAIR):
            sl = slice(C * p, C * (p + 1))
            hi = prod[:, sl].astype(BF)
            lo = (prod[:, sl] - hi.astype(F32)).astype(BF)
            dl_ref[:, sl] = _dot(hi, segm) + _dot(lo, segm)

    row = lambda i: (i, 0)
    return pl.pallas_call(
        body, name="dy_gate_bwd", grid=(NCH,),
        in_specs=[pl.BlockSpec((C, D), row), pl.BlockSpec((DMIX, D), lambda i: (0, 0)),
                  pl.BlockSpec((C, D), row), pl.BlockSpec((C, D), lambda i: (i, GB_R)),
                  pl.BlockSpec((C, D), row), pl.BlockSpec((C, D), lambda i: (i, GB_F)),
                  pl.BlockSpec((C, C), lambda i: (0, 0))],
        out_specs=[pl.BlockSpec((C, D), row)] * 5,
        out_shape=[jax.ShapeDtypeStruct((T, D), BF)] * 4 + [jax.ShapeDtypeStruct((T, D), F32)],
        compiler_params=_params(("parallel",)),
    )(dob, wout, r, z, a, z, seg)


def _du_norm_bwd(dzm, dzf, wm, wf, hpad, g, dopad):
    tm, tk = 544, 1024
    nk = WMAIN // tk

    def body(dzm_ref, dzf_ref, wm_ref, wf_ref, h_ref, g_ref, do_ref, gh_ref, dg_ref, acc):
        i, k = pl.program_id(0), pl.program_id(1)

        @pl.when(k == 0)
        def _():
            acc[...] = _dg(dzf_ref[...], wf_ref[...], NT)

        acc[...] += _dg(dzm_ref[...], wm_ref[...], NT)

        @pl.when(k == nk - 1)
        def _():
            du = acc[...]
            h = h_ref[...]
            gg = g_ref[...]
            rs = lax.rsqrt(jnp.mean(h * h, axis=1, keepdims=True) + EPS)
            hn = h * rs
            part = jnp.sum(du * hn, axis=0, keepdims=True)

            @pl.when(i == 0)
            def _():
                dg_ref[...] = part

            @pl.when(i > 0)
            def _():
                dg_ref[...] += part

            dhn = du * gg
            gh_ref[...] = rs * (dhn - hn * jnp.mean(dhn * hn, axis=1, keepdims=True)) + do_ref[...]

    return pl.pallas_call(
        body, name="du_norm_bwd", grid=(T // tm, nk),
        in_specs=[pl.BlockSpec((tm, tk), lambda i, k: (i, k)), pl.BlockSpec((tm, C), lambda i, k: (i, 0)),
                  pl.BlockSpec((D, tk), lambda i, k: (0, k)), pl.BlockSpec((D, C), lambda i, k: (0, 0)),
                  pl.BlockSpec((tm, D), lambda i, k: (i, 0)), pl.BlockSpec((1, D), lambda i, k: (0, 0)),
                  pl.BlockSpec((tm, D), lambda i, k: (i, 0))],
        out_specs=[pl.BlockSpec((tm, D), lambda i, k: (i, 0)), pl.BlockSpec((1, D), lambda i, k: (0, 0))],
        out_shape=[jax.ShapeDtypeStruct((T, D), F32), jax.ShapeDtypeStruct((1, D), F32)],
        scratch_shapes=[pltpu.VMEM((tm, D), F32)],
        compiler_params=_params(("arbitrary", "arbitrary")),
    )(dzm, dzf, wm, wf, hpad, g, dopad)


def _local_step(x, tgt, meta, norm_g, wm, wf, b_f, wout, final_g):
    cst = _constants()
    hpad = jnp.concatenate([jnp.pad(meta, ((PAD, 0), (0, 0))), x], axis=0)
    bf_pad = jnp.pad(b_f, ((0, 0), (0, C - NFF)))
    u, ut = _norm_in(hpad, norm_g)
    z = _mm_nn(u, wm, T // 2, 512, "in_proj")
    zf = _mm_nn(u, wf, T // 2, C, "in_proj_ff")
    r, sprev = _ret_fwd(z, cst)
    cb, ct = _fox_prep(zf, bf_pad, cst)
    a, g = _fox_fwd(z, cb, ct, cst)
    yt, dopad, dob, loss8, dfg = _out_loss(r, z, a, wout, x, tgt, final_g)
    dr, da, dzrg, dzfg, delta = _dy_gate_bwd(dob, wout, r, z, a, cst["seg"])
    dwout = _mm_nn(yt, dob, 512, D, "dw_out")
    dzq_r, dzk_r, dzv_r = _ret_bwd(z, cst, sprev, dr)
    dzq_f, drow, dzk_f, dzv_f, dcol = _fox_bwd(z, da, g, delta, ct, cst)
    dzf, dbf = _fox_gate_bwd(drow, dcol, zf, bf_pad, cst)
    dzm = jnp.concatenate([dzq_r, dzk_r, dzv_r, dzrg, dzq_f, dzk_f, dzv_f, dzfg], axis=1)
    dwm = _mm_nn(ut, dzm, 512, 1024, "dw_in")
    dwf = _mm_nn(ut, dzf, D, C, "dw_in_ff")
    gh, dng = _du_norm_bwd(dzm, dzf, wm, wf, hpad, norm_g, dopad)
    dwin = jnp.concatenate([dwm, dwf[:, :NFF]], axis=1)
    return (loss8[0, 0], gh[C:], gh[PAD:C], dng, dwin, dbf[:, :NFF], dwout, dfg)


def _place():
    x, y, c = lax.axis_index("x"), lax.axis_index("y"), lax.axis_index("c")
    return x, y, c


def _other_chips(x, y):
    return [(1 - x, y, 2 * (1 - x) + y), (x, 1 - y, 2 * x + (1 - y)), (1 - x, 1 - y, 2 * (1 - x) + (1 - y))]


def _all_gather_shards(shards):
    n = len(shards)

    def body(*refs):
        ins, outs = refs[:n], refs[n:2 * n]
        send_sems, recv_sems = refs[2 * n:]
        x, y, c = _place()
        me_s = 2 * x + y
        sib = (x, y, 1 - c)
        chips = _other_chips(x, y)
        sends, waits = [], []
        for a in range(n):
            rows = ins[a].shape[0] // 2
            half = pl.ds(c * rows, rows)
            for k, (cx, cy, cs) in enumerate(chips):
                sends.append(pltpu.make_async_remote_copy(
                    src_ref=ins[a].at[half], dst_ref=outs[a].at[me_s, half],
                    send_sem=send_sems.at[6 * a + k], recv_sem=recv_sems.at[6 * a + k],
                    device_id=(cx, cy, c), device_id_type=MESH))
                sends[-1].start()
        for a in range(n):
            rows = ins[a].shape[0] // 2
            half = pl.ds(c * rows, rows)
            other = pl.ds((1 - c) * rows, rows)
            for k, (cx, cy, cs) in enumerate(chips):
                pltpu.make_async_remote_copy(
                    src_ref=outs[a].at[cs, half], dst_ref=outs[a].at[cs, half],
                    send_sem=send_sems.at[6 * a + k], recv_sem=recv_sems.at[6 * a + k],
                    device_id=(cx, cy, c), device_id_type=MESH).wait_recv()
                fwd = pltpu.make_async_remote_copy(
                    src_ref=outs[a].at[cs, half], dst_ref=outs[a].at[cs, half],
                    send_sem=send_sems.at[6 * a + 3 + k], recv_sem=recv_sems.at[6 * a + 3 + k],
                    device_id=sib, device_id_type=MESH)
                fwd.start()
                sends.append(fwd)
                waits.append(pltpu.make_async_remote_copy(
                    src_ref=outs[a].at[cs, other], dst_ref=outs[a].at[cs, other],
                    send_sem=send_sems.at[6 * a + 3 + k], recv_sem=recv_sems.at[6 * a + 3 + k],
                    device_id=sib, device_id_type=MESH))
        for w in waits:
            w.wait_recv()
        for s in sends:
            s.wait_send()

    return pl.pallas_call(
        body, name="all_gather_w",
        in_specs=[ANY] * n, out_specs=[ANY] * n,
        out_shape=[jax.ShapeDtypeStruct((4,) + s.shape, s.dtype) for s in shards],
        scratch_shapes=[pltpu.SemaphoreType.DMA((6 * n,)), pltpu.SemaphoreType.DMA((6 * n,))],
    )(*shards)


def _pair_swap(arrs, small):
    n = len(arrs)

    def body(*refs):
        ins, sm = refs[:n], refs[n]
        outs, smo = refs[n + 1:2 * n + 1], refs[2 * n + 1]
        send_sems, recv_sems = refs[2 * n + 2:]
        x, y, c = _place()
        sib = (x, y, 1 - c)
        cps = []
        for a in range(n):
            rows = ins[a].shape[1] // 2
            cps.append(pltpu.make_async_remote_copy(
                src_ref=ins[a].at[:, pl.ds((1 - c) * rows, rows)], dst_ref=outs[a],
                send_sem=send_sems.at[a], recv_sem=recv_sems.at[a], device_id=sib, device_id_type=MESH))
        cps.append(pltpu.make_async_remote_copy(
            src_ref=sm, dst_ref=smo, send_sem=send_sems.at[n], recv_sem=recv_sems.at[n],
            device_id=sib, device_id_type=MESH))
        for cp in cps:
            cp.start()
        for cp in cps:
            cp.wait()

    return pl.pallas_call(
        body, name="rs_pair_swap",
        in_specs=[ANY] * (n + 1), out_specs=[ANY] * (n + 1),
        out_shape=[jax.ShapeDtypeStruct((4, a.shape[1] // 2, a.shape[2]), a.dtype) for a in arrs]
        + [jax.ShapeDtypeStruct(small.shape, small.dtype)],
        scratch_shapes=[pltpu.SemaphoreType.DMA((n + 1,)), pltpu.SemaphoreType.DMA((n + 1,))],
    )(*arrs, small)


def _chip_exchange(parts, small):
    n = len(parts)

    def body(*refs):
        ins, sm = refs[:n], refs[n]
        outs, smo = refs[n + 1:2 * n + 1], refs[2 * n + 1]
        send_sems, recv_sems = refs[2 * n + 2:]
        x, y, c = _place()
        me_s = 2 * x + y
        chips = _other_chips(x, y)
        cps = []
        for a in range(n + 1):
            src = ins[a] if a < n else sm
            dst = outs[a] if a < n else smo
            for k, (cx, cy, cs) in enumerate(chips):
                cps.append(pltpu.make_async_remote_copy(
                    src_ref=src.at[cs] if a < n else src, dst_ref=dst.at[me_s],
                    send_sem=send_sems.at[3 * a + k], recv_sem=recv_sems.at[3 * a + k],
                    device_id=(cx, cy, c), device_id_type=MESH))
        for cp in cps:
            cp.start()
        for cp in cps:
            cp.wait()

    return pl.pallas_call(
        body, name="rs_chip_exchange",
        in_specs=[ANY] * (n + 1), out_specs=[ANY] * (n + 1),
        out_shape=[jax.ShapeDtypeStruct(p.shape, p.dtype) for p in parts]
        + [jax.ShapeDtypeStruct((4,) + small.shape, small.dtype)],
        scratch_shapes=[pltpu.SemaphoreType.DMA((3 * (n + 1),)), pltpu.SemaphoreType.DMA((3 * (n + 1),))],
    )(*parts, small)


def _pair_send(halves):
    n = len(halves)

    def body(*refs):
        ins, outs = refs[:n], refs[n:2 * n]
        send_sems, recv_sems = refs[2 * n:]
        x, y, c = _place()
        cps = [pltpu.make_async_remote_copy(
            src_ref=ins[a], dst_ref=outs[a], send_sem=send_sems.at[a], recv_sem=recv_sems.at[a],
            device_id=(x, y, 1 - c), device_id_type=MESH) for a in range(n)]
        for cp in cps:
            cp.start()
        for cp in cps:
            cp.wait()

    return pl.pallas_call(
        body, name="rs_pair_send",
        in_specs=[ANY] * n, out_specs=[ANY] * n,
        out_shape=[jax.ShapeDtypeStruct(h.shape, h.dtype) for h in halves],
        scratch_shapes=[pltpu.SemaphoreType.DMA((n,)), pltpu.SemaphoreType.DMA((n,))],
    )(*halves)


def _row_block(rows):
    for tb in (256, 128, 64, 32, 16, 8):
        if rows % tb == 0:
            return tb
    return rows


def _add_halves(full, recv, name, out_dtype):
    _, r2, w = recv.shape
    tb = _row_block(r2)
    nb = r2 // tb
    c = lax.axis_index("c")

    def body(c_ref, a_ref, b_ref, o_ref):
        o_ref[...] = (a_ref[...] + b_ref[...]).astype(o_ref.dtype)

    return pl.pallas_call(
        body, name=name,
        grid_spec=pltpu.PrefetchScalarGridSpec(
            num_scalar_prefetch=1, grid=(4, nb),
            in_specs=[pl.BlockSpec((1, tb, w), lambda s, i, cr: (s, cr[0] * nb + i, 0)),
                      pl.BlockSpec((1, tb, w), lambda s, i, cr: (s, i, 0))],
            out_specs=pl.BlockSpec((1, tb, w), lambda s, i, cr: (s, i, 0))),
        out_shape=jax.ShapeDtypeStruct(recv.shape, out_dtype),
        compiler_params=_params(("parallel", "parallel")),
    )(jnp.reshape(c, (1,)).astype(jnp.int32), full, recv)


def _add2(a, b, name):
    def body(a_ref, b_ref, o_ref):
        o_ref[...] = a_ref[...] + b_ref[...]

    return pl.pallas_call(body, name=name, out_shape=jax.ShapeDtypeStruct(a.shape, a.dtype))(a, b)


def _sum4(buf, own, name):
    _, r, w = buf.shape
    tb = _row_block(r)
    me_s = 2 * lax.axis_index("x") + lax.axis_index("y")
    by_dest = own.ndim == 3

    def body(s_ref, b_ref, own_ref, o_ref):
        mine = (own_ref[0] if by_dest else own_ref[...]).astype(F32)
        terms = [jnp.where(s_ref[0] == t, mine, b_ref[t].astype(F32)) for t in range(4)]
        o_ref[...] = ((terms[0] + terms[1]) + terms[2]) + terms[3]

    own_spec = (pl.BlockSpec((1, tb, w), lambda i, sr: (sr[0], i, 0)) if by_dest
                else pl.BlockSpec((tb, w), lambda i, sr: (i, 0)))
    return pl.pallas_call(
        body, name=name,
        grid_spec=pltpu.PrefetchScalarGridSpec(
            num_scalar_prefetch=1, grid=(r // tb,),
            in_specs=[pl.BlockSpec((4, tb, w), lambda i, sr: (0, i, 0)), own_spec],
            out_specs=pl.BlockSpec((tb, w), lambda i, sr: (i, 0))),
        out_shape=jax.ShapeDtypeStruct((r, w), F32),
        compiler_params=_params(("parallel",)),
    )(jnp.reshape(me_s, (1,)).astype(jnp.int32), buf, own)


def _adamw_math(w, g, m, v):
    mn = B1 * m + (1.0 - B1) * g
    vn = B2 * v + (1.0 - B2) * (g * g)
    m_hat = mn / (1.0 - B1 ** STEP)
    v_hat = vn / (1.0 - B2 ** STEP)
    return -LR * (m_hat / (jnp.sqrt(v_hat) + AEPS) + WD * w), mn, vn


def _adamw(w, g, m, v, name):
    r, c_ = w.shape
    tb = _row_block(r)

    def body(w_ref, g_ref, m_ref, v_ref, d_ref, mo_ref, vo_ref):
        d_ref[...], mo_ref[...], vo_ref[...] = _adamw_math(w_ref[...], g_ref[...], m_ref[...], v_ref[...])

    spec = pl.BlockSpec((tb, c_), lambda i: (i, 0))
    return pl.pallas_call(
        body, name=name, grid=(r // tb,),
        in_specs=[spec] * 4, out_specs=[spec] * 3,
        out_shape=[jax.ShapeDtypeStruct(w.shape, F32)] * 3,
        compiler_params=_params(("parallel",)),
    )(w, g, m, v)


def _adamw_halves(w, g_mine, g_sib, m, v, name):
    r, c_ = w.shape
    r2 = g_mine.shape[0]
    tb = _row_block(r2)
    nb = r2 // tb
    c = lax.axis_index("c")

    def body(c_ref, w_ref, gm_ref, gs_ref, m_ref, v_ref, g_ref, d_ref, mo_ref, vo_ref):
        g = jnp.where(pl.program_id(0) == c_ref[0], gm_ref[...], gs_ref[...])
        g_ref[...] = g
        d_ref[...], mo_ref[...], vo_ref[...] = _adamw_math(w_ref[...], g, m_ref[...], v_ref[...])

    full = pl.BlockSpec((tb, c_), lambda h, i, cr: (h * nb + i, 0))
    half = pl.BlockSpec((tb, c_), lambda h, i, cr: (i, 0))
    return pl.pallas_call(
        body, name=name,
        grid_spec=pltpu.PrefetchScalarGridSpec(
            num_scalar_prefetch=1, grid=(2, nb),
            in_specs=[full, half, half, full, full], out_specs=[full] * 4),
        out_shape=[jax.ShapeDtypeStruct(w.shape, F32)] * 4,
        compiler_params=_params(("parallel", "parallel")),
    )(jnp.reshape(c, (1,)).astype(jnp.int32), w, g_mine, g_sib, m, v)


def kernel(x, meta_tokens, norm_g, w_in, b_f, w_out, final_g, loss_target, m_meta_tokens, m_norm_g, m_w_in, m_b_f, m_w_out, m_final_g, v_meta_tokens, v_norm_g, v_w_in, v_b_f, v_w_out, v_final_g):
    me_s = 2 * lax.axis_index("x") + lax.axis_index("y")
    own = [w_in[0].astype(BF), w_out[0].astype(BF), meta_tokens]
    gathered = _all_gather_shards(own)
    gin, gout, gmeta = [[jnp.where(me_s == s, o, g[s]) for s in range(4)] for o, g in zip(own, gathered)]
    wfull = jnp.concatenate(gin, axis=1)
    wm = wfull[:, :WMAIN]
    wf = jnp.pad(wfull[:, WMAIN:], ((0, 0), (0, C - NFF)))
    wout = jnp.concatenate(gout, axis=0)
    meta = jnp.concatenate(gmeta, axis=1)

    loss, gx, dmeta, dng, dwin, dbf, dwout, dfg = _local_step(
        x[0], loss_target[0], meta, norm_g, wm, wf, b_f, wout, final_g.reshape(1, D))

    g_in = jnp.stack([dwin[:, WSH * s:WSH * (s + 1)] for s in range(4)])
    g_out = dwout.reshape(4, DMIX // 4, D)
    g_meta = jnp.stack([dmeta[:, 256 * s:256 * (s + 1)] for s in range(4)])
    small = jnp.concatenate([dng, dfg, jnp.pad(dbf, ((0, 0), (0, D - NFF))),
                             jnp.pad(jnp.reshape(loss, (1, 1)), ((0, 0), (0, D - 1))),
                             jnp.zeros((4, D), F32)], axis=0)
    r_in, r_out, r_meta, r_small = _pair_swap([g_in, g_out, g_meta], small)
    p_in = _add_halves(g_in, r_in, "pair_add_in", BF)
    p_out = _add_halves(g_out, r_out, "pair_add_out", BF)
    p_meta = _add_halves(g_meta, r_meta, "pair_add_meta", F32)
    p_small = _add2(small, r_small, "pair_add_small")
    e_in, e_out, e_meta, e_small = _chip_exchange([p_in, p_out, p_meta], p_small)
    h_in, h_out, h_meta = _sum4(e_in, p_in, "sum_in"), _sum4(e_out, p_out, "sum_out"), _sum4(e_meta, p_meta, "sum_meta")
    tot = _sum4(e_small, p_small, "sum_small")
    s_in, s_out, s_meta = _pair_send([h_in, h_out, h_meta])
    g_norm, g_final, g_bf, loss_all = tot[0:1], tot[1], tot[2:3, :NFF], tot[3, 0]

    gw_meta, d_meta, nm_meta, nv_meta = _adamw_halves(meta_tokens, h_meta, s_meta, m_meta_tokens, v_meta_tokens,
                                                      "adamw_meta")
    d_norm, nm_norm, nv_norm = _adamw(norm_g, g_norm, m_norm_g, v_norm_g, "adamw_norm")
    gw_in, d_in, nm_in, nv_in = _adamw_halves(w_in[0], h_in, s_in, m_w_in[0], v_w_in[0], "adamw_in")
    d_bf, nm_bf, nv_bf = _adamw(b_f, g_bf, m_b_f, v_b_f, "adamw_bf")
    gw_out, d_out, nm_out, nv_out = _adamw_halves(w_out[0], h_out, s_out, m_w_out[0], v_w_out[0], "adamw_out")
    d_fin, nm_fin, nv_fin = _adamw(final_g.reshape(1, D), g_final.reshape(1, D), m_final_g.reshape(1, D),
                                   v_final_g.reshape(1, D), "adamw_final")
    return (loss_all, gx[None], gw_meta, g_norm, gw_in[None], g_bf, gw_out[None], g_final,
            d_meta, d_norm, d_in[None], d_bf, d_out[None], d_fin.reshape(D),
            nm_meta, nm_norm, nm_in[None], nm_bf, nm_out[None], nm_fin.reshape(D),
            nv_meta, nv_norm, nv_in[None], nv_bf, nv_out[None], nv_fin.reshape(D))
```

```python
import numpy as np
import jax
import jax.numpy as jnp
from jax import lax
from jax.experimental import pallas as pl
from jax.experimental.pallas import tpu as pltpu

D = 1024
SEQ = 2048
NMETA = 16
C = 128
PAD = C - NMETA
T = PAD + NMETA + SEQ
NCH = T // C
RH, RDK, RDV = 4, 128, 256
FH, FD = 16, 64
NPAIR = FH // 2
WMAIN = 7168
NFF = 16
WIN = WMAIN + NFF
WSH = WIN // 4
DMIX = 2048
EPS = 1e-6
NEG = -1e30
RSCALE = RDK ** -0.5
FSCALE = FD ** -0.5
ROPE_BASE = 10000.0
LR, B1, B2, AEPS, WD, STEP = 0.001, 0.9, 0.999, 1e-08, 0.01, 10

BF = jnp.bfloat16
F32 = jnp.float32
NT = (((1,), (1,)), ((), ()))
TN = (((0,), (0,)), ((), ()))
HI = lax.Precision.HIGHEST
MESH = pl.DeviceIdType.MESH
ANY = pl.BlockSpec(memory_space=pl.ANY)
VMEM_LIMIT = 48 * 1024 * 1024

QB_R, KB_R = 0, 4
VB_R = 4
GB_R, GB_F = 2, 6
QB_F, KB_F, VB_F = 24, 32, 40


def _dot(a, b):
    return jnp.dot(a, b, preferred_element_type=F32)


def _dg(a, b, dims):
    return lax.dot_general(a, b, dims, preferred_element_type=F32)


def _params(sem=None):
    return pltpu.CompilerParams(dimension_semantics=sem, vmem_limit_bytes=VMEM_LIMIT)


def _constants():
    pos = jnp.arange(T, dtype=F32) - PAD
    inv = ROPE_BASE ** (-jnp.arange(0, RDK, 2, dtype=F32) / RDK)
    ang = pos[:, None] * inv[None, :]
    cos, sin = jnp.cos(ang), jnp.sin(ang)
    cos2 = jnp.concatenate([cos, cos], axis=1)
    sin2 = jnp.concatenate([-sin, sin], axis=1)
    log_gamma = jnp.log1p(-jnp.exp2(-5.0 - jnp.arange(RH, dtype=F32)))
    idx = jnp.arange(C, dtype=F32)
    diff = idx[:, None] - idx[None, :]
    dmask = jnp.where(diff[None] >= 0, jnp.exp(log_gamma[:, None, None] * jnp.maximum(diff, 0.0)[None]), 0.0)
    zeta = jnp.exp(log_gamma[:, None] * (C - 1.0 - idx)[None, :])
    xi = jnp.exp(log_gamma[:, None] * (idx + 1.0)[None, :])
    gdec = jnp.exp(log_gamma * C)
    zeta_b = jnp.broadcast_to(zeta[:, :, None], (RH, C, RDK))
    xi_b = jnp.broadcast_to(xi[:, :, None], (RH, C, RDK))
    gdec_b = jnp.broadcast_to(gdec[:, None, None], (RH, RDK, RDV))
    tri = jnp.asarray(np.tril(np.ones((C, C), np.float32)))
    head_of_lane = np.arange(FH * FD) // FD
    spread = (np.arange(C)[:, None] == head_of_lane[None, :]).astype(np.float32)
    pick = ((np.arange(FH * FD)[:, None] % FD == 0)
            & (head_of_lane[:, None] == np.arange(C)[None, :])).astype(np.float32)
    seg = (np.arange(C)[:, None] // FD == np.arange(C)[None, :] // FD).astype(np.float32)
    ones_aug = np.concatenate([np.tile((np.arange(C) < FD)[None, :], (C, 1)),
                               np.tile((np.arange(C) >= FD)[None, :], (C, 1))], axis=0).astype(np.float32)
    lane = np.arange(2 * C) % C
    causal = np.where(lane[None, :] <= np.arange(C)[:, None], 0.0, NEG).astype(np.float32)
    mask_bias = np.stack([np.zeros((C, 2 * C), np.float32), causal, np.full((C, 2 * C), NEG, np.float32)])
    return dict(cos2=cos2, sin2=sin2, dmask=dmask, zeta=zeta_b, xi=xi_b, gdec=gdec_b, tri=tri,
                mask_bias=jnp.asarray(mask_bias),
                spread=jnp.asarray(spread), pick=jnp.asarray(pick), seg=jnp.asarray(seg, dtype=BF),
                ones_aug=jnp.asarray(ones_aug, dtype=BF))


def _norm_in(hpad, g):
    def body(h_ref, g_ref, u_ref, ut_ref):
        h = h_ref[...]
        rs = lax.rsqrt(jnp.mean(h * h, axis=1, keepdims=True) + EPS)
        u = h * rs * g_ref[...]
        u_ref[...] = u.astype(BF)
        ut_ref[...] = u.T.astype(BF)

    return pl.pallas_call(
        body, name="norm_in", grid=(NCH,),
        in_specs=[pl.BlockSpec((C, D), lambda i: (i, 0)), pl.BlockSpec((1, D), lambda i: (0, 0))],
        out_specs=[pl.BlockSpec((C, D), lambda i: (i, 0)), pl.BlockSpec((D, C), lambda i: (0, i))],
        out_shape=[jax.ShapeDtypeStruct((T, D), BF), jax.ShapeDtypeStruct((D, T), BF)],
        compiler_params=_params(("parallel",)),
    )(hpad, g)


def _mm_nn(a, b, tm, tn, name):
    m, k = a.shape
    _, n = b.shape

    def body(a_ref, b_ref, o_ref):
        o_ref[...] = _dot(a_ref[...], b_ref[...])

    return pl.pallas_call(
        body, name=name, grid=(m // tm, n // tn),
        in_specs=[pl.BlockSpec((tm, k), lambda i, j: (i, 0)), pl.BlockSpec((k, tn), lambda i, j: (0, j))],
        out_specs=pl.BlockSpec((tm, tn), lambda i, j: (i, j)),
        out_shape=jax.ShapeDtypeStruct((m, n), F32),
        compiler_params=_params(("parallel", "parallel")),
    )(a, b)


def _rot(x, cos2, sin2):
    return x * cos2 + pltpu.roll(x, 64, 1) * sin2


def _ret_specs(chunk):
    return [
        pl.BlockSpec((C, RDK), lambda h, n: (chunk(n), QB_R + h)),
        pl.BlockSpec((C, RDK), lambda h, n: (chunk(n), KB_R + h)),
        pl.BlockSpec((C, RDV), lambda h, n: (chunk(n), VB_R + h)),
        pl.BlockSpec((C, RDK), lambda h, n: (chunk(n), 0)),
        pl.BlockSpec((C, RDK), lambda h, n: (chunk(n), 0)),
        pl.BlockSpec((1, C, C), lambda h, n: (h, 0, 0)),
        pl.BlockSpec((1, C, RDK), lambda h, n: (h, 0, 0)),
        pl.BlockSpec((1, C, RDK), lambda h, n: (h, 0, 0)),
        pl.BlockSpec((1, RDK, RDV), lambda h, n: (h, 0, 0)),
    ]


def _ret_fwd(z, cst):
    def body(q_ref, k_ref, v_ref, cos_ref, sin_ref, dm_ref, xi_ref, zt_ref, gd_ref, r_ref, sp_ref, st):
        n = pl.program_id(1)

        @pl.when(n == 0)
        def _():
            st[...] = jnp.zeros_like(st)

        cos, sin = cos_ref[...], sin_ref[...]
        qr = _rot(q_ref[...], cos, sin)
        kr = _rot(k_ref[...], cos, sin) * RSCALE
        qb, kb, vb = qr.astype(BF), kr.astype(BF), v_ref[...].astype(BF)
        sd = (_dg(qb, kb, NT) * dm_ref[0]).astype(BF)
        state = st[...]
        sp_ref[0, 0] = state
        qx = (qr * xi_ref[0]).astype(BF)
        r_ref[...] = _dot(sd, vb) + _dot(qx, state.astype(BF))
        kz = (kr * zt_ref[0]).astype(BF)
        st[...] = state * gd_ref[0] + _dg(kz, vb, TN)

    return pl.pallas_call(
        body, name="ret_fwd", grid=(RH, NCH),
        in_specs=_ret_specs(lambda n: n),
        out_specs=[pl.BlockSpec((C, RDV), lambda h, n: (n, h)),
                   pl.BlockSpec((1, 1, RDK, RDV), lambda h, n: (n, h, 0, 0))],
        out_shape=[jax.ShapeDtypeStruct((T, RH * RDV), F32), jax.ShapeDtypeStruct((NCH, RH, RDK, RDV), F32)],
        scratch_shapes=[pltpu.VMEM((RDK, RDV), F32)],
        compiler_params=_params(("parallel", "arbitrary")),
    )(z, z, z, cst["cos2"], cst["sin2"], cst["dmask"], cst["xi"], cst["zeta"], cst["gdec"])


def _ret_bwd(z, cst, sprev, dr):
    def body(q_ref, k_ref, v_ref, cos_ref, sin_ref, dm_ref, xi_ref, zt_ref, gd_ref, sp_ref, dr_ref,
             dq_ref, dk_ref, dv_ref, gst):
        i = pl.program_id(1)

        @pl.when(i == 0)
        def _():
            gst[...] = jnp.zeros_like(gst)

        cos, sin = cos_ref[...], sin_ref[...]
        dm, xi, zt = dm_ref[0], xi_ref[0], zt_ref[0]
        qr = _rot(q_ref[...], cos, sin)
        kr = _rot(k_ref[...], cos, sin) * RSCALE
        qb, kb, vb = qr.astype(BF), kr.astype(BF), v_ref[...].astype(BF)
        sd = (_dg(qb, kb, NT) * dm).astype(BF)
        qx = (qr * xi).astype(BF)
        kz = (kr * zt).astype(BF)
        drb = dr_ref[...]
        sb = sp_ref[0, 0].astype(BF)
        g = gst[...]
        gb = g.astype(BF)
        ds = (_dg(drb, vb, NT) * dm).astype(BF)
        dq = _dot(ds, kb) + _dg(drb, sb, NT) * xi
        dk = _dg(ds, qb, TN) + _dg(vb, gb, NT) * zt
        dv = _dg(sd, drb, TN) + _dot(kz, gb)
        gst[...] = g * gd_ref[0] + _dg(qx, drb, TN)
        dq_ref[...] = (dq * cos + pltpu.roll(dq * sin, 64, 1)).astype(BF)
        dkr = dk * RSCALE
        dk_ref[...] = (dkr * cos + pltpu.roll(dkr * sin, 64, 1)).astype(BF)
        dv_ref[...] = dv.astype(BF)

    rev = lambda n: NCH - 1 - n
    return pl.pallas_call(
        body, name="ret_bwd", grid=(RH, NCH),
        in_specs=_ret_specs(rev) + [
            pl.BlockSpec((1, 1, RDK, RDV), lambda h, n: (rev(n), h, 0, 0)),
            pl.BlockSpec((C, RDV), lambda h, n: (rev(n), h)),
        ],
        out_specs=[pl.BlockSpec((C, RDK), lambda h, n: (rev(n), h)),
                   pl.BlockSpec((C, RDK), lambda h, n: (rev(n), h)),
                   pl.BlockSpec((C, RDV), lambda h, n: (rev(n), h))],
        out_shape=[jax.ShapeDtypeStruct((T, RH * RDK), BF), jax.ShapeDtypeStruct((T, RH * RDK), BF),
                   jax.ShapeDtypeStruct((T, RH * RDV), BF)],
        scratch_shapes=[pltpu.VMEM((RDK, RDV), F32)],
        compiler_params=_params(("parallel", "arbitrary")),
    )(z, z, z, cst["cos2"], cst["sin2"], cst["dmask"], cst["xi"], cst["zeta"], cst["gdec"], sprev, dr)


def _log_sigmoid(x):
    return -(jnp.maximum(-x, 0.0) + jnp.log1p(jnp.exp(-jnp.abs(x))))


def _fox_prep(zf, bf_pad, cst):
    def body(zf_ref, b_ref, tri_ref, spread_ref, cb_ref, ct_ref, carry):
        n = pl.program_id(0)

        @pl.when(n == 0)
        def _():
            carry[...] = jnp.zeros_like(carry)

        ls = _log_sigmoid(zf_ref[...] + b_ref[...])
        row = n * C + lax.broadcasted_iota(jnp.int32, (C, C), 0)
        lf = jnp.where(row >= PAD, ls, 0.0)
        cc = jnp.dot(tri_ref[...], lf, precision=HI, preferred_element_type=F32) + carry[0:1, :]
        carry[...] = jnp.broadcast_to(cc[C - 1:C, :], carry.shape)
        cb_ref[...] = jnp.dot(cc, spread_ref[...], precision=HI, preferred_element_type=F32)
        pos = n * C + lax.broadcasted_iota(jnp.int32, (FH, C), 1)
        ct_ref[0] = jnp.where(pos >= PAD, cc.T[:FH, :], -NEG)

    return pl.pallas_call(
        body, name="fox_prep", grid=(NCH,),
        in_specs=[pl.BlockSpec((C, C), lambda n: (n, 0)), pl.BlockSpec((1, C), lambda n: (0, 0)),
                  pl.BlockSpec((C, C), lambda n: (0, 0)), pl.BlockSpec((C, FH * FD), lambda n: (0, 0))],
        out_specs=[pl.BlockSpec((C, FH * FD), lambda n: (_fox_pos(n), 0)),
                   pl.BlockSpec((1, FH, C), lambda n: (n, 0, 0))],
        out_shape=[jax.ShapeDtypeStruct((TROWS, FH * FD), F32), jax.ShapeDtypeStruct((NCH, FH, C), F32)],
        scratch_shapes=[pltpu.VMEM((8, C), F32)],
        compiler_params=_params(("arbitrary",)),
    )(zf, bf_pad, cst["tri"], cst["spread"])


def _lo_lanes(shape):
    return lax.broadcasted_iota(jnp.int32, shape, 1) < FD


def _split_heads(x):
    lo = _lo_lanes(x.shape)
    zero = jnp.zeros_like(x)
    return jnp.concatenate([jnp.where(lo, x, zero), jnp.where(lo, zero, x)], axis=0)


def _spread2(x):
    lo = _lo_lanes(x.shape)
    r = pltpu.roll(x, FD, 1)
    return jnp.concatenate([jnp.where(lo, x, r), jnp.where(lo, r, x)], axis=1)


NSTEP = (NCH + 1) // 2
NTILE = NCH + 1
TROWS = T + C


def _fox_tile(s, t):
    second = t > s
    j = jnp.where(second, t - s - 1, t)
    iq = jnp.where(second, NCH - 1 - s, s)
    kind = jnp.where(second & (s == NSTEP - 1), 2, (j == iq).astype(jnp.int32))
    return second.astype(jnp.int32), j, kind


def _fox_pos(i):
    return jnp.where(i < NSTEP, 2 * i, 2 * (NCH - 1 - i) + 1)


FOX_ORDER = [2 * i if i < NSTEP else 2 * (NCH - 1 - i) + 1 for i in range(NCH)]


def _fox_pair_specs():
    first = pl.BlockSpec((C, C), lambda p, s: (2 * s, p))
    second = pl.BlockSpec((C, C), lambda p, s: (jnp.where(s == NSTEP - 1, 2 * s, 2 * s + 1), p))
    both = pl.BlockSpec((2 * C, C), lambda p, s: (s, p))
    return first, second, both


def _fox_q_specs():
    return (pl.BlockSpec((C, C), lambda p, s: (s, QB_F + p)),
            pl.BlockSpec((C, C), lambda p, s: (NCH - 1 - s, QB_F + p)))


def _fox_key_bias(ct_ref, p, j):
    return jnp.concatenate([ct_ref[j, pl.ds(2 * p, 1), :], ct_ref[j, pl.ds(2 * p + 1, 1), :]], axis=1)


def _fox_fwd(z, cb, ct, cst):
    def body(qa_ref, qb_ref, k_ref, v_ref, ca_ref, cb_ref, ct_ref, ones_ref, mb_ref,
             a_ref, g_ref, kks, vvs, q2, ci2, m2, sbuf):
        p, s = pl.program_id(0), pl.program_id(1)

        @pl.when(s == 0)
        def _():
            ones = ones_ref[...]

            def prep(j, carry):
                rows = pl.ds(pl.multiple_of(j * C, C), C)
                kks[j] = _split_heads(k_ref[rows, :]).astype(BF)
                vvs[j] = jnp.concatenate([_split_heads(v_ref[rows, :]).astype(BF), ones], axis=1)
                return carry

            lax.fori_loop(0, NCH, prep, 0)

        for w, (q_ref, c_ref) in enumerate(((qa_ref, ca_ref), (qb_ref, cb_ref))):
            q2[w] = (q_ref[...] * FSCALE).astype(BF)
            ci2[w] = _spread2(c_ref[...])

        tiles = [_fox_tile(s, t) for t in range(NTILE)]
        neg = jnp.full((C, 2 * C), NEG, F32)
        mx = [neg, neg]
        for t, (sel, j, kind) in enumerate(tiles):
            st = _dg(q2[sel], kks[j], NT) + ((ci2[sel] - _fox_key_bias(ct_ref, p, j)) + mb_ref[kind])
            sbuf[t] = st
            mx = [jnp.maximum(mx[0], jnp.where(t <= s, st, neg)), jnp.maximum(mx[1], jnp.where(t <= s, neg, st))]
        for w in range(2):
            m2[w] = jnp.concatenate(
                [jnp.broadcast_to(jnp.max(mx[w][:, :C], axis=1, keepdims=True), (C, C)),
                 jnp.broadcast_to(jnp.max(mx[w][:, C:], axis=1, keepdims=True), (C, C))], axis=1)

        zero = jnp.zeros((C, 2 * C), F32)
        acc = [zero, zero]
        for t, (sel, j, _) in enumerate(tiles):
            part = _dot(jnp.exp(sbuf[t] - m2[sel]).astype(BF), vvs[j])
            acc = [acc[0] + jnp.where(t <= s, part, zero), acc[1] + jnp.where(t <= s, zero, part)]
        lo = _lo_lanes((C, C))
        for w, c_ref in enumerate((ca_ref, cb_ref)):
            res = acc[w]
            l = res[:, C:]
            a_ref[C * w:C * (w + 1), :] = res[:, :C] / l
            mw = m2[w]
            g_ref[C * w:C * (w + 1), :] = c_ref[...] - (jnp.where(lo, mw[:, :C], mw[:, C:]) + jnp.log(l))

    qa, qb = _fox_q_specs()
    ca, cbs, both = _fox_pair_specs()
    return pl.pallas_call(
        body, name="fox_fwd", grid=(NPAIR, NSTEP),
        in_specs=[qa, qb,
                  pl.BlockSpec((T, C), lambda p, s: (0, KB_F + p)),
                  pl.BlockSpec((T, C), lambda p, s: (0, VB_F + p)),
                  ca, cbs,
                  pl.BlockSpec((NCH, FH, C), lambda p, s: (0, 0, 0)),
                  pl.BlockSpec((2 * C, C), lambda p, s: (0, 0)),
                  pl.BlockSpec((3, C, 2 * C), lambda p, s: (0, 0, 0))],
        out_specs=[both, both],
        out_shape=[jax.ShapeDtypeStruct((TROWS, FH * FD), F32)] * 2,
        scratch_shapes=[pltpu.VMEM((NCH, 2 * C, C), BF), pltpu.VMEM((NCH, 2 * C, 2 * C), BF),
                        pltpu.VMEM((2, C, C), BF), pltpu.VMEM((2, C, 2 * C), F32), pltpu.VMEM((2, C, 2 * C), F32),
                        pltpu.VMEM((NTILE, C, 2 * C), F32)],
        compiler_params=_params(("parallel", "arbitrary")),
    )(z, z, z, z, cb, cb, ct, cst["ones_aug"], cst["mask_bias"])


def _fox_bwd(z, da, g, delta, ct, cst):
    grp = 9

    def body(qa_ref, qb_ref, daa_ref, dab_ref, ga_ref, gb_ref, dla_ref, dlb_ref, k_ref, v_ref, ct_ref, ones_ref,
             mb_ref, dq_ref, dr_ref, dk_ref, dv_ref, dcs_ref,
             kks, vvs, q2, qq2, dd2, da2, gi2, dl2, dq2, dvb, dkb, dkacc, dvacc, csacc):
        p, s = pl.program_id(0), pl.program_id(1)
        ones = ones_ref[...]

        @pl.when(s == 0)
        def _():
            dkacc[...] = jnp.zeros_like(dkacc)
            dvacc[...] = jnp.zeros_like(dvacc)
            csacc[...] = jnp.zeros_like(csacc)

            def prep(j, carry):
                rows = pl.ds(pl.multiple_of(j * C, C), C)
                kks[j] = _split_heads(k_ref[rows, :]).astype(BF)
                vvs[j] = _split_heads(v_ref[rows, :]).astype(BF)
                return carry

            lax.fori_loop(0, NCH, prep, 0)

        for w, (q_ref, d_ref, g_ref, l_ref) in enumerate(((qa_ref, daa_ref, ga_ref, dla_ref),
                                                          (qb_ref, dab_ref, gb_ref, dlb_ref))):
            qf = q_ref[...]
            q2[w] = (qf * FSCALE).astype(BF)
            qq2[w] = jnp.concatenate([_split_heads(qf).astype(BF), ones], axis=1)
            da2[w] = d_ref[...]
            dd2[w] = _split_heads(d_ref[...].astype(F32)).astype(BF)
            gi2[w] = _spread2(g_ref[...])
            dl2[w] = _spread2(l_ref[...])
        dq2[...] = jnp.zeros_like(dq2)
        zero = jnp.zeros((C, 2 * C), F32)

        def group(gi, carry):
            ts = [gi * grp + u for u in range(grp)]
            tiles = [_fox_tile(s, t) for t in ts]
            kk = [kks[j] for _, j, _ in tiles]
            ss = [_dg(q2[sel], kj, NT) + ((gi2[sel] - _fox_key_bias(ct_ref, p, j)) + mb_ref[kind])
                  for kj, (sel, j, kind) in zip(kk, tiles)]
            dps = [_dg(da2[sel], vvs[j], NT) for sel, j, _ in tiles]
            pes = [jnp.exp(st) for st in ss]
            dss = [pe * (dp - dl2[sel]) * FSCALE for pe, dp, (sel, _, _) in zip(pes, dps, tiles)]
            pts = [jnp.concatenate([pe[:, :C].T, pe[:, C:].T], axis=1).astype(BF) for pe in pes]
            dsts = [jnp.concatenate([ds[:, :C].T, ds[:, C:].T], axis=1).astype(BF) for ds in dss]
            dvs = [_dot(pt, dd2[sel]) for pt, (sel, _, _) in zip(pts, tiles)]
            rs = [_dot(dst, qq2[sel]) for dst, (sel, _, _) in zip(dsts, tiles)]
            parts = [_dot(ds.astype(BF), jnp.concatenate([kj, ones], axis=1)) for ds, kj in zip(dss, kk)]
            for t, dv, rr in zip(ts, dvs, rs):
                dvb[t] = dv
                dkb[t] = rr
            pa, pb = zero, zero
            for t, part in zip(ts, parts):
                pa = pa + jnp.where(t <= s, part, zero)
                pb = pb + jnp.where(t <= s, zero, part)
            dq2[0] += pa
            dq2[1] += pb
            return carry

        lax.fori_loop(0, NTILE // grp, group, 0)

        def scatter(t, carry):
            _, j, _ = _fox_tile(s, t)
            r = pl.ds(pl.multiple_of(j * C, C), C)
            dvacc[r, :] += dvb[t]
            dkacc[r, :] += dkb[t, :, :C]
            csacc[r, :] += dkb[t, :, C:]
            return carry

        lax.fori_loop(0, NTILE, scatter, 0)
        for w in range(2):
            res = dq2[w]
            dq_ref[C * w:C * (w + 1), :] = res[:, :C].astype(BF)
            dr_ref[C * w:C * (w + 1), :] = res[:, C:]

        @pl.when(s == NSTEP - 1)
        def _():
            dk_ref[...] = dkacc[...].astype(BF)
            dv_ref[...] = dvacc[...].astype(BF)
            dcs_ref[...] = csacc[...]

    qa, qb = _fox_q_specs()
    ba, bb, both = _fox_pair_specs()
    col = pl.BlockSpec((T, C), lambda p, s: (0, p))
    return pl.pallas_call(
        body, name="fox_bwd", grid=(NPAIR, NSTEP),
        in_specs=[qa, qb, ba, bb, ba, bb, ba, bb,
                  pl.BlockSpec((T, C), lambda p, s: (0, KB_F + p)),
                  pl.BlockSpec((T, C), lambda p, s: (0, VB_F + p)),
                  pl.BlockSpec((NCH, FH, C), lambda p, s: (0, 0, 0)),
                  pl.BlockSpec((2 * C, C), lambda p, s: (0, 0)),
                  pl.BlockSpec((3, C, 2 * C), lambda p, s: (0, 0, 0))],
        out_specs=[both, both, col, col, col],
        out_shape=[jax.ShapeDtypeStruct((TROWS, FH * FD), BF), jax.ShapeDtypeStruct((TROWS, FH * FD), F32),
                   jax.ShapeDtypeStruct((T, FH * FD), BF), jax.ShapeDtypeStruct((T, FH * FD), BF),
                   jax.ShapeDtypeStruct((T, FH * FD), F32)],
        scratch_shapes=[pltpu.VMEM((NCH, 2 * C, C), BF), pltpu.VMEM((NCH, 2 * C, C), BF),
                        pltpu.VMEM((2, C, C), BF), pltpu.VMEM((2, 2 * C, 2 * C), BF), pltpu.VMEM((2, 2 * C, C), BF),
                        pltpu.VMEM((2, C, C), BF), pltpu.VMEM((2, C, 2 * C), F32), pltpu.VMEM((2, C, 2 * C), F32),
                        pltpu.VMEM((2, C, 2 * C), F32),
                        pltpu.VMEM((NTILE, C, C), F32), pltpu.VMEM((NTILE, C, 2 * C), F32),
                        pltpu.VMEM((T, C), F32), pltpu.VMEM((T, C), F32), pltpu.VMEM((T, C), F32)],
        compiler_params=_params(("parallel", "arbitrary")),
    )(z, z, da, da, g, g, delta, delta, z, z, ct, cst["ones_aug"], cst["mask_bias"])


def _fox_gate_bwd(drow, dcol, zf, bf_pad, cst):
    def body(dr_ref, dc_ref, zf_ref, b_ref, tri_ref, pick_ref, dff_ref, db_ref, carry):
        s = pl.program_id(0)
        n = NCH - 1 - s

        @pl.when(s == 0)
        def _():
            carry[...] = jnp.zeros_like(carry)
            db_ref[...] = jnp.zeros_like(db_ref)

        dcb = jnp.dot((dr_ref[...] - dc_ref[...]) * (1.0 / FSCALE), pick_ref[...], precision=HI,
                      preferred_element_type=F32)
        suf = lax.dot_general(tri_ref[...], dcb, TN, precision=HI, preferred_element_type=F32) + carry[0:1, :]
        carry[...] = jnp.broadcast_to(suf[0:1, :], carry.shape)
        x = zf_ref[...] + b_ref[...]
        row = n * C + lax.broadcasted_iota(jnp.int32, (C, C), 0)
        dff = jnp.where(row >= PAD, suf * (1.0 - jax.nn.sigmoid(x)), 0.0)
        dff_ref[...] = dff.astype(BF)
        db_ref[...] += jnp.sum(dff, axis=0, keepdims=True)

    rev = lambda s: (NCH - 1 - s, 0)
    return pl.pallas_call(
        body, name="fox_gate_bwd", grid=(NCH,),
        in_specs=[pl.BlockSpec((C, FH * FD), lambda s: (_fox_pos(NCH - 1 - s), 0)),
                  pl.BlockSpec((C, FH * FD), rev), pl.BlockSpec((C, C), rev),
                  pl.BlockSpec((1, C), lambda s: (0, 0)), pl.BlockSpec((C, C), lambda s: (0, 0)),
                  pl.BlockSpec((FH * FD, C), lambda s: (0, 0))],
        out_specs=[pl.BlockSpec((C, C), rev), pl.BlockSpec((1, C), lambda s: (0, 0))],
        out_shape=[jax.ShapeDtypeStruct((T, C), BF), jax.ShapeDtypeStruct((1, C), F32)],
        scratch_shapes=[pltpu.VMEM((8, C), F32)],
        compiler_params=_params(("arbitrary",)),
    )(drow, dcol, zf, bf_pad, cst["tri"], cst["pick"])


def _gated(r, rg, a, fg):
    rn, rs = [], []
    for h in range(RH):
        rh = r[:, RDV * h:RDV * (h + 1)]
        s = lax.rsqrt(jnp.mean(rh * rh, axis=1, keepdims=True) + EPS)
        rn.append(rh * s)
        rs.append(s)
    rn = jnp.concatenate(rn, axis=1)
    y = jnp.concatenate([rn * (rg * jax.nn.sigmoid(rg)), a * (fg * jax.nn.sigmoid(fg))], axis=1)
    return y, rn, rs


def _out_loss(r, z, a, wout, x, tgt, fgain):
    def body(r_ref, rg_ref, a_ref, fg_ref, w_ref, x_ref, t_ref, g_ref, yt_ref, do_ref, dob_ref, loss_ref, dg_ref):
        i = pl.program_id(0)

        @pl.when(i == 0)
        def _():
            yt_ref[...] = jnp.zeros_like(yt_ref)
            do_ref[...] = jnp.zeros_like(do_ref)
            dob_ref[...] = jnp.zeros_like(dob_ref)
            loss_ref[...] = jnp.zeros_like(loss_ref)
            dg_ref[...] = jnp.zeros_like(dg_ref)

        @pl.when(i > 0)
        def _():
            y, _, _ = _gated(r_ref[...], rg_ref[...], a_ref[...], fg_ref[...])
            yt_ref[...] = y.T.astype(BF)
            o = x_ref[...] + _dot(y.astype(BF), w_ref[...])
            rs = lax.rsqrt(jnp.mean(o * o, axis=1, keepdims=True) + EPS)
            on = o * rs
            g = g_ref[...]
            e = on * g - t_ref[...]
            loss_ref[...] += 0.5 * jnp.sum(jnp.mean(e * e, axis=1, keepdims=True))
            dyh = e * (1.0 / D)
            dg_ref[...] += jnp.sum(dyh * on, axis=0, keepdims=True)
            don = dyh * g
            do = rs * (don - on * jnp.mean(don * on, axis=1, keepdims=True))
            do_ref[...] = do
            dob_ref[...] = do.astype(BF)

    tok = lambda i: (jnp.maximum(i - 1, 0), 0)
    return pl.pallas_call(
        body, name="out_loss", grid=(NCH,),
        in_specs=[pl.BlockSpec((C, D), lambda i: (i, 0)), pl.BlockSpec((C, D), lambda i: (i, GB_R)),
                  pl.BlockSpec((C, D), lambda i: (_fox_pos(i), 0)), pl.BlockSpec((C, D), lambda i: (i, GB_F)),
                  pl.BlockSpec((DMIX, D), lambda i: (0, 0)),
                  pl.BlockSpec((C, D), tok), pl.BlockSpec((C, D), tok), pl.BlockSpec((1, D), lambda i: (0, 0))],
        out_specs=[pl.BlockSpec((DMIX, C), lambda i: (0, i)), pl.BlockSpec((C, D), lambda i: (i, 0)),
                   pl.BlockSpec((C, D), lambda i: (i, 0)), pl.BlockSpec((8, C), lambda i: (0, 0)),
                   pl.BlockSpec((1, D), lambda i: (0, 0))],
        out_shape=[jax.ShapeDtypeStruct((DMIX, T), BF), jax.ShapeDtypeStruct((T, D), F32),
                   jax.ShapeDtypeStruct((T, D), BF), jax.ShapeDtypeStruct((8, C), F32),
                   jax.ShapeDtypeStruct((1, D), F32)],
        compiler_params=_params(("arbitrary",)),
    )(r, z, a, z, wout, x, tgt, fgain)


def _dsilu(x):
    s = jax.nn.sigmoid(x)
    return s * (1.0 + x * (1.0 - s))


def _dy_gate_bwd(dob, wout, r, z, a, seg):
    def body(do_ref, w_ref, r_ref, rg_ref, a_ref, fg_ref, seg_ref, dr_ref, da_ref, drg_ref, dfg_ref, dl_ref):
        dy = _dg(do_ref[...], w_ref[...], NT)
        rg, fg, a_ = rg_ref[...], fg_ref[...], a_ref[...]
        _, rn, rs = _gated(r_ref[...], rg, a_, fg)
        dyr, dyf = dy[:, :D], dy[:, D:]
        drn = dyr * (rg * jax.nn.sigmoid(rg))
        drg_ref[...] = (dyr * rn * _dsilu(rg)).astype(BF)
        for h in range(RH):
            sl = slice(RDV * h, RDV * (h + 1))
            dh, nh = drn[:, sl], rn[:, sl]
            dr_ref[:, sl] = (rs[h] * (dh - nh * jnp.mean(dh * nh, axis=1, keepdims=True))).astype(BF)
        dab = (dyf * (fg * jax.nn.sigmoid(fg))).astype(BF)
        da_ref[...] = dab
        dfg_ref[...] = (dyf * a_ * _dsilu(fg)).astype(BF)
        prod = dab.astype(F32) * a_
        segm = seg_ref[...]
        for p in range(NPAIR):
            sl = slice(C * p, C * (p + 1))
            hi = prod[:, sl].astype(BF)
            lo = (prod[:, sl] - hi.astype(F32)).astype(BF)
            dl_ref[:, sl] = _dot(hi, segm) + _dot(lo, segm)

    row = pl.BlockSpec((C, D), lambda i: (i, 0))
    fox = pl.BlockSpec((C, D), lambda i: (_fox_pos(i), 0))
    return pl.pallas_call(
        body, name="dy_gate_bwd", grid=(NCH,),
        in_specs=[row, pl.BlockSpec((DMIX, D), lambda i: (0, 0)),
                  row, pl.BlockSpec((C, D), lambda i: (i, GB_R)),
                  fox, pl.BlockSpec((C, D), lambda i: (i, GB_F)),
                  pl.BlockSpec((C, C), lambda i: (0, 0))],
        out_specs=[row, fox, row, row, fox],
        out_shape=[jax.ShapeDtypeStruct((T, D), BF), jax.ShapeDtypeStruct((TROWS, D), BF),
                   jax.ShapeDtypeStruct((T, D), BF), jax.ShapeDtypeStruct((T, D), BF),
                   jax.ShapeDtypeStruct((TROWS, D), F32)],
        compiler_params=_params(("parallel",)),
    )(dob, wout, r, z, a, z, seg)


def _du_norm_bwd(dzm, dzf, wm, wf, hpad, g, dopad):
    tm, tk = 544, 1024
    nk = WMAIN // tk

    def body(dzm_ref, dzf_ref, wm_ref, wf_ref, h_ref, g_ref, do_ref, gh_ref, dg_ref, acc):
        i, k = pl.program_id(0), pl.program_id(1)

        @pl.when(k == 0)
        def _():
            acc[...] = _dg(dzf_ref[...], wf_ref[...], NT)

        acc[...] += _dg(dzm_ref[...], wm_ref[...], NT)

        @pl.when(k == nk - 1)
        def _():
            du = acc[...]
            h = h_ref[...]
            gg = g_ref[...]
            rs = lax.rsqrt(jnp.mean(h * h, axis=1, keepdims=True) + EPS)
            hn = h * rs
            part = jnp.sum(du * hn, axis=0, keepdims=True)

            @pl.when(i == 0)
            def _():
                dg_ref[...] = part

            @pl.when(i > 0)
            def _():
                dg_ref[...] += part

            dhn = du * gg
            gh_ref[...] = rs * (dhn - hn * jnp.mean(dhn * hn, axis=1, keepdims=True)) + do_ref[...]

    return pl.pallas_call(
        body, name="du_norm_bwd", grid=(T // tm, nk),
        in_specs=[pl.BlockSpec((tm, tk), lambda i, k: (i, k)), pl.BlockSpec((tm, C), lambda i, k: (i, 0)),
                  pl.BlockSpec((D, tk), lambda i, k: (0, k)), pl.BlockSpec((D, C), lambda i, k: (0, 0)),
                  pl.BlockSpec((tm, D), lambda i, k: (i, 0)), pl.BlockSpec((1, D), lambda i, k: (0, 0)),
                  pl.BlockSpec((tm, D), lambda i, k: (i, 0))],
        out_specs=[pl.BlockSpec((tm, D), lambda i, k: (i, 0)), pl.BlockSpec((1, D), lambda i, k: (0, 0))],
        out_shape=[jax.ShapeDtypeStruct((T, D), F32), jax.ShapeDtypeStruct((1, D), F32)],
        scratch_shapes=[pltpu.VMEM((tm, D), F32)],
        compiler_params=_params(("arbitrary", "arbitrary")),
    )(dzm, dzf, wm, wf, hpad, g, dopad)


def _local_step(x, tgt, meta, norm_g, wm, wf, b_f, wout, final_g):
    cst = _constants()
    hpad = jnp.concatenate([jnp.pad(meta, ((PAD, 0), (0, 0))), x], axis=0)
    bf_pad = jnp.pad(b_f, ((0, 0), (0, C - NFF)))
    u, ut = _norm_in(hpad, norm_g)
    z = _mm_nn(u, wm, T // 2, 512, "in_proj")
    zf = _mm_nn(u, wf, T // 2, C, "in_proj_ff")
    r, sprev = _ret_fwd(z, cst)
    cb, ct = _fox_prep(zf, bf_pad, cst)
    a, g = _fox_fwd(z, cb, ct, cst)
    yt, dopad, dob, loss8, dfg = _out_loss(r, z, a, wout, x, tgt, final_g)
    dr, da, dzrg, dzfg, delta = _dy_gate_bwd(dob, wout, r, z, a, cst["seg"])
    dwout = _mm_nn(yt, dob, 512, D, "dw_out")
    dzq_r, dzk_r, dzv_r = _ret_bwd(z, cst, sprev, dr)
    dq_po, drow, dzk_f, dzv_f, dcol = _fox_bwd(z, da, g, delta, ct, cst)
    dzq_f = jnp.concatenate([dq_po[C * pos:C * (pos + 1)] for pos in FOX_ORDER], axis=0)
    dzf, dbf = _fox_gate_bwd(drow, dcol, zf, bf_pad, cst)
    dzm = jnp.concatenate([dzq_r, dzk_r, dzv_r, dzrg, dzq_f, dzk_f, dzv_f, dzfg], axis=1)
    dwm = _mm_nn(ut, dzm, 512, 1024, "dw_in")
    dwf = _mm_nn(ut, dzf, D, C, "dw_in_ff")
    gh, dng = _du_norm_bwd(dzm, dzf, wm, wf, hpad, norm_g, dopad)
    dwin = jnp.concatenate([dwm, dwf[:, :NFF]], axis=1)
    return (loss8[0, 0], gh[C:], gh[PAD:C], dng, dwin, dbf[:, :NFF], dwout, dfg)


def _place():
    x, y, c = lax.axis_index("x"), lax.axis_index("y"), lax.axis_index("c")
    return x, y, c


def _other_chips(x, y):
    return [(1 - x, y, 2 * (1 - x) + y), (x, 1 - y, 2 * x + (1 - y)), (1 - x, 1 - y, 2 * (1 - x) + (1 - y))]


def _all_gather_shards(shards):
    n = len(shards)

    def body(*refs):
        ins, outs = refs[:n], refs[n:2 * n]
        send_sems, recv_sems = refs[2 * n:]
        x, y, c = _place()
        me_s = 2 * x + y
        sib = (x, y, 1 - c)
        chips = _other_chips(x, y)
        sends, waits = [], []
        for a in range(n):
            rows = ins[a].shape[0] // 2
            half = pl.ds(c * rows, rows)
            for k, (cx, cy, cs) in enumerate(chips):
                sends.append(pltpu.make_async_remote_copy(
                    src_ref=ins[a].at[half], dst_ref=outs[a].at[me_s, half],
                    send_sem=send_sems.at[6 * a + k], recv_sem=recv_sems.at[6 * a + k],
                    device_id=(cx, cy, c), device_id_type=MESH))
                sends[-1].start()
        for a in range(n):
            rows = ins[a].shape[0] // 2
            half = pl.ds(c * rows, rows)
            other = pl.ds((1 - c) * rows, rows)
            for k, (cx, cy, cs) in enumerate(chips):
                pltpu.make_async_remote_copy(
                    src_ref=outs[a].at[cs, half], dst_ref=outs[a].at[cs, half],
                    send_sem=send_sems.at[6 * a + k], recv_sem=recv_sems.at[6 * a + k],
                    device_id=(cx, cy, c), device_id_type=MESH).wait_recv()
                fwd = pltpu.make_async_remote_copy(
                    src_ref=outs[a].at[cs, half], dst_ref=outs[a].at[cs, half],
                    send_sem=send_sems.at[6 * a + 3 + k], recv_sem=recv_sems.at[6 * a + 3 + k],
                    device_id=sib, device_id_type=MESH)
                fwd.start()
                sends.append(fwd)
                waits.append(pltpu.make_async_remote_copy(
                    src_ref=outs[a].at[cs, other], dst_ref=outs[a].at[cs, other],
                    send_sem=send_sems.at[6 * a + 3 + k], recv_sem=recv_sems.at[6 * a + 3 + k],
                    device_id=sib, device_id_type=MESH))
        for w in waits:
            w.wait_recv()
        for s in sends:
            s.wait_send()

    return pl.pallas_call(
        body, name="all_gather_w",
        in_specs=[ANY] * n, out_specs=[ANY] * n,
        out_shape=[jax.ShapeDtypeStruct((4,) + s.shape, s.dtype) for s in shards],
        scratch_shapes=[pltpu.SemaphoreType.DMA((6 * n,)), pltpu.SemaphoreType.DMA((6 * n,))],
    )(*shards)


def _pair_swap(arrs, small):
    n = len(arrs)

    def body(*refs):
        ins, sm = refs[:n], refs[n]
        outs, smo = refs[n + 1:2 * n + 1], refs[2 * n + 1]
        send_sems, recv_sems = refs[2 * n + 2:]
        x, y, c = _place()
        sib = (x, y, 1 - c)
        cps = []
        for a in range(n):
            rows = ins[a].shape[1] // 2
            cps.append(pltpu.make_async_remote_copy(
                src_ref=ins[a].at[:, pl.ds((1 - c) * rows, rows)], dst_ref=outs[a],
                send_sem=send_sems.at[a], recv_sem=recv_sems.at[a], device_id=sib, device_id_type=MESH))
        cps.append(pltpu.make_async_remote_copy(
            src_ref=sm, dst_ref=smo, send_sem=send_sems.at[n], recv_sem=recv_sems.at[n],
            device_id=sib, device_id_type=MESH))
        for cp in cps:
            cp.start()
        for cp in cps:
            cp.wait()

    return pl.pallas_call(
        body, name="rs_pair_swap",
        in_specs=[ANY] * (n + 1), out_specs=[ANY] * (n + 1),
        out_shape=[jax.ShapeDtypeStruct((4, a.shape[1] // 2, a.shape[2]), a.dtype) for a in arrs]
        + [jax.ShapeDtypeStruct(small.shape, small.dtype)],
        scratch_shapes=[pltpu.SemaphoreType.DMA((n + 1,)), pltpu.SemaphoreType.DMA((n + 1,))],
    )(*arrs, small)


def _chip_exchange(parts, small):
    n = len(parts)

    def body(*refs):
        ins, sm = refs[:n], refs[n]
        outs, smo = refs[n + 1:2 * n + 1], refs[2 * n + 1]
        send_sems, recv_sems = refs[2 * n + 2:]
        x, y, c = _place()
        me_s = 2 * x + y
        chips = _other_chips(x, y)
        cps = []
        for a in range(n + 1):
            src = ins[a] if a < n else sm
            dst = outs[a] if a < n else smo
            for k, (cx, cy, cs) in enumerate(chips):
                cps.append(pltpu.make_async_remote_copy(
                    src_ref=src.at[cs] if a < n else src, dst_ref=dst.at[me_s],
                    send_sem=send_sems.at[3 * a + k], recv_sem=recv_sems.at[3 * a + k],
                    device_id=(cx, cy, c), device_id_type=MESH))
        for cp in cps:
            cp.start()
        for cp in cps:
            cp.wait()

    return pl.pallas_call(
        body, name="rs_chip_exchange",
        in_specs=[ANY] * (n + 1), out_specs=[ANY] * (n + 1),
        out_shape=[jax.ShapeDtypeStruct(p.shape, p.dtype) for p in parts]
        + [jax.ShapeDtypeStruct((4,) + small.shape, small.dtype)],
        scratch_shapes=[pltpu.SemaphoreType.DMA((3 * (n + 1),)), pltpu.SemaphoreType.DMA((3 * (n + 1),))],
    )(*parts, small)


def _pair_send(halves):
    n = len(halves)

    def body(*refs):
        ins, outs = refs[:n], refs[n:2 * n]
        send_sems, recv_sems = refs[2 * n:]
        x, y, c = _place()
        cps = [pltpu.make_async_remote_copy(
            src_ref=ins[a], dst_ref=outs[a], send_sem=send_sems.at[a], recv_sem=recv_sems.at[a],
            device_id=(x, y, 1 - c), device_id_type=MESH) for a in range(n)]
        for cp in cps:
            cp.start()
        for cp in cps:
            cp.wait()

    return pl.pallas_call(
        body, name="rs_pair_send",
        in_specs=[ANY] * n, out_specs=[ANY] * n,
        out_shape=[jax.ShapeDtypeStruct(h.shape, h.dtype) for h in halves],
        scratch_shapes=[pltpu.SemaphoreType.DMA((n,)), pltpu.SemaphoreType.DMA((n,))],
    )(*halves)


def _row_block(rows):
    for tb in (256, 128, 64, 32, 16, 8):
        if rows % tb == 0:
            return tb
    return rows


def _add_halves(full, recv, name, out_dtype):
    _, r2, w = recv.shape
    tb = _row_block(r2)
    nb = r2 // tb
    c = lax.axis_index("c")

    def body(c_ref, a_ref, b_ref, o_ref):
        o_ref[...] = (a_ref[...] + b_ref[...]).astype(o_ref.dtype)

    return pl.pallas_call(
        body, name=name,
        grid_spec=pltpu.PrefetchScalarGridSpec(
            num_scalar_prefetch=1, grid=(4, nb),
            in_specs=[pl.BlockSpec((1, tb, w), lambda s, i, cr: (s, cr[0] * nb + i, 0)),
                      pl.BlockSpec((1, tb, w), lambda s, i, cr: (s, i, 0))],
            out_specs=pl.BlockSpec((1, tb, w), lambda s, i, cr: (s, i, 0))),
        out_shape=jax.ShapeDtypeStruct(recv.shape, out_dtype),
        compiler_params=_params(("parallel", "parallel")),
    )(jnp.reshape(c, (1,)).astype(jnp.int32), full, recv)


def _add2(a, b, name):
    def body(a_ref, b_ref, o_ref):
        o_ref[...] = a_ref[...] + b_ref[...]

    return pl.pallas_call(body, name=name, out_shape=jax.ShapeDtypeStruct(a.shape, a.dtype))(a, b)


def _sum4(buf, own, name):
    _, r, w = buf.shape
    tb = _row_block(r)
    me_s = 2 * lax.axis_index("x") + lax.axis_index("y")
    by_dest = own.ndim == 3

    def body(s_ref, b_ref, own_ref, o_ref):
        mine = (own_ref[0] if by_dest else own_ref[...]).astype(F32)
        terms = [jnp.where(s_ref[0] == t, mine, b_ref[t].astype(F32)) for t in range(4)]
        o_ref[...] = ((terms[0] + terms[1]) + terms[2]) + terms[3]

    own_spec = (pl.BlockSpec((1, tb, w), lambda i, sr: (sr[0], i, 0)) if by_dest
                else pl.BlockSpec((tb, w), lambda i, sr: (i, 0)))
    return pl.pallas_call(
        body, name=name,
        grid_spec=pltpu.PrefetchScalarGridSpec(
            num_scalar_prefetch=1, grid=(r // tb,),
            in_specs=[pl.BlockSpec((4, tb, w), lambda i, sr: (0, i, 0)), own_spec],
            out_specs=pl.BlockSpec((tb, w), lambda i, sr: (i, 0))),
        out_shape=jax.ShapeDtypeStruct((r, w), F32),
        compiler_params=_params(("parallel",)),
    )(jnp.reshape(me_s, (1,)).astype(jnp.int32), buf, own)


def _adamw_math(w, g, m, v):
    mn = B1 * m + (1.0 - B1) * g
    vn = B2 * v + (1.0 - B2) * (g * g)
    m_hat = mn / (1.0 - B1 ** STEP)
    v_hat = vn / (1.0 - B2 ** STEP)
    return -LR * (m_hat / (jnp.sqrt(v_hat) + AEPS) + WD * w), mn, vn


def _adamw(w, g, m, v, name):
    r, c_ = w.shape
    tb = _row_block(r)

    def body(w_ref, g_ref, m_ref, v_ref, d_ref, mo_ref, vo_ref):
        d_ref[...], mo_ref[...], vo_ref[...] = _adamw_math(w_ref[...], g_ref[...], m_ref[...], v_ref[...])

    spec = pl.BlockSpec((tb, c_), lambda i: (i, 0))
    return pl.pallas_call(
        body, name=name, grid=(r // tb,),
        in_specs=[spec] * 4, out_specs=[spec] * 3,
        out_shape=[jax.ShapeDtypeStruct(w.shape, F32)] * 3,
        compiler_params=_params(("parallel",)),
    )(w, g, m, v)


def _adamw_halves(w, g_mine, g_sib, m, v, name):
    r, c_ = w.shape
    r2 = g_mine.shape[0]
    tb = _row_block(r2)
    nb = r2 // tb
    c = lax.axis_index("c")

    def body(c_ref, w_ref, gm_ref, gs_ref, m_ref, v_ref, g_ref, d_ref, mo_ref, vo_ref):
        g = jnp.where(pl.program_id(0) == c_ref[0], gm_ref[...], gs_ref[...])
        g_ref[...] = g
        d_ref[...], mo_ref[...], vo_ref[...] = _adamw_math(w_ref[...], g, m_ref[...], v_ref[...])

    full = pl.BlockSpec((tb, c_), lambda h, i, cr: (h * nb + i, 0))
    half = pl.BlockSpec((tb, c_), lambda h, i, cr: (i, 0))
    return pl.pallas_call(
        body, name=name,
        grid_spec=pltpu.PrefetchScalarGridSpec(
            num_scalar_prefetch=1, grid=(2, nb),
            in_specs=[full, half, half, full, full], out_specs=[full] * 4),
        out_shape=[jax.ShapeDtypeStruct(w.shape, F32)] * 4,
        compiler_params=_params(("parallel", "parallel")),
    )(jnp.reshape(c, (1,)).astype(jnp.int32), w, g_mine, g_sib, m, v)


def kernel(x, meta_tokens, norm_g, w_in, b_f, w_out, final_g, loss_target, m_meta_tokens, m_norm_g, m_w_in, m_b_f, m_w_out, m_final_g, v_meta_tokens, v_norm_g, v_w_in, v_b_f, v_w_out, v_final_g):
    me_s = 2 * lax.axis_index("x") + lax.axis_index("y")
    own = [w_in[0].astype(BF), w_out[0].astype(BF), meta_tokens]
    gathered = _all_gather_shards(own)
    gin, gout, gmeta = [[jnp.where(me_s == s, o, g[s]) for s in range(4)] for o, g in zip(own, gathered)]
    wfull = jnp.concatenate(gin, axis=1)
    wm = wfull[:, :WMAIN]
    wf = jnp.pad(wfull[:, WMAIN:], ((0, 0), (0, C - NFF)))
    wout = jnp.concatenate(gout, axis=0)
    meta = jnp.concatenate(gmeta, axis=1)

    loss, gx, dmeta, dng, dwin, dbf, dwout, dfg = _local_step(
        x[0], loss_target[0], meta, norm_g, wm, wf, b_f, wout, final_g.reshape(1, D))

    g_in = jnp.stack([dwin[:, WSH * s:WSH * (s + 1)] for s in range(4)])
    g_out = dwout.reshape(4, DMIX // 4, D)
    g_meta = jnp.stack([dmeta[:, 256 * s:256 * (s + 1)] for s in range(4)])
    small = jnp.concatenate([dng, dfg, jnp.pad(dbf, ((0, 0), (0, D - NFF))),
                             jnp.pad(jnp.reshape(loss, (1, 1)), ((0, 0), (0, D - 1))),
                             jnp.zeros((4, D), F32)], axis=0)
    r_in, r_out, r_meta, r_small = _pair_swap([g_in, g_out, g_meta], small)
    p_in = _add_halves(g_in, r_in, "pair_add_in", BF)
    p_out = _add_halves(g_out, r_out, "pair_add_out", BF)
    p_meta = _add_halves(g_meta, r_meta, "pair_add_meta", F32)
    p_small = _add2(small, r_small, "pair_add_small")
    e_in, e_out, e_meta, e_small = _chip_exchange([p_in, p_out, p_meta], p_small)
    h_in, h_out, h_meta = _sum4(e_in, p_in, "sum_in"), _sum4(e_out, p_out, "sum_out"), _sum4(e_meta, p_meta, "sum_meta")
    tot = _sum4(e_small, p_small, "sum_small")
    s_in, s_out, s_meta = _pair_send([h_in, h_out, h_meta])
    g_norm, g_final, g_bf, loss_all = tot[0:1], tot[1], tot[2:3, :NFF], tot[3, 0]

    gw_meta, d_meta, nm_meta, nv_meta = _adamw_halves(meta_tokens, h_meta, s_meta, m_meta_tokens, v_meta_tokens,
                                                      "adamw_meta")
    d_norm, nm_norm, nv_norm = _adamw(norm_g, g_norm, m_norm_g, v_norm_g, "adamw_norm")
    gw_in, d_in, nm_in, nv_in = _adamw_halves(w_in[0], h_in, s_in, m_w_in[0], v_w_in[0], "adamw_in")
    d_bf, nm_bf, nv_bf = _adamw(b_f, g_bf, m_b_f, v_b_f, "adamw_bf")
    gw_out, d_out, nm_out, nv_out = _adamw_halves(w_out[0], h_out, s_out, m_w_out[0], v_w_out[0], "adamw_out")
    d_fin, nm_fin, nv_fin = _adamw(final_g.reshape(1, D), g_final.reshape(1, D), m_final_g.reshape(1, D),
                                   v_final_g.reshape(1, D), "adamw_final")
    return (loss_all, gx[None], gw_meta, g_norm, gw_in[None], g_bf, gw_out[None], g_final,
            d_meta, d_norm, d_in[None], d_bf, d_out[None], d_fin.reshape(D),
            nm_meta, nm_norm, nm_in[None], nm_bf, nm_out[None], nm_fin.reshape(D),
            nv_meta, nv_norm, nv_in[None], nv_bf, nv_out[None], nv_fin.reshape(D))
```

```python
import numpy as np
import jax
import jax.numpy as jnp
from jax import lax
from jax.experimental import pallas as pl
from jax.experimental.pallas import tpu as pltpu

D = 1024
SEQ = 2048
NMETA = 16
C = 128
PAD = C - NMETA
T = PAD + NMETA + SEQ
NCH = T // C
RH, RDK, RDV = 4, 128, 256
FH, FD = 16, 64
NPAIR = FH // 2
WMAIN = 7168
NFF = 16
WIN = WMAIN + NFF
WSH = WIN // 4
WPADROWS = 1824
DMIX = 2048
EPS = 1e-6
NEG = -1e30
RSCALE = RDK ** -0.5
FSCALE = FD ** -0.5
ROPE_BASE = 10000.0
LR, B1, B2, AEPS, WD, STEP = 0.001, 0.9, 0.999, 1e-08, 0.01, 10

BF = jnp.bfloat16
F32 = jnp.float32
NT = (((1,), (1,)), ((), ()))
TN = (((0,), (0,)), ((), ()))
HI = lax.Precision.HIGHEST
MESH = pl.DeviceIdType.MESH
ANY = pl.BlockSpec(memory_space=pl.ANY)
VMEM_LIMIT = 48 * 1024 * 1024

QB_R, KB_R = 0, 4
VB_R = 4
GB_R, GB_F = 2, 6
QB_F, KB_F, VB_F = 24, 32, 40


def _dot(a, b):
    return jnp.dot(a, b, preferred_element_type=F32)


def _dg(a, b, dims):
    return lax.dot_general(a, b, dims, preferred_element_type=F32)


def _params(sem=None):
    return pltpu.CompilerParams(dimension_semantics=sem, vmem_limit_bytes=VMEM_LIMIT)


def _constants():
    pos = jnp.arange(T, dtype=F32) - PAD
    inv = ROPE_BASE ** (-jnp.arange(0, RDK, 2, dtype=F32) / RDK)
    ang = pos[:, None] * inv[None, :]
    cos, sin = jnp.cos(ang), jnp.sin(ang)
    cos2 = jnp.concatenate([cos, cos], axis=1)
    sin2 = jnp.concatenate([-sin, sin], axis=1)
    log_gamma = jnp.log1p(-jnp.exp2(-5.0 - jnp.arange(RH, dtype=F32)))
    idx = jnp.arange(C, dtype=F32)
    diff = idx[:, None] - idx[None, :]
    dmask = jnp.where(diff[None] >= 0, jnp.exp(log_gamma[:, None, None] * jnp.maximum(diff, 0.0)[None]), 0.0)
    zeta = jnp.exp(log_gamma[:, None] * (C - 1.0 - idx)[None, :])
    xi = jnp.exp(log_gamma[:, None] * (idx + 1.0)[None, :])
    gdec = jnp.exp(log_gamma * C)
    zeta_b = jnp.broadcast_to(zeta[:, :, None], (RH, C, RDK))
    xi_b = jnp.broadcast_to(xi[:, :, None], (RH, C, RDK))
    gdec_b = jnp.broadcast_to(gdec[:, None, None], (RH, RDK, RDV))
    tri = jnp.asarray(np.tril(np.ones((C, C), np.float32)))
    head_of_lane = np.arange(FH * FD) // FD
    spread = (np.arange(C)[:, None] == head_of_lane[None, :]).astype(np.float32)
    pick = ((np.arange(FH * FD)[:, None] % FD == 0)
            & (head_of_lane[:, None] == np.arange(C)[None, :])).astype(np.float32)
    seg = (np.arange(C)[:, None] // FD == np.arange(C)[None, :] // FD).astype(np.float32)
    ones_aug = np.concatenate([np.tile((np.arange(C) < FD)[None, :], (C, 1)),
                               np.tile((np.arange(C) >= FD)[None, :], (C, 1))], axis=0).astype(np.float32)
    lane = np.arange(2 * C) % C
    causal = np.where(lane[None, :] <= np.arange(C)[:, None], 0.0, NEG).astype(np.float32)
    mask_bias = np.stack([np.zeros((C, 2 * C), np.float32), causal, np.full((C, 2 * C), NEG, np.float32)])
    return dict(cos2=cos2, sin2=sin2, dmask=dmask, zeta=zeta_b, xi=xi_b, gdec=gdec_b, tri=tri,
                mask_bias=jnp.asarray(mask_bias),
                spread=jnp.asarray(spread), pick=jnp.asarray(pick), seg=jnp.asarray(seg, dtype=BF),
                ones_aug=jnp.asarray(ones_aug, dtype=BF))


def _norm_in(hpad, g):
    def body(h_ref, g_ref, u_ref):
        h = h_ref[...]
        rs = lax.rsqrt(jnp.mean(h * h, axis=1, keepdims=True) + EPS)
        u_ref[...] = (h * rs * g_ref[...]).astype(BF)

    return pl.pallas_call(
        body, name="norm_in", grid=(NCH,),
        in_specs=[pl.BlockSpec((C, D), lambda i: (i, 0)), pl.BlockSpec((1, D), lambda i: (0, 0))],
        out_specs=pl.BlockSpec((C, D), lambda i: (i, 0)),
        out_shape=jax.ShapeDtypeStruct((T, D), BF),
        compiler_params=_params(("parallel",)),
    )(hpad, g)


def _mm_nt(a, b, n, tm, tn, name):
    m, k = a.shape

    def body(a_ref, b_ref, o_ref):
        o_ref[...] = _dg(a_ref[...], b_ref[...], NT)

    return pl.pallas_call(
        body, name=name, grid=(m // tm, n // tn),
        in_specs=[pl.BlockSpec((tm, k), lambda i, j: (i, 0)), pl.BlockSpec((tn, k), lambda i, j: (j, 0))],
        out_specs=pl.BlockSpec((tm, tn), lambda i, j: (i, j)),
        out_shape=jax.ShapeDtypeStruct((m, n), F32),
        compiler_params=_params(("parallel", "parallel")),
    )(a, b)


def _mm_nn(a, b, tm, tn, name):
    m, k = a.shape
    _, n = b.shape

    def body(a_ref, b_ref, o_ref):
        o_ref[...] = _dot(a_ref[...], b_ref[...])

    return pl.pallas_call(
        body, name=name, grid=(m // tm, n // tn),
        in_specs=[pl.BlockSpec((tm, k), lambda i, j: (i, 0)), pl.BlockSpec((k, tn), lambda i, j: (0, j))],
        out_specs=pl.BlockSpec((tm, tn), lambda i, j: (i, j)),
        out_shape=jax.ShapeDtypeStruct((m, n), F32),
        compiler_params=_params(("parallel", "parallel")),
    )(a, b)


def _rot(x, cos2, sin2):
    return x * cos2 + pltpu.roll(x, 64, 1) * sin2


def _ret_specs(chunk):
    return [
        pl.BlockSpec((C, RDK), lambda h, n: (chunk(n), QB_R + h)),
        pl.BlockSpec((C, RDK), lambda h, n: (chunk(n), KB_R + h)),
        pl.BlockSpec((C, RDV), lambda h, n: (chunk(n), VB_R + h)),
        pl.BlockSpec((C, RDK), lambda h, n: (chunk(n), 0)),
        pl.BlockSpec((C, RDK), lambda h, n: (chunk(n), 0)),
        pl.BlockSpec((1, C, C), lambda h, n: (h, 0, 0)),
        pl.BlockSpec((1, C, RDK), lambda h, n: (h, 0, 0)),
        pl.BlockSpec((1, C, RDK), lambda h, n: (h, 0, 0)),
        pl.BlockSpec((1, RDK, RDV), lambda h, n: (h, 0, 0)),
    ]


def _ret_fwd(z, cst):
    def body(q_ref, k_ref, v_ref, cos_ref, sin_ref, dm_ref, xi_ref, zt_ref, gd_ref, r_ref, sp_ref, st):
        n = pl.program_id(1)

        @pl.when(n == 0)
        def _():
            st[...] = jnp.zeros_like(st)

        cos, sin = cos_ref[...], sin_ref[...]
        qr = _rot(q_ref[...], cos, sin)
        kr = _rot(k_ref[...], cos, sin) * RSCALE
        qb, kb, vb = qr.astype(BF), kr.astype(BF), v_ref[...].astype(BF)
        sd = (_dg(qb, kb, NT) * dm_ref[0]).astype(BF)
        state = st[...]
        sp_ref[0, 0] = state
        qx = (qr * xi_ref[0]).astype(BF)
        r_ref[...] = _dot(sd, vb) + _dot(qx, state.astype(BF))
        kz = (kr * zt_ref[0]).astype(BF)
        st[...] = state * gd_ref[0] + _dg(kz, vb, TN)

    return pl.pallas_call(
        body, name="ret_fwd", grid=(RH, NCH),
        in_specs=_ret_specs(lambda n: n),
        out_specs=[pl.BlockSpec((C, RDV), lambda h, n: (n, h)),
                   pl.BlockSpec((1, 1, RDK, RDV), lambda h, n: (n, h, 0, 0))],
        out_shape=[jax.ShapeDtypeStruct((T, RH * RDV), F32), jax.ShapeDtypeStruct((NCH, RH, RDK, RDV), F32)],
        scratch_shapes=[pltpu.VMEM((RDK, RDV), F32)],
        compiler_params=_params(("parallel", "arbitrary")),
    )(z, z, z, cst["cos2"], cst["sin2"], cst["dmask"], cst["xi"], cst["zeta"], cst["gdec"])


def _ret_bwd(z, cst, sprev, dr):
    def body(q_ref, k_ref, v_ref, cos_ref, sin_ref, dm_ref, xi_ref, zt_ref, gd_ref, sp_ref, dr_ref,
             dq_ref, dk_ref, dv_ref, gst):
        i = pl.program_id(1)

        @pl.when(i == 0)
        def _():
            gst[...] = jnp.zeros_like(gst)

        cos, sin = cos_ref[...], sin_ref[...]
        dm, xi, zt = dm_ref[0], xi_ref[0], zt_ref[0]
        qr = _rot(q_ref[...], cos, sin)
        kr = _rot(k_ref[...], cos, sin) * RSCALE
        qb, kb, vb = qr.astype(BF), kr.astype(BF), v_ref[...].astype(BF)
        sd = (_dg(qb, kb, NT) * dm).astype(BF)
        qx = (qr * xi).astype(BF)
        kz = (kr * zt).astype(BF)
        drb = dr_ref[...]
        sb = sp_ref[0, 0].astype(BF)
        g = gst[...]
        gb = g.astype(BF)
        ds = (_dg(drb, vb, NT) * dm).astype(BF)
        dq = _dot(ds, kb) + _dg(drb, sb, NT) * xi
        dk = _dg(ds, qb, TN) + _dg(vb, gb, NT) * zt
        dv = _dg(sd, drb, TN) + _dot(kz, gb)
        gst[...] = g * gd_ref[0] + _dg(qx, drb, TN)
        dq_ref[...] = (dq * cos + pltpu.roll(dq * sin, 64, 1)).astype(BF)
        dkr = dk * RSCALE
        dk_ref[...] = (dkr * cos + pltpu.roll(dkr * sin, 64, 1)).astype(BF)
        dv_ref[...] = dv.astype(BF)

    rev = lambda n: NCH - 1 - n
    return pl.pallas_call(
        body, name="ret_bwd", grid=(RH, NCH),
        in_specs=_ret_specs(rev) + [
            pl.BlockSpec((1, 1, RDK, RDV), lambda h, n: (rev(n), h, 0, 0)),
            pl.BlockSpec((C, RDV), lambda h, n: (rev(n), h)),
        ],
        out_specs=[pl.BlockSpec((C, RDK), lambda h, n: (rev(n), h)),
                   pl.BlockSpec((C, RDK), lambda h, n: (rev(n), h)),
                   pl.BlockSpec((C, RDV), lambda h, n: (rev(n), h))],
        out_shape=[jax.ShapeDtypeStruct((T, RH * RDK), BF), jax.ShapeDtypeStruct((T, RH * RDK), BF),
                   jax.ShapeDtypeStruct((T, RH * RDV), BF)],
        scratch_shapes=[pltpu.VMEM((RDK, RDV), F32)],
        compiler_params=_params(("parallel", "arbitrary")),
    )(z, z, z, cst["cos2"], cst["sin2"], cst["dmask"], cst["xi"], cst["zeta"], cst["gdec"], sprev, dr)


def _log_sigmoid(x):
    return -(jnp.maximum(-x, 0.0) + jnp.log1p(jnp.exp(-jnp.abs(x))))


def _fox_prep(zf, bf_pad, cst):
    def body(zf_ref, b_ref, tri_ref, spread_ref, cb_ref, ct_ref, carry):
        n = pl.program_id(0)

        @pl.when(n == 0)
        def _():
            carry[...] = jnp.zeros_like(carry)

        ls = _log_sigmoid(zf_ref[...] + b_ref[...])
        row = n * C + lax.broadcasted_iota(jnp.int32, (C, C), 0)
        lf = jnp.where(row >= PAD, ls, 0.0)
        cc = jnp.dot(tri_ref[...], lf, precision=HI, preferred_element_type=F32) + carry[0:1, :]
        carry[...] = jnp.broadcast_to(cc[C - 1:C, :], carry.shape)
        cb_ref[...] = jnp.dot(cc, spread_ref[...], precision=HI, preferred_element_type=F32)
        pos = n * C + lax.broadcasted_iota(jnp.int32, (FH, C), 1)
        ct_ref[0] = jnp.where(pos >= PAD, cc.T[:FH, :], -NEG)

    return pl.pallas_call(
        body, name="fox_prep", grid=(NCH,),
        in_specs=[pl.BlockSpec((C, C), lambda n: (n, 0)), pl.BlockSpec((1, C), lambda n: (0, 0)),
                  pl.BlockSpec((C, C), lambda n: (0, 0)), pl.BlockSpec((C, FH * FD), lambda n: (0, 0))],
        out_specs=[pl.BlockSpec((C, FH * FD), lambda n: (_fox_pos(n), 0)),
                   pl.BlockSpec((1, FH, C), lambda n: (n, 0, 0))],
        out_shape=[jax.ShapeDtypeStruct((TROWS, FH * FD), F32), jax.ShapeDtypeStruct((NCH, FH, C), F32)],
        scratch_shapes=[pltpu.VMEM((8, C), F32)],
        compiler_params=_params(("arbitrary",)),
    )(zf, bf_pad, cst["tri"], cst["spread"])


def _lo_lanes(shape):
    return lax.broadcasted_iota(jnp.int32, shape, 1) < FD


def _split_heads(x):
    lo = _lo_lanes(x.shape)
    zero = jnp.zeros_like(x)
    return jnp.concatenate([jnp.where(lo, x, zero), jnp.where(lo, zero, x)], axis=0)


def _spread2(x):
    lo = _lo_lanes(x.shape)
    r = pltpu.roll(x, FD, 1)
    return jnp.concatenate([jnp.where(lo, x, r), jnp.where(lo, r, x)], axis=1)


NSTEP = (NCH + 1) // 2
NTILE = NCH + 1
TROWS = T + C


def _fox_tile(s, t):
    second = t > s
    j = jnp.where(second, t - s - 1, t)
    iq = jnp.where(second, NCH - 1 - s, s)
    kind = jnp.where(second & (s == NSTEP - 1), 2, (j == iq).astype(jnp.int32))
    return second.astype(jnp.int32), j, kind


def _fox_pos(i):
    return jnp.where(i < NSTEP, 2 * i, 2 * (NCH - 1 - i) + 1)


FOX_ORDER = [2 * i if i < NSTEP else 2 * (NCH - 1 - i) + 1 for i in range(NCH)]


def _fox_pair_specs():
    first = pl.BlockSpec((C, C), lambda p, s: (2 * s, p))
    second = pl.BlockSpec((C, C), lambda p, s: (jnp.where(s == NSTEP - 1, 2 * s, 2 * s + 1), p))
    both = pl.BlockSpec((2 * C, C), lambda p, s: (s, p))
    return first, second, both


def _fox_q_specs():
    return (pl.BlockSpec((C, C), lambda p, s: (s, QB_F + p)),
            pl.BlockSpec((C, C), lambda p, s: (NCH - 1 - s, QB_F + p)))


def _fox_key_bias(ct_ref, p, j):
    return jnp.concatenate([ct_ref[j, pl.ds(2 * p, 1), :], ct_ref[j, pl.ds(2 * p + 1, 1), :]], axis=1)


def _fox_fwd(z, cb, ct, cst):
    def body(qa_ref, qb_ref, k_ref, v_ref, ca_ref, cb_ref, ct_ref, ones_ref, mb_ref,
             a_ref, g_ref, kks, vvs, q2, ci2, m2, sbuf):
        p, s = pl.program_id(0), pl.program_id(1)

        @pl.when(s == 0)
        def _():
            ones = ones_ref[...]

            def prep(j, carry):
                rows = pl.ds(pl.multiple_of(j * C, C), C)
                kks[j] = _split_heads(k_ref[rows, :]).astype(BF)
                vvs[j] = jnp.concatenate([_split_heads(v_ref[rows, :]).astype(BF), ones], axis=1)
                return carry

            lax.fori_loop(0, NCH, prep, 0)

        for w, (q_ref, c_ref) in enumerate(((qa_ref, ca_ref), (qb_ref, cb_ref))):
            q2[w] = (q_ref[...] * FSCALE).astype(BF)
            ci2[w] = _spread2(c_ref[...])

        tiles = [_fox_tile(s, t) for t in range(NTILE)]
        neg = jnp.full((C, 2 * C), NEG, F32)
        mx = [neg, neg]
        for t, (sel, j, kind) in enumerate(tiles):
            st = _dg(q2[sel], kks[j], NT) + ((ci2[sel] - _fox_key_bias(ct_ref, p, j)) + mb_ref[kind])
            sbuf[t] = st
            mx = [jnp.maximum(mx[0], jnp.where(t <= s, st, neg)), jnp.maximum(mx[1], jnp.where(t <= s, neg, st))]
        for w in range(2):
            m2[w] = jnp.concatenate(
                [jnp.broadcast_to(jnp.max(mx[w][:, :C], axis=1, keepdims=True), (C, C)),
                 jnp.broadcast_to(jnp.max(mx[w][:, C:], axis=1, keepdims=True), (C, C))], axis=1)

        zero = jnp.zeros((C, 2 * C), F32)
        acc = [zero, zero]
        for t, (sel, j, _) in enumerate(tiles):
            part = _dot(jnp.exp(sbuf[t] - m2[sel]).astype(BF), vvs[j])
            acc = [acc[0] + jnp.where(t <= s, part, zero), acc[1] + jnp.where(t <= s, zero, part)]
        lo = _lo_lanes((C, C))
        for w, c_ref in enumerate((ca_ref, cb_ref)):
            res = acc[w]
            l = res[:, C:]
            a_ref[C * w:C * (w + 1), :] = res[:, :C] / l
            mw = m2[w]
            g_ref[C * w:C * (w + 1), :] = c_ref[...] - (jnp.where(lo, mw[:, :C], mw[:, C:]) + jnp.log(l))

    qa, qb = _fox_q_specs()
    ca, cbs, both = _fox_pair_specs()
    return pl.pallas_call(
        body, name="fox_fwd", grid=(NPAIR, NSTEP),
        in_specs=[qa, qb,
                  pl.BlockSpec((T, C), lambda p, s: (0, KB_F + p)),
                  pl.BlockSpec((T, C), lambda p, s: (0, VB_F + p)),
                  ca, cbs,
                  pl.BlockSpec((NCH, FH, C), lambda p, s: (0, 0, 0)),
                  pl.BlockSpec((2 * C, C), lambda p, s: (0, 0)),
                  pl.BlockSpec((3, C, 2 * C), lambda p, s: (0, 0, 0))],
        out_specs=[both, both],
        out_shape=[jax.ShapeDtypeStruct((TROWS, FH * FD), F32)] * 2,
        scratch_shapes=[pltpu.VMEM((NCH, 2 * C, C), BF), pltpu.VMEM((NCH, 2 * C, 2 * C), BF),
                        pltpu.VMEM((2, C, C), BF), pltpu.VMEM((2, C, 2 * C), F32), pltpu.VMEM((2, C, 2 * C), F32),
                        pltpu.VMEM((NTILE, C, 2 * C), F32)],
        compiler_params=_params(("parallel", "arbitrary")),
    )(z, z, z, z, cb, cb, ct, cst["ones_aug"], cst["mask_bias"])


def _fox_bwd(z, da, g, delta, ct, cst):
    grp = 9

    def body(qa_ref, qb_ref, daa_ref, dab_ref, ga_ref, gb_ref, dla_ref, dlb_ref, k_ref, v_ref, ct_ref, ones_ref,
             mb_ref, dq_ref, dr_ref, dk_ref, dv_ref, dcs_ref,
             kks, vvs, q2, qq2, dd2, da2, gi2, dl2, dq2, dvb, dkb, dkacc, dvacc, csacc):
        p, s = pl.program_id(0), pl.program_id(1)
        ones = ones_ref[...]

        @pl.when(s == 0)
        def _():
            dkacc[...] = jnp.zeros_like(dkacc)
            dvacc[...] = jnp.zeros_like(dvacc)
            csacc[...] = jnp.zeros_like(csacc)

            def prep(j, carry):
                rows = pl.ds(pl.multiple_of(j * C, C), C)
                kks[j] = _split_heads(k_ref[rows, :]).astype(BF)
                vvs[j] = _split_heads(v_ref[rows, :]).astype(BF)
                return carry

            lax.fori_loop(0, NCH, prep, 0)

        for w, (q_ref, d_ref, g_ref, l_ref) in enumerate(((qa_ref, daa_ref, ga_ref, dla_ref),
                                                          (qb_ref, dab_ref, gb_ref, dlb_ref))):
            qf = q_ref[...]
            q2[w] = (qf * FSCALE).astype(BF)
            qq2[w] = jnp.concatenate([_split_heads(qf).astype(BF), ones], axis=1)
            da2[w] = d_ref[...]
            dd2[w] = _split_heads(d_ref[...].astype(F32)).astype(BF)
            gi2[w] = _spread2(g_ref[...])
            dl2[w] = _spread2(l_ref[...])
        dq2[...] = jnp.zeros_like(dq2)
        zero = jnp.zeros((C, 2 * C), F32)

        def group(gi, carry):
            ts = [gi * grp + u for u in range(grp)]
            tiles = [_fox_tile(s, t) for t in ts]
            kk = [kks[j] for _, j, _ in tiles]
            ss = [_dg(q2[sel], kj, NT) + ((gi2[sel] - _fox_key_bias(ct_ref, p, j)) + mb_ref[kind])
                  for kj, (sel, j, kind) in zip(kk, tiles)]
            dps = [_dg(da2[sel], vvs[j], NT) for sel, j, _ in tiles]
            pes = [jnp.exp(st) for st in ss]
            dss = [pe * (dp - dl2[sel]) * FSCALE for pe, dp, (sel, _, _) in zip(pes, dps, tiles)]
            pts = [jnp.concatenate([pe[:, :C].T, pe[:, C:].T], axis=1).astype(BF) for pe in pes]
            dsts = [jnp.concatenate([ds[:, :C].T, ds[:, C:].T], axis=1).astype(BF) for ds in dss]
            dvs = [_dot(pt, dd2[sel]) for pt, (sel, _, _) in zip(pts, tiles)]
            rs = [_dot(dst, qq2[sel]) for dst, (sel, _, _) in zip(dsts, tiles)]
            parts = [_dot(ds.astype(BF), jnp.concatenate([kj, ones], axis=1)) for ds, kj in zip(dss, kk)]
            for t, dv, rr in zip(ts, dvs, rs):
                dvb[t] = dv
                dkb[t] = rr
            pa, pb = zero, zero
            for t, part in zip(ts, parts):
                pa = pa + jnp.where(t <= s, part, zero)
                pb = pb + jnp.where(t <= s, zero, part)
            dq2[0] += pa
            dq2[1] += pb
            return carry

        lax.fori_loop(0, NTILE // grp, group, 0)

        def scatter(t, carry):
            _, j, _ = _fox_tile(s, t)
            r = pl.ds(pl.multiple_of(j * C, C), C)
            dvacc[r, :] += dvb[t]
            dkacc[r, :] += dkb[t, :, :C]
            csacc[r, :] += dkb[t, :, C:]
            return carry

        lax.fori_loop(0, NTILE, scatter, 0)
        for w in range(2):
            res = dq2[w]
            dq_ref[C * w:C * (w + 1), :] = res[:, :C].astype(BF)
            dr_ref[C * w:C * (w + 1), :] = res[:, C:]

        @pl.when(s == NSTEP - 1)
        def _():
            dk_ref[...] = dkacc[...].astype(BF)
            dv_ref[...] = dvacc[...].astype(BF)
            dcs_ref[...] = csacc[...]

    qa, qb = _fox_q_specs()
    ba, bb, both = _fox_pair_specs()
    col = pl.BlockSpec((T, C), lambda p, s: (0, p))
    return pl.pallas_call(
        body, name="fox_bwd", grid=(NPAIR, NSTEP),
        in_specs=[qa, qb, ba, bb, ba, bb, ba, bb,
                  pl.BlockSpec((T, C), lambda p, s: (0, KB_F + p)),
                  pl.BlockSpec((T, C), lambda p, s: (0, VB_F + p)),
                  pl.BlockSpec((NCH, FH, C), lambda p, s: (0, 0, 0)),
                  pl.BlockSpec((2 * C, C), lambda p, s: (0, 0)),
                  pl.BlockSpec((3, C, 2 * C), lambda p, s: (0, 0, 0))],
        out_specs=[both, both, col, col, col],
        out_shape=[jax.ShapeDtypeStruct((TROWS, FH * FD), BF), jax.ShapeDtypeStruct((TROWS, FH * FD), F32),
                   jax.ShapeDtypeStruct((T, FH * FD), BF), jax.ShapeDtypeStruct((T, FH * FD), BF),
                   jax.ShapeDtypeStruct((T, FH * FD), F32)],
        scratch_shapes=[pltpu.VMEM((NCH, 2 * C, C), BF), pltpu.VMEM((NCH, 2 * C, C), BF),
                        pltpu.VMEM((2, C, C), BF), pltpu.VMEM((2, 2 * C, 2 * C), BF), pltpu.VMEM((2, 2 * C, C), BF),
                        pltpu.VMEM((2, C, C), BF), pltpu.VMEM((2, C, 2 * C), F32), pltpu.VMEM((2, C, 2 * C), F32),
                        pltpu.VMEM((2, C, 2 * C), F32),
                        pltpu.VMEM((NTILE, C, C), F32), pltpu.VMEM((NTILE, C, 2 * C), F32),
                        pltpu.VMEM((T, C), F32), pltpu.VMEM((T, C), F32), pltpu.VMEM((T, C), F32)],
        compiler_params=_params(("parallel", "arbitrary")),
    )(z, z, da, da, g, g, delta, delta, z, z, ct, cst["ones_aug"], cst["mask_bias"])


def _fox_gate_bwd(drow, dcol, zf, bf_pad, cst):
    def body(dr_ref, dc_ref, zf_ref, b_ref, tri_ref, pick_ref, dff_ref, db_ref, carry):
        s = pl.program_id(0)
        n = NCH - 1 - s

        @pl.when(s == 0)
        def _():
            carry[...] = jnp.zeros_like(carry)
            db_ref[...] = jnp.zeros_like(db_ref)

        dcb = jnp.dot((dr_ref[...] - dc_ref[...]) * (1.0 / FSCALE), pick_ref[...], precision=HI,
                      preferred_element_type=F32)
        suf = lax.dot_general(tri_ref[...], dcb, TN, precision=HI, preferred_element_type=F32) + carry[0:1, :]
        carry[...] = jnp.broadcast_to(suf[0:1, :], carry.shape)
        x = zf_ref[...] + b_ref[...]
        row = n * C + lax.broadcasted_iota(jnp.int32, (C, C), 0)
        dff = jnp.where(row >= PAD, suf * (1.0 - jax.nn.sigmoid(x)), 0.0)
        dff_ref[...] = dff.astype(BF)
        db_ref[...] += jnp.sum(dff, axis=0, keepdims=True)

    rev = lambda s: (NCH - 1 - s, 0)
    return pl.pallas_call(
        body, name="fox_gate_bwd", grid=(NCH,),
        in_specs=[pl.BlockSpec((C, FH * FD), lambda s: (_fox_pos(NCH - 1 - s), 0)),
                  pl.BlockSpec((C, FH * FD), rev), pl.BlockSpec((C, C), rev),
                  pl.BlockSpec((1, C), lambda s: (0, 0)), pl.BlockSpec((C, C), lambda s: (0, 0)),
                  pl.BlockSpec((FH * FD, C), lambda s: (0, 0))],
        out_specs=[pl.BlockSpec((C, C), rev), pl.BlockSpec((1, C), lambda s: (0, 0))],
        out_shape=[jax.ShapeDtypeStruct((T, C), BF), jax.ShapeDtypeStruct((1, C), F32)],
        scratch_shapes=[pltpu.VMEM((8, C), F32)],
        compiler_params=_params(("arbitrary",)),
    )(drow, dcol, zf, bf_pad, cst["tri"], cst["pick"])


def _gated(r, rg, a, fg):
    rn, rs = [], []
    for h in range(RH):
        rh = r[:, RDV * h:RDV * (h + 1)]
        s = lax.rsqrt(jnp.mean(rh * rh, axis=1, keepdims=True) + EPS)
        rn.append(rh * s)
        rs.append(s)
    rn = jnp.concatenate(rn, axis=1)
    y = jnp.concatenate([rn * (rg * jax.nn.sigmoid(rg)), a * (fg * jax.nn.sigmoid(fg))], axis=1)
    return y, rn, rs


def _out_loss(r, z, a, wout, x, tgt, fgain):
    def body(r_ref, rg_ref, a_ref, fg_ref, w_ref, x_ref, t_ref, g_ref, yt_ref, do_ref, dob_ref, loss_ref, dg_ref):
        i = pl.program_id(0)

        @pl.when(i == 0)
        def _():
            yt_ref[...] = jnp.zeros_like(yt_ref)
            do_ref[...] = jnp.zeros_like(do_ref)
            dob_ref[...] = jnp.zeros_like(dob_ref)
            loss_ref[...] = jnp.zeros_like(loss_ref)
            dg_ref[...] = jnp.zeros_like(dg_ref)

        @pl.when(i > 0)
        def _():
            y, _, _ = _gated(r_ref[...], rg_ref[...], a_ref[...], fg_ref[...])
            yt_ref[...] = y.T.astype(BF)
            o = x_ref[...] + _dot(y.astype(BF), w_ref[...])
            rs = lax.rsqrt(jnp.mean(o * o, axis=1, keepdims=True) + EPS)
            on = o * rs
            g = g_ref[...]
            e = on * g - t_ref[...]
            loss_ref[...] += 0.5 * jnp.sum(jnp.mean(e * e, axis=1, keepdims=True))
            dyh = e * (1.0 / D)
            dg_ref[...] += jnp.sum(dyh * on, axis=0, keepdims=True)
            don = dyh * g
            do = rs * (don - on * jnp.mean(don * on, axis=1, keepdims=True))
            do_ref[...] = do
            dob_ref[...] = do.astype(BF)

    tok = lambda i: (jnp.maximum(i - 1, 0), 0)
    return pl.pallas_call(
        body, name="out_loss", grid=(NCH,),
        in_specs=[pl.BlockSpec((C, D), lambda i: (i, 0)), pl.BlockSpec((C, D), lambda i: (i, GB_R)),
                  pl.BlockSpec((C, D), lambda i: (_fox_pos(i), 0)), pl.BlockSpec((C, D), lambda i: (i, GB_F)),
                  pl.BlockSpec((DMIX, D), lambda i: (0, 0)),
                  pl.BlockSpec((C, D), tok), pl.BlockSpec((C, D), tok), pl.BlockSpec((1, D), lambda i: (0, 0))],
        out_specs=[pl.BlockSpec((DMIX, C), lambda i: (0, i)), pl.BlockSpec((C, D), lambda i: (i, 0)),
                   pl.BlockSpec((C, D), lambda i: (i, 0)), pl.BlockSpec((8, C), lambda i: (0, 0)),
                   pl.BlockSpec((1, D), lambda i: (0, 0))],
        out_shape=[jax.ShapeDtypeStruct((DMIX, T), BF), jax.ShapeDtypeStruct((T, D), F32),
                   jax.ShapeDtypeStruct((T, D), BF), jax.ShapeDtypeStruct((8, C), F32),
                   jax.ShapeDtypeStruct((1, D), F32)],
        compiler_params=_params(("arbitrary",)),
    )(r, z, a, z, wout, x, tgt, fgain)


def _dsilu(x):
    s = jax.nn.sigmoid(x)
    return s * (1.0 + x * (1.0 - s))


def _dy_gate_bwd(dob, wout, r, z, a, seg):
    def body(do_ref, w_ref, r_ref, rg_ref, a_ref, fg_ref, seg_ref, dr_ref, da_ref, drg_ref, dfg_ref, dl_ref):
        dy = _dg(do_ref[...], w_ref[...], NT)
        rg, fg, a_ = rg_ref[...], fg_ref[...], a_ref[...]
        _, rn, rs = _gated(r_ref[...], rg, a_, fg)
        dyr, dyf = dy[:, :D], dy[:, D:]
        drn = dyr * (rg * jax.nn.sigmoid(rg))
        drg_ref[...] = (dyr * rn * _dsilu(rg)).astype(BF)
        for h in range(RH):
            sl = slice(RDV * h, RDV * (h + 1))
            dh, nh = drn[:, sl], rn[:, sl]
            dr_ref[:, sl] = (rs[h] * (dh - nh * jnp.mean(dh * nh, axis=1, keepdims=True))).astype(BF)
        dab = (dyf * (fg * jax.nn.sigmoid(fg))).astype(BF)
        da_ref[...] = dab
        dfg_ref[...] = (dyf * a_ * _dsilu(fg)).astype(BF)
        prod = dab.astype(F32) * a_
        segm = seg_ref[...]
        for p in range(NPAIR):
            sl = slice(C * p, C * (p + 1))
            hi = prod[:, sl].astype(BF)
            lo = (prod[:, sl] - hi.astype(F32)).astype(BF)
            dl_ref[:, sl] = _dot(hi, segm) + _dot(lo, segm)

    row = pl.BlockSpec((C, D), lambda i: (i, 0))
    fox = pl.BlockSpec((C, D), lambda i: (_fox_pos(i), 0))
    return pl.pallas_call(
        body, name="dy_gate_bwd", grid=(NCH,),
        in_specs=[row, pl.BlockSpec((DMIX, D), lambda i: (0, 0)),
                  row, pl.BlockSpec((C, D), lambda i: (i, GB_R)),
                  fox, pl.BlockSpec((C, D), lambda i: (i, GB_F)),
                  pl.BlockSpec((C, C), lambda i: (0, 0))],
        out_specs=[row, fox, row, row, fox],
        out_shape=[jax.ShapeDtypeStruct((T, D), BF), jax.ShapeDtypeStruct((TROWS, D), BF),
                   jax.ShapeDtypeStruct((T, D), BF), jax.ShapeDtypeStruct((T, D), BF),
                   jax.ShapeDtypeStruct((TROWS, D), F32)],
        compiler_params=_params(("parallel",)),
    )(dob, wout, r, z, a, z, seg)


DZ_WIDTHS = (512, 512, 1024, 1024, 1024, 1024, 1024, 1024)


def _du_norm_bwd(dzs, dzf, wt, wft, hpad, g, dopad):
    tm, tk = 544, 1024
    nk = WMAIN // tk

    def body(rq_ref, rk_ref, rv_ref, rg_ref, fq_ref, fk_ref, fv_ref, fg_ref, dzf_ref, w_ref, wf_ref, h_ref, g_ref,
             do_ref, gh_ref, dg_ref, acc):
        i, k = pl.program_id(0), pl.program_id(1)

        @pl.when(k == 0)
        def _():
            acc[...] = (_dot(dzf_ref[...], wf_ref[...]) + _dot(rq_ref[...], w_ref[:512, :])
                        + _dot(rk_ref[...], w_ref[512:, :]))

        for kk, piece in enumerate((rv_ref, rg_ref, fq_ref, fk_ref, fv_ref, fg_ref), start=1):
            @pl.when(k == kk)
            def _(piece=piece):
                acc[...] += _dot(piece[...], w_ref[...])

        @pl.when(k == nk - 1)
        def _():
            du = acc[...]
            h = h_ref[...]
            gg = g_ref[...]
            rs = lax.rsqrt(jnp.mean(h * h, axis=1, keepdims=True) + EPS)
            hn = h * rs
            part = jnp.sum(du * hn, axis=0, keepdims=True)

            @pl.when(i == 0)
            def _():
                dg_ref[...] = part

            @pl.when(i > 0)
            def _():
                dg_ref[...] += part

            dhn = du * gg
            gh_ref[...] = rs * (dhn - hn * jnp.mean(dhn * hn, axis=1, keepdims=True)) + do_ref[...]

    return pl.pallas_call(
        body, name="du_norm_bwd", grid=(T // tm, nk),
        in_specs=[pl.BlockSpec((tm, w), lambda i, k: (i, 0)) for w in DZ_WIDTHS]
        + [pl.BlockSpec((tm, C), lambda i, k: (i, 0)),
           pl.BlockSpec((tk, D), lambda i, k: (k, 0)), pl.BlockSpec((C, D), lambda i, k: (0, 0)),
           pl.BlockSpec((tm, D), lambda i, k: (i, 0)), pl.BlockSpec((1, D), lambda i, k: (0, 0)),
           pl.BlockSpec((tm, D), lambda i, k: (i, 0))],
        out_specs=[pl.BlockSpec((tm, D), lambda i, k: (i, 0)), pl.BlockSpec((1, D), lambda i, k: (0, 0))],
        out_shape=[jax.ShapeDtypeStruct((T, D), F32), jax.ShapeDtypeStruct((1, D), F32)],
        scratch_shapes=[pltpu.VMEM((tm, D), F32)],
        compiler_params=_params(("arbitrary", "arbitrary")),
    )(*dzs, dzf, wt, wft, hpad, g, dopad)


GROWS = 7424


def _dw_in(dzs, dzf, u):
    tn = 256
    nmain = WMAIN // tn
    first, blocks = [], []
    for w in DZ_WIDTHS:
        first.append(sum(blocks))
        blocks.append(w // tn)

    def body(rq_ref, rk_ref, rv_ref, rg_ref, fq_ref, fk_ref, fv_ref, fg_ref, dzf_ref, u_ref, o_ref):
        gidx = pl.program_id(0)
        ub = u_ref[...]
        for piece, g0, nb in zip((rq_ref, rk_ref, rv_ref, rg_ref, fq_ref, fk_ref, fv_ref, fg_ref), first, blocks):
            @pl.when((gidx >= g0) & (gidx < g0 + nb))
            def _(piece=piece):
                o_ref[...] = _dg(piece[...], ub, TN)

        @pl.when(gidx == nmain)
        def _():
            o_ref[:C, :] = _dg(dzf_ref[...], ub, TN)
            o_ref[C:, :] = jnp.zeros((tn - C, D), F32)

    def piece_spec(g0, nb):
        return pl.BlockSpec((T, tn), lambda gidx: (0, jnp.clip(gidx - g0, 0, nb - 1)))

    return pl.pallas_call(
        body, name="dw_in", grid=(nmain + 1,),
        in_specs=[piece_spec(g0, nb) for g0, nb in zip(first, blocks)]
        + [pl.BlockSpec((T, C), lambda gidx: (0, 0)), pl.BlockSpec((T, D), lambda gidx: (0, 0))],
        out_specs=pl.BlockSpec((tn, D), lambda gidx: (gidx, 0)),
        out_shape=jax.ShapeDtypeStruct((GROWS, D), F32),
        compiler_params=_params(("arbitrary",)),
    )(*dzs, dzf, u)


def _token_order(x_po):
    def body(i_ref, o_ref):
        o_ref[...] = i_ref[...]

    return pl.pallas_call(
        body, name="token_order", grid=(NCH,),
        in_specs=[pl.BlockSpec((C, D), lambda i: (_fox_pos(i), 0))],
        out_specs=pl.BlockSpec((C, D), lambda i: (i, 0)),
        out_shape=jax.ShapeDtypeStruct((T, D), x_po.dtype),
        compiler_params=_params(("parallel",)),
    )(x_po)


def _local_step(x, tgt, meta, norm_g, wt, wft, b_f, wout, final_g):
    cst = _constants()
    hpad = jnp.concatenate([jnp.pad(meta, ((PAD, 0), (0, 0))), x], axis=0)
    bf_pad = jnp.pad(b_f, ((0, 0), (0, C - NFF)))
    u = _norm_in(hpad, norm_g)
    z = _mm_nt(u, wt, WMAIN, T // 2, 512, "in_proj")
    zf = _mm_nt(u, wft, C, T // 2, C, "in_proj_ff")
    r, sprev = _ret_fwd(z, cst)
    cb, ct = _fox_prep(zf, bf_pad, cst)
    a, g = _fox_fwd(z, cb, ct, cst)
    yt, dopad, dob, loss8, dfg = _out_loss(r, z, a, wout, x, tgt, final_g)
    dr, da, dzrg, dzfg, delta = _dy_gate_bwd(dob, wout, r, z, a, cst["seg"])
    dwout = _mm_nn(yt, dob, 512, D, "dw_out")
    dzq_r, dzk_r, dzv_r = _ret_bwd(z, cst, sprev, dr)
    dq_po, drow, dzk_f, dzv_f, dcol = _fox_bwd(z, da, g, delta, ct, cst)
    dzf, dbf = _fox_gate_bwd(drow, dcol, zf, bf_pad, cst)
    dzs = [dzq_r, dzk_r, dzv_r, dzrg, _token_order(dq_po), dzk_f, dzv_f, dzfg]
    gwt = _dw_in(dzs, dzf, u)
    gh, dng = _du_norm_bwd(dzs, dzf, wt, wft, hpad, norm_g, dopad)
    return (loss8[0, 0], gh[C:], gh[PAD:C], dng, gwt, dbf[:, :NFF], dwout, dfg)


def _place():
    x, y, c = lax.axis_index("x"), lax.axis_index("y"), lax.axis_index("c")
    return x, y, c


def _other_chips(x, y):
    return [(1 - x, y, 2 * (1 - x) + y), (x, 1 - y, 2 * x + (1 - y)), (1 - x, 1 - y, 2 * (1 - x) + (1 - y))]


def _all_gather_shards(shards):
    n = len(shards)

    def body(*refs):
        ins, outs = refs[:n], refs[n:2 * n]
        send_sems, recv_sems = refs[2 * n:]
        x, y, c = _place()
        me_s = 2 * x + y
        sib = (x, y, 1 - c)
        chips = _other_chips(x, y)
        sends, waits = [], []
        for a in range(n):
            rows = ins[a].shape[0] // 2
            half = pl.ds(c * rows, rows)
            for k, (cx, cy, cs) in enumerate(chips):
                sends.append(pltpu.make_async_remote_copy(
                    src_ref=ins[a].at[half], dst_ref=outs[a].at[me_s, half],
                    send_sem=send_sems.at[6 * a + k], recv_sem=recv_sems.at[6 * a + k],
                    device_id=(cx, cy, c), device_id_type=MESH))
                sends[-1].start()
        for a in range(n):
            rows = ins[a].shape[0] // 2
            half = pl.ds(c * rows, rows)
            other = pl.ds((1 - c) * rows, rows)
            for k, (cx, cy, cs) in enumerate(chips):
                pltpu.make_async_remote_copy(
                    src_ref=outs[a].at[cs, half], dst_ref=outs[a].at[cs, half],
                    send_sem=send_sems.at[6 * a + k], recv_sem=recv_sems.at[6 * a + k],
                    device_id=(cx, cy, c), device_id_type=MESH).wait_recv()
                fwd = pltpu.make_async_remote_copy(
                    src_ref=outs[a].at[cs, half], dst_ref=outs[a].at[cs, half],
                    send_sem=send_sems.at[6 * a + 3 + k], recv_sem=recv_sems.at[6 * a + 3 + k],
                    device_id=sib, device_id_type=MESH)
                fwd.start()
                sends.append(fwd)
                waits.append(pltpu.make_async_remote_copy(
                    src_ref=outs[a].at[cs, other], dst_ref=outs[a].at[cs, other],
                    send_sem=send_sems.at[6 * a + 3 + k], recv_sem=recv_sems.at[6 * a + 3 + k],
                    device_id=sib, device_id_type=MESH))
        for w in waits:
            w.wait_recv()
        for s in sends:
            s.wait_send()

    return pl.pallas_call(
        body, name="all_gather_w",
        in_specs=[ANY] * n, out_specs=[ANY] * n,
        out_shape=[jax.ShapeDtypeStruct((4,) + s.shape, s.dtype) for s in shards],
        scratch_shapes=[pltpu.SemaphoreType.DMA((6 * n,)), pltpu.SemaphoreType.DMA((6 * n,))],
    )(*shards)


WOFF, WLEN = 1792, 2048
WHALF = WLEN // 2


def _pair_swap(gwt, arrs, small):
    n = len(arrs)

    def body(*refs):
        gw, ins, sm = refs[0], refs[1:n + 1], refs[n + 1]
        gwo, outs, smo = refs[n + 2], refs[n + 3:2 * n + 3], refs[2 * n + 3]
        send_sems, recv_sems = refs[2 * n + 4:]
        x, y, c = _place()
        sib = (x, y, 1 - c)
        cps = []
        for k in range(4):
            cps.append(pltpu.make_async_remote_copy(
                src_ref=gw.at[pl.ds(WOFF * k + (1 - c) * WHALF, WHALF)], dst_ref=gwo.at[k],
                send_sem=send_sems.at[k], recv_sem=recv_sems.at[k], device_id=sib, device_id_type=MESH))
        for a in range(n):
            rows = ins[a].shape[1] // 2
            cps.append(pltpu.make_async_remote_copy(
                src_ref=ins[a].at[:, pl.ds((1 - c) * rows, rows)], dst_ref=outs[a],
                send_sem=send_sems.at[4 + a], recv_sem=recv_sems.at[4 + a], device_id=sib, device_id_type=MESH))
        cps.append(pltpu.make_async_remote_copy(
            src_ref=sm, dst_ref=smo, send_sem=send_sems.at[4 + n], recv_sem=recv_sems.at[4 + n],
            device_id=sib, device_id_type=MESH))
        for cp in cps:
            cp.start()
        for cp in cps:
            cp.wait()

    return pl.pallas_call(
        body, name="rs_pair_swap",
        in_specs=[ANY] * (n + 2), out_specs=[ANY] * (n + 2),
        out_shape=[jax.ShapeDtypeStruct((4, WHALF, D), gwt.dtype)]
        + [jax.ShapeDtypeStruct((4, a.shape[1] // 2, a.shape[2]), a.dtype) for a in arrs]
        + [jax.ShapeDtypeStruct(small.shape, small.dtype)],
        scratch_shapes=[pltpu.SemaphoreType.DMA((n + 5,)), pltpu.SemaphoreType.DMA((n + 5,))],
    )(gwt, *arrs, small)


def _add_windows(gwt, recv):
    tb = 256
    nb = WHALF // tb
    c = lax.axis_index("c")

    def body(c_ref, a_ref, b_ref, o_ref):
        o_ref[0] = (a_ref[...] + b_ref[0]).astype(BF)

    return pl.pallas_call(
        body, name="pair_add_in",
        grid_spec=pltpu.PrefetchScalarGridSpec(
            num_scalar_prefetch=1, grid=(4, nb),
            in_specs=[pl.BlockSpec((tb, D), lambda k, i, cr: ((WOFF // tb) * k + nb * cr[0] + i, 0)),
                      pl.BlockSpec((1, tb, D), lambda k, i, cr: (k, i, 0))],
            out_specs=pl.BlockSpec((1, tb, D), lambda k, i, cr: (k, i, 0))),
        out_shape=jax.ShapeDtypeStruct(recv.shape, BF),
        compiler_params=_params(("parallel", "parallel")),
    )(jnp.reshape(c, (1,)).astype(jnp.int32), gwt, recv)


def _chip_exchange(parts, small):
    n = len(parts)

    def body(*refs):
        ins, sm = refs[:n], refs[n]
        outs, smo = refs[n + 1:2 * n + 1], refs[2 * n + 1]
        send_sems, recv_sems = refs[2 * n + 2:]
        x, y, c = _place()
        me_s = 2 * x + y
        chips = _other_chips(x, y)
        cps = []
        for a in range(n + 1):
            src = ins[a] if a < n else sm
            dst = outs[a] if a < n else smo
            for k, (cx, cy, cs) in enumerate(chips):
                cps.append(pltpu.make_async_remote_copy(
                    src_ref=src.at[cs] if a < n else src, dst_ref=dst.at[me_s],
                    send_sem=send_sems.at[3 * a + k], recv_sem=recv_sems.at[3 * a + k],
                    device_id=(cx, cy, c), device_id_type=MESH))
        for cp in cps:
            cp.start()
        for cp in cps:
            cp.wait()

    return pl.pallas_call(
        body, name="rs_chip_exchange",
        in_specs=[ANY] * (n + 1), out_specs=[ANY] * (n + 1),
        out_shape=[jax.ShapeDtypeStruct(p.shape, p.dtype) for p in parts]
        + [jax.ShapeDtypeStruct((4,) + small.shape, small.dtype)],
        scratch_shapes=[pltpu.SemaphoreType.DMA((3 * (n + 1),)), pltpu.SemaphoreType.DMA((3 * (n + 1),))],
    )(*parts, small)


def _pair_send(halves):
    n = len(halves)

    def body(*refs):
        ins, outs = refs[:n], refs[n:2 * n]
        send_sems, recv_sems = refs[2 * n:]
        x, y, c = _place()
        cps = [pltpu.make_async_remote_copy(
            src_ref=ins[a], dst_ref=outs[a], send_sem=send_sems.at[a], recv_sem=recv_sems.at[a],
            device_id=(x, y, 1 - c), device_id_type=MESH) for a in range(n)]
        for cp in cps:
            cp.start()
        for cp in cps:
            cp.wait()

    return pl.pallas_call(
        body, name="rs_pair_send",
        in_specs=[ANY] * n, out_specs=[ANY] * n,
        out_shape=[jax.ShapeDtypeStruct(h.shape, h.dtype) for h in halves],
        scratch_shapes=[pltpu.SemaphoreType.DMA((n,)), pltpu.SemaphoreType.DMA((n,))],
    )(*halves)


def _row_block(rows):
    for tb in (256, 128, 64, 32, 16, 8):
        if rows % tb == 0:
            return tb
    return rows


def _add_halves(full, recv, name, out_dtype):
    _, r2, w = recv.shape
    tb = _row_block(r2)
    nb = r2 // tb
    c = lax.axis_index("c")

    def body(c_ref, a_ref, b_ref, o_ref):
        o_ref[...] = (a_ref[...] + b_ref[...]).astype(o_ref.dtype)

    return pl.pallas_call(
        body, name=name,
        grid_spec=pltpu.PrefetchScalarGridSpec(
            num_scalar_prefetch=1, grid=(4, nb),
            in_specs=[pl.BlockSpec((1, tb, w), lambda s, i, cr: (s, cr[0] * nb + i, 0)),
                      pl.BlockSpec((1, tb, w), lambda s, i, cr: (s, i, 0))],
            out_specs=pl.BlockSpec((1, tb, w), lambda s, i, cr: (s, i, 0))),
        out_shape=jax.ShapeDtypeStruct(recv.shape, out_dtype),
        compiler_params=_params(("parallel", "parallel")),
    )(jnp.reshape(c, (1,)).astype(jnp.int32), full, recv)


def _add2(a, b, name):
    def body(a_ref, b_ref, o_ref):
        o_ref[...] = a_ref[...] + b_ref[...]

    return pl.pallas_call(body, name=name, out_shape=jax.ShapeDtypeStruct(a.shape, a.dtype))(a, b)


def _sum4(buf, own, name):
    _, r, w = buf.shape
    tb = _row_block(r)
    me_s = 2 * lax.axis_index("x") + lax.axis_index("y")
    by_dest = own.ndim == 3

    def body(s_ref, b_ref, own_ref, o_ref):
        mine = (own_ref[0] if by_dest else own_ref[...]).astype(F32)
        terms = [jnp.where(s_ref[0] == t, mine, b_ref[t].astype(F32)) for t in range(4)]
        o_ref[...] = ((terms[0] + terms[1]) + terms[2]) + terms[3]

    own_spec = (pl.BlockSpec((1, tb, w), lambda i, sr: (sr[0], i, 0)) if by_dest
                else pl.BlockSpec((tb, w), lambda i, sr: (i, 0)))
    return pl.pallas_call(
        body, name=name,
        grid_spec=pltpu.PrefetchScalarGridSpec(
            num_scalar_prefetch=1, grid=(r // tb,),
            in_specs=[pl.BlockSpec((4, tb, w), lambda i, sr: (0, i, 0)), own_spec],
            out_specs=pl.BlockSpec((tb, w), lambda i, sr: (i, 0))),
        out_shape=jax.ShapeDtypeStruct((r, w), F32),
        compiler_params=_params(("parallel",)),
    )(jnp.reshape(me_s, (1,)).astype(jnp.int32), buf, own)


def _adamw_math(w, g, m, v):
    mn = B1 * m + (1.0 - B1) * g
    vn = B2 * v + (1.0 - B2) * (g * g)
    m_hat = mn / (1.0 - B1 ** STEP)
    v_hat = vn / (1.0 - B2 ** STEP)
    return -LR * (m_hat / (jnp.sqrt(v_hat) + AEPS) + WD * w), mn, vn


def _adamw(w, g, m, v, name):
    r, c_ = w.shape
    tb = _row_block(r)
    if tb == r and r > 512:
        tb = 256

    def body(w_ref, g_ref, m_ref, v_ref, d_ref, mo_ref, vo_ref):
        d_ref[...], mo_ref[...], vo_ref[...] = _adamw_math(w_ref[...], g_ref[...], m_ref[...], v_ref[...])

    spec = pl.BlockSpec((tb, c_), lambda i: (i, 0))
    return pl.pallas_call(
        body, name=name, grid=(pl.cdiv(r, tb),),
        in_specs=[spec] * 4, out_specs=[spec] * 3,
        out_shape=[jax.ShapeDtypeStruct(w.shape, F32)] * 3,
        compiler_params=_params(("parallel",)),
    )(w, g, m, v)


def _adamw_halves(w, g_mine, g_sib, m, v, name):
    r, c_ = w.shape
    r2 = g_mine.shape[0]
    tb = _row_block(r2)
    nb = r2 // tb
    c = lax.axis_index("c")

    def body(c_ref, w_ref, gm_ref, gs_ref, m_ref, v_ref, g_ref, d_ref, mo_ref, vo_ref):
        g = jnp.where(pl.program_id(0) == c_ref[0], gm_ref[...], gs_ref[...])
        g_ref[...] = g
        d_ref[...], mo_ref[...], vo_ref[...] = _adamw_math(w_ref[...], g, m_ref[...], v_ref[...])

    full = pl.BlockSpec((tb, c_), lambda h, i, cr: (h * nb + i, 0))
    half = pl.BlockSpec((tb, c_), lambda h, i, cr: (i, 0))
    return pl.pallas_call(
        body, name=name,
        grid_spec=pltpu.PrefetchScalarGridSpec(
            num_scalar_prefetch=1, grid=(2, nb),
            in_specs=[full, half, half, full, full], out_specs=[full] * 4),
        out_shape=[jax.ShapeDtypeStruct(w.shape, F32)] * 4,
        compiler_params=_params(("parallel", "parallel")),
    )(jnp.reshape(c, (1,)).astype(jnp.int32), w, g_mine, g_sib, m, v)


def kernel(x, meta_tokens, norm_g, w_in, b_f, w_out, final_g, loss_target, m_meta_tokens, m_norm_g, m_w_in, m_b_f, m_w_out, m_final_g, v_meta_tokens, v_norm_g, v_w_in, v_b_f, v_w_out, v_final_g):
    me_s = 2 * lax.axis_index("x") + lax.axis_index("y")
    core = lax.axis_index("c")
    wt, mt, vt = [jnp.swapaxes(t[0], 0, 1) for t in (w_in, m_w_in, v_w_in)]

    own = [jnp.pad(wt.astype(BF), ((0, WPADROWS - WSH), (0, 0))), w_out[0].astype(BF), meta_tokens]
    gathered = _all_gather_shards(own)
    gin, gout, gmeta = [[jnp.where(me_s == s, o, g[s]) for s in range(4)] for o, g in zip(own, gathered)]
    wt_full = jnp.concatenate([t[:WSH] for t in gin], axis=0)
    wft = jnp.pad(wt_full[WMAIN:], ((0, C - NFF), (0, 0)))
    wout = jnp.concatenate(gout, axis=0)
    meta = jnp.concatenate(gmeta, axis=1)

    loss, gx, dmeta, dng, gwt, dbf, dwout, dfg = _local_step(
        x[0], loss_target[0], meta, norm_g, wt_full, wft, b_f, wout, final_g.reshape(1, D))

    g_out = dwout.reshape(4, DMIX // 4, D)
    g_meta = jnp.stack([dmeta[:, 256 * s:256 * (s + 1)] for s in range(4)])
    small = jnp.concatenate([dng, dfg, jnp.pad(dbf, ((0, 0), (0, D - NFF))),
                             jnp.pad(jnp.reshape(loss, (1, 1)), ((0, 0), (0, D - 1))),
                             jnp.zeros((4, D), F32)], axis=0)
    r_in, r_out, r_meta, r_small = _pair_swap(gwt, [g_out, g_meta], small)
    p_in = _add_windows(gwt, r_in)
    p_out = _add_halves(g_out, r_out, "pair_add_out", BF)
    p_meta = _add_halves(g_meta, r_meta, "pair_add_meta", F32)
    p_small = _add2(small, r_small, "pair_add_small")
    e_in, e_out, e_meta, e_small = _chip_exchange([p_in, p_out, p_meta], p_small)
    h_in, h_out, h_meta = _sum4(e_in, p_in, "sum_in"), _sum4(e_out, p_out, "sum_out"), _sum4(e_meta, p_meta, "sum_meta")
    tot = _sum4(e_small, p_small, "sum_small")
    s_in, s_out, s_meta = _pair_send([h_in, h_out, h_meta])
    g_norm, g_final, g_bf, loss_all = tot[0:1], tot[1], tot[2:3, :NFF], tot[3, 0]

    gw_meta, d_meta, nm_meta, nv_meta = _adamw_halves(meta_tokens, h_meta, s_meta, m_meta_tokens, v_meta_tokens,
                                                      "adamw_meta")
    d_norm, nm_norm, nv_norm = _adamw(norm_g, g_norm, m_norm_g, v_norm_g, "adamw_norm")
    window = jnp.concatenate([jnp.where(core == 0, h_in, s_in), jnp.where(core == 0, s_in, h_in)], axis=0)
    gwt_own = lax.dynamic_slice(window, (4 * me_s, 0), (WSH, D))
    d_in, nm_in, nv_in = _adamw(wt, gwt_own, mt, vt, "adamw_in")
    gw_in, d_in, nm_in, nv_in = [jnp.swapaxes(t, 0, 1)[None] for t in (gwt_own, d_in, nm_in, nv_in)]
    d_bf, nm_bf, nv_bf = _adamw(b_f, g_bf, m_b_f, v_b_f, "adamw_bf")
    gw_out, d_out, nm_out, nv_out = _adamw_halves(w_out[0], h_out, s_out, m_w_out[0], v_w_out[0], "adamw_out")
    d_fin, nm_fin, nv_fin = _adamw(final_g.reshape(1, D), g_final.reshape(1, D), m_final_g.reshape(1, D),
                                   v_final_g.reshape(1, D), "adamw_final")
    return (loss_all, gx[None], gw_meta, g_norm, gw_in, g_bf, gw_out[None], g_final,
            d_meta, d_norm, d_in, d_bf, d_out[None], d_fin.reshape(D),
            nm_meta, nm_norm, nm_in, nm_bf, nm_out[None], nm_fin.reshape(D),
            nv_meta, nv_norm, nv_in, nv_bf, nv_out[None], nv_fin.reshape(D))
```

```python
import numpy as np
import jax
import jax.numpy as jnp
from jax import lax
from jax.experimental import pallas as pl
from jax.experimental.pallas import tpu as pltpu

D = 1024
SEQ = 2048
NMETA = 16
C = 128
PAD = C - NMETA
T = PAD + NMETA + SEQ
NCH = T // C
RH, RDK, RDV = 4, 128, 256
FH, FD = 16, 64
NPAIR = FH // 2
WMAIN = 7168
NFF = 16
WIN = WMAIN + NFF
WSH = WIN // 4
WPADROWS = 1824
DMIX = 2048
EPS = 1e-6
NEG = -1e30
RSCALE = RDK ** -0.5
FSCALE = FD ** -0.5
ROPE_BASE = 10000.0
LR, B1, B2, AEPS, WD, STEP = 0.001, 0.9, 0.999, 1e-08, 0.01, 10

BF = jnp.bfloat16
F32 = jnp.float32
NT = (((1,), (1,)), ((), ()))
TN = (((0,), (0,)), ((), ()))
HI = lax.Precision.HIGHEST
MESH = pl.DeviceIdType.MESH
ANY = pl.BlockSpec(memory_space=pl.ANY)
VMEM_LIMIT = 48 * 1024 * 1024

QB_R, KB_R = 0, 4
VB_R = 4
GB_R, GB_F = 2, 6
QB_F, KB_F, VB_F = 24, 32, 40


def _dot(a, b):
    return jnp.dot(a, b, preferred_element_type=F32)


def _dg(a, b, dims):
    return lax.dot_general(a, b, dims, preferred_element_type=F32)


def _params(sem=None):
    return pltpu.CompilerParams(dimension_semantics=sem, vmem_limit_bytes=VMEM_LIMIT)


def _constants():
    pos = jnp.arange(T, dtype=F32) - PAD
    inv = ROPE_BASE ** (-jnp.arange(0, RDK, 2, dtype=F32) / RDK)
    ang = pos[:, None] * inv[None, :]
    cos, sin = jnp.cos(ang), jnp.sin(ang)
    cos2 = jnp.concatenate([cos, cos], axis=1)
    sin2 = jnp.concatenate([-sin, sin], axis=1)
    log_gamma = jnp.log1p(-jnp.exp2(-5.0 - jnp.arange(RH, dtype=F32)))
    idx = jnp.arange(C, dtype=F32)
    diff = idx[:, None] - idx[None, :]
    dmask = jnp.where(diff[None] >= 0, jnp.exp(log_gamma[:, None, None] * jnp.maximum(diff, 0.0)[None]), 0.0)
    zeta = jnp.exp(log_gamma[:, None] * (C - 1.0 - idx)[None, :])
    xi = jnp.exp(log_gamma[:, None] * (idx + 1.0)[None, :])
    gdec = jnp.exp(log_gamma * C)
    zeta_b = jnp.broadcast_to(zeta[:, :, None], (RH, C, RDK))
    xi_b = jnp.broadcast_to(xi[:, :, None], (RH, C, RDK))
    gdec_b = jnp.broadcast_to(gdec[:, None, None], (RH, RDK, RDV))
    tri = jnp.asarray(np.tril(np.ones((C, C), np.float32)))
    head_of_lane = np.arange(FH * FD) // FD
    spread = (np.arange(C)[:, None] == head_of_lane[None, :]).astype(np.float32)
    pick = ((np.arange(FH * FD)[:, None] % FD == 0)
            & (head_of_lane[:, None] == np.arange(C)[None, :])).astype(np.float32)
    seg = (np.arange(C)[:, None] // FD == np.arange(C)[None, :] // FD).astype(np.float32)
    ones_aug = np.concatenate([np.tile((np.arange(C) < FD)[None, :], (C, 1)),
                               np.tile((np.arange(C) >= FD)[None, :], (C, 1))], axis=0).astype(np.float32)
    lane = np.arange(2 * C) % C
    causal = np.where(lane[None, :] <= np.arange(C)[:, None], 0.0, NEG).astype(np.float32)
    mask_bias = np.stack([np.zeros((C, 2 * C), np.float32), causal, np.full((C, 2 * C), NEG, np.float32)])
    return dict(cos2=cos2, sin2=sin2, dmask=dmask, zeta=zeta_b, xi=xi_b, gdec=gdec_b, tri=tri,
                mask_bias=jnp.asarray(mask_bias),
                spread=jnp.asarray(spread), pick=jnp.asarray(pick), seg=jnp.asarray(seg, dtype=BF),
                ones_aug=jnp.asarray(ones_aug, dtype=BF))


def _norm_in(hpad, g):
    def body(h_ref, g_ref, u_ref, ut_ref):
        h = h_ref[...]
        rs = lax.rsqrt(jnp.mean(h * h, axis=1, keepdims=True) + EPS)
        u = h * rs * g_ref[...]
        u_ref[...] = u.astype(BF)
        ut_ref[...] = u.T.astype(BF)

    return pl.pallas_call(
        body, name="norm_in", grid=(NCH,),
        in_specs=[pl.BlockSpec((C, D), lambda i: (i, 0)), pl.BlockSpec((1, D), lambda i: (0, 0))],
        out_specs=[pl.BlockSpec((C, D), lambda i: (i, 0)), pl.BlockSpec((D, C), lambda i: (0, i))],
        out_shape=[jax.ShapeDtypeStruct((T, D), BF), jax.ShapeDtypeStruct((D, T), BF)],
        compiler_params=_params(("parallel",)),
    )(hpad, g)


def _mm_nt(a, b, n, tm, tn, name):
    m, k = a.shape

    def body(a_ref, b_ref, o_ref):
        o_ref[...] = _dg(a_ref[...], b_ref[...], NT)

    return pl.pallas_call(
        body, name=name, grid=(m // tm, n // tn),
        in_specs=[pl.BlockSpec((tm, k), lambda i, j: (i, 0)), pl.BlockSpec((tn, k), lambda i, j: (j, 0))],
        out_specs=pl.BlockSpec((tm, tn), lambda i, j: (i, j)),
        out_shape=jax.ShapeDtypeStruct((m, n), F32),
        compiler_params=_params(("parallel", "parallel")),
    )(a, b)


def _mm_nn(a, b, tm, tn, name):
    m, k = a.shape
    _, n = b.shape

    def body(a_ref, b_ref, o_ref):
        o_ref[...] = _dot(a_ref[...], b_ref[...])

    return pl.pallas_call(
        body, name=name, grid=(m // tm, n // tn),
        in_specs=[pl.BlockSpec((tm, k), lambda i, j: (i, 0)), pl.BlockSpec((k, tn), lambda i, j: (0, j))],
        out_specs=pl.BlockSpec((tm, tn), lambda i, j: (i, j)),
        out_shape=jax.ShapeDtypeStruct((m, n), F32),
        compiler_params=_params(("parallel", "parallel")),
    )(a, b)


def _rot(x, cos2, sin2):
    return x * cos2 + pltpu.roll(x, 64, 1) * sin2


def _ret_specs(chunk):
    return [
        pl.BlockSpec((C, RDK), lambda h, n: (chunk(n), QB_R + h)),
        pl.BlockSpec((C, RDK), lambda h, n: (chunk(n), KB_R + h)),
        pl.BlockSpec((C, RDV), lambda h, n: (chunk(n), VB_R + h)),
        pl.BlockSpec((C, RDK), lambda h, n: (chunk(n), 0)),
        pl.BlockSpec((C, RDK), lambda h, n: (chunk(n), 0)),
        pl.BlockSpec((1, C, C), lambda h, n: (h, 0, 0)),
        pl.BlockSpec((1, C, RDK), lambda h, n: (h, 0, 0)),
        pl.BlockSpec((1, C, RDK), lambda h, n: (h, 0, 0)),
        pl.BlockSpec((1, RDK, RDV), lambda h, n: (h, 0, 0)),
    ]


def _ret_fwd(z, cst):
    def body(q_ref, k_ref, v_ref, cos_ref, sin_ref, dm_ref, xi_ref, zt_ref, gd_ref, r_ref, sp_ref, st):
        n = pl.program_id(1)

        @pl.when(n == 0)
        def _():
            st[...] = jnp.zeros_like(st)

        cos, sin = cos_ref[...], sin_ref[...]
        qr = _rot(q_ref[...], cos, sin)
        kr = _rot(k_ref[...], cos, sin) * RSCALE
        qb, kb, vb = qr.astype(BF), kr.astype(BF), v_ref[...].astype(BF)
        sd = (_dg(qb, kb, NT) * dm_ref[0]).astype(BF)
        state = st[...]
        sp_ref[0, 0] = state
        qx = (qr * xi_ref[0]).astype(BF)
        r_ref[...] = _dot(sd, vb) + _dot(qx, state.astype(BF))
        kz = (kr * zt_ref[0]).astype(BF)
        st[...] = state * gd_ref[0] + _dg(kz, vb, TN)

    return pl.pallas_call(
        body, name="ret_fwd", grid=(RH, NCH),
        in_specs=_ret_specs(lambda n: n),
        out_specs=[pl.BlockSpec((C, RDV), lambda h, n: (n, h)),
                   pl.BlockSpec((1, 1, RDK, RDV), lambda h, n: (n, h, 0, 0))],
        out_shape=[jax.ShapeDtypeStruct((T, RH * RDV), F32), jax.ShapeDtypeStruct((NCH, RH, RDK, RDV), F32)],
        scratch_shapes=[pltpu.VMEM((RDK, RDV), F32)],
        compiler_params=_params(("parallel", "arbitrary")),
    )(z, z, z, cst["cos2"], cst["sin2"], cst["dmask"], cst["xi"], cst["zeta"], cst["gdec"])


def _ret_bwd(z, cst, sprev, dr):
    def body(q_ref, k_ref, v_ref, cos_ref, sin_ref, dm_ref, xi_ref, zt_ref, gd_ref, sp_ref, dr_ref,
             dq_ref, dk_ref, dv_ref, gst):
        i = pl.program_id(1)

        @pl.when(i == 0)
        def _():
            gst[...] = jnp.zeros_like(gst)

        cos, sin = cos_ref[...], sin_ref[...]
        dm, xi, zt = dm_ref[0], xi_ref[0], zt_ref[0]
        qr = _rot(q_ref[...], cos, sin)
        kr = _rot(k_ref[...], cos, sin) * RSCALE
        qb, kb, vb = qr.astype(BF), kr.astype(BF), v_ref[...].astype(BF)
        sd = (_dg(qb, kb, NT) * dm).astype(BF)
        qx = (qr * xi).astype(BF)
        kz = (kr * zt).astype(BF)
        drb = dr_ref[...]
        sb = sp_ref[0, 0].astype(BF)
        g = gst[...]
        gb = g.astype(BF)
        ds = (_dg(drb, vb, NT) * dm).astype(BF)
        dq = _dot(ds, kb) + _dg(drb, sb, NT) * xi
        dk = _dg(ds, qb, TN) + _dg(vb, gb, NT) * zt
        dv = _dg(sd, drb, TN) + _dot(kz, gb)
        gst[...] = g * gd_ref[0] + _dg(qx, drb, TN)
        dq_ref[...] = (dq * cos + pltpu.roll(dq * sin, 64, 1)).astype(BF)
        dkr = dk * RSCALE
        dk_ref[...] = (dkr * cos + pltpu.roll(dkr * sin, 64, 1)).astype(BF)
        dv_ref[...] = dv.astype(BF)

    rev = lambda n: NCH - 1 - n
    return pl.pallas_call(
        body, name="ret_bwd", grid=(RH, NCH),
        in_specs=_ret_specs(rev) + [
            pl.BlockSpec((1, 1, RDK, RDV), lambda h, n: (rev(n), h, 0, 0)),
            pl.BlockSpec((C, RDV), lambda h, n: (rev(n), h)),
        ],
        out_specs=[pl.BlockSpec((C, RDK), lambda h, n: (rev(n), h)),
                   pl.BlockSpec((C, RDK), lambda h, n: (rev(n), h)),
                   pl.BlockSpec((C, RDV), lambda h, n: (rev(n), h))],
        out_shape=[jax.ShapeDtypeStruct((T, RH * RDK), BF), jax.ShapeDtypeStruct((T, RH * RDK), BF),
                   jax.ShapeDtypeStruct((T, RH * RDV), BF)],
        scratch_shapes=[pltpu.VMEM((RDK, RDV), F32)],
        compiler_params=_params(("parallel", "arbitrary")),
    )(z, z, z, cst["cos2"], cst["sin2"], cst["dmask"], cst["xi"], cst["zeta"], cst["gdec"], sprev, dr)


def _log_sigmoid(x):
    return -(jnp.maximum(-x, 0.0) + jnp.log1p(jnp.exp(-jnp.abs(x))))


def _fox_prep(zf, bf_pad, cst):
    def body(zf_ref, b_ref, tri_ref, spread_ref, cb_ref, ct_ref, carry):
        n = pl.program_id(0)

        @pl.when(n == 0)
        def _():
            carry[...] = jnp.zeros_like(carry)

        ls = _log_sigmoid(zf_ref[...] + b_ref[...])
        row = n * C + lax.broadcasted_iota(jnp.int32, (C, C), 0)
        lf = jnp.where(row >= PAD, ls, 0.0)
        cc = jnp.dot(tri_ref[...], lf, precision=HI, preferred_element_type=F32) + carry[0:1, :]
        carry[...] = jnp.broadcast_to(cc[C - 1:C, :], carry.shape)
        cb_ref[...] = jnp.dot(cc, spread_ref[...], precision=HI, preferred_element_type=F32)
        pos = n * C + lax.broadcasted_iota(jnp.int32, (FH, C), 1)
        ct_ref[0] = jnp.where(pos >= PAD, cc.T[:FH, :], -NEG)

    return pl.pallas_call(
        body, name="fox_prep", grid=(NCH,),
        in_specs=[pl.BlockSpec((C, C), lambda n: (n, 0)), pl.BlockSpec((1, C), lambda n: (0, 0)),
                  pl.BlockSpec((C, C), lambda n: (0, 0)), pl.BlockSpec((C, FH * FD), lambda n: (0, 0))],
        out_specs=[pl.BlockSpec((C, FH * FD), lambda n: (_fox_pos(n), 0)),
                   pl.BlockSpec((1, FH, C), lambda n: (n, 0, 0))],
        out_shape=[jax.ShapeDtypeStruct((TROWS, FH * FD), F32), jax.ShapeDtypeStruct((NCH, FH, C), F32)],
        scratch_shapes=[pltpu.VMEM((8, C), F32)],
        compiler_params=_params(("arbitrary",)),
    )(zf, bf_pad, cst["tri"], cst["spread"])


def _lo_lanes(shape):
    return lax.broadcasted_iota(jnp.int32, shape, 1) < FD


def _split_heads(x):
    lo = _lo_lanes(x.shape)
    zero = jnp.zeros_like(x)
    return jnp.concatenate([jnp.where(lo, x, zero), jnp.where(lo, zero, x)], axis=0)


def _spread2(x):
    lo = _lo_lanes(x.shape)
    r = pltpu.roll(x, FD, 1)
    return jnp.concatenate([jnp.where(lo, x, r), jnp.where(lo, r, x)], axis=1)


NSTEP = (NCH + 1) // 2
NTILE = NCH + 1
TROWS = T + C


def _fox_tile(s, t):
    second = t > s
    j = jnp.where(second, t - s - 1, t)
    iq = jnp.where(second, NCH - 1 - s, s)
    kind = jnp.where(second & (s == NSTEP - 1), 2, (j == iq).astype(jnp.int32))
    return second.astype(jnp.int32), j, kind


def _fox_pos(i):
    return jnp.where(i < NSTEP, 2 * i, 2 * (NCH - 1 - i) + 1)


FOX_ORDER = [2 * i if i < NSTEP else 2 * (NCH - 1 - i) + 1 for i in range(NCH)]


def _fox_pair_specs():
    first = pl.BlockSpec((C, C), lambda p, s: (2 * s, p))
    second = pl.BlockSpec((C, C), lambda p, s: (jnp.where(s == NSTEP - 1, 2 * s, 2 * s + 1), p))
    both = pl.BlockSpec((2 * C, C), lambda p, s: (s, p))
    return first, second, both


def _fox_q_specs():
    return (pl.BlockSpec((C, C), lambda p, s: (s, QB_F + p)),
            pl.BlockSpec((C, C), lambda p, s: (NCH - 1 - s, QB_F + p)))


def _fox_key_bias(ct_ref, p, j):
    return jnp.concatenate([ct_ref[j, pl.ds(2 * p, 1), :], ct_ref[j, pl.ds(2 * p + 1, 1), :]], axis=1)


def _fox_fwd(z, cb, ct, cst):
    def body(qa_ref, qb_ref, k_ref, v_ref, ca_ref, cb_ref, ct_ref, ones_ref, mb_ref,
             a_ref, g_ref, kks, vvs, q2, ci2, m2, sbuf):
        p, s = pl.program_id(0), pl.program_id(1)

        @pl.when(s == 0)
        def _():
            ones = ones_ref[...]

            def prep(j, carry):
                rows = pl.ds(pl.multiple_of(j * C, C), C)
                kks[j] = _split_heads(k_ref[rows, :]).astype(BF)
                vvs[j] = jnp.concatenate([_split_heads(v_ref[rows, :]).astype(BF), ones], axis=1)
                return carry

            lax.fori_loop(0, NCH, prep, 0)

        for w, (q_ref, c_ref) in enumerate(((qa_ref, ca_ref), (qb_ref, cb_ref))):
            q2[w] = (q_ref[...] * FSCALE).astype(BF)
            ci2[w] = _spread2(c_ref[...])

        tiles = [_fox_tile(s, t) for t in range(NTILE)]
        neg = jnp.full((C, 2 * C), NEG, F32)
        mx = [neg, neg]
        for t, (sel, j, kind) in enumerate(tiles):
            st = _dg(q2[sel], kks[j], NT) + ((ci2[sel] - _fox_key_bias(ct_ref, p, j)) + mb_ref[kind])
            sbuf[t] = st
            mx = [jnp.maximum(mx[0], jnp.where(t <= s, st, neg)), jnp.maximum(mx[1], jnp.where(t <= s, neg, st))]
        for w in range(2):
            m2[w] = jnp.concatenate(
                [jnp.broadcast_to(jnp.max(mx[w][:, :C], axis=1, keepdims=True), (C, C)),
                 jnp.broadcast_to(jnp.max(mx[w][:, C:], axis=1, keepdims=True), (C, C))], axis=1)

        zero = jnp.zeros((C, 2 * C), F32)
        acc = [zero, zero]
        for t, (sel, j, _) in enumerate(tiles):
            part = _dot(jnp.exp(sbuf[t] - m2[sel]).astype(BF), vvs[j])
            acc = [acc[0] + jnp.where(t <= s, part, zero), acc[1] + jnp.where(t <= s, zero, part)]
        lo = _lo_lanes((C, C))
        for w, c_ref in enumerate((ca_ref, cb_ref)):
            res = acc[w]
            l = res[:, C:]
            a_ref[C * w:C * (w + 1), :] = res[:, :C] / l
            mw = m2[w]
            g_ref[C * w:C * (w + 1), :] = c_ref[...] - (jnp.where(lo, mw[:, :C], mw[:, C:]) + jnp.log(l))

    qa, qb = _fox_q_specs()
    ca, cbs, both = _fox_pair_specs()
    return pl.pallas_call(
        body, name="fox_fwd", grid=(NPAIR, NSTEP),
        in_specs=[qa, qb,
                  pl.BlockSpec((T, C), lambda p, s: (0, KB_F + p)),
                  pl.BlockSpec((T, C), lambda p, s: (0, VB_F + p)),
                  ca, cbs,
                  pl.BlockSpec((NCH, FH, C), lambda p, s: (0, 0, 0)),
                  pl.BlockSpec((2 * C, C), lambda p, s: (0, 0)),
                  pl.BlockSpec((3, C, 2 * C), lambda p, s: (0, 0, 0))],
        out_specs=[both, both],
        out_shape=[jax.ShapeDtypeStruct((TROWS, FH * FD), F32)] * 2,
        scratch_shapes=[pltpu.VMEM((NCH, 2 * C, C), BF), pltpu.VMEM((NCH, 2 * C, 2 * C), BF),
                        pltpu.VMEM((2, C, C), BF), pltpu.VMEM((2, C, 2 * C), F32), pltpu.VMEM((2, C, 2 * C), F32),
                        pltpu.VMEM((NTILE, C, 2 * C), F32)],
        compiler_params=_params(("parallel", "arbitrary")),
    )(z, z, z, z, cb, cb, ct, cst["ones_aug"], cst["mask_bias"])


def _fox_bwd(z, da, g, delta, ct, cst):
    grp = 9

    def body(qa_ref, qb_ref, daa_ref, dab_ref, ga_ref, gb_ref, dla_ref, dlb_ref, k_ref, v_ref, ct_ref, ones_ref,
             mb_ref, dq_ref, dr_ref, dk_ref, dv_ref, dcs_ref,
             kks, vvs, q2, qq2, dd2, da2, gi2, dl2, dq2, dvb, dkb, dkacc, dvacc, csacc):
        p, s = pl.program_id(0), pl.program_id(1)
        ones = ones_ref[...]

        @pl.when(s == 0)
        def _():
            dkacc[...] = jnp.zeros_like(dkacc)
            dvacc[...] = jnp.zeros_like(dvacc)
            csacc[...] = jnp.zeros_like(csacc)

            def prep(j, carry):
                rows = pl.ds(pl.multiple_of(j * C, C), C)
                kks[j] = _split_heads(k_ref[rows, :]).astype(BF)
                vvs[j] = _split_heads(v_ref[rows, :]).astype(BF)
                return carry

            lax.fori_loop(0, NCH, prep, 0)

        for w, (q_ref, d_ref, g_ref, l_ref) in enumerate(((qa_ref, daa_ref, ga_ref, dla_ref),
                                                          (qb_ref, dab_ref, gb_ref, dlb_ref))):
            qf = q_ref[...]
            q2[w] = (qf * FSCALE).astype(BF)
            qq2[w] = jnp.concatenate([_split_heads(qf).astype(BF), ones], axis=1)
            da2[w] = d_ref[...]
            dd2[w] = _split_heads(d_ref[...].astype(F32)).astype(BF)
            gi2[w] = _spread2(g_ref[...])
            dl2[w] = _spread2(l_ref[...])
        dq2[...] = jnp.zeros_like(dq2)
        zero = jnp.zeros((C, 2 * C), F32)

        def group(gi, carry):
            ts = [gi * grp + u for u in range(grp)]
            tiles = [_fox_tile(s, t) for t in ts]
            kk = [kks[j] for _, j, _ in tiles]
            ss = [_dg(q2[sel], kj, NT) + ((gi2[sel] - _fox_key_bias(ct_ref, p, j)) + mb_ref[kind])
                  for kj, (sel, j, kind) in zip(kk, tiles)]
            dps = [_dg(da2[sel], vvs[j], NT) for sel, j, _ in tiles]
            pes = [jnp.exp(st) for st in ss]
            dss = [pe * (dp - dl2[sel]) * FSCALE for pe, dp, (sel, _, _) in zip(pes, dps, tiles)]
            pts = [jnp.concatenate([pe[:, :C].T, pe[:, C:].T], axis=1).astype(BF) for pe in pes]
            dsts = [jnp.concatenate([ds[:, :C].T, ds[:, C:].T], axis=1).astype(BF) for ds in dss]
            dvs = [_dot(pt, dd2[sel]) for pt, (sel, _, _) in zip(pts, tiles)]
            rs = [_dot(dst, qq2[sel]) for dst, (sel, _, _) in zip(dsts, tiles)]
            parts = [_dot(ds.astype(BF), jnp.concatenate([kj, ones], axis=1)) for ds, kj in zip(dss, kk)]
            for t, dv, rr in zip(ts, dvs, rs):
                dvb[t] = dv
                dkb[t] = rr
            pa, pb = zero, zero
            for t, part in zip(ts, parts):
                pa = pa + jnp.where(t <= s, part, zero)
                pb = pb + jnp.where(t <= s, zero, part)
            dq2[0] += pa
            dq2[1] += pb
            return carry

        lax.fori_loop(0, NTILE // grp, group, 0)

        def scatter(t, carry):
            _, j, _ = _fox_tile(s, t)
            r = pl.ds(pl.multiple_of(j * C, C), C)
            dvacc[r, :] += dvb[t]
            dkacc[r, :] += dkb[t, :, :C]
            csacc[r, :] += dkb[t, :, C:]
            return carry

        lax.fori_loop(0, NTILE, scatter, 0)
        for w in range(2):
            res = dq2[w]
            dq_ref[C * w:C * (w + 1), :] = res[:, :C].astype(BF)
            dr_ref[C * w:C * (w + 1), :] = res[:, C:]

        @pl.when(s == NSTEP - 1)
        def _():
            dk_ref[...] = dkacc[...].astype(BF)
            dv_ref[...] = dvacc[...].astype(BF)
            dcs_ref[...] = csacc[...]

    qa, qb = _fox_q_specs()
    ba, bb, both = _fox_pair_specs()
    col = pl.BlockSpec((T, C), lambda p, s: (0, p))
    return pl.pallas_call(
        body, name="fox_bwd", grid=(NPAIR, NSTEP),
        in_specs=[qa, qb, ba, bb, ba, bb, ba, bb,
                  pl.BlockSpec((T, C), lambda p, s: (0, KB_F + p)),
                  pl.BlockSpec((T, C), lambda p, s: (0, VB_F + p)),
                  pl.BlockSpec((NCH, FH, C), lambda p, s: (0, 0, 0)),
                  pl.BlockSpec((2 * C, C), lambda p, s: (0, 0)),
                  pl.BlockSpec((3, C, 2 * C), lambda p, s: (0, 0, 0))],
        out_specs=[both, both, col, col, col],
        out_shape=[jax.ShapeDtypeStruct((TROWS, FH * FD), BF), jax.ShapeDtypeStruct((TROWS, FH * FD), F32),
                   jax.ShapeDtypeStruct((T, FH * FD), BF), jax.ShapeDtypeStruct((T, FH * FD), BF),
                   jax.ShapeDtypeStruct((T, FH * FD), F32)],
        scratch_shapes=[pltpu.VMEM((NCH, 2 * C, C), BF), pltpu.VMEM((NCH, 2 * C, C), BF),
                        pltpu.VMEM((2, C, C), BF), pltpu.VMEM((2, 2 * C, 2 * C), BF), pltpu.VMEM((2, 2 * C, C), BF),
                        pltpu.VMEM((2, C, C), BF), pltpu.VMEM((2, C, 2 * C), F32), pltpu.VMEM((2, C, 2 * C), F32),
                        pltpu.VMEM((2, C, 2 * C), F32),
                        pltpu.VMEM((NTILE, C, C), F32), pltpu.VMEM((NTILE, C, 2 * C), F32),
                        pltpu.VMEM((T, C), F32), pltpu.VMEM((T, C), F32), pltpu.VMEM((T, C), F32)],
        compiler_params=_params(("parallel", "arbitrary")),
    )(z, z, da, da, g, g, delta, delta, z, z, ct, cst["ones_aug"], cst["mask_bias"])


def _fox_gate_bwd(drow, dcol, zf, bf_pad, cst):
    def body(dr_ref, dc_ref, zf_ref, b_ref, tri_ref, pick_ref, dff_ref, db_ref, carry):
        s = pl.program_id(0)
        n = NCH - 1 - s

        @pl.when(s == 0)
        def _():
            carry[...] = jnp.zeros_like(carry)
            db_ref[...] = jnp.zeros_like(db_ref)

        dcb = jnp.dot((dr_ref[...] - dc_ref[...]) * (1.0 / FSCALE), pick_ref[...], precision=HI,
                      preferred_element_type=F32)
        suf = lax.dot_general(tri_ref[...], dcb, TN, precision=HI, preferred_element_type=F32) + carry[0:1, :]
        carry[...] = jnp.broadcast_to(suf[0:1, :], carry.shape)
        x = zf_ref[...] + b_ref[...]
        row = n * C + lax.broadcasted_iota(jnp.int32, (C, C), 0)
        dff = jnp.where(row >= PAD, suf * (1.0 - jax.nn.sigmoid(x)), 0.0)
        dff_ref[...] = dff.astype(BF)
        db_ref[...] += jnp.sum(dff, axis=0, keepdims=True)

    rev = lambda s: (NCH - 1 - s, 0)
    return pl.pallas_call(
        body, name="fox_gate_bwd", grid=(NCH,),
        in_specs=[pl.BlockSpec((C, FH * FD), lambda s: (_fox_pos(NCH - 1 - s), 0)),
                  pl.BlockSpec((C, FH * FD), rev), pl.BlockSpec((C, C), rev),
                  pl.BlockSpec((1, C), lambda s: (0, 0)), pl.BlockSpec((C, C), lambda s: (0, 0)),
                  pl.BlockSpec((FH * FD, C), lambda s: (0, 0))],
        out_specs=[pl.BlockSpec((C, C), rev), pl.BlockSpec((1, C), lambda s: (0, 0))],
        out_shape=[jax.ShapeDtypeStruct((T, C), BF), jax.ShapeDtypeStruct((1, C), F32)],
        scratch_shapes=[pltpu.VMEM((8, C), F32)],
        compiler_params=_params(("arbitrary",)),
    )(drow, dcol, zf, bf_pad, cst["tri"], cst["pick"])


def _gated(r, rg, a, fg):
    rn, rs = [], []
    for h in range(RH):
        rh = r[:, RDV * h:RDV * (h + 1)]
        s = lax.rsqrt(jnp.mean(rh * rh, axis=1, keepdims=True) + EPS)
        rn.append(rh * s)
        rs.append(s)
    rn = jnp.concatenate(rn, axis=1)
    y = jnp.concatenate([rn * (rg * jax.nn.sigmoid(rg)), a * (fg * jax.nn.sigmoid(fg))], axis=1)
    return y, rn, rs


def _out_loss(r, z, a, wout, x, tgt, fgain):
    def body(r_ref, rg_ref, a_ref, fg_ref, w_ref, x_ref, t_ref, g_ref, yt_ref, do_ref, dob_ref, loss_ref, dg_ref):
        i = pl.program_id(0)

        @pl.when(i == 0)
        def _():
            yt_ref[...] = jnp.zeros_like(yt_ref)
            do_ref[...] = jnp.zeros_like(do_ref)
            dob_ref[...] = jnp.zeros_like(dob_ref)
            loss_ref[...] = jnp.zeros_like(loss_ref)
            dg_ref[...] = jnp.zeros_like(dg_ref)

        @pl.when(i > 0)
        def _():
            y, _, _ = _gated(r_ref[...], rg_ref[...], a_ref[...], fg_ref[...])
            yt_ref[...] = y.T.astype(BF)
            o = x_ref[...] + _dot(y.astype(BF), w_ref[...])
            rs = lax.rsqrt(jnp.mean(o * o, axis=1, keepdims=True) + EPS)
            on = o * rs
            g = g_ref[...]
            e = on * g - t_ref[...]
            loss_ref[...] += 0.5 * jnp.sum(jnp.mean(e * e, axis=1, keepdims=True))
            dyh = e * (1.0 / D)
            dg_ref[...] += jnp.sum(dyh * on, axis=0, keepdims=True)
            don = dyh * g
            do = rs * (don - on * jnp.mean(don * on, axis=1, keepdims=True))
            do_ref[...] = do
            dob_ref[...] = do.astype(BF)

    tok = lambda i: (jnp.maximum(i - 1, 0), 0)
    return pl.pallas_call(
        body, name="out_loss", grid=(NCH,),
        in_specs=[pl.BlockSpec((C, D), lambda i: (i, 0)), pl.BlockSpec((C, D), lambda i: (i, GB_R)),
                  pl.BlockSpec((C, D), lambda i: (_fox_pos(i), 0)), pl.BlockSpec((C, D), lambda i: (i, GB_F)),
                  pl.BlockSpec((DMIX, D), lambda i: (0, 0)),
                  pl.BlockSpec((C, D), tok), pl.BlockSpec((C, D), tok), pl.BlockSpec((1, D), lambda i: (0, 0))],
        out_specs=[pl.BlockSpec((DMIX, C), lambda i: (0, i)), pl.BlockSpec((C, D), lambda i: (i, 0)),
                   pl.BlockSpec((C, D), lambda i: (i, 0)), pl.BlockSpec((8, C), lambda i: (0, 0)),
                   pl.BlockSpec((1, D), lambda i: (0, 0))],
        out_shape=[jax.ShapeDtypeStruct((DMIX, T), BF), jax.ShapeDtypeStruct((T, D), F32),
                   jax.ShapeDtypeStruct((T, D), BF), jax.ShapeDtypeStruct((8, C), F32),
                   jax.ShapeDtypeStruct((1, D), F32)],
        compiler_params=_params(("arbitrary",)),
    )(r, z, a, z, wout, x, tgt, fgain)


def _dsilu(x):
    s = jax.nn.sigmoid(x)
    return s * (1.0 + x * (1.0 - s))


def _dy_gate_bwd(dob, wout, r, z, a, seg):
    def body(do_ref, w_ref, r_ref, rg_ref, a_ref, fg_ref, seg_ref, dr_ref, da_ref, drg_ref, dfg_ref, dl_ref):
        dy = _dg(do_ref[...], w_ref[...], NT)
        rg, fg, a_ = rg_ref[...], fg_ref[...], a_ref[...]
        _, rn, rs = _gated(r_ref[...], rg, a_, fg)
        dyr, dyf = dy[:, :D], dy[:, D:]
        drn = dyr * (rg * jax.nn.sigmoid(rg))
        drg_ref[...] = (dyr * rn * _dsilu(rg)).astype(BF)
        for h in range(RH):
            sl = slice(RDV * h, RDV * (h + 1))
            dh, nh = drn[:, sl], rn[:, sl]
            dr_ref[:, sl] = (rs[h] * (dh - nh * jnp.mean(dh * nh, axis=1, keepdims=True))).astype(BF)
        dab = (dyf * (fg * jax.nn.sigmoid(fg))).astype(BF)
        da_ref[...] = dab
        dfg_ref[...] = (dyf * a_ * _dsilu(fg)).astype(BF)
        prod = dab.astype(F32) * a_
        segm = seg_ref[...]
        for p in range(NPAIR):
            sl = slice(C * p, C * (p + 1))
            hi = prod[:, sl].astype(BF)
            lo = (prod[:, sl] - hi.astype(F32)).astype(BF)
            dl_ref[:, sl] = _dot(hi, segm) + _dot(lo, segm)

    row = pl.BlockSpec((C, D), lambda i: (i, 0))
    fox = pl.BlockSpec((C, D), lambda i: (_fox_pos(i), 0))
    return pl.pallas_call(
        body, name="dy_gate_bwd", grid=(NCH,),
        in_specs=[row, pl.BlockSpec((DMIX, D), lambda i: (0, 0)),
                  row, pl.BlockSpec((C, D), lambda i: (i, GB_R)),
                  fox, pl.BlockSpec((C, D), lambda i: (i, GB_F)),
                  pl.BlockSpec((C, C), lambda i: (0, 0))],
        out_specs=[row, fox, row, row, fox],
        out_shape=[jax.ShapeDtypeStruct((T, D), BF), jax.ShapeDtypeStruct((TROWS, D), BF),
                   jax.ShapeDtypeStruct((T, D), BF), jax.ShapeDtypeStruct((T, D), BF),
                   jax.ShapeDtypeStruct((TROWS, D), F32)],
        compiler_params=_params(("parallel",)),
    )(dob, wout, r, z, a, z, seg)


DZ_WIDTHS = (512, 512, 1024, 1024, 1024, 1024, 1024, 1024)


def _du_norm_bwd(dzs, dzf, wt, wft, hpad, g, dopad):
    tm, tk = 544, 1024
    nk = WMAIN // tk

    def body(rq_ref, rk_ref, rv_ref, rg_ref, fq_ref, fk_ref, fv_ref, fg_ref, dzf_ref, w_ref, wf_ref, h_ref, g_ref,
             do_ref, gh_ref, dg_ref, acc):
        i, k = pl.program_id(0), pl.program_id(1)

        @pl.when(k == 0)
        def _():
            acc[...] = (_dot(dzf_ref[...], wf_ref[...]) + _dot(rq_ref[...], w_ref[:512, :])
                        + _dot(rk_ref[...], w_ref[512:, :]))

        for kk, piece in enumerate((rv_ref, rg_ref, fq_ref, fk_ref, fv_ref, fg_ref), start=1):
            @pl.when(k == kk)
            def _(piece=piece):
                acc[...] += _dot(piece[...], w_ref[...])

        @pl.when(k == nk - 1)
        def _():
            du = acc[...]
            h = h_ref[...]
            gg = g_ref[...]
            rs = lax.rsqrt(jnp.mean(h * h, axis=1, keepdims=True) + EPS)
            hn = h * rs
            part = jnp.sum(du * hn, axis=0, keepdims=True)

            @pl.when(i == 0)
            def _():
                dg_ref[...] = part

            @pl.when(i > 0)
            def _():
                dg_ref[...] += part

            dhn = du * gg
            gh_ref[...] = rs * (dhn - hn * jnp.mean(dhn * hn, axis=1, keepdims=True)) + do_ref[...]

    return pl.pallas_call(
        body, name="du_norm_bwd", grid=(T // tm, nk),
        in_specs=[pl.BlockSpec((tm, w), lambda i, k: (i, 0)) for w in DZ_WIDTHS]
        + [pl.BlockSpec((tm, C), lambda i, k: (i, 0)),
           pl.BlockSpec((tk, D), lambda i, k: (k, 0)), pl.BlockSpec((C, D), lambda i, k: (0, 0)),
           pl.BlockSpec((tm, D), lambda i, k: (i, 0)), pl.BlockSpec((1, D), lambda i, k: (0, 0)),
           pl.BlockSpec((tm, D), lambda i, k: (i, 0))],
        out_specs=[pl.BlockSpec((tm, D), lambda i, k: (i, 0)), pl.BlockSpec((1, D), lambda i, k: (0, 0))],
        out_shape=[jax.ShapeDtypeStruct((T, D), F32), jax.ShapeDtypeStruct((1, D), F32)],
        scratch_shapes=[pltpu.VMEM((tm, D), F32)],
        compiler_params=_params(("arbitrary", "arbitrary")),
    )(*dzs, dzf, wt, wft, hpad, g, dopad)


GROWS = 7424


def _dw_in(dzs, dzf, ut):
    tn = 256
    nmain = WMAIN // tn
    first, blocks = [], []
    for w in DZ_WIDTHS:
        first.append(sum(blocks))
        blocks.append(w // tn)

    def body(rq_ref, rk_ref, rv_ref, rg_ref, fq_ref, fk_ref, fv_ref, fg_ref, dzf_ref, ut_ref, o_ref):
        gidx = pl.program_id(0)
        for piece, g0, nb in zip((rq_ref, rk_ref, rv_ref, rg_ref, fq_ref, fk_ref, fv_ref, fg_ref), first, blocks):
            @pl.when((gidx >= g0) & (gidx < g0 + nb))
            def _(piece=piece):
                o_ref[...] = _dot(ut_ref[...], piece[...]).T

        @pl.when(gidx == nmain)
        def _():
            o_ref[:C, :] = _dot(ut_ref[...], dzf_ref[...]).T
            o_ref[C:, :] = jnp.zeros((tn - C, D), F32)

    def piece_spec(g0, nb):
        return pl.BlockSpec((T, tn), lambda gidx: (0, jnp.clip(gidx - g0, 0, nb - 1)))

    return pl.pallas_call(
        body, name="dw_in", grid=(nmain + 1,),
        in_specs=[piece_spec(g0, nb) for g0, nb in zip(first, blocks)]
        + [pl.BlockSpec((T, C), lambda gidx: (0, 0)), pl.BlockSpec((D, T), lambda gidx: (0, 0))],
        out_specs=pl.BlockSpec((tn, D), lambda gidx: (gidx, 0)),
        out_shape=jax.ShapeDtypeStruct((GROWS, D), F32),
        compiler_params=_params(("arbitrary",)),
    )(*dzs, dzf, ut)


def _token_order(x_po):
    def body(i_ref, o_ref):
        o_ref[...] = i_ref[...]

    return pl.pallas_call(
        body, name="token_order", grid=(NCH,),
        in_specs=[pl.BlockSpec((C, D), lambda i: (_fox_pos(i), 0))],
        out_specs=pl.BlockSpec((C, D), lambda i: (i, 0)),
        out_shape=jax.ShapeDtypeStruct((T, D), x_po.dtype),
        compiler_params=_params(("parallel",)),
    )(x_po)


def _local_step(x, tgt, meta, norm_g, wt, wft, b_f, wout, final_g):
    cst = _constants()
    hpad = jnp.concatenate([jnp.pad(meta, ((PAD, 0), (0, 0))), x], axis=0)
    bf_pad = jnp.pad(b_f, ((0, 0), (0, C - NFF)))
    u, ut = _norm_in(hpad, norm_g)
    z = _mm_nt(u, wt, WMAIN, T // 2, 512, "in_proj")
    zf = _mm_nt(u, wft, C, T // 2, C, "in_proj_ff")
    r, sprev = _ret_fwd(z, cst)
    cb, ct = _fox_prep(zf, bf_pad, cst)
    a, g = _fox_fwd(z, cb, ct, cst)
    yt, dopad, dob, loss8, dfg = _out_loss(r, z, a, wout, x, tgt, final_g)
    dr, da, dzrg, dzfg, delta = _dy_gate_bwd(dob, wout, r, z, a, cst["seg"])
    dwout = _mm_nn(yt, dob, 512, D, "dw_out")
    dzq_r, dzk_r, dzv_r = _ret_bwd(z, cst, sprev, dr)
    dq_po, drow, dzk_f, dzv_f, dcol = _fox_bwd(z, da, g, delta, ct, cst)
    dzf, dbf = _fox_gate_bwd(drow, dcol, zf, bf_pad, cst)
    dzs = [dzq_r, dzk_r, dzv_r, dzrg, _token_order(dq_po), dzk_f, dzv_f, dzfg]
    gwt = _dw_in(dzs, dzf, ut)
    gh, dng = _du_norm_bwd(dzs, dzf, wt, wft, hpad, norm_g, dopad)
    return (loss8[0, 0], gh[C:], gh[PAD:C], dng, gwt, dbf[:, :NFF], dwout, dfg)


def _place():
    x, y, c = lax.axis_index("x"), lax.axis_index("y"), lax.axis_index("c")
    return x, y, c


def _other_chips(x, y):
    return [(1 - x, y, 2 * (1 - x) + y), (x, 1 - y, 2 * x + (1 - y)), (1 - x, 1 - y, 2 * (1 - x) + (1 - y))]


def _all_gather_shards(shards):
    n = len(shards)

    def body(*refs):
        ins, outs = refs[:n], refs[n:2 * n]
        send_sems, recv_sems = refs[2 * n:]
        x, y, c = _place()
        me_s = 2 * x + y
        sib = (x, y, 1 - c)
        chips = _other_chips(x, y)
        sends, waits = [], []
        for a in range(n):
            rows = ins[a].shape[0] // 2
            half = pl.ds(c * rows, rows)
            for k, (cx, cy, cs) in enumerate(chips):
                sends.append(pltpu.make_async_remote_copy(
                    src_ref=ins[a].at[half], dst_ref=outs[a].at[me_s, half],
                    send_sem=send_sems.at[6 * a + k], recv_sem=recv_sems.at[6 * a + k],
                    device_id=(cx, cy, c), device_id_type=MESH))
                sends[-1].start()
        for a in range(n):
            rows = ins[a].shape[0] // 2
            half = pl.ds(c * rows, rows)
            other = pl.ds((1 - c) * rows, rows)
            for k, (cx, cy, cs) in enumerate(chips):
                pltpu.make_async_remote_copy(
                    src_ref=outs[a].at[cs, half], dst_ref=outs[a].at[cs, half],
                    send_sem=send_sems.at[6 * a + k], recv_sem=recv_sems.at[6 * a + k],
                    device_id=(cx, cy, c), device_id_type=MESH).wait_recv()
                fwd = pltpu.make_async_remote_copy(
                    src_ref=outs[a].at[cs, half], dst_ref=outs[a].at[cs, half],
                    send_sem=send_sems.at[6 * a + 3 + k], recv_sem=recv_sems.at[6 * a + 3 + k],
                    device_id=sib, device_id_type=MESH)
                fwd.start()
                sends.append(fwd)
                waits.append(pltpu.make_async_remote_copy(
                    src_ref=outs[a].at[cs, other], dst_ref=outs[a].at[cs, other],
                    send_sem=send_sems.at[6 * a + 3 + k], recv_sem=recv_sems.at[6 * a + 3 + k],
                    device_id=sib, device_id_type=MESH))
        for w in waits:
            w.wait_recv()
        for s in sends:
            s.wait_send()

    return pl.pallas_call(
        body, name="all_gather_w",
        in_specs=[ANY] * n, out_specs=[ANY] * n,
        out_shape=[jax.ShapeDtypeStruct((4,) + s.shape, s.dtype) for s in shards],
        scratch_shapes=[pltpu.SemaphoreType.DMA((6 * n,)), pltpu.SemaphoreType.DMA((6 * n,))],
    )(*shards)


WOFF, WLEN = 1792, 2048
WHALF = WLEN // 2


def _pair_swap(gwt, arrs, small):
    n = len(arrs)

    def body(*refs):
        gw, ins, sm = refs[0], refs[1:n + 1], refs[n + 1]
        gwo, outs, smo = refs[n + 2], refs[n + 3:2 * n + 3], refs[2 * n + 3]
        send_sems, recv_sems = refs[2 * n + 4:]
        x, y, c = _place()
        sib = (x, y, 1 - c)
        cps = []
        for k in range(4):
            cps.append(pltpu.make_async_remote_copy(
                src_ref=gw.at[pl.ds(WOFF * k + (1 - c) * WHALF, WHALF)], dst_ref=gwo.at[k],
                send_sem=send_sems.at[k], recv_sem=recv_sems.at[k], device_id=sib, device_id_type=MESH))
        for a in range(n):
            rows = ins[a].shape[1] // 2
            cps.append(pltpu.make_async_remote_copy(
                src_ref=ins[a].at[:, pl.ds((1 - c) * rows, rows)], dst_ref=outs[a],
                send_sem=send_sems.at[4 + a], recv_sem=recv_sems.at[4 + a], device_id=sib, device_id_type=MESH))
        cps.append(pltpu.make_async_remote_copy(
            src_ref=sm, dst_ref=smo, send_sem=send_sems.at[4 + n], recv_sem=recv_sems.at[4 + n],
            device_id=sib, device_id_type=MESH))
        for cp in cps:
            cp.start()
        for cp in cps:
            cp.wait()

    return pl.pallas_call(
        body, name="rs_pair_swap",
        in_specs=[ANY] * (n + 2), out_specs=[ANY] * (n + 2),
        out_shape=[jax.ShapeDtypeStruct((4, WHALF, D), gwt.dtype)]
        + [jax.ShapeDtypeStruct((4, a.shape[1] // 2, a.shape[2]), a.dtype) for a in arrs]
        + [jax.ShapeDtypeStruct(small.shape, small.dtype)],
        scratch_shapes=[pltpu.SemaphoreType.DMA((n + 5,)), pltpu.SemaphoreType.DMA((n + 5,))],
    )(gwt, *arrs, small)


def _add_windows(gwt, recv):
    tb = 256
    nb = WHALF // tb
    c = lax.axis_index("c")

    def body(c_ref, a_ref, b_ref, o_ref):
        o_ref[0] = (a_ref[...] + b_ref[0]).astype(BF)

    return pl.pallas_call(
        body, name="pair_add_in",
        grid_spec=pltpu.PrefetchScalarGridSpec(
            num_scalar_prefetch=1, grid=(4, nb),
            in_specs=[pl.BlockSpec((tb, D), lambda k, i, cr: ((WOFF // tb) * k + nb * cr[0] + i, 0)),
                      pl.BlockSpec((1, tb, D), lambda k, i, cr: (k, i, 0))],
            out_specs=pl.BlockSpec((1, tb, D), lambda k, i, cr: (k, i, 0))),
        out_shape=jax.ShapeDtypeStruct(recv.shape, BF),
        compiler_params=_params(("parallel", "parallel")),
    )(jnp.reshape(c, (1,)).astype(jnp.int32), gwt, recv)


def _chip_exchange(parts, small):
    n = len(parts)

    def body(*refs):
        ins, sm = refs[:n], refs[n]
        outs, smo = refs[n + 1:2 * n + 1], refs[2 * n + 1]
        send_sems, recv_sems = refs[2 * n + 2:]
        x, y, c = _place()
        me_s = 2 * x + y
        chips = _other_chips(x, y)
        cps = []
        for a in range(n + 1):
            src = ins[a] if a < n else sm
            dst = outs[a] if a < n else smo
            for k, (cx, cy, cs) in enumerate(chips):
                cps.append(pltpu.make_async_remote_copy(
                    src_ref=src.at[cs] if a < n else src, dst_ref=dst.at[me_s],
                    send_sem=send_sems.at[3 * a + k], recv_sem=recv_sems.at[3 * a + k],
                    device_id=(cx, cy, c), device_id_type=MESH))
        for cp in cps:
            cp.start()
        for cp in cps:
            cp.wait()

    return pl.pallas_call(
        body, name="rs_chip_exchange",
        in_specs=[ANY] * (n + 1), out_specs=[ANY] * (n + 1),
        out_shape=[jax.ShapeDtypeStruct(p.shape, p.dtype) for p in parts]
        + [jax.ShapeDtypeStruct((4,) + small.shape, small.dtype)],
        scratch_shapes=[pltpu.SemaphoreType.DMA((3 * (n + 1),)), pltpu.SemaphoreType.DMA((3 * (n + 1),))],
    )(*parts, small)


def _pair_send(halves):
    n = len(halves)

    def body(*refs):
        ins, outs = refs[:n], refs[n:2 * n]
        send_sems, recv_sems = refs[2 * n:]
        x, y, c = _place()
        cps = [pltpu.make_async_remote_copy(
            src_ref=ins[a], dst_ref=outs[a], send_sem=send_sems.at[a], recv_sem=recv_sems.at[a],
            device_id=(x, y, 1 - c), device_id_type=MESH) for a in range(n)]
        for cp in cps:
            cp.start()
        for cp in cps:
            cp.wait()

    return pl.pallas_call(
        body, name="rs_pair_send",
        in_specs=[ANY] * n, out_specs=[ANY] * n,
        out_shape=[jax.ShapeDtypeStruct(h.shape, h.dtype) for h in halves],
        scratch_shapes=[pltpu.SemaphoreType.DMA((n,)), pltpu.SemaphoreType.DMA((n,))],
    )(*halves)


def _row_block(rows):
    for tb in (256, 128, 64, 32, 16, 8):
        if rows % tb == 0:
            return tb
    return rows


def _add_halves(full, recv, name, out_dtype):
    _, r2, w = recv.shape
    tb = _row_block(r2)
    nb = r2 // tb
    c = lax.axis_index("c")

    def body(c_ref, a_ref, b_ref, o_ref):
        o_ref[...] = (a_ref[...] + b_ref[...]).astype(o_ref.dtype)

    return pl.pallas_call(
        body, name=name,
        grid_spec=pltpu.PrefetchScalarGridSpec(
            num_scalar_prefetch=1, grid=(4, nb),
            in_specs=[pl.BlockSpec((1, tb, w), lambda s, i, cr: (s, cr[0] * nb + i, 0)),
                      pl.BlockSpec((1, tb, w), lambda s, i, cr: (s, i, 0))],
            out_specs=pl.BlockSpec((1, tb, w), lambda s, i, cr: (s, i, 0))),
        out_shape=jax.ShapeDtypeStruct(recv.shape, out_dtype),
        compiler_params=_params(("parallel", "parallel")),
    )(jnp.reshape(c, (1,)).astype(jnp.int32), full, recv)


def _add2(a, b, name):
    def body(a_ref, b_ref, o_ref):
        o_ref[...] = a_ref[...] + b_ref[...]

    return pl.pallas_call(body, name=name, out_shape=jax.ShapeDtypeStruct(a.shape, a.dtype))(a, b)


def _sum4(buf, own, name):
    _, r, w = buf.shape
    tb = _row_block(r)
    me_s = 2 * lax.axis_index("x") + lax.axis_index("y")
    by_dest = own.ndim == 3

    def body(s_ref, b_ref, own_ref, o_ref):
        mine = (own_ref[0] if by_dest else own_ref[...]).astype(F32)
        terms = [jnp.where(s_ref[0] == t, mine, b_ref[t].astype(F32)) for t in range(4)]
        o_ref[...] = ((terms[0] + terms[1]) + terms[2]) + terms[3]

    own_spec = (pl.BlockSpec((1, tb, w), lambda i, sr: (sr[0], i, 0)) if by_dest
                else pl.BlockSpec((tb, w), lambda i, sr: (i, 0)))
    return pl.pallas_call(
        body, name=name,
        grid_spec=pltpu.PrefetchScalarGridSpec(
            num_scalar_prefetch=1, grid=(r // tb,),
            in_specs=[pl.BlockSpec((4, tb, w), lambda i, sr: (0, i, 0)), own_spec],
            out_specs=pl.BlockSpec((tb, w), lambda i, sr: (i, 0))),
        out_shape=jax.ShapeDtypeStruct((r, w), F32),
        compiler_params=_params(("parallel",)),
    )(jnp.reshape(me_s, (1,)).astype(jnp.int32), buf, own)


def _adamw_math(w, g, m, v):
    mn = B1 * m + (1.0 - B1) * g
    vn = B2 * v + (1.0 - B2) * (g * g)
    m_hat = mn / (1.0 - B1 ** STEP)
    v_hat = vn / (1.0 - B2 ** STEP)
    return -LR * (m_hat / (jnp.sqrt(v_hat) + AEPS) + WD * w), mn, vn


def _adamw(w, g, m, v, name):
    r, c_ = w.shape
    tb = _row_block(r)
    if tb == r and r > 512:
        tb = 256

    def body(w_ref, g_ref, m_ref, v_ref, d_ref, mo_ref, vo_ref):
        d_ref[...], mo_ref[...], vo_ref[...] = _adamw_math(w_ref[...], g_ref[...], m_ref[...], v_ref[...])

    spec = pl.BlockSpec((tb, c_), lambda i: (i, 0))
    return pl.pallas_call(
        body, name=name, grid=(pl.cdiv(r, tb),),
        in_specs=[spec] * 4, out_specs=[spec] * 3,
        out_shape=[jax.ShapeDtypeStruct(w.shape, F32)] * 3,
        compiler_params=_params(("parallel",)),
    )(w, g, m, v)


def _adamw_halves(w, g_mine, g_sib, m, v, name):
    r, c_ = w.shape
    r2 = g_mine.shape[0]
    tb = _row_block(r2)
    nb = r2 // tb
    c = lax.axis_index("c")

    def body(c_ref, w_ref, gm_ref, gs_ref, m_ref, v_ref, g_ref, d_ref, mo_ref, vo_ref):
        g = jnp.where(pl.program_id(0) == c_ref[0], gm_ref[...], gs_ref[...])
        g_ref[...] = g
        d_ref[...], mo_ref[...], vo_ref[...] = _adamw_math(w_ref[...], g, m_ref[...], v_ref[...])

    full = pl.BlockSpec((tb, c_), lambda h, i, cr: (h * nb + i, 0))
    half = pl.BlockSpec((tb, c_), lambda h, i, cr: (i, 0))
    return pl.pallas_call(
        body, name=name,
        grid_spec=pltpu.PrefetchScalarGridSpec(
            num_scalar_prefetch=1, grid=(2, nb),
            in_specs=[full, half, half, full, full], out_specs=[full] * 4),
        out_shape=[jax.ShapeDtypeStruct(w.shape, F32)] * 4,
        compiler_params=_params(("parallel", "parallel")),
    )(jnp.reshape(c, (1,)).astype(jnp.int32), w, g_mine, g_sib, m, v)


def kernel(x, meta_tokens, norm_g, w_in, b_f, w_out, final_g, loss_target, m_meta_tokens, m_norm_g, m_w_in, m_b_f, m_w_out, m_final_g, v_meta_tokens, v_norm_g, v_w_in, v_b_f, v_w_out, v_final_g):
    me_s = 2 * lax.axis_index("x") + lax.axis_index("y")
    core = lax.axis_index("c")
    wt, mt, vt = [jnp.swapaxes(t[0], 0, 1) for t in (w_in, m_w_in, v_w_in)]

    own_win = lax.dynamic_update_slice(jnp.zeros((WPADROWS, D), F32), wt, (4 * me_s, 0)).astype(BF)
    own = [own_win, w_out[0].astype(BF), meta_tokens]
    gathered = _all_gather_shards(own)
    mine = jnp.arange(4) == me_s
    win, gout, gmeta = [jnp.where(mine[:, None, None], o[None], g) for o, g in zip(own, gathered)]
    lap = WPADROWS - WOFF
    tails = jnp.concatenate([jnp.zeros((1, lap, D), BF), win[:-1, WOFF:]], axis=0)
    wt_main = jnp.concatenate([win[:, :lap] + tails, win[:, lap:WOFF]], axis=1).reshape(WMAIN, D)
    wft = jnp.pad(win[3, WOFF:WOFF + NFF], ((0, C - NFF), (0, 0)))
    wout = gout.reshape(DMIX, D)
    meta = jnp.concatenate([gmeta[s] for s in range(4)], axis=1)

    loss, gx, dmeta, dng, gwt, dbf, dwout, dfg = _local_step(
        x[0], loss_target[0], meta, norm_g, wt_main, wft, b_f, wout, final_g.reshape(1, D))

    g_out = dwout.reshape(4, DMIX // 4, D)
    g_meta = jnp.stack([dmeta[:, 256 * s:256 * (s + 1)] for s in range(4)])
    small = jnp.concatenate([dng, dfg, jnp.pad(dbf, ((0, 0), (0, D - NFF))),
                             jnp.pad(jnp.reshape(loss, (1, 1)), ((0, 0), (0, D - 1))),
                             jnp.zeros((4, D), F32)], axis=0)
    r_in, r_out, r_meta, r_small = _pair_swap(gwt, [g_out, g_meta], small)
    p_in = _add_windows(gwt, r_in)
    p_out = _add_halves(g_out, r_out, "pair_add_out", BF)
    p_meta = _add_halves(g_meta, r_meta, "pair_add_meta", F32)
    p_small = _add2(small, r_small, "pair_add_small")
    e_in, e_out, e_meta, e_small = _chip_exchange([p_in, p_out, p_meta], p_small)
    h_in, h_out, h_meta = _sum4(e_in, p_in, "sum_in"), _sum4(e_out, p_out, "sum_out"), _sum4(e_meta, p_meta, "sum_meta")
    tot = _sum4(e_small, p_small, "sum_small")
    s_in, s_out, s_meta = _pair_send([h_in, h_out, h_meta])
    g_norm, g_final, g_bf, loss_all = tot[0:1], tot[1], tot[2:3, :NFF], tot[3, 0]

    gw_meta, d_meta, nm_meta, nv_meta = _adamw_halves(meta_tokens, h_meta, s_meta, m_meta_tokens, v_meta_tokens,
                                                      "adamw_meta")
    d_norm, nm_norm, nv_norm = _adamw(norm_g, g_norm, m_norm_g, v_norm_g, "adamw_norm")
    window = jnp.concatenate([jnp.where(core == 0, h_in, s_in), jnp.where(core == 0, s_in, h_in)], axis=0)
    gwt_own = lax.dynamic_slice(window, (4 * me_s, 0), (WSH, D))
    d_in, nm_in, nv_in = _adamw(wt, gwt_own, mt, vt, "adamw_in")
    gw_in, d_in, nm_in, nv_in = [jnp.swapaxes(t, 0, 1)[None] for t in (gwt_own, d_in, nm_in, nv_in)]
    d_bf, nm_bf, nv_bf = _adamw(b_f, g_bf, m_b_f, v_b_f, "adamw_bf")
    gw_out, d_out, nm_out, nv_out = _adamw_halves(w_out[0], h_out, s_out, m_w_out[0], v_w_out[0], "adamw_out")
    d_fin, nm_fin, nv_fin = _adamw(final_g.reshape(1, D), g_final.reshape(1, D), m_final_g.reshape(1, D),
                                   v_final_g.reshape(1, D), "adamw_final")
    return (loss_all, gx[None], gw_meta, g_norm, gw_in, g_bf, gw_out[None], g_final,
            d_meta, d_norm, d_in, d_bf, d_out[None], d_fin.reshape(D),
            nm_meta, nm_norm, nm_in, nm_bf, nm_out[None], nm_fin.reshape(D),
            nv_meta, nv_norm, nv_in, nv_bf, nv_out[None], nv_fin.reshape(D))
```

```python
import numpy as np
import jax
import jax.numpy as jnp
from jax import lax
from jax.experimental import pallas as pl
from jax.experimental.pallas import tpu as pltpu

D = 1024
SEQ = 2048
NMETA = 16
C = 128
PAD = C - NMETA
T = PAD + NMETA + SEQ
NCH = T // C
RH, RDK, RDV = 4, 128, 256
FH, FD = 16, 64
NPAIR = FH // 2
WMAIN = 7168
NFF = 16
WIN = WMAIN + NFF
WSH = WIN // 4
WPADROWS = 1824
DMIX = 2048
EPS = 1e-6
NEG = -1e30
RSCALE = RDK ** -0.5
FSCALE = FD ** -0.5
ROPE_BASE = 10000.0
LR, B1, B2, AEPS, WD, STEP = 0.001, 0.9, 0.999, 1e-08, 0.01, 10

BF = jnp.bfloat16
F32 = jnp.float32
NT = (((1,), (1,)), ((), ()))
TN = (((0,), (0,)), ((), ()))
HI = lax.Precision.HIGHEST
MESH = pl.DeviceIdType.MESH
ANY = pl.BlockSpec(memory_space=pl.ANY)
VMEM_LIMIT = 48 * 1024 * 1024

QB_R, KB_R = 0, 4
VB_R = 4
GB_R, GB_F = 2, 6
QB_F, KB_F, VB_F = 24, 32, 40


def _dot(a, b):
    return jnp.dot(a, b, preferred_element_type=F32)


def _dg(a, b, dims):
    return lax.dot_general(a, b, dims, preferred_element_type=F32)


def _params(sem=None):
    return pltpu.CompilerParams(dimension_semantics=sem, vmem_limit_bytes=VMEM_LIMIT)


def _constants():
    pos = jnp.arange(T, dtype=F32) - PAD
    inv = ROPE_BASE ** (-jnp.arange(0, RDK, 2, dtype=F32) / RDK)
    ang = pos[:, None] * inv[None, :]
    cos, sin = jnp.cos(ang), jnp.sin(ang)
    cos2 = jnp.concatenate([cos, cos], axis=1)
    sin2 = jnp.concatenate([-sin, sin], axis=1)
    log_gamma = jnp.log1p(-jnp.exp2(-5.0 - jnp.arange(RH, dtype=F32)))
    idx = jnp.arange(C, dtype=F32)
    diff = idx[:, None] - idx[None, :]
    dmask = jnp.where(diff[None] >= 0, jnp.exp(log_gamma[:, None, None] * jnp.maximum(diff, 0.0)[None]), 0.0)
    zeta = jnp.exp(log_gamma[:, None] * (C - 1.0 - idx)[None, :])
    xi = jnp.exp(log_gamma[:, None] * (idx + 1.0)[None, :])
    gdec = jnp.exp(log_gamma * C)
    zeta_b = jnp.broadcast_to(zeta[:, :, None], (RH, C, RDK))
    xi_b = jnp.broadcast_to(xi[:, :, None], (RH, C, RDK))
    gdec_b = jnp.broadcast_to(gdec[:, None, None], (RH, RDK, RDV))
    tri = jnp.asarray(np.tril(np.ones((C, C), np.float32)))
    head_of_lane = np.arange(FH * FD) // FD
    spread = (np.arange(C)[:, None] == head_of_lane[None, :]).astype(np.float32)
    pick = ((np.arange(FH * FD)[:, None] % FD == 0)
            & (head_of_lane[:, None] == np.arange(C)[None, :])).astype(np.float32)
    seg = (np.arange(C)[:, None] // FD == np.arange(C)[None, :] // FD).astype(np.float32)
    ones_aug = np.concatenate([np.tile((np.arange(C) < FD)[None, :], (C, 1)),
                               np.tile((np.arange(C) >= FD)[None, :], (C, 1))], axis=0).astype(np.float32)
    lane = np.arange(2 * C) % C
    causal = np.where(lane[None, :] <= np.arange(C)[:, None], 0.0, NEG).astype(np.float32)
    mask_bias = np.stack([np.zeros((C, 2 * C), np.float32), causal, np.full((C, 2 * C), NEG, np.float32)])
    return dict(cos2=cos2, sin2=sin2, dmask=dmask, zeta=zeta_b, xi=xi_b, gdec=gdec_b, tri=tri,
                mask_bias=jnp.asarray(mask_bias),
                spread=jnp.asarray(spread), pick=jnp.asarray(pick), seg=jnp.asarray(seg, dtype=BF),
                ones_aug=jnp.asarray(ones_aug, dtype=BF))


def _norm_in(hpad, g):
    def body(h_ref, g_ref, u_ref, ut_ref):
        h = h_ref[...]
        rs = lax.rsqrt(jnp.mean(h * h, axis=1, keepdims=True) + EPS)
        u = h * rs * g_ref[...]
        u_ref[...] = u.astype(BF)
        ut_ref[...] = u.T.astype(BF)

    return pl.pallas_call(
        body, name="norm_in", grid=(NCH,),
        in_specs=[pl.BlockSpec((C, D), lambda i: (i, 0)), pl.BlockSpec((1, D), lambda i: (0, 0))],
        out_specs=[pl.BlockSpec((C, D), lambda i: (i, 0)), pl.BlockSpec((D, C), lambda i: (0, i))],
        out_shape=[jax.ShapeDtypeStruct((T, D), BF), jax.ShapeDtypeStruct((D, T), BF)],
        compiler_params=_params(("parallel",)),
    )(hpad, g)


def _mm_nt(a, b, n, tm, tn, name):
    m, k = a.shape

    def body(a_ref, b_ref, o_ref):
        o_ref[...] = _dg(a_ref[...], b_ref[...], NT)

    return pl.pallas_call(
        body, name=name, grid=(m // tm, n // tn),
        in_specs=[pl.BlockSpec((tm, k), lambda i, j: (i, 0)), pl.BlockSpec((tn, k), lambda i, j: (j, 0))],
        out_specs=pl.BlockSpec((tm, tn), lambda i, j: (i, j)),
        out_shape=jax.ShapeDtypeStruct((m, n), F32),
        compiler_params=_params(("parallel", "parallel")),
    )(a, b)


def _mm_nn(a, b, tm, tn, name):
    m, k = a.shape
    _, n = b.shape

    def body(a_ref, b_ref, o_ref):
        o_ref[...] = _dot(a_ref[...], b_ref[...])

    return pl.pallas_call(
        body, name=name, grid=(m // tm, n // tn),
        in_specs=[pl.BlockSpec((tm, k), lambda i, j: (i, 0)), pl.BlockSpec((k, tn), lambda i, j: (0, j))],
        out_specs=pl.BlockSpec((tm, tn), lambda i, j: (i, j)),
        out_shape=jax.ShapeDtypeStruct((m, n), F32),
        compiler_params=_params(("parallel", "parallel")),
    )(a, b)


def _rot(x, cos2, sin2):
    return x * cos2 + pltpu.roll(x, 64, 1) * sin2


def _ret_specs(chunk):
    whole = lambda shape: pl.BlockSpec(shape, lambda n: (0,) * len(shape))
    return [
        pl.BlockSpec((C, RH * RDK), lambda n: (chunk(n), 0)),
        pl.BlockSpec((C, RH * RDK), lambda n: (chunk(n), 1)),
        pl.BlockSpec((C, RH * RDV), lambda n: (chunk(n), 1)),
        pl.BlockSpec((C, RDK), lambda n: (chunk(n), 0)),
        pl.BlockSpec((C, RDK), lambda n: (chunk(n), 0)),
        whole((RH, C, C)), whole((RH, C, RDK)), whole((RH, C, RDK)), whole((RH, RDK, RDV)),
    ]


def _ret_heads(q_ref, k_ref, v_ref, cos, sin):
    qr = [_rot(q_ref[:, RDK * h:RDK * (h + 1)], cos, sin) for h in range(RH)]
    kr = [_rot(k_ref[:, RDK * h:RDK * (h + 1)], cos, sin) * RSCALE for h in range(RH)]
    vb = [v_ref[:, RDV * h:RDV * (h + 1)].astype(BF) for h in range(RH)]
    return qr, kr, [t.astype(BF) for t in qr], [t.astype(BF) for t in kr], vb


def _ret_fwd(z, cst):
    def body(q_ref, k_ref, v_ref, cos_ref, sin_ref, dm_ref, xi_ref, zt_ref, gd_ref, r_ref, sp_ref, st):
        n = pl.program_id(0)

        @pl.when(n == 0)
        def _():
            st[...] = jnp.zeros_like(st)

        hs = range(RH)
        qr, kr, qb, kb, vb = _ret_heads(q_ref, k_ref, v_ref, cos_ref[...], sin_ref[...])
        sd = [(_dg(qb[h], kb[h], NT) * dm_ref[h]).astype(BF) for h in hs]
        state = [st[h] for h in hs]
        qx = [(qr[h] * xi_ref[h]).astype(BF) for h in hs]
        kz = [(kr[h] * zt_ref[h]).astype(BF) for h in hs]
        out = [_dot(sd[h], vb[h]) + _dot(qx[h], state[h].astype(BF)) for h in hs]
        kv = [_dg(kz[h], vb[h], TN) for h in hs]
        for h in hs:
            sp_ref[0, h] = state[h]
            r_ref[:, RDV * h:RDV * (h + 1)] = out[h]
            st[h] = state[h] * gd_ref[h] + kv[h]

    return pl.pallas_call(
        body, name="ret_fwd", grid=(NCH,),
        in_specs=_ret_specs(lambda n: n),
        out_specs=[pl.BlockSpec((C, RH * RDV), lambda n: (n, 0)),
                   pl.BlockSpec((1, RH, RDK, RDV), lambda n: (n, 0, 0, 0))],
        out_shape=[jax.ShapeDtypeStruct((T, RH * RDV), F32), jax.ShapeDtypeStruct((NCH, RH, RDK, RDV), F32)],
        scratch_shapes=[pltpu.VMEM((RH, RDK, RDV), F32)],
        compiler_params=_params(("arbitrary",)),
    )(z, z, z, cst["cos2"], cst["sin2"], cst["dmask"], cst["xi"], cst["zeta"], cst["gdec"])


def _ret_bwd(z, cst, sprev, dr):
    def body(q_ref, k_ref, v_ref, cos_ref, sin_ref, dm_ref, xi_ref, zt_ref, gd_ref, sp_ref, dr_ref,
             dq_ref, dk_ref, dv_ref, gst):
        i = pl.program_id(0)

        @pl.when(i == 0)
        def _():
            gst[...] = jnp.zeros_like(gst)

        hs = range(RH)
        cos, sin = cos_ref[...], sin_ref[...]
        qr, kr, qb, kb, vb = _ret_heads(q_ref, k_ref, v_ref, cos, sin)
        dm = [dm_ref[h] for h in hs]
        xi = [xi_ref[h] for h in hs]
        zt = [zt_ref[h] for h in hs]
        sd = [(_dg(qb[h], kb[h], NT) * dm[h]).astype(BF) for h in hs]
        qx = [(qr[h] * xi[h]).astype(BF) for h in hs]
        kz = [(kr[h] * zt[h]).astype(BF) for h in hs]
        drb = [dr_ref[:, RDV * h:RDV * (h + 1)] for h in hs]
        sb = [sp_ref[0, h].astype(BF) for h in hs]
        g = [gst[h] for h in hs]
        gb = [t.astype(BF) for t in g]
        ds = [(_dg(drb[h], vb[h], NT) * dm[h]).astype(BF) for h in hs]
        dq = [_dot(ds[h], kb[h]) + _dg(drb[h], sb[h], NT) * xi[h] for h in hs]
        dk = [(_dg(ds[h], qb[h], TN) + _dg(vb[h], gb[h], NT) * zt[h]) * RSCALE for h in hs]
        dv = [_dg(sd[h], drb[h], TN) + _dot(kz[h], gb[h]) for h in hs]
        gn = [g[h] * gd_ref[h] + _dg(qx[h], drb[h], TN) for h in hs]
        for h in hs:
            gst[h] = gn[h]
            dq_ref[:, RDK * h:RDK * (h + 1)] = (dq[h] * cos + pltpu.roll(dq[h] * sin, 64, 1)).astype(BF)
            dk_ref[:, RDK * h:RDK * (h + 1)] = (dk[h] * cos + pltpu.roll(dk[h] * sin, 64, 1)).astype(BF)
            dv_ref[:, RDV * h:RDV * (h + 1)] = dv[h].astype(BF)

    rev = lambda n: NCH - 1 - n
    return pl.pallas_call(
        body, name="ret_bwd", grid=(NCH,),
        in_specs=_ret_specs(rev) + [
            pl.BlockSpec((1, RH, RDK, RDV), lambda n: (rev(n), 0, 0, 0)),
            pl.BlockSpec((C, RH * RDV), lambda n: (rev(n), 0)),
        ],
        out_specs=[pl.BlockSpec((C, RH * RDK), lambda n: (rev(n), 0)),
                   pl.BlockSpec((C, RH * RDK), lambda n: (rev(n), 0)),
                   pl.BlockSpec((C, RH * RDV), lambda n: (rev(n), 0))],
        out_shape=[jax.ShapeDtypeStruct((T, RH * RDK), BF), jax.ShapeDtypeStruct((T, RH * RDK), BF),
                   jax.ShapeDtypeStruct((T, RH * RDV), BF)],
        scratch_shapes=[pltpu.VMEM((RH, RDK, RDV), F32)],
        compiler_params=_params(("arbitrary",)),
    )(z, z, z, cst["cos2"], cst["sin2"], cst["dmask"], cst["xi"], cst["zeta"], cst["gdec"], sprev, dr)


def _log_sigmoid(x):
    return -(jnp.maximum(-x, 0.0) + jnp.log1p(jnp.exp(-jnp.abs(x))))


def _fox_prep(zf, bf_pad, cst):
    def body(zf_ref, b_ref, tri_ref, spread_ref, cb_ref, ct_ref, carry):
        n = pl.program_id(0)

        @pl.when(n == 0)
        def _():
            carry[...] = jnp.zeros_like(carry)

        ls = _log_sigmoid(zf_ref[...] + b_ref[...])
        row = n * C + lax.broadcasted_iota(jnp.int32, (C, C), 0)
        lf = jnp.where(row >= PAD, ls, 0.0)
        cc = jnp.dot(tri_ref[...], lf, precision=HI, preferred_element_type=F32) + carry[0:1, :]
        carry[...] = jnp.broadcast_to(cc[C - 1:C, :], carry.shape)
        cb_ref[...] = jnp.dot(cc, spread_ref[...], precision=HI, preferred_element_type=F32)
        pos = n * C + lax.broadcasted_iota(jnp.int32, (FH, C), 1)
        ct_ref[0] = jnp.where(pos >= PAD, cc.T[:FH, :], -NEG)

    return pl.pallas_call(
        body, name="fox_prep", grid=(NCH,),
        in_specs=[pl.BlockSpec((C, C), lambda n: (n, 0)), pl.BlockSpec((1, C), lambda n: (0, 0)),
                  pl.BlockSpec((C, C), lambda n: (0, 0)), pl.BlockSpec((C, FH * FD), lambda n: (0, 0))],
        out_specs=[pl.BlockSpec((C, FH * FD), lambda n: (_fox_pos(n), 0)),
                   pl.BlockSpec((1, FH, C), lambda n: (n, 0, 0))],
        out_shape=[jax.ShapeDtypeStruct((TROWS, FH * FD), F32), jax.ShapeDtypeStruct((NCH, FH, C), F32)],
        scratch_shapes=[pltpu.VMEM((8, C), F32)],
        compiler_params=_params(("arbitrary",)),
    )(zf, bf_pad, cst["tri"], cst["spread"])


def _lo_lanes(shape):
    return lax.broadcasted_iota(jnp.int32, shape, 1) < FD


def _split_heads(x):
    lo = _lo_lanes(x.shape)
    zero = jnp.zeros_like(x)
    return jnp.concatenate([jnp.where(lo, x, zero), jnp.where(lo, zero, x)], axis=0)


def _spread2(x):
    lo = _lo_lanes(x.shape)
    r = pltpu.roll(x, FD, 1)
    return jnp.concatenate([jnp.where(lo, x, r), jnp.where(lo, r, x)], axis=1)


NSTEP = (NCH + 1) // 2
NTILE = NCH + 1
TROWS = T + C


def _fox_tile(s, t):
    second = t > s
    j = jnp.where(second, t - s - 1, t)
    iq = jnp.where(second, NCH - 1 - s, s)
    kind = jnp.where(second & (s == NSTEP - 1), 2, (j == iq).astype(jnp.int32))
    return second.astype(jnp.int32), j, kind


def _fox_pos(i):
    return jnp.where(i < NSTEP, 2 * i, 2 * (NCH - 1 - i) + 1)


FOX_ORDER = [2 * i if i < NSTEP else 2 * (NCH - 1 - i) + 1 for i in range(NCH)]


def _fox_pair_specs():
    first = pl.BlockSpec((C, C), lambda p, s: (2 * s, p))
    second = pl.BlockSpec((C, C), lambda p, s: (jnp.where(s == NSTEP - 1, 2 * s, 2 * s + 1), p))
    both = pl.BlockSpec((2 * C, C), lambda p, s: (s, p))
    return first, second, both


def _fox_q_specs():
    return (pl.BlockSpec((C, C), lambda p, s: (s, QB_F + p)),
            pl.BlockSpec((C, C), lambda p, s: (NCH - 1 - s, QB_F + p)))


def _fox_key_bias(ct_ref, p, j):
    return jnp.concatenate([ct_ref[j, pl.ds(2 * p, 1), :], ct_ref[j, pl.ds(2 * p + 1, 1), :]], axis=1)


def _fox_fwd(z, cb, ct, cst):
    def body(qa_ref, qb_ref, k_ref, v_ref, ca_ref, cb_ref, ct_ref, ones_ref, mb_ref,
             a_ref, g_ref, kks, vvs, q2, ci2, m2, sbuf):
        p, s = pl.program_id(0), pl.program_id(1)

        @pl.when(s == 0)
        def _():
            ones = ones_ref[...]

            def prep(j, carry):
                rows = pl.ds(pl.multiple_of(j * C, C), C)
                kks[j] = _split_heads(k_ref[rows, :]).astype(BF)
                vvs[j] = jnp.concatenate([_split_heads(v_ref[rows, :]).astype(BF), ones], axis=1)
                return carry

            lax.fori_loop(0, NCH, prep, 0)

        for w, (q_ref, c_ref) in enumerate(((qa_ref, ca_ref), (qb_ref, cb_ref))):
            q2[w] = (q_ref[...] * FSCALE).astype(BF)
            ci2[w] = _spread2(c_ref[...])

        tiles = [_fox_tile(s, t) for t in range(NTILE)]
        neg = jnp.full((C, 2 * C), NEG, F32)
        mx = [neg, neg]
        for t, (sel, j, kind) in enumerate(tiles):
            st = _dg(q2[sel], kks[j], NT) + ((ci2[sel] - _fox_key_bias(ct_ref, p, j)) + mb_ref[kind])
            sbuf[t] = st
            mx = [jnp.maximum(mx[0], jnp.where(t <= s, st, neg)), jnp.maximum(mx[1], jnp.where(t <= s, neg, st))]
        for w in range(2):
            m2[w] = jnp.concatenate(
                [jnp.broadcast_to(jnp.max(mx[w][:, :C], axis=1, keepdims=True), (C, C)),
                 jnp.broadcast_to(jnp.max(mx[w][:, C:], axis=1, keepdims=True), (C, C))], axis=1)

        zero = jnp.zeros((C, 2 * C), F32)
        acc = [zero, zero]
        for t, (sel, j, _) in enumerate(tiles):
            part = _dot(jnp.exp(sbuf[t] - m2[sel]).astype(BF), vvs[j])
            acc = [acc[0] + jnp.where(t <= s, part, zero), acc[1] + jnp.where(t <= s, zero, part)]
        lo = _lo_lanes((C, C))
        for w, c_ref in enumerate((ca_ref, cb_ref)):
            res = acc[w]
            l = res[:, C:]
            a_ref[C * w:C * (w + 1), :] = res[:, :C] / l
            mw = m2[w]
            g_ref[C * w:C * (w + 1), :] = c_ref[...] - (jnp.where(lo, mw[:, :C], mw[:, C:]) + jnp.log(l))

    qa, qb = _fox_q_specs()
    ca, cbs, both = _fox_pair_specs()
    return pl.pallas_call(
        body, name="fox_fwd", grid=(NPAIR, NSTEP),
        in_specs=[qa, qb,
                  pl.BlockSpec((T, C), lambda p, s: (0, KB_F + p)),
                  pl.BlockSpec((T, C), lambda p, s: (0, VB_F + p)),
                  ca, cbs,
                  pl.BlockSpec((NCH, FH, C), lambda p, s: (0, 0, 0)),
                  pl.BlockSpec((2 * C, C), lambda p, s: (0, 0)),
                  pl.BlockSpec((3, C, 2 * C), lambda p, s: (0, 0, 0))],
        out_specs=[both, both],
        out_shape=[jax.ShapeDtypeStruct((TROWS, FH * FD), F32)] * 2,
        scratch_shapes=[pltpu.VMEM((NCH, 2 * C, C), BF), pltpu.VMEM((NCH, 2 * C, 2 * C), BF),
                        pltpu.VMEM((2, C, C), BF), pltpu.VMEM((2, C, 2 * C), F32), pltpu.VMEM((2, C, 2 * C), F32),
                        pltpu.VMEM((NTILE, C, 2 * C), F32)],
        compiler_params=_params(("parallel", "arbitrary")),
    )(z, z, z, z, cb, cb, ct, cst["ones_aug"], cst["mask_bias"])


def _fox_bwd(z, da, g, delta, ct, cst):
    grp = 9

    def body(qa_ref, qb_ref, daa_ref, dab_ref, ga_ref, gb_ref, dla_ref, dlb_ref, k_ref, v_ref, ct_ref, ones_ref,
             mb_ref, dq_ref, dr_ref, dk_ref, dv_ref, dcs_ref,
             kks, vvs, q2, qq2, dd2, da2, gi2, dl2, dq2, dvb, dkb, dkacc, dvacc, csacc):
        p, s = pl.program_id(0), pl.program_id(1)
        ones = ones_ref[...]

        @pl.when(s == 0)
        def _():
            dkacc[...] = jnp.zeros_like(dkacc)
            dvacc[...] = jnp.zeros_like(dvacc)
            csacc[...] = jnp.zeros_like(csacc)

            def prep(j, carry):
                rows = pl.ds(pl.multiple_of(j * C, C), C)
                kks[j] = _split_heads(k_ref[rows, :]).astype(BF)
                vvs[j] = _split_heads(v_ref[rows, :]).astype(BF)
                return carry

            lax.fori_loop(0, NCH, prep, 0)

        for w, (q_ref, d_ref, g_ref, l_ref) in enumerate(((qa_ref, daa_ref, ga_ref, dla_ref),
                                                          (qb_ref, dab_ref, gb_ref, dlb_ref))):
            qf = q_ref[...]
            q2[w] = (qf * FSCALE).astype(BF)
            qq2[w] = jnp.concatenate([_split_heads(qf).astype(BF), ones], axis=1)
            da2[w] = d_ref[...]
            dd2[w] = _split_heads(d_ref[...].astype(F32)).astype(BF)
            gi2[w] = _spread2(g_ref[...])
            dl2[w] = _spread2(l_ref[...])
        dq2[...] = jnp.zeros_like(dq2)
        zero = jnp.zeros((C, 2 * C), F32)

        def group(gi, carry):
            ts = [gi * grp + u for u in range(grp)]
            tiles = [_fox_tile(s, t) for t in ts]
            kk = [kks[j] for _, j, _ in tiles]
            ss = [_dg(q2[sel], kj, NT) + ((gi2[sel] - _fox_key_bias(ct_ref, p, j)) + mb_ref[kind])
                  for kj, (sel, j, kind) in zip(kk, tiles)]
            dps = [_dg(da2[sel], vvs[j], NT) for sel, j, _ in tiles]
            pes = [jnp.exp(st) for st in ss]
            dss = [pe * (dp - dl2[sel]) * FSCALE for pe, dp, (sel, _, _) in zip(pes, dps, tiles)]
            pts = [jnp.concatenate([pe[:, :C].T, pe[:, C:].T], axis=1).astype(BF) for pe in pes]
            dsts = [jnp.concatenate([ds[:, :C].T, ds[:, C:].T], axis=1).astype(BF) for ds in dss]
            dvs = [_dot(pt, dd2[sel]) for pt, (sel, _, _) in zip(pts, tiles)]
            rs = [_dot(dst, qq2[sel]) for dst, (sel, _, _) in zip(dsts, tiles)]
            parts = [_dot(ds.astype(BF), jnp.concatenate([kj, ones], axis=1)) for ds, kj in zip(dss, kk)]
            for t, dv, rr in zip(ts, dvs, rs):
                dvb[t] = dv
                dkb[t] = rr
            pa, pb = zero, zero
            for t, part in zip(ts, parts):
                pa = pa + jnp.where(t <= s, part, zero)
                pb = pb + jnp.where(t <= s, zero, part)
            dq2[0] += pa
            dq2[1] += pb
            return carry

        lax.fori_loop(0, NTILE // grp, group, 0)

        def scatter(t, carry):
            _, j, _ = _fox_tile(s, t)
            r = pl.ds(pl.multiple_of(j * C, C), C)
            dvacc[r, :] += dvb[t]
            dkacc[r, :] += dkb[t, :, :C]
            csacc[r, :] += dkb[t, :, C:]
            return carry

        lax.fori_loop(0, NTILE, scatter, 0)
        for w in range(2):
            res = dq2[w]
            dq_ref[C * w:C * (w + 1), :] = res[:, :C].astype(BF)
            dr_ref[C * w:C * (w + 1), :] = res[:, C:]

        @pl.when(s == NSTEP - 1)
        def _():
            dk_ref[...] = dkacc[...].astype(BF)
            dv_ref[...] = dvacc[...].astype(BF)
            dcs_ref[...] = csacc[...]

    qa, qb = _fox_q_specs()
    ba, bb, both = _fox_pair_specs()
    col = pl.BlockSpec((T, C), lambda p, s: (0, p))
    return pl.pallas_call(
        body, name="fox_bwd", grid=(NPAIR, NSTEP),
        in_specs=[qa, qb, ba, bb, ba, bb, ba, bb,
                  pl.BlockSpec((T, C), lambda p, s: (0, KB_F + p)),
                  pl.BlockSpec((T, C), lambda p, s: (0, VB_F + p)),
                  pl.BlockSpec((NCH, FH, C), lambda p, s: (0, 0, 0)),
                  pl.BlockSpec((2 * C, C), lambda p, s: (0, 0)),
                  pl.BlockSpec((3, C, 2 * C), lambda p, s: (0, 0, 0))],
        out_specs=[both, both, col, col, col],
        out_shape=[jax.ShapeDtypeStruct((TROWS, FH * FD), BF), jax.ShapeDtypeStruct((TROWS, FH * FD), F32),
                   jax.ShapeDtypeStruct((T, FH * FD), BF), jax.ShapeDtypeStruct((T, FH * FD), BF),
                   jax.ShapeDtypeStruct((T, FH * FD), F32)],
        scratch_shapes=[pltpu.VMEM((NCH, 2 * C, C), BF), pltpu.VMEM((NCH, 2 * C, C), BF),
                        pltpu.VMEM((2, C, C), BF), pltpu.VMEM((2, 2 * C, 2 * C), BF), pltpu.VMEM((2, 2 * C, C), BF),
                        pltpu.VMEM((2, C, C), BF), pltpu.VMEM((2, C, 2 * C), F32), pltpu.VMEM((2, C, 2 * C), F32),
                        pltpu.VMEM((2, C, 2 * C), F32),
                        pltpu.VMEM((NTILE, C, C), F32), pltpu.VMEM((NTILE, C, 2 * C), F32),
                        pltpu.VMEM((T, C), F32), pltpu.VMEM((T, C), F32), pltpu.VMEM((T, C), F32)],
        compiler_params=_params(("parallel", "arbitrary")),
    )(z, z, da, da, g, g, delta, delta, z, z, ct, cst["ones_aug"], cst["mask_bias"])


def _fox_gate_bwd(drow, dcol, zf, bf_pad, cst):
    def body(dr_ref, dc_ref, zf_ref, b_ref, tri_ref, pick_ref, dff_ref, db_ref, carry):
        s = pl.program_id(0)
        n = NCH - 1 - s

        @pl.when(s == 0)
        def _():
            carry[...] = jnp.zeros_like(carry)
            db_ref[...] = jnp.zeros_like(db_ref)

        dcb = jnp.dot((dr_ref[...] - dc_ref[...]) * (1.0 / FSCALE), pick_ref[...], precision=HI,
                      preferred_element_type=F32)
        suf = lax.dot_general(tri_ref[...], dcb, TN, precision=HI, preferred_element_type=F32) + carry[0:1, :]
        carry[...] = jnp.broadcast_to(suf[0:1, :], carry.shape)
        x = zf_ref[...] + b_ref[...]
        row = n * C + lax.broadcasted_iota(jnp.int32, (C, C), 0)
        dff = jnp.where(row >= PAD, suf * (1.0 - jax.nn.sigmoid(x)), 0.0)
        dff_ref[...] = dff.astype(BF)
        db_ref[...] += jnp.sum(dff, axis=0, keepdims=True)

    rev = lambda s: (NCH - 1 - s, 0)
    return pl.pallas_call(
        body, name="fox_gate_bwd", grid=(NCH,),
        in_specs=[pl.BlockSpec((C, FH * FD), lambda s: (_fox_pos(NCH - 1 - s), 0)),
                  pl.BlockSpec((C, FH * FD), rev), pl.BlockSpec((C, C), rev),
                  pl.BlockSpec((1, C), lambda s: (0, 0)), pl.BlockSpec((C, C), lambda s: (0, 0)),
                  pl.BlockSpec((FH * FD, C), lambda s: (0, 0))],
        out_specs=[pl.BlockSpec((C, C), rev), pl.BlockSpec((1, C), lambda s: (0, 0))],
        out_shape=[jax.ShapeDtypeStruct((T, C), BF), jax.ShapeDtypeStruct((1, C), F32)],
        scratch_shapes=[pltpu.VMEM((8, C), F32)],
        compiler_params=_params(("arbitrary",)),
    )(drow, dcol, zf, bf_pad, cst["tri"], cst["pick"])


def _gated(r, rg, a, fg):
    rn, rs = [], []
    for h in range(RH):
        rh = r[:, RDV * h:RDV * (h + 1)]
        s = lax.rsqrt(jnp.mean(rh * rh, axis=1, keepdims=True) + EPS)
        rn.append(rh * s)
        rs.append(s)
    rn = jnp.concatenate(rn, axis=1)
    y = jnp.concatenate([rn * (rg * jax.nn.sigmoid(rg)), a * (fg * jax.nn.sigmoid(fg))], axis=1)
    return y, rn, rs


def _out_loss(r, z, a, wout, x, tgt, fgain):
    def body(r_ref, rg_ref, a_ref, fg_ref, w_ref, x_ref, t_ref, g_ref, yt_ref, do_ref, dob_ref, loss_ref, dg_ref):
        i = pl.program_id(0)

        @pl.when(i == 0)
        def _():
            yt_ref[...] = jnp.zeros_like(yt_ref)
            do_ref[...] = jnp.zeros_like(do_ref)
            dob_ref[...] = jnp.zeros_like(dob_ref)
            loss_ref[...] = jnp.zeros_like(loss_ref)
            dg_ref[...] = jnp.zeros_like(dg_ref)

        @pl.when(i > 0)
        def _():
            y, _, _ = _gated(r_ref[...], rg_ref[...], a_ref[...], fg_ref[...])
            yt_ref[...] = y.T.astype(BF)
            o = x_ref[...] + _dot(y.astype(BF), w_ref[...])
            rs = lax.rsqrt(jnp.mean(o * o, axis=1, keepdims=True) + EPS)
            on = o * rs
            g = g_ref[...]
            e = on * g - t_ref[...]
            loss_ref[...] += 0.5 * jnp.sum(jnp.mean(e * e, axis=1, keepdims=True))
            dyh = e * (1.0 / D)
            dg_ref[...] += jnp.sum(dyh * on, axis=0, keepdims=True)
            don = dyh * g
            do = rs * (don - on * jnp.mean(don * on, axis=1, keepdims=True))
            do_ref[...] = do
            dob_ref[...] = do.astype(BF)

    tok = lambda i: (jnp.maximum(i - 1, 0), 0)
    return pl.pallas_call(
        body, name="out_loss", grid=(NCH,),
        in_specs=[pl.BlockSpec((C, D), lambda i: (i, 0)), pl.BlockSpec((C, D), lambda i: (i, GB_R)),
                  pl.BlockSpec((C, D), lambda i: (_fox_pos(i), 0)), pl.BlockSpec((C, D), lambda i: (i, GB_F)),
                  pl.BlockSpec((DMIX, D), lambda i: (0, 0)),
                  pl.BlockSpec((C, D), tok), pl.BlockSpec((C, D), tok), pl.BlockSpec((1, D), lambda i: (0, 0))],
        out_specs=[pl.BlockSpec((DMIX, C), lambda i: (0, i)), pl.BlockSpec((C, D), lambda i: (i, 0)),
                   pl.BlockSpec((C, D), lambda i: (i, 0)), pl.BlockSpec((8, C), lambda i: (0, 0)),
                   pl.BlockSpec((1, D), lambda i: (0, 0))],
        out_shape=[jax.ShapeDtypeStruct((DMIX, T), BF), jax.ShapeDtypeStruct((T, D), F32),
                   jax.ShapeDtypeStruct((T, D), BF), jax.ShapeDtypeStruct((8, C), F32),
                   jax.ShapeDtypeStruct((1, D), F32)],
        compiler_params=_params(("arbitrary",)),
    )(r, z, a, z, wout, x, tgt, fgain)


def _dsilu(x):
    s = jax.nn.sigmoid(x)
    return s * (1.0 + x * (1.0 - s))


def _dy_gate_bwd(dob, wout, r, z, a, seg):
    def body(do_ref, w_ref, r_ref, rg_ref, a_ref, fg_ref, seg_ref, dr_ref, da_ref, drg_ref, dfg_ref, dl_ref):
        dy = _dg(do_ref[...], w_ref[...], NT)
        rg, fg, a_ = rg_ref[...], fg_ref[...], a_ref[...]
        _, rn, rs = _gated(r_ref[...], rg, a_, fg)
        dyr, dyf = dy[:, :D], dy[:, D:]
        drn = dyr * (rg * jax.nn.sigmoid(rg))
        drg_ref[...] = (dyr * rn * _dsilu(rg)).astype(BF)
        for h in range(RH):
            sl = slice(RDV * h, RDV * (h + 1))
            dh, nh = drn[:, sl], rn[:, sl]
            dr_ref[:, sl] = (rs[h] * (dh - nh * jnp.mean(dh * nh, axis=1, keepdims=True))).astype(BF)
        dab = (dyf * (fg * jax.nn.sigmoid(fg))).astype(BF)
        da_ref[...] = dab
        dfg_ref[...] = (dyf * a_ * _dsilu(fg)).astype(BF)
        prod = dab.astype(F32) * a_
        segm = seg_ref[...]
        for p in range(NPAIR):
            sl = slice(C * p, C * (p + 1))
            hi = prod[:, sl].astype(BF)
            lo = (prod[:, sl] - hi.astype(F32)).astype(BF)
            dl_ref[:, sl] = _dot(hi, segm) + _dot(lo, segm)

    row = pl.BlockSpec((C, D), lambda i: (i, 0))
    fox = pl.BlockSpec((C, D), lambda i: (_fox_pos(i), 0))
    return pl.pallas_call(
        body, name="dy_gate_bwd", grid=(NCH,),
        in_specs=[row, pl.BlockSpec((DMIX, D), lambda i: (0, 0)),
                  row, pl.BlockSpec((C, D), lambda i: (i, GB_R)),
                  fox, pl.BlockSpec((C, D), lambda i: (i, GB_F)),
                  pl.BlockSpec((C, C), lambda i: (0, 0))],
        out_specs=[row, fox, row, row, fox],
        out_shape=[jax.ShapeDtypeStruct((T, D), BF), jax.ShapeDtypeStruct((TROWS, D), BF),
                   jax.ShapeDtypeStruct((T, D), BF), jax.ShapeDtypeStruct((T, D), BF),
                   jax.ShapeDtypeStruct((TROWS, D), F32)],
        compiler_params=_params(("parallel",)),
    )(dob, wout, r, z, a, z, seg)


DZ_WIDTHS = (512, 512, 1024, 1024, 1024, 1024, 1024, 1024)


def _du_norm_bwd(dzs, dzf, wt, wft, hpad, g, dopad):
    tm, tk = 544, 1024
    nk = WMAIN // tk

    def body(rq_ref, rk_ref, rv_ref, rg_ref, fq_ref, fk_ref, fv_ref, fg_ref, dzf_ref, w_ref, wf_ref, h_ref, g_ref,
             do_ref, gh_ref, dg_ref, acc):
        i, k = pl.program_id(0), pl.program_id(1)

        @pl.when(k == 0)
        def _():
            acc[...] = (_dot(dzf_ref[...], wf_ref[...]) + _dot(rq_ref[...], w_ref[:512, :])
                        + _dot(rk_ref[...], w_ref[512:, :]))

        for kk, piece in enumerate((rv_ref, rg_ref, fq_ref, fk_ref, fv_ref, fg_ref), start=1):
            @pl.when(k == kk)
            def _(piece=piece):
                acc[...] += _dot(piece[...], w_ref[...])

        @pl.when(k == nk - 1)
        def _():
            du = acc[...]
            h = h_ref[...]
            gg = g_ref[...]
            rs = lax.rsqrt(jnp.mean(h * h, axis=1, keepdims=True) + EPS)
            hn = h * rs
            part = jnp.sum(du * hn, axis=0, keepdims=True)

            @pl.when(i == 0)
            def _():
                dg_ref[...] = part

            @pl.when(i > 0)
            def _():
                dg_ref[...] += part

            dhn = du * gg
            gh_ref[...] = rs * (dhn - hn * jnp.mean(dhn * hn, axis=1, keepdims=True)) + do_ref[...]

    return pl.pallas_call(
        body, name="du_norm_bwd", grid=(T // tm, nk),
        in_specs=[pl.BlockSpec((tm, w), lambda i, k: (i, 0)) for w in DZ_WIDTHS]
        + [pl.BlockSpec((tm, C), lambda i, k: (i, 0)),
           pl.BlockSpec((tk, D), lambda i, k: (k, 0)), pl.BlockSpec((C, D), lambda i, k: (0, 0)),
           pl.BlockSpec((tm, D), lambda i, k: (i, 0)), pl.BlockSpec((1, D), lambda i, k: (0, 0)),
           pl.BlockSpec((tm, D), lambda i, k: (i, 0))],
        out_specs=[pl.BlockSpec((tm, D), lambda i, k: (i, 0)), pl.BlockSpec((1, D), lambda i, k: (0, 0))],
        out_shape=[jax.ShapeDtypeStruct((T, D), F32), jax.ShapeDtypeStruct((1, D), F32)],
        scratch_shapes=[pltpu.VMEM((tm, D), F32)],
        compiler_params=_params(("arbitrary", "arbitrary")),
    )(*dzs, dzf, wt, wft, hpad, g, dopad)


GROWS = 7424


def _dw_in(dzs, dzf, ut):
    tn = 256
    nmain = WMAIN // tn
    first, blocks = [], []
    for w in DZ_WIDTHS:
        first.append(sum(blocks))
        blocks.append(w // tn)

    def body(rq_ref, rk_ref, rv_ref, rg_ref, fq_ref, fk_ref, fv_ref, fg_ref, dzf_ref, ut_ref, o_ref):
        gidx = pl.program_id(0)
        for piece, g0, nb in zip((rq_ref, rk_ref, rv_ref, rg_ref, fq_ref, fk_ref, fv_ref, fg_ref), first, blocks):
            @pl.when((gidx >= g0) & (gidx < g0 + nb))
            def _(piece=piece):
                o_ref[...] = _dot(ut_ref[...], piece[...]).T

        @pl.when(gidx == nmain)
        def _():
            o_ref[:C, :] = _dot(ut_ref[...], dzf_ref[...]).T
            o_ref[C:, :] = jnp.zeros((tn - C, D), F32)

    def piece_spec(g0, nb):
        return pl.BlockSpec((T, tn), lambda gidx: (0, jnp.clip(gidx - g0, 0, nb - 1)))

    return pl.pallas_call(
        body, name="dw_in", grid=(nmain + 1,),
        in_specs=[piece_spec(g0, nb) for g0, nb in zip(first, blocks)]
        + [pl.BlockSpec((T, C), lambda gidx: (0, 0)), pl.BlockSpec((D, T), lambda gidx: (0, 0))],
        out_specs=pl.BlockSpec((tn, D), lambda gidx: (gidx, 0)),
        out_shape=jax.ShapeDtypeStruct((GROWS, D), F32),
        compiler_params=_params(("arbitrary",)),
    )(*dzs, dzf, ut)


def _token_order(x_po):
    def body(i_ref, o_ref):
        o_ref[...] = i_ref[...]

    return pl.pallas_call(
        body, name="token_order", grid=(NCH,),
        in_specs=[pl.BlockSpec((C, D), lambda i: (_fox_pos(i), 0))],
        out_specs=pl.BlockSpec((C, D), lambda i: (i, 0)),
        out_shape=jax.ShapeDtypeStruct((T, D), x_po.dtype),
        compiler_params=_params(("parallel",)),
    )(x_po)


def _local_step(x, tgt, meta, norm_g, wt, wft, b_f, wout, final_g):
    cst = _constants()
    hpad = jnp.concatenate([jnp.pad(meta, ((PAD, 0), (0, 0))), x], axis=0)
    bf_pad = jnp.pad(b_f, ((0, 0), (0, C - NFF)))
    u, ut = _norm_in(hpad, norm_g)
    z = _mm_nt(u, wt, WMAIN, T // 2, 512, "in_proj")
    zf = _mm_nt(u, wft, C, T // 2, C, "in_proj_ff")
    r, sprev = _ret_fwd(z, cst)
    cb, ct = _fox_prep(zf, bf_pad, cst)
    a, g = _fox_fwd(z, cb, ct, cst)
    yt, dopad, dob, loss8, dfg = _out_loss(r, z, a, wout, x, tgt, final_g)
    dr, da, dzrg, dzfg, delta = _dy_gate_bwd(dob, wout, r, z, a, cst["seg"])
    dwout = _mm_nn(yt, dob, 512, D, "dw_out")
    dzq_r, dzk_r, dzv_r = _ret_bwd(z, cst, sprev, dr)
    dq_po, drow, dzk_f, dzv_f, dcol = _fox_bwd(z, da, g, delta, ct, cst)
    dzf, dbf = _fox_gate_bwd(drow, dcol, zf, bf_pad, cst)
    dzs = [dzq_r, dzk_r, dzv_r, dzrg, _token_order(dq_po), dzk_f, dzv_f, dzfg]
    gwt = _dw_in(dzs, dzf, ut)
    gh, dng = _du_norm_bwd(dzs, dzf, wt, wft, hpad, norm_g, dopad)
    return (loss8[0, 0], gh[C:], gh[PAD:C], dng, gwt, dbf[:, :NFF], dwout, dfg)


def _place():
    x, y, c = lax.axis_index("x"), lax.axis_index("y"), lax.axis_index("c")
    return x, y, c


def _other_chips(x, y):
    return [(1 - x, y, 2 * (1 - x) + y), (x, 1 - y, 2 * x + (1 - y)), (1 - x, 1 - y, 2 * (1 - x) + (1 - y))]


def _all_gather_shards(shards):
    n = len(shards)

    def body(*refs):
        ins, outs = refs[:n], refs[n:2 * n]
        send_sems, recv_sems = refs[2 * n:]
        x, y, c = _place()
        me_s = 2 * x + y
        sib = (x, y, 1 - c)
        chips = _other_chips(x, y)
        sends, waits = [], []
        for a in range(n):
            rows = ins[a].shape[0] // 2
            half = pl.ds(c * rows, rows)
            for k, (cx, cy, cs) in enumerate(chips):
                sends.append(pltpu.make_async_remote_copy(
                    src_ref=ins[a].at[half], dst_ref=outs[a].at[me_s, half],
                    send_sem=send_sems.at[6 * a + k], recv_sem=recv_sems.at[6 * a + k],
                    device_id=(cx, cy, c), device_id_type=MESH))
                sends[-1].start()
        for a in range(n):
            rows = ins[a].shape[0] // 2
            half = pl.ds(c * rows, rows)
            other = pl.ds((1 - c) * rows, rows)
            for k, (cx, cy, cs) in enumerate(chips):
                pltpu.make_async_remote_copy(
                    src_ref=outs[a].at[cs, half], dst_ref=outs[a].at[cs, half],
                    send_sem=send_sems.at[6 * a + k], recv_sem=recv_sems.at[6 * a + k],
                    device_id=(cx, cy, c), device_id_type=MESH).wait_recv()
                fwd = pltpu.make_async_remote_copy(
                    src_ref=outs[a].at[cs, half], dst_ref=outs[a].at[cs, half],
                    send_sem=send_sems.at[6 * a + 3 + k], recv_sem=recv_sems.at[6 * a + 3 + k],
                    device_id=sib, device_id_type=MESH)
                fwd.start()
                sends.append(fwd)
                waits.append(pltpu.make_async_remote_copy(
                    src_ref=outs[a].at[cs, other], dst_ref=outs[a].at[cs, other],
                    send_sem=send_sems.at[6 * a + 3 + k], recv_sem=recv_sems.at[6 * a + 3 + k],
                    device_id=sib, device_id_type=MESH))
        for w in waits:
            w.wait_recv()
        for s in sends:
            s.wait_send()

    return pl.pallas_call(
        body, name="all_gather_w",
        in_specs=[ANY] * n, out_specs=[ANY] * n,
        out_shape=[jax.ShapeDtypeStruct((4,) + s.shape, s.dtype) for s in shards],
        scratch_shapes=[pltpu.SemaphoreType.DMA((6 * n,)), pltpu.SemaphoreType.DMA((6 * n,))],
    )(*shards)


WOFF, WLEN = 1792, 2048
WHALF = WLEN // 2


def _pair_swap(gwt, arrs, small):
    n = len(arrs)

    def body(*refs):
        gw, ins, sm = refs[0], refs[1:n + 1], refs[n + 1]
        gwo, outs, smo = refs[n + 2], refs[n + 3:2 * n + 3], refs[2 * n + 3]
        send_sems, recv_sems = refs[2 * n + 4:]
        x, y, c = _place()
        sib = (x, y, 1 - c)
        cps = []
        for k in range(4):
            cps.append(pltpu.make_async_remote_copy(
                src_ref=gw.at[pl.ds(WOFF * k + (1 - c) * WHALF, WHALF)], dst_ref=gwo.at[k],
                send_sem=send_sems.at[k], recv_sem=recv_sems.at[k], device_id=sib, device_id_type=MESH))
        for a in range(n):
            rows = ins[a].shape[1] // 2
            cps.append(pltpu.make_async_remote_copy(
                src_ref=ins[a].at[:, pl.ds((1 - c) * rows, rows)], dst_ref=outs[a],
                send_sem=send_sems.at[4 + a], recv_sem=recv_sems.at[4 + a], device_id=sib, device_id_type=MESH))
        cps.append(pltpu.make_async_remote_copy(
            src_ref=sm, dst_ref=smo, send_sem=send_sems.at[4 + n], recv_sem=recv_sems.at[4 + n],
            device_id=sib, device_id_type=MESH))
        for cp in cps:
            cp.start()
        for cp in cps:
            cp.wait()

    return pl.pallas_call(
        body, name="rs_pair_swap",
        in_specs=[ANY] * (n + 2), out_specs=[ANY] * (n + 2),
        out_shape=[jax.ShapeDtypeStruct((4, WHALF, D), gwt.dtype)]
        + [jax.ShapeDtypeStruct((4, a.shape[1] // 2, a.shape[2]), a.dtype) for a in arrs]
        + [jax.ShapeDtypeStruct(small.shape, small.dtype)],
        scratch_shapes=[pltpu.SemaphoreType.DMA((n + 5,)), pltpu.SemaphoreType.DMA((n + 5,))],
    )(gwt, *arrs, small)


def _add_windows(gwt, recv):
    tb = 256
    nb = WHALF // tb
    c = lax.axis_index("c")

    def body(c_ref, a_ref, b_ref, o_ref):
        o_ref[0] = (a_ref[...] + b_ref[0]).astype(BF)

    return pl.pallas_call(
        body, name="pair_add_in",
        grid_spec=pltpu.PrefetchScalarGridSpec(
            num_scalar_prefetch=1, grid=(4, nb),
            in_specs=[pl.BlockSpec((tb, D), lambda k, i, cr: ((WOFF // tb) * k + nb * cr[0] + i, 0)),
                      pl.BlockSpec((1, tb, D), lambda k, i, cr: (k, i, 0))],
            out_specs=pl.BlockSpec((1, tb, D), lambda k, i, cr: (k, i, 0))),
        out_shape=jax.ShapeDtypeStruct(recv.shape, BF),
        compiler_params=_params(("parallel", "parallel")),
    )(jnp.reshape(c, (1,)).astype(jnp.int32), gwt, recv)


def _chip_exchange(parts, small):
    n = len(parts)

    def body(*refs):
        ins, sm = refs[:n], refs[n]
        outs, smo = refs[n + 1:2 * n + 1], refs[2 * n + 1]
        send_sems, recv_sems = refs[2 * n + 2:]
        x, y, c = _place()
        me_s = 2 * x + y
        chips = _other_chips(x, y)
        cps = []
        for a in range(n + 1):
            src = ins[a] if a < n else sm
            dst = outs[a] if a < n else smo
            for k, (cx, cy, cs) in enumerate(chips):
                cps.append(pltpu.make_async_remote_copy(
                    src_ref=src.at[cs] if a < n else src, dst_ref=dst.at[me_s],
                    send_sem=send_sems.at[3 * a + k], recv_sem=recv_sems.at[3 * a + k],
                    device_id=(cx, cy, c), device_id_type=MESH))
        for cp in cps:
            cp.start()
        for cp in cps:
            cp.wait()

    return pl.pallas_call(
        body, name="rs_chip_exchange",
        in_specs=[ANY] * (n + 1), out_specs=[ANY] * (n + 1),
        out_shape=[jax.ShapeDtypeStruct(p.shape, p.dtype) for p in parts]
        + [jax.ShapeDtypeStruct((4,) + small.shape, small.dtype)],
        scratch_shapes=[pltpu.SemaphoreType.DMA((3 * (n + 1),)), pltpu.SemaphoreType.DMA((3 * (n + 1),))],
    )(*parts, small)


def _pair_send(halves):
    n = len(halves)

    def body(*refs):
        ins, outs = refs[:n], refs[n:2 * n]
        send_sems, recv_sems = refs[2 * n:]
        x, y, c = _place()
        cps = [pltpu.make_async_remote_copy(
            src_ref=ins[a], dst_ref=outs[a], send_sem=send_sems.at[a], recv_sem=recv_sems.at[a],
            device_id=(x, y, 1 - c), device_id_type=MESH) for a in range(n)]
        for cp in cps:
            cp.start()
        for cp in cps:
            cp.wait()

    return pl.pallas_call(
        body, name="rs_pair_send",
        in_specs=[ANY] * n, out_specs=[ANY] * n,
        out_shape=[jax.ShapeDtypeStruct(h.shape, h.dtype) for h in halves],
        scratch_shapes=[pltpu.SemaphoreType.DMA((n,)), pltpu.SemaphoreType.DMA((n,))],
    )(*halves)


def _row_block(rows):
    for tb in (256, 128, 64, 32, 16, 8):
        if rows % tb == 0:
            return tb
    return rows


def _add_halves(full, recv, name, out_dtype):
    _, r2, w = recv.shape
    tb = _row_block(r2)
    nb = r2 // tb
    c = lax.axis_index("c")

    def body(c_ref, a_ref, b_ref, o_ref):
        o_ref[...] = (a_ref[...] + b_ref[...]).astype(o_ref.dtype)

    return pl.pallas_call(
        body, name=name,
        grid_spec=pltpu.PrefetchScalarGridSpec(
            num_scalar_prefetch=1, grid=(4, nb),
            in_specs=[pl.BlockSpec((1, tb, w), lambda s, i, cr: (s, cr[0] * nb + i, 0)),
                      pl.BlockSpec((1, tb, w), lambda s, i, cr: (s, i, 0))],
            out_specs=pl.BlockSpec((1, tb, w), lambda s, i, cr: (s, i, 0))),
        out_shape=jax.ShapeDtypeStruct(recv.shape, out_dtype),
        compiler_params=_params(("parallel", "parallel")),
    )(jnp.reshape(c, (1,)).astype(jnp.int32), full, recv)


def _add2(a, b, name):
    def body(a_ref, b_ref, o_ref):
        o_ref[...] = a_ref[...] + b_ref[...]

    return pl.pallas_call(body, name=name, out_shape=jax.ShapeDtypeStruct(a.shape, a.dtype))(a, b)


def _sum4(buf, own, name):
    _, r, w = buf.shape
    tb = _row_block(r)
    me_s = 2 * lax.axis_index("x") + lax.axis_index("y")
    by_dest = own.ndim == 3

    def body(s_ref, b_ref, own_ref, o_ref):
        mine = (own_ref[0] if by_dest else own_ref[...]).astype(F32)
        terms = [jnp.where(s_ref[0] == t, mine, b_ref[t].astype(F32)) for t in range(4)]
        o_ref[...] = ((terms[0] + terms[1]) + terms[2]) + terms[3]

    own_spec = (pl.BlockSpec((1, tb, w), lambda i, sr: (sr[0], i, 0)) if by_dest
                else pl.BlockSpec((tb, w), lambda i, sr: (i, 0)))
    return pl.pallas_call(
        body, name=name,
        grid_spec=pltpu.PrefetchScalarGridSpec(
            num_scalar_prefetch=1, grid=(r // tb,),
            in_specs=[pl.BlockSpec((4, tb, w), lambda i, sr: (0, i, 0)), own_spec],
            out_specs=pl.BlockSpec((tb, w), lambda i, sr: (i, 0))),
        out_shape=jax.ShapeDtypeStruct((r, w), F32),
        compiler_params=_params(("parallel",)),
    )(jnp.reshape(me_s, (1,)).astype(jnp.int32), buf, own)


def _adamw_math(w, g, m, v):
    mn = B1 * m + (1.0 - B1) * g
    vn = B2 * v + (1.0 - B2) * (g * g)
    m_hat = mn / (1.0 - B1 ** STEP)
    v_hat = vn / (1.0 - B2 ** STEP)
    return -LR * (m_hat / (jnp.sqrt(v_hat) + AEPS) + WD * w), mn, vn


def _adamw(w, g, m, v, name):
    r, c_ = w.shape
    tb = _row_block(r)
    if tb == r and r > 512:
        tb = 256

    def body(w_ref, g_ref, m_ref, v_ref, d_ref, mo_ref, vo_ref):
        d_ref[...], mo_ref[...], vo_ref[...] = _adamw_math(w_ref[...], g_ref[...], m_ref[...], v_ref[...])

    spec = pl.BlockSpec((tb, c_), lambda i: (i, 0))
    return pl.pallas_call(
        body, name=name, grid=(pl.cdiv(r, tb),),
        in_specs=[spec] * 4, out_specs=[spec] * 3,
        out_shape=[jax.ShapeDtypeStruct(w.shape, F32)] * 3,
        compiler_params=_params(("parallel",)),
    )(w, g, m, v)


def _adamw_halves(w, g_mine, g_sib, m, v, name):
    r, c_ = w.shape
    r2 = g_mine.shape[0]
    tb = _row_block(r2)
    nb = r2 // tb
    c = lax.axis_index("c")

    def body(c_ref, w_ref, gm_ref, gs_ref, m_ref, v_ref, g_ref, d_ref, mo_ref, vo_ref):
        g = jnp.where(pl.program_id(0) == c_ref[0], gm_ref[...], gs_ref[...])
        g_ref[...] = g
        d_ref[...], mo_ref[...], vo_ref[...] = _adamw_math(w_ref[...], g, m_ref[...], v_ref[...])

    full = pl.BlockSpec((tb, c_), lambda h, i, cr: (h * nb + i, 0))
    half = pl.BlockSpec((tb, c_), lambda h, i, cr: (i, 0))
    return pl.pallas_call(
        body, name=name,
        grid_spec=pltpu.PrefetchScalarGridSpec(
            num_scalar_prefetch=1, grid=(2, nb),
            in_specs=[full, half, half, full, full], out_specs=[full] * 4),
        out_shape=[jax.ShapeDtypeStruct(w.shape, F32)] * 4,
        compiler_params=_params(("parallel", "parallel")),
    )(jnp.reshape(c, (1,)).astype(jnp.int32), w, g_mine, g_sib, m, v)


def kernel(x, meta_tokens, norm_g, w_in, b_f, w_out, final_g, loss_target, m_meta_tokens, m_norm_g, m_w_in, m_b_f, m_w_out, m_final_g, v_meta_tokens, v_norm_g, v_w_in, v_b_f, v_w_out, v_final_g):
    me_s = 2 * lax.axis_index("x") + lax.axis_index("y")
    core = lax.axis_index("c")
    wt, mt, vt = [jnp.swapaxes(t[0], 0, 1) for t in (w_in, m_w_in, v_w_in)]

    own_win = lax.dynamic_update_slice(jnp.zeros((WPADROWS, D), F32), wt, (4 * me_s, 0)).astype(BF)
    own = [own_win, w_out[0].astype(BF), meta_tokens]
    gathered = _all_gather_shards(own)
    mine = jnp.arange(4) == me_s
    win, gout, gmeta = [jnp.where(mine[:, None, None], o[None], g) for o, g in zip(own, gathered)]
    lap = WPADROWS - WOFF
    tails = jnp.concatenate([jnp.zeros((1, lap, D), BF), win[:-1, WOFF:]], axis=0)
    wt_main = jnp.concatenate([win[:, :lap] + tails, win[:, lap:WOFF]], axis=1).reshape(WMAIN, D)
    wft = jnp.pad(win[3, WOFF:WOFF + NFF], ((0, C - NFF), (0, 0)))
    wout = gout.reshape(DMIX, D)
    meta = jnp.concatenate([gmeta[s] for s in range(4)], axis=1)

    loss, gx, dmeta, dng, gwt, dbf, dwout, dfg = _local_step(
        x[0], loss_target[0], meta, norm_g, wt_main, wft, b_f, wout, final_g.reshape(1, D))

    g_out = dwout.reshape(4, DMIX // 4, D)
    g_meta = jnp.stack([dmeta[:, 256 * s:256 * (s + 1)] for s in range(4)])
    small = jnp.concatenate([dng, dfg, jnp.pad(dbf, ((0, 0), (0, D - NFF))),
                             jnp.pad(jnp.reshape(loss, (1, 1)), ((0, 0), (0, D - 1))),
                             jnp.zeros((4, D), F32)], axis=0)
    r_in, r_out, r_meta, r_small = _pair_swap(gwt, [g_out, g_meta], small)
    p_in = _add_windows(gwt, r_in)
    p_out = _add_halves(g_out, r_out, "pair_add_out", BF)
    p_meta = _add_halves(g_meta, r_meta, "pair_add_meta", F32)
    p_small = _add2(small, r_small, "pair_add_small")
    e_in, e_out, e_meta, e_small = _chip_exchange([p_in, p_out, p_meta], p_small)
    h_in, h_out, h_meta = _sum4(e_in, p_in, "sum_in"), _sum4(e_out, p_out, "sum_out"), _sum4(e_meta, p_meta, "sum_meta")
    tot = _sum4(e_small, p_small, "sum_small")
    s_in, s_out, s_meta = _pair_send([h_in, h_out, h_meta])
    g_norm, g_final, g_bf, loss_all = tot[0:1], tot[1], tot[2:3, :NFF], tot[3, 0]

    gw_meta, d_meta, nm_meta, nv_meta = _adamw_halves(meta_tokens, h_meta, s_meta, m_meta_tokens, v_meta_tokens,
                                                      "adamw_meta")
    d_norm, nm_norm, nv_norm = _adamw(norm_g, g_norm, m_norm_g, v_norm_g, "adamw_norm")
    window = jnp.concatenate([jnp.where(core == 0, h_in, s_in), jnp.where(core == 0, s_in, h_in)], axis=0)
    gwt_own = lax.dynamic_slice(window, (4 * me_s, 0), (WSH, D))
    d_in, nm_in, nv_in = _adamw(wt, gwt_own, mt, vt, "adamw_in")
    gw_in, d_in, nm_in, nv_in = [jnp.swapaxes(t, 0, 1)[None] for t in (gwt_own, d_in, nm_in, nv_in)]
    d_bf, nm_bf, nv_bf = _adamw(b_f, g_bf, m_b_f, v_b_f, "adamw_bf")
    gw_out, d_out, nm_out, nv_out = _adamw_halves(w_out[0], h_out, s_out, m_w_out[0], v_w_out[0], "adamw_out")
    d_fin, nm_fin, nv_fin = _adamw(final_g.reshape(1, D), g_final.reshape(1, D), m_final_g.reshape(1, D),
                                   v_final_g.reshape(1, D), "adamw_final")
    return (loss_all, gx[None], gw_meta, g_norm, gw_in, g_bf, gw_out[None], g_final,
            d_meta, d_norm, d_in, d_bf, d_out[None], d_fin.reshape(D),
            nm_meta, nm_norm, nm_in, nm_bf, nm_out[None], nm_fin.reshape(D),
            nv_meta, nv_norm, nv_in, nv_bf, nv_out[None], nv_fin.reshape(D))
```

```python
import numpy as np
import jax
import jax.numpy as jnp
from jax import lax
from jax.experimental import pallas as pl
from jax.experimental.pallas import tpu as pltpu

D = 1024
SEQ = 2048
NMETA = 16
C = 128
PAD = C - NMETA
T = PAD + NMETA + SEQ
NCH = T // C
RH, RDK, RDV = 4, 128, 256
FH, FD = 16, 64
NPAIR = FH // 2
WMAIN = 7168
NFF = 16
WIN = WMAIN + NFF
WSH = WIN // 4
WPADROWS = 1824
DMIX = 2048
EPS = 1e-6
NEG = -1e30
RSCALE = RDK ** -0.5
FSCALE = FD ** -0.5
ROPE_BASE = 10000.0
LR, B1, B2, AEPS, WD, STEP = 0.001, 0.9, 0.999, 1e-08, 0.01, 10

BF = jnp.bfloat16
F32 = jnp.float32
NT = (((1,), (1,)), ((), ()))
TN = (((0,), (0,)), ((), ()))
HI = lax.Precision.HIGHEST
MESH = pl.DeviceIdType.MESH
ANY = pl.BlockSpec(memory_space=pl.ANY)
VMEM_LIMIT = 48 * 1024 * 1024

QB_R, KB_R = 0, 4
VB_R = 4
GB_R, GB_F = 2, 6
QB_F, KB_F, VB_F = 24, 32, 40


def _dot(a, b):
    return jnp.dot(a, b, preferred_element_type=F32)


def _dg(a, b, dims):
    return lax.dot_general(a, b, dims, preferred_element_type=F32)


def _params(sem=None):
    return pltpu.CompilerParams(dimension_semantics=sem, vmem_limit_bytes=VMEM_LIMIT)


def _constants():
    pos = jnp.arange(T, dtype=F32) - PAD
    inv = ROPE_BASE ** (-jnp.arange(0, RDK, 2, dtype=F32) / RDK)
    ang = pos[:, None] * inv[None, :]
    cos, sin = jnp.cos(ang), jnp.sin(ang)
    cos2 = jnp.concatenate([cos, cos], axis=1)
    sin2 = jnp.concatenate([-sin, sin], axis=1)
    log_gamma = jnp.log1p(-jnp.exp2(-5.0 - jnp.arange(RH, dtype=F32)))
    idx = jnp.arange(C, dtype=F32)
    diff = idx[:, None] - idx[None, :]
    dmask = jnp.where(diff[None] >= 0, jnp.exp(log_gamma[:, None, None] * jnp.maximum(diff, 0.0)[None]), 0.0)
    zeta = jnp.exp(log_gamma[:, None] * (C - 1.0 - idx)[None, :])
    xi = jnp.exp(log_gamma[:, None] * (idx + 1.0)[None, :])
    gdec = jnp.exp(log_gamma * C)
    zeta_b = jnp.broadcast_to(zeta[:, :, None], (RH, C, RDK))
    xi_b = jnp.broadcast_to(xi[:, :, None], (RH, C, RDK))
    gdec_b = jnp.broadcast_to(gdec[:, None, None], (RH, RDK, RDV))
    tri = jnp.asarray(np.tril(np.ones((C, C), np.float32)))
    head_of_lane = np.arange(FH * FD) // FD
    spread = (np.arange(C)[:, None] == head_of_lane[None, :]).astype(np.float32)
    pick = ((np.arange(FH * FD)[:, None] % FD == 0)
            & (head_of_lane[:, None] == np.arange(C)[None, :])).astype(np.float32)
    seg = (np.arange(C)[:, None] // FD == np.arange(C)[None, :] // FD).astype(np.float32)
    ones_aug = np.concatenate([np.tile((np.arange(C) < FD)[None, :], (C, 1)),
                               np.tile((np.arange(C) >= FD)[None, :], (C, 1))], axis=0).astype(np.float32)
    lane = np.arange(2 * C) % C
    causal = np.where(lane[None, :] <= np.arange(C)[:, None], 0.0, NEG).astype(np.float32)
    mask_bias = np.stack([np.zeros((C, 2 * C), np.float32), causal, np.full((C, 2 * C), NEG, np.float32)])
    return dict(cos2=cos2, sin2=sin2, dmask=dmask, zeta=zeta_b, xi=xi_b, gdec=gdec_b, tri=tri,
                mask_bias=jnp.asarray(mask_bias),
                spread=jnp.asarray(spread), pick=jnp.asarray(pick), seg=jnp.asarray(seg, dtype=BF),
                ones_aug=jnp.asarray(ones_aug, dtype=BF))


def _norm_in(hpad, g):
    def body(h_ref, g_ref, u_ref, ut_ref):
        h = h_ref[...]
        rs = lax.rsqrt(jnp.mean(h * h, axis=1, keepdims=True) + EPS)
        u = h * rs * g_ref[...]
        u_ref[...] = u.astype(BF)
        ut_ref[...] = u.T.astype(BF)

    return pl.pallas_call(
        body, name="norm_in", grid=(NCH,),
        in_specs=[pl.BlockSpec((C, D), lambda i: (i, 0)), pl.BlockSpec((1, D), lambda i: (0, 0))],
        out_specs=[pl.BlockSpec((C, D), lambda i: (i, 0)), pl.BlockSpec((D, C), lambda i: (0, i))],
        out_shape=[jax.ShapeDtypeStruct((T, D), BF), jax.ShapeDtypeStruct((D, T), BF)],
        compiler_params=_params(("parallel",)),
    )(hpad, g)


def _mm_nt(a, b, n, tm, tn, name):
    m, k = a.shape

    def body(a_ref, b_ref, o_ref):
        o_ref[...] = _dg(a_ref[...], b_ref[...], NT)

    return pl.pallas_call(
        body, name=name, grid=(m // tm, n // tn),
        in_specs=[pl.BlockSpec((tm, k), lambda i, j: (i, 0)), pl.BlockSpec((tn, k), lambda i, j: (j, 0))],
        out_specs=pl.BlockSpec((tm, tn), lambda i, j: (i, j)),
        out_shape=jax.ShapeDtypeStruct((m, n), F32),
        compiler_params=_params(("parallel", "parallel")),
    )(a, b)


def _mm_nn(a, b, tm, tn, name):
    m, k = a.shape
    _, n = b.shape

    def body(a_ref, b_ref, o_ref):
        o_ref[...] = _dot(a_ref[...], b_ref[...])

    return pl.pallas_call(
        body, name=name, grid=(m // tm, n // tn),
        in_specs=[pl.BlockSpec((tm, k), lambda i, j: (i, 0)), pl.BlockSpec((k, tn), lambda i, j: (0, j))],
        out_specs=pl.BlockSpec((tm, tn), lambda i, j: (i, j)),
        out_shape=jax.ShapeDtypeStruct((m, n), F32),
        compiler_params=_params(("parallel", "parallel")),
    )(a, b)


def _rot(x, cos2, sin2):
    return x * cos2 + pltpu.roll(x, 64, 1) * sin2


def _ret_specs(chunk):
    whole = lambda shape: pl.BlockSpec(shape, lambda n: (0,) * len(shape))
    return [
        pl.BlockSpec((C, RH * RDK), lambda n: (chunk(n), 0)),
        pl.BlockSpec((C, RH * RDK), lambda n: (chunk(n), 1)),
        pl.BlockSpec((C, RH * RDV), lambda n: (chunk(n), 1)),
        pl.BlockSpec((C, RDK), lambda n: (chunk(n), 0)),
        pl.BlockSpec((C, RDK), lambda n: (chunk(n), 0)),
        whole((RH, C, C)), whole((RH, C, RDK)), whole((RH, C, RDK)), whole((RH, RDK, RDV)),
    ]


def _ret_heads(q_ref, k_ref, v_ref, cos, sin):
    qr = [_rot(q_ref[:, RDK * h:RDK * (h + 1)], cos, sin) for h in range(RH)]
    kr = [_rot(k_ref[:, RDK * h:RDK * (h + 1)], cos, sin) * RSCALE for h in range(RH)]
    vb = [v_ref[:, RDV * h:RDV * (h + 1)].astype(BF) for h in range(RH)]
    return qr, kr, [t.astype(BF) for t in qr], [t.astype(BF) for t in kr], vb


def _ret_fwd(z, cst):
    def body(q_ref, k_ref, v_ref, cos_ref, sin_ref, dm_ref, xi_ref, zt_ref, gd_ref, r_ref, sp_ref, st):
        n = pl.program_id(0)

        @pl.when(n == 0)
        def _():
            st[...] = jnp.zeros_like(st)

        hs = range(RH)
        qr, kr, qb, kb, vb = _ret_heads(q_ref, k_ref, v_ref, cos_ref[...], sin_ref[...])
        sd = [(_dg(qb[h], kb[h], NT) * dm_ref[h]).astype(BF) for h in hs]
        state = [st[h] for h in hs]
        qx = [(qr[h] * xi_ref[h]).astype(BF) for h in hs]
        kz = [(kr[h] * zt_ref[h]).astype(BF) for h in hs]
        out = [_dot(sd[h], vb[h]) + _dot(qx[h], state[h].astype(BF)) for h in hs]
        kv = [_dg(kz[h], vb[h], TN) for h in hs]
        for h in hs:
            sp_ref[0, h] = state[h]
            r_ref[:, RDV * h:RDV * (h + 1)] = out[h]
            st[h] = state[h] * gd_ref[h] + kv[h]

    return pl.pallas_call(
        body, name="ret_fwd", grid=(NCH,),
        in_specs=_ret_specs(lambda n: n),
        out_specs=[pl.BlockSpec((C, RH * RDV), lambda n: (n, 0)),
                   pl.BlockSpec((1, RH, RDK, RDV), lambda n: (n, 0, 0, 0))],
        out_shape=[jax.ShapeDtypeStruct((T, RH * RDV), F32), jax.ShapeDtypeStruct((NCH, RH, RDK, RDV), F32)],
        scratch_shapes=[pltpu.VMEM((RH, RDK, RDV), F32)],
        compiler_params=_params(("arbitrary",)),
    )(z, z, z, cst["cos2"], cst["sin2"], cst["dmask"], cst["xi"], cst["zeta"], cst["gdec"])


def _ret_bwd(z, cst, sprev, dr):
    def body(q_ref, k_ref, v_ref, cos_ref, sin_ref, dm_ref, xi_ref, zt_ref, gd_ref, sp_ref, dr_ref,
             dq_ref, dk_ref, dv_ref, gst):
        i = pl.program_id(0)

        @pl.when(i == 0)
        def _():
            gst[...] = jnp.zeros_like(gst)

        hs = range(RH)
        cos, sin = cos_ref[...], sin_ref[...]
        qr, kr, qb, kb, vb = _ret_heads(q_ref, k_ref, v_ref, cos, sin)
        dm = [dm_ref[h] for h in hs]
        xi = [xi_ref[h] for h in hs]
        zt = [zt_ref[h] for h in hs]
        sd = [(_dg(qb[h], kb[h], NT) * dm[h]).astype(BF) for h in hs]
        qx = [(qr[h] * xi[h]).astype(BF) for h in hs]
        kz = [(kr[h] * zt[h]).astype(BF) for h in hs]
        drb = [dr_ref[:, RDV * h:RDV * (h + 1)] for h in hs]
        sb = [sp_ref[0, h].astype(BF) for h in hs]
        g = [gst[h] for h in hs]
        gb = [t.astype(BF) for t in g]
        ds = [(_dg(drb[h], vb[h], NT) * dm[h]).astype(BF) for h in hs]
        dq = [_dot(ds[h], kb[h]) + _dg(drb[h], sb[h], NT) * xi[h] for h in hs]
        dk = [(_dg(ds[h], qb[h], TN) + _dg(vb[h], gb[h], NT) * zt[h]) * RSCALE for h in hs]
        dv = [_dg(sd[h], drb[h], TN) + _dot(kz[h], gb[h]) for h in hs]
        gn = [g[h] * gd_ref[h] + _dg(qx[h], drb[h], TN) for h in hs]
        for h in hs:
            gst[h] = gn[h]
            dq_ref[:, RDK * h:RDK * (h + 1)] = (dq[h] * cos + pltpu.roll(dq[h] * sin, 64, 1)).astype(BF)
            dk_ref[:, RDK * h:RDK * (h + 1)] = (dk[h] * cos + pltpu.roll(dk[h] * sin, 64, 1)).astype(BF)
            dv_ref[:, RDV * h:RDV * (h + 1)] = dv[h].astype(BF)

    rev = lambda n: NCH - 1 - n
    return pl.pallas_call(
        body, name="ret_bwd", grid=(NCH,),
        in_specs=_ret_specs(rev) + [
            pl.BlockSpec((1, RH, RDK, RDV), lambda n: (rev(n), 0, 0, 0)),
            pl.BlockSpec((C, RH * RDV), lambda n: (rev(n), 0)),
        ],
        out_specs=[pl.BlockSpec((C, RH * RDK), lambda n: (rev(n), 0)),
                   pl.BlockSpec((C, RH * RDK), lambda n: (rev(n), 0)),
                   pl.BlockSpec((C, RH * RDV), lambda n: (rev(n), 0))],
        out_shape=[jax.ShapeDtypeStruct((T, RH * RDK), BF), jax.ShapeDtypeStruct((T, RH * RDK), BF),
                   jax.ShapeDtypeStruct((T, RH * RDV), BF)],
        scratch_shapes=[pltpu.VMEM((RH, RDK, RDV), F32)],
        compiler_params=_params(("arbitrary",)),
    )(z, z, z, cst["cos2"], cst["sin2"], cst["dmask"], cst["xi"], cst["zeta"], cst["gdec"], sprev, dr)


def _log_sigmoid(x):
    return -(jnp.maximum(-x, 0.0) + jnp.log1p(jnp.exp(-jnp.abs(x))))


def _fox_prep(zf, bf_pad, cst):
    def body(zf_ref, b_ref, tri_ref, spread_ref, cb_ref, ct_ref, carry):
        n = pl.program_id(0)

        @pl.when(n == 0)
        def _():
            carry[...] = jnp.zeros_like(carry)

        ls = _log_sigmoid(zf_ref[...] + b_ref[...])
        row = n * C + lax.broadcasted_iota(jnp.int32, (C, C), 0)
        lf = jnp.where(row >= PAD, ls, 0.0)
        cc = jnp.dot(tri_ref[...], lf, precision=HI, preferred_element_type=F32) + carry[0:1, :]
        carry[...] = jnp.broadcast_to(cc[C - 1:C, :], carry.shape)
        cb_ref[...] = jnp.dot(cc, spread_ref[...], precision=HI, preferred_element_type=F32)
        pos = n * C + lax.broadcasted_iota(jnp.int32, (FH, C), 1)
        ct_ref[0] = jnp.where(pos >= PAD, cc.T[:FH, :], -NEG)

    return pl.pallas_call(
        body, name="fox_prep", grid=(NCH,),
        in_specs=[pl.BlockSpec((C, C), lambda n: (n, 0)), pl.BlockSpec((1, C), lambda n: (0, 0)),
                  pl.BlockSpec((C, C), lambda n: (0, 0)), pl.BlockSpec((C, FH * FD), lambda n: (0, 0))],
        out_specs=[pl.BlockSpec((C, FH * FD), lambda n: (_fox_pos(n), 0)),
                   pl.BlockSpec((1, FH, C), lambda n: (n, 0, 0))],
        out_shape=[jax.ShapeDtypeStruct((TROWS, FH * FD), F32), jax.ShapeDtypeStruct((NCH, FH, C), F32)],
        scratch_shapes=[pltpu.VMEM((8, C), F32)],
        compiler_params=_params(("arbitrary",)),
    )(zf, bf_pad, cst["tri"], cst["spread"])


def _lo_lanes(shape):
    return lax.broadcasted_iota(jnp.int32, shape, 1) < FD


def _split_heads(x):
    lo = _lo_lanes(x.shape)
    zero = jnp.zeros_like(x)
    return jnp.concatenate([jnp.where(lo, x, zero), jnp.where(lo, zero, x)], axis=0)


def _spread2(x):
    lo = _lo_lanes(x.shape)
    r = pltpu.roll(x, FD, 1)
    return jnp.concatenate([jnp.where(lo, x, r), jnp.where(lo, r, x)], axis=1)


NSTEP = (NCH + 1) // 2
NTILE = NCH + 1
TROWS = T + C


def _fox_tile(s, t):
    second = t > s
    j = jnp.where(second, t - s - 1, t)
    iq = jnp.where(second, NCH - 1 - s, s)
    kind = jnp.where(second & (s == NSTEP - 1), 2, (j == iq).astype(jnp.int32))
    return second.astype(jnp.int32), j, kind


def _fox_pos(i):
    return jnp.where(i < NSTEP, 2 * i, 2 * (NCH - 1 - i) + 1)


FOX_ORDER = [2 * i if i < NSTEP else 2 * (NCH - 1 - i) + 1 for i in range(NCH)]


def _fox_pair_specs():
    first = pl.BlockSpec((C, C), lambda p, s: (2 * s, p))
    second = pl.BlockSpec((C, C), lambda p, s: (jnp.where(s == NSTEP - 1, 2 * s, 2 * s + 1), p))
    both = pl.BlockSpec((2 * C, C), lambda p, s: (s, p))
    return first, second, both


def _fox_q_specs():
    return (pl.BlockSpec((C, C), lambda p, s: (s, QB_F + p)),
            pl.BlockSpec((C, C), lambda p, s: (NCH - 1 - s, QB_F + p)))


def _fox_key_bias(ct_ref, p, j):
    return jnp.concatenate([ct_ref[j, pl.ds(2 * p, 1), :], ct_ref[j, pl.ds(2 * p + 1, 1), :]], axis=1)


def _fox_fwd(z, cb, ct, cst):
    def body(qa_ref, qb_ref, k_ref, v_ref, ca_ref, cb_ref, ct_ref, ones_ref, mb_ref,
             a_ref, g_ref, kks, vvs, q2, ci2, m2, sbuf):
        p, s = pl.program_id(0), pl.program_id(1)

        @pl.when(s == 0)
        def _():
            ones = ones_ref[...]

            def prep(j, carry):
                rows = pl.ds(pl.multiple_of(j * C, C), C)
                kks[j] = _split_heads(k_ref[rows, :]).astype(BF)
                vvs[j] = jnp.concatenate([_split_heads(v_ref[rows, :]).astype(BF), ones], axis=1)
                return carry

            lax.fori_loop(0, NCH, prep, 0)

        for w, (q_ref, c_ref) in enumerate(((qa_ref, ca_ref), (qb_ref, cb_ref))):
            q2[w] = (q_ref[...] * FSCALE).astype(BF)
            ci2[w] = _spread2(c_ref[...])

        tiles = [_fox_tile(s, t) for t in range(NTILE)]
        neg = jnp.full((C, 2 * C), NEG, F32)
        mx = [neg, neg]
        for t, (sel, j, kind) in enumerate(tiles):
            st = _dg(q2[sel], kks[j], NT) + ((ci2[sel] - _fox_key_bias(ct_ref, p, j)) + mb_ref[kind])
            sbuf[t] = st
            mx = [jnp.maximum(mx[0], jnp.where(t <= s, st, neg)), jnp.maximum(mx[1], jnp.where(t <= s, neg, st))]
        for w in range(2):
            m2[w] = jnp.concatenate(
                [jnp.broadcast_to(jnp.max(mx[w][:, :C], axis=1, keepdims=True), (C, C)),
                 jnp.broadcast_to(jnp.max(mx[w][:, C:], axis=1, keepdims=True), (C, C))], axis=1)

        zero = jnp.zeros((C, 2 * C), F32)
        acc = [zero, zero]
        for t, (sel, j, _) in enumerate(tiles):
            part = _dot(jnp.exp(sbuf[t] - m2[sel]).astype(BF), vvs[j])
            acc = [acc[0] + jnp.where(t <= s, part, zero), acc[1] + jnp.where(t <= s, zero, part)]
        lo = _lo_lanes((C, C))
        for w, c_ref in enumerate((ca_ref, cb_ref)):
            res = acc[w]
            l = res[:, C:]
            a_ref[C * w:C * (w + 1), :] = res[:, :C] / l
            mw = m2[w]
            g_ref[C * w:C * (w + 1), :] = c_ref[...] - (jnp.where(lo, mw[:, :C], mw[:, C:]) + jnp.log(l))

    qa, qb = _fox_q_specs()
    ca, cbs, both = _fox_pair_specs()
    return pl.pallas_call(
        body, name="fox_fwd", grid=(NPAIR, NSTEP),
        in_specs=[qa, qb,
                  pl.BlockSpec((T, C), lambda p, s: (0, KB_F + p)),
                  pl.BlockSpec((T, C), lambda p, s: (0, VB_F + p)),
                  ca, cbs,
                  pl.BlockSpec((NCH, FH, C), lambda p, s: (0, 0, 0)),
                  pl.BlockSpec((2 * C, C), lambda p, s: (0, 0)),
                  pl.BlockSpec((3, C, 2 * C), lambda p, s: (0, 0, 0))],
        out_specs=[both, both],
        out_shape=[jax.ShapeDtypeStruct((TROWS, FH * FD), F32)] * 2,
        scratch_shapes=[pltpu.VMEM((NCH, 2 * C, C), BF), pltpu.VMEM((NCH, 2 * C, 2 * C), BF),
                        pltpu.VMEM((2, C, C), BF), pltpu.VMEM((2, C, 2 * C), F32), pltpu.VMEM((2, C, 2 * C), F32),
                        pltpu.VMEM((NTILE, C, 2 * C), F32)],
        compiler_params=_params(("parallel", "arbitrary")),
    )(z, z, z, z, cb, cb, ct, cst["ones_aug"], cst["mask_bias"])


def _fox_bwd(z, da, g, delta, ct, cst):
    grp = 9

    def body(qa_ref, qb_ref, daa_ref, dab_ref, ga_ref, gb_ref, dla_ref, dlb_ref, k_ref, v_ref, ct_ref, ones_ref,
             mb_ref, dq_ref, dr_ref, dk_ref, dv_ref, dcs_ref,
             kks, vvs, q2, qq2, dd2, da2, gi2, dl2, dq2, dvb, dkb, dkacc, dvacc, csacc):
        p, s = pl.program_id(0), pl.program_id(1)
        ones = ones_ref[...]

        @pl.when(s == 0)
        def _():
            dkacc[...] = jnp.zeros_like(dkacc)
            dvacc[...] = jnp.zeros_like(dvacc)
            csacc[...] = jnp.zeros_like(csacc)

            def prep(j, carry):
                rows = pl.ds(pl.multiple_of(j * C, C), C)
                kks[j] = _split_heads(k_ref[rows, :]).astype(BF)
                vvs[j] = _split_heads(v_ref[rows, :]).astype(BF)
                return carry

            lax.fori_loop(0, NCH, prep, 0)

        for w, (q_ref, d_ref, g_ref, l_ref) in enumerate(((qa_ref, daa_ref, ga_ref, dla_ref),
                                                          (qb_ref, dab_ref, gb_ref, dlb_ref))):
            qf = q_ref[...]
            q2[w] = (qf * FSCALE).astype(BF)
            qq2[w] = jnp.concatenate([_split_heads(qf).astype(BF), ones], axis=1)
            da2[w] = d_ref[...]
            dd2[w] = _split_heads(d_ref[...].astype(F32)).astype(BF)
            gi2[w] = _spread2(g_ref[...])
            dl2[w] = _spread2(l_ref[...])
        dq2[...] = jnp.zeros_like(dq2)
        zero = jnp.zeros((C, 2 * C), F32)

        def group(gi, carry):
            ts = [gi * grp + u for u in range(grp)]
            tiles = [_fox_tile(s, t) for t in ts]
            kk = [kks[j] for _, j, _ in tiles]
            ss = [_dg(q2[sel], kj, NT) + ((gi2[sel] - _fox_key_bias(ct_ref, p, j)) + mb_ref[kind])
                  for kj, (sel, j, kind) in zip(kk, tiles)]
            dps = [_dg(da2[sel], vvs[j], NT) for sel, j, _ in tiles]
            pes = [jnp.exp(st) for st in ss]
            dss = [pe * (dp - dl2[sel]) * FSCALE for pe, dp, (sel, _, _) in zip(pes, dps, tiles)]
            pts = [jnp.concatenate([pe[:, :C].T, pe[:, C:].T], axis=1).astype(BF) for pe in pes]
            dsts = [jnp.concatenate([ds[:, :C].T, ds[:, C:].T], axis=1).astype(BF) for ds in dss]
            dvs = [_dot(pt, dd2[sel]) for pt, (sel, _, _) in zip(pts, tiles)]
            rs = [_dot(dst, qq2[sel]) for dst, (sel, _, _) in zip(dsts, tiles)]
            parts = [_dot(ds.astype(BF), jnp.concatenate([kj, ones], axis=1)) for ds, kj in zip(dss, kk)]
            for t, dv, rr in zip(ts, dvs, rs):
                dvb[t] = dv
                dkb[t] = rr
            pa, pb = zero, zero
            for t, part in zip(ts, parts):
                pa = pa + jnp.where(t <= s, part, zero)
                pb = pb + jnp.where(t <= s, zero, part)
            dq2[0] += pa
            dq2[1] += pb
            return carry

        lax.fori_loop(0, NTILE // grp, group, 0)

        def scatter(t, carry):
            _, j, _ = _fox_tile(s, t)
            r = pl.ds(pl.multiple_of(j * C, C), C)
            dvacc[r, :] += dvb[t]
            dkacc[r, :] += dkb[t, :, :C]
            csacc[r, :] += dkb[t, :, C:]
            return carry

        lax.fori_loop(0, NTILE, scatter, 0)
        for w in range(2):
            res = dq2[w]
            dq_ref[C * w:C * (w + 1), :] = res[:, :C].astype(BF)
            dr_ref[C * w:C * (w + 1), :] = res[:, C:]

        @pl.when(s == NSTEP - 1)
        def _():
            dk_ref[...] = dkacc[...].astype(BF)
            dv_ref[...] = dvacc[...].astype(BF)
            dcs_ref[...] = csacc[...]

    qa, qb = _fox_q_specs()
    ba, bb, both = _fox_pair_specs()
    col = pl.BlockSpec((T, C), lambda p, s: (0, p))
    return pl.pallas_call(
        body, name="fox_bwd", grid=(NPAIR, NSTEP),
        in_specs=[qa, qb, ba, bb, ba, bb, ba, bb,
                  pl.BlockSpec((T, C), lambda p, s: (0, KB_F + p)),
                  pl.BlockSpec((T, C), lambda p, s: (0, VB_F + p)),
                  pl.BlockSpec((NCH, FH, C), lambda p, s: (0, 0, 0)),
                  pl.BlockSpec((2 * C, C), lambda p, s: (0, 0)),
                  pl.BlockSpec((3, C, 2 * C), lambda p, s: (0, 0, 0))],
        out_specs=[both, both, col, col, col],
        out_shape=[jax.ShapeDtypeStruct((TROWS, FH * FD), BF), jax.ShapeDtypeStruct((TROWS, FH * FD), F32),
                   jax.ShapeDtypeStruct((T, FH * FD), BF), jax.ShapeDtypeStruct((T, FH * FD), BF),
                   jax.ShapeDtypeStruct((T, FH * FD), F32)],
        scratch_shapes=[pltpu.VMEM((NCH, 2 * C, C), BF), pltpu.VMEM((NCH, 2 * C, C), BF),
                        pltpu.VMEM((2, C, C), BF), pltpu.VMEM((2, 2 * C, 2 * C), BF), pltpu.VMEM((2, 2 * C, C), BF),
                        pltpu.VMEM((2, C, C), BF), pltpu.VMEM((2, C, 2 * C), F32), pltpu.VMEM((2, C, 2 * C), F32),
                        pltpu.VMEM((2, C, 2 * C), F32),
                        pltpu.VMEM((NTILE, C, C), F32), pltpu.VMEM((NTILE, C, 2 * C), F32),
                        pltpu.VMEM((T, C), F32), pltpu.VMEM((T, C), F32), pltpu.VMEM((T, C), F32)],
        compiler_params=_params(("parallel", "arbitrary")),
    )(z, z, da, da, g, g, delta, delta, z, z, ct, cst["ones_aug"], cst["mask_bias"])


def _fox_gate_bwd(drow, dcol, zf, bf_pad, cst):
    def body(dr_ref, dc_ref, zf_ref, b_ref, tri_ref, pick_ref, dff_ref, db_ref, carry):
        s = pl.program_id(0)
        n = NCH - 1 - s

        @pl.when(s == 0)
        def _():
            carry[...] = jnp.zeros_like(carry)
            db_ref[...] = jnp.zeros_like(db_ref)

        dcb = jnp.dot((dr_ref[...] - dc_ref[...]) * (1.0 / FSCALE), pick_ref[...], precision=HI,
                      preferred_element_type=F32)
        suf = lax.dot_general(tri_ref[...], dcb, TN, precision=HI, preferred_element_type=F32) + carry[0:1, :]
        carry[...] = jnp.broadcast_to(suf[0:1, :], carry.shape)
        x = zf_ref[...] + b_ref[...]
        row = n * C + lax.broadcasted_iota(jnp.int32, (C, C), 0)
        dff = jnp.where(row >= PAD, suf * (1.0 - jax.nn.sigmoid(x)), 0.0)
        dff_ref[...] = dff.astype(BF)
        db_ref[...] += jnp.sum(dff, axis=0, keepdims=True)

    rev = lambda s: (NCH - 1 - s, 0)
    return pl.pallas_call(
        body, name="fox_gate_bwd", grid=(NCH,),
        in_specs=[pl.BlockSpec((C, FH * FD), lambda s: (_fox_pos(NCH - 1 - s), 0)),
                  pl.BlockSpec((C, FH * FD), rev), pl.BlockSpec((C, C), rev),
                  pl.BlockSpec((1, C), lambda s: (0, 0)), pl.BlockSpec((C, C), lambda s: (0, 0)),
                  pl.BlockSpec((FH * FD, C), lambda s: (0, 0))],
        out_specs=[pl.BlockSpec((C, C), rev), pl.BlockSpec((1, C), lambda s: (0, 0))],
        out_shape=[jax.ShapeDtypeStruct((T, C), BF), jax.ShapeDtypeStruct((1, C), F32)],
        scratch_shapes=[pltpu.VMEM((8, C), F32)],
        compiler_params=_params(("arbitrary",)),
    )(drow, dcol, zf, bf_pad, cst["tri"], cst["pick"])


def _gated(r, rg, a, fg):
    rn, rs = [], []
    for h in range(RH):
        rh = r[:, RDV * h:RDV * (h + 1)]
        s = lax.rsqrt(jnp.mean(rh * rh, axis=1, keepdims=True) + EPS)
        rn.append(rh * s)
        rs.append(s)
    rn = jnp.concatenate(rn, axis=1)
    y = jnp.concatenate([rn * (rg * jax.nn.sigmoid(rg)), a * (fg * jax.nn.sigmoid(fg))], axis=1)
    return y, rn, rs


def _out_loss(r, z, a, wout, x, tgt, fgain):
    def body(r_ref, rg_ref, a_ref, fg_ref, w_ref, x_ref, t_ref, g_ref, yt_ref, do_ref, dob_ref, loss_ref, dg_ref):
        i = pl.program_id(0)

        @pl.when(i == 0)
        def _():
            yt_ref[...] = jnp.zeros_like(yt_ref)
            do_ref[...] = jnp.zeros_like(do_ref)
            dob_ref[...] = jnp.zeros_like(dob_ref)
            loss_ref[...] = jnp.zeros_like(loss_ref)
            dg_ref[...] = jnp.zeros_like(dg_ref)

        @pl.when(i > 0)
        def _():
            y, _, _ = _gated(r_ref[...], rg_ref[...], a_ref[...], fg_ref[...])
            yt_ref[...] = y.T.astype(BF)
            o = x_ref[...] + _dot(y.astype(BF), w_ref[...])
            rs = lax.rsqrt(jnp.mean(o * o, axis=1, keepdims=True) + EPS)
            on = o * rs
            g = g_ref[...]
            e = on * g - t_ref[...]
            loss_ref[...] += 0.5 * jnp.sum(jnp.mean(e * e, axis=1, keepdims=True))
            dyh = e * (1.0 / D)
            dg_ref[...] += jnp.sum(dyh * on, axis=0, keepdims=True)
            don = dyh * g
            do = rs * (don - on * jnp.mean(don * on, axis=1, keepdims=True))
            do_ref[...] = do
            dob_ref[...] = do.astype(BF)

    tok = lambda i: (jnp.maximum(i - 1, 0), 0)
    return pl.pallas_call(
        body, name="out_loss", grid=(NCH,),
        in_specs=[pl.BlockSpec((C, D), lambda i: (i, 0)), pl.BlockSpec((C, D), lambda i: (i, GB_R)),
                  pl.BlockSpec((C, D), lambda i: (_fox_pos(i), 0)), pl.BlockSpec((C, D), lambda i: (i, GB_F)),
                  pl.BlockSpec((DMIX, D), lambda i: (0, 0)),
                  pl.BlockSpec((C, D), tok), pl.BlockSpec((C, D), tok), pl.BlockSpec((1, D), lambda i: (0, 0))],
        out_specs=[pl.BlockSpec((DMIX, C), lambda i: (0, i)), pl.BlockSpec((C, D), lambda i: (i, 0)),
                   pl.BlockSpec((C, D), lambda i: (i, 0)), pl.BlockSpec((8, C), lambda i: (0, 0)),
                   pl.BlockSpec((1, D), lambda i: (0, 0))],
        out_shape=[jax.ShapeDtypeStruct((DMIX, T), BF), jax.ShapeDtypeStruct((T, D), F32),
                   jax.ShapeDtypeStruct((T, D), BF), jax.ShapeDtypeStruct((8, C), F32),
                   jax.ShapeDtypeStruct((1, D), F32)],
        compiler_params=_params(("arbitrary",)),
    )(r, z, a, z, wout, x, tgt, fgain)


def _dsilu(x):
    s = jax.nn.sigmoid(x)
    return s * (1.0 + x * (1.0 - s))


def _dy_gate_bwd(dob, wout, r, z, a, seg):
    def body(do_ref, w_ref, r_ref, rg_ref, a_ref, fg_ref, seg_ref, dr_ref, da_ref, drg_ref, dfg_ref, dl_ref):
        dy = _dg(do_ref[...], w_ref[...], NT)
        rg, fg, a_ = rg_ref[...], fg_ref[...], a_ref[...]
        _, rn, rs = _gated(r_ref[...], rg, a_, fg)
        dyr, dyf = dy[:, :D], dy[:, D:]
        drn = dyr * (rg * jax.nn.sigmoid(rg))
        drg_ref[...] = (dyr * rn * _dsilu(rg)).astype(BF)
        for h in range(RH):
            sl = slice(RDV * h, RDV * (h + 1))
            dh, nh = drn[:, sl], rn[:, sl]
            dr_ref[:, sl] = (rs[h] * (dh - nh * jnp.mean(dh * nh, axis=1, keepdims=True))).astype(BF)
        dab = (dyf * (fg * jax.nn.sigmoid(fg))).astype(BF)
        da_ref[...] = dab
        dfg_ref[...] = (dyf * a_ * _dsilu(fg)).astype(BF)
        prod = dab.astype(F32) * a_
        segm = seg_ref[...]
        for p in range(NPAIR):
            sl = slice(C * p, C * (p + 1))
            hi = prod[:, sl].astype(BF)
            lo = (prod[:, sl] - hi.astype(F32)).astype(BF)
            dl_ref[:, sl] = _dot(hi, segm) + _dot(lo, segm)

    row = pl.BlockSpec((C, D), lambda i: (i, 0))
    fox = pl.BlockSpec((C, D), lambda i: (_fox_pos(i), 0))
    return pl.pallas_call(
        body, name="dy_gate_bwd", grid=(NCH,),
        in_specs=[row, pl.BlockSpec((DMIX, D), lambda i: (0, 0)),
                  row, pl.BlockSpec((C, D), lambda i: (i, GB_R)),
                  fox, pl.BlockSpec((C, D), lambda i: (i, GB_F)),
                  pl.BlockSpec((C, C), lambda i: (0, 0))],
        out_specs=[row, fox, row, row, fox],
        out_shape=[jax.ShapeDtypeStruct((T, D), BF), jax.ShapeDtypeStruct((TROWS, D), BF),
                   jax.ShapeDtypeStruct((T, D), BF), jax.ShapeDtypeStruct((T, D), BF),
                   jax.ShapeDtypeStruct((TROWS, D), F32)],
        compiler_params=_params(("parallel",)),
    )(dob, wout, r, z, a, z, seg)


DZ_WIDTHS = (512, 512, 1024, 1024, 1024, 1024, 1024, 1024)


def _du_norm_bwd(dzs, dzf, wt, wft, hpad, g, dopad, parts=()):
    tm, tk = 544, 1024
    nk = WMAIN // tk
    ni = T // tm
    n = len(parts)

    def body(rq_ref, rk_ref, rv_ref, rg_ref, fq_ref, fk_ref, fv_ref, fg_ref, dzf_ref, w_ref, wf_ref, h_ref, g_ref,
             do_ref, *rest):
        part_refs, (gh_ref, dg_ref), land_refs = rest[:n], rest[n:n + 2], rest[n + 2:2 * n + 2]
        acc = rest[2 * n + 2]
        i, k = pl.program_id(0), pl.program_id(1)

        if n:
            send_sems, recv_sems = rest[2 * n + 3:]
            x, y, c = _place()
            me_s = 2 * x + y
            copies = [pltpu.make_async_remote_copy(
                src_ref=part_refs[a].at[cs], dst_ref=land_refs[a].at[me_s],
                send_sem=send_sems.at[3 * a + j], recv_sem=recv_sems.at[3 * a + j],
                device_id=(cx, cy, c), device_id_type=MESH)
                for a in range(n) for j, (cx, cy, cs) in enumerate(_other_chips(x, y))]

            @pl.when((i == 0) & (k == 0))
            def _():
                for cp in copies:
                    cp.start()

            @pl.when((i == ni - 1) & (k == nk - 1))
            def _():
                for cp in copies:
                    cp.wait()

        @pl.when(k == 0)
        def _():
            acc[...] = (_dot(dzf_ref[...], wf_ref[...]) + _dot(rq_ref[...], w_ref[:512, :])
                        + _dot(rk_ref[...], w_ref[512:, :]))

        for kk, piece in enumerate((rv_ref, rg_ref, fq_ref, fk_ref, fv_ref, fg_ref), start=1):
            @pl.when(k == kk)
            def _(piece=piece):
                acc[...] += _dot(piece[...], w_ref[...])

        @pl.when(k == nk - 1)
        def _():
            du = acc[...]
            h = h_ref[...]
            gg = g_ref[...]
            rs = lax.rsqrt(jnp.mean(h * h, axis=1, keepdims=True) + EPS)
            hn = h * rs
            part = jnp.sum(du * hn, axis=0, keepdims=True)

            @pl.when(i == 0)
            def _():
                dg_ref[...] = part

            @pl.when(i > 0)
            def _():
                dg_ref[...] += part

            dhn = du * gg
            gh_ref[...] = rs * (dhn - hn * jnp.mean(dhn * hn, axis=1, keepdims=True)) + do_ref[...]

    sems = [pltpu.SemaphoreType.DMA((3 * n,)), pltpu.SemaphoreType.DMA((3 * n,))] if n else []
    return pl.pallas_call(
        body, name="du_norm_bwd", grid=(ni, nk),
        in_specs=[pl.BlockSpec((tm, w), lambda i, k: (i, 0)) for w in DZ_WIDTHS]
        + [pl.BlockSpec((tm, C), lambda i, k: (i, 0)),
           pl.BlockSpec((tk, D), lambda i, k: (k, 0)), pl.BlockSpec((C, D), lambda i, k: (0, 0)),
           pl.BlockSpec((tm, D), lambda i, k: (i, 0)), pl.BlockSpec((1, D), lambda i, k: (0, 0)),
           pl.BlockSpec((tm, D), lambda i, k: (i, 0))] + [ANY] * n,
        out_specs=[pl.BlockSpec((tm, D), lambda i, k: (i, 0)), pl.BlockSpec((1, D), lambda i, k: (0, 0))] + [ANY] * n,
        out_shape=[jax.ShapeDtypeStruct((T, D), F32), jax.ShapeDtypeStruct((1, D), F32)]
        + [jax.ShapeDtypeStruct(p.shape, p.dtype) for p in parts],
        scratch_shapes=[pltpu.VMEM((tm, D), F32)] + sems,
        compiler_params=_params(("arbitrary", "arbitrary")),
    )(*dzs, dzf, wt, wft, hpad, g, dopad, *parts)


GROWS = 7424


def _dw_in(dzs, dzf, ut):
    tn = 256
    nmain = WMAIN // tn
    first, blocks = [], []
    for w in DZ_WIDTHS:
        first.append(sum(blocks))
        blocks.append(w // tn)

    def body(rq_ref, rk_ref, rv_ref, rg_ref, fq_ref, fk_ref, fv_ref, fg_ref, dzf_ref, ut_ref, o_ref):
        gidx = pl.program_id(0)
        for piece, g0, nb in zip((rq_ref, rk_ref, rv_ref, rg_ref, fq_ref, fk_ref, fv_ref, fg_ref), first, blocks):
            @pl.when((gidx >= g0) & (gidx < g0 + nb))
            def _(piece=piece):
                o_ref[...] = _dot(ut_ref[...], piece[...]).T

        @pl.when(gidx == nmain)
        def _():
            o_ref[:C, :] = _dot(ut_ref[...], dzf_ref[...]).T
            o_ref[C:, :] = jnp.zeros((tn - C, D), F32)

    def piece_spec(g0, nb):
        return pl.BlockSpec((T, tn), lambda gidx: (0, jnp.clip(gidx - g0, 0, nb - 1)))

    return pl.pallas_call(
        body, name="dw_in", grid=(nmain + 1,),
        in_specs=[piece_spec(g0, nb) for g0, nb in zip(first, blocks)]
        + [pl.BlockSpec((T, C), lambda gidx: (0, 0)), pl.BlockSpec((D, T), lambda gidx: (0, 0))],
        out_specs=pl.BlockSpec((tn, D), lambda gidx: (gidx, 0)),
        out_shape=jax.ShapeDtypeStruct((GROWS, D), F32),
        compiler_params=_params(("arbitrary",)),
    )(*dzs, dzf, ut)


def _token_order(x_po):
    def body(i_ref, o_ref):
        o_ref[...] = i_ref[...]

    return pl.pallas_call(
        body, name="token_order", grid=(NCH,),
        in_specs=[pl.BlockSpec((C, D), lambda i: (_fox_pos(i), 0))],
        out_specs=pl.BlockSpec((C, D), lambda i: (i, 0)),
        out_shape=jax.ShapeDtypeStruct((T, D), x_po.dtype),
        compiler_params=_params(("parallel",)),
    )(x_po)


def _local_step(x, tgt, meta, norm_g, wt, wft, b_f, wout, final_g, chip_sums=None):
    cst = _constants()
    hpad = jnp.concatenate([jnp.pad(meta, ((PAD, 0), (0, 0))), x], axis=0)
    bf_pad = jnp.pad(b_f, ((0, 0), (0, C - NFF)))
    u, ut = _norm_in(hpad, norm_g)
    z = _mm_nt(u, wt, WMAIN, T // 2, 512, "in_proj")
    zf = _mm_nt(u, wft, C, T // 2, C, "in_proj_ff")
    r, sprev = _ret_fwd(z, cst)
    cb, ct = _fox_prep(zf, bf_pad, cst)
    a, g = _fox_fwd(z, cb, ct, cst)
    yt, dopad, dob, loss8, dfg = _out_loss(r, z, a, wout, x, tgt, final_g)
    dr, da, dzrg, dzfg, delta = _dy_gate_bwd(dob, wout, r, z, a, cst["seg"])
    dwout = _mm_nn(yt, dob, 512, D, "dw_out")
    dzq_r, dzk_r, dzv_r = _ret_bwd(z, cst, sprev, dr)
    dq_po, drow, dzk_f, dzv_f, dcol = _fox_bwd(z, da, g, delta, ct, cst)
    dzf, dbf = _fox_gate_bwd(drow, dcol, zf, bf_pad, cst)
    dzs = [dzq_r, dzk_r, dzv_r, dzrg, _token_order(dq_po), dzk_f, dzv_f, dzfg]
    gwt = _dw_in(dzs, dzf, ut)
    parts = chip_sums(gwt, dwout) if chip_sums else []
    gh, dng, *landed = _du_norm_bwd(dzs, dzf, wt, wft, hpad, norm_g, dopad, parts)
    return (loss8[0, 0], gh[C:], gh[PAD:C], dng, gwt, dbf[:, :NFF], dwout, dfg, parts, landed)


def _place():
    x, y, c = lax.axis_index("x"), lax.axis_index("y"), lax.axis_index("c")
    return x, y, c


def _other_chips(x, y):
    return [(1 - x, y, 2 * (1 - x) + y), (x, 1 - y, 2 * x + (1 - y)), (1 - x, 1 - y, 2 * (1 - x) + (1 - y))]


def _all_gather_shards(shards):
    n = len(shards)

    def body(*refs):
        ins, outs = refs[:n], refs[n:2 * n]
        send_sems, recv_sems = refs[2 * n:]
        x, y, c = _place()
        me_s = 2 * x + y
        sib = (x, y, 1 - c)
        chips = _other_chips(x, y)
        sends, waits = [], []
        for a in range(n):
            rows = ins[a].shape[0] // 2
            half = pl.ds(c * rows, rows)
            for k, (cx, cy, cs) in enumerate(chips):
                sends.append(pltpu.make_async_remote_copy(
                    src_ref=ins[a].at[half], dst_ref=outs[a].at[me_s, half],
                    send_sem=send_sems.at[6 * a + k], recv_sem=recv_sems.at[6 * a + k],
                    device_id=(cx, cy, c), device_id_type=MESH))
                sends[-1].start()
        for a in range(n):
            rows = ins[a].shape[0] // 2
            half = pl.ds(c * rows, rows)
            other = pl.ds((1 - c) * rows, rows)
            for k, (cx, cy, cs) in enumerate(chips):
                pltpu.make_async_remote_copy(
                    src_ref=outs[a].at[cs, half], dst_ref=outs[a].at[cs, half],
                    send_sem=send_sems.at[6 * a + k], recv_sem=recv_sems.at[6 * a + k],
                    device_id=(cx, cy, c), device_id_type=MESH).wait_recv()
                fwd = pltpu.make_async_remote_copy(
                    src_ref=outs[a].at[cs, half], dst_ref=outs[a].at[cs, half],
                    send_sem=send_sems.at[6 * a + 3 + k], recv_sem=recv_sems.at[6 * a + 3 + k],
                    device_id=sib, device_id_type=MESH)
                fwd.start()
                sends.append(fwd)
                waits.append(pltpu.make_async_remote_copy(
                    src_ref=outs[a].at[cs, other], dst_ref=outs[a].at[cs, other],
                    send_sem=send_sems.at[6 * a + 3 + k], recv_sem=recv_sems.at[6 * a + 3 + k],
                    device_id=sib, device_id_type=MESH))
        for w in waits:
            w.wait_recv()
        for s in sends:
            s.wait_send()

    return pl.pallas_call(
        body, name="all_gather_w",
        in_specs=[ANY] * n, out_specs=[ANY] * n,
        out_shape=[jax.ShapeDtypeStruct((4,) + s.shape, s.dtype) for s in shards],
        scratch_shapes=[pltpu.SemaphoreType.DMA((6 * n,)), pltpu.SemaphoreType.DMA((6 * n,))],
    )(*shards)


WOFF, WLEN = 1792, 2048
WHALF = WLEN // 2


def _pair_swap(gwt, arrs):
    n = len(arrs)

    def body(*refs):
        gw, ins = refs[0], refs[1:n + 1]
        gwo, outs = refs[n + 1], refs[n + 2:2 * n + 2]
        send_sems, recv_sems = refs[2 * n + 2:]
        x, y, c = _place()
        sib = (x, y, 1 - c)
        cps = []
        for k in range(4):
            cps.append(pltpu.make_async_remote_copy(
                src_ref=gw.at[pl.ds(WOFF * k + (1 - c) * WHALF, WHALF)], dst_ref=gwo.at[k],
                send_sem=send_sems.at[k], recv_sem=recv_sems.at[k], device_id=sib, device_id_type=MESH))
        for a in range(n):
            rows = ins[a].shape[1] // 2
            cps.append(pltpu.make_async_remote_copy(
                src_ref=ins[a].at[:, pl.ds((1 - c) * rows, rows)], dst_ref=outs[a],
                send_sem=send_sems.at[4 + a], recv_sem=recv_sems.at[4 + a], device_id=sib, device_id_type=MESH))
        for cp in cps:
            cp.start()
        for cp in cps:
            cp.wait()

    return pl.pallas_call(
        body, name="rs_pair_swap",
        in_specs=[ANY] * (n + 1), out_specs=[ANY] * (n + 1),
        out_shape=[jax.ShapeDtypeStruct((4, WHALF, D), gwt.dtype)]
        + [jax.ShapeDtypeStruct((4, a.shape[1] // 2, a.shape[2]), a.dtype) for a in arrs],
        scratch_shapes=[pltpu.SemaphoreType.DMA((n + 4,)), pltpu.SemaphoreType.DMA((n + 4,))],
    )(gwt, *arrs)


def _add_windows(gwt, recv):
    tb = 256
    nb = WHALF // tb
    c = lax.axis_index("c")

    def body(c_ref, a_ref, b_ref, o_ref):
        o_ref[0] = (a_ref[...] + b_ref[0]).astype(BF)

    return pl.pallas_call(
        body, name="pair_add_in",
        grid_spec=pltpu.PrefetchScalarGridSpec(
            num_scalar_prefetch=1, grid=(4, nb),
            in_specs=[pl.BlockSpec((tb, D), lambda k, i, cr: ((WOFF // tb) * k + nb * cr[0] + i, 0)),
                      pl.BlockSpec((1, tb, D), lambda k, i, cr: (k, i, 0))],
            out_specs=pl.BlockSpec((1, tb, D), lambda k, i, cr: (k, i, 0))),
        out_shape=jax.ShapeDtypeStruct(recv.shape, BF),
        compiler_params=_params(("parallel", "parallel")),
    )(jnp.reshape(c, (1,)).astype(jnp.int32), gwt, recv)


def _chip_exchange(parts, small):
    n = len(parts)

    def body(*refs):
        ins, sm = refs[:n], refs[n]
        outs, smo = refs[n + 1:2 * n + 1], refs[2 * n + 1]
        send_sems, recv_sems = refs[2 * n + 2:]
        x, y, c = _place()
        me_s = 2 * x + y
        chips = _other_chips(x, y)
        cps = []
        for a in range(n + 1):
            src = ins[a] if a < n else sm
            dst = outs[a] if a < n else smo
            for k, (cx, cy, cs) in enumerate(chips):
                cps.append(pltpu.make_async_remote_copy(
                    src_ref=src.at[cs] if a < n else src, dst_ref=dst.at[me_s],
                    send_sem=send_sems.at[3 * a + k], recv_sem=recv_sems.at[3 * a + k],
                    device_id=(cx, cy, c), device_id_type=MESH))
        for cp in cps:
            cp.start()
        for cp in cps:
            cp.wait()

    return pl.pallas_call(
        body, name="rs_chip_exchange",
        in_specs=[ANY] * (n + 1), out_specs=[ANY] * (n + 1),
        out_shape=[jax.ShapeDtypeStruct(p.shape, p.dtype) for p in parts]
        + [jax.ShapeDtypeStruct((4,) + small.shape, small.dtype)],
        scratch_shapes=[pltpu.SemaphoreType.DMA((3 * (n + 1),)), pltpu.SemaphoreType.DMA((3 * (n + 1),))],
    )(*parts, small)


def _pair_send(halves):
    n = len(halves)

    def body(*refs):
        ins, outs = refs[:n], refs[n:2 * n]
        send_sems, recv_sems = refs[2 * n:]
        x, y, c = _place()
        cps = [pltpu.make_async_remote_copy(
            src_ref=ins[a], dst_ref=outs[a], send_sem=send_sems.at[a], recv_sem=recv_sems.at[a],
            device_id=(x, y, 1 - c), device_id_type=MESH) for a in range(n)]
        for cp in cps:
            cp.start()
        for cp in cps:
            cp.wait()

    return pl.pallas_call(
        body, name="rs_pair_send",
        in_specs=[ANY] * n, out_specs=[ANY] * n,
        out_shape=[jax.ShapeDtypeStruct(h.shape, h.dtype) for h in halves],
        scratch_shapes=[pltpu.SemaphoreType.DMA((n,)), pltpu.SemaphoreType.DMA((n,))],
    )(*halves)


def _row_block(rows):
    for tb in (256, 128, 64, 32, 16, 8):
        if rows % tb == 0:
            return tb
    return rows


def _add_halves(full, recv, name, out_dtype):
    _, r2, w = recv.shape
    tb = _row_block(r2)
    nb = r2 // tb
    c = lax.axis_index("c")

    def body(c_ref, a_ref, b_ref, o_ref):
        o_ref[...] = (a_ref[...] + b_ref[...]).astype(o_ref.dtype)

    return pl.pallas_call(
        body, name=name,
        grid_spec=pltpu.PrefetchScalarGridSpec(
            num_scalar_prefetch=1, grid=(4, nb),
            in_specs=[pl.BlockSpec((1, tb, w), lambda s, i, cr: (s, cr[0] * nb + i, 0)),
                      pl.BlockSpec((1, tb, w), lambda s, i, cr: (s, i, 0))],
            out_specs=pl.BlockSpec((1, tb, w), lambda s, i, cr: (s, i, 0))),
        out_shape=jax.ShapeDtypeStruct(recv.shape, out_dtype),
        compiler_params=_params(("parallel", "parallel")),
    )(jnp.reshape(c, (1,)).astype(jnp.int32), full, recv)


def _add2(a, b, name):
    def body(a_ref, b_ref, o_ref):
        o_ref[...] = a_ref[...] + b_ref[...]

    return pl.pallas_call(body, name=name, out_shape=jax.ShapeDtypeStruct(a.shape, a.dtype))(a, b)


def _sum4(buf, own, name):
    _, r, w = buf.shape
    tb = _row_block(r)
    me_s = 2 * lax.axis_index("x") + lax.axis_index("y")
    by_dest = own.ndim == 3

    def body(s_ref, b_ref, own_ref, o_ref):
        mine = (own_ref[0] if by_dest else own_ref[...]).astype(F32)
        terms = [jnp.where(s_ref[0] == t, mine, b_ref[t].astype(F32)) for t in range(4)]
        o_ref[...] = ((terms[0] + terms[1]) + terms[2]) + terms[3]

    own_spec = (pl.BlockSpec((1, tb, w), lambda i, sr: (sr[0], i, 0)) if by_dest
                else pl.BlockSpec((tb, w), lambda i, sr: (i, 0)))
    return pl.pallas_call(
        body, name=name,
        grid_spec=pltpu.PrefetchScalarGridSpec(
            num_scalar_prefetch=1, grid=(r // tb,),
            in_specs=[pl.BlockSpec((4, tb, w), lambda i, sr: (0, i, 0)), own_spec],
            out_specs=pl.BlockSpec((tb, w), lambda i, sr: (i, 0))),
        out_shape=jax.ShapeDtypeStruct((r, w), F32),
        compiler_params=_params(("parallel",)),
    )(jnp.reshape(me_s, (1,)).astype(jnp.int32), buf, own)


def _adamw_math(w, g, m, v):
    mn = B1 * m + (1.0 - B1) * g
    vn = B2 * v + (1.0 - B2) * (g * g)
    m_hat = mn / (1.0 - B1 ** STEP)
    v_hat = vn / (1.0 - B2 ** STEP)
    return -LR * (m_hat / (jnp.sqrt(v_hat) + AEPS) + WD * w), mn, vn


def _adamw(w, g, m, v, name):
    r, c_ = w.shape
    tb = _row_block(r)
    if tb == r and r > 512:
        tb = 256

    def body(w_ref, g_ref, m_ref, v_ref, d_ref, mo_ref, vo_ref):
        d_ref[...], mo_ref[...], vo_ref[...] = _adamw_math(w_ref[...], g_ref[...], m_ref[...], v_ref[...])

    spec = pl.BlockSpec((tb, c_), lambda i: (i, 0))
    return pl.pallas_call(
        body, name=name, grid=(pl.cdiv(r, tb),),
        in_specs=[spec] * 4, out_specs=[spec] * 3,
        out_shape=[jax.ShapeDtypeStruct(w.shape, F32)] * 3,
        compiler_params=_params(("parallel",)),
    )(w, g, m, v)


def _adamw_halves(w, g_mine, g_sib, m, v, name):
    r, c_ = w.shape
    r2 = g_mine.shape[0]
    tb = _row_block(r2)
    nb = r2 // tb
    c = lax.axis_index("c")

    def body(c_ref, w_ref, gm_ref, gs_ref, m_ref, v_ref, g_ref, d_ref, mo_ref, vo_ref):
        g = jnp.where(pl.program_id(0) == c_ref[0], gm_ref[...], gs_ref[...])
        g_ref[...] = g
        d_ref[...], mo_ref[...], vo_ref[...] = _adamw_math(w_ref[...], g, m_ref[...], v_ref[...])

    full = pl.BlockSpec((tb, c_), lambda h, i, cr: (h * nb + i, 0))
    half = pl.BlockSpec((tb, c_), lambda h, i, cr: (i, 0))
    return pl.pallas_call(
        body, name=name,
        grid_spec=pltpu.PrefetchScalarGridSpec(
            num_scalar_prefetch=1, grid=(2, nb),
            in_specs=[full, half, half, full, full], out_specs=[full] * 4),
        out_shape=[jax.ShapeDtypeStruct(w.shape, F32)] * 4,
        compiler_params=_params(("parallel", "parallel")),
    )(jnp.reshape(c, (1,)).astype(jnp.int32), w, g_mine, g_sib, m, v)


def kernel(x, meta_tokens, norm_g, w_in, b_f, w_out, final_g, loss_target, m_meta_tokens, m_norm_g, m_w_in, m_b_f, m_w_out, m_final_g, v_meta_tokens, v_norm_g, v_w_in, v_b_f, v_w_out, v_final_g):
    me_s = 2 * lax.axis_index("x") + lax.axis_index("y")
    core = lax.axis_index("c")
    wt, mt, vt = [jnp.swapaxes(t[0], 0, 1) for t in (w_in, m_w_in, v_w_in)]

    own_win = lax.dynamic_update_slice(jnp.zeros((WPADROWS, D), F32), wt, (4 * me_s, 0)).astype(BF)
    own = [own_win, w_out[0].astype(BF), meta_tokens]
    gathered = _all_gather_shards(own)
    mine = jnp.arange(4) == me_s
    win, gout, gmeta = [jnp.where(mine[:, None, None], o[None], g) for o, g in zip(own, gathered)]
    lap = WPADROWS - WOFF
    tails = jnp.concatenate([jnp.zeros((1, lap, D), BF), win[:-1, WOFF:]], axis=0)
    wt_main = jnp.concatenate([win[:, :lap] + tails, win[:, lap:WOFF]], axis=1).reshape(WMAIN, D)
    wft = jnp.pad(win[3, WOFF:WOFF + NFF], ((0, C - NFF), (0, 0)))
    wout = gout.reshape(DMIX, D)
    meta = jnp.concatenate([gmeta[s] for s in range(4)], axis=1)

    def chip_sums(gwt, dwout):
        g_out = dwout.reshape(4, DMIX // 4, D)
        r_in, r_out = _pair_swap(gwt, [g_out])
        return [_add_windows(gwt, r_in), _add_halves(g_out, r_out, "pair_add_out", BF)]

    loss, gx, dmeta, dng, gwt, dbf, dwout, dfg, (p_in, p_out), (e_in, e_out) = _local_step(
        x[0], loss_target[0], meta, norm_g, wt_main, wft, b_f, wout, final_g.reshape(1, D), chip_sums)

    g_meta = jnp.stack([dmeta[:, 256 * s:256 * (s + 1)] for s in range(4)])
    small = jnp.concatenate([dng, dfg, jnp.pad(dbf, ((0, 0), (0, D - NFF))),
                             jnp.pad(jnp.reshape(loss, (1, 1)), ((0, 0), (0, D - 1))),
                             jnp.zeros((4, D), F32)], axis=0)
    e_meta, e_small = _chip_exchange([g_meta], small)
    h_in, h_out = _sum4(e_in, p_in, "sum_in"), _sum4(e_out, p_out, "sum_out")
    h_meta, h_small = _sum4(e_meta, g_meta, "sum_meta"), _sum4(e_small, small, "sum_small")
    s_in, s_out, s_meta, s_small = _pair_send([h_in, h_out, h_meta, h_small])
    gw_meta = _add2(h_meta, s_meta, "pair_add_meta")
    tot = _add2(h_small, s_small, "pair_add_small")
    g_norm, g_final, g_bf, loss_all = tot[0:1], tot[1], tot[2:3, :NFF], tot[3, 0]

    d_meta, nm_meta, nv_meta = _adamw(meta_tokens, gw_meta, m_meta_tokens, v_meta_tokens, "adamw_meta")
    d_norm, nm_norm, nv_norm = _adamw(norm_g, g_norm, m_norm_g, v_norm_g, "adamw_norm")
    window = jnp.concatenate([jnp.where(core == 0, h_in, s_in), jnp.where(core == 0, s_in, h_in)], axis=0)
    gwt_own = lax.dynamic_slice(window, (4 * me_s, 0), (WSH, D))
    d_in, nm_in, nv_in = _adamw(wt, gwt_own, mt, vt, "adamw_in")
    gw_in, d_in, nm_in, nv_in = [jnp.swapaxes(t, 0, 1)[None] for t in (gwt_own, d_in, nm_in, nv_in)]
    d_bf, nm_bf, nv_bf = _adamw(b_f, g_bf, m_b_f, v_b_f, "adamw_bf")
    gw_out, d_out, nm_out, nv_out = _adamw_halves(w_out[0], h_out, s_out, m_w_out[0], v_w_out[0], "adamw_out")
    d_fin, nm_fin, nv_fin = _adamw(final_g.reshape(1, D), g_final.reshape(1, D), m_final_g.reshape(1, D),
                                   v_final_g.reshape(1, D), "adamw_final")
    return (loss_all, gx[None], gw_meta, g_norm, gw_in, g_bf, gw_out[None], g_final,
            d_meta, d_norm, d_in, d_bf, d_out[None], d_fin.reshape(D),
            nm_meta, nm_norm, nm_in, nm_bf, nm_out[None], nm_fin.reshape(D),
            nv_meta, nv_norm, nv_in, nv_bf, nv_out[None], nv_fin.reshape(D))
```

```python
import numpy as np
import jax
import jax.numpy as jnp
from jax import lax
from jax.experimental import pallas as pl
from jax.experimental.pallas import tpu as pltpu

D = 1024
SEQ = 2048
NMETA = 16
C = 128
PAD = C - NMETA
T = PAD + NMETA + SEQ
NCH = T // C
RH, RDK, RDV = 4, 128, 256
FH, FD = 16, 64
NPAIR = FH // 2
WMAIN = 7168
NFF = 16
WIN = WMAIN + NFF
WSH = WIN // 4
WPADROWS = 1824
DMIX = 2048
EPS = 1e-6
NEG = -1e30
RSCALE = RDK ** -0.5
FSCALE = FD ** -0.5
ROPE_BASE = 10000.0
LR, B1, B2, AEPS, WD, STEP = 0.001, 0.9, 0.999, 1e-08, 0.01, 10

BF = jnp.bfloat16
F32 = jnp.float32
NT = (((1,), (1,)), ((), ()))
TN = (((0,), (0,)), ((), ()))
HI = lax.Precision.HIGHEST
MESH = pl.DeviceIdType.MESH
ANY = pl.BlockSpec(memory_space=pl.ANY)
VMEM_LIMIT = 48 * 1024 * 1024

QB_R, KB_R = 0, 4
VB_R = 4
GB_R, GB_F = 2, 6
QB_F, KB_F, VB_F = 24, 32, 40


def _dot(a, b):
    return jnp.dot(a, b, preferred_element_type=F32)


def _dg(a, b, dims):
    return lax.dot_general(a, b, dims, preferred_element_type=F32)


def _params(sem=None):
    return pltpu.CompilerParams(dimension_semantics=sem, vmem_limit_bytes=VMEM_LIMIT)


def _constants():
    pos = jnp.arange(T, dtype=F32) - PAD
    inv = ROPE_BASE ** (-jnp.arange(0, RDK, 2, dtype=F32) / RDK)
    ang = pos[:, None] * inv[None, :]
    cos, sin = jnp.cos(ang), jnp.sin(ang)
    cos2 = jnp.concatenate([cos, cos], axis=1)
    sin2 = jnp.concatenate([-sin, sin], axis=1)
    log_gamma = jnp.log1p(-jnp.exp2(-5.0 - jnp.arange(RH, dtype=F32)))
    idx = jnp.arange(C, dtype=F32)
    diff = idx[:, None] - idx[None, :]
    dmask = jnp.where(diff[None] >= 0, jnp.exp(log_gamma[:, None, None] * jnp.maximum(diff, 0.0)[None]), 0.0)
    zeta = jnp.exp(log_gamma[:, None] * (C - 1.0 - idx)[None, :])
    xi = jnp.exp(log_gamma[:, None] * (idx + 1.0)[None, :])
    gdec = jnp.exp(log_gamma * C)
    zeta_b = jnp.broadcast_to(zeta[:, :, None], (RH, C, RDK))
    xi_b = jnp.broadcast_to(xi[:, :, None], (RH, C, RDK))
    gdec_b = jnp.broadcast_to(gdec[:, None, None], (RH, RDK, RDV))
    tri = jnp.asarray(np.tril(np.ones((C, C), np.float32)))
    head_of_lane = np.arange(FH * FD) // FD
    spread = (np.arange(C)[:, None] == head_of_lane[None, :]).astype(np.float32)
    pick = ((np.arange(FH * FD)[:, None] % FD == 0)
            & (head_of_lane[:, None] == np.arange(C)[None, :])).astype(np.float32)
    seg = (np.arange(C)[:, None] // FD == np.arange(C)[None, :] // FD).astype(np.float32)
    ones_aug = np.concatenate([np.tile((np.arange(C) < FD)[None, :], (C, 1)),
                               np.tile((np.arange(C) >= FD)[None, :], (C, 1))], axis=0).astype(np.float32)
    lane = np.arange(2 * C) % C
    causal = np.where(lane[None, :] <= np.arange(C)[:, None], 0.0, NEG).astype(np.float32)
    mask_bias = np.stack([np.zeros((C, 2 * C), np.float32), causal])
    return dict(cos2=cos2, sin2=sin2, dmask=dmask, zeta=zeta_b, xi=xi_b, gdec=gdec_b, tri=tri,
                mask_bias=jnp.asarray(mask_bias),
                spread=jnp.asarray(spread), pick=jnp.asarray(pick), seg=jnp.asarray(seg, dtype=BF),
                ones_aug=jnp.asarray(ones_aug, dtype=BF))


def _norm_in(hpad, g):
    def body(h_ref, g_ref, u_ref, ut_ref):
        h = h_ref[...]
        rs = lax.rsqrt(jnp.mean(h * h, axis=1, keepdims=True) + EPS)
        u = h * rs * g_ref[...]
        u_ref[...] = u.astype(BF)
        ut_ref[...] = u.T.astype(BF)

    return pl.pallas_call(
        body, name="norm_in", grid=(NCH,),
        in_specs=[pl.BlockSpec((C, D), lambda i: (i, 0)), pl.BlockSpec((1, D), lambda i: (0, 0))],
        out_specs=[pl.BlockSpec((C, D), lambda i: (i, 0)), pl.BlockSpec((D, C), lambda i: (0, i))],
        out_shape=[jax.ShapeDtypeStruct((T, D), BF), jax.ShapeDtypeStruct((D, T), BF)],
        compiler_params=_params(("parallel",)),
    )(hpad, g)


def _mm_nt(a, b, n, tm, tn, name):
    m, k = a.shape

    def body(a_ref, b_ref, o_ref):
        o_ref[...] = _dg(a_ref[...], b_ref[...], NT)

    return pl.pallas_call(
        body, name=name, grid=(m // tm, n // tn),
        in_specs=[pl.BlockSpec((tm, k), lambda i, j: (i, 0)), pl.BlockSpec((tn, k), lambda i, j: (j, 0))],
        out_specs=pl.BlockSpec((tm, tn), lambda i, j: (i, j)),
        out_shape=jax.ShapeDtypeStruct((m, n), F32),
        compiler_params=_params(("parallel", "parallel")),
    )(a, b)


def _mm_nn(a, b, tm, tn, name):
    m, k = a.shape
    _, n = b.shape

    def body(a_ref, b_ref, o_ref):
        o_ref[...] = _dot(a_ref[...], b_ref[...])

    return pl.pallas_call(
        body, name=name, grid=(m // tm, n // tn),
        in_specs=[pl.BlockSpec((tm, k), lambda i, j: (i, 0)), pl.BlockSpec((k, tn), lambda i, j: (0, j))],
        out_specs=pl.BlockSpec((tm, tn), lambda i, j: (i, j)),
        out_shape=jax.ShapeDtypeStruct((m, n), F32),
        compiler_params=_params(("parallel", "parallel")),
    )(a, b)


def _rot(x, cos2, sin2):
    return x * cos2 + pltpu.roll(x, 64, 1) * sin2


def _ret_specs(chunk):
    whole = lambda shape: pl.BlockSpec(shape, lambda n: (0,) * len(shape))
    return [
        pl.BlockSpec((C, RH * RDK), lambda n: (chunk(n), 0)),
        pl.BlockSpec((C, RH * RDK), lambda n: (chunk(n), 1)),
        pl.BlockSpec((C, RH * RDV), lambda n: (chunk(n), 1)),
        pl.BlockSpec((C, RDK), lambda n: (chunk(n), 0)),
        pl.BlockSpec((C, RDK), lambda n: (chunk(n), 0)),
        whole((RH, C, C)), whole((RH, C, RDK)), whole((RH, C, RDK)), whole((RH, RDK, RDV)),
    ]


def _ret_heads(q_ref, k_ref, v_ref, cos, sin):
    qr = [_rot(q_ref[:, RDK * h:RDK * (h + 1)], cos, sin) for h in range(RH)]
    kr = [_rot(k_ref[:, RDK * h:RDK * (h + 1)], cos, sin) * RSCALE for h in range(RH)]
    vb = [v_ref[:, RDV * h:RDV * (h + 1)].astype(BF) for h in range(RH)]
    return qr, kr, [t.astype(BF) for t in qr], [t.astype(BF) for t in kr], vb


def _ret_fwd(z, cst):
    def body(q_ref, k_ref, v_ref, cos_ref, sin_ref, dm_ref, xi_ref, zt_ref, gd_ref, r_ref, sp_ref, st):
        n = pl.program_id(0)

        @pl.when(n == 0)
        def _():
            st[...] = jnp.zeros_like(st)

        hs = range(RH)
        qr, kr, qb, kb, vb = _ret_heads(q_ref, k_ref, v_ref, cos_ref[...], sin_ref[...])
        sd = [(_dg(qb[h], kb[h], NT) * dm_ref[h]).astype(BF) for h in hs]
        state = [st[h] for h in hs]
        qx = [(qr[h] * xi_ref[h]).astype(BF) for h in hs]
        kz = [(kr[h] * zt_ref[h]).astype(BF) for h in hs]
        out = [_dot(sd[h], vb[h]) + _dot(qx[h], state[h].astype(BF)) for h in hs]
        kv = [_dg(kz[h], vb[h], TN) for h in hs]
        for h in hs:
            sp_ref[0, h] = state[h]
            r_ref[:, RDV * h:RDV * (h + 1)] = out[h]
            st[h] = state[h] * gd_ref[h] + kv[h]

    return pl.pallas_call(
        body, name="ret_fwd", grid=(NCH,),
        in_specs=_ret_specs(lambda n: n),
        out_specs=[pl.BlockSpec((C, RH * RDV), lambda n: (n, 0)),
                   pl.BlockSpec((1, RH, RDK, RDV), lambda n: (n, 0, 0, 0))],
        out_shape=[jax.ShapeDtypeStruct((T, RH * RDV), F32), jax.ShapeDtypeStruct((NCH, RH, RDK, RDV), F32)],
        scratch_shapes=[pltpu.VMEM((RH, RDK, RDV), F32)],
        compiler_params=_params(("arbitrary",)),
    )(z, z, z, cst["cos2"], cst["sin2"], cst["dmask"], cst["xi"], cst["zeta"], cst["gdec"])


def _ret_bwd(z, cst, sprev, dr):
    def body(q_ref, k_ref, v_ref, cos_ref, sin_ref, dm_ref, xi_ref, zt_ref, gd_ref, sp_ref, dr_ref,
             dq_ref, dk_ref, dv_ref, gst):
        i = pl.program_id(0)

        @pl.when(i == 0)
        def _():
            gst[...] = jnp.zeros_like(gst)

        hs = range(RH)
        cos, sin = cos_ref[...], sin_ref[...]
        qr, kr, qb, kb, vb = _ret_heads(q_ref, k_ref, v_ref, cos, sin)
        dm = [dm_ref[h] for h in hs]
        xi = [xi_ref[h] for h in hs]
        zt = [zt_ref[h] for h in hs]
        sd = [(_dg(qb[h], kb[h], NT) * dm[h]).astype(BF) for h in hs]
        qx = [(qr[h] * xi[h]).astype(BF) for h in hs]
        kz = [(kr[h] * zt[h]).astype(BF) for h in hs]
        drb = [dr_ref[:, RDV * h:RDV * (h + 1)] for h in hs]
        sb = [sp_ref[0, h].astype(BF) for h in hs]
        g = [gst[h] for h in hs]
        gb = [t.astype(BF) for t in g]
        ds = [(_dg(drb[h], vb[h], NT) * dm[h]).astype(BF) for h in hs]
        dq = [_dot(ds[h], kb[h]) + _dg(drb[h], sb[h], NT) * xi[h] for h in hs]
        dk = [(_dg(ds[h], qb[h], TN) + _dg(vb[h], gb[h], NT) * zt[h]) * RSCALE for h in hs]
        dv = [_dg(sd[h], drb[h], TN) + _dot(kz[h], gb[h]) for h in hs]
        gn = [g[h] * gd_ref[h] + _dg(qx[h], drb[h], TN) for h in hs]
        for h in hs:
            gst[h] = gn[h]
            dq_ref[:, RDK * h:RDK * (h + 1)] = (dq[h] * cos + pltpu.roll(dq[h] * sin, 64, 1)).astype(BF)
            dk_ref[:, RDK * h:RDK * (h + 1)] = (dk[h] * cos + pltpu.roll(dk[h] * sin, 64, 1)).astype(BF)
            dv_ref[:, RDV * h:RDV * (h + 1)] = dv[h].astype(BF)

    rev = lambda n: NCH - 1 - n
    return pl.pallas_call(
        body, name="ret_bwd", grid=(NCH,),
        in_specs=_ret_specs(rev) + [
            pl.BlockSpec((1, RH, RDK, RDV), lambda n: (rev(n), 0, 0, 0)),
            pl.BlockSpec((C, RH * RDV), lambda n: (rev(n), 0)),
        ],
        out_specs=[pl.BlockSpec((C, RH * RDK), lambda n: (rev(n), 0)),
                   pl.BlockSpec((C, RH * RDK), lambda n: (rev(n), 0)),
                   pl.BlockSpec((C, RH * RDV), lambda n: (rev(n), 0))],
        out_shape=[jax.ShapeDtypeStruct((T, RH * RDK), BF), jax.ShapeDtypeStruct((T, RH * RDK), BF),
                   jax.ShapeDtypeStruct((T, RH * RDV), BF)],
        scratch_shapes=[pltpu.VMEM((RH, RDK, RDV), F32)],
        compiler_params=_params(("arbitrary",)),
    )(z, z, z, cst["cos2"], cst["sin2"], cst["dmask"], cst["xi"], cst["zeta"], cst["gdec"], sprev, dr)


def _log_sigmoid(x):
    return -(jnp.maximum(-x, 0.0) + jnp.log1p(jnp.exp(-jnp.abs(x))))


def _fox_prep(zf, bf_pad, cst):
    def body(zf_ref, b_ref, tri_ref, ct_ref, carry):
        n = pl.program_id(0)

        @pl.when(n == 0)
        def _():
            carry[...] = jnp.zeros_like(carry)

        ls = _log_sigmoid(zf_ref[...] + b_ref[...])
        row = n * C + lax.broadcasted_iota(jnp.int32, (C, C), 0)
        lf = jnp.where(row >= PAD, ls, 0.0)
        cc = jnp.dot(tri_ref[...], lf, precision=HI, preferred_element_type=F32) + carry[0:1, :]
        carry[...] = jnp.broadcast_to(cc[C - 1:C, :], carry.shape)
        pos = n * C + lax.broadcasted_iota(jnp.int32, (FH, C), 1)
        ct_ref[0] = jnp.where(pos >= PAD, cc.T[:FH, :], -NEG)

    return pl.pallas_call(
        body, name="fox_prep", grid=(NCH,),
        in_specs=[pl.BlockSpec((C, C), lambda n: (n, 0)), pl.BlockSpec((1, C), lambda n: (0, 0)),
                  pl.BlockSpec((C, C), lambda n: (0, 0))],
        out_specs=pl.BlockSpec((1, FH, C), lambda n: (n, 0, 0)),
        out_shape=jax.ShapeDtypeStruct((NCH, FH, C), F32),
        scratch_shapes=[pltpu.VMEM((8, C), F32)],
        compiler_params=_params(("arbitrary",)),
    )(zf, bf_pad, cst["tri"])


def _lo_lanes(shape):
    return lax.broadcasted_iota(jnp.int32, shape, 1) < FD


def _split_heads(x):
    lo = _lo_lanes(x.shape)
    zero = jnp.zeros_like(x)
    return jnp.concatenate([jnp.where(lo, x, zero), jnp.where(lo, zero, x)], axis=0)


def _spread2(x):
    lo = _lo_lanes(x.shape)
    r = pltpu.roll(x, FD, 1)
    return jnp.concatenate([jnp.where(lo, x, r), jnp.where(lo, r, x)], axis=1)


NSTEP = (NCH + 1) // 2
NTILE = NCH + 1
TROWS = T + C


def _fox_tile(s, t):
    second = t > s
    return second.astype(jnp.int32), jnp.where(second, t - s - 1, s - t)


def _fox_pos(i):
    return jnp.where(i < NSTEP, 2 * i, 2 * (NCH - 1 - i) + 1)


FOX_ORDER = [2 * i if i < NSTEP else 2 * (NCH - 1 - i) + 1 for i in range(NCH)]


def _fox_pair_specs():
    first = pl.BlockSpec((C, C), lambda p, s: (2 * s, p))
    second = pl.BlockSpec((C, C), lambda p, s: (jnp.where(s == NSTEP - 1, 2 * s, 2 * s + 1), p))
    both = pl.BlockSpec((2 * C, C), lambda p, s: (s, p))
    return first, second, both


def _fox_q_specs():
    return (pl.BlockSpec((C, C), lambda p, s: (s, QB_F + p)),
            pl.BlockSpec((C, C), lambda p, s: (NCH - 1 - s, QB_F + p)))


def _fox_key_bias(ct_ref, p, j):
    return jnp.concatenate([ct_ref[j, pl.ds(2 * p, 1), :], ct_ref[j, pl.ds(2 * p + 1, 1), :]], axis=1)


def _fox_fwd(z, ct, cst):
    def body(qa_ref, qb_ref, k_ref, v_ref, ct_ref, ones_ref, mb_ref, a_ref, g_ref, kks, vvs, q2, m2, sbuf):
        p, s = pl.program_id(0), pl.program_id(1)

        @pl.when(s == 0)
        def _():
            ones = ones_ref[...]

            def prep(j, carry):
                rows = pl.ds(pl.multiple_of(j * C, C), C)
                kks[j] = _split_heads(k_ref[rows, :]).astype(BF)
                vvs[j] = jnp.concatenate([_split_heads(v_ref[rows, :]).astype(BF), ones], axis=1)
                return carry

            lax.fori_loop(0, NCH, prep, 0)

        q2[0] = (qa_ref[...] * FSCALE).astype(BF)
        q2[1] = (qb_ref[...] * FSCALE).astype(BF)

        tiles = [_fox_tile(s, t) for t in range(NTILE)]
        causal = mb_ref[1]
        neg = jnp.full((C, 2 * C), NEG, F32)
        run, first = neg, neg
        for t, (sel, j) in enumerate(tiles):
            st = _dg(q2[sel], kks[j], NT) - _fox_key_bias(ct_ref, p, j)
            if t in (0, NTILE - 1):
                st = st + causal
            sbuf[t] = st
            run = jnp.maximum(jnp.where(t == s + 1, neg, run), st)
            first = jnp.where(t == s, run, first)
        for w, mx in enumerate((first, run)):
            m2[w] = jnp.concatenate(
                [jnp.broadcast_to(jnp.max(mx[:, :C], axis=1, keepdims=True), (C, C)),
                 jnp.broadcast_to(jnp.max(mx[:, C:], axis=1, keepdims=True), (C, C))], axis=1)

        zero = jnp.zeros((C, 2 * C), F32)
        run, first = zero, zero
        for t, (sel, j) in enumerate(tiles):
            run = jnp.where(t == s + 1, zero, run) + _dot(jnp.exp(sbuf[t] - m2[sel]).astype(BF), vvs[j])
            first = jnp.where(t == s, run, first)
        lo = _lo_lanes((C, C))
        for w, res in enumerate((first, run)):
            l = res[:, C:]
            a_ref[C * w:C * (w + 1), :] = res[:, :C] / l
            mw = m2[w]
            g_ref[C * w:C * (w + 1), :] = -(jnp.where(lo, mw[:, :C], mw[:, C:]) + jnp.log(l))

    qa, qb = _fox_q_specs()
    both = _fox_pair_specs()[2]
    return pl.pallas_call(
        body, name="fox_fwd", grid=(NPAIR, NSTEP),
        in_specs=[qa, qb,
                  pl.BlockSpec((T, C), lambda p, s: (0, KB_F + p)),
                  pl.BlockSpec((T, C), lambda p, s: (0, VB_F + p)),
                  pl.BlockSpec((NCH, FH, C), lambda p, s: (0, 0, 0)),
                  pl.BlockSpec((2 * C, C), lambda p, s: (0, 0)),
                  pl.BlockSpec((2, C, 2 * C), lambda p, s: (0, 0, 0))],
        out_specs=[both, both],
        out_shape=[jax.ShapeDtypeStruct((TROWS, FH * FD), F32)] * 2,
        scratch_shapes=[pltpu.VMEM((NCH, 2 * C, C), BF), pltpu.VMEM((NCH, 2 * C, 2 * C), BF),
                        pltpu.VMEM((2, C, C), BF), pltpu.VMEM((2, C, 2 * C), F32),
                        pltpu.VMEM((NTILE, C, 2 * C), F32)],
        compiler_params=_params(("parallel", "arbitrary")),
    )(z, z, z, z, ct, cst["ones_aug"], cst["mask_bias"])


def _fox_bwd(z, da, g, delta, ct, cst):
    grp = 9

    def body(qa_ref, qb_ref, daa_ref, dab_ref, ga_ref, gb_ref, dla_ref, dlb_ref, k_ref, v_ref, ct_ref, ones_ref,
             mb_ref, dq_ref, dr_ref, dk_ref, dv_ref, dcs_ref,
             kks, vvs, q2, qq2, dd2, da2, gi2, dl2, dq2, dvb, dkb, dkacc, dvacc, csacc):
        p, s = pl.program_id(0), pl.program_id(1)
        ones = ones_ref[...]

        @pl.when(s == 0)
        def _():
            dkacc[...] = jnp.zeros_like(dkacc)
            dvacc[...] = jnp.zeros_like(dvacc)
            csacc[...] = jnp.zeros_like(csacc)

            def prep(j, carry):
                rows = pl.ds(pl.multiple_of(j * C, C), C)
                kks[j] = _split_heads(k_ref[rows, :]).astype(BF)
                vvs[j] = _split_heads(v_ref[rows, :]).astype(BF)
                return carry

            lax.fori_loop(0, NCH, prep, 0)

        for w, (q_ref, d_ref, g_ref, l_ref) in enumerate(((qa_ref, daa_ref, ga_ref, dla_ref),
                                                          (qb_ref, dab_ref, gb_ref, dlb_ref))):
            qf = q_ref[...]
            q2[w] = (qf * FSCALE).astype(BF)
            qq2[w] = jnp.concatenate([_split_heads(qf).astype(BF), ones], axis=1)
            da2[w] = d_ref[...]
            dd2[w] = _split_heads(d_ref[...].astype(F32)).astype(BF)
            gi2[w] = _spread2(g_ref[...])
            dl2[w] = _spread2(l_ref[...])
        dq2[...] = jnp.zeros_like(dq2)
        zero = jnp.zeros((C, 2 * C), F32)

        def group(gi, carry):
            ts = [gi * grp + u for u in range(grp)]
            tiles = [_fox_tile(s, t) for t in ts]
            kk = [kks[j] for _, j in tiles]
            ss = [_dg(q2[sel], kj, NT) + (gi2[sel] - _fox_key_bias(ct_ref, p, j)) for kj, (sel, j) in zip(kk, tiles)]
            ss[0] = ss[0] + mb_ref[(gi == 0).astype(jnp.int32)]
            ss[-1] = ss[-1] + mb_ref[(gi == 1).astype(jnp.int32)]
            dps = [_dg(da2[sel], vvs[j], NT) for sel, j in tiles]
            pes = [jnp.exp(st) for st in ss]
            dss = [pe * (dp - dl2[sel]) * FSCALE for pe, dp, (sel, _) in zip(pes, dps, tiles)]
            pts = [jnp.concatenate([pe[:, :C].T, pe[:, C:].T], axis=1).astype(BF) for pe in pes]
            dsts = [jnp.concatenate([ds[:, :C].T, ds[:, C:].T], axis=1).astype(BF) for ds in dss]
            dvs = [_dot(pt, dd2[sel]) for pt, (sel, _) in zip(pts, tiles)]
            rs = [_dot(dst, qq2[sel]) for dst, (sel, _) in zip(dsts, tiles)]
            parts = [_dot(ds.astype(BF), jnp.concatenate([kj, ones], axis=1)) for ds, kj in zip(dss, kk)]
            for t, dv, rr in zip(ts, dvs, rs):
                dvb[t] = dv
                dkb[t] = rr
            pa, pb = zero, zero
            for t, part in zip(ts, parts):
                pa = pa + jnp.where(t <= s, part, zero)
                pb = pb + jnp.where(t <= s, zero, part)
            dq2[0] += pa
            dq2[1] += pb
            return carry

        ntile = jnp.where(s == NSTEP - 1, grp, NTILE)
        lax.fori_loop(0, ntile // grp, group, 0)

        def scatter(t, carry):
            _, j = _fox_tile(s, t)
            r = pl.ds(pl.multiple_of(j * C, C), C)
            dvacc[r, :] += dvb[t]
            dkacc[r, :] += dkb[t, :, :C]
            csacc[r, :] += dkb[t, :, C:]
            return carry

        lax.fori_loop(0, ntile, scatter, 0)
        for w in range(2):
            res = dq2[w]
            dq_ref[C * w:C * (w + 1), :] = res[:, :C].astype(BF)
            dr_ref[C * w:C * (w + 1), :] = res[:, C:]

        @pl.when(s == NSTEP - 1)
        def _():
            dk_ref[...] = dkacc[...].astype(BF)
            dv_ref[...] = dvacc[...].astype(BF)
            dcs_ref[...] = csacc[...]

    qa, qb = _fox_q_specs()
    ba, bb, both = _fox_pair_specs()
    col = pl.BlockSpec((T, C), lambda p, s: (0, p))
    return pl.pallas_call(
        body, name="fox_bwd", grid=(NPAIR, NSTEP),
        in_specs=[qa, qb, ba, bb, ba, bb, ba, bb,
                  pl.BlockSpec((T, C), lambda p, s: (0, KB_F + p)),
                  pl.BlockSpec((T, C), lambda p, s: (0, VB_F + p)),
                  pl.BlockSpec((NCH, FH, C), lambda p, s: (0, 0, 0)),
                  pl.BlockSpec((2 * C, C), lambda p, s: (0, 0)),
                  pl.BlockSpec((2, C, 2 * C), lambda p, s: (0, 0, 0))],
        out_specs=[both, both, col, col, col],
        out_shape=[jax.ShapeDtypeStruct((TROWS, FH * FD), BF), jax.ShapeDtypeStruct((TROWS, FH * FD), F32),
                   jax.ShapeDtypeStruct((T, FH * FD), BF), jax.ShapeDtypeStruct((T, FH * FD), BF),
                   jax.ShapeDtypeStruct((T, FH * FD), F32)],
        scratch_shapes=[pltpu.VMEM((NCH, 2 * C, C), BF), pltpu.VMEM((NCH, 2 * C, C), BF),
                        pltpu.VMEM((2, C, C), BF), pltpu.VMEM((2, 2 * C, 2 * C), BF), pltpu.VMEM((2, 2 * C, C), BF),
                        pltpu.VMEM((2, C, C), BF), pltpu.VMEM((2, C, 2 * C), F32), pltpu.VMEM((2, C, 2 * C), F32),
                        pltpu.VMEM((2, C, 2 * C), F32),
                        pltpu.VMEM((NTILE, C, C), F32), pltpu.VMEM((NTILE, C, 2 * C), F32),
                        pltpu.VMEM((T, C), F32), pltpu.VMEM((T, C), F32), pltpu.VMEM((T, C), F32)],
        compiler_params=_params(("parallel", "arbitrary")),
    )(z, z, da, da, g, g, delta, delta, z, z, ct, cst["ones_aug"], cst["mask_bias"])


def _fox_gate_bwd(drow, dcol, zf, bf_pad, cst):
    def body(dr_ref, dc_ref, zf_ref, b_ref, tri_ref, pick_ref, dff_ref, db_ref, carry):
        s = pl.program_id(0)
        n = NCH - 1 - s

        @pl.when(s == 0)
        def _():
            carry[...] = jnp.zeros_like(carry)
            db_ref[...] = jnp.zeros_like(db_ref)

        dcb = jnp.dot((dr_ref[...] - dc_ref[...]) * (1.0 / FSCALE), pick_ref[...], precision=HI,
                      preferred_element_type=F32)
        suf = lax.dot_general(tri_ref[...], dcb, TN, precision=HI, preferred_element_type=F32) + carry[0:1, :]
        carry[...] = jnp.broadcast_to(suf[0:1, :], carry.shape)
        x = zf_ref[...] + b_ref[...]
        row = n * C + lax.broadcasted_iota(jnp.int32, (C, C), 0)
        dff = jnp.where(row >= PAD, suf * (1.0 - jax.nn.sigmoid(x)), 0.0)
        dff_ref[...] = dff.astype(BF)
        db_ref[...] += jnp.sum(dff, axis=0, keepdims=True)

    rev = lambda s: (NCH - 1 - s, 0)
    return pl.pallas_call(
        body, name="fox_gate_bwd", grid=(NCH,),
        in_specs=[pl.BlockSpec((C, FH * FD), lambda s: (_fox_pos(NCH - 1 - s), 0)),
                  pl.BlockSpec((C, FH * FD), rev), pl.BlockSpec((C, C), rev),
                  pl.BlockSpec((1, C), lambda s: (0, 0)), pl.BlockSpec((C, C), lambda s: (0, 0)),
                  pl.BlockSpec((FH * FD, C), lambda s: (0, 0))],
        out_specs=[pl.BlockSpec((C, C), rev), pl.BlockSpec((1, C), lambda s: (0, 0))],
        out_shape=[jax.ShapeDtypeStruct((T, C), BF), jax.ShapeDtypeStruct((1, C), F32)],
        scratch_shapes=[pltpu.VMEM((8, C), F32)],
        compiler_params=_params(("arbitrary",)),
    )(drow, dcol, zf, bf_pad, cst["tri"], cst["pick"])


def _gated(r, rg, a, fg):
    rn, rs = [], []
    for h in range(RH):
        rh = r[:, RDV * h:RDV * (h + 1)]
        s = lax.rsqrt(jnp.mean(rh * rh, axis=1, keepdims=True) + EPS)
        rn.append(rh * s)
        rs.append(s)
    rn = jnp.concatenate(rn, axis=1)
    y = jnp.concatenate([rn * (rg * jax.nn.sigmoid(rg)), a * (fg * jax.nn.sigmoid(fg))], axis=1)
    return y, rn, rs


def _out_loss(r, z, a, wout, x, tgt, fgain):
    def body(r_ref, rg_ref, a_ref, fg_ref, w_ref, x_ref, t_ref, g_ref, yt_ref, do_ref, dob_ref, loss_ref, dg_ref):
        i = pl.program_id(0)

        @pl.when(i == 0)
        def _():
            yt_ref[...] = jnp.zeros_like(yt_ref)
            do_ref[...] = jnp.zeros_like(do_ref)
            dob_ref[...] = jnp.zeros_like(dob_ref)
            loss_ref[...] = jnp.zeros_like(loss_ref)
            dg_ref[...] = jnp.zeros_like(dg_ref)

        @pl.when(i > 0)
        def _():
            y, _, _ = _gated(r_ref[...], rg_ref[...], a_ref[...], fg_ref[...])
            yt_ref[...] = y.T.astype(BF)
            o = x_ref[...] + _dot(y.astype(BF), w_ref[...])
            rs = lax.rsqrt(jnp.mean(o * o, axis=1, keepdims=True) + EPS)
            on = o * rs
            g = g_ref[...]
            e = on * g - t_ref[...]
            loss_ref[...] += 0.5 * jnp.sum(jnp.mean(e * e, axis=1, keepdims=True))
            dyh = e * (1.0 / D)
            dg_ref[...] += jnp.sum(dyh * on, axis=0, keepdims=True)
            don = dyh * g
            do = rs * (don - on * jnp.mean(don * on, axis=1, keepdims=True))
            do_ref[...] = do
            dob_ref[...] = do.astype(BF)

    tok = lambda i: (jnp.maximum(i - 1, 0), 0)
    return pl.pallas_call(
        body, name="out_loss", grid=(NCH,),
        in_specs=[pl.BlockSpec((C, D), lambda i: (i, 0)), pl.BlockSpec((C, D), lambda i: (i, GB_R)),
                  pl.BlockSpec((C, D), lambda i: (_fox_pos(i), 0)), pl.BlockSpec((C, D), lambda i: (i, GB_F)),
                  pl.BlockSpec((DMIX, D), lambda i: (0, 0)),
                  pl.BlockSpec((C, D), tok), pl.BlockSpec((C, D), tok), pl.BlockSpec((1, D), lambda i: (0, 0))],
        out_specs=[pl.BlockSpec((DMIX, C), lambda i: (0, i)), pl.BlockSpec((C, D), lambda i: (i, 0)),
                   pl.BlockSpec((C, D), lambda i: (i, 0)), pl.BlockSpec((8, C), lambda i: (0, 0)),
                   pl.BlockSpec((1, D), lambda i: (0, 0))],
        out_shape=[jax.ShapeDtypeStruct((DMIX, T), BF), jax.ShapeDtypeStruct((T, D), F32),
                   jax.ShapeDtypeStruct((T, D), BF), jax.ShapeDtypeStruct((8, C), F32),
                   jax.ShapeDtypeStruct((1, D), F32)],
        compiler_params=_params(("arbitrary",)),
    )(r, z, a, z, wout, x, tgt, fgain)


def _dsilu(x):
    s = jax.nn.sigmoid(x)
    return s * (1.0 + x * (1.0 - s))


def _dy_gate_bwd(dob, wout, r, z, a, seg):
    def body(do_ref, w_ref, r_ref, rg_ref, a_ref, fg_ref, seg_ref, dr_ref, da_ref, drg_ref, dfg_ref, dl_ref):
        dy = _dg(do_ref[...], w_ref[...], NT)
        rg, fg, a_ = rg_ref[...], fg_ref[...], a_ref[...]
        _, rn, rs = _gated(r_ref[...], rg, a_, fg)
        dyr, dyf = dy[:, :D], dy[:, D:]
        drn = dyr * (rg * jax.nn.sigmoid(rg))
        drg_ref[...] = (dyr * rn * _dsilu(rg)).astype(BF)
        for h in range(RH):
            sl = slice(RDV * h, RDV * (h + 1))
            dh, nh = drn[:, sl], rn[:, sl]
            dr_ref[:, sl] = (rs[h] * (dh - nh * jnp.mean(dh * nh, axis=1, keepdims=True))).astype(BF)
        dab = (dyf * (fg * jax.nn.sigmoid(fg))).astype(BF)
        da_ref[...] = dab
        dfg_ref[...] = (dyf * a_ * _dsilu(fg)).astype(BF)
        prod = dab.astype(F32) * a_
        segm = seg_ref[...]
        for p in range(NPAIR):
            sl = slice(C * p, C * (p + 1))
            hi = prod[:, sl].astype(BF)
            lo = (prod[:, sl] - hi.astype(F32)).astype(BF)
            dl_ref[:, sl] = _dot(hi, segm) + _dot(lo, segm)

    row = pl.BlockSpec((C, D), lambda i: (i, 0))
    fox = pl.BlockSpec((C, D), lambda i: (_fox_pos(i), 0))
    return pl.pallas_call(
        body, name="dy_gate_bwd", grid=(NCH,),
        in_specs=[row, pl.BlockSpec((DMIX, D), lambda i: (0, 0)),
                  row, pl.BlockSpec((C, D), lambda i: (i, GB_R)),
                  fox, pl.BlockSpec((C, D), lambda i: (i, GB_F)),
                  pl.BlockSpec((C, C), lambda i: (0, 0))],
        out_specs=[row, fox, row, row, fox],
        out_shape=[jax.ShapeDtypeStruct((T, D), BF), jax.ShapeDtypeStruct((TROWS, D), BF),
                   jax.ShapeDtypeStruct((T, D), BF), jax.ShapeDtypeStruct((T, D), BF),
                   jax.ShapeDtypeStruct((TROWS, D), F32)],
        compiler_params=_params(("parallel",)),
    )(dob, wout, r, z, a, z, seg)


DZ_WIDTHS = (512, 512, 1024, 1024, 1024, 1024, 1024, 1024)


def _du_norm_bwd(dzs, dzf, wt, wft, hpad, g, dopad, parts=()):
    tm, tk = 544, 1024
    nk = WMAIN // tk
    ni = T // tm
    n = len(parts)

    def body(rq_ref, rk_ref, rv_ref, rg_ref, fq_ref, fk_ref, fv_ref, fg_ref, dzf_ref, w_ref, wf_ref, h_ref, g_ref,
             do_ref, *rest):
        part_refs, (gh_ref, dg_ref), land_refs = rest[:n], rest[n:n + 2], rest[n + 2:2 * n + 2]
        acc = rest[2 * n + 2]
        i, k = pl.program_id(0), pl.program_id(1)

        if n:
            send_sems, recv_sems = rest[2 * n + 3:]
            x, y, c = _place()
            me_s = 2 * x + y
            copies = [pltpu.make_async_remote_copy(
                src_ref=part_refs[a].at[cs], dst_ref=land_refs[a].at[me_s],
                send_sem=send_sems.at[3 * a + j], recv_sem=recv_sems.at[3 * a + j],
                device_id=(cx, cy, c), device_id_type=MESH)
                for a in range(n) for j, (cx, cy, cs) in enumerate(_other_chips(x, y))]

            @pl.when((i == 0) & (k == 0))
            def _():
                for cp in copies:
                    cp.start()

            @pl.when((i == ni - 1) & (k == nk - 1))
            def _():
                for cp in copies:
                    cp.wait()

        @pl.when(k == 0)
        def _():
            acc[...] = (_dot(dzf_ref[...], wf_ref[...]) + _dot(rq_ref[...], w_ref[:512, :])
                        + _dot(rk_ref[...], w_ref[512:, :]))

        for kk, piece in enumerate((rv_ref, rg_ref, fq_ref, fk_ref, fv_ref, fg_ref), start=1):
            @pl.when(k == kk)
            def _(piece=piece):
                acc[...] += _dot(piece[...], w_ref[...])

        @pl.when(k == nk - 1)
        def _():
            du = acc[...]
            h = h_ref[...]
            gg = g_ref[...]
            rs = lax.rsqrt(jnp.mean(h * h, axis=1, keepdims=True) + EPS)
            hn = h * rs
            part = jnp.sum(du * hn, axis=0, keepdims=True)

            @pl.when(i == 0)
            def _():
                dg_ref[...] = part

            @pl.when(i > 0)
            def _():
                dg_ref[...] += part

            dhn = du * gg
            gh_ref[...] = rs * (dhn - hn * jnp.mean(dhn * hn, axis=1, keepdims=True)) + do_ref[...]

    sems = [pltpu.SemaphoreType.DMA((3 * n,)), pltpu.SemaphoreType.DMA((3 * n,))] if n else []
    return pl.pallas_call(
        body, name="du_norm_bwd", grid=(ni, nk),
        in_specs=[pl.BlockSpec((tm, w), lambda i, k: (i, 0)) for w in DZ_WIDTHS]
        + [pl.BlockSpec((tm, C), lambda i, k: (i, 0)),
           pl.BlockSpec((tk, D), lambda i, k: (k, 0)), pl.BlockSpec((C, D), lambda i, k: (0, 0)),
           pl.BlockSpec((tm, D), lambda i, k: (i, 0)), pl.BlockSpec((1, D), lambda i, k: (0, 0)),
           pl.BlockSpec((tm, D), lambda i, k: (i, 0))] + [ANY] * n,
        out_specs=[pl.BlockSpec((tm, D), lambda i, k: (i, 0)), pl.BlockSpec((1, D), lambda i, k: (0, 0))] + [ANY] * n,
        out_shape=[jax.ShapeDtypeStruct((T, D), F32), jax.ShapeDtypeStruct((1, D), F32)]
        + [jax.ShapeDtypeStruct(p.shape, p.dtype) for p in parts],
        scratch_shapes=[pltpu.VMEM((tm, D), F32)] + sems,
        compiler_params=_params(("arbitrary", "arbitrary")),
    )(*dzs, dzf, wt, wft, hpad, g, dopad, *parts)


GROWS = 7424


def _dw_in(dzs, dzf, ut):
    tn = 256
    nmain = WMAIN // tn
    first, blocks = [], []
    for w in DZ_WIDTHS:
        first.append(sum(blocks))
        blocks.append(w // tn)

    def body(rq_ref, rk_ref, rv_ref, rg_ref, fq_ref, fk_ref, fv_ref, fg_ref, dzf_ref, ut_ref, o_ref):
        gidx = pl.program_id(0)
        for piece, g0, nb in zip((rq_ref, rk_ref, rv_ref, rg_ref, fq_ref, fk_ref, fv_ref, fg_ref), first, blocks):
            @pl.when((gidx >= g0) & (gidx < g0 + nb))
            def _(piece=piece):
                o_ref[...] = _dot(ut_ref[...], piece[...]).T

        @pl.when(gidx == nmain)
        def _():
            o_ref[:C, :] = _dot(ut_ref[...], dzf_ref[...]).T
            o_ref[C:, :] = jnp.zeros((tn - C, D), F32)

    def piece_spec(g0, nb):
        return pl.BlockSpec((T, tn), lambda gidx: (0, jnp.clip(gidx - g0, 0, nb - 1)))

    return pl.pallas_call(
        body, name="dw_in", grid=(nmain + 1,),
        in_specs=[piece_spec(g0, nb) for g0, nb in zip(first, blocks)]
        + [pl.BlockSpec((T, C), lambda gidx: (0, 0)), pl.BlockSpec((D, T), lambda gidx: (0, 0))],
        out_specs=pl.BlockSpec((tn, D), lambda gidx: (gidx, 0)),
        out_shape=jax.ShapeDtypeStruct((GROWS, D), F32),
        compiler_params=_params(("arbitrary",)),
    )(*dzs, dzf, ut)


def _token_order(x_po):
    def body(i_ref, o_ref):
        o_ref[...] = i_ref[...]

    return pl.pallas_call(
        body, name="token_order", grid=(NCH,),
        in_specs=[pl.BlockSpec((C, D), lambda i: (_fox_pos(i), 0))],
        out_specs=pl.BlockSpec((C, D), lambda i: (i, 0)),
        out_shape=jax.ShapeDtypeStruct((T, D), x_po.dtype),
        compiler_params=_params(("parallel",)),
    )(x_po)


def _local_step(x, tgt, meta, norm_g, wt, wft, b_f, wout, final_g, chip_sums=None):
    cst = _constants()
    hpad = jnp.concatenate([jnp.pad(meta, ((PAD, 0), (0, 0))), x], axis=0)
    bf_pad = jnp.pad(b_f, ((0, 0), (0, C - NFF)))
    u, ut = _norm_in(hpad, norm_g)
    z = _mm_nt(u, wt, WMAIN, T // 2, 512, "in_proj")
    zf = _mm_nt(u, wft, C, T // 2, C, "in_proj_ff")
    r, sprev = _ret_fwd(z, cst)
    ct = _fox_prep(zf, bf_pad, cst)
    a, g = _fox_fwd(z, ct, cst)
    yt, dopad, dob, loss8, dfg = _out_loss(r, z, a, wout, x, tgt, final_g)
    dr, da, dzrg, dzfg, delta = _dy_gate_bwd(dob, wout, r, z, a, cst["seg"])
    dwout = _mm_nn(yt, dob, 512, D, "dw_out")
    dzq_r, dzk_r, dzv_r = _ret_bwd(z, cst, sprev, dr)
    dq_po, drow, dzk_f, dzv_f, dcol = _fox_bwd(z, da, g, delta, ct, cst)
    dzf, dbf = _fox_gate_bwd(drow, dcol, zf, bf_pad, cst)
    dzs = [dzq_r, dzk_r, dzv_r, dzrg, _token_order(dq_po), dzk_f, dzv_f, dzfg]
    gwt = _dw_in(dzs, dzf, ut)
    parts = chip_sums(gwt, dwout) if chip_sums else []
    gh, dng, *landed = _du_norm_bwd(dzs, dzf, wt, wft, hpad, norm_g, dopad, parts)
    return (loss8[0, 0], gh[C:], gh[PAD:C], dng, gwt, dbf[:, :NFF], dwout, dfg, parts, landed)


def _place():
    x, y, c = lax.axis_index("x"), lax.axis_index("y"), lax.axis_index("c")
    return x, y, c


def _other_chips(x, y):
    return [(1 - x, y, 2 * (1 - x) + y), (x, 1 - y, 2 * x + (1 - y)), (1 - x, 1 - y, 2 * (1 - x) + (1 - y))]


def _all_gather_shards(shards):
    n = len(shards)

    def body(*refs):
        ins, outs = refs[:n], refs[n:2 * n]
        send_sems, recv_sems = refs[2 * n:]
        x, y, c = _place()
        me_s = 2 * x + y
        sib = (x, y, 1 - c)
        chips = _other_chips(x, y)
        sends, waits = [], []
        for a in range(n):
            rows = ins[a].shape[0] // 2
            half = pl.ds(c * rows, rows)
            for k, (cx, cy, cs) in enumerate(chips):
                sends.append(pltpu.make_async_remote_copy(
                    src_ref=ins[a].at[half], dst_ref=outs[a].at[me_s, half],
                    send_sem=send_sems.at[6 * a + k], recv_sem=recv_sems.at[6 * a + k],
                    device_id=(cx, cy, c), device_id_type=MESH))
                sends[-1].start()
        for a in range(n):
            rows = ins[a].shape[0] // 2
            half = pl.ds(c * rows, rows)
            other = pl.ds((1 - c) * rows, rows)
            for k, (cx, cy, cs) in enumerate(chips):
                pltpu.make_async_remote_copy(
                    src_ref=outs[a].at[cs, half], dst_ref=outs[a].at[cs, half],
                    send_sem=send_sems.at[6 * a + k], recv_sem=recv_sems.at[6 * a + k],
                    device_id=(cx, cy, c), device_id_type=MESH).wait_recv()
                fwd = pltpu.make_async_remote_copy(
                    src_ref=outs[a].at[cs, half], dst_ref=outs[a].at[cs, half],
                    send_sem=send_sems.at[6 * a + 3 + k], recv_sem=recv_sems.at[6 * a + 3 + k],
                    device_id=sib, device_id_type=MESH)
                fwd.start()
                sends.append(fwd)
                waits.append(pltpu.make_async_remote_copy(
                    src_ref=outs[a].at[cs, other], dst_ref=outs[a].at[cs, other],
                    send_sem=send_sems.at[6 * a + 3 + k], recv_sem=recv_sems.at[6 * a + 3 + k],
                    device_id=sib, device_id_type=MESH))
        for w in waits:
            w.wait_recv()
        for s in sends:
            s.wait_send()

    return pl.pallas_call(
        body, name="all_gather_w",
        in_specs=[ANY] * n, out_specs=[ANY] * n,
        out_shape=[jax.ShapeDtypeStruct((4,) + s.shape, s.dtype) for s in shards],
        scratch_shapes=[pltpu.SemaphoreType.DMA((6 * n,)), pltpu.SemaphoreType.DMA((6 * n,))],
    )(*shards)


WOFF, WLEN = 1792, 2048
WHALF = WLEN // 2


def _pair_swap(gwt, arrs):
    n = len(arrs)

    def body(*refs):
        gw, ins = refs[0], refs[1:n + 1]
        gwo, outs = refs[n + 1], refs[n + 2:2 * n + 2]
        send_sems, recv_sems = refs[2 * n + 2:]
        x, y, c = _place()
        sib = (x, y, 1 - c)
        cps = []
        for k in range(4):
            cps.append(pltpu.make_async_remote_copy(
                src_ref=gw.at[pl.ds(WOFF * k + (1 - c) * WHALF, WHALF)], dst_ref=gwo.at[k],
                send_sem=send_sems.at[k], recv_sem=recv_sems.at[k], device_id=sib, device_id_type=MESH))
        for a in range(n):
            rows = ins[a].shape[1] // 2
            cps.append(pltpu.make_async_remote_copy(
                src_ref=ins[a].at[:, pl.ds((1 - c) * rows, rows)], dst_ref=outs[a],
                send_sem=send_sems.at[4 + a], recv_sem=recv_sems.at[4 + a], device_id=sib, device_id_type=MESH))
        for cp in cps:
            cp.start()
        for cp in cps:
            cp.wait()

    return pl.pallas_call(
        body, name="rs_pair_swap",
        in_specs=[ANY] * (n + 1), out_specs=[ANY] * (n + 1),
        out_shape=[jax.ShapeDtypeStruct((4, WHALF, D), gwt.dtype)]
        + [jax.ShapeDtypeStruct((4, a.shape[1] // 2, a.shape[2]), a.dtype) for a in arrs],
        scratch_shapes=[pltpu.SemaphoreType.DMA((n + 4,)), pltpu.SemaphoreType.DMA((n + 4,))],
    )(gwt, *arrs)


def _add_windows(gwt, recv):
    tb = 256
    nb = WHALF // tb
    c = lax.axis_index("c")

    def body(c_ref, a_ref, b_ref, o_ref):
        o_ref[0] = (a_ref[...] + b_ref[0]).astype(BF)

    return pl.pallas_call(
        body, name="pair_add_in",
        grid_spec=pltpu.PrefetchScalarGridSpec(
            num_scalar_prefetch=1, grid=(4, nb),
            in_specs=[pl.BlockSpec((tb, D), lambda k, i, cr: ((WOFF // tb) * k + nb * cr[0] + i, 0)),
                      pl.BlockSpec((1, tb, D), lambda k, i, cr: (k, i, 0))],
            out_specs=pl.BlockSpec((1, tb, D), lambda k, i, cr: (k, i, 0))),
        out_shape=jax.ShapeDtypeStruct(recv.shape, BF),
        compiler_params=_params(("parallel", "parallel")),
    )(jnp.reshape(c, (1,)).astype(jnp.int32), gwt, recv)


def _chip_exchange(parts, small):
    n = len(parts)

    def body(*refs):
        ins, sm = refs[:n], refs[n]
        outs, smo = refs[n + 1:2 * n + 1], refs[2 * n + 1]
        send_sems, recv_sems = refs[2 * n + 2:]
        x, y, c = _place()
        me_s = 2 * x + y
        chips = _other_chips(x, y)
        cps = []
        for a in range(n + 1):
            src = ins[a] if a < n else sm
            dst = outs[a] if a < n else smo
            for k, (cx, cy, cs) in enumerate(chips):
                cps.append(pltpu.make_async_remote_copy(
                    src_ref=src.at[cs] if a < n else src, dst_ref=dst.at[me_s],
                    send_sem=send_sems.at[3 * a + k], recv_sem=recv_sems.at[3 * a + k],
                    device_id=(cx, cy, c), device_id_type=MESH))
        for cp in cps:
            cp.start()
        for cp in cps:
            cp.wait()

    return pl.pallas_call(
        body, name="rs_chip_exchange",
        in_specs=[ANY] * (n + 1), out_specs=[ANY] * (n + 1),
        out_shape=[jax.ShapeDtypeStruct(p.shape, p.dtype) for p in parts]
        + [jax.ShapeDtypeStruct((4,) + small.shape, small.dtype)],
        scratch_shapes=[pltpu.SemaphoreType.DMA((3 * (n + 1),)), pltpu.SemaphoreType.DMA((3 * (n + 1),))],
    )(*parts, small)


def _pair_send(halves):
    n = len(halves)

    def body(*refs):
        ins, outs = refs[:n], refs[n:2 * n]
        send_sems, recv_sems = refs[2 * n:]
        x, y, c = _place()
        cps = [pltpu.make_async_remote_copy(
            src_ref=ins[a], dst_ref=outs[a], send_sem=send_sems.at[a], recv_sem=recv_sems.at[a],
            device_id=(x, y, 1 - c), device_id_type=MESH) for a in range(n)]
        for cp in cps:
            cp.start()
        for cp in cps:
            cp.wait()

    return pl.pallas_call(
        body, name="rs_pair_send",
        in_specs=[ANY] * n, out_specs=[ANY] * n,
        out_shape=[jax.ShapeDtypeStruct(h.shape, h.dtype) for h in halves],
        scratch_shapes=[pltpu.SemaphoreType.DMA((n,)), pltpu.SemaphoreType.DMA((n,))],
    )(*halves)


def _row_block(rows):
    for tb in (256, 128, 64, 32, 16, 8):
        if rows % tb == 0:
            return tb
    return rows


def _add_halves(full, recv, name, out_dtype):
    _, r2, w = recv.shape
    tb = _row_block(r2)
    nb = r2 // tb
    c = lax.axis_index("c")

    def body(c_ref, a_ref, b_ref, o_ref):
        o_ref[...] = (a_ref[...] + b_ref[...]).astype(o_ref.dtype)

    return pl.pallas_call(
        body, name=name,
        grid_spec=pltpu.PrefetchScalarGridSpec(
            num_scalar_prefetch=1, grid=(4, nb),
            in_specs=[pl.BlockSpec((1, tb, w), lambda s, i, cr: (s, cr[0] * nb + i, 0)),
                      pl.BlockSpec((1, tb, w), lambda s, i, cr: (s, i, 0))],
            out_specs=pl.BlockSpec((1, tb, w), lambda s, i, cr: (s, i, 0))),
        out_shape=jax.ShapeDtypeStruct(recv.shape, out_dtype),
        compiler_params=_params(("parallel", "parallel")),
    )(jnp.reshape(c, (1,)).astype(jnp.int32), full, recv)


def _add2(a, b, name):
    def body(a_ref, b_ref, o_ref):
        o_ref[...] = a_ref[...] + b_ref[...]

    return pl.pallas_call(body, name=name, out_shape=jax.ShapeDtypeStruct(a.shape, a.dtype))(a, b)


def _sum4(buf, own, name):
    _, r, w = buf.shape
    tb = _row_block(r)
    me_s = 2 * lax.axis_index("x") + lax.axis_index("y")
    by_dest = own.ndim == 3

    def body(s_ref, b_ref, own_ref, o_ref):
        mine = (own_ref[0] if by_dest else own_ref[...]).astype(F32)
        terms = [jnp.where(s_ref[0] == t, mine, b_ref[t].astype(F32)) for t in range(4)]
        o_ref[...] = ((terms[0] + terms[1]) + terms[2]) + terms[3]

    own_spec = (pl.BlockSpec((1, tb, w), lambda i, sr: (sr[0], i, 0)) if by_dest
                else pl.BlockSpec((tb, w), lambda i, sr: (i, 0)))
    return pl.pallas_call(
        body, name=name,
        grid_spec=pltpu.PrefetchScalarGridSpec(
            num_scalar_prefetch=1, grid=(r // tb,),
            in_specs=[pl.BlockSpec((4, tb, w), lambda i, sr: (0, i, 0)), own_spec],
            out_specs=pl.BlockSpec((tb, w), lambda i, sr: (i, 0))),
        out_shape=jax.ShapeDtypeStruct((r, w), F32),
        compiler_params=_params(("parallel",)),
    )(jnp.reshape(me_s, (1,)).astype(jnp.int32), buf, own)


def _adamw_math(w, g, m, v):
    mn = B1 * m + (1.0 - B1) * g
    vn = B2 * v + (1.0 - B2) * (g * g)
    m_hat = mn / (1.0 - B1 ** STEP)
    v_hat = vn / (1.0 - B2 ** STEP)
    return -LR * (m_hat / (jnp.sqrt(v_hat) + AEPS) + WD * w), mn, vn


def _adamw(w, g, m, v, name):
    r, c_ = w.shape
    tb = _row_block(r)
    if tb == r and r > 512:
        tb = 256

    def body(w_ref, g_ref, m_ref, v_ref, d_ref, mo_ref, vo_ref):
        d_ref[...], mo_ref[...], vo_ref[...] = _adamw_math(w_ref[...], g_ref[...], m_ref[...], v_ref[...])

    spec = pl.BlockSpec((tb, c_), lambda i: (i, 0))
    return pl.pallas_call(
        body, name=name, grid=(pl.cdiv(r, tb),),
        in_specs=[spec] * 4, out_specs=[spec] * 3,
        out_shape=[jax.ShapeDtypeStruct(w.shape, F32)] * 3,
        compiler_params=_params(("parallel",)),
    )(w, g, m, v)


def _adamw_halves(w, g_mine, g_sib, m, v, name):
    r, c_ = w.shape
    r2 = g_mine.shape[0]
    tb = _row_block(r2)
    nb = r2 // tb
    c = lax.axis_index("c")

    def body(c_ref, w_ref, gm_ref, gs_ref, m_ref, v_ref, g_ref, d_ref, mo_ref, vo_ref):
        g = jnp.where(pl.program_id(0) == c_ref[0], gm_ref[...], gs_ref[...])
        g_ref[...] = g
        d_ref[...], mo_ref[...], vo_ref[...] = _adamw_math(w_ref[...], g, m_ref[...], v_ref[...])

    full = pl.BlockSpec((tb, c_), lambda h, i, cr: (h * nb + i, 0))
    half = pl.BlockSpec((tb, c_), lambda h, i, cr: (i, 0))
    return pl.pallas_call(
        body, name=name,
        grid_spec=pltpu.PrefetchScalarGridSpec(
            num_scalar_prefetch=1, grid=(2, nb),
            in_specs=[full, half, half, full, full], out_specs=[full] * 4),
        out_shape=[jax.ShapeDtypeStruct(w.shape, F32)] * 4,
        compiler_params=_params(("parallel", "parallel")),
    )(jnp.reshape(c, (1,)).astype(jnp.int32), w, g_mine, g_sib, m, v)


def kernel(x, meta_tokens, norm_g, w_in, b_f, w_out, final_g, loss_target, m_meta_tokens, m_norm_g, m_w_in, m_b_f, m_w_out, m_final_g, v_meta_tokens, v_norm_g, v_w_in, v_b_f, v_w_out, v_final_g):
    me_s = 2 * lax.axis_index("x") + lax.axis_index("y")
    core = lax.axis_index("c")
    wt, mt, vt = [jnp.swapaxes(t[0], 0, 1) for t in (w_in, m_w_in, v_w_in)]

    own_win = lax.dynamic_update_slice(jnp.zeros((WPADROWS, D), F32), wt, (4 * me_s, 0)).astype(BF)
    own = [own_win, w_out[0].astype(BF), meta_tokens]
    gathered = _all_gather_shards(own)
    mine = jnp.arange(4) == me_s
    win, gout, gmeta = [jnp.where(mine[:, None, None], o[None], g) for o, g in zip(own, gathered)]
    lap = WPADROWS - WOFF
    tails = jnp.concatenate([jnp.zeros((1, lap, D), BF), win[:-1, WOFF:]], axis=0)
    wt_main = jnp.concatenate([win[:, :lap] + tails, win[:, lap:WOFF]], axis=1).reshape(WMAIN, D)
    wft = jnp.pad(win[3, WOFF:WOFF + NFF], ((0, C - NFF), (0, 0)))
    wout = gout.reshape(DMIX, D)
    meta = jnp.concatenate([gmeta[s] for s in range(4)], axis=1)

    def chip_sums(gwt, dwout):
        g_out = dwout.reshape(4, DMIX // 4, D)
        r_in, r_out = _pair_swap(gwt, [g_out])
        return [_add_windows(gwt, r_in), _add_halves(g_out, r_out, "pair_add_out", BF)]

    loss, gx, dmeta, dng, gwt, dbf, dwout, dfg, (p_in, p_out), (e_in, e_out) = _local_step(
        x[0], loss_target[0], meta, norm_g, wt_main, wft, b_f, wout, final_g.reshape(1, D), chip_sums)

    g_meta = jnp.stack([dmeta[:, 256 * s:256 * (s + 1)] for s in range(4)])
    small = jnp.concatenate([dng, dfg, jnp.pad(dbf, ((0, 0), (0, D - NFF))),
                             jnp.pad(jnp.reshape(loss, (1, 1)), ((0, 0), (0, D - 1))),
                             jnp.zeros((4, D), F32)], axis=0)
    e_meta, e_small = _chip_exchange([g_meta], small)
    h_in, h_out = _sum4(e_in, p_in, "sum_in"), _sum4(e_out, p_out, "sum_out")
    h_meta, h_small = _sum4(e_meta, g_meta, "sum_meta"), _sum4(e_small, small, "sum_small")
    s_in, s_out, s_meta, s_small = _pair_send([h_in, h_out, h_meta, h_small])
    gw_meta = _add2(h_meta, s_meta, "pair_add_meta")
    tot = _add2(h_small, s_small, "pair_add_small")
    g_norm, g_final, g_bf, loss_all = tot[0:1], tot[1], tot[2:3, :NFF], tot[3, 0]

    d_meta, nm_meta, nv_meta = _adamw(meta_tokens, gw_meta, m_meta_tokens, v_meta_tokens, "adamw_meta")
    d_norm, nm_norm, nv_norm = _adamw(norm_g, g_norm, m_norm_g, v_norm_g, "adamw_norm")
    window = jnp.concatenate([jnp.where(core == 0, h_in, s_in), jnp.where(core == 0, s_in, h_in)], axis=0)
    gwt_own = lax.dynamic_slice(window, (4 * me_s, 0), (WSH, D))
    d_in, nm_in, nv_in = _adamw(wt, gwt_own, mt, vt, "adamw_in")
    gw_in, d_in, nm_in, nv_in = [jnp.swapaxes(t, 0, 1)[None] for t in (gwt_own, d_in, nm_in, nv_in)]
    d_bf, nm_bf, nv_bf = _adamw(b_f, g_bf, m_b_f, v_b_f, "adamw_bf")
    gw_out, d_out, nm_out, nv_out = _adamw_halves(w_out[0], h_out, s_out, m_w_out[0], v_w_out[0], "adamw_out")
    d_fin, nm_fin, nv_fin = _adamw(final_g.reshape(1, D), g_final.reshape(1, D), m_final_g.reshape(1, D),
                                   v_final_g.reshape(1, D), "adamw_final")
    return (loss_all, gx[None], gw_meta, g_norm, gw_in, g_bf, gw_out[None], g_final,
            d_meta, d_norm, d_in, d_bf, d_out[None], d_fin.reshape(D),
            nm_meta, nm_norm, nm_in, nm_bf, nm_out[None], nm_fin.reshape(D),
            nv_meta, nv_norm, nv_in, nv_bf, nv_out[None], nv_fin.reshape(D))
```

```python
import numpy as np
import jax
import jax.numpy as jnp
from jax import lax
from jax.experimental import pallas as pl
from jax.experimental.pallas import tpu as pltpu

D = 1024
SEQ = 2048
NMETA = 16
C = 128
PAD = C - NMETA
T = PAD + NMETA + SEQ
NCH = T // C
RH, RDK, RDV = 4, 128, 256
FH, FD = 16, 64
NPAIR = FH // 2
WMAIN = 7168
NFF = 16
WIN = WMAIN + NFF
WSH = WIN // 4
WPADROWS = 1824
DMIX = 2048
EPS = 1e-6
NEG = -1e30
RSCALE = RDK ** -0.5
FSCALE = FD ** -0.5
ROPE_BASE = 10000.0
LR, B1, B2, AEPS, WD, STEP = 0.001, 0.9, 0.999, 1e-08, 0.01, 10

BF = jnp.bfloat16
F32 = jnp.float32
NT = (((1,), (1,)), ((), ()))
TN = (((0,), (0,)), ((), ()))
NN_DIMS = (((1,), (0,)), ((), ()))
MESH = pl.DeviceIdType.MESH
ANY = pl.BlockSpec(memory_space=pl.ANY)
VMEM_LIMIT = 48 * 1024 * 1024

QB_R, KB_R = 0, 4
VB_R = 4
GB_R, GB_F = 2, 6
QB_F, KB_F, VB_F = 24, 32, 40


def _dot(a, b):
    return jnp.dot(a, b, preferred_element_type=F32)


def _dg(a, b, dims):
    return lax.dot_general(a, b, dims, preferred_element_type=F32)


def _params(sem=None):
    return pltpu.CompilerParams(dimension_semantics=sem, vmem_limit_bytes=VMEM_LIMIT)


def _constants():
    pos = jnp.arange(T, dtype=F32) - PAD
    inv = ROPE_BASE ** (-jnp.arange(0, RDK, 2, dtype=F32) / RDK)
    ang = pos[:, None] * inv[None, :]
    cos, sin = jnp.cos(ang), jnp.sin(ang)
    cos2 = jnp.concatenate([cos, cos], axis=1)
    sin2 = jnp.concatenate([-sin, sin], axis=1)
    log_gamma = jnp.log1p(-jnp.exp2(-5.0 - jnp.arange(RH, dtype=F32)))
    idx = jnp.arange(C, dtype=F32)
    diff = idx[:, None] - idx[None, :]
    dmask = jnp.where(diff[None] >= 0, jnp.exp(log_gamma[:, None, None] * jnp.maximum(diff, 0.0)[None]), 0.0)
    zeta = jnp.exp(log_gamma[:, None] * (C - 1.0 - idx)[None, :])
    xi = jnp.exp(log_gamma[:, None] * (idx + 1.0)[None, :])
    gdec = jnp.exp(log_gamma * C)
    zeta_b = jnp.broadcast_to(zeta[:, :, None], (RH, C, RDK))
    xi_b = jnp.broadcast_to(xi[:, :, None], (RH, C, RDK))
    gdec_b = jnp.broadcast_to(gdec[:, None, None], (RH, RDK, RDV))
    tri = jnp.asarray(np.tril(np.ones((C, C), np.float32)), dtype=BF)
    head_of_lane = np.arange(FH * FD) // FD
    pick = ((np.arange(FH * FD)[:, None] % FD == 0)
            & (head_of_lane[:, None] == np.arange(C)[None, :])).astype(np.float32)
    seg = (np.arange(C)[:, None] // FD == np.arange(C)[None, :] // FD).astype(np.float32)
    ones_aug = np.concatenate([np.tile((np.arange(C) < FD)[None, :], (C, 1)),
                               np.tile((np.arange(C) >= FD)[None, :], (C, 1))], axis=0).astype(np.float32)
    lane = np.arange(2 * C) % C
    causal = np.where(lane[None, :] <= np.arange(C)[:, None], 0.0, NEG).astype(np.float32)
    mask_bias = np.stack([np.zeros((C, 2 * C), np.float32), causal])
    return dict(cos2=cos2, sin2=sin2, dmask=dmask, zeta=zeta_b, xi=xi_b, gdec=gdec_b, tri=tri,
                mask_bias=jnp.asarray(mask_bias), pick=jnp.asarray(pick, dtype=BF), seg=jnp.asarray(seg, dtype=BF),
                ones_aug=jnp.asarray(ones_aug, dtype=BF))


def _norm_in(hpad, g):
    def body(h_ref, g_ref, u_ref, ut_ref):
        h = h_ref[...]
        rs = lax.rsqrt(jnp.mean(h * h, axis=1, keepdims=True) + EPS)
        u = h * rs * g_ref[...]
        u_ref[...] = u.astype(BF)
        ut_ref[...] = u.T.astype(BF)

    return pl.pallas_call(
        body, name="norm_in", grid=(NCH,),
        in_specs=[pl.BlockSpec((C, D), lambda i: (i, 0)), pl.BlockSpec((1, D), lambda i: (0, 0))],
        out_specs=[pl.BlockSpec((C, D), lambda i: (i, 0)), pl.BlockSpec((D, C), lambda i: (0, i))],
        out_shape=[jax.ShapeDtypeStruct((T, D), BF), jax.ShapeDtypeStruct((D, T), BF)],
        compiler_params=_params(("parallel",)),
    )(hpad, g)


def _mm_nt(a, b, n, tm, tn, name):
    m, k = a.shape

    def body(a_ref, b_ref, o_ref):
        o_ref[...] = _dg(a_ref[...], b_ref[...], NT)

    return pl.pallas_call(
        body, name=name, grid=(m // tm, n // tn),
        in_specs=[pl.BlockSpec((tm, k), lambda i, j: (i, 0)), pl.BlockSpec((tn, k), lambda i, j: (j, 0))],
        out_specs=pl.BlockSpec((tm, tn), lambda i, j: (i, j)),
        out_shape=jax.ShapeDtypeStruct((m, n), F32),
        compiler_params=_params(("parallel", "parallel")),
    )(a, b)


def _mm_nn(a, b, tm, tn, name):
    m, k = a.shape
    _, n = b.shape

    def body(a_ref, b_ref, o_ref):
        o_ref[...] = _dot(a_ref[...], b_ref[...])

    return pl.pallas_call(
        body, name=name, grid=(m // tm, n // tn),
        in_specs=[pl.BlockSpec((tm, k), lambda i, j: (i, 0)), pl.BlockSpec((k, tn), lambda i, j: (0, j))],
        out_specs=pl.BlockSpec((tm, tn), lambda i, j: (i, j)),
        out_shape=jax.ShapeDtypeStruct((m, n), F32),
        compiler_params=_params(("parallel", "parallel")),
    )(a, b)


def _rot(x, cos2, sin2):
    return x * cos2 + pltpu.roll(x, 64, 1) * sin2


def _ret_specs(chunk):
    whole = lambda shape: pl.BlockSpec(shape, lambda n: (0,) * len(shape))
    return [
        pl.BlockSpec((C, RH * RDK), lambda n: (chunk(n), 0)),
        pl.BlockSpec((C, RH * RDK), lambda n: (chunk(n), 1)),
        pl.BlockSpec((C, RH * RDV), lambda n: (chunk(n), 1)),
        pl.BlockSpec((C, RDK), lambda n: (chunk(n), 0)),
        pl.BlockSpec((C, RDK), lambda n: (chunk(n), 0)),
        whole((RH, C, C)), whole((RH, C, RDK)), whole((RH, C, RDK)), whole((RH, RDK, RDV)),
    ]


def _ret_heads(q_ref, k_ref, v_ref, cos, sin):
    qr = [_rot(q_ref[:, RDK * h:RDK * (h + 1)], cos, sin) for h in range(RH)]
    kr = [_rot(k_ref[:, RDK * h:RDK * (h + 1)], cos, sin) * RSCALE for h in range(RH)]
    vb = [v_ref[:, RDV * h:RDV * (h + 1)].astype(BF) for h in range(RH)]
    return qr, kr, [t.astype(BF) for t in qr], [t.astype(BF) for t in kr], vb


def _ret_fwd(z, cst):
    def body(q_ref, k_ref, v_ref, cos_ref, sin_ref, dm_ref, xi_ref, zt_ref, gd_ref, r_ref, sp_ref, st):
        n = pl.program_id(0)

        @pl.when(n == 0)
        def _():
            st[...] = jnp.zeros_like(st)

        hs = range(RH)
        qr, kr, qb, kb, vb = _ret_heads(q_ref, k_ref, v_ref, cos_ref[...], sin_ref[...])
        sd = [(_dg(qb[h], kb[h], NT) * dm_ref[h]).astype(BF) for h in hs]
        state = [st[h] for h in hs]
        qx = [(qr[h] * xi_ref[h]).astype(BF) for h in hs]
        kz = [(kr[h] * zt_ref[h]).astype(BF) for h in hs]
        out = [_dot(sd[h], vb[h]) + _dot(qx[h], state[h].astype(BF)) for h in hs]
        kv = [_dg(kz[h], vb[h], TN) for h in hs]
        for h in hs:
            sp_ref[0, h] = state[h]
            r_ref[:, RDV * h:RDV * (h + 1)] = out[h]
            st[h] = state[h] * gd_ref[h] + kv[h]

    return pl.pallas_call(
        body, name="ret_fwd", grid=(NCH,),
        in_specs=_ret_specs(lambda n: n),
        out_specs=[pl.BlockSpec((C, RH * RDV), lambda n: (n, 0)),
                   pl.BlockSpec((1, RH, RDK, RDV), lambda n: (n, 0, 0, 0))],
        out_shape=[jax.ShapeDtypeStruct((T, RH * RDV), F32), jax.ShapeDtypeStruct((NCH, RH, RDK, RDV), F32)],
        scratch_shapes=[pltpu.VMEM((RH, RDK, RDV), F32)],
        compiler_params=_params(("arbitrary",)),
    )(z, z, z, cst["cos2"], cst["sin2"], cst["dmask"], cst["xi"], cst["zeta"], cst["gdec"])


def _ret_bwd(z, cst, sprev, dr):
    def body(q_ref, k_ref, v_ref, cos_ref, sin_ref, dm_ref, xi_ref, zt_ref, gd_ref, sp_ref, dr_ref,
             dq_ref, dk_ref, dv_ref, gst):
        i = pl.program_id(0)

        @pl.when(i == 0)
        def _():
            gst[...] = jnp.zeros_like(gst)

        hs = range(RH)
        cos, sin = cos_ref[...], sin_ref[...]
        qr, kr, qb, kb, vb = _ret_heads(q_ref, k_ref, v_ref, cos, sin)
        dm = [dm_ref[h] for h in hs]
        xi = [xi_ref[h] for h in hs]
        zt = [zt_ref[h] for h in hs]
        sd = [(_dg(qb[h], kb[h], NT) * dm[h]).astype(BF) for h in hs]
        qx = [(qr[h] * xi[h]).astype(BF) for h in hs]
        kz = [(kr[h] * zt[h]).astype(BF) for h in hs]
        drb = [dr_ref[:, RDV * h:RDV * (h + 1)] for h in hs]
        sb = [sp_ref[0, h].astype(BF) for h in hs]
        g = [gst[h] for h in hs]
        gb = [t.astype(BF) for t in g]
        ds = [(_dg(drb[h], vb[h], NT) * dm[h]).astype(BF) for h in hs]
        dq = [_dot(ds[h], kb[h]) + _dg(drb[h], sb[h], NT) * xi[h] for h in hs]
        dk = [(_dg(ds[h], qb[h], TN) + _dg(vb[h], gb[h], NT) * zt[h]) * RSCALE for h in hs]
        dv = [_dg(sd[h], drb[h], TN) + _dot(kz[h], gb[h]) for h in hs]
        gn = [g[h] * gd_ref[h] + _dg(qx[h], drb[h], TN) for h in hs]
        for h in hs:
            gst[h] = gn[h]
            dq_ref[:, RDK * h:RDK * (h + 1)] = (dq[h] * cos + pltpu.roll(dq[h] * sin, 64, 1)).astype(BF)
            dk_ref[:, RDK * h:RDK * (h + 1)] = (dk[h] * cos + pltpu.roll(dk[h] * sin, 64, 1)).astype(BF)
            dv_ref[:, RDV * h:RDV * (h + 1)] = dv[h].astype(BF)

    rev = lambda n: NCH - 1 - n
    return pl.pallas_call(
        body, name="ret_bwd", grid=(NCH,),
        in_specs=_ret_specs(rev) + [
            pl.BlockSpec((1, RH, RDK, RDV), lambda n: (rev(n), 0, 0, 0)),
            pl.BlockSpec((C, RH * RDV), lambda n: (rev(n), 0)),
        ],
        out_specs=[pl.BlockSpec((C, RH * RDK), lambda n: (rev(n), 0)),
                   pl.BlockSpec((C, RH * RDK), lambda n: (rev(n), 0)),
                   pl.BlockSpec((C, RH * RDV), lambda n: (rev(n), 0))],
        out_shape=[jax.ShapeDtypeStruct((T, RH * RDK), BF), jax.ShapeDtypeStruct((T, RH * RDK), BF),
                   jax.ShapeDtypeStruct((T, RH * RDV), BF)],
        scratch_shapes=[pltpu.VMEM((RH, RDK, RDV), F32)],
        compiler_params=_params(("arbitrary",)),
    )(z, z, z, cst["cos2"], cst["sin2"], cst["dmask"], cst["xi"], cst["zeta"], cst["gdec"], sprev, dr)


def _place():
    x, y, c = lax.axis_index("x"), lax.axis_index("y"), lax.axis_index("c")
    return x, y, c


def _other_chips(x, y):
    return [(1 - x, y, 2 * (1 - x) + y), (x, 1 - y, 2 * x + (1 - y)), (1 - x, 1 - y, 2 * (1 - x) + (1 - y))]


def _chip_copies(srcs, lands, send_sems, recv_sems, by_dest):
    x, y, c = _place()
    me_s = 2 * x + y
    return [pltpu.make_async_remote_copy(
        src_ref=src.at[cs] if by_dest else src, dst_ref=land.at[me_s],
        send_sem=send_sems.at[3 * a + j], recv_sem=recv_sems.at[3 * a + j],
        device_id=(cx, cy, c), device_id_type=MESH)
        for a, (src, land) in enumerate(zip(srcs, lands)) for j, (cx, cy, cs) in enumerate(_other_chips(x, y))]


def _split_dot(x, mat01, dims=NN_DIMS, x_first=True):
    acc, rest = None, x
    for _ in range(3):
        piece = rest.astype(BF)
        part = _dg(piece, mat01, dims) if x_first else _dg(mat01, piece, dims)
        acc = part if acc is None else acc + part
        rest = rest - piece.astype(F32)
    return acc


def _log_sigmoid(x):
    return -(jnp.maximum(-x, 0.0) + jnp.log1p(jnp.exp(-jnp.abs(x))))


def _fox_prep(zf, bf_pad, cst):
    def body(zf_ref, b_ref, tri_ref, ct_ref, carry):
        n = pl.program_id(0)

        @pl.when(n == 0)
        def _():
            carry[...] = jnp.zeros_like(carry)

        ls = _log_sigmoid(zf_ref[...] + b_ref[...])
        row = n * C + lax.broadcasted_iota(jnp.int32, (C, C), 0)
        lf = jnp.where(row >= PAD, ls, 0.0)
        cc = _split_dot(lf, tri_ref[...], x_first=False) + carry[0:1, :]
        carry[...] = jnp.broadcast_to(cc[C - 1:C, :], carry.shape)
        pos = n * C + lax.broadcasted_iota(jnp.int32, (FH, C), 1)
        ct_ref[0] = jnp.where(pos >= PAD, cc.T[:FH, :], -NEG)

    return pl.pallas_call(
        body, name="fox_prep", grid=(NCH,),
        in_specs=[pl.BlockSpec((C, C), lambda n: (n, 0)), pl.BlockSpec((1, C), lambda n: (0, 0)),
                  pl.BlockSpec((C, C), lambda n: (0, 0))],
        out_specs=pl.BlockSpec((1, FH, C), lambda n: (n, 0, 0)),
        out_shape=jax.ShapeDtypeStruct((NCH, FH, C), F32),
        scratch_shapes=[pltpu.VMEM((8, C), F32)],
        compiler_params=_params(("arbitrary",)),
    )(zf, bf_pad, cst["tri"])


def _lo_lanes(shape):
    return lax.broadcasted_iota(jnp.int32, shape, 1) < FD


def _split_heads(x):
    lo = _lo_lanes(x.shape)
    zero = jnp.zeros_like(x)
    return jnp.concatenate([jnp.where(lo, x, zero), jnp.where(lo, zero, x)], axis=0)


def _spread2(x):
    lo = _lo_lanes(x.shape)
    r = pltpu.roll(x, FD, 1)
    return jnp.concatenate([jnp.where(lo, x, r), jnp.where(lo, r, x)], axis=1)


NSTEP = (NCH + 1) // 2
NTILE = NCH + 1
TROWS = T + C


def _fox_tile(s, t):
    second = t > s
    return second.astype(jnp.int32), jnp.where(second, t - s - 1, s - t)


def _fox_pos(i):
    return jnp.where(i < NSTEP, 2 * i, 2 * (NCH - 1 - i) + 1)


FOX_ORDER = [2 * i if i < NSTEP else 2 * (NCH - 1 - i) + 1 for i in range(NCH)]


def _fox_pair_specs():
    first = pl.BlockSpec((C, C), lambda p, s: (2 * s, p))
    second = pl.BlockSpec((C, C), lambda p, s: (jnp.where(s == NSTEP - 1, 2 * s, 2 * s + 1), p))
    both = pl.BlockSpec((2 * C, C), lambda p, s: (s, p))
    return first, second, both


def _fox_q_specs():
    return (pl.BlockSpec((C, C), lambda p, s: (s, QB_F + p)),
            pl.BlockSpec((C, C), lambda p, s: (NCH - 1 - s, QB_F + p)))


def _fox_key_bias(ct_ref, p, j):
    return jnp.concatenate([ct_ref[j, pl.ds(2 * p, 1), :], ct_ref[j, pl.ds(2 * p + 1, 1), :]], axis=1)


def _fox_fwd(z, ct, cst, share):
    n = 0 if share is None else 1

    def body(qa_ref, qb_ref, k_ref, v_ref, ct_ref, ones_ref, mb_ref, *rest):
        share_refs, (a_ref, g_ref), land_refs = rest[:n], rest[n:n + 2], rest[n + 2:2 * n + 2]
        kks, vvs, q2, m2, sbuf = rest[2 * n + 2:2 * n + 7]
        p, s = pl.program_id(0), pl.program_id(1)
        if n:
            copies = _chip_copies(share_refs, land_refs, *rest[2 * n + 7:], by_dest=False)

            @pl.when((p == 0) & (s == 0))
            def _():
                for cp in copies:
                    cp.start()

            @pl.when((p == NPAIR - 1) & (s == NSTEP - 1))
            def _():
                for cp in copies:
                    cp.wait()

        @pl.when(s == 0)
        def _():
            ones = ones_ref[...]

            def prep(j, carry):
                rows = pl.ds(pl.multiple_of(j * C, C), C)
                kks[j] = _split_heads(k_ref[rows, :]).astype(BF)
                vvs[j] = jnp.concatenate([_split_heads(v_ref[rows, :]).astype(BF), ones], axis=1)
                return carry

            lax.fori_loop(0, NCH, prep, 0)

        q2[0] = (qa_ref[...] * FSCALE).astype(BF)
        q2[1] = (qb_ref[...] * FSCALE).astype(BF)

        tiles = [_fox_tile(s, t) for t in range(NTILE)]
        causal = mb_ref[1]
        neg = jnp.full((C, 2 * C), NEG, F32)
        run, first = neg, neg
        for t, (sel, j) in enumerate(tiles):
            st = _dg(q2[sel], kks[j], NT) - _fox_key_bias(ct_ref, p, j)
            if t in (0, NTILE - 1):
                st = st + causal
            sbuf[t] = st
            run = jnp.maximum(jnp.where(t == s + 1, neg, run), st)
            first = jnp.where(t == s, run, first)
        for w, mx in enumerate((first, run)):
            m2[w] = jnp.concatenate(
                [jnp.broadcast_to(jnp.max(mx[:, :C], axis=1, keepdims=True), (C, C)),
                 jnp.broadcast_to(jnp.max(mx[:, C:], axis=1, keepdims=True), (C, C))], axis=1)

        zero = jnp.zeros((C, 2 * C), F32)
        run, first = zero, zero
        for t, (sel, j) in enumerate(tiles):
            run = jnp.where(t == s + 1, zero, run) + _dot(jnp.exp(sbuf[t] - m2[sel]).astype(BF), vvs[j])
            first = jnp.where(t == s, run, first)
        lo = _lo_lanes((C, C))
        for w, res in enumerate((first, run)):
            l = res[:, C:]
            a_ref[C * w:C * (w + 1), :] = res[:, :C] / l
            mw = m2[w]
            g_ref[C * w:C * (w + 1), :] = -(jnp.where(lo, mw[:, :C], mw[:, C:]) + jnp.log(l))

    qa, qb = _fox_q_specs()
    both = _fox_pair_specs()[2]
    return pl.pallas_call(
        body, name="fox_fwd", grid=(NPAIR, NSTEP),
        in_specs=[qa, qb,
                  pl.BlockSpec((T, C), lambda p, s: (0, KB_F + p)),
                  pl.BlockSpec((T, C), lambda p, s: (0, VB_F + p)),
                  pl.BlockSpec((NCH, FH, C), lambda p, s: (0, 0, 0)),
                  pl.BlockSpec((2 * C, C), lambda p, s: (0, 0)),
                  pl.BlockSpec((2, C, 2 * C), lambda p, s: (0, 0, 0))] + [ANY] * n,
        out_specs=[both, both] + [ANY] * n,
        out_shape=[jax.ShapeDtypeStruct((TROWS, FH * FD), F32)] * 2
        + ([jax.ShapeDtypeStruct((4,) + share.shape, share.dtype)] if n else []),
        scratch_shapes=[pltpu.VMEM((NCH, 2 * C, C), BF), pltpu.VMEM((NCH, 2 * C, 2 * C), BF),
                        pltpu.VMEM((2, C, C), BF), pltpu.VMEM((2, C, 2 * C), F32),
                        pltpu.VMEM((NTILE, C, 2 * C), F32)]
        + [pltpu.SemaphoreType.DMA((3,)), pltpu.SemaphoreType.DMA((3,))] * n,
        compiler_params=_params(("arbitrary", "arbitrary")),
    )(z, z, z, z, ct, cst["ones_aug"], cst["mask_bias"], *([share] * n))


def _fox_bwd(z, da, g, delta, ct, cst):
    grp = 9

    def body(qa_ref, qb_ref, daa_ref, dab_ref, ga_ref, gb_ref, dla_ref, dlb_ref, k_ref, v_ref, ct_ref, ones_ref,
             mb_ref, dq_ref, dr_ref, dk_ref, dv_ref, dcs_ref,
             kks, vvs, q2, qq2, dd2, da2, gi2, dl2, dq2, dvb, dkb, dkacc, dvacc, csacc):
        p, s = pl.program_id(0), pl.program_id(1)
        ones = ones_ref[...]

        @pl.when(s == 0)
        def _():
            dkacc[...] = jnp.zeros_like(dkacc)
            dvacc[...] = jnp.zeros_like(dvacc)
            csacc[...] = jnp.zeros_like(csacc)

            def prep(j, carry):
                rows = pl.ds(pl.multiple_of(j * C, C), C)
                kks[j] = _split_heads(k_ref[rows, :]).astype(BF)
                vvs[j] = _split_heads(v_ref[rows, :]).astype(BF)
                return carry

            lax.fori_loop(0, NCH, prep, 0)

        for w, (q_ref, d_ref, g_ref, l_ref) in enumerate(((qa_ref, daa_ref, ga_ref, dla_ref),
                                                          (qb_ref, dab_ref, gb_ref, dlb_ref))):
            qf = q_ref[...]
            q2[w] = (qf * FSCALE).astype(BF)
            qq2[w] = jnp.concatenate([_split_heads(qf).astype(BF), ones], axis=1)
            da2[w] = d_ref[...]
            dd2[w] = _split_heads(d_ref[...].astype(F32)).astype(BF)
            gi2[w] = _spread2(g_ref[...])
            dl2[w] = _spread2(l_ref[...])
        dq2[...] = jnp.zeros_like(dq2)
        zero = jnp.zeros((C, 2 * C), F32)

        def group(gi, carry):
            ts = [gi * grp + u for u in range(grp)]
            tiles = [_fox_tile(s, t) for t in ts]
            kk = [kks[j] for _, j in tiles]
            ss = [_dg(q2[sel], kj, NT) + (gi2[sel] - _fox_key_bias(ct_ref, p, j)) for kj, (sel, j) in zip(kk, tiles)]
            ss[0] = ss[0] + mb_ref[(gi == 0).astype(jnp.int32)]
            ss[-1] = ss[-1] + mb_ref[(gi == 1).astype(jnp.int32)]
            dps = [_dg(da2[sel], vvs[j], NT) for sel, j in tiles]
            pes = [jnp.exp(st) for st in ss]
            dss = [pe * (dp - dl2[sel]) * FSCALE for pe, dp, (sel, _) in zip(pes, dps, tiles)]
            pts = [jnp.concatenate([pe[:, :C].T, pe[:, C:].T], axis=1).astype(BF) for pe in pes]
            dsts = [jnp.concatenate([ds[:, :C].T, ds[:, C:].T], axis=1).astype(BF) for ds in dss]
            dvs = [_dot(pt, dd2[sel]) for pt, (sel, _) in zip(pts, tiles)]
            rs = [_dot(dst, qq2[sel]) for dst, (sel, _) in zip(dsts, tiles)]
            parts = [_dot(ds.astype(BF), jnp.concatenate([kj, ones], axis=1)) for ds, kj in zip(dss, kk)]
            for t, dv, rr in zip(ts, dvs, rs):
                dvb[t] = dv
                dkb[t] = rr
            pa, pb = zero, zero
            for t, part in zip(ts, parts):
                pa = pa + jnp.where(t <= s, part, zero)
                pb = pb + jnp.where(t <= s, zero, part)
            dq2[0] += pa
            dq2[1] += pb
            return carry

        ntile = jnp.where(s == NSTEP - 1, grp, NTILE)
        lax.fori_loop(0, ntile // grp, group, 0)

        def scatter(t, carry):
            _, j = _fox_tile(s, t)
            r = pl.ds(pl.multiple_of(j * C, C), C)
            dvacc[r, :] += dvb[t]
            dkacc[r, :] += dkb[t, :, :C]
            csacc[r, :] += dkb[t, :, C:]
            return carry

        lax.fori_loop(0, ntile, scatter, 0)
        for w in range(2):
            res = dq2[w]
            dq_ref[C * w:C * (w + 1), :] = res[:, :C].astype(BF)
            dr_ref[C * w:C * (w + 1), :] = res[:, C:]

        @pl.when(s == NSTEP - 1)
        def _():
            dk_ref[...] = dkacc[...].astype(BF)
            dv_ref[...] = dvacc[...].astype(BF)
            dcs_ref[...] = csacc[...]

    qa, qb = _fox_q_specs()
    ba, bb, both = _fox_pair_specs()
    col = pl.BlockSpec((T, C), lambda p, s: (0, p))
    return pl.pallas_call(
        body, name="fox_bwd", grid=(NPAIR, NSTEP),
        in_specs=[qa, qb, ba, bb, ba, bb, ba, bb,
                  pl.BlockSpec((T, C), lambda p, s: (0, KB_F + p)),
                  pl.BlockSpec((T, C), lambda p, s: (0, VB_F + p)),
                  pl.BlockSpec((NCH, FH, C), lambda p, s: (0, 0, 0)),
                  pl.BlockSpec((2 * C, C), lambda p, s: (0, 0)),
                  pl.BlockSpec((2, C, 2 * C), lambda p, s: (0, 0, 0))],
        out_specs=[both, both, col, col, col],
        out_shape=[jax.ShapeDtypeStruct((TROWS, FH * FD), BF), jax.ShapeDtypeStruct((TROWS, FH * FD), F32),
                   jax.ShapeDtypeStruct((T, FH * FD), BF), jax.ShapeDtypeStruct((T, FH * FD), BF),
                   jax.ShapeDtypeStruct((T, FH * FD), F32)],
        scratch_shapes=[pltpu.VMEM((NCH, 2 * C, C), BF), pltpu.VMEM((NCH, 2 * C, C), BF),
                        pltpu.VMEM((2, C, C), BF), pltpu.VMEM((2, 2 * C, 2 * C), BF), pltpu.VMEM((2, 2 * C, C), BF),
                        pltpu.VMEM((2, C, C), BF), pltpu.VMEM((2, C, 2 * C), F32), pltpu.VMEM((2, C, 2 * C), F32),
                        pltpu.VMEM((2, C, 2 * C), F32),
                        pltpu.VMEM((NTILE, C, C), F32), pltpu.VMEM((NTILE, C, 2 * C), F32),
                        pltpu.VMEM((T, C), F32), pltpu.VMEM((T, C), F32), pltpu.VMEM((T, C), F32)],
        compiler_params=_params(("parallel", "arbitrary")),
    )(z, z, da, da, g, g, delta, delta, z, z, ct, cst["ones_aug"], cst["mask_bias"])


def _fox_gate_bwd(drow, dcol, zf, bf_pad, cst):
    def body(dr_ref, dc_ref, zf_ref, b_ref, tri_ref, pick_ref, dff_ref, db_ref, carry):
        s = pl.program_id(0)
        n = NCH - 1 - s

        @pl.when(s == 0)
        def _():
            carry[...] = jnp.zeros_like(carry)
            db_ref[...] = jnp.zeros_like(db_ref)

        dcb = _split_dot((dr_ref[...] - dc_ref[...]) * (1.0 / FSCALE), pick_ref[...])
        suf = _split_dot(dcb, tri_ref[...], TN, x_first=False) + carry[0:1, :]
        carry[...] = jnp.broadcast_to(suf[0:1, :], carry.shape)
        x = zf_ref[...] + b_ref[...]
        row = n * C + lax.broadcasted_iota(jnp.int32, (C, C), 0)
        dff = jnp.where(row >= PAD, suf * (1.0 - jax.nn.sigmoid(x)), 0.0)
        dff_ref[...] = dff.astype(BF)
        db_ref[...] += jnp.sum(dff, axis=0, keepdims=True)

    rev = lambda s: (NCH - 1 - s, 0)
    return pl.pallas_call(
        body, name="fox_gate_bwd", grid=(NCH,),
        in_specs=[pl.BlockSpec((C, FH * FD), lambda s: (_fox_pos(NCH - 1 - s), 0)),
                  pl.BlockSpec((C, FH * FD), rev), pl.BlockSpec((C, C), rev),
                  pl.BlockSpec((1, C), lambda s: (0, 0)), pl.BlockSpec((C, C), lambda s: (0, 0)),
                  pl.BlockSpec((FH * FD, C), lambda s: (0, 0))],
        out_specs=[pl.BlockSpec((C, C), rev), pl.BlockSpec((1, C), lambda s: (0, 0))],
        out_shape=[jax.ShapeDtypeStruct((T, C), BF), jax.ShapeDtypeStruct((1, C), F32)],
        scratch_shapes=[pltpu.VMEM((8, C), F32)],
        compiler_params=_params(("arbitrary",)),
    )(drow, dcol, zf, bf_pad, cst["tri"], cst["pick"])


def _gated(r, rg, a, fg):
    rn, rs = [], []
    for h in range(RH):
        rh = r[:, RDV * h:RDV * (h + 1)]
        s = lax.rsqrt(jnp.mean(rh * rh, axis=1, keepdims=True) + EPS)
        rn.append(rh * s)
        rs.append(s)
    rn = jnp.concatenate(rn, axis=1)
    y = jnp.concatenate([rn * (rg * jax.nn.sigmoid(rg)), a * (fg * jax.nn.sigmoid(fg))], axis=1)
    return y, rn, rs


def _out_loss(r, z, a, wout, x, tgt, fgain):
    def body(r_ref, rg_ref, a_ref, fg_ref, w_ref, x_ref, t_ref, g_ref, yt_ref, do_ref, dob_ref, loss_ref, dg_ref):
        i = pl.program_id(0)

        @pl.when(i == 0)
        def _():
            yt_ref[...] = jnp.zeros_like(yt_ref)
            do_ref[...] = jnp.zeros_like(do_ref)
            dob_ref[...] = jnp.zeros_like(dob_ref)
            loss_ref[...] = jnp.zeros_like(loss_ref)
            dg_ref[...] = jnp.zeros_like(dg_ref)

        @pl.when(i > 0)
        def _():
            y, _, _ = _gated(r_ref[...], rg_ref[...], a_ref[...], fg_ref[...])
            yt_ref[...] = y.T.astype(BF)
            o = x_ref[...] + _dot(y.astype(BF), w_ref[...])
            rs = lax.rsqrt(jnp.mean(o * o, axis=1, keepdims=True) + EPS)
            on = o * rs
            g = g_ref[...]
            e = on * g - t_ref[...]
            loss_ref[...] += 0.5 * jnp.sum(jnp.mean(e * e, axis=1, keepdims=True))
            dyh = e * (1.0 / D)
            dg_ref[...] += jnp.sum(dyh * on, axis=0, keepdims=True)
            don = dyh * g
            do = rs * (don - on * jnp.mean(don * on, axis=1, keepdims=True))
            do_ref[...] = do
            dob_ref[...] = do.astype(BF)

    tok = lambda i: (jnp.maximum(i - 1, 0), 0)
    return pl.pallas_call(
        body, name="out_loss", grid=(NCH,),
        in_specs=[pl.BlockSpec((C, D), lambda i: (i, 0)), pl.BlockSpec((C, D), lambda i: (i, GB_R)),
                  pl.BlockSpec((C, D), lambda i: (_fox_pos(i), 0)), pl.BlockSpec((C, D), lambda i: (i, GB_F)),
                  pl.BlockSpec((DMIX, D), lambda i: (0, 0)),
                  pl.BlockSpec((C, D), tok), pl.BlockSpec((C, D), tok), pl.BlockSpec((1, D), lambda i: (0, 0))],
        out_specs=[pl.BlockSpec((DMIX, C), lambda i: (0, i)), pl.BlockSpec((C, D), lambda i: (i, 0)),
                   pl.BlockSpec((C, D), lambda i: (i, 0)), pl.BlockSpec((8, C), lambda i: (0, 0)),
                   pl.BlockSpec((1, D), lambda i: (0, 0))],
        out_shape=[jax.ShapeDtypeStruct((DMIX, T), BF), jax.ShapeDtypeStruct((T, D), F32),
                   jax.ShapeDtypeStruct((T, D), BF), jax.ShapeDtypeStruct((8, C), F32),
                   jax.ShapeDtypeStruct((1, D), F32)],
        compiler_params=_params(("arbitrary",)),
    )(r, z, a, z, wout, x, tgt, fgain)


def _dsilu(x):
    s = jax.nn.sigmoid(x)
    return s * (1.0 + x * (1.0 - s))


def _dy_gate_bwd(dob, wout, r, z, a, seg):
    def body(do_ref, w_ref, r_ref, rg_ref, a_ref, fg_ref, seg_ref, dr_ref, da_ref, drg_ref, dfg_ref, dl_ref):
        dy = _dg(do_ref[...], w_ref[...], NT)
        rg, fg, a_ = rg_ref[...], fg_ref[...], a_ref[...]
        _, rn, rs = _gated(r_ref[...], rg, a_, fg)
        dyr, dyf = dy[:, :D], dy[:, D:]
        drn = dyr * (rg * jax.nn.sigmoid(rg))
        drg_ref[...] = (dyr * rn * _dsilu(rg)).astype(BF)
        for h in range(RH):
            sl = slice(RDV * h, RDV * (h + 1))
            dh, nh = drn[:, sl], rn[:, sl]
            dr_ref[:, sl] = (rs[h] * (dh - nh * jnp.mean(dh * nh, axis=1, keepdims=True))).astype(BF)
        dab = (dyf * (fg * jax.nn.sigmoid(fg))).astype(BF)
        da_ref[...] = dab
        dfg_ref[...] = (dyf * a_ * _dsilu(fg)).astype(BF)
        prod = dab.astype(F32) * a_
        segm = seg_ref[...]
        for p in range(NPAIR):
            sl = slice(C * p, C * (p + 1))
            hi = prod[:, sl].astype(BF)
            lo = (prod[:, sl] - hi.astype(F32)).astype(BF)
            dl_ref[:, sl] = _dot(hi, segm) + _dot(lo, segm)

    row = pl.BlockSpec((C, D), lambda i: (i, 0))
    fox = pl.BlockSpec((C, D), lambda i: (_fox_pos(i), 0))
    return pl.pallas_call(
        body, name="dy_gate_bwd", grid=(NCH,),
        in_specs=[row, pl.BlockSpec((DMIX, D), lambda i: (0, 0)),
                  row, pl.BlockSpec((C, D), lambda i: (i, GB_R)),
                  fox, pl.BlockSpec((C, D), lambda i: (i, GB_F)),
                  pl.BlockSpec((C, C), lambda i: (0, 0))],
        out_specs=[row, fox, row, row, fox],
        out_shape=[jax.ShapeDtypeStruct((T, D), BF), jax.ShapeDtypeStruct((TROWS, D), BF),
                   jax.ShapeDtypeStruct((T, D), BF), jax.ShapeDtypeStruct((T, D), BF),
                   jax.ShapeDtypeStruct((TROWS, D), F32)],
        compiler_params=_params(("parallel",)),
    )(dob, wout, r, z, a, z, seg)


DZ_WIDTHS = (512, 512, 1024, 1024, 1024, 1024, 1024, 1024)


def _du_norm_bwd(dzs, dzf, wt, wft, hpad, g, dopad, parts=()):
    tm, tk = 544, 1024
    nk = WMAIN // tk
    ni = T // tm
    n = len(parts)

    def body(rq_ref, rk_ref, rv_ref, rg_ref, fq_ref, fk_ref, fv_ref, fg_ref, dzf_ref, w_ref, wf_ref, h_ref, g_ref,
             do_ref, *rest):
        part_refs, (gh_ref, dg_ref), land_refs = rest[:n], rest[n:n + 2], rest[n + 2:2 * n + 2]
        acc = rest[2 * n + 2]
        i, k = pl.program_id(0), pl.program_id(1)

        if n:
            send_sems, recv_sems = rest[2 * n + 3:]
            copies = _chip_copies(part_refs, land_refs, send_sems, recv_sems, by_dest=True)

            @pl.when((i == 0) & (k == 0))
            def _():
                for cp in copies:
                    cp.start()

            @pl.when((i == ni - 1) & (k == nk - 1))
            def _():
                for cp in copies:
                    cp.wait()

        @pl.when(k == 0)
        def _():
            acc[...] = (_dot(dzf_ref[...], wf_ref[...]) + _dot(rq_ref[...], w_ref[:512, :])
                        + _dot(rk_ref[...], w_ref[512:, :]))

        for kk, piece in enumerate((rv_ref, rg_ref, fq_ref, fk_ref, fv_ref, fg_ref), start=1):
            @pl.when(k == kk)
            def _(piece=piece):
                acc[...] += _dot(piece[...], w_ref[...])

        @pl.when(k == nk - 1)
        def _():
            du = acc[...]
            h = h_ref[...]
            gg = g_ref[...]
            rs = lax.rsqrt(jnp.mean(h * h, axis=1, keepdims=True) + EPS)
            hn = h * rs
            part = jnp.sum(du * hn, axis=0, keepdims=True)

            @pl.when(i == 0)
            def _():
                dg_ref[...] = part

            @pl.when(i > 0)
            def _():
                dg_ref[...] += part

            dhn = du * gg
            gh_ref[...] = rs * (dhn - hn * jnp.mean(dhn * hn, axis=1, keepdims=True)) + do_ref[...]

    sems = [pltpu.SemaphoreType.DMA((3 * n,)), pltpu.SemaphoreType.DMA((3 * n,))] if n else []
    return pl.pallas_call(
        body, name="du_norm_bwd", grid=(ni, nk),
        in_specs=[pl.BlockSpec((tm, w), lambda i, k: (i, 0)) for w in DZ_WIDTHS]
        + [pl.BlockSpec((tm, C), lambda i, k: (i, 0)),
           pl.BlockSpec((tk, D), lambda i, k: (k, 0)), pl.BlockSpec((C, D), lambda i, k: (0, 0)),
           pl.BlockSpec((tm, D), lambda i, k: (i, 0)), pl.BlockSpec((1, D), lambda i, k: (0, 0)),
           pl.BlockSpec((tm, D), lambda i, k: (i, 0))] + [ANY] * n,
        out_specs=[pl.BlockSpec((tm, D), lambda i, k: (i, 0)), pl.BlockSpec((1, D), lambda i, k: (0, 0))] + [ANY] * n,
        out_shape=[jax.ShapeDtypeStruct((T, D), F32), jax.ShapeDtypeStruct((1, D), F32)]
        + [jax.ShapeDtypeStruct(p.shape, p.dtype) for p in parts],
        scratch_shapes=[pltpu.VMEM((tm, D), F32)] + sems,
        compiler_params=_params(("arbitrary", "arbitrary")),
    )(*dzs, dzf, wt, wft, hpad, g, dopad, *parts)


GROWS = 7424


def _dw_in(dzs, dzf, ut):
    tn = 256
    nmain = WMAIN // tn
    first, blocks = [], []
    for w in DZ_WIDTHS:
        first.append(sum(blocks))
        blocks.append(w // tn)

    def body(rq_ref, rk_ref, rv_ref, rg_ref, fq_ref, fk_ref, fv_ref, fg_ref, dzf_ref, ut_ref, o_ref):
        gidx = pl.program_id(0)
        for piece, g0, nb in zip((rq_ref, rk_ref, rv_ref, rg_ref, fq_ref, fk_ref, fv_ref, fg_ref), first, blocks):
            @pl.when((gidx >= g0) & (gidx < g0 + nb))
            def _(piece=piece):
                o_ref[...] = _dot(ut_ref[...], piece[...]).T

        @pl.when(gidx == nmain)
        def _():
            o_ref[:C, :] = _dot(ut_ref[...], dzf_ref[...]).T
            o_ref[C:, :] = jnp.zeros((tn - C, D), F32)

    def piece_spec(g0, nb):
        return pl.BlockSpec((T, tn), lambda gidx: (0, jnp.clip(gidx - g0, 0, nb - 1)))

    return pl.pallas_call(
        body, name="dw_in", grid=(nmain + 1,),
        in_specs=[piece_spec(g0, nb) for g0, nb in zip(first, blocks)]
        + [pl.BlockSpec((T, C), lambda gidx: (0, 0)), pl.BlockSpec((D, T), lambda gidx: (0, 0))],
        out_specs=pl.BlockSpec((tn, D), lambda gidx: (gidx, 0)),
        out_shape=jax.ShapeDtypeStruct((GROWS, D), F32),
        compiler_params=_params(("arbitrary",)),
    )(*dzs, dzf, ut)


def _token_order(x_po):
    def body(i_ref, o_ref):
        o_ref[...] = i_ref[...]

    return pl.pallas_call(
        body, name="token_order", grid=(NCH,),
        in_specs=[pl.BlockSpec((C, D), lambda i: (_fox_pos(i), 0))],
        out_specs=pl.BlockSpec((C, D), lambda i: (i, 0)),
        out_shape=jax.ShapeDtypeStruct((T, D), x_po.dtype),
        compiler_params=_params(("parallel",)),
    )(x_po)


def _local_step(x, tgt, meta, norm_g, wt, wft, b_f, wout, final_g, chip_sums=None, wout_full=None):
    cst = _constants()
    hpad = jnp.concatenate([jnp.pad(meta, ((PAD, 0), (0, 0))), x], axis=0)
    bf_pad = jnp.pad(b_f, ((0, 0), (0, C - NFF)))
    u, ut = _norm_in(hpad, norm_g)
    z = _mm_nt(u, wt, WMAIN, T // 2, 512, "in_proj")
    zf = _mm_nt(u, wft, C, T // 2, C, "in_proj_ff")
    r, sprev = _ret_fwd(z, cst)
    ct = _fox_prep(zf, bf_pad, cst)
    if wout_full is None:
        a, g = _fox_fwd(z, ct, cst, None)
    else:
        a, g, landed_wout = _fox_fwd(z, ct, cst, wout)
        wout = wout_full(landed_wout)
    yt, dopad, dob, loss8, dfg = _out_loss(r, z, a, wout, x, tgt, final_g)
    dr, da, dzrg, dzfg, delta = _dy_gate_bwd(dob, wout, r, z, a, cst["seg"])
    dwout = _mm_nn(yt, dob, 512, D, "dw_out")
    dzq_r, dzk_r, dzv_r = _ret_bwd(z, cst, sprev, dr)
    dq_po, drow, dzk_f, dzv_f, dcol = _fox_bwd(z, da, g, delta, ct, cst)
    dzf, dbf = _fox_gate_bwd(drow, dcol, zf, bf_pad, cst)
    dzs = [dzq_r, dzk_r, dzv_r, dzrg, _token_order(dq_po), dzk_f, dzv_f, dzfg]
    gwt = _dw_in(dzs, dzf, ut)
    parts = chip_sums(gwt, dwout) if chip_sums else []
    gh, dng, *landed = _du_norm_bwd(dzs, dzf, wt, wft, hpad, norm_g, dopad, parts)
    return (loss8[0, 0], gh[C:], gh[PAD:C], dng, gwt, dbf[:, :NFF], dwout, dfg, parts, landed)


def _all_gather_shards(shards):
    n = len(shards)

    def body(*refs):
        ins, outs = refs[:n], refs[n:2 * n]
        send_sems, recv_sems = refs[2 * n:]
        x, y, c = _place()
        me_s = 2 * x + y
        sib = (x, y, 1 - c)
        chips = _other_chips(x, y)
        sends, waits = [], []
        for a in range(n):
            rows = ins[a].shape[0] // 2
            half = pl.ds(c * rows, rows)
            for k, (cx, cy, cs) in enumerate(chips):
                sends.append(pltpu.make_async_remote_copy(
                    src_ref=ins[a].at[half], dst_ref=outs[a].at[me_s, half],
                    send_sem=send_sems.at[6 * a + k], recv_sem=recv_sems.at[6 * a + k],
                    device_id=(cx, cy, c), device_id_type=MESH))
                sends[-1].start()
        for a in range(n):
            rows = ins[a].shape[0] // 2
            half = pl.ds(c * rows, rows)
            other = pl.ds((1 - c) * rows, rows)
            for k, (cx, cy, cs) in enumerate(chips):
                pltpu.make_async_remote_copy(
                    src_ref=outs[a].at[cs, half], dst_ref=outs[a].at[cs, half],
                    send_sem=send_sems.at[6 * a + k], recv_sem=recv_sems.at[6 * a + k],
                    device_id=(cx, cy, c), device_id_type=MESH).wait_recv()
                fwd = pltpu.make_async_remote_copy(
                    src_ref=outs[a].at[cs, half], dst_ref=outs[a].at[cs, half],
                    send_sem=send_sems.at[6 * a + 3 + k], recv_sem=recv_sems.at[6 * a + 3 + k],
                    device_id=sib, device_id_type=MESH)
                fwd.start()
                sends.append(fwd)
                waits.append(pltpu.make_async_remote_copy(
                    src_ref=outs[a].at[cs, other], dst_ref=outs[a].at[cs, other],
                    send_sem=send_sems.at[6 * a + 3 + k], recv_sem=recv_sems.at[6 * a + 3 + k],
                    device_id=sib, device_id_type=MESH))
        for w in waits:
            w.wait_recv()
        for s in sends:
            s.wait_send()

    return pl.pallas_call(
        body, name="all_gather_w",
        in_specs=[ANY] * n, out_specs=[ANY] * n,
        out_shape=[jax.ShapeDtypeStruct((4,) + s.shape, s.dtype) for s in shards],
        scratch_shapes=[pltpu.SemaphoreType.DMA((6 * n,)), pltpu.SemaphoreType.DMA((6 * n,))],
    )(*shards)


WOFF, WLEN = 1792, 2048
WHALF = WLEN // 2


def _pair_swap(gwt, arrs):
    n = len(arrs)

    def body(*refs):
        gw, ins = refs[0], refs[1:n + 1]
        gwo, outs = refs[n + 1], refs[n + 2:2 * n + 2]
        send_sems, recv_sems = refs[2 * n + 2:]
        x, y, c = _place()
        sib = (x, y, 1 - c)
        cps = []
        for k in range(4):
            cps.append(pltpu.make_async_remote_copy(
                src_ref=gw.at[pl.ds(WOFF * k + (1 - c) * WHALF, WHALF)], dst_ref=gwo.at[k],
                send_sem=send_sems.at[k], recv_sem=recv_sems.at[k], device_id=sib, device_id_type=MESH))
        for a in range(n):
            rows = ins[a].shape[1] // 2
            cps.append(pltpu.make_async_remote_copy(
                src_ref=ins[a].at[:, pl.ds((1 - c) * rows, rows)], dst_ref=outs[a],
                send_sem=send_sems.at[4 + a], recv_sem=recv_sems.at[4 + a], device_id=sib, device_id_type=MESH))
        for cp in cps:
            cp.start()
        for cp in cps:
            cp.wait()

    return pl.pallas_call(
        body, name="rs_pair_swap",
        in_specs=[ANY] * (n + 1), out_specs=[ANY] * (n + 1),
        out_shape=[jax.ShapeDtypeStruct((4, WHALF, D), gwt.dtype)]
        + [jax.ShapeDtypeStruct((4, a.shape[1] // 2, a.shape[2]), a.dtype) for a in arrs],
        scratch_shapes=[pltpu.SemaphoreType.DMA((n + 4,)), pltpu.SemaphoreType.DMA((n + 4,))],
    )(gwt, *arrs)


def _add_windows(gwt, recv):
    tb = 256
    nb = WHALF // tb
    c = lax.axis_index("c")

    def body(c_ref, a_ref, b_ref, o_ref):
        o_ref[0] = (a_ref[...] + b_ref[0]).astype(BF)

    return pl.pallas_call(
        body, name="pair_add_in",
        grid_spec=pltpu.PrefetchScalarGridSpec(
            num_scalar_prefetch=1, grid=(4, nb),
            in_specs=[pl.BlockSpec((tb, D), lambda k, i, cr: ((WOFF // tb) * k + nb * cr[0] + i, 0)),
                      pl.BlockSpec((1, tb, D), lambda k, i, cr: (k, i, 0))],
            out_specs=pl.BlockSpec((1, tb, D), lambda k, i, cr: (k, i, 0))),
        out_shape=jax.ShapeDtypeStruct(recv.shape, BF),
        compiler_params=_params(("parallel", "parallel")),
    )(jnp.reshape(c, (1,)).astype(jnp.int32), gwt, recv)


def _chip_exchange(parts, small):
    n = len(parts)

    def body(*refs):
        ins, sm = refs[:n], refs[n]
        outs, smo = refs[n + 1:2 * n + 1], refs[2 * n + 1]
        send_sems, recv_sems = refs[2 * n + 2:]
        cps = _chip_copies(ins, outs, send_sems, recv_sems, by_dest=True)
        cps += _chip_copies([sm], [smo], send_sems.at[pl.ds(3 * n, 3)], recv_sems.at[pl.ds(3 * n, 3)], by_dest=False)
        for cp in cps:
            cp.start()
        for cp in cps:
            cp.wait()

    return pl.pallas_call(
        body, name="rs_chip_exchange",
        in_specs=[ANY] * (n + 1), out_specs=[ANY] * (n + 1),
        out_shape=[jax.ShapeDtypeStruct(p.shape, p.dtype) for p in parts]
        + [jax.ShapeDtypeStruct((4,) + small.shape, small.dtype)],
        scratch_shapes=[pltpu.SemaphoreType.DMA((3 * (n + 1),)), pltpu.SemaphoreType.DMA((3 * (n + 1),))],
    )(*parts, small)


def _pair_send(halves):
    n = len(halves)

    def body(*refs):
        ins, outs = refs[:n], refs[n:2 * n]
        send_sems, recv_sems = refs[2 * n:]
        x, y, c = _place()
        cps = [pltpu.make_async_remote_copy(
            src_ref=ins[a], dst_ref=outs[a], send_sem=send_sems.at[a], recv_sem=recv_sems.at[a],
            device_id=(x, y, 1 - c), device_id_type=MESH) for a in range(n)]
        for cp in cps:
            cp.start()
        for cp in cps:
            cp.wait()

    return pl.pallas_call(
        body, name="rs_pair_send",
        in_specs=[ANY] * n, out_specs=[ANY] * n,
        out_shape=[jax.ShapeDtypeStruct(h.shape, h.dtype) for h in halves],
        scratch_shapes=[pltpu.SemaphoreType.DMA((n,)), pltpu.SemaphoreType.DMA((n,))],
    )(*halves)


def _row_block(rows):
    for tb in (256, 128, 64, 32, 16, 8):
        if rows % tb == 0:
            return tb
    return rows


def _add_halves(full, recv, name, out_dtype):
    _, r2, w = recv.shape
    tb = _row_block(r2)
    nb = r2 // tb
    c = lax.axis_index("c")

    def body(c_ref, a_ref, b_ref, o_ref):
        o_ref[...] = (a_ref[...] + b_ref[...]).astype(o_ref.dtype)

    return pl.pallas_call(
        body, name=name,
        grid_spec=pltpu.PrefetchScalarGridSpec(
            num_scalar_prefetch=1, grid=(4, nb),
            in_specs=[pl.BlockSpec((1, tb, w), lambda s, i, cr: (s, cr[0] * nb + i, 0)),
                      pl.BlockSpec((1, tb, w), lambda s, i, cr: (s, i, 0))],
            out_specs=pl.BlockSpec((1, tb, w), lambda s, i, cr: (s, i, 0))),
        out_shape=jax.ShapeDtypeStruct(recv.shape, out_dtype),
        compiler_params=_params(("parallel", "parallel")),
    )(jnp.reshape(c, (1,)).astype(jnp.int32), full, recv)


def _add2(a, b, name):
    def body(a_ref, b_ref, o_ref):
        o_ref[...] = a_ref[...] + b_ref[...]

    return pl.pallas_call(body, name=name, out_shape=jax.ShapeDtypeStruct(a.shape, a.dtype))(a, b)


def _sum4(buf, own, name):
    _, r, w = buf.shape
    tb = _row_block(r)
    me_s = 2 * lax.axis_index("x") + lax.axis_index("y")
    by_dest = own.ndim == 3

    def body(s_ref, b_ref, own_ref, o_ref):
        mine = (own_ref[0] if by_dest else own_ref[...]).astype(F32)
        terms = [jnp.where(s_ref[0] == t, mine, b_ref[t].astype(F32)) for t in range(4)]
        o_ref[...] = ((terms[0] + terms[1]) + terms[2]) + terms[3]

    own_spec = (pl.BlockSpec((1, tb, w), lambda i, sr: (sr[0], i, 0)) if by_dest
                else pl.BlockSpec((tb, w), lambda i, sr: (i, 0)))
    return pl.pallas_call(
        body, name=name,
        grid_spec=pltpu.PrefetchScalarGridSpec(
            num_scalar_prefetch=1, grid=(r // tb,),
            in_specs=[pl.BlockSpec((4, tb, w), lambda i, sr: (0, i, 0)), own_spec],
            out_specs=pl.BlockSpec((tb, w), lambda i, sr: (i, 0))),
        out_shape=jax.ShapeDtypeStruct((r, w), F32),
        compiler_params=_params(("parallel",)),
    )(jnp.reshape(me_s, (1,)).astype(jnp.int32), buf, own)


def _adamw_math(w, g, m, v):
    mn = B1 * m + (1.0 - B1) * g
    vn = B2 * v + (1.0 - B2) * (g * g)
    m_hat = mn / (1.0 - B1 ** STEP)
    v_hat = vn / (1.0 - B2 ** STEP)
    return -LR * (m_hat / (jnp.sqrt(v_hat) + AEPS) + WD * w), mn, vn


def _adamw(w, g, m, v, name):
    r, c_ = w.shape
    tb = _row_block(r)
    if tb == r and r > 512:
        tb = 256

    def body(w_ref, g_ref, m_ref, v_ref, d_ref, mo_ref, vo_ref):
        d_ref[...], mo_ref[...], vo_ref[...] = _adamw_math(w_ref[...], g_ref[...], m_ref[...], v_ref[...])

    spec = pl.BlockSpec((tb, c_), lambda i: (i, 0))
    return pl.pallas_call(
        body, name=name, grid=(pl.cdiv(r, tb),),
        in_specs=[spec] * 4, out_specs=[spec] * 3,
        out_shape=[jax.ShapeDtypeStruct(w.shape, F32)] * 3,
        compiler_params=_params(("parallel",)),
    )(w, g, m, v)


def _adamw_halves(w, g_mine, g_sib, m, v, name):
    r, c_ = w.shape
    r2 = g_mine.shape[0]
    tb = _row_block(r2)
    nb = r2 // tb
    c = lax.axis_index("c")

    def body(c_ref, w_ref, gm_ref, gs_ref, m_ref, v_ref, g_ref, d_ref, mo_ref, vo_ref):
        g = jnp.where(pl.program_id(0) == c_ref[0], gm_ref[...], gs_ref[...])
        g_ref[...] = g
        d_ref[...], mo_ref[...], vo_ref[...] = _adamw_math(w_ref[...], g, m_ref[...], v_ref[...])

    full = pl.BlockSpec((tb, c_), lambda h, i, cr: (h * nb + i, 0))
    half = pl.BlockSpec((tb, c_), lambda h, i, cr: (i, 0))
    return pl.pallas_call(
        body, name=name,
        grid_spec=pltpu.PrefetchScalarGridSpec(
            num_scalar_prefetch=1, grid=(2, nb),
            in_specs=[full, half, half, full, full], out_specs=[full] * 4),
        out_shape=[jax.ShapeDtypeStruct(w.shape, F32)] * 4,
        compiler_params=_params(("parallel", "parallel")),
    )(jnp.reshape(c, (1,)).astype(jnp.int32), w, g_mine, g_sib, m, v)


def kernel(x, meta_tokens, norm_g, w_in, b_f, w_out, final_g, loss_target, m_meta_tokens, m_norm_g, m_w_in, m_b_f, m_w_out, m_final_g, v_meta_tokens, v_norm_g, v_w_in, v_b_f, v_w_out, v_final_g):
    me_s = 2 * lax.axis_index("x") + lax.axis_index("y")
    core = lax.axis_index("c")
    wt, mt, vt = [jnp.swapaxes(t[0], 0, 1) for t in (w_in, m_w_in, v_w_in)]

    own_win = lax.dynamic_update_slice(jnp.zeros((WPADROWS, D), F32), wt, (4 * me_s, 0)).astype(BF)
    own = [own_win, meta_tokens]
    gathered = _all_gather_shards(own)
    mine = (jnp.arange(4) == me_s)[:, None, None]
    win, gmeta = [jnp.where(mine, o[None], g) for o, g in zip(own, gathered)]
    wout_own = w_out[0].astype(BF)
    lap = WPADROWS - WOFF
    tails = jnp.concatenate([jnp.zeros((1, lap, D), BF), win[:-1, WOFF:]], axis=0)
    wt_main = jnp.concatenate([win[:, :lap] + tails, win[:, lap:WOFF]], axis=1).reshape(WMAIN, D)
    wft = jnp.pad(win[3, WOFF:WOFF + NFF], ((0, C - NFF), (0, 0)))
    meta = jnp.concatenate([gmeta[s] for s in range(4)], axis=1)

    def wout_full(landed):
        return jnp.where(mine, wout_own[None], landed).reshape(DMIX, D)

    def chip_sums(gwt, dwout):
        g_out = dwout.reshape(4, DMIX // 4, D)
        r_in, r_out = _pair_swap(gwt, [g_out])
        return [_add_windows(gwt, r_in), _add_halves(g_out, r_out, "pair_add_out", BF)]

    loss, gx, dmeta, dng, gwt, dbf, dwout, dfg, (p_in, p_out), (e_in, e_out) = _local_step(
        x[0], loss_target[0], meta, norm_g, wt_main, wft, b_f, wout_own, final_g.reshape(1, D), chip_sums, wout_full)

    g_meta = jnp.stack([dmeta[:, 256 * s:256 * (s + 1)] for s in range(4)])
    small = jnp.concatenate([dng, dfg, jnp.pad(dbf, ((0, 0), (0, D - NFF))),
                             jnp.pad(jnp.reshape(loss, (1, 1)), ((0, 0), (0, D - 1))),
                             jnp.zeros((4, D), F32)], axis=0)
    e_meta, e_small = _chip_exchange([g_meta], small)
    h_in, h_out = _sum4(e_in, p_in, "sum_in"), _sum4(e_out, p_out, "sum_out")
    h_meta, h_small = _sum4(e_meta, g_meta, "sum_meta"), _sum4(e_small, small, "sum_small")
    s_in, s_out, s_meta, s_small = _pair_send([h_in, h_out, h_meta, h_small])
    gw_meta = _add2(h_meta, s_meta, "pair_add_meta")
    tot = _add2(h_small, s_small, "pair_add_small")
    g_norm, g_final, g_bf, loss_all = tot[0:1], tot[1], tot[2:3, :NFF], tot[3, 0]

    d_meta, nm_meta, nv_meta = _adamw(meta_tokens, gw_meta, m_meta_tokens, v_meta_tokens, "adamw_meta")
    d_norm, nm_norm, nv_norm = _adamw(norm_g, g_norm, m_norm_g, v_norm_g, "adamw_norm")
    window = jnp.concatenate([jnp.where(core == 0, h_in, s_in), jnp.where(core == 0, s_in, h_in)], axis=0)
    gwt_own = lax.dynamic_slice(window, (4 * me_s, 0), (WSH, D))
    d_in, nm_in, nv_in = _adamw(wt, gwt_own, mt, vt, "adamw_in")
    gw_in, d_in, nm_in, nv_in = [jnp.swapaxes(t, 0, 1)[None] for t in (gwt_own, d_in, nm_in, nv_in)]
    d_bf, nm_bf, nv_bf = _adamw(b_f, g_bf, m_b_f, v_b_f, "adamw_bf")
    gw_out, d_out, nm_out, nv_out = _adamw_halves(w_out[0], h_out, s_out, m_w_out[0], v_w_out[0], "adamw_out")
    d_fin, nm_fin, nv_fin = _adamw(final_g.reshape(1, D), g_final.reshape(1, D), m_final_g.reshape(1, D),
                                   v_final_g.reshape(1, D), "adamw_final")
    return (loss_all, gx[None], gw_meta, g_norm, gw_in, g_bf, gw_out[None], g_final,
            d_meta, d_norm, d_in, d_bf, d_out[None], d_fin.reshape(D),
            nm_meta, nm_norm, nm_in, nm_bf, nm_out[None], nm_fin.reshape(D),
            nv_meta, nv_norm, nv_in, nv_bf, nv_out[None], nv_fin.reshape(D))
```

```python
import numpy as np
import jax
import jax.numpy as jnp
from jax import lax
from jax.experimental import pallas as pl
from jax.experimental.pallas import tpu as pltpu

D = 1024
SEQ = 2048
NMETA = 16
C = 128
PAD = C - NMETA
T = PAD + NMETA + SEQ
NCH = T // C
RH, RDK, RDV = 4, 128, 256
FH, FD = 16, 64
NPAIR = FH // 2
WMAIN = 7168
NFF = 16
WIN = WMAIN + NFF
WSH = WIN // 4
WPADROWS = 1824
DMIX = 2048
EPS = 1e-6
NEG = -1e30
RSCALE = RDK ** -0.5
FSCALE = FD ** -0.5
ROPE_BASE = 10000.0
LR, B1, B2, AEPS, WD, STEP = 0.001, 0.9, 0.999, 1e-08, 0.01, 10

BF = jnp.bfloat16
F32 = jnp.float32
NT = (((1,), (1,)), ((), ()))
TN = (((0,), (0,)), ((), ()))
NN_DIMS = (((1,), (0,)), ((), ()))
MESH = pl.DeviceIdType.MESH
ANY = pl.BlockSpec(memory_space=pl.ANY)
VMEM_LIMIT = 48 * 1024 * 1024
DW_VMEM_LIMIT = 56 * 1024 * 1024

QB_R, KB_R = 0, 4
VB_R = 4
GB_R, GB_F = 2, 6
QB_F, KB_F, VB_F = 24, 32, 40


def _dot(a, b):
    return jnp.dot(a, b, preferred_element_type=F32)


def _dg(a, b, dims):
    return lax.dot_general(a, b, dims, preferred_element_type=F32)


def _params(sem=None):
    return pltpu.CompilerParams(dimension_semantics=sem, vmem_limit_bytes=VMEM_LIMIT)


def _constants():
    pos = jnp.arange(T, dtype=F32) - PAD
    inv = ROPE_BASE ** (-jnp.arange(0, RDK, 2, dtype=F32) / RDK)
    ang = pos[:, None] * inv[None, :]
    cos, sin = jnp.cos(ang), jnp.sin(ang)
    cos2 = jnp.concatenate([cos, cos], axis=1)
    sin2 = jnp.concatenate([-sin, sin], axis=1)
    log_gamma = jnp.log1p(-jnp.exp2(-5.0 - jnp.arange(RH, dtype=F32)))
    idx = jnp.arange(C, dtype=F32)
    diff = idx[:, None] - idx[None, :]
    dmask = jnp.where(diff[None] >= 0, jnp.exp(log_gamma[:, None, None] * jnp.maximum(diff, 0.0)[None]), 0.0)
    zeta = jnp.exp(log_gamma[:, None] * (C - 1.0 - idx)[None, :])
    xi = jnp.exp(log_gamma[:, None] * (idx + 1.0)[None, :])
    gdec = jnp.exp(log_gamma * C)
    zeta_b = jnp.broadcast_to(zeta[:, :, None], (RH, C, RDK))
    xi_b = jnp.broadcast_to(xi[:, :, None], (RH, C, RDK))
    gdec_b = jnp.broadcast_to(gdec[:, None, None], (RH, RDK, RDV))
    tri = jnp.asarray(np.tril(np.ones((C, C), np.float32)), dtype=BF)
    head_of_lane = np.arange(FH * FD) // FD
    pick = ((np.arange(FH * FD)[:, None] % FD == 0)
            & (head_of_lane[:, None] == np.arange(C)[None, :])).astype(np.float32)
    seg = (np.arange(C)[:, None] // FD == np.arange(C)[None, :] // FD).astype(np.float32)
    ones_aug = np.concatenate([np.tile((np.arange(C) < FD)[None, :], (C, 1)),
                               np.tile((np.arange(C) >= FD)[None, :], (C, 1))], axis=0).astype(np.float32)
    lane = np.arange(2 * C) % C
    causal = np.where(lane[None, :] <= np.arange(C)[:, None], 0.0, NEG).astype(np.float32)
    mask_bias = np.stack([np.zeros((C, 2 * C), np.float32), causal])
    return dict(cos2=cos2, sin2=sin2, dmask=dmask, zeta=zeta_b, xi=xi_b, gdec=gdec_b, tri=tri,
                mask_bias=jnp.asarray(mask_bias), pick=jnp.asarray(pick, dtype=BF), seg=jnp.asarray(seg, dtype=BF),
                ones_aug=jnp.asarray(ones_aug, dtype=BF))


def _norm_in(hpad, g):
    def body(h_ref, g_ref, u_ref, ut_ref):
        h = h_ref[...]
        rs = lax.rsqrt(jnp.mean(h * h, axis=1, keepdims=True) + EPS)
        u = h * rs * g_ref[...]
        u_ref[...] = u.astype(BF)
        ut_ref[...] = u.T.astype(BF)

    return pl.pallas_call(
        body, name="norm_in", grid=(NCH,),
        in_specs=[pl.BlockSpec((C, D), lambda i: (i, 0)), pl.BlockSpec((1, D), lambda i: (0, 0))],
        out_specs=[pl.BlockSpec((C, D), lambda i: (i, 0)), pl.BlockSpec((D, C), lambda i: (0, i))],
        out_shape=[jax.ShapeDtypeStruct((T, D), BF), jax.ShapeDtypeStruct((D, T), BF)],
        compiler_params=_params(("parallel",)),
    )(hpad, g)


def _mm_nt(a, b, n, tm, tn, name):
    m, k = a.shape

    def body(a_ref, b_ref, o_ref):
        o_ref[...] = _dg(a_ref[...], b_ref[...], NT)

    return pl.pallas_call(
        body, name=name, grid=(m // tm, n // tn),
        in_specs=[pl.BlockSpec((tm, k), lambda i, j: (i, 0)), pl.BlockSpec((tn, k), lambda i, j: (j, 0))],
        out_specs=pl.BlockSpec((tm, tn), lambda i, j: (i, j)),
        out_shape=jax.ShapeDtypeStruct((m, n), F32),
        compiler_params=_params(("parallel", "parallel")),
    )(a, b)


def _mm_nn(a, b, tm, tn, name, out_dtype=F32):
    m, k = a.shape
    _, n = b.shape

    def body(a_ref, b_ref, o_ref):
        o_ref[...] = _dot(a_ref[...], b_ref[...]).astype(out_dtype)

    return pl.pallas_call(
        body, name=name, grid=(m // tm, n // tn),
        in_specs=[pl.BlockSpec((tm, k), lambda i, j: (i, 0)), pl.BlockSpec((k, tn), lambda i, j: (0, j))],
        out_specs=pl.BlockSpec((tm, tn), lambda i, j: (i, j)),
        out_shape=jax.ShapeDtypeStruct((m, n), out_dtype),
        compiler_params=_params(("parallel", "parallel")),
    )(a, b)


def _rot(x, cos2, sin2):
    return x * cos2 + pltpu.roll(x, 64, 1) * sin2


def _ret_specs(chunk):
    whole = lambda shape: pl.BlockSpec(shape, lambda n: (0,) * len(shape))
    return [
        pl.BlockSpec((C, RH * RDK), lambda n: (chunk(n), 0)),
        pl.BlockSpec((C, RH * RDK), lambda n: (chunk(n), 1)),
        pl.BlockSpec((C, RH * RDV), lambda n: (chunk(n), 1)),
        pl.BlockSpec((C, RDK), lambda n: (chunk(n), 0)),
        pl.BlockSpec((C, RDK), lambda n: (chunk(n), 0)),
        whole((RH, C, C)), whole((RH, C, RDK)), whole((RH, C, RDK)), whole((RH, RDK, RDV)),
    ]


def _ret_heads(q_ref, k_ref, v_ref, cos, sin):
    qr = [_rot(q_ref[:, RDK * h:RDK * (h + 1)], cos, sin) for h in range(RH)]
    kr = [_rot(k_ref[:, RDK * h:RDK * (h + 1)], cos, sin) * RSCALE for h in range(RH)]
    vb = [v_ref[:, RDV * h:RDV * (h + 1)].astype(BF) for h in range(RH)]
    return qr, kr, [t.astype(BF) for t in qr], [t.astype(BF) for t in kr], vb


def _ret_fwd(z, cst):
    def body(q_ref, k_ref, v_ref, cos_ref, sin_ref, dm_ref, xi_ref, zt_ref, gd_ref, r_ref, sp_ref, st):
        n = pl.program_id(0)

        @pl.when(n == 0)
        def _():
            st[...] = jnp.zeros_like(st)

        hs = range(RH)
        qr, kr, qb, kb, vb = _ret_heads(q_ref, k_ref, v_ref, cos_ref[...], sin_ref[...])
        sd = [(_dg(qb[h], kb[h], NT) * dm_ref[h]).astype(BF) for h in hs]
        state = [st[h] for h in hs]
        qx = [(qr[h] * xi_ref[h]).astype(BF) for h in hs]
        kz = [(kr[h] * zt_ref[h]).astype(BF) for h in hs]
        out = [_dot(sd[h], vb[h]) + _dot(qx[h], state[h].astype(BF)) for h in hs]
        kv = [_dg(kz[h], vb[h], TN) for h in hs]
        for h in hs:
            sp_ref[0, h] = state[h]
            r_ref[:, RDV * h:RDV * (h + 1)] = out[h]
            st[h] = state[h] * gd_ref[h] + kv[h]

    return pl.pallas_call(
        body, name="ret_fwd", grid=(NCH,),
        in_specs=_ret_specs(lambda n: n),
        out_specs=[pl.BlockSpec((C, RH * RDV), lambda n: (n, 0)),
                   pl.BlockSpec((1, RH, RDK, RDV), lambda n: (n, 0, 0, 0))],
        out_shape=[jax.ShapeDtypeStruct((T, RH * RDV), F32), jax.ShapeDtypeStruct((NCH, RH, RDK, RDV), F32)],
        scratch_shapes=[pltpu.VMEM((RH, RDK, RDV), F32)],
        compiler_params=_params(("arbitrary",)),
    )(z, z, z, cst["cos2"], cst["sin2"], cst["dmask"], cst["xi"], cst["zeta"], cst["gdec"])


def _ret_bwd(z, cst, sprev, dr):
    def body(q_ref, k_ref, v_ref, cos_ref, sin_ref, dm_ref, xi_ref, zt_ref, gd_ref, sp_ref, dr_ref,
             dq_ref, dk_ref, dv_ref, gst):
        i = pl.program_id(0)

        @pl.when(i == 0)
        def _():
            gst[...] = jnp.zeros_like(gst)

        hs = range(RH)
        cos, sin = cos_ref[...], sin_ref[...]
        qr, kr, qb, kb, vb = _ret_heads(q_ref, k_ref, v_ref, cos, sin)
        dm = [dm_ref[h] for h in hs]
        xi = [xi_ref[h] for h in hs]
        zt = [zt_ref[h] for h in hs]
        sd = [(_dg(qb[h], kb[h], NT) * dm[h]).astype(BF) for h in hs]
        qx = [(qr[h] * xi[h]).astype(BF) for h in hs]
        kz = [(kr[h] * zt[h]).astype(BF) for h in hs]
        drb = [dr_ref[:, RDV * h:RDV * (h + 1)] for h in hs]
        sb = [sp_ref[0, h].astype(BF) for h in hs]
        g = [gst[h] for h in hs]
        gb = [t.astype(BF) for t in g]
        ds = [(_dg(drb[h], vb[h], NT) * dm[h]).astype(BF) for h in hs]
        dq = [_dot(ds[h], kb[h]) + _dg(drb[h], sb[h], NT) * xi[h] for h in hs]
        dk = [(_dg(ds[h], qb[h], TN) + _dg(vb[h], gb[h], NT) * zt[h]) * RSCALE for h in hs]
        dv = [_dg(sd[h], drb[h], TN) + _dot(kz[h], gb[h]) for h in hs]
        gn = [g[h] * gd_ref[h] + _dg(qx[h], drb[h], TN) for h in hs]
        for h in hs:
            gst[h] = gn[h]
            dq_ref[:, RDK * h:RDK * (h + 1)] = (dq[h] * cos + pltpu.roll(dq[h] * sin, 64, 1)).astype(BF)
            dk_ref[:, RDK * h:RDK * (h + 1)] = (dk[h] * cos + pltpu.roll(dk[h] * sin, 64, 1)).astype(BF)
            dv_ref[:, RDV * h:RDV * (h + 1)] = dv[h].astype(BF)

    rev = lambda n: NCH - 1 - n
    return pl.pallas_call(
        body, name="ret_bwd", grid=(NCH,),
        in_specs=_ret_specs(rev) + [
            pl.BlockSpec((1, RH, RDK, RDV), lambda n: (rev(n), 0, 0, 0)),
            pl.BlockSpec((C, RH * RDV), lambda n: (rev(n), 0)),
        ],
        out_specs=[pl.BlockSpec((C, RH * RDK), lambda n: (rev(n), 0)),
                   pl.BlockSpec((C, RH * RDK), lambda n: (rev(n), 0)),
                   pl.BlockSpec((C, RH * RDV), lambda n: (rev(n), 0))],
        out_shape=[jax.ShapeDtypeStruct((T, RH * RDK), BF), jax.ShapeDtypeStruct((T, RH * RDK), BF),
                   jax.ShapeDtypeStruct((T, RH * RDV), BF)],
        scratch_shapes=[pltpu.VMEM((RH, RDK, RDV), F32)],
        compiler_params=_params(("arbitrary",)),
    )(z, z, z, cst["cos2"], cst["sin2"], cst["dmask"], cst["xi"], cst["zeta"], cst["gdec"], sprev, dr)


def _place():
    x, y, c = lax.axis_index("x"), lax.axis_index("y"), lax.axis_index("c")
    return x, y, c


def _other_chips(x, y):
    return [(1 - x, y, 2 * (1 - x) + y), (x, 1 - y, 2 * x + (1 - y)), (1 - x, 1 - y, 2 * (1 - x) + (1 - y))]


def _chip_copies(srcs, lands, send_sems, recv_sems, by_dest):
    x, y, c = _place()
    me_s = 2 * x + y
    return [pltpu.make_async_remote_copy(
        src_ref=src.at[cs] if by_dest else src, dst_ref=land.at[me_s],
        send_sem=send_sems.at[3 * a + j], recv_sem=recv_sems.at[3 * a + j],
        device_id=(cx, cy, c), device_id_type=MESH)
        for a, (src, land) in enumerate(zip(srcs, lands)) for j, (cx, cy, cs) in enumerate(_other_chips(x, y))]


def _split_dot(x, mat01, dims=NN_DIMS, x_first=True):
    acc, rest = None, x
    for _ in range(3):
        piece = rest.astype(BF)
        part = _dg(piece, mat01, dims) if x_first else _dg(mat01, piece, dims)
        acc = part if acc is None else acc + part
        rest = rest - piece.astype(F32)
    return acc


def _log_sigmoid(x):
    return -(jnp.maximum(-x, 0.0) + jnp.log1p(jnp.exp(-jnp.abs(x))))


def _fox_prep(zf, bf_pad, cst):
    def body(zf_ref, b_ref, tri_ref, ct_ref, carry):
        n = pl.program_id(0)

        @pl.when(n == 0)
        def _():
            carry[...] = jnp.zeros_like(carry)

        ls = _log_sigmoid(zf_ref[...] + b_ref[...])
        row = n * C + lax.broadcasted_iota(jnp.int32, (C, C), 0)
        lf = jnp.where(row >= PAD, ls, 0.0)
        cc = _split_dot(lf, tri_ref[...], x_first=False) + carry[0:1, :]
        carry[...] = jnp.broadcast_to(cc[C - 1:C, :], carry.shape)
        pos = n * C + lax.broadcasted_iota(jnp.int32, (FH, C), 1)
        ct_ref[0] = jnp.where(pos >= PAD, cc.T[:FH, :], -NEG)

    return pl.pallas_call(
        body, name="fox_prep", grid=(NCH,),
        in_specs=[pl.BlockSpec((C, C), lambda n: (n, 0)), pl.BlockSpec((1, C), lambda n: (0, 0)),
                  pl.BlockSpec((C, C), lambda n: (0, 0))],
        out_specs=pl.BlockSpec((1, FH, C), lambda n: (n, 0, 0)),
        out_shape=jax.ShapeDtypeStruct((NCH, FH, C), F32),
        scratch_shapes=[pltpu.VMEM((8, C), F32)],
        compiler_params=_params(("arbitrary",)),
    )(zf, bf_pad, cst["tri"])


def _lo_lanes(shape):
    return lax.broadcasted_iota(jnp.int32, shape, 1) < FD


def _split_heads(x):
    lo = _lo_lanes(x.shape)
    zero = jnp.zeros_like(x)
    return jnp.concatenate([jnp.where(lo, x, zero), jnp.where(lo, zero, x)], axis=0)


def _spread2(x):
    lo = _lo_lanes(x.shape)
    r = pltpu.roll(x, FD, 1)
    return jnp.concatenate([jnp.where(lo, x, r), jnp.where(lo, r, x)], axis=1)


NSTEP = (NCH + 1) // 2
NTILE = NCH + 1
TROWS = T + C


def _fox_tile(s, t):
    second = t > s
    return second.astype(jnp.int32), jnp.where(second, t - s - 1, s - t)


def _fox_pos(i):
    return jnp.where(i < NSTEP, 2 * i, 2 * (NCH - 1 - i) + 1)


FOX_ORDER = [2 * i if i < NSTEP else 2 * (NCH - 1 - i) + 1 for i in range(NCH)]


def _fox_pair_specs():
    first = pl.BlockSpec((C, C), lambda p, s: (2 * s, p))
    second = pl.BlockSpec((C, C), lambda p, s: (jnp.where(s == NSTEP - 1, 2 * s, 2 * s + 1), p))
    both = pl.BlockSpec((2 * C, C), lambda p, s: (s, p))
    return first, second, both


def _fox_q_specs():
    return (pl.BlockSpec((C, C), lambda p, s: (s, QB_F + p)),
            pl.BlockSpec((C, C), lambda p, s: (NCH - 1 - s, QB_F + p)))


def _fox_key_bias(ct_ref, p, j):
    return jnp.concatenate([ct_ref[j, pl.ds(2 * p, 1), :], ct_ref[j, pl.ds(2 * p + 1, 1), :]], axis=1)


def _fox_fwd(z, ct, cst, share):
    n = 0 if share is None else 1

    def body(qa_ref, qb_ref, k_ref, v_ref, ct_ref, ones_ref, mb_ref, *rest):
        share_refs, (a_ref, g_ref), land_refs = rest[:n], rest[n:n + 2], rest[n + 2:2 * n + 2]
        kks, vvs, q2, m2, sbuf = rest[2 * n + 2:2 * n + 7]
        p, s = pl.program_id(0), pl.program_id(1)
        if n:
            copies = _chip_copies(share_refs, land_refs, *rest[2 * n + 7:], by_dest=False)

            @pl.when((p == 0) & (s == 0))
            def _():
                for cp in copies:
                    cp.start()

            @pl.when((p == NPAIR - 1) & (s == NSTEP - 1))
            def _():
                for cp in copies:
                    cp.wait()

        @pl.when(s == 0)
        def _():
            ones = ones_ref[...]

            def prep(j, carry):
                rows = pl.ds(pl.multiple_of(j * C, C), C)
                kks[j] = _split_heads(k_ref[rows, :]).astype(BF)
                vvs[j] = jnp.concatenate([_split_heads(v_ref[rows, :]).astype(BF), ones], axis=1)
                return carry

            lax.fori_loop(0, NCH, prep, 0)

        q2[0] = (qa_ref[...] * FSCALE).astype(BF)
        q2[1] = (qb_ref[...] * FSCALE).astype(BF)

        tiles = [_fox_tile(s, t) for t in range(NTILE)]
        causal = mb_ref[1]
        neg = jnp.full((C, 2 * C), NEG, F32)
        run, first = neg, neg
        for t, (sel, j) in enumerate(tiles):
            st = _dg(q2[sel], kks[j], NT) - _fox_key_bias(ct_ref, p, j)
            if t in (0, NTILE - 1):
                st = st + causal
            sbuf[t] = st
            run = jnp.maximum(jnp.where(t == s + 1, neg, run), st)
            first = jnp.where(t == s, run, first)
        for w, mx in enumerate((first, run)):
            m2[w] = jnp.concatenate(
                [jnp.broadcast_to(jnp.max(mx[:, :C], axis=1, keepdims=True), (C, C)),
                 jnp.broadcast_to(jnp.max(mx[:, C:], axis=1, keepdims=True), (C, C))], axis=1)

        zero = jnp.zeros((C, 2 * C), F32)
        run, first = zero, zero
        for t, (sel, j) in enumerate(tiles):
            run = jnp.where(t == s + 1, zero, run) + _dot(jnp.exp(sbuf[t] - m2[sel]).astype(BF), vvs[j])
            first = jnp.where(t == s, run, first)
        lo = _lo_lanes((C, C))
        for w, res in enumerate((first, run)):
            l = res[:, C:]
            a_ref[C * w:C * (w + 1), :] = res[:, :C] / l
            mw = m2[w]
            g_ref[C * w:C * (w + 1), :] = -(jnp.where(lo, mw[:, :C], mw[:, C:]) + jnp.log(l))

    qa, qb = _fox_q_specs()
    both = _fox_pair_specs()[2]
    return pl.pallas_call(
        body, name="fox_fwd", grid=(NPAIR, NSTEP),
        in_specs=[qa, qb,
                  pl.BlockSpec((T, C), lambda p, s: (0, KB_F + p)),
                  pl.BlockSpec((T, C), lambda p, s: (0, VB_F + p)),
                  pl.BlockSpec((NCH, FH, C), lambda p, s: (0, 0, 0)),
                  pl.BlockSpec((2 * C, C), lambda p, s: (0, 0)),
                  pl.BlockSpec((2, C, 2 * C), lambda p, s: (0, 0, 0))] + [ANY] * n,
        out_specs=[both, both] + [ANY] * n,
        out_shape=[jax.ShapeDtypeStruct((TROWS, FH * FD), F32)] * 2
        + ([jax.ShapeDtypeStruct((4,) + share.shape, share.dtype)] if n else []),
        scratch_shapes=[pltpu.VMEM((NCH, 2 * C, C), BF), pltpu.VMEM((NCH, 2 * C, 2 * C), BF),
                        pltpu.VMEM((2, C, C), BF), pltpu.VMEM((2, C, 2 * C), F32),
                        pltpu.VMEM((NTILE, C, 2 * C), F32)]
        + [pltpu.SemaphoreType.DMA((3,)), pltpu.SemaphoreType.DMA((3,))] * n,
        compiler_params=_params(("arbitrary", "arbitrary")),
    )(z, z, z, z, ct, cst["ones_aug"], cst["mask_bias"], *([share] * n))


def _fox_bwd(z, da, g, delta, ct, cst):
    grp = 9

    def body(qa_ref, qb_ref, daa_ref, dab_ref, ga_ref, gb_ref, dla_ref, dlb_ref, k_ref, v_ref, ct_ref, ones_ref,
             mb_ref, dq_ref, dr_ref, dk_ref, dv_ref, dcs_ref,
             kks, vvs, q2, qq2, dd2, da2, gi2, dl2, dq2, dvb, dkb, dkacc, dvacc, csacc):
        p, s = pl.program_id(0), pl.program_id(1)
        ones = ones_ref[...]

        @pl.when(s == 0)
        def _():
            dkacc[...] = jnp.zeros_like(dkacc)
            dvacc[...] = jnp.zeros_like(dvacc)
            csacc[...] = jnp.zeros_like(csacc)

            def prep(j, carry):
                rows = pl.ds(pl.multiple_of(j * C, C), C)
                kks[j] = _split_heads(k_ref[rows, :]).astype(BF)
                vvs[j] = _split_heads(v_ref[rows, :]).astype(BF)
                return carry

            lax.fori_loop(0, NCH, prep, 0)

        for w, (q_ref, d_ref, g_ref, l_ref) in enumerate(((qa_ref, daa_ref, ga_ref, dla_ref),
                                                          (qb_ref, dab_ref, gb_ref, dlb_ref))):
            qf = q_ref[...]
            q2[w] = (qf * FSCALE).astype(BF)
            qq2[w] = jnp.concatenate([_split_heads(qf).astype(BF), ones], axis=1)
            da2[w] = d_ref[...]
            dd2[w] = _split_heads(d_ref[...].astype(F32)).astype(BF)
            gi2[w] = _spread2(g_ref[...])
            dl2[w] = _spread2(l_ref[...])
        dq2[...] = jnp.zeros_like(dq2)
        zero = jnp.zeros((C, 2 * C), F32)

        def group(gi, carry):
            ts = [gi * grp + u for u in range(grp)]
            tiles = [_fox_tile(s, t) for t in ts]
            kk = [kks[j] for _, j in tiles]
            ss = [_dg(q2[sel], kj, NT) + (gi2[sel] - _fox_key_bias(ct_ref, p, j)) for kj, (sel, j) in zip(kk, tiles)]
            ss[0] = ss[0] + mb_ref[(gi == 0).astype(jnp.int32)]
            ss[-1] = ss[-1] + mb_ref[(gi == 1).astype(jnp.int32)]
            dps = [_dg(da2[sel], vvs[j], NT) for sel, j in tiles]
            pes = [jnp.exp(st) for st in ss]
            dss = [pe * (dp - dl2[sel]) * FSCALE for pe, dp, (sel, _) in zip(pes, dps, tiles)]
            pts = [jnp.concatenate([pe[:, :C].T, pe[:, C:].T], axis=1).astype(BF) for pe in pes]
            dsts = [jnp.concatenate([ds[:, :C].T, ds[:, C:].T], axis=1).astype(BF) for ds in dss]
            dvs = [_dot(pt, dd2[sel]) for pt, (sel, _) in zip(pts, tiles)]
            rs = [_dot(dst, qq2[sel]) for dst, (sel, _) in zip(dsts, tiles)]
            parts = [_dot(ds.astype(BF), jnp.concatenate([kj, ones], axis=1)) for ds, kj in zip(dss, kk)]
            for t, dv, rr in zip(ts, dvs, rs):
                dvb[t] = dv
                dkb[t] = rr
            pa, pb = zero, zero
            for t, part in zip(ts, parts):
                pa = pa + jnp.where(t <= s, part, zero)
                pb = pb + jnp.where(t <= s, zero, part)
            dq2[0] += pa
            dq2[1] += pb
            return carry

        ntile = jnp.where(s == NSTEP - 1, grp, NTILE)
        lax.fori_loop(0, ntile // grp, group, 0)

        def scatter(t, carry):
            _, j = _fox_tile(s, t)
            r = pl.ds(pl.multiple_of(j * C, C), C)
            dvacc[r, :] += dvb[t]
            dkacc[r, :] += dkb[t, :, :C]
            csacc[r, :] += dkb[t, :, C:]
            return carry

        lax.fori_loop(0, ntile, scatter, 0)
        for w in range(2):
            res = dq2[w]
            dq_ref[C * w:C * (w + 1), :] = res[:, :C].astype(BF)
            dr_ref[C * w:C * (w + 1), :] = res[:, C:]

        @pl.when(s == NSTEP - 1)
        def _():
            dk_ref[...] = dkacc[...].astype(BF)
            dv_ref[...] = dvacc[...].astype(BF)
            dcs_ref[...] = csacc[...]

    qa, qb = _fox_q_specs()
    ba, bb, both = _fox_pair_specs()
    col = pl.BlockSpec((T, C), lambda p, s: (0, p))
    return pl.pallas_call(
        body, name="fox_bwd", grid=(NPAIR, NSTEP),
        in_specs=[qa, qb, ba, bb, ba, bb, ba, bb,
                  pl.BlockSpec((T, C), lambda p, s: (0, KB_F + p)),
                  pl.BlockSpec((T, C), lambda p, s: (0, VB_F + p)),
                  pl.BlockSpec((NCH, FH, C), lambda p, s: (0, 0, 0)),
                  pl.BlockSpec((2 * C, C), lambda p, s: (0, 0)),
                  pl.BlockSpec((2, C, 2 * C), lambda p, s: (0, 0, 0))],
        out_specs=[both, both, col, col, col],
        out_shape=[jax.ShapeDtypeStruct((TROWS, FH * FD), BF), jax.ShapeDtypeStruct((TROWS, FH * FD), F32),
                   jax.ShapeDtypeStruct((T, FH * FD), BF), jax.ShapeDtypeStruct((T, FH * FD), BF),
                   jax.ShapeDtypeStruct((T, FH * FD), F32)],
        scratch_shapes=[pltpu.VMEM((NCH, 2 * C, C), BF), pltpu.VMEM((NCH, 2 * C, C), BF),
                        pltpu.VMEM((2, C, C), BF), pltpu.VMEM((2, 2 * C, 2 * C), BF), pltpu.VMEM((2, 2 * C, C), BF),
                        pltpu.VMEM((2, C, C), BF), pltpu.VMEM((2, C, 2 * C), F32), pltpu.VMEM((2, C, 2 * C), F32),
                        pltpu.VMEM((2, C, 2 * C), F32),
                        pltpu.VMEM((NTILE, C, C), F32), pltpu.VMEM((NTILE, C, 2 * C), F32),
                        pltpu.VMEM((T, C), F32), pltpu.VMEM((T, C), F32), pltpu.VMEM((T, C), F32)],
        compiler_params=_params(("parallel", "arbitrary")),
    )(z, z, da, da, g, g, delta, delta, z, z, ct, cst["ones_aug"], cst["mask_bias"])


def _fox_gate_bwd(drow, dcol, zf, bf_pad, cst):
    def body(dr_ref, dc_ref, zf_ref, b_ref, tri_ref, pick_ref, dff_ref, db_ref, carry):
        s = pl.program_id(0)
        n = NCH - 1 - s

        @pl.when(s == 0)
        def _():
            carry[...] = jnp.zeros_like(carry)
            db_ref[...] = jnp.zeros_like(db_ref)

        dcb = _split_dot((dr_ref[...] - dc_ref[...]) * (1.0 / FSCALE), pick_ref[...])
        suf = _split_dot(dcb, tri_ref[...], TN, x_first=False) + carry[0:1, :]
        carry[...] = jnp.broadcast_to(suf[0:1, :], carry.shape)
        x = zf_ref[...] + b_ref[...]
        row = n * C + lax.broadcasted_iota(jnp.int32, (C, C), 0)
        dff = jnp.where(row >= PAD, suf * (1.0 - jax.nn.sigmoid(x)), 0.0)
        dff_ref[...] = dff.astype(BF)
        db_ref[...] += jnp.sum(dff, axis=0, keepdims=True)

    rev = lambda s: (NCH - 1 - s, 0)
    return pl.pallas_call(
        body, name="fox_gate_bwd", grid=(NCH,),
        in_specs=[pl.BlockSpec((C, FH * FD), lambda s: (_fox_pos(NCH - 1 - s), 0)),
                  pl.BlockSpec((C, FH * FD), rev), pl.BlockSpec((C, C), rev),
                  pl.BlockSpec((1, C), lambda s: (0, 0)), pl.BlockSpec((C, C), lambda s: (0, 0)),
                  pl.BlockSpec((FH * FD, C), lambda s: (0, 0))],
        out_specs=[pl.BlockSpec((C, C), rev), pl.BlockSpec((1, C), lambda s: (0, 0))],
        out_shape=[jax.ShapeDtypeStruct((T, C), BF), jax.ShapeDtypeStruct((1, C), F32)],
        scratch_shapes=[pltpu.VMEM((8, C), F32)],
        compiler_params=_params(("arbitrary",)),
    )(drow, dcol, zf, bf_pad, cst["tri"], cst["pick"])


def _gated(r, rg, a, fg):
    rn, rs = [], []
    for h in range(RH):
        rh = r[:, RDV * h:RDV * (h + 1)]
        s = lax.rsqrt(jnp.mean(rh * rh, axis=1, keepdims=True) + EPS)
        rn.append(rh * s)
        rs.append(s)
    rn = jnp.concatenate(rn, axis=1)
    y = jnp.concatenate([rn * (rg * jax.nn.sigmoid(rg)), a * (fg * jax.nn.sigmoid(fg))], axis=1)
    return y, rn, rs


def _out_loss(r, z, a, wout, x, tgt, fgain):
    def body(r_ref, rg_ref, a_ref, fg_ref, w_ref, x_ref, t_ref, g_ref, yt_ref, do_ref, dob_ref, loss_ref, dg_ref):
        i = pl.program_id(0)

        @pl.when(i == 0)
        def _():
            yt_ref[...] = jnp.zeros_like(yt_ref)
            do_ref[...] = jnp.zeros_like(do_ref)
            dob_ref[...] = jnp.zeros_like(dob_ref)
            loss_ref[...] = jnp.zeros_like(loss_ref)
            dg_ref[...] = jnp.zeros_like(dg_ref)

        @pl.when(i > 0)
        def _():
            y, _, _ = _gated(r_ref[...], rg_ref[...], a_ref[...], fg_ref[...])
            yt_ref[...] = y.T.astype(BF)
            o = x_ref[...] + _dot(y.astype(BF), w_ref[...])
            rs = lax.rsqrt(jnp.mean(o * o, axis=1, keepdims=True) + EPS)
            on = o * rs
            g = g_ref[...]
            e = on * g - t_ref[...]
            loss_ref[...] += 0.5 * jnp.sum(jnp.mean(e * e, axis=1, keepdims=True))
            dyh = e * (1.0 / D)
            dg_ref[...] += jnp.sum(dyh * on, axis=0, keepdims=True)
            don = dyh * g
            do = rs * (don - on * jnp.mean(don * on, axis=1, keepdims=True))
            do_ref[...] = do
            dob_ref[...] = do.astype(BF)

    tok = lambda i: (jnp.maximum(i - 1, 0), 0)
    return pl.pallas_call(
        body, name="out_loss", grid=(NCH,),
        in_specs=[pl.BlockSpec((C, D), lambda i: (i, 0)), pl.BlockSpec((C, D), lambda i: (i, GB_R)),
                  pl.BlockSpec((C, D), lambda i: (_fox_pos(i), 0)), pl.BlockSpec((C, D), lambda i: (i, GB_F)),
                  pl.BlockSpec((DMIX, D), lambda i: (0, 0)),
                  pl.BlockSpec((C, D), tok), pl.BlockSpec((C, D), tok), pl.BlockSpec((1, D), lambda i: (0, 0))],
        out_specs=[pl.BlockSpec((DMIX, C), lambda i: (0, i)), pl.BlockSpec((C, D), lambda i: (i, 0)),
                   pl.BlockSpec((C, D), lambda i: (i, 0)), pl.BlockSpec((8, C), lambda i: (0, 0)),
                   pl.BlockSpec((1, D), lambda i: (0, 0))],
        out_shape=[jax.ShapeDtypeStruct((DMIX, T), BF), jax.ShapeDtypeStruct((T, D), F32),
                   jax.ShapeDtypeStruct((T, D), BF), jax.ShapeDtypeStruct((8, C), F32),
                   jax.ShapeDtypeStruct((1, D), F32)],
        compiler_params=_params(("arbitrary",)),
    )(r, z, a, z, wout, x, tgt, fgain)


def _dsilu(x):
    s = jax.nn.sigmoid(x)
    return s * (1.0 + x * (1.0 - s))


def _dy_gate_bwd(dob, wout, r, z, a, seg):
    def body(do_ref, w_ref, r_ref, rg_ref, a_ref, fg_ref, seg_ref, dr_ref, da_ref, drg_ref, dfg_ref, dl_ref):
        dy = _dg(do_ref[...], w_ref[...], NT)
        rg, fg, a_ = rg_ref[...], fg_ref[...], a_ref[...]
        _, rn, rs = _gated(r_ref[...], rg, a_, fg)
        dyr, dyf = dy[:, :D], dy[:, D:]
        drn = dyr * (rg * jax.nn.sigmoid(rg))
        drg_ref[...] = (dyr * rn * _dsilu(rg)).astype(BF)
        for h in range(RH):
            sl = slice(RDV * h, RDV * (h + 1))
            dh, nh = drn[:, sl], rn[:, sl]
            dr_ref[:, sl] = (rs[h] * (dh - nh * jnp.mean(dh * nh, axis=1, keepdims=True))).astype(BF)
        dab = (dyf * (fg * jax.nn.sigmoid(fg))).astype(BF)
        da_ref[...] = dab
        dfg_ref[...] = (dyf * a_ * _dsilu(fg)).astype(BF)
        prod = dab.astype(F32) * a_
        segm = seg_ref[...]
        for p in range(NPAIR):
            sl = slice(C * p, C * (p + 1))
            hi = prod[:, sl].astype(BF)
            lo = (prod[:, sl] - hi.astype(F32)).astype(BF)
            dl_ref[:, sl] = _dot(hi, segm) + _dot(lo, segm)

    row = pl.BlockSpec((C, D), lambda i: (i, 0))
    fox = pl.BlockSpec((C, D), lambda i: (_fox_pos(i), 0))
    return pl.pallas_call(
        body, name="dy_gate_bwd", grid=(NCH,),
        in_specs=[row, pl.BlockSpec((DMIX, D), lambda i: (0, 0)),
                  row, pl.BlockSpec((C, D), lambda i: (i, GB_R)),
                  fox, pl.BlockSpec((C, D), lambda i: (i, GB_F)),
                  pl.BlockSpec((C, C), lambda i: (0, 0))],
        out_specs=[row, fox, row, row, fox],
        out_shape=[jax.ShapeDtypeStruct((T, D), BF), jax.ShapeDtypeStruct((TROWS, D), BF),
                   jax.ShapeDtypeStruct((T, D), BF), jax.ShapeDtypeStruct((T, D), BF),
                   jax.ShapeDtypeStruct((TROWS, D), F32)],
        compiler_params=_params(("parallel",)),
    )(dob, wout, r, z, a, z, seg)


DZ_WIDTHS = (512, 512, 1024, 1024, 1024, 1024, 1024, 1024)


def _du_norm_bwd(dzs, dzf, wt, wft, hpad, g, dopad, parts=()):
    tm, tk = 544, 1024
    nk = WMAIN // tk
    ni = T // tm
    n = len(parts)

    def body(rq_ref, rk_ref, rv_ref, rg_ref, fq_ref, fk_ref, fv_ref, fg_ref, dzf_ref, w_ref, wf_ref, h_ref, g_ref,
             do_ref, *rest):
        part_refs, (gh_ref, dg_ref), land_refs = rest[:n], rest[n:n + 2], rest[n + 2:2 * n + 2]
        acc = rest[2 * n + 2]
        i, k = pl.program_id(0), pl.program_id(1)

        if n:
            send_sems, recv_sems = rest[2 * n + 3:]
            copies = _chip_copies(part_refs, land_refs, send_sems, recv_sems, by_dest=True)

            @pl.when((i == 0) & (k == 0))
            def _():
                for cp in copies:
                    cp.start()

            @pl.when((i == ni - 1) & (k == nk - 1))
            def _():
                for cp in copies:
                    cp.wait()

        @pl.when(k == 0)
        def _():
            acc[...] = (_dot(dzf_ref[...], wf_ref[...]) + _dot(rq_ref[...], w_ref[:512, :])
                        + _dot(rk_ref[...], w_ref[512:, :]))

        for kk, piece in enumerate((rv_ref, rg_ref, fq_ref, fk_ref, fv_ref, fg_ref), start=1):
            @pl.when(k == kk)
            def _(piece=piece):
                acc[...] += _dot(piece[...], w_ref[...])

        @pl.when(k == nk - 1)
        def _():
            du = acc[...]
            h = h_ref[...]
            gg = g_ref[...]
            rs = lax.rsqrt(jnp.mean(h * h, axis=1, keepdims=True) + EPS)
            hn = h * rs
            part = jnp.sum(du * hn, axis=0, keepdims=True)

            @pl.when(i == 0)
            def _():
                dg_ref[...] = part

            @pl.when(i > 0)
            def _():
                dg_ref[...] += part

            dhn = du * gg
            gh_ref[...] = rs * (dhn - hn * jnp.mean(dhn * hn, axis=1, keepdims=True)) + do_ref[...]

    sems = [pltpu.SemaphoreType.DMA((3 * n,)), pltpu.SemaphoreType.DMA((3 * n,))] if n else []
    return pl.pallas_call(
        body, name="du_norm_bwd", grid=(ni, nk),
        in_specs=[pl.BlockSpec((tm, w), lambda i, k: (i, 0)) for w in DZ_WIDTHS]
        + [pl.BlockSpec((tm, C), lambda i, k: (i, 0)),
           pl.BlockSpec((tk, D), lambda i, k: (k, 0)), pl.BlockSpec((C, D), lambda i, k: (0, 0)),
           pl.BlockSpec((tm, D), lambda i, k: (i, 0)), pl.BlockSpec((1, D), lambda i, k: (0, 0)),
           pl.BlockSpec((tm, D), lambda i, k: (i, 0))] + [ANY] * n,
        out_specs=[pl.BlockSpec((tm, D), lambda i, k: (i, 0)), pl.BlockSpec((1, D), lambda i, k: (0, 0))] + [ANY] * n,
        out_shape=[jax.ShapeDtypeStruct((T, D), F32), jax.ShapeDtypeStruct((1, D), F32)]
        + [jax.ShapeDtypeStruct(p.shape, p.dtype) for p in parts],
        scratch_shapes=[pltpu.VMEM((tm, D), F32)] + sems,
        compiler_params=_params(("arbitrary", "arbitrary")),
    )(*dzs, dzf, wt, wft, hpad, g, dopad, *parts)


GROWS = 7680


def _dw_in(dzs, dzf, ut):
    tn = 512
    nmain = WMAIN // tn
    first, blocks = [], []
    for w in DZ_WIDTHS:
        first.append(sum(blocks))
        blocks.append(w // tn)

    def body(rq_ref, rk_ref, rv_ref, rg_ref, fq_ref, fk_ref, fv_ref, fg_ref, dzf_ref, ut_ref, o_ref):
        gidx = pl.program_id(0)
        for piece, g0, nb in zip((rq_ref, rk_ref, rv_ref, rg_ref, fq_ref, fk_ref, fv_ref, fg_ref), first, blocks):
            @pl.when((gidx >= g0) & (gidx < g0 + nb))
            def _(piece=piece):
                o_ref[...] = _dot(ut_ref[...], piece[...]).T.astype(BF)

        @pl.when(gidx == nmain)
        def _():
            o_ref[:C, :] = _dot(ut_ref[...], dzf_ref[...]).T.astype(BF)
            o_ref[C:, :] = jnp.zeros((tn - C, D), BF)

    def piece_spec(g0, nb):
        return pl.BlockSpec((T, tn), lambda gidx: (0, jnp.clip(gidx - g0, 0, nb - 1)))

    return pl.pallas_call(
        body, name="dw_in", grid=(nmain + 1,),
        in_specs=[piece_spec(g0, nb) for g0, nb in zip(first, blocks)]
        + [pl.BlockSpec((T, C), lambda gidx: (0, 0)), pl.BlockSpec((D, T), lambda gidx: (0, 0))],
        out_specs=pl.BlockSpec((tn, D), lambda gidx: (gidx, 0)),
        out_shape=jax.ShapeDtypeStruct((GROWS, D), BF),
        compiler_params=pltpu.CompilerParams(dimension_semantics=("arbitrary",), vmem_limit_bytes=DW_VMEM_LIMIT),
    )(*dzs, dzf, ut)


def _token_order(x_po):
    def body(i_ref, o_ref):
        o_ref[...] = i_ref[...]

    return pl.pallas_call(
        body, name="token_order", grid=(NCH,),
        in_specs=[pl.BlockSpec((C, D), lambda i: (_fox_pos(i), 0))],
        out_specs=pl.BlockSpec((C, D), lambda i: (i, 0)),
        out_shape=jax.ShapeDtypeStruct((T, D), x_po.dtype),
        compiler_params=_params(("parallel",)),
    )(x_po)


def _local_step(x, tgt, meta, norm_g, wt, wft, b_f, wout, final_g, chip_sums=None, wout_full=None):
    cst = _constants()
    hpad = jnp.concatenate([jnp.pad(meta, ((PAD, 0), (0, 0))), x], axis=0)
    bf_pad = jnp.pad(b_f, ((0, 0), (0, C - NFF)))
    u, ut = _norm_in(hpad, norm_g)
    z = _mm_nt(u, wt, WMAIN, T // 2, 512, "in_proj")
    zf = _mm_nt(u, wft, C, T // 2, C, "in_proj_ff")
    r, sprev = _ret_fwd(z, cst)
    ct = _fox_prep(zf, bf_pad, cst)
    if wout_full is None:
        a, g = _fox_fwd(z, ct, cst, None)
    else:
        a, g, landed_wout = _fox_fwd(z, ct, cst, wout)
        wout = wout_full(landed_wout)
    yt, dopad, dob, loss8, dfg = _out_loss(r, z, a, wout, x, tgt, final_g)
    dr, da, dzrg, dzfg, delta = _dy_gate_bwd(dob, wout, r, z, a, cst["seg"])
    dwout = _mm_nn(yt, dob, 512, D, "dw_out", BF)
    dzq_r, dzk_r, dzv_r = _ret_bwd(z, cst, sprev, dr)
    dq_po, drow, dzk_f, dzv_f, dcol = _fox_bwd(z, da, g, delta, ct, cst)
    dzf, dbf = _fox_gate_bwd(drow, dcol, zf, bf_pad, cst)
    dzs = [dzq_r, dzk_r, dzv_r, dzrg, _token_order(dq_po), dzk_f, dzv_f, dzfg]
    gwt = _dw_in(dzs, dzf, ut)
    parts = chip_sums(gwt, dwout) if chip_sums else []
    gh, dng, *landed = _du_norm_bwd(dzs, dzf, wt, wft, hpad, norm_g, dopad, parts)
    return (loss8[0, 0], gh[C:], gh[PAD:C], dng, gwt, dbf[:, :NFF], dwout, dfg, parts, landed)


def _all_gather_shards(shards):
    n = len(shards)

    def body(*refs):
        ins, outs = refs[:n], refs[n:2 * n]
        send_sems, recv_sems = refs[2 * n:]
        x, y, c = _place()
        me_s = 2 * x + y
        sib = (x, y, 1 - c)
        chips = _other_chips(x, y)
        sends, waits = [], []
        for a in range(n):
            rows = ins[a].shape[0] // 2
            half = pl.ds(c * rows, rows)
            for k, (cx, cy, cs) in enumerate(chips):
                sends.append(pltpu.make_async_remote_copy(
                    src_ref=ins[a].at[half], dst_ref=outs[a].at[me_s, half],
                    send_sem=send_sems.at[6 * a + k], recv_sem=recv_sems.at[6 * a + k],
                    device_id=(cx, cy, c), device_id_type=MESH))
                sends[-1].start()
        for a in range(n):
            rows = ins[a].shape[0] // 2
            half = pl.ds(c * rows, rows)
            other = pl.ds((1 - c) * rows, rows)
            for k, (cx, cy, cs) in enumerate(chips):
                pltpu.make_async_remote_copy(
                    src_ref=outs[a].at[cs, half], dst_ref=outs[a].at[cs, half],
                    send_sem=send_sems.at[6 * a + k], recv_sem=recv_sems.at[6 * a + k],
                    device_id=(cx, cy, c), device_id_type=MESH).wait_recv()
                fwd = pltpu.make_async_remote_copy(
                    src_ref=outs[a].at[cs, half], dst_ref=outs[a].at[cs, half],
                    send_sem=send_sems.at[6 * a + 3 + k], recv_sem=recv_sems.at[6 * a + 3 + k],
                    device_id=sib, device_id_type=MESH)
                fwd.start()
                sends.append(fwd)
                waits.append(pltpu.make_async_remote_copy(
                    src_ref=outs[a].at[cs, other], dst_ref=outs[a].at[cs, other],
                    send_sem=send_sems.at[6 * a + 3 + k], recv_sem=recv_sems.at[6 * a + 3 + k],
                    device_id=sib, device_id_type=MESH))
        for w in waits:
            w.wait_recv()
        for s in sends:
            s.wait_send()

    return pl.pallas_call(
        body, name="all_gather_w",
        in_specs=[ANY] * n, out_specs=[ANY] * n,
        out_shape=[jax.ShapeDtypeStruct((4,) + s.shape, s.dtype) for s in shards],
        scratch_shapes=[pltpu.SemaphoreType.DMA((6 * n,)), pltpu.SemaphoreType.DMA((6 * n,))],
    )(*shards)


WOFF, WLEN = 1792, 2048
WHALF = WLEN // 2


def _pair_swap(gwt, arrs):
    n = len(arrs)

    def body(*refs):
        gw, ins = refs[0], refs[1:n + 1]
        gwo, outs = refs[n + 1], refs[n + 2:2 * n + 2]
        send_sems, recv_sems = refs[2 * n + 2:]
        x, y, c = _place()
        sib = (x, y, 1 - c)
        cps = []
        for k in range(4):
            cps.append(pltpu.make_async_remote_copy(
                src_ref=gw.at[pl.ds(WOFF * k + (1 - c) * WHALF, WHALF)], dst_ref=gwo.at[k],
                send_sem=send_sems.at[k], recv_sem=recv_sems.at[k], device_id=sib, device_id_type=MESH))
        for a in range(n):
            rows = ins[a].shape[1] // 2
            cps.append(pltpu.make_async_remote_copy(
                src_ref=ins[a].at[:, pl.ds((1 - c) * rows, rows)], dst_ref=outs[a],
                send_sem=send_sems.at[4 + a], recv_sem=recv_sems.at[4 + a], device_id=sib, device_id_type=MESH))
        for cp in cps:
            cp.start()
        for cp in cps:
            cp.wait()

    return pl.pallas_call(
        body, name="rs_pair_swap",
        in_specs=[ANY] * (n + 1), out_specs=[ANY] * (n + 1),
        out_shape=[jax.ShapeDtypeStruct((4, WHALF, D), gwt.dtype)]
        + [jax.ShapeDtypeStruct((4, a.shape[1] // 2, a.shape[2]), a.dtype) for a in arrs],
        scratch_shapes=[pltpu.SemaphoreType.DMA((n + 4,)), pltpu.SemaphoreType.DMA((n + 4,))],
    )(gwt, *arrs)


def _add_windows(gwt, recv):
    tb = 256
    nb = WHALF // tb
    c = lax.axis_index("c")

    def body(c_ref, a_ref, b_ref, o_ref):
        o_ref[0] = (a_ref[...].astype(F32) + b_ref[0].astype(F32)).astype(BF)

    return pl.pallas_call(
        body, name="pair_add_in",
        grid_spec=pltpu.PrefetchScalarGridSpec(
            num_scalar_prefetch=1, grid=(4, nb),
            in_specs=[pl.BlockSpec((tb, D), lambda k, i, cr: ((WOFF // tb) * k + nb * cr[0] + i, 0)),
                      pl.BlockSpec((1, tb, D), lambda k, i, cr: (k, i, 0))],
            out_specs=pl.BlockSpec((1, tb, D), lambda k, i, cr: (k, i, 0))),
        out_shape=jax.ShapeDtypeStruct(recv.shape, BF),
        compiler_params=_params(("parallel", "parallel")),
    )(jnp.reshape(c, (1,)).astype(jnp.int32), gwt, recv)


def _chip_exchange(parts, small):
    n = len(parts)

    def body(*refs):
        ins, sm = refs[:n], refs[n]
        outs, smo = refs[n + 1:2 * n + 1], refs[2 * n + 1]
        send_sems, recv_sems = refs[2 * n + 2:]
        cps = _chip_copies(ins, outs, send_sems, recv_sems, by_dest=True)
        cps += _chip_copies([sm], [smo], send_sems.at[pl.ds(3 * n, 3)], recv_sems.at[pl.ds(3 * n, 3)], by_dest=False)
        for cp in cps:
            cp.start()
        for cp in cps:
            cp.wait()

    return pl.pallas_call(
        body, name="rs_chip_exchange",
        in_specs=[ANY] * (n + 1), out_specs=[ANY] * (n + 1),
        out_shape=[jax.ShapeDtypeStruct(p.shape, p.dtype) for p in parts]
        + [jax.ShapeDtypeStruct((4,) + small.shape, small.dtype)],
        scratch_shapes=[pltpu.SemaphoreType.DMA((3 * (n + 1),)), pltpu.SemaphoreType.DMA((3 * (n + 1),))],
    )(*parts, small)


def _pair_send(halves):
    n = len(halves)

    def body(*refs):
        ins, outs = refs[:n], refs[n:2 * n]
        send_sems, recv_sems = refs[2 * n:]
        x, y, c = _place()
        cps = [pltpu.make_async_remote_copy(
            src_ref=ins[a], dst_ref=outs[a], send_sem=send_sems.at[a], recv_sem=recv_sems.at[a],
            device_id=(x, y, 1 - c), device_id_type=MESH) for a in range(n)]
        for cp in cps:
            cp.start()
        for cp in cps:
            cp.wait()

    return pl.pallas_call(
        body, name="rs_pair_send",
        in_specs=[ANY] * n, out_specs=[ANY] * n,
        out_shape=[jax.ShapeDtypeStruct(h.shape, h.dtype) for h in halves],
        scratch_shapes=[pltpu.SemaphoreType.DMA((n,)), pltpu.SemaphoreType.DMA((n,))],
    )(*halves)


def _row_block(rows):
    for tb in (256, 128, 64, 32, 16, 8):
        if rows % tb == 0:
            return tb
    return rows


def _add_halves(full, recv, name, out_dtype):
    _, r2, w = recv.shape
    tb = _row_block(r2)
    nb = r2 // tb
    c = lax.axis_index("c")

    def body(c_ref, a_ref, b_ref, o_ref):
        o_ref[...] = (a_ref[...].astype(F32) + b_ref[...].astype(F32)).astype(o_ref.dtype)

    return pl.pallas_call(
        body, name=name,
        grid_spec=pltpu.PrefetchScalarGridSpec(
            num_scalar_prefetch=1, grid=(4, nb),
            in_specs=[pl.BlockSpec((1, tb, w), lambda s, i, cr: (s, cr[0] * nb + i, 0)),
                      pl.BlockSpec((1, tb, w), lambda s, i, cr: (s, i, 0))],
            out_specs=pl.BlockSpec((1, tb, w), lambda s, i, cr: (s, i, 0))),
        out_shape=jax.ShapeDtypeStruct(recv.shape, out_dtype),
        compiler_params=_params(("parallel", "parallel")),
    )(jnp.reshape(c, (1,)).astype(jnp.int32), full, recv)


def _add2(a, b, name):
    def body(a_ref, b_ref, o_ref):
        o_ref[...] = a_ref[...] + b_ref[...]

    return pl.pallas_call(body, name=name, out_shape=jax.ShapeDtypeStruct(a.shape, a.dtype))(a, b)


def _sum4(buf, own, name):
    _, r, w = buf.shape
    tb = _row_block(r)
    me_s = 2 * lax.axis_index("x") + lax.axis_index("y")
    by_dest = own.ndim == 3

    def body(s_ref, b_ref, own_ref, o_ref):
        mine = (own_ref[0] if by_dest else own_ref[...]).astype(F32)
        terms = [jnp.where(s_ref[0] == t, mine, b_ref[t].astype(F32)) for t in range(4)]
        o_ref[...] = ((terms[0] + terms[1]) + terms[2]) + terms[3]

    own_spec = (pl.BlockSpec((1, tb, w), lambda i, sr: (sr[0], i, 0)) if by_dest
                else pl.BlockSpec((tb, w), lambda i, sr: (i, 0)))
    return pl.pallas_call(
        body, name=name,
        grid_spec=pltpu.PrefetchScalarGridSpec(
            num_scalar_prefetch=1, grid=(r // tb,),
            in_specs=[pl.BlockSpec((4, tb, w), lambda i, sr: (0, i, 0)), own_spec],
            out_specs=pl.BlockSpec((tb, w), lambda i, sr: (i, 0))),
        out_shape=jax.ShapeDtypeStruct((r, w), F32),
        compiler_params=_params(("parallel",)),
    )(jnp.reshape(me_s, (1,)).astype(jnp.int32), buf, own)


def _adamw_math(w, g, m, v):
    mn = B1 * m + (1.0 - B1) * g
    vn = B2 * v + (1.0 - B2) * (g * g)
    m_hat = mn / (1.0 - B1 ** STEP)
    v_hat = vn / (1.0 - B2 ** STEP)
    return -LR * (m_hat / (jnp.sqrt(v_hat) + AEPS) + WD * w), mn, vn


def _adamw(w, g, m, v, name):
    r, c_ = w.shape
    tb = _row_block(r)
    if tb == r and r > 512:
        tb = 256

    def body(w_ref, g_ref, m_ref, v_ref, d_ref, mo_ref, vo_ref):
        d_ref[...], mo_ref[...], vo_ref[...] = _adamw_math(w_ref[...], g_ref[...], m_ref[...], v_ref[...])

    spec = pl.BlockSpec((tb, c_), lambda i: (i, 0))
    return pl.pallas_call(
        body, name=name, grid=(pl.cdiv(r, tb),),
        in_specs=[spec] * 4, out_specs=[spec] * 3,
        out_shape=[jax.ShapeDtypeStruct(w.shape, F32)] * 3,
        compiler_params=_params(("parallel",)),
    )(w, g, m, v)


def _adamw_halves(w, g_mine, g_sib, m, v, name):
    r, c_ = w.shape
    r2 = g_mine.shape[0]
    tb = _row_block(r2)
    nb = r2 // tb
    c = lax.axis_index("c")

    def body(c_ref, w_ref, gm_ref, gs_ref, m_ref, v_ref, g_ref, d_ref, mo_ref, vo_ref):
        g = jnp.where(pl.program_id(0) == c_ref[0], gm_ref[...], gs_ref[...])
        g_ref[...] = g
        d_ref[...], mo_ref[...], vo_ref[...] = _adamw_math(w_ref[...], g, m_ref[...], v_ref[...])

    full = pl.BlockSpec((tb, c_), lambda h, i, cr: (h * nb + i, 0))
    half = pl.BlockSpec((tb, c_), lambda h, i, cr: (i, 0))
    return pl.pallas_call(
        body, name=name,
        grid_spec=pltpu.PrefetchScalarGridSpec(
            num_scalar_prefetch=1, grid=(2, nb),
            in_specs=[full, half, half, full, full], out_specs=[full] * 4),
        out_shape=[jax.ShapeDtypeStruct(w.shape, F32)] * 4,
        compiler_params=_params(("parallel", "parallel")),
    )(jnp.reshape(c, (1,)).astype(jnp.int32), w, g_mine, g_sib, m, v)


def kernel(x, meta_tokens, norm_g, w_in, b_f, w_out, final_g, loss_target, m_meta_tokens, m_norm_g, m_w_in, m_b_f, m_w_out, m_final_g, v_meta_tokens, v_norm_g, v_w_in, v_b_f, v_w_out, v_final_g):
    me_s = 2 * lax.axis_index("x") + lax.axis_index("y")
    core = lax.axis_index("c")
    wt, mt, vt = [jnp.swapaxes(t[0], 0, 1) for t in (w_in, m_w_in, v_w_in)]

    own_win = lax.dynamic_update_slice(jnp.zeros((WPADROWS, D), F32), wt, (4 * me_s, 0)).astype(BF)
    own = [own_win, meta_tokens]
    gathered = _all_gather_shards(own)
    mine = (jnp.arange(4) == me_s)[:, None, None]
    win, gmeta = [jnp.where(mine, o[None], g) for o, g in zip(own, gathered)]
    wout_own = w_out[0].astype(BF)
    lap = WPADROWS - WOFF
    tails = jnp.concatenate([jnp.zeros((1, lap, D), BF), win[:-1, WOFF:]], axis=0)
    wt_main = jnp.concatenate([win[:, :lap] + tails, win[:, lap:WOFF]], axis=1).reshape(WMAIN, D)
    wft = jnp.pad(win[3, WOFF:WOFF + NFF], ((0, C - NFF), (0, 0)))
    meta = jnp.concatenate([gmeta[s] for s in range(4)], axis=1)

    def wout_full(landed):
        return jnp.where(mine, wout_own[None], landed).reshape(DMIX, D)

    def chip_sums(gwt, dwout):
        g_out = dwout.reshape(4, DMIX // 4, D)
        r_in, r_out = _pair_swap(gwt, [g_out])
        return [_add_windows(gwt, r_in), _add_halves(g_out, r_out, "pair_add_out", BF)]

    loss, gx, dmeta, dng, gwt, dbf, dwout, dfg, (p_in, p_out), (e_in, e_out) = _local_step(
        x[0], loss_target[0], meta, norm_g, wt_main, wft, b_f, wout_own, final_g.reshape(1, D), chip_sums, wout_full)

    g_meta = jnp.stack([dmeta[:, 256 * s:256 * (s + 1)] for s in range(4)])
    small = jnp.concatenate([dng, dfg, jnp.pad(dbf, ((0, 0), (0, D - NFF))),
                             jnp.pad(jnp.reshape(loss, (1, 1)), ((0, 0), (0, D - 1))),
                             jnp.zeros((4, D), F32)], axis=0)
    e_meta, e_small = _chip_exchange([g_meta], small)
    h_in, h_out = _sum4(e_in, p_in, "sum_in"), _sum4(e_out, p_out, "sum_out")
    h_meta, h_small = _sum4(e_meta, g_meta, "sum_meta"), _sum4(e_small, small, "sum_small")
    s_in, s_out, s_meta, s_small = _pair_send([h_in, h_out, h_meta, h_small])
    gw_meta = _add2(h_meta, s_meta, "pair_add_meta")
    tot = _add2(h_small, s_small, "pair_add_small")
    g_norm, g_final, g_bf, loss_all = tot[0:1], tot[1], tot[2:3, :NFF], tot[3, 0]

    d_meta, nm_meta, nv_meta = _adamw(meta_tokens, gw_meta, m_meta_tokens, v_meta_tokens, "adamw_meta")
    d_norm, nm_norm, nv_norm = _adamw(norm_g, g_norm, m_norm_g, v_norm_g, "adamw_norm")
    window = jnp.concatenate([jnp.where(core == 0, h_in, s_in), jnp.where(core == 0, s_in, h_in)], axis=0)
    gwt_own = lax.dynamic_slice(window, (4 * me_s, 0), (WSH, D))
    d_in, nm_in, nv_in = _adamw(wt, gwt_own, mt, vt, "adamw_in")
    gw_in, d_in, nm_in, nv_in = [jnp.swapaxes(t, 0, 1)[None] for t in (gwt_own, d_in, nm_in, nv_in)]
    d_bf, nm_bf, nv_bf = _adamw(b_f, g_bf, m_b_f, v_b_f, "adamw_bf")
    gw_out, d_out, nm_out, nv_out = _adamw_halves(w_out[0], h_out, s_out, m_w_out[0], v_w_out[0], "adamw_out")
    d_fin, nm_fin, nv_fin = _adamw(final_g.reshape(1, D), g_final.reshape(1, D), m_final_g.reshape(1, D),
                                   v_final_g.reshape(1, D), "adamw_final")
    return (loss_all, gx[None], gw_meta, g_norm, gw_in, g_bf, gw_out[None], g_final,
            d_meta, d_norm, d_in, d_bf, d_out[None], d_fin.reshape(D),
            nm_meta, nm_norm, nm_in, nm_bf, nm_out[None], nm_fin.reshape(D),
            nv_meta, nv_norm, nv_in, nv_bf, nv_out[None], nv_fin.reshape(D))
```

```python
import numpy as np
import jax
import jax.numpy as jnp
from jax import lax
from jax.experimental import pallas as pl
from jax.experimental.pallas import tpu as pltpu

D = 1024
SEQ = 2048
NMETA = 16
C = 128
PAD = C - NMETA
T = PAD + NMETA + SEQ
NCH = T // C
RH, RDK, RDV = 4, 128, 256
FH, FD = 16, 64
NPAIR = FH // 2
WMAIN = 7168
NFF = 16
WIN = WMAIN + NFF
WSH = WIN // 4
WPADROWS = 1824
DMIX = 2048
EPS = 1e-6
NEG = -1e30
RSCALE = RDK ** -0.5
FSCALE = FD ** -0.5
ROPE_BASE = 10000.0
LR, B1, B2, AEPS, WD, STEP = 0.001, 0.9, 0.999, 1e-08, 0.01, 10

BF = jnp.bfloat16
F32 = jnp.float32
NT = (((1,), (1,)), ((), ()))
TN = (((0,), (0,)), ((), ()))
NN_DIMS = (((1,), (0,)), ((), ()))
MESH = pl.DeviceIdType.MESH
ANY = pl.BlockSpec(memory_space=pl.ANY)
VMEM_LIMIT = 48 * 1024 * 1024
DW_VMEM_LIMIT = 56 * 1024 * 1024

QB_R, KB_R = 0, 4
VB_R = 4
GB_R, GB_F = 2, 6
QB_F, KB_F, VB_F = 24, 32, 40


def _dot(a, b):
    return jnp.dot(a, b, preferred_element_type=F32)


def _dg(a, b, dims):
    return lax.dot_general(a, b, dims, preferred_element_type=F32)


def _params(sem=None):
    return pltpu.CompilerParams(dimension_semantics=sem, vmem_limit_bytes=VMEM_LIMIT)


def _constants():
    pos = jnp.arange(T, dtype=F32) - PAD
    inv = ROPE_BASE ** (-jnp.arange(0, RDK, 2, dtype=F32) / RDK)
    ang = pos[:, None] * inv[None, :]
    cos, sin = jnp.cos(ang), jnp.sin(ang)
    cos2 = jnp.concatenate([cos, cos], axis=1)
    sin2 = jnp.concatenate([-sin, sin], axis=1)
    log_gamma = jnp.log1p(-jnp.exp2(-5.0 - jnp.arange(RH, dtype=F32)))
    idx = jnp.arange(C, dtype=F32)
    diff = idx[:, None] - idx[None, :]
    dmask = jnp.where(diff[None] >= 0, jnp.exp(log_gamma[:, None, None] * jnp.maximum(diff, 0.0)[None]), 0.0)
    zeta = jnp.exp(log_gamma[:, None] * (C - 1.0 - idx)[None, :])
    xi = jnp.exp(log_gamma[:, None] * (idx + 1.0)[None, :])
    gdec = jnp.exp(log_gamma * C)
    zeta_b = jnp.broadcast_to(zeta[:, :, None], (RH, C, RDK))
    xi_b = jnp.broadcast_to(xi[:, :, None], (RH, C, RDK))
    gdec_b = jnp.broadcast_to(gdec[:, None, None], (RH, RDK, RDV))
    tri = jnp.asarray(np.tril(np.ones((C, C), np.float32)), dtype=BF)
    head_of_lane = np.arange(FH * FD) // FD
    pick = ((np.arange(FH * FD)[:, None] % FD == 0)
            & (head_of_lane[:, None] == np.arange(C)[None, :])).astype(np.float32)
    seg = (np.arange(C)[:, None] // FD == np.arange(C)[None, :] // FD).astype(np.float32)
    ones_aug = np.concatenate([np.tile((np.arange(C) < FD)[None, :], (C, 1)),
                               np.tile((np.arange(C) >= FD)[None, :], (C, 1))], axis=0).astype(np.float32)
    lane = np.arange(2 * C) % C
    causal = np.where(lane[None, :] <= np.arange(C)[:, None], 0.0, NEG).astype(np.float32)
    mask_bias = np.stack([np.zeros((C, 2 * C), np.float32), causal])
    return dict(cos2=cos2, sin2=sin2, dmask=dmask, zeta=zeta_b, xi=xi_b, gdec=gdec_b, tri=tri,
                mask_bias=jnp.asarray(mask_bias), pick=jnp.asarray(pick, dtype=BF), seg=jnp.asarray(seg, dtype=BF),
                ones_aug=jnp.asarray(ones_aug, dtype=BF))


def _norm_in(hpad, g):
    def body(h_ref, g_ref, u_ref, ut_ref):
        h = h_ref[...]
        rs = lax.rsqrt(jnp.mean(h * h, axis=1, keepdims=True) + EPS)
        u = h * rs * g_ref[...]
        u_ref[...] = u.astype(BF)
        ut_ref[...] = u.T.astype(BF)

    return pl.pallas_call(
        body, name="norm_in", grid=(NCH,),
        in_specs=[pl.BlockSpec((C, D), lambda i: (i, 0)), pl.BlockSpec((1, D), lambda i: (0, 0))],
        out_specs=[pl.BlockSpec((C, D), lambda i: (i, 0)), pl.BlockSpec((D, C), lambda i: (0, i))],
        out_shape=[jax.ShapeDtypeStruct((T, D), BF), jax.ShapeDtypeStruct((D, T), BF)],
        compiler_params=_params(("parallel",)),
    )(hpad, g)


def _mm_nt(a, b, n, tm, tn, name, out_dtype=F32):
    m, k = a.shape

    def body(a_ref, b_ref, o_ref):
        o_ref[...] = _dg(a_ref[...], b_ref[...], NT).astype(out_dtype)

    return pl.pallas_call(
        body, name=name, grid=(m // tm, n // tn),
        in_specs=[pl.BlockSpec((tm, k), lambda i, j: (i, 0)), pl.BlockSpec((tn, k), lambda i, j: (j, 0))],
        out_specs=pl.BlockSpec((tm, tn), lambda i, j: (i, j)),
        out_shape=jax.ShapeDtypeStruct((m, n), out_dtype),
        compiler_params=_params(("parallel", "parallel")),
    )(a, b)


def _mm_nn(a, b, tm, tn, name, out_dtype=F32):
    m, k = a.shape
    _, n = b.shape

    def body(a_ref, b_ref, o_ref):
        o_ref[...] = _dot(a_ref[...], b_ref[...]).astype(out_dtype)

    return pl.pallas_call(
        body, name=name, grid=(m // tm, n // tn),
        in_specs=[pl.BlockSpec((tm, k), lambda i, j: (i, 0)), pl.BlockSpec((k, tn), lambda i, j: (0, j))],
        out_specs=pl.BlockSpec((tm, tn), lambda i, j: (i, j)),
        out_shape=jax.ShapeDtypeStruct((m, n), out_dtype),
        compiler_params=_params(("parallel", "parallel")),
    )(a, b)


def _rot(x, cos2, sin2):
    return x * cos2 + pltpu.roll(x, 64, 1) * sin2


def _ret_specs(chunk):
    whole = lambda shape: pl.BlockSpec(shape, lambda n: (0,) * len(shape))
    return [
        pl.BlockSpec((C, RH * RDK), lambda n: (chunk(n), 0)),
        pl.BlockSpec((C, RH * RDK), lambda n: (chunk(n), 1)),
        pl.BlockSpec((C, RH * RDV), lambda n: (chunk(n), 1)),
        pl.BlockSpec((C, RDK), lambda n: (chunk(n), 0)),
        pl.BlockSpec((C, RDK), lambda n: (chunk(n), 0)),
        whole((RH, C, C)), whole((RH, C, RDK)), whole((RH, C, RDK)), whole((RH, RDK, RDV)),
    ]


def _ret_heads(q_ref, k_ref, v_ref, cos, sin):
    qr = [_rot(q_ref[:, RDK * h:RDK * (h + 1)].astype(F32), cos, sin) for h in range(RH)]
    kr = [_rot(k_ref[:, RDK * h:RDK * (h + 1)].astype(F32), cos, sin) * RSCALE for h in range(RH)]
    vb = [v_ref[:, RDV * h:RDV * (h + 1)].astype(BF) for h in range(RH)]
    return qr, kr, [t.astype(BF) for t in qr], [t.astype(BF) for t in kr], vb


def _ret_fwd(z, cst):
    def body(q_ref, k_ref, v_ref, cos_ref, sin_ref, dm_ref, xi_ref, zt_ref, gd_ref, r_ref, sp_ref, st):
        n = pl.program_id(0)

        @pl.when(n == 0)
        def _():
            st[...] = jnp.zeros_like(st)

        hs = range(RH)
        qr, kr, qb, kb, vb = _ret_heads(q_ref, k_ref, v_ref, cos_ref[...], sin_ref[...])
        sd = [(_dg(qb[h], kb[h], NT) * dm_ref[h]).astype(BF) for h in hs]
        state = [st[h] for h in hs]
        qx = [(qr[h] * xi_ref[h]).astype(BF) for h in hs]
        kz = [(kr[h] * zt_ref[h]).astype(BF) for h in hs]
        out = [_dot(sd[h], vb[h]) + _dot(qx[h], state[h].astype(BF)) for h in hs]
        kv = [_dg(kz[h], vb[h], TN) for h in hs]
        for h in hs:
            sp_ref[0, h] = state[h]
            r_ref[:, RDV * h:RDV * (h + 1)] = out[h]
            st[h] = state[h] * gd_ref[h] + kv[h]

    return pl.pallas_call(
        body, name="ret_fwd", grid=(NCH,),
        in_specs=_ret_specs(lambda n: n),
        out_specs=[pl.BlockSpec((C, RH * RDV), lambda n: (n, 0)),
                   pl.BlockSpec((1, RH, RDK, RDV), lambda n: (n, 0, 0, 0))],
        out_shape=[jax.ShapeDtypeStruct((T, RH * RDV), F32), jax.ShapeDtypeStruct((NCH, RH, RDK, RDV), F32)],
        scratch_shapes=[pltpu.VMEM((RH, RDK, RDV), F32)],
        compiler_params=_params(("arbitrary",)),
    )(z, z, z, cst["cos2"], cst["sin2"], cst["dmask"], cst["xi"], cst["zeta"], cst["gdec"])


def _ret_bwd(z, cst, sprev, dr):
    def body(q_ref, k_ref, v_ref, cos_ref, sin_ref, dm_ref, xi_ref, zt_ref, gd_ref, sp_ref, dr_ref,
             dq_ref, dk_ref, dv_ref, gst):
        i = pl.program_id(0)

        @pl.when(i == 0)
        def _():
            gst[...] = jnp.zeros_like(gst)

        hs = range(RH)
        cos, sin = cos_ref[...], sin_ref[...]
        qr, kr, qb, kb, vb = _ret_heads(q_ref, k_ref, v_ref, cos, sin)
        dm = [dm_ref[h] for h in hs]
        xi = [xi_ref[h] for h in hs]
        zt = [zt_ref[h] for h in hs]
        sd = [(_dg(qb[h], kb[h], NT) * dm[h]).astype(BF) for h in hs]
        qx = [(qr[h] * xi[h]).astype(BF) for h in hs]
        kz = [(kr[h] * zt[h]).astype(BF) for h in hs]
        drb = [dr_ref[:, RDV * h:RDV * (h + 1)] for h in hs]
        sb = [sp_ref[0, h].astype(BF) for h in hs]
        g = [gst[h] for h in hs]
        gb = [t.astype(BF) for t in g]
        ds = [(_dg(drb[h], vb[h], NT) * dm[h]).astype(BF) for h in hs]
        dq = [_dot(ds[h], kb[h]) + _dg(drb[h], sb[h], NT) * xi[h] for h in hs]
        dk = [(_dg(ds[h], qb[h], TN) + _dg(vb[h], gb[h], NT) * zt[h]) * RSCALE for h in hs]
        dv = [_dg(sd[h], drb[h], TN) + _dot(kz[h], gb[h]) for h in hs]
        gn = [g[h] * gd_ref[h] + _dg(qx[h], drb[h], TN) for h in hs]
        for h in hs:
            gst[h] = gn[h]
            dq_ref[:, RDK * h:RDK * (h + 1)] = (dq[h] * cos + pltpu.roll(dq[h] * sin, 64, 1)).astype(BF)
            dk_ref[:, RDK * h:RDK * (h + 1)] = (dk[h] * cos + pltpu.roll(dk[h] * sin, 64, 1)).astype(BF)
            dv_ref[:, RDV * h:RDV * (h + 1)] = dv[h].astype(BF)

    rev = lambda n: NCH - 1 - n
    return pl.pallas_call(
        body, name="ret_bwd", grid=(NCH,),
        in_specs=_ret_specs(rev) + [
            pl.BlockSpec((1, RH, RDK, RDV), lambda n: (rev(n), 0, 0, 0)),
            pl.BlockSpec((C, RH * RDV), lambda n: (rev(n), 0)),
        ],
        out_specs=[pl.BlockSpec((C, RH * RDK), lambda n: (rev(n), 0)),
                   pl.BlockSpec((C, RH * RDK), lambda n: (rev(n), 0)),
                   pl.BlockSpec((C, RH * RDV), lambda n: (rev(n), 0))],
        out_shape=[jax.ShapeDtypeStruct((T, RH * RDK), BF), jax.ShapeDtypeStruct((T, RH * RDK), BF),
                   jax.ShapeDtypeStruct((T, RH * RDV), BF)],
        scratch_shapes=[pltpu.VMEM((RH, RDK, RDV), F32)],
        compiler_params=_params(("arbitrary",)),
    )(z, z, z, cst["cos2"], cst["sin2"], cst["dmask"], cst["xi"], cst["zeta"], cst["gdec"], sprev, dr)


def _place():
    x, y, c = lax.axis_index("x"), lax.axis_index("y"), lax.axis_index("c")
    return x, y, c


def _other_chips(x, y):
    return [(1 - x, y, 2 * (1 - x) + y), (x, 1 - y, 2 * x + (1 - y)), (1 - x, 1 - y, 2 * (1 - x) + (1 - y))]


def _chip_copies(srcs, lands, send_sems, recv_sems, by_dest):
    x, y, c = _place()
    me_s = 2 * x + y
    return [pltpu.make_async_remote_copy(
        src_ref=src.at[cs] if by_dest else src, dst_ref=land.at[me_s],
        send_sem=send_sems.at[3 * a + j], recv_sem=recv_sems.at[3 * a + j],
        device_id=(cx, cy, c), device_id_type=MESH)
        for a, (src, land) in enumerate(zip(srcs, lands)) for j, (cx, cy, cs) in enumerate(_other_chips(x, y))]


def _split_dot(x, mat01, dims=NN_DIMS, x_first=True):
    acc, rest = None, x
    for _ in range(3):
        piece = rest.astype(BF)
        part = _dg(piece, mat01, dims) if x_first else _dg(mat01, piece, dims)
        acc = part if acc is None else acc + part
        rest = rest - piece.astype(F32)
    return acc


def _log_sigmoid(x):
    return -(jnp.maximum(-x, 0.0) + jnp.log1p(jnp.exp(-jnp.abs(x))))


def _fox_prep(zf, bf_pad, cst):
    def body(zf_ref, b_ref, tri_ref, ct_ref, carry):
        n = pl.program_id(0)

        @pl.when(n == 0)
        def _():
            carry[...] = jnp.zeros_like(carry)

        ls = _log_sigmoid(zf_ref[...] + b_ref[...])
        row = n * C + lax.broadcasted_iota(jnp.int32, (C, C), 0)
        lf = jnp.where(row >= PAD, ls, 0.0)
        cc = _split_dot(lf, tri_ref[...], x_first=False) + carry[0:1, :]
        carry[...] = jnp.broadcast_to(cc[C - 1:C, :], carry.shape)
        pos = n * C + lax.broadcasted_iota(jnp.int32, (FH, C), 1)
        ct_ref[0] = jnp.where(pos >= PAD, cc.T[:FH, :], -NEG)

    return pl.pallas_call(
        body, name="fox_prep", grid=(NCH,),
        in_specs=[pl.BlockSpec((C, C), lambda n: (n, 0)), pl.BlockSpec((1, C), lambda n: (0, 0)),
                  pl.BlockSpec((C, C), lambda n: (0, 0))],
        out_specs=pl.BlockSpec((1, FH, C), lambda n: (n, 0, 0)),
        out_shape=jax.ShapeDtypeStruct((NCH, FH, C), F32),
        scratch_shapes=[pltpu.VMEM((8, C), F32)],
        compiler_params=_params(("arbitrary",)),
    )(zf, bf_pad, cst["tri"])


def _lo_lanes(shape):
    return lax.broadcasted_iota(jnp.int32, shape, 1) < FD


def _split_heads(x):
    lo = _lo_lanes(x.shape)
    zero = jnp.zeros_like(x)
    return jnp.concatenate([jnp.where(lo, x, zero), jnp.where(lo, zero, x)], axis=0)


def _spread2(x):
    lo = _lo_lanes(x.shape)
    r = pltpu.roll(x, FD, 1)
    return jnp.concatenate([jnp.where(lo, x, r), jnp.where(lo, r, x)], axis=1)


NSTEP = (NCH + 1) // 2
NTILE = NCH + 1
TROWS = T + C


def _fox_tile(s, t):
    second = t > s
    return second.astype(jnp.int32), jnp.where(second, t - s - 1, s - t)


def _fox_pos(i):
    return jnp.where(i < NSTEP, 2 * i, 2 * (NCH - 1 - i) + 1)


FOX_ORDER = [2 * i if i < NSTEP else 2 * (NCH - 1 - i) + 1 for i in range(NCH)]


def _fox_pair_specs():
    first = pl.BlockSpec((C, C), lambda p, s: (2 * s, p))
    second = pl.BlockSpec((C, C), lambda p, s: (jnp.where(s == NSTEP - 1, 2 * s, 2 * s + 1), p))
    both = pl.BlockSpec((2 * C, C), lambda p, s: (s, p))
    return first, second, both


def _fox_q_specs():
    return (pl.BlockSpec((C, C), lambda p, s: (s, QB_F + p)),
            pl.BlockSpec((C, C), lambda p, s: (NCH - 1 - s, QB_F + p)))


def _fox_key_bias(ct_ref, p, j):
    return jnp.concatenate([ct_ref[j, pl.ds(2 * p, 1), :], ct_ref[j, pl.ds(2 * p + 1, 1), :]], axis=1)


def _fox_fwd(z, ct, cst, share):
    n = 0 if share is None else 1

    def body(qa_ref, qb_ref, k_ref, v_ref, ct_ref, ones_ref, mb_ref, *rest):
        share_refs, (a_ref, g_ref), land_refs = rest[:n], rest[n:n + 2], rest[n + 2:2 * n + 2]
        kks, vvs, q2, m2, sbuf = rest[2 * n + 2:2 * n + 7]
        p, s = pl.program_id(0), pl.program_id(1)
        if n:
            copies = _chip_copies(share_refs, land_refs, *rest[2 * n + 7:], by_dest=False)

            @pl.when((p == 0) & (s == 0))
            def _():
                for cp in copies:
                    cp.start()

            @pl.when((p == NPAIR - 1) & (s == NSTEP - 1))
            def _():
                for cp in copies:
                    cp.wait()

        @pl.when(s == 0)
        def _():
            ones = ones_ref[...]

            def prep(j, carry):
                rows = pl.ds(pl.multiple_of(j * C, C), C)
                kks[j] = _split_heads(k_ref[rows, :].astype(F32)).astype(BF)
                vvs[j] = jnp.concatenate([_split_heads(v_ref[rows, :].astype(F32)).astype(BF), ones], axis=1)
                return carry

            lax.fori_loop(0, NCH, prep, 0)

        q2[0] = (qa_ref[...].astype(F32) * FSCALE).astype(BF)
        q2[1] = (qb_ref[...].astype(F32) * FSCALE).astype(BF)

        tiles = [_fox_tile(s, t) for t in range(NTILE)]
        causal = mb_ref[1]
        neg = jnp.full((C, 2 * C), NEG, F32)
        run, first = neg, neg
        for t, (sel, j) in enumerate(tiles):
            st = _dg(q2[sel], kks[j], NT) - _fox_key_bias(ct_ref, p, j)
            if t in (0, NTILE - 1):
                st = st + causal
            sbuf[t] = st
            run = jnp.maximum(jnp.where(t == s + 1, neg, run), st)
            first = jnp.where(t == s, run, first)
        for w, mx in enumerate((first, run)):
            m2[w] = jnp.concatenate(
                [jnp.broadcast_to(jnp.max(mx[:, :C], axis=1, keepdims=True), (C, C)),
                 jnp.broadcast_to(jnp.max(mx[:, C:], axis=1, keepdims=True), (C, C))], axis=1)

        zero = jnp.zeros((C, 2 * C), F32)
        run, first = zero, zero
        for t, (sel, j) in enumerate(tiles):
            run = jnp.where(t == s + 1, zero, run) + _dot(jnp.exp(sbuf[t] - m2[sel]).astype(BF), vvs[j])
            first = jnp.where(t == s, run, first)
        lo = _lo_lanes((C, C))
        for w, res in enumerate((first, run)):
            l = res[:, C:]
            a_ref[C * w:C * (w + 1), :] = res[:, :C] / l
            mw = m2[w]
            g_ref[C * w:C * (w + 1), :] = -(jnp.where(lo, mw[:, :C], mw[:, C:]) + jnp.log(l))

    qa, qb = _fox_q_specs()
    both = _fox_pair_specs()[2]
    return pl.pallas_call(
        body, name="fox_fwd", grid=(NPAIR, NSTEP),
        in_specs=[qa, qb,
                  pl.BlockSpec((T, C), lambda p, s: (0, KB_F + p)),
                  pl.BlockSpec((T, C), lambda p, s: (0, VB_F + p)),
                  pl.BlockSpec((NCH, FH, C), lambda p, s: (0, 0, 0)),
                  pl.BlockSpec((2 * C, C), lambda p, s: (0, 0)),
                  pl.BlockSpec((2, C, 2 * C), lambda p, s: (0, 0, 0))] + [ANY] * n,
        out_specs=[both, both] + [ANY] * n,
        out_shape=[jax.ShapeDtypeStruct((TROWS, FH * FD), F32)] * 2
        + ([jax.ShapeDtypeStruct((4,) + share.shape, share.dtype)] if n else []),
        scratch_shapes=[pltpu.VMEM((NCH, 2 * C, C), BF), pltpu.VMEM((NCH, 2 * C, 2 * C), BF),
                        pltpu.VMEM((2, C, C), BF), pltpu.VMEM((2, C, 2 * C), F32),
                        pltpu.VMEM((NTILE, C, 2 * C), F32)]
        + [pltpu.SemaphoreType.DMA((3,)), pltpu.SemaphoreType.DMA((3,))] * n,
        compiler_params=_params(("arbitrary", "arbitrary")),
    )(z, z, z, z, ct, cst["ones_aug"], cst["mask_bias"], *([share] * n))


def _fox_bwd(z, da, g, delta, ct, cst):
    grp = 9

    def body(qa_ref, qb_ref, daa_ref, dab_ref, ga_ref, gb_ref, dla_ref, dlb_ref, k_ref, v_ref, ct_ref, ones_ref,
             mb_ref, dq_ref, dr_ref, dk_ref, dv_ref, dcs_ref,
             kks, vvs, q2, qq2, dd2, da2, gi2, dl2, dq2, dvb, dkb, dkacc, dvacc, csacc):
        p, s = pl.program_id(0), pl.program_id(1)
        ones = ones_ref[...]

        @pl.when(s == 0)
        def _():
            dkacc[...] = jnp.zeros_like(dkacc)
            dvacc[...] = jnp.zeros_like(dvacc)
            csacc[...] = jnp.zeros_like(csacc)

            def prep(j, carry):
                rows = pl.ds(pl.multiple_of(j * C, C), C)
                kks[j] = _split_heads(k_ref[rows, :].astype(F32)).astype(BF)
                vvs[j] = _split_heads(v_ref[rows, :].astype(F32)).astype(BF)
                return carry

            lax.fori_loop(0, NCH, prep, 0)

        for w, (q_ref, d_ref, g_ref, l_ref) in enumerate(((qa_ref, daa_ref, ga_ref, dla_ref),
                                                          (qb_ref, dab_ref, gb_ref, dlb_ref))):
            qf = q_ref[...].astype(F32)
            q2[w] = (qf * FSCALE).astype(BF)
            qq2[w] = jnp.concatenate([_split_heads(qf).astype(BF), ones], axis=1)
            da2[w] = d_ref[...]
            dd2[w] = _split_heads(d_ref[...].astype(F32)).astype(BF)
            gi2[w] = _spread2(g_ref[...])
            dl2[w] = _spread2(l_ref[...])
        dq2[...] = jnp.zeros_like(dq2)
        zero = jnp.zeros((C, 2 * C), F32)

        def group(gi, carry):
            ts = [gi * grp + u for u in range(grp)]
            tiles = [_fox_tile(s, t) for t in ts]
            kk = [kks[j] for _, j in tiles]
            ss = [_dg(q2[sel], kj, NT) + (gi2[sel] - _fox_key_bias(ct_ref, p, j)) for kj, (sel, j) in zip(kk, tiles)]
            ss[0] = ss[0] + mb_ref[(gi == 0).astype(jnp.int32)]
            ss[-1] = ss[-1] + mb_ref[(gi == 1).astype(jnp.int32)]
            dps = [_dg(da2[sel], vvs[j], NT) for sel, j in tiles]
            pes = [jnp.exp(st) for st in ss]
            dss = [pe * (dp - dl2[sel]) * FSCALE for pe, dp, (sel, _) in zip(pes, dps, tiles)]
            pts = [jnp.concatenate([pe[:, :C].T, pe[:, C:].T], axis=1).astype(BF) for pe in pes]
            dsts = [jnp.concatenate([ds[:, :C].T, ds[:, C:].T], axis=1).astype(BF) for ds in dss]
            dvs = [_dot(pt, dd2[sel]) for pt, (sel, _) in zip(pts, tiles)]
            rs = [_dot(dst, qq2[sel]) for dst, (sel, _) in zip(dsts, tiles)]
            parts = [_dot(ds.astype(BF), jnp.concatenate([kj, ones], axis=1)) for ds, kj in zip(dss, kk)]
            for t, dv, rr in zip(ts, dvs, rs):
                dvb[t] = dv
                dkb[t] = rr
            pa, pb = zero, zero
            for t, part in zip(ts, parts):
                pa = pa + jnp.where(t <= s, part, zero)
                pb = pb + jnp.where(t <= s, zero, part)
            dq2[0] += pa
            dq2[1] += pb
            return carry

        ntile = jnp.where(s == NSTEP - 1, grp, NTILE)
        lax.fori_loop(0, ntile // grp, group, 0)

        def scatter(t, carry):
            _, j = _fox_tile(s, t)
            r = pl.ds(pl.multiple_of(j * C, C), C)
            dvacc[r, :] += dvb[t]
            dkacc[r, :] += dkb[t, :, :C]
            csacc[r, :] += dkb[t, :, C:]
            return carry

        lax.fori_loop(0, ntile, scatter, 0)
        for w in range(2):
            res = dq2[w]
            dq_ref[C * w:C * (w + 1), :] = res[:, :C].astype(BF)
            dr_ref[C * w:C * (w + 1), :] = res[:, C:]

        @pl.when(s == NSTEP - 1)
        def _():
            dk_ref[...] = dkacc[...].astype(BF)
            dv_ref[...] = dvacc[...].astype(BF)
            dcs_ref[...] = csacc[...]

    qa, qb = _fox_q_specs()
    ba, bb, both = _fox_pair_specs()
    col = pl.BlockSpec((T, C), lambda p, s: (0, p))
    return pl.pallas_call(
        body, name="fox_bwd", grid=(NPAIR, NSTEP),
        in_specs=[qa, qb, ba, bb, ba, bb, ba, bb,
                  pl.BlockSpec((T, C), lambda p, s: (0, KB_F + p)),
                  pl.BlockSpec((T, C), lambda p, s: (0, VB_F + p)),
                  pl.BlockSpec((NCH, FH, C), lambda p, s: (0, 0, 0)),
                  pl.BlockSpec((2 * C, C), lambda p, s: (0, 0)),
                  pl.BlockSpec((2, C, 2 * C), lambda p, s: (0, 0, 0))],
        out_specs=[both, both, col, col, col],
        out_shape=[jax.ShapeDtypeStruct((TROWS, FH * FD), BF), jax.ShapeDtypeStruct((TROWS, FH * FD), F32),
                   jax.ShapeDtypeStruct((T, FH * FD), BF), jax.ShapeDtypeStruct((T, FH * FD), BF),
                   jax.ShapeDtypeStruct((T, FH * FD), F32)],
        scratch_shapes=[pltpu.VMEM((NCH, 2 * C, C), BF), pltpu.VMEM((NCH, 2 * C, C), BF),
                        pltpu.VMEM((2, C, C), BF), pltpu.VMEM((2, 2 * C, 2 * C), BF), pltpu.VMEM((2, 2 * C, C), BF),
                        pltpu.VMEM((2, C, C), BF), pltpu.VMEM((2, C, 2 * C), F32), pltpu.VMEM((2, C, 2 * C), F32),
                        pltpu.VMEM((2, C, 2 * C), F32),
                        pltpu.VMEM((NTILE, C, C), F32), pltpu.VMEM((NTILE, C, 2 * C), F32),
                        pltpu.VMEM((T, C), F32), pltpu.VMEM((T, C), F32), pltpu.VMEM((T, C), F32)],
        compiler_params=_params(("parallel", "arbitrary")),
    )(z, z, da, da, g, g, delta, delta, z, z, ct, cst["ones_aug"], cst["mask_bias"])


def _fox_gate_bwd(drow, dcol, zf, bf_pad, cst):
    def body(dr_ref, dc_ref, zf_ref, b_ref, tri_ref, pick_ref, dff_ref, db_ref, carry):
        s = pl.program_id(0)
        n = NCH - 1 - s

        @pl.when(s == 0)
        def _():
            carry[...] = jnp.zeros_like(carry)
            db_ref[...] = jnp.zeros_like(db_ref)

        dcb = _split_dot((dr_ref[...] - dc_ref[...]) * (1.0 / FSCALE), pick_ref[...])
        suf = _split_dot(dcb, tri_ref[...], TN, x_first=False) + carry[0:1, :]
        carry[...] = jnp.broadcast_to(suf[0:1, :], carry.shape)
        x = zf_ref[...] + b_ref[...]
        row = n * C + lax.broadcasted_iota(jnp.int32, (C, C), 0)
        dff = jnp.where(row >= PAD, suf * (1.0 - jax.nn.sigmoid(x)), 0.0)
        dff_ref[...] = dff.astype(BF)
        db_ref[...] += jnp.sum(dff, axis=0, keepdims=True)

    rev = lambda s: (NCH - 1 - s, 0)
    return pl.pallas_call(
        body, name="fox_gate_bwd", grid=(NCH,),
        in_specs=[pl.BlockSpec((C, FH * FD), lambda s: (_fox_pos(NCH - 1 - s), 0)),
                  pl.BlockSpec((C, FH * FD), rev), pl.BlockSpec((C, C), rev),
                  pl.BlockSpec((1, C), lambda s: (0, 0)), pl.BlockSpec((C, C), lambda s: (0, 0)),
                  pl.BlockSpec((FH * FD, C), lambda s: (0, 0))],
        out_specs=[pl.BlockSpec((C, C), rev), pl.BlockSpec((1, C), lambda s: (0, 0))],
        out_shape=[jax.ShapeDtypeStruct((T, C), BF), jax.ShapeDtypeStruct((1, C), F32)],
        scratch_shapes=[pltpu.VMEM((8, C), F32)],
        compiler_params=_params(("arbitrary",)),
    )(drow, dcol, zf, bf_pad, cst["tri"], cst["pick"])


def _gated(r, rg, a, fg):
    rn, rs = [], []
    for h in range(RH):
        rh = r[:, RDV * h:RDV * (h + 1)]
        s = lax.rsqrt(jnp.mean(rh * rh, axis=1, keepdims=True) + EPS)
        rn.append(rh * s)
        rs.append(s)
    rn = jnp.concatenate(rn, axis=1)
    y = jnp.concatenate([rn * (rg * jax.nn.sigmoid(rg)), a * (fg * jax.nn.sigmoid(fg))], axis=1)
    return y, rn, rs


def _out_loss(r, z, a, wout, x, tgt, fgain):
    def body(r_ref, rg_ref, a_ref, fg_ref, w_ref, x_ref, t_ref, g_ref, yt_ref, do_ref, dob_ref, loss_ref, dg_ref):
        i = pl.program_id(0)

        @pl.when(i == 0)
        def _():
            yt_ref[...] = jnp.zeros_like(yt_ref)
            do_ref[...] = jnp.zeros_like(do_ref)
            dob_ref[...] = jnp.zeros_like(dob_ref)
            loss_ref[...] = jnp.zeros_like(loss_ref)
            dg_ref[...] = jnp.zeros_like(dg_ref)

        @pl.when(i > 0)
        def _():
            y, _, _ = _gated(r_ref[...], rg_ref[...].astype(F32), a_ref[...], fg_ref[...].astype(F32))
            yt_ref[...] = y.T.astype(BF)
            o = x_ref[...] + _dot(y.astype(BF), w_ref[...])
            rs = lax.rsqrt(jnp.mean(o * o, axis=1, keepdims=True) + EPS)
            on = o * rs
            g = g_ref[...]
            e = on * g - t_ref[...]
            loss_ref[...] += 0.5 * jnp.sum(jnp.mean(e * e, axis=1, keepdims=True))
            dyh = e * (1.0 / D)
            dg_ref[...] += jnp.sum(dyh * on, axis=0, keepdims=True)
            don = dyh * g
            do = rs * (don - on * jnp.mean(don * on, axis=1, keepdims=True))
            do_ref[...] = do
            dob_ref[...] = do.astype(BF)

    tok = lambda i: (jnp.maximum(i - 1, 0), 0)
    return pl.pallas_call(
        body, name="out_loss", grid=(NCH,),
        in_specs=[pl.BlockSpec((C, D), lambda i: (i, 0)), pl.BlockSpec((C, D), lambda i: (i, GB_R)),
                  pl.BlockSpec((C, D), lambda i: (_fox_pos(i), 0)), pl.BlockSpec((C, D), lambda i: (i, GB_F)),
                  pl.BlockSpec((DMIX, D), lambda i: (0, 0)),
                  pl.BlockSpec((C, D), tok), pl.BlockSpec((C, D), tok), pl.BlockSpec((1, D), lambda i: (0, 0))],
        out_specs=[pl.BlockSpec((DMIX, C), lambda i: (0, i)), pl.BlockSpec((C, D), lambda i: (i, 0)),
                   pl.BlockSpec((C, D), lambda i: (i, 0)), pl.BlockSpec((8, C), lambda i: (0, 0)),
                   pl.BlockSpec((1, D), lambda i: (0, 0))],
        out_shape=[jax.ShapeDtypeStruct((DMIX, T), BF), jax.ShapeDtypeStruct((T, D), F32),
                   jax.ShapeDtypeStruct((T, D), BF), jax.ShapeDtypeStruct((8, C), F32),
                   jax.ShapeDtypeStruct((1, D), F32)],
        compiler_params=_params(("arbitrary",)),
    )(r, z, a, z, wout, x, tgt, fgain)


def _dsilu(x):
    s = jax.nn.sigmoid(x)
    return s * (1.0 + x * (1.0 - s))


def _dy_gate_bwd(dob, wout, r, z, a, seg):
    def body(do_ref, w_ref, r_ref, rg_ref, a_ref, fg_ref, seg_ref, dr_ref, da_ref, drg_ref, dfg_ref, dl_ref):
        dy = _dg(do_ref[...], w_ref[...], NT)
        rg, fg, a_ = rg_ref[...].astype(F32), fg_ref[...].astype(F32), a_ref[...]
        _, rn, rs = _gated(r_ref[...], rg, a_, fg)
        dyr, dyf = dy[:, :D], dy[:, D:]
        drn = dyr * (rg * jax.nn.sigmoid(rg))
        drg_ref[...] = (dyr * rn * _dsilu(rg)).astype(BF)
        for h in range(RH):
            sl = slice(RDV * h, RDV * (h + 1))
            dh, nh = drn[:, sl], rn[:, sl]
            dr_ref[:, sl] = (rs[h] * (dh - nh * jnp.mean(dh * nh, axis=1, keepdims=True))).astype(BF)
        dab = (dyf * (fg * jax.nn.sigmoid(fg))).astype(BF)
        da_ref[...] = dab
        dfg_ref[...] = (dyf * a_ * _dsilu(fg)).astype(BF)
        prod = dab.astype(F32) * a_
        segm = seg_ref[...]
        for p in range(NPAIR):
            sl = slice(C * p, C * (p + 1))
            hi = prod[:, sl].astype(BF)
            lo = (prod[:, sl] - hi.astype(F32)).astype(BF)
            dl_ref[:, sl] = _dot(hi, segm) + _dot(lo, segm)

    row = pl.BlockSpec((C, D), lambda i: (i, 0))
    fox = pl.BlockSpec((C, D), lambda i: (_fox_pos(i), 0))
    return pl.pallas_call(
        body, name="dy_gate_bwd", grid=(NCH,),
        in_specs=[row, pl.BlockSpec((DMIX, D), lambda i: (0, 0)),
                  row, pl.BlockSpec((C, D), lambda i: (i, GB_R)),
                  fox, pl.BlockSpec((C, D), lambda i: (i, GB_F)),
                  pl.BlockSpec((C, C), lambda i: (0, 0))],
        out_specs=[row, fox, row, row, fox],
        out_shape=[jax.ShapeDtypeStruct((T, D), BF), jax.ShapeDtypeStruct((TROWS, D), BF),
                   jax.ShapeDtypeStruct((T, D), BF), jax.ShapeDtypeStruct((T, D), BF),
                   jax.ShapeDtypeStruct((TROWS, D), F32)],
        compiler_params=_params(("parallel",)),
    )(dob, wout, r, z, a, z, seg)


DZ_WIDTHS = (512, 512, 1024, 1024, 1024, 1024, 1024, 1024)


def _du_norm_bwd(dzs, dzf, wt, wft, hpad, g, dopad, parts=()):
    tm, tk = 544, 1024
    nk = WMAIN // tk
    ni = T // tm
    n = len(parts)

    def body(rq_ref, rk_ref, rv_ref, rg_ref, fq_ref, fk_ref, fv_ref, fg_ref, dzf_ref, w_ref, wf_ref, h_ref, g_ref,
             do_ref, *rest):
        part_refs, (gh_ref, dg_ref), land_refs = rest[:n], rest[n:n + 2], rest[n + 2:2 * n + 2]
        acc = rest[2 * n + 2]
        i, k = pl.program_id(0), pl.program_id(1)

        if n:
            send_sems, recv_sems = rest[2 * n + 3:]
            copies = _chip_copies(part_refs, land_refs, send_sems, recv_sems, by_dest=True)

            @pl.when((i == 0) & (k == 0))
            def _():
                for cp in copies:
                    cp.start()

            @pl.when((i == ni - 1) & (k == nk - 1))
            def _():
                for cp in copies:
                    cp.wait()

        @pl.when(k == 0)
        def _():
            acc[...] = (_dot(dzf_ref[...], wf_ref[...]) + _dot(rq_ref[...], w_ref[:512, :])
                        + _dot(rk_ref[...], w_ref[512:, :]))

        for kk, piece in enumerate((rv_ref, rg_ref, fq_ref, fk_ref, fv_ref, fg_ref), start=1):
            @pl.when(k == kk)
            def _(piece=piece):
                acc[...] += _dot(piece[...], w_ref[...])

        @pl.when(k == nk - 1)
        def _():
            du = acc[...]
            h = h_ref[...]
            gg = g_ref[...]
            rs = lax.rsqrt(jnp.mean(h * h, axis=1, keepdims=True) + EPS)
            hn = h * rs
            part = jnp.sum(du * hn, axis=0, keepdims=True)

            @pl.when(i == 0)
            def _():
                dg_ref[...] = part

            @pl.when(i > 0)
            def _():
                dg_ref[...] += part

            dhn = du * gg
            gh_ref[...] = rs * (dhn - hn * jnp.mean(dhn * hn, axis=1, keepdims=True)) + do_ref[...]

    sems = [pltpu.SemaphoreType.DMA((3 * n,)), pltpu.SemaphoreType.DMA((3 * n,))] if n else []
    return pl.pallas_call(
        body, name="du_norm_bwd", grid=(ni, nk),
        in_specs=[pl.BlockSpec((tm, w), lambda i, k: (i, 0)) for w in DZ_WIDTHS]
        + [pl.BlockSpec((tm, C), lambda i, k: (i, 0)),
           pl.BlockSpec((tk, D), lambda i, k: (k, 0)), pl.BlockSpec((C, D), lambda i, k: (0, 0)),
           pl.BlockSpec((tm, D), lambda i, k: (i, 0)), pl.BlockSpec((1, D), lambda i, k: (0, 0)),
           pl.BlockSpec((tm, D), lambda i, k: (i, 0))] + [ANY] * n,
        out_specs=[pl.BlockSpec((tm, D), lambda i, k: (i, 0)), pl.BlockSpec((1, D), lambda i, k: (0, 0))] + [ANY] * n,
        out_shape=[jax.ShapeDtypeStruct((T, D), F32), jax.ShapeDtypeStruct((1, D), F32)]
        + [jax.ShapeDtypeStruct(p.shape, p.dtype) for p in parts],
        scratch_shapes=[pltpu.VMEM((tm, D), F32)] + sems,
        compiler_params=_params(("arbitrary", "arbitrary")),
    )(*dzs, dzf, wt, wft, hpad, g, dopad, *parts)


GROWS = 7680


def _dw_in(dzs, dzf, ut):
    tn = 512
    nmain = WMAIN // tn
    first, blocks = [], []
    for w in DZ_WIDTHS:
        first.append(sum(blocks))
        blocks.append(w // tn)

    def body(rq_ref, rk_ref, rv_ref, rg_ref, fq_ref, fk_ref, fv_ref, fg_ref, dzf_ref, ut_ref, o_ref):
        gidx = pl.program_id(0)
        for piece, g0, nb in zip((rq_ref, rk_ref, rv_ref, rg_ref, fq_ref, fk_ref, fv_ref, fg_ref), first, blocks):
            @pl.when((gidx >= g0) & (gidx < g0 + nb))
            def _(piece=piece):
                o_ref[...] = _dot(ut_ref[...], piece[...]).T.astype(BF)

        @pl.when(gidx == nmain)
        def _():
            o_ref[:C, :] = _dot(ut_ref[...], dzf_ref[...]).T.astype(BF)
            o_ref[C:, :] = jnp.zeros((tn - C, D), BF)

    def piece_spec(g0, nb):
        return pl.BlockSpec((T, tn), lambda gidx: (0, jnp.clip(gidx - g0, 0, nb - 1)))

    return pl.pallas_call(
        body, name="dw_in", grid=(nmain + 1,),
        in_specs=[piece_spec(g0, nb) for g0, nb in zip(first, blocks)]
        + [pl.BlockSpec((T, C), lambda gidx: (0, 0)), pl.BlockSpec((D, T), lambda gidx: (0, 0))],
        out_specs=pl.BlockSpec((tn, D), lambda gidx: (gidx, 0)),
        out_shape=jax.ShapeDtypeStruct((GROWS, D), BF),
        compiler_params=pltpu.CompilerParams(dimension_semantics=("arbitrary",), vmem_limit_bytes=DW_VMEM_LIMIT),
    )(*dzs, dzf, ut)


def _token_order(x_po):
    def body(i_ref, o_ref):
        o_ref[...] = i_ref[...]

    return pl.pallas_call(
        body, name="token_order", grid=(NCH,),
        in_specs=[pl.BlockSpec((C, D), lambda i: (_fox_pos(i), 0))],
        out_specs=pl.BlockSpec((C, D), lambda i: (i, 0)),
        out_shape=jax.ShapeDtypeStruct((T, D), x_po.dtype),
        compiler_params=_params(("parallel",)),
    )(x_po)


def _local_step(x, tgt, meta, norm_g, wt, wft, b_f, wout, final_g, chip_sums=None, wout_full=None):
    cst = _constants()
    hpad = jnp.concatenate([jnp.pad(meta, ((PAD, 0), (0, 0))), x], axis=0)
    bf_pad = jnp.pad(b_f, ((0, 0), (0, C - NFF)))
    u, ut = _norm_in(hpad, norm_g)
    z = _mm_nt(u, wt, WMAIN, T // 2, 512, "in_proj", BF)
    zf = _mm_nt(u, wft, C, T // 2, C, "in_proj_ff")
    r, sprev = _ret_fwd(z, cst)
    ct = _fox_prep(zf, bf_pad, cst)
    if wout_full is None:
        a, g = _fox_fwd(z, ct, cst, None)
    else:
        a, g, landed_wout = _fox_fwd(z, ct, cst, wout)
        wout = wout_full(landed_wout)
    yt, dopad, dob, loss8, dfg = _out_loss(r, z, a, wout, x, tgt, final_g)
    dr, da, dzrg, dzfg, delta = _dy_gate_bwd(dob, wout, r, z, a, cst["seg"])
    dwout = _mm_nn(yt, dob, 512, D, "dw_out", BF)
    dzq_r, dzk_r, dzv_r = _ret_bwd(z, cst, sprev, dr)
    dq_po, drow, dzk_f, dzv_f, dcol = _fox_bwd(z, da, g, delta, ct, cst)
    dzf, dbf = _fox_gate_bwd(drow, dcol, zf, bf_pad, cst)
    dzs = [dzq_r, dzk_r, dzv_r, dzrg, _token_order(dq_po), dzk_f, dzv_f, dzfg]
    gwt = _dw_in(dzs, dzf, ut)
    parts = chip_sums(gwt, dwout) if chip_sums else []
    gh, dng, *landed = _du_norm_bwd(dzs, dzf, wt, wft, hpad, norm_g, dopad, parts)
    return (loss8[0, 0], gh[C:], gh[PAD:C], dng, gwt, dbf[:, :NFF], dwout, dfg, parts, landed)


def _all_gather_shards(shards):
    n = len(shards)

    def body(*refs):
        ins, outs = refs[:n], refs[n:2 * n]
        send_sems, recv_sems = refs[2 * n:]
        x, y, c = _place()
        me_s = 2 * x + y
        sib = (x, y, 1 - c)
        chips = _other_chips(x, y)
        sends, waits = [], []
        for a in range(n):
            rows = ins[a].shape[0] // 2
            half = pl.ds(c * rows, rows)
            for k, (cx, cy, cs) in enumerate(chips):
                sends.append(pltpu.make_async_remote_copy(
                    src_ref=ins[a].at[half], dst_ref=outs[a].at[me_s, half],
                    send_sem=send_sems.at[6 * a + k], recv_sem=recv_sems.at[6 * a + k],
                    device_id=(cx, cy, c), device_id_type=MESH))
                sends[-1].start()
        for a in range(n):
            rows = ins[a].shape[0] // 2
            half = pl.ds(c * rows, rows)
            other = pl.ds((1 - c) * rows, rows)
            for k, (cx, cy, cs) in enumerate(chips):
                pltpu.make_async_remote_copy(
                    src_ref=outs[a].at[cs, half], dst_ref=outs[a].at[cs, half],
                    send_sem=send_sems.at[6 * a + k], recv_sem=recv_sems.at[6 * a + k],
                    device_id=(cx, cy, c), device_id_type=MESH).wait_recv()
                fwd = pltpu.make_async_remote_copy(
                    src_ref=outs[a].at[cs, half], dst_ref=outs[a].at[cs, half],
                    send_sem=send_sems.at[6 * a + 3 + k], recv_sem=recv_sems.at[6 * a + 3 + k],
                    device_id=sib, device_id_type=MESH)
                fwd.start()
                sends.append(fwd)
                waits.append(pltpu.make_async_remote_copy(
                    src_ref=outs[a].at[cs, other], dst_ref=outs[a].at[cs, other],
                    send_sem=send_sems.at[6 * a + 3 + k], recv_sem=recv_sems.at[6 * a + 3 + k],
                    device_id=sib, device_id_type=MESH))
        for w in waits:
            w.wait_recv()
        for s in sends:
            s.wait_send()

    return pl.pallas_call(
        body, name="all_gather_w",
        in_specs=[ANY] * n, out_specs=[ANY] * n,
        out_shape=[jax.ShapeDtypeStruct((4,) + s.shape, s.dtype) for s in shards],
        scratch_shapes=[pltpu.SemaphoreType.DMA((6 * n,)), pltpu.SemaphoreType.DMA((6 * n,))],
    )(*shards)


WOFF, WLEN = 1792, 2048
WHALF = WLEN // 2


def _pair_swap(gwt, arrs):
    n = len(arrs)

    def body(*refs):
        gw, ins = refs[0], refs[1:n + 1]
        gwo, outs = refs[n + 1], refs[n + 2:2 * n + 2]
        send_sems, recv_sems = refs[2 * n + 2:]
        x, y, c = _place()
        sib = (x, y, 1 - c)
        cps = []
        for k in range(4):
            cps.append(pltpu.make_async_remote_copy(
                src_ref=gw.at[pl.ds(WOFF * k + (1 - c) * WHALF, WHALF)], dst_ref=gwo.at[k],
                send_sem=send_sems.at[k], recv_sem=recv_sems.at[k], device_id=sib, device_id_type=MESH))
        for a in range(n):
            rows = ins[a].shape[1] // 2
            cps.append(pltpu.make_async_remote_copy(
                src_ref=ins[a].at[:, pl.ds((1 - c) * rows, rows)], dst_ref=outs[a],
                send_sem=send_sems.at[4 + a], recv_sem=recv_sems.at[4 + a], device_id=sib, device_id_type=MESH))
        for cp in cps:
            cp.start()
        for cp in cps:
            cp.wait()

    return pl.pallas_call(
        body, name="rs_pair_swap",
        in_specs=[ANY] * (n + 1), out_specs=[ANY] * (n + 1),
        out_shape=[jax.ShapeDtypeStruct((4, WHALF, D), gwt.dtype)]
        + [jax.ShapeDtypeStruct((4, a.shape[1] // 2, a.shape[2]), a.dtype) for a in arrs],
        scratch_shapes=[pltpu.SemaphoreType.DMA((n + 4,)), pltpu.SemaphoreType.DMA((n + 4,))],
    )(gwt, *arrs)


def _add_windows(gwt, recv):
    tb = 256
    nb = WHALF // tb
    c = lax.axis_index("c")

    def body(c_ref, a_ref, b_ref, o_ref):
        o_ref[0] = (a_ref[...].astype(F32) + b_ref[0].astype(F32)).astype(BF)

    return pl.pallas_call(
        body, name="pair_add_in",
        grid_spec=pltpu.PrefetchScalarGridSpec(
            num_scalar_prefetch=1, grid=(4, nb),
            in_specs=[pl.BlockSpec((tb, D), lambda k, i, cr: ((WOFF // tb) * k + nb * cr[0] + i, 0)),
                      pl.BlockSpec((1, tb, D), lambda k, i, cr: (k, i, 0))],
            out_specs=pl.BlockSpec((1, tb, D), lambda k, i, cr: (k, i, 0))),
        out_shape=jax.ShapeDtypeStruct(recv.shape, BF),
        compiler_params=_params(("parallel", "parallel")),
    )(jnp.reshape(c, (1,)).astype(jnp.int32), gwt, recv)


def _chip_exchange(parts, small):
    n = len(parts)

    def body(*refs):
        ins, sm = refs[:n], refs[n]
        outs, smo = refs[n + 1:2 * n + 1], refs[2 * n + 1]
        send_sems, recv_sems = refs[2 * n + 2:]
        cps = _chip_copies(ins, outs, send_sems, recv_sems, by_dest=True)
        cps += _chip_copies([sm], [smo], send_sems.at[pl.ds(3 * n, 3)], recv_sems.at[pl.ds(3 * n, 3)], by_dest=False)
        for cp in cps:
            cp.start()
        for cp in cps:
            cp.wait()

    return pl.pallas_call(
        body, name="rs_chip_exchange",
        in_specs=[ANY] * (n + 1), out_specs=[ANY] * (n + 1),
        out_shape=[jax.ShapeDtypeStruct(p.shape, p.dtype) for p in parts]
        + [jax.ShapeDtypeStruct((4,) + small.shape, small.dtype)],
        scratch_shapes=[pltpu.SemaphoreType.DMA((3 * (n + 1),)), pltpu.SemaphoreType.DMA((3 * (n + 1),))],
    )(*parts, small)


def _pair_send(halves):
    n = len(halves)

    def body(*refs):
        ins, outs = refs[:n], refs[n:2 * n]
        send_sems, recv_sems = refs[2 * n:]
        x, y, c = _place()
        cps = [pltpu.make_async_remote_copy(
            src_ref=ins[a], dst_ref=outs[a], send_sem=send_sems.at[a], recv_sem=recv_sems.at[a],
            device_id=(x, y, 1 - c), device_id_type=MESH) for a in range(n)]
        for cp in cps:
            cp.start()
        for cp in cps:
            cp.wait()

    return pl.pallas_call(
        body, name="rs_pair_send",
        in_specs=[ANY] * n, out_specs=[ANY] * n,
        out_shape=[jax.ShapeDtypeStruct(h.shape, h.dtype) for h in halves],
        scratch_shapes=[pltpu.SemaphoreType.DMA((n,)), pltpu.SemaphoreType.DMA((n,))],
    )(*halves)


def _row_block(rows):
    for tb in (256, 128, 64, 32, 16, 8):
        if rows % tb == 0:
            return tb
    return rows


def _add_halves(full, recv, name, out_dtype):
    _, r2, w = recv.shape
    tb = _row_block(r2)
    nb = r2 // tb
    c = lax.axis_index("c")

    def body(c_ref, a_ref, b_ref, o_ref):
        o_ref[...] = (a_ref[...].astype(F32) + b_ref[...].astype(F32)).astype(o_ref.dtype)

    return pl.pallas_call(
        body, name=name,
        grid_spec=pltpu.PrefetchScalarGridSpec(
            num_scalar_prefetch=1, grid=(4, nb),
            in_specs=[pl.BlockSpec((1, tb, w), lambda s, i, cr: (s, cr[0] * nb + i, 0)),
                      pl.BlockSpec((1, tb, w), lambda s, i, cr: (s, i, 0))],
            out_specs=pl.BlockSpec((1, tb, w), lambda s, i, cr: (s, i, 0))),
        out_shape=jax.ShapeDtypeStruct(recv.shape, out_dtype),
        compiler_params=_params(("parallel", "parallel")),
    )(jnp.reshape(c, (1,)).astype(jnp.int32), full, recv)


def _add2(a, b, name):
    def body(a_ref, b_ref, o_ref):
        o_ref[...] = a_ref[...] + b_ref[...]

    return pl.pallas_call(body, name=name, out_shape=jax.ShapeDtypeStruct(a.shape, a.dtype))(a, b)


def _sum4(buf, own, name):
    _, r, w = buf.shape
    tb = _row_block(r)
    me_s = 2 * lax.axis_index("x") + lax.axis_index("y")
    by_dest = own.ndim == 3

    def body(s_ref, b_ref, own_ref, o_ref):
        mine = (own_ref[0] if by_dest else own_ref[...]).astype(F32)
        terms = [jnp.where(s_ref[0] == t, mine, b_ref[t].astype(F32)) for t in range(4)]
        o_ref[...] = ((terms[0] + terms[1]) + terms[2]) + terms[3]

    own_spec = (pl.BlockSpec((1, tb, w), lambda i, sr: (sr[0], i, 0)) if by_dest
                else pl.BlockSpec((tb, w), lambda i, sr: (i, 0)))
    return pl.pallas_call(
        body, name=name,
        grid_spec=pltpu.PrefetchScalarGridSpec(
            num_scalar_prefetch=1, grid=(r // tb,),
            in_specs=[pl.BlockSpec((4, tb, w), lambda i, sr: (0, i, 0)), own_spec],
            out_specs=pl.BlockSpec((tb, w), lambda i, sr: (i, 0))),
        out_shape=jax.ShapeDtypeStruct((r, w), F32),
        compiler_params=_params(("parallel",)),
    )(jnp.reshape(me_s, (1,)).astype(jnp.int32), buf, own)


def _adamw_math(w, g, m, v):
    mn = B1 * m + (1.0 - B1) * g
    vn = B2 * v + (1.0 - B2) * (g * g)
    m_hat = mn / (1.0 - B1 ** STEP)
    v_hat = vn / (1.0 - B2 ** STEP)
    return -LR * (m_hat / (jnp.sqrt(v_hat) + AEPS) + WD * w), mn, vn


def _adamw(w, g, m, v, name):
    r, c_ = w.shape
    tb = _row_block(r)
    if tb == r and r > 512:
        tb = 256

    def body(w_ref, g_ref, m_ref, v_ref, d_ref, mo_ref, vo_ref):
        d_ref[...], mo_ref[...], vo_ref[...] = _adamw_math(w_ref[...], g_ref[...], m_ref[...], v_ref[...])

    spec = pl.BlockSpec((tb, c_), lambda i: (i, 0))
    return pl.pallas_call(
        body, name=name, grid=(pl.cdiv(r, tb),),
        in_specs=[spec] * 4, out_specs=[spec] * 3,
        out_shape=[jax.ShapeDtypeStruct(w.shape, F32)] * 3,
        compiler_params=_params(("parallel",)),
    )(w, g, m, v)


def _adamw_halves(w, g_mine, g_sib, m, v, name):
    r, c_ = w.shape
    r2 = g_mine.shape[0]
    tb = _row_block(r2)
    nb = r2 // tb
    c = lax.axis_index("c")

    def body(c_ref, w_ref, gm_ref, gs_ref, m_ref, v_ref, g_ref, d_ref, mo_ref, vo_ref):
        g = jnp.where(pl.program_id(0) == c_ref[0], gm_ref[...], gs_ref[...])
        g_ref[...] = g
        d_ref[...], mo_ref[...], vo_ref[...] = _adamw_math(w_ref[...], g, m_ref[...], v_ref[...])

    full = pl.BlockSpec((tb, c_), lambda h, i, cr: (h * nb + i, 0))
    half = pl.BlockSpec((tb, c_), lambda h, i, cr: (i, 0))
    return pl.pallas_call(
        body, name=name,
        grid_spec=pltpu.PrefetchScalarGridSpec(
            num_scalar_prefetch=1, grid=(2, nb),
            in_specs=[full, half, half, full, full], out_specs=[full] * 4),
        out_shape=[jax.ShapeDtypeStruct(w.shape, F32)] * 4,
        compiler_params=_params(("parallel", "parallel")),
    )(jnp.reshape(c, (1,)).astype(jnp.int32), w, g_mine, g_sib, m, v)


def kernel(x, meta_tokens, norm_g, w_in, b_f, w_out, final_g, loss_target, m_meta_tokens, m_norm_g, m_w_in, m_b_f, m_w_out, m_final_g, v_meta_tokens, v_norm_g, v_w_in, v_b_f, v_w_out, v_final_g):
    me_s = 2 * lax.axis_index("x") + lax.axis_index("y")
    core = lax.axis_index("c")
    wt, mt, vt = [jnp.swapaxes(t[0], 0, 1) for t in (w_in, m_w_in, v_w_in)]

    own_win = lax.dynamic_update_slice(jnp.zeros((WPADROWS, D), F32), wt, (4 * me_s, 0)).astype(BF)
    own = [own_win, meta_tokens]
    gathered = _all_gather_shards(own)
    mine = (jnp.arange(4) == me_s)[:, None, None]
    win, gmeta = [jnp.where(mine, o[None], g) for o, g in zip(own, gathered)]
    wout_own = w_out[0].astype(BF)
    lap = WPADROWS - WOFF
    tails = jnp.concatenate([jnp.zeros((1, lap, D), BF), win[:-1, WOFF:]], axis=0)
    wt_main = jnp.concatenate([win[:, :lap] + tails, win[:, lap:WOFF]], axis=1).reshape(WMAIN, D)
    wft = jnp.pad(win[3, WOFF:WOFF + NFF], ((0, C - NFF), (0, 0)))
    meta = jnp.concatenate([gmeta[s] for s in range(4)], axis=1)

    def wout_full(landed):
        return jnp.where(mine, wout_own[None], landed).reshape(DMIX, D)

    def chip_sums(gwt, dwout):
        g_out = dwout.reshape(4, DMIX // 4, D)
        r_in, r_out = _pair_swap(gwt, [g_out])
        return [_add_windows(gwt, r_in), _add_halves(g_out, r_out, "pair_add_out", BF)]

    loss, gx, dmeta, dng, gwt, dbf, dwout, dfg, (p_in, p_out), (e_in, e_out) = _local_step(
        x[0], loss_target[0], meta, norm_g, wt_main, wft, b_f, wout_own, final_g.reshape(1, D), chip_sums, wout_full)

    g_meta = jnp.stack([dmeta[:, 256 * s:256 * (s + 1)] for s in range(4)])
    small = jnp.concatenate([dng, dfg, jnp.pad(dbf, ((0, 0), (0, D - NFF))),
                             jnp.pad(jnp.reshape(loss, (1, 1)), ((0, 0), (0, D - 1))),
                             jnp.zeros((4, D), F32)], axis=0)
    e_meta, e_small = _chip_exchange([g_meta], small)
    h_in, h_out = _sum4(e_in, p_in, "sum_in"), _sum4(e_out, p_out, "sum_out")
    h_meta, h_small = _sum4(e_meta, g_meta, "sum_meta"), _sum4(e_small, small, "sum_small")
    s_in, s_out, s_meta, s_small = _pair_send([h_in, h_out, h_meta, h_small])
    gw_meta = _add2(h_meta, s_meta, "pair_add_meta")
    tot = _add2(h_small, s_small, "pair_add_small")
    g_norm, g_final, g_bf, loss_all = tot[0:1], tot[1], tot[2:3, :NFF], tot[3, 0]

    d_meta, nm_meta, nv_meta = _adamw(meta_tokens, gw_meta, m_meta_tokens, v_meta_tokens, "adamw_meta")
    d_norm, nm_norm, nv_norm = _adamw(norm_g, g_norm, m_norm_g, v_norm_g, "adamw_norm")
    window = jnp.concatenate([jnp.where(core == 0, h_in, s_in), jnp.where(core == 0, s_in, h_in)], axis=0)
    gwt_own = lax.dynamic_slice(window, (4 * me_s, 0), (WSH, D))
    d_in, nm_in, nv_in = _adamw(wt, gwt_own, mt, vt, "adamw_in")
    gw_in, d_in, nm_in, nv_in = [jnp.swapaxes(t, 0, 1)[None] for t in (gwt_own, d_in, nm_in, nv_in)]
    d_bf, nm_bf, nv_bf = _adamw(b_f, g_bf, m_b_f, v_b_f, "adamw_bf")
    gw_out, d_out, nm_out, nv_out = _adamw_halves(w_out[0], h_out, s_out, m_w_out[0], v_w_out[0], "adamw_out")
    d_fin, nm_fin, nv_fin = _adamw(final_g.reshape(1, D), g_final.reshape(1, D), m_final_g.reshape(1, D),
                                   v_final_g.reshape(1, D), "adamw_final")
    return (loss_all, gx[None], gw_meta, g_norm, gw_in, g_bf, gw_out[None], g_final,
            d_meta, d_norm, d_in, d_bf, d_out[None], d_fin.reshape(D),
            nm_meta, nm_norm, nm_in, nm_bf, nm_out[None], nm_fin.reshape(D),
            nv_meta, nv_norm, nv_in, nv_bf, nv_out[None], nv_fin.reshape(D))
```

```python
import numpy as np
import jax
import jax.numpy as jnp
from jax import lax
from jax.experimental import pallas as pl
from jax.experimental.pallas import tpu as pltpu

D = 1024
SEQ = 2048
NMETA = 16
C = 128
PAD = C - NMETA
T = PAD + NMETA + SEQ
NCH = T // C
RH, RDK, RDV = 4, 128, 256
FH, FD = 16, 64
NPAIR = FH // 2
WMAIN = 7168
NFF = 16
WIN = WMAIN + NFF
WSH = WIN // 4
WPADROWS = 1824
DMIX = 2048
EPS = 1e-6
NEG = -1e30
RSCALE = RDK ** -0.5
FSCALE = FD ** -0.5
ROPE_BASE = 10000.0
LR, B1, B2, AEPS, WD, STEP = 0.001, 0.9, 0.999, 1e-08, 0.01, 10

BF = jnp.bfloat16
F32 = jnp.float32
NT = (((1,), (1,)), ((), ()))
TN = (((0,), (0,)), ((), ()))
NN_DIMS = (((1,), (0,)), ((), ()))
MESH = pl.DeviceIdType.MESH
ANY = pl.BlockSpec(memory_space=pl.ANY)
VMEM_LIMIT = 48 * 1024 * 1024
DW_VMEM_LIMIT = 56 * 1024 * 1024

QB_R, KB_R = 0, 4
VB_R = 4
GB_R, GB_F = 2, 6
QB_F, KB_F, VB_F = 24, 32, 40


def _dot(a, b):
    return jnp.dot(a, b, preferred_element_type=F32)


def _dg(a, b, dims):
    return lax.dot_general(a, b, dims, preferred_element_type=F32)


def _params(sem=None):
    return pltpu.CompilerParams(dimension_semantics=sem, vmem_limit_bytes=VMEM_LIMIT)


def _constants():
    pos = jnp.arange(T, dtype=F32) - PAD
    inv = ROPE_BASE ** (-jnp.arange(0, RDK, 2, dtype=F32) / RDK)
    ang = pos[:, None] * inv[None, :]
    cos, sin = jnp.cos(ang), jnp.sin(ang)
    cos2 = jnp.concatenate([cos, cos], axis=1)
    sin2 = jnp.concatenate([-sin, sin], axis=1)
    log_gamma = jnp.log1p(-jnp.exp2(-5.0 - jnp.arange(RH, dtype=F32)))
    idx = jnp.arange(C, dtype=F32)
    diff = idx[:, None] - idx[None, :]
    dmask = jnp.where(diff[None] >= 0, jnp.exp(log_gamma[:, None, None] * jnp.maximum(diff, 0.0)[None]), 0.0)
    zeta = jnp.exp(log_gamma[:, None] * (C - 1.0 - idx)[None, :])
    xi = jnp.exp(log_gamma[:, None] * (idx + 1.0)[None, :])
    gdec = jnp.exp(log_gamma * C)
    zeta_b = jnp.broadcast_to(zeta[:, :, None], (RH, C, RDK))
    xi_b = jnp.broadcast_to(xi[:, :, None], (RH, C, RDK))
    gdec_b = jnp.broadcast_to(gdec[:, None, None], (RH, RDK, RDV))
    tri = jnp.asarray(np.tril(np.ones((C, C), np.float32)), dtype=BF)
    head_of_lane = np.arange(FH * FD) // FD
    pick = ((np.arange(FH * FD)[:, None] % FD == 0)
            & (head_of_lane[:, None] == np.arange(C)[None, :])).astype(np.float32)
    seg = (np.arange(C)[:, None] // FD == np.arange(C)[None, :] // FD).astype(np.float32)
    ones_aug = np.concatenate([np.tile((np.arange(C) < FD)[None, :], (C, 1)),
                               np.tile((np.arange(C) >= FD)[None, :], (C, 1))], axis=0).astype(np.float32)
    lane = np.arange(2 * C) % C
    causal = np.where(lane[None, :] <= np.arange(C)[:, None], 0.0, NEG).astype(np.float32)
    mask_bias = np.stack([np.zeros((C, 2 * C), np.float32), causal])
    return dict(cos2=cos2, sin2=sin2, dmask=dmask, zeta=zeta_b, xi=xi_b, gdec=gdec_b, tri=tri,
                mask_bias=jnp.asarray(mask_bias), pick=jnp.asarray(pick, dtype=BF), seg=jnp.asarray(seg, dtype=BF),
                ones_aug=jnp.asarray(ones_aug, dtype=BF))


def _rowblk(n):
    return (n + NCH - 1) % NCH


def _norm_in(x, metap, g):
    def body(x_ref, m_ref, g_ref, u_ref, ut_ref):
        h = jnp.where(pl.program_id(0) == NCH - 1, m_ref[...], x_ref[...])
        rs = lax.rsqrt(jnp.mean(h * h, axis=1, keepdims=True) + EPS)
        u = h * rs * g_ref[...]
        u_ref[...] = u.astype(BF)
        ut_ref[...] = u.T.astype(BF)

    return pl.pallas_call(
        body, name="norm_in", grid=(NCH,),
        in_specs=[pl.BlockSpec((C, D), lambda i: (jnp.minimum(i, NCH - 2), 0)),
                  pl.BlockSpec((C, D), lambda i: (0, 0)), pl.BlockSpec((1, D), lambda i: (0, 0))],
        out_specs=[pl.BlockSpec((C, D), lambda i: (i, 0)), pl.BlockSpec((D, C), lambda i: (0, i))],
        out_shape=[jax.ShapeDtypeStruct((T, D), BF), jax.ShapeDtypeStruct((D, T), BF)],
        compiler_params=_params(("parallel",)),
    )(x, metap, g)


def _mm_nt(a, b, n, tm, tn, name, out_dtype=F32):
    m, k = a.shape

    def body(a_ref, b_ref, o_ref):
        o_ref[...] = _dg(a_ref[...], b_ref[...], NT).astype(out_dtype)

    return pl.pallas_call(
        body, name=name, grid=(m // tm, n // tn),
        in_specs=[pl.BlockSpec((tm, k), lambda i, j: (i, 0)), pl.BlockSpec((tn, k), lambda i, j: (j, 0))],
        out_specs=pl.BlockSpec((tm, tn), lambda i, j: (i, j)),
        out_shape=jax.ShapeDtypeStruct((m, n), out_dtype),
        compiler_params=_params(("parallel", "parallel")),
    )(a, b)


def _mm_nn(a, b, tm, tn, name, out_dtype=F32):
    m, k = a.shape
    _, n = b.shape

    def body(a_ref, b_ref, o_ref):
        o_ref[...] = _dot(a_ref[...], b_ref[...]).astype(out_dtype)

    return pl.pallas_call(
        body, name=name, grid=(m // tm, n // tn),
        in_specs=[pl.BlockSpec((tm, k), lambda i, j: (i, 0)), pl.BlockSpec((k, tn), lambda i, j: (0, j))],
        out_specs=pl.BlockSpec((tm, tn), lambda i, j: (i, j)),
        out_shape=jax.ShapeDtypeStruct((m, n), out_dtype),
        compiler_params=_params(("parallel", "parallel")),
    )(a, b)


def _rot(x, cos2, sin2):
    return x * cos2 + pltpu.roll(x, 64, 1) * sin2


def _ret_specs(chunk):
    whole = lambda shape: pl.BlockSpec(shape, lambda n: (0,) * len(shape))
    return [
        pl.BlockSpec((C, RH * RDK), lambda n: (_rowblk(chunk(n)), 0)),
        pl.BlockSpec((C, RH * RDK), lambda n: (_rowblk(chunk(n)), 1)),
        pl.BlockSpec((C, RH * RDV), lambda n: (_rowblk(chunk(n)), 1)),
        pl.BlockSpec((C, RDK), lambda n: (chunk(n), 0)),
        pl.BlockSpec((C, RDK), lambda n: (chunk(n), 0)),
        whole((RH, C, C)), whole((RH, C, RDK)), whole((RH, C, RDK)), whole((RH, RDK, RDV)),
    ]


def _ret_heads(q_ref, k_ref, v_ref, cos, sin):
    qr = [_rot(q_ref[:, RDK * h:RDK * (h + 1)], cos, sin) for h in range(RH)]
    kr = [_rot(k_ref[:, RDK * h:RDK * (h + 1)], cos, sin) * RSCALE for h in range(RH)]
    vb = [v_ref[:, RDV * h:RDV * (h + 1)].astype(BF) for h in range(RH)]
    return qr, kr, [t.astype(BF) for t in qr], [t.astype(BF) for t in kr], vb


def _ret_fwd(z, cst):
    def body(q_ref, k_ref, v_ref, cos_ref, sin_ref, dm_ref, xi_ref, zt_ref, gd_ref, r_ref, sp_ref, st):
        n = pl.program_id(0)

        @pl.when(n == 0)
        def _():
            st[...] = jnp.zeros_like(st)

        hs = range(RH)
        qr, kr, qb, kb, vb = _ret_heads(q_ref, k_ref, v_ref, cos_ref[...], sin_ref[...])
        sd = [(_dg(qb[h], kb[h], NT) * dm_ref[h]).astype(BF) for h in hs]
        state = [st[h] for h in hs]
        qx = [(qr[h] * xi_ref[h]).astype(BF) for h in hs]
        kz = [(kr[h] * zt_ref[h]).astype(BF) for h in hs]
        out = [_dot(sd[h], vb[h]) + _dot(qx[h], state[h].astype(BF)) for h in hs]
        kv = [_dg(kz[h], vb[h], TN) for h in hs]
        for h in hs:
            sp_ref[0, h] = state[h]
            r_ref[:, RDV * h:RDV * (h + 1)] = out[h]
            st[h] = state[h] * gd_ref[h] + kv[h]

    return pl.pallas_call(
        body, name="ret_fwd", grid=(NCH,),
        in_specs=_ret_specs(lambda n: n),
        out_specs=[pl.BlockSpec((C, RH * RDV), lambda n: (_rowblk(n), 0)),
                   pl.BlockSpec((1, RH, RDK, RDV), lambda n: (n, 0, 0, 0))],
        out_shape=[jax.ShapeDtypeStruct((T, RH * RDV), F32), jax.ShapeDtypeStruct((NCH, RH, RDK, RDV), F32)],
        scratch_shapes=[pltpu.VMEM((RH, RDK, RDV), F32)],
        compiler_params=_params(("arbitrary",)),
    )(z, z, z, cst["cos2"], cst["sin2"], cst["dmask"], cst["xi"], cst["zeta"], cst["gdec"])


def _ret_bwd(z, cst, sprev, dr):
    def body(q_ref, k_ref, v_ref, cos_ref, sin_ref, dm_ref, xi_ref, zt_ref, gd_ref, sp_ref, dr_ref,
             dq_ref, dk_ref, dv_ref, gst):
        i = pl.program_id(0)

        @pl.when(i == 0)
        def _():
            gst[...] = jnp.zeros_like(gst)

        hs = range(RH)
        cos, sin = cos_ref[...], sin_ref[...]
        qr, kr, qb, kb, vb = _ret_heads(q_ref, k_ref, v_ref, cos, sin)
        dm = [dm_ref[h] for h in hs]
        xi = [xi_ref[h] for h in hs]
        zt = [zt_ref[h] for h in hs]
        sd = [(_dg(qb[h], kb[h], NT) * dm[h]).astype(BF) for h in hs]
        qx = [(qr[h] * xi[h]).astype(BF) for h in hs]
        kz = [(kr[h] * zt[h]).astype(BF) for h in hs]
        drb = [dr_ref[:, RDV * h:RDV * (h + 1)] for h in hs]
        sb = [sp_ref[0, h].astype(BF) for h in hs]
        g = [gst[h] for h in hs]
        gb = [t.astype(BF) for t in g]
        ds = [(_dg(drb[h], vb[h], NT) * dm[h]).astype(BF) for h in hs]
        dq = [_dot(ds[h], kb[h]) + _dg(drb[h], sb[h], NT) * xi[h] for h in hs]
        dk = [(_dg(ds[h], qb[h], TN) + _dg(vb[h], gb[h], NT) * zt[h]) * RSCALE for h in hs]
        dv = [_dg(sd[h], drb[h], TN) + _dot(kz[h], gb[h]) for h in hs]
        gn = [g[h] * gd_ref[h] + _dg(qx[h], drb[h], TN) for h in hs]
        for h in hs:
            gst[h] = gn[h]
            dq_ref[:, RDK * h:RDK * (h + 1)] = (dq[h] * cos + pltpu.roll(dq[h] * sin, 64, 1)).astype(BF)
            dk_ref[:, RDK * h:RDK * (h + 1)] = (dk[h] * cos + pltpu.roll(dk[h] * sin, 64, 1)).astype(BF)
            dv_ref[:, RDV * h:RDV * (h + 1)] = dv[h].astype(BF)

    rev = lambda n: NCH - 1 - n
    return pl.pallas_call(
        body, name="ret_bwd", grid=(NCH,),
        in_specs=_ret_specs(rev) + [
            pl.BlockSpec((1, RH, RDK, RDV), lambda n: (rev(n), 0, 0, 0)),
            pl.BlockSpec((C, RH * RDV), lambda n: (_rowblk(rev(n)), 0)),
        ],
        out_specs=[pl.BlockSpec((C, RH * RDK), lambda n: (_rowblk(rev(n)), 0)),
                   pl.BlockSpec((C, RH * RDK), lambda n: (_rowblk(rev(n)), 0)),
                   pl.BlockSpec((C, RH * RDV), lambda n: (_rowblk(rev(n)), 0))],
        out_shape=[jax.ShapeDtypeStruct((T, RH * RDK), BF), jax.ShapeDtypeStruct((T, RH * RDK), BF),
                   jax.ShapeDtypeStruct((T, RH * RDV), BF)],
        scratch_shapes=[pltpu.VMEM((RH, RDK, RDV), F32)],
        compiler_params=_params(("arbitrary",)),
    )(z, z, z, cst["cos2"], cst["sin2"], cst["dmask"], cst["xi"], cst["zeta"], cst["gdec"], sprev, dr)


def _place():
    x, y, c = lax.axis_index("x"), lax.axis_index("y"), lax.axis_index("c")
    return x, y, c


def _other_chips(x, y):
    return [(1 - x, y, 2 * (1 - x) + y), (x, 1 - y, 2 * x + (1 - y)), (1 - x, 1 - y, 2 * (1 - x) + (1 - y))]


def _chip_copies(srcs, lands, send_sems, recv_sems, by_dest):
    x, y, c = _place()
    me_s = 2 * x + y
    return [pltpu.make_async_remote_copy(
        src_ref=src.at[cs] if by_dest else src, dst_ref=land.at[me_s],
        send_sem=send_sems.at[3 * a + j], recv_sem=recv_sems.at[3 * a + j],
        device_id=(cx, cy, c), device_id_type=MESH)
        for a, (src, land) in enumerate(zip(srcs, lands)) for j, (cx, cy, cs) in enumerate(_other_chips(x, y))]


def _split_dot(x, mat01, dims=NN_DIMS, x_first=True):
    acc, rest = None, x
    for _ in range(3):
        piece = rest.astype(BF)
        part = _dg(piece, mat01, dims) if x_first else _dg(mat01, piece, dims)
        acc = part if acc is None else acc + part
        rest = rest - piece.astype(F32)
    return acc


def _log_sigmoid(x):
    return -(jnp.maximum(-x, 0.0) + jnp.log1p(jnp.exp(-jnp.abs(x))))


def _fox_prep(zf, bf_pad, cst):
    def body(zf_ref, b_ref, tri_ref, ct_ref, carry):
        n = pl.program_id(0)

        @pl.when(n == 0)
        def _():
            carry[...] = jnp.zeros_like(carry)

        ls = _log_sigmoid(zf_ref[...] + b_ref[...])
        row = n * C + lax.broadcasted_iota(jnp.int32, (C, C), 0)
        lf = jnp.where(row >= PAD, ls, 0.0)
        cc = _split_dot(lf, tri_ref[...], x_first=False) + carry[0:1, :]
        carry[...] = jnp.broadcast_to(cc[C - 1:C, :], carry.shape)
        pos = n * C + lax.broadcasted_iota(jnp.int32, (FH, C), 1)
        ct_ref[0] = jnp.where(pos >= PAD, cc.T[:FH, :], -NEG)

    return pl.pallas_call(
        body, name="fox_prep", grid=(NCH,),
        in_specs=[pl.BlockSpec((C, C), lambda n: (_rowblk(n), 0)), pl.BlockSpec((1, C), lambda n: (0, 0)),
                  pl.BlockSpec((C, C), lambda n: (0, 0))],
        out_specs=pl.BlockSpec((1, FH, C), lambda n: (n, 0, 0)),
        out_shape=jax.ShapeDtypeStruct((NCH, FH, C), F32),
        scratch_shapes=[pltpu.VMEM((8, C), F32)],
        compiler_params=_params(("arbitrary",)),
    )(zf, bf_pad, cst["tri"])


def _lo_lanes(shape):
    return lax.broadcasted_iota(jnp.int32, shape, 1) < FD


def _split_heads(x):
    lo = _lo_lanes(x.shape)
    zero = jnp.zeros_like(x)
    return jnp.concatenate([jnp.where(lo, x, zero), jnp.where(lo, zero, x)], axis=0)


def _spread2(x):
    lo = _lo_lanes(x.shape)
    r = pltpu.roll(x, FD, 1)
    return jnp.concatenate([jnp.where(lo, x, r), jnp.where(lo, r, x)], axis=1)


NSTEP = (NCH + 1) // 2
NTILE = NCH + 1
TROWS = T + C


def _fox_tile(s, t):
    second = t > s
    return second.astype(jnp.int32), jnp.where(second, t - s - 1, s - t)


def _fox_pos(i):
    return jnp.where(i < NSTEP, 2 * i, 2 * (NCH - 1 - i) + 1)


FOX_ORDER = [2 * i if i < NSTEP else 2 * (NCH - 1 - i) + 1 for i in range(NCH)]


def _fox_pair_specs():
    first = pl.BlockSpec((C, C), lambda p, s: (2 * s, p))
    second = pl.BlockSpec((C, C), lambda p, s: (jnp.where(s == NSTEP - 1, 2 * s, 2 * s + 1), p))
    both = pl.BlockSpec((2 * C, C), lambda p, s: (s, p))
    return first, second, both


def _fox_q_specs():
    return (pl.BlockSpec((C, C), lambda p, s: (_rowblk(s), QB_F + p)),
            pl.BlockSpec((C, C), lambda p, s: (_rowblk(NCH - 1 - s), QB_F + p)))


def _fox_key_bias(ct_ref, p, j):
    return jnp.concatenate([ct_ref[j, pl.ds(2 * p, 1), :], ct_ref[j, pl.ds(2 * p + 1, 1), :]], axis=1)


def _fox_fwd(z, ct, cst, share):
    n = 0 if share is None else 1

    def body(qa_ref, qb_ref, k_ref, v_ref, ct_ref, ones_ref, mb_ref, *rest):
        share_refs, (a_ref, g_ref), land_refs = rest[:n], rest[n:n + 2], rest[n + 2:2 * n + 2]
        kks, vvs, q2, m2, sbuf = rest[2 * n + 2:2 * n + 7]
        p, s = pl.program_id(0), pl.program_id(1)
        if n:
            copies = _chip_copies(share_refs, land_refs, *rest[2 * n + 7:], by_dest=False)

            @pl.when((p == 0) & (s == 0))
            def _():
                for cp in copies:
                    cp.start()

            @pl.when((p == NPAIR - 1) & (s == NSTEP - 1))
            def _():
                for cp in copies:
                    cp.wait()

        @pl.when(s == 0)
        def _():
            ones = ones_ref[...]

            def prep(j, carry):
                rows = pl.ds(pl.multiple_of(_rowblk(j) * C, C), C)
                kks[j] = _split_heads(k_ref[rows, :]).astype(BF)
                vvs[j] = jnp.concatenate([_split_heads(v_ref[rows, :]).astype(BF), ones], axis=1)
                return carry

            lax.fori_loop(0, NCH, prep, 0)

        q2[0] = (qa_ref[...] * FSCALE).astype(BF)
        q2[1] = (qb_ref[...] * FSCALE).astype(BF)

        tiles = [_fox_tile(s, t) for t in range(NTILE)]
        causal = mb_ref[1]
        neg = jnp.full((C, 2 * C), NEG, F32)
        run, first = neg, neg
        for t, (sel, j) in enumerate(tiles):
            st = _dg(q2[sel], kks[j], NT) - _fox_key_bias(ct_ref, p, j)
            if t in (0, NTILE - 1):
                st = st + causal
            sbuf[t] = st
            run = jnp.maximum(jnp.where(t == s + 1, neg, run), st)
            first = jnp.where(t == s, run, first)
        for w, mx in enumerate((first, run)):
            m2[w] = jnp.concatenate(
                [jnp.broadcast_to(jnp.max(mx[:, :C], axis=1, keepdims=True), (C, C)),
                 jnp.broadcast_to(jnp.max(mx[:, C:], axis=1, keepdims=True), (C, C))], axis=1)

        zero = jnp.zeros((C, 2 * C), F32)
        run, first = zero, zero
        for t, (sel, j) in enumerate(tiles):
            run = jnp.where(t == s + 1, zero, run) + _dot(jnp.exp(sbuf[t] - m2[sel]).astype(BF), vvs[j])
            first = jnp.where(t == s, run, first)
        lo = _lo_lanes((C, C))
        for w, res in enumerate((first, run)):
            l = res[:, C:]
            a_ref[C * w:C * (w + 1), :] = res[:, :C] / l
            mw = m2[w]
            g_ref[C * w:C * (w + 1), :] = -(jnp.where(lo, mw[:, :C], mw[:, C:]) + jnp.log(l))

    qa, qb = _fox_q_specs()
    both = _fox_pair_specs()[2]
    return pl.pallas_call(
        body, name="fox_fwd", grid=(NPAIR, NSTEP),
        in_specs=[qa, qb,
                  pl.BlockSpec((T, C), lambda p, s: (0, KB_F + p)),
                  pl.BlockSpec((T, C), lambda p, s: (0, VB_F + p)),
                  pl.BlockSpec((NCH, FH, C), lambda p, s: (0, 0, 0)),
                  pl.BlockSpec((2 * C, C), lambda p, s: (0, 0)),
                  pl.BlockSpec((2, C, 2 * C), lambda p, s: (0, 0, 0))] + [ANY] * n,
        out_specs=[both, both] + [ANY] * n,
        out_shape=[jax.ShapeDtypeStruct((TROWS, FH * FD), F32)] * 2
        + ([jax.ShapeDtypeStruct((4,) + share.shape, share.dtype)] if n else []),
        scratch_shapes=[pltpu.VMEM((NCH, 2 * C, C), BF), pltpu.VMEM((NCH, 2 * C, 2 * C), BF),
                        pltpu.VMEM((2, C, C), BF), pltpu.VMEM((2, C, 2 * C), F32),
                        pltpu.VMEM((NTILE, C, 2 * C), F32)]
        + [pltpu.SemaphoreType.DMA((3,)), pltpu.SemaphoreType.DMA((3,))] * n,
        compiler_params=_params(("arbitrary", "arbitrary")),
    )(z, z, z, z, ct, cst["ones_aug"], cst["mask_bias"], *([share] * n))


def _fox_bwd(z, da, g, delta, ct, cst):
    grp = 9

    def body(qa_ref, qb_ref, daa_ref, dab_ref, ga_ref, gb_ref, dla_ref, dlb_ref, k_ref, v_ref, ct_ref, ones_ref,
             mb_ref, dq_ref, dr_ref, dk_ref, dv_ref, dcs_ref,
             kks, vvs, q2, qq2, dd2, da2, gi2, dl2, dq2, dvb, dkb, dkacc, dvacc, csacc):
        p, s = pl.program_id(0), pl.program_id(1)
        ones = ones_ref[...]

        @pl.when(s == 0)
        def _():
            dkacc[...] = jnp.zeros_like(dkacc)
            dvacc[...] = jnp.zeros_like(dvacc)
            csacc[...] = jnp.zeros_like(csacc)

            def prep(j, carry):
                rows = pl.ds(pl.multiple_of(_rowblk(j) * C, C), C)
                kks[j] = _split_heads(k_ref[rows, :]).astype(BF)
                vvs[j] = _split_heads(v_ref[rows, :]).astype(BF)
                return carry

            lax.fori_loop(0, NCH, prep, 0)

        for w, (q_ref, d_ref, g_ref, l_ref) in enumerate(((qa_ref, daa_ref, ga_ref, dla_ref),
                                                          (qb_ref, dab_ref, gb_ref, dlb_ref))):
            qf = q_ref[...]
            q2[w] = (qf * FSCALE).astype(BF)
            qq2[w] = jnp.concatenate([_split_heads(qf).astype(BF), ones], axis=1)
            da2[w] = d_ref[...]
            dd2[w] = _split_heads(d_ref[...].astype(F32)).astype(BF)
            gi2[w] = _spread2(g_ref[...])
            dl2[w] = _spread2(l_ref[...])
        dq2[...] = jnp.zeros_like(dq2)
        zero = jnp.zeros((C, 2 * C), F32)

        def group(gi, carry):
            ts = [gi * grp + u for u in range(grp)]
            tiles = [_fox_tile(s, t) for t in ts]
            kk = [kks[j] for _, j in tiles]
            ss = [_dg(q2[sel], kj, NT) + (gi2[sel] - _fox_key_bias(ct_ref, p, j)) for kj, (sel, j) in zip(kk, tiles)]
            ss[0] = ss[0] + mb_ref[(gi == 0).astype(jnp.int32)]
            ss[-1] = ss[-1] + mb_ref[(gi == 1).astype(jnp.int32)]
            dps = [_dg(da2[sel], vvs[j], NT) for sel, j in tiles]
            pes = [jnp.exp(st) for st in ss]
            dss = [pe * (dp - dl2[sel]) * FSCALE for pe, dp, (sel, _) in zip(pes, dps, tiles)]
            pts = [jnp.concatenate([pe[:, :C].T, pe[:, C:].T], axis=1).astype(BF) for pe in pes]
            dsts = [jnp.concatenate([ds[:, :C].T, ds[:, C:].T], axis=1).astype(BF) for ds in dss]
            dvs = [_dot(pt, dd2[sel]) for pt, (sel, _) in zip(pts, tiles)]
            rs = [_dot(dst, qq2[sel]) for dst, (sel, _) in zip(dsts, tiles)]
            parts = [_dot(ds.astype(BF), jnp.concatenate([kj, ones], axis=1)) for ds, kj in zip(dss, kk)]
            for t, dv, rr in zip(ts, dvs, rs):
                dvb[t] = dv
                dkb[t] = rr
            pa, pb = zero, zero
            for t, part in zip(ts, parts):
                pa = pa + jnp.where(t <= s, part, zero)
                pb = pb + jnp.where(t <= s, zero, part)
            dq2[0] += pa
            dq2[1] += pb
            return carry

        ntile = jnp.where(s == NSTEP - 1, grp, NTILE)
        lax.fori_loop(0, ntile // grp, group, 0)

        def scatter(t, carry):
            _, j = _fox_tile(s, t)
            r = pl.ds(pl.multiple_of(_rowblk(j) * C, C), C)
            dvacc[r, :] += dvb[t]
            dkacc[r, :] += dkb[t, :, :C]
            csacc[r, :] += dkb[t, :, C:]
            return carry

        lax.fori_loop(0, ntile, scatter, 0)
        for w in range(2):
            res = dq2[w]
            dq_ref[C * w:C * (w + 1), :] = res[:, :C].astype(BF)
            dr_ref[C * w:C * (w + 1), :] = res[:, C:]

        @pl.when(s == NSTEP - 1)
        def _():
            dk_ref[...] = dkacc[...].astype(BF)
            dv_ref[...] = dvacc[...].astype(BF)
            dcs_ref[...] = csacc[...]

    qa, qb = _fox_q_specs()
    ba, bb, both = _fox_pair_specs()
    col = pl.BlockSpec((T, C), lambda p, s: (0, p))
    return pl.pallas_call(
        body, name="fox_bwd", grid=(NPAIR, NSTEP),
        in_specs=[qa, qb, ba, bb, ba, bb, ba, bb,
                  pl.BlockSpec((T, C), lambda p, s: (0, KB_F + p)),
                  pl.BlockSpec((T, C), lambda p, s: (0, VB_F + p)),
                  pl.BlockSpec((NCH, FH, C), lambda p, s: (0, 0, 0)),
                  pl.BlockSpec((2 * C, C), lambda p, s: (0, 0)),
                  pl.BlockSpec((2, C, 2 * C), lambda p, s: (0, 0, 0))],
        out_specs=[both, both, col, col, col],
        out_shape=[jax.ShapeDtypeStruct((TROWS, FH * FD), BF), jax.ShapeDtypeStruct((TROWS, FH * FD), F32),
                   jax.ShapeDtypeStruct((T, FH * FD), BF), jax.ShapeDtypeStruct((T, FH * FD), BF),
                   jax.ShapeDtypeStruct((T, FH * FD), F32)],
        scratch_shapes=[pltpu.VMEM((NCH, 2 * C, C), BF), pltpu.VMEM((NCH, 2 * C, C), BF),
                        pltpu.VMEM((2, C, C), BF), pltpu.VMEM((2, 2 * C, 2 * C), BF), pltpu.VMEM((2, 2 * C, C), BF),
                        pltpu.VMEM((2, C, C), BF), pltpu.VMEM((2, C, 2 * C), F32), pltpu.VMEM((2, C, 2 * C), F32),
                        pltpu.VMEM((2, C, 2 * C), F32),
                        pltpu.VMEM((NTILE, C, C), F32), pltpu.VMEM((NTILE, C, 2 * C), F32),
                        pltpu.VMEM((T, C), F32), pltpu.VMEM((T, C), F32), pltpu.VMEM((T, C), F32)],
        compiler_params=_params(("parallel", "arbitrary")),
    )(z, z, da, da, g, g, delta, delta, z, z, ct, cst["ones_aug"], cst["mask_bias"])


def _fox_gate_bwd(drow, dcol, zf, bf_pad, cst):
    def body(dr_ref, dc_ref, zf_ref, b_ref, tri_ref, pick_ref, dff_ref, db_ref, carry):
        s = pl.program_id(0)
        n = NCH - 1 - s

        @pl.when(s == 0)
        def _():
            carry[...] = jnp.zeros_like(carry)
            db_ref[...] = jnp.zeros_like(db_ref)

        dcb = _split_dot((dr_ref[...] - dc_ref[...]) * (1.0 / FSCALE), pick_ref[...])
        suf = _split_dot(dcb, tri_ref[...], TN, x_first=False) + carry[0:1, :]
        carry[...] = jnp.broadcast_to(suf[0:1, :], carry.shape)
        x = zf_ref[...] + b_ref[...]
        row = n * C + lax.broadcasted_iota(jnp.int32, (C, C), 0)
        dff = jnp.where(row >= PAD, suf * (1.0 - jax.nn.sigmoid(x)), 0.0)
        dff_ref[...] = dff.astype(BF)
        db_ref[...] += jnp.sum(dff, axis=0, keepdims=True)

    rev = lambda s: (_rowblk(NCH - 1 - s), 0)
    return pl.pallas_call(
        body, name="fox_gate_bwd", grid=(NCH,),
        in_specs=[pl.BlockSpec((C, FH * FD), lambda s: (_fox_pos(NCH - 1 - s), 0)),
                  pl.BlockSpec((C, FH * FD), rev), pl.BlockSpec((C, C), rev),
                  pl.BlockSpec((1, C), lambda s: (0, 0)), pl.BlockSpec((C, C), lambda s: (0, 0)),
                  pl.BlockSpec((FH * FD, C), lambda s: (0, 0))],
        out_specs=[pl.BlockSpec((C, C), rev), pl.BlockSpec((1, C), lambda s: (0, 0))],
        out_shape=[jax.ShapeDtypeStruct((T, C), BF), jax.ShapeDtypeStruct((1, C), F32)],
        scratch_shapes=[pltpu.VMEM((8, C), F32)],
        compiler_params=_params(("arbitrary",)),
    )(drow, dcol, zf, bf_pad, cst["tri"], cst["pick"])


def _gated(r, rg, a, fg):
    rn, rs = [], []
    for h in range(RH):
        rh = r[:, RDV * h:RDV * (h + 1)]
        s = lax.rsqrt(jnp.mean(rh * rh, axis=1, keepdims=True) + EPS)
        rn.append(rh * s)
        rs.append(s)
    rn = jnp.concatenate(rn, axis=1)
    y = jnp.concatenate([rn * (rg * jax.nn.sigmoid(rg)), a * (fg * jax.nn.sigmoid(fg))], axis=1)
    return y, rn, rs


def _out_loss(r, z, a, wout, x, tgt, fgain):
    def body(r_ref, rg_ref, a_ref, fg_ref, w_ref, x_ref, t_ref, g_ref, yt_ref, do_ref, dob_ref, loss_ref, dg_ref):
        i = pl.program_id(0)

        @pl.when(i == 0)
        def _():
            yt_ref[...] = jnp.zeros_like(yt_ref)
            do_ref[...] = jnp.zeros_like(do_ref)
            dob_ref[...] = jnp.zeros_like(dob_ref)
            loss_ref[...] = jnp.zeros_like(loss_ref)
            dg_ref[...] = jnp.zeros_like(dg_ref)

        @pl.when(i > 0)
        def _():
            y, _, _ = _gated(r_ref[...], rg_ref[...], a_ref[...], fg_ref[...])
            yt_ref[...] = y.T.astype(BF)
            o = x_ref[...] + _dot(y.astype(BF), w_ref[...])
            rs = lax.rsqrt(jnp.mean(o * o, axis=1, keepdims=True) + EPS)
            on = o * rs
            g = g_ref[...]
            e = on * g - t_ref[...]
            loss_ref[...] += 0.5 * jnp.sum(jnp.mean(e * e, axis=1, keepdims=True))
            dyh = e * (1.0 / D)
            dg_ref[...] += jnp.sum(dyh * on, axis=0, keepdims=True)
            don = dyh * g
            do = rs * (don - on * jnp.mean(don * on, axis=1, keepdims=True))
            do_ref[...] = do
            dob_ref[...] = do.astype(BF)

    tok = lambda i: (jnp.maximum(i - 1, 0), 0)
    return pl.pallas_call(
        body, name="out_loss", grid=(NCH,),
        in_specs=[pl.BlockSpec((C, D), lambda i: (_rowblk(i), 0)), pl.BlockSpec((C, D), lambda i: (_rowblk(i), GB_R)),
                  pl.BlockSpec((C, D), lambda i: (_fox_pos(i), 0)), pl.BlockSpec((C, D), lambda i: (_rowblk(i), GB_F)),
                  pl.BlockSpec((DMIX, D), lambda i: (0, 0)),
                  pl.BlockSpec((C, D), tok), pl.BlockSpec((C, D), tok), pl.BlockSpec((1, D), lambda i: (0, 0))],
        out_specs=[pl.BlockSpec((DMIX, C), lambda i: (0, _rowblk(i))), pl.BlockSpec((C, D), lambda i: (_rowblk(i), 0)),
                   pl.BlockSpec((C, D), lambda i: (_rowblk(i), 0)), pl.BlockSpec((8, C), lambda i: (0, 0)),
                   pl.BlockSpec((1, D), lambda i: (0, 0))],
        out_shape=[jax.ShapeDtypeStruct((DMIX, T), BF), jax.ShapeDtypeStruct((T, D), F32),
                   jax.ShapeDtypeStruct((T, D), BF), jax.ShapeDtypeStruct((8, C), F32),
                   jax.ShapeDtypeStruct((1, D), F32)],
        compiler_params=_params(("arbitrary",)),
    )(r, z, a, z, wout, x, tgt, fgain)


def _dsilu(x):
    s = jax.nn.sigmoid(x)
    return s * (1.0 + x * (1.0 - s))


def _dy_gate_bwd(dob, wout, r, z, a, seg):
    def body(do_ref, w_ref, r_ref, rg_ref, a_ref, fg_ref, seg_ref, dr_ref, da_ref, drg_ref, dfg_ref, dl_ref):
        dy = _dg(do_ref[...], w_ref[...], NT)
        rg, fg, a_ = rg_ref[...], fg_ref[...], a_ref[...]
        _, rn, rs = _gated(r_ref[...], rg, a_, fg)
        dyr, dyf = dy[:, :D], dy[:, D:]
        drn = dyr * (rg * jax.nn.sigmoid(rg))
        drg_ref[...] = (dyr * rn * _dsilu(rg)).astype(BF)
        for h in range(RH):
            sl = slice(RDV * h, RDV * (h + 1))
            dh, nh = drn[:, sl], rn[:, sl]
            dr_ref[:, sl] = (rs[h] * (dh - nh * jnp.mean(dh * nh, axis=1, keepdims=True))).astype(BF)
        dab = (dyf * (fg * jax.nn.sigmoid(fg))).astype(BF)
        da_ref[...] = dab
        dfg_ref[...] = (dyf * a_ * _dsilu(fg)).astype(BF)
        prod = dab.astype(F32) * a_
        segm = seg_ref[...]
        for p in range(NPAIR):
            sl = slice(C * p, C * (p + 1))
            hi = prod[:, sl].astype(BF)
            lo = (prod[:, sl] - hi.astype(F32)).astype(BF)
            dl_ref[:, sl] = _dot(hi, segm) + _dot(lo, segm)

    row = pl.BlockSpec((C, D), lambda i: (_rowblk(i), 0))
    fox = pl.BlockSpec((C, D), lambda i: (_fox_pos(i), 0))
    return pl.pallas_call(
        body, name="dy_gate_bwd", grid=(NCH,),
        in_specs=[row, pl.BlockSpec((DMIX, D), lambda i: (0, 0)),
                  row, pl.BlockSpec((C, D), lambda i: (_rowblk(i), GB_R)),
                  fox, pl.BlockSpec((C, D), lambda i: (_rowblk(i), GB_F)),
                  pl.BlockSpec((C, C), lambda i: (0, 0))],
        out_specs=[row, fox, row, row, fox],
        out_shape=[jax.ShapeDtypeStruct((T, D), BF), jax.ShapeDtypeStruct((TROWS, D), BF),
                   jax.ShapeDtypeStruct((T, D), BF), jax.ShapeDtypeStruct((T, D), BF),
                   jax.ShapeDtypeStruct((TROWS, D), F32)],
        compiler_params=_params(("parallel",)),
    )(dob, wout, r, z, a, z, seg)


DZ_WIDTHS = (512, 512, 1024, 1024, 1024, 1024, 1024, 1024)


def _du_norm_bwd(dzs, dzf, wt, wft, x, metap, g, dopad, parts=()):
    tm, tk = 512, 1024
    nk = WMAIN // tk
    ni = pl.cdiv(T, tm)
    n = len(parts)

    def body(*refs):
        tok_in, lead_in = refs[:10], refs[10:20]
        w_ref, wf_ref, x_ref, m_ref, g_ref = refs[20:25]
        rest = refs[25:]
        part_refs, (gx_ref, gm_ref, dg_ref), land_refs = rest[:n], rest[n:n + 3], rest[n + 3:2 * n + 3]
        acc = rest[2 * n + 3]
        i, k = pl.program_id(0), pl.program_id(1)

        if n:
            send_sems, recv_sems = rest[2 * n + 4:]
            copies = _chip_copies(part_refs, land_refs, send_sems, recv_sems, by_dest=True)

            @pl.when((i == 0) & (k == 0))
            def _():
                for cp in copies:
                    cp.start()

            @pl.when((i == ni - 1) & (k == nk - 1))
            def _():
                for cp in copies:
                    cp.wait()

        def accumulate(rows, ins):
            rq_ref, rk_ref, *wide, dzf_ref = ins[:9]

            @pl.when(k == 0)
            def _():
                acc[rows, :] = (_dot(dzf_ref[...], wf_ref[...]) + _dot(rq_ref[...], w_ref[:512, :])
                                + _dot(rk_ref[...], w_ref[512:, :]))

            for kk, piece in enumerate(wide, start=1):
                @pl.when(k == kk)
                def _(piece=piece):
                    acc[rows, :] += _dot(piece[...], w_ref[...])

        def norm_bwd(du, h, do):
            rs = lax.rsqrt(jnp.mean(h * h, axis=1, keepdims=True) + EPS)
            hn = h * rs
            dhn = du * g_ref[...]
            dh = rs * (dhn - hn * jnp.mean(dhn * hn, axis=1, keepdims=True)) + do
            return dh, jnp.sum(du * hn, axis=0, keepdims=True)

        @pl.when(i < ni - 1)
        def _():
            accumulate(slice(None), tok_in)

            @pl.when(k == nk - 1)
            def _():
                gx_ref[...], part = norm_bwd(acc[...], x_ref[...], tok_in[9][...])

                @pl.when(i == 0)
                def _():
                    dg_ref[...] = part

                @pl.when(i > 0)
                def _():
                    dg_ref[...] += part

        @pl.when(i == ni - 1)
        def _():
            accumulate(slice(0, C), lead_in)

            @pl.when(k == nk - 1)
            def _():
                gm_ref[...], part = norm_bwd(acc[:C, :], m_ref[...], lead_in[9][...])
                dg_ref[...] += part

    sems = [pltpu.SemaphoreType.DMA((3 * n,)), pltpu.SemaphoreType.DMA((3 * n,))] if n else []
    widths = DZ_WIDTHS + (C, D)
    tok_specs = [pl.BlockSpec((tm, w), lambda i, k: (jnp.minimum(i, ni - 2), 0)) for w in widths]
    lead_specs = [pl.BlockSpec((C, w), lambda i, k: (NCH - 1, 0)) for w in widths]
    by_rows = dzs + [dzf, dopad]
    return pl.pallas_call(
        body, name="du_norm_bwd", grid=(ni, nk),
        in_specs=tok_specs + lead_specs
        + [pl.BlockSpec((tk, D), lambda i, k: (k, 0)), pl.BlockSpec((C, D), lambda i, k: (0, 0)),
           tok_specs[-1], pl.BlockSpec((C, D), lambda i, k: (0, 0)), pl.BlockSpec((1, D), lambda i, k: (0, 0))]
        + [ANY] * n,
        out_specs=[tok_specs[-1], pl.BlockSpec((C, D), lambda i, k: (0, 0)),
                   pl.BlockSpec((1, D), lambda i, k: (0, 0))] + [ANY] * n,
        out_shape=[jax.ShapeDtypeStruct((SEQ, D), F32), jax.ShapeDtypeStruct((C, D), F32),
                   jax.ShapeDtypeStruct((1, D), F32)] + [jax.ShapeDtypeStruct(p.shape, p.dtype) for p in parts],
        scratch_shapes=[pltpu.VMEM((tm, D), F32)] + sems,
        compiler_params=_params(("arbitrary", "arbitrary")),
    )(*by_rows, *by_rows, wt, wft, x, metap, g, *parts)


GROWS = 7680


def _dw_in(dzs, dzf, ut):
    tn = 512
    nmain = WMAIN // tn
    first, blocks = [], []
    for w in DZ_WIDTHS:
        first.append(sum(blocks))
        blocks.append(w // tn)

    def body(rq_ref, rk_ref, rv_ref, rg_ref, fq_ref, fk_ref, fv_ref, fg_ref, dzf_ref, ut_ref, o_ref):
        gidx = pl.program_id(0)
        for piece, g0, nb in zip((rq_ref, rk_ref, rv_ref, rg_ref, fq_ref, fk_ref, fv_ref, fg_ref), first, blocks):
            @pl.when((gidx >= g0) & (gidx < g0 + nb))
            def _(piece=piece):
                o_ref[...] = _dot(ut_ref[...], piece[...]).T.astype(BF)

        @pl.when(gidx == nmain)
        def _():
            o_ref[:C, :] = _dot(ut_ref[...], dzf_ref[...]).T.astype(BF)
            o_ref[C:, :] = jnp.zeros((tn - C, D), BF)

    def piece_spec(g0, nb):
        return pl.BlockSpec((T, tn), lambda gidx: (0, jnp.clip(gidx - g0, 0, nb - 1)))

    return pl.pallas_call(
        body, name="dw_in", grid=(nmain + 1,),
        in_specs=[piece_spec(g0, nb) for g0, nb in zip(first, blocks)]
        + [pl.BlockSpec((T, C), lambda gidx: (0, 0)), pl.BlockSpec((D, T), lambda gidx: (0, 0))],
        out_specs=pl.BlockSpec((tn, D), lambda gidx: (gidx, 0)),
        out_shape=jax.ShapeDtypeStruct((GROWS, D), BF),
        compiler_params=pltpu.CompilerParams(dimension_semantics=("arbitrary",), vmem_limit_bytes=DW_VMEM_LIMIT),
    )(*dzs, dzf, ut)


def _token_order(x_po):
    def body(i_ref, o_ref):
        o_ref[...] = i_ref[...]

    return pl.pallas_call(
        body, name="token_order", grid=(NCH,),
        in_specs=[pl.BlockSpec((C, D), lambda i: (_fox_pos(i), 0))],
        out_specs=pl.BlockSpec((C, D), lambda i: (_rowblk(i), 0)),
        out_shape=jax.ShapeDtypeStruct((T, D), x_po.dtype),
        compiler_params=_params(("parallel",)),
    )(x_po)


def _local_step(x, tgt, meta, norm_g, wt, wft, b_f, wout, final_g, chip_sums=None, wout_full=None):
    cst = _constants()
    metap = jnp.pad(meta, ((PAD, 0), (0, 0)))
    bf_pad = jnp.pad(b_f, ((0, 0), (0, C - NFF)))
    u, ut = _norm_in(x, metap, norm_g)
    z = _mm_nt(u, wt, WMAIN, T // 2, 512, "in_proj")
    zf = _mm_nt(u, wft, C, T // 2, C, "in_proj_ff")
    r, sprev = _ret_fwd(z, cst)
    ct = _fox_prep(zf, bf_pad, cst)
    if wout_full is None:
        a, g = _fox_fwd(z, ct, cst, None)
    else:
        a, g, landed_wout = _fox_fwd(z, ct, cst, wout)
        wout = wout_full(landed_wout)
    yt, dopad, dob, loss8, dfg = _out_loss(r, z, a, wout, x, tgt, final_g)
    dr, da, dzrg, dzfg, delta = _dy_gate_bwd(dob, wout, r, z, a, cst["seg"])
    dwout = _mm_nn(yt, dob, 512, D, "dw_out", BF)
    dzq_r, dzk_r, dzv_r = _ret_bwd(z, cst, sprev, dr)
    dq_po, drow, dzk_f, dzv_f, dcol = _fox_bwd(z, da, g, delta, ct, cst)
    dzf, dbf = _fox_gate_bwd(drow, dcol, zf, bf_pad, cst)
    dzs = [dzq_r, dzk_r, dzv_r, dzrg, _token_order(dq_po), dzk_f, dzv_f, dzfg]
    gwt = _dw_in(dzs, dzf, ut)
    parts = chip_sums(gwt, dwout) if chip_sums else []
    gx, gm, dng, *landed = _du_norm_bwd(dzs, dzf, wt, wft, x, metap, norm_g, dopad, parts)
    return (loss8[0, 0], gx, gm[PAD:], dng, gwt, dbf[:, :NFF], dwout, dfg, parts, landed)


def _all_gather_shards(shards):
    n = len(shards)

    def body(*refs):
        ins, outs = refs[:n], refs[n:2 * n]
        send_sems, recv_sems = refs[2 * n:]
        x, y, c = _place()
        me_s = 2 * x + y
        sib = (x, y, 1 - c)
        chips = _other_chips(x, y)
        sends, waits = [], []
        for a in range(n):
            rows = ins[a].shape[0] // 2
            half = pl.ds(c * rows, rows)
            for k, (cx, cy, cs) in enumerate(chips):
                sends.append(pltpu.make_async_remote_copy(
                    src_ref=ins[a].at[half], dst_ref=outs[a].at[me_s, half],
                    send_sem=send_sems.at[6 * a + k], recv_sem=recv_sems.at[6 * a + k],
                    device_id=(cx, cy, c), device_id_type=MESH))
                sends[-1].start()
        for a in range(n):
            rows = ins[a].shape[0] // 2
            half = pl.ds(c * rows, rows)
            other = pl.ds((1 - c) * rows, rows)
            for k, (cx, cy, cs) in enumerate(chips):
                pltpu.make_async_remote_copy(
                    src_ref=outs[a].at[cs, half], dst_ref=outs[a].at[cs, half],
                    send_sem=send_sems.at[6 * a + k], recv_sem=recv_sems.at[6 * a + k],
                    device_id=(cx, cy, c), device_id_type=MESH).wait_recv()
                fwd = pltpu.make_async_remote_copy(
                    src_ref=outs[a].at[cs, half], dst_ref=outs[a].at[cs, half],
                    send_sem=send_sems.at[6 * a + 3 + k], recv_sem=recv_sems.at[6 * a + 3 + k],
                    device_id=sib, device_id_type=MESH)
                fwd.start()
                sends.append(fwd)
                waits.append(pltpu.make_async_remote_copy(
                    src_ref=outs[a].at[cs, other], dst_ref=outs[a].at[cs, other],
                    send_sem=send_sems.at[6 * a + 3 + k], recv_sem=recv_sems.at[6 * a + 3 + k],
                    device_id=sib, device_id_type=MESH))
        for w in waits:
            w.wait_recv()
        for s in sends:
            s.wait_send()

    return pl.pallas_call(
        body, name="all_gather_w",
        in_specs=[ANY] * n, out_specs=[ANY] * n,
        out_shape=[jax.ShapeDtypeStruct((4,) + s.shape, s.dtype) for s in shards],
        scratch_shapes=[pltpu.SemaphoreType.DMA((6 * n,)), pltpu.SemaphoreType.DMA((6 * n,))],
    )(*shards)


WOFF, WLEN = 1792, 2048
WHALF = WLEN // 2


def _pair_swap(gwt, arrs):
    n = len(arrs)

    def body(*refs):
        gw, ins = refs[0], refs[1:n + 1]
        gwo, outs = refs[n + 1], refs[n + 2:2 * n + 2]
        send_sems, recv_sems = refs[2 * n + 2:]
        x, y, c = _place()
        sib = (x, y, 1 - c)
        cps = []
        for k in range(4):
            cps.append(pltpu.make_async_remote_copy(
                src_ref=gw.at[pl.ds(WOFF * k + (1 - c) * WHALF, WHALF)], dst_ref=gwo.at[k],
                send_sem=send_sems.at[k], recv_sem=recv_sems.at[k], device_id=sib, device_id_type=MESH))
        for a in range(n):
            rows = ins[a].shape[1] // 2
            cps.append(pltpu.make_async_remote_copy(
                src_ref=ins[a].at[:, pl.ds((1 - c) * rows, rows)], dst_ref=outs[a],
                send_sem=send_sems.at[4 + a], recv_sem=recv_sems.at[4 + a], device_id=sib, device_id_type=MESH))
        for cp in cps:
            cp.start()
        for cp in cps:
            cp.wait()

    return pl.pallas_call(
        body, name="rs_pair_swap",
        in_specs=[ANY] * (n + 1), out_specs=[ANY] * (n + 1),
        out_shape=[jax.ShapeDtypeStruct((4, WHALF, D), gwt.dtype)]
        + [jax.ShapeDtypeStruct((4, a.shape[1] // 2, a.shape[2]), a.dtype) for a in arrs],
        scratch_shapes=[pltpu.SemaphoreType.DMA((n + 4,)), pltpu.SemaphoreType.DMA((n + 4,))],
    )(gwt, *arrs)


def _add_windows(gwt, recv):
    tb = 256
    nb = WHALF // tb
    c = lax.axis_index("c")

    def body(c_ref, a_ref, b_ref, o_ref):
        o_ref[0] = (a_ref[...].astype(F32) + b_ref[0].astype(F32)).astype(BF)

    return pl.pallas_call(
        body, name="pair_add_in",
        grid_spec=pltpu.PrefetchScalarGridSpec(
            num_scalar_prefetch=1, grid=(4, nb),
            in_specs=[pl.BlockSpec((tb, D), lambda k, i, cr: ((WOFF // tb) * k + nb * cr[0] + i, 0)),
                      pl.BlockSpec((1, tb, D), lambda k, i, cr: (k, i, 0))],
            out_specs=pl.BlockSpec((1, tb, D), lambda k, i, cr: (k, i, 0))),
        out_shape=jax.ShapeDtypeStruct(recv.shape, BF),
        compiler_params=_params(("parallel", "parallel")),
    )(jnp.reshape(c, (1,)).astype(jnp.int32), gwt, recv)


def _chip_exchange(parts, small):
    n = len(parts)

    def body(*refs):
        ins, sm = refs[:n], refs[n]
        outs, smo = refs[n + 1:2 * n + 1], refs[2 * n + 1]
        send_sems, recv_sems = refs[2 * n + 2:]
        cps = _chip_copies(ins, outs, send_sems, recv_sems, by_dest=True)
        cps += _chip_copies([sm], [smo], send_sems.at[pl.ds(3 * n, 3)], recv_sems.at[pl.ds(3 * n, 3)], by_dest=False)
        for cp in cps:
            cp.start()
        for cp in cps:
            cp.wait()

    return pl.pallas_call(
        body, name="rs_chip_exchange",
        in_specs=[ANY] * (n + 1), out_specs=[ANY] * (n + 1),
        out_shape=[jax.ShapeDtypeStruct(p.shape, p.dtype) for p in parts]
        + [jax.ShapeDtypeStruct((4,) + small.shape, small.dtype)],
        scratch_shapes=[pltpu.SemaphoreType.DMA((3 * (n + 1),)), pltpu.SemaphoreType.DMA((3 * (n + 1),))],
    )(*parts, small)


def _pair_send(halves):
    n = len(halves)

    def body(*refs):
        ins, outs = refs[:n], refs[n:2 * n]
        send_sems, recv_sems = refs[2 * n:]
        x, y, c = _place()
        cps = [pltpu.make_async_remote_copy(
            src_ref=ins[a], dst_ref=outs[a], send_sem=send_sems.at[a], recv_sem=recv_sems.at[a],
            device_id=(x, y, 1 - c), device_id_type=MESH) for a in range(n)]
        for cp in cps:
            cp.start()
        for cp in cps:
            cp.wait()

    return pl.pallas_call(
        body, name="rs_pair_send",
        in_specs=[ANY] * n, out_specs=[ANY] * n,
        out_shape=[jax.ShapeDtypeStruct(h.shape, h.dtype) for h in halves],
        scratch_shapes=[pltpu.SemaphoreType.DMA((n,)), pltpu.SemaphoreType.DMA((n,))],
    )(*halves)


def _row_block(rows):
    for tb in (256, 128, 64, 32, 16, 8):
        if rows % tb == 0:
            return tb
    return rows


def _add_halves(full, recv, name, out_dtype):
    _, r2, w = recv.shape
    tb = _row_block(r2)
    nb = r2 // tb
    c = lax.axis_index("c")

    def body(c_ref, a_ref, b_ref, o_ref):
        o_ref[...] = (a_ref[...].astype(F32) + b_ref[...].astype(F32)).astype(o_ref.dtype)

    return pl.pallas_call(
        body, name=name,
        grid_spec=pltpu.PrefetchScalarGridSpec(
            num_scalar_prefetch=1, grid=(4, nb),
            in_specs=[pl.BlockSpec((1, tb, w), lambda s, i, cr: (s, cr[0] * nb + i, 0)),
                      pl.BlockSpec((1, tb, w), lambda s, i, cr: (s, i, 0))],
            out_specs=pl.BlockSpec((1, tb, w), lambda s, i, cr: (s, i, 0))),
        out_shape=jax.ShapeDtypeStruct(recv.shape, out_dtype),
        compiler_params=_params(("parallel", "parallel")),
    )(jnp.reshape(c, (1,)).astype(jnp.int32), full, recv)


def _add2(a, b, name):
    def body(a_ref, b_ref, o_ref):
        o_ref[...] = a_ref[...] + b_ref[...]

    return pl.pallas_call(body, name=name, out_shape=jax.ShapeDtypeStruct(a.shape, a.dtype))(a, b)


def _sum4(buf, own, name):
    _, r, w = buf.shape
    tb = _row_block(r)
    me_s = 2 * lax.axis_index("x") + lax.axis_index("y")
    by_dest = own.ndim == 3

    def body(s_ref, b_ref, own_ref, o_ref):
        mine = (own_ref[0] if by_dest else own_ref[...]).astype(F32)
        terms = [jnp.where(s_ref[0] == t, mine, b_ref[t].astype(F32)) for t in range(4)]
        o_ref[...] = ((terms[0] + terms[1]) + terms[2]) + terms[3]

    own_spec = (pl.BlockSpec((1, tb, w), lambda i, sr: (sr[0], i, 0)) if by_dest
                else pl.BlockSpec((tb, w), lambda i, sr: (i, 0)))
    return pl.pallas_call(
        body, name=name,
        grid_spec=pltpu.PrefetchScalarGridSpec(
            num_scalar_prefetch=1, grid=(r // tb,),
            in_specs=[pl.BlockSpec((4, tb, w), lambda i, sr: (0, i, 0)), own_spec],
            out_specs=pl.BlockSpec((tb, w), lambda i, sr: (i, 0))),
        out_shape=jax.ShapeDtypeStruct((r, w), F32),
        compiler_params=_params(("parallel",)),
    )(jnp.reshape(me_s, (1,)).astype(jnp.int32), buf, own)


def _adamw_math(w, g, m, v):
    mn = B1 * m + (1.0 - B1) * g
    vn = B2 * v + (1.0 - B2) * (g * g)
    m_hat = mn / (1.0 - B1 ** STEP)
    v_hat = vn / (1.0 - B2 ** STEP)
    return -LR * (m_hat / (jnp.sqrt(v_hat) + AEPS) + WD * w), mn, vn


def _adamw(w, g, m, v, name):
    r, c_ = w.shape
    tb = _row_block(r)
    if tb == r and r > 512:
        tb = 256

    def body(w_ref, g_ref, m_ref, v_ref, d_ref, mo_ref, vo_ref):
        d_ref[...], mo_ref[...], vo_ref[...] = _adamw_math(w_ref[...], g_ref[...], m_ref[...], v_ref[...])

    spec = pl.BlockSpec((tb, c_), lambda i: (i, 0))
    return pl.pallas_call(
        body, name=name, grid=(pl.cdiv(r, tb),),
        in_specs=[spec] * 4, out_specs=[spec] * 3,
        out_shape=[jax.ShapeDtypeStruct(w.shape, F32)] * 3,
        compiler_params=_params(("parallel",)),
    )(w, g, m, v)


def _adamw_halves(w, g_mine, g_sib, m, v, name):
    r, c_ = w.shape
    r2 = g_mine.shape[0]
    tb = _row_block(r2)
    nb = r2 // tb
    c = lax.axis_index("c")

    def body(c_ref, w_ref, gm_ref, gs_ref, m_ref, v_ref, g_ref, d_ref, mo_ref, vo_ref):
        g = jnp.where(pl.program_id(0) == c_ref[0], gm_ref[...], gs_ref[...])
        g_ref[...] = g
        d_ref[...], mo_ref[...], vo_ref[...] = _adamw_math(w_ref[...], g, m_ref[...], v_ref[...])

    full = pl.BlockSpec((tb, c_), lambda h, i, cr: (h * nb + i, 0))
    half = pl.BlockSpec((tb, c_), lambda h, i, cr: (i, 0))
    return pl.pallas_call(
        body, name=name,
        grid_spec=pltpu.PrefetchScalarGridSpec(
            num_scalar_prefetch=1, grid=(2, nb),
            in_specs=[full, half, half, full, full], out_specs=[full] * 4),
        out_shape=[jax.ShapeDtypeStruct(w.shape, F32)] * 4,
        compiler_params=_params(("parallel", "parallel")),
    )(jnp.reshape(c, (1,)).astype(jnp.int32), w, g_mine, g_sib, m, v)


def kernel(x, meta_tokens, norm_g, w_in, b_f, w_out, final_g, loss_target, m_meta_tokens, m_norm_g, m_w_in, m_b_f, m_w_out, m_final_g, v_meta_tokens, v_norm_g, v_w_in, v_b_f, v_w_out, v_final_g):
    me_s = 2 * lax.axis_index("x") + lax.axis_index("y")
    core = lax.axis_index("c")
    wt, mt, vt = [jnp.swapaxes(t[0], 0, 1) for t in (w_in, m_w_in, v_w_in)]

    own_win = lax.dynamic_update_slice(jnp.zeros((WPADROWS, D), F32), wt, (4 * me_s, 0)).astype(BF)
    own = [own_win, meta_tokens]
    gathered = _all_gather_shards(own)
    mine = (jnp.arange(4) == me_s)[:, None, None]
    win, gmeta = [jnp.where(mine, o[None], g) for o, g in zip(own, gathered)]
    wout_own = w_out[0].astype(BF)
    lap = WPADROWS - WOFF
    tails = jnp.concatenate([jnp.zeros((1, lap, D), BF), win[:-1, WOFF:]], axis=0)
    wt_main = jnp.concatenate([win[:, :lap] + tails, win[:, lap:WOFF]], axis=1).reshape(WMAIN, D)
    wft = jnp.pad(win[3, WOFF:WOFF + NFF], ((0, C - NFF), (0, 0)))
    meta = jnp.concatenate([gmeta[s] for s in range(4)], axis=1)

    def wout_full(landed):
        return jnp.where(mine, wout_own[None], landed).reshape(DMIX, D)

    def chip_sums(gwt, dwout):
        g_out = dwout.reshape(4, DMIX // 4, D)
        r_in, r_out = _pair_swap(gwt, [g_out])
        return [_add_windows(gwt, r_in), _add_halves(g_out, r_out, "pair_add_out", BF)]

    loss, gx, dmeta, dng, gwt, dbf, dwout, dfg, (p_in, p_out), (e_in, e_out) = _local_step(
        x[0], loss_target[0], meta, norm_g, wt_main, wft, b_f, wout_own, final_g.reshape(1, D), chip_sums, wout_full)

    g_meta = jnp.stack([dmeta[:, 256 * s:256 * (s + 1)] for s in range(4)])
    small = jnp.concatenate([dng, dfg, jnp.pad(dbf, ((0, 0), (0, D - NFF))),
                             jnp.pad(jnp.reshape(loss, (1, 1)), ((0, 0), (0, D - 1))),
                             jnp.zeros((4, D), F32)], axis=0)
    e_meta, e_small = _chip_exchange([g_meta], small)
    h_in, h_out = _sum4(e_in, p_in, "sum_in"), _sum4(e_out, p_out, "sum_out")
    h_meta, h_small = _sum4(e_meta, g_meta, "sum_meta"), _sum4(e_small, small, "sum_small")
    s_in, s_out, s_meta, s_small = _pair_send([h_in, h_out, h_meta, h_small])
    gw_meta = _add2(h_meta, s_meta, "pair_add_meta")
    tot = _add2(h_small, s_small, "pair_add_small")
    g_norm, g_final, g_bf, loss_all = tot[0:1], tot[1], tot[2:3, :NFF], tot[3, 0]

    d_meta, nm_meta, nv_meta = _adamw(meta_tokens, gw_meta, m_meta_tokens, v_meta_tokens, "adamw_meta")
    d_norm, nm_norm, nv_norm = _adamw(norm_g, g_norm, m_norm_g, v_norm_g, "adamw_norm")
    window = jnp.concatenate([jnp.where(core == 0, h_in, s_in), jnp.where(core == 0, s_in, h_in)], axis=0)
    gwt_own = lax.dynamic_slice(window, (4 * me_s, 0), (WSH, D))
    d_in, nm_in, nv_in = _adamw(wt, gwt_own, mt, vt, "adamw_in")
    gw_in, d_in, nm_in, nv_in = [jnp.swapaxes(t, 0, 1)[None] for t in (gwt_own, d_in, nm_in, nv_in)]
    d_bf, nm_bf, nv_bf = _adamw(b_f, g_bf, m_b_f, v_b_f, "adamw_bf")
    gw_out, d_out, nm_out, nv_out = _adamw_halves(w_out[0], h_out, s_out, m_w_out[0], v_w_out[0], "adamw_out")
    d_fin, nm_fin, nv_fin = _adamw(final_g.reshape(1, D), g_final.reshape(1, D), m_final_g.reshape(1, D),
                                   v_final_g.reshape(1, D), "adamw_final")
    return (loss_all, gx[None], gw_meta, g_norm, gw_in, g_bf, gw_out[None], g_final,
            d_meta, d_norm, d_in, d_bf, d_out[None], d_fin.reshape(D),
            nm_meta, nm_norm, nm_in, nm_bf, nm_out[None], nm_fin.reshape(D),
            nv_meta, nv_norm, nv_in, nv_bf, nv_out[None], nv_fin.reshape(D))
```

```python
import numpy as np
import jax
import jax.numpy as jnp
from jax import lax
from jax.experimental import pallas as pl
from jax.experimental.pallas import tpu as pltpu

D = 1024
SEQ = 2048
NMETA = 16
C = 128
PAD = C - NMETA
T = PAD + NMETA + SEQ
NCH = T // C
RH, RDK, RDV = 4, 128, 256
FH, FD = 16, 64
NPAIR = FH // 2
WMAIN = 7168
NFF = 16
WIN = WMAIN + NFF
WSH = WIN // 4
WPADROWS = 1824
DMIX = 2048
EPS = 1e-6
NEG = -1e30
RSCALE = RDK ** -0.5
FSCALE = FD ** -0.5
ROPE_BASE = 10000.0
LR, B1, B2, AEPS, WD, STEP = 0.001, 0.9, 0.999, 1e-08, 0.01, 10

BF = jnp.bfloat16
F32 = jnp.float32
NT = (((1,), (1,)), ((), ()))
TN = (((0,), (0,)), ((), ()))
NN_DIMS = (((1,), (0,)), ((), ()))
MESH = pl.DeviceIdType.MESH
ANY = pl.BlockSpec(memory_space=pl.ANY)
VMEM_LIMIT = 48 * 1024 * 1024
DW_VMEM_LIMIT = 56 * 1024 * 1024

QB_R, KB_R = 0, 4
VB_R = 4
GB_R, GB_F = 2, 6
QB_F, KB_F, VB_F = 24, 32, 40


def _dot(a, b):
    return jnp.dot(a, b, preferred_element_type=F32)


def _dg(a, b, dims):
    return lax.dot_general(a, b, dims, preferred_element_type=F32)


def _params(sem=None):
    return pltpu.CompilerParams(dimension_semantics=sem, vmem_limit_bytes=VMEM_LIMIT)


def _constants():
    pos = jnp.arange(T, dtype=F32) - PAD
    inv = ROPE_BASE ** (-jnp.arange(0, RDK, 2, dtype=F32) / RDK)
    ang = pos[:, None] * inv[None, :]
    cos, sin = jnp.cos(ang), jnp.sin(ang)
    cos2 = jnp.concatenate([cos, cos], axis=1)
    sin2 = jnp.concatenate([-sin, sin], axis=1)
    log_gamma = jnp.log1p(-jnp.exp2(-5.0 - jnp.arange(RH, dtype=F32)))
    idx = jnp.arange(C, dtype=F32)
    diff = idx[:, None] - idx[None, :]
    dmask = jnp.where(diff[None] >= 0, jnp.exp(log_gamma[:, None, None] * jnp.maximum(diff, 0.0)[None]), 0.0)
    zeta = jnp.exp(log_gamma[:, None] * (C - 1.0 - idx)[None, :])
    xi = jnp.exp(log_gamma[:, None] * (idx + 1.0)[None, :])
    gdec = jnp.exp(log_gamma * C)
    zeta_b = jnp.broadcast_to(zeta[:, :, None], (RH, C, RDK))
    xi_b = jnp.broadcast_to(xi[:, :, None], (RH, C, RDK))
    gdec_b = jnp.broadcast_to(gdec[:, None, None], (RH, RDK, RDV))
    tri = jnp.asarray(np.tril(np.ones((C, C), np.float32)), dtype=BF)
    head_of_lane = np.arange(FH * FD) // FD
    pick = ((np.arange(FH * FD)[:, None] % FD == 0)
            & (head_of_lane[:, None] == np.arange(C)[None, :])).astype(np.float32)
    seg = (np.arange(C)[:, None] // FD == np.arange(C)[None, :] // FD).astype(np.float32)
    ones_aug = np.concatenate([np.tile((np.arange(C) < FD)[None, :], (C, 1)),
                               np.tile((np.arange(C) >= FD)[None, :], (C, 1))], axis=0).astype(np.float32)
    lane = np.arange(2 * C) % C
    causal = np.where(lane[None, :] <= np.arange(C)[:, None], 0.0, NEG).astype(np.float32)
    mask_bias = np.stack([np.zeros((C, 2 * C), np.float32), causal])
    return dict(cos2=cos2, sin2=sin2, dmask=dmask, zeta=zeta_b, xi=xi_b, gdec=gdec_b, tri=tri,
                mask_bias=jnp.asarray(mask_bias), pick=jnp.asarray(pick, dtype=BF), seg=jnp.asarray(seg, dtype=BF),
                ones_aug=jnp.asarray(ones_aug, dtype=BF))


def _norm_in(hpad, g):
    def body(h_ref, g_ref, u_ref, ut_ref):
        h = h_ref[...]
        rs = lax.rsqrt(jnp.mean(h * h, axis=1, keepdims=True) + EPS)
        u = h * rs * g_ref[...]
        u_ref[...] = u.astype(BF)
        ut_ref[...] = u.T.astype(BF)

    return pl.pallas_call(
        body, name="norm_in", grid=(NCH,),
        in_specs=[pl.BlockSpec((C, D), lambda i: (i, 0)), pl.BlockSpec((1, D), lambda i: (0, 0))],
        out_specs=[pl.BlockSpec((C, D), lambda i: (i, 0)), pl.BlockSpec((D, C), lambda i: (0, i))],
        out_shape=[jax.ShapeDtypeStruct((T, D), BF), jax.ShapeDtypeStruct((D, T), BF)],
        compiler_params=_params(("parallel",)),
    )(hpad, g)


def _mm_nt(a, b, n, tm, tn, name):
    m, k = a.shape

    def body(a_ref, b_ref, o_ref):
        o_ref[...] = _dg(a_ref[...], b_ref[...], NT)

    return pl.pallas_call(
        body, name=name, grid=(m // tm, n // tn),
        in_specs=[pl.BlockSpec((tm, k), lambda i, j: (i, 0)), pl.BlockSpec((tn, k), lambda i, j: (j, 0))],
        out_specs=pl.BlockSpec((tm, tn), lambda i, j: (i, j)),
        out_shape=jax.ShapeDtypeStruct((m, n), F32),
        compiler_params=_params(("parallel", "parallel")),
    )(a, b)


def _mm_nn(a, b, tm, tn, name, out_dtype=F32):
    m, k = a.shape
    _, n = b.shape

    def body(a_ref, b_ref, o_ref):
        o_ref[...] = _dot(a_ref[...], b_ref[...]).astype(out_dtype)

    return pl.pallas_call(
        body, name=name, grid=(m // tm, n // tn),
        in_specs=[pl.BlockSpec((tm, k), lambda i, j: (i, 0)), pl.BlockSpec((k, tn), lambda i, j: (0, j))],
        out_specs=pl.BlockSpec((tm, tn), lambda i, j: (i, j)),
        out_shape=jax.ShapeDtypeStruct((m, n), out_dtype),
        compiler_params=_params(("parallel", "parallel")),
    )(a, b)


def _rot(x, cos2, sin2):
    return x * cos2 + pltpu.roll(x, 64, 1) * sin2


def _ret_specs(chunk):
    whole = lambda shape: pl.BlockSpec(shape, lambda n: (0,) * len(shape))
    return [
        pl.BlockSpec((C, RH * RDK), lambda n: (chunk(n), 0)),
        pl.BlockSpec((C, RH * RDK), lambda n: (chunk(n), 1)),
        pl.BlockSpec((C, RH * RDV), lambda n: (chunk(n), 1)),
        pl.BlockSpec((C, RDK), lambda n: (chunk(n), 0)),
        pl.BlockSpec((C, RDK), lambda n: (chunk(n), 0)),
        whole((RH, C, C)), whole((RH, C, RDK)), whole((RH, C, RDK)), whole((RH, RDK, RDV)),
    ]


def _ret_heads(q_ref, k_ref, v_ref, cos, sin):
    qr = [_rot(q_ref[:, RDK * h:RDK * (h + 1)], cos, sin) for h in range(RH)]
    kr = [_rot(k_ref[:, RDK * h:RDK * (h + 1)], cos, sin) * RSCALE for h in range(RH)]
    vb = [v_ref[:, RDV * h:RDV * (h + 1)].astype(BF) for h in range(RH)]
    return qr, kr, [t.astype(BF) for t in qr], [t.astype(BF) for t in kr], vb


def _ret_fwd(z, cst):
    def body(q_ref, k_ref, v_ref, cos_ref, sin_ref, dm_ref, xi_ref, zt_ref, gd_ref, r_ref, sp_ref, st):
        n = pl.program_id(0)

        @pl.when(n == 0)
        def _():
            st[...] = jnp.zeros_like(st)

        hs = range(RH)
        qr, kr, qb, kb, vb = _ret_heads(q_ref, k_ref, v_ref, cos_ref[...], sin_ref[...])
        sd = [(_dg(qb[h], kb[h], NT) * dm_ref[h]).astype(BF) for h in hs]
        state = [st[h] for h in hs]
        qx = [(qr[h] * xi_ref[h]).astype(BF) for h in hs]
        kz = [(kr[h] * zt_ref[h]).astype(BF) for h in hs]
        out = [_dot(sd[h], vb[h]) + _dot(qx[h], state[h].astype(BF)) for h in hs]
        kv = [_dg(kz[h], vb[h], TN) for h in hs]
        for h in hs:
            sp_ref[0, h] = state[h]
            r_ref[:, RDV * h:RDV * (h + 1)] = out[h]
            st[h] = state[h] * gd_ref[h] + kv[h]

    return pl.pallas_call(
        body, name="ret_fwd", grid=(NCH,),
        in_specs=_ret_specs(lambda n: n),
        out_specs=[pl.BlockSpec((C, RH * RDV), lambda n: (n, 0)),
                   pl.BlockSpec((1, RH, RDK, RDV), lambda n: (n, 0, 0, 0))],
        out_shape=[jax.ShapeDtypeStruct((T, RH * RDV), F32), jax.ShapeDtypeStruct((NCH, RH, RDK, RDV), F32)],
        scratch_shapes=[pltpu.VMEM((RH, RDK, RDV), F32)],
        compiler_params=_params(("arbitrary",)),
    )(z, z, z, cst["cos2"], cst["sin2"], cst["dmask"], cst["xi"], cst["zeta"], cst["gdec"])


def _ret_bwd(z, cst, sprev, dr):
    def body(q_ref, k_ref, v_ref, cos_ref, sin_ref, dm_ref, xi_ref, zt_ref, gd_ref, sp_ref, dr_ref,
             dq_ref, dk_ref, dv_ref, gst):
        i = pl.program_id(0)

        @pl.when(i == 0)
        def _():
            gst[...] = jnp.zeros_like(gst)

        hs = range(RH)
        cos, sin = cos_ref[...], sin_ref[...]
        qr, kr, qb, kb, vb = _ret_heads(q_ref, k_ref, v_ref, cos, sin)
        dm = [dm_ref[h] for h in hs]
        xi = [xi_ref[h] for h in hs]
        zt = [zt_ref[h] for h in hs]
        sd = [(_dg(qb[h], kb[h], NT) * dm[h]).astype(BF) for h in hs]
        qx = [(qr[h] * xi[h]).astype(BF) for h in hs]
        kz = [(kr[h] * zt[h]).astype(BF) for h in hs]
        drb = [dr_ref[:, RDV * h:RDV * (h + 1)] for h in hs]
        sb = [sp_ref[0, h].astype(BF) for h in hs]
        g = [gst[h] for h in hs]
        gb = [t.astype(BF) for t in g]
        ds = [(_dg(drb[h], vb[h], NT) * dm[h]).astype(BF) for h in hs]
        dq = [_dot(ds[h], kb[h]) + _dg(drb[h], sb[h], NT) * xi[h] for h in hs]
        dk = [(_dg(ds[h], qb[h], TN) + _dg(vb[h], gb[h], NT) * zt[h]) * RSCALE for h in hs]
        dv = [_dg(sd[h], drb[h], TN) + _dot(kz[h], gb[h]) for h in hs]
        gn = [g[h] * gd_ref[h] + _dg(qx[h], drb[h], TN) for h in hs]
        for h in hs:
            gst[h] = gn[h]
            dq_ref[:, RDK * h:RDK * (h + 1)] = (dq[h] * cos + pltpu.roll(dq[h] * sin, 64, 1)).astype(BF)
            dk_ref[:, RDK * h:RDK * (h + 1)] = (dk[h] * cos + pltpu.roll(dk[h] * sin, 64, 1)).astype(BF)
            dv_ref[:, RDV * h:RDV * (h + 1)] = dv[h].astype(BF)

    rev = lambda n: NCH - 1 - n
    return pl.pallas_call(
        body, name="ret_bwd", grid=(NCH,),
        in_specs=_ret_specs(rev) + [
            pl.BlockSpec((1, RH, RDK, RDV), lambda n: (rev(n), 0, 0, 0)),
            pl.BlockSpec((C, RH * RDV), lambda n: (rev(n), 0)),
        ],
        out_specs=[pl.BlockSpec((C, RH * RDK), lambda n: (rev(n), 0)),
                   pl.BlockSpec((C, RH * RDK), lambda n: (rev(n), 0)),
                   pl.BlockSpec((C, RH * RDV), lambda n: (rev(n), 0))],
        out_shape=[jax.ShapeDtypeStruct((T, RH * RDK), BF), jax.ShapeDtypeStruct((T, RH * RDK), BF),
                   jax.ShapeDtypeStruct((T, RH * RDV), BF)],
        scratch_shapes=[pltpu.VMEM((RH, RDK, RDV), F32)],
        compiler_params=_params(("arbitrary",)),
    )(z, z, z, cst["cos2"], cst["sin2"], cst["dmask"], cst["xi"], cst["zeta"], cst["gdec"], sprev, dr)


def _place():
    x, y, c = lax.axis_index("x"), lax.axis_index("y"), lax.axis_index("c")
    return x, y, c


def _other_chips(x, y):
    return [(1 - x, y, 2 * (1 - x) + y), (x, 1 - y, 2 * x + (1 - y)), (1 - x, 1 - y, 2 * (1 - x) + (1 - y))]


def _chip_copies(srcs, lands, send_sems, recv_sems, by_dest):
    x, y, c = _place()
    me_s = 2 * x + y
    return [pltpu.make_async_remote_copy(
        src_ref=src.at[cs] if by_dest else src, dst_ref=land.at[me_s],
        send_sem=send_sems.at[3 * a + j], recv_sem=recv_sems.at[3 * a + j],
        device_id=(cx, cy, c), device_id_type=MESH)
        for a, (src, land) in enumerate(zip(srcs, lands)) for j, (cx, cy, cs) in enumerate(_other_chips(x, y))]


def _split_dot(x, mat01, dims=NN_DIMS, x_first=True):
    acc, rest = None, x
    for _ in range(3):
        piece = rest.astype(BF)
        part = _dg(piece, mat01, dims) if x_first else _dg(mat01, piece, dims)
        acc = part if acc is None else acc + part
        rest = rest - piece.astype(F32)
    return acc


def _log_sigmoid(x):
    return -(jnp.maximum(-x, 0.0) + jnp.log1p(jnp.exp(-jnp.abs(x))))


def _fox_prep(zf, bf_pad, cst):
    def body(zf_ref, b_ref, tri_ref, ct_ref, carry):
        n = pl.program_id(0)

        @pl.when(n == 0)
        def _():
            carry[...] = jnp.zeros_like(carry)

        ls = _log_sigmoid(zf_ref[...] + b_ref[...])
        row = n * C + lax.broadcasted_iota(jnp.int32, (C, C), 0)
        lf = jnp.where(row >= PAD, ls, 0.0)
        cc = _split_dot(lf, tri_ref[...], x_first=False) + carry[0:1, :]
        carry[...] = jnp.broadcast_to(cc[C - 1:C, :], carry.shape)
        pos = n * C + lax.broadcasted_iota(jnp.int32, (FH, C), 1)
        ct_ref[0] = jnp.where(pos >= PAD, cc.T[:FH, :], -NEG)

    return pl.pallas_call(
        body, name="fox_prep", grid=(NCH,),
        in_specs=[pl.BlockSpec((C, C), lambda n: (n, 0)), pl.BlockSpec((1, C), lambda n: (0, 0)),
                  pl.BlockSpec((C, C), lambda n: (0, 0))],
        out_specs=pl.BlockSpec((1, FH, C), lambda n: (n, 0, 0)),
        out_shape=jax.ShapeDtypeStruct((NCH, FH, C), F32),
        scratch_shapes=[pltpu.VMEM((8, C), F32)],
        compiler_params=_params(("arbitrary",)),
    )(zf, bf_pad, cst["tri"])


def _lo_lanes(shape):
    return lax.broadcasted_iota(jnp.int32, shape, 1) < FD


def _split_heads(x):
    lo = _lo_lanes(x.shape)
    zero = jnp.zeros_like(x)
    return jnp.concatenate([jnp.where(lo, x, zero), jnp.where(lo, zero, x)], axis=0)


def _spread2(x):
    lo = _lo_lanes(x.shape)
    r = pltpu.roll(x, FD, 1)
    return jnp.concatenate([jnp.where(lo, x, r), jnp.where(lo, r, x)], axis=1)


NSTEP = (NCH + 1) // 2
NTILE = NCH + 1
TROWS = T + C


def _fox_tile(s, t):
    second = t > s
    return second.astype(jnp.int32), jnp.where(second, t - s - 1, s - t)


def _fox_pos(i):
    return jnp.where(i < NSTEP, 2 * i, 2 * (NCH - 1 - i) + 1)


FOX_ORDER = [2 * i if i < NSTEP else 2 * (NCH - 1 - i) + 1 for i in range(NCH)]


def _fox_pair_specs():
    first = pl.BlockSpec((C, C), lambda p, s: (2 * s, p))
    second = pl.BlockSpec((C, C), lambda p, s: (jnp.where(s == NSTEP - 1, 2 * s, 2 * s + 1), p))
    both = pl.BlockSpec((2 * C, C), lambda p, s: (s, p))
    return first, second, both


def _fox_q_specs():
    return (pl.BlockSpec((C, C), lambda p, s: (s, QB_F + p)),
            pl.BlockSpec((C, C), lambda p, s: (NCH - 1 - s, QB_F + p)))


def _fox_key_bias(ct_ref, p, j):
    return jnp.concatenate([ct_ref[j, pl.ds(2 * p, 1), :], ct_ref[j, pl.ds(2 * p + 1, 1), :]], axis=1)


def _fox_fwd(z, ct, cst, share):
    n = 0 if share is None else 1

    def body(qa_ref, qb_ref, k_ref, v_ref, ct_ref, ones_ref, mb_ref, *rest):
        share_refs, (a_ref, g_ref), land_refs = rest[:n], rest[n:n + 2], rest[n + 2:2 * n + 2]
        kks, vvs, q2, m2, sbuf = rest[2 * n + 2:2 * n + 7]
        p, s = pl.program_id(0), pl.program_id(1)
        if n:
            copies = _chip_copies(share_refs, land_refs, *rest[2 * n + 7:], by_dest=False)

            @pl.when((p == 0) & (s == 0))
            def _():
                for cp in copies:
                    cp.start()

            @pl.when((p == NPAIR - 1) & (s == NSTEP - 1))
            def _():
                for cp in copies:
                    cp.wait()

        @pl.when(s == 0)
        def _():
            ones = ones_ref[...]

            def prep(j, carry):
                rows = pl.ds(pl.multiple_of(j * C, C), C)
                kks[j] = _split_heads(k_ref[rows, :]).astype(BF)
                vvs[j] = jnp.concatenate([_split_heads(v_ref[rows, :]).astype(BF), ones], axis=1)
                return carry

            lax.fori_loop(0, NCH, prep, 0)

        q2[0] = (qa_ref[...] * FSCALE).astype(BF)
        q2[1] = (qb_ref[...] * FSCALE).astype(BF)

        tiles = [_fox_tile(s, t) for t in range(NTILE)]
        causal = mb_ref[1]
        neg = jnp.full((C, 2 * C), NEG, F32)
        run, first = neg, neg
        for t, (sel, j) in enumerate(tiles):
            st = _dg(q2[sel], kks[j], NT) - _fox_key_bias(ct_ref, p, j)
            if t in (0, NTILE - 1):
                st = st + causal
            sbuf[t] = st
            run = jnp.maximum(jnp.where(t == s + 1, neg, run), st)
            first = jnp.where(t == s, run, first)
        for w, mx in enumerate((first, run)):
            m2[w] = jnp.concatenate(
                [jnp.broadcast_to(jnp.max(mx[:, :C], axis=1, keepdims=True), (C, C)),
                 jnp.broadcast_to(jnp.max(mx[:, C:], axis=1, keepdims=True), (C, C))], axis=1)

        zero = jnp.zeros((C, 2 * C), F32)
        run, first = zero, zero
        for t, (sel, j) in enumerate(tiles):
            run = jnp.where(t == s + 1, zero, run) + _dot(jnp.exp(sbuf[t] - m2[sel]).astype(BF), vvs[j])
            first = jnp.where(t == s, run, first)
        lo = _lo_lanes((C, C))
        for w, res in enumerate((first, run)):
            l = res[:, C:]
            a_ref[C * w:C * (w + 1), :] = res[:, :C] / l
            mw = m2[w]
            g_ref[C * w:C * (w + 1), :] = -(jnp.where(lo, mw[:, :C], mw[:, C:]) + jnp.log(l))

    qa, qb = _fox_q_specs()
    both = _fox_pair_specs()[2]
    return pl.pallas_call(
        body, name="fox_fwd", grid=(NPAIR, NSTEP),
        in_specs=[qa, qb,
                  pl.BlockSpec((T, C), lambda p, s: (0, KB_F + p)),
                  pl.BlockSpec((T, C), lambda p, s: (0, VB_F + p)),
                  pl.BlockSpec((NCH, FH, C), lambda p, s: (0, 0, 0)),
                  pl.BlockSpec((2 * C, C), lambda p, s: (0, 0)),
                  pl.BlockSpec((2, C, 2 * C), lambda p, s: (0, 0, 0))] + [ANY] * n,
        out_specs=[both, both] + [ANY] * n,
        out_shape=[jax.ShapeDtypeStruct((TROWS, FH * FD), F32)] * 2
        + ([jax.ShapeDtypeStruct((4,) + share.shape, share.dtype)] if n else []),
        scratch_shapes=[pltpu.VMEM((NCH, 2 * C, C), BF), pltpu.VMEM((NCH, 2 * C, 2 * C), BF),
                        pltpu.VMEM((2, C, C), BF), pltpu.VMEM((2, C, 2 * C), F32),
                        pltpu.VMEM((NTILE, C, 2 * C), F32)]
        + [pltpu.SemaphoreType.DMA((3,)), pltpu.SemaphoreType.DMA((3,))] * n,
        compiler_params=_params(("arbitrary", "arbitrary")),
    )(z, z, z, z, ct, cst["ones_aug"], cst["mask_bias"], *([share] * n))


def _fox_bwd(z, da, g, delta, ct, cst):
    grp = 9

    def body(qa_ref, qb_ref, daa_ref, dab_ref, ga_ref, gb_ref, dla_ref, dlb_ref, k_ref, v_ref, ct_ref, ones_ref,
             mb_ref, dq_ref, dr_ref, dk_ref, dv_ref, dcs_ref,
             kks, vvs, q2, qq2, dd2, da2, gi2, dl2, dq2, dvb, dkb, dkacc, dvacc, csacc):
        p, s = pl.program_id(0), pl.program_id(1)
        ones = ones_ref[...]

        @pl.when(s == 0)
        def _():
            dkacc[...] = jnp.zeros_like(dkacc)
            dvacc[...] = jnp.zeros_like(dvacc)
            csacc[...] = jnp.zeros_like(csacc)

            def prep(j, carry):
                rows = pl.ds(pl.multiple_of(j * C, C), C)
                kks[j] = _split_heads(k_ref[rows, :]).astype(BF)
                vvs[j] = _split_heads(v_ref[rows, :]).astype(BF)
                return carry

            lax.fori_loop(0, NCH, prep, 0)

        for w, (q_ref, d_ref, g_ref, l_ref) in enumerate(((qa_ref, daa_ref, ga_ref, dla_ref),
                                                          (qb_ref, dab_ref, gb_ref, dlb_ref))):
            qf = q_ref[...]
            q2[w] = (qf * FSCALE).astype(BF)
            qq2[w] = jnp.concatenate([_split_heads(qf).astype(BF), ones], axis=1)
            da2[w] = d_ref[...]
            dd2[w] = _split_heads(d_ref[...].astype(F32)).astype(BF)
            gi2[w] = _spread2(g_ref[...])
            dl2[w] = _spread2(l_ref[...])
        dq2[...] = jnp.zeros_like(dq2)
        zero = jnp.zeros((C, 2 * C), F32)

        def group(gi, carry):
            ts = [gi * grp + u for u in range(grp)]
            tiles = [_fox_tile(s, t) for t in ts]
            kk = [kks[j] for _, j in tiles]
            ss = [_dg(q2[sel], kj, NT) + (gi2[sel] - _fox_key_bias(ct_ref, p, j)) for kj, (sel, j) in zip(kk, tiles)]
            ss[0] = ss[0] + mb_ref[(gi == 0).astype(jnp.int32)]
            ss[-1] = ss[-1] + mb_ref[(gi == 1).astype(jnp.int32)]
            dps = [_dg(da2[sel], vvs[j], NT) for sel, j in tiles]
            pes = [jnp.exp(st) for st in ss]
            dss = [pe * (dp - dl2[sel]) * FSCALE for pe, dp, (sel, _) in zip(pes, dps, tiles)]
            pts = [jnp.concatenate([pe[:, :C].T, pe[:, C:].T], axis=1).astype(BF) for pe in pes]
            dsts = [jnp.concatenate([ds[:, :C].T, ds[:, C:].T], axis=1).astype(BF) for ds in dss]
            dvs = [_dot(pt, dd2[sel]) for pt, (sel, _) in zip(pts, tiles)]
            rs = [_dot(dst, qq2[sel]) for dst, (sel, _) in zip(dsts, tiles)]
            parts = [_dot(ds.astype(BF), jnp.concatenate([kj, ones], axis=1)) for ds, kj in zip(dss, kk)]
            for t, dv, rr in zip(ts, dvs, rs):
                dvb[t] = dv
                dkb[t] = rr
            pa, pb = zero, zero
            for t, part in zip(ts, parts):
                pa = pa + jnp.where(t <= s, part, zero)
                pb = pb + jnp.where(t <= s, zero, part)
            dq2[0] += pa
            dq2[1] += pb
            return carry

        ntile = jnp.where(s == NSTEP - 1, grp, NTILE)
        lax.fori_loop(0, ntile // grp, group, 0)

        def scatter(t, carry):
            _, j = _fox_tile(s, t)
            r = pl.ds(pl.multiple_of(j * C, C), C)
            dvacc[r, :] += dvb[t]
            dkacc[r, :] += dkb[t, :, :C]
            csacc[r, :] += dkb[t, :, C:]
            return carry

        lax.fori_loop(0, ntile, scatter, 0)
        for w in range(2):
            res = dq2[w]
            dq_ref[C * w:C * (w + 1), :] = res[:, :C].astype(BF)
            dr_ref[C * w:C * (w + 1), :] = res[:, C:]

        @pl.when(s == NSTEP - 1)
        def _():
            dk_ref[...] = dkacc[...].astype(BF)
            dv_ref[...] = dvacc[...].astype(BF)
            dcs_ref[...] = csacc[...]

    qa, qb = _fox_q_specs()
    ba, bb, both = _fox_pair_specs()
    col = pl.BlockSpec((T, C), lambda p, s: (0, p))
    return pl.pallas_call(
        body, name="fox_bwd", grid=(NPAIR, NSTEP),
        in_specs=[qa, qb, ba, bb, ba, bb, ba, bb,
                  pl.BlockSpec((T, C), lambda p, s: (0, KB_F + p)),
                  pl.BlockSpec((T, C), lambda p, s: (0, VB_F + p)),
                  pl.BlockSpec((NCH, FH, C), lambda p, s: (0, 0, 0)),
                  pl.BlockSpec((2 * C, C), lambda p, s: (0, 0)),
                  pl.BlockSpec((2, C, 2 * C), lambda p, s: (0, 0, 0))],
        out_specs=[both, both, col, col, col],
        out_shape=[jax.ShapeDtypeStruct((TROWS, FH * FD), BF), jax.ShapeDtypeStruct((TROWS, FH * FD), F32),
                   jax.ShapeDtypeStruct((T, FH * FD), BF), jax.ShapeDtypeStruct((T, FH * FD), BF),
                   jax.ShapeDtypeStruct((T, FH * FD), F32)],
        scratch_shapes=[pltpu.VMEM((NCH, 2 * C, C), BF), pltpu.VMEM((NCH, 2 * C, C), BF),
                        pltpu.VMEM((2, C, C), BF), pltpu.VMEM((2, 2 * C, 2 * C), BF), pltpu.VMEM((2, 2 * C, C), BF),
                        pltpu.VMEM((2, C, C), BF), pltpu.VMEM((2, C, 2 * C), F32), pltpu.VMEM((2, C, 2 * C), F32),
                        pltpu.VMEM((2, C, 2 * C), F32),
                        pltpu.VMEM((NTILE, C, C), F32), pltpu.VMEM((NTILE, C, 2 * C), F32),
                        pltpu.VMEM((T, C), F32), pltpu.VMEM((T, C), F32), pltpu.VMEM((T, C), F32)],
        compiler_params=_params(("parallel", "arbitrary")),
    )(z, z, da, da, g, g, delta, delta, z, z, ct, cst["ones_aug"], cst["mask_bias"])


def _fox_gate_bwd(drow, dcol, zf, bf_pad, cst):
    def body(dr_ref, dc_ref, zf_ref, b_ref, tri_ref, pick_ref, dff_ref, db_ref, carry):
        s = pl.program_id(0)
        n = NCH - 1 - s

        @pl.when(s == 0)
        def _():
            carry[...] = jnp.zeros_like(carry)
            db_ref[...] = jnp.zeros_like(db_ref)

        dcb = _split_dot((dr_ref[...] - dc_ref[...]) * (1.0 / FSCALE), pick_ref[...])
        suf = _split_dot(dcb, tri_ref[...], TN, x_first=False) + carry[0:1, :]
        carry[...] = jnp.broadcast_to(suf[0:1, :], carry.shape)
        x = zf_ref[...] + b_ref[...]
        row = n * C + lax.broadcasted_iota(jnp.int32, (C, C), 0)
        dff = jnp.where(row >= PAD, suf * (1.0 - jax.nn.sigmoid(x)), 0.0)
        dff_ref[...] = dff.astype(BF)
        db_ref[...] += jnp.sum(dff, axis=0, keepdims=True)

    rev = lambda s: (NCH - 1 - s, 0)
    return pl.pallas_call(
        body, name="fox_gate_bwd", grid=(NCH,),
        in_specs=[pl.BlockSpec((C, FH * FD), lambda s: (_fox_pos(NCH - 1 - s), 0)),
                  pl.BlockSpec((C, FH * FD), rev), pl.BlockSpec((C, C), rev),
                  pl.BlockSpec((1, C), lambda s: (0, 0)), pl.BlockSpec((C, C), lambda s: (0, 0)),
                  pl.BlockSpec((FH * FD, C), lambda s: (0, 0))],
        out_specs=[pl.BlockSpec((C, C), rev), pl.BlockSpec((1, C), lambda s: (0, 0))],
        out_shape=[jax.ShapeDtypeStruct((T, C), BF), jax.ShapeDtypeStruct((1, C), F32)],
        scratch_shapes=[pltpu.VMEM((8, C), F32)],
        compiler_params=_params(("arbitrary",)),
    )(drow, dcol, zf, bf_pad, cst["tri"], cst["pick"])


def _gated(r, rg, a, fg):
    rn, rs = [], []
    for h in range(RH):
        rh = r[:, RDV * h:RDV * (h + 1)]
        s = lax.rsqrt(jnp.mean(rh * rh, axis=1, keepdims=True) + EPS)
        rn.append(rh * s)
        rs.append(s)
    rn = jnp.concatenate(rn, axis=1)
    y = jnp.concatenate([rn * (rg * jax.nn.sigmoid(rg)), a * (fg * jax.nn.sigmoid(fg))], axis=1)
    return y, rn, rs


def _out_loss(r, z, a, wout, x, tgt, fgain):
    def body(r_ref, rg_ref, a_ref, fg_ref, w_ref, x_ref, t_ref, g_ref, yt_ref, do_ref, dob_ref, loss_ref, dg_ref):
        i = pl.program_id(0)

        @pl.when(i == 0)
        def _():
            yt_ref[...] = jnp.zeros_like(yt_ref)
            do_ref[...] = jnp.zeros_like(do_ref)
            dob_ref[...] = jnp.zeros_like(dob_ref)
            loss_ref[...] = jnp.zeros_like(loss_ref)
            dg_ref[...] = jnp.zeros_like(dg_ref)

        @pl.when(i > 0)
        def _():
            y, _, _ = _gated(r_ref[...], rg_ref[...], a_ref[...], fg_ref[...])
            yt_ref[...] = y.T.astype(BF)
            o = x_ref[...] + _dot(y.astype(BF), w_ref[...])
            rs = lax.rsqrt(jnp.mean(o * o, axis=1, keepdims=True) + EPS)
            on = o * rs
            g = g_ref[...]
            e = on * g - t_ref[...]
            loss_ref[...] += 0.5 * jnp.sum(jnp.mean(e * e, axis=1, keepdims=True))
            dyh = e * (1.0 / D)
            dg_ref[...] += jnp.sum(dyh * on, axis=0, keepdims=True)
            don = dyh * g
            do = rs * (don - on * jnp.mean(don * on, axis=1, keepdims=True))
            do_ref[...] = do
            dob_ref[...] = do.astype(BF)

    tok = lambda i: (jnp.maximum(i - 1, 0), 0)
    return pl.pallas_call(
        body, name="out_loss", grid=(NCH,),
        in_specs=[pl.BlockSpec((C, D), lambda i: (i, 0)), pl.BlockSpec((C, D), lambda i: (i, GB_R)),
                  pl.BlockSpec((C, D), lambda i: (_fox_pos(i), 0)), pl.BlockSpec((C, D), lambda i: (i, GB_F)),
                  pl.BlockSpec((DMIX, D), lambda i: (0, 0)),
                  pl.BlockSpec((C, D), tok), pl.BlockSpec((C, D), tok), pl.BlockSpec((1, D), lambda i: (0, 0))],
        out_specs=[pl.BlockSpec((DMIX, C), lambda i: (0, i)), pl.BlockSpec((C, D), lambda i: (i, 0)),
                   pl.BlockSpec((C, D), lambda i: (i, 0)), pl.BlockSpec((8, C), lambda i: (0, 0)),
                   pl.BlockSpec((1, D), lambda i: (0, 0))],
        out_shape=[jax.ShapeDtypeStruct((DMIX, T), BF), jax.ShapeDtypeStruct((T, D), F32),
                   jax.ShapeDtypeStruct((T, D), BF), jax.ShapeDtypeStruct((8, C), F32),
                   jax.ShapeDtypeStruct((1, D), F32)],
        compiler_params=_params(("arbitrary",)),
    )(r, z, a, z, wout, x, tgt, fgain)


def _dsilu(x):
    s = jax.nn.sigmoid(x)
    return s * (1.0 + x * (1.0 - s))


def _dy_gate_bwd(dob, wout, r, z, a, seg):
    def body(do_ref, w_ref, r_ref, rg_ref, a_ref, fg_ref, seg_ref, dr_ref, da_ref, drg_ref, dfg_ref, dl_ref):
        dy = _dg(do_ref[...], w_ref[...], NT)
        rg, fg, a_ = rg_ref[...], fg_ref[...], a_ref[...]
        _, rn, rs = _gated(r_ref[...], rg, a_, fg)
        dyr, dyf = dy[:, :D], dy[:, D:]
        drn = dyr * (rg * jax.nn.sigmoid(rg))
        drg_ref[...] = (dyr * rn * _dsilu(rg)).astype(BF)
        for h in range(RH):
            sl = slice(RDV * h, RDV * (h + 1))
            dh, nh = drn[:, sl], rn[:, sl]
            dr_ref[:, sl] = (rs[h] * (dh - nh * jnp.mean(dh * nh, axis=1, keepdims=True))).astype(BF)
        dab = (dyf * (fg * jax.nn.sigmoid(fg))).astype(BF)
        da_ref[...] = dab
        dfg_ref[...] = (dyf * a_ * _dsilu(fg)).astype(BF)
        prod = dab.astype(F32) * a_
        segm = seg_ref[...]
        for p in range(NPAIR):
            sl = slice(C * p, C * (p + 1))
            hi = prod[:, sl].astype(BF)
            lo = (prod[:, sl] - hi.astype(F32)).astype(BF)
            dl_ref[:, sl] = _dot(hi, segm) + _dot(lo, segm)

    row = pl.BlockSpec((C, D), lambda i: (i, 0))
    fox = pl.BlockSpec((C, D), lambda i: (_fox_pos(i), 0))
    return pl.pallas_call(
        body, name="dy_gate_bwd", grid=(NCH,),
        in_specs=[row, pl.BlockSpec((DMIX, D), lambda i: (0, 0)),
                  row, pl.BlockSpec((C, D), lambda i: (i, GB_R)),
                  fox, pl.BlockSpec((C, D), lambda i: (i, GB_F)),
                  pl.BlockSpec((C, C), lambda i: (0, 0))],
        out_specs=[row, fox, row, row, fox],
        out_shape=[jax.ShapeDtypeStruct((T, D), BF), jax.ShapeDtypeStruct((TROWS, D), BF),
                   jax.ShapeDtypeStruct((T, D), BF), jax.ShapeDtypeStruct((T, D), BF),
                   jax.ShapeDtypeStruct((TROWS, D), F32)],
        compiler_params=_params(("parallel",)),
    )(dob, wout, r, z, a, z, seg)


DZ_WIDTHS = (512, 512, 1024, 1024, 1024, 1024, 1024, 1024)


def _du_norm_bwd(dzs, dzf, wt, wft, hpad, g, dopad, parts=()):
    tm, tk = 544, 1024
    nk = WMAIN // tk
    ni = T // tm
    n = len(parts)

    def body(rq_ref, rk_ref, rv_ref, rg_ref, fq_ref, fk_ref, fv_ref, fg_ref, dzf_ref, w_ref, wf_ref, h_ref, g_ref,
             do_ref, *rest):
        part_refs, (gh_ref, dg_ref), land_refs = rest[:n], rest[n:n + 2], rest[n + 2:2 * n + 2]
        acc = rest[2 * n + 2]
        i, k = pl.program_id(0), pl.program_id(1)

        if n:
            send_sems, recv_sems = rest[2 * n + 3:]
            copies = _chip_copies(part_refs, land_refs, send_sems, recv_sems, by_dest=True)

            @pl.when((i == 0) & (k == 0))
            def _():
                for cp in copies:
                    cp.start()

            @pl.when((i == ni - 1) & (k == nk - 1))
            def _():
                for cp in copies:
                    cp.wait()

        @pl.when(k == 0)
        def _():
            acc[...] = (_dot(dzf_ref[...], wf_ref[...]) + _dot(rq_ref[...], w_ref[:512, :])
                        + _dot(rk_ref[...], w_ref[512:, :]))

        for kk, piece in enumerate((rv_ref, rg_ref, fq_ref, fk_ref, fv_ref, fg_ref), start=1):
            @pl.when(k == kk)
            def _(piece=piece):
                acc[...] += _dot(piece[...], w_ref[...])

        @pl.when(k == nk - 1)
        def _():
            du = acc[...]
            h = h_ref[...]
            gg = g_ref[...]
            rs = lax.rsqrt(jnp.mean(h * h, axis=1, keepdims=True) + EPS)
            hn = h * rs
            part = jnp.sum(du * hn, axis=0, keepdims=True)

            @pl.when(i == 0)
            def _():
                dg_ref[...] = part

            @pl.when(i > 0)
            def _():
                dg_ref[...] += part

            dhn = du * gg
            gh_ref[...] = rs * (dhn - hn * jnp.mean(dhn * hn, axis=1, keepdims=True)) + do_ref[...]

    sems = [pltpu.SemaphoreType.DMA((3 * n,)), pltpu.SemaphoreType.DMA((3 * n,))] if n else []
    return pl.pallas_call(
        body, name="du_norm_bwd", grid=(ni, nk),
        in_specs=[pl.BlockSpec((tm, w), lambda i, k: (i, 0)) for w in DZ_WIDTHS]
        + [pl.BlockSpec((tm, C), lambda i, k: (i, 0)),
           pl.BlockSpec((tk, D), lambda i, k: (k, 0)), pl.BlockSpec((C, D), lambda i, k: (0, 0)),
           pl.BlockSpec((tm, D), lambda i, k: (i, 0)), pl.BlockSpec((1, D), lambda i, k: (0, 0)),
           pl.BlockSpec((tm, D), lambda i, k: (i, 0))] + [ANY] * n,
        out_specs=[pl.BlockSpec((tm, D), lambda i, k: (i, 0)), pl.BlockSpec((1, D), lambda i, k: (0, 0))] + [ANY] * n,
        out_shape=[jax.ShapeDtypeStruct((T, D), F32), jax.ShapeDtypeStruct((1, D), F32)]
        + [jax.ShapeDtypeStruct(p.shape, p.dtype) for p in parts],
        scratch_shapes=[pltpu.VMEM((tm, D), F32)] + sems,
        compiler_params=_params(("arbitrary", "arbitrary")),
    )(*dzs, dzf, wt, wft, hpad, g, dopad, *parts)


GROWS = 7680


def _dw_in(dzs, dzf, ut):
    tn = 512
    nmain = WMAIN // tn
    first, blocks = [], []
    for w in DZ_WIDTHS:
        first.append(sum(blocks))
        blocks.append(w // tn)

    def body(rq_ref, rk_ref, rv_ref, rg_ref, fq_ref, fk_ref, fv_ref, fg_ref, dzf_ref, ut_ref, o_ref):
        gidx = pl.program_id(0)
        for piece, g0, nb in zip((rq_ref, rk_ref, rv_ref, rg_ref, fq_ref, fk_ref, fv_ref, fg_ref), first, blocks):
            @pl.when((gidx >= g0) & (gidx < g0 + nb))
            def _(piece=piece):
                o_ref[...] = _dot(ut_ref[...], piece[...]).T.astype(BF)

        @pl.when(gidx == nmain)
        def _():
            o_ref[:C, :] = _dot(ut_ref[...], dzf_ref[...]).T.astype(BF)
            o_ref[C:, :] = jnp.zeros((tn - C, D), BF)

    def piece_spec(g0, nb):
        return pl.BlockSpec((T, tn), lambda gidx: (0, jnp.clip(gidx - g0, 0, nb - 1)))

    return pl.pallas_call(
        body, name="dw_in", grid=(nmain + 1,),
        in_specs=[piece_spec(g0, nb) for g0, nb in zip(first, blocks)]
        + [pl.BlockSpec((T, C), lambda gidx: (0, 0)), pl.BlockSpec((D, T), lambda gidx: (0, 0))],
        out_specs=pl.BlockSpec((tn, D), lambda gidx: (gidx, 0)),
        out_shape=jax.ShapeDtypeStruct((GROWS, D), BF),
        compiler_params=pltpu.CompilerParams(dimension_semantics=("arbitrary",), vmem_limit_bytes=DW_VMEM_LIMIT),
    )(*dzs, dzf, ut)


def _token_order(x_po):
    def body(i_ref, o_ref):
        o_ref[...] = i_ref[...]

    return pl.pallas_call(
        body, name="token_order", grid=(NCH,),
        in_specs=[pl.BlockSpec((C, D), lambda i: (_fox_pos(i), 0))],
        out_specs=pl.BlockSpec((C, D), lambda i: (i, 0)),
        out_shape=jax.ShapeDtypeStruct((T, D), x_po.dtype),
        compiler_params=_params(("parallel",)),
    )(x_po)


def _local_step(x, tgt, meta, norm_g, wt, wft, b_f, wout, final_g, chip_sums=None, wout_full=None):
    cst = _constants()
    hpad = jnp.concatenate([jnp.pad(meta, ((PAD, 0), (0, 0))), x], axis=0)
    bf_pad = jnp.pad(b_f, ((0, 0), (0, C - NFF)))
    u, ut = _norm_in(hpad, norm_g)
    z = _mm_nt(u, wt, WMAIN, T // 2, 512, "in_proj")
    zf = _mm_nt(u, wft, C, T // 2, C, "in_proj_ff")
    r, sprev = _ret_fwd(z, cst)
    ct = _fox_prep(zf, bf_pad, cst)
    if wout_full is None:
        a, g = _fox_fwd(z, ct, cst, None)
    else:
        a, g, landed_wout = _fox_fwd(z, ct, cst, wout)
        wout = wout_full(landed_wout)
    yt, dopad, dob, loss8, dfg = _out_loss(r, z, a, wout, x, tgt, final_g)
    dr, da, dzrg, dzfg, delta = _dy_gate_bwd(dob, wout, r, z, a, cst["seg"])
    dwout = _mm_nn(yt, dob, 512, D, "dw_out", BF)
    dzq_r, dzk_r, dzv_r = _ret_bwd(z, cst, sprev, dr)
    dq_po, drow, dzk_f, dzv_f, dcol = _fox_bwd(z, da, g, delta, ct, cst)
    dzf, dbf = _fox_gate_bwd(drow, dcol, zf, bf_pad, cst)
    dzs = [dzq_r, dzk_r, dzv_r, dzrg, _token_order(dq_po), dzk_f, dzv_f, dzfg]
    gwt = _dw_in(dzs, dzf, ut)
    parts = chip_sums(gwt, dwout) if chip_sums else []
    gh, dng, *landed = _du_norm_bwd(dzs, dzf, wt, wft, hpad, norm_g, dopad, parts)
    return (loss8[0, 0], gh[C:], gh[PAD:C], dng, gwt, dbf[:, :NFF], dwout, dfg, parts, landed)


WOFF, WLEN = 1792, 2048
WHALF = WLEN // 2
LAP = WPADROWS - WOFF


def _gather_weights(own_win, meta):
    half_main, half_lap, half_meta = WOFF // 2, LAP // 2, meta.shape[0] // 2

    def body(win_ref, meta_ref, w_ref, laps_ref, gm_ref, send_sems, recv_sems, local_sems, stage, lapbuf, headbuf):
        x, y, c = _place()
        me_s = 2 * x + y
        sib = (x, y, 1 - c)
        chips = _other_chips(x, y)
        kinds = [
            (lambda h: win_ref.at[pl.ds(half_main * h, half_main)],
             lambda s, h: w_ref.at[pl.ds(WOFF * s + half_main * h, half_main)]),
            (lambda h: win_ref.at[pl.ds(WOFF + half_lap * h, half_lap)],
             lambda s, h: laps_ref.at[s, pl.ds(half_lap * h, half_lap)]),
            (lambda h: meta_ref.at[pl.ds(half_meta * h, half_meta)],
             lambda s, h: gm_ref.at[s, pl.ds(half_meta * h, half_meta)]),
        ]
        own_in = pltpu.make_async_copy(win_ref.at[pl.ds(0, WOFF)], stage, local_sems.at[0])
        own_in.start()
        own_lap_in = pltpu.make_async_copy(win_ref.at[pl.ds(WOFF, LAP)], lapbuf.at[0], local_sems.at[1])
        own_lap_in.start()
        sends, waits = [], []
        for a, (src, dst) in enumerate(kinds):
            for k, (cx, cy, cs) in enumerate(chips):
                sends.append(pltpu.make_async_remote_copy(
                    src_ref=src(c), dst_ref=dst(me_s, c),
                    send_sem=send_sems.at[6 * a + k], recv_sem=recv_sems.at[6 * a + k],
                    device_id=(cx, cy, c), device_id_type=MESH))
                sends[-1].start()
        own_in.wait()
        own_out = pltpu.make_async_copy(stage, w_ref.at[pl.ds(WOFF * me_s, WOFF)], local_sems.at[0])
        own_out.start()
        own_lap_in.wait()
        own_lap_out = pltpu.make_async_copy(lapbuf.at[0], laps_ref.at[me_s], local_sems.at[1])
        own_lap_out.start()
        for a, (src, dst) in enumerate(kinds):
            for k, (cx, cy, cs) in enumerate(chips):
                pltpu.make_async_remote_copy(
                    src_ref=dst(cs, c), dst_ref=dst(cs, c),
                    send_sem=send_sems.at[6 * a + k], recv_sem=recv_sems.at[6 * a + k],
                    device_id=(cx, cy, c), device_id_type=MESH).wait_recv()
                fwd = pltpu.make_async_remote_copy(
                    src_ref=dst(cs, c), dst_ref=dst(cs, c),
                    send_sem=send_sems.at[6 * a + 3 + k], recv_sem=recv_sems.at[6 * a + 3 + k],
                    device_id=sib, device_id_type=MESH)
                fwd.start()
                sends.append(fwd)
                waits.append(pltpu.make_async_remote_copy(
                    src_ref=dst(cs, 1 - c), dst_ref=dst(cs, 1 - c),
                    send_sem=send_sems.at[6 * a + 3 + k], recv_sem=recv_sems.at[6 * a + 3 + k],
                    device_id=sib, device_id_type=MESH))
        for w in waits:
            w.wait_recv()
        for s in sends:
            s.wait_send()
        own_out.wait()
        own_lap_out.wait()
        for s in range(1, 4):
            head = w_ref.at[pl.ds(WOFF * s, LAP)]
            loads = [pltpu.make_async_copy(laps_ref.at[s - 1], lapbuf.at[1], local_sems.at[2]),
                     pltpu.make_async_copy(head, headbuf, local_sems.at[3])]
            for cp in loads:
                cp.start()
            for cp in loads:
                cp.wait()
            headbuf[...] = (headbuf[...].astype(F32) + lapbuf[1].astype(F32)).astype(BF)
            store = pltpu.make_async_copy(headbuf, head, local_sems.at[3])
            store.start()
            store.wait()

    return pl.pallas_call(
        body, name="all_gather_w",
        in_specs=[ANY] * 2, out_specs=[ANY] * 3,
        out_shape=[jax.ShapeDtypeStruct((WMAIN, D), own_win.dtype), jax.ShapeDtypeStruct((4, LAP, D), own_win.dtype),
                   jax.ShapeDtypeStruct((4,) + meta.shape, meta.dtype)],
        scratch_shapes=[pltpu.SemaphoreType.DMA((18,)), pltpu.SemaphoreType.DMA((18,)), pltpu.SemaphoreType.DMA((4,)),
                        pltpu.VMEM((WOFF, D), own_win.dtype), pltpu.VMEM((2, LAP, D), own_win.dtype),
                        pltpu.VMEM((LAP, D), own_win.dtype)],
        compiler_params=pltpu.CompilerParams(vmem_limit_bytes=VMEM_LIMIT),
    )(own_win, meta)


def _pair_swap(gwt, arrs):
    n = len(arrs)

    def body(*refs):
        gw, ins = refs[0], refs[1:n + 1]
        gwo, outs = refs[n + 1], refs[n + 2:2 * n + 2]
        send_sems, recv_sems = refs[2 * n + 2:]
        x, y, c = _place()
        sib = (x, y, 1 - c)
        cps = []
        for k in range(4):
            cps.append(pltpu.make_async_remote_copy(
                src_ref=gw.at[pl.ds(WOFF * k + (1 - c) * WHALF, WHALF)], dst_ref=gwo.at[k],
                send_sem=send_sems.at[k], recv_sem=recv_sems.at[k], device_id=sib, device_id_type=MESH))
        for a in range(n):
            rows = ins[a].shape[1] // 2
            cps.append(pltpu.make_async_remote_copy(
                src_ref=ins[a].at[:, pl.ds((1 - c) * rows, rows)], dst_ref=outs[a],
                send_sem=send_sems.at[4 + a], recv_sem=recv_sems.at[4 + a], device_id=sib, device_id_type=MESH))
        for cp in cps:
            cp.start()
        for cp in cps:
            cp.wait()

    return pl.pallas_call(
        body, name="rs_pair_swap",
        in_specs=[ANY] * (n + 1), out_specs=[ANY] * (n + 1),
        out_shape=[jax.ShapeDtypeStruct((4, WHALF, D), gwt.dtype)]
        + [jax.ShapeDtypeStruct((4, a.shape[1] // 2, a.shape[2]), a.dtype) for a in arrs],
        scratch_shapes=[pltpu.SemaphoreType.DMA((n + 4,)), pltpu.SemaphoreType.DMA((n + 4,))],
    )(gwt, *arrs)


def _add_windows(gwt, recv):
    tb = 256
    nb = WHALF // tb
    c = lax.axis_index("c")

    def body(c_ref, a_ref, b_ref, o_ref):
        o_ref[0] = (a_ref[...].astype(F32) + b_ref[0].astype(F32)).astype(BF)

    return pl.pallas_call(
        body, name="pair_add_in",
        grid_spec=pltpu.PrefetchScalarGridSpec(
            num_scalar_prefetch=1, grid=(4, nb),
            in_specs=[pl.BlockSpec((tb, D), lambda k, i, cr: ((WOFF // tb) * k + nb * cr[0] + i, 0)),
                      pl.BlockSpec((1, tb, D), lambda k, i, cr: (k, i, 0))],
            out_specs=pl.BlockSpec((1, tb, D), lambda k, i, cr: (k, i, 0))),
        out_shape=jax.ShapeDtypeStruct(recv.shape, BF),
        compiler_params=_params(("parallel", "parallel")),
    )(jnp.reshape(c, (1,)).astype(jnp.int32), gwt, recv)


def _chip_exchange(parts, small):
    n = len(parts)

    def body(*refs):
        ins, sm = refs[:n], refs[n]
        outs, smo = refs[n + 1:2 * n + 1], refs[2 * n + 1]
        send_sems, recv_sems = refs[2 * n + 2:]
        cps = _chip_copies(ins, outs, send_sems, recv_sems, by_dest=True)
        cps += _chip_copies([sm], [smo], send_sems.at[pl.ds(3 * n, 3)], recv_sems.at[pl.ds(3 * n, 3)], by_dest=False)
        for cp in cps:
            cp.start()
        for cp in cps:
            cp.wait()

    return pl.pallas_call(
        body, name="rs_chip_exchange",
        in_specs=[ANY] * (n + 1), out_specs=[ANY] * (n + 1),
        out_shape=[jax.ShapeDtypeStruct(p.shape, p.dtype) for p in parts]
        + [jax.ShapeDtypeStruct((4,) + small.shape, small.dtype)],
        scratch_shapes=[pltpu.SemaphoreType.DMA((3 * (n + 1),)), pltpu.SemaphoreType.DMA((3 * (n + 1),))],
    )(*parts, small)


def _pair_send(halves):
    n = len(halves)

    def body(*refs):
        ins, outs = refs[:n], refs[n:2 * n]
        send_sems, recv_sems = refs[2 * n:]
        x, y, c = _place()
        cps = [pltpu.make_async_remote_copy(
            src_ref=ins[a], dst_ref=outs[a], send_sem=send_sems.at[a], recv_sem=recv_sems.at[a],
            device_id=(x, y, 1 - c), device_id_type=MESH) for a in range(n)]
        for cp in cps:
            cp.start()
        for cp in cps:
            cp.wait()

    return pl.pallas_call(
        body, name="rs_pair_send",
        in_specs=[ANY] * n, out_specs=[ANY] * n,
        out_shape=[jax.ShapeDtypeStruct(h.shape, h.dtype) for h in halves],
        scratch_shapes=[pltpu.SemaphoreType.DMA((n,)), pltpu.SemaphoreType.DMA((n,))],
    )(*halves)


def _row_block(rows):
    for tb in (256, 128, 64, 32, 16, 8):
        if rows % tb == 0:
            return tb
    return rows


def _add_halves(full, recv, name, out_dtype):
    _, r2, w = recv.shape
    tb = _row_block(r2)
    nb = r2 // tb
    c = lax.axis_index("c")

    def body(c_ref, a_ref, b_ref, o_ref):
        o_ref[...] = (a_ref[...].astype(F32) + b_ref[...].astype(F32)).astype(o_ref.dtype)

    return pl.pallas_call(
        body, name=name,
        grid_spec=pltpu.PrefetchScalarGridSpec(
            num_scalar_prefetch=1, grid=(4, nb),
            in_specs=[pl.BlockSpec((1, tb, w), lambda s, i, cr: (s, cr[0] * nb + i, 0)),
                      pl.BlockSpec((1, tb, w), lambda s, i, cr: (s, i, 0))],
            out_specs=pl.BlockSpec((1, tb, w), lambda s, i, cr: (s, i, 0))),
        out_shape=jax.ShapeDtypeStruct(recv.shape, out_dtype),
        compiler_params=_params(("parallel", "parallel")),
    )(jnp.reshape(c, (1,)).astype(jnp.int32), full, recv)


def _add2(a, b, name):
    def body(a_ref, b_ref, o_ref):
        o_ref[...] = a_ref[...] + b_ref[...]

    return pl.pallas_call(body, name=name, out_shape=jax.ShapeDtypeStruct(a.shape, a.dtype))(a, b)


def _sum4(buf, own, name):
    _, r, w = buf.shape
    tb = _row_block(r)
    me_s = 2 * lax.axis_index("x") + lax.axis_index("y")
    by_dest = own.ndim == 3

    def body(s_ref, b_ref, own_ref, o_ref):
        mine = (own_ref[0] if by_dest else own_ref[...]).astype(F32)
        terms = [jnp.where(s_ref[0] == t, mine, b_ref[t].astype(F32)) for t in range(4)]
        o_ref[...] = ((terms[0] + terms[1]) + terms[2]) + terms[3]

    own_spec = (pl.BlockSpec((1, tb, w), lambda i, sr: (sr[0], i, 0)) if by_dest
                else pl.BlockSpec((tb, w), lambda i, sr: (i, 0)))
    return pl.pallas_call(
        body, name=name,
        grid_spec=pltpu.PrefetchScalarGridSpec(
            num_scalar_prefetch=1, grid=(r // tb,),
            in_specs=[pl.BlockSpec((4, tb, w), lambda i, sr: (0, i, 0)), own_spec],
            out_specs=pl.BlockSpec((tb, w), lambda i, sr: (i, 0))),
        out_shape=jax.ShapeDtypeStruct((r, w), F32),
        compiler_params=_params(("parallel",)),
    )(jnp.reshape(me_s, (1,)).astype(jnp.int32), buf, own)


def _adamw_math(w, g, m, v):
    mn = B1 * m + (1.0 - B1) * g
    vn = B2 * v + (1.0 - B2) * (g * g)
    m_hat = mn / (1.0 - B1 ** STEP)
    v_hat = vn / (1.0 - B2 ** STEP)
    return -LR * (m_hat / (jnp.sqrt(v_hat) + AEPS) + WD * w), mn, vn


def _adamw(w, g, m, v, name):
    r, c_ = w.shape
    tb = _row_block(r)
    if tb == r and r > 512:
        tb = 256

    def body(w_ref, g_ref, m_ref, v_ref, d_ref, mo_ref, vo_ref):
        d_ref[...], mo_ref[...], vo_ref[...] = _adamw_math(w_ref[...], g_ref[...], m_ref[...], v_ref[...])

    spec = pl.BlockSpec((tb, c_), lambda i: (i, 0))
    return pl.pallas_call(
        body, name=name, grid=(pl.cdiv(r, tb),),
        in_specs=[spec] * 4, out_specs=[spec] * 3,
        out_shape=[jax.ShapeDtypeStruct(w.shape, F32)] * 3,
        compiler_params=_params(("parallel",)),
    )(w, g, m, v)


def _adamw_halves(w, g_mine, g_sib, m, v, name):
    r, c_ = w.shape
    r2 = g_mine.shape[0]
    tb = _row_block(r2)
    nb = r2 // tb
    c = lax.axis_index("c")

    def body(c_ref, w_ref, gm_ref, gs_ref, m_ref, v_ref, g_ref, d_ref, mo_ref, vo_ref):
        g = jnp.where(pl.program_id(0) == c_ref[0], gm_ref[...], gs_ref[...])
        g_ref[...] = g
        d_ref[...], mo_ref[...], vo_ref[...] = _adamw_math(w_ref[...], g, m_ref[...], v_ref[...])

    full = pl.BlockSpec((tb, c_), lambda h, i, cr: (h * nb + i, 0))
    half = pl.BlockSpec((tb, c_), lambda h, i, cr: (i, 0))
    return pl.pallas_call(
        body, name=name,
        grid_spec=pltpu.PrefetchScalarGridSpec(
            num_scalar_prefetch=1, grid=(2, nb),
            in_specs=[full, half, half, full, full], out_specs=[full] * 4),
        out_shape=[jax.ShapeDtypeStruct(w.shape, F32)] * 4,
        compiler_params=_params(("parallel", "parallel")),
    )(jnp.reshape(c, (1,)).astype(jnp.int32), w, g_mine, g_sib, m, v)


def kernel(x, meta_tokens, norm_g, w_in, b_f, w_out, final_g, loss_target, m_meta_tokens, m_norm_g, m_w_in, m_b_f, m_w_out, m_final_g, v_meta_tokens, v_norm_g, v_w_in, v_b_f, v_w_out, v_final_g):
    me_s = 2 * lax.axis_index("x") + lax.axis_index("y")
    core = lax.axis_index("c")
    wt, mt, vt = [jnp.swapaxes(t[0], 0, 1) for t in (w_in, m_w_in, v_w_in)]

    own_win = lax.dynamic_update_slice(jnp.zeros((WPADROWS, D), F32), wt, (4 * me_s, 0)).astype(BF)
    wt_main, laps, gmeta = _gather_weights(own_win, meta_tokens)
    wft = jnp.pad(laps[3, :NFF], ((0, C - NFF), (0, 0)))
    mine = (jnp.arange(4) == me_s)[:, None, None]
    gmeta = jnp.where(mine, meta_tokens[None], gmeta)
    wout_own = w_out[0].astype(BF)
    meta = jnp.concatenate([gmeta[s] for s in range(4)], axis=1)

    def wout_full(landed):
        return jnp.where(mine, wout_own[None], landed).reshape(DMIX, D)

    def chip_sums(gwt, dwout):
        g_out = dwout.reshape(4, DMIX // 4, D)
        r_in, r_out = _pair_swap(gwt, [g_out])
        return [_add_windows(gwt, r_in), _add_halves(g_out, r_out, "pair_add_out", BF)]

    loss, gx, dmeta, dng, gwt, dbf, dwout, dfg, (p_in, p_out), (e_in, e_out) = _local_step(
        x[0], loss_target[0], meta, norm_g, wt_main, wft, b_f, wout_own, final_g.reshape(1, D), chip_sums, wout_full)

    g_meta = jnp.stack([dmeta[:, 256 * s:256 * (s + 1)] for s in range(4)])
    small = jnp.concatenate([dng, dfg, jnp.pad(dbf, ((0, 0), (0, D - NFF))),
                             jnp.pad(jnp.reshape(loss, (1, 1)), ((0, 0), (0, D - 1))),
                             jnp.zeros((4, D), F32)], axis=0)
    e_meta, e_small = _chip_exchange([g_meta], small)
    h_in, h_out = _sum4(e_in, p_in, "sum_in"), _sum4(e_out, p_out, "sum_out")
    h_meta, h_small = _sum4(e_meta, g_meta, "sum_meta"), _sum4(e_small, small, "sum_small")
    s_in, s_out, s_meta, s_small = _pair_send([h_in, h_out, h_meta, h_small])
    gw_meta = _add2(h_meta, s_meta, "pair_add_meta")
    tot = _add2(h_small, s_small, "pair_add_small")
    g_norm, g_final, g_bf, loss_all = tot[0:1], tot[1], tot[2:3, :NFF], tot[3, 0]

    d_meta, nm_meta, nv_meta = _adamw(meta_tokens, gw_meta, m_meta_tokens, v_meta_tokens, "adamw_meta")
    d_norm, nm_norm, nv_norm = _adamw(norm_g, g_norm, m_norm_g, v_norm_g, "adamw_norm")
    window = jnp.concatenate([jnp.where(core == 0, h_in, s_in), jnp.where(core == 0, s_in, h_in)], axis=0)
    gwt_own = lax.dynamic_slice(window, (4 * me_s, 0), (WSH, D))
    d_in, nm_in, nv_in = _adamw(wt, gwt_own, mt, vt, "adamw_in")
    gw_in, d_in, nm_in, nv_in = [jnp.swapaxes(t, 0, 1)[None] for t in (gwt_own, d_in, nm_in, nv_in)]
    d_bf, nm_bf, nv_bf = _adamw(b_f, g_bf, m_b_f, v_b_f, "adamw_bf")
    gw_out, d_out, nm_out, nv_out = _adamw_halves(w_out[0], h_out, s_out, m_w_out[0], v_w_out[0], "adamw_out")
    d_fin, nm_fin, nv_fin = _adamw(final_g.reshape(1, D), g_final.reshape(1, D), m_final_g.reshape(1, D),
                                   v_final_g.reshape(1, D), "adamw_final")
    return (loss_all, gx[None], gw_meta, g_norm, gw_in, g_bf, gw_out[None], g_final,
            d_meta, d_norm, d_in, d_bf, d_out[None], d_fin.reshape(D),
            nm_meta, nm_norm, nm_in, nm_bf, nm_out[None], nm_fin.reshape(D),
            nv_meta, nv_norm, nv_in, nv_bf, nv_out[None], nv_fin.reshape(D))
```

```python
import numpy as np
import jax
import jax.numpy as jnp
from jax import lax
from jax.experimental import pallas as pl
from jax.experimental.pallas import tpu as pltpu

D = 1024
SEQ = 2048
NMETA = 16
C = 128
PAD = C - NMETA
T = PAD + NMETA + SEQ
NCH = T // C
RH, RDK, RDV = 4, 128, 256
FH, FD = 16, 64
NPAIR = FH // 2
WMAIN = 7168
NFF = 16
WIN = WMAIN + NFF
WSH = WIN // 4
WPADROWS = 1824
DMIX = 2048
EPS = 1e-6
NEG = -1e30
RSCALE = RDK ** -0.5
FSCALE = FD ** -0.5
ROPE_BASE = 10000.0
LR, B1, B2, AEPS, WD, STEP = 0.001, 0.9, 0.999, 1e-08, 0.01, 10

BF = jnp.bfloat16
F32 = jnp.float32
NT = (((1,), (1,)), ((), ()))
TN = (((0,), (0,)), ((), ()))
NN_DIMS = (((1,), (0,)), ((), ()))
MESH = pl.DeviceIdType.MESH
ANY = pl.BlockSpec(memory_space=pl.ANY)
VMEM_LIMIT = 48 * 1024 * 1024
DW_VMEM_LIMIT = 56 * 1024 * 1024

GB_R, GB_F = 2, 6
QB_F, KB_F, VB_F = 24, 32, 40


def _dot(a, b):
    return jnp.dot(a, b, preferred_element_type=F32)


def _dg(a, b, dims):
    return lax.dot_general(a, b, dims, preferred_element_type=F32)


def _params(sem=None):
    return pltpu.CompilerParams(dimension_semantics=sem, vmem_limit_bytes=VMEM_LIMIT)


def _constants():
    pos = jnp.arange(T, dtype=F32) - PAD
    inv = ROPE_BASE ** (-jnp.arange(0, RDK, 2, dtype=F32) / RDK)
    ang = pos[:, None] * inv[None, :]
    cos, sin = jnp.cos(ang), jnp.sin(ang)
    cos2 = jnp.concatenate([cos, cos], axis=1)
    sin2 = jnp.concatenate([-sin, sin], axis=1)
    log_gamma = jnp.log1p(-jnp.exp2(-5.0 - jnp.arange(RH, dtype=F32)))
    idx = jnp.arange(C, dtype=F32)
    diff = idx[:, None] - idx[None, :]
    dmask = jnp.where(diff[None] >= 0, jnp.exp(log_gamma[:, None, None] * jnp.maximum(diff, 0.0)[None]), 0.0)
    zeta = jnp.exp(log_gamma[:, None] * (C - 1.0 - idx)[None, :])
    xi = jnp.exp(log_gamma[:, None] * (idx + 1.0)[None, :])
    gdec = jnp.exp(log_gamma * C)
    zeta_b = jnp.broadcast_to(zeta[:, :, None], (RH, C, RDK))
    xi_b = jnp.broadcast_to(xi[:, :, None], (RH, C, RDK))
    gdec_b = jnp.broadcast_to(gdec[:, None, None], (RH, RDK, RDV))
    tri = jnp.asarray(np.tril(np.ones((C, C), np.float32)), dtype=BF)
    head_of_lane = np.arange(FH * FD) // FD
    pick = ((np.arange(FH * FD)[:, None] % FD == 0)
            & (head_of_lane[:, None] == np.arange(C)[None, :])).astype(np.float32)
    seg = (np.arange(C)[:, None] // FD == np.arange(C)[None, :] // FD).astype(np.float32)
    ones_aug = np.concatenate([np.tile((np.arange(C) < FD)[None, :], (C, 1)),
                               np.tile((np.arange(C) >= FD)[None, :], (C, 1))], axis=0).astype(np.float32)
    lane = np.arange(2 * C) % C
    causal = np.where(lane[None, :] <= np.arange(C)[:, None], 0.0, NEG).astype(np.float32)
    mask_bias = np.stack([np.zeros((C, 2 * C), np.float32), causal])
    return dict(cos2=cos2, sin2=sin2, dmask=dmask, zeta=zeta_b, xi=xi_b, gdec=gdec_b, tri=tri,
                mask_bias=jnp.asarray(mask_bias), pick=jnp.asarray(pick, dtype=BF), seg=jnp.asarray(seg, dtype=BF),
                ones_aug=jnp.asarray(ones_aug, dtype=BF))


def _norm_in(hpad, g):
    def body(h_ref, g_ref, u_ref, ut_ref):
        h = h_ref[...]
        rs = lax.rsqrt(jnp.mean(h * h, axis=1, keepdims=True) + EPS)
        u = h * rs * g_ref[...]
        u_ref[...] = u.astype(BF)
        ut_ref[...] = u.T.astype(BF)

    return pl.pallas_call(
        body, name="norm_in", grid=(NCH,),
        in_specs=[pl.BlockSpec((C, D), lambda i: (i, 0)), pl.BlockSpec((1, D), lambda i: (0, 0))],
        out_specs=[pl.BlockSpec((C, D), lambda i: (i, 0)), pl.BlockSpec((D, C), lambda i: (0, i))],
        out_shape=[jax.ShapeDtypeStruct((T, D), BF), jax.ShapeDtypeStruct((D, T), BF)],
        compiler_params=_params(("parallel",)),
    )(hpad, g)


def _mm_nt(a, b, n, tm, tn, name):
    m, k = a.shape

    def body(a_ref, b_ref, o_ref):
        o_ref[...] = _dg(a_ref[...], b_ref[...], NT)

    return pl.pallas_call(
        body, name=name, grid=(m // tm, n // tn),
        in_specs=[pl.BlockSpec((tm, k), lambda i, j: (i, 0)), pl.BlockSpec((tn, k), lambda i, j: (j, 0))],
        out_specs=pl.BlockSpec((tm, tn), lambda i, j: (i, j)),
        out_shape=jax.ShapeDtypeStruct((m, n), F32),
        compiler_params=_params(("parallel", "parallel")),
    )(a, b)


def _mm_nn(a, b, tm, tn, name, out_dtype=F32):
    m, k = a.shape
    _, n = b.shape

    def body(a_ref, b_ref, o_ref):
        o_ref[...] = _dot(a_ref[...], b_ref[...]).astype(out_dtype)

    return pl.pallas_call(
        body, name=name, grid=(m // tm, n // tn),
        in_specs=[pl.BlockSpec((tm, k), lambda i, j: (i, 0)), pl.BlockSpec((k, tn), lambda i, j: (0, j))],
        out_specs=pl.BlockSpec((tm, tn), lambda i, j: (i, j)),
        out_shape=jax.ShapeDtypeStruct((m, n), out_dtype),
        compiler_params=_params(("parallel", "parallel")),
    )(a, b)


def _rot(x, cos2, sin2):
    return x * cos2 + pltpu.roll(x, 64, 1) * sin2


def _ret_specs(chunk):
    whole = lambda shape: pl.BlockSpec(shape, lambda n: (0,) * len(shape))
    return [
        pl.BlockSpec((C, RH * RDK), lambda n: (chunk(n), 0)),
        pl.BlockSpec((C, RH * RDK), lambda n: (chunk(n), 1)),
        pl.BlockSpec((C, RH * RDV), lambda n: (chunk(n), 1)),
        pl.BlockSpec((C, RDK), lambda n: (chunk(n), 0)),
        pl.BlockSpec((C, RDK), lambda n: (chunk(n), 0)),
        whole((RH, C, C)), whole((RH, C, RDK)), whole((RH, C, RDK)), whole((RH, RDK, RDV)),
    ]


def _ret_heads(q_ref, k_ref, v_ref, cos, sin):
    qr = [_rot(q_ref[:, RDK * h:RDK * (h + 1)], cos, sin) for h in range(RH)]
    kr = [_rot(k_ref[:, RDK * h:RDK * (h + 1)], cos, sin) * RSCALE for h in range(RH)]
    vb = [v_ref[:, RDV * h:RDV * (h + 1)].astype(BF) for h in range(RH)]
    return qr, kr, [t.astype(BF) for t in qr], [t.astype(BF) for t in kr], vb


def _ret_fwd(z, cst):
    def body(q_ref, k_ref, v_ref, cos_ref, sin_ref, dm_ref, xi_ref, zt_ref, gd_ref, r_ref, sp_ref, st):
        n = pl.program_id(0)

        @pl.when(n == 0)
        def _():
            st[...] = jnp.zeros_like(st)

        hs = range(RH)
        qr, kr, qb, kb, vb = _ret_heads(q_ref, k_ref, v_ref, cos_ref[...], sin_ref[...])
        sd = [(_dg(qb[h], kb[h], NT) * dm_ref[h]).astype(BF) for h in hs]
        state = [st[h] for h in hs]
        qx = [(qr[h] * xi_ref[h]).astype(BF) for h in hs]
        kz = [(kr[h] * zt_ref[h]).astype(BF) for h in hs]
        out = [_dot(sd[h], vb[h]) + _dot(qx[h], state[h].astype(BF)) for h in hs]
        kv = [_dg(kz[h], vb[h], TN) for h in hs]
        for h in hs:
            sp_ref[0, h] = state[h]
            r_ref[:, RDV * h:RDV * (h + 1)] = out[h]
            st[h] = state[h] * gd_ref[h] + kv[h]

    return pl.pallas_call(
        body, name="ret_fwd", grid=(NCH,),
        in_specs=_ret_specs(lambda n: n),
        out_specs=[pl.BlockSpec((C, RH * RDV), lambda n: (n, 0)),
                   pl.BlockSpec((1, RH, RDK, RDV), lambda n: (n, 0, 0, 0))],
        out_shape=[jax.ShapeDtypeStruct((T, RH * RDV), F32), jax.ShapeDtypeStruct((NCH, RH, RDK, RDV), F32)],
        scratch_shapes=[pltpu.VMEM((RH, RDK, RDV), F32)],
        compiler_params=_params(("arbitrary",)),
    )(z, z, z, cst["cos2"], cst["sin2"], cst["dmask"], cst["xi"], cst["zeta"], cst["gdec"])


def _ret_bwd(z, cst, sprev, dr):
    def body(q_ref, k_ref, v_ref, cos_ref, sin_ref, dm_ref, xi_ref, zt_ref, gd_ref, sp_ref, dr_ref,
             dq_ref, dk_ref, dv_ref, gst):
        i = pl.program_id(0)

        @pl.when(i == 0)
        def _():
            gst[...] = jnp.zeros_like(gst)

        hs = range(RH)
        cos, sin = cos_ref[...], sin_ref[...]
        qr, kr, qb, kb, vb = _ret_heads(q_ref, k_ref, v_ref, cos, sin)
        dm = [dm_ref[h] for h in hs]
        xi = [xi_ref[h] for h in hs]
        zt = [zt_ref[h] for h in hs]
        sd = [(_dg(qb[h], kb[h], NT) * dm[h]).astype(BF) for h in hs]
        qx = [(qr[h] * xi[h]).astype(BF) for h in hs]
        kz = [(kr[h] * zt[h]).astype(BF) for h in hs]
        drb = [dr_ref[:, RDV * h:RDV * (h + 1)] for h in hs]
        sb = [sp_ref[0, h].astype(BF) for h in hs]
        g = [gst[h] for h in hs]
        gb = [t.astype(BF) for t in g]
        ds = [(_dg(drb[h], vb[h], NT) * dm[h]).astype(BF) for h in hs]
        dq = [_dot(ds[h], kb[h]) + _dg(drb[h], sb[h], NT) * xi[h] for h in hs]
        dk = [(_dg(ds[h], qb[h], TN) + _dg(vb[h], gb[h], NT) * zt[h]) * RSCALE for h in hs]
        dv = [_dg(sd[h], drb[h], TN) + _dot(kz[h], gb[h]) for h in hs]
        gn = [g[h] * gd_ref[h] + _dg(qx[h], drb[h], TN) for h in hs]
        for h in hs:
            gst[h] = gn[h]
            dq_ref[:, RDK * h:RDK * (h + 1)] = (dq[h] * cos + pltpu.roll(dq[h] * sin, 64, 1)).astype(BF)
            dk_ref[:, RDK * h:RDK * (h + 1)] = (dk[h] * cos + pltpu.roll(dk[h] * sin, 64, 1)).astype(BF)
            dv_ref[:, RDV * h:RDV * (h + 1)] = dv[h].astype(BF)

    rev = lambda n: NCH - 1 - n
    return pl.pallas_call(
        body, name="ret_bwd", grid=(NCH,),
        in_specs=_ret_specs(rev) + [
            pl.BlockSpec((1, RH, RDK, RDV), lambda n: (rev(n), 0, 0, 0)),
            pl.BlockSpec((C, RH * RDV), lambda n: (rev(n), 0)),
        ],
        out_specs=[pl.BlockSpec((C, RH * RDK), lambda n: (rev(n), 0)),
                   pl.BlockSpec((C, RH * RDK), lambda n: (rev(n), 0)),
                   pl.BlockSpec((C, RH * RDV), lambda n: (rev(n), 0))],
        out_shape=[jax.ShapeDtypeStruct((T, RH * RDK), BF), jax.ShapeDtypeStruct((T, RH * RDK), BF),
                   jax.ShapeDtypeStruct((T, RH * RDV), BF)],
        scratch_shapes=[pltpu.VMEM((RH, RDK, RDV), F32)],
        compiler_params=_params(("arbitrary",)),
    )(z, z, z, cst["cos2"], cst["sin2"], cst["dmask"], cst["xi"], cst["zeta"], cst["gdec"], sprev, dr)


def _place():
    x, y, c = lax.axis_index("x"), lax.axis_index("y"), lax.axis_index("c")
    return x, y, c


def _other_chips(x, y):
    return [(1 - x, y, 2 * (1 - x) + y), (x, 1 - y, 2 * x + (1 - y)), (1 - x, 1 - y, 2 * (1 - x) + (1 - y))]


def _chip_copies(srcs, lands, send_sems, recv_sems, by_dest):
    x, y, c = _place()
    me_s = 2 * x + y
    return [pltpu.make_async_remote_copy(
        src_ref=src.at[cs] if by_dest else src, dst_ref=land.at[me_s],
        send_sem=send_sems.at[3 * a + j], recv_sem=recv_sems.at[3 * a + j],
        device_id=(cx, cy, c), device_id_type=MESH)
        for a, (src, land) in enumerate(zip(srcs, lands)) for j, (cx, cy, cs) in enumerate(_other_chips(x, y))]


def _split_dot(x, mat01, dims=NN_DIMS, x_first=True):
    acc, rest = None, x
    for _ in range(3):
        piece = rest.astype(BF)
        part = _dg(piece, mat01, dims) if x_first else _dg(mat01, piece, dims)
        acc = part if acc is None else acc + part
        rest = rest - piece.astype(F32)
    return acc


def _log_sigmoid(x):
    return -(jnp.maximum(-x, 0.0) + jnp.log1p(jnp.exp(-jnp.abs(x))))


def _fox_prep(zf, bf_pad, cst):
    def body(zf_ref, b_ref, tri_ref, ct_ref, carry):
        n = pl.program_id(0)

        @pl.when(n == 0)
        def _():
            carry[...] = jnp.zeros_like(carry)

        ls = _log_sigmoid(zf_ref[...] + b_ref[...])
        row = n * C + lax.broadcasted_iota(jnp.int32, (C, C), 0)
        lf = jnp.where(row >= PAD, ls, 0.0)
        cc = _split_dot(lf, tri_ref[...], x_first=False) + carry[0:1, :]
        carry[...] = jnp.broadcast_to(cc[C - 1:C, :], carry.shape)
        pos = n * C + lax.broadcasted_iota(jnp.int32, (FH, C), 1)
        ct_ref[0] = jnp.where(pos >= PAD, cc.T[:FH, :], -NEG)

    return pl.pallas_call(
        body, name="fox_prep", grid=(NCH,),
        in_specs=[pl.BlockSpec((C, C), lambda n: (n, 0)), pl.BlockSpec((1, C), lambda n: (0, 0)),
                  pl.BlockSpec((C, C), lambda n: (0, 0))],
        out_specs=pl.BlockSpec((1, FH, C), lambda n: (n, 0, 0)),
        out_shape=jax.ShapeDtypeStruct((NCH, FH, C), F32),
        scratch_shapes=[pltpu.VMEM((8, C), F32)],
        compiler_params=_params(("arbitrary",)),
    )(zf, bf_pad, cst["tri"])


def _lo_lanes(shape):
    return lax.broadcasted_iota(jnp.int32, shape, 1) < FD


def _split_heads(x):
    lo = _lo_lanes(x.shape)
    zero = jnp.zeros_like(x)
    return jnp.concatenate([jnp.where(lo, x, zero), jnp.where(lo, zero, x)], axis=0)


def _spread2(x):
    lo = _lo_lanes(x.shape)
    r = pltpu.roll(x, FD, 1)
    return jnp.concatenate([jnp.where(lo, x, r), jnp.where(lo, r, x)], axis=1)


NSTEP = (NCH + 1) // 2
NTILE = NCH + 1
TROWS = T + C


def _fox_tile(s, t):
    second = t > s
    return second.astype(jnp.int32), jnp.where(second, t - s - 1, s - t)


def _fox_pos(i):
    return jnp.where(i < NSTEP, 2 * i, 2 * (NCH - 1 - i) + 1)


FOX_ORDER = [2 * i if i < NSTEP else 2 * (NCH - 1 - i) + 1 for i in range(NCH)]


def _fox_pair_specs():
    first = pl.BlockSpec((C, C), lambda p, s: (2 * s, p))
    second = pl.BlockSpec((C, C), lambda p, s: (jnp.where(s == NSTEP - 1, 2 * s, 2 * s + 1), p))
    both = pl.BlockSpec((2 * C, C), lambda p, s: (s, p))
    return first, second, both


def _fox_q_specs():
    return (pl.BlockSpec((C, C), lambda p, s: (s, QB_F + p)),
            pl.BlockSpec((C, C), lambda p, s: (NCH - 1 - s, QB_F + p)))


def _fox_key_bias(ct_ref, p, j):
    return jnp.concatenate([ct_ref[j, pl.ds(2 * p, 1), :], ct_ref[j, pl.ds(2 * p + 1, 1), :]], axis=1)


def _fox_fwd(z, ct, cst, share):
    n = 0 if share is None else 1

    def body(qa_ref, qb_ref, k_ref, v_ref, ct_ref, ones_ref, mb_ref, *rest):
        share_refs, (a_ref, g_ref), land_refs = rest[:n], rest[n:n + 2], rest[n + 2:2 * n + 2]
        kks, vvs, q2, m2, sbuf = rest[2 * n + 2:2 * n + 7]
        p, s = pl.program_id(0), pl.program_id(1)
        if n:
            copies = _chip_copies(share_refs, land_refs, *rest[2 * n + 7:], by_dest=False)

            @pl.when((p == 0) & (s == 0))
            def _():
                for cp in copies:
                    cp.start()

            @pl.when((p == NPAIR - 1) & (s == NSTEP - 1))
            def _():
                for cp in copies:
                    cp.wait()

        @pl.when(s == 0)
        def _():
            ones = ones_ref[...]

            def prep(j, carry):
                rows = pl.ds(pl.multiple_of(j * C, C), C)
                kks[j] = _split_heads(k_ref[rows, :]).astype(BF)
                vvs[j] = jnp.concatenate([_split_heads(v_ref[rows, :]).astype(BF), ones], axis=1)
                return carry

            lax.fori_loop(0, NCH, prep, 0)

        q2[0] = (qa_ref[...] * FSCALE).astype(BF)
        q2[1] = (qb_ref[...] * FSCALE).astype(BF)

        tiles = [_fox_tile(s, t) for t in range(NTILE)]
        causal = mb_ref[1]
        neg = jnp.full((C, 2 * C), NEG, F32)
        run, first = neg, neg
        for t, (sel, j) in enumerate(tiles):
            st = _dg(q2[sel], kks[j], NT) - _fox_key_bias(ct_ref, p, j)
            if t in (0, NTILE - 1):
                st = st + causal
            sbuf[t] = st
            run = jnp.maximum(jnp.where(t == s + 1, neg, run), st)
            first = jnp.where(t == s, run, first)
        for w, mx in enumerate((first, run)):
            m2[w] = jnp.concatenate(
                [jnp.broadcast_to(jnp.max(mx[:, :C], axis=1, keepdims=True), (C, C)),
                 jnp.broadcast_to(jnp.max(mx[:, C:], axis=1, keepdims=True), (C, C))], axis=1)

        zero = jnp.zeros((C, 2 * C), F32)
        run, first = zero, zero
        for t, (sel, j) in enumerate(tiles):
            run = jnp.where(t == s + 1, zero, run) + _dot(jnp.exp(sbuf[t] - m2[sel]).astype(BF), vvs[j])
            first = jnp.where(t == s, run, first)
        lo = _lo_lanes((C, C))
        for w, res in enumerate((first, run)):
            l = res[:, C:]
            a_ref[C * w:C * (w + 1), :] = res[:, :C] / l
            mw = m2[w]
            g_ref[C * w:C * (w + 1), :] = -(jnp.where(lo, mw[:, :C], mw[:, C:]) + jnp.log(l))

    qa, qb = _fox_q_specs()
    both = _fox_pair_specs()[2]
    return pl.pallas_call(
        body, name="fox_fwd", grid=(NPAIR, NSTEP),
        in_specs=[qa, qb,
                  pl.BlockSpec((T, C), lambda p, s: (0, KB_F + p)),
                  pl.BlockSpec((T, C), lambda p, s: (0, VB_F + p)),
                  pl.BlockSpec((NCH, FH, C), lambda p, s: (0, 0, 0)),
                  pl.BlockSpec((2 * C, C), lambda p, s: (0, 0)),
                  pl.BlockSpec((2, C, 2 * C), lambda p, s: (0, 0, 0))] + [ANY] * n,
        out_specs=[both, both] + [ANY] * n,
        out_shape=[jax.ShapeDtypeStruct((TROWS, FH * FD), F32)] * 2
        + ([jax.ShapeDtypeStruct((4,) + share.shape, share.dtype)] if n else []),
        scratch_shapes=[pltpu.VMEM((NCH, 2 * C, C), BF), pltpu.VMEM((NCH, 2 * C, 2 * C), BF),
                        pltpu.VMEM((2, C, C), BF), pltpu.VMEM((2, C, 2 * C), F32),
                        pltpu.VMEM((NTILE, C, 2 * C), F32)]
        + [pltpu.SemaphoreType.DMA((3,)), pltpu.SemaphoreType.DMA((3,))] * n,
        compiler_params=_params(("arbitrary", "arbitrary")),
    )(z, z, z, z, ct, cst["ones_aug"], cst["mask_bias"], *([share] * n))


def _fox_bwd(z, da, g, delta, ct, cst):
    grp = 9

    def body(qa_ref, qb_ref, daa_ref, dab_ref, ga_ref, gb_ref, dla_ref, dlb_ref, k_ref, v_ref, ct_ref, ones_ref,
             mb_ref, dq_ref, dr_ref, dk_ref, dv_ref, dcs_ref,
             kks, vvs, q2, qq2, dd2, da2, gi2, dl2, dq2, dvb, dkb, dkacc, dvacc, csacc):
        p, s = pl.program_id(0), pl.program_id(1)
        ones = ones_ref[...]

        @pl.when(s == 0)
        def _():
            dkacc[...] = jnp.zeros_like(dkacc)
            dvacc[...] = jnp.zeros_like(dvacc)
            csacc[...] = jnp.zeros_like(csacc)

            def prep(j, carry):
                rows = pl.ds(pl.multiple_of(j * C, C), C)
                kks[j] = _split_heads(k_ref[rows, :]).astype(BF)
                vvs[j] = _split_heads(v_ref[rows, :]).astype(BF)
                return carry

            lax.fori_loop(0, NCH, prep, 0)

        for w, (q_ref, d_ref, g_ref, l_ref) in enumerate(((qa_ref, daa_ref, ga_ref, dla_ref),
                                                          (qb_ref, dab_ref, gb_ref, dlb_ref))):
            qf = q_ref[...]
            q2[w] = (qf * FSCALE).astype(BF)
            qq2[w] = jnp.concatenate([_split_heads(qf).astype(BF), ones], axis=1)
            da2[w] = d_ref[...]
            dd2[w] = _split_heads(d_ref[...].astype(F32)).astype(BF)
            gi2[w] = _spread2(g_ref[...])
            dl2[w] = _spread2(l_ref[...])
        dq2[...] = jnp.zeros_like(dq2)
        zero = jnp.zeros((C, 2 * C), F32)

        def group(gi, carry):
            ts = [gi * grp + u for u in range(grp)]
            tiles = [_fox_tile(s, t) for t in ts]
            kk = [kks[j] for _, j in tiles]
            ss = [_dg(q2[sel], kj, NT) + (gi2[sel] - _fox_key_bias(ct_ref, p, j)) for kj, (sel, j) in zip(kk, tiles)]
            ss[0] = ss[0] + mb_ref[(gi == 0).astype(jnp.int32)]
            ss[-1] = ss[-1] + mb_ref[(gi == 1).astype(jnp.int32)]
            dps = [_dg(da2[sel], vvs[j], NT) for sel, j in tiles]
            pes = [jnp.exp(st) for st in ss]
            dss = [pe * (dp - dl2[sel]) * FSCALE for pe, dp, (sel, _) in zip(pes, dps, tiles)]
            pts = [jnp.concatenate([pe[:, :C].T, pe[:, C:].T], axis=1).astype(BF) for pe in pes]
            dsts = [jnp.concatenate([ds[:, :C].T, ds[:, C:].T], axis=1).astype(BF) for ds in dss]
            dvs = [_dot(pt, dd2[sel]) for pt, (sel, _) in zip(pts, tiles)]
            rs = [_dot(dst, qq2[sel]) for dst, (sel, _) in zip(dsts, tiles)]
            parts = [_dot(ds.astype(BF), jnp.concatenate([kj, ones], axis=1)) for ds, kj in zip(dss, kk)]
            for t, dv, rr in zip(ts, dvs, rs):
                dvb[t] = dv
                dkb[t] = rr
            pa, pb = zero, zero
            for t, part in zip(ts, parts):
                pa = pa + jnp.where(t <= s, part, zero)
                pb = pb + jnp.where(t <= s, zero, part)
            dq2[0] += pa
            dq2[1] += pb
            return carry

        ntile = jnp.where(s == NSTEP - 1, grp, NTILE)
        lax.fori_loop(0, ntile // grp, group, 0)

        def scatter(t, carry):
            _, j = _fox_tile(s, t)
            r = pl.ds(pl.multiple_of(j * C, C), C)
            dvacc[r, :] += dvb[t]
            dkacc[r, :] += dkb[t, :, :C]
            csacc[r, :] += dkb[t, :, C:]
            return carry

        lax.fori_loop(0, ntile, scatter, 0)
        for w in range(2):
            res = dq2[w]
            dq_ref[C * w:C * (w + 1), :] = res[:, :C].astype(BF)
            dr_ref[C * w:C * (w + 1), :] = res[:, C:]

        @pl.when(s == NSTEP - 1)
        def _():
            dk_ref[...] = dkacc[...].astype(BF)
            dv_ref[...] = dvacc[...].astype(BF)
            dcs_ref[...] = csacc[...]

    qa, qb = _fox_q_specs()
    ba, bb, both = _fox_pair_specs()
    col = pl.BlockSpec((T, C), lambda p, s: (0, p))
    return pl.pallas_call(
        body, name="fox_bwd", grid=(NPAIR, NSTEP),
        in_specs=[qa, qb, ba, bb, ba, bb, ba, bb,
                  pl.BlockSpec((T, C), lambda p, s: (0, KB_F + p)),
                  pl.BlockSpec((T, C), lambda p, s: (0, VB_F + p)),
                  pl.BlockSpec((NCH, FH, C), lambda p, s: (0, 0, 0)),
                  pl.BlockSpec((2 * C, C), lambda p, s: (0, 0)),
                  pl.BlockSpec((2, C, 2 * C), lambda p, s: (0, 0, 0))],
        out_specs=[both, both, col, col, col],
        out_shape=[jax.ShapeDtypeStruct((TROWS, FH * FD), BF), jax.ShapeDtypeStruct((TROWS, FH * FD), F32),
                   jax.ShapeDtypeStruct((T, FH * FD), BF), jax.ShapeDtypeStruct((T, FH * FD), BF),
                   jax.ShapeDtypeStruct((T, FH * FD), F32)],
        scratch_shapes=[pltpu.VMEM((NCH, 2 * C, C), BF), pltpu.VMEM((NCH, 2 * C, C), BF),
                        pltpu.VMEM((2, C, C), BF), pltpu.VMEM((2, 2 * C, 2 * C), BF), pltpu.VMEM((2, 2 * C, C), BF),
                        pltpu.VMEM((2, C, C), BF), pltpu.VMEM((2, C, 2 * C), F32), pltpu.VMEM((2, C, 2 * C), F32),
                        pltpu.VMEM((2, C, 2 * C), F32),
                        pltpu.VMEM((NTILE, C, C), F32), pltpu.VMEM((NTILE, C, 2 * C), F32),
                        pltpu.VMEM((T, C), F32), pltpu.VMEM((T, C), F32), pltpu.VMEM((T, C), F32)],
        compiler_params=_params(("parallel", "arbitrary")),
    )(z, z, da, da, g, g, delta, delta, z, z, ct, cst["ones_aug"], cst["mask_bias"])


def _fox_gate_bwd(drow, dcol, zf, bf_pad, cst):
    def body(dr_ref, dc_ref, zf_ref, b_ref, tri_ref, pick_ref, dff_ref, db_ref, carry):
        s = pl.program_id(0)
        n = NCH - 1 - s

        @pl.when(s == 0)
        def _():
            carry[...] = jnp.zeros_like(carry)
            db_ref[...] = jnp.zeros_like(db_ref)

        dcb = _split_dot((dr_ref[...] - dc_ref[...]) * (1.0 / FSCALE), pick_ref[...])
        suf = _split_dot(dcb, tri_ref[...], TN, x_first=False) + carry[0:1, :]
        carry[...] = jnp.broadcast_to(suf[0:1, :], carry.shape)
        x = zf_ref[...] + b_ref[...]
        row = n * C + lax.broadcasted_iota(jnp.int32, (C, C), 0)
        dff = jnp.where(row >= PAD, suf * (1.0 - jax.nn.sigmoid(x)), 0.0)
        dff_ref[...] = dff.astype(BF)
        db_ref[...] += jnp.sum(dff, axis=0, keepdims=True)

    rev = lambda s: (NCH - 1 - s, 0)
    return pl.pallas_call(
        body, name="fox_gate_bwd", grid=(NCH,),
        in_specs=[pl.BlockSpec((C, FH * FD), lambda s: (_fox_pos(NCH - 1 - s), 0)),
                  pl.BlockSpec((C, FH * FD), rev), pl.BlockSpec((C, C), rev),
                  pl.BlockSpec((1, C), lambda s: (0, 0)), pl.BlockSpec((C, C), lambda s: (0, 0)),
                  pl.BlockSpec((FH * FD, C), lambda s: (0, 0))],
        out_specs=[pl.BlockSpec((C, C), rev), pl.BlockSpec((1, C), lambda s: (0, 0))],
        out_shape=[jax.ShapeDtypeStruct((T, C), BF), jax.ShapeDtypeStruct((1, C), F32)],
        scratch_shapes=[pltpu.VMEM((8, C), F32)],
        compiler_params=_params(("arbitrary",)),
    )(drow, dcol, zf, bf_pad, cst["tri"], cst["pick"])


def _head_norm(r):
    rn, rs = [], []
    for h in range(RH):
        rh = r[:, RDV * h:RDV * (h + 1)]
        s = lax.rsqrt(jnp.mean(rh * rh, axis=1, keepdims=True) + EPS)
        rn.append(rh * s)
        rs.append(s)
    return jnp.concatenate(rn, axis=1), rs


def _gated(r, rg, a, fg):
    rn, _ = _head_norm(r)
    return jnp.concatenate([rn * (rg * jax.nn.sigmoid(rg)), a * (fg * jax.nn.sigmoid(fg))], axis=1)


def _out_loss(r, z, a, wout, x, tgt, fgain):
    def body(r_ref, rg_ref, a_ref, fg_ref, w_ref, x_ref, t_ref, g_ref, yt_ref, do_ref, dob_ref, loss_ref, dg_ref):
        i = pl.program_id(0)

        @pl.when(i == 0)
        def _():
            yt_ref[...] = jnp.zeros_like(yt_ref)
            do_ref[...] = jnp.zeros_like(do_ref)
            dob_ref[...] = jnp.zeros_like(dob_ref)
            loss_ref[...] = jnp.zeros_like(loss_ref)
            dg_ref[...] = jnp.zeros_like(dg_ref)

        @pl.when(i > 0)
        def _():
            y = _gated(r_ref[...], rg_ref[...], a_ref[...], fg_ref[...])
            yt_ref[...] = y.T.astype(BF)
            o = x_ref[...] + _dot(y.astype(BF), w_ref[...])
            rs = lax.rsqrt(jnp.mean(o * o, axis=1, keepdims=True) + EPS)
            on = o * rs
            g = g_ref[...]
            e = on * g - t_ref[...]
            loss_ref[...] += 0.5 * jnp.sum(jnp.mean(e * e, axis=1, keepdims=True))
            dyh = e * (1.0 / D)
            dg_ref[...] += jnp.sum(dyh * on, axis=0, keepdims=True)
            don = dyh * g
            do = rs * (don - on * jnp.mean(don * on, axis=1, keepdims=True))
            do_ref[...] = do
            dob_ref[...] = do.astype(BF)

    tok = lambda i: (jnp.maximum(i - 1, 0), 0)
    return pl.pallas_call(
        body, name="out_loss", grid=(NCH,),
        in_specs=[pl.BlockSpec((C, D), lambda i: (i, 0)), pl.BlockSpec((C, D), lambda i: (i, GB_R)),
                  pl.BlockSpec((C, D), lambda i: (_fox_pos(i), 0)), pl.BlockSpec((C, D), lambda i: (i, GB_F)),
                  pl.BlockSpec((DMIX, D), lambda i: (0, 0)),
                  pl.BlockSpec((C, D), tok), pl.BlockSpec((C, D), tok), pl.BlockSpec((1, D), lambda i: (0, 0))],
        out_specs=[pl.BlockSpec((DMIX, C), lambda i: (0, i)), pl.BlockSpec((C, D), lambda i: (i, 0)),
                   pl.BlockSpec((C, D), lambda i: (i, 0)), pl.BlockSpec((8, C), lambda i: (0, 0)),
                   pl.BlockSpec((1, D), lambda i: (0, 0))],
        out_shape=[jax.ShapeDtypeStruct((DMIX, T), BF), jax.ShapeDtypeStruct((T, D), F32),
                   jax.ShapeDtypeStruct((T, D), BF), jax.ShapeDtypeStruct((8, C), F32),
                   jax.ShapeDtypeStruct((1, D), F32)],
        compiler_params=_params(("arbitrary",)),
    )(r, z, a, z, wout, x, tgt, fgain)


def _silu_and_grad(x):
    s = jax.nn.sigmoid(x)
    return x * s, s * (1.0 + x * (1.0 - s))


def _dy_gate_bwd(dob, wout, r, z, a, seg):
    def body(do_ref, w_ref, r_ref, rg_ref, a_ref, fg_ref, seg_ref, dr_ref, da_ref, drg_ref, dfg_ref, dl_ref):
        dy = _dg(do_ref[...], w_ref[...], NT)
        a_ = a_ref[...]
        rn, rs = _head_norm(r_ref[...])
        silu_rg, dsilu_rg = _silu_and_grad(rg_ref[...])
        silu_fg, dsilu_fg = _silu_and_grad(fg_ref[...])
        dyr, dyf = dy[:, :D], dy[:, D:]
        drn = dyr * silu_rg
        drg_ref[...] = (dyr * rn * dsilu_rg).astype(BF)
        for h in range(RH):
            sl = slice(RDV * h, RDV * (h + 1))
            dh, nh = drn[:, sl], rn[:, sl]
            dr_ref[:, sl] = (rs[h] * (dh - nh * jnp.mean(dh * nh, axis=1, keepdims=True))).astype(BF)
        dab = (dyf * silu_fg).astype(BF)
        da_ref[...] = dab
        dfg_ref[...] = (dyf * a_ * dsilu_fg).astype(BF)
        prod = dab.astype(F32) * a_
        segm = seg_ref[...]
        for p in range(NPAIR):
            sl = slice(C * p, C * (p + 1))
            hi = prod[:, sl].astype(BF)
            lo = (prod[:, sl] - hi.astype(F32)).astype(BF)
            dl_ref[:, sl] = _dot(hi, segm) + _dot(lo, segm)

    row = pl.BlockSpec((C, D), lambda i: (i, 0))
    fox = pl.BlockSpec((C, D), lambda i: (_fox_pos(i), 0))
    return pl.pallas_call(
        body, name="dy_gate_bwd", grid=(NCH,),
        in_specs=[row, pl.BlockSpec((DMIX, D), lambda i: (0, 0)),
                  row, pl.BlockSpec((C, D), lambda i: (i, GB_R)),
                  fox, pl.BlockSpec((C, D), lambda i: (i, GB_F)),
                  pl.BlockSpec((C, C), lambda i: (0, 0))],
        out_specs=[row, fox, row, row, fox],
        out_shape=[jax.ShapeDtypeStruct((T, D), BF), jax.ShapeDtypeStruct((TROWS, D), BF),
                   jax.ShapeDtypeStruct((T, D), BF), jax.ShapeDtypeStruct((T, D), BF),
                   jax.ShapeDtypeStruct((TROWS, D), F32)],
        compiler_params=_params(("parallel",)),
    )(dob, wout, r, z, a, z, seg)


DZ_WIDTHS = (512, 512, 1024, 1024, 1024, 1024, 1024, 1024)


def _du_norm_bwd(dzs, dzf, wt, wft, hpad, g, dopad, parts=()):
    tm, tk = 544, 1024
    nk = WMAIN // tk
    ni = T // tm
    n = len(parts)

    def body(rq_ref, rk_ref, rv_ref, rg_ref, fq_ref, fk_ref, fv_ref, fg_ref, dzf_ref, w_ref, wf_ref, h_ref, g_ref,
             do_ref, *rest):
        part_refs, (gh_ref, dg_ref), land_refs = rest[:n], rest[n:n + 2], rest[n + 2:2 * n + 2]
        acc = rest[2 * n + 2]
        i, k = pl.program_id(0), pl.program_id(1)

        if n:
            send_sems, recv_sems = rest[2 * n + 3:]
            copies = _chip_copies(part_refs, land_refs, send_sems, recv_sems, by_dest=True)

            @pl.when((i == 0) & (k == 0))
            def _():
                for cp in copies:
                    cp.start()

            @pl.when((i == ni - 1) & (k == nk - 1))
            def _():
                for cp in copies:
                    cp.wait()

        @pl.when(k == 0)
        def _():
            acc[...] = (_dot(dzf_ref[...], wf_ref[...]) + _dot(rq_ref[...], w_ref[:512, :])
                        + _dot(rk_ref[...], w_ref[512:, :]))

        for kk, piece in enumerate((rv_ref, rg_ref, fq_ref, fk_ref, fv_ref, fg_ref), start=1):
            @pl.when(k == kk)
            def _(piece=piece):
                acc[...] += _dot(piece[...], w_ref[...])

        @pl.when(k == nk - 1)
        def _():
            du = acc[...]
            h = h_ref[...]
            gg = g_ref[...]
            rs = lax.rsqrt(jnp.mean(h * h, axis=1, keepdims=True) + EPS)
            hn = h * rs
            part = jnp.sum(du * hn, axis=0, keepdims=True)

            @pl.when(i == 0)
            def _():
                dg_ref[...] = part

            @pl.when(i > 0)
            def _():
                dg_ref[...] += part

            dhn = du * gg
            gh_ref[...] = rs * (dhn - hn * jnp.mean(dhn * hn, axis=1, keepdims=True)) + do_ref[...]

    sems = [pltpu.SemaphoreType.DMA((3 * n,)), pltpu.SemaphoreType.DMA((3 * n,))] if n else []
    return pl.pallas_call(
        body, name="du_norm_bwd", grid=(ni, nk),
        in_specs=[pl.BlockSpec((tm, w), lambda i, k: (i, 0)) for w in DZ_WIDTHS]
        + [pl.BlockSpec((tm, C), lambda i, k: (i, 0)),
           pl.BlockSpec((tk, D), lambda i, k: (k, 0)), pl.BlockSpec((C, D), lambda i, k: (0, 0)),
           pl.BlockSpec((tm, D), lambda i, k: (i, 0)), pl.BlockSpec((1, D), lambda i, k: (0, 0)),
           pl.BlockSpec((tm, D), lambda i, k: (i, 0))] + [ANY] * n,
        out_specs=[pl.BlockSpec((tm, D), lambda i, k: (i, 0)), pl.BlockSpec((1, D), lambda i, k: (0, 0))] + [ANY] * n,
        out_shape=[jax.ShapeDtypeStruct((T, D), F32), jax.ShapeDtypeStruct((1, D), F32)]
        + [jax.ShapeDtypeStruct(p.shape, p.dtype) for p in parts],
        scratch_shapes=[pltpu.VMEM((tm, D), F32)] + sems,
        compiler_params=_params(("arbitrary", "arbitrary")),
    )(*dzs, dzf, wt, wft, hpad, g, dopad, *parts)


GROWS = 7680


def _dw_in(dzs, dzf, ut):
    tn = 512
    nmain = WMAIN // tn
    first, blocks = [], []
    for w in DZ_WIDTHS:
        first.append(sum(blocks))
        blocks.append(w // tn)

    def body(rq_ref, rk_ref, rv_ref, rg_ref, fq_ref, fk_ref, fv_ref, fg_ref, dzf_ref, ut_ref, o_ref):
        gidx = pl.program_id(0)
        for piece, g0, nb in zip((rq_ref, rk_ref, rv_ref, rg_ref, fq_ref, fk_ref, fv_ref, fg_ref), first, blocks):
            @pl.when((gidx >= g0) & (gidx < g0 + nb))
            def _(piece=piece):
                o_ref[...] = _dot(ut_ref[...], piece[...]).T.astype(BF)

        @pl.when(gidx == nmain)
        def _():
            o_ref[:C, :] = _dot(ut_ref[...], dzf_ref[...]).T.astype(BF)
            o_ref[C:, :] = jnp.zeros((tn - C, D), BF)

    def piece_spec(g0, nb):
        return pl.BlockSpec((T, tn), lambda gidx: (0, jnp.clip(gidx - g0, 0, nb - 1)))

    return pl.pallas_call(
        body, name="dw_in", grid=(nmain + 1,),
        in_specs=[piece_spec(g0, nb) for g0, nb in zip(first, blocks)]
        + [pl.BlockSpec((T, C), lambda gidx: (0, 0)), pl.BlockSpec((D, T), lambda gidx: (0, 0))],
        out_specs=pl.BlockSpec((tn, D), lambda gidx: (gidx, 0)),
        out_shape=jax.ShapeDtypeStruct((GROWS, D), BF),
        compiler_params=pltpu.CompilerParams(dimension_semantics=("arbitrary",), vmem_limit_bytes=DW_VMEM_LIMIT),
    )(*dzs, dzf, ut)


def _token_order(x_po):
    def body(i_ref, o_ref):
        o_ref[...] = i_ref[...]

    return pl.pallas_call(
        body, name="token_order", grid=(NCH,),
        in_specs=[pl.BlockSpec((C, D), lambda i: (_fox_pos(i), 0))],
        out_specs=pl.BlockSpec((C, D), lambda i: (i, 0)),
        out_shape=jax.ShapeDtypeStruct((T, D), x_po.dtype),
        compiler_params=_params(("parallel",)),
    )(x_po)


def _local_step(x, tgt, meta, norm_g, wt, wft, b_f, wout, final_g, chip_sums=None, wout_full=None):
    cst = _constants()
    hpad = jnp.concatenate([jnp.pad(meta, ((PAD, 0), (0, 0))), x], axis=0)
    bf_pad = jnp.pad(b_f, ((0, 0), (0, C - NFF)))
    u, ut = _norm_in(hpad, norm_g)
    z = _mm_nt(u, wt, WMAIN, T // 2, 1024, "in_proj")
    zf = _mm_nt(u, wft, C, T // 2, C, "in_proj_ff")
    r, sprev = _ret_fwd(z, cst)
    ct = _fox_prep(zf, bf_pad, cst)
    if wout_full is None:
        a, g = _fox_fwd(z, ct, cst, None)
    else:
        a, g, landed_wout = _fox_fwd(z, ct, cst, wout)
        wout = wout_full(landed_wout)
    yt, dopad, dob, loss8, dfg = _out_loss(r, z, a, wout, x, tgt, final_g)
    dr, da, dzrg, dzfg, delta = _dy_gate_bwd(dob, wout, r, z, a, cst["seg"])
    dwout = _mm_nn(yt, dob, 512, D, "dw_out", BF)
    dzq_r, dzk_r, dzv_r = _ret_bwd(z, cst, sprev, dr)
    dq_po, drow, dzk_f, dzv_f, dcol = _fox_bwd(z, da, g, delta, ct, cst)
    dzf, dbf = _fox_gate_bwd(drow, dcol, zf, bf_pad, cst)
    dzs = [dzq_r, dzk_r, dzv_r, dzrg, _token_order(dq_po), dzk_f, dzv_f, dzfg]
    gwt = _dw_in(dzs, dzf, ut)
    parts = chip_sums(gwt, dwout) if chip_sums else []
    gh, dng, *landed = _du_norm_bwd(dzs, dzf, wt, wft, hpad, norm_g, dopad, parts)
    return (loss8[0, 0], gh[C:], gh[PAD:C], dng, gwt, dbf[:, :NFF], dwout, dfg, parts, landed)


WOFF, WLEN = 1792, 2048
WHALF = WLEN // 2
LAP = WPADROWS - WOFF


def _gather_weights(own_win, meta):
    half_main, half_lap, half_meta = WOFF // 2, LAP // 2, meta.shape[0] // 2

    def body(win_ref, meta_ref, w_ref, laps_ref, gm_ref, send_sems, recv_sems, local_sems, stage, lapbuf, headbuf):
        x, y, c = _place()
        me_s = 2 * x + y
        sib = (x, y, 1 - c)
        chips = _other_chips(x, y)
        kinds = [
            (lambda h: win_ref.at[pl.ds(half_main * h, half_main)],
             lambda s, h: w_ref.at[pl.ds(WOFF * s + half_main * h, half_main)]),
            (lambda h: win_ref.at[pl.ds(WOFF + half_lap * h, half_lap)],
             lambda s, h: laps_ref.at[s, pl.ds(half_lap * h, half_lap)]),
            (lambda h: meta_ref.at[pl.ds(half_meta * h, half_meta)],
             lambda s, h: gm_ref.at[s, pl.ds(half_meta * h, half_meta)]),
        ]
        own_in = pltpu.make_async_copy(win_ref.at[pl.ds(0, WOFF)], stage, local_sems.at[0])
        own_in.start()
        own_lap_in = pltpu.make_async_copy(win_ref.at[pl.ds(WOFF, LAP)], lapbuf.at[0], local_sems.at[1])
        own_lap_in.start()
        sends, waits = [], []
        for a, (src, dst) in enumerate(kinds):
            for k, (cx, cy, cs) in enumerate(chips):
                sends.append(pltpu.make_async_remote_copy(
                    src_ref=src(c), dst_ref=dst(me_s, c),
                    send_sem=send_sems.at[6 * a + k], recv_sem=recv_sems.at[6 * a + k],
                    device_id=(cx, cy, c), device_id_type=MESH))
                sends[-1].start()
        own_in.wait()
        own_out = pltpu.make_async_copy(stage, w_ref.at[pl.ds(WOFF * me_s, WOFF)], local_sems.at[0])
        own_out.start()
        own_lap_in.wait()
        own_lap_out = pltpu.make_async_copy(lapbuf.at[0], laps_ref.at[me_s], local_sems.at[1])
        own_lap_out.start()
        for a, (src, dst) in enumerate(kinds):
            for k, (cx, cy, cs) in enumerate(chips):
                pltpu.make_async_remote_copy(
                    src_ref=dst(cs, c), dst_ref=dst(cs, c),
                    send_sem=send_sems.at[6 * a + k], recv_sem=recv_sems.at[6 * a + k],
                    device_id=(cx, cy, c), device_id_type=MESH).wait_recv()
                fwd = pltpu.make_async_remote_copy(
                    src_ref=dst(cs, c), dst_ref=dst(cs, c),
                    send_sem=send_sems.at[6 * a + 3 + k], recv_sem=recv_sems.at[6 * a + 3 + k],
                    device_id=sib, device_id_type=MESH)
                fwd.start()
                sends.append(fwd)
                waits.append(pltpu.make_async_remote_copy(
                    src_ref=dst(cs, 1 - c), dst_ref=dst(cs, 1 - c),
                    send_sem=send_sems.at[6 * a + 3 + k], recv_sem=recv_sems.at[6 * a + 3 + k],
                    device_id=sib, device_id_type=MESH))
        for w in waits:
            w.wait_recv()
        for s in sends:
            s.wait_send()
        own_out.wait()
        own_lap_out.wait()
        for s in range(1, 4):
            head = w_ref.at[pl.ds(WOFF * s, LAP)]
            loads = [pltpu.make_async_copy(laps_ref.at[s - 1], lapbuf.at[1], local_sems.at[2]),
                     pltpu.make_async_copy(head, headbuf, local_sems.at[3])]
            for cp in loads:
                cp.start()
            for cp in loads:
                cp.wait()
            headbuf[...] = (headbuf[...].astype(F32) + lapbuf[1].astype(F32)).astype(BF)
            store = pltpu.make_async_copy(headbuf, head, local_sems.at[3])
            store.start()
            store.wait()

    return pl.pallas_call(
        body, name="all_gather_w",
        in_specs=[ANY] * 2, out_specs=[ANY] * 3,
        out_shape=[jax.ShapeDtypeStruct((WMAIN, D), own_win.dtype), jax.ShapeDtypeStruct((4, LAP, D), own_win.dtype),
                   jax.ShapeDtypeStruct((4,) + meta.shape, meta.dtype)],
        scratch_shapes=[pltpu.SemaphoreType.DMA((18,)), pltpu.SemaphoreType.DMA((18,)), pltpu.SemaphoreType.DMA((4,)),
                        pltpu.VMEM((WOFF, D), own_win.dtype), pltpu.VMEM((2, LAP, D), own_win.dtype),
                        pltpu.VMEM((LAP, D), own_win.dtype)],
        compiler_params=pltpu.CompilerParams(vmem_limit_bytes=VMEM_LIMIT),
    )(own_win, meta)


def _pair_swap(gwt, arrs):
    n = len(arrs)

    def body(*refs):
        gw, ins = refs[0], refs[1:n + 1]
        gwo, outs = refs[n + 1], refs[n + 2:2 * n + 2]
        send_sems, recv_sems = refs[2 * n + 2:]
        x, y, c = _place()
        sib = (x, y, 1 - c)
        cps = []
        for k in range(4):
            cps.append(pltpu.make_async_remote_copy(
                src_ref=gw.at[pl.ds(WOFF * k + (1 - c) * WHALF, WHALF)], dst_ref=gwo.at[k],
                send_sem=send_sems.at[k], recv_sem=recv_sems.at[k], device_id=sib, device_id_type=MESH))
        for a in range(n):
            rows = ins[a].shape[1] // 2
            cps.append(pltpu.make_async_remote_copy(
                src_ref=ins[a].at[:, pl.ds((1 - c) * rows, rows)], dst_ref=outs[a],
                send_sem=send_sems.at[4 + a], recv_sem=recv_sems.at[4 + a], device_id=sib, device_id_type=MESH))
        for cp in cps:
            cp.start()
        for cp in cps:
            cp.wait()

    return pl.pallas_call(
        body, name="rs_pair_swap",
        in_specs=[ANY] * (n + 1), out_specs=[ANY] * (n + 1),
        out_shape=[jax.ShapeDtypeStruct((4, WHALF, D), gwt.dtype)]
        + [jax.ShapeDtypeStruct((4, a.shape[1] // 2, a.shape[2]), a.dtype) for a in arrs],
        scratch_shapes=[pltpu.SemaphoreType.DMA((n + 4,)), pltpu.SemaphoreType.DMA((n + 4,))],
    )(gwt, *arrs)


def _add_windows(gwt, recv):
    tb = 256
    nb = WHALF // tb
    c = lax.axis_index("c")

    def body(c_ref, a_ref, b_ref, o_ref):
        o_ref[0] = (a_ref[...].astype(F32) + b_ref[0].astype(F32)).astype(BF)

    return pl.pallas_call(
        body, name="pair_add_in",
        grid_spec=pltpu.PrefetchScalarGridSpec(
            num_scalar_prefetch=1, grid=(4, nb),
            in_specs=[pl.BlockSpec((tb, D), lambda k, i, cr: ((WOFF // tb) * k + nb * cr[0] + i, 0)),
                      pl.BlockSpec((1, tb, D), lambda k, i, cr: (k, i, 0))],
            out_specs=pl.BlockSpec((1, tb, D), lambda k, i, cr: (k, i, 0))),
        out_shape=jax.ShapeDtypeStruct(recv.shape, BF),
        compiler_params=_params(("parallel", "parallel")),
    )(jnp.reshape(c, (1,)).astype(jnp.int32), gwt, recv)


def _chip_exchange(parts, small):
    n = len(parts)

    def body(*refs):
        ins, sm = refs[:n], refs[n]
        outs, smo = refs[n + 1:2 * n + 1], refs[2 * n + 1]
        send_sems, recv_sems = refs[2 * n + 2:]
        cps = _chip_copies(ins, outs, send_sems, recv_sems, by_dest=True)
        cps += _chip_copies([sm], [smo], send_sems.at[pl.ds(3 * n, 3)], recv_sems.at[pl.ds(3 * n, 3)], by_dest=False)
        for cp in cps:
            cp.start()
        for cp in cps:
            cp.wait()

    return pl.pallas_call(
        body, name="rs_chip_exchange",
        in_specs=[ANY] * (n + 1), out_specs=[ANY] * (n + 1),
        out_shape=[jax.ShapeDtypeStruct(p.shape, p.dtype) for p in parts]
        + [jax.ShapeDtypeStruct((4,) + small.shape, small.dtype)],
        scratch_shapes=[pltpu.SemaphoreType.DMA((3 * (n + 1),)), pltpu.SemaphoreType.DMA((3 * (n + 1),))],
    )(*parts, small)


def _pair_send(halves):
    n = len(halves)

    def body(*refs):
        ins, outs = refs[:n], refs[n:2 * n]
        send_sems, recv_sems = refs[2 * n:]
        x, y, c = _place()
        cps = [pltpu.make_async_remote_copy(
            src_ref=ins[a], dst_ref=outs[a], send_sem=send_sems.at[a], recv_sem=recv_sems.at[a],
            device_id=(x, y, 1 - c), device_id_type=MESH) for a in range(n)]
        for cp in cps:
            cp.start()
        for cp in cps:
            cp.wait()

    return pl.pallas_call(
        body, name="rs_pair_send",
        in_specs=[ANY] * n, out_specs=[ANY] * n,
        out_shape=[jax.ShapeDtypeStruct(h.shape, h.dtype) for h in halves],
        scratch_shapes=[pltpu.SemaphoreType.DMA((n,)), pltpu.SemaphoreType.DMA((n,))],
    )(*halves)


def _row_block(rows):
    for tb in (256, 128, 64, 32, 16, 8):
        if rows % tb == 0:
            return tb
    return rows


def _add_halves(full, recv, name, out_dtype):
    _, r2, w = recv.shape
    tb = _row_block(r2)
    nb = r2 // tb
    c = lax.axis_index("c")

    def body(c_ref, a_ref, b_ref, o_ref):
        o_ref[...] = (a_ref[...].astype(F32) + b_ref[...].astype(F32)).astype(o_ref.dtype)

    return pl.pallas_call(
        body, name=name,
        grid_spec=pltpu.PrefetchScalarGridSpec(
            num_scalar_prefetch=1, grid=(4, nb),
            in_specs=[pl.BlockSpec((1, tb, w), lambda s, i, cr: (s, cr[0] * nb + i, 0)),
                      pl.BlockSpec((1, tb, w), lambda s, i, cr: (s, i, 0))],
            out_specs=pl.BlockSpec((1, tb, w), lambda s, i, cr: (s, i, 0))),
        out_shape=jax.ShapeDtypeStruct(recv.shape, out_dtype),
        compiler_params=_params(("parallel", "parallel")),
    )(jnp.reshape(c, (1,)).astype(jnp.int32), full, recv)


def _add2(a, b, name):
    def body(a_ref, b_ref, o_ref):
        o_ref[...] = a_ref[...] + b_ref[...]

    return pl.pallas_call(body, name=name, out_shape=jax.ShapeDtypeStruct(a.shape, a.dtype))(a, b)


def _sum4(buf, own, name):
    _, r, w = buf.shape
    tb = _row_block(r)
    me_s = 2 * lax.axis_index("x") + lax.axis_index("y")
    by_dest = own.ndim == 3

    def body(s_ref, b_ref, own_ref, o_ref):
        mine = (own_ref[0] if by_dest else own_ref[...]).astype(F32)
        terms = [jnp.where(s_ref[0] == t, mine, b_ref[t].astype(F32)) for t in range(4)]
        o_ref[...] = ((terms[0] + terms[1]) + terms[2]) + terms[3]

    own_spec = (pl.BlockSpec((1, tb, w), lambda i, sr: (sr[0], i, 0)) if by_dest
                else pl.BlockSpec((tb, w), lambda i, sr: (i, 0)))
    return pl.pallas_call(
        body, name=name,
        grid_spec=pltpu.PrefetchScalarGridSpec(
            num_scalar_prefetch=1, grid=(r // tb,),
            in_specs=[pl.BlockSpec((4, tb, w), lambda i, sr: (0, i, 0)), own_spec],
            out_specs=pl.BlockSpec((tb, w), lambda i, sr: (i, 0))),
        out_shape=jax.ShapeDtypeStruct((r, w), F32),
        compiler_params=_params(("parallel",)),
    )(jnp.reshape(me_s, (1,)).astype(jnp.int32), buf, own)


def _adamw_math(w, g, m, v):
    mn = B1 * m + (1.0 - B1) * g
    vn = B2 * v + (1.0 - B2) * (g * g)
    m_hat = mn / (1.0 - B1 ** STEP)
    v_hat = vn / (1.0 - B2 ** STEP)
    return -LR * (m_hat / (jnp.sqrt(v_hat) + AEPS) + WD * w), mn, vn


def _adamw(w, g, m, v, name):
    r, c_ = w.shape
    tb = _row_block(r)
    if tb == r and r > 512:
        tb = 256

    def body(w_ref, g_ref, m_ref, v_ref, d_ref, mo_ref, vo_ref):
        d_ref[...], mo_ref[...], vo_ref[...] = _adamw_math(w_ref[...], g_ref[...], m_ref[...], v_ref[...])

    spec = pl.BlockSpec((tb, c_), lambda i: (i, 0))
    return pl.pallas_call(
        body, name=name, grid=(pl.cdiv(r, tb),),
        in_specs=[spec] * 4, out_specs=[spec] * 3,
        out_shape=[jax.ShapeDtypeStruct(w.shape, F32)] * 3,
        compiler_params=_params(("parallel",)),
    )(w, g, m, v)


def _adamw_halves(w, g_mine, g_sib, m, v, name):
    r, c_ = w.shape
    r2 = g_mine.shape[0]
    tb = _row_block(r2)
    nb = r2 // tb
    c = lax.axis_index("c")

    def body(c_ref, w_ref, gm_ref, gs_ref, m_ref, v_ref, g_ref, d_ref, mo_ref, vo_ref):
        g = jnp.where(pl.program_id(0) == c_ref[0], gm_ref[...], gs_ref[...])
        g_ref[...] = g
        d_ref[...], mo_ref[...], vo_ref[...] = _adamw_math(w_ref[...], g, m_ref[...], v_ref[...])

    full = pl.BlockSpec((tb, c_), lambda h, i, cr: (h * nb + i, 0))
    half = pl.BlockSpec((tb, c_), lambda h, i, cr: (i, 0))
    return pl.pallas_call(
        body, name=name,
        grid_spec=pltpu.PrefetchScalarGridSpec(
            num_scalar_prefetch=1, grid=(2, nb),
            in_specs=[full, half, half, full, full], out_specs=[full] * 4),
        out_shape=[jax.ShapeDtypeStruct(w.shape, F32)] * 4,
        compiler_params=_params(("parallel", "parallel")),
    )(jnp.reshape(c, (1,)).astype(jnp.int32), w, g_mine, g_sib, m, v)


def kernel(x, meta_tokens, norm_g, w_in, b_f, w_out, final_g, loss_target, m_meta_tokens, m_norm_g, m_w_in, m_b_f, m_w_out, m_final_g, v_meta_tokens, v_norm_g, v_w_in, v_b_f, v_w_out, v_final_g):
    me_s = 2 * lax.axis_index("x") + lax.axis_index("y")
    core = lax.axis_index("c")
    wt, mt, vt = [jnp.swapaxes(t[0], 0, 1) for t in (w_in, m_w_in, v_w_in)]

    own_win = lax.dynamic_update_slice(jnp.zeros((WPADROWS, D), F32), wt, (4 * me_s, 0)).astype(BF)
    wt_main, laps, gmeta = _gather_weights(own_win, meta_tokens)
    wft = jnp.pad(laps[3, :NFF], ((0, C - NFF), (0, 0)))
    mine = (jnp.arange(4) == me_s)[:, None, None]
    gmeta = jnp.where(mine, meta_tokens[None], gmeta)
    wout_own = w_out[0].astype(BF)
    meta = jnp.concatenate([gmeta[s] for s in range(4)], axis=1)

    def wout_full(landed):
        return jnp.where(mine, wout_own[None], landed).reshape(DMIX, D)

    def chip_sums(gwt, dwout):
        g_out = dwout.reshape(4, DMIX // 4, D)
        r_in, r_out = _pair_swap(gwt, [g_out])
        return [_add_windows(gwt, r_in), _add_halves(g_out, r_out, "pair_add_out", BF)]

    loss, gx, dmeta, dng, gwt, dbf, dwout, dfg, (p_in, p_out), (e_in, e_out) = _local_step(
        x[0], loss_target[0], meta, norm_g, wt_main, wft, b_f, wout_own, final_g.reshape(1, D), chip_sums, wout_full)

    g_meta = jnp.stack([dmeta[:, 256 * s:256 * (s + 1)] for s in range(4)])
    small = jnp.concatenate([dng, dfg, jnp.pad(dbf, ((0, 0), (0, D - NFF))),
                             jnp.pad(jnp.reshape(loss, (1, 1)), ((0, 0), (0, D - 1))),
                             jnp.zeros((4, D), F32)], axis=0)
    e_meta, e_small = _chip_exchange([g_meta], small)
    h_in, h_out = _sum4(e_in, p_in, "sum_in"), _sum4(e_out, p_out, "sum_out")
    h_meta, h_small = _sum4(e_meta, g_meta, "sum_meta"), _sum4(e_small, small, "sum_small")
    s_in, s_out, s_meta, s_small = _pair_send([h_in, h_out, h_meta, h_small])
    gw_meta = _add2(h_meta, s_meta, "pair_add_meta")
    tot = _add2(h_small, s_small, "pair_add_small")
    g_norm, g_final, g_bf, loss_all = tot[0:1], tot[1], tot[2:3, :NFF], tot[3, 0]

    d_meta, nm_meta, nv_meta = _adamw(meta_tokens, gw_meta, m_meta_tokens, v_meta_tokens, "adamw_meta")
    d_norm, nm_norm, nv_norm = _adamw(norm_g, g_norm, m_norm_g, v_norm_g, "adamw_norm")
    window = jnp.concatenate([jnp.where(core == 0, h_in, s_in), jnp.where(core == 0, s_in, h_in)], axis=0)
    gwt_own = lax.dynamic_slice(window, (4 * me_s, 0), (WSH, D))
    d_in, nm_in, nv_in = _adamw(wt, gwt_own, mt, vt, "adamw_in")
    gw_in, d_in, nm_in, nv_in = [jnp.swapaxes(t, 0, 1)[None] for t in (gwt_own, d_in, nm_in, nv_in)]
    d_bf, nm_bf, nv_bf = _adamw(b_f, g_bf, m_b_f, v_b_f, "adamw_bf")
    gw_out, d_out, nm_out, nv_out = _adamw_halves(w_out[0], h_out, s_out, m_w_out[0], v_w_out[0], "adamw_out")
    d_fin, nm_fin, nv_fin = _adamw(final_g.reshape(1, D), g_final.reshape(1, D), m_final_g.reshape(1, D),
                                   v_final_g.reshape(1, D), "adamw_final")
    return (loss_all, gx[None], gw_meta, g_norm, gw_in, g_bf, gw_out[None], g_final,
            d_meta, d_norm, d_in, d_bf, d_out[None], d_fin.reshape(D),
            nm_meta, nm_norm, nm_in, nm_bf, nm_out[None], nm_fin.reshape(D),
            nv_meta, nv_norm, nv_in, nv_bf, nv_out[None], nv_fin.reshape(D))
```

```python
import numpy as np
import jax
import jax.numpy as jnp
from jax import lax
from jax.experimental import pallas as pl
from jax.experimental.pallas import tpu as pltpu

D = 1024
SEQ = 2048
NMETA = 16
C = 128
PAD = C - NMETA
T = PAD + NMETA + SEQ
NCH = T // C
RH, RDK, RDV = 4, 128, 256
FH, FD = 16, 64
NPAIR = FH // 2
WMAIN = 7168
NFF = 16
WIN = WMAIN + NFF
WSH = WIN // 4
WPADROWS = 1824
DMIX = 2048
EPS = 1e-6
NEG = -1e30
RSCALE = RDK ** -0.5
FSCALE = FD ** -0.5
ROPE_BASE = 10000.0
LR, B1, B2, AEPS, WD, STEP = 0.001, 0.9, 0.999, 1e-08, 0.01, 10

BF = jnp.bfloat16
F32 = jnp.float32
NT = (((1,), (1,)), ((), ()))
TN = (((0,), (0,)), ((), ()))
NN_DIMS = (((1,), (0,)), ((), ()))
MESH = pl.DeviceIdType.MESH
ANY = pl.BlockSpec(memory_space=pl.ANY)
VMEM_LIMIT = 48 * 1024 * 1024
DW_VMEM_LIMIT = 56 * 1024 * 1024

GB_R, GB_F = 2, 6
QB_F, KB_F, VB_F = 24, 32, 40


def _dot(a, b):
    return jnp.dot(a, b, preferred_element_type=F32)


def _dg(a, b, dims):
    return lax.dot_general(a, b, dims, preferred_element_type=F32)


def _params(sem=None):
    return pltpu.CompilerParams(dimension_semantics=sem, vmem_limit_bytes=VMEM_LIMIT)


def _constants():
    pos = jnp.arange(T, dtype=F32) - PAD
    inv = ROPE_BASE ** (-jnp.arange(0, RDK, 2, dtype=F32) / RDK)
    ang = pos[:, None] * inv[None, :]
    cos, sin = jnp.cos(ang), jnp.sin(ang)
    cos2 = jnp.concatenate([cos, cos], axis=1)
    sin2 = jnp.concatenate([-sin, sin], axis=1)
    log_gamma = jnp.log1p(-jnp.exp2(-5.0 - jnp.arange(RH, dtype=F32)))
    idx = jnp.arange(C, dtype=F32)
    diff = idx[:, None] - idx[None, :]
    dmask = jnp.where(diff[None] >= 0, jnp.exp(log_gamma[:, None, None] * jnp.maximum(diff, 0.0)[None]), 0.0)
    zeta = jnp.exp(log_gamma[:, None] * (C - 1.0 - idx)[None, :])
    xi = jnp.exp(log_gamma[:, None] * (idx + 1.0)[None, :])
    gdec = jnp.exp(log_gamma * C)
    zeta_b = jnp.broadcast_to(zeta[:, :, None], (RH, C, RDK))
    xi_b = jnp.broadcast_to(xi[:, :, None], (RH, C, RDK))
    gdec_b = jnp.broadcast_to(gdec[:, None, None], (RH, RDK, RDV))
    tri = jnp.asarray(np.tril(np.ones((C, C), np.float32)), dtype=BF)
    head_of_lane = np.arange(FH * FD) // FD
    pick = ((np.arange(FH * FD)[:, None] % FD == 0)
            & (head_of_lane[:, None] == np.arange(C)[None, :])).astype(np.float32)
    seg = (np.arange(C)[:, None] // FD == np.arange(C)[None, :] // FD).astype(np.float32)
    ones_aug = np.concatenate([np.tile((np.arange(C) < FD)[None, :], (C, 1)),
                               np.tile((np.arange(C) >= FD)[None, :], (C, 1))], axis=0).astype(np.float32)
    lane = np.arange(2 * C) % C
    causal = np.where(lane[None, :] <= np.arange(C)[:, None], 0.0, NEG).astype(np.float32)
    mask_bias = np.stack([np.zeros((C, 2 * C), np.float32), causal])
    return dict(cos2=cos2, sin2=sin2, dmask=dmask, zeta=zeta_b, xi=xi_b, gdec=gdec_b, tri=tri,
                mask_bias=jnp.asarray(mask_bias), pick=jnp.asarray(pick, dtype=BF), seg=jnp.asarray(seg, dtype=BF),
                ones_aug=jnp.asarray(ones_aug, dtype=BF))


def _norm_in(hpad, g):
    def body(h_ref, g_ref, u_ref, ut_ref):
        h = h_ref[...]
        rs = lax.rsqrt(jnp.mean(h * h, axis=1, keepdims=True) + EPS)
        u = h * rs * g_ref[...]
        u_ref[...] = u.astype(BF)
        ut_ref[...] = u.T.astype(BF)

    return pl.pallas_call(
        body, name="norm_in", grid=(NCH,),
        in_specs=[pl.BlockSpec((C, D), lambda i: (i, 0)), pl.BlockSpec((1, D), lambda i: (0, 0))],
        out_specs=[pl.BlockSpec((C, D), lambda i: (i, 0)), pl.BlockSpec((D, C), lambda i: (0, i))],
        out_shape=[jax.ShapeDtypeStruct((T, D), BF), jax.ShapeDtypeStruct((D, T), BF)],
        compiler_params=_params(("parallel",)),
    )(hpad, g)


def _mm_nt(a, b, n, tm, tn, name):
    m, k = a.shape

    def body(a_ref, b_ref, o_ref):
        o_ref[...] = _dg(a_ref[...], b_ref[...], NT)

    return pl.pallas_call(
        body, name=name, grid=(m // tm, n // tn),
        in_specs=[pl.BlockSpec((tm, k), lambda i, j: (i, 0)), pl.BlockSpec((tn, k), lambda i, j: (j, 0))],
        out_specs=pl.BlockSpec((tm, tn), lambda i, j: (i, j)),
        out_shape=jax.ShapeDtypeStruct((m, n), F32),
        compiler_params=_params(("parallel", "parallel")),
    )(a, b)


def _mm_nn(a, b, tm, tn, name, out_dtype=F32):
    m, k = a.shape
    _, n = b.shape

    def body(a_ref, b_ref, o_ref):
        o_ref[...] = _dot(a_ref[...], b_ref[...]).astype(out_dtype)

    return pl.pallas_call(
        body, name=name, grid=(m // tm, n // tn),
        in_specs=[pl.BlockSpec((tm, k), lambda i, j: (i, 0)), pl.BlockSpec((k, tn), lambda i, j: (0, j))],
        out_specs=pl.BlockSpec((tm, tn), lambda i, j: (i, j)),
        out_shape=jax.ShapeDtypeStruct((m, n), out_dtype),
        compiler_params=_params(("parallel", "parallel")),
    )(a, b)


def _rot(x, cos2, sin2):
    return x * cos2 + pltpu.roll(x, 64, 1) * sin2


def _ret_specs(chunk):
    whole = lambda shape: pl.BlockSpec(shape, lambda n: (0,) * len(shape))
    return [
        pl.BlockSpec((C, RH * RDK), lambda n: (chunk(n), 0)),
        pl.BlockSpec((C, RH * RDK), lambda n: (chunk(n), 1)),
        pl.BlockSpec((C, RH * RDV), lambda n: (chunk(n), 1)),
        pl.BlockSpec((C, RDK), lambda n: (chunk(n), 0)),
        pl.BlockSpec((C, RDK), lambda n: (chunk(n), 0)),
        whole((RH, C, C)), whole((RH, C, RDK)), whole((RH, C, RDK)), whole((RH, RDK, RDV)),
    ]


def _ret_heads(q_ref, k_ref, v_ref, cos, sin):
    qr = [_rot(q_ref[:, RDK * h:RDK * (h + 1)], cos, sin) for h in range(RH)]
    kr = [_rot(k_ref[:, RDK * h:RDK * (h + 1)], cos, sin) * RSCALE for h in range(RH)]
    vb = [v_ref[:, RDV * h:RDV * (h + 1)].astype(BF) for h in range(RH)]
    return qr, kr, [t.astype(BF) for t in qr], [t.astype(BF) for t in kr], vb


def _ret_fwd(z, cst):
    def body(q_ref, k_ref, v_ref, cos_ref, sin_ref, dm_ref, xi_ref, zt_ref, gd_ref, r_ref, sp_ref, st):
        n = pl.program_id(0)

        @pl.when(n == 0)
        def _():
            st[...] = jnp.zeros_like(st)

        hs = range(RH)
        qr, kr, qb, kb, vb = _ret_heads(q_ref, k_ref, v_ref, cos_ref[...], sin_ref[...])
        sd = [(_dg(qb[h], kb[h], NT) * dm_ref[h]).astype(BF) for h in hs]
        state = [st[h] for h in hs]
        qx = [(qr[h] * xi_ref[h]).astype(BF) for h in hs]
        kz = [(kr[h] * zt_ref[h]).astype(BF) for h in hs]
        out = [_dot(sd[h], vb[h]) + _dot(qx[h], state[h].astype(BF)) for h in hs]
        kv = [_dg(kz[h], vb[h], TN) for h in hs]
        for h in hs:
            sp_ref[0, h] = state[h]
            r_ref[:, RDV * h:RDV * (h + 1)] = out[h]
            st[h] = state[h] * gd_ref[h] + kv[h]

    return pl.pallas_call(
        body, name="ret_fwd", grid=(NCH,),
        in_specs=_ret_specs(lambda n: n),
        out_specs=[pl.BlockSpec((C, RH * RDV), lambda n: (n, 0)),
                   pl.BlockSpec((1, RH, RDK, RDV), lambda n: (n, 0, 0, 0))],
        out_shape=[jax.ShapeDtypeStruct((T, RH * RDV), F32), jax.ShapeDtypeStruct((NCH, RH, RDK, RDV), F32)],
        scratch_shapes=[pltpu.VMEM((RH, RDK, RDV), F32)],
        compiler_params=_params(("arbitrary",)),
    )(z, z, z, cst["cos2"], cst["sin2"], cst["dmask"], cst["xi"], cst["zeta"], cst["gdec"])


def _ret_bwd(z, cst, sprev, dr):
    def body(q_ref, k_ref, v_ref, cos_ref, sin_ref, dm_ref, xi_ref, zt_ref, gd_ref, sp_ref, dr_ref,
             dq_ref, dk_ref, dv_ref, gst):
        i = pl.program_id(0)

        @pl.when(i == 0)
        def _():
            gst[...] = jnp.zeros_like(gst)

        hs = range(RH)
        cos, sin = cos_ref[...], sin_ref[...]
        qr, kr, qb, kb, vb = _ret_heads(q_ref, k_ref, v_ref, cos, sin)
        dm = [dm_ref[h] for h in hs]
        xi = [xi_ref[h] for h in hs]
        zt = [zt_ref[h] for h in hs]
        sd = [(_dg(qb[h], kb[h], NT) * dm[h]).astype(BF) for h in hs]
        qx = [(qr[h] * xi[h]).astype(BF) for h in hs]
        kz = [(kr[h] * zt[h]).astype(BF) for h in hs]
        drb = [dr_ref[:, RDV * h:RDV * (h + 1)] for h in hs]
        sb = [sp_ref[0, h].astype(BF) for h in hs]
        g = [gst[h] for h in hs]
        gb = [t.astype(BF) for t in g]
        ds = [(_dg(drb[h], vb[h], NT) * dm[h]).astype(BF) for h in hs]
        dq = [_dot(ds[h], kb[h]) + _dg(drb[h], sb[h], NT) * xi[h] for h in hs]
        dk = [(_dg(ds[h], qb[h], TN) + _dg(vb[h], gb[h], NT) * zt[h]) * RSCALE for h in hs]
        dv = [_dg(sd[h], drb[h], TN) + _dot(kz[h], gb[h]) for h in hs]
        gn = [g[h] * gd_ref[h] + _dg(qx[h], drb[h], TN) for h in hs]
        for h in hs:
            gst[h] = gn[h]
            dq_ref[:, RDK * h:RDK * (h + 1)] = (dq[h] * cos + pltpu.roll(dq[h] * sin, 64, 1)).astype(BF)
            dk_ref[:, RDK * h:RDK * (h + 1)] = (dk[h] * cos + pltpu.roll(dk[h] * sin, 64, 1)).astype(BF)
            dv_ref[:, RDV * h:RDV * (h + 1)] = dv[h].astype(BF)

    rev = lambda n: NCH - 1 - n
    return pl.pallas_call(
        body, name="ret_bwd", grid=(NCH,),
        in_specs=_ret_specs(rev) + [
            pl.BlockSpec((1, RH, RDK, RDV), lambda n: (rev(n), 0, 0, 0)),
            pl.BlockSpec((C, RH * RDV), lambda n: (rev(n), 0)),
        ],
        out_specs=[pl.BlockSpec((C, RH * RDK), lambda n: (rev(n), 0)),
                   pl.BlockSpec((C, RH * RDK), lambda n: (rev(n), 0)),
                   pl.BlockSpec((C, RH * RDV), lambda n: (rev(n), 0))],
        out_shape=[jax.ShapeDtypeStruct((T, RH * RDK), BF), jax.ShapeDtypeStruct((T, RH * RDK), BF),
                   jax.ShapeDtypeStruct((T, RH * RDV), BF)],
        scratch_shapes=[pltpu.VMEM((RH, RDK, RDV), F32)],
        compiler_params=_params(("arbitrary",)),
    )(z, z, z, cst["cos2"], cst["sin2"], cst["dmask"], cst["xi"], cst["zeta"], cst["gdec"], sprev, dr)


def _place():
    x, y, c = lax.axis_index("x"), lax.axis_index("y"), lax.axis_index("c")
    return x, y, c


def _other_chips(x, y):
    return [(1 - x, y, 2 * (1 - x) + y), (x, 1 - y, 2 * x + (1 - y)), (1 - x, 1 - y, 2 * (1 - x) + (1 - y))]


def _chip_copies(srcs, lands, send_sems, recv_sems, by_dest):
    x, y, c = _place()
    me_s = 2 * x + y
    return [pltpu.make_async_remote_copy(
        src_ref=src.at[cs] if by_dest else src, dst_ref=land.at[me_s],
        send_sem=send_sems.at[3 * a + j], recv_sem=recv_sems.at[3 * a + j],
        device_id=(cx, cy, c), device_id_type=MESH)
        for a, (src, land) in enumerate(zip(srcs, lands)) for j, (cx, cy, cs) in enumerate(_other_chips(x, y))]


def _split_dot(x, mat01, dims=NN_DIMS, x_first=True):
    acc, rest = None, x
    for _ in range(3):
        piece = rest.astype(BF)
        part = _dg(piece, mat01, dims) if x_first else _dg(mat01, piece, dims)
        acc = part if acc is None else acc + part
        rest = rest - piece.astype(F32)
    return acc


def _log_sigmoid(x):
    return -(jnp.maximum(-x, 0.0) + jnp.log1p(jnp.exp(-jnp.abs(x))))


def _fox_prep(zf, bf_pad, cst):
    def body(zf_ref, b_ref, tri_ref, ct_ref, carry):
        n = pl.program_id(0)

        @pl.when(n == 0)
        def _():
            carry[...] = jnp.zeros_like(carry)

        ls = _log_sigmoid(zf_ref[...] + b_ref[...])
        row = n * C + lax.broadcasted_iota(jnp.int32, (C, C), 0)
        lf = jnp.where(row >= PAD, ls, 0.0)
        cc = _split_dot(lf, tri_ref[...], x_first=False) + carry[0:1, :]
        carry[...] = jnp.broadcast_to(cc[C - 1:C, :], carry.shape)
        pos = n * C + lax.broadcasted_iota(jnp.int32, (FH, C), 1)
        ct_ref[0] = jnp.where(pos >= PAD, cc.T[:FH, :], -NEG)

    return pl.pallas_call(
        body, name="fox_prep", grid=(NCH,),
        in_specs=[pl.BlockSpec((C, C), lambda n: (n, 0)), pl.BlockSpec((1, C), lambda n: (0, 0)),
                  pl.BlockSpec((C, C), lambda n: (0, 0))],
        out_specs=pl.BlockSpec((1, FH, C), lambda n: (n, 0, 0)),
        out_shape=jax.ShapeDtypeStruct((NCH, FH, C), F32),
        scratch_shapes=[pltpu.VMEM((8, C), F32)],
        compiler_params=_params(("arbitrary",)),
    )(zf, bf_pad, cst["tri"])


def _lo_lanes(shape):
    return lax.broadcasted_iota(jnp.int32, shape, 1) < FD


def _split_heads(x):
    lo = _lo_lanes(x.shape)
    zero = jnp.zeros_like(x)
    return jnp.concatenate([jnp.where(lo, x, zero), jnp.where(lo, zero, x)], axis=0)


def _spread2(x):
    lo = _lo_lanes(x.shape)
    r = pltpu.roll(x, FD, 1)
    return jnp.concatenate([jnp.where(lo, x, r), jnp.where(lo, r, x)], axis=1)


NSTEP = (NCH + 1) // 2
NTILE = NCH + 1
TROWS = T + C


def _fox_tile(s, t):
    second = t > s
    return second.astype(jnp.int32), jnp.where(second, t - s - 1, s - t)


def _fox_pos(i):
    return jnp.where(i < NSTEP, 2 * i, 2 * (NCH - 1 - i) + 1)


FOX_ORDER = [2 * i if i < NSTEP else 2 * (NCH - 1 - i) + 1 for i in range(NCH)]


def _fox_pair_specs():
    first = pl.BlockSpec((C, C), lambda p, s: (2 * s, p))
    second = pl.BlockSpec((C, C), lambda p, s: (jnp.where(s == NSTEP - 1, 2 * s, 2 * s + 1), p))
    both = pl.BlockSpec((2 * C, C), lambda p, s: (s, p))
    return first, second, both


def _fox_q_specs():
    return (pl.BlockSpec((C, C), lambda p, s: (s, QB_F + p)),
            pl.BlockSpec((C, C), lambda p, s: (NCH - 1 - s, QB_F + p)))


def _fox_key_bias(ct_ref, p, j):
    return jnp.concatenate([ct_ref[j, pl.ds(2 * p, 1), :], ct_ref[j, pl.ds(2 * p + 1, 1), :]], axis=1)


def _fox_fwd(z, ct, cst, share):
    n = 0 if share is None else 1

    def body(qa_ref, qb_ref, k_ref, v_ref, ct_ref, ones_ref, mb_ref, *rest):
        share_refs, (a_ref, g_ref), land_refs = rest[:n], rest[n:n + 2], rest[n + 2:2 * n + 2]
        kks, vvs, q2, m2, sbuf = rest[2 * n + 2:2 * n + 7]
        p, s = pl.program_id(0), pl.program_id(1)
        if n:
            copies = _chip_copies(share_refs, land_refs, *rest[2 * n + 7:], by_dest=False)

            @pl.when((p == 0) & (s == 0))
            def _():
                for cp in copies:
                    cp.start()

            @pl.when((p == NPAIR - 1) & (s == NSTEP - 1))
            def _():
                for cp in copies:
                    cp.wait()

        @pl.when(s == 0)
        def _():
            ones = ones_ref[...]

            def prep(j, carry):
                rows = pl.ds(pl.multiple_of(j * C, C), C)
                kks[j] = _split_heads(k_ref[rows, :]).astype(BF)
                vvs[j] = jnp.concatenate([_split_heads(v_ref[rows, :]).astype(BF), ones], axis=1)
                return carry

            lax.fori_loop(0, NCH, prep, 0)

        q2[0] = (qa_ref[...] * FSCALE).astype(BF)
        q2[1] = (qb_ref[...] * FSCALE).astype(BF)

        tiles = [_fox_tile(s, t) for t in range(NTILE)]
        causal = mb_ref[1]
        neg = jnp.full((C, 2 * C), NEG, F32)
        run, first = neg, neg
        for t, (sel, j) in enumerate(tiles):
            st = _dg(q2[sel], kks[j], NT) - _fox_key_bias(ct_ref, p, j)
            if t in (0, NTILE - 1):
                st = st + causal
            sbuf[t] = st
            run = jnp.maximum(jnp.where(t == s + 1, neg, run), st)
            first = jnp.where(t == s, run, first)
        for w, mx in enumerate((first, run)):
            m2[w] = jnp.concatenate(
                [jnp.broadcast_to(jnp.max(mx[:, :C], axis=1, keepdims=True), (C, C)),
                 jnp.broadcast_to(jnp.max(mx[:, C:], axis=1, keepdims=True), (C, C))], axis=1)

        zero = jnp.zeros((C, 2 * C), F32)
        run, first = zero, zero
        for t, (sel, j) in enumerate(tiles):
            run = jnp.where(t == s + 1, zero, run) + _dot(jnp.exp(sbuf[t] - m2[sel]).astype(BF), vvs[j])
            first = jnp.where(t == s, run, first)
        lo = _lo_lanes((C, C))
        for w, res in enumerate((first, run)):
            l = res[:, C:]
            a_ref[C * w:C * (w + 1), :] = res[:, :C] / l
            mw = m2[w]
            g_ref[C * w:C * (w + 1), :] = -(jnp.where(lo, mw[:, :C], mw[:, C:]) + jnp.log(l))

    qa, qb = _fox_q_specs()
    both = _fox_pair_specs()[2]
    return pl.pallas_call(
        body, name="fox_fwd", grid=(NPAIR, NSTEP),
        in_specs=[qa, qb,
                  pl.BlockSpec((T, C), lambda p, s: (0, KB_F + p)),
                  pl.BlockSpec((T, C), lambda p, s: (0, VB_F + p)),
                  pl.BlockSpec((NCH, FH, C), lambda p, s: (0, 0, 0)),
                  pl.BlockSpec((2 * C, C), lambda p, s: (0, 0)),
                  pl.BlockSpec((2, C, 2 * C), lambda p, s: (0, 0, 0))] + [ANY] * n,
        out_specs=[both, both] + [ANY] * n,
        out_shape=[jax.ShapeDtypeStruct((TROWS, FH * FD), F32)] * 2
        + ([jax.ShapeDtypeStruct((4,) + share.shape, share.dtype)] if n else []),
        scratch_shapes=[pltpu.VMEM((NCH, 2 * C, C), BF), pltpu.VMEM((NCH, 2 * C, 2 * C), BF),
                        pltpu.VMEM((2, C, C), BF), pltpu.VMEM((2, C, 2 * C), F32),
                        pltpu.VMEM((NTILE, C, 2 * C), F32)]
        + [pltpu.SemaphoreType.DMA((3,)), pltpu.SemaphoreType.DMA((3,))] * n,
        compiler_params=_params(("arbitrary", "arbitrary")),
    )(z, z, z, z, ct, cst["ones_aug"], cst["mask_bias"], *([share] * n))


def _fox_bwd(z, da, g, delta, ct, cst):
    grp = 9

    def body(qa_ref, qb_ref, daa_ref, dab_ref, ga_ref, gb_ref, dla_ref, dlb_ref, k_ref, v_ref, ct_ref, ones_ref,
             mb_ref, dq_ref, dr_ref, dk_ref, dv_ref, dcs_ref,
             kks, vvs, q2, qq2, dd2, da2, gi2, dl2, dq2, dvb, dkb, dkacc, dvacc, csacc):
        p, s = pl.program_id(0), pl.program_id(1)
        ones = ones_ref[...]

        @pl.when(s == 0)
        def _():
            dkacc[...] = jnp.zeros_like(dkacc)
            dvacc[...] = jnp.zeros_like(dvacc)
            csacc[...] = jnp.zeros_like(csacc)

            def prep(j, carry):
                rows = pl.ds(pl.multiple_of(j * C, C), C)
                kks[j] = _split_heads(k_ref[rows, :]).astype(BF)
                vvs[j] = _split_heads(v_ref[rows, :]).astype(BF)
                return carry

            lax.fori_loop(0, NCH, prep, 0)

        for w, (q_ref, d_ref, g_ref, l_ref) in enumerate(((qa_ref, daa_ref, ga_ref, dla_ref),
                                                          (qb_ref, dab_ref, gb_ref, dlb_ref))):
            qf = q_ref[...]
            q2[w] = (qf * FSCALE).astype(BF)
            qq2[w] = jnp.concatenate([_split_heads(qf).astype(BF), ones], axis=1)
            da2[w] = d_ref[...]
            dd2[w] = _split_heads(d_ref[...].astype(F32)).astype(BF)
            gi2[w] = _spread2(g_ref[...])
            dl2[w] = _spread2(l_ref[...])
        dq2[...] = jnp.zeros_like(dq2)
        zero = jnp.zeros((C, 2 * C), F32)

        def group(gi, carry):
            ts = [gi * grp + u for u in range(grp)]
            tiles = [_fox_tile(s, t) for t in ts]
            kk = [kks[j] for _, j in tiles]
            ss = [_dg(q2[sel], kj, NT) + (gi2[sel] - _fox_key_bias(ct_ref, p, j)) for kj, (sel, j) in zip(kk, tiles)]
            ss[0] = ss[0] + mb_ref[(gi == 0).astype(jnp.int32)]
            ss[-1] = ss[-1] + mb_ref[(gi == 1).astype(jnp.int32)]
            dps = [_dg(da2[sel], vvs[j], NT) for sel, j in tiles]
            pes = [jnp.exp(st) for st in ss]
            dss = [pe * (dp - dl2[sel]) * FSCALE for pe, dp, (sel, _) in zip(pes, dps, tiles)]
            pts = [jnp.concatenate([pe[:, :C].T, pe[:, C:].T], axis=1).astype(BF) for pe in pes]
            dsts = [jnp.concatenate([ds[:, :C].T, ds[:, C:].T], axis=1).astype(BF) for ds in dss]
            dvs = [_dot(pt, dd2[sel]) for pt, (sel, _) in zip(pts, tiles)]
            rs = [_dot(dst, qq2[sel]) for dst, (sel, _) in zip(dsts, tiles)]
            parts = [_dot(ds.astype(BF), jnp.concatenate([kj, ones], axis=1)) for ds, kj in zip(dss, kk)]
            for t, dv, rr in zip(ts, dvs, rs):
                dvb[t] = dv
                dkb[t] = rr
            pa, pb = zero, zero
            for t, part in zip(ts, parts):
                pa = pa + jnp.where(t <= s, part, zero)
                pb = pb + jnp.where(t <= s, zero, part)
            dq2[0] += pa
            dq2[1] += pb
            return carry

        ntile = jnp.where(s == NSTEP - 1, grp, NTILE)
        lax.fori_loop(0, ntile // grp, group, 0)

        def scatter(t, carry):
            _, j = _fox_tile(s, t)
            r = pl.ds(pl.multiple_of(j * C, C), C)
            dvacc[r, :] += dvb[t]
            dkacc[r, :] += dkb[t, :, :C]
            csacc[r, :] += dkb[t, :, C:]
            return carry

        lax.fori_loop(0, ntile, scatter, 0)
        for w in range(2):
            res = dq2[w]
            dq_ref[C * w:C * (w + 1), :] = res[:, :C].astype(BF)
            dr_ref[C * w:C * (w + 1), :] = res[:, C:]

        @pl.when(s == NSTEP - 1)
        def _():
            dk_ref[...] = dkacc[...].astype(BF)
            dv_ref[...] = dvacc[...].astype(BF)
            dcs_ref[...] = csacc[...]

    qa, qb = _fox_q_specs()
    ba, bb, both = _fox_pair_specs()
    col = pl.BlockSpec((T, C), lambda p, s: (0, p))
    return pl.pallas_call(
        body, name="fox_bwd", grid=(NPAIR, NSTEP),
        in_specs=[qa, qb, ba, bb, ba, bb, ba, bb,
                  pl.BlockSpec((T, C), lambda p, s: (0, KB_F + p)),
                  pl.BlockSpec((T, C), lambda p, s: (0, VB_F + p)),
                  pl.BlockSpec((NCH, FH, C), lambda p, s: (0, 0, 0)),
                  pl.BlockSpec((2 * C, C), lambda p, s: (0, 0)),
                  pl.BlockSpec((2, C, 2 * C), lambda p, s: (0, 0, 0))],
        out_specs=[both, both, col, col, col],
        out_shape=[jax.ShapeDtypeStruct((TROWS, FH * FD), BF), jax.ShapeDtypeStruct((TROWS, FH * FD), F32),
                   jax.ShapeDtypeStruct((T, FH * FD), BF), jax.ShapeDtypeStruct((T, FH * FD), BF),
                   jax.ShapeDtypeStruct((T, FH * FD), F32)],
        scratch_shapes=[pltpu.VMEM((NCH, 2 * C, C), BF), pltpu.VMEM((NCH, 2 * C, C), BF),
                        pltpu.VMEM((2, C, C), BF), pltpu.VMEM((2, 2 * C, 2 * C), BF), pltpu.VMEM((2, 2 * C, C), BF),
                        pltpu.VMEM((2, C, C), BF), pltpu.VMEM((2, C, 2 * C), F32), pltpu.VMEM((2, C, 2 * C), F32),
                        pltpu.VMEM((2, C, 2 * C), F32),
                        pltpu.VMEM((NTILE, C, C), F32), pltpu.VMEM((NTILE, C, 2 * C), F32),
                        pltpu.VMEM((T, C), F32), pltpu.VMEM((T, C), F32), pltpu.VMEM((T, C), F32)],
        compiler_params=_params(("parallel", "arbitrary")),
    )(z, z, da, da, g, g, delta, delta, z, z, ct, cst["ones_aug"], cst["mask_bias"])


def _fox_gate_bwd(drow, dcol, zf, bf_pad, cst):
    def body(dr_ref, dc_ref, zf_ref, b_ref, tri_ref, pick_ref, dff_ref, db_ref, carry):
        s = pl.program_id(0)
        n = NCH - 1 - s

        @pl.when(s == 0)
        def _():
            carry[...] = jnp.zeros_like(carry)
            db_ref[...] = jnp.zeros_like(db_ref)

        dcb = _split_dot((dr_ref[...] - dc_ref[...]) * (1.0 / FSCALE), pick_ref[...])
        suf = _split_dot(dcb, tri_ref[...], TN, x_first=False) + carry[0:1, :]
        carry[...] = jnp.broadcast_to(suf[0:1, :], carry.shape)
        x = zf_ref[...] + b_ref[...]
        row = n * C + lax.broadcasted_iota(jnp.int32, (C, C), 0)
        dff = jnp.where(row >= PAD, suf * (1.0 - jax.nn.sigmoid(x)), 0.0)
        dff_ref[...] = dff.astype(BF)
        db_ref[...] += jnp.sum(dff, axis=0, keepdims=True)

    rev = lambda s: (NCH - 1 - s, 0)
    return pl.pallas_call(
        body, name="fox_gate_bwd", grid=(NCH,),
        in_specs=[pl.BlockSpec((C, FH * FD), lambda s: (_fox_pos(NCH - 1 - s), 0)),
                  pl.BlockSpec((C, FH * FD), rev), pl.BlockSpec((C, C), rev),
                  pl.BlockSpec((1, C), lambda s: (0, 0)), pl.BlockSpec((C, C), lambda s: (0, 0)),
                  pl.BlockSpec((FH * FD, C), lambda s: (0, 0))],
        out_specs=[pl.BlockSpec((C, C), rev), pl.BlockSpec((1, C), lambda s: (0, 0))],
        out_shape=[jax.ShapeDtypeStruct((T, C), BF), jax.ShapeDtypeStruct((1, C), F32)],
        scratch_shapes=[pltpu.VMEM((8, C), F32)],
        compiler_params=_params(("arbitrary",)),
    )(drow, dcol, zf, bf_pad, cst["tri"], cst["pick"])


def _head_norm(r):
    rn, rs = [], []
    for h in range(RH):
        rh = r[:, RDV * h:RDV * (h + 1)]
        s = lax.rsqrt(jnp.mean(rh * rh, axis=1, keepdims=True) + EPS)
        rn.append(rh * s)
        rs.append(s)
    return jnp.concatenate(rn, axis=1), rs


def _gated(r, rg, a, fg):
    rn, _ = _head_norm(r)
    return jnp.concatenate([rn * (rg * jax.nn.sigmoid(rg)), a * (fg * jax.nn.sigmoid(fg))], axis=1)


def _out_loss(r, z, a, wout, x, tgt, fgain):
    def body(r_ref, rg_ref, a_ref, fg_ref, w_ref, x_ref, t_ref, g_ref, yt_ref, do_ref, dob_ref, loss_ref, dg_ref):
        i = pl.program_id(0)

        @pl.when(i == 0)
        def _():
            yt_ref[...] = jnp.zeros_like(yt_ref)
            do_ref[...] = jnp.zeros_like(do_ref)
            dob_ref[...] = jnp.zeros_like(dob_ref)
            loss_ref[...] = jnp.zeros_like(loss_ref)
            dg_ref[...] = jnp.zeros_like(dg_ref)

        @pl.when(i > 0)
        def _():
            y = _gated(r_ref[...], rg_ref[...], a_ref[...], fg_ref[...])
            yt_ref[...] = y.T.astype(BF)
            o = x_ref[...] + _dot(y.astype(BF), w_ref[...])
            rs = lax.rsqrt(jnp.mean(o * o, axis=1, keepdims=True) + EPS)
            on = o * rs
            g = g_ref[...]
            e = on * g - t_ref[...]
            loss_ref[...] += 0.5 * jnp.sum(jnp.mean(e * e, axis=1, keepdims=True))
            dyh = e * (1.0 / D)
            dg_ref[...] += jnp.sum(dyh * on, axis=0, keepdims=True)
            don = dyh * g
            do = rs * (don - on * jnp.mean(don * on, axis=1, keepdims=True))
            do_ref[...] = do
            dob_ref[...] = do.astype(BF)

    tok = lambda i: (jnp.maximum(i - 1, 0), 0)
    return pl.pallas_call(
        body, name="out_loss", grid=(NCH,),
        in_specs=[pl.BlockSpec((C, D), lambda i: (i, 0)), pl.BlockSpec((C, D), lambda i: (i, GB_R)),
                  pl.BlockSpec((C, D), lambda i: (_fox_pos(i), 0)), pl.BlockSpec((C, D), lambda i: (i, GB_F)),
                  pl.BlockSpec((DMIX, D), lambda i: (0, 0)),
                  pl.BlockSpec((C, D), tok), pl.BlockSpec((C, D), tok), pl.BlockSpec((1, D), lambda i: (0, 0))],
        out_specs=[pl.BlockSpec((DMIX, C), lambda i: (0, i)), pl.BlockSpec((C, D), lambda i: (i, 0)),
                   pl.BlockSpec((C, D), lambda i: (i, 0)), pl.BlockSpec((8, C), lambda i: (0, 0)),
                   pl.BlockSpec((1, D), lambda i: (0, 0))],
        out_shape=[jax.ShapeDtypeStruct((DMIX, T), BF), jax.ShapeDtypeStruct((T, D), F32),
                   jax.ShapeDtypeStruct((T, D), BF), jax.ShapeDtypeStruct((8, C), F32),
                   jax.ShapeDtypeStruct((1, D), F32)],
        compiler_params=_params(("arbitrary",)),
    )(r, z, a, z, wout, x, tgt, fgain)


def _silu_and_grad(x):
    s = jax.nn.sigmoid(x)
    return x * s, s * (1.0 + x * (1.0 - s))


def _dy_gate_bwd(dob, wout, r, z, a, seg):
    def body(do_ref, w_ref, r_ref, rg_ref, a_ref, fg_ref, seg_ref, dr_ref, da_ref, drg_ref, dfg_ref, dl_ref):
        dy = _dg(do_ref[...], w_ref[...], NT)
        a_ = a_ref[...]
        rn, rs = _head_norm(r_ref[...])
        silu_rg, dsilu_rg = _silu_and_grad(rg_ref[...])
        silu_fg, dsilu_fg = _silu_and_grad(fg_ref[...])
        dyr, dyf = dy[:, :D], dy[:, D:]
        drn = dyr * silu_rg
        drg_ref[...] = (dyr * rn * dsilu_rg).astype(BF)
        for h in range(RH):
            sl = slice(RDV * h, RDV * (h + 1))
            dh, nh = drn[:, sl], rn[:, sl]
            dr_ref[:, sl] = (rs[h] * (dh - nh * jnp.mean(dh * nh, axis=1, keepdims=True))).astype(BF)
        dab = (dyf * silu_fg).astype(BF)
        da_ref[...] = dab
        dfg_ref[...] = (dyf * a_ * dsilu_fg).astype(BF)
        prod = dab.astype(F32) * a_
        segm = seg_ref[...]
        for p in range(NPAIR):
            sl = slice(C * p, C * (p + 1))
            hi = prod[:, sl].astype(BF)
            lo = (prod[:, sl] - hi.astype(F32)).astype(BF)
            dl_ref[:, sl] = _dot(hi, segm) + _dot(lo, segm)

    row = pl.BlockSpec((C, D), lambda i: (i, 0))
    fox = pl.BlockSpec((C, D), lambda i: (_fox_pos(i), 0))
    return pl.pallas_call(
        body, name="dy_gate_bwd", grid=(NCH,),
        in_specs=[row, pl.BlockSpec((DMIX, D), lambda i: (0, 0)),
                  row, pl.BlockSpec((C, D), lambda i: (i, GB_R)),
                  fox, pl.BlockSpec((C, D), lambda i: (i, GB_F)),
                  pl.BlockSpec((C, C), lambda i: (0, 0))],
        out_specs=[row, fox, row, row, fox],
        out_shape=[jax.ShapeDtypeStruct((T, D), BF), jax.ShapeDtypeStruct((TROWS, D), BF),
                   jax.ShapeDtypeStruct((T, D), BF), jax.ShapeDtypeStruct((T, D), BF),
                   jax.ShapeDtypeStruct((TROWS, D), F32)],
        compiler_params=_params(("parallel",)),
    )(dob, wout, r, z, a, z, seg)


DZ_WIDTHS = (512, 512, 1024, 1024, 1024, 1024, 1024, 1024)


def _du_norm_bwd(dzs, dzf, wt, wft, hpad, g, dopad, parts=()):
    tm, tk = 544, 1024
    nk = WMAIN // tk
    ni = T // tm
    n = len(parts)

    def body(rq_ref, rk_ref, rv_ref, rg_ref, fq_ref, fk_ref, fv_ref, fg_ref, dzf_ref, w_ref, wf_ref, h_ref, g_ref,
             do_ref, *rest):
        part_refs, (gh_ref, dg_ref), land_refs = rest[:n], rest[n:n + 2], rest[n + 2:2 * n + 2]
        acc = rest[2 * n + 2]
        i, k = pl.program_id(0), pl.program_id(1)

        if n:
            send_sems, recv_sems = rest[2 * n + 3:]
            copies = _chip_copies(part_refs, land_refs, send_sems, recv_sems, by_dest=True)

            @pl.when((i == 0) & (k == 0))
            def _():
                for cp in copies:
                    cp.start()

            @pl.when((i == ni - 1) & (k == nk - 1))
            def _():
                for cp in copies:
                    cp.wait()

        @pl.when(k == 0)
        def _():
            acc[...] = (_dot(dzf_ref[...], wf_ref[...]) + _dot(rq_ref[...], w_ref[:512, :])
                        + _dot(rk_ref[...], w_ref[512:, :]))

        for kk, piece in enumerate((rv_ref, rg_ref, fq_ref, fk_ref, fv_ref, fg_ref), start=1):
            @pl.when(k == kk)
            def _(piece=piece):
                acc[...] += _dot(piece[...], w_ref[...])

        @pl.when(k == nk - 1)
        def _():
            du = acc[...]
            h = h_ref[...]
            gg = g_ref[...]
            rs = lax.rsqrt(jnp.mean(h * h, axis=1, keepdims=True) + EPS)
            hn = h * rs
            part = jnp.sum(du * hn, axis=0, keepdims=True)

            @pl.when(i == 0)
            def _():
                dg_ref[...] = part

            @pl.when(i > 0)
            def _():
                dg_ref[...] += part

            dhn = du * gg
            gh_ref[...] = rs * (dhn - hn * jnp.mean(dhn * hn, axis=1, keepdims=True)) + do_ref[...]

    sems = [pltpu.SemaphoreType.DMA((3 * n,)), pltpu.SemaphoreType.DMA((3 * n,))] if n else []
    return pl.pallas_call(
        body, name="du_norm_bwd", grid=(ni, nk),
        in_specs=[pl.BlockSpec((tm, w), lambda i, k: (i, 0)) for w in DZ_WIDTHS]
        + [pl.BlockSpec((tm, C), lambda i, k: (i, 0)),
           pl.BlockSpec((tk, D), lambda i, k: (k, 0)), pl.BlockSpec((C, D), lambda i, k: (0, 0)),
           pl.BlockSpec((tm, D), lambda i, k: (i, 0)), pl.BlockSpec((1, D), lambda i, k: (0, 0)),
           pl.BlockSpec((tm, D), lambda i, k: (i, 0))] + [ANY] * n,
        out_specs=[pl.BlockSpec((tm, D), lambda i, k: (i, 0)), pl.BlockSpec((1, D), lambda i, k: (0, 0))] + [ANY] * n,
        out_shape=[jax.ShapeDtypeStruct((T, D), F32), jax.ShapeDtypeStruct((1, D), F32)]
        + [jax.ShapeDtypeStruct(p.shape, p.dtype) for p in parts],
        scratch_shapes=[pltpu.VMEM((tm, D), F32)] + sems,
        compiler_params=_params(("arbitrary", "arbitrary")),
    )(*dzs, dzf, wt, wft, hpad, g, dopad, *parts)


GROWS = 7680


def _dw_in(dzs, dzf, ut):
    tn = 512
    nmain = WMAIN // tn
    first, blocks = [], []
    for w in DZ_WIDTHS:
        first.append(sum(blocks))
        blocks.append(w // tn)

    def body(rq_ref, rk_ref, rv_ref, rg_ref, fq_ref, fk_ref, fv_ref, fg_ref, dzf_ref, ut_ref, o_ref):
        gidx = pl.program_id(0)
        for piece, g0, nb in zip((rq_ref, rk_ref, rv_ref, rg_ref, fq_ref, fk_ref, fv_ref, fg_ref), first, blocks):
            @pl.when((gidx >= g0) & (gidx < g0 + nb))
            def _(piece=piece):
                o_ref[...] = _dot(ut_ref[...], piece[...]).T.astype(BF)

        @pl.when(gidx == nmain)
        def _():
            o_ref[:C, :] = _dot(ut_ref[...], dzf_ref[...]).T.astype(BF)
            o_ref[C:, :] = jnp.zeros((tn - C, D), BF)

    def piece_spec(g0, nb):
        return pl.BlockSpec((T, tn), lambda gidx: (0, jnp.clip(gidx - g0, 0, nb - 1)))

    return pl.pallas_call(
        body, name="dw_in", grid=(nmain + 1,),
        in_specs=[piece_spec(g0, nb) for g0, nb in zip(first, blocks)]
        + [pl.BlockSpec((T, C), lambda gidx: (0, 0)), pl.BlockSpec((D, T), lambda gidx: (0, 0))],
        out_specs=pl.BlockSpec((tn, D), lambda gidx: (gidx, 0)),
        out_shape=jax.ShapeDtypeStruct((GROWS, D), BF),
        compiler_params=pltpu.CompilerParams(dimension_semantics=("arbitrary",), vmem_limit_bytes=DW_VMEM_LIMIT),
    )(*dzs, dzf, ut)


def _token_order(x_po):
    def body(i_ref, o_ref):
        o_ref[...] = i_ref[...]

    return pl.pallas_call(
        body, name="token_order", grid=(NCH,),
        in_specs=[pl.BlockSpec((C, D), lambda i: (_fox_pos(i), 0))],
        out_specs=pl.BlockSpec((C, D), lambda i: (i, 0)),
        out_shape=jax.ShapeDtypeStruct((T, D), x_po.dtype),
        compiler_params=_params(("parallel",)),
    )(x_po)


def _local_step(x, tgt, meta, norm_g, wt, wft, b_f, wout, final_g, chip_sums=None, wout_full=None):
    cst = _constants()
    hpad = jnp.concatenate([jnp.pad(meta, ((PAD, 0), (0, 0))), x], axis=0)
    bf_pad = jnp.pad(b_f, ((0, 0), (0, C - NFF)))
    u, ut = _norm_in(hpad, norm_g)
    z = _mm_nt(u, wt, WMAIN, T // 2, 1024, "in_proj")
    zf = _mm_nt(u, wft, C, T // 2, C, "in_proj_ff")
    r, sprev = _ret_fwd(z, cst)
    ct = _fox_prep(zf, bf_pad, cst)
    if wout_full is None:
        a, g = _fox_fwd(z, ct, cst, None)
    else:
        a, g, landed_wout = _fox_fwd(z, ct, cst, wout)
        wout = wout_full(landed_wout)
    yt, dopad, dob, loss8, dfg = _out_loss(r, z, a, wout, x, tgt, final_g)
    dr, da, dzrg, dzfg, delta = _dy_gate_bwd(dob, wout, r, z, a, cst["seg"])
    dwout = _mm_nn(yt, dob, 512, D, "dw_out", BF)
    dzq_r, dzk_r, dzv_r = _ret_bwd(z, cst, sprev, dr)
    dq_po, drow, dzk_f, dzv_f, dcol = _fox_bwd(z, da, g, delta, ct, cst)
    dzf, dbf = _fox_gate_bwd(drow, dcol, zf, bf_pad, cst)
    dzs = [dzq_r, dzk_r, dzv_r, dzrg, _token_order(dq_po), dzk_f, dzv_f, dzfg]
    gwt = _dw_in(dzs, dzf, ut)
    parts = chip_sums(gwt, dwout) if chip_sums else []
    gh, dng, *landed = _du_norm_bwd(dzs, dzf, wt, wft, hpad, norm_g, dopad, parts)
    return (loss8[0, 0], gh[C:], gh[PAD:C], dng, gwt, dbf[:, :NFF], dwout, dfg, parts, landed)


WOFF, WLEN = 1792, 2048
WHALF = WLEN // 2
LAP = WPADROWS - WOFF


def _own_window(w3):
    rows, sub, lanes = w3.shape
    pad = WPADROWS - rows
    tb = 96
    nb = WPADROWS // tb
    half = rows // 2

    def body(w_ref, o_ref, buf, sems):
        x, y, _ = _place()
        shift = 4 * (2 * x + y)
        buf[pl.ds(0, pad)] = jnp.zeros((pad, sub, lanes), F32)
        buf[pl.ds(rows, pad)] = jnp.zeros((pad, sub, lanes), F32)
        cps = [pltpu.make_async_copy(w_ref.at[pl.ds(half * h, half)], buf.at[pl.ds(shift + half * h, half)],
                                     sems.at[h]) for h in range(2)]
        for cp in cps:
            cp.start()

        def block(i, carry):
            r0 = pl.multiple_of(i * tb, tb)
            o_ref[pl.ds(r0, tb), :] = buf[pl.ds(r0, tb)].reshape(tb, sub * lanes).astype(BF)
            return carry

        cps[0].wait()
        lax.fori_loop(0, half // tb, block, 0)
        cps[1].wait()
        lax.fori_loop(half // tb, nb, block, 0)

    return pl.pallas_call(
        body, name="own_window",
        in_specs=[ANY], out_shape=jax.ShapeDtypeStruct((WPADROWS, sub * lanes), BF),
        scratch_shapes=[pltpu.VMEM((WPADROWS, sub, lanes), F32), pltpu.SemaphoreType.DMA((2,))],
        compiler_params=pltpu.CompilerParams(vmem_limit_bytes=VMEM_LIMIT),
    )(w3)


def _gather_weights(own_win, meta):
    half_main, half_lap, half_meta = WOFF // 2, LAP // 2, meta.shape[0] // 2

    def body(win_ref, meta_ref, w_ref, laps_ref, gm_ref, send_sems, recv_sems, local_sems, stage, lapbuf, headbuf):
        x, y, c = _place()
        me_s = 2 * x + y
        sib = (x, y, 1 - c)
        chips = _other_chips(x, y)
        kinds = [
            (lambda h: win_ref.at[pl.ds(half_main * h, half_main)],
             lambda s, h: w_ref.at[pl.ds(WOFF * s + half_main * h, half_main)]),
            (lambda h: win_ref.at[pl.ds(WOFF + half_lap * h, half_lap)],
             lambda s, h: laps_ref.at[s, pl.ds(half_lap * h, half_lap)]),
            (lambda h: meta_ref.at[pl.ds(half_meta * h, half_meta)],
             lambda s, h: gm_ref.at[s, pl.ds(half_meta * h, half_meta)]),
        ]
        own_in = pltpu.make_async_copy(win_ref.at[pl.ds(0, WOFF)], stage, local_sems.at[0])
        own_in.start()
        own_lap_in = pltpu.make_async_copy(win_ref.at[pl.ds(WOFF, LAP)], lapbuf.at[0], local_sems.at[1])
        own_lap_in.start()
        sends, waits = [], []
        for a, (src, dst) in enumerate(kinds):
            for k, (cx, cy, cs) in enumerate(chips):
                sends.append(pltpu.make_async_remote_copy(
                    src_ref=src(c), dst_ref=dst(me_s, c),
                    send_sem=send_sems.at[6 * a + k], recv_sem=recv_sems.at[6 * a + k],
                    device_id=(cx, cy, c), device_id_type=MESH))
                sends[-1].start()
        own_in.wait()
        own_out = pltpu.make_async_copy(stage, w_ref.at[pl.ds(WOFF * me_s, WOFF)], local_sems.at[0])
        own_out.start()
        own_lap_in.wait()
        own_lap_out = pltpu.make_async_copy(lapbuf.at[0], laps_ref.at[me_s], local_sems.at[1])
        own_lap_out.start()
        for a, (src, dst) in enumerate(kinds):
            for k, (cx, cy, cs) in enumerate(chips):
                pltpu.make_async_remote_copy(
                    src_ref=dst(cs, c), dst_ref=dst(cs, c),
                    send_sem=send_sems.at[6 * a + k], recv_sem=recv_sems.at[6 * a + k],
                    device_id=(cx, cy, c), device_id_type=MESH).wait_recv()
                fwd = pltpu.make_async_remote_copy(
                    src_ref=dst(cs, c), dst_ref=dst(cs, c),
                    send_sem=send_sems.at[6 * a + 3 + k], recv_sem=recv_sems.at[6 * a + 3 + k],
                    device_id=sib, device_id_type=MESH)
                fwd.start()
                sends.append(fwd)
                waits.append(pltpu.make_async_remote_copy(
                    src_ref=dst(cs, 1 - c), dst_ref=dst(cs, 1 - c),
                    send_sem=send_sems.at[6 * a + 3 + k], recv_sem=recv_sems.at[6 * a + 3 + k],
                    device_id=sib, device_id_type=MESH))
        for w in waits:
            w.wait_recv()
        for s in sends:
            s.wait_send()
        own_out.wait()
        own_lap_out.wait()
        for s in range(1, 4):
            head = w_ref.at[pl.ds(WOFF * s, LAP)]
            loads = [pltpu.make_async_copy(laps_ref.at[s - 1], lapbuf.at[1], local_sems.at[2]),
                     pltpu.make_async_copy(head, headbuf, local_sems.at[3])]
            for cp in loads:
                cp.start()
            for cp in loads:
                cp.wait()
            headbuf[...] = (headbuf[...].astype(F32) + lapbuf[1].astype(F32)).astype(BF)
            store = pltpu.make_async_copy(headbuf, head, local_sems.at[3])
            store.start()
            store.wait()

    return pl.pallas_call(
        body, name="all_gather_w",
        in_specs=[ANY] * 2, out_specs=[ANY] * 3,
        out_shape=[jax.ShapeDtypeStruct((WMAIN, D), own_win.dtype), jax.ShapeDtypeStruct((4, LAP, D), own_win.dtype),
                   jax.ShapeDtypeStruct((4,) + meta.shape, meta.dtype)],
        scratch_shapes=[pltpu.SemaphoreType.DMA((18,)), pltpu.SemaphoreType.DMA((18,)), pltpu.SemaphoreType.DMA((4,)),
                        pltpu.VMEM((WOFF, D), own_win.dtype), pltpu.VMEM((2, LAP, D), own_win.dtype),
                        pltpu.VMEM((LAP, D), own_win.dtype)],
        compiler_params=pltpu.CompilerParams(vmem_limit_bytes=VMEM_LIMIT),
    )(own_win, meta)


def _pair_swap(gwt, arrs):
    n = len(arrs)

    def body(*refs):
        gw, ins = refs[0], refs[1:n + 1]
        gwo, outs = refs[n + 1], refs[n + 2:2 * n + 2]
        send_sems, recv_sems = refs[2 * n + 2:]
        x, y, c = _place()
        sib = (x, y, 1 - c)
        cps = []
        for k in range(4):
            cps.append(pltpu.make_async_remote_copy(
                src_ref=gw.at[pl.ds(WOFF * k + (1 - c) * WHALF, WHALF)], dst_ref=gwo.at[k],
                send_sem=send_sems.at[k], recv_sem=recv_sems.at[k], device_id=sib, device_id_type=MESH))
        for a in range(n):
            rows = ins[a].shape[1] // 2
            cps.append(pltpu.make_async_remote_copy(
                src_ref=ins[a].at[:, pl.ds((1 - c) * rows, rows)], dst_ref=outs[a],
                send_sem=send_sems.at[4 + a], recv_sem=recv_sems.at[4 + a], device_id=sib, device_id_type=MESH))
        for cp in cps:
            cp.start()
        for cp in cps:
            cp.wait()

    return pl.pallas_call(
        body, name="rs_pair_swap",
        in_specs=[ANY] * (n + 1), out_specs=[ANY] * (n + 1),
        out_shape=[jax.ShapeDtypeStruct((4, WHALF, D), gwt.dtype)]
        + [jax.ShapeDtypeStruct((4, a.shape[1] // 2, a.shape[2]), a.dtype) for a in arrs],
        scratch_shapes=[pltpu.SemaphoreType.DMA((n + 4,)), pltpu.SemaphoreType.DMA((n + 4,))],
    )(gwt, *arrs)


def _add_windows(gwt, recv):
    tb = 256
    nb = WHALF // tb
    c = lax.axis_index("c")

    def body(c_ref, a_ref, b_ref, o_ref):
        o_ref[0] = (a_ref[...].astype(F32) + b_ref[0].astype(F32)).astype(BF)

    return pl.pallas_call(
        body, name="pair_add_in",
        grid_spec=pltpu.PrefetchScalarGridSpec(
            num_scalar_prefetch=1, grid=(4, nb),
            in_specs=[pl.BlockSpec((tb, D), lambda k, i, cr: ((WOFF // tb) * k + nb * cr[0] + i, 0)),
                      pl.BlockSpec((1, tb, D), lambda k, i, cr: (k, i, 0))],
            out_specs=pl.BlockSpec((1, tb, D), lambda k, i, cr: (k, i, 0))),
        out_shape=jax.ShapeDtypeStruct(recv.shape, BF),
        compiler_params=_params(("parallel", "parallel")),
    )(jnp.reshape(c, (1,)).astype(jnp.int32), gwt, recv)


def _chip_exchange(parts, small):
    n = len(parts)

    def body(*refs):
        ins, sm = refs[:n], refs[n]
        outs, smo = refs[n + 1:2 * n + 1], refs[2 * n + 1]
        send_sems, recv_sems = refs[2 * n + 2:]
        cps = _chip_copies(ins, outs, send_sems, recv_sems, by_dest=True)
        cps += _chip_copies([sm], [smo], send_sems.at[pl.ds(3 * n, 3)], recv_sems.at[pl.ds(3 * n, 3)], by_dest=False)
        for cp in cps:
            cp.start()
        for cp in cps:
            cp.wait()

    return pl.pallas_call(
        body, name="rs_chip_exchange",
        in_specs=[ANY] * (n + 1), out_specs=[ANY] * (n + 1),
        out_shape=[jax.ShapeDtypeStruct(p.shape, p.dtype) for p in parts]
        + [jax.ShapeDtypeStruct((4,) + small.shape, small.dtype)],
        scratch_shapes=[pltpu.SemaphoreType.DMA((3 * (n + 1),)), pltpu.SemaphoreType.DMA((3 * (n + 1),))],
    )(*parts, small)


def _pair_send(halves):
    n = len(halves)

    def body(*refs):
        ins, outs = refs[:n], refs[n:2 * n]
        send_sems, recv_sems = refs[2 * n:]
        x, y, c = _place()
        cps = [pltpu.make_async_remote_copy(
            src_ref=ins[a], dst_ref=outs[a], send_sem=send_sems.at[a], recv_sem=recv_sems.at[a],
            device_id=(x, y, 1 - c), device_id_type=MESH) for a in range(n)]
        for cp in cps:
            cp.start()
        for cp in cps:
            cp.wait()

    return pl.pallas_call(
        body, name="rs_pair_send",
        in_specs=[ANY] * n, out_specs=[ANY] * n,
        out_shape=[jax.ShapeDtypeStruct(h.shape, h.dtype) for h in halves],
        scratch_shapes=[pltpu.SemaphoreType.DMA((n,)), pltpu.SemaphoreType.DMA((n,))],
    )(*halves)


def _row_block(rows):
    for tb in (256, 128, 64, 32, 16, 8):
        if rows % tb == 0:
            return tb
    return rows


def _add_halves(full, recv, name, out_dtype):
    _, r2, w = recv.shape
    tb = _row_block(r2)
    nb = r2 // tb
    c = lax.axis_index("c")

    def body(c_ref, a_ref, b_ref, o_ref):
        o_ref[...] = (a_ref[...].astype(F32) + b_ref[...].astype(F32)).astype(o_ref.dtype)

    return pl.pallas_call(
        body, name=name,
        grid_spec=pltpu.PrefetchScalarGridSpec(
            num_scalar_prefetch=1, grid=(4, nb),
            in_specs=[pl.BlockSpec((1, tb, w), lambda s, i, cr: (s, cr[0] * nb + i, 0)),
                      pl.BlockSpec((1, tb, w), lambda s, i, cr: (s, i, 0))],
            out_specs=pl.BlockSpec((1, tb, w), lambda s, i, cr: (s, i, 0))),
        out_shape=jax.ShapeDtypeStruct(recv.shape, out_dtype),
        compiler_params=_params(("parallel", "parallel")),
    )(jnp.reshape(c, (1,)).astype(jnp.int32), full, recv)


def _add2(a, b, name):
    def body(a_ref, b_ref, o_ref):
        o_ref[...] = a_ref[...] + b_ref[...]

    return pl.pallas_call(body, name=name, out_shape=jax.ShapeDtypeStruct(a.shape, a.dtype))(a, b)


def _sum4(buf, own, name):
    _, r, w = buf.shape
    tb = _row_block(r)
    me_s = 2 * lax.axis_index("x") + lax.axis_index("y")
    by_dest = own.ndim == 3

    def body(s_ref, b_ref, own_ref, o_ref):
        mine = (own_ref[0] if by_dest else own_ref[...]).astype(F32)
        terms = [jnp.where(s_ref[0] == t, mine, b_ref[t].astype(F32)) for t in range(4)]
        o_ref[...] = ((terms[0] + terms[1]) + terms[2]) + terms[3]

    own_spec = (pl.BlockSpec((1, tb, w), lambda i, sr: (sr[0], i, 0)) if by_dest
                else pl.BlockSpec((tb, w), lambda i, sr: (i, 0)))
    return pl.pallas_call(
        body, name=name,
        grid_spec=pltpu.PrefetchScalarGridSpec(
            num_scalar_prefetch=1, grid=(r // tb,),
            in_specs=[pl.BlockSpec((4, tb, w), lambda i, sr: (0, i, 0)), own_spec],
            out_specs=pl.BlockSpec((tb, w), lambda i, sr: (i, 0))),
        out_shape=jax.ShapeDtypeStruct((r, w), F32),
        compiler_params=_params(("parallel",)),
    )(jnp.reshape(me_s, (1,)).astype(jnp.int32), buf, own)


def _adamw_math(w, g, m, v):
    mn = B1 * m + (1.0 - B1) * g
    vn = B2 * v + (1.0 - B2) * (g * g)
    m_hat = mn / (1.0 - B1 ** STEP)
    v_hat = vn / (1.0 - B2 ** STEP)
    return -LR * (m_hat / (jnp.sqrt(v_hat) + AEPS) + WD * w), mn, vn


def _adamw(w, g, m, v, name):
    r, c_ = w.shape
    tb = _row_block(r)
    if tb == r and r > 512:
        tb = 256

    def body(w_ref, g_ref, m_ref, v_ref, d_ref, mo_ref, vo_ref):
        d_ref[...], mo_ref[...], vo_ref[...] = _adamw_math(w_ref[...], g_ref[...], m_ref[...], v_ref[...])

    spec = pl.BlockSpec((tb, c_), lambda i: (i, 0))
    return pl.pallas_call(
        body, name=name, grid=(pl.cdiv(r, tb),),
        in_specs=[spec] * 4, out_specs=[spec] * 3,
        out_shape=[jax.ShapeDtypeStruct(w.shape, F32)] * 3,
        compiler_params=_params(("parallel",)),
    )(w, g, m, v)


def _adamw_rows(w, g, m, v, name):
    r = w.shape[0]
    tb = 256
    sub, lanes = w.shape[1:]

    def body(w_ref, g_ref, m_ref, v_ref, go_ref, d_ref, mo_ref, vo_ref):
        g = g_ref[...].reshape(tb, sub, lanes)
        go_ref[...] = g
        d_ref[...], mo_ref[...], vo_ref[...] = _adamw_math(w_ref[...], g, m_ref[...], v_ref[...])

    spec = pl.BlockSpec((tb, sub, lanes), lambda i: (i, 0, 0))
    return pl.pallas_call(
        body, name=name, grid=(pl.cdiv(r, tb),),
        in_specs=[spec, pl.BlockSpec((tb, sub * lanes), lambda i: (i, 0)), spec, spec], out_specs=[spec] * 4,
        out_shape=[jax.ShapeDtypeStruct(w.shape, F32)] * 4,
        compiler_params=_params(("parallel",)),
    )(w, g, m, v)


def _adamw_halves(w, g_mine, g_sib, m, v, name):
    r, c_ = w.shape
    r2 = g_mine.shape[0]
    tb = _row_block(r2)
    nb = r2 // tb
    c = lax.axis_index("c")

    def body(c_ref, w_ref, gm_ref, gs_ref, m_ref, v_ref, g_ref, d_ref, mo_ref, vo_ref):
        g = jnp.where(pl.program_id(0) == c_ref[0], gm_ref[...], gs_ref[...])
        g_ref[...] = g
        d_ref[...], mo_ref[...], vo_ref[...] = _adamw_math(w_ref[...], g, m_ref[...], v_ref[...])

    full = pl.BlockSpec((tb, c_), lambda h, i, cr: (h * nb + i, 0))
    half = pl.BlockSpec((tb, c_), lambda h, i, cr: (i, 0))
    return pl.pallas_call(
        body, name=name,
        grid_spec=pltpu.PrefetchScalarGridSpec(
            num_scalar_prefetch=1, grid=(2, nb),
            in_specs=[full, half, half, full, full], out_specs=[full] * 4),
        out_shape=[jax.ShapeDtypeStruct(w.shape, F32)] * 4,
        compiler_params=_params(("parallel", "parallel")),
    )(jnp.reshape(c, (1,)).astype(jnp.int32), w, g_mine, g_sib, m, v)


def kernel(x, meta_tokens, norm_g, w_in, b_f, w_out, final_g, loss_target, m_meta_tokens, m_norm_g, m_w_in, m_b_f, m_w_out, m_final_g, v_meta_tokens, v_norm_g, v_w_in, v_b_f, v_w_out, v_final_g):
    me_s = 2 * lax.axis_index("x") + lax.axis_index("y")
    core = lax.axis_index("c")
    w3, m3, v3 = [jnp.transpose(jnp.reshape(t[0], (D // C, C, WSH)), (2, 0, 1)) for t in (w_in, m_w_in, v_w_in)]

    wt_main, laps, gmeta = _gather_weights(_own_window(w3), meta_tokens)
    wft = jnp.pad(laps[3, :NFF], ((0, C - NFF), (0, 0)))
    mine = (jnp.arange(4) == me_s)[:, None, None]
    gmeta = jnp.where(mine, meta_tokens[None], gmeta)
    wout_own = w_out[0].astype(BF)
    meta = jnp.concatenate([gmeta[s] for s in range(4)], axis=1)

    def wout_full(landed):
        return jnp.where(mine, wout_own[None], landed).reshape(DMIX, D)

    def chip_sums(gwt, dwout):
        g_out = dwout.reshape(4, DMIX // 4, D)
        r_in, r_out = _pair_swap(gwt, [g_out])
        return [_add_windows(gwt, r_in), _add_halves(g_out, r_out, "pair_add_out", BF)]

    loss, gx, dmeta, dng, gwt, dbf, dwout, dfg, (p_in, p_out), (e_in, e_out) = _local_step(
        x[0], loss_target[0], meta, norm_g, wt_main, wft, b_f, wout_own, final_g.reshape(1, D), chip_sums, wout_full)

    g_meta = jnp.stack([dmeta[:, 256 * s:256 * (s + 1)] for s in range(4)])
    small = jnp.concatenate([dng, dfg, jnp.pad(dbf, ((0, 0), (0, D - NFF))),
                             jnp.pad(jnp.reshape(loss, (1, 1)), ((0, 0), (0, D - 1))),
                             jnp.zeros((4, D), F32)], axis=0)
    e_meta, e_small = _chip_exchange([g_meta], small)
    h_in, h_out = _sum4(e_in, p_in, "sum_in"), _sum4(e_out, p_out, "sum_out")
    h_meta, h_small = _sum4(e_meta, g_meta, "sum_meta"), _sum4(e_small, small, "sum_small")
    s_in, s_out, s_meta, s_small = _pair_send([h_in, h_out, h_meta, h_small])
    gw_meta = _add2(h_meta, s_meta, "pair_add_meta")
    tot = _add2(h_small, s_small, "pair_add_small")
    g_norm, g_final, g_bf, loss_all = tot[0:1], tot[1], tot[2:3, :NFF], tot[3, 0]

    d_meta, nm_meta, nv_meta = _adamw(meta_tokens, gw_meta, m_meta_tokens, v_meta_tokens, "adamw_meta")
    d_norm, nm_norm, nv_norm = _adamw(norm_g, g_norm, m_norm_g, v_norm_g, "adamw_norm")
    window = jnp.concatenate([jnp.where(core == 0, h_in, s_in), jnp.where(core == 0, s_in, h_in)], axis=0)
    gwt_own = lax.dynamic_slice(window, (4 * me_s, 0), (WSH, D))
    outs_in = _adamw_rows(w3, gwt_own, m3, v3, "adamw_in")
    gw_in, d_in, nm_in, nv_in = [jnp.reshape(jnp.transpose(t, (1, 2, 0)), (1, D, WSH)) for t in outs_in]
    d_bf, nm_bf, nv_bf = _adamw(b_f, g_bf, m_b_f, v_b_f, "adamw_bf")
    gw_out, d_out, nm_out, nv_out = _adamw_halves(w_out[0], h_out, s_out, m_w_out[0], v_w_out[0], "adamw_out")
    d_fin, nm_fin, nv_fin = _adamw(final_g.reshape(1, D), g_final.reshape(1, D), m_final_g.reshape(1, D),
                                   v_final_g.reshape(1, D), "adamw_final")
    return (loss_all, gx[None], gw_meta, g_norm, gw_in, g_bf, gw_out[None], g_final,
            d_meta, d_norm, d_in, d_bf, d_out[None], d_fin.reshape(D),
            nm_meta, nm_norm, nm_in, nm_bf, nm_out[None], nm_fin.reshape(D),
            nv_meta, nv_norm, nv_in, nv_bf, nv_out[None], nv_fin.reshape(D))
```

```python
import numpy as np
import jax
import jax.numpy as jnp
from jax import lax
from jax.experimental import pallas as pl
from jax.experimental.pallas import tpu as pltpu

D = 1024
SEQ = 2048
NMETA = 16
C = 128
PAD = C - NMETA
T = PAD + NMETA + SEQ
NCH = T // C
RH, RDK, RDV = 4, 128, 256
FH, FD = 16, 64
NPAIR = FH // 2
WMAIN = 7168
NFF = 16
WIN = WMAIN + NFF
WSH = WIN // 4
WPADROWS = 1824
DMIX = 2048
EPS = 1e-6
NEG = -1e30
RSCALE = RDK ** -0.5
FSCALE = FD ** -0.5
ROPE_BASE = 10000.0
LR, B1, B2, AEPS, WD, STEP = 0.001, 0.9, 0.999, 1e-08, 0.01, 10

BF = jnp.bfloat16
F32 = jnp.float32
NT = (((1,), (1,)), ((), ()))
TN = (((0,), (0,)), ((), ()))
NN_DIMS = (((1,), (0,)), ((), ()))
MESH = pl.DeviceIdType.MESH
ANY = pl.BlockSpec(memory_space=pl.ANY)
VMEM_LIMIT = 48 * 1024 * 1024
DW_VMEM_LIMIT = 56 * 1024 * 1024

GB_R, GB_F = 2, 6
QB_F, KB_F, VB_F = 24, 32, 40


def _dot(a, b):
    return jnp.dot(a, b, preferred_element_type=F32)


def _dg(a, b, dims):
    return lax.dot_general(a, b, dims, preferred_element_type=F32)


def _params(sem=None):
    return pltpu.CompilerParams(dimension_semantics=sem, vmem_limit_bytes=VMEM_LIMIT)


def _constants():
    pos = jnp.arange(T, dtype=F32) - PAD
    inv = ROPE_BASE ** (-jnp.arange(0, RDK, 2, dtype=F32) / RDK)
    ang = pos[:, None] * inv[None, :]
    cos, sin = jnp.cos(ang), jnp.sin(ang)
    cos2 = jnp.concatenate([cos, cos], axis=1)
    sin2 = jnp.concatenate([-sin, sin], axis=1)
    log_gamma = jnp.log1p(-jnp.exp2(-5.0 - jnp.arange(RH, dtype=F32)))
    idx = jnp.arange(C, dtype=F32)
    diff = idx[:, None] - idx[None, :]
    dmask = jnp.where(diff[None] >= 0, jnp.exp(log_gamma[:, None, None] * jnp.maximum(diff, 0.0)[None]), 0.0)
    zeta = jnp.exp(log_gamma[:, None] * (C - 1.0 - idx)[None, :])
    xi = jnp.exp(log_gamma[:, None] * (idx + 1.0)[None, :])
    gdec = jnp.exp(log_gamma * C)
    zeta_b = jnp.broadcast_to(zeta[:, :, None], (RH, C, RDK))
    xi_b = jnp.broadcast_to(xi[:, :, None], (RH, C, RDK))
    gdec_b = jnp.broadcast_to(gdec[:, None, None], (RH, RDK, RDV))
    tri = jnp.asarray(np.tril(np.ones((C, C), np.float32)), dtype=BF)
    head_of_lane = np.arange(FH * FD) // FD
    pick = ((np.arange(FH * FD)[:, None] % FD == 0)
            & (head_of_lane[:, None] == np.arange(C)[None, :])).astype(np.float32)
    seg = (np.arange(C)[:, None] // FD == np.arange(C)[None, :] // FD).astype(np.float32)
    ones_aug = np.concatenate([np.tile((np.arange(C) < FD)[None, :], (C, 1)),
                               np.tile((np.arange(C) >= FD)[None, :], (C, 1))], axis=0).astype(np.float32)
    lane = np.arange(2 * C) % C
    causal = np.where(lane[None, :] <= np.arange(C)[:, None], 0.0, NEG).astype(np.float32)
    mask_bias = np.stack([np.zeros((C, 2 * C), np.float32), causal])
    return dict(cos2=cos2, sin2=sin2, dmask=dmask, zeta=zeta_b, xi=xi_b, gdec=gdec_b, tri=tri,
                mask_bias=jnp.asarray(mask_bias), pick=jnp.asarray(pick, dtype=BF), seg=jnp.asarray(seg, dtype=BF),
                ones_aug=jnp.asarray(ones_aug, dtype=BF))


def _norm_rows(h, g):
    return h * lax.rsqrt(jnp.mean(h * h, axis=1, keepdims=True) + EPS) * g


def _mm_nt(a, b, n, tm, tn, name):
    m, k = a.shape

    def body(a_ref, b_ref, o_ref):
        o_ref[...] = _dg(a_ref[...], b_ref[...], NT)

    return pl.pallas_call(
        body, name=name, grid=(m // tm, n // tn),
        in_specs=[pl.BlockSpec((tm, k), lambda i, j: (i, 0)), pl.BlockSpec((tn, k), lambda i, j: (j, 0))],
        out_specs=pl.BlockSpec((tm, tn), lambda i, j: (i, j)),
        out_shape=jax.ShapeDtypeStruct((m, n), F32),
        compiler_params=_params(("parallel", "parallel")),
    )(a, b)


def _mm_nn(a, b, tm, tn, name, out_dtype=F32):
    m, k = a.shape
    _, n = b.shape

    def body(a_ref, b_ref, o_ref):
        o_ref[...] = _dot(a_ref[...], b_ref[...]).astype(out_dtype)

    return pl.pallas_call(
        body, name=name, grid=(m // tm, n // tn),
        in_specs=[pl.BlockSpec((tm, k), lambda i, j: (i, 0)), pl.BlockSpec((k, tn), lambda i, j: (0, j))],
        out_specs=pl.BlockSpec((tm, tn), lambda i, j: (i, j)),
        out_shape=jax.ShapeDtypeStruct((m, n), out_dtype),
        compiler_params=_params(("parallel", "parallel")),
    )(a, b)


def _rot(x, cos2, sin2):
    return x * cos2 + pltpu.roll(x, 64, 1) * sin2


def _ret_specs(chunk):
    whole = lambda shape: pl.BlockSpec(shape, lambda n: (0,) * len(shape))
    return [
        pl.BlockSpec((C, RH * RDK), lambda n: (chunk(n), 0)),
        pl.BlockSpec((C, RH * RDK), lambda n: (chunk(n), 1)),
        pl.BlockSpec((C, RH * RDV), lambda n: (chunk(n), 1)),
        pl.BlockSpec((C, RDK), lambda n: (chunk(n), 0)),
        pl.BlockSpec((C, RDK), lambda n: (chunk(n), 0)),
        whole((RH, C, C)), whole((RH, C, RDK)), whole((RH, C, RDK)), whole((RH, RDK, RDV)),
    ]


def _ret_heads(q_ref, k_ref, v_ref, cos, sin):
    qr = [_rot(q_ref[:, RDK * h:RDK * (h + 1)], cos, sin) for h in range(RH)]
    kr = [_rot(k_ref[:, RDK * h:RDK * (h + 1)], cos, sin) * RSCALE for h in range(RH)]
    vb = [v_ref[:, RDV * h:RDV * (h + 1)].astype(BF) for h in range(RH)]
    return qr, kr, [t.astype(BF) for t in qr], [t.astype(BF) for t in kr], vb


def _ret_fwd(z, cst):
    def body(q_ref, k_ref, v_ref, cos_ref, sin_ref, dm_ref, xi_ref, zt_ref, gd_ref, r_ref, sp_ref, st):
        n = pl.program_id(0)

        @pl.when(n == 0)
        def _():
            st[...] = jnp.zeros_like(st)

        hs = range(RH)
        qr, kr, qb, kb, vb = _ret_heads(q_ref, k_ref, v_ref, cos_ref[...], sin_ref[...])
        sd = [(_dg(qb[h], kb[h], NT) * dm_ref[h]).astype(BF) for h in hs]
        state = [st[h] for h in hs]
        qx = [(qr[h] * xi_ref[h]).astype(BF) for h in hs]
        kz = [(kr[h] * zt_ref[h]).astype(BF) for h in hs]
        out = [_dot(sd[h], vb[h]) + _dot(qx[h], state[h].astype(BF)) for h in hs]
        kv = [_dg(kz[h], vb[h], TN) for h in hs]
        for h in hs:
            sp_ref[0, h] = state[h]
            r_ref[:, RDV * h:RDV * (h + 1)] = out[h]
            st[h] = state[h] * gd_ref[h] + kv[h]

    return pl.pallas_call(
        body, name="ret_fwd", grid=(NCH,),
        in_specs=_ret_specs(lambda n: n),
        out_specs=[pl.BlockSpec((C, RH * RDV), lambda n: (n, 0)),
                   pl.BlockSpec((1, RH, RDK, RDV), lambda n: (n, 0, 0, 0))],
        out_shape=[jax.ShapeDtypeStruct((T, RH * RDV), F32), jax.ShapeDtypeStruct((NCH, RH, RDK, RDV), F32)],
        scratch_shapes=[pltpu.VMEM((RH, RDK, RDV), F32)],
        compiler_params=_params(("arbitrary",)),
    )(z, z, z, cst["cos2"], cst["sin2"], cst["dmask"], cst["xi"], cst["zeta"], cst["gdec"])


def _ret_bwd(z, cst, sprev, dr):
    def body(q_ref, k_ref, v_ref, cos_ref, sin_ref, dm_ref, xi_ref, zt_ref, gd_ref, sp_ref, dr_ref,
             dq_ref, dk_ref, dv_ref, gst):
        i = pl.program_id(0)

        @pl.when(i == 0)
        def _():
            gst[...] = jnp.zeros_like(gst)

        hs = range(RH)
        cos, sin = cos_ref[...], sin_ref[...]
        qr, kr, qb, kb, vb = _ret_heads(q_ref, k_ref, v_ref, cos, sin)
        dm = [dm_ref[h] for h in hs]
        xi = [xi_ref[h] for h in hs]
        zt = [zt_ref[h] for h in hs]
        sd = [(_dg(qb[h], kb[h], NT) * dm[h]).astype(BF) for h in hs]
        qx = [(qr[h] * xi[h]).astype(BF) for h in hs]
        kz = [(kr[h] * zt[h]).astype(BF) for h in hs]
        drb = [dr_ref[:, RDV * h:RDV * (h + 1)] for h in hs]
        sb = [sp_ref[0, h].astype(BF) for h in hs]
        g = [gst[h] for h in hs]
        gb = [t.astype(BF) for t in g]
        ds = [(_dg(drb[h], vb[h], NT) * dm[h]).astype(BF) for h in hs]
        dq = [_dot(ds[h], kb[h]) + _dg(drb[h], sb[h], NT) * xi[h] for h in hs]
        dk = [(_dg(ds[h], qb[h], TN) + _dg(vb[h], gb[h], NT) * zt[h]) * RSCALE for h in hs]
        dv = [_dg(sd[h], drb[h], TN) + _dot(kz[h], gb[h]) for h in hs]
        gn = [g[h] * gd_ref[h] + _dg(qx[h], drb[h], TN) for h in hs]
        for h in hs:
            gst[h] = gn[h]
            dq_ref[:, RDK * h:RDK * (h + 1)] = (dq[h] * cos + pltpu.roll(dq[h] * sin, 64, 1)).astype(BF)
            dk_ref[:, RDK * h:RDK * (h + 1)] = (dk[h] * cos + pltpu.roll(dk[h] * sin, 64, 1)).astype(BF)
            dv_ref[:, RDV * h:RDV * (h + 1)] = dv[h].astype(BF)

    rev = lambda n: NCH - 1 - n
    return pl.pallas_call(
        body, name="ret_bwd", grid=(NCH,),
        in_specs=_ret_specs(rev) + [
            pl.BlockSpec((1, RH, RDK, RDV), lambda n: (rev(n), 0, 0, 0)),
            pl.BlockSpec((C, RH * RDV), lambda n: (rev(n), 0)),
        ],
        out_specs=[pl.BlockSpec((C, RH * RDK), lambda n: (rev(n), 0)),
                   pl.BlockSpec((C, RH * RDK), lambda n: (rev(n), 0)),
                   pl.BlockSpec((C, RH * RDV), lambda n: (rev(n), 0))],
        out_shape=[jax.ShapeDtypeStruct((T, RH * RDK), BF), jax.ShapeDtypeStruct((T, RH * RDK), BF),
                   jax.ShapeDtypeStruct((T, RH * RDV), BF)],
        scratch_shapes=[pltpu.VMEM((RH, RDK, RDV), F32)],
        compiler_params=_params(("arbitrary",)),
    )(z, z, z, cst["cos2"], cst["sin2"], cst["dmask"], cst["xi"], cst["zeta"], cst["gdec"], sprev, dr)


def _place():
    x, y, c = lax.axis_index("x"), lax.axis_index("y"), lax.axis_index("c")
    return x, y, c


def _other_chips(x, y):
    return [(1 - x, y, 2 * (1 - x) + y), (x, 1 - y, 2 * x + (1 - y)), (1 - x, 1 - y, 2 * (1 - x) + (1 - y))]


def _chip_copies(srcs, lands, send_sems, recv_sems, by_dest):
    x, y, c = _place()
    me_s = 2 * x + y
    return [pltpu.make_async_remote_copy(
        src_ref=src.at[cs] if by_dest else src, dst_ref=land.at[me_s],
        send_sem=send_sems.at[3 * a + j], recv_sem=recv_sems.at[3 * a + j],
        device_id=(cx, cy, c), device_id_type=MESH)
        for a, (src, land) in enumerate(zip(srcs, lands)) for j, (cx, cy, cs) in enumerate(_other_chips(x, y))]


def _split_dot(x, mat01, dims=NN_DIMS, x_first=True):
    acc, rest = None, x
    for _ in range(3):
        piece = rest.astype(BF)
        part = _dg(piece, mat01, dims) if x_first else _dg(mat01, piece, dims)
        acc = part if acc is None else acc + part
        rest = rest - piece.astype(F32)
    return acc


def _log_sigmoid(x):
    return -(jnp.maximum(-x, 0.0) + jnp.log1p(jnp.exp(-jnp.abs(x))))


def _fox_prep(zf, bf_pad, cst):
    def body(zf_ref, b_ref, tri_ref, ct_ref, carry):
        n = pl.program_id(0)

        @pl.when(n == 0)
        def _():
            carry[...] = jnp.zeros_like(carry)

        ls = _log_sigmoid(zf_ref[...] + b_ref[...])
        row = n * C + lax.broadcasted_iota(jnp.int32, (C, C), 0)
        lf = jnp.where(row >= PAD, ls, 0.0)
        cc = _split_dot(lf, tri_ref[...], x_first=False) + carry[0:1, :]
        carry[...] = jnp.broadcast_to(cc[C - 1:C, :], carry.shape)
        pos = n * C + lax.broadcasted_iota(jnp.int32, (FH, C), 1)
        ct_ref[0] = jnp.where(pos >= PAD, cc.T[:FH, :], -NEG)

    return pl.pallas_call(
        body, name="fox_prep", grid=(NCH,),
        in_specs=[pl.BlockSpec((C, C), lambda n: (n, 0)), pl.BlockSpec((1, C), lambda n: (0, 0)),
                  pl.BlockSpec((C, C), lambda n: (0, 0))],
        out_specs=pl.BlockSpec((1, FH, C), lambda n: (n, 0, 0)),
        out_shape=jax.ShapeDtypeStruct((NCH, FH, C), F32),
        scratch_shapes=[pltpu.VMEM((8, C), F32)],
        compiler_params=_params(("arbitrary",)),
    )(zf, bf_pad, cst["tri"])


def _lo_lanes(shape):
    return lax.broadcasted_iota(jnp.int32, shape, 1) < FD


def _split_heads(x):
    lo = _lo_lanes(x.shape)
    zero = jnp.zeros_like(x)
    return jnp.concatenate([jnp.where(lo, x, zero), jnp.where(lo, zero, x)], axis=0)


def _spread2(x):
    lo = _lo_lanes(x.shape)
    r = pltpu.roll(x, FD, 1)
    return jnp.concatenate([jnp.where(lo, x, r), jnp.where(lo, r, x)], axis=1)


NSTEP = (NCH + 1) // 2
NTILE = NCH + 1
TROWS = T + C


def _fox_tile(s, t):
    second = t > s
    return second.astype(jnp.int32), jnp.where(second, t - s - 1, s - t)


def _fox_pos(i):
    return jnp.where(i < NSTEP, 2 * i, 2 * (NCH - 1 - i) + 1)


FOX_ORDER = [2 * i if i < NSTEP else 2 * (NCH - 1 - i) + 1 for i in range(NCH)]


def _fox_pair_specs():
    first = pl.BlockSpec((C, C), lambda p, s: (2 * s, p))
    second = pl.BlockSpec((C, C), lambda p, s: (jnp.where(s == NSTEP - 1, 2 * s, 2 * s + 1), p))
    both = pl.BlockSpec((2 * C, C), lambda p, s: (s, p))
    return first, second, both


def _fox_q_specs():
    return (pl.BlockSpec((C, C), lambda p, s: (s, QB_F + p)),
            pl.BlockSpec((C, C), lambda p, s: (NCH - 1 - s, QB_F + p)))


def _fox_key_bias(ct_ref, p, j):
    return jnp.concatenate([ct_ref[j, pl.ds(2 * p, 1), :], ct_ref[j, pl.ds(2 * p + 1, 1), :]], axis=1)


def _fox_fwd(z, ct, cst, share):
    n = 0 if share is None else 1

    def body(qa_ref, qb_ref, k_ref, v_ref, ct_ref, ones_ref, mb_ref, *rest):
        share_refs, (a_ref, g_ref), land_refs = rest[:n], rest[n:n + 2], rest[n + 2:2 * n + 2]
        kks, vvs, q2, m2, sbuf = rest[2 * n + 2:2 * n + 7]
        p, s = pl.program_id(0), pl.program_id(1)
        if n:
            copies = _chip_copies(share_refs, land_refs, *rest[2 * n + 7:], by_dest=False)

            @pl.when((p == 0) & (s == 0))
            def _():
                for cp in copies:
                    cp.start()

            @pl.when((p == NPAIR - 1) & (s == NSTEP - 1))
            def _():
                for cp in copies:
                    cp.wait()

        @pl.when(s == 0)
        def _():
            ones = ones_ref[...]

            def prep(j, carry):
                rows = pl.ds(pl.multiple_of(j * C, C), C)
                kks[j] = _split_heads(k_ref[rows, :]).astype(BF)
                vvs[j] = jnp.concatenate([_split_heads(v_ref[rows, :]).astype(BF), ones], axis=1)
                return carry

            lax.fori_loop(0, NCH, prep, 0)

        q2[0] = (qa_ref[...] * FSCALE).astype(BF)
        q2[1] = (qb_ref[...] * FSCALE).astype(BF)

        tiles = [_fox_tile(s, t) for t in range(NTILE)]
        causal = mb_ref[1]
        neg = jnp.full((C, 2 * C), NEG, F32)
        run, first = neg, neg
        for t, (sel, j) in enumerate(tiles):
            st = _dg(q2[sel], kks[j], NT) - _fox_key_bias(ct_ref, p, j)
            if t in (0, NTILE - 1):
                st = st + causal
            sbuf[t] = st
            run = jnp.maximum(jnp.where(t == s + 1, neg, run), st)
            first = jnp.where(t == s, run, first)
        for w, mx in enumerate((first, run)):
            m2[w] = jnp.concatenate(
                [jnp.broadcast_to(jnp.max(mx[:, :C], axis=1, keepdims=True), (C, C)),
                 jnp.broadcast_to(jnp.max(mx[:, C:], axis=1, keepdims=True), (C, C))], axis=1)

        zero = jnp.zeros((C, 2 * C), F32)
        run, first = zero, zero
        for t, (sel, j) in enumerate(tiles):
            run = jnp.where(t == s + 1, zero, run) + _dot(jnp.exp(sbuf[t] - m2[sel]).astype(BF), vvs[j])
            first = jnp.where(t == s, run, first)
        lo = _lo_lanes((C, C))
        for w, res in enumerate((first, run)):
            l = res[:, C:]
            a_ref[C * w:C * (w + 1), :] = res[:, :C] / l
            mw = m2[w]
            g_ref[C * w:C * (w + 1), :] = -(jnp.where(lo, mw[:, :C], mw[:, C:]) + jnp.log(l))

    qa, qb = _fox_q_specs()
    both = _fox_pair_specs()[2]
    return pl.pallas_call(
        body, name="fox_fwd", grid=(NPAIR, NSTEP),
        in_specs=[qa, qb,
                  pl.BlockSpec((T, C), lambda p, s: (0, KB_F + p)),
                  pl.BlockSpec((T, C), lambda p, s: (0, VB_F + p)),
                  pl.BlockSpec((NCH, FH, C), lambda p, s: (0, 0, 0)),
                  pl.BlockSpec((2 * C, C), lambda p, s: (0, 0)),
                  pl.BlockSpec((2, C, 2 * C), lambda p, s: (0, 0, 0))] + [ANY] * n,
        out_specs=[both, both] + [ANY] * n,
        out_shape=[jax.ShapeDtypeStruct((TROWS, FH * FD), F32)] * 2
        + ([jax.ShapeDtypeStruct((4,) + share.shape, share.dtype)] if n else []),
        scratch_shapes=[pltpu.VMEM((NCH, 2 * C, C), BF), pltpu.VMEM((NCH, 2 * C, 2 * C), BF),
                        pltpu.VMEM((2, C, C), BF), pltpu.VMEM((2, C, 2 * C), F32),
                        pltpu.VMEM((NTILE, C, 2 * C), F32)]
        + [pltpu.SemaphoreType.DMA((3,)), pltpu.SemaphoreType.DMA((3,))] * n,
        compiler_params=_params(("arbitrary", "arbitrary")),
    )(z, z, z, z, ct, cst["ones_aug"], cst["mask_bias"], *([share] * n))


def _fox_bwd(z, da, g, delta, ct, cst):
    grp = 9

    def body(qa_ref, qb_ref, daa_ref, dab_ref, ga_ref, gb_ref, dla_ref, dlb_ref, k_ref, v_ref, ct_ref, ones_ref,
             mb_ref, dq_ref, dr_ref, dk_ref, dv_ref, dcs_ref,
             kks, vvs, q2, qq2, dd2, da2, gi2, dl2, dq2, dvb, dkb, dkacc, dvacc, csacc):
        p, s = pl.program_id(0), pl.program_id(1)
        ones = ones_ref[...]

        @pl.when(s == 0)
        def _():
            dkacc[...] = jnp.zeros_like(dkacc)
            dvacc[...] = jnp.zeros_like(dvacc)
            csacc[...] = jnp.zeros_like(csacc)

            def prep(j, carry):
                rows = pl.ds(pl.multiple_of(j * C, C), C)
                kks[j] = _split_heads(k_ref[rows, :]).astype(BF)
                vvs[j] = _split_heads(v_ref[rows, :]).astype(BF)
                return carry

            lax.fori_loop(0, NCH, prep, 0)

        for w, (q_ref, d_ref, g_ref, l_ref) in enumerate(((qa_ref, daa_ref, ga_ref, dla_ref),
                                                          (qb_ref, dab_ref, gb_ref, dlb_ref))):
            qf = q_ref[...]
            q2[w] = (qf * FSCALE).astype(BF)
            qq2[w] = jnp.concatenate([_split_heads(qf).astype(BF), ones], axis=1)
            da2[w] = d_ref[...]
            dd2[w] = _split_heads(d_ref[...].astype(F32)).astype(BF)
            gi2[w] = _spread2(g_ref[...])
            dl2[w] = _spread2(l_ref[...])
        dq2[...] = jnp.zeros_like(dq2)
        zero = jnp.zeros((C, 2 * C), F32)

        def group(gi, carry):
            ts = [gi * grp + u for u in range(grp)]
            tiles = [_fox_tile(s, t) for t in ts]
            kk = [kks[j] for _, j in tiles]
            ss = [_dg(q2[sel], kj, NT) + (gi2[sel] - _fox_key_bias(ct_ref, p, j)) for kj, (sel, j) in zip(kk, tiles)]
            ss[0] = ss[0] + mb_ref[(gi == 0).astype(jnp.int32)]
            ss[-1] = ss[-1] + mb_ref[(gi == 1).astype(jnp.int32)]
            dps = [_dg(da2[sel], vvs[j], NT) for sel, j in tiles]
            pes = [jnp.exp(st) for st in ss]
            dss = [pe * (dp - dl2[sel]) * FSCALE for pe, dp, (sel, _) in zip(pes, dps, tiles)]
            pts = [jnp.concatenate([pe[:, :C].T, pe[:, C:].T], axis=1).astype(BF) for pe in pes]
            dsts = [jnp.concatenate([ds[:, :C].T, ds[:, C:].T], axis=1).astype(BF) for ds in dss]
            dvs = [_dot(pt, dd2[sel]) for pt, (sel, _) in zip(pts, tiles)]
            rs = [_dot(dst, qq2[sel]) for dst, (sel, _) in zip(dsts, tiles)]
            parts = [_dot(ds.astype(BF), jnp.concatenate([kj, ones], axis=1)) for ds, kj in zip(dss, kk)]
            for t, dv, rr in zip(ts, dvs, rs):
                dvb[t] = dv
                dkb[t] = rr
            pa, pb = zero, zero
            for t, part in zip(ts, parts):
                pa = pa + jnp.where(t <= s, part, zero)
                pb = pb + jnp.where(t <= s, zero, part)
            dq2[0] += pa
            dq2[1] += pb
            return carry

        ntile = jnp.where(s == NSTEP - 1, grp, NTILE)
        lax.fori_loop(0, ntile // grp, group, 0)

        def scatter(t, carry):
            _, j = _fox_tile(s, t)
            r = pl.ds(pl.multiple_of(j * C, C), C)
            dvacc[r, :] += dvb[t]
            dkacc[r, :] += dkb[t, :, :C]
            csacc[r, :] += dkb[t, :, C:]
            return carry

        lax.fori_loop(0, ntile, scatter, 0)
        for w in range(2):
            res = dq2[w]
            dq_ref[C * w:C * (w + 1), :] = res[:, :C].astype(BF)
            dr_ref[C * w:C * (w + 1), :] = res[:, C:]

        @pl.when(s == NSTEP - 1)
        def _():
            dk_ref[...] = dkacc[...].astype(BF)
            dv_ref[...] = dvacc[...].astype(BF)
            dcs_ref[...] = csacc[...]

    qa, qb = _fox_q_specs()
    ba, bb, both = _fox_pair_specs()
    col = pl.BlockSpec((T, C), lambda p, s: (0, p))
    return pl.pallas_call(
        body, name="fox_bwd", grid=(NPAIR, NSTEP),
        in_specs=[qa, qb, ba, bb, ba, bb, ba, bb,
                  pl.BlockSpec((T, C), lambda p, s: (0, KB_F + p)),
                  pl.BlockSpec((T, C), lambda p, s: (0, VB_F + p)),
                  pl.BlockSpec((NCH, FH, C), lambda p, s: (0, 0, 0)),
                  pl.BlockSpec((2 * C, C), lambda p, s: (0, 0)),
                  pl.BlockSpec((2, C, 2 * C), lambda p, s: (0, 0, 0))],
        out_specs=[both, both, col, col, col],
        out_shape=[jax.ShapeDtypeStruct((TROWS, FH * FD), BF), jax.ShapeDtypeStruct((TROWS, FH * FD), F32),
                   jax.ShapeDtypeStruct((T, FH * FD), BF), jax.ShapeDtypeStruct((T, FH * FD), BF),
                   jax.ShapeDtypeStruct((T, FH * FD), F32)],
        scratch_shapes=[pltpu.VMEM((NCH, 2 * C, C), BF), pltpu.VMEM((NCH, 2 * C, C), BF),
                        pltpu.VMEM((2, C, C), BF), pltpu.VMEM((2, 2 * C, 2 * C), BF), pltpu.VMEM((2, 2 * C, C), BF),
                        pltpu.VMEM((2, C, C), BF), pltpu.VMEM((2, C, 2 * C), F32), pltpu.VMEM((2, C, 2 * C), F32),
                        pltpu.VMEM((2, C, 2 * C), F32),
                        pltpu.VMEM((NTILE, C, C), F32), pltpu.VMEM((NTILE, C, 2 * C), F32),
                        pltpu.VMEM((T, C), F32), pltpu.VMEM((T, C), F32), pltpu.VMEM((T, C), F32)],
        compiler_params=_params(("parallel", "arbitrary")),
    )(z, z, da, da, g, g, delta, delta, z, z, ct, cst["ones_aug"], cst["mask_bias"])


def _fox_gate_bwd(drow, dcol, zf, bf_pad, cst):
    def body(dr_ref, dc_ref, zf_ref, b_ref, tri_ref, pick_ref, dff_ref, db_ref, carry):
        s = pl.program_id(0)
        n = NCH - 1 - s

        @pl.when(s == 0)
        def _():
            carry[...] = jnp.zeros_like(carry)
            db_ref[...] = jnp.zeros_like(db_ref)

        dcb = _split_dot((dr_ref[...] - dc_ref[...]) * (1.0 / FSCALE), pick_ref[...])
        suf = _split_dot(dcb, tri_ref[...], TN, x_first=False) + carry[0:1, :]
        carry[...] = jnp.broadcast_to(suf[0:1, :], carry.shape)
        x = zf_ref[...] + b_ref[...]
        row = n * C + lax.broadcasted_iota(jnp.int32, (C, C), 0)
        dff = jnp.where(row >= PAD, suf * (1.0 - jax.nn.sigmoid(x)), 0.0)
        dff_ref[...] = dff.astype(BF)
        db_ref[...] += jnp.sum(dff, axis=0, keepdims=True)

    rev = lambda s: (NCH - 1 - s, 0)
    return pl.pallas_call(
        body, name="fox_gate_bwd", grid=(NCH,),
        in_specs=[pl.BlockSpec((C, FH * FD), lambda s: (_fox_pos(NCH - 1 - s), 0)),
                  pl.BlockSpec((C, FH * FD), rev), pl.BlockSpec((C, C), rev),
                  pl.BlockSpec((1, C), lambda s: (0, 0)), pl.BlockSpec((C, C), lambda s: (0, 0)),
                  pl.BlockSpec((FH * FD, C), lambda s: (0, 0))],
        out_specs=[pl.BlockSpec((C, C), rev), pl.BlockSpec((1, C), lambda s: (0, 0))],
        out_shape=[jax.ShapeDtypeStruct((T, C), BF), jax.ShapeDtypeStruct((1, C), F32)],
        scratch_shapes=[pltpu.VMEM((8, C), F32)],
        compiler_params=_params(("arbitrary",)),
    )(drow, dcol, zf, bf_pad, cst["tri"], cst["pick"])


def _head_norm(r):
    rn, rs = [], []
    for h in range(RH):
        rh = r[:, RDV * h:RDV * (h + 1)]
        s = lax.rsqrt(jnp.mean(rh * rh, axis=1, keepdims=True) + EPS)
        rn.append(rh * s)
        rs.append(s)
    return jnp.concatenate(rn, axis=1), rs


def _gated(r, rg, a, fg):
    rn, _ = _head_norm(r)
    return jnp.concatenate([rn * (rg * jax.nn.sigmoid(rg)), a * (fg * jax.nn.sigmoid(fg))], axis=1)


def _out_loss(r, z, a, wout, x, tgt, fgain):
    def body(r_ref, rg_ref, a_ref, fg_ref, w_ref, x_ref, t_ref, g_ref, yt_ref, do_ref, dob_ref, loss_ref, dg_ref):
        i = pl.program_id(0)

        @pl.when(i == 0)
        def _():
            yt_ref[...] = jnp.zeros_like(yt_ref)
            do_ref[...] = jnp.zeros_like(do_ref)
            dob_ref[...] = jnp.zeros_like(dob_ref)
            loss_ref[...] = jnp.zeros_like(loss_ref)
            dg_ref[...] = jnp.zeros_like(dg_ref)

        @pl.when(i > 0)
        def _():
            y = _gated(r_ref[...], rg_ref[...], a_ref[...], fg_ref[...])
            yt_ref[...] = y.T.astype(BF)
            o = x_ref[...] + _dot(y.astype(BF), w_ref[...])
            rs = lax.rsqrt(jnp.mean(o * o, axis=1, keepdims=True) + EPS)
            on = o * rs
            g = g_ref[...]
            e = on * g - t_ref[...]
            loss_ref[...] += 0.5 * jnp.sum(jnp.mean(e * e, axis=1, keepdims=True))
            dyh = e * (1.0 / D)
            dg_ref[...] += jnp.sum(dyh * on, axis=0, keepdims=True)
            don = dyh * g
            do = rs * (don - on * jnp.mean(don * on, axis=1, keepdims=True))
            do_ref[...] = do
            dob_ref[...] = do.astype(BF)

    tok = lambda i: (jnp.maximum(i - 1, 0), 0)
    return pl.pallas_call(
        body, name="out_loss", grid=(NCH,),
        in_specs=[pl.BlockSpec((C, D), lambda i: (i, 0)), pl.BlockSpec((C, D), lambda i: (i, GB_R)),
                  pl.BlockSpec((C, D), lambda i: (_fox_pos(i), 0)), pl.BlockSpec((C, D), lambda i: (i, GB_F)),
                  pl.BlockSpec((DMIX, D), lambda i: (0, 0)),
                  pl.BlockSpec((C, D), tok), pl.BlockSpec((C, D), tok), pl.BlockSpec((1, D), lambda i: (0, 0))],
        out_specs=[pl.BlockSpec((DMIX, C), lambda i: (0, i)), pl.BlockSpec((C, D), lambda i: (i, 0)),
                   pl.BlockSpec((C, D), lambda i: (i, 0)), pl.BlockSpec((8, C), lambda i: (0, 0)),
                   pl.BlockSpec((1, D), lambda i: (0, 0))],
        out_shape=[jax.ShapeDtypeStruct((DMIX, T), BF), jax.ShapeDtypeStruct((T, D), F32),
                   jax.ShapeDtypeStruct((T, D), BF), jax.ShapeDtypeStruct((8, C), F32),
                   jax.ShapeDtypeStruct((1, D), F32)],
        compiler_params=_params(("arbitrary",)),
    )(r, z, a, z, wout, x, tgt, fgain)


def _silu_and_grad(x):
    s = jax.nn.sigmoid(x)
    return x * s, s * (1.0 + x * (1.0 - s))


def _dy_gate_bwd(dob, wout, r, z, a, seg):
    def body(do_ref, w_ref, r_ref, rg_ref, a_ref, fg_ref, seg_ref, dr_ref, da_ref, drg_ref, dfg_ref, dl_ref):
        dy = _dg(do_ref[...], w_ref[...], NT)
        a_ = a_ref[...]
        rn, rs = _head_norm(r_ref[...])
        silu_rg, dsilu_rg = _silu_and_grad(rg_ref[...])
        silu_fg, dsilu_fg = _silu_and_grad(fg_ref[...])
        dyr, dyf = dy[:, :D], dy[:, D:]
        drn = dyr * silu_rg
        drg_ref[...] = (dyr * rn * dsilu_rg).astype(BF)
        for h in range(RH):
            sl = slice(RDV * h, RDV * (h + 1))
            dh, nh = drn[:, sl], rn[:, sl]
            dr_ref[:, sl] = (rs[h] * (dh - nh * jnp.mean(dh * nh, axis=1, keepdims=True))).astype(BF)
        dab = (dyf * silu_fg).astype(BF)
        da_ref[...] = dab
        dfg_ref[...] = (dyf * a_ * dsilu_fg).astype(BF)
        prod = dab.astype(F32) * a_
        segm = seg_ref[...]
        for p in range(NPAIR):
            sl = slice(C * p, C * (p + 1))
            hi = prod[:, sl].astype(BF)
            lo = (prod[:, sl] - hi.astype(F32)).astype(BF)
            dl_ref[:, sl] = _dot(hi, segm) + _dot(lo, segm)

    row = pl.BlockSpec((C, D), lambda i: (i, 0))
    fox = pl.BlockSpec((C, D), lambda i: (_fox_pos(i), 0))
    return pl.pallas_call(
        body, name="dy_gate_bwd", grid=(NCH,),
        in_specs=[row, pl.BlockSpec((DMIX, D), lambda i: (0, 0)),
                  row, pl.BlockSpec((C, D), lambda i: (i, GB_R)),
                  fox, pl.BlockSpec((C, D), lambda i: (i, GB_F)),
                  pl.BlockSpec((C, C), lambda i: (0, 0))],
        out_specs=[row, fox, row, row, fox],
        out_shape=[jax.ShapeDtypeStruct((T, D), BF), jax.ShapeDtypeStruct((TROWS, D), BF),
                   jax.ShapeDtypeStruct((T, D), BF), jax.ShapeDtypeStruct((T, D), BF),
                   jax.ShapeDtypeStruct((TROWS, D), F32)],
        compiler_params=_params(("parallel",)),
    )(dob, wout, r, z, a, z, seg)


DZ_WIDTHS = (512, 512, 1024, 1024, 1024, 1024, 1024, 1024)


def _du_norm_bwd(dzs, dzf, wt, wft, hpad, g, dopad, parts=()):
    tm, tk = 544, 1024
    nk = WMAIN // tk
    ni = T // tm
    n = len(parts)

    def body(rq_ref, rk_ref, rv_ref, rg_ref, fq_ref, fk_ref, fv_ref, fg_ref, dzf_ref, w_ref, wf_ref, h_ref, g_ref,
             do_ref, *rest):
        part_refs, (gh_ref, dg_ref), land_refs = rest[:n], rest[n:n + 2], rest[n + 2:2 * n + 2]
        acc = rest[2 * n + 2]
        i, k = pl.program_id(0), pl.program_id(1)

        if n:
            send_sems, recv_sems = rest[2 * n + 3:]
            copies = _chip_copies(part_refs, land_refs, send_sems, recv_sems, by_dest=True)

            @pl.when((i == 0) & (k == 0))
            def _():
                for cp in copies:
                    cp.start()

            @pl.when((i == ni - 1) & (k == nk - 1))
            def _():
                for cp in copies:
                    cp.wait()

        @pl.when(k == 0)
        def _():
            acc[...] = (_dot(dzf_ref[...], wf_ref[...]) + _dot(rq_ref[...], w_ref[:512, :])
                        + _dot(rk_ref[...], w_ref[512:, :]))

        for kk, piece in enumerate((rv_ref, rg_ref, fq_ref, fk_ref, fv_ref, fg_ref), start=1):
            @pl.when(k == kk)
            def _(piece=piece):
                acc[...] += _dot(piece[...], w_ref[...])

        @pl.when(k == nk - 1)
        def _():
            du = acc[...]
            h = h_ref[...]
            gg = g_ref[...]
            rs = lax.rsqrt(jnp.mean(h * h, axis=1, keepdims=True) + EPS)
            hn = h * rs
            part = jnp.sum(du * hn, axis=0, keepdims=True)

            @pl.when(i == 0)
            def _():
                dg_ref[...] = part

            @pl.when(i > 0)
            def _():
                dg_ref[...] += part

            dhn = du * gg
            gh_ref[...] = rs * (dhn - hn * jnp.mean(dhn * hn, axis=1, keepdims=True)) + do_ref[...]

    sems = [pltpu.SemaphoreType.DMA((3 * n,)), pltpu.SemaphoreType.DMA((3 * n,))] if n else []
    return pl.pallas_call(
        body, name="du_norm_bwd", grid=(ni, nk),
        in_specs=[pl.BlockSpec((tm, w), lambda i, k: (i, 0)) for w in DZ_WIDTHS]
        + [pl.BlockSpec((tm, C), lambda i, k: (i, 0)),
           pl.BlockSpec((tk, D), lambda i, k: (k, 0)), pl.BlockSpec((C, D), lambda i, k: (0, 0)),
           pl.BlockSpec((tm, D), lambda i, k: (i, 0)), pl.BlockSpec((1, D), lambda i, k: (0, 0)),
           pl.BlockSpec((tm, D), lambda i, k: (i, 0))] + [ANY] * n,
        out_specs=[pl.BlockSpec((tm, D), lambda i, k: (i, 0)), pl.BlockSpec((1, D), lambda i, k: (0, 0))] + [ANY] * n,
        out_shape=[jax.ShapeDtypeStruct((T, D), F32), jax.ShapeDtypeStruct((1, D), F32)]
        + [jax.ShapeDtypeStruct(p.shape, p.dtype) for p in parts],
        scratch_shapes=[pltpu.VMEM((tm, D), F32)] + sems,
        compiler_params=_params(("arbitrary", "arbitrary")),
    )(*dzs, dzf, wt, wft, hpad, g, dopad, *parts)


GROWS = 7680


def _dw_in(dzs, dzf, ut):
    tn = 512
    nmain = WMAIN // tn
    first, blocks = [], []
    for w in DZ_WIDTHS:
        first.append(sum(blocks))
        blocks.append(w // tn)

    def body(rq_ref, rk_ref, rv_ref, rg_ref, fq_ref, fk_ref, fv_ref, fg_ref, dzf_ref, ut_ref, o_ref):
        gidx = pl.program_id(0)
        for piece, g0, nb in zip((rq_ref, rk_ref, rv_ref, rg_ref, fq_ref, fk_ref, fv_ref, fg_ref), first, blocks):
            @pl.when((gidx >= g0) & (gidx < g0 + nb))
            def _(piece=piece):
                o_ref[...] = _dot(ut_ref[...], piece[...]).T.astype(BF)

        @pl.when(gidx == nmain)
        def _():
            o_ref[:C, :] = _dot(ut_ref[...], dzf_ref[...]).T.astype(BF)
            o_ref[C:, :] = jnp.zeros((tn - C, D), BF)

    def piece_spec(g0, nb):
        return pl.BlockSpec((T, tn), lambda gidx: (0, jnp.clip(gidx - g0, 0, nb - 1)))

    return pl.pallas_call(
        body, name="dw_in", grid=(nmain + 1,),
        in_specs=[piece_spec(g0, nb) for g0, nb in zip(first, blocks)]
        + [pl.BlockSpec((T, C), lambda gidx: (0, 0)), pl.BlockSpec((D, T), lambda gidx: (0, 0))],
        out_specs=pl.BlockSpec((tn, D), lambda gidx: (gidx, 0)),
        out_shape=jax.ShapeDtypeStruct((GROWS, D), BF),
        compiler_params=pltpu.CompilerParams(dimension_semantics=("arbitrary",), vmem_limit_bytes=DW_VMEM_LIMIT),
    )(*dzs, dzf, ut)


def _token_order(x_po):
    def body(i_ref, o_ref):
        o_ref[...] = i_ref[...]

    return pl.pallas_call(
        body, name="token_order", grid=(NCH,),
        in_specs=[pl.BlockSpec((C, D), lambda i: (_fox_pos(i), 0))],
        out_specs=pl.BlockSpec((C, D), lambda i: (i, 0)),
        out_shape=jax.ShapeDtypeStruct((T, D), x_po.dtype),
        compiler_params=_params(("parallel",)),
    )(x_po)


def _local_step(x, tgt, normed, norm_g, wt, wft, b_f, wout, final_g, chip_sums=None, wout_full=None):
    cst = _constants()
    hpad, u, ut = normed
    bf_pad = jnp.pad(b_f, ((0, 0), (0, C - NFF)))
    z = _mm_nt(u, wt, WMAIN, T // 2, 1024, "in_proj")
    zf = _mm_nt(u, wft, C, T // 2, C, "in_proj_ff")
    r, sprev = _ret_fwd(z, cst)
    ct = _fox_prep(zf, bf_pad, cst)
    if wout_full is None:
        a, g = _fox_fwd(z, ct, cst, None)
    else:
        a, g, landed_wout = _fox_fwd(z, ct, cst, wout)
        wout = wout_full(landed_wout)
    yt, dopad, dob, loss8, dfg = _out_loss(r, z, a, wout, x, tgt, final_g)
    dr, da, dzrg, dzfg, delta = _dy_gate_bwd(dob, wout, r, z, a, cst["seg"])
    dwout = _mm_nn(yt, dob, 512, D, "dw_out", BF)
    dzq_r, dzk_r, dzv_r = _ret_bwd(z, cst, sprev, dr)
    dq_po, drow, dzk_f, dzv_f, dcol = _fox_bwd(z, da, g, delta, ct, cst)
    dzf, dbf = _fox_gate_bwd(drow, dcol, zf, bf_pad, cst)
    dzs = [dzq_r, dzk_r, dzv_r, dzrg, _token_order(dq_po), dzk_f, dzv_f, dzfg]
    gwt = _dw_in(dzs, dzf, ut)
    parts = chip_sums(gwt, dwout) if chip_sums else []
    gh, dng, *landed = _du_norm_bwd(dzs, dzf, wt, wft, hpad, norm_g, dopad, parts)
    return (loss8[0, 0], gh[C:], gh[PAD:C], dng, gwt, dbf[:, :NFF], dwout, dfg, parts, landed)


WOFF, WLEN = 1792, 2048
WHALF = WLEN // 2
LAP = WPADROWS - WOFF


def _own_window(w3):
    rows, sub, lanes = w3.shape
    pad = WPADROWS - rows
    tb = 96
    nb = WPADROWS // tb
    half = rows // 2

    def body(w_ref, o_ref, buf, sems):
        x, y, _ = _place()
        shift = 4 * (2 * x + y)
        buf[pl.ds(0, pad)] = jnp.zeros((pad, sub, lanes), F32)
        buf[pl.ds(rows, pad)] = jnp.zeros((pad, sub, lanes), F32)
        cps = [pltpu.make_async_copy(w_ref.at[pl.ds(half * h, half)], buf.at[pl.ds(shift + half * h, half)],
                                     sems.at[h]) for h in range(2)]
        for cp in cps:
            cp.start()

        def block(i, carry):
            r0 = pl.multiple_of(i * tb, tb)
            o_ref[pl.ds(r0, tb), :] = buf[pl.ds(r0, tb)].reshape(tb, sub * lanes).astype(BF)
            return carry

        cps[0].wait()
        lax.fori_loop(0, half // tb, block, 0)
        cps[1].wait()
        lax.fori_loop(half // tb, nb, block, 0)

    return pl.pallas_call(
        body, name="own_window",
        in_specs=[ANY], out_shape=jax.ShapeDtypeStruct((WPADROWS, sub * lanes), BF),
        scratch_shapes=[pltpu.VMEM((WPADROWS, sub, lanes), F32), pltpu.SemaphoreType.DMA((2,))],
        compiler_params=pltpu.CompilerParams(vmem_limit_bytes=VMEM_LIMIT),
    )(w3)


def _gather_weights(own_win, meta, x, norm_g):
    half_main, half_lap, half_meta = WOFF // 2, LAP // 2, meta.shape[0] // 2
    last = NCH - 1

    def body(win_ref, meta_ref, x_ref, g_ref, w_ref, laps_ref, gm_ref, h_ref, u_ref, ut_ref,
             send_sems, recv_sems, local_sems, stage, lapbuf, headbuf, metabuf):
        step = pl.program_id(0)
        x, y, c = _place()
        me_s = 2 * x + y
        sib = (x, y, 1 - c)
        chips = _other_chips(x, y)

        def emit(h):
            u = _norm_rows(h, g_ref[...])
            h_ref[...] = h
            u_ref[...] = u.astype(BF)
            ut_ref[...] = u.T.astype(BF)

        kinds = [
            (lambda h: win_ref.at[pl.ds(half_main * h, half_main)],
             lambda s, h: w_ref.at[pl.ds(WOFF * s + half_main * h, half_main)]),
            (lambda h: win_ref.at[pl.ds(WOFF + half_lap * h, half_lap)],
             lambda s, h: laps_ref.at[s, pl.ds(half_lap * h, half_lap)]),
            (lambda h: meta_ref.at[pl.ds(half_meta * h, half_meta)],
             lambda s, h: gm_ref.at[s, pl.ds(half_meta * h, half_meta)]),
        ]
        own_in = pltpu.make_async_copy(win_ref.at[pl.ds(0, WOFF)], stage, local_sems.at[0])
        own_lap_in = pltpu.make_async_copy(win_ref.at[pl.ds(WOFF, LAP)], lapbuf.at[0], local_sems.at[1])
        own_out = pltpu.make_async_copy(stage, w_ref.at[pl.ds(WOFF * me_s, WOFF)], local_sems.at[0])
        own_lap_out = pltpu.make_async_copy(lapbuf.at[0], laps_ref.at[me_s], local_sems.at[1])
        sends, arrivals, forwards, forwarded = [], [], [], []
        for a, (src, dst) in enumerate(kinds):
            for k, (cx, cy, cs) in enumerate(chips):
                there = dict(send_sem=send_sems.at[6 * a + k], recv_sem=recv_sems.at[6 * a + k],
                             device_id=(cx, cy, c), device_id_type=MESH)
                across = dict(send_sem=send_sems.at[6 * a + 3 + k], recv_sem=recv_sems.at[6 * a + 3 + k],
                              device_id=sib, device_id_type=MESH)
                sends.append(pltpu.make_async_remote_copy(src_ref=src(c), dst_ref=dst(me_s, c), **there))
                arrivals.append(pltpu.make_async_remote_copy(src_ref=dst(cs, c), dst_ref=dst(cs, c), **there))
                forwards.append(pltpu.make_async_remote_copy(src_ref=dst(cs, c), dst_ref=dst(cs, c), **across))
                forwarded.append(pltpu.make_async_remote_copy(
                    src_ref=dst(cs, 1 - c), dst_ref=dst(cs, 1 - c), **across))

        @pl.when(step == 0)
        def _():
            own_in.start()
            own_lap_in.start()
            for cp in sends:
                cp.start()
            own_in.wait()
            own_out.start()
            own_lap_in.wait()
            own_lap_out.start()

        @pl.when(step < last)
        def _():
            emit(x_ref[...])

        @pl.when(step == last)
        def _():
            for cp, fwd in zip(arrivals, forwards):
                cp.wait_recv()
                fwd.start()
            for cp in forwarded:
                cp.wait_recv()
            for cp in sends + forwards:
                cp.wait_send()
            own_out.wait()
            own_lap_out.wait()
            for s in range(1, 4):
                head = w_ref.at[pl.ds(WOFF * s, LAP)]
                loads = [pltpu.make_async_copy(laps_ref.at[s - 1], lapbuf.at[1], local_sems.at[2]),
                         pltpu.make_async_copy(head, headbuf, local_sems.at[3])]
                for cp in loads:
                    cp.start()
                for cp in loads:
                    cp.wait()
                headbuf[...] = (headbuf[...].astype(F32) + lapbuf[1].astype(F32)).astype(BF)
                store = pltpu.make_async_copy(headbuf, head, local_sems.at[3])
                store.start()
                store.wait()
            loads = [pltpu.make_async_copy(meta_ref, metabuf.at[me_s], local_sems.at[0])]
            loads += [pltpu.make_async_copy(gm_ref.at[cs], metabuf.at[cs], local_sems.at[1 + k])
                      for k, (_, _, cs) in enumerate(chips)]
            for cp in loads:
                cp.start()
            for cp in loads:
                cp.wait()
            tokens = jnp.concatenate([metabuf[s] for s in range(4)], axis=1)
            emit(jnp.concatenate([jnp.zeros((PAD, D), F32), tokens], axis=0))

    def chunk(i):
        return (i + 1) % NCH

    return pl.pallas_call(
        body, name="all_gather_w", grid=(NCH,),
        in_specs=[ANY, ANY, pl.BlockSpec((C, D), lambda i: (jnp.minimum(i, last - 1), 0)),
                  pl.BlockSpec((1, D), lambda i: (0, 0))],
        out_specs=[ANY] * 3 + [pl.BlockSpec((C, D), lambda i: (chunk(i), 0))] * 2
        + [pl.BlockSpec((D, C), lambda i: (0, chunk(i)))],
        out_shape=[jax.ShapeDtypeStruct((WMAIN, D), own_win.dtype), jax.ShapeDtypeStruct((4, LAP, D), own_win.dtype),
                   jax.ShapeDtypeStruct((4,) + meta.shape, meta.dtype),
                   jax.ShapeDtypeStruct((T, D), F32), jax.ShapeDtypeStruct((T, D), BF),
                   jax.ShapeDtypeStruct((D, T), BF)],
        scratch_shapes=[pltpu.SemaphoreType.DMA((18,)), pltpu.SemaphoreType.DMA((18,)), pltpu.SemaphoreType.DMA((4,)),
                        pltpu.VMEM((WOFF, D), own_win.dtype), pltpu.VMEM((2, LAP, D), own_win.dtype),
                        pltpu.VMEM((LAP, D), own_win.dtype), pltpu.VMEM((4,) + meta.shape, meta.dtype)],
        compiler_params=_params(("arbitrary",)),
    )(own_win, meta, x, norm_g)


def _pair_swap(gwt, arrs):
    n = len(arrs)

    def body(*refs):
        gw, ins = refs[0], refs[1:n + 1]
        gwo, outs = refs[n + 1], refs[n + 2:2 * n + 2]
        send_sems, recv_sems = refs[2 * n + 2:]
        x, y, c = _place()
        sib = (x, y, 1 - c)
        cps = []
        for k in range(4):
            cps.append(pltpu.make_async_remote_copy(
                src_ref=gw.at[pl.ds(WOFF * k + (1 - c) * WHALF, WHALF)], dst_ref=gwo.at[k],
                send_sem=send_sems.at[k], recv_sem=recv_sems.at[k], device_id=sib, device_id_type=MESH))
        for a in range(n):
            rows = ins[a].shape[1] // 2
            cps.append(pltpu.make_async_remote_copy(
                src_ref=ins[a].at[:, pl.ds((1 - c) * rows, rows)], dst_ref=outs[a],
                send_sem=send_sems.at[4 + a], recv_sem=recv_sems.at[4 + a], device_id=sib, device_id_type=MESH))
        for cp in cps:
            cp.start()
        for cp in cps:
            cp.wait()

    return pl.pallas_call(
        body, name="rs_pair_swap",
        in_specs=[ANY] * (n + 1), out_specs=[ANY] * (n + 1),
        out_shape=[jax.ShapeDtypeStruct((4, WHALF, D), gwt.dtype)]
        + [jax.ShapeDtypeStruct((4, a.shape[1] // 2, a.shape[2]), a.dtype) for a in arrs],
        scratch_shapes=[pltpu.SemaphoreType.DMA((n + 4,)), pltpu.SemaphoreType.DMA((n + 4,))],
    )(gwt, *arrs)


def _add_windows(gwt, recv):
    tb = 256
    nb = WHALF // tb
    c = lax.axis_index("c")

    def body(c_ref, a_ref, b_ref, o_ref):
        o_ref[0] = (a_ref[...].astype(F32) + b_ref[0].astype(F32)).astype(BF)

    return pl.pallas_call(
        body, name="pair_add_in",
        grid_spec=pltpu.PrefetchScalarGridSpec(
            num_scalar_prefetch=1, grid=(4, nb),
            in_specs=[pl.BlockSpec((tb, D), lambda k, i, cr: ((WOFF // tb) * k + nb * cr[0] + i, 0)),
                      pl.BlockSpec((1, tb, D), lambda k, i, cr: (k, i, 0))],
            out_specs=pl.BlockSpec((1, tb, D), lambda k, i, cr: (k, i, 0))),
        out_shape=jax.ShapeDtypeStruct(recv.shape, BF),
        compiler_params=_params(("parallel", "parallel")),
    )(jnp.reshape(c, (1,)).astype(jnp.int32), gwt, recv)


def _chip_exchange(parts, small):
    n = len(parts)

    def body(*refs):
        ins, sm = refs[:n], refs[n]
        outs, smo = refs[n + 1:2 * n + 1], refs[2 * n + 1]
        send_sems, recv_sems = refs[2 * n + 2:]
        cps = _chip_copies(ins, outs, send_sems, recv_sems, by_dest=True)
        cps += _chip_copies([sm], [smo], send_sems.at[pl.ds(3 * n, 3)], recv_sems.at[pl.ds(3 * n, 3)], by_dest=False)
        for cp in cps:
            cp.start()
        for cp in cps:
            cp.wait()

    return pl.pallas_call(
        body, name="rs_chip_exchange",
        in_specs=[ANY] * (n + 1), out_specs=[ANY] * (n + 1),
        out_shape=[jax.ShapeDtypeStruct(p.shape, p.dtype) for p in parts]
        + [jax.ShapeDtypeStruct((4,) + small.shape, small.dtype)],
        scratch_shapes=[pltpu.SemaphoreType.DMA((3 * (n + 1),)), pltpu.SemaphoreType.DMA((3 * (n + 1),))],
    )(*parts, small)


def _pair_send(halves):
    n = len(halves)

    def body(*refs):
        ins, outs = refs[:n], refs[n:2 * n]
        send_sems, recv_sems = refs[2 * n:]
        x, y, c = _place()
        cps = [pltpu.make_async_remote_copy(
            src_ref=ins[a], dst_ref=outs[a], send_sem=send_sems.at[a], recv_sem=recv_sems.at[a],
            device_id=(x, y, 1 - c), device_id_type=MESH) for a in range(n)]
        for cp in cps:
            cp.start()
        for cp in cps:
            cp.wait()

    return pl.pallas_call(
        body, name="rs_pair_send",
        in_specs=[ANY] * n, out_specs=[ANY] * n,
        out_shape=[jax.ShapeDtypeStruct(h.shape, h.dtype) for h in halves],
        scratch_shapes=[pltpu.SemaphoreType.DMA((n,)), pltpu.SemaphoreType.DMA((n,))],
    )(*halves)


def _row_block(rows):
    for tb in (256, 128, 64, 32, 16, 8):
        if rows % tb == 0:
            return tb
    return rows


def _add_halves(full, recv, name, out_dtype):
    _, r2, w = recv.shape
    tb = _row_block(r2)
    nb = r2 // tb
    c = lax.axis_index("c")

    def body(c_ref, a_ref, b_ref, o_ref):
        o_ref[...] = (a_ref[...].astype(F32) + b_ref[...].astype(F32)).astype(o_ref.dtype)

    return pl.pallas_call(
        body, name=name,
        grid_spec=pltpu.PrefetchScalarGridSpec(
            num_scalar_prefetch=1, grid=(4, nb),
            in_specs=[pl.BlockSpec((1, tb, w), lambda s, i, cr: (s, cr[0] * nb + i, 0)),
                      pl.BlockSpec((1, tb, w), lambda s, i, cr: (s, i, 0))],
            out_specs=pl.BlockSpec((1, tb, w), lambda s, i, cr: (s, i, 0))),
        out_shape=jax.ShapeDtypeStruct(recv.shape, out_dtype),
        compiler_params=_params(("parallel", "parallel")),
    )(jnp.reshape(c, (1,)).astype(jnp.int32), full, recv)


def _add2(a, b, name):
    def body(a_ref, b_ref, o_ref):
        o_ref[...] = a_ref[...] + b_ref[...]

    return pl.pallas_call(body, name=name, out_shape=jax.ShapeDtypeStruct(a.shape, a.dtype))(a, b)


def _sum4(buf, own, name):
    _, r, w = buf.shape
    tb = _row_block(r)
    me_s = 2 * lax.axis_index("x") + lax.axis_index("y")
    by_dest = own.ndim == 3

    def body(s_ref, b_ref, own_ref, o_ref):
        mine = (own_ref[0] if by_dest else own_ref[...]).astype(F32)
        terms = [jnp.where(s_ref[0] == t, mine, b_ref[t].astype(F32)) for t in range(4)]
        o_ref[...] = ((terms[0] + terms[1]) + terms[2]) + terms[3]

    own_spec = (pl.BlockSpec((1, tb, w), lambda i, sr: (sr[0], i, 0)) if by_dest
                else pl.BlockSpec((tb, w), lambda i, sr: (i, 0)))
    return pl.pallas_call(
        body, name=name,
        grid_spec=pltpu.PrefetchScalarGridSpec(
            num_scalar_prefetch=1, grid=(r // tb,),
            in_specs=[pl.BlockSpec((4, tb, w), lambda i, sr: (0, i, 0)), own_spec],
            out_specs=pl.BlockSpec((tb, w), lambda i, sr: (i, 0))),
        out_shape=jax.ShapeDtypeStruct((r, w), F32),
        compiler_params=_params(("parallel",)),
    )(jnp.reshape(me_s, (1,)).astype(jnp.int32), buf, own)


def _adamw_math(w, g, m, v):
    mn = B1 * m + (1.0 - B1) * g
    vn = B2 * v + (1.0 - B2) * (g * g)
    m_hat = mn / (1.0 - B1 ** STEP)
    v_hat = vn / (1.0 - B2 ** STEP)
    return -LR * (m_hat / (jnp.sqrt(v_hat) + AEPS) + WD * w), mn, vn


def _adamw(w, g, m, v, name):
    r, c_ = w.shape
    tb = _row_block(r)
    if tb == r and r > 512:
        tb = 256

    def body(w_ref, g_ref, m_ref, v_ref, d_ref, mo_ref, vo_ref):
        d_ref[...], mo_ref[...], vo_ref[...] = _adamw_math(w_ref[...], g_ref[...], m_ref[...], v_ref[...])

    spec = pl.BlockSpec((tb, c_), lambda i: (i, 0))
    return pl.pallas_call(
        body, name=name, grid=(pl.cdiv(r, tb),),
        in_specs=[spec] * 4, out_specs=[spec] * 3,
        out_shape=[jax.ShapeDtypeStruct(w.shape, F32)] * 3,
        compiler_params=_params(("parallel",)),
    )(w, g, m, v)


def _adamw_rows(w, g, m, v, name):
    r = w.shape[0]
    tb = 256
    sub, lanes = w.shape[1:]

    def body(w_ref, g_ref, m_ref, v_ref, go_ref, d_ref, mo_ref, vo_ref):
        g = g_ref[...].reshape(tb, sub, lanes)
        go_ref[...] = g
        d_ref[...], mo_ref[...], vo_ref[...] = _adamw_math(w_ref[...], g, m_ref[...], v_ref[...])

    spec = pl.BlockSpec((tb, sub, lanes), lambda i: (i, 0, 0))
    return pl.pallas_call(
        body, name=name, grid=(pl.cdiv(r, tb),),
        in_specs=[spec, pl.BlockSpec((tb, sub * lanes), lambda i: (i, 0)), spec, spec], out_specs=[spec] * 4,
        out_shape=[jax.ShapeDtypeStruct(w.shape, F32)] * 4,
        compiler_params=_params(("parallel",)),
    )(w, g, m, v)


def _adamw_halves(w, g_mine, g_sib, m, v, name):
    r, c_ = w.shape
    r2 = g_mine.shape[0]
    tb = _row_block(r2)
    nb = r2 // tb
    c = lax.axis_index("c")

    def body(c_ref, w_ref, gm_ref, gs_ref, m_ref, v_ref, g_ref, d_ref, mo_ref, vo_ref):
        g = jnp.where(pl.program_id(0) == c_ref[0], gm_ref[...], gs_ref[...])
        g_ref[...] = g
        d_ref[...], mo_ref[...], vo_ref[...] = _adamw_math(w_ref[...], g, m_ref[...], v_ref[...])

    full = pl.BlockSpec((tb, c_), lambda h, i, cr: (h * nb + i, 0))
    half = pl.BlockSpec((tb, c_), lambda h, i, cr: (i, 0))
    return pl.pallas_call(
        body, name=name,
        grid_spec=pltpu.PrefetchScalarGridSpec(
            num_scalar_prefetch=1, grid=(2, nb),
            in_specs=[full, half, half, full, full], out_specs=[full] * 4),
        out_shape=[jax.ShapeDtypeStruct(w.shape, F32)] * 4,
        compiler_params=_params(("parallel", "parallel")),
    )(jnp.reshape(c, (1,)).astype(jnp.int32), w, g_mine, g_sib, m, v)


def kernel(x, meta_tokens, norm_g, w_in, b_f, w_out, final_g, loss_target, m_meta_tokens, m_norm_g, m_w_in, m_b_f, m_w_out, m_final_g, v_meta_tokens, v_norm_g, v_w_in, v_b_f, v_w_out, v_final_g):
    me_s = 2 * lax.axis_index("x") + lax.axis_index("y")
    core = lax.axis_index("c")
    w3, m3, v3 = [jnp.transpose(jnp.reshape(t[0], (D // C, C, WSH)), (2, 0, 1)) for t in (w_in, m_w_in, v_w_in)]

    wt_main, laps, _, *normed = _gather_weights(_own_window(w3), meta_tokens, x[0], norm_g)
    wft = jnp.pad(laps[3, :NFF], ((0, C - NFF), (0, 0)))
    mine = (jnp.arange(4) == me_s)[:, None, None]
    wout_own = w_out[0].astype(BF)

    def wout_full(landed):
        return jnp.where(mine, wout_own[None], landed).reshape(DMIX, D)

    def chip_sums(gwt, dwout):
        g_out = dwout.reshape(4, DMIX // 4, D)
        r_in, r_out = _pair_swap(gwt, [g_out])
        return [_add_windows(gwt, r_in), _add_halves(g_out, r_out, "pair_add_out", BF)]

    loss, gx, dmeta, dng, gwt, dbf, dwout, dfg, (p_in, p_out), (e_in, e_out) = _local_step(
        x[0], loss_target[0], normed, norm_g, wt_main, wft, b_f, wout_own, final_g.reshape(1, D), chip_sums, wout_full)

    g_meta = jnp.stack([dmeta[:, 256 * s:256 * (s + 1)] for s in range(4)])
    small = jnp.concatenate([dng, dfg, jnp.pad(dbf, ((0, 0), (0, D - NFF))),
                             jnp.pad(jnp.reshape(loss, (1, 1)), ((0, 0), (0, D - 1))),
                             jnp.zeros((4, D), F32)], axis=0)
    e_meta, e_small = _chip_exchange([g_meta], small)
    h_in, h_out = _sum4(e_in, p_in, "sum_in"), _sum4(e_out, p_out, "sum_out")
    h_meta, h_small = _sum4(e_meta, g_meta, "sum_meta"), _sum4(e_small, small, "sum_small")
    s_in, s_out, s_meta, s_small = _pair_send([h_in, h_out, h_meta, h_small])
    gw_meta = _add2(h_meta, s_meta, "pair_add_meta")
    tot = _add2(h_small, s_small, "pair_add_small")
    g_norm, g_final, g_bf, loss_all = tot[0:1], tot[1], tot[2:3, :NFF], tot[3, 0]

    d_meta, nm_meta, nv_meta = _adamw(meta_tokens, gw_meta, m_meta_tokens, v_meta_tokens, "adamw_meta")
    d_norm, nm_norm, nv_norm = _adamw(norm_g, g_norm, m_norm_g, v_norm_g, "adamw_norm")
    window = jnp.concatenate([jnp.where(core == 0, h_in, s_in), jnp.where(core == 0, s_in, h_in)], axis=0)
    gwt_own = lax.dynamic_slice(window, (4 * me_s, 0), (WSH, D))
    outs_in = _adamw_rows(w3, gwt_own, m3, v3, "adamw_in")
    gw_in, d_in, nm_in, nv_in = [jnp.reshape(jnp.transpose(t, (1, 2, 0)), (1, D, WSH)) for t in outs_in]
    d_bf, nm_bf, nv_bf = _adamw(b_f, g_bf, m_b_f, v_b_f, "adamw_bf")
    gw_out, d_out, nm_out, nv_out = _adamw_halves(w_out[0], h_out, s_out, m_w_out[0], v_w_out[0], "adamw_out")
    d_fin, nm_fin, nv_fin = _adamw(final_g.reshape(1, D), g_final.reshape(1, D), m_final_g.reshape(1, D),
                                   v_final_g.reshape(1, D), "adamw_final")
    return (loss_all, gx[None], gw_meta, g_norm, gw_in, g_bf, gw_out[None], g_final,
            d_meta, d_norm, d_in, d_bf, d_out[None], d_fin.reshape(D),
            nm_meta, nm_norm, nm_in, nm_bf, nm_out[None], nm_fin.reshape(D),
            nv_meta, nv_norm, nv_in, nv_bf, nv_out[None], nv_fin.reshape(D))
```

```python
import numpy as np
import jax
import jax.numpy as jnp
from jax import lax
from jax.experimental import pallas as pl
from jax.experimental.pallas import tpu as pltpu

D = 1024
SEQ = 2048
NMETA = 16
C = 128
PAD = C - NMETA
T = PAD + NMETA + SEQ
NCH = T // C
RH, RDK, RDV = 4, 128, 256
FH, FD = 16, 64
NPAIR = FH // 2
WMAIN = 7168
NFF = 16
WIN = WMAIN + NFF
WSH = WIN // 4
WPADROWS = 1824
DMIX = 2048
EPS = 1e-6
NEG = -1e30
RSCALE = RDK ** -0.5
FSCALE = FD ** -0.5
ROPE_BASE = 10000.0
LR, B1, B2, AEPS, WD, STEP = 0.001, 0.9, 0.999, 1e-08, 0.01, 10

BF = jnp.bfloat16
F32 = jnp.float32
NT = (((1,), (1,)), ((), ()))
TN = (((0,), (0,)), ((), ()))
NN_DIMS = (((1,), (0,)), ((), ()))
MESH = pl.DeviceIdType.MESH
ANY = pl.BlockSpec(memory_space=pl.ANY)
VMEM_LIMIT = 48 * 1024 * 1024
DW_VMEM_LIMIT = 56 * 1024 * 1024

GB_R, GB_F = 2, 6
QB_F, KB_F, VB_F = 24, 32, 40


def _dot(a, b):
    return jnp.dot(a, b, preferred_element_type=F32)


def _dg(a, b, dims):
    return lax.dot_general(a, b, dims, preferred_element_type=F32)


def _params(sem=None):
    return pltpu.CompilerParams(dimension_semantics=sem, vmem_limit_bytes=VMEM_LIMIT)


def _constants():
    pos = jnp.arange(T, dtype=F32) - PAD
    inv = ROPE_BASE ** (-jnp.arange(0, RDK, 2, dtype=F32) / RDK)
    ang = pos[:, None] * inv[None, :]
    cos, sin = jnp.cos(ang), jnp.sin(ang)
    cos2 = jnp.concatenate([cos, cos], axis=1)
    sin2 = jnp.concatenate([-sin, sin], axis=1)
    log_gamma = jnp.log1p(-jnp.exp2(-5.0 - jnp.arange(RH, dtype=F32)))
    idx = jnp.arange(C, dtype=F32)
    diff = idx[:, None] - idx[None, :]
    dmask = jnp.where(diff[None] >= 0, jnp.exp(log_gamma[:, None, None] * jnp.maximum(diff, 0.0)[None]), 0.0)
    zeta = jnp.exp(log_gamma[:, None] * (C - 1.0 - idx)[None, :])
    xi = jnp.exp(log_gamma[:, None] * (idx + 1.0)[None, :])
    gdec = jnp.exp(log_gamma * C)
    zeta_b = jnp.broadcast_to(zeta[:, :, None], (RH, C, RDK))
    xi_b = jnp.broadcast_to(xi[:, :, None], (RH, C, RDK))
    gdec_b = jnp.broadcast_to(gdec[:, None, None], (RH, RDK, RDV))
    tri = jnp.asarray(np.tril(np.ones((C, C), np.float32)), dtype=BF)
    head_of_lane = np.arange(FH * FD) // FD
    pick = ((np.arange(FH * FD)[:, None] % FD == 0)
            & (head_of_lane[:, None] == np.arange(C)[None, :])).astype(np.float32)
    seg = (np.arange(C)[:, None] // FD == np.arange(C)[None, :] // FD).astype(np.float32)
    ones_aug = np.concatenate([np.tile((np.arange(C) < FD)[None, :], (C, 1)),
                               np.tile((np.arange(C) >= FD)[None, :], (C, 1))], axis=0).astype(np.float32)
    lane = np.arange(2 * C) % C
    causal = np.where(lane[None, :] <= np.arange(C)[:, None], 0.0, NEG).astype(np.float32)
    mask_bias = np.stack([np.zeros((C, 2 * C), np.float32), causal])
    return dict(cos2=cos2, sin2=sin2, dmask=dmask, zeta=zeta_b, xi=xi_b, gdec=gdec_b, tri=tri,
                mask_bias=jnp.asarray(mask_bias), pick=jnp.asarray(pick, dtype=BF), seg=jnp.asarray(seg, dtype=BF),
                ones_aug=jnp.asarray(ones_aug, dtype=BF))


def _norm_rows(h, g):
    return h * lax.rsqrt(jnp.mean(h * h, axis=1, keepdims=True) + EPS) * g


def _mm_nt(a, b, n, tm, tn, name):
    m, k = a.shape

    def body(a_ref, b_ref, o_ref):
        o_ref[...] = _dg(a_ref[...], b_ref[...], NT)

    return pl.pallas_call(
        body, name=name, grid=(m // tm, n // tn),
        in_specs=[pl.BlockSpec((tm, k), lambda i, j: (i, 0)), pl.BlockSpec((tn, k), lambda i, j: (j, 0))],
        out_specs=pl.BlockSpec((tm, tn), lambda i, j: (i, j)),
        out_shape=jax.ShapeDtypeStruct((m, n), F32),
        compiler_params=_params(("parallel", "parallel")),
    )(a, b)


def _mm_nn(a, b, tm, tn, name, out_dtype=F32):
    m, k = a.shape
    _, n = b.shape

    def body(a_ref, b_ref, o_ref):
        o_ref[...] = _dot(a_ref[...], b_ref[...]).astype(out_dtype)

    return pl.pallas_call(
        body, name=name, grid=(m // tm, n // tn),
        in_specs=[pl.BlockSpec((tm, k), lambda i, j: (i, 0)), pl.BlockSpec((k, tn), lambda i, j: (0, j))],
        out_specs=pl.BlockSpec((tm, tn), lambda i, j: (i, j)),
        out_shape=jax.ShapeDtypeStruct((m, n), out_dtype),
        compiler_params=_params(("parallel", "parallel")),
    )(a, b)


def _rot(x, cos2, sin2):
    return x * cos2 + pltpu.roll(x, 64, 1) * sin2


def _ret_specs(chunk):
    whole = lambda shape: pl.BlockSpec(shape, lambda n: (0,) * len(shape))
    return [
        pl.BlockSpec((C, RH * RDK), lambda n: (chunk(n), 0)),
        pl.BlockSpec((C, RH * RDK), lambda n: (chunk(n), 1)),
        pl.BlockSpec((C, RH * RDV), lambda n: (chunk(n), 1)),
        pl.BlockSpec((C, RDK), lambda n: (chunk(n), 0)),
        pl.BlockSpec((C, RDK), lambda n: (chunk(n), 0)),
        whole((RH, C, C)), whole((RH, C, RDK)), whole((RH, C, RDK)), whole((RH, RDK, RDV)),
    ]


def _ret_heads(q_ref, k_ref, v_ref, cos, sin):
    qr = [_rot(q_ref[:, RDK * h:RDK * (h + 1)], cos, sin) for h in range(RH)]
    kr = [_rot(k_ref[:, RDK * h:RDK * (h + 1)], cos, sin) * RSCALE for h in range(RH)]
    vb = [v_ref[:, RDV * h:RDV * (h + 1)].astype(BF) for h in range(RH)]
    return qr, kr, [t.astype(BF) for t in qr], [t.astype(BF) for t in kr], vb


def _ret_fwd(z, cst):
    def body(q_ref, k_ref, v_ref, cos_ref, sin_ref, dm_ref, xi_ref, zt_ref, gd_ref, r_ref, sp_ref, st):
        n = pl.program_id(0)

        @pl.when(n == 0)
        def _():
            st[...] = jnp.zeros_like(st)

        hs = range(RH)
        qr, kr, qb, kb, vb = _ret_heads(q_ref, k_ref, v_ref, cos_ref[...], sin_ref[...])
        sd = [(_dg(qb[h], kb[h], NT) * dm_ref[h]).astype(BF) for h in hs]
        state = [st[h] for h in hs]
        qx = [(qr[h] * xi_ref[h]).astype(BF) for h in hs]
        kz = [(kr[h] * zt_ref[h]).astype(BF) for h in hs]
        out = [_dot(sd[h], vb[h]) + _dot(qx[h], state[h].astype(BF)) for h in hs]
        kv = [_dg(kz[h], vb[h], TN) for h in hs]
        for h in hs:
            sp_ref[0, h] = state[h]
            r_ref[:, RDV * h:RDV * (h + 1)] = out[h]
            st[h] = state[h] * gd_ref[h] + kv[h]

    return pl.pallas_call(
        body, name="ret_fwd", grid=(NCH,),
        in_specs=_ret_specs(lambda n: n),
        out_specs=[pl.BlockSpec((C, RH * RDV), lambda n: (n, 0)),
                   pl.BlockSpec((1, RH, RDK, RDV), lambda n: (n, 0, 0, 0))],
        out_shape=[jax.ShapeDtypeStruct((T, RH * RDV), F32), jax.ShapeDtypeStruct((NCH, RH, RDK, RDV), F32)],
        scratch_shapes=[pltpu.VMEM((RH, RDK, RDV), F32)],
        compiler_params=_params(("arbitrary",)),
    )(z, z, z, cst["cos2"], cst["sin2"], cst["dmask"], cst["xi"], cst["zeta"], cst["gdec"])


def _ret_bwd(z, cst, sprev, dr):
    def body(q_ref, k_ref, v_ref, cos_ref, sin_ref, dm_ref, xi_ref, zt_ref, gd_ref, sp_ref, dr_ref,
             dq_ref, dk_ref, dv_ref, gst):
        i = pl.program_id(0)

        @pl.when(i == 0)
        def _():
            gst[...] = jnp.zeros_like(gst)

        hs = range(RH)
        cos, sin = cos_ref[...], sin_ref[...]
        qr, kr, qb, kb, vb = _ret_heads(q_ref, k_ref, v_ref, cos, sin)
        dm = [dm_ref[h] for h in hs]
        xi = [xi_ref[h] for h in hs]
        zt = [zt_ref[h] for h in hs]
        sd = [(_dg(qb[h], kb[h], NT) * dm[h]).astype(BF) for h in hs]
        qx = [(qr[h] * xi[h]).astype(BF) for h in hs]
        kz = [(kr[h] * zt[h]).astype(BF) for h in hs]
        drb = [dr_ref[:, RDV * h:RDV * (h + 1)] for h in hs]
        sb = [sp_ref[0, h].astype(BF) for h in hs]
        g = [gst[h] for h in hs]
        gb = [t.astype(BF) for t in g]
        ds = [(_dg(drb[h], vb[h], NT) * dm[h]).astype(BF) for h in hs]
        dq = [_dot(ds[h], kb[h]) + _dg(drb[h], sb[h], NT) * xi[h] for h in hs]
        dk = [(_dg(ds[h], qb[h], TN) + _dg(vb[h], gb[h], NT) * zt[h]) * RSCALE for h in hs]
        dv = [_dg(sd[h], drb[h], TN) + _dot(kz[h], gb[h]) for h in hs]
        gn = [g[h] * gd_ref[h] + _dg(qx[h], drb[h], TN) for h in hs]
        for h in hs:
            gst[h] = gn[h]
            dq_ref[:, RDK * h:RDK * (h + 1)] = (dq[h] * cos + pltpu.roll(dq[h] * sin, 64, 1)).astype(BF)
            dk_ref[:, RDK * h:RDK * (h + 1)] = (dk[h] * cos + pltpu.roll(dk[h] * sin, 64, 1)).astype(BF)
            dv_ref[:, RDV * h:RDV * (h + 1)] = dv[h].astype(BF)

    rev = lambda n: NCH - 1 - n
    return pl.pallas_call(
        body, name="ret_bwd", grid=(NCH,),
        in_specs=_ret_specs(rev) + [
            pl.BlockSpec((1, RH, RDK, RDV), lambda n: (rev(n), 0, 0, 0)),
            pl.BlockSpec((C, RH * RDV), lambda n: (rev(n), 0)),
        ],
        out_specs=[pl.BlockSpec((C, RH * RDK), lambda n: (rev(n), 0)),
                   pl.BlockSpec((C, RH * RDK), lambda n: (rev(n), 0)),
                   pl.BlockSpec((C, RH * RDV), lambda n: (rev(n), 0))],
        out_shape=[jax.ShapeDtypeStruct((T, RH * RDK), BF), jax.ShapeDtypeStruct((T, RH * RDK), BF),
                   jax.ShapeDtypeStruct((T, RH * RDV), BF)],
        scratch_shapes=[pltpu.VMEM((RH, RDK, RDV), F32)],
        compiler_params=_params(("arbitrary",)),
    )(z, z, z, cst["cos2"], cst["sin2"], cst["dmask"], cst["xi"], cst["zeta"], cst["gdec"], sprev, dr)


def _place():
    x, y, c = lax.axis_index("x"), lax.axis_index("y"), lax.axis_index("c")
    return x, y, c


def _other_chips(x, y):
    return [(1 - x, y, 2 * (1 - x) + y), (x, 1 - y, 2 * x + (1 - y)), (1 - x, 1 - y, 2 * (1 - x) + (1 - y))]


def _chip_copies(srcs, lands, send_sems, recv_sems, by_dest):
    x, y, c = _place()
    me_s = 2 * x + y
    return [pltpu.make_async_remote_copy(
        src_ref=src.at[cs] if by_dest else src, dst_ref=land.at[me_s],
        send_sem=send_sems.at[3 * a + j], recv_sem=recv_sems.at[3 * a + j],
        device_id=(cx, cy, c), device_id_type=MESH)
        for a, (src, land) in enumerate(zip(srcs, lands)) for j, (cx, cy, cs) in enumerate(_other_chips(x, y))]


def _split_dot(x, mat01, dims=NN_DIMS, x_first=True):
    acc, rest = None, x
    for _ in range(3):
        piece = rest.astype(BF)
        part = _dg(piece, mat01, dims) if x_first else _dg(mat01, piece, dims)
        acc = part if acc is None else acc + part
        rest = rest - piece.astype(F32)
    return acc


def _log_sigmoid(x):
    return -(jnp.maximum(-x, 0.0) + jnp.log1p(jnp.exp(-jnp.abs(x))))


def _fox_prep(zf, bf_pad, cst):
    def body(zf_ref, b_ref, tri_ref, ct_ref, carry):
        n = pl.program_id(0)

        @pl.when(n == 0)
        def _():
            carry[...] = jnp.zeros_like(carry)

        ls = _log_sigmoid(zf_ref[...] + b_ref[...])
        row = n * C + lax.broadcasted_iota(jnp.int32, (C, C), 0)
        lf = jnp.where(row >= PAD, ls, 0.0)
        cc = _split_dot(lf, tri_ref[...], x_first=False) + carry[0:1, :]
        carry[...] = jnp.broadcast_to(cc[C - 1:C, :], carry.shape)
        pos = n * C + lax.broadcasted_iota(jnp.int32, (FH, C), 1)
        ct_ref[0] = jnp.where(pos >= PAD, cc.T[:FH, :], -NEG)

    return pl.pallas_call(
        body, name="fox_prep", grid=(NCH,),
        in_specs=[pl.BlockSpec((C, C), lambda n: (n, 0)), pl.BlockSpec((1, C), lambda n: (0, 0)),
                  pl.BlockSpec((C, C), lambda n: (0, 0))],
        out_specs=pl.BlockSpec((1, FH, C), lambda n: (n, 0, 0)),
        out_shape=jax.ShapeDtypeStruct((NCH, FH, C), F32),
        scratch_shapes=[pltpu.VMEM((8, C), F32)],
        compiler_params=_params(("arbitrary",)),
    )(zf, bf_pad, cst["tri"])


def _lo_lanes(shape):
    return lax.broadcasted_iota(jnp.int32, shape, 1) < FD


def _split_heads(x):
    lo = _lo_lanes(x.shape)
    zero = jnp.zeros_like(x)
    return jnp.concatenate([jnp.where(lo, x, zero), jnp.where(lo, zero, x)], axis=0)


def _spread2(x):
    lo = _lo_lanes(x.shape)
    r = pltpu.roll(x, FD, 1)
    return jnp.concatenate([jnp.where(lo, x, r), jnp.where(lo, r, x)], axis=1)


NSTEP = (NCH + 1) // 2
NTILE = NCH + 1
TROWS = T + C


def _fox_tile(s, t):
    second = t > s
    return second.astype(jnp.int32), jnp.where(second, t - s - 1, s - t)


def _fox_pos(i):
    return jnp.where(i < NSTEP, 2 * i, 2 * (NCH - 1 - i) + 1)


FOX_ORDER = [2 * i if i < NSTEP else 2 * (NCH - 1 - i) + 1 for i in range(NCH)]


def _fox_pair_specs():
    first = pl.BlockSpec((C, C), lambda p, s: (2 * s, p))
    second = pl.BlockSpec((C, C), lambda p, s: (jnp.where(s == NSTEP - 1, 2 * s, 2 * s + 1), p))
    both = pl.BlockSpec((2 * C, C), lambda p, s: (s, p))
    return first, second, both


def _fox_q_specs():
    return (pl.BlockSpec((C, C), lambda p, s: (s, QB_F + p)),
            pl.BlockSpec((C, C), lambda p, s: (NCH - 1 - s, QB_F + p)))


def _fox_key_bias(ct_ref, p, j):
    return jnp.concatenate([ct_ref[j, pl.ds(2 * p, 1), :], ct_ref[j, pl.ds(2 * p + 1, 1), :]], axis=1)


def _fox_kv(z_ref, kbuf, vbuf, sems, p):
    def copies(pair, slot):
        return [pltpu.make_async_copy(z_ref.at[:, pl.ds(pl.multiple_of((first + pair) * C, C), C)], buf.at[slot],
                                      sems.at[i, slot]) for i, (first, buf) in enumerate(((KB_F, kbuf), (VB_F, vbuf)))]

    slot = p % 2

    @pl.when(p == 0)
    def _():
        for cp in copies(0, 0):
            cp.start()

    for cp in copies(p, slot):
        cp.wait()

    @pl.when(p + 1 < NPAIR)
    def _():
        for cp in copies(p + 1, 1 - slot):
            cp.start()

    return slot


def _fox_fwd(z, ct, cst, share):
    n = 0 if share is None else 1

    def body(qa_ref, qb_ref, z_ref, ct_ref, ones_ref, mb_ref, *rest):
        share_refs, (a_ref, g_ref), land_refs = rest[:n], rest[n:n + 2], rest[n + 2:2 * n + 2]
        kks, vvs, q2, m2, sbuf, kbuf, vbuf, kv_sems = rest[2 * n + 2:2 * n + 10]
        p, s = pl.program_id(0), pl.program_id(1)
        if n:
            copies = _chip_copies(share_refs, land_refs, *rest[2 * n + 10:], by_dest=False)

            @pl.when((p == 0) & (s == 0))
            def _():
                for cp in copies:
                    cp.start()

            @pl.when((p == NPAIR - 1) & (s == NSTEP - 1))
            def _():
                for cp in copies:
                    cp.wait()

        @pl.when(s == 0)
        def _():
            ones = ones_ref[...]
            slot = _fox_kv(z_ref, kbuf, vbuf, kv_sems, p)

            def prep(j, carry):
                rows = pl.ds(pl.multiple_of(j * C, C), C)
                kks[j] = _split_heads(kbuf[slot, rows, :]).astype(BF)
                vvs[j] = jnp.concatenate([_split_heads(vbuf[slot, rows, :]).astype(BF), ones], axis=1)
                return carry

            lax.fori_loop(0, NCH, prep, 0)

        q2[0] = (qa_ref[...] * FSCALE).astype(BF)
        q2[1] = (qb_ref[...] * FSCALE).astype(BF)

        tiles = [_fox_tile(s, t) for t in range(NTILE)]
        causal = mb_ref[1]
        neg = jnp.full((C, 2 * C), NEG, F32)
        run, first = neg, neg
        for t, (sel, j) in enumerate(tiles):
            st = _dg(q2[sel], kks[j], NT) - _fox_key_bias(ct_ref, p, j)
            if t in (0, NTILE - 1):
                st = st + causal
            sbuf[t] = st
            run = jnp.maximum(jnp.where(t == s + 1, neg, run), st)
            first = jnp.where(t == s, run, first)
        for w, mx in enumerate((first, run)):
            m2[w] = jnp.concatenate(
                [jnp.broadcast_to(jnp.max(mx[:, :C], axis=1, keepdims=True), (C, C)),
                 jnp.broadcast_to(jnp.max(mx[:, C:], axis=1, keepdims=True), (C, C))], axis=1)

        zero = jnp.zeros((C, 2 * C), F32)
        run, first = zero, zero
        for t, (sel, j) in enumerate(tiles):
            run = jnp.where(t == s + 1, zero, run) + _dot(jnp.exp(sbuf[t] - m2[sel]).astype(BF), vvs[j])
            first = jnp.where(t == s, run, first)
        lo = _lo_lanes((C, C))
        for w, res in enumerate((first, run)):
            l = res[:, C:]
            a_ref[C * w:C * (w + 1), :] = res[:, :C] / l
            mw = m2[w]
            g_ref[C * w:C * (w + 1), :] = -(jnp.where(lo, mw[:, :C], mw[:, C:]) + jnp.log(l))

    qa, qb = _fox_q_specs()
    both = _fox_pair_specs()[2]
    return pl.pallas_call(
        body, name="fox_fwd", grid=(NPAIR, NSTEP),
        in_specs=[qa, qb, ANY,
                  pl.BlockSpec((NCH, FH, C), lambda p, s: (0, 0, 0)),
                  pl.BlockSpec((2 * C, C), lambda p, s: (0, 0)),
                  pl.BlockSpec((2, C, 2 * C), lambda p, s: (0, 0, 0))] + [ANY] * n,
        out_specs=[both, both] + [ANY] * n,
        out_shape=[jax.ShapeDtypeStruct((TROWS, FH * FD), F32)] * 2
        + ([jax.ShapeDtypeStruct((4,) + share.shape, share.dtype)] if n else []),
        scratch_shapes=[pltpu.VMEM((NCH, 2 * C, C), BF), pltpu.VMEM((NCH, 2 * C, 2 * C), BF),
                        pltpu.VMEM((2, C, C), BF), pltpu.VMEM((2, C, 2 * C), F32),
                        pltpu.VMEM((NTILE, C, 2 * C), F32),
                        pltpu.VMEM((2, T, C), F32), pltpu.VMEM((2, T, C), F32), pltpu.SemaphoreType.DMA((2, 2))]
        + [pltpu.SemaphoreType.DMA((3,)), pltpu.SemaphoreType.DMA((3,))] * n,
        compiler_params=_params(("arbitrary", "arbitrary")),
    )(z, z, z, ct, cst["ones_aug"], cst["mask_bias"], *([share] * n))


def _fox_bwd(z, da, g, delta, ct, cst):
    grp = 9

    def body(qa_ref, qb_ref, daa_ref, dab_ref, ga_ref, gb_ref, dla_ref, dlb_ref, z_ref, ct_ref, ones_ref,
             mb_ref, dq_ref, dr_ref, dk_ref, dv_ref, dcs_ref,
             kks, vvs, q2, qq2, dd2, da2, gi2, dl2, dq2, dvb, dkb, dkacc, dvacc, csacc, kbuf, vbuf, kv_sems):
        p, s = pl.program_id(0), pl.program_id(1)
        ones = ones_ref[...]

        @pl.when(s == 0)
        def _():
            dkacc[...] = jnp.zeros_like(dkacc)
            dvacc[...] = jnp.zeros_like(dvacc)
            csacc[...] = jnp.zeros_like(csacc)
            slot = _fox_kv(z_ref, kbuf, vbuf, kv_sems, p)

            def prep(j, carry):
                rows = pl.ds(pl.multiple_of(j * C, C), C)
                kks[j] = _split_heads(kbuf[slot, rows, :]).astype(BF)
                vvs[j] = _split_heads(vbuf[slot, rows, :]).astype(BF)
                return carry

            lax.fori_loop(0, NCH, prep, 0)

        for w, (q_ref, d_ref, g_ref, l_ref) in enumerate(((qa_ref, daa_ref, ga_ref, dla_ref),
                                                          (qb_ref, dab_ref, gb_ref, dlb_ref))):
            qf = q_ref[...]
            q2[w] = (qf * FSCALE).astype(BF)
            qq2[w] = jnp.concatenate([_split_heads(qf).astype(BF), ones], axis=1)
            da2[w] = d_ref[...]
            dd2[w] = _split_heads(d_ref[...].astype(F32)).astype(BF)
            gi2[w] = _spread2(g_ref[...])
            dl2[w] = _spread2(l_ref[...])
        dq2[...] = jnp.zeros_like(dq2)
        zero = jnp.zeros((C, 2 * C), F32)

        def group(gi, carry):
            ts = [gi * grp + u for u in range(grp)]
            tiles = [_fox_tile(s, t) for t in ts]
            kk = [kks[j] for _, j in tiles]
            ss = [_dg(q2[sel], kj, NT) + (gi2[sel] - _fox_key_bias(ct_ref, p, j)) for kj, (sel, j) in zip(kk, tiles)]
            ss[0] = ss[0] + mb_ref[(gi == 0).astype(jnp.int32)]
            ss[-1] = ss[-1] + mb_ref[(gi == 1).astype(jnp.int32)]
            dps = [_dg(da2[sel], vvs[j], NT) for sel, j in tiles]
            pes = [jnp.exp(st) for st in ss]
            dss = [pe * (dp - dl2[sel]) * FSCALE for pe, dp, (sel, _) in zip(pes, dps, tiles)]
            pts = [jnp.concatenate([pe[:, :C].T, pe[:, C:].T], axis=1).astype(BF) for pe in pes]
            dsts = [jnp.concatenate([ds[:, :C].T, ds[:, C:].T], axis=1).astype(BF) for ds in dss]
            dvs = [_dot(pt, dd2[sel]) for pt, (sel, _) in zip(pts, tiles)]
            rs = [_dot(dst, qq2[sel]) for dst, (sel, _) in zip(dsts, tiles)]
            parts = [_dot(ds.astype(BF), jnp.concatenate([kj, ones], axis=1)) for ds, kj in zip(dss, kk)]
            for t, dv, rr in zip(ts, dvs, rs):
                dvb[t] = dv
                dkb[t] = rr
            pa, pb = zero, zero
            for t, part in zip(ts, parts):
                pa = pa + jnp.where(t <= s, part, zero)
                pb = pb + jnp.where(t <= s, zero, part)
            dq2[0] += pa
            dq2[1] += pb
            return carry

        ntile = jnp.where(s == NSTEP - 1, grp, NTILE)
        lax.fori_loop(0, ntile // grp, group, 0)

        def scatter(t, carry):
            _, j = _fox_tile(s, t)
            r = pl.ds(pl.multiple_of(j * C, C), C)
            dvacc[r, :] += dvb[t]
            dkacc[r, :] += dkb[t, :, :C]
            csacc[r, :] += dkb[t, :, C:]
            return carry

        lax.fori_loop(0, ntile, scatter, 0)
        for w in range(2):
            res = dq2[w]
            dq_ref[C * w:C * (w + 1), :] = res[:, :C].astype(BF)
            dr_ref[C * w:C * (w + 1), :] = res[:, C:]

        @pl.when(s == NSTEP - 1)
        def _():
            dk_ref[...] = dkacc[...].astype(BF)
            dv_ref[...] = dvacc[...].astype(BF)
            dcs_ref[...] = csacc[...]

    qa, qb = _fox_q_specs()
    ba, bb, both = _fox_pair_specs()
    col = pl.BlockSpec((T, C), lambda p, s: (0, p))
    return pl.pallas_call(
        body, name="fox_bwd", grid=(NPAIR, NSTEP),
        in_specs=[qa, qb, ba, bb, ba, bb, ba, bb, ANY,
                  pl.BlockSpec((NCH, FH, C), lambda p, s: (0, 0, 0)),
                  pl.BlockSpec((2 * C, C), lambda p, s: (0, 0)),
                  pl.BlockSpec((2, C, 2 * C), lambda p, s: (0, 0, 0))],
        out_specs=[both, both, col, col, col],
        out_shape=[jax.ShapeDtypeStruct((TROWS, FH * FD), BF), jax.ShapeDtypeStruct((TROWS, FH * FD), F32),
                   jax.ShapeDtypeStruct((T, FH * FD), BF), jax.ShapeDtypeStruct((T, FH * FD), BF),
                   jax.ShapeDtypeStruct((T, FH * FD), F32)],
        scratch_shapes=[pltpu.VMEM((NCH, 2 * C, C), BF), pltpu.VMEM((NCH, 2 * C, C), BF),
                        pltpu.VMEM((2, C, C), BF), pltpu.VMEM((2, 2 * C, 2 * C), BF), pltpu.VMEM((2, 2 * C, C), BF),
                        pltpu.VMEM((2, C, C), BF), pltpu.VMEM((2, C, 2 * C), F32), pltpu.VMEM((2, C, 2 * C), F32),
                        pltpu.VMEM((2, C, 2 * C), F32),
                        pltpu.VMEM((NTILE, C, C), F32), pltpu.VMEM((NTILE, C, 2 * C), F32),
                        pltpu.VMEM((T, C), F32), pltpu.VMEM((T, C), F32), pltpu.VMEM((T, C), F32),
                        pltpu.VMEM((2, T, C), F32), pltpu.VMEM((2, T, C), F32), pltpu.SemaphoreType.DMA((2, 2))],
        compiler_params=_params(("arbitrary", "arbitrary")),
    )(z, z, da, da, g, g, delta, delta, z, ct, cst["ones_aug"], cst["mask_bias"])


def _fox_gate_bwd(drow, dcol, zf, bf_pad, cst):
    def body(dr_ref, dc_ref, zf_ref, b_ref, tri_ref, pick_ref, dff_ref, db_ref, carry):
        s = pl.program_id(0)
        n = NCH - 1 - s

        @pl.when(s == 0)
        def _():
            carry[...] = jnp.zeros_like(carry)
            db_ref[...] = jnp.zeros_like(db_ref)

        dcb = _split_dot((dr_ref[...] - dc_ref[...]) * (1.0 / FSCALE), pick_ref[...])
        suf = _split_dot(dcb, tri_ref[...], TN, x_first=False) + carry[0:1, :]
        carry[...] = jnp.broadcast_to(suf[0:1, :], carry.shape)
        x = zf_ref[...] + b_ref[...]
        row = n * C + lax.broadcasted_iota(jnp.int32, (C, C), 0)
        dff = jnp.where(row >= PAD, suf * (1.0 - jax.nn.sigmoid(x)), 0.0)
        dff_ref[...] = dff.astype(BF)
        db_ref[...] += jnp.sum(dff, axis=0, keepdims=True)

    rev = lambda s: (NCH - 1 - s, 0)
    return pl.pallas_call(
        body, name="fox_gate_bwd", grid=(NCH,),
        in_specs=[pl.BlockSpec((C, FH * FD), lambda s: (_fox_pos(NCH - 1 - s), 0)),
                  pl.BlockSpec((C, FH * FD), rev), pl.BlockSpec((C, C), rev),
                  pl.BlockSpec((1, C), lambda s: (0, 0)), pl.BlockSpec((C, C), lambda s: (0, 0)),
                  pl.BlockSpec((FH * FD, C), lambda s: (0, 0))],
        out_specs=[pl.BlockSpec((C, C), rev), pl.BlockSpec((1, C), lambda s: (0, 0))],
        out_shape=[jax.ShapeDtypeStruct((T, C), BF), jax.ShapeDtypeStruct((1, C), F32)],
        scratch_shapes=[pltpu.VMEM((8, C), F32)],
        compiler_params=_params(("arbitrary",)),
    )(drow, dcol, zf, bf_pad, cst["tri"], cst["pick"])


def _head_norm(r):
    rn, rs = [], []
    for h in range(RH):
        rh = r[:, RDV * h:RDV * (h + 1)]
        s = lax.rsqrt(jnp.mean(rh * rh, axis=1, keepdims=True) + EPS)
        rn.append(rh * s)
        rs.append(s)
    return jnp.concatenate(rn, axis=1), rs


def _gated(r, rg, a, fg):
    rn, _ = _head_norm(r)
    return jnp.concatenate([rn * (rg * jax.nn.sigmoid(rg)), a * (fg * jax.nn.sigmoid(fg))], axis=1)


def _out_loss(r, z, a, wout, x, tgt, fgain):
    def body(r_ref, rg_ref, a_ref, fg_ref, w_ref, x_ref, t_ref, g_ref, yt_ref, do_ref, dob_ref, loss_ref, dg_ref):
        i = pl.program_id(0)

        @pl.when(i == 0)
        def _():
            yt_ref[...] = jnp.zeros_like(yt_ref)
            do_ref[...] = jnp.zeros_like(do_ref)
            dob_ref[...] = jnp.zeros_like(dob_ref)
            loss_ref[...] = jnp.zeros_like(loss_ref)
            dg_ref[...] = jnp.zeros_like(dg_ref)

        @pl.when(i > 0)
        def _():
            y = _gated(r_ref[...], rg_ref[...], a_ref[...], fg_ref[...])
            yt_ref[...] = y.T.astype(BF)
            o = x_ref[...] + _dot(y.astype(BF), w_ref[...])
            rs = lax.rsqrt(jnp.mean(o * o, axis=1, keepdims=True) + EPS)
            on = o * rs
            g = g_ref[...]
            e = on * g - t_ref[...]
            loss_ref[...] += 0.5 * jnp.sum(jnp.mean(e * e, axis=1, keepdims=True))
            dyh = e * (1.0 / D)
            dg_ref[...] += jnp.sum(dyh * on, axis=0, keepdims=True)
            don = dyh * g
            do = rs * (don - on * jnp.mean(don * on, axis=1, keepdims=True))
            do_ref[...] = do
            dob_ref[...] = do.astype(BF)

    tok = lambda i: (jnp.maximum(i - 1, 0), 0)
    return pl.pallas_call(
        body, name="out_loss", grid=(NCH,),
        in_specs=[pl.BlockSpec((C, D), lambda i: (i, 0)), pl.BlockSpec((C, D), lambda i: (i, GB_R)),
                  pl.BlockSpec((C, D), lambda i: (_fox_pos(i), 0)), pl.BlockSpec((C, D), lambda i: (i, GB_F)),
                  pl.BlockSpec((DMIX, D), lambda i: (0, 0)),
                  pl.BlockSpec((C, D), tok), pl.BlockSpec((C, D), tok), pl.BlockSpec((1, D), lambda i: (0, 0))],
        out_specs=[pl.BlockSpec((DMIX, C), lambda i: (0, i)), pl.BlockSpec((C, D), lambda i: (i, 0)),
                   pl.BlockSpec((C, D), lambda i: (i, 0)), pl.BlockSpec((8, C), lambda i: (0, 0)),
                   pl.BlockSpec((1, D), lambda i: (0, 0))],
        out_shape=[jax.ShapeDtypeStruct((DMIX, T), BF), jax.ShapeDtypeStruct((T, D), F32),
                   jax.ShapeDtypeStruct((T, D), BF), jax.ShapeDtypeStruct((8, C), F32),
                   jax.ShapeDtypeStruct((1, D), F32)],
        compiler_params=_params(("arbitrary",)),
    )(r, z, a, z, wout, x, tgt, fgain)


def _silu_and_grad(x):
    s = jax.nn.sigmoid(x)
    return x * s, s * (1.0 + x * (1.0 - s))


def _dy_gate_bwd(dob, wout, r, z, a, seg):
    def body(do_ref, w_ref, r_ref, rg_ref, a_ref, fg_ref, seg_ref, dr_ref, da_ref, drg_ref, dfg_ref, dl_ref):
        dy = _dg(do_ref[...], w_ref[...], NT)
        a_ = a_ref[...]
        rn, rs = _head_norm(r_ref[...])
        silu_rg, dsilu_rg = _silu_and_grad(rg_ref[...])
        silu_fg, dsilu_fg = _silu_and_grad(fg_ref[...])
        dyr, dyf = dy[:, :D], dy[:, D:]
        drn = dyr * silu_rg
        drg_ref[...] = (dyr * rn * dsilu_rg).astype(BF)
        for h in range(RH):
            sl = slice(RDV * h, RDV * (h + 1))
            dh, nh = drn[:, sl], rn[:, sl]
            dr_ref[:, sl] = (rs[h] * (dh - nh * jnp.mean(dh * nh, axis=1, keepdims=True))).astype(BF)
        dab = (dyf * silu_fg).astype(BF)
        da_ref[...] = dab
        dfg_ref[...] = (dyf * a_ * dsilu_fg).astype(BF)
        prod = dab.astype(F32) * a_
        segm = seg_ref[...]
        for p in range(NPAIR):
            sl = slice(C * p, C * (p + 1))
            hi = prod[:, sl].astype(BF)
            lo = (prod[:, sl] - hi.astype(F32)).astype(BF)
            dl_ref[:, sl] = _dot(hi, segm) + _dot(lo, segm)

    row = pl.BlockSpec((C, D), lambda i: (i, 0))
    fox = pl.BlockSpec((C, D), lambda i: (_fox_pos(i), 0))
    return pl.pallas_call(
        body, name="dy_gate_bwd", grid=(NCH,),
        in_specs=[row, pl.BlockSpec((DMIX, D), lambda i: (0, 0)),
                  row, pl.BlockSpec((C, D), lambda i: (i, GB_R)),
                  fox, pl.BlockSpec((C, D), lambda i: (i, GB_F)),
                  pl.BlockSpec((C, C), lambda i: (0, 0))],
        out_specs=[row, fox, row, row, fox],
        out_shape=[jax.ShapeDtypeStruct((T, D), BF), jax.ShapeDtypeStruct((TROWS, D), BF),
                   jax.ShapeDtypeStruct((T, D), BF), jax.ShapeDtypeStruct((T, D), BF),
                   jax.ShapeDtypeStruct((TROWS, D), F32)],
        compiler_params=_params(("parallel",)),
    )(dob, wout, r, z, a, z, seg)


DZ_WIDTHS = (512, 512, 1024, 1024, 1024, 1024, 1024, 1024)


def _du_norm_bwd(dzs, dzf, wt, wft, hpad, g, dopad, parts=()):
    tm, tk = 544, 1024
    nk = WMAIN // tk
    ni = T // tm
    n = len(parts)

    def body(rq_ref, rk_ref, rv_ref, rg_ref, fq_ref, fk_ref, fv_ref, fg_ref, dzf_ref, w_ref, wf_ref, h_ref, g_ref,
             do_ref, *rest):
        part_refs, (gh_ref, dg_ref), land_refs = rest[:n], rest[n:n + 2], rest[n + 2:2 * n + 2]
        acc = rest[2 * n + 2]
        i, k = pl.program_id(0), pl.program_id(1)

        if n:
            send_sems, recv_sems = rest[2 * n + 3:]
            copies = _chip_copies(part_refs, land_refs, send_sems, recv_sems, by_dest=True)

            @pl.when((i == 0) & (k == 0))
            def _():
                for cp in copies:
                    cp.start()

            @pl.when((i == ni - 1) & (k == nk - 1))
            def _():
                for cp in copies:
                    cp.wait()

        @pl.when(k == 0)
        def _():
            acc[...] = (_dot(dzf_ref[...], wf_ref[...]) + _dot(rq_ref[...], w_ref[:512, :])
                        + _dot(rk_ref[...], w_ref[512:, :]))

        for kk, piece in enumerate((rv_ref, rg_ref, fq_ref, fk_ref, fv_ref, fg_ref), start=1):
            @pl.when(k == kk)
            def _(piece=piece):
                acc[...] += _dot(piece[...], w_ref[...])

        @pl.when(k == nk - 1)
        def _():
            du = acc[...]
            h = h_ref[...]
            gg = g_ref[...]
            rs = lax.rsqrt(jnp.mean(h * h, axis=1, keepdims=True) + EPS)
            hn = h * rs
            part = jnp.sum(du * hn, axis=0, keepdims=True)

            @pl.when(i == 0)
            def _():
                dg_ref[...] = part

            @pl.when(i > 0)
            def _():
                dg_ref[...] += part

            dhn = du * gg
            gh_ref[...] = rs * (dhn - hn * jnp.mean(dhn * hn, axis=1, keepdims=True)) + do_ref[...]

    sems = [pltpu.SemaphoreType.DMA((3 * n,)), pltpu.SemaphoreType.DMA((3 * n,))] if n else []
    return pl.pallas_call(
        body, name="du_norm_bwd", grid=(ni, nk),
        in_specs=[pl.BlockSpec((tm, w), lambda i, k: (i, 0)) for w in DZ_WIDTHS]
        + [pl.BlockSpec((tm, C), lambda i, k: (i, 0)),
           pl.BlockSpec((tk, D), lambda i, k: (k, 0)), pl.BlockSpec((C, D), lambda i, k: (0, 0)),
           pl.BlockSpec((tm, D), lambda i, k: (i, 0)), pl.BlockSpec((1, D), lambda i, k: (0, 0)),
           pl.BlockSpec((tm, D), lambda i, k: (i, 0))] + [ANY] * n,
        out_specs=[pl.BlockSpec((tm, D), lambda i, k: (i, 0)), pl.BlockSpec((1, D), lambda i, k: (0, 0))] + [ANY] * n,
        out_shape=[jax.ShapeDtypeStruct((T, D), F32), jax.ShapeDtypeStruct((1, D), F32)]
        + [jax.ShapeDtypeStruct(p.shape, p.dtype) for p in parts],
        scratch_shapes=[pltpu.VMEM((tm, D), F32)] + sems,
        compiler_params=_params(("arbitrary", "arbitrary")),
    )(*dzs, dzf, wt, wft, hpad, g, dopad, *parts)


GROWS = 7680


def _dw_in(dzs, dzf, ut):
    tn = 512
    nmain = WMAIN // tn
    first, blocks = [], []
    for w in DZ_WIDTHS:
        first.append(sum(blocks))
        blocks.append(w // tn)

    def body(rq_ref, rk_ref, rv_ref, rg_ref, fq_ref, fk_ref, fv_ref, fg_ref, dzf_ref, ut_ref, o_ref):
        gidx = pl.program_id(0)
        for piece, g0, nb in zip((rq_ref, rk_ref, rv_ref, rg_ref, fq_ref, fk_ref, fv_ref, fg_ref), first, blocks):
            @pl.when((gidx >= g0) & (gidx < g0 + nb))
            def _(piece=piece):
                o_ref[...] = _dot(ut_ref[...], piece[...]).T.astype(BF)

        @pl.when(gidx == nmain)
        def _():
            o_ref[:C, :] = _dot(ut_ref[...], dzf_ref[...]).T.astype(BF)
            o_ref[C:, :] = jnp.zeros((tn - C, D), BF)

    def piece_spec(g0, nb):
        return pl.BlockSpec((T, tn), lambda gidx: (0, jnp.clip(gidx - g0, 0, nb - 1)))

    return pl.pallas_call(
        body, name="dw_in", grid=(nmain + 1,),
        in_specs=[piece_spec(g0, nb) for g0, nb in zip(first, blocks)]
        + [pl.BlockSpec((T, C), lambda gidx: (0, 0)), pl.BlockSpec((D, T), lambda gidx: (0, 0))],
        out_specs=pl.BlockSpec((tn, D), lambda gidx: (gidx, 0)),
        out_shape=jax.ShapeDtypeStruct((GROWS, D), BF),
        compiler_params=pltpu.CompilerParams(dimension_semantics=("arbitrary",), vmem_limit_bytes=DW_VMEM_LIMIT),
    )(*dzs, dzf, ut)


def _token_order(x_po):
    def body(i_ref, o_ref):
        o_ref[...] = i_ref[...]

    return pl.pallas_call(
        body, name="token_order", grid=(NCH,),
        in_specs=[pl.BlockSpec((C, D), lambda i: (_fox_pos(i), 0))],
        out_specs=pl.BlockSpec((C, D), lambda i: (i, 0)),
        out_shape=jax.ShapeDtypeStruct((T, D), x_po.dtype),
        compiler_params=_params(("parallel",)),
    )(x_po)


def _local_step(x, tgt, normed, norm_g, wt, wft, b_f, wout, final_g, chip_sums=None, wout_full=None):
    cst = _constants()
    hpad, u, ut = normed
    bf_pad = jnp.pad(b_f, ((0, 0), (0, C - NFF)))
    z = _mm_nt(u, wt, WMAIN, T // 2, 1024, "in_proj")
    zf = _mm_nt(u, wft, C, T // 2, C, "in_proj_ff")
    r, sprev = _ret_fwd(z, cst)
    ct = _fox_prep(zf, bf_pad, cst)
    if wout_full is None:
        a, g = _fox_fwd(z, ct, cst, None)
    else:
        a, g, landed_wout = _fox_fwd(z, ct, cst, wout)
        wout = wout_full(landed_wout)
    yt, dopad, dob, loss8, dfg = _out_loss(r, z, a, wout, x, tgt, final_g)
    dr, da, dzrg, dzfg, delta = _dy_gate_bwd(dob, wout, r, z, a, cst["seg"])
    dwout = _mm_nn(yt, dob, 512, D, "dw_out", BF)
    dzq_r, dzk_r, dzv_r = _ret_bwd(z, cst, sprev, dr)
    dq_po, drow, dzk_f, dzv_f, dcol = _fox_bwd(z, da, g, delta, ct, cst)
    dzf, dbf = _fox_gate_bwd(drow, dcol, zf, bf_pad, cst)
    dzs = [dzq_r, dzk_r, dzv_r, dzrg, _token_order(dq_po), dzk_f, dzv_f, dzfg]
    gwt = _dw_in(dzs, dzf, ut)
    parts = chip_sums(gwt, dwout) if chip_sums else []
    gh, dng, *landed = _du_norm_bwd(dzs, dzf, wt, wft, hpad, norm_g, dopad, parts)
    return (loss8[0, 0], gh[C:], gh[PAD:C], dng, gwt, dbf[:, :NFF], dwout, dfg, parts, landed)


WOFF, WLEN = 1792, 2048
WHALF = WLEN // 2
LAP = WPADROWS - WOFF


def _own_window(w3):
    rows, sub, lanes = w3.shape
    pad = WPADROWS - rows
    tb = 96
    nb = WPADROWS // tb
    half = rows // 2

    def body(w_ref, o_ref, buf, sems):
        x, y, _ = _place()
        shift = 4 * (2 * x + y)
        buf[pl.ds(0, pad)] = jnp.zeros((pad, sub, lanes), F32)
        buf[pl.ds(rows, pad)] = jnp.zeros((pad, sub, lanes), F32)
        cps = [pltpu.make_async_copy(w_ref.at[pl.ds(half * h, half)], buf.at[pl.ds(shift + half * h, half)],
                                     sems.at[h]) for h in range(2)]
        for cp in cps:
            cp.start()

        def block(i, carry):
            r0 = pl.multiple_of(i * tb, tb)
            o_ref[pl.ds(r0, tb), :] = buf[pl.ds(r0, tb)].reshape(tb, sub * lanes).astype(BF)
            return carry

        cps[0].wait()
        lax.fori_loop(0, half // tb, block, 0)
        cps[1].wait()
        lax.fori_loop(half // tb, nb, block, 0)

    return pl.pallas_call(
        body, name="own_window",
        in_specs=[ANY], out_shape=jax.ShapeDtypeStruct((WPADROWS, sub * lanes), BF),
        scratch_shapes=[pltpu.VMEM((WPADROWS, sub, lanes), F32), pltpu.SemaphoreType.DMA((2,))],
        compiler_params=pltpu.CompilerParams(vmem_limit_bytes=VMEM_LIMIT),
    )(w3)


def _gather_weights(own_win, meta, x, norm_g):
    half_main, half_lap, half_meta = WOFF // 2, LAP // 2, meta.shape[0] // 2
    last = NCH - 1

    def body(win_ref, meta_ref, x_ref, g_ref, w_ref, laps_ref, gm_ref, h_ref, u_ref, ut_ref,
             send_sems, recv_sems, local_sems, stage, lapbuf, headbuf, metabuf):
        step = pl.program_id(0)
        x, y, c = _place()
        me_s = 2 * x + y
        sib = (x, y, 1 - c)
        chips = _other_chips(x, y)

        def emit(h):
            u = _norm_rows(h, g_ref[...])
            h_ref[...] = h
            u_ref[...] = u.astype(BF)
            ut_ref[...] = u.T.astype(BF)

        kinds = [
            (lambda h: win_ref.at[pl.ds(half_main * h, half_main)],
             lambda s, h: w_ref.at[pl.ds(WOFF * s + half_main * h, half_main)]),
            (lambda h: win_ref.at[pl.ds(WOFF + half_lap * h, half_lap)],
             lambda s, h: laps_ref.at[s, pl.ds(half_lap * h, half_lap)]),
            (lambda h: meta_ref.at[pl.ds(half_meta * h, half_meta)],
             lambda s, h: gm_ref.at[s, pl.ds(half_meta * h, half_meta)]),
        ]
        own_in = pltpu.make_async_copy(win_ref.at[pl.ds(0, WOFF)], stage, local_sems.at[0])
        own_lap_in = pltpu.make_async_copy(win_ref.at[pl.ds(WOFF, LAP)], lapbuf.at[0], local_sems.at[1])
        own_out = pltpu.make_async_copy(stage, w_ref.at[pl.ds(WOFF * me_s, WOFF)], local_sems.at[0])
        own_lap_out = pltpu.make_async_copy(lapbuf.at[0], laps_ref.at[me_s], local_sems.at[1])
        sends, arrivals, forwards, forwarded = [], [], [], []
        for a, (src, dst) in enumerate(kinds):
            for k, (cx, cy, cs) in enumerate(chips):
                there = dict(send_sem=send_sems.at[6 * a + k], recv_sem=recv_sems.at[6 * a + k],
                             device_id=(cx, cy, c), device_id_type=MESH)
                across = dict(send_sem=send_sems.at[6 * a + 3 + k], recv_sem=recv_sems.at[6 * a + 3 + k],
                              device_id=sib, device_id_type=MESH)
                sends.append(pltpu.make_async_remote_copy(src_ref=src(c), dst_ref=dst(me_s, c), **there))
                arrivals.append(pltpu.make_async_remote_copy(src_ref=dst(cs, c), dst_ref=dst(cs, c), **there))
                forwards.append(pltpu.make_async_remote_copy(src_ref=dst(cs, c), dst_ref=dst(cs, c), **across))
                forwarded.append(pltpu.make_async_remote_copy(
                    src_ref=dst(cs, 1 - c), dst_ref=dst(cs, 1 - c), **across))

        @pl.when(step == 0)
        def _():
            own_in.start()
            own_lap_in.start()
            for cp in sends:
                cp.start()
            own_in.wait()
            own_out.start()
            own_lap_in.wait()
            own_lap_out.start()

        @pl.when(step < last)
        def _():
            emit(x_ref[...])

        @pl.when(step == last)
        def _():
            for cp, fwd in zip(arrivals, forwards):
                cp.wait_recv()
                fwd.start()
            for cp in forwarded:
                cp.wait_recv()
            for cp in sends + forwards:
                cp.wait_send()
            own_out.wait()
            own_lap_out.wait()
            for s in range(1, 4):
                head = w_ref.at[pl.ds(WOFF * s, LAP)]
                loads = [pltpu.make_async_copy(laps_ref.at[s - 1], lapbuf.at[1], local_sems.at[2]),
                         pltpu.make_async_copy(head, headbuf, local_sems.at[3])]
                for cp in loads:
                    cp.start()
                for cp in loads:
                    cp.wait()
                headbuf[...] = (headbuf[...].astype(F32) + lapbuf[1].astype(F32)).astype(BF)
                store = pltpu.make_async_copy(headbuf, head, local_sems.at[3])
                store.start()
                store.wait()
            loads = [pltpu.make_async_copy(meta_ref, metabuf.at[me_s], local_sems.at[0])]
            loads += [pltpu.make_async_copy(gm_ref.at[cs], metabuf.at[cs], local_sems.at[1 + k])
                      for k, (_, _, cs) in enumerate(chips)]
            for cp in loads:
                cp.start()
            for cp in loads:
                cp.wait()
            tokens = jnp.concatenate([metabuf[s] for s in range(4)], axis=1)
            emit(jnp.concatenate([jnp.zeros((PAD, D), F32), tokens], axis=0))

    def chunk(i):
        return (i + 1) % NCH

    return pl.pallas_call(
        body, name="all_gather_w", grid=(NCH,),
        in_specs=[ANY, ANY, pl.BlockSpec((C, D), lambda i: (jnp.minimum(i, last - 1), 0)),
                  pl.BlockSpec((1, D), lambda i: (0, 0))],
        out_specs=[ANY] * 3 + [pl.BlockSpec((C, D), lambda i: (chunk(i), 0))] * 2
        + [pl.BlockSpec((D, C), lambda i: (0, chunk(i)))],
        out_shape=[jax.ShapeDtypeStruct((WMAIN, D), own_win.dtype), jax.ShapeDtypeStruct((4, LAP, D), own_win.dtype),
                   jax.ShapeDtypeStruct((4,) + meta.shape, meta.dtype),
                   jax.ShapeDtypeStruct((T, D), F32), jax.ShapeDtypeStruct((T, D), BF),
                   jax.ShapeDtypeStruct((D, T), BF)],
        scratch_shapes=[pltpu.SemaphoreType.DMA((18,)), pltpu.SemaphoreType.DMA((18,)), pltpu.SemaphoreType.DMA((4,)),
                        pltpu.VMEM((WOFF, D), own_win.dtype), pltpu.VMEM((2, LAP, D), own_win.dtype),
                        pltpu.VMEM((LAP, D), own_win.dtype), pltpu.VMEM((4,) + meta.shape, meta.dtype)],
        compiler_params=_params(("arbitrary",)),
    )(own_win, meta, x, norm_g)


def _pair_swap(gwt, arrs):
    n = len(arrs)

    def body(*refs):
        gw, ins = refs[0], refs[1:n + 1]
        gwo, outs = refs[n + 1], refs[n + 2:2 * n + 2]
        send_sems, recv_sems = refs[2 * n + 2:]
        x, y, c = _place()
        sib = (x, y, 1 - c)
        cps = []
        for k in range(4):
            cps.append(pltpu.make_async_remote_copy(
                src_ref=gw.at[pl.ds(WOFF * k + (1 - c) * WHALF, WHALF)], dst_ref=gwo.at[k],
                send_sem=send_sems.at[k], recv_sem=recv_sems.at[k], device_id=sib, device_id_type=MESH))
        for a in range(n):
            rows = ins[a].shape[1] // 2
            cps.append(pltpu.make_async_remote_copy(
                src_ref=ins[a].at[:, pl.ds((1 - c) * rows, rows)], dst_ref=outs[a],
                send_sem=send_sems.at[4 + a], recv_sem=recv_sems.at[4 + a], device_id=sib, device_id_type=MESH))
        for cp in cps:
            cp.start()
        for cp in cps:
            cp.wait()

    return pl.pallas_call(
        body, name="rs_pair_swap",
        in_specs=[ANY] * (n + 1), out_specs=[ANY] * (n + 1),
        out_shape=[jax.ShapeDtypeStruct((4, WHALF, D), gwt.dtype)]
        + [jax.ShapeDtypeStruct((4, a.shape[1] // 2, a.shape[2]), a.dtype) for a in arrs],
        scratch_shapes=[pltpu.SemaphoreType.DMA((n + 4,)), pltpu.SemaphoreType.DMA((n + 4,))],
    )(gwt, *arrs)


def _add_windows(gwt, recv):
    tb = 256
    nb = WHALF // tb
    c = lax.axis_index("c")

    def body(c_ref, a_ref, b_ref, o_ref):
        o_ref[0] = (a_ref[...].astype(F32) + b_ref[0].astype(F32)).astype(BF)

    return pl.pallas_call(
        body, name="pair_add_in",
        grid_spec=pltpu.PrefetchScalarGridSpec(
            num_scalar_prefetch=1, grid=(4, nb),
            in_specs=[pl.BlockSpec((tb, D), lambda k, i, cr: ((WOFF // tb) * k + nb * cr[0] + i, 0)),
                      pl.BlockSpec((1, tb, D), lambda k, i, cr: (k, i, 0))],
            out_specs=pl.BlockSpec((1, tb, D), lambda k, i, cr: (k, i, 0))),
        out_shape=jax.ShapeDtypeStruct(recv.shape, BF),
        compiler_params=_params(("parallel", "parallel")),
    )(jnp.reshape(c, (1,)).astype(jnp.int32), gwt, recv)


def _chip_exchange(parts, small):
    n = len(parts)

    def body(*refs):
        ins, sm = refs[:n], refs[n]
        outs, smo = refs[n + 1:2 * n + 1], refs[2 * n + 1]
        send_sems, recv_sems = refs[2 * n + 2:]
        cps = _chip_copies(ins, outs, send_sems, recv_sems, by_dest=True)
        cps += _chip_copies([sm], [smo], send_sems.at[pl.ds(3 * n, 3)], recv_sems.at[pl.ds(3 * n, 3)], by_dest=False)
        for cp in cps:
            cp.start()
        for cp in cps:
            cp.wait()

    return pl.pallas_call(
        body, name="rs_chip_exchange",
        in_specs=[ANY] * (n + 1), out_specs=[ANY] * (n + 1),
        out_shape=[jax.ShapeDtypeStruct(p.shape, p.dtype) for p in parts]
        + [jax.ShapeDtypeStruct((4,) + small.shape, small.dtype)],
        scratch_shapes=[pltpu.SemaphoreType.DMA((3 * (n + 1),)), pltpu.SemaphoreType.DMA((3 * (n + 1),))],
    )(*parts, small)


def _pair_send(halves):
    n = len(halves)

    def body(*refs):
        ins, outs = refs[:n], refs[n:2 * n]
        send_sems, recv_sems = refs[2 * n:]
        x, y, c = _place()
        cps = [pltpu.make_async_remote_copy(
            src_ref=ins[a], dst_ref=outs[a], send_sem=send_sems.at[a], recv_sem=recv_sems.at[a],
            device_id=(x, y, 1 - c), device_id_type=MESH) for a in range(n)]
        for cp in cps:
            cp.start()
        for cp in cps:
            cp.wait()

    return pl.pallas_call(
        body, name="rs_pair_send",
        in_specs=[ANY] * n, out_specs=[ANY] * n,
        out_shape=[jax.ShapeDtypeStruct(h.shape, h.dtype) for h in halves],
        scratch_shapes=[pltpu.SemaphoreType.DMA((n,)), pltpu.SemaphoreType.DMA((n,))],
    )(*halves)


def _row_block(rows):
    for tb in (256, 128, 64, 32, 16, 8):
        if rows % tb == 0:
            return tb
    return rows


def _add_halves(full, recv, name, out_dtype):
    _, r2, w = recv.shape
    tb = _row_block(r2)
    nb = r2 // tb
    c = lax.axis_index("c")

    def body(c_ref, a_ref, b_ref, o_ref):
        o_ref[...] = (a_ref[...].astype(F32) + b_ref[...].astype(F32)).astype(o_ref.dtype)

    return pl.pallas_call(
        body, name=name,
        grid_spec=pltpu.PrefetchScalarGridSpec(
            num_scalar_prefetch=1, grid=(4, nb),
            in_specs=[pl.BlockSpec((1, tb, w), lambda s, i, cr: (s, cr[0] * nb + i, 0)),
                      pl.BlockSpec((1, tb, w), lambda s, i, cr: (s, i, 0))],
            out_specs=pl.BlockSpec((1, tb, w), lambda s, i, cr: (s, i, 0))),
        out_shape=jax.ShapeDtypeStruct(recv.shape, out_dtype),
        compiler_params=_params(("parallel", "parallel")),
    )(jnp.reshape(c, (1,)).astype(jnp.int32), full, recv)


def _add2(a, b, name):
    def body(a_ref, b_ref, o_ref):
        o_ref[...] = a_ref[...] + b_ref[...]

    return pl.pallas_call(body, name=name, out_shape=jax.ShapeDtypeStruct(a.shape, a.dtype))(a, b)


def _sum4(buf, own, name):
    _, r, w = buf.shape
    tb = _row_block(r)
    me_s = 2 * lax.axis_index("x") + lax.axis_index("y")
    by_dest = own.ndim == 3

    def body(s_ref, b_ref, own_ref, o_ref):
        mine = (own_ref[0] if by_dest else own_ref[...]).astype(F32)
        terms = [jnp.where(s_ref[0] == t, mine, b_ref[t].astype(F32)) for t in range(4)]
        o_ref[...] = ((terms[0] + terms[1]) + terms[2]) + terms[3]

    own_spec = (pl.BlockSpec((1, tb, w), lambda i, sr: (sr[0], i, 0)) if by_dest
                else pl.BlockSpec((tb, w), lambda i, sr: (i, 0)))
    return pl.pallas_call(
        body, name=name,
        grid_spec=pltpu.PrefetchScalarGridSpec(
            num_scalar_prefetch=1, grid=(r // tb,),
            in_specs=[pl.BlockSpec((4, tb, w), lambda i, sr: (0, i, 0)), own_spec],
            out_specs=pl.BlockSpec((tb, w), lambda i, sr: (i, 0))),
        out_shape=jax.ShapeDtypeStruct((r, w), F32),
        compiler_params=_params(("parallel",)),
    )(jnp.reshape(me_s, (1,)).astype(jnp.int32), buf, own)


def _adamw_math(w, g, m, v):
    mn = B1 * m + (1.0 - B1) * g
    vn = B2 * v + (1.0 - B2) * (g * g)
    m_hat = mn / (1.0 - B1 ** STEP)
    v_hat = vn / (1.0 - B2 ** STEP)
    return -LR * (m_hat / (jnp.sqrt(v_hat) + AEPS) + WD * w), mn, vn


def _adamw(w, g, m, v, name):
    r, c_ = w.shape
    tb = _row_block(r)
    if tb == r and r > 512:
        tb = 256

    def body(w_ref, g_ref, m_ref, v_ref, d_ref, mo_ref, vo_ref):
        d_ref[...], mo_ref[...], vo_ref[...] = _adamw_math(w_ref[...], g_ref[...], m_ref[...], v_ref[...])

    spec = pl.BlockSpec((tb, c_), lambda i: (i, 0))
    return pl.pallas_call(
        body, name=name, grid=(pl.cdiv(r, tb),),
        in_specs=[spec] * 4, out_specs=[spec] * 3,
        out_shape=[jax.ShapeDtypeStruct(w.shape, F32)] * 3,
        compiler_params=_params(("parallel",)),
    )(w, g, m, v)


def _adamw_rows(w, g, m, v, name):
    r = w.shape[0]
    tb = 256
    sub, lanes = w.shape[1:]

    def body(w_ref, g_ref, m_ref, v_ref, go_ref, d_ref, mo_ref, vo_ref):
        g = g_ref[...].reshape(tb, sub, lanes)
        go_ref[...] = g
        d_ref[...], mo_ref[...], vo_ref[...] = _adamw_math(w_ref[...], g, m_ref[...], v_ref[...])

    spec = pl.BlockSpec((tb, sub, lanes), lambda i: (i, 0, 0))
    return pl.pallas_call(
        body, name=name, grid=(pl.cdiv(r, tb),),
        in_specs=[spec, pl.BlockSpec((tb, sub * lanes), lambda i: (i, 0)), spec, spec], out_specs=[spec] * 4,
        out_shape=[jax.ShapeDtypeStruct(w.shape, F32)] * 4,
        compiler_params=_params(("parallel",)),
    )(w, g, m, v)


def _adamw_halves(w, g_mine, g_sib, m, v, name):
    r, c_ = w.shape
    r2 = g_mine.shape[0]
    tb = _row_block(r2)
    nb = r2 // tb
    c = lax.axis_index("c")

    def body(c_ref, w_ref, gm_ref, gs_ref, m_ref, v_ref, g_ref, d_ref, mo_ref, vo_ref):
        g = jnp.where(pl.program_id(0) == c_ref[0], gm_ref[...], gs_ref[...])
        g_ref[...] = g
        d_ref[...], mo_ref[...], vo_ref[...] = _adamw_math(w_ref[...], g, m_ref[...], v_ref[...])

    full = pl.BlockSpec((tb, c_), lambda h, i, cr: (h * nb + i, 0))
    half = pl.BlockSpec((tb, c_), lambda h, i, cr: (i, 0))
    return pl.pallas_call(
        body, name=name,
        grid_spec=pltpu.PrefetchScalarGridSpec(
            num_scalar_prefetch=1, grid=(2, nb),
            in_specs=[full, half, half, full, full], out_specs=[full] * 4),
        out_shape=[jax.ShapeDtypeStruct(w.shape, F32)] * 4,
        compiler_params=_params(("parallel", "parallel")),
    )(jnp.reshape(c, (1,)).astype(jnp.int32), w, g_mine, g_sib, m, v)


def kernel(x, meta_tokens, norm_g, w_in, b_f, w_out, final_g, loss_target, m_meta_tokens, m_norm_g, m_w_in, m_b_f, m_w_out, m_final_g, v_meta_tokens, v_norm_g, v_w_in, v_b_f, v_w_out, v_final_g):
    me_s = 2 * lax.axis_index("x") + lax.axis_index("y")
    core = lax.axis_index("c")
    w3, m3, v3 = [jnp.transpose(jnp.reshape(t[0], (D // C, C, WSH)), (2, 0, 1)) for t in (w_in, m_w_in, v_w_in)]

    wt_main, laps, _, *normed = _gather_weights(_own_window(w3), meta_tokens, x[0], norm_g)
    wft = jnp.pad(laps[3, :NFF], ((0, C - NFF), (0, 0)))
    mine = (jnp.arange(4) == me_s)[:, None, None]
    wout_own = w_out[0].astype(BF)

    def wout_full(landed):
        return jnp.where(mine, wout_own[None], landed).reshape(DMIX, D)

    def chip_sums(gwt, dwout):
        g_out = dwout.reshape(4, DMIX // 4, D)
        r_in, r_out = _pair_swap(gwt, [g_out])
        return [_add_windows(gwt, r_in), _add_halves(g_out, r_out, "pair_add_out", BF)]

    loss, gx, dmeta, dng, gwt, dbf, dwout, dfg, (p_in, p_out), (e_in, e_out) = _local_step(
        x[0], loss_target[0], normed, norm_g, wt_main, wft, b_f, wout_own, final_g.reshape(1, D), chip_sums, wout_full)

    g_meta = jnp.stack([dmeta[:, 256 * s:256 * (s + 1)] for s in range(4)])
    small = jnp.concatenate([dng, dfg, jnp.pad(dbf, ((0, 0), (0, D - NFF))),
                             jnp.pad(jnp.reshape(loss, (1, 1)), ((0, 0), (0, D - 1))),
                             jnp.zeros((4, D), F32)], axis=0)
    e_meta, e_small = _chip_exchange([g_meta], small)
    h_in, h_out = _sum4(e_in, p_in, "sum_in"), _sum4(e_out, p_out, "sum_out")
    h_meta, h_small = _sum4(e_meta, g_meta, "sum_meta"), _sum4(e_small, small, "sum_small")
    s_in, s_out, s_meta, s_small = _pair_send([h_in, h_out, h_meta, h_small])
    gw_meta = _add2(h_meta, s_meta, "pair_add_meta")
    tot = _add2(h_small, s_small, "pair_add_small")
    g_norm, g_final, g_bf, loss_all = tot[0:1], tot[1], tot[2:3, :NFF], tot[3, 0]

    d_meta, nm_meta, nv_meta = _adamw(meta_tokens, gw_meta, m_meta_tokens, v_meta_tokens, "adamw_meta")
    d_norm, nm_norm, nv_norm = _adamw(norm_g, g_norm, m_norm_g, v_norm_g, "adamw_norm")
    window = jnp.concatenate([jnp.where(core == 0, h_in, s_in), jnp.where(core == 0, s_in, h_in)], axis=0)
    gwt_own = lax.dynamic_slice(window, (4 * me_s, 0), (WSH, D))
    outs_in = _adamw_rows(w3, gwt_own, m3, v3, "adamw_in")
    gw_in, d_in, nm_in, nv_in = [jnp.reshape(jnp.transpose(t, (1, 2, 0)), (1, D, WSH)) for t in outs_in]
    d_bf, nm_bf, nv_bf = _adamw(b_f, g_bf, m_b_f, v_b_f, "adamw_bf")
    gw_out, d_out, nm_out, nv_out = _adamw_halves(w_out[0], h_out, s_out, m_w_out[0], v_w_out[0], "adamw_out")
    d_fin, nm_fin, nv_fin = _adamw(final_g.reshape(1, D), g_final.reshape(1, D), m_final_g.reshape(1, D),
                                   v_final_g.reshape(1, D), "adamw_final")
    return (loss_all, gx[None], gw_meta, g_norm, gw_in, g_bf, gw_out[None], g_final,
            d_meta, d_norm, d_in, d_bf, d_out[None], d_fin.reshape(D),
            nm_meta, nm_norm, nm_in, nm_bf, nm_out[None], nm_fin.reshape(D),
            nv_meta, nv_norm, nv_in, nv_bf, nv_out[None], nv_fin.reshape(D))
```

```python
import functools

import numpy as np
import jax
import jax.numpy as jnp
from jax import lax
from jax.experimental import pallas as pl
from jax.experimental.pallas import tpu as pltpu

D = 1024
SEQ = 2048
NMETA = 16
C = 128
PAD = C - NMETA
T = PAD + NMETA + SEQ
NCH = T // C
RH, RDK, RDV = 4, 128, 256
FH, FD = 16, 64
NPAIR = FH // 2
WMAIN = 7168
NFF = 16
WIN = WMAIN + NFF
WSH = WIN // 4
WPADROWS = 1824
DMIX = 2048
EPS = 1e-6
NEG = -1e30
RSCALE = RDK ** -0.5
FSCALE = FD ** -0.5
ROPE_BASE = 10000.0
LR, B1, B2, AEPS, WD, STEP = 0.001, 0.9, 0.999, 1e-08, 0.01, 10

BF = jnp.bfloat16
F32 = jnp.float32
NT = (((1,), (1,)), ((), ()))
TN = (((0,), (0,)), ((), ()))
NN_DIMS = (((1,), (0,)), ((), ()))
MESH = pl.DeviceIdType.MESH
ANY = pl.BlockSpec(memory_space=pl.ANY)
VMEM_LIMIT = 48 * 1024 * 1024
DW_VMEM_LIMIT = 56 * 1024 * 1024

GB_R, GB_F = 2, 6
QB_F, KB_F, VB_F = 24, 32, 40


def _dot(a, b):
    return jnp.dot(a, b, preferred_element_type=F32)


def _dg(a, b, dims):
    return lax.dot_general(a, b, dims, preferred_element_type=F32)


def _params(sem=None):
    return pltpu.CompilerParams(dimension_semantics=sem, vmem_limit_bytes=VMEM_LIMIT)


def _constants():
    pos = jnp.arange(T, dtype=F32) - PAD
    inv = ROPE_BASE ** (-jnp.arange(0, RDK, 2, dtype=F32) / RDK)
    ang = pos[:, None] * inv[None, :]
    cos, sin = jnp.cos(ang), jnp.sin(ang)
    cos2 = jnp.concatenate([cos, cos], axis=1)
    sin2 = jnp.concatenate([-sin, sin], axis=1)
    log_gamma = jnp.log1p(-jnp.exp2(-5.0 - jnp.arange(RH, dtype=F32)))
    idx = jnp.arange(C, dtype=F32)
    diff = idx[:, None] - idx[None, :]
    dmask = jnp.where(diff[None] >= 0, jnp.exp(log_gamma[:, None, None] * jnp.maximum(diff, 0.0)[None]), 0.0)
    zeta = jnp.exp(log_gamma[:, None] * (C - 1.0 - idx)[None, :])
    xi = jnp.exp(log_gamma[:, None] * (idx + 1.0)[None, :])
    gdec = jnp.exp(log_gamma * C)
    zeta_b = jnp.broadcast_to(zeta[:, :, None], (RH, C, RDK))
    xi_b = jnp.broadcast_to(xi[:, :, None], (RH, C, RDK))
    gdec_b = jnp.broadcast_to(gdec[:, None, None], (RH, RDK, RDV))
    tri = jnp.asarray(np.tril(np.ones((C, C), np.float32)), dtype=BF)
    head_of_lane = np.arange(FH * FD) // FD
    pick = ((np.arange(FH * FD)[:, None] % FD == 0)
            & (head_of_lane[:, None] == np.arange(C)[None, :])).astype(np.float32)
    seg = (np.arange(C)[:, None] // FD == np.arange(C)[None, :] // FD).astype(np.float32)
    ones_aug = np.concatenate([np.tile((np.arange(C) < FD)[None, :], (C, 1)),
                               np.tile((np.arange(C) >= FD)[None, :], (C, 1))], axis=0).astype(np.float32)
    lane = np.arange(2 * C) % C
    causal = np.where(lane[None, :] <= np.arange(C)[:, None], 0.0, NEG).astype(np.float32)
    mask_bias = np.stack([np.zeros((C, 2 * C), np.float32), causal])
    return dict(cos2=cos2, sin2=sin2, dmask=dmask, zeta=zeta_b, xi=xi_b, gdec=gdec_b, tri=tri,
                mask_bias=jnp.asarray(mask_bias), pick=jnp.asarray(pick, dtype=BF), seg=jnp.asarray(seg, dtype=BF),
                ones_aug=jnp.asarray(ones_aug, dtype=BF))


def _norm_rows(h, g):
    return h * lax.rsqrt(jnp.mean(h * h, axis=1, keepdims=True) + EPS) * g


def _mm_nt(a, b, n, tm, tn, name):
    m, k = a.shape

    def body(a_ref, b_ref, o_ref):
        o_ref[...] = _dg(a_ref[...], b_ref[...], NT)

    return pl.pallas_call(
        body, name=name, grid=(m // tm, n // tn),
        in_specs=[pl.BlockSpec((tm, k), lambda i, j: (i, 0)), pl.BlockSpec((tn, k), lambda i, j: (j, 0))],
        out_specs=pl.BlockSpec((tm, tn), lambda i, j: (i, j)),
        out_shape=jax.ShapeDtypeStruct((m, n), F32),
        compiler_params=_params(("parallel", "parallel")),
    )(a, b)


def _mm_nn(a, b, tm, tn, name, out_dtype=F32):
    m, k = a.shape
    _, n = b.shape

    def body(a_ref, b_ref, o_ref):
        o_ref[...] = _dot(a_ref[...], b_ref[...]).astype(out_dtype)

    return pl.pallas_call(
        body, name=name, grid=(m // tm, n // tn),
        in_specs=[pl.BlockSpec((tm, k), lambda i, j: (i, 0)), pl.BlockSpec((k, tn), lambda i, j: (0, j))],
        out_specs=pl.BlockSpec((tm, tn), lambda i, j: (i, j)),
        out_shape=jax.ShapeDtypeStruct((m, n), out_dtype),
        compiler_params=_params(("parallel", "parallel")),
    )(a, b)


def _rot(x, cos2, sin2):
    return x * cos2 + pltpu.roll(x, 64, 1) * sin2


def _ret_specs(chunk):
    whole = lambda shape: pl.BlockSpec(shape, lambda n: (0,) * len(shape))
    return [
        pl.BlockSpec((C, RH * RDK), lambda n: (chunk(n), 0)),
        pl.BlockSpec((C, RH * RDK), lambda n: (chunk(n), 1)),
        pl.BlockSpec((C, RH * RDV), lambda n: (chunk(n), 1)),
        pl.BlockSpec((C, RDK), lambda n: (chunk(n), 0)),
        pl.BlockSpec((C, RDK), lambda n: (chunk(n), 0)),
        whole((RH, C, C)), whole((RH, C, RDK)), whole((RH, C, RDK)), whole((RH, RDK, RDV)),
    ]


def _ret_heads(q_ref, k_ref, v_ref, cos, sin):
    qr = [_rot(q_ref[:, RDK * h:RDK * (h + 1)], cos, sin) for h in range(RH)]
    kr = [_rot(k_ref[:, RDK * h:RDK * (h + 1)], cos, sin) * RSCALE for h in range(RH)]
    vb = [v_ref[:, RDV * h:RDV * (h + 1)].astype(BF) for h in range(RH)]
    return qr, kr, [t.astype(BF) for t in qr], [t.astype(BF) for t in kr], vb


def _ret_fwd(z, cst):
    def body(q_ref, k_ref, v_ref, cos_ref, sin_ref, dm_ref, xi_ref, zt_ref, gd_ref, r_ref, sp_ref, st):
        n = pl.program_id(0)

        @pl.when(n == 0)
        def _():
            st[...] = jnp.zeros_like(st)

        hs = range(RH)
        qr, kr, qb, kb, vb = _ret_heads(q_ref, k_ref, v_ref, cos_ref[...], sin_ref[...])
        sd = [(_dg(qb[h], kb[h], NT) * dm_ref[h]).astype(BF) for h in hs]
        state = [st[h] for h in hs]
        qx = [(qr[h] * xi_ref[h]).astype(BF) for h in hs]
        kz = [(kr[h] * zt_ref[h]).astype(BF) for h in hs]
        out = [_dot(sd[h], vb[h]) + _dot(qx[h], state[h].astype(BF)) for h in hs]
        kv = [_dg(kz[h], vb[h], TN) for h in hs]
        for h in hs:
            sp_ref[0, h] = state[h]
            r_ref[:, RDV * h:RDV * (h + 1)] = out[h]
            st[h] = state[h] * gd_ref[h] + kv[h]

    return pl.pallas_call(
        body, name="ret_fwd", grid=(NCH,),
        in_specs=_ret_specs(lambda n: n),
        out_specs=[pl.BlockSpec((C, RH * RDV), lambda n: (n, 0)),
                   pl.BlockSpec((1, RH, RDK, RDV), lambda n: (n, 0, 0, 0))],
        out_shape=[jax.ShapeDtypeStruct((T, RH * RDV), F32), jax.ShapeDtypeStruct((NCH, RH, RDK, RDV), F32)],
        scratch_shapes=[pltpu.VMEM((RH, RDK, RDV), F32)],
        compiler_params=_params(("arbitrary",)),
    )(z, z, z, cst["cos2"], cst["sin2"], cst["dmask"], cst["xi"], cst["zeta"], cst["gdec"])


def _ret_bwd(z, cst, sprev, dr):
    def body(q_ref, k_ref, v_ref, cos_ref, sin_ref, dm_ref, xi_ref, zt_ref, gd_ref, sp_ref, dr_ref,
             dq_ref, dk_ref, dv_ref, gst):
        i = pl.program_id(0)

        @pl.when(i == 0)
        def _():
            gst[...] = jnp.zeros_like(gst)

        hs = range(RH)
        cos, sin = cos_ref[...], sin_ref[...]
        qr, kr, qb, kb, vb = _ret_heads(q_ref, k_ref, v_ref, cos, sin)
        dm = [dm_ref[h] for h in hs]
        xi = [xi_ref[h] for h in hs]
        zt = [zt_ref[h] for h in hs]
        sd = [(_dg(qb[h], kb[h], NT) * dm[h]).astype(BF) for h in hs]
        qx = [(qr[h] * xi[h]).astype(BF) for h in hs]
        kz = [(kr[h] * zt[h]).astype(BF) for h in hs]
        drb = [dr_ref[:, RDV * h:RDV * (h + 1)] for h in hs]
        sb = [sp_ref[0, h].astype(BF) for h in hs]
        g = [gst[h] for h in hs]
        gb = [t.astype(BF) for t in g]
        ds = [(_dg(drb[h], vb[h], NT) * dm[h]).astype(BF) for h in hs]
        dq = [_dot(ds[h], kb[h]) + _dg(drb[h], sb[h], NT) * xi[h] for h in hs]
        dk = [(_dg(ds[h], qb[h], TN) + _dg(vb[h], gb[h], NT) * zt[h]) * RSCALE for h in hs]
        dv = [_dg(sd[h], drb[h], TN) + _dot(kz[h], gb[h]) for h in hs]
        gn = [g[h] * gd_ref[h] + _dg(qx[h], drb[h], TN) for h in hs]
        for h in hs:
            gst[h] = gn[h]
            dq_ref[:, RDK * h:RDK * (h + 1)] = (dq[h] * cos + pltpu.roll(dq[h] * sin, 64, 1)).astype(BF)
            dk_ref[:, RDK * h:RDK * (h + 1)] = (dk[h] * cos + pltpu.roll(dk[h] * sin, 64, 1)).astype(BF)
            dv_ref[:, RDV * h:RDV * (h + 1)] = dv[h].astype(BF)

    rev = lambda n: NCH - 1 - n
    return pl.pallas_call(
        body, name="ret_bwd", grid=(NCH,),
        in_specs=_ret_specs(rev) + [
            pl.BlockSpec((1, RH, RDK, RDV), lambda n: (rev(n), 0, 0, 0)),
            pl.BlockSpec((C, RH * RDV), lambda n: (rev(n), 0)),
        ],
        out_specs=[pl.BlockSpec((C, RH * RDK), lambda n: (rev(n), 0)),
                   pl.BlockSpec((C, RH * RDK), lambda n: (rev(n), 0)),
                   pl.BlockSpec((C, RH * RDV), lambda n: (rev(n), 0))],
        out_shape=[jax.ShapeDtypeStruct((T, RH * RDK), BF), jax.ShapeDtypeStruct((T, RH * RDK), BF),
                   jax.ShapeDtypeStruct((T, RH * RDV), BF)],
        scratch_shapes=[pltpu.VMEM((RH, RDK, RDV), F32)],
        compiler_params=_params(("arbitrary",)),
    )(z, z, z, cst["cos2"], cst["sin2"], cst["dmask"], cst["xi"], cst["zeta"], cst["gdec"], sprev, dr)


def _place():
    x, y, c = lax.axis_index("x"), lax.axis_index("y"), lax.axis_index("c")
    return x, y, c


def _other_chips(x, y):
    return [(1 - x, y, 2 * (1 - x) + y), (x, 1 - y, 2 * x + (1 - y)), (1 - x, 1 - y, 2 * (1 - x) + (1 - y))]


def _chip_copies(srcs, lands, send_sems, recv_sems, by_dest):
    x, y, c = _place()
    me_s = 2 * x + y
    return [pltpu.make_async_remote_copy(
        src_ref=src.at[cs] if by_dest else src, dst_ref=land.at[me_s],
        send_sem=send_sems.at[3 * a + j], recv_sem=recv_sems.at[3 * a + j],
        device_id=(cx, cy, c), device_id_type=MESH)
        for a, (src, land) in enumerate(zip(srcs, lands)) for j, (cx, cy, cs) in enumerate(_other_chips(x, y))]


def _split_dot(x, mat01, dims=NN_DIMS, x_first=True):
    acc, rest = None, x
    for _ in range(3):
        piece = rest.astype(BF)
        part = _dg(piece, mat01, dims) if x_first else _dg(mat01, piece, dims)
        acc = part if acc is None else acc + part
        rest = rest - piece.astype(F32)
    return acc


def _log_sigmoid(x):
    return -(jnp.maximum(-x, 0.0) + jnp.log1p(jnp.exp(-jnp.abs(x))))


def _fox_prep(zf, bf_pad, cst):
    def body(zf_ref, b_ref, tri_ref, ct_ref, carry):
        n = pl.program_id(0)

        @pl.when(n == 0)
        def _():
            carry[...] = jnp.zeros_like(carry)

        ls = _log_sigmoid(zf_ref[...] + b_ref[...])
        row = n * C + lax.broadcasted_iota(jnp.int32, (C, C), 0)
        lf = jnp.where(row >= PAD, ls, 0.0)
        cc = _split_dot(lf, tri_ref[...], x_first=False) + carry[0:1, :]
        carry[...] = jnp.broadcast_to(cc[C - 1:C, :], carry.shape)
        pos = n * C + lax.broadcasted_iota(jnp.int32, (FH, C), 1)
        ct_ref[0] = jnp.where(pos >= PAD, cc.T[:FH, :], -NEG)

    return pl.pallas_call(
        body, name="fox_prep", grid=(NCH,),
        in_specs=[pl.BlockSpec((C, C), lambda n: (n, 0)), pl.BlockSpec((1, C), lambda n: (0, 0)),
                  pl.BlockSpec((C, C), lambda n: (0, 0))],
        out_specs=pl.BlockSpec((1, FH, C), lambda n: (n, 0, 0)),
        out_shape=jax.ShapeDtypeStruct((NCH, FH, C), F32),
        scratch_shapes=[pltpu.VMEM((8, C), F32)],
        compiler_params=_params(("arbitrary",)),
    )(zf, bf_pad, cst["tri"])


def _lo_lanes(shape):
    return lax.broadcasted_iota(jnp.int32, shape, 1) < FD


def _split_heads(x):
    lo = _lo_lanes(x.shape)
    zero = jnp.zeros_like(x)
    return jnp.concatenate([jnp.where(lo, x, zero), jnp.where(lo, zero, x)], axis=0)


def _spread2(x):
    lo = _lo_lanes(x.shape)
    r = pltpu.roll(x, FD, 1)
    return jnp.concatenate([jnp.where(lo, x, r), jnp.where(lo, r, x)], axis=1)


NSTEP = (NCH + 1) // 2
NTILE = NCH + 1
TROWS = T + C


def _fox_tile(s, t):
    second = t > s
    return second.astype(jnp.int32), jnp.where(second, t - s - 1, s - t)


def _fox_pos(i):
    return jnp.where(i < NSTEP, 2 * i, 2 * (NCH - 1 - i) + 1)


FOX_ORDER = [2 * i if i < NSTEP else 2 * (NCH - 1 - i) + 1 for i in range(NCH)]


def _fox_pair_specs():
    first = pl.BlockSpec((C, C), lambda p, s: (2 * s, p))
    second = pl.BlockSpec((C, C), lambda p, s: (jnp.where(s == NSTEP - 1, 2 * s, 2 * s + 1), p))
    both = pl.BlockSpec((2 * C, C), lambda p, s: (s, p))
    return first, second, both


def _fox_q_specs():
    return (pl.BlockSpec((C, C), lambda p, s: (s, QB_F + p)),
            pl.BlockSpec((C, C), lambda p, s: (NCH - 1 - s, QB_F + p)))


def _fox_key_bias(ct_ref, p, j):
    return jnp.concatenate([ct_ref[j, pl.ds(2 * p, 1), :], ct_ref[j, pl.ds(2 * p + 1, 1), :]], axis=1)


def _fox_kv(z_ref, kbuf, vbuf, sems, p):
    def copies(pair, slot):
        return [pltpu.make_async_copy(z_ref.at[:, pl.ds(pl.multiple_of((first + pair) * C, C), C)], buf.at[slot],
                                      sems.at[i, slot]) for i, (first, buf) in enumerate(((KB_F, kbuf), (VB_F, vbuf)))]

    slot = p % 2

    @pl.when(p == 0)
    def _():
        for cp in copies(0, 0):
            cp.start()

    for cp in copies(p, slot):
        cp.wait()

    @pl.when(p + 1 < NPAIR)
    def _():
        for cp in copies(p + 1, 1 - slot):
            cp.start()

    return slot


def _fox_fwd(z, ct, cst, share):
    n = 0 if share is None else 1

    def body(qa_ref, qb_ref, z_ref, ct_ref, ones_ref, mb_ref, *rest):
        share_refs, (a_ref, g_ref), land_refs = rest[:n], rest[n:n + 2], rest[n + 2:2 * n + 2]
        kks, vvs, q2, m2, sbuf, kbuf, vbuf, kv_sems = rest[2 * n + 2:2 * n + 10]
        p, s = pl.program_id(0), pl.program_id(1)
        if n:
            copies = _chip_copies(share_refs, land_refs, *rest[2 * n + 10:], by_dest=False)

            @pl.when((p == 0) & (s == 0))
            def _():
                for cp in copies:
                    cp.start()

            @pl.when((p == NPAIR - 1) & (s == NSTEP - 1))
            def _():
                for cp in copies:
                    cp.wait()

        @pl.when(s == 0)
        def _():
            ones = ones_ref[...]
            slot = _fox_kv(z_ref, kbuf, vbuf, kv_sems, p)

            def prep(j, carry):
                rows = pl.ds(pl.multiple_of(j * C, C), C)
                kks[j] = _split_heads(kbuf[slot, rows, :]).astype(BF)
                vvs[j] = jnp.concatenate([_split_heads(vbuf[slot, rows, :]).astype(BF), ones], axis=1)
                return carry

            lax.fori_loop(0, NCH, prep, 0)

        q2[0] = (qa_ref[...] * FSCALE).astype(BF)
        q2[1] = (qb_ref[...] * FSCALE).astype(BF)

        tiles = [_fox_tile(s, t) for t in range(NTILE)]
        causal = mb_ref[1]
        neg = jnp.full((C, 2 * C), NEG, F32)
        run, first = neg, neg
        for t, (sel, j) in enumerate(tiles):
            st = _dg(q2[sel], kks[j], NT) - _fox_key_bias(ct_ref, p, j)
            if t in (0, NTILE - 1):
                st = st + causal
            sbuf[t] = st
            run = jnp.maximum(jnp.where(t == s + 1, neg, run), st)
            first = jnp.where(t == s, run, first)
        for w, mx in enumerate((first, run)):
            m2[w] = jnp.concatenate(
                [jnp.broadcast_to(jnp.max(mx[:, :C], axis=1, keepdims=True), (C, C)),
                 jnp.broadcast_to(jnp.max(mx[:, C:], axis=1, keepdims=True), (C, C))], axis=1)

        zero = jnp.zeros((C, 2 * C), F32)
        run, first = zero, zero
        for t, (sel, j) in enumerate(tiles):
            run = jnp.where(t == s + 1, zero, run) + _dot(jnp.exp(sbuf[t] - m2[sel]).astype(BF), vvs[j])
            first = jnp.where(t == s, run, first)
        lo = _lo_lanes((C, C))
        for w, res in enumerate((first, run)):
            l = res[:, C:]
            a_ref[C * w:C * (w + 1), :] = res[:, :C] / l
            mw = m2[w]
            g_ref[C * w:C * (w + 1), :] = -(jnp.where(lo, mw[:, :C], mw[:, C:]) + jnp.log(l))

    qa, qb = _fox_q_specs()
    both = _fox_pair_specs()[2]
    return pl.pallas_call(
        body, name="fox_fwd", grid=(NPAIR, NSTEP),
        in_specs=[qa, qb, ANY,
                  pl.BlockSpec((NCH, FH, C), lambda p, s: (0, 0, 0)),
                  pl.BlockSpec((2 * C, C), lambda p, s: (0, 0)),
                  pl.BlockSpec((2, C, 2 * C), lambda p, s: (0, 0, 0))] + [ANY] * n,
        out_specs=[both, both] + [ANY] * n,
        out_shape=[jax.ShapeDtypeStruct((TROWS, FH * FD), F32)] * 2
        + ([jax.ShapeDtypeStruct((4,) + share.shape, share.dtype)] if n else []),
        scratch_shapes=[pltpu.VMEM((NCH, 2 * C, C), BF), pltpu.VMEM((NCH, 2 * C, 2 * C), BF),
                        pltpu.VMEM((2, C, C), BF), pltpu.VMEM((2, C, 2 * C), F32),
                        pltpu.VMEM((NTILE, C, 2 * C), F32),
                        pltpu.VMEM((2, T, C), F32), pltpu.VMEM((2, T, C), F32), pltpu.SemaphoreType.DMA((2, 2))]
        + [pltpu.SemaphoreType.DMA((3,)), pltpu.SemaphoreType.DMA((3,))] * n,
        compiler_params=_params(("arbitrary", "arbitrary")),
    )(z, z, z, ct, cst["ones_aug"], cst["mask_bias"], *([share] * n))


def _fox_bwd(z, da, g, delta, ct, cst):
    grp = 9

    def body(qa_ref, qb_ref, daa_ref, dab_ref, ga_ref, gb_ref, dla_ref, dlb_ref, z_ref, ct_ref, ones_ref,
             mb_ref, dq_ref, dr_ref, dk_ref, dv_ref, dcs_ref,
             kks, vvs, q2, qq2, dd2, da2, gi2, dl2, dq2, dvb, dkb, dkacc, dvacc, csacc, kbuf, vbuf, kv_sems):
        p, s = pl.program_id(0), pl.program_id(1)
        ones = ones_ref[...]

        @pl.when(s == 0)
        def _():
            dkacc[...] = jnp.zeros_like(dkacc)
            dvacc[...] = jnp.zeros_like(dvacc)
            csacc[...] = jnp.zeros_like(csacc)
            slot = _fox_kv(z_ref, kbuf, vbuf, kv_sems, p)

            def prep(j, carry):
                rows = pl.ds(pl.multiple_of(j * C, C), C)
                kks[j] = _split_heads(kbuf[slot, rows, :]).astype(BF)
                vvs[j] = _split_heads(vbuf[slot, rows, :]).astype(BF)
                return carry

            lax.fori_loop(0, NCH, prep, 0)

        for w, (q_ref, d_ref, g_ref, l_ref) in enumerate(((qa_ref, daa_ref, ga_ref, dla_ref),
                                                          (qb_ref, dab_ref, gb_ref, dlb_ref))):
            qf = q_ref[...]
            q2[w] = (qf * FSCALE).astype(BF)
            qq2[w] = jnp.concatenate([_split_heads(qf).astype(BF), ones], axis=1)
            da2[w] = d_ref[...]
            dd2[w] = _split_heads(d_ref[...].astype(F32)).astype(BF)
            gi2[w] = _spread2(g_ref[...])
            dl2[w] = _spread2(l_ref[...])
        dq2[...] = jnp.zeros_like(dq2)
        zero = jnp.zeros((C, 2 * C), F32)

        def group(gi, carry):
            ts = [gi * grp + u for u in range(grp)]
            tiles = [_fox_tile(s, t) for t in ts]
            kk = [kks[j] for _, j in tiles]
            ss = [_dg(q2[sel], kj, NT) + (gi2[sel] - _fox_key_bias(ct_ref, p, j)) for kj, (sel, j) in zip(kk, tiles)]
            ss[0] = ss[0] + mb_ref[(gi == 0).astype(jnp.int32)]
            ss[-1] = ss[-1] + mb_ref[(gi == 1).astype(jnp.int32)]
            dps = [_dg(da2[sel], vvs[j], NT) for sel, j in tiles]
            pes = [jnp.exp(st) for st in ss]
            dss = [pe * (dp - dl2[sel]) * FSCALE for pe, dp, (sel, _) in zip(pes, dps, tiles)]
            pts = [jnp.concatenate([pe[:, :C].T, pe[:, C:].T], axis=1).astype(BF) for pe in pes]
            dsts = [jnp.concatenate([ds[:, :C].T, ds[:, C:].T], axis=1).astype(BF) for ds in dss]
            dvs = [_dot(pt, dd2[sel]) for pt, (sel, _) in zip(pts, tiles)]
            rs = [_dot(dst, qq2[sel]) for dst, (sel, _) in zip(dsts, tiles)]
            parts = [_dot(ds.astype(BF), jnp.concatenate([kj, ones], axis=1)) for ds, kj in zip(dss, kk)]
            for t, dv, rr in zip(ts, dvs, rs):
                dvb[t] = dv
                dkb[t] = rr
            pa, pb = zero, zero
            for t, part in zip(ts, parts):
                pa = pa + jnp.where(t <= s, part, zero)
                pb = pb + jnp.where(t <= s, zero, part)
            dq2[0] += pa
            dq2[1] += pb
            return carry

        ntile = jnp.where(s == NSTEP - 1, grp, NTILE)
        lax.fori_loop(0, ntile // grp, group, 0)

        def scatter(t, carry):
            _, j = _fox_tile(s, t)
            r = pl.ds(pl.multiple_of(j * C, C), C)
            dvacc[r, :] += dvb[t]
            dkacc[r, :] += dkb[t, :, :C]
            csacc[r, :] += dkb[t, :, C:]
            return carry

        lax.fori_loop(0, ntile, scatter, 0)
        for w in range(2):
            res = dq2[w]
            dq_ref[C * w:C * (w + 1), :] = res[:, :C].astype(BF)
            dr_ref[C * w:C * (w + 1), :] = res[:, C:]

        @pl.when(s == NSTEP - 1)
        def _():
            dk_ref[...] = dkacc[...].astype(BF)
            dv_ref[...] = dvacc[...].astype(BF)
            dcs_ref[...] = csacc[...]

    qa, qb = _fox_q_specs()
    ba, bb, both = _fox_pair_specs()
    col = pl.BlockSpec((T, C), lambda p, s: (0, p))
    return pl.pallas_call(
        body, name="fox_bwd", grid=(NPAIR, NSTEP),
        in_specs=[qa, qb, ba, bb, ba, bb, ba, bb, ANY,
                  pl.BlockSpec((NCH, FH, C), lambda p, s: (0, 0, 0)),
                  pl.BlockSpec((2 * C, C), lambda p, s: (0, 0)),
                  pl.BlockSpec((2, C, 2 * C), lambda p, s: (0, 0, 0))],
        out_specs=[both, both, col, col, col],
        out_shape=[jax.ShapeDtypeStruct((TROWS, FH * FD), BF), jax.ShapeDtypeStruct((TROWS, FH * FD), F32),
                   jax.ShapeDtypeStruct((T, FH * FD), BF), jax.ShapeDtypeStruct((T, FH * FD), BF),
                   jax.ShapeDtypeStruct((T, FH * FD), F32)],
        scratch_shapes=[pltpu.VMEM((NCH, 2 * C, C), BF), pltpu.VMEM((NCH, 2 * C, C), BF),
                        pltpu.VMEM((2, C, C), BF), pltpu.VMEM((2, 2 * C, 2 * C), BF), pltpu.VMEM((2, 2 * C, C), BF),
                        pltpu.VMEM((2, C, C), BF), pltpu.VMEM((2, C, 2 * C), F32), pltpu.VMEM((2, C, 2 * C), F32),
                        pltpu.VMEM((2, C, 2 * C), F32),
                        pltpu.VMEM((NTILE, C, C), F32), pltpu.VMEM((NTILE, C, 2 * C), F32),
                        pltpu.VMEM((T, C), F32), pltpu.VMEM((T, C), F32), pltpu.VMEM((T, C), F32),
                        pltpu.VMEM((2, T, C), F32), pltpu.VMEM((2, T, C), F32), pltpu.SemaphoreType.DMA((2, 2))],
        compiler_params=_params(("arbitrary", "arbitrary")),
    )(z, z, da, da, g, g, delta, delta, z, ct, cst["ones_aug"], cst["mask_bias"])


def _fox_gate_bwd(drow, dcol, zf, bf_pad, cst):
    def body(dr_ref, dc_ref, zf_ref, b_ref, tri_ref, pick_ref, dff_ref, db_ref, carry):
        s = pl.program_id(0)
        n = NCH - 1 - s

        @pl.when(s == 0)
        def _():
            carry[...] = jnp.zeros_like(carry)
            db_ref[...] = jnp.zeros_like(db_ref)

        dcb = _split_dot((dr_ref[...] - dc_ref[...]) * (1.0 / FSCALE), pick_ref[...])
        suf = _split_dot(dcb, tri_ref[...], TN, x_first=False) + carry[0:1, :]
        carry[...] = jnp.broadcast_to(suf[0:1, :], carry.shape)
        x = zf_ref[...] + b_ref[...]
        row = n * C + lax.broadcasted_iota(jnp.int32, (C, C), 0)
        dff = jnp.where(row >= PAD, suf * (1.0 - jax.nn.sigmoid(x)), 0.0)
        dff_ref[...] = dff.astype(BF)
        db_ref[...] += jnp.sum(dff, axis=0, keepdims=True)

    rev = lambda s: (NCH - 1 - s, 0)
    return pl.pallas_call(
        body, name="fox_gate_bwd", grid=(NCH,),
        in_specs=[pl.BlockSpec((C, FH * FD), lambda s: (_fox_pos(NCH - 1 - s), 0)),
                  pl.BlockSpec((C, FH * FD), rev), pl.BlockSpec((C, C), rev),
                  pl.BlockSpec((1, C), lambda s: (0, 0)), pl.BlockSpec((C, C), lambda s: (0, 0)),
                  pl.BlockSpec((FH * FD, C), lambda s: (0, 0))],
        out_specs=[pl.BlockSpec((C, C), rev), pl.BlockSpec((1, C), lambda s: (0, 0))],
        out_shape=[jax.ShapeDtypeStruct((T, C), BF), jax.ShapeDtypeStruct((1, C), F32)],
        scratch_shapes=[pltpu.VMEM((8, C), F32)],
        compiler_params=_params(("arbitrary",)),
    )(drow, dcol, zf, bf_pad, cst["tri"], cst["pick"])


def _head_norm(r):
    rn, rs = [], []
    for h in range(RH):
        rh = r[:, RDV * h:RDV * (h + 1)]
        s = lax.rsqrt(jnp.mean(rh * rh, axis=1, keepdims=True) + EPS)
        rn.append(rh * s)
        rs.append(s)
    return jnp.concatenate(rn, axis=1), rs


def _gated(r, rg, a, fg):
    rn, _ = _head_norm(r)
    return jnp.concatenate([rn * (rg * jax.nn.sigmoid(rg)), a * (fg * jax.nn.sigmoid(fg))], axis=1)


def _out_loss(r, z, a, wout, x, tgt, fgain):
    def body(r_ref, rg_ref, a_ref, fg_ref, w_ref, x_ref, t_ref, g_ref, yt_ref, do_ref, dob_ref, loss_ref, dg_ref):
        i = pl.program_id(0)

        @pl.when(i == 0)
        def _():
            yt_ref[...] = jnp.zeros_like(yt_ref)
            do_ref[...] = jnp.zeros_like(do_ref)
            dob_ref[...] = jnp.zeros_like(dob_ref)
            loss_ref[...] = jnp.zeros_like(loss_ref)
            dg_ref[...] = jnp.zeros_like(dg_ref)

        @pl.when(i > 0)
        def _():
            y = _gated(r_ref[...], rg_ref[...], a_ref[...], fg_ref[...])
            yt_ref[...] = y.T.astype(BF)
            o = x_ref[...] + _dot(y.astype(BF), w_ref[...])
            rs = lax.rsqrt(jnp.mean(o * o, axis=1, keepdims=True) + EPS)
            on = o * rs
            g = g_ref[...]
            e = on * g - t_ref[...]
            loss_ref[...] += 0.5 * jnp.sum(jnp.mean(e * e, axis=1, keepdims=True))
            dyh = e * (1.0 / D)
            dg_ref[...] += jnp.sum(dyh * on, axis=0, keepdims=True)
            don = dyh * g
            do = rs * (don - on * jnp.mean(don * on, axis=1, keepdims=True))
            do_ref[...] = do
            dob_ref[...] = do.astype(BF)

    tok = lambda i: (jnp.maximum(i - 1, 0), 0)
    return pl.pallas_call(
        body, name="out_loss", grid=(NCH,),
        in_specs=[pl.BlockSpec((C, D), lambda i: (i, 0)), pl.BlockSpec((C, D), lambda i: (i, GB_R)),
                  pl.BlockSpec((C, D), lambda i: (_fox_pos(i), 0)), pl.BlockSpec((C, D), lambda i: (i, GB_F)),
                  pl.BlockSpec((DMIX, D), lambda i: (0, 0)),
                  pl.BlockSpec((C, D), tok), pl.BlockSpec((C, D), tok), pl.BlockSpec((1, D), lambda i: (0, 0))],
        out_specs=[pl.BlockSpec((DMIX, C), lambda i: (0, i)), pl.BlockSpec((C, D), lambda i: (i, 0)),
                   pl.BlockSpec((C, D), lambda i: (i, 0)), pl.BlockSpec((8, C), lambda i: (0, 0)),
                   pl.BlockSpec((1, D), lambda i: (0, 0))],
        out_shape=[jax.ShapeDtypeStruct((DMIX, T), BF), jax.ShapeDtypeStruct((T, D), F32),
                   jax.ShapeDtypeStruct((T, D), BF), jax.ShapeDtypeStruct((8, C), F32),
                   jax.ShapeDtypeStruct((1, D), F32)],
        compiler_params=_params(("arbitrary",)),
    )(r, z, a, z, wout, x, tgt, fgain)


def _silu_and_grad(x):
    s = jax.nn.sigmoid(x)
    return x * s, s * (1.0 + x * (1.0 - s))


def _dy_gate_bwd(dob, wout, r, z, a, seg):
    def body(do_ref, w_ref, r_ref, rg_ref, a_ref, fg_ref, seg_ref, dr_ref, da_ref, drg_ref, dfg_ref, dl_ref):
        dy = _dg(do_ref[...], w_ref[...], NT)
        a_ = a_ref[...]
        rn, rs = _head_norm(r_ref[...])
        silu_rg, dsilu_rg = _silu_and_grad(rg_ref[...])
        silu_fg, dsilu_fg = _silu_and_grad(fg_ref[...])
        dyr, dyf = dy[:, :D], dy[:, D:]
        drn = dyr * silu_rg
        drg_ref[...] = (dyr * rn * dsilu_rg).astype(BF)
        for h in range(RH):
            sl = slice(RDV * h, RDV * (h + 1))
            dh, nh = drn[:, sl], rn[:, sl]
            dr_ref[:, sl] = (rs[h] * (dh - nh * jnp.mean(dh * nh, axis=1, keepdims=True))).astype(BF)
        dab = (dyf * silu_fg).astype(BF)
        da_ref[...] = dab
        dfg_ref[...] = (dyf * a_ * dsilu_fg).astype(BF)
        prod = dab.astype(F32) * a_
        segm = seg_ref[...]
        for p in range(NPAIR):
            sl = slice(C * p, C * (p + 1))
            hi = prod[:, sl].astype(BF)
            lo = (prod[:, sl] - hi.astype(F32)).astype(BF)
            dl_ref[:, sl] = _dot(hi, segm) + _dot(lo, segm)

    row = pl.BlockSpec((C, D), lambda i: (i, 0))
    fox = pl.BlockSpec((C, D), lambda i: (_fox_pos(i), 0))
    return pl.pallas_call(
        body, name="dy_gate_bwd", grid=(NCH,),
        in_specs=[row, pl.BlockSpec((DMIX, D), lambda i: (0, 0)),
                  row, pl.BlockSpec((C, D), lambda i: (i, GB_R)),
                  fox, pl.BlockSpec((C, D), lambda i: (i, GB_F)),
                  pl.BlockSpec((C, C), lambda i: (0, 0))],
        out_specs=[row, fox, row, row, fox],
        out_shape=[jax.ShapeDtypeStruct((T, D), BF), jax.ShapeDtypeStruct((TROWS, D), BF),
                   jax.ShapeDtypeStruct((T, D), BF), jax.ShapeDtypeStruct((T, D), BF),
                   jax.ShapeDtypeStruct((TROWS, D), F32)],
        compiler_params=_params(("parallel",)),
    )(dob, wout, r, z, a, z, seg)


DZ_WIDTHS = (512, 512, 1024, 1024, 1024, 1024, 1024, 1024)


def _du_norm_bwd(dzs, dzf, wt, wft, hpad, g, dopad, blocks=None, carried=None, hosted=None):
    tm, tk = 544, 1024
    nk = WMAIN // tk
    i0, ni = blocks or (0, T // tm)
    srcs, land_shapes, ncopies, make_copies = hosted or ((), (), 0, None)
    n, nl, nc = len(srcs), len(land_shapes), int(carried is not None)

    def body(rq_ref, rk_ref, rv_ref, rg_ref, fq_ref, fk_ref, fv_ref, fg_ref, dzf_ref, w_ref, wf_ref, h_ref, g_ref,
             do_ref, *rest):
        rest = rest[nc:]
        src_refs, (gh_ref, dg_ref), land_refs = rest[:n], rest[n:n + 2], rest[n + 2:n + 2 + nl]
        acc = rest[n + 2 + nl]
        i, k = pl.program_id(0), pl.program_id(1)

        if hosted:
            copies = make_copies(src_refs, land_refs, *rest[n + 3 + nl:])

            @pl.when((i == 0) & (k == 0))
            def _():
                for cp in copies:
                    cp.start()

            @pl.when((i == ni - 1) & (k == nk - 1))
            def _():
                for cp in copies:
                    cp.wait()

        @pl.when(k == 0)
        def _():
            acc[...] = (_dot(dzf_ref[...], wf_ref[...]) + _dot(rq_ref[...], w_ref[:512, :])
                        + _dot(rk_ref[...], w_ref[512:, :]))

        for kk, piece in enumerate((rv_ref, rg_ref, fq_ref, fk_ref, fv_ref, fg_ref), start=1):
            @pl.when(k == kk)
            def _(piece=piece):
                acc[...] += _dot(piece[...], w_ref[...])

        @pl.when(k == nk - 1)
        def _():
            du = acc[...]
            h = h_ref[...]
            gg = g_ref[...]
            rs = lax.rsqrt(jnp.mean(h * h, axis=1, keepdims=True) + EPS)
            hn = h * rs
            part = jnp.sum(du * hn, axis=0, keepdims=True)

            @pl.when(i == 0)
            def _():
                dg_ref[...] = part

            @pl.when(i > 0)
            def _():
                dg_ref[...] += part

            dhn = du * gg
            gh_ref[...] = rs * (dhn - hn * jnp.mean(dhn * hn, axis=1, keepdims=True)) + do_ref[...]

    def rows(width):
        return pl.BlockSpec((tm, width), lambda i, k: (i0 + i, 0))

    sems = [pltpu.SemaphoreType.DMA((ncopies,)), pltpu.SemaphoreType.DMA((ncopies,))] if hosted else []
    return pl.pallas_call(
        body, name="du_norm_bwd", grid=(ni, nk),
        in_specs=[rows(w) for w in DZ_WIDTHS]
        + [rows(C), pl.BlockSpec((tk, D), lambda i, k: (k, 0)), pl.BlockSpec((C, D), lambda i, k: (0, 0)),
           rows(D), pl.BlockSpec((1, D), lambda i, k: (0, 0)), rows(D)] + [ANY] * (nc + n),
        out_specs=[rows(D), pl.BlockSpec((1, D), lambda i, k: (0, 0))] + [ANY] * nl,
        out_shape=[jax.ShapeDtypeStruct((T, D), F32), jax.ShapeDtypeStruct((1, D), F32)] + list(land_shapes),
        input_output_aliases={len(DZ_WIDTHS) + 6: 0} if nc else {},
        scratch_shapes=[pltpu.VMEM((tm, D), F32)] + sems,
        compiler_params=_params(("arbitrary", "arbitrary")),
    )(*dzs, dzf, wt, wft, hpad, g, dopad, *([carried] * nc), *srcs)


GROWS = 7680


def _dw_in(dzs, dzf, ut):
    tn = 512
    nmain = WMAIN // tn
    first, blocks = [], []
    for w in DZ_WIDTHS:
        first.append(sum(blocks))
        blocks.append(w // tn)

    def body(rq_ref, rk_ref, rv_ref, rg_ref, fq_ref, fk_ref, fv_ref, fg_ref, dzf_ref, ut_ref, o_ref):
        gidx = pl.program_id(0)
        for piece, g0, nb in zip((rq_ref, rk_ref, rv_ref, rg_ref, fq_ref, fk_ref, fv_ref, fg_ref), first, blocks):
            @pl.when((gidx >= g0) & (gidx < g0 + nb))
            def _(piece=piece):
                o_ref[...] = _dot(ut_ref[...], piece[...]).T.astype(BF)

        @pl.when(gidx == nmain)
        def _():
            o_ref[:C, :] = _dot(ut_ref[...], dzf_ref[...]).T.astype(BF)
            o_ref[C:, :] = jnp.zeros((tn - C, D), BF)

    def piece_spec(g0, nb):
        return pl.BlockSpec((T, tn), lambda gidx: (0, jnp.clip(gidx - g0, 0, nb - 1)))

    return pl.pallas_call(
        body, name="dw_in", grid=(nmain + 1,),
        in_specs=[piece_spec(g0, nb) for g0, nb in zip(first, blocks)]
        + [pl.BlockSpec((T, C), lambda gidx: (0, 0)), pl.BlockSpec((D, T), lambda gidx: (0, 0))],
        out_specs=pl.BlockSpec((tn, D), lambda gidx: (gidx, 0)),
        out_shape=jax.ShapeDtypeStruct((GROWS, D), BF),
        compiler_params=pltpu.CompilerParams(dimension_semantics=("arbitrary",), vmem_limit_bytes=DW_VMEM_LIMIT),
    )(*dzs, dzf, ut)


def _token_order(x_po):
    def body(i_ref, o_ref):
        o_ref[...] = i_ref[...]

    return pl.pallas_call(
        body, name="token_order", grid=(NCH,),
        in_specs=[pl.BlockSpec((C, D), lambda i: (_fox_pos(i), 0))],
        out_specs=pl.BlockSpec((C, D), lambda i: (i, 0)),
        out_shape=jax.ShapeDtypeStruct((T, D), x_po.dtype),
        compiler_params=_params(("parallel",)),
    )(x_po)


def _local_step(x, tgt, normed, norm_g, wt, wft, b_f, wout, final_g, reduce_scatter=False, wout_full=None):
    cst = _constants()
    hpad, u, ut = normed
    bf_pad = jnp.pad(b_f, ((0, 0), (0, C - NFF)))
    z = _mm_nt(u, wt, WMAIN, T // 2, 1024, "in_proj")
    zf = _mm_nt(u, wft, C, T // 2, C, "in_proj_ff")
    r, sprev = _ret_fwd(z, cst)
    ct = _fox_prep(zf, bf_pad, cst)
    if wout_full is None:
        a, g = _fox_fwd(z, ct, cst, None)
    else:
        a, g, landed_wout = _fox_fwd(z, ct, cst, wout)
        wout = wout_full(landed_wout)
    yt, dopad, dob, loss8, dfg = _out_loss(r, z, a, wout, x, tgt, final_g)
    dr, da, dzrg, dzfg, delta = _dy_gate_bwd(dob, wout, r, z, a, cst["seg"])
    dwout = _mm_nn(yt, dob, 512, D, "dw_out", BF)
    dzq_r, dzk_r, dzv_r = _ret_bwd(z, cst, sprev, dr)
    dq_po, drow, dzk_f, dzv_f, dcol = _fox_bwd(z, da, g, delta, ct, cst)
    dzf, dbf = _fox_gate_bwd(drow, dcol, zf, bf_pad, cst)
    dzs = [dzq_r, dzk_r, dzv_r, dzrg, _token_order(dq_po), dzk_f, dzv_f, dzfg]
    gwt = _dw_in(dzs, dzf, ut)
    du_args = (dzs, dzf, wt, wft, hpad, norm_g, dopad)
    if not reduce_scatter:
        gh, dng = _du_norm_bwd(*du_args)
        return (loss8[0, 0], gh[C:], gh[PAD:C], dng, gwt, dbf[:, :NFF], dwout, dfg, [], [])
    g_out = dwout.reshape(4, DMIX // 4, D)
    gh, dng_a, r_in, r_out = _du_norm_bwd(*du_args, blocks=(0, 2), hosted=_hosted_pair_swap(gwt, [g_out]))
    parts = [_add_windows(gwt, r_in), _add_halves(g_out, r_out, "pair_add_out", BF)]
    gh, dng_b, *landed = _du_norm_bwd(*du_args, blocks=(2, 2), carried=gh, hosted=_hosted_chip_exchange(parts))
    return (loss8[0, 0], gh[C:], gh[PAD:C], dng_a + dng_b, gwt, dbf[:, :NFF], dwout, dfg, parts, landed)


WOFF, WLEN = 1792, 2048
WHALF = WLEN // 2
LAP = WPADROWS - WOFF


def _own_window(w3):
    rows, sub, lanes = w3.shape
    pad = WPADROWS - rows
    tb = 96
    nb = WPADROWS // tb
    half = rows // 2

    def body(w_ref, o_ref, buf, sems):
        x, y, _ = _place()
        shift = 4 * (2 * x + y)
        buf[pl.ds(0, pad)] = jnp.zeros((pad, sub, lanes), F32)
        buf[pl.ds(rows, pad)] = jnp.zeros((pad, sub, lanes), F32)
        cps = [pltpu.make_async_copy(w_ref.at[pl.ds(half * h, half)], buf.at[pl.ds(shift + half * h, half)],
                                     sems.at[h]) for h in range(2)]
        for cp in cps:
            cp.start()

        def block(i, carry):
            r0 = pl.multiple_of(i * tb, tb)
            o_ref[pl.ds(r0, tb), :] = buf[pl.ds(r0, tb)].reshape(tb, sub * lanes).astype(BF)
            return carry

        cps[0].wait()
        lax.fori_loop(0, half // tb, block, 0)
        cps[1].wait()
        lax.fori_loop(half // tb, nb, block, 0)

    return pl.pallas_call(
        body, name="own_window",
        in_specs=[ANY], out_shape=jax.ShapeDtypeStruct((WPADROWS, sub * lanes), BF),
        scratch_shapes=[pltpu.VMEM((WPADROWS, sub, lanes), F32), pltpu.SemaphoreType.DMA((2,))],
        compiler_params=pltpu.CompilerParams(vmem_limit_bytes=VMEM_LIMIT),
    )(w3)


def _gather_weights(own_win, meta, x, norm_g):
    half_main, half_lap, half_meta = WOFF // 2, LAP // 2, meta.shape[0] // 2
    last = NCH - 1

    def body(win_ref, meta_ref, x_ref, g_ref, w_ref, laps_ref, gm_ref, h_ref, u_ref, ut_ref,
             send_sems, recv_sems, local_sems, stage, lapbuf, headbuf, metabuf):
        step = pl.program_id(0)
        x, y, c = _place()
        me_s = 2 * x + y
        sib = (x, y, 1 - c)
        chips = _other_chips(x, y)

        def emit(h):
            u = _norm_rows(h, g_ref[...])
            h_ref[...] = h
            u_ref[...] = u.astype(BF)
            ut_ref[...] = u.T.astype(BF)

        kinds = [
            (lambda h: win_ref.at[pl.ds(half_main * h, half_main)],
             lambda s, h: w_ref.at[pl.ds(WOFF * s + half_main * h, half_main)]),
            (lambda h: win_ref.at[pl.ds(WOFF + half_lap * h, half_lap)],
             lambda s, h: laps_ref.at[s, pl.ds(half_lap * h, half_lap)]),
            (lambda h: meta_ref.at[pl.ds(half_meta * h, half_meta)],
             lambda s, h: gm_ref.at[s, pl.ds(half_meta * h, half_meta)]),
        ]
        own_in = pltpu.make_async_copy(win_ref.at[pl.ds(0, WOFF)], stage, local_sems.at[0])
        own_lap_in = pltpu.make_async_copy(win_ref.at[pl.ds(WOFF, LAP)], lapbuf.at[0], local_sems.at[1])
        own_out = pltpu.make_async_copy(stage, w_ref.at[pl.ds(WOFF * me_s, WOFF)], local_sems.at[0])
        own_lap_out = pltpu.make_async_copy(lapbuf.at[0], laps_ref.at[me_s], local_sems.at[1])
        sends, arrivals, forwards, forwarded = [], [], [], []
        for a, (src, dst) in enumerate(kinds):
            for k, (cx, cy, cs) in enumerate(chips):
                there = dict(send_sem=send_sems.at[6 * a + k], recv_sem=recv_sems.at[6 * a + k],
                             device_id=(cx, cy, c), device_id_type=MESH)
                across = dict(send_sem=send_sems.at[6 * a + 3 + k], recv_sem=recv_sems.at[6 * a + 3 + k],
                              device_id=sib, device_id_type=MESH)
                sends.append(pltpu.make_async_remote_copy(src_ref=src(c), dst_ref=dst(me_s, c), **there))
                arrivals.append(pltpu.make_async_remote_copy(src_ref=dst(cs, c), dst_ref=dst(cs, c), **there))
                forwards.append(pltpu.make_async_remote_copy(src_ref=dst(cs, c), dst_ref=dst(cs, c), **across))
                forwarded.append(pltpu.make_async_remote_copy(
                    src_ref=dst(cs, 1 - c), dst_ref=dst(cs, 1 - c), **across))

        @pl.when(step == 0)
        def _():
            own_in.start()
            own_lap_in.start()
            for cp in sends:
                cp.start()
            own_in.wait()
            own_out.start()
            own_lap_in.wait()
            own_lap_out.start()

        @pl.when(step < last)
        def _():
            emit(x_ref[...])

        @pl.when(step == last)
        def _():
            for cp, fwd in zip(arrivals, forwards):
                cp.wait_recv()
                fwd.start()
            for cp in forwarded:
                cp.wait_recv()
            for cp in sends + forwards:
                cp.wait_send()
            own_out.wait()
            own_lap_out.wait()
            for s in range(1, 4):
                head = w_ref.at[pl.ds(WOFF * s, LAP)]
                loads = [pltpu.make_async_copy(laps_ref.at[s - 1], lapbuf.at[1], local_sems.at[2]),
                         pltpu.make_async_copy(head, headbuf, local_sems.at[3])]
                for cp in loads:
                    cp.start()
                for cp in loads:
                    cp.wait()
                headbuf[...] = (headbuf[...].astype(F32) + lapbuf[1].astype(F32)).astype(BF)
                store = pltpu.make_async_copy(headbuf, head, local_sems.at[3])
                store.start()
                store.wait()
            loads = [pltpu.make_async_copy(meta_ref, metabuf.at[me_s], local_sems.at[0])]
            loads += [pltpu.make_async_copy(gm_ref.at[cs], metabuf.at[cs], local_sems.at[1 + k])
                      for k, (_, _, cs) in enumerate(chips)]
            for cp in loads:
                cp.start()
            for cp in loads:
                cp.wait()
            tokens = jnp.concatenate([metabuf[s] for s in range(4)], axis=1)
            emit(jnp.concatenate([jnp.zeros((PAD, D), F32), tokens], axis=0))

    def chunk(i):
        return (i + 1) % NCH

    return pl.pallas_call(
        body, name="all_gather_w", grid=(NCH,),
        in_specs=[ANY, ANY, pl.BlockSpec((C, D), lambda i: (jnp.minimum(i, last - 1), 0)),
                  pl.BlockSpec((1, D), lambda i: (0, 0))],
        out_specs=[ANY] * 3 + [pl.BlockSpec((C, D), lambda i: (chunk(i), 0))] * 2
        + [pl.BlockSpec((D, C), lambda i: (0, chunk(i)))],
        out_shape=[jax.ShapeDtypeStruct((WMAIN, D), own_win.dtype), jax.ShapeDtypeStruct((4, LAP, D), own_win.dtype),
                   jax.ShapeDtypeStruct((4,) + meta.shape, meta.dtype),
                   jax.ShapeDtypeStruct((T, D), F32), jax.ShapeDtypeStruct((T, D), BF),
                   jax.ShapeDtypeStruct((D, T), BF)],
        scratch_shapes=[pltpu.SemaphoreType.DMA((18,)), pltpu.SemaphoreType.DMA((18,)), pltpu.SemaphoreType.DMA((4,)),
                        pltpu.VMEM((WOFF, D), own_win.dtype), pltpu.VMEM((2, LAP, D), own_win.dtype),
                        pltpu.VMEM((LAP, D), own_win.dtype), pltpu.VMEM((4,) + meta.shape, meta.dtype)],
        compiler_params=_params(("arbitrary",)),
    )(own_win, meta, x, norm_g)


def _pair_copies(srcs, lands, send_sems, recv_sems):
    x, y, c = _place()
    sib = dict(device_id=(x, y, 1 - c), device_id_type=MESH)
    cps = [pltpu.make_async_remote_copy(
        src_ref=srcs[0].at[pl.ds(WOFF * k + (1 - c) * WHALF, WHALF)], dst_ref=lands[0].at[k],
        send_sem=send_sems.at[k], recv_sem=recv_sems.at[k], **sib) for k in range(4)]
    for a, (src, land) in enumerate(zip(srcs[1:], lands[1:]), start=4):
        rows = src.shape[1] // 2
        cps.append(pltpu.make_async_remote_copy(
            src_ref=src.at[:, pl.ds((1 - c) * rows, rows)], dst_ref=land,
            send_sem=send_sems.at[a], recv_sem=recv_sems.at[a], **sib))
    return cps


def _hosted_pair_swap(gwt, arrs):
    shapes = [jax.ShapeDtypeStruct((4, WHALF, D), gwt.dtype)]
    shapes += [jax.ShapeDtypeStruct((4, a.shape[1] // 2, a.shape[2]), a.dtype) for a in arrs]
    return [gwt, *arrs], shapes, 4 + len(arrs), _pair_copies


def _hosted_chip_exchange(parts):
    shapes = [jax.ShapeDtypeStruct(p.shape, p.dtype) for p in parts]
    return list(parts), shapes, 3 * len(parts), functools.partial(_chip_copies, by_dest=True)


def _add_windows(gwt, recv):
    tb = 256
    nb = WHALF // tb
    c = lax.axis_index("c")

    def body(c_ref, a_ref, b_ref, o_ref):
        o_ref[0] = (a_ref[...].astype(F32) + b_ref[0].astype(F32)).astype(BF)

    return pl.pallas_call(
        body, name="pair_add_in",
        grid_spec=pltpu.PrefetchScalarGridSpec(
            num_scalar_prefetch=1, grid=(4, nb),
            in_specs=[pl.BlockSpec((tb, D), lambda k, i, cr: ((WOFF // tb) * k + nb * cr[0] + i, 0)),
                      pl.BlockSpec((1, tb, D), lambda k, i, cr: (k, i, 0))],
            out_specs=pl.BlockSpec((1, tb, D), lambda k, i, cr: (k, i, 0))),
        out_shape=jax.ShapeDtypeStruct(recv.shape, BF),
        compiler_params=_params(("parallel", "parallel")),
    )(jnp.reshape(c, (1,)).astype(jnp.int32), gwt, recv)


def _chip_exchange(parts, small):
    n = len(parts)

    def body(*refs):
        ins, sm = refs[:n], refs[n]
        outs, smo = refs[n + 1:2 * n + 1], refs[2 * n + 1]
        send_sems, recv_sems = refs[2 * n + 2:]
        cps = _chip_copies(ins, outs, send_sems, recv_sems, by_dest=True)
        cps += _chip_copies([sm], [smo], send_sems.at[pl.ds(3 * n, 3)], recv_sems.at[pl.ds(3 * n, 3)], by_dest=False)
        for cp in cps:
            cp.start()
        for cp in cps:
            cp.wait()

    return pl.pallas_call(
        body, name="rs_chip_exchange",
        in_specs=[ANY] * (n + 1), out_specs=[ANY] * (n + 1),
        out_shape=[jax.ShapeDtypeStruct(p.shape, p.dtype) for p in parts]
        + [jax.ShapeDtypeStruct((4,) + small.shape, small.dtype)],
        scratch_shapes=[pltpu.SemaphoreType.DMA((3 * (n + 1),)), pltpu.SemaphoreType.DMA((3 * (n + 1),))],
    )(*parts, small)


def _pair_send(halves):
    n = len(halves)

    def body(*refs):
        ins, outs = refs[:n], refs[n:2 * n]
        send_sems, recv_sems = refs[2 * n:]
        x, y, c = _place()
        cps = [pltpu.make_async_remote_copy(
            src_ref=ins[a], dst_ref=outs[a], send_sem=send_sems.at[a], recv_sem=recv_sems.at[a],
            device_id=(x, y, 1 - c), device_id_type=MESH) for a in range(n)]
        for cp in cps:
            cp.start()
        for cp in cps:
            cp.wait()

    return pl.pallas_call(
        body, name="rs_pair_send",
        in_specs=[ANY] * n, out_specs=[ANY] * n,
        out_shape=[jax.ShapeDtypeStruct(h.shape, h.dtype) for h in halves],
        scratch_shapes=[pltpu.SemaphoreType.DMA((n,)), pltpu.SemaphoreType.DMA((n,))],
    )(*halves)


def _row_block(rows):
    for tb in (256, 128, 64, 32, 16, 8):
        if rows % tb == 0:
            return tb
    return rows


def _add_halves(full, recv, name, out_dtype):
    _, r2, w = recv.shape
    tb = _row_block(r2)
    nb = r2 // tb
    c = lax.axis_index("c")

    def body(c_ref, a_ref, b_ref, o_ref):
        o_ref[...] = (a_ref[...].astype(F32) + b_ref[...].astype(F32)).astype(o_ref.dtype)

    return pl.pallas_call(
        body, name=name,
        grid_spec=pltpu.PrefetchScalarGridSpec(
            num_scalar_prefetch=1, grid=(4, nb),
            in_specs=[pl.BlockSpec((1, tb, w), lambda s, i, cr: (s, cr[0] * nb + i, 0)),
                      pl.BlockSpec((1, tb, w), lambda s, i, cr: (s, i, 0))],
            out_specs=pl.BlockSpec((1, tb, w), lambda s, i, cr: (s, i, 0))),
        out_shape=jax.ShapeDtypeStruct(recv.shape, out_dtype),
        compiler_params=_params(("parallel", "parallel")),
    )(jnp.reshape(c, (1,)).astype(jnp.int32), full, recv)


def _add2(a, b, name):
    def body(a_ref, b_ref, o_ref):
        o_ref[...] = a_ref[...] + b_ref[...]

    return pl.pallas_call(body, name=name, out_shape=jax.ShapeDtypeStruct(a.shape, a.dtype))(a, b)


def _sum4(buf, own, name):
    _, r, w = buf.shape
    tb = _row_block(r)
    me_s = 2 * lax.axis_index("x") + lax.axis_index("y")
    by_dest = own.ndim == 3

    def body(s_ref, b_ref, own_ref, o_ref):
        mine = (own_ref[0] if by_dest else own_ref[...]).astype(F32)
        terms = [jnp.where(s_ref[0] == t, mine, b_ref[t].astype(F32)) for t in range(4)]
        o_ref[...] = ((terms[0] + terms[1]) + terms[2]) + terms[3]

    own_spec = (pl.BlockSpec((1, tb, w), lambda i, sr: (sr[0], i, 0)) if by_dest
                else pl.BlockSpec((tb, w), lambda i, sr: (i, 0)))
    return pl.pallas_call(
        body, name=name,
        grid_spec=pltpu.PrefetchScalarGridSpec(
            num_scalar_prefetch=1, grid=(r // tb,),
            in_specs=[pl.BlockSpec((4, tb, w), lambda i, sr: (0, i, 0)), own_spec],
            out_specs=pl.BlockSpec((tb, w), lambda i, sr: (i, 0))),
        out_shape=jax.ShapeDtypeStruct((r, w), F32),
        compiler_params=_params(("parallel",)),
    )(jnp.reshape(me_s, (1,)).astype(jnp.int32), buf, own)


def _adamw_math(w, g, m, v):
    mn = B1 * m + (1.0 - B1) * g
    vn = B2 * v + (1.0 - B2) * (g * g)
    m_hat = mn / (1.0 - B1 ** STEP)
    v_hat = vn / (1.0 - B2 ** STEP)
    return -LR * (m_hat / (jnp.sqrt(v_hat) + AEPS) + WD * w), mn, vn


def _adamw(w, g, m, v, name):
    r, c_ = w.shape
    tb = _row_block(r)
    if tb == r and r > 512:
        tb = 256

    def body(w_ref, g_ref, m_ref, v_ref, d_ref, mo_ref, vo_ref):
        d_ref[...], mo_ref[...], vo_ref[...] = _adamw_math(w_ref[...], g_ref[...], m_ref[...], v_ref[...])

    spec = pl.BlockSpec((tb, c_), lambda i: (i, 0))
    return pl.pallas_call(
        body, name=name, grid=(pl.cdiv(r, tb),),
        in_specs=[spec] * 4, out_specs=[spec] * 3,
        out_shape=[jax.ShapeDtypeStruct(w.shape, F32)] * 3,
        compiler_params=_params(("parallel",)),
    )(w, g, m, v)


def _adamw_rows(w, g_mine, g_sib, m, v, name):
    r = w.shape[0]
    tb = 256
    sub, lanes = w.shape[1:]
    nh = g_mine.shape[0] // tb
    nsteps = pl.cdiv(r, tb)
    assert nsteps <= 2 * nh and 4 * 3 + r <= 2 * nh * tb
    x, y, c = _place()
    place = jnp.stack([c, 4 * (2 * x + y)]).astype(jnp.int32)

    def body(p_ref, w_ref, mc_ref, sc_ref, mn_ref, sn_ref, m_ref, v_ref, go_ref, d_ref, mo_ref, vo_ref, buf):
        i = pl.program_id(0)
        for at, blk, mine_ref, sib_ref in ((0, i, mc_ref, sc_ref), (1, jnp.minimum(i + 1, 2 * nh - 1), mn_ref, sn_ref)):
            rows = jnp.where(blk // nh == p_ref[0], mine_ref[...], sib_ref[...])
            buf[tb * at:tb * (at + 1)] = rows.reshape(tb, sub, lanes)
        g = buf[pl.ds(p_ref[1], tb)]
        go_ref[...] = g
        d_ref[...], mo_ref[...], vo_ref[...] = _adamw_math(w_ref[...], g, m_ref[...], v_ref[...])

    def half_spec(ahead, sibling):
        def index(i, pr):
            half = (1 - pr[0]) if sibling else pr[0]
            return (jnp.clip(jnp.minimum(i + ahead, 2 * nh - 1) - nh * half, 0, nh - 1), 0)
        return pl.BlockSpec((tb, sub * lanes), index)

    spec = pl.BlockSpec((tb, sub, lanes), lambda i, pr: (i, 0, 0))
    return pl.pallas_call(
        body, name=name,
        grid_spec=pltpu.PrefetchScalarGridSpec(
            num_scalar_prefetch=1, grid=(nsteps,),
            in_specs=[spec, half_spec(0, False), half_spec(0, True), half_spec(1, False), half_spec(1, True),
                      spec, spec],
            out_specs=[spec] * 4,
            scratch_shapes=[pltpu.VMEM((2 * tb, sub, lanes), F32)]),
        out_shape=[jax.ShapeDtypeStruct(w.shape, F32)] * 4,
        compiler_params=_params(("parallel",)),
    )(place, w, g_mine, g_sib, g_mine, g_sib, m, v)


def _adamw_halves(w, g_mine, g_sib, m, v, name):
    r, c_ = w.shape
    r2 = g_mine.shape[0]
    tb = _row_block(r2)
    nb = r2 // tb
    c = lax.axis_index("c")

    def body(c_ref, w_ref, gm_ref, gs_ref, m_ref, v_ref, g_ref, d_ref, mo_ref, vo_ref):
        g = jnp.where(pl.program_id(0) == c_ref[0], gm_ref[...], gs_ref[...])
        g_ref[...] = g
        d_ref[...], mo_ref[...], vo_ref[...] = _adamw_math(w_ref[...], g, m_ref[...], v_ref[...])

    full = pl.BlockSpec((tb, c_), lambda h, i, cr: (h * nb + i, 0))
    half = pl.BlockSpec((tb, c_), lambda h, i, cr: (i, 0))
    return pl.pallas_call(
        body, name=name,
        grid_spec=pltpu.PrefetchScalarGridSpec(
            num_scalar_prefetch=1, grid=(2, nb),
            in_specs=[full, half, half, full, full], out_specs=[full] * 4),
        out_shape=[jax.ShapeDtypeStruct(w.shape, F32)] * 4,
        compiler_params=_params(("parallel", "parallel")),
    )(jnp.reshape(c, (1,)).astype(jnp.int32), w, g_mine, g_sib, m, v)


def kernel(x, meta_tokens, norm_g, w_in, b_f, w_out, final_g, loss_target, m_meta_tokens, m_norm_g, m_w_in, m_b_f, m_w_out, m_final_g, v_meta_tokens, v_norm_g, v_w_in, v_b_f, v_w_out, v_final_g):
    me_s = 2 * lax.axis_index("x") + lax.axis_index("y")
    w3, m3, v3 = [jnp.transpose(jnp.reshape(t[0], (D // C, C, WSH)), (2, 0, 1)) for t in (w_in, m_w_in, v_w_in)]

    wt_main, laps, _, *normed = _gather_weights(_own_window(w3), meta_tokens, x[0], norm_g)
    wft = jnp.pad(laps[3, :NFF], ((0, C - NFF), (0, 0)))
    mine = (jnp.arange(4) == me_s)[:, None, None]
    wout_own = w_out[0].astype(BF)

    def wout_full(landed):
        return jnp.where(mine, wout_own[None], landed).reshape(DMIX, D)

    loss, gx, dmeta, dng, gwt, dbf, dwout, dfg, (p_in, p_out), (e_in, e_out) = _local_step(
        x[0], loss_target[0], normed, norm_g, wt_main, wft, b_f, wout_own, final_g.reshape(1, D), True, wout_full)

    g_meta = jnp.stack([dmeta[:, 256 * s:256 * (s + 1)] for s in range(4)])
    small = jnp.concatenate([dng, dfg, jnp.pad(dbf, ((0, 0), (0, D - NFF))),
                             jnp.pad(jnp.reshape(loss, (1, 1)), ((0, 0), (0, D - 1))),
                             jnp.zeros((4, D), F32)], axis=0)
    e_meta, e_small = _chip_exchange([g_meta], small)
    h_in, h_out = _sum4(e_in, p_in, "sum_in"), _sum4(e_out, p_out, "sum_out")
    h_meta, h_small = _sum4(e_meta, g_meta, "sum_meta"), _sum4(e_small, small, "sum_small")
    s_in, s_out, s_meta, s_small = _pair_send([h_in, h_out, h_meta, h_small])
    gw_meta = _add2(h_meta, s_meta, "pair_add_meta")
    tot = _add2(h_small, s_small, "pair_add_small")
    g_norm, g_final, g_bf, loss_all = tot[0:1], tot[1], tot[2:3, :NFF], tot[3, 0]

    d_meta, nm_meta, nv_meta = _adamw(meta_tokens, gw_meta, m_meta_tokens, v_meta_tokens, "adamw_meta")
    d_norm, nm_norm, nv_norm = _adamw(norm_g, g_norm, m_norm_g, v_norm_g, "adamw_norm")
    outs_in = _adamw_rows(w3, h_in, s_in, m3, v3, "adamw_in")
    gw_in, d_in, nm_in, nv_in = [jnp.reshape(jnp.transpose(t, (1, 2, 0)), (1, D, WSH)) for t in outs_in]
    d_bf, nm_bf, nv_bf = _adamw(b_f, g_bf, m_b_f, v_b_f, "adamw_bf")
    gw_out, d_out, nm_out, nv_out = _adamw_halves(w_out[0], h_out, s_out, m_w_out[0], v_w_out[0], "adamw_out")
    d_fin, nm_fin, nv_fin = _adamw(final_g.reshape(1, D), g_final.reshape(1, D), m_final_g.reshape(1, D),
                                   v_final_g.reshape(1, D), "adamw_final")
    return (loss_all, gx[None], gw_meta, g_norm, gw_in, g_bf, gw_out[None], g_final,
            d_meta, d_norm, d_in, d_bf, d_out[None], d_fin.reshape(D),
            nm_meta, nm_norm, nm_in, nm_bf, nm_out[None], nm_fin.reshape(D),
            nv_meta, nv_norm, nv_in, nv_bf, nv_out[None], nv_fin.reshape(D))
```

```python
import numpy as np
import jax
import jax.numpy as jnp
from jax import lax
from jax.experimental import pallas as pl
from jax.experimental.pallas import tpu as pltpu

D = 1024
SEQ = 2048
NMETA = 16
C = 128
PAD = C - NMETA
T = PAD + NMETA + SEQ
NCH = T // C
RH, RDK, RDV = 4, 128, 256
FH, FD = 16, 64
NPAIR = FH // 2
WMAIN = 7168
NFF = 16
WIN = WMAIN + NFF
WSH = WIN // 4
WPADROWS = 1824
DMIX = 2048
EPS = 1e-6
NEG = -1e30
RSCALE = RDK ** -0.5
FSCALE = FD ** -0.5
ROPE_BASE = 10000.0
LR, B1, B2, AEPS, WD, STEP = 0.001, 0.9, 0.999, 1e-08, 0.01, 10

BF = jnp.bfloat16
F32 = jnp.float32
NT = (((1,), (1,)), ((), ()))
TN = (((0,), (0,)), ((), ()))
NN_DIMS = (((1,), (0,)), ((), ()))
MESH = pl.DeviceIdType.MESH
ANY = pl.BlockSpec(memory_space=pl.ANY)
VMEM_LIMIT = 48 * 1024 * 1024
DW_VMEM_LIMIT = 56 * 1024 * 1024

GB_R, GB_F = 2, 6
QB_F, KB_F, VB_F = 24, 32, 40


def _dot(a, b):
    return jnp.dot(a, b, preferred_element_type=F32)


def _dg(a, b, dims):
    return lax.dot_general(a, b, dims, preferred_element_type=F32)


def _params(sem=None):
    return pltpu.CompilerParams(dimension_semantics=sem, vmem_limit_bytes=VMEM_LIMIT)


def _constants():
    pos = jnp.arange(T, dtype=F32) - PAD
    inv = ROPE_BASE ** (-jnp.arange(0, RDK, 2, dtype=F32) / RDK)
    ang = pos[:, None] * inv[None, :]
    cos, sin = jnp.cos(ang), jnp.sin(ang)
    cos2 = jnp.concatenate([cos, cos], axis=1)
    sin2 = jnp.concatenate([-sin, sin], axis=1)
    log_gamma = jnp.log1p(-jnp.exp2(-5.0 - jnp.arange(RH, dtype=F32)))
    idx = jnp.arange(C, dtype=F32)
    diff = idx[:, None] - idx[None, :]
    dmask = jnp.where(diff[None] >= 0, jnp.exp(log_gamma[:, None, None] * jnp.maximum(diff, 0.0)[None]), 0.0)
    zeta = jnp.exp(log_gamma[:, None] * (C - 1.0 - idx)[None, :])
    xi = jnp.exp(log_gamma[:, None] * (idx + 1.0)[None, :])
    gdec = jnp.exp(log_gamma * C)
    zeta_b = jnp.broadcast_to(zeta[:, :, None], (RH, C, RDK))
    xi_b = jnp.broadcast_to(xi[:, :, None], (RH, C, RDK))
    gdec_b = jnp.broadcast_to(gdec[:, None, None], (RH, RDK, RDV))
    tri = jnp.asarray(np.tril(np.ones((C, C), np.float32)), dtype=BF)
    head_of_lane = np.arange(FH * FD) // FD
    pick = ((np.arange(FH * FD)[:, None] % FD == 0)
            & (head_of_lane[:, None] == np.arange(C)[None, :])).astype(np.float32)
    seg = (np.arange(C)[:, None] // FD == np.arange(C)[None, :] // FD).astype(np.float32)
    ones_aug = np.concatenate([np.tile((np.arange(C) < FD)[None, :], (C, 1)),
                               np.tile((np.arange(C) >= FD)[None, :], (C, 1))], axis=0).astype(np.float32)
    lane = np.arange(2 * C) % C
    causal = np.where(lane[None, :] <= np.arange(C)[:, None], 0.0, NEG).astype(np.float32)
    mask_bias = np.stack([np.zeros((C, 2 * C), np.float32), causal])
    return dict(cos2=cos2, sin2=sin2, dmask=dmask, zeta=zeta_b, xi=xi_b, gdec=gdec_b, tri=tri,
                mask_bias=jnp.asarray(mask_bias), pick=jnp.asarray(pick, dtype=BF), seg=jnp.asarray(seg, dtype=BF),
                ones_aug=jnp.asarray(ones_aug, dtype=BF))


def _norm_rows(h, g):
    return h * lax.rsqrt(jnp.mean(h * h, axis=1, keepdims=True) + EPS) * g


def _mm_nt(a, b, n, tm, tn, name):
    m, k = a.shape

    def body(a_ref, b_ref, o_ref):
        o_ref[...] = _dg(a_ref[...], b_ref[...], NT)

    return pl.pallas_call(
        body, name=name, grid=(m // tm, n // tn),
        in_specs=[pl.BlockSpec((tm, k), lambda i, j: (i, 0)), pl.BlockSpec((tn, k), lambda i, j: (j, 0))],
        out_specs=pl.BlockSpec((tm, tn), lambda i, j: (i, j)),
        out_shape=jax.ShapeDtypeStruct((m, n), F32),
        compiler_params=_params(("parallel", "parallel")),
    )(a, b)


def _mm_nn(a, b, tm, tn, name, out_dtype=F32):
    m, k = a.shape
    _, n = b.shape

    def body(a_ref, b_ref, o_ref):
        o_ref[...] = _dot(a_ref[...], b_ref[...]).astype(out_dtype)

    return pl.pallas_call(
        body, name=name, grid=(m // tm, n // tn),
        in_specs=[pl.BlockSpec((tm, k), lambda i, j: (i, 0)), pl.BlockSpec((k, tn), lambda i, j: (0, j))],
        out_specs=pl.BlockSpec((tm, tn), lambda i, j: (i, j)),
        out_shape=jax.ShapeDtypeStruct((m, n), out_dtype),
        compiler_params=_params(("parallel", "parallel")),
    )(a, b)


def _rot(x, cos2, sin2):
    return x * cos2 + pltpu.roll(x, 64, 1) * sin2


def _ret_specs(chunk):
    whole = lambda shape: pl.BlockSpec(shape, lambda n: (0,) * len(shape))
    return [
        pl.BlockSpec((C, RH * RDK), lambda n: (chunk(n), 0)),
        pl.BlockSpec((C, RH * RDK), lambda n: (chunk(n), 1)),
        pl.BlockSpec((C, RH * RDV), lambda n: (chunk(n), 1)),
        pl.BlockSpec((C, RDK), lambda n: (chunk(n), 0)),
        pl.BlockSpec((C, RDK), lambda n: (chunk(n), 0)),
        whole((RH, C, C)), whole((RH, C, RDK)), whole((RH, C, RDK)), whole((RH, RDK, RDV)),
    ]


def _ret_heads(q_ref, k_ref, v_ref, cos, sin):
    qr = [_rot(q_ref[:, RDK * h:RDK * (h + 1)], cos, sin) for h in range(RH)]
    kr = [_rot(k_ref[:, RDK * h:RDK * (h + 1)], cos, sin) * RSCALE for h in range(RH)]
    vb = [v_ref[:, RDV * h:RDV * (h + 1)].astype(BF) for h in range(RH)]
    return qr, kr, [t.astype(BF) for t in qr], [t.astype(BF) for t in kr], vb


def _ret_fwd(z, cst):
    def body(q_ref, k_ref, v_ref, cos_ref, sin_ref, dm_ref, xi_ref, zt_ref, gd_ref, r_ref, sp_ref, st):
        n = pl.program_id(0)

        @pl.when(n == 0)
        def _():
            st[...] = jnp.zeros_like(st)

        hs = range(RH)
        qr, kr, qb, kb, vb = _ret_heads(q_ref, k_ref, v_ref, cos_ref[...], sin_ref[...])
        sd = [(_dg(qb[h], kb[h], NT) * dm_ref[h]).astype(BF) for h in hs]
        state = [st[h] for h in hs]
        qx = [(qr[h] * xi_ref[h]).astype(BF) for h in hs]
        kz = [(kr[h] * zt_ref[h]).astype(BF) for h in hs]
        out = [_dot(sd[h], vb[h]) + _dot(qx[h], state[h].astype(BF)) for h in hs]
        kv = [_dg(kz[h], vb[h], TN) for h in hs]
        for h in hs:
            sp_ref[0, h] = state[h]
            r_ref[:, RDV * h:RDV * (h + 1)] = out[h]
            st[h] = state[h] * gd_ref[h] + kv[h]

    return pl.pallas_call(
        body, name="ret_fwd", grid=(NCH,),
        in_specs=_ret_specs(lambda n: n),
        out_specs=[pl.BlockSpec((C, RH * RDV), lambda n: (n, 0)),
                   pl.BlockSpec((1, RH, RDK, RDV), lambda n: (n, 0, 0, 0))],
        out_shape=[jax.ShapeDtypeStruct((T, RH * RDV), F32), jax.ShapeDtypeStruct((NCH, RH, RDK, RDV), F32)],
        scratch_shapes=[pltpu.VMEM((RH, RDK, RDV), F32)],
        compiler_params=_params(("arbitrary",)),
    )(z, z, z, cst["cos2"], cst["sin2"], cst["dmask"], cst["xi"], cst["zeta"], cst["gdec"])


def _ret_bwd(z, cst, sprev, dr):
    def body(q_ref, k_ref, v_ref, cos_ref, sin_ref, dm_ref, xi_ref, zt_ref, gd_ref, sp_ref, dr_ref,
             dq_ref, dk_ref, dv_ref, gst):
        i = pl.program_id(0)

        @pl.when(i == 0)
        def _():
            gst[...] = jnp.zeros_like(gst)

        hs = range(RH)
        cos, sin = cos_ref[...], sin_ref[...]
        qr, kr, qb, kb, vb = _ret_heads(q_ref, k_ref, v_ref, cos, sin)
        dm = [dm_ref[h] for h in hs]
        xi = [xi_ref[h] for h in hs]
        zt = [zt_ref[h] for h in hs]
        sd = [(_dg(qb[h], kb[h], NT) * dm[h]).astype(BF) for h in hs]
        qx = [(qr[h] * xi[h]).astype(BF) for h in hs]
        kz = [(kr[h] * zt[h]).astype(BF) for h in hs]
        drb = [dr_ref[:, RDV * h:RDV * (h + 1)] for h in hs]
        sb = [sp_ref[0, h].astype(BF) for h in hs]
        g = [gst[h] for h in hs]
        gb = [t.astype(BF) for t in g]
        ds = [(_dg(drb[h], vb[h], NT) * dm[h]).astype(BF) for h in hs]
        dq = [_dot(ds[h], kb[h]) + _dg(drb[h], sb[h], NT) * xi[h] for h in hs]
        dk = [(_dg(ds[h], qb[h], TN) + _dg(vb[h], gb[h], NT) * zt[h]) * RSCALE for h in hs]
        dv = [_dg(sd[h], drb[h], TN) + _dot(kz[h], gb[h]) for h in hs]
        gn = [g[h] * gd_ref[h] + _dg(qx[h], drb[h], TN) for h in hs]
        for h in hs:
            gst[h] = gn[h]
            dq_ref[:, RDK * h:RDK * (h + 1)] = (dq[h] * cos + pltpu.roll(dq[h] * sin, 64, 1)).astype(BF)
            dk_ref[:, RDK * h:RDK * (h + 1)] = (dk[h] * cos + pltpu.roll(dk[h] * sin, 64, 1)).astype(BF)
            dv_ref[:, RDV * h:RDV * (h + 1)] = dv[h].astype(BF)

    rev = lambda n: NCH - 1 - n
    return pl.pallas_call(
        body, name="ret_bwd", grid=(NCH,),
        in_specs=_ret_specs(rev) + [
            pl.BlockSpec((1, RH, RDK, RDV), lambda n: (rev(n), 0, 0, 0)),
            pl.BlockSpec((C, RH * RDV), lambda n: (rev(n), 0)),
        ],
        out_specs=[pl.BlockSpec((C, RH * RDK), lambda n: (rev(n), 0)),
                   pl.BlockSpec((C, RH * RDK), lambda n: (rev(n), 0)),
                   pl.BlockSpec((C, RH * RDV), lambda n: (rev(n), 0))],
        out_shape=[jax.ShapeDtypeStruct((T, RH * RDK), BF), jax.ShapeDtypeStruct((T, RH * RDK), BF),
                   jax.ShapeDtypeStruct((T, RH * RDV), BF)],
        scratch_shapes=[pltpu.VMEM((RH, RDK, RDV), F32)],
        compiler_params=_params(("arbitrary",)),
    )(z, z, z, cst["cos2"], cst["sin2"], cst["dmask"], cst["xi"], cst["zeta"], cst["gdec"], sprev, dr)


def _place():
    x, y, c = lax.axis_index("x"), lax.axis_index("y"), lax.axis_index("c")
    return x, y, c


def _other_chips(x, y):
    return [(1 - x, y, 2 * (1 - x) + y), (x, 1 - y, 2 * x + (1 - y)), (1 - x, 1 - y, 2 * (1 - x) + (1 - y))]


def _chip_copies(srcs, lands, send_sems, recv_sems, by_dest):
    x, y, c = _place()
    me_s = 2 * x + y
    return [pltpu.make_async_remote_copy(
        src_ref=src.at[cs] if by_dest else src, dst_ref=land.at[me_s],
        send_sem=send_sems.at[3 * a + j], recv_sem=recv_sems.at[3 * a + j],
        device_id=(cx, cy, c), device_id_type=MESH)
        for a, (src, land) in enumerate(zip(srcs, lands)) for j, (cx, cy, cs) in enumerate(_other_chips(x, y))]


def _split_dot(x, mat01, dims=NN_DIMS, x_first=True):
    acc, rest = None, x
    for _ in range(3):
        piece = rest.astype(BF)
        part = _dg(piece, mat01, dims) if x_first else _dg(mat01, piece, dims)
        acc = part if acc is None else acc + part
        rest = rest - piece.astype(F32)
    return acc


def _log_sigmoid(x):
    return -(jnp.maximum(-x, 0.0) + jnp.log1p(jnp.exp(-jnp.abs(x))))


def _fox_prep(zf, bf_pad, cst):
    def body(zf_ref, b_ref, tri_ref, ct_ref, carry):
        n = pl.program_id(0)

        @pl.when(n == 0)
        def _():
            carry[...] = jnp.zeros_like(carry)

        ls = _log_sigmoid(zf_ref[...] + b_ref[...])
        row = n * C + lax.broadcasted_iota(jnp.int32, (C, C), 0)
        lf = jnp.where(row >= PAD, ls, 0.0)
        cc = _split_dot(lf, tri_ref[...], x_first=False) + carry[0:1, :]
        carry[...] = jnp.broadcast_to(cc[C - 1:C, :], carry.shape)
        pos = n * C + lax.broadcasted_iota(jnp.int32, (FH, C), 1)
        ct_ref[0] = jnp.where(pos >= PAD, cc.T[:FH, :], -NEG)

    return pl.pallas_call(
        body, name="fox_prep", grid=(NCH,),
        in_specs=[pl.BlockSpec((C, C), lambda n: (n, 0)), pl.BlockSpec((1, C), lambda n: (0, 0)),
                  pl.BlockSpec((C, C), lambda n: (0, 0))],
        out_specs=pl.BlockSpec((1, FH, C), lambda n: (n, 0, 0)),
        out_shape=jax.ShapeDtypeStruct((NCH, FH, C), F32),
        scratch_shapes=[pltpu.VMEM((8, C), F32)],
        compiler_params=_params(("arbitrary",)),
    )(zf, bf_pad, cst["tri"])


def _lo_lanes(shape):
    return lax.broadcasted_iota(jnp.int32, shape, 1) < FD


def _split_heads(x):
    lo = _lo_lanes(x.shape)
    zero = jnp.zeros_like(x)
    return jnp.concatenate([jnp.where(lo, x, zero), jnp.where(lo, zero, x)], axis=0)


def _spread2(x):
    lo = _lo_lanes(x.shape)
    r = pltpu.roll(x, FD, 1)
    return jnp.concatenate([jnp.where(lo, x, r), jnp.where(lo, r, x)], axis=1)


NSTEP = (NCH + 1) // 2
NTILE = NCH + 1
TROWS = T + C


def _fox_tile(s, t):
    second = t > s
    return second.astype(jnp.int32), jnp.where(second, t - s - 1, s - t)


def _fox_pos(i):
    return jnp.where(i < NSTEP, 2 * i, 2 * (NCH - 1 - i) + 1)


FOX_ORDER = [2 * i if i < NSTEP else 2 * (NCH - 1 - i) + 1 for i in range(NCH)]


def _fox_pair_specs():
    first = pl.BlockSpec((C, C), lambda p, s: (2 * s, p))
    second = pl.BlockSpec((C, C), lambda p, s: (jnp.where(s == NSTEP - 1, 2 * s, 2 * s + 1), p))
    both = pl.BlockSpec((2 * C, C), lambda p, s: (s, p))
    return first, second, both


def _fox_q_specs():
    return (pl.BlockSpec((C, C), lambda p, s: (s, QB_F + p)),
            pl.BlockSpec((C, C), lambda p, s: (NCH - 1 - s, QB_F + p)))


def _fox_key_bias(ct_ref, p, j):
    return jnp.concatenate([ct_ref[j, pl.ds(2 * p, 1), :], ct_ref[j, pl.ds(2 * p + 1, 1), :]], axis=1)


def _fox_kv(z_ref, kbuf, vbuf, sems, p):
    def copies(pair, slot):
        return [pltpu.make_async_copy(z_ref.at[:, pl.ds(pl.multiple_of((first + pair) * C, C), C)], buf.at[slot],
                                      sems.at[i, slot]) for i, (first, buf) in enumerate(((KB_F, kbuf), (VB_F, vbuf)))]

    slot = p % 2

    @pl.when(p == 0)
    def _():
        for cp in copies(0, 0):
            cp.start()

    for cp in copies(p, slot):
        cp.wait()

    @pl.when(p + 1 < NPAIR)
    def _():
        for cp in copies(p + 1, 1 - slot):
            cp.start()

    return slot


def _fox_fwd(z, ct, cst, share):
    n = 0 if share is None else 1

    def body(qa_ref, qb_ref, z_ref, ct_ref, ones_ref, mb_ref, *rest):
        share_refs, (a_ref, g_ref), land_refs = rest[:n], rest[n:n + 2], rest[n + 2:2 * n + 2]
        kks, vvs, q2, m2, sbuf, kbuf, vbuf, kv_sems = rest[2 * n + 2:2 * n + 10]
        p, s = pl.program_id(0), pl.program_id(1)
        if n:
            copies = _chip_copies(share_refs, land_refs, *rest[2 * n + 10:], by_dest=False)

            @pl.when((p == 0) & (s == 0))
            def _():
                for cp in copies:
                    cp.start()

            @pl.when((p == NPAIR - 1) & (s == NSTEP - 1))
            def _():
                for cp in copies:
                    cp.wait()

        @pl.when(s == 0)
        def _():
            ones = ones_ref[...]
            slot = _fox_kv(z_ref, kbuf, vbuf, kv_sems, p)

            def prep(j, carry):
                rows = pl.ds(pl.multiple_of(j * C, C), C)
                kks[j] = _split_heads(kbuf[slot, rows, :]).astype(BF)
                vvs[j] = jnp.concatenate([_split_heads(vbuf[slot, rows, :]).astype(BF), ones], axis=1)
                return carry

            lax.fori_loop(0, NCH, prep, 0)

        q2[0] = (qa_ref[...] * FSCALE).astype(BF)
        q2[1] = (qb_ref[...] * FSCALE).astype(BF)

        tiles = [_fox_tile(s, t) for t in range(NTILE)]
        causal = mb_ref[1]
        neg = jnp.full((C, 2 * C), NEG, F32)
        run, first = neg, neg
        for t, (sel, j) in enumerate(tiles):
            st = _dg(q2[sel], kks[j], NT) - _fox_key_bias(ct_ref, p, j)
            if t in (0, NTILE - 1):
                st = st + causal
            sbuf[t] = st
            run = jnp.maximum(jnp.where(t == s + 1, neg, run), st)
            first = jnp.where(t == s, run, first)
        for w, mx in enumerate((first, run)):
            m2[w] = jnp.concatenate(
                [jnp.broadcast_to(jnp.max(mx[:, :C], axis=1, keepdims=True), (C, C)),
                 jnp.broadcast_to(jnp.max(mx[:, C:], axis=1, keepdims=True), (C, C))], axis=1)

        zero = jnp.zeros((C, 2 * C), F32)
        run, first = zero, zero
        for t, (sel, j) in enumerate(tiles):
            run = jnp.where(t == s + 1, zero, run) + _dot(jnp.exp(sbuf[t] - m2[sel]).astype(BF), vvs[j])
            first = jnp.where(t == s, run, first)
        lo = _lo_lanes((C, C))
        for w, res in enumerate((first, run)):
            l = res[:, C:]
            a_ref[C * w:C * (w + 1), :] = res[:, :C] / l
            mw = m2[w]
            g_ref[C * w:C * (w + 1), :] = -(jnp.where(lo, mw[:, :C], mw[:, C:]) + jnp.log(l))

    qa, qb = _fox_q_specs()
    both = _fox_pair_specs()[2]
    return pl.pallas_call(
        body, name="fox_fwd", grid=(NPAIR, NSTEP),
        in_specs=[qa, qb, ANY,
                  pl.BlockSpec((NCH, FH, C), lambda p, s: (0, 0, 0)),
                  pl.BlockSpec((2 * C, C), lambda p, s: (0, 0)),
                  pl.BlockSpec((2, C, 2 * C), lambda p, s: (0, 0, 0))] + [ANY] * n,
        out_specs=[both, both] + [ANY] * n,
        out_shape=[jax.ShapeDtypeStruct((TROWS, FH * FD), F32)] * 2
        + ([jax.ShapeDtypeStruct((4,) + share.shape, share.dtype)] if n else []),
        scratch_shapes=[pltpu.VMEM((NCH, 2 * C, C), BF), pltpu.VMEM((NCH, 2 * C, 2 * C), BF),
                        pltpu.VMEM((2, C, C), BF), pltpu.VMEM((2, C, 2 * C), F32),
                        pltpu.VMEM((NTILE, C, 2 * C), F32),
                        pltpu.VMEM((2, T, C), F32), pltpu.VMEM((2, T, C), F32), pltpu.SemaphoreType.DMA((2, 2))]
        + [pltpu.SemaphoreType.DMA((3,)), pltpu.SemaphoreType.DMA((3,))] * n,
        compiler_params=_params(("arbitrary", "arbitrary")),
    )(z, z, z, ct, cst["ones_aug"], cst["mask_bias"], *([share] * n))


def _fox_bwd(z, da, g, delta, ct, cst, parts=()):
    grp = 9

    n = len(parts)

    def body(qa_ref, qb_ref, daa_ref, dab_ref, ga_ref, gb_ref, dla_ref, dlb_ref, z_ref, ct_ref, ones_ref,
             mb_ref, *rest):
        part_refs, (dq_ref, dr_ref, dk_ref, dv_ref, dcs_ref), land_refs = rest[:n], rest[n:n + 5], rest[n + 5:2 * n + 5]
        (kks, vvs, q2, qq2, dd2, da2, gi2, dl2, dq2, dvb, dkb, dkacc, dvacc, csacc, kbuf, vbuf,
         kv_sems) = rest[2 * n + 5:2 * n + 22]
        p, s = pl.program_id(0), pl.program_id(1)
        ones = ones_ref[...]
        if n:
            copies = _chip_copies(part_refs, land_refs, *rest[2 * n + 22:], by_dest=True)

            @pl.when((p == 0) & (s == 0))
            def _():
                for cp in copies:
                    cp.start()

            @pl.when((p == NPAIR - 1) & (s == NSTEP - 1))
            def _():
                for cp in copies:
                    cp.wait()

        @pl.when(s == 0)
        def _():
            dkacc[...] = jnp.zeros_like(dkacc)
            dvacc[...] = jnp.zeros_like(dvacc)
            csacc[...] = jnp.zeros_like(csacc)
            slot = _fox_kv(z_ref, kbuf, vbuf, kv_sems, p)

            def prep(j, carry):
                rows = pl.ds(pl.multiple_of(j * C, C), C)
                kks[j] = _split_heads(kbuf[slot, rows, :]).astype(BF)
                vvs[j] = _split_heads(vbuf[slot, rows, :]).astype(BF)
                return carry

            lax.fori_loop(0, NCH, prep, 0)

        for w, (q_ref, d_ref, g_ref, l_ref) in enumerate(((qa_ref, daa_ref, ga_ref, dla_ref),
                                                          (qb_ref, dab_ref, gb_ref, dlb_ref))):
            qf = q_ref[...]
            q2[w] = (qf * FSCALE).astype(BF)
            qq2[w] = jnp.concatenate([_split_heads(qf).astype(BF), ones], axis=1)
            da2[w] = d_ref[...]
            dd2[w] = _split_heads(d_ref[...].astype(F32)).astype(BF)
            gi2[w] = _spread2(g_ref[...])
            dl2[w] = _spread2(l_ref[...])
        dq2[...] = jnp.zeros_like(dq2)
        zero = jnp.zeros((C, 2 * C), F32)

        def group(gi, carry):
            ts = [gi * grp + u for u in range(grp)]
            tiles = [_fox_tile(s, t) for t in ts]
            kk = [kks[j] for _, j in tiles]
            ss = [_dg(q2[sel], kj, NT) + (gi2[sel] - _fox_key_bias(ct_ref, p, j)) for kj, (sel, j) in zip(kk, tiles)]
            ss[0] = ss[0] + mb_ref[(gi == 0).astype(jnp.int32)]
            ss[-1] = ss[-1] + mb_ref[(gi == 1).astype(jnp.int32)]
            dps = [_dg(da2[sel], vvs[j], NT) for sel, j in tiles]
            pes = [jnp.exp(st) for st in ss]
            dss = [pe * (dp - dl2[sel]) * FSCALE for pe, dp, (sel, _) in zip(pes, dps, tiles)]
            pts = [jnp.concatenate([pe[:, :C].T, pe[:, C:].T], axis=1).astype(BF) for pe in pes]
            dsts = [jnp.concatenate([ds[:, :C].T, ds[:, C:].T], axis=1).astype(BF) for ds in dss]
            dvs = [_dot(pt, dd2[sel]) for pt, (sel, _) in zip(pts, tiles)]
            rs = [_dot(dst, qq2[sel]) for dst, (sel, _) in zip(dsts, tiles)]
            parts = [_dot(ds.astype(BF), jnp.concatenate([kj, ones], axis=1)) for ds, kj in zip(dss, kk)]
            for t, dv, rr in zip(ts, dvs, rs):
                dvb[t] = dv
                dkb[t] = rr
            pa, pb = zero, zero
            for t, part in zip(ts, parts):
                pa = pa + jnp.where(t <= s, part, zero)
                pb = pb + jnp.where(t <= s, zero, part)
            dq2[0] += pa
            dq2[1] += pb
            return carry

        ntile = jnp.where(s == NSTEP - 1, grp, NTILE)
        lax.fori_loop(0, ntile // grp, group, 0)

        def scatter(t, carry):
            _, j = _fox_tile(s, t)
            r = pl.ds(pl.multiple_of(j * C, C), C)
            dvacc[r, :] += dvb[t]
            dkacc[r, :] += dkb[t, :, :C]
            csacc[r, :] += dkb[t, :, C:]
            return carry

        lax.fori_loop(0, ntile, scatter, 0)
        for w in range(2):
            res = dq2[w]
            dq_ref[C * w:C * (w + 1), :] = res[:, :C].astype(BF)
            dr_ref[C * w:C * (w + 1), :] = res[:, C:]

        @pl.when(s == NSTEP - 1)
        def _():
            dk_ref[...] = dkacc[...].astype(BF)
            dv_ref[...] = dvacc[...].astype(BF)
            dcs_ref[...] = csacc[...]

    qa, qb = _fox_q_specs()
    ba, bb, both = _fox_pair_specs()
    col = pl.BlockSpec((T, C), lambda p, s: (0, p))
    return pl.pallas_call(
        body, name="fox_bwd", grid=(NPAIR, NSTEP),
        in_specs=[qa, qb, ba, bb, ba, bb, ba, bb, ANY,
                  pl.BlockSpec((NCH, FH, C), lambda p, s: (0, 0, 0)),
                  pl.BlockSpec((2 * C, C), lambda p, s: (0, 0)),
                  pl.BlockSpec((2, C, 2 * C), lambda p, s: (0, 0, 0))] + [ANY] * n,
        out_specs=[both, both, col, col, col] + [ANY] * n,
        out_shape=[jax.ShapeDtypeStruct((TROWS, FH * FD), BF), jax.ShapeDtypeStruct((TROWS, FH * FD), F32),
                   jax.ShapeDtypeStruct((T, FH * FD), BF), jax.ShapeDtypeStruct((T, FH * FD), BF),
                   jax.ShapeDtypeStruct((T, FH * FD), F32)]
        + [jax.ShapeDtypeStruct(p.shape, p.dtype) for p in parts],
        scratch_shapes=[pltpu.VMEM((NCH, 2 * C, C), BF), pltpu.VMEM((NCH, 2 * C, C), BF),
                        pltpu.VMEM((2, C, C), BF), pltpu.VMEM((2, 2 * C, 2 * C), BF), pltpu.VMEM((2, 2 * C, C), BF),
                        pltpu.VMEM((2, C, C), BF), pltpu.VMEM((2, C, 2 * C), F32), pltpu.VMEM((2, C, 2 * C), F32),
                        pltpu.VMEM((2, C, 2 * C), F32),
                        pltpu.VMEM((NTILE, C, C), F32), pltpu.VMEM((NTILE, C, 2 * C), F32),
                        pltpu.VMEM((T, C), F32), pltpu.VMEM((T, C), F32), pltpu.VMEM((T, C), F32),
                        pltpu.VMEM((2, T, C), F32), pltpu.VMEM((2, T, C), F32), pltpu.SemaphoreType.DMA((2, 2))]
        + ([pltpu.SemaphoreType.DMA((3 * n,)), pltpu.SemaphoreType.DMA((3 * n,))] if n else []),
        compiler_params=_params(("arbitrary", "arbitrary")),
    )(z, z, da, da, g, g, delta, delta, z, ct, cst["ones_aug"], cst["mask_bias"], *parts)


def _fox_gate_bwd(drow, dcol, zf, bf_pad, cst):
    def body(dr_ref, dc_ref, zf_ref, b_ref, tri_ref, pick_ref, dff_ref, db_ref, carry):
        s = pl.program_id(0)
        n = NCH - 1 - s

        @pl.when(s == 0)
        def _():
            carry[...] = jnp.zeros_like(carry)
            db_ref[...] = jnp.zeros_like(db_ref)

        dcb = _split_dot((dr_ref[...] - dc_ref[...]) * (1.0 / FSCALE), pick_ref[...])
        suf = _split_dot(dcb, tri_ref[...], TN, x_first=False) + carry[0:1, :]
        carry[...] = jnp.broadcast_to(suf[0:1, :], carry.shape)
        x = zf_ref[...] + b_ref[...]
        row = n * C + lax.broadcasted_iota(jnp.int32, (C, C), 0)
        dff = jnp.where(row >= PAD, suf * (1.0 - jax.nn.sigmoid(x)), 0.0)
        dff_ref[...] = dff.astype(BF)
        db_ref[...] += jnp.sum(dff, axis=0, keepdims=True)

    rev = lambda s: (NCH - 1 - s, 0)
    return pl.pallas_call(
        body, name="fox_gate_bwd", grid=(NCH,),
        in_specs=[pl.BlockSpec((C, FH * FD), lambda s: (_fox_pos(NCH - 1 - s), 0)),
                  pl.BlockSpec((C, FH * FD), rev), pl.BlockSpec((C, C), rev),
                  pl.BlockSpec((1, C), lambda s: (0, 0)), pl.BlockSpec((C, C), lambda s: (0, 0)),
                  pl.BlockSpec((FH * FD, C), lambda s: (0, 0))],
        out_specs=[pl.BlockSpec((C, C), rev), pl.BlockSpec((1, C), lambda s: (0, 0))],
        out_shape=[jax.ShapeDtypeStruct((T, C), BF), jax.ShapeDtypeStruct((1, C), F32)],
        scratch_shapes=[pltpu.VMEM((8, C), F32)],
        compiler_params=_params(("arbitrary",)),
    )(drow, dcol, zf, bf_pad, cst["tri"], cst["pick"])


def _head_norm(r):
    rn, rs = [], []
    for h in range(RH):
        rh = r[:, RDV * h:RDV * (h + 1)]
        s = lax.rsqrt(jnp.mean(rh * rh, axis=1, keepdims=True) + EPS)
        rn.append(rh * s)
        rs.append(s)
    return jnp.concatenate(rn, axis=1), rs


def _gated(r, rg, a, fg):
    rn, _ = _head_norm(r)
    return jnp.concatenate([rn * (rg * jax.nn.sigmoid(rg)), a * (fg * jax.nn.sigmoid(fg))], axis=1)


def _out_loss(r, z, a, wout, x, tgt, fgain):
    def body(r_ref, rg_ref, a_ref, fg_ref, w_ref, x_ref, t_ref, g_ref, yt_ref, do_ref, dob_ref, loss_ref, dg_ref):
        i = pl.program_id(0)

        @pl.when(i == 0)
        def _():
            yt_ref[...] = jnp.zeros_like(yt_ref)
            do_ref[...] = jnp.zeros_like(do_ref)
            dob_ref[...] = jnp.zeros_like(dob_ref)
            loss_ref[...] = jnp.zeros_like(loss_ref)
            dg_ref[...] = jnp.zeros_like(dg_ref)

        @pl.when(i > 0)
        def _():
            y = _gated(r_ref[...], rg_ref[...], a_ref[...], fg_ref[...])
            yt_ref[...] = y.T.astype(BF)
            o = x_ref[...] + _dot(y.astype(BF), w_ref[...])
            rs = lax.rsqrt(jnp.mean(o * o, axis=1, keepdims=True) + EPS)
            on = o * rs
            g = g_ref[...]
            e = on * g - t_ref[...]
            loss_ref[...] += 0.5 * jnp.sum(jnp.mean(e * e, axis=1, keepdims=True))
            dyh = e * (1.0 / D)
            dg_ref[...] += jnp.sum(dyh * on, axis=0, keepdims=True)
            don = dyh * g
            do = rs * (don - on * jnp.mean(don * on, axis=1, keepdims=True))
            do_ref[...] = do
            dob_ref[...] = do.astype(BF)

    tok = lambda i: (jnp.maximum(i - 1, 0), 0)
    return pl.pallas_call(
        body, name="out_loss", grid=(NCH,),
        in_specs=[pl.BlockSpec((C, D), lambda i: (i, 0)), pl.BlockSpec((C, D), lambda i: (i, GB_R)),
                  pl.BlockSpec((C, D), lambda i: (_fox_pos(i), 0)), pl.BlockSpec((C, D), lambda i: (i, GB_F)),
                  pl.BlockSpec((DMIX, D), lambda i: (0, 0)),
                  pl.BlockSpec((C, D), tok), pl.BlockSpec((C, D), tok), pl.BlockSpec((1, D), lambda i: (0, 0))],
        out_specs=[pl.BlockSpec((DMIX, C), lambda i: (0, i)), pl.BlockSpec((C, D), lambda i: (i, 0)),
                   pl.BlockSpec((C, D), lambda i: (i, 0)), pl.BlockSpec((8, C), lambda i: (0, 0)),
                   pl.BlockSpec((1, D), lambda i: (0, 0))],
        out_shape=[jax.ShapeDtypeStruct((DMIX, T), BF), jax.ShapeDtypeStruct((T, D), F32),
                   jax.ShapeDtypeStruct((T, D), BF), jax.ShapeDtypeStruct((8, C), F32),
                   jax.ShapeDtypeStruct((1, D), F32)],
        compiler_params=_params(("arbitrary",)),
    )(r, z, a, z, wout, x, tgt, fgain)


def _silu_and_grad(x):
    s = jax.nn.sigmoid(x)
    return x * s, s * (1.0 + x * (1.0 - s))


def _dy_gate_bwd(dob, wout, r, z, a, seg):
    def body(do_ref, w_ref, r_ref, rg_ref, a_ref, fg_ref, seg_ref, dr_ref, da_ref, drg_ref, dfg_ref, dl_ref):
        dy = _dg(do_ref[...], w_ref[...], NT)
        a_ = a_ref[...]
        rn, rs = _head_norm(r_ref[...])
        silu_rg, dsilu_rg = _silu_and_grad(rg_ref[...])
        silu_fg, dsilu_fg = _silu_and_grad(fg_ref[...])
        dyr, dyf = dy[:, :D], dy[:, D:]
        drn = dyr * silu_rg
        drg_ref[...] = (dyr * rn * dsilu_rg).astype(BF)
        for h in range(RH):
            sl = slice(RDV * h, RDV * (h + 1))
            dh, nh = drn[:, sl], rn[:, sl]
            dr_ref[:, sl] = (rs[h] * (dh - nh * jnp.mean(dh * nh, axis=1, keepdims=True))).astype(BF)
        dab = (dyf * silu_fg).astype(BF)
        da_ref[...] = dab
        dfg_ref[...] = (dyf * a_ * dsilu_fg).astype(BF)
        prod = dab.astype(F32) * a_
        segm = seg_ref[...]
        for p in range(NPAIR):
            sl = slice(C * p, C * (p + 1))
            hi = prod[:, sl].astype(BF)
            lo = (prod[:, sl] - hi.astype(F32)).astype(BF)
            dl_ref[:, sl] = _dot(hi, segm) + _dot(lo, segm)

    row = pl.BlockSpec((C, D), lambda i: (i, 0))
    fox = pl.BlockSpec((C, D), lambda i: (_fox_pos(i), 0))
    return pl.pallas_call(
        body, name="dy_gate_bwd", grid=(NCH,),
        in_specs=[row, pl.BlockSpec((DMIX, D), lambda i: (0, 0)),
                  row, pl.BlockSpec((C, D), lambda i: (i, GB_R)),
                  fox, pl.BlockSpec((C, D), lambda i: (i, GB_F)),
                  pl.BlockSpec((C, C), lambda i: (0, 0))],
        out_specs=[row, fox, row, row, fox],
        out_shape=[jax.ShapeDtypeStruct((T, D), BF), jax.ShapeDtypeStruct((TROWS, D), BF),
                   jax.ShapeDtypeStruct((T, D), BF), jax.ShapeDtypeStruct((T, D), BF),
                   jax.ShapeDtypeStruct((TROWS, D), F32)],
        compiler_params=_params(("parallel",)),
    )(dob, wout, r, z, a, z, seg)


DZ_WIDTHS = (512, 512, 1024, 1024, 1024, 1024, 1024, 1024)


def _du_norm_bwd(dzs, dzf, wt, wft, hpad, g, dopad, parts=()):
    tm, tk = 544, 1024
    nk = WMAIN // tk
    ni = T // tm
    n = len(parts)

    def body(rq_ref, rk_ref, rv_ref, rg_ref, fq_ref, fk_ref, fv_ref, fg_ref, dzf_ref, w_ref, wf_ref, h_ref, g_ref,
             do_ref, *rest):
        part_refs, (gh_ref, dg_ref), land_refs = rest[:n], rest[n:n + 2], rest[n + 2:2 * n + 2]
        acc = rest[2 * n + 2]
        i, k = pl.program_id(0), pl.program_id(1)

        if n:
            send_sems, recv_sems = rest[2 * n + 3:]
            copies = _chip_copies(part_refs, land_refs, send_sems, recv_sems, by_dest=True)

            @pl.when((i == 0) & (k == 0))
            def _():
                for cp in copies:
                    cp.start()

            @pl.when((i == ni - 1) & (k == nk - 1))
            def _():
                for cp in copies:
                    cp.wait()

        @pl.when(k == 0)
        def _():
            acc[...] = (_dot(dzf_ref[...], wf_ref[...]) + _dot(rq_ref[...], w_ref[:512, :])
                        + _dot(rk_ref[...], w_ref[512:, :]))

        for kk, piece in enumerate((rv_ref, rg_ref, fq_ref, fk_ref, fv_ref, fg_ref), start=1):
            @pl.when(k == kk)
            def _(piece=piece):
                acc[...] += _dot(piece[...], w_ref[...])

        @pl.when(k == nk - 1)
        def _():
            du = acc[...]
            h = h_ref[...]
            gg = g_ref[...]
            rs = lax.rsqrt(jnp.mean(h * h, axis=1, keepdims=True) + EPS)
            hn = h * rs
            part = jnp.sum(du * hn, axis=0, keepdims=True)

            @pl.when(i == 0)
            def _():
                dg_ref[...] = part

            @pl.when(i > 0)
            def _():
                dg_ref[...] += part

            dhn = du * gg
            gh_ref[...] = rs * (dhn - hn * jnp.mean(dhn * hn, axis=1, keepdims=True)) + do_ref[...]

    sems = [pltpu.SemaphoreType.DMA((3 * n,)), pltpu.SemaphoreType.DMA((3 * n,))] if n else []
    return pl.pallas_call(
        body, name="du_norm_bwd", grid=(ni, nk),
        in_specs=[pl.BlockSpec((tm, w), lambda i, k: (i, 0)) for w in DZ_WIDTHS]
        + [pl.BlockSpec((tm, C), lambda i, k: (i, 0)),
           pl.BlockSpec((tk, D), lambda i, k: (k, 0)), pl.BlockSpec((C, D), lambda i, k: (0, 0)),
           pl.BlockSpec((tm, D), lambda i, k: (i, 0)), pl.BlockSpec((1, D), lambda i, k: (0, 0)),
           pl.BlockSpec((tm, D), lambda i, k: (i, 0))] + [ANY] * n,
        out_specs=[pl.BlockSpec((tm, D), lambda i, k: (i, 0)), pl.BlockSpec((1, D), lambda i, k: (0, 0))] + [ANY] * n,
        out_shape=[jax.ShapeDtypeStruct((T, D), F32), jax.ShapeDtypeStruct((1, D), F32)]
        + [jax.ShapeDtypeStruct(p.shape, p.dtype) for p in parts],
        scratch_shapes=[pltpu.VMEM((tm, D), F32)] + sems,
        compiler_params=_params(("arbitrary", "arbitrary")),
    )(*dzs, dzf, wt, wft, hpad, g, dopad, *parts)


GROWS = 7680


def _dw_in(dzs, dzf, ut):
    tn = 512
    nmain = WMAIN // tn
    first, blocks = [], []
    for w in DZ_WIDTHS:
        first.append(sum(blocks))
        blocks.append(w // tn)

    def body(rq_ref, rk_ref, rv_ref, rg_ref, fq_ref, fk_ref, fv_ref, fg_ref, dzf_ref, ut_ref, o_ref):
        gidx = pl.program_id(0)
        for piece, g0, nb in zip((rq_ref, rk_ref, rv_ref, rg_ref, fq_ref, fk_ref, fv_ref, fg_ref), first, blocks):
            @pl.when((gidx >= g0) & (gidx < g0 + nb))
            def _(piece=piece):
                o_ref[...] = _dot(ut_ref[...], piece[...]).T.astype(BF)

        @pl.when(gidx == nmain)
        def _():
            o_ref[:C, :] = _dot(ut_ref[...], dzf_ref[...]).T.astype(BF)
            o_ref[C:, :] = jnp.zeros((tn - C, D), BF)

    def piece_spec(g0, nb):
        return pl.BlockSpec((T, tn), lambda gidx: (0, jnp.clip(gidx - g0, 0, nb - 1)))

    return pl.pallas_call(
        body, name="dw_in", grid=(nmain + 1,),
        in_specs=[piece_spec(g0, nb) for g0, nb in zip(first, blocks)]
        + [pl.BlockSpec((T, C), lambda gidx: (0, 0)), pl.BlockSpec((D, T), lambda gidx: (0, 0))],
        out_specs=pl.BlockSpec((tn, D), lambda gidx: (gidx, 0)),
        out_shape=jax.ShapeDtypeStruct((GROWS, D), BF),
        compiler_params=pltpu.CompilerParams(dimension_semantics=("arbitrary",), vmem_limit_bytes=DW_VMEM_LIMIT),
    )(*dzs, dzf, ut)


def _token_order(x_po):
    def body(i_ref, o_ref):
        o_ref[...] = i_ref[...]

    return pl.pallas_call(
        body, name="token_order", grid=(NCH,),
        in_specs=[pl.BlockSpec((C, D), lambda i: (_fox_pos(i), 0))],
        out_specs=pl.BlockSpec((C, D), lambda i: (i, 0)),
        out_shape=jax.ShapeDtypeStruct((T, D), x_po.dtype),
        compiler_params=_params(("parallel",)),
    )(x_po)


def _local_step(x, tgt, normed, norm_g, wt, wft, b_f, wout, final_g, reduce_scatter=False, wout_full=None):
    cst = _constants()
    hpad, u, ut = normed
    bf_pad = jnp.pad(b_f, ((0, 0), (0, C - NFF)))
    z = _mm_nt(u, wt, WMAIN, T // 2, 1024, "in_proj")
    zf = _mm_nt(u, wft, C, T // 2, C, "in_proj_ff")
    r, sprev = _ret_fwd(z, cst)
    ct = _fox_prep(zf, bf_pad, cst)
    if wout_full is None:
        a, g = _fox_fwd(z, ct, cst, None)
    else:
        a, g, landed_wout = _fox_fwd(z, ct, cst, wout)
        wout = wout_full(landed_wout)
    yt, dopad, dob, loss8, dfg = _out_loss(r, z, a, wout, x, tgt, final_g)
    dr, da, dzrg, dzfg, delta = _dy_gate_bwd(dob, wout, r, z, a, cst["seg"])
    dwout = _mm_nn(yt, dob, 512, D, "dw_out", BF)
    dzq_r, dzk_r, dzv_r = _ret_bwd(z, cst, sprev, dr)
    p_out = []
    if reduce_scatter:
        g_out = dwout.reshape(4, DMIX // 4, D)
        p_out = [_add_halves(g_out, *_pair_swap(None, [g_out], "rs_pair_swap_out"), "pair_add_out", BF)]
    dq_po, drow, dzk_f, dzv_f, dcol, *e_out = _fox_bwd(z, da, g, delta, ct, cst, p_out)
    dzf, dbf = _fox_gate_bwd(drow, dcol, zf, bf_pad, cst)
    dzs = [dzq_r, dzk_r, dzv_r, dzrg, _token_order(dq_po), dzk_f, dzv_f, dzfg]
    gwt = _dw_in(dzs, dzf, ut)
    p_in = [_add_windows(gwt, *_pair_swap(gwt, [], "rs_pair_swap_in"))] if reduce_scatter else []
    gh, dng, *e_in = _du_norm_bwd(dzs, dzf, wt, wft, hpad, norm_g, dopad, p_in)
    return (loss8[0, 0], gh[C:], gh[PAD:C], dng, gwt, dbf[:, :NFF], dwout, dfg, p_in + p_out, e_in + e_out)


WOFF, WLEN = 1792, 2048
WHALF = WLEN // 2
LAP = WPADROWS - WOFF


def _own_window(w3):
    rows, sub, lanes = w3.shape
    pad = WPADROWS - rows
    tb = 96
    nb = WPADROWS // tb
    half = rows // 2

    def body(w_ref, o_ref, buf, sems):
        x, y, _ = _place()
        shift = 4 * (2 * x + y)
        buf[pl.ds(0, pad)] = jnp.zeros((pad, sub, lanes), F32)
        buf[pl.ds(rows, pad)] = jnp.zeros((pad, sub, lanes), F32)
        cps = [pltpu.make_async_copy(w_ref.at[pl.ds(half * h, half)], buf.at[pl.ds(shift + half * h, half)],
                                     sems.at[h]) for h in range(2)]
        for cp in cps:
            cp.start()

        def block(i, carry):
            r0 = pl.multiple_of(i * tb, tb)
            o_ref[pl.ds(r0, tb), :] = buf[pl.ds(r0, tb)].reshape(tb, sub * lanes).astype(BF)
            return carry

        cps[0].wait()
        lax.fori_loop(0, half // tb, block, 0)
        cps[1].wait()
        lax.fori_loop(half // tb, nb, block, 0)

    return pl.pallas_call(
        body, name="own_window",
        in_specs=[ANY], out_shape=jax.ShapeDtypeStruct((WPADROWS, sub * lanes), BF),
        scratch_shapes=[pltpu.VMEM((WPADROWS, sub, lanes), F32), pltpu.SemaphoreType.DMA((2,))],
        compiler_params=pltpu.CompilerParams(vmem_limit_bytes=VMEM_LIMIT),
    )(w3)


def _gather_weights(own_win, meta, x, norm_g):
    half_main, half_lap, half_meta = WOFF // 2, LAP // 2, meta.shape[0] // 2
    last = NCH - 1

    def body(win_ref, meta_ref, x_ref, g_ref, w_ref, laps_ref, gm_ref, h_ref, u_ref, ut_ref,
             send_sems, recv_sems, local_sems, stage, lapbuf, headbuf, metabuf):
        step = pl.program_id(0)
        x, y, c = _place()
        me_s = 2 * x + y
        sib = (x, y, 1 - c)
        chips = _other_chips(x, y)

        def emit(h):
            u = _norm_rows(h, g_ref[...])
            h_ref[...] = h
            u_ref[...] = u.astype(BF)
            ut_ref[...] = u.T.astype(BF)

        kinds = [
            (lambda h: win_ref.at[pl.ds(half_main * h, half_main)],
             lambda s, h: w_ref.at[pl.ds(WOFF * s + half_main * h, half_main)]),
            (lambda h: win_ref.at[pl.ds(WOFF + half_lap * h, half_lap)],
             lambda s, h: laps_ref.at[s, pl.ds(half_lap * h, half_lap)]),
            (lambda h: meta_ref.at[pl.ds(half_meta * h, half_meta)],
             lambda s, h: gm_ref.at[s, pl.ds(half_meta * h, half_meta)]),
        ]
        own_in = pltpu.make_async_copy(win_ref.at[pl.ds(0, WOFF)], stage, local_sems.at[0])
        own_lap_in = pltpu.make_async_copy(win_ref.at[pl.ds(WOFF, LAP)], lapbuf.at[0], local_sems.at[1])
        own_out = pltpu.make_async_copy(stage, w_ref.at[pl.ds(WOFF * me_s, WOFF)], local_sems.at[0])
        own_lap_out = pltpu.make_async_copy(lapbuf.at[0], laps_ref.at[me_s], local_sems.at[1])
        sends, arrivals, forwards, forwarded = [], [], [], []
        for a, (src, dst) in enumerate(kinds):
            for k, (cx, cy, cs) in enumerate(chips):
                there = dict(send_sem=send_sems.at[6 * a + k], recv_sem=recv_sems.at[6 * a + k],
                             device_id=(cx, cy, c), device_id_type=MESH)
                across = dict(send_sem=send_sems.at[6 * a + 3 + k], recv_sem=recv_sems.at[6 * a + 3 + k],
                              device_id=sib, device_id_type=MESH)
                sends.append(pltpu.make_async_remote_copy(src_ref=src(c), dst_ref=dst(me_s, c), **there))
                arrivals.append(pltpu.make_async_remote_copy(src_ref=dst(cs, c), dst_ref=dst(cs, c), **there))
                forwards.append(pltpu.make_async_remote_copy(src_ref=dst(cs, c), dst_ref=dst(cs, c), **across))
                forwarded.append(pltpu.make_async_remote_copy(
                    src_ref=dst(cs, 1 - c), dst_ref=dst(cs, 1 - c), **across))

        @pl.when(step == 0)
        def _():
            own_in.start()
            own_lap_in.start()
            for cp in sends:
                cp.start()
            own_in.wait()
            own_out.start()
            own_lap_in.wait()
            own_lap_out.start()

        @pl.when(step < last)
        def _():
            emit(x_ref[...])

        @pl.when(step == last)
        def _():
            for cp, fwd in zip(arrivals, forwards):
                cp.wait_recv()
                fwd.start()
            for cp in forwarded:
                cp.wait_recv()
            for cp in sends + forwards:
                cp.wait_send()
            own_out.wait()
            own_lap_out.wait()
            for s in range(1, 4):
                head = w_ref.at[pl.ds(WOFF * s, LAP)]
                loads = [pltpu.make_async_copy(laps_ref.at[s - 1], lapbuf.at[1], local_sems.at[2]),
                         pltpu.make_async_copy(head, headbuf, local_sems.at[3])]
                for cp in loads:
                    cp.start()
                for cp in loads:
                    cp.wait()
                headbuf[...] = (headbuf[...].astype(F32) + lapbuf[1].astype(F32)).astype(BF)
                store = pltpu.make_async_copy(headbuf, head, local_sems.at[3])
                store.start()
                store.wait()
            loads = [pltpu.make_async_copy(meta_ref, metabuf.at[me_s], local_sems.at[0])]
            loads += [pltpu.make_async_copy(gm_ref.at[cs], metabuf.at[cs], local_sems.at[1 + k])
                      for k, (_, _, cs) in enumerate(chips)]
            for cp in loads:
                cp.start()
            for cp in loads:
                cp.wait()
            tokens = jnp.concatenate([metabuf[s] for s in range(4)], axis=1)
            emit(jnp.concatenate([jnp.zeros((PAD, D), F32), tokens], axis=0))

    def chunk(i):
        return (i + 1) % NCH

    return pl.pallas_call(
        body, name="all_gather_w", grid=(NCH,),
        in_specs=[ANY, ANY, pl.BlockSpec((C, D), lambda i: (jnp.minimum(i, last - 1), 0)),
                  pl.BlockSpec((1, D), lambda i: (0, 0))],
        out_specs=[ANY] * 3 + [pl.BlockSpec((C, D), lambda i: (chunk(i), 0))] * 2
        + [pl.BlockSpec((D, C), lambda i: (0, chunk(i)))],
        out_shape=[jax.ShapeDtypeStruct((WMAIN, D), own_win.dtype), jax.ShapeDtypeStruct((4, LAP, D), own_win.dtype),
                   jax.ShapeDtypeStruct((4,) + meta.shape, meta.dtype),
                   jax.ShapeDtypeStruct((T, D), F32), jax.ShapeDtypeStruct((T, D), BF),
                   jax.ShapeDtypeStruct((D, T), BF)],
        scratch_shapes=[pltpu.SemaphoreType.DMA((18,)), pltpu.SemaphoreType.DMA((18,)), pltpu.SemaphoreType.DMA((4,)),
                        pltpu.VMEM((WOFF, D), own_win.dtype), pltpu.VMEM((2, LAP, D), own_win.dtype),
                        pltpu.VMEM((LAP, D), own_win.dtype), pltpu.VMEM((4,) + meta.shape, meta.dtype)],
        compiler_params=_params(("arbitrary",)),
    )(own_win, meta, x, norm_g)


def _pair_swap(gwt, arrs, name):
    n = len(arrs)
    wins = [] if gwt is None else [gwt]
    m = n + len(wins)
    nsem = n + 4 * len(wins)

    def body(*refs):
        ins, outs, (send_sems, recv_sems) = refs[:m], refs[m:2 * m], refs[2 * m:]
        x, y, c = _place()
        sib = dict(device_id=(x, y, 1 - c), device_id_type=MESH)
        cps = []
        for k in range(4 * len(wins)):
            cps.append(pltpu.make_async_remote_copy(
                src_ref=ins[n].at[pl.ds(WOFF * k + (1 - c) * WHALF, WHALF)], dst_ref=outs[n].at[k],
                send_sem=send_sems.at[n + k], recv_sem=recv_sems.at[n + k], **sib))
        for a in range(n):
            rows = ins[a].shape[1] // 2
            cps.append(pltpu.make_async_remote_copy(
                src_ref=ins[a].at[:, pl.ds((1 - c) * rows, rows)], dst_ref=outs[a],
                send_sem=send_sems.at[a], recv_sem=recv_sems.at[a], **sib))
        for cp in cps:
            cp.start()
        for cp in cps:
            cp.wait()

    return pl.pallas_call(
        body, name=name,
        in_specs=[ANY] * m, out_specs=[ANY] * m,
        out_shape=[jax.ShapeDtypeStruct((4, a.shape[1] // 2, a.shape[2]), a.dtype) for a in arrs]
        + [jax.ShapeDtypeStruct((4, WHALF, D), w.dtype) for w in wins],
        scratch_shapes=[pltpu.SemaphoreType.DMA((nsem,)), pltpu.SemaphoreType.DMA((nsem,))],
    )(*arrs, *wins)


def _add_windows(gwt, recv):
    tb = 256
    nb = WHALF // tb
    c = lax.axis_index("c")

    def body(c_ref, a_ref, b_ref, o_ref):
        o_ref[0] = (a_ref[...].astype(F32) + b_ref[0].astype(F32)).astype(BF)

    return pl.pallas_call(
        body, name="pair_add_in",
        grid_spec=pltpu.PrefetchScalarGridSpec(
            num_scalar_prefetch=1, grid=(4, nb),
            in_specs=[pl.BlockSpec((tb, D), lambda k, i, cr: ((WOFF // tb) * k + nb * cr[0] + i, 0)),
                      pl.BlockSpec((1, tb, D), lambda k, i, cr: (k, i, 0))],
            out_specs=pl.BlockSpec((1, tb, D), lambda k, i, cr: (k, i, 0))),
        out_shape=jax.ShapeDtypeStruct(recv.shape, BF),
        compiler_params=_params(("parallel", "parallel")),
    )(jnp.reshape(c, (1,)).astype(jnp.int32), gwt, recv)


def _chip_exchange(parts, small):
    n = len(parts)

    def body(*refs):
        ins, sm = refs[:n], refs[n]
        outs, smo = refs[n + 1:2 * n + 1], refs[2 * n + 1]
        send_sems, recv_sems = refs[2 * n + 2:]
        cps = _chip_copies(ins, outs, send_sems, recv_sems, by_dest=True)
        cps += _chip_copies([sm], [smo], send_sems.at[pl.ds(3 * n, 3)], recv_sems.at[pl.ds(3 * n, 3)], by_dest=False)
        for cp in cps:
            cp.start()
        for cp in cps:
            cp.wait()

    return pl.pallas_call(
        body, name="rs_chip_exchange",
        in_specs=[ANY] * (n + 1), out_specs=[ANY] * (n + 1),
        out_shape=[jax.ShapeDtypeStruct(p.shape, p.dtype) for p in parts]
        + [jax.ShapeDtypeStruct((4,) + small.shape, small.dtype)],
        scratch_shapes=[pltpu.SemaphoreType.DMA((3 * (n + 1),)), pltpu.SemaphoreType.DMA((3 * (n + 1),))],
    )(*parts, small)


def _pair_send(halves):
    n = len(halves)

    def body(*refs):
        ins, outs = refs[:n], refs[n:2 * n]
        send_sems, recv_sems = refs[2 * n:]
        x, y, c = _place()
        cps = [pltpu.make_async_remote_copy(
            src_ref=ins[a], dst_ref=outs[a], send_sem=send_sems.at[a], recv_sem=recv_sems.at[a],
            device_id=(x, y, 1 - c), device_id_type=MESH) for a in range(n)]
        for cp in cps:
            cp.start()
        for cp in cps:
            cp.wait()

    return pl.pallas_call(
        body, name="rs_pair_send",
        in_specs=[ANY] * n, out_specs=[ANY] * n,
        out_shape=[jax.ShapeDtypeStruct(h.shape, h.dtype) for h in halves],
        scratch_shapes=[pltpu.SemaphoreType.DMA((n,)), pltpu.SemaphoreType.DMA((n,))],
    )(*halves)


def _row_block(rows):
    for tb in (256, 128, 64, 32, 16, 8):
        if rows % tb == 0:
            return tb
    return rows


def _add_halves(full, recv, name, out_dtype):
    _, r2, w = recv.shape
    tb = _row_block(r2)
    nb = r2 // tb
    c = lax.axis_index("c")

    def body(c_ref, a_ref, b_ref, o_ref):
        o_ref[...] = (a_ref[...].astype(F32) + b_ref[...].astype(F32)).astype(o_ref.dtype)

    return pl.pallas_call(
        body, name=name,
        grid_spec=pltpu.PrefetchScalarGridSpec(
            num_scalar_prefetch=1, grid=(4, nb),
            in_specs=[pl.BlockSpec((1, tb, w), lambda s, i, cr: (s, cr[0] * nb + i, 0)),
                      pl.BlockSpec((1, tb, w), lambda s, i, cr: (s, i, 0))],
            out_specs=pl.BlockSpec((1, tb, w), lambda s, i, cr: (s, i, 0))),
        out_shape=jax.ShapeDtypeStruct(recv.shape, out_dtype),
        compiler_params=_params(("parallel", "parallel")),
    )(jnp.reshape(c, (1,)).astype(jnp.int32), full, recv)


def _add2(a, b, name):
    def body(a_ref, b_ref, o_ref):
        o_ref[...] = a_ref[...] + b_ref[...]

    return pl.pallas_call(body, name=name, out_shape=jax.ShapeDtypeStruct(a.shape, a.dtype))(a, b)


def _sum4(buf, own, name):
    _, r, w = buf.shape
    tb = _row_block(r)
    me_s = 2 * lax.axis_index("x") + lax.axis_index("y")
    by_dest = own.ndim == 3

    def body(s_ref, b_ref, own_ref, o_ref):
        mine = (own_ref[0] if by_dest else own_ref[...]).astype(F32)
        terms = [jnp.where(s_ref[0] == t, mine, b_ref[t].astype(F32)) for t in range(4)]
        o_ref[...] = ((terms[0] + terms[1]) + terms[2]) + terms[3]

    own_spec = (pl.BlockSpec((1, tb, w), lambda i, sr: (sr[0], i, 0)) if by_dest
                else pl.BlockSpec((tb, w), lambda i, sr: (i, 0)))
    return pl.pallas_call(
        body, name=name,
        grid_spec=pltpu.PrefetchScalarGridSpec(
            num_scalar_prefetch=1, grid=(r // tb,),
            in_specs=[pl.BlockSpec((4, tb, w), lambda i, sr: (0, i, 0)), own_spec],
            out_specs=pl.BlockSpec((tb, w), lambda i, sr: (i, 0))),
        out_shape=jax.ShapeDtypeStruct((r, w), F32),
        compiler_params=_params(("parallel",)),
    )(jnp.reshape(me_s, (1,)).astype(jnp.int32), buf, own)


def _adamw_math(w, g, m, v):
    mn = B1 * m + (1.0 - B1) * g
    vn = B2 * v + (1.0 - B2) * (g * g)
    m_hat = mn / (1.0 - B1 ** STEP)
    v_hat = vn / (1.0 - B2 ** STEP)
    return -LR * (m_hat / (jnp.sqrt(v_hat) + AEPS) + WD * w), mn, vn


def _adamw(w, g, m, v, name):
    r, c_ = w.shape
    tb = _row_block(r)
    if tb == r and r > 512:
        tb = 256

    def body(w_ref, g_ref, m_ref, v_ref, d_ref, mo_ref, vo_ref):
        d_ref[...], mo_ref[...], vo_ref[...] = _adamw_math(w_ref[...], g_ref[...], m_ref[...], v_ref[...])

    spec = pl.BlockSpec((tb, c_), lambda i: (i, 0))
    return pl.pallas_call(
        body, name=name, grid=(pl.cdiv(r, tb),),
        in_specs=[spec] * 4, out_specs=[spec] * 3,
        out_shape=[jax.ShapeDtypeStruct(w.shape, F32)] * 3,
        compiler_params=_params(("parallel",)),
    )(w, g, m, v)


def _adamw_rows(w, g_mine, g_sib, m, v, name):
    r = w.shape[0]
    tb = 256
    sub, lanes = w.shape[1:]
    nh = g_mine.shape[0] // tb
    nsteps = pl.cdiv(r, tb)
    assert nsteps <= 2 * nh and 4 * 3 + r <= 2 * nh * tb
    x, y, c = _place()
    place = jnp.stack([c, 4 * (2 * x + y)]).astype(jnp.int32)

    def body(p_ref, w_ref, mc_ref, sc_ref, mn_ref, sn_ref, m_ref, v_ref, go_ref, d_ref, mo_ref, vo_ref, buf):
        i = pl.program_id(0)
        for at, blk, mine_ref, sib_ref in ((0, i, mc_ref, sc_ref), (1, jnp.minimum(i + 1, 2 * nh - 1), mn_ref, sn_ref)):
            rows = jnp.where(blk // nh == p_ref[0], mine_ref[...], sib_ref[...])
            buf[tb * at:tb * (at + 1)] = rows.reshape(tb, sub, lanes)
        g = buf[pl.ds(p_ref[1], tb)]
        go_ref[...] = g
        d_ref[...], mo_ref[...], vo_ref[...] = _adamw_math(w_ref[...], g, m_ref[...], v_ref[...])

    def half_spec(ahead, sibling):
        def index(i, pr):
            half = (1 - pr[0]) if sibling else pr[0]
            return (jnp.clip(jnp.minimum(i + ahead, 2 * nh - 1) - nh * half, 0, nh - 1), 0)
        return pl.BlockSpec((tb, sub * lanes), index)

    spec = pl.BlockSpec((tb, sub, lanes), lambda i, pr: (i, 0, 0))
    return pl.pallas_call(
        body, name=name,
        grid_spec=pltpu.PrefetchScalarGridSpec(
            num_scalar_prefetch=1, grid=(nsteps,),
            in_specs=[spec, half_spec(0, False), half_spec(0, True), half_spec(1, False), half_spec(1, True),
                      spec, spec],
            out_specs=[spec] * 4,
            scratch_shapes=[pltpu.VMEM((2 * tb, sub, lanes), F32)]),
        out_shape=[jax.ShapeDtypeStruct(w.shape, F32)] * 4,
        compiler_params=_params(("parallel",)),
    )(place, w, g_mine, g_sib, g_mine, g_sib, m, v)


def _adamw_halves(w, g_mine, g_sib, m, v, name):
    r, c_ = w.shape
    r2 = g_mine.shape[0]
    tb = _row_block(r2)
    nb = r2 // tb
    c = lax.axis_index("c")

    def body(c_ref, w_ref, gm_ref, gs_ref, m_ref, v_ref, g_ref, d_ref, mo_ref, vo_ref):
        g = jnp.where(pl.program_id(0) == c_ref[0], gm_ref[...], gs_ref[...])
        g_ref[...] = g
        d_ref[...], mo_ref[...], vo_ref[...] = _adamw_math(w_ref[...], g, m_ref[...], v_ref[...])

    full = pl.BlockSpec((tb, c_), lambda h, i, cr: (h * nb + i, 0))
    half = pl.BlockSpec((tb, c_), lambda h, i, cr: (i, 0))
    return pl.pallas_call(
        body, name=name,
        grid_spec=pltpu.PrefetchScalarGridSpec(
            num_scalar_prefetch=1, grid=(2, nb),
            in_specs=[full, half, half, full, full], out_specs=[full] * 4),
        out_shape=[jax.ShapeDtypeStruct(w.shape, F32)] * 4,
        compiler_params=_params(("parallel", "parallel")),
    )(jnp.reshape(c, (1,)).astype(jnp.int32), w, g_mine, g_sib, m, v)


def kernel(x, meta_tokens, norm_g, w_in, b_f, w_out, final_g, loss_target, m_meta_tokens, m_norm_g, m_w_in, m_b_f, m_w_out, m_final_g, v_meta_tokens, v_norm_g, v_w_in, v_b_f, v_w_out, v_final_g):
    me_s = 2 * lax.axis_index("x") + lax.axis_index("y")
    w3, m3, v3 = [jnp.transpose(jnp.reshape(t[0], (D // C, C, WSH)), (2, 0, 1)) for t in (w_in, m_w_in, v_w_in)]

    wt_main, laps, _, *normed = _gather_weights(_own_window(w3), meta_tokens, x[0], norm_g)
    wft = jnp.pad(laps[3, :NFF], ((0, C - NFF), (0, 0)))
    mine = (jnp.arange(4) == me_s)[:, None, None]
    wout_own = w_out[0].astype(BF)

    def wout_full(landed):
        return jnp.where(mine, wout_own[None], landed).reshape(DMIX, D)

    loss, gx, dmeta, dng, gwt, dbf, dwout, dfg, (p_in, p_out), (e_in, e_out) = _local_step(
        x[0], loss_target[0], normed, norm_g, wt_main, wft, b_f, wout_own, final_g.reshape(1, D), True, wout_full)

    g_meta = jnp.stack([dmeta[:, 256 * s:256 * (s + 1)] for s in range(4)])
    small = jnp.concatenate([dng, dfg, jnp.pad(dbf, ((0, 0), (0, D - NFF))),
                             jnp.pad(jnp.reshape(loss, (1, 1)), ((0, 0), (0, D - 1))),
                             jnp.zeros((4, D), F32)], axis=0)
    e_meta, e_small = _chip_exchange([g_meta], small)
    h_in, h_out = _sum4(e_in, p_in, "sum_in"), _sum4(e_out, p_out, "sum_out")
    h_meta, h_small = _sum4(e_meta, g_meta, "sum_meta"), _sum4(e_small, small, "sum_small")
    s_in, s_out, s_meta, s_small = _pair_send([h_in, h_out, h_meta, h_small])
    gw_meta = _add2(h_meta, s_meta, "pair_add_meta")
    tot = _add2(h_small, s_small, "pair_add_small")
    g_norm, g_final, g_bf, loss_all = tot[0:1], tot[1], tot[2:3, :NFF], tot[3, 0]

    d_meta, nm_meta, nv_meta = _adamw(meta_tokens, gw_meta, m_meta_tokens, v_meta_tokens, "adamw_meta")
    d_norm, nm_norm, nv_norm = _adamw(norm_g, g_norm, m_norm_g, v_norm_g, "adamw_norm")
    outs_in = _adamw_rows(w3, h_in, s_in, m3, v3, "adamw_in")
    gw_in, d_in, nm_in, nv_in = [jnp.reshape(jnp.transpose(t, (1, 2, 0)), (1, D, WSH)) for t in outs_in]
    d_bf, nm_bf, nv_bf = _adamw(b_f, g_bf, m_b_f, v_b_f, "adamw_bf")
    gw_out, d_out, nm_out, nv_out = _adamw_halves(w_out[0], h_out, s_out, m_w_out[0], v_w_out[0], "adamw_out")
    d_fin, nm_fin, nv_fin = _adamw(final_g.reshape(1, D), g_final.reshape(1, D), m_final_g.reshape(1, D),
                                   v_final_g.reshape(1, D), "adamw_final")
    return (loss_all, gx[None], gw_meta, g_norm, gw_in, g_bf, gw_out[None], g_final,
            d_meta, d_norm, d_in, d_bf, d_out[None], d_fin.reshape(D),
            nm_meta, nm_norm, nm_in, nm_bf, nm_out[None], nm_fin.reshape(D),
            nv_meta, nv_norm, nv_in, nv_bf, nv_out[None], nv_fin.reshape(D))
```

```python
import numpy as np
import jax
import jax.numpy as jnp
from jax import lax
from jax.experimental import pallas as pl
from jax.experimental.pallas import tpu as pltpu

D = 1024
SEQ = 2048
NMETA = 16
C = 128
PAD = C - NMETA
T = PAD + NMETA + SEQ
NCH = T // C
RH, RDK, RDV = 4, 128, 256
FH, FD = 16, 64
NPAIR = FH // 2
WMAIN = 7168
NFF = 16
WIN = WMAIN + NFF
WSH = WIN // 4
WPADROWS = 1824
DMIX = 2048
EPS = 1e-6
NEG = -1e30
RSCALE = RDK ** -0.5
FSCALE = FD ** -0.5
ROPE_BASE = 10000.0
LR, B1, B2, AEPS, WD, STEP = 0.001, 0.9, 0.999, 1e-08, 0.01, 10

BF = jnp.bfloat16
F32 = jnp.float32
NT = (((1,), (1,)), ((), ()))
TN = (((0,), (0,)), ((), ()))
NN_DIMS = (((1,), (0,)), ((), ()))
MESH = pl.DeviceIdType.MESH
ANY = pl.BlockSpec(memory_space=pl.ANY)
VMEM_LIMIT = 48 * 1024 * 1024
DW_VMEM_LIMIT = 56 * 1024 * 1024

GB_R, GB_F = 2, 6
QB_F, KB_F, VB_F = 24, 32, 40


def _dot(a, b):
    return jnp.dot(a, b, preferred_element_type=F32)


def _dg(a, b, dims):
    return lax.dot_general(a, b, dims, preferred_element_type=F32)


def _params(sem=None):
    return pltpu.CompilerParams(dimension_semantics=sem, vmem_limit_bytes=VMEM_LIMIT)


def _constants():
    pos = jnp.arange(T, dtype=F32) - PAD
    inv = ROPE_BASE ** (-jnp.arange(0, RDK, 2, dtype=F32) / RDK)
    ang = pos[:, None] * inv[None, :]
    cos, sin = jnp.cos(ang), jnp.sin(ang)
    cos2 = jnp.concatenate([cos, cos], axis=1)
    sin2 = jnp.concatenate([-sin, sin], axis=1)
    log_gamma = jnp.log1p(-jnp.exp2(-5.0 - jnp.arange(RH, dtype=F32)))
    idx = jnp.arange(C, dtype=F32)
    diff = idx[:, None] - idx[None, :]
    dmask = jnp.where(diff[None] >= 0, jnp.exp(log_gamma[:, None, None] * jnp.maximum(diff, 0.0)[None]), 0.0)
    zeta = jnp.exp(log_gamma[:, None] * (C - 1.0 - idx)[None, :])
    xi = jnp.exp(log_gamma[:, None] * (idx + 1.0)[None, :])
    gdec = jnp.exp(log_gamma * C)
    zeta_b = jnp.broadcast_to(zeta[:, :, None], (RH, C, RDK))
    xi_b = jnp.broadcast_to(xi[:, :, None], (RH, C, RDK))
    gdec_b = jnp.broadcast_to(gdec[:, None, None], (RH, RDK, RDV))
    tri = jnp.asarray(np.tril(np.ones((C, C), np.float32)), dtype=BF)
    head_of_lane = np.arange(FH * FD) // FD
    pick = ((np.arange(FH * FD)[:, None] % FD == 0)
            & (head_of_lane[:, None] == np.arange(C)[None, :])).astype(np.float32)
    seg = (np.arange(C)[:, None] // FD == np.arange(C)[None, :] // FD).astype(np.float32)
    ones_aug = np.concatenate([np.tile((np.arange(C) < FD)[None, :], (C, 1)),
                               np.tile((np.arange(C) >= FD)[None, :], (C, 1))], axis=0).astype(np.float32)
    lane = np.arange(2 * C) % C
    causal = np.where(lane[None, :] <= np.arange(C)[:, None], 0.0, NEG).astype(np.float32)
    mask_bias = np.stack([np.zeros((C, 2 * C), np.float32), causal])
    return dict(cos2=cos2, sin2=sin2, dmask=dmask, zeta=zeta_b, xi=xi_b, gdec=gdec_b, tri=tri,
                mask_bias=jnp.asarray(mask_bias), pick=jnp.asarray(pick, dtype=BF), seg=jnp.asarray(seg, dtype=BF),
                ones_aug=jnp.asarray(ones_aug, dtype=BF))


def _norm_rows(h, g):
    return h * lax.rsqrt(jnp.mean(h * h, axis=1, keepdims=True) + EPS) * g


def _mm_nt(a, b, n, tm, tn, name):
    m, k = a.shape

    def body(a_ref, b_ref, o_ref):
        o_ref[...] = _dg(a_ref[...], b_ref[...], NT)

    return pl.pallas_call(
        body, name=name, grid=(m // tm, n // tn),
        in_specs=[pl.BlockSpec((tm, k), lambda i, j: (i, 0)), pl.BlockSpec((tn, k), lambda i, j: (j, 0))],
        out_specs=pl.BlockSpec((tm, tn), lambda i, j: (i, j)),
        out_shape=jax.ShapeDtypeStruct((m, n), F32),
        compiler_params=_params(("parallel", "parallel")),
    )(a, b)


def _mm_nn(a, b, tm, tn, name, out_dtype=F32):
    m, k = a.shape
    _, n = b.shape

    def body(a_ref, b_ref, o_ref):
        o_ref[...] = _dot(a_ref[...], b_ref[...]).astype(out_dtype)

    return pl.pallas_call(
        body, name=name, grid=(m // tm, n // tn),
        in_specs=[pl.BlockSpec((tm, k), lambda i, j: (i, 0)), pl.BlockSpec((k, tn), lambda i, j: (0, j))],
        out_specs=pl.BlockSpec((tm, tn), lambda i, j: (i, j)),
        out_shape=jax.ShapeDtypeStruct((m, n), out_dtype),
        compiler_params=_params(("parallel", "parallel")),
    )(a, b)


def _rot(x, cos2, sin2):
    return x * cos2 + pltpu.roll(x, 64, 1) * sin2


def _ret_specs(chunk):
    whole = lambda shape: pl.BlockSpec(shape, lambda n: (0,) * len(shape))
    return [
        pl.BlockSpec((C, RH * RDK), lambda n: (chunk(n), 0)),
        pl.BlockSpec((C, RH * RDK), lambda n: (chunk(n), 1)),
        pl.BlockSpec((C, RH * RDV), lambda n: (chunk(n), 1)),
        pl.BlockSpec((C, RDK), lambda n: (chunk(n), 0)),
        pl.BlockSpec((C, RDK), lambda n: (chunk(n), 0)),
        whole((RH, C, C)), whole((RH, C, RDK)), whole((RH, C, RDK)), whole((RH, RDK, RDV)),
    ]


def _ret_heads(q_ref, k_ref, v_ref, cos, sin):
    qr = [_rot(q_ref[:, RDK * h:RDK * (h + 1)], cos, sin) for h in range(RH)]
    kr = [_rot(k_ref[:, RDK * h:RDK * (h + 1)], cos, sin) * RSCALE for h in range(RH)]
    vb = [v_ref[:, RDV * h:RDV * (h + 1)].astype(BF) for h in range(RH)]
    return qr, kr, [t.astype(BF) for t in qr], [t.astype(BF) for t in kr], vb


def _ret_fwd(z, cst):
    def body(q_ref, k_ref, v_ref, cos_ref, sin_ref, dm_ref, xi_ref, zt_ref, gd_ref, r_ref, sp_ref, st):
        n = pl.program_id(0)

        @pl.when(n == 0)
        def _():
            st[...] = jnp.zeros_like(st)

        hs = range(RH)
        qr, kr, qb, kb, vb = _ret_heads(q_ref, k_ref, v_ref, cos_ref[...], sin_ref[...])
        sd = [(_dg(qb[h], kb[h], NT) * dm_ref[h]).astype(BF) for h in hs]
        state = [st[h] for h in hs]
        qx = [(qr[h] * xi_ref[h]).astype(BF) for h in hs]
        kz = [(kr[h] * zt_ref[h]).astype(BF) for h in hs]
        out = [_dot(sd[h], vb[h]) + _dot(qx[h], state[h].astype(BF)) for h in hs]
        kv = [_dg(kz[h], vb[h], TN) for h in hs]
        for h in hs:
            sp_ref[0, h] = state[h]
            r_ref[:, RDV * h:RDV * (h + 1)] = out[h]
            st[h] = state[h] * gd_ref[h] + kv[h]

    return pl.pallas_call(
        body, name="ret_fwd", grid=(NCH,),
        in_specs=_ret_specs(lambda n: n),
        out_specs=[pl.BlockSpec((C, RH * RDV), lambda n: (n, 0)),
                   pl.BlockSpec((1, RH, RDK, RDV), lambda n: (n, 0, 0, 0))],
        out_shape=[jax.ShapeDtypeStruct((T, RH * RDV), F32), jax.ShapeDtypeStruct((NCH, RH, RDK, RDV), F32)],
        scratch_shapes=[pltpu.VMEM((RH, RDK, RDV), F32)],
        compiler_params=_params(("arbitrary",)),
    )(z, z, z, cst["cos2"], cst["sin2"], cst["dmask"], cst["xi"], cst["zeta"], cst["gdec"])


def _ret_bwd(z, cst, sprev, dr):
    def body(q_ref, k_ref, v_ref, cos_ref, sin_ref, dm_ref, xi_ref, zt_ref, gd_ref, sp_ref, dr_ref,
             dq_ref, dk_ref, dv_ref, gst):
        i = pl.program_id(0)

        @pl.when(i == 0)
        def _():
            gst[...] = jnp.zeros_like(gst)

        hs = range(RH)
        cos, sin = cos_ref[...], sin_ref[...]
        qr, kr, qb, kb, vb = _ret_heads(q_ref, k_ref, v_ref, cos, sin)
        dm = [dm_ref[h] for h in hs]
        xi = [xi_ref[h] for h in hs]
        zt = [zt_ref[h] for h in hs]
        sd = [(_dg(qb[h], kb[h], NT) * dm[h]).astype(BF) for h in hs]
        qx = [(qr[h] * xi[h]).astype(BF) for h in hs]
        kz = [(kr[h] * zt[h]).astype(BF) for h in hs]
        drb = [dr_ref[:, RDV * h:RDV * (h + 1)] for h in hs]
        sb = [sp_ref[0, h].astype(BF) for h in hs]
        g = [gst[h] for h in hs]
        gb = [t.astype(BF) for t in g]
        ds = [(_dg(drb[h], vb[h], NT) * dm[h]).astype(BF) for h in hs]
        dq = [_dot(ds[h], kb[h]) + _dg(drb[h], sb[h], NT) * xi[h] for h in hs]
        dk = [(_dg(ds[h], qb[h], TN) + _dg(vb[h], gb[h], NT) * zt[h]) * RSCALE for h in hs]
        dv = [_dg(sd[h], drb[h], TN) + _dot(kz[h], gb[h]) for h in hs]
        gn = [g[h] * gd_ref[h] + _dg(qx[h], drb[h], TN) for h in hs]
        for h in hs:
            gst[h] = gn[h]
            dq_ref[:, RDK * h:RDK * (h + 1)] = (dq[h] * cos + pltpu.roll(dq[h] * sin, 64, 1)).astype(BF)
            dk_ref[:, RDK * h:RDK * (h + 1)] = (dk[h] * cos + pltpu.roll(dk[h] * sin, 64, 1)).astype(BF)
            dv_ref[:, RDV * h:RDV * (h + 1)] = dv[h].astype(BF)

    rev = lambda n: NCH - 1 - n
    return pl.pallas_call(
        body, name="ret_bwd", grid=(NCH,),
        in_specs=_ret_specs(rev) + [
            pl.BlockSpec((1, RH, RDK, RDV), lambda n: (rev(n), 0, 0, 0)),
            pl.BlockSpec((C, RH * RDV), lambda n: (rev(n), 0)),
        ],
        out_specs=[pl.BlockSpec((C, RH * RDK), lambda n: (rev(n), 0)),
                   pl.BlockSpec((C, RH * RDK), lambda n: (rev(n), 0)),
                   pl.BlockSpec((C, RH * RDV), lambda n: (rev(n), 0))],
        out_shape=[jax.ShapeDtypeStruct((T, RH * RDK), BF), jax.ShapeDtypeStruct((T, RH * RDK), BF),
                   jax.ShapeDtypeStruct((T, RH * RDV), BF)],
        scratch_shapes=[pltpu.VMEM((RH, RDK, RDV), F32)],
        compiler_params=_params(("arbitrary",)),
    )(z, z, z, cst["cos2"], cst["sin2"], cst["dmask"], cst["xi"], cst["zeta"], cst["gdec"], sprev, dr)


def _place():
    x, y, c = lax.axis_index("x"), lax.axis_index("y"), lax.axis_index("c")
    return x, y, c


def _other_chips(x, y):
    return [(1 - x, y, 2 * (1 - x) + y), (x, 1 - y, 2 * x + (1 - y)), (1 - x, 1 - y, 2 * (1 - x) + (1 - y))]


def _chip_copies(srcs, lands, send_sems, recv_sems, by_dest):
    x, y, c = _place()
    me_s = 2 * x + y
    return [pltpu.make_async_remote_copy(
        src_ref=src.at[cs] if by_dest else src, dst_ref=land.at[me_s],
        send_sem=send_sems.at[3 * a + j], recv_sem=recv_sems.at[3 * a + j],
        device_id=(cx, cy, c), device_id_type=MESH)
        for a, (src, land) in enumerate(zip(srcs, lands)) for j, (cx, cy, cs) in enumerate(_other_chips(x, y))]


def _split_dot(x, mat01, dims=NN_DIMS, x_first=True):
    acc, rest = None, x
    for _ in range(3):
        piece = rest.astype(BF)
        part = _dg(piece, mat01, dims) if x_first else _dg(mat01, piece, dims)
        acc = part if acc is None else acc + part
        rest = rest - piece.astype(F32)
    return acc


def _log_sigmoid(x):
    return -(jnp.maximum(-x, 0.0) + jnp.log1p(jnp.exp(-jnp.abs(x))))


def _fox_prep(zf, bf_pad, cst):
    def body(zf_ref, b_ref, tri_ref, ct_ref, carry):
        n = pl.program_id(0)

        @pl.when(n == 0)
        def _():
            carry[...] = jnp.zeros_like(carry)

        ls = _log_sigmoid(zf_ref[...] + b_ref[...])
        row = n * C + lax.broadcasted_iota(jnp.int32, (C, C), 0)
        lf = jnp.where(row >= PAD, ls, 0.0)
        cc = _split_dot(lf, tri_ref[...], x_first=False) + carry[0:1, :]
        carry[...] = jnp.broadcast_to(cc[C - 1:C, :], carry.shape)
        pos = n * C + lax.broadcasted_iota(jnp.int32, (FH, C), 1)
        ct_ref[0] = jnp.where(pos >= PAD, cc.T[:FH, :], -NEG)

    return pl.pallas_call(
        body, name="fox_prep", grid=(NCH,),
        in_specs=[pl.BlockSpec((C, C), lambda n: (n, 0)), pl.BlockSpec((1, C), lambda n: (0, 0)),
                  pl.BlockSpec((C, C), lambda n: (0, 0))],
        out_specs=pl.BlockSpec((1, FH, C), lambda n: (n, 0, 0)),
        out_shape=jax.ShapeDtypeStruct((NCH, FH, C), F32),
        scratch_shapes=[pltpu.VMEM((8, C), F32)],
        compiler_params=_params(("arbitrary",)),
    )(zf, bf_pad, cst["tri"])


def _lo_lanes(shape):
    return lax.broadcasted_iota(jnp.int32, shape, 1) < FD


def _split_heads(x):
    lo = _lo_lanes(x.shape)
    zero = jnp.zeros_like(x)
    return jnp.concatenate([jnp.where(lo, x, zero), jnp.where(lo, zero, x)], axis=0)


def _spread2(x):
    lo = _lo_lanes(x.shape)
    r = pltpu.roll(x, FD, 1)
    return jnp.concatenate([jnp.where(lo, x, r), jnp.where(lo, r, x)], axis=1)


NSTEP = (NCH + 1) // 2
NTILE = NCH + 1
TROWS = T + C


def _fox_tile(s, t):
    second = t > s
    return second.astype(jnp.int32), jnp.where(second, t - s - 1, s - t)


def _fox_pos(i):
    return jnp.where(i < NSTEP, 2 * i, 2 * (NCH - 1 - i) + 1)


FOX_ORDER = [2 * i if i < NSTEP else 2 * (NCH - 1 - i) + 1 for i in range(NCH)]


def _fox_pair_columns():
    return pl.BlockSpec((TROWS, C), lambda p, s: (0, p))


def _fox_key_bias(ct_ref, p, j):
    return jnp.concatenate([ct_ref[j, pl.ds(2 * p, 1), :], ct_ref[j, pl.ds(2 * p + 1, 1), :]], axis=1)


def _fox_columns(cols, sems, p):
    def copies(pair, slot):
        return [pltpu.make_async_copy(
            src.at[pl.ds(0, buf.shape[1]), pl.ds(pl.multiple_of((first + pair) * C, C), C)], buf.at[slot],
            sems.at[i, slot]) for i, (src, first, buf) in enumerate(cols)]

    @pl.when(p == 0)
    def _():
        for cp in copies(0, 0):
            cp.start()

    for cp in copies(p, p % 2):
        cp.wait()

    @pl.when(p + 1 < NPAIR)
    def _():
        for cp in copies(p + 1, 1 - p % 2):
            cp.start()


def _rows(block, size=C):
    return pl.ds(pl.multiple_of(block * size, size), size)


def _fox_fwd(z, ct, cst, share):
    n = 0 if share is None else 1

    def body(z_ref, ct_ref, ones_ref, mb_ref, *rest):
        share_refs, (a_ref, g_ref), land_refs = rest[:n], rest[n:n + 2], rest[n + 2:2 * n + 2]
        kks, vvs, q2, m2, sbuf, qbuf, kbuf, vbuf, col_sems = rest[2 * n + 2:2 * n + 11]
        p, s = pl.program_id(0), pl.program_id(1)
        slot = p % 2
        if n:
            copies = _chip_copies(share_refs, land_refs, *rest[2 * n + 11:], by_dest=False)

            @pl.when((p == 0) & (s == 0))
            def _():
                for cp in copies:
                    cp.start()

            @pl.when((p == NPAIR - 1) & (s == NSTEP - 1))
            def _():
                for cp in copies:
                    cp.wait()

        @pl.when(s == 0)
        def _():
            ones = ones_ref[...]
            _fox_columns([(z_ref, QB_F, qbuf), (z_ref, KB_F, kbuf), (z_ref, VB_F, vbuf)], col_sems, p)

            def prep(j, carry):
                kks[j] = _split_heads(kbuf[slot, _rows(j), :]).astype(BF)
                vvs[j] = jnp.concatenate([_split_heads(vbuf[slot, _rows(j), :]).astype(BF), ones], axis=1)
                return carry

            lax.fori_loop(0, NCH, prep, 0)

        q2[0] = (qbuf[slot, _rows(s), :] * FSCALE).astype(BF)
        q2[1] = (qbuf[slot, _rows(NCH - 1 - s), :] * FSCALE).astype(BF)

        tiles = [_fox_tile(s, t) for t in range(NTILE)]
        causal = mb_ref[1]
        neg = jnp.full((C, 2 * C), NEG, F32)
        run, first = neg, neg
        for t, (sel, j) in enumerate(tiles):
            st = _dg(q2[sel], kks[j], NT) - _fox_key_bias(ct_ref, p, j)
            if t in (0, NTILE - 1):
                st = st + causal
            sbuf[t] = st
            run = jnp.maximum(jnp.where(t == s + 1, neg, run), st)
            first = jnp.where(t == s, run, first)
        for w, mx in enumerate((first, run)):
            m2[w] = jnp.concatenate(
                [jnp.broadcast_to(jnp.max(mx[:, :C], axis=1, keepdims=True), (C, C)),
                 jnp.broadcast_to(jnp.max(mx[:, C:], axis=1, keepdims=True), (C, C))], axis=1)

        zero = jnp.zeros((C, 2 * C), F32)
        run, first = zero, zero
        for t, (sel, j) in enumerate(tiles):
            run = jnp.where(t == s + 1, zero, run) + _dot(jnp.exp(sbuf[t] - m2[sel]).astype(BF), vvs[j])
            first = jnp.where(t == s, run, first)
        lo = _lo_lanes((C, C))
        for w, res in enumerate((first, run)):
            l = res[:, C:]
            a_ref[_rows(2 * s + w), :] = res[:, :C] / l
            mw = m2[w]
            g_ref[_rows(2 * s + w), :] = -(jnp.where(lo, mw[:, :C], mw[:, C:]) + jnp.log(l))

    col = _fox_pair_columns()
    return pl.pallas_call(
        body, name="fox_fwd", grid=(NPAIR, NSTEP),
        in_specs=[ANY,
                  pl.BlockSpec((NCH, FH, C), lambda p, s: (0, 0, 0)),
                  pl.BlockSpec((2 * C, C), lambda p, s: (0, 0)),
                  pl.BlockSpec((2, C, 2 * C), lambda p, s: (0, 0, 0))] + [ANY] * n,
        out_specs=[col, col] + [ANY] * n,
        out_shape=[jax.ShapeDtypeStruct((TROWS, FH * FD), F32)] * 2
        + ([jax.ShapeDtypeStruct((4,) + share.shape, share.dtype)] if n else []),
        scratch_shapes=[pltpu.VMEM((NCH, 2 * C, C), BF), pltpu.VMEM((NCH, 2 * C, 2 * C), BF),
                        pltpu.VMEM((2, C, C), BF), pltpu.VMEM((2, C, 2 * C), F32),
                        pltpu.VMEM((NTILE, C, 2 * C), F32),
                        pltpu.VMEM((2, T, C), F32), pltpu.VMEM((2, T, C), F32), pltpu.VMEM((2, T, C), F32),
                        pltpu.SemaphoreType.DMA((3, 2))]
        + [pltpu.SemaphoreType.DMA((3,)), pltpu.SemaphoreType.DMA((3,))] * n,
        compiler_params=_params(("arbitrary", "arbitrary")),
    )(z, ct, cst["ones_aug"], cst["mask_bias"], *([share] * n))


def _fox_bwd(z, da, g, delta, ct, cst, parts=()):
    grp = 9

    n = len(parts)

    def body(z_ref, da_ref, g_ref, dl_ref, ct_ref, ones_ref, mb_ref, *rest):
        part_refs, (dq_ref, dr_ref, dk_ref, dv_ref, dcs_ref), land_refs = rest[:n], rest[n:n + 5], rest[n + 5:2 * n + 5]
        (kks, vvs, q2, qq2, dd2, da2, gi2, dl2, dq2, dvb, dkb, dkacc, dvacc, csacc, qbuf, kbuf, vbuf, dabuf, gbuf,
         dlbuf, col_sems) = rest[2 * n + 5:2 * n + 26]
        p, s = pl.program_id(0), pl.program_id(1)
        slot = p % 2
        ones = ones_ref[...]
        if n:
            copies = _chip_copies(part_refs, land_refs, *rest[2 * n + 26:], by_dest=True)

            @pl.when((p == 0) & (s == 0))
            def _():
                for cp in copies:
                    cp.start()

            @pl.when((p == NPAIR - 1) & (s == NSTEP - 1))
            def _():
                for cp in copies:
                    cp.wait()

        @pl.when(s == 0)
        def _():
            dkacc[...] = jnp.zeros_like(dkacc)
            dvacc[...] = jnp.zeros_like(dvacc)
            csacc[...] = jnp.zeros_like(csacc)
            _fox_columns([(z_ref, QB_F, qbuf), (z_ref, KB_F, kbuf), (z_ref, VB_F, vbuf), (da_ref, 0, dabuf),
                          (g_ref, 0, gbuf), (dl_ref, 0, dlbuf)], col_sems, p)

            def prep(j, carry):
                kks[j] = _split_heads(kbuf[slot, _rows(j), :]).astype(BF)
                vvs[j] = _split_heads(vbuf[slot, _rows(j), :]).astype(BF)
                return carry

            lax.fori_loop(0, NCH, prep, 0)

        for w, (chunk, blk) in enumerate(((s, 2 * s), (NCH - 1 - s, jnp.where(s == NSTEP - 1, 2 * s, 2 * s + 1)))):
            qf = qbuf[slot, _rows(chunk), :]
            q2[w] = (qf * FSCALE).astype(BF)
            qq2[w] = jnp.concatenate([_split_heads(qf).astype(BF), ones], axis=1)
            da2[w] = dabuf[slot, _rows(blk), :]
            dd2[w] = _split_heads(da2[w].astype(F32)).astype(BF)
            gi2[w] = _spread2(gbuf[slot, _rows(blk), :])
            dl2[w] = _spread2(dlbuf[slot, _rows(blk), :])
        dq2[...] = jnp.zeros_like(dq2)
        zero = jnp.zeros((C, 2 * C), F32)

        def group(gi, carry):
            ts = [gi * grp + u for u in range(grp)]
            tiles = [_fox_tile(s, t) for t in ts]
            kk = [kks[j] for _, j in tiles]
            ss = [_dg(q2[sel], kj, NT) + (gi2[sel] - _fox_key_bias(ct_ref, p, j)) for kj, (sel, j) in zip(kk, tiles)]
            ss[0] = ss[0] + mb_ref[(gi == 0).astype(jnp.int32)]
            ss[-1] = ss[-1] + mb_ref[(gi == 1).astype(jnp.int32)]
            dps = [_dg(da2[sel], vvs[j], NT) for sel, j in tiles]
            pes = [jnp.exp(st) for st in ss]
            dss = [pe * (dp - dl2[sel]) * FSCALE for pe, dp, (sel, _) in zip(pes, dps, tiles)]
            pts = [jnp.concatenate([pe[:, :C].T, pe[:, C:].T], axis=1).astype(BF) for pe in pes]
            dsts = [jnp.concatenate([ds[:, :C].T, ds[:, C:].T], axis=1).astype(BF) for ds in dss]
            dvs = [_dot(pt, dd2[sel]) for pt, (sel, _) in zip(pts, tiles)]
            rs = [_dot(dst, qq2[sel]) for dst, (sel, _) in zip(dsts, tiles)]
            parts = [_dot(ds.astype(BF), jnp.concatenate([kj, ones], axis=1)) for ds, kj in zip(dss, kk)]
            for t, dv, rr in zip(ts, dvs, rs):
                dvb[t] = dv
                dkb[t] = rr
            pa, pb = zero, zero
            for t, part in zip(ts, parts):
                pa = pa + jnp.where(t <= s, part, zero)
                pb = pb + jnp.where(t <= s, zero, part)
            dq2[0] += pa
            dq2[1] += pb
            return carry

        ntile = jnp.where(s == NSTEP - 1, grp, NTILE)
        lax.fori_loop(0, ntile // grp, group, 0)

        def scatter(t, carry):
            _, j = _fox_tile(s, t)
            r = pl.ds(pl.multiple_of(j * C, C), C)
            dvacc[r, :] += dvb[t]
            dkacc[r, :] += dkb[t, :, :C]
            csacc[r, :] += dkb[t, :, C:]
            return carry

        lax.fori_loop(0, ntile, scatter, 0)
        for w in range(2):
            res = dq2[w]
            dq_ref[_rows(2 * s + w), :] = res[:, :C].astype(BF)
            dr_ref[_rows(2 * s + w), :] = res[:, C:]

        @pl.when(s == NSTEP - 1)
        def _():
            dk_ref[...] = dkacc[...].astype(BF)
            dv_ref[...] = dvacc[...].astype(BF)
            dcs_ref[...] = csacc[...]

    both = _fox_pair_columns()
    col = pl.BlockSpec((T, C), lambda p, s: (0, p))
    return pl.pallas_call(
        body, name="fox_bwd", grid=(NPAIR, NSTEP),
        in_specs=[ANY] * 4
        + [pl.BlockSpec((NCH, FH, C), lambda p, s: (0, 0, 0)),
           pl.BlockSpec((2 * C, C), lambda p, s: (0, 0)),
           pl.BlockSpec((2, C, 2 * C), lambda p, s: (0, 0, 0))] + [ANY] * n,
        out_specs=[both, both, col, col, col] + [ANY] * n,
        out_shape=[jax.ShapeDtypeStruct((TROWS, FH * FD), BF), jax.ShapeDtypeStruct((TROWS, FH * FD), F32),
                   jax.ShapeDtypeStruct((T, FH * FD), BF), jax.ShapeDtypeStruct((T, FH * FD), BF),
                   jax.ShapeDtypeStruct((T, FH * FD), F32)]
        + [jax.ShapeDtypeStruct(p.shape, p.dtype) for p in parts],
        scratch_shapes=[pltpu.VMEM((NCH, 2 * C, C), BF), pltpu.VMEM((NCH, 2 * C, C), BF),
                        pltpu.VMEM((2, C, C), BF), pltpu.VMEM((2, 2 * C, 2 * C), BF), pltpu.VMEM((2, 2 * C, C), BF),
                        pltpu.VMEM((2, C, C), BF), pltpu.VMEM((2, C, 2 * C), F32), pltpu.VMEM((2, C, 2 * C), F32),
                        pltpu.VMEM((2, C, 2 * C), F32),
                        pltpu.VMEM((NTILE, C, C), F32), pltpu.VMEM((NTILE, C, 2 * C), F32),
                        pltpu.VMEM((T, C), F32), pltpu.VMEM((T, C), F32), pltpu.VMEM((T, C), F32),
                        pltpu.VMEM((2, T, C), F32), pltpu.VMEM((2, T, C), F32), pltpu.VMEM((2, T, C), F32),
                        pltpu.VMEM((2, T, C), BF), pltpu.VMEM((2, T, C), F32), pltpu.VMEM((2, T, C), F32),
                        pltpu.SemaphoreType.DMA((6, 2))]
        + ([pltpu.SemaphoreType.DMA((3 * n,)), pltpu.SemaphoreType.DMA((3 * n,))] if n else []),
        compiler_params=_params(("arbitrary", "arbitrary")),
    )(z, da, g, delta, ct, cst["ones_aug"], cst["mask_bias"], *parts)


def _fox_gate_bwd(drow, dcol, zf, bf_pad, cst):
    def body(dr_ref, dc_ref, zf_ref, b_ref, tri_ref, pick_ref, dff_ref, db_ref, carry):
        s = pl.program_id(0)
        n = NCH - 1 - s

        @pl.when(s == 0)
        def _():
            carry[...] = jnp.zeros_like(carry)
            db_ref[...] = jnp.zeros_like(db_ref)

        dcb = _split_dot((dr_ref[...] - dc_ref[...]) * (1.0 / FSCALE), pick_ref[...])
        suf = _split_dot(dcb, tri_ref[...], TN, x_first=False) + carry[0:1, :]
        carry[...] = jnp.broadcast_to(suf[0:1, :], carry.shape)
        x = zf_ref[...] + b_ref[...]
        row = n * C + lax.broadcasted_iota(jnp.int32, (C, C), 0)
        dff = jnp.where(row >= PAD, suf * (1.0 - jax.nn.sigmoid(x)), 0.0)
        dff_ref[...] = dff.astype(BF)
        db_ref[...] += jnp.sum(dff, axis=0, keepdims=True)

    rev = lambda s: (NCH - 1 - s, 0)
    return pl.pallas_call(
        body, name="fox_gate_bwd", grid=(NCH,),
        in_specs=[pl.BlockSpec((C, FH * FD), lambda s: (_fox_pos(NCH - 1 - s), 0)),
                  pl.BlockSpec((C, FH * FD), rev), pl.BlockSpec((C, C), rev),
                  pl.BlockSpec((1, C), lambda s: (0, 0)), pl.BlockSpec((C, C), lambda s: (0, 0)),
                  pl.BlockSpec((FH * FD, C), lambda s: (0, 0))],
        out_specs=[pl.BlockSpec((C, C), rev), pl.BlockSpec((1, C), lambda s: (0, 0))],
        out_shape=[jax.ShapeDtypeStruct((T, C), BF), jax.ShapeDtypeStruct((1, C), F32)],
        scratch_shapes=[pltpu.VMEM((8, C), F32)],
        compiler_params=_params(("arbitrary",)),
    )(drow, dcol, zf, bf_pad, cst["tri"], cst["pick"])


def _head_norm(r):
    rn, rs = [], []
    for h in range(RH):
        rh = r[:, RDV * h:RDV * (h + 1)]
        s = lax.rsqrt(jnp.mean(rh * rh, axis=1, keepdims=True) + EPS)
        rn.append(rh * s)
        rs.append(s)
    return jnp.concatenate(rn, axis=1), rs


def _gated(r, rg, a, fg):
    rn, _ = _head_norm(r)
    return jnp.concatenate([rn * (rg * jax.nn.sigmoid(rg)), a * (fg * jax.nn.sigmoid(fg))], axis=1)


def _out_loss(r, z, a, wout, x, tgt, fgain):
    def body(r_ref, rg_ref, a_ref, fg_ref, w_ref, x_ref, t_ref, g_ref, yt_ref, do_ref, dob_ref, loss_ref, dg_ref):
        i = pl.program_id(0)

        @pl.when(i == 0)
        def _():
            yt_ref[...] = jnp.zeros_like(yt_ref)
            do_ref[...] = jnp.zeros_like(do_ref)
            dob_ref[...] = jnp.zeros_like(dob_ref)
            loss_ref[...] = jnp.zeros_like(loss_ref)
            dg_ref[...] = jnp.zeros_like(dg_ref)

        @pl.when(i > 0)
        def _():
            y = _gated(r_ref[...], rg_ref[...], a_ref[...], fg_ref[...])
            yt_ref[...] = y.T.astype(BF)
            o = x_ref[...] + _dot(y.astype(BF), w_ref[...])
            rs = lax.rsqrt(jnp.mean(o * o, axis=1, keepdims=True) + EPS)
            on = o * rs
            g = g_ref[...]
            e = on * g - t_ref[...]
            loss_ref[...] += 0.5 * jnp.sum(jnp.mean(e * e, axis=1, keepdims=True))
            dyh = e * (1.0 / D)
            dg_ref[...] += jnp.sum(dyh * on, axis=0, keepdims=True)
            don = dyh * g
            do = rs * (don - on * jnp.mean(don * on, axis=1, keepdims=True))
            do_ref[...] = do
            dob_ref[...] = do.astype(BF)

    tok = lambda i: (jnp.maximum(i - 1, 0), 0)
    return pl.pallas_call(
        body, name="out_loss", grid=(NCH,),
        in_specs=[pl.BlockSpec((C, D), lambda i: (i, 0)), pl.BlockSpec((C, D), lambda i: (i, GB_R)),
                  pl.BlockSpec((C, D), lambda i: (_fox_pos(i), 0)), pl.BlockSpec((C, D), lambda i: (i, GB_F)),
                  pl.BlockSpec((DMIX, D), lambda i: (0, 0)),
                  pl.BlockSpec((C, D), tok), pl.BlockSpec((C, D), tok), pl.BlockSpec((1, D), lambda i: (0, 0))],
        out_specs=[pl.BlockSpec((DMIX, C), lambda i: (0, i)), pl.BlockSpec((C, D), lambda i: (i, 0)),
                   pl.BlockSpec((C, D), lambda i: (i, 0)), pl.BlockSpec((8, C), lambda i: (0, 0)),
                   pl.BlockSpec((1, D), lambda i: (0, 0))],
        out_shape=[jax.ShapeDtypeStruct((DMIX, T), BF), jax.ShapeDtypeStruct((T, D), F32),
                   jax.ShapeDtypeStruct((T, D), BF), jax.ShapeDtypeStruct((8, C), F32),
                   jax.ShapeDtypeStruct((1, D), F32)],
        compiler_params=_params(("arbitrary",)),
    )(r, z, a, z, wout, x, tgt, fgain)


def _silu_and_grad(x):
    s = jax.nn.sigmoid(x)
    return x * s, s * (1.0 + x * (1.0 - s))


def _dy_gate_bwd(dob, wout, r, z, a, seg):
    def body(do_ref, w_ref, r_ref, rg_ref, a_ref, fg_ref, seg_ref, dr_ref, da_ref, drg_ref, dfg_ref, dl_ref):
        dy = _dg(do_ref[...], w_ref[...], NT)
        a_ = a_ref[...]
        rn, rs = _head_norm(r_ref[...])
        silu_rg, dsilu_rg = _silu_and_grad(rg_ref[...])
        silu_fg, dsilu_fg = _silu_and_grad(fg_ref[...])
        dyr, dyf = dy[:, :D], dy[:, D:]
        drn = dyr * silu_rg
        drg_ref[...] = (dyr * rn * dsilu_rg).astype(BF)
        for h in range(RH):
            sl = slice(RDV * h, RDV * (h + 1))
            dh, nh = drn[:, sl], rn[:, sl]
            dr_ref[:, sl] = (rs[h] * (dh - nh * jnp.mean(dh * nh, axis=1, keepdims=True))).astype(BF)
        dab = (dyf * silu_fg).astype(BF)
        da_ref[...] = dab
        dfg_ref[...] = (dyf * a_ * dsilu_fg).astype(BF)
        prod = dab.astype(F32) * a_
        segm = seg_ref[...]
        for p in range(NPAIR):
            sl = slice(C * p, C * (p + 1))
            hi = prod[:, sl].astype(BF)
            lo = (prod[:, sl] - hi.astype(F32)).astype(BF)
            dl_ref[:, sl] = _dot(hi, segm) + _dot(lo, segm)

    row = pl.BlockSpec((C, D), lambda i: (i, 0))
    fox = pl.BlockSpec((C, D), lambda i: (_fox_pos(i), 0))
    return pl.pallas_call(
        body, name="dy_gate_bwd", grid=(NCH,),
        in_specs=[row, pl.BlockSpec((DMIX, D), lambda i: (0, 0)),
                  row, pl.BlockSpec((C, D), lambda i: (i, GB_R)),
                  fox, pl.BlockSpec((C, D), lambda i: (i, GB_F)),
                  pl.BlockSpec((C, C), lambda i: (0, 0))],
        out_specs=[row, fox, row, row, fox],
        out_shape=[jax.ShapeDtypeStruct((T, D), BF), jax.ShapeDtypeStruct((TROWS, D), BF),
                   jax.ShapeDtypeStruct((T, D), BF), jax.ShapeDtypeStruct((T, D), BF),
                   jax.ShapeDtypeStruct((TROWS, D), F32)],
        compiler_params=_params(("parallel",)),
    )(dob, wout, r, z, a, z, seg)


DZ_WIDTHS = (512, 512, 1024, 1024, 1024, 1024, 1024, 1024)


def _du_norm_bwd(dzs, dzf, wt, wft, hpad, g, dopad, parts=()):
    tm, tk = 544, 1024
    nk = WMAIN // tk
    ni = T // tm
    n = len(parts)

    def body(rq_ref, rk_ref, rv_ref, rg_ref, fq_ref, fk_ref, fv_ref, fg_ref, dzf_ref, w_ref, wf_ref, h_ref, g_ref,
             do_ref, *rest):
        part_refs, (gh_ref, dg_ref), land_refs = rest[:n], rest[n:n + 2], rest[n + 2:2 * n + 2]
        acc = rest[2 * n + 2]
        i, k = pl.program_id(0), pl.program_id(1)

        if n:
            send_sems, recv_sems = rest[2 * n + 3:]
            copies = _chip_copies(part_refs, land_refs, send_sems, recv_sems, by_dest=True)

            @pl.when((i == 0) & (k == 0))
            def _():
                for cp in copies:
                    cp.start()

            @pl.when((i == ni - 1) & (k == nk - 1))
            def _():
                for cp in copies:
                    cp.wait()

        @pl.when(k == 0)
        def _():
            acc[...] = (_dot(dzf_ref[...], wf_ref[...]) + _dot(rq_ref[...], w_ref[:512, :])
                        + _dot(rk_ref[...], w_ref[512:, :]))

        for kk, piece in enumerate((rv_ref, rg_ref, fq_ref, fk_ref, fv_ref, fg_ref), start=1):
            @pl.when(k == kk)
            def _(piece=piece):
                acc[...] += _dot(piece[...], w_ref[...])

        @pl.when(k == nk - 1)
        def _():
            du = acc[...]
            h = h_ref[...]
            gg = g_ref[...]
            rs = lax.rsqrt(jnp.mean(h * h, axis=1, keepdims=True) + EPS)
            hn = h * rs
            part = jnp.sum(du * hn, axis=0, keepdims=True)

            @pl.when(i == 0)
            def _():
                dg_ref[...] = part

            @pl.when(i > 0)
            def _():
                dg_ref[...] += part

            dhn = du * gg
            gh_ref[...] = rs * (dhn - hn * jnp.mean(dhn * hn, axis=1, keepdims=True)) + do_ref[...]

    sems = [pltpu.SemaphoreType.DMA((3 * n,)), pltpu.SemaphoreType.DMA((3 * n,))] if n else []
    return pl.pallas_call(
        body, name="du_norm_bwd", grid=(ni, nk),
        in_specs=[pl.BlockSpec((tm, w), lambda i, k: (i, 0)) for w in DZ_WIDTHS]
        + [pl.BlockSpec((tm, C), lambda i, k: (i, 0)),
           pl.BlockSpec((tk, D), lambda i, k: (k, 0)), pl.BlockSpec((C, D), lambda i, k: (0, 0)),
           pl.BlockSpec((tm, D), lambda i, k: (i, 0)), pl.BlockSpec((1, D), lambda i, k: (0, 0)),
           pl.BlockSpec((tm, D), lambda i, k: (i, 0))] + [ANY] * n,
        out_specs=[pl.BlockSpec((tm, D), lambda i, k: (i, 0)), pl.BlockSpec((1, D), lambda i, k: (0, 0))] + [ANY] * n,
        out_shape=[jax.ShapeDtypeStruct((T, D), F32), jax.ShapeDtypeStruct((1, D), F32)]
        + [jax.ShapeDtypeStruct(p.shape, p.dtype) for p in parts],
        scratch_shapes=[pltpu.VMEM((tm, D), F32)] + sems,
        compiler_params=_params(("arbitrary", "arbitrary")),
    )(*dzs, dzf, wt, wft, hpad, g, dopad, *parts)


GROWS = 7680


def _dw_in(dzs, dzf, ut):
    tn = 512
    nmain = WMAIN // tn
    first, blocks = [], []
    for w in DZ_WIDTHS:
        first.append(sum(blocks))
        blocks.append(w // tn)

    def body(rq_ref, rk_ref, rv_ref, rg_ref, fq_ref, fk_ref, fv_ref, fg_ref, dzf_ref, ut_ref, o_ref):
        gidx = pl.program_id(0)
        for piece, g0, nb in zip((rq_ref, rk_ref, rv_ref, rg_ref, fq_ref, fk_ref, fv_ref, fg_ref), first, blocks):
            @pl.when((gidx >= g0) & (gidx < g0 + nb))
            def _(piece=piece):
                o_ref[...] = _dot(ut_ref[...], piece[...]).T.astype(BF)

        @pl.when(gidx == nmain)
        def _():
            o_ref[:C, :] = _dot(ut_ref[...], dzf_ref[...]).T.astype(BF)
            o_ref[C:, :] = jnp.zeros((tn - C, D), BF)

    def piece_spec(g0, nb):
        return pl.BlockSpec((T, tn), lambda gidx: (0, jnp.clip(gidx - g0, 0, nb - 1)))

    return pl.pallas_call(
        body, name="dw_in", grid=(nmain + 1,),
        in_specs=[piece_spec(g0, nb) for g0, nb in zip(first, blocks)]
        + [pl.BlockSpec((T, C), lambda gidx: (0, 0)), pl.BlockSpec((D, T), lambda gidx: (0, 0))],
        out_specs=pl.BlockSpec((tn, D), lambda gidx: (gidx, 0)),
        out_shape=jax.ShapeDtypeStruct((GROWS, D), BF),
        compiler_params=pltpu.CompilerParams(dimension_semantics=("arbitrary",), vmem_limit_bytes=DW_VMEM_LIMIT),
    )(*dzs, dzf, ut)


def _token_order(x_po):
    def body(i_ref, o_ref):
        o_ref[...] = i_ref[...]

    return pl.pallas_call(
        body, name="token_order", grid=(NCH,),
        in_specs=[pl.BlockSpec((C, D), lambda i: (_fox_pos(i), 0))],
        out_specs=pl.BlockSpec((C, D), lambda i: (i, 0)),
        out_shape=jax.ShapeDtypeStruct((T, D), x_po.dtype),
        compiler_params=_params(("parallel",)),
    )(x_po)


def _local_step(x, tgt, normed, norm_g, wt, wft, b_f, wout, final_g, reduce_scatter=False, wout_full=None):
    cst = _constants()
    hpad, u, ut = normed
    bf_pad = jnp.pad(b_f, ((0, 0), (0, C - NFF)))
    z = _mm_nt(u, wt, WMAIN, T // 2, 1024, "in_proj")
    zf = _mm_nt(u, wft, C, T // 2, C, "in_proj_ff")
    r, sprev = _ret_fwd(z, cst)
    ct = _fox_prep(zf, bf_pad, cst)
    if wout_full is None:
        a, g = _fox_fwd(z, ct, cst, None)
    else:
        a, g, landed_wout = _fox_fwd(z, ct, cst, wout)
        wout = wout_full(landed_wout)
    yt, dopad, dob, loss8, dfg = _out_loss(r, z, a, wout, x, tgt, final_g)
    dr, da, dzrg, dzfg, delta = _dy_gate_bwd(dob, wout, r, z, a, cst["seg"])
    dwout = _mm_nn(yt, dob, 512, D, "dw_out", BF)
    dzq_r, dzk_r, dzv_r = _ret_bwd(z, cst, sprev, dr)
    p_out = []
    if reduce_scatter:
        g_out = dwout.reshape(4, DMIX // 4, D)
        p_out = [_add_halves(g_out, *_pair_swap(None, [g_out], "rs_pair_swap_out"), "pair_add_out", BF)]
    dq_po, drow, dzk_f, dzv_f, dcol, *e_out = _fox_bwd(z, da, g, delta, ct, cst, p_out)
    dzf, dbf = _fox_gate_bwd(drow, dcol, zf, bf_pad, cst)
    dzs = [dzq_r, dzk_r, dzv_r, dzrg, _token_order(dq_po), dzk_f, dzv_f, dzfg]
    gwt = _dw_in(dzs, dzf, ut)
    p_in = [_add_windows(gwt, *_pair_swap(gwt, [], "rs_pair_swap_in"))] if reduce_scatter else []
    gh, dng, *e_in = _du_norm_bwd(dzs, dzf, wt, wft, hpad, norm_g, dopad, p_in)
    return (loss8[0, 0], gh[C:], gh[PAD:C], dng, gwt, dbf[:, :NFF], dwout, dfg, p_in + p_out, e_in + e_out)


WOFF, WLEN = 1792, 2048
WHALF = WLEN // 2
LAP = WPADROWS - WOFF


def _own_window(w3):
    rows, sub, lanes = w3.shape
    pad = WPADROWS - rows
    tb = 96
    nb = WPADROWS // tb
    half = rows // 2

    def body(w_ref, o_ref, buf, sems):
        x, y, _ = _place()
        shift = 4 * (2 * x + y)
        buf[pl.ds(0, pad)] = jnp.zeros((pad, sub, lanes), F32)
        buf[pl.ds(rows, pad)] = jnp.zeros((pad, sub, lanes), F32)
        cps = [pltpu.make_async_copy(w_ref.at[pl.ds(half * h, half)], buf.at[pl.ds(shift + half * h, half)],
                                     sems.at[h]) for h in range(2)]
        for cp in cps:
            cp.start()

        def block(i, carry):
            r0 = pl.multiple_of(i * tb, tb)
            o_ref[pl.ds(r0, tb), :] = buf[pl.ds(r0, tb)].reshape(tb, sub * lanes).astype(BF)
            return carry

        cps[0].wait()
        lax.fori_loop(0, half // tb, block, 0)
        cps[1].wait()
        lax.fori_loop(half // tb, nb, block, 0)

    return pl.pallas_call(
        body, name="own_window",
        in_specs=[ANY], out_shape=jax.ShapeDtypeStruct((WPADROWS, sub * lanes), BF),
        scratch_shapes=[pltpu.VMEM((WPADROWS, sub, lanes), F32), pltpu.SemaphoreType.DMA((2,))],
        compiler_params=pltpu.CompilerParams(vmem_limit_bytes=VMEM_LIMIT),
    )(w3)


def _gather_weights(own_win, meta, x, norm_g):
    half_main, half_lap, half_meta = WOFF // 2, LAP // 2, meta.shape[0] // 2
    last = NCH - 1

    def body(win_ref, meta_ref, x_ref, g_ref, w_ref, laps_ref, gm_ref, h_ref, u_ref, ut_ref,
             send_sems, recv_sems, local_sems, stage, lapbuf, headbuf, metabuf):
        step = pl.program_id(0)
        x, y, c = _place()
        me_s = 2 * x + y
        sib = (x, y, 1 - c)
        chips = _other_chips(x, y)

        def emit(h):
            u = _norm_rows(h, g_ref[...])
            h_ref[...] = h
            u_ref[...] = u.astype(BF)
            ut_ref[...] = u.T.astype(BF)

        kinds = [
            (lambda h: win_ref.at[pl.ds(half_main * h, half_main)],
             lambda s, h: w_ref.at[pl.ds(WOFF * s + half_main * h, half_main)]),
            (lambda h: win_ref.at[pl.ds(WOFF + half_lap * h, half_lap)],
             lambda s, h: laps_ref.at[s, pl.ds(half_lap * h, half_lap)]),
            (lambda h: meta_ref.at[pl.ds(half_meta * h, half_meta)],
             lambda s, h: gm_ref.at[s, pl.ds(half_meta * h, half_meta)]),
        ]
        own_in = pltpu.make_async_copy(win_ref.at[pl.ds(0, WOFF)], stage, local_sems.at[0])
        own_lap_in = pltpu.make_async_copy(win_ref.at[pl.ds(WOFF, LAP)], lapbuf.at[0], local_sems.at[1])
        own_out = pltpu.make_async_copy(stage, w_ref.at[pl.ds(WOFF * me_s, WOFF)], local_sems.at[0])
        own_lap_out = pltpu.make_async_copy(lapbuf.at[0], laps_ref.at[me_s], local_sems.at[1])
        sends, arrivals, forwards, forwarded = [], [], [], []
        for a, (src, dst) in enumerate(kinds):
            for k, (cx, cy, cs) in enumerate(chips):
                there = dict(send_sem=send_sems.at[6 * a + k], recv_sem=recv_sems.at[6 * a + k],
                             device_id=(cx, cy, c), device_id_type=MESH)
                across = dict(send_sem=send_sems.at[6 * a + 3 + k], recv_sem=recv_sems.at[6 * a + 3 + k],
                              device_id=sib, device_id_type=MESH)
                sends.append(pltpu.make_async_remote_copy(src_ref=src(c), dst_ref=dst(me_s, c), **there))
                arrivals.append(pltpu.make_async_remote_copy(src_ref=dst(cs, c), dst_ref=dst(cs, c), **there))
                forwards.append(pltpu.make_async_remote_copy(src_ref=dst(cs, c), dst_ref=dst(cs, c), **across))
                forwarded.append(pltpu.make_async_remote_copy(
                    src_ref=dst(cs, 1 - c), dst_ref=dst(cs, 1 - c), **across))

        @pl.when(step == 0)
        def _():
            own_in.start()
            own_lap_in.start()
            for cp in sends:
                cp.start()
            own_in.wait()
            own_out.start()
            own_lap_in.wait()
            own_lap_out.start()

        @pl.when(step < last)
        def _():
            emit(x_ref[...])

        @pl.when(step == last)
        def _():
            for cp, fwd in zip(arrivals, forwards):
                cp.wait_recv()
                fwd.start()
            for cp in forwarded:
                cp.wait_recv()
            for cp in sends + forwards:
                cp.wait_send()
            own_out.wait()
            own_lap_out.wait()
            for s in range(1, 4):
                head = w_ref.at[pl.ds(WOFF * s, LAP)]
                loads = [pltpu.make_async_copy(laps_ref.at[s - 1], lapbuf.at[1], local_sems.at[2]),
                         pltpu.make_async_copy(head, headbuf, local_sems.at[3])]
                for cp in loads:
                    cp.start()
                for cp in loads:
                    cp.wait()
                headbuf[...] = (headbuf[...].astype(F32) + lapbuf[1].astype(F32)).astype(BF)
                store = pltpu.make_async_copy(headbuf, head, local_sems.at[3])
                store.start()
                store.wait()
            loads = [pltpu.make_async_copy(meta_ref, metabuf.at[me_s], local_sems.at[0])]
            loads += [pltpu.make_async_copy(gm_ref.at[cs], metabuf.at[cs], local_sems.at[1 + k])
                      for k, (_, _, cs) in enumerate(chips)]
            for cp in loads:
                cp.start()
            for cp in loads:
                cp.wait()
            tokens = jnp.concatenate([metabuf[s] for s in range(4)], axis=1)
            emit(jnp.concatenate([jnp.zeros((PAD, D), F32), tokens], axis=0))

    def chunk(i):
        return (i + 1) % NCH

    return pl.pallas_call(
        body, name="all_gather_w", grid=(NCH,),
        in_specs=[ANY, ANY, pl.BlockSpec((C, D), lambda i: (jnp.minimum(i, last - 1), 0)),
                  pl.BlockSpec((1, D), lambda i: (0, 0))],
        out_specs=[ANY] * 3 + [pl.BlockSpec((C, D), lambda i: (chunk(i), 0))] * 2
        + [pl.BlockSpec((D, C), lambda i: (0, chunk(i)))],
        out_shape=[jax.ShapeDtypeStruct((WMAIN, D), own_win.dtype), jax.ShapeDtypeStruct((4, LAP, D), own_win.dtype),
                   jax.ShapeDtypeStruct((4,) + meta.shape, meta.dtype),
                   jax.ShapeDtypeStruct((T, D), F32), jax.ShapeDtypeStruct((T, D), BF),
                   jax.ShapeDtypeStruct((D, T), BF)],
        scratch_shapes=[pltpu.SemaphoreType.DMA((18,)), pltpu.SemaphoreType.DMA((18,)), pltpu.SemaphoreType.DMA((4,)),
                        pltpu.VMEM((WOFF, D), own_win.dtype), pltpu.VMEM((2, LAP, D), own_win.dtype),
                        pltpu.VMEM((LAP, D), own_win.dtype), pltpu.VMEM((4,) + meta.shape, meta.dtype)],
        compiler_params=_params(("arbitrary",)),
    )(own_win, meta, x, norm_g)


def _pair_swap(gwt, arrs, name):
    n = len(arrs)
    wins = [] if gwt is None else [gwt]
    m = n + len(wins)
    nsem = n + 4 * len(wins)

    def body(*refs):
        ins, outs, (send_sems, recv_sems) = refs[:m], refs[m:2 * m], refs[2 * m:]
        x, y, c = _place()
        sib = dict(device_id=(x, y, 1 - c), device_id_type=MESH)
        cps = []
        for k in range(4 * len(wins)):
            cps.append(pltpu.make_async_remote_copy(
                src_ref=ins[n].at[pl.ds(WOFF * k + (1 - c) * WHALF, WHALF)], dst_ref=outs[n].at[k],
                send_sem=send_sems.at[n + k], recv_sem=recv_sems.at[n + k], **sib))
        for a in range(n):
            rows = ins[a].shape[1] // 2
            cps.append(pltpu.make_async_remote_copy(
                src_ref=ins[a].at[:, pl.ds((1 - c) * rows, rows)], dst_ref=outs[a],
                send_sem=send_sems.at[a], recv_sem=recv_sems.at[a], **sib))
        for cp in cps:
            cp.start()
        for cp in cps:
            cp.wait()

    return pl.pallas_call(
        body, name=name,
        in_specs=[ANY] * m, out_specs=[ANY] * m,
        out_shape=[jax.ShapeDtypeStruct((4, a.shape[1] // 2, a.shape[2]), a.dtype) for a in arrs]
        + [jax.ShapeDtypeStruct((4, WHALF, D), w.dtype) for w in wins],
        scratch_shapes=[pltpu.SemaphoreType.DMA((nsem,)), pltpu.SemaphoreType.DMA((nsem,))],
    )(*arrs, *wins)


def _add_windows(gwt, recv):
    tb = 256
    nb = WHALF // tb
    c = lax.axis_index("c")

    def body(c_ref, a_ref, b_ref, o_ref):
        o_ref[0] = (a_ref[...].astype(F32) + b_ref[0].astype(F32)).astype(BF)

    return pl.pallas_call(
        body, name="pair_add_in",
        grid_spec=pltpu.PrefetchScalarGridSpec(
            num_scalar_prefetch=1, grid=(4, nb),
            in_specs=[pl.BlockSpec((tb, D), lambda k, i, cr: ((WOFF // tb) * k + nb * cr[0] + i, 0)),
                      pl.BlockSpec((1, tb, D), lambda k, i, cr: (k, i, 0))],
            out_specs=pl.BlockSpec((1, tb, D), lambda k, i, cr: (k, i, 0))),
        out_shape=jax.ShapeDtypeStruct(recv.shape, BF),
        compiler_params=_params(("parallel", "parallel")),
    )(jnp.reshape(c, (1,)).astype(jnp.int32), gwt, recv)


def _chip_exchange(parts, small):
    n = len(parts)

    def body(*refs):
        ins, sm = refs[:n], refs[n]
        outs, smo = refs[n + 1:2 * n + 1], refs[2 * n + 1]
        send_sems, recv_sems = refs[2 * n + 2:]
        cps = _chip_copies(ins, outs, send_sems, recv_sems, by_dest=True)
        cps += _chip_copies([sm], [smo], send_sems.at[pl.ds(3 * n, 3)], recv_sems.at[pl.ds(3 * n, 3)], by_dest=False)
        for cp in cps:
            cp.start()
        for cp in cps:
            cp.wait()

    return pl.pallas_call(
        body, name="rs_chip_exchange",
        in_specs=[ANY] * (n + 1), out_specs=[ANY] * (n + 1),
        out_shape=[jax.ShapeDtypeStruct(p.shape, p.dtype) for p in parts]
        + [jax.ShapeDtypeStruct((4,) + small.shape, small.dtype)],
        scratch_shapes=[pltpu.SemaphoreType.DMA((3 * (n + 1),)), pltpu.SemaphoreType.DMA((3 * (n + 1),))],
    )(*parts, small)


def _pair_send(halves):
    n = len(halves)

    def body(*refs):
        ins, outs = refs[:n], refs[n:2 * n]
        send_sems, recv_sems = refs[2 * n:]
        x, y, c = _place()
        cps = [pltpu.make_async_remote_copy(
            src_ref=ins[a], dst_ref=outs[a], send_sem=send_sems.at[a], recv_sem=recv_sems.at[a],
            device_id=(x, y, 1 - c), device_id_type=MESH) for a in range(n)]
        for cp in cps:
            cp.start()
        for cp in cps:
            cp.wait()

    return pl.pallas_call(
        body, name="rs_pair_send",
        in_specs=[ANY] * n, out_specs=[ANY] * n,
        out_shape=[jax.ShapeDtypeStruct(h.shape, h.dtype) for h in halves],
        scratch_shapes=[pltpu.SemaphoreType.DMA((n,)), pltpu.SemaphoreType.DMA((n,))],
    )(*halves)


def _row_block(rows):
    for tb in (256, 128, 64, 32, 16, 8):
        if rows % tb == 0:
            return tb
    return rows


def _add_halves(full, recv, name, out_dtype):
    _, r2, w = recv.shape
    tb = _row_block(r2)
    nb = r2 // tb
    c = lax.axis_index("c")

    def body(c_ref, a_ref, b_ref, o_ref):
        o_ref[...] = (a_ref[...].astype(F32) + b_ref[...].astype(F32)).astype(o_ref.dtype)

    return pl.pallas_call(
        body, name=name,
        grid_spec=pltpu.PrefetchScalarGridSpec(
            num_scalar_prefetch=1, grid=(4, nb),
            in_specs=[pl.BlockSpec((1, tb, w), lambda s, i, cr: (s, cr[0] * nb + i, 0)),
                      pl.BlockSpec((1, tb, w), lambda s, i, cr: (s, i, 0))],
            out_specs=pl.BlockSpec((1, tb, w), lambda s, i, cr: (s, i, 0))),
        out_shape=jax.ShapeDtypeStruct(recv.shape, out_dtype),
        compiler_params=_params(("parallel", "parallel")),
    )(jnp.reshape(c, (1,)).astype(jnp.int32), full, recv)


def _add2(a, b, name):
    def body(a_ref, b_ref, o_ref):
        o_ref[...] = a_ref[...] + b_ref[...]

    return pl.pallas_call(body, name=name, out_shape=jax.ShapeDtypeStruct(a.shape, a.dtype))(a, b)


def _sum4(buf, own, name):
    _, r, w = buf.shape
    tb = _row_block(r)
    me_s = 2 * lax.axis_index("x") + lax.axis_index("y")
    by_dest = own.ndim == 3

    def body(s_ref, b_ref, own_ref, o_ref):
        mine = (own_ref[0] if by_dest else own_ref[...]).astype(F32)
        terms = [jnp.where(s_ref[0] == t, mine, b_ref[t].astype(F32)) for t in range(4)]
        o_ref[...] = ((terms[0] + terms[1]) + terms[2]) + terms[3]

    own_spec = (pl.BlockSpec((1, tb, w), lambda i, sr: (sr[0], i, 0)) if by_dest
                else pl.BlockSpec((tb, w), lambda i, sr: (i, 0)))
    return pl.pallas_call(
        body, name=name,
        grid_spec=pltpu.PrefetchScalarGridSpec(
            num_scalar_prefetch=1, grid=(r // tb,),
            in_specs=[pl.BlockSpec((4, tb, w), lambda i, sr: (0, i, 0)), own_spec],
            out_specs=pl.BlockSpec((tb, w), lambda i, sr: (i, 0))),
        out_shape=jax.ShapeDtypeStruct((r, w), F32),
        compiler_params=_params(("parallel",)),
    )(jnp.reshape(me_s, (1,)).astype(jnp.int32), buf, own)


def _adamw_math(w, g, m, v):
    mn = B1 * m + (1.0 - B1) * g
    vn = B2 * v + (1.0 - B2) * (g * g)
    m_hat = mn / (1.0 - B1 ** STEP)
    v_hat = vn / (1.0 - B2 ** STEP)
    return -LR * (m_hat / (jnp.sqrt(v_hat) + AEPS) + WD * w), mn, vn


def _adamw(w, g, m, v, name):
    r, c_ = w.shape
    tb = _row_block(r)
    if tb == r and r > 512:
        tb = 256

    def body(w_ref, g_ref, m_ref, v_ref, d_ref, mo_ref, vo_ref):
        d_ref[...], mo_ref[...], vo_ref[...] = _adamw_math(w_ref[...], g_ref[...], m_ref[...], v_ref[...])

    spec = pl.BlockSpec((tb, c_), lambda i: (i, 0))
    return pl.pallas_call(
        body, name=name, grid=(pl.cdiv(r, tb),),
        in_specs=[spec] * 4, out_specs=[spec] * 3,
        out_shape=[jax.ShapeDtypeStruct(w.shape, F32)] * 3,
        compiler_params=_params(("parallel",)),
    )(w, g, m, v)


def _adamw_rows(w, g_mine, g_sib, m, v, name):
    r = w.shape[0]
    tb = 256
    sub, lanes = w.shape[1:]
    nh = g_mine.shape[0] // tb
    nsteps = pl.cdiv(r, tb)
    assert nsteps <= 2 * nh and 4 * 3 + r <= 2 * nh * tb
    x, y, c = _place()
    place = jnp.stack([c, 4 * (2 * x + y)]).astype(jnp.int32)

    def body(p_ref, w_ref, mc_ref, sc_ref, mn_ref, sn_ref, m_ref, v_ref, go_ref, d_ref, mo_ref, vo_ref, buf):
        i = pl.program_id(0)
        for at, blk, mine_ref, sib_ref in ((0, i, mc_ref, sc_ref), (1, jnp.minimum(i + 1, 2 * nh - 1), mn_ref, sn_ref)):
            rows = jnp.where(blk // nh == p_ref[0], mine_ref[...], sib_ref[...])
            buf[tb * at:tb * (at + 1)] = rows.reshape(tb, sub, lanes)
        g = buf[pl.ds(p_ref[1], tb)]
        go_ref[...] = g
        d_ref[...], mo_ref[...], vo_ref[...] = _adamw_math(w_ref[...], g, m_ref[...], v_ref[...])

    def half_spec(ahead, sibling):
        def index(i, pr):
            half = (1 - pr[0]) if sibling else pr[0]
            return (jnp.clip(jnp.minimum(i + ahead, 2 * nh - 1) - nh * half, 0, nh - 1), 0)
        return pl.BlockSpec((tb, sub * lanes), index)

    spec = pl.BlockSpec((tb, sub, lanes), lambda i, pr: (i, 0, 0))
    return pl.pallas_call(
        body, name=name,
        grid_spec=pltpu.PrefetchScalarGridSpec(
            num_scalar_prefetch=1, grid=(nsteps,),
            in_specs=[spec, half_spec(0, False), half_spec(0, True), half_spec(1, False), half_spec(1, True),
                      spec, spec],
            out_specs=[spec] * 4,
            scratch_shapes=[pltpu.VMEM((2 * tb, sub, lanes), F32)]),
        out_shape=[jax.ShapeDtypeStruct(w.shape, F32)] * 4,
        compiler_params=_params(("parallel",)),
    )(place, w, g_mine, g_sib, g_mine, g_sib, m, v)


def _adamw_halves(w, g_mine, g_sib, m, v, name):
    r, c_ = w.shape
    r2 = g_mine.shape[0]
    tb = _row_block(r2)
    nb = r2 // tb
    c = lax.axis_index("c")

    def body(c_ref, w_ref, gm_ref, gs_ref, m_ref, v_ref, g_ref, d_ref, mo_ref, vo_ref):
        g = jnp.where(pl.program_id(0) == c_ref[0], gm_ref[...], gs_ref[...])
        g_ref[...] = g
        d_ref[...], mo_ref[...], vo_ref[...] = _adamw_math(w_ref[...], g, m_ref[...], v_ref[...])

    full = pl.BlockSpec((tb, c_), lambda h, i, cr: (h * nb + i, 0))
    half = pl.BlockSpec((tb, c_), lambda h, i, cr: (i, 0))
    return pl.pallas_call(
        body, name=name,
        grid_spec=pltpu.PrefetchScalarGridSpec(
            num_scalar_prefetch=1, grid=(2, nb),
            in_specs=[full, half, half, full, full], out_specs=[full] * 4),
        out_shape=[jax.ShapeDtypeStruct(w.shape, F32)] * 4,
        compiler_params=_params(("parallel", "parallel")),
    )(jnp.reshape(c, (1,)).astype(jnp.int32), w, g_mine, g_sib, m, v)


def kernel(x, meta_tokens, norm_g, w_in, b_f, w_out, final_g, loss_target, m_meta_tokens, m_norm_g, m_w_in, m_b_f, m_w_out, m_final_g, v_meta_tokens, v_norm_g, v_w_in, v_b_f, v_w_out, v_final_g):
    me_s = 2 * lax.axis_index("x") + lax.axis_index("y")
    w3, m3, v3 = [jnp.transpose(jnp.reshape(t[0], (D // C, C, WSH)), (2, 0, 1)) for t in (w_in, m_w_in, v_w_in)]

    wt_main, laps, _, *normed = _gather_weights(_own_window(w3), meta_tokens, x[0], norm_g)
    wft = jnp.pad(laps[3, :NFF], ((0, C - NFF), (0, 0)))
    mine = (jnp.arange(4) == me_s)[:, None, None]
    wout_own = w_out[0].astype(BF)

    def wout_full(landed):
        return jnp.where(mine, wout_own[None], landed).reshape(DMIX, D)

    loss, gx, dmeta, dng, gwt, dbf, dwout, dfg, (p_in, p_out), (e_in, e_out) = _local_step(
        x[0], loss_target[0], normed, norm_g, wt_main, wft, b_f, wout_own, final_g.reshape(1, D), True, wout_full)

    g_meta = jnp.stack([dmeta[:, 256 * s:256 * (s + 1)] for s in range(4)])
    small = jnp.concatenate([dng, dfg, jnp.pad(dbf, ((0, 0), (0, D - NFF))),
                             jnp.pad(jnp.reshape(loss, (1, 1)), ((0, 0), (0, D - 1))),
                             jnp.zeros((4, D), F32)], axis=0)
    e_meta, e_small = _chip_exchange([g_meta], small)
    h_in, h_out = _sum4(e_in, p_in, "sum_in"), _sum4(e_out, p_out, "sum_out")
    h_meta, h_small = _sum4(e_meta, g_meta, "sum_meta"), _sum4(e_small, small, "sum_small")
    s_in, s_out, s_meta, s_small = _pair_send([h_in, h_out, h_meta, h_small])
    gw_meta = _add2(h_meta, s_meta, "pair_add_meta")
    tot = _add2(h_small, s_small, "pair_add_small")
    g_norm, g_final, g_bf, loss_all = tot[0:1], tot[1], tot[2:3, :NFF], tot[3, 0]

    d_meta, nm_meta, nv_meta = _adamw(meta_tokens, gw_meta, m_meta_tokens, v_meta_tokens, "adamw_meta")
    d_norm, nm_norm, nv_norm = _adamw(norm_g, g_norm, m_norm_g, v_norm_g, "adamw_norm")
    outs_in = _adamw_rows(w3, h_in, s_in, m3, v3, "adamw_in")
    gw_in, d_in, nm_in, nv_in = [jnp.reshape(jnp.transpose(t, (1, 2, 0)), (1, D, WSH)) for t in outs_in]
    d_bf, nm_bf, nv_bf = _adamw(b_f, g_bf, m_b_f, v_b_f, "adamw_bf")
    gw_out, d_out, nm_out, nv_out = _adamw_halves(w_out[0], h_out, s_out, m_w_out[0], v_w_out[0], "adamw_out")
    d_fin, nm_fin, nv_fin = _adamw(final_g.reshape(1, D), g_final.reshape(1, D), m_final_g.reshape(1, D),
                                   v_final_g.reshape(1, D), "adamw_final")
    return (loss_all, gx[None], gw_meta, g_norm, gw_in, g_bf, gw_out[None], g_final,
            d_meta, d_norm, d_in, d_bf, d_out[None], d_fin.reshape(D),
            nm_meta, nm_norm, nm_in, nm_bf, nm_out[None], nm_fin.reshape(D),
            nv_meta, nv_norm, nv_in, nv_bf, nv_out[None], nv_fin.reshape(D))
```

```python
import numpy as np
import jax
import jax.numpy as jnp
from jax import lax
from jax.experimental import pallas as pl
from jax.experimental.pallas import tpu as pltpu

D = 1024
SEQ = 2048
NMETA = 16
C = 128
PAD = C - NMETA
T = PAD + NMETA + SEQ
NCH = T // C
RH, RDK, RDV = 4, 128, 256
FH, FD = 16, 64
NPAIR = FH // 2
WMAIN = 7168
NFF = 16
WIN = WMAIN + NFF
WSH = WIN // 4
WPADROWS = 1824
DMIX = 2048
EPS = 1e-6
NEG = -1e30
RSCALE = RDK ** -0.5
FSCALE = FD ** -0.5
ROPE_BASE = 10000.0
LR, B1, B2, AEPS, WD, STEP = 0.001, 0.9, 0.999, 1e-08, 0.01, 10

BF = jnp.bfloat16
F32 = jnp.float32
NT = (((1,), (1,)), ((), ()))
TN = (((0,), (0,)), ((), ()))
NN_DIMS = (((1,), (0,)), ((), ()))
MESH = pl.DeviceIdType.MESH
ANY = pl.BlockSpec(memory_space=pl.ANY)
VMEM_LIMIT = 48 * 1024 * 1024
DW_VMEM_LIMIT = 56 * 1024 * 1024

GB_R, GB_F = 2, 6
QB_F, KB_F, VB_F = 24, 32, 40


def _dot(a, b):
    return jnp.dot(a, b, preferred_element_type=F32)


def _dg(a, b, dims):
    return lax.dot_general(a, b, dims, preferred_element_type=F32)


def _params(sem=None):
    return pltpu.CompilerParams(dimension_semantics=sem, vmem_limit_bytes=VMEM_LIMIT)


def _constants():
    pos = jnp.arange(T, dtype=F32) - PAD
    inv = ROPE_BASE ** (-jnp.arange(0, RDK, 2, dtype=F32) / RDK)
    ang = pos[:, None] * inv[None, :]
    cos, sin = jnp.cos(ang), jnp.sin(ang)
    cos2 = jnp.concatenate([cos, cos], axis=1)
    sin2 = jnp.concatenate([-sin, sin], axis=1)
    log_gamma = jnp.log1p(-jnp.exp2(-5.0 - jnp.arange(RH, dtype=F32)))
    idx = jnp.arange(C, dtype=F32)
    diff = idx[:, None] - idx[None, :]
    dmask = jnp.where(diff[None] >= 0, jnp.exp(log_gamma[:, None, None] * jnp.maximum(diff, 0.0)[None]), 0.0)
    zeta = jnp.exp(log_gamma[:, None] * (C - 1.0 - idx)[None, :])
    xi = jnp.exp(log_gamma[:, None] * (idx + 1.0)[None, :])
    gdec = jnp.exp(log_gamma * C)
    zeta_b = jnp.broadcast_to(zeta[:, :, None], (RH, C, RDK))
    xi_b = jnp.broadcast_to(xi[:, :, None], (RH, C, RDK))
    gdec_b = jnp.broadcast_to(gdec[:, None, None], (RH, RDK, RDV))
    tri = jnp.asarray(np.tril(np.ones((C, C), np.float32)), dtype=BF)
    head_of_lane = np.arange(FH * FD) // FD
    pick = ((np.arange(FH * FD)[:, None] % FD == 0)
            & (head_of_lane[:, None] == np.arange(C)[None, :])).astype(np.float32)
    seg = (np.arange(C)[:, None] // FD == np.arange(C)[None, :] // FD).astype(np.float32)
    ones_aug = np.concatenate([np.tile((np.arange(C) < FD)[None, :], (C, 1)),
                               np.tile((np.arange(C) >= FD)[None, :], (C, 1))], axis=0).astype(np.float32)
    lane = np.arange(2 * C) % C
    causal = np.where(lane[None, :] <= np.arange(C)[:, None], 0.0, NEG).astype(np.float32)
    mask_bias = np.stack([np.zeros((C, 2 * C), np.float32), causal])
    return dict(cos2=cos2, sin2=sin2, dmask=dmask, zeta=zeta_b, xi=xi_b, gdec=gdec_b, tri=tri,
                mask_bias=jnp.asarray(mask_bias), pick=jnp.asarray(pick, dtype=BF), seg=jnp.asarray(seg, dtype=BF),
                ones_aug=jnp.asarray(ones_aug, dtype=BF))


def _norm_rows(h, g):
    return h * lax.rsqrt(jnp.mean(h * h, axis=1, keepdims=True) + EPS) * g


def _mm_nt(a, b, n, tm, tn, name):
    m, k = a.shape

    def body(a_ref, b_ref, o_ref):
        o_ref[...] = _dg(a_ref[...], b_ref[...], NT)

    return pl.pallas_call(
        body, name=name, grid=(m // tm, n // tn),
        in_specs=[pl.BlockSpec((tm, k), lambda i, j: (i, 0)), pl.BlockSpec((tn, k), lambda i, j: (j, 0))],
        out_specs=pl.BlockSpec((tm, tn), lambda i, j: (i, j)),
        out_shape=jax.ShapeDtypeStruct((m, n), F32),
        compiler_params=_params(("parallel", "parallel")),
    )(a, b)


def _mm_nn(a, b, tm, tn, name, out_dtype=F32):
    m, k = a.shape
    _, n = b.shape

    def body(a_ref, b_ref, o_ref):
        o_ref[...] = _dot(a_ref[...], b_ref[...]).astype(out_dtype)

    return pl.pallas_call(
        body, name=name, grid=(m // tm, n // tn),
        in_specs=[pl.BlockSpec((tm, k), lambda i, j: (i, 0)), pl.BlockSpec((k, tn), lambda i, j: (0, j))],
        out_specs=pl.BlockSpec((tm, tn), lambda i, j: (i, j)),
        out_shape=jax.ShapeDtypeStruct((m, n), out_dtype),
        compiler_params=_params(("parallel", "parallel")),
    )(a, b)


def _rot(x, cos2, sin2):
    return x * cos2 + pltpu.roll(x, 64, 1) * sin2


def _ret_specs(chunk):
    whole = lambda shape: pl.BlockSpec(shape, lambda n: (0,) * len(shape))
    return [
        pl.BlockSpec((C, RH * RDK), lambda n: (chunk(n), 0)),
        pl.BlockSpec((C, RH * RDK), lambda n: (chunk(n), 1)),
        pl.BlockSpec((C, RH * RDV), lambda n: (chunk(n), 1)),
        pl.BlockSpec((C, RDK), lambda n: (chunk(n), 0)),
        pl.BlockSpec((C, RDK), lambda n: (chunk(n), 0)),
        whole((RH, C, C)), whole((RH, C, RDK)), whole((RH, C, RDK)), whole((RH, RDK, RDV)),
    ]


def _ret_heads(q_ref, k_ref, v_ref, cos, sin):
    qr = [_rot(q_ref[:, RDK * h:RDK * (h + 1)], cos, sin) for h in range(RH)]
    kr = [_rot(k_ref[:, RDK * h:RDK * (h + 1)], cos, sin) * RSCALE for h in range(RH)]
    vb = [v_ref[:, RDV * h:RDV * (h + 1)].astype(BF) for h in range(RH)]
    return qr, kr, [t.astype(BF) for t in qr], [t.astype(BF) for t in kr], vb


def _ret_fwd(z, cst):
    def body(q_ref, k_ref, v_ref, cos_ref, sin_ref, dm_ref, xi_ref, zt_ref, gd_ref, r_ref, sp_ref, st):
        n = pl.program_id(0)

        @pl.when(n == 0)
        def _():
            st[...] = jnp.zeros_like(st)

        hs = range(RH)
        qr, kr, qb, kb, vb = _ret_heads(q_ref, k_ref, v_ref, cos_ref[...], sin_ref[...])
        sd = [(_dg(qb[h], kb[h], NT) * dm_ref[h]).astype(BF) for h in hs]
        state = [st[h] for h in hs]
        qx = [(qr[h] * xi_ref[h]).astype(BF) for h in hs]
        kz = [(kr[h] * zt_ref[h]).astype(BF) for h in hs]
        out = [_dot(sd[h], vb[h]) + _dot(qx[h], state[h].astype(BF)) for h in hs]
        kv = [_dg(kz[h], vb[h], TN) for h in hs]
        for h in hs:
            sp_ref[0, h] = state[h]
            r_ref[:, RDV * h:RDV * (h + 1)] = out[h]
            st[h] = state[h] * gd_ref[h] + kv[h]

    return pl.pallas_call(
        body, name="ret_fwd", grid=(NCH,),
        in_specs=_ret_specs(lambda n: n),
        out_specs=[pl.BlockSpec((C, RH * RDV), lambda n: (n, 0)),
                   pl.BlockSpec((1, RH, RDK, RDV), lambda n: (n, 0, 0, 0))],
        out_shape=[jax.ShapeDtypeStruct((T, RH * RDV), F32), jax.ShapeDtypeStruct((NCH, RH, RDK, RDV), F32)],
        scratch_shapes=[pltpu.VMEM((RH, RDK, RDV), F32)],
        compiler_params=_params(("arbitrary",)),
    )(z, z, z, cst["cos2"], cst["sin2"], cst["dmask"], cst["xi"], cst["zeta"], cst["gdec"])


def _ret_bwd(z, cst, sprev, dr):
    def body(q_ref, k_ref, v_ref, cos_ref, sin_ref, dm_ref, xi_ref, zt_ref, gd_ref, sp_ref, dr_ref,
             dq_ref, dk_ref, dv_ref, gst):
        i = pl.program_id(0)

        @pl.when(i == 0)
        def _():
            gst[...] = jnp.zeros_like(gst)

        hs = range(RH)
        cos, sin = cos_ref[...], sin_ref[...]
        qr, kr, qb, kb, vb = _ret_heads(q_ref, k_ref, v_ref, cos, sin)
        dm = [dm_ref[h] for h in hs]
        xi = [xi_ref[h] for h in hs]
        zt = [zt_ref[h] for h in hs]
        sd = [(_dg(qb[h], kb[h], NT) * dm[h]).astype(BF) for h in hs]
        qx = [(qr[h] * xi[h]).astype(BF) for h in hs]
        kz = [(kr[h] * zt[h]).astype(BF) for h in hs]
        drb = [dr_ref[:, RDV * h:RDV * (h + 1)] for h in hs]
        sb = [sp_ref[0, h].astype(BF) for h in hs]
        g = [gst[h] for h in hs]
        gb = [t.astype(BF) for t in g]
        ds = [(_dg(drb[h], vb[h], NT) * dm[h]).astype(BF) for h in hs]
        dq = [_dot(ds[h], kb[h]) + _dg(drb[h], sb[h], NT) * xi[h] for h in hs]
        dk = [(_dg(ds[h], qb[h], TN) + _dg(vb[h], gb[h], NT) * zt[h]) * RSCALE for h in hs]
        dv = [_dg(sd[h], drb[h], TN) + _dot(kz[h], gb[h]) for h in hs]
        gn = [g[h] * gd_ref[h] + _dg(qx[h], drb[h], TN) for h in hs]
        for h in hs:
            gst[h] = gn[h]
            dq_ref[:, RDK * h:RDK * (h + 1)] = (dq[h] * cos + pltpu.roll(dq[h] * sin, 64, 1)).astype(BF)
            dk_ref[:, RDK * h:RDK * (h + 1)] = (dk[h] * cos + pltpu.roll(dk[h] * sin, 64, 1)).astype(BF)
            dv_ref[:, RDV * h:RDV * (h + 1)] = dv[h].astype(BF)

    rev = lambda n: NCH - 1 - n
    return pl.pallas_call(
        body, name="ret_bwd", grid=(NCH,),
        in_specs=_ret_specs(rev) + [
            pl.BlockSpec((1, RH, RDK, RDV), lambda n: (rev(n), 0, 0, 0)),
            pl.BlockSpec((C, RH * RDV), lambda n: (rev(n), 0)),
        ],
        out_specs=[pl.BlockSpec((C, RH * RDK), lambda n: (rev(n), 0)),
                   pl.BlockSpec((C, RH * RDK), lambda n: (rev(n), 0)),
                   pl.BlockSpec((C, RH * RDV), lambda n: (rev(n), 0))],
        out_shape=[jax.ShapeDtypeStruct((T, RH * RDK), BF), jax.ShapeDtypeStruct((T, RH * RDK), BF),
                   jax.ShapeDtypeStruct((T, RH * RDV), BF)],
        scratch_shapes=[pltpu.VMEM((RH, RDK, RDV), F32)],
        compiler_params=_params(("arbitrary",)),
    )(z, z, z, cst["cos2"], cst["sin2"], cst["dmask"], cst["xi"], cst["zeta"], cst["gdec"], sprev, dr)


def _place():
    x, y, c = lax.axis_index("x"), lax.axis_index("y"), lax.axis_index("c")
    return x, y, c


def _other_chips(x, y):
    return [(1 - x, y, 2 * (1 - x) + y), (x, 1 - y, 2 * x + (1 - y)), (1 - x, 1 - y, 2 * (1 - x) + (1 - y))]


def _chip_copies(srcs, lands, send_sems, recv_sems, by_dest):
    x, y, c = _place()
    me_s = 2 * x + y
    return [pltpu.make_async_remote_copy(
        src_ref=src.at[cs] if by_dest else src, dst_ref=land.at[me_s],
        send_sem=send_sems.at[3 * a + j], recv_sem=recv_sems.at[3 * a + j],
        device_id=(cx, cy, c), device_id_type=MESH)
        for a, (src, land) in enumerate(zip(srcs, lands)) for j, (cx, cy, cs) in enumerate(_other_chips(x, y))]


def _split_dot(x, mat01, dims=NN_DIMS, x_first=True):
    acc, rest = None, x
    for _ in range(3):
        piece = rest.astype(BF)
        part = _dg(piece, mat01, dims) if x_first else _dg(mat01, piece, dims)
        acc = part if acc is None else acc + part
        rest = rest - piece.astype(F32)
    return acc


def _log_sigmoid(x):
    return -(jnp.maximum(-x, 0.0) + jnp.log1p(jnp.exp(-jnp.abs(x))))


def _fox_prep(zf, bf_pad, cst):
    def body(zf_ref, b_ref, tri_ref, ct_ref, carry):
        n = pl.program_id(0)

        @pl.when(n == 0)
        def _():
            carry[...] = jnp.zeros_like(carry)

        ls = _log_sigmoid(zf_ref[...] + b_ref[...])
        row = n * C + lax.broadcasted_iota(jnp.int32, (C, C), 0)
        lf = jnp.where(row >= PAD, ls, 0.0)
        cc = _split_dot(lf, tri_ref[...], x_first=False) + carry[0:1, :]
        carry[...] = jnp.broadcast_to(cc[C - 1:C, :], carry.shape)
        pos = n * C + lax.broadcasted_iota(jnp.int32, (FH, C), 1)
        ct_ref[0] = jnp.where(pos >= PAD, cc.T[:FH, :], -NEG)

    return pl.pallas_call(
        body, name="fox_prep", grid=(NCH,),
        in_specs=[pl.BlockSpec((C, C), lambda n: (n, 0)), pl.BlockSpec((1, C), lambda n: (0, 0)),
                  pl.BlockSpec((C, C), lambda n: (0, 0))],
        out_specs=pl.BlockSpec((1, FH, C), lambda n: (n, 0, 0)),
        out_shape=jax.ShapeDtypeStruct((NCH, FH, C), F32),
        scratch_shapes=[pltpu.VMEM((8, C), F32)],
        compiler_params=_params(("arbitrary",)),
    )(zf, bf_pad, cst["tri"])


def _lo_lanes(shape):
    return lax.broadcasted_iota(jnp.int32, shape, 1) < FD


def _split_heads(x):
    lo = _lo_lanes(x.shape)
    zero = jnp.zeros_like(x)
    return jnp.concatenate([jnp.where(lo, x, zero), jnp.where(lo, zero, x)], axis=0)


def _spread2(x):
    lo = _lo_lanes(x.shape)
    r = pltpu.roll(x, FD, 1)
    return jnp.concatenate([jnp.where(lo, x, r), jnp.where(lo, r, x)], axis=1)


NSTEP = (NCH + 1) // 2
NTILE = NCH + 1
TROWS = T + C


def _fox_tile(s, t):
    second = t > s
    return second.astype(jnp.int32), jnp.where(second, t - s - 1, s - t)


def _fox_pos(i):
    return jnp.where(i < NSTEP, 2 * i, 2 * (NCH - 1 - i) + 1)


FOX_ORDER = [2 * i if i < NSTEP else 2 * (NCH - 1 - i) + 1 for i in range(NCH)]


def _fox_pair_columns():
    return pl.BlockSpec((TROWS, C), lambda p, s: (0, p))


def _fox_key_bias(ct_ref, p, j):
    return jnp.concatenate([ct_ref[j, pl.ds(2 * p, 1), :], ct_ref[j, pl.ds(2 * p + 1, 1), :]], axis=1)


def _fox_columns(cols, sems, p):
    def copies(pair, slot):
        return [pltpu.make_async_copy(
            src.at[pl.ds(0, buf.shape[1]), pl.ds(pl.multiple_of((first + pair) * C, C), C)], buf.at[slot],
            sems.at[i, slot]) for i, (src, first, buf) in enumerate(cols)]

    @pl.when(p == 0)
    def _():
        for cp in copies(0, 0):
            cp.start()

    for cp in copies(p, p % 2):
        cp.wait()

    @pl.when(p + 1 < NPAIR)
    def _():
        for cp in copies(p + 1, 1 - p % 2):
            cp.start()


def _rows(block, size=C):
    return pl.ds(pl.multiple_of(block * size, size), size)


def _fox_fwd(z, ct, cst, share):
    n = 0 if share is None else 1

    def body(z_ref, ct_ref, ones_ref, mb_ref, *rest):
        share_refs, (a_ref, g_ref), land_refs = rest[:n], rest[n:n + 2], rest[n + 2:2 * n + 2]
        kks, vvs, q2, m2, sbuf, qbuf, kbuf, vbuf, col_sems = rest[2 * n + 2:2 * n + 11]
        p, s = pl.program_id(0), pl.program_id(1)
        slot = p % 2
        if n:
            copies = _chip_copies(share_refs, land_refs, *rest[2 * n + 11:], by_dest=False)

            @pl.when((p == 0) & (s == 0))
            def _():
                for cp in copies:
                    cp.start()

            @pl.when((p == NPAIR - 1) & (s == NSTEP - 1))
            def _():
                for cp in copies:
                    cp.wait()

        @pl.when(s == 0)
        def _():
            ones = ones_ref[...]
            _fox_columns([(z_ref, QB_F, qbuf), (z_ref, KB_F, kbuf), (z_ref, VB_F, vbuf)], col_sems, p)

            def prep(j, carry):
                kks[j] = _split_heads(kbuf[slot, _rows(j), :]).astype(BF)
                vvs[j] = jnp.concatenate([_split_heads(vbuf[slot, _rows(j), :]).astype(BF), ones], axis=1)
                return carry

            lax.fori_loop(0, NCH, prep, 0)

        q2[0] = (qbuf[slot, _rows(s), :] * FSCALE).astype(BF)
        q2[1] = (qbuf[slot, _rows(NCH - 1 - s), :] * FSCALE).astype(BF)

        tiles = [_fox_tile(s, t) for t in range(NTILE)]
        causal = mb_ref[1]
        neg = jnp.full((C, 2 * C), NEG, F32)
        run, first = neg, neg
        for t, (sel, j) in enumerate(tiles):
            st = _dg(q2[sel], kks[j], NT) - _fox_key_bias(ct_ref, p, j)
            if t in (0, NTILE - 1):
                st = st + causal
            sbuf[t] = st
            run = jnp.maximum(jnp.where(t == s + 1, neg, run), st)
            first = jnp.where(t == s, run, first)
        for w, mx in enumerate((first, run)):
            m2[w] = jnp.concatenate(
                [jnp.broadcast_to(jnp.max(mx[:, :C], axis=1, keepdims=True), (C, C)),
                 jnp.broadcast_to(jnp.max(mx[:, C:], axis=1, keepdims=True), (C, C))], axis=1)

        zero = jnp.zeros((C, 2 * C), F32)
        run, first = zero, zero
        for t, (sel, j) in enumerate(tiles):
            run = jnp.where(t == s + 1, zero, run) + _dot(jnp.exp(sbuf[t] - m2[sel]).astype(BF), vvs[j])
            first = jnp.where(t == s, run, first)
        lo = _lo_lanes((C, C))
        for w, res in enumerate((first, run)):
            l = res[:, C:]
            a_ref[_rows(2 * s + w), :] = res[:, :C] / l
            mw = m2[w]
            g_ref[_rows(2 * s + w), :] = -(jnp.where(lo, mw[:, :C], mw[:, C:]) + jnp.log(l))

    col = _fox_pair_columns()
    return pl.pallas_call(
        body, name="fox_fwd", grid=(NPAIR, NSTEP),
        in_specs=[ANY,
                  pl.BlockSpec((NCH, FH, C), lambda p, s: (0, 0, 0)),
                  pl.BlockSpec((2 * C, C), lambda p, s: (0, 0)),
                  pl.BlockSpec((2, C, 2 * C), lambda p, s: (0, 0, 0))] + [ANY] * n,
        out_specs=[col, col] + [ANY] * n,
        out_shape=[jax.ShapeDtypeStruct((TROWS, FH * FD), F32)] * 2
        + ([jax.ShapeDtypeStruct((4,) + share.shape, share.dtype)] if n else []),
        scratch_shapes=[pltpu.VMEM((NCH, 2 * C, C), BF), pltpu.VMEM((NCH, 2 * C, 2 * C), BF),
                        pltpu.VMEM((2, C, C), BF), pltpu.VMEM((2, C, 2 * C), F32),
                        pltpu.VMEM((NTILE, C, 2 * C), F32),
                        pltpu.VMEM((2, T, C), F32), pltpu.VMEM((2, T, C), F32), pltpu.VMEM((2, T, C), F32),
                        pltpu.SemaphoreType.DMA((3, 2))]
        + [pltpu.SemaphoreType.DMA((3,)), pltpu.SemaphoreType.DMA((3,))] * n,
        compiler_params=_params(("arbitrary", "arbitrary")),
    )(z, ct, cst["ones_aug"], cst["mask_bias"], *([share] * n))


def _fox_bwd(z, da, g, delta, ct, cst, parts=()):
    grp = 9

    n = len(parts)

    def body(z_ref, da_ref, g_ref, dl_ref, ct_ref, ones_ref, mb_ref, *rest):
        part_refs, (dq_ref, dr_ref, dk_ref, dv_ref, dcs_ref), land_refs = rest[:n], rest[n:n + 5], rest[n + 5:2 * n + 5]
        (kks, vvs, q2, qq2, dd2, da2, gi2, dl2, dq2, dvb, dkb, dkacc, dvacc, csacc, qbuf, kbuf, vbuf, dabuf, gbuf,
         dlbuf, col_sems) = rest[2 * n + 5:2 * n + 26]
        p, s = pl.program_id(0), pl.program_id(1)
        slot = p % 2
        ones = ones_ref[...]
        if n:
            copies = _chip_copies(part_refs, land_refs, *rest[2 * n + 26:], by_dest=True)

            @pl.when((p == 0) & (s == 0))
            def _():
                for cp in copies:
                    cp.start()

            @pl.when((p == NPAIR - 1) & (s == NSTEP - 1))
            def _():
                for cp in copies:
                    cp.wait()

        @pl.when(s == 0)
        def _():
            dkacc[...] = jnp.zeros_like(dkacc)
            dvacc[...] = jnp.zeros_like(dvacc)
            csacc[...] = jnp.zeros_like(csacc)
            _fox_columns([(z_ref, QB_F, qbuf), (z_ref, KB_F, kbuf), (z_ref, VB_F, vbuf), (da_ref, 0, dabuf),
                          (g_ref, 0, gbuf), (dl_ref, 0, dlbuf)], col_sems, p)

            def prep(j, carry):
                kks[j] = _split_heads(kbuf[slot, _rows(j), :]).astype(BF)
                vvs[j] = _split_heads(vbuf[slot, _rows(j), :]).astype(BF)
                return carry

            lax.fori_loop(0, NCH, prep, 0)

        for w, (chunk, blk) in enumerate(((s, 2 * s), (NCH - 1 - s, jnp.where(s == NSTEP - 1, 2 * s, 2 * s + 1)))):
            qf = qbuf[slot, _rows(chunk), :]
            q2[w] = (qf * FSCALE).astype(BF)
            qq2[w] = jnp.concatenate([_split_heads(qf).astype(BF), ones], axis=1)
            da2[w] = dabuf[slot, _rows(blk), :]
            dd2[w] = _split_heads(da2[w].astype(F32)).astype(BF)
            gi2[w] = _spread2(gbuf[slot, _rows(blk), :])
            dl2[w] = _spread2(dlbuf[slot, _rows(blk), :])
        dq2[...] = jnp.zeros_like(dq2)
        zero = jnp.zeros((C, 2 * C), F32)

        def group(gi, carry):
            ts = [gi * grp + u for u in range(grp)]
            tiles = [_fox_tile(s, t) for t in ts]
            kk = [kks[j] for _, j in tiles]
            ss = [_dg(q2[sel], kj, NT) + (gi2[sel] - _fox_key_bias(ct_ref, p, j)) for kj, (sel, j) in zip(kk, tiles)]
            ss[0] = ss[0] + mb_ref[(gi == 0).astype(jnp.int32)]
            ss[-1] = ss[-1] + mb_ref[(gi == 1).astype(jnp.int32)]
            dps = [_dg(da2[sel], vvs[j], NT) for sel, j in tiles]
            pes = [jnp.exp(st) for st in ss]
            dss = [pe * (dp - dl2[sel]) * FSCALE for pe, dp, (sel, _) in zip(pes, dps, tiles)]
            pts = [jnp.concatenate([pe[:, :C].T, pe[:, C:].T], axis=1).astype(BF) for pe in pes]
            dsts = [jnp.concatenate([ds[:, :C].T, ds[:, C:].T], axis=1).astype(BF) for ds in dss]
            dvs = [_dot(pt, dd2[sel]) for pt, (sel, _) in zip(pts, tiles)]
            rs = [_dot(dst, qq2[sel]) for dst, (sel, _) in zip(dsts, tiles)]
            parts = [_dot(ds.astype(BF), jnp.concatenate([kj, ones], axis=1)) for ds, kj in zip(dss, kk)]
            for t, dv, rr in zip(ts, dvs, rs):
                dvb[t] = dv
                dkb[t] = rr
            pa, pb = zero, zero
            for t, part in zip(ts, parts):
                pa = pa + jnp.where(t <= s, part, zero)
                pb = pb + jnp.where(t <= s, zero, part)
            dq2[0] += pa
            dq2[1] += pb
            return carry

        ntile = jnp.where(s == NSTEP - 1, grp, NTILE)
        lax.fori_loop(0, ntile // grp, group, 0)

        def scatter(t, carry):
            _, j = _fox_tile(s, t)
            r = pl.ds(pl.multiple_of(j * C, C), C)
            dvacc[r, :] += dvb[t]
            dkacc[r, :] += dkb[t, :, :C]
            csacc[r, :] += dkb[t, :, C:]
            return carry

        lax.fori_loop(0, ntile, scatter, 0)
        for w in range(2):
            res = dq2[w]
            dq_ref[_rows(2 * s + w), :] = res[:, :C].astype(BF)
            dr_ref[_rows(2 * s + w), :] = res[:, C:]

        @pl.when(s == NSTEP - 1)
        def _():
            dk_ref[...] = dkacc[...].astype(BF)
            dv_ref[...] = dvacc[...].astype(BF)
            dcs_ref[...] = csacc[...]

    both = _fox_pair_columns()
    col = pl.BlockSpec((T, C), lambda p, s: (0, p))
    return pl.pallas_call(
        body, name="fox_bwd", grid=(NPAIR, NSTEP),
        in_specs=[ANY] * 4
        + [pl.BlockSpec((NCH, FH, C), lambda p, s: (0, 0, 0)),
           pl.BlockSpec((2 * C, C), lambda p, s: (0, 0)),
           pl.BlockSpec((2, C, 2 * C), lambda p, s: (0, 0, 0))] + [ANY] * n,
        out_specs=[both, both, col, col, col] + [ANY] * n,
        out_shape=[jax.ShapeDtypeStruct((TROWS, FH * FD), BF), jax.ShapeDtypeStruct((TROWS, FH * FD), F32),
                   jax.ShapeDtypeStruct((T, FH * FD), BF), jax.ShapeDtypeStruct((T, FH * FD), BF),
                   jax.ShapeDtypeStruct((T, FH * FD), F32)]
        + [jax.ShapeDtypeStruct(p.shape, p.dtype) for p in parts],
        scratch_shapes=[pltpu.VMEM((NCH, 2 * C, C), BF), pltpu.VMEM((NCH, 2 * C, C), BF),
                        pltpu.VMEM((2, C, C), BF), pltpu.VMEM((2, 2 * C, 2 * C), BF), pltpu.VMEM((2, 2 * C, C), BF),
                        pltpu.VMEM((2, C, C), BF), pltpu.VMEM((2, C, 2 * C), F32), pltpu.VMEM((2, C, 2 * C), F32),
                        pltpu.VMEM((2, C, 2 * C), F32),
                        pltpu.VMEM((NTILE, C, C), F32), pltpu.VMEM((NTILE, C, 2 * C), F32),
                        pltpu.VMEM((T, C), F32), pltpu.VMEM((T, C), F32), pltpu.VMEM((T, C), F32),
                        pltpu.VMEM((2, T, C), F32), pltpu.VMEM((2, T, C), F32), pltpu.VMEM((2, T, C), F32),
                        pltpu.VMEM((2, T, C), BF), pltpu.VMEM((2, T, C), F32), pltpu.VMEM((2, T, C), F32),
                        pltpu.SemaphoreType.DMA((6, 2))]
        + ([pltpu.SemaphoreType.DMA((3 * n,)), pltpu.SemaphoreType.DMA((3 * n,))] if n else []),
        compiler_params=_params(("arbitrary", "arbitrary")),
    )(z, da, g, delta, ct, cst["ones_aug"], cst["mask_bias"], *parts)


def _fox_gate_bwd(drow, dcol, zf, bf_pad, cst):
    def body(dr_ref, dc_ref, zf_ref, b_ref, tri_ref, pick_ref, dff_ref, db_ref, carry):
        s = pl.program_id(0)
        n = NCH - 1 - s

        @pl.when(s == 0)
        def _():
            carry[...] = jnp.zeros_like(carry)
            db_ref[...] = jnp.zeros_like(db_ref)

        dcb = _split_dot((dr_ref[...] - dc_ref[...]) * (1.0 / FSCALE), pick_ref[...])
        suf = _split_dot(dcb, tri_ref[...], TN, x_first=False) + carry[0:1, :]
        carry[...] = jnp.broadcast_to(suf[0:1, :], carry.shape)
        x = zf_ref[...] + b_ref[...]
        row = n * C + lax.broadcasted_iota(jnp.int32, (C, C), 0)
        dff = jnp.where(row >= PAD, suf * (1.0 - jax.nn.sigmoid(x)), 0.0)
        dff_ref[...] = dff.astype(BF)
        db_ref[...] += jnp.sum(dff, axis=0, keepdims=True)

    rev = lambda s: (NCH - 1 - s, 0)
    return pl.pallas_call(
        body, name="fox_gate_bwd", grid=(NCH,),
        in_specs=[pl.BlockSpec((C, FH * FD), lambda s: (_fox_pos(NCH - 1 - s), 0)),
                  pl.BlockSpec((C, FH * FD), rev), pl.BlockSpec((C, C), rev),
                  pl.BlockSpec((1, C), lambda s: (0, 0)), pl.BlockSpec((C, C), lambda s: (0, 0)),
                  pl.BlockSpec((FH * FD, C), lambda s: (0, 0))],
        out_specs=[pl.BlockSpec((C, C), rev), pl.BlockSpec((1, C), lambda s: (0, 0))],
        out_shape=[jax.ShapeDtypeStruct((T, C), BF), jax.ShapeDtypeStruct((1, C), F32)],
        scratch_shapes=[pltpu.VMEM((8, C), F32)],
        compiler_params=_params(("arbitrary",)),
    )(drow, dcol, zf, bf_pad, cst["tri"], cst["pick"])


def _head_norm(r):
    rn, rs = [], []
    for h in range(RH):
        rh = r[:, RDV * h:RDV * (h + 1)]
        s = lax.rsqrt(jnp.mean(rh * rh, axis=1, keepdims=True) + EPS)
        rn.append(rh * s)
        rs.append(s)
    return jnp.concatenate(rn, axis=1), rs


def _gated(r, rg, a, fg):
    rn, _ = _head_norm(r)
    return jnp.concatenate([rn * (rg * jax.nn.sigmoid(rg)), a * (fg * jax.nn.sigmoid(fg))], axis=1)


def _out_loss(r, z, a, wout, x, tgt, fgain):
    def body(r_ref, rg_ref, a_ref, fg_ref, w_ref, x_ref, t_ref, g_ref, yt_ref, do_ref, dob_ref, loss_ref, dg_ref):
        i = pl.program_id(0)

        @pl.when(i == 0)
        def _():
            yt_ref[...] = jnp.zeros_like(yt_ref)
            do_ref[...] = jnp.zeros_like(do_ref)
            dob_ref[...] = jnp.zeros_like(dob_ref)
            loss_ref[...] = jnp.zeros_like(loss_ref)
            dg_ref[...] = jnp.zeros_like(dg_ref)

        @pl.when(i > 0)
        def _():
            y = _gated(r_ref[...], rg_ref[...], a_ref[...], fg_ref[...])
            yt_ref[...] = y.T.astype(BF)
            o = x_ref[...] + _dot(y.astype(BF), w_ref[...])
            rs = lax.rsqrt(jnp.mean(o * o, axis=1, keepdims=True) + EPS)
            on = o * rs
            g = g_ref[...]
            e = on * g - t_ref[...]
            loss_ref[...] += 0.5 * jnp.sum(jnp.mean(e * e, axis=1, keepdims=True))
            dyh = e * (1.0 / D)
            dg_ref[...] += jnp.sum(dyh * on, axis=0, keepdims=True)
            don = dyh * g
            do = rs * (don - on * jnp.mean(don * on, axis=1, keepdims=True))
            do_ref[...] = do
            dob_ref[...] = do.astype(BF)

    tok = lambda i: (jnp.maximum(i - 1, 0), 0)
    return pl.pallas_call(
        body, name="out_loss", grid=(NCH,),
        in_specs=[pl.BlockSpec((C, D), lambda i: (i, 0)), pl.BlockSpec((C, D), lambda i: (i, GB_R)),
                  pl.BlockSpec((C, D), lambda i: (_fox_pos(i), 0)), pl.BlockSpec((C, D), lambda i: (i, GB_F)),
                  pl.BlockSpec((DMIX, D), lambda i: (0, 0)),
                  pl.BlockSpec((C, D), tok), pl.BlockSpec((C, D), tok), pl.BlockSpec((1, D), lambda i: (0, 0))],
        out_specs=[pl.BlockSpec((DMIX, C), lambda i: (0, i)), pl.BlockSpec((C, D), lambda i: (i, 0)),
                   pl.BlockSpec((C, D), lambda i: (i, 0)), pl.BlockSpec((8, C), lambda i: (0, 0)),
                   pl.BlockSpec((1, D), lambda i: (0, 0))],
        out_shape=[jax.ShapeDtypeStruct((DMIX, T), BF), jax.ShapeDtypeStruct((T, D), F32),
                   jax.ShapeDtypeStruct((T, D), BF), jax.ShapeDtypeStruct((8, C), F32),
                   jax.ShapeDtypeStruct((1, D), F32)],
        compiler_params=_params(("arbitrary",)),
    )(r, z, a, z, wout, x, tgt, fgain)


def _silu_and_grad(x):
    s = jax.nn.sigmoid(x)
    return x * s, s * (1.0 + x * (1.0 - s))


def _dy_gate_bwd(dob, wout, r, z, a, seg, swap=()):
    n = len(swap)

    def body(do_ref, w_ref, r_ref, rg_ref, a_ref, fg_ref, seg_ref, *rest):
        (dr_ref, da_ref, drg_ref, dfg_ref, dl_ref) = rest[n:n + 5]
        if n:
            copies = _pair_copies(rest[:n], rest[n + 5:2 * n + 5], *rest[2 * n + 5:], n)

            @pl.when(pl.program_id(0) == 0)
            def _():
                for cp in copies:
                    cp.start()

            @pl.when(pl.program_id(0) == NCH - 1)
            def _():
                for cp in copies:
                    cp.wait()

        dy = _dg(do_ref[...], w_ref[...], NT)
        a_ = a_ref[...]
        rn, rs = _head_norm(r_ref[...])
        silu_rg, dsilu_rg = _silu_and_grad(rg_ref[...])
        silu_fg, dsilu_fg = _silu_and_grad(fg_ref[...])
        dyr, dyf = dy[:, :D], dy[:, D:]
        drn = dyr * silu_rg
        drg_ref[...] = (dyr * rn * dsilu_rg).astype(BF)
        for h in range(RH):
            sl = slice(RDV * h, RDV * (h + 1))
            dh, nh = drn[:, sl], rn[:, sl]
            dr_ref[:, sl] = (rs[h] * (dh - nh * jnp.mean(dh * nh, axis=1, keepdims=True))).astype(BF)
        dab = (dyf * silu_fg).astype(BF)
        da_ref[...] = dab
        dfg_ref[...] = (dyf * a_ * dsilu_fg).astype(BF)
        prod = dab.astype(F32) * a_
        segm = seg_ref[...]
        for p in range(NPAIR):
            sl = slice(C * p, C * (p + 1))
            hi = prod[:, sl].astype(BF)
            lo = (prod[:, sl] - hi.astype(F32)).astype(BF)
            dl_ref[:, sl] = _dot(hi, segm) + _dot(lo, segm)

    row = pl.BlockSpec((C, D), lambda i: (i, 0))
    fox = pl.BlockSpec((C, D), lambda i: (_fox_pos(i), 0))
    return pl.pallas_call(
        body, name="dy_gate_bwd", grid=(NCH,),
        in_specs=[row, pl.BlockSpec((DMIX, D), lambda i: (0, 0)),
                  row, pl.BlockSpec((C, D), lambda i: (i, GB_R)),
                  fox, pl.BlockSpec((C, D), lambda i: (i, GB_F)),
                  pl.BlockSpec((C, C), lambda i: (0, 0))] + [ANY] * n,
        out_specs=[row, fox, row, row, fox] + [ANY] * n,
        out_shape=[jax.ShapeDtypeStruct((T, D), BF), jax.ShapeDtypeStruct((TROWS, D), BF),
                   jax.ShapeDtypeStruct((T, D), BF), jax.ShapeDtypeStruct((T, D), BF),
                   jax.ShapeDtypeStruct((TROWS, D), F32)]
        + [jax.ShapeDtypeStruct((4, s.shape[1] // 2, s.shape[2]), s.dtype) for s in swap],
        scratch_shapes=[pltpu.SemaphoreType.DMA((n,)), pltpu.SemaphoreType.DMA((n,))] if n else [],
        compiler_params=_params(("arbitrary",)),
    )(dob, wout, r, z, a, z, seg, *swap)


DZ_WIDTHS = (512, 512, 1024, 1024, 1024, 1024, 1024, 1024)


def _du_norm_bwd(dzs, dzf, wt, wft, hpad, g, dopad, parts=()):
    tm, tk = 544, 1024
    nk = WMAIN // tk
    ni = T // tm
    n = len(parts)

    def body(rq_ref, rk_ref, rv_ref, rg_ref, fq_ref, fk_ref, fv_ref, fg_ref, dzf_ref, w_ref, wf_ref, h_ref, g_ref,
             do_ref, *rest):
        part_refs, (gh_ref, dg_ref), land_refs = rest[:n], rest[n:n + 2], rest[n + 2:2 * n + 2]
        acc = rest[2 * n + 2]
        i, k = pl.program_id(0), pl.program_id(1)

        if n:
            send_sems, recv_sems = rest[2 * n + 3:]
            copies = _chip_copies(part_refs, land_refs, send_sems, recv_sems, by_dest=True)

            @pl.when((i == 0) & (k == 0))
            def _():
                for cp in copies:
                    cp.start()

            @pl.when((i == ni - 1) & (k == nk - 1))
            def _():
                for cp in copies:
                    cp.wait()

        @pl.when(k == 0)
        def _():
            acc[...] = (_dot(dzf_ref[...], wf_ref[...]) + _dot(rq_ref[...], w_ref[:512, :])
                        + _dot(rk_ref[...], w_ref[512:, :]))

        for kk, piece in enumerate((rv_ref, rg_ref, fq_ref, fk_ref, fv_ref, fg_ref), start=1):
            @pl.when(k == kk)
            def _(piece=piece):
                acc[...] += _dot(piece[...], w_ref[...])

        @pl.when(k == nk - 1)
        def _():
            du = acc[...]
            h = h_ref[...]
            gg = g_ref[...]
            rs = lax.rsqrt(jnp.mean(h * h, axis=1, keepdims=True) + EPS)
            hn = h * rs
            part = jnp.sum(du * hn, axis=0, keepdims=True)

            @pl.when(i == 0)
            def _():
                dg_ref[...] = part

            @pl.when(i > 0)
            def _():
                dg_ref[...] += part

            dhn = du * gg
            gh_ref[...] = rs * (dhn - hn * jnp.mean(dhn * hn, axis=1, keepdims=True)) + do_ref[...]

    sems = [pltpu.SemaphoreType.DMA((3 * n,)), pltpu.SemaphoreType.DMA((3 * n,))] if n else []
    return pl.pallas_call(
        body, name="du_norm_bwd", grid=(ni, nk),
        in_specs=[pl.BlockSpec((tm, w), lambda i, k: (i, 0)) for w in DZ_WIDTHS]
        + [pl.BlockSpec((tm, C), lambda i, k: (i, 0)),
           pl.BlockSpec((tk, D), lambda i, k: (k, 0)), pl.BlockSpec((C, D), lambda i, k: (0, 0)),
           pl.BlockSpec((tm, D), lambda i, k: (i, 0)), pl.BlockSpec((1, D), lambda i, k: (0, 0)),
           pl.BlockSpec((tm, D), lambda i, k: (i, 0))] + [ANY] * n,
        out_specs=[pl.BlockSpec((tm, D), lambda i, k: (i, 0)), pl.BlockSpec((1, D), lambda i, k: (0, 0))] + [ANY] * n,
        out_shape=[jax.ShapeDtypeStruct((T, D), F32), jax.ShapeDtypeStruct((1, D), F32)]
        + [jax.ShapeDtypeStruct(p.shape, p.dtype) for p in parts],
        scratch_shapes=[pltpu.VMEM((tm, D), F32)] + sems,
        compiler_params=_params(("arbitrary", "arbitrary")),
    )(*dzs, dzf, wt, wft, hpad, g, dopad, *parts)


GROWS = 7680


def _dw_in(dzs, dzf, ut):
    tn = 512
    nmain = WMAIN // tn
    first, blocks = [], []
    for w in DZ_WIDTHS:
        first.append(sum(blocks))
        blocks.append(w // tn)

    def body(rq_ref, rk_ref, rv_ref, rg_ref, fq_ref, fk_ref, fv_ref, fg_ref, dzf_ref, ut_ref, o_ref):
        gidx = pl.program_id(0)
        for piece, g0, nb in zip((rq_ref, rk_ref, rv_ref, rg_ref, fq_ref, fk_ref, fv_ref, fg_ref), first, blocks):
            @pl.when((gidx >= g0) & (gidx < g0 + nb))
            def _(piece=piece):
                o_ref[...] = _dot(ut_ref[...], piece[...]).T.astype(BF)

        @pl.when(gidx == nmain)
        def _():
            o_ref[:C, :] = _dot(ut_ref[...], dzf_ref[...]).T.astype(BF)
            o_ref[C:, :] = jnp.zeros((tn - C, D), BF)

    def piece_spec(g0, nb):
        return pl.BlockSpec((T, tn), lambda gidx: (0, jnp.clip(gidx - g0, 0, nb - 1)))

    return pl.pallas_call(
        body, name="dw_in", grid=(nmain + 1,),
        in_specs=[piece_spec(g0, nb) for g0, nb in zip(first, blocks)]
        + [pl.BlockSpec((T, C), lambda gidx: (0, 0)), pl.BlockSpec((D, T), lambda gidx: (0, 0))],
        out_specs=pl.BlockSpec((tn, D), lambda gidx: (gidx, 0)),
        out_shape=jax.ShapeDtypeStruct((GROWS, D), BF),
        compiler_params=pltpu.CompilerParams(dimension_semantics=("arbitrary",), vmem_limit_bytes=DW_VMEM_LIMIT),
    )(*dzs, dzf, ut)


def _token_order(x_po):
    def body(i_ref, o_ref):
        o_ref[...] = i_ref[...]

    return pl.pallas_call(
        body, name="token_order", grid=(NCH,),
        in_specs=[pl.BlockSpec((C, D), lambda i: (_fox_pos(i), 0))],
        out_specs=pl.BlockSpec((C, D), lambda i: (i, 0)),
        out_shape=jax.ShapeDtypeStruct((T, D), x_po.dtype),
        compiler_params=_params(("parallel",)),
    )(x_po)


def _local_step(x, tgt, normed, norm_g, wt, wft, b_f, wout, final_g, reduce_scatter=False, wout_full=None):
    cst = _constants()
    hpad, u, ut = normed
    bf_pad = jnp.pad(b_f, ((0, 0), (0, C - NFF)))
    z = _mm_nt(u, wt, WMAIN, T // 2, 1024, "in_proj")
    zf = _mm_nt(u, wft, C, T // 2, C, "in_proj_ff")
    r, sprev = _ret_fwd(z, cst)
    ct = _fox_prep(zf, bf_pad, cst)
    if wout_full is None:
        a, g = _fox_fwd(z, ct, cst, None)
    else:
        a, g, landed_wout = _fox_fwd(z, ct, cst, wout)
        wout = wout_full(landed_wout)
    yt, dopad, dob, loss8, dfg = _out_loss(r, z, a, wout, x, tgt, final_g)
    dwout = _mm_nn(yt, dob, 512, D, "dw_out", BF)
    g_out = [dwout.reshape(4, DMIX // 4, D)] if reduce_scatter else []
    dr, da, dzrg, dzfg, delta, *r_out = _dy_gate_bwd(dob, wout, r, z, a, cst["seg"], g_out)
    p_out = [_add_halves(g_out[0], r_out[0], "pair_add_out", BF)] if reduce_scatter else []
    dzq_r, dzk_r, dzv_r = _ret_bwd(z, cst, sprev, dr)
    dq_po, drow, dzk_f, dzv_f, dcol, *e_out = _fox_bwd(z, da, g, delta, ct, cst, p_out)
    dzf, dbf = _fox_gate_bwd(drow, dcol, zf, bf_pad, cst)
    dzs = [dzq_r, dzk_r, dzv_r, dzrg, _token_order(dq_po), dzk_f, dzv_f, dzfg]
    gwt = _dw_in(dzs, dzf, ut)
    p_in = [_add_windows(gwt, *_pair_swap(gwt, [], "rs_pair_swap_in"))] if reduce_scatter else []
    gh, dng, *e_in = _du_norm_bwd(dzs, dzf, wt, wft, hpad, norm_g, dopad, p_in)
    return (loss8[0, 0], gh[C:], gh[PAD:C], dng, gwt, dbf[:, :NFF], dwout, dfg, p_in + p_out, e_in + e_out)


WOFF, WLEN = 1792, 2048
WHALF = WLEN // 2
LAP = WPADROWS - WOFF


def _own_window(w3):
    rows, sub, lanes = w3.shape
    pad = WPADROWS - rows
    tb = 96
    nb = WPADROWS // tb
    half = rows // 2

    def body(w_ref, o_ref, buf, sems):
        x, y, _ = _place()
        shift = 4 * (2 * x + y)
        buf[pl.ds(0, pad)] = jnp.zeros((pad, sub, lanes), F32)
        buf[pl.ds(rows, pad)] = jnp.zeros((pad, sub, lanes), F32)
        cps = [pltpu.make_async_copy(w_ref.at[pl.ds(half * h, half)], buf.at[pl.ds(shift + half * h, half)],
                                     sems.at[h]) for h in range(2)]
        for cp in cps:
            cp.start()

        def block(i, carry):
            r0 = pl.multiple_of(i * tb, tb)
            o_ref[pl.ds(r0, tb), :] = buf[pl.ds(r0, tb)].reshape(tb, sub * lanes).astype(BF)
            return carry

        cps[0].wait()
        lax.fori_loop(0, half // tb, block, 0)
        cps[1].wait()
        lax.fori_loop(half // tb, nb, block, 0)

    return pl.pallas_call(
        body, name="own_window",
        in_specs=[ANY], out_shape=jax.ShapeDtypeStruct((WPADROWS, sub * lanes), BF),
        scratch_shapes=[pltpu.VMEM((WPADROWS, sub, lanes), F32), pltpu.SemaphoreType.DMA((2,))],
        compiler_params=pltpu.CompilerParams(vmem_limit_bytes=VMEM_LIMIT),
    )(w3)


def _gather_weights(own_win, meta, x, norm_g):
    half_main, half_lap, half_meta = WOFF // 2, LAP // 2, meta.shape[0] // 2
    last = NCH - 1

    def body(win_ref, meta_ref, x_ref, g_ref, w_ref, laps_ref, gm_ref, h_ref, u_ref, ut_ref,
             send_sems, recv_sems, local_sems, stage, lapbuf, headbuf, metabuf):
        step = pl.program_id(0)
        x, y, c = _place()
        me_s = 2 * x + y
        sib = (x, y, 1 - c)
        chips = _other_chips(x, y)

        def emit(h):
            u = _norm_rows(h, g_ref[...])
            h_ref[...] = h
            u_ref[...] = u.astype(BF)
            ut_ref[...] = u.T.astype(BF)

        kinds = [
            (lambda h: win_ref.at[pl.ds(half_main * h, half_main)],
             lambda s, h: w_ref.at[pl.ds(WOFF * s + half_main * h, half_main)]),
            (lambda h: win_ref.at[pl.ds(WOFF + half_lap * h, half_lap)],
             lambda s, h: laps_ref.at[s, pl.ds(half_lap * h, half_lap)]),
            (lambda h: meta_ref.at[pl.ds(half_meta * h, half_meta)],
             lambda s, h: gm_ref.at[s, pl.ds(half_meta * h, half_meta)]),
        ]
        own_in = pltpu.make_async_copy(win_ref.at[pl.ds(0, WOFF)], stage, local_sems.at[0])
        own_lap_in = pltpu.make_async_copy(win_ref.at[pl.ds(WOFF, LAP)], lapbuf.at[0], local_sems.at[1])
        own_out = pltpu.make_async_copy(stage, w_ref.at[pl.ds(WOFF * me_s, WOFF)], local_sems.at[0])
        own_lap_out = pltpu.make_async_copy(lapbuf.at[0], laps_ref.at[me_s], local_sems.at[1])
        sends, arrivals, forwards, forwarded = [], [], [], []
        for a, (src, dst) in enumerate(kinds):
            for k, (cx, cy, cs) in enumerate(chips):
                there = dict(send_sem=send_sems.at[6 * a + k], recv_sem=recv_sems.at[6 * a + k],
                             device_id=(cx, cy, c), device_id_type=MESH)
                across = dict(send_sem=send_sems.at[6 * a + 3 + k], recv_sem=recv_sems.at[6 * a + 3 + k],
                              device_id=sib, device_id_type=MESH)
                sends.append(pltpu.make_async_remote_copy(src_ref=src(c), dst_ref=dst(me_s, c), **there))
                arrivals.append(pltpu.make_async_remote_copy(src_ref=dst(cs, c), dst_ref=dst(cs, c), **there))
                forwards.append(pltpu.make_async_remote_copy(src_ref=dst(cs, c), dst_ref=dst(cs, c), **across))
                forwarded.append(pltpu.make_async_remote_copy(
                    src_ref=dst(cs, 1 - c), dst_ref=dst(cs, 1 - c), **across))

        @pl.when(step == 0)
        def _():
            own_in.start()
            own_lap_in.start()
            for cp in sends:
                cp.start()
            own_in.wait()
            own_out.start()
            own_lap_in.wait()
            own_lap_out.start()

        @pl.when(step < last)
        def _():
            emit(x_ref[...])

        @pl.when(step == last)
        def _():
            for cp, fwd in zip(arrivals, forwards):
                cp.wait_recv()
                fwd.start()
            for cp in forwarded:
                cp.wait_recv()
            for cp in sends + forwards:
                cp.wait_send()
            own_out.wait()
            own_lap_out.wait()
            for s in range(1, 4):
                head = w_ref.at[pl.ds(WOFF * s, LAP)]
                loads = [pltpu.make_async_copy(laps_ref.at[s - 1], lapbuf.at[1], local_sems.at[2]),
                         pltpu.make_async_copy(head, headbuf, local_sems.at[3])]
                for cp in loads:
                    cp.start()
                for cp in loads:
                    cp.wait()
                headbuf[...] = (headbuf[...].astype(F32) + lapbuf[1].astype(F32)).astype(BF)
                store = pltpu.make_async_copy(headbuf, head, local_sems.at[3])
                store.start()
                store.wait()
            loads = [pltpu.make_async_copy(meta_ref, metabuf.at[me_s], local_sems.at[0])]
            loads += [pltpu.make_async_copy(gm_ref.at[cs], metabuf.at[cs], local_sems.at[1 + k])
                      for k, (_, _, cs) in enumerate(chips)]
            for cp in loads:
                cp.start()
            for cp in loads:
                cp.wait()
            tokens = jnp.concatenate([metabuf[s] for s in range(4)], axis=1)
            emit(jnp.concatenate([jnp.zeros((PAD, D), F32), tokens], axis=0))

    def chunk(i):
        return (i + 1) % NCH

    return pl.pallas_call(
        body, name="all_gather_w", grid=(NCH,),
        in_specs=[ANY, ANY, pl.BlockSpec((C, D), lambda i: (jnp.minimum(i, last - 1), 0)),
                  pl.BlockSpec((1, D), lambda i: (0, 0))],
        out_specs=[ANY] * 3 + [pl.BlockSpec((C, D), lambda i: (chunk(i), 0))] * 2
        + [pl.BlockSpec((D, C), lambda i: (0, chunk(i)))],
        out_shape=[jax.ShapeDtypeStruct((WMAIN, D), own_win.dtype), jax.ShapeDtypeStruct((4, LAP, D), own_win.dtype),
                   jax.ShapeDtypeStruct((4,) + meta.shape, meta.dtype),
                   jax.ShapeDtypeStruct((T, D), F32), jax.ShapeDtypeStruct((T, D), BF),
                   jax.ShapeDtypeStruct((D, T), BF)],
        scratch_shapes=[pltpu.SemaphoreType.DMA((18,)), pltpu.SemaphoreType.DMA((18,)), pltpu.SemaphoreType.DMA((4,)),
                        pltpu.VMEM((WOFF, D), own_win.dtype), pltpu.VMEM((2, LAP, D), own_win.dtype),
                        pltpu.VMEM((LAP, D), own_win.dtype), pltpu.VMEM((4,) + meta.shape, meta.dtype)],
        compiler_params=_params(("arbitrary",)),
    )(own_win, meta, x, norm_g)


def _pair_copies(ins, outs, send_sems, recv_sems, n):
    x, y, c = _place()
    sib = dict(device_id=(x, y, 1 - c), device_id_type=MESH)
    cps = []
    for a in range(n):
        rows = ins[a].shape[1] // 2
        cps.append(pltpu.make_async_remote_copy(
            src_ref=ins[a].at[:, pl.ds((1 - c) * rows, rows)], dst_ref=outs[a],
            send_sem=send_sems.at[a], recv_sem=recv_sems.at[a], **sib))
    for k in range(4 * (len(ins) - n)):
        cps.append(pltpu.make_async_remote_copy(
            src_ref=ins[n].at[pl.ds(WOFF * k + (1 - c) * WHALF, WHALF)], dst_ref=outs[n].at[k],
            send_sem=send_sems.at[n + k], recv_sem=recv_sems.at[n + k], **sib))
    return cps


def _pair_swap(gwt, arrs, name):
    n = len(arrs)
    wins = [] if gwt is None else [gwt]
    m = n + len(wins)
    nsem = n + 4 * len(wins)

    def body(*refs):
        cps = _pair_copies(refs[:m], refs[m:2 * m], *refs[2 * m:], n)
        for cp in cps:
            cp.start()
        for cp in cps:
            cp.wait()

    return pl.pallas_call(
        body, name=name,
        in_specs=[ANY] * m, out_specs=[ANY] * m,
        out_shape=[jax.ShapeDtypeStruct((4, a.shape[1] // 2, a.shape[2]), a.dtype) for a in arrs]
        + [jax.ShapeDtypeStruct((4, WHALF, D), w.dtype) for w in wins],
        scratch_shapes=[pltpu.SemaphoreType.DMA((nsem,)), pltpu.SemaphoreType.DMA((nsem,))],
    )(*arrs, *wins)


def _add_windows(gwt, recv):
    tb = 256
    nb = WHALF // tb
    c = lax.axis_index("c")

    def body(c_ref, a_ref, b_ref, o_ref):
        o_ref[0] = (a_ref[...].astype(F32) + b_ref[0].astype(F32)).astype(BF)

    return pl.pallas_call(
        body, name="pair_add_in",
        grid_spec=pltpu.PrefetchScalarGridSpec(
            num_scalar_prefetch=1, grid=(4, nb),
            in_specs=[pl.BlockSpec((tb, D), lambda k, i, cr: ((WOFF // tb) * k + nb * cr[0] + i, 0)),
                      pl.BlockSpec((1, tb, D), lambda k, i, cr: (k, i, 0))],
            out_specs=pl.BlockSpec((1, tb, D), lambda k, i, cr: (k, i, 0))),
        out_shape=jax.ShapeDtypeStruct(recv.shape, BF),
        compiler_params=_params(("parallel", "parallel")),
    )(jnp.reshape(c, (1,)).astype(jnp.int32), gwt, recv)


def _chip_exchange(parts, small):
    n = len(parts)

    def body(*refs):
        ins, sm = refs[:n], refs[n]
        outs, smo = refs[n + 1:2 * n + 1], refs[2 * n + 1]
        send_sems, recv_sems = refs[2 * n + 2:]
        cps = _chip_copies(ins, outs, send_sems, recv_sems, by_dest=True)
        cps += _chip_copies([sm], [smo], send_sems.at[pl.ds(3 * n, 3)], recv_sems.at[pl.ds(3 * n, 3)], by_dest=False)
        for cp in cps:
            cp.start()
        for cp in cps:
            cp.wait()

    return pl.pallas_call(
        body, name="rs_chip_exchange",
        in_specs=[ANY] * (n + 1), out_specs=[ANY] * (n + 1),
        out_shape=[jax.ShapeDtypeStruct(p.shape, p.dtype) for p in parts]
        + [jax.ShapeDtypeStruct((4,) + small.shape, small.dtype)],
        scratch_shapes=[pltpu.SemaphoreType.DMA((3 * (n + 1),)), pltpu.SemaphoreType.DMA((3 * (n + 1),))],
    )(*parts, small)


def _pair_send(halves):
    n = len(halves)

    def body(*refs):
        ins, outs = refs[:n], refs[n:2 * n]
        send_sems, recv_sems = refs[2 * n:]
        x, y, c = _place()
        cps = [pltpu.make_async_remote_copy(
            src_ref=ins[a], dst_ref=outs[a], send_sem=send_sems.at[a], recv_sem=recv_sems.at[a],
            device_id=(x, y, 1 - c), device_id_type=MESH) for a in range(n)]
        for cp in cps:
            cp.start()
        for cp in cps:
            cp.wait()

    return pl.pallas_call(
        body, name="rs_pair_send",
        in_specs=[ANY] * n, out_specs=[ANY] * n,
        out_shape=[jax.ShapeDtypeStruct(h.shape, h.dtype) for h in halves],
        scratch_shapes=[pltpu.SemaphoreType.DMA((n,)), pltpu.SemaphoreType.DMA((n,))],
    )(*halves)


def _row_block(rows):
    for tb in (256, 128, 64, 32, 16, 8):
        if rows % tb == 0:
            return tb
    return rows


def _add_halves(full, recv, name, out_dtype):
    _, r2, w = recv.shape
    tb = _row_block(r2)
    nb = r2 // tb
    c = lax.axis_index("c")

    def body(c_ref, a_ref, b_ref, o_ref):
        o_ref[...] = (a_ref[...].astype(F32) + b_ref[...].astype(F32)).astype(o_ref.dtype)

    return pl.pallas_call(
        body, name=name,
        grid_spec=pltpu.PrefetchScalarGridSpec(
            num_scalar_prefetch=1, grid=(4, nb),
            in_specs=[pl.BlockSpec((1, tb, w), lambda s, i, cr: (s, cr[0] * nb + i, 0)),
                      pl.BlockSpec((1, tb, w), lambda s, i, cr: (s, i, 0))],
            out_specs=pl.BlockSpec((1, tb, w), lambda s, i, cr: (s, i, 0))),
        out_shape=jax.ShapeDtypeStruct(recv.shape, out_dtype),
        compiler_params=_params(("parallel", "parallel")),
    )(jnp.reshape(c, (1,)).astype(jnp.int32), full, recv)


def _add2(a, b, name):
    def body(a_ref, b_ref, o_ref):
        o_ref[...] = a_ref[...] + b_ref[...]

    return pl.pallas_call(body, name=name, out_shape=jax.ShapeDtypeStruct(a.shape, a.dtype))(a, b)


def _sum4(buf, own, name):
    _, r, w = buf.shape
    tb = _row_block(r)
    me_s = 2 * lax.axis_index("x") + lax.axis_index("y")
    by_dest = own.ndim == 3

    def body(s_ref, b_ref, own_ref, o_ref):
        mine = (own_ref[0] if by_dest else own_ref[...]).astype(F32)
        terms = [jnp.where(s_ref[0] == t, mine, b_ref[t].astype(F32)) for t in range(4)]
        o_ref[...] = ((terms[0] + terms[1]) + terms[2]) + terms[3]

    own_spec = (pl.BlockSpec((1, tb, w), lambda i, sr: (sr[0], i, 0)) if by_dest
                else pl.BlockSpec((tb, w), lambda i, sr: (i, 0)))
    return pl.pallas_call(
        body, name=name,
        grid_spec=pltpu.PrefetchScalarGridSpec(
            num_scalar_prefetch=1, grid=(r // tb,),
            in_specs=[pl.BlockSpec((4, tb, w), lambda i, sr: (0, i, 0)), own_spec],
            out_specs=pl.BlockSpec((tb, w), lambda i, sr: (i, 0))),
        out_shape=jax.ShapeDtypeStruct((r, w), F32),
        compiler_params=_params(("parallel",)),
    )(jnp.reshape(me_s, (1,)).astype(jnp.int32), buf, own)


def _adamw_math(w, g, m, v):
    mn = B1 * m + (1.0 - B1) * g
    vn = B2 * v + (1.0 - B2) * (g * g)
    m_hat = mn / (1.0 - B1 ** STEP)
    v_hat = vn / (1.0 - B2 ** STEP)
    return -LR * (m_hat / (jnp.sqrt(v_hat) + AEPS) + WD * w), mn, vn


def _adamw(w, g, m, v, name):
    r, c_ = w.shape
    tb = _row_block(r)
    if tb == r and r > 512:
        tb = 256

    def body(w_ref, g_ref, m_ref, v_ref, d_ref, mo_ref, vo_ref):
        d_ref[...], mo_ref[...], vo_ref[...] = _adamw_math(w_ref[...], g_ref[...], m_ref[...], v_ref[...])

    spec = pl.BlockSpec((tb, c_), lambda i: (i, 0))
    return pl.pallas_call(
        body, name=name, grid=(pl.cdiv(r, tb),),
        in_specs=[spec] * 4, out_specs=[spec] * 3,
        out_shape=[jax.ShapeDtypeStruct(w.shape, F32)] * 3,
        compiler_params=_params(("parallel",)),
    )(w, g, m, v)


def _adamw_rows(w, g_mine, g_sib, m, v, name):
    r = w.shape[0]
    tb = 256
    sub, lanes = w.shape[1:]
    nh = g_mine.shape[0] // tb
    nsteps = pl.cdiv(r, tb)
    assert nsteps <= 2 * nh and 4 * 3 + r <= 2 * nh * tb
    x, y, c = _place()
    place = jnp.stack([c, 4 * (2 * x + y)]).astype(jnp.int32)

    def body(p_ref, w_ref, mc_ref, sc_ref, mn_ref, sn_ref, m_ref, v_ref, go_ref, d_ref, mo_ref, vo_ref, buf):
        i = pl.program_id(0)
        for at, blk, mine_ref, sib_ref in ((0, i, mc_ref, sc_ref), (1, jnp.minimum(i + 1, 2 * nh - 1), mn_ref, sn_ref)):
            rows = jnp.where(blk // nh == p_ref[0], mine_ref[...], sib_ref[...])
            buf[tb * at:tb * (at + 1)] = rows.reshape(tb, sub, lanes)
        g = buf[pl.ds(p_ref[1], tb)]
        go_ref[...] = g
        d_ref[...], mo_ref[...], vo_ref[...] = _adamw_math(w_ref[...], g, m_ref[...], v_ref[...])

    def half_spec(ahead, sibling):
        def index(i, pr):
            half = (1 - pr[0]) if sibling else pr[0]
            return (jnp.clip(jnp.minimum(i + ahead, 2 * nh - 1) - nh * half, 0, nh - 1), 0)
        return pl.BlockSpec((tb, sub * lanes), index)

    spec = pl.BlockSpec((tb, sub, lanes), lambda i, pr: (i, 0, 0))
    return pl.pallas_call(
        body, name=name,
        grid_spec=pltpu.PrefetchScalarGridSpec(
            num_scalar_prefetch=1, grid=(nsteps,),
            in_specs=[spec, half_spec(0, False), half_spec(0, True), half_spec(1, False), half_spec(1, True),
                      spec, spec],
            out_specs=[spec] * 4,
            scratch_shapes=[pltpu.VMEM((2 * tb, sub, lanes), F32)]),
        out_shape=[jax.ShapeDtypeStruct(w.shape, F32)] * 4,
        compiler_params=_params(("parallel",)),
    )(place, w, g_mine, g_sib, g_mine, g_sib, m, v)


def _adamw_halves(w, g_mine, g_sib, m, v, name):
    r, c_ = w.shape
    r2 = g_mine.shape[0]
    tb = _row_block(r2)
    nb = r2 // tb
    c = lax.axis_index("c")

    def body(c_ref, w_ref, gm_ref, gs_ref, m_ref, v_ref, g_ref, d_ref, mo_ref, vo_ref):
        g = jnp.where(pl.program_id(0) == c_ref[0], gm_ref[...], gs_ref[...])
        g_ref[...] = g
        d_ref[...], mo_ref[...], vo_ref[...] = _adamw_math(w_ref[...], g, m_ref[...], v_ref[...])

    full = pl.BlockSpec((tb, c_), lambda h, i, cr: (h * nb + i, 0))
    half = pl.BlockSpec((tb, c_), lambda h, i, cr: (i, 0))
    return pl.pallas_call(
        body, name=name,
        grid_spec=pltpu.PrefetchScalarGridSpec(
            num_scalar_prefetch=1, grid=(2, nb),
            in_specs=[full, half, half, full, full], out_specs=[full] * 4),
        out_shape=[jax.ShapeDtypeStruct(w.shape, F32)] * 4,
        compiler_params=_params(("parallel", "parallel")),
    )(jnp.reshape(c, (1,)).astype(jnp.int32), w, g_mine, g_sib, m, v)


def kernel(x, meta_tokens, norm_g, w_in, b_f, w_out, final_g, loss_target, m_meta_tokens, m_norm_g, m_w_in, m_b_f, m_w_out, m_final_g, v_meta_tokens, v_norm_g, v_w_in, v_b_f, v_w_out, v_final_g):
    me_s = 2 * lax.axis_index("x") + lax.axis_index("y")
    w3, m3, v3 = [jnp.transpose(jnp.reshape(t[0], (D // C, C, WSH)), (2, 0, 1)) for t in (w_in, m_w_in, v_w_in)]

    wt_main, laps, _, *normed = _gather_weights(_own_window(w3), meta_tokens, x[0], norm_g)
    wft = jnp.pad(laps[3, :NFF], ((0, C - NFF), (0, 0)))
    mine = (jnp.arange(4) == me_s)[:, None, None]
    wout_own = w_out[0].astype(BF)

    def wout_full(landed):
        return jnp.where(mine, wout_own[None], landed).reshape(DMIX, D)

    loss, gx, dmeta, dng, gwt, dbf, dwout, dfg, (p_in, p_out), (e_in, e_out) = _local_step(
        x[0], loss_target[0], normed, norm_g, wt_main, wft, b_f, wout_own, final_g.reshape(1, D), True, wout_full)

    g_meta = jnp.stack([dmeta[:, 256 * s:256 * (s + 1)] for s in range(4)])
    small = jnp.concatenate([dng, dfg, jnp.pad(dbf, ((0, 0), (0, D - NFF))),
                             jnp.pad(jnp.reshape(loss, (1, 1)), ((0, 0), (0, D - 1))),
                             jnp.zeros((4, D), F32)], axis=0)
    e_meta, e_small = _chip_exchange([g_meta], small)
    h_in, h_out = _sum4(e_in, p_in, "sum_in"), _sum4(e_out, p_out, "sum_out")
    h_meta, h_small = _sum4(e_meta, g_meta, "sum_meta"), _sum4(e_small, small, "sum_small")
    s_in, s_out, s_meta, s_small = _pair_send([h_in, h_out, h_meta, h_small])
    gw_meta = _add2(h_meta, s_meta, "pair_add_meta")
    tot = _add2(h_small, s_small, "pair_add_small")
    g_norm, g_final, g_bf, loss_all = tot[0:1], tot[1], tot[2:3, :NFF], tot[3, 0]

    d_meta, nm_meta, nv_meta = _adamw(meta_tokens, gw_meta, m_meta_tokens, v_meta_tokens, "adamw_meta")
    d_norm, nm_norm, nv_norm = _adamw(norm_g, g_norm, m_norm_g, v_norm_g, "adamw_norm")
    outs_in = _adamw_rows(w3, h_in, s_in, m3, v3, "adamw_in")
    gw_in, d_in, nm_in, nv_in = [jnp.reshape(jnp.transpose(t, (1, 2, 0)), (1, D, WSH)) for t in outs_in]
    d_bf, nm_bf, nv_bf = _adamw(b_f, g_bf, m_b_f, v_b_f, "adamw_bf")
    gw_out, d_out, nm_out, nv_out = _adamw_halves(w_out[0], h_out, s_out, m_w_out[0], v_w_out[0], "adamw_out")
    d_fin, nm_fin, nv_fin = _adamw(final_g.reshape(1, D), g_final.reshape(1, D), m_final_g.reshape(1, D),
                                   v_final_g.reshape(1, D), "adamw_final")
    return (loss_all, gx[None], gw_meta, g_norm, gw_in, g_bf, gw_out[None], g_final,
            d_meta, d_norm, d_in, d_bf, d_out[None], d_fin.reshape(D),
            nm_meta, nm_norm, nm_in, nm_bf, nm_out[None], nm_fin.reshape(D),
            nv_meta, nv_norm, nv_in, nv_bf, nv_out[None], nv_fin.reshape(D))
```

```python
import numpy as np
import jax
import jax.numpy as jnp
from jax import lax
from jax.experimental import pallas as pl
from jax.experimental.pallas import tpu as pltpu

D = 1024
SEQ = 2048
NMETA = 16
C = 128
PAD = C - NMETA
T = PAD + NMETA + SEQ
NCH = T // C
RH, RDK, RDV = 4, 128, 256
FH, FD = 16, 64
NPAIR = FH // 2
WMAIN = 7168
NFF = 16
WIN = WMAIN + NFF
WSH = WIN // 4
WPADROWS = 1824
DMIX = 2048
EPS = 1e-6
NEG = -1e30
RSCALE = RDK ** -0.5
FSCALE = FD ** -0.5
ROPE_BASE = 10000.0
LR, B1, B2, AEPS, WD, STEP = 0.001, 0.9, 0.999, 1e-08, 0.01, 10

BF = jnp.bfloat16
F32 = jnp.float32
NT = (((1,), (1,)), ((), ()))
TN = (((0,), (0,)), ((), ()))
NN_DIMS = (((1,), (0,)), ((), ()))
MESH = pl.DeviceIdType.MESH
ANY = pl.BlockSpec(memory_space=pl.ANY)
VMEM_LIMIT = 48 * 1024 * 1024
DW_VMEM_LIMIT = 56 * 1024 * 1024

GB_R, GB_F = 2, 6
QB_F, KB_F, VB_F = 24, 32, 40


def _dot(a, b):
    return jnp.dot(a, b, preferred_element_type=F32)


def _dg(a, b, dims):
    return lax.dot_general(a, b, dims, preferred_element_type=F32)


def _params(sem=None):
    return pltpu.CompilerParams(dimension_semantics=sem, vmem_limit_bytes=VMEM_LIMIT)


def _constants():
    pos = jnp.arange(T, dtype=F32) - PAD
    inv = ROPE_BASE ** (-jnp.arange(0, RDK, 2, dtype=F32) / RDK)
    ang = pos[:, None] * inv[None, :]
    cos, sin = jnp.cos(ang), jnp.sin(ang)
    cos2 = jnp.concatenate([cos, cos], axis=1)
    sin2 = jnp.concatenate([-sin, sin], axis=1)
    log_gamma = jnp.log1p(-jnp.exp2(-5.0 - jnp.arange(RH, dtype=F32)))
    idx = jnp.arange(C, dtype=F32)
    diff = idx[:, None] - idx[None, :]
    dmask = jnp.where(diff[None] >= 0, jnp.exp(log_gamma[:, None, None] * jnp.maximum(diff, 0.0)[None]), 0.0)
    zeta = jnp.exp(log_gamma[:, None] * (C - 1.0 - idx)[None, :])
    xi = jnp.exp(log_gamma[:, None] * (idx + 1.0)[None, :])
    gdec = jnp.exp(log_gamma * C)
    zeta_b = jnp.broadcast_to(zeta[:, :, None], (RH, C, RDK))
    xi_b = jnp.broadcast_to(xi[:, :, None], (RH, C, RDK))
    gdec_b = jnp.broadcast_to(gdec[:, None, None], (RH, RDK, RDV))
    tri = jnp.asarray(np.tril(np.ones((C, C), np.float32)), dtype=BF)
    head_of_lane = np.arange(FH * FD) // FD
    pick = ((np.arange(FH * FD)[:, None] % FD == 0)
            & (head_of_lane[:, None] == np.arange(C)[None, :])).astype(np.float32)
    seg = (np.arange(C)[:, None] // FD == np.arange(C)[None, :] // FD).astype(np.float32)
    ones_aug = np.concatenate([np.tile((np.arange(C) < FD)[None, :], (C, 1)),
                               np.tile((np.arange(C) >= FD)[None, :], (C, 1))], axis=0).astype(np.float32)
    lane = np.arange(2 * C) % C
    causal = np.where(lane[None, :] <= np.arange(C)[:, None], 0.0, NEG).astype(np.float32)
    mask_bias = np.stack([np.zeros((C, 2 * C), np.float32), causal])
    return dict(cos2=cos2, sin2=sin2, dmask=dmask, zeta=zeta_b, xi=xi_b, gdec=gdec_b, tri=tri,
                mask_bias=jnp.asarray(mask_bias), pick=jnp.asarray(pick, dtype=BF), seg=jnp.asarray(seg, dtype=BF),
                ones_aug=jnp.asarray(ones_aug, dtype=BF))


def _norm_rows(h, g):
    return h * lax.rsqrt(jnp.mean(h * h, axis=1, keepdims=True) + EPS) * g


def _mm_nt(a, b, n, tm, tn, name):
    m, k = a.shape

    def body(a_ref, b_ref, o_ref):
        o_ref[...] = _dg(a_ref[...], b_ref[...], NT)

    return pl.pallas_call(
        body, name=name, grid=(m // tm, n // tn),
        in_specs=[pl.BlockSpec((tm, k), lambda i, j: (i, 0)), pl.BlockSpec((tn, k), lambda i, j: (j, 0))],
        out_specs=pl.BlockSpec((tm, tn), lambda i, j: (i, j)),
        out_shape=jax.ShapeDtypeStruct((m, n), F32),
        compiler_params=_params(("parallel", "parallel")),
    )(a, b)


def _mm_nn(a, b, tm, tn, name, out_dtype=F32):
    m, k = a.shape
    _, n = b.shape

    def body(a_ref, b_ref, o_ref):
        o_ref[...] = _dot(a_ref[...], b_ref[...]).astype(out_dtype)

    return pl.pallas_call(
        body, name=name, grid=(m // tm, n // tn),
        in_specs=[pl.BlockSpec((tm, k), lambda i, j: (i, 0)), pl.BlockSpec((k, tn), lambda i, j: (0, j))],
        out_specs=pl.BlockSpec((tm, tn), lambda i, j: (i, j)),
        out_shape=jax.ShapeDtypeStruct((m, n), out_dtype),
        compiler_params=_params(("parallel", "parallel")),
    )(a, b)


def _rot(x, cos2, sin2):
    return x * cos2 + pltpu.roll(x, 64, 1) * sin2


def _ret_specs(chunk):
    whole = lambda shape: pl.BlockSpec(shape, lambda n: (0,) * len(shape))
    return [
        pl.BlockSpec((C, RH * RDK), lambda n: (chunk(n), 0)),
        pl.BlockSpec((C, RH * RDK), lambda n: (chunk(n), 1)),
        pl.BlockSpec((C, RH * RDV), lambda n: (chunk(n), 1)),
        pl.BlockSpec((C, RDK), lambda n: (chunk(n), 0)),
        pl.BlockSpec((C, RDK), lambda n: (chunk(n), 0)),
        whole((RH, C, C)), whole((RH, C, RDK)), whole((RH, C, RDK)), whole((RH, RDK, RDV)),
    ]


def _ret_heads(q_ref, k_ref, v_ref, cos, sin):
    qr = [_rot(q_ref[:, RDK * h:RDK * (h + 1)], cos, sin) for h in range(RH)]
    kr = [_rot(k_ref[:, RDK * h:RDK * (h + 1)], cos, sin) * RSCALE for h in range(RH)]
    vb = [v_ref[:, RDV * h:RDV * (h + 1)].astype(BF) for h in range(RH)]
    return qr, kr, [t.astype(BF) for t in qr], [t.astype(BF) for t in kr], vb


def _ret_fwd(z, cst):
    def body(q_ref, k_ref, v_ref, cos_ref, sin_ref, dm_ref, xi_ref, zt_ref, gd_ref, r_ref, sp_ref, st):
        n = pl.program_id(0)

        @pl.when(n == 0)
        def _():
            st[...] = jnp.zeros_like(st)

        hs = range(RH)
        qr, kr, qb, kb, vb = _ret_heads(q_ref, k_ref, v_ref, cos_ref[...], sin_ref[...])
        sd = [(_dg(qb[h], kb[h], NT) * dm_ref[h]).astype(BF) for h in hs]
        state = [st[h] for h in hs]
        qx = [(qr[h] * xi_ref[h]).astype(BF) for h in hs]
        kz = [(kr[h] * zt_ref[h]).astype(BF) for h in hs]
        out = [_dot(sd[h], vb[h]) + _dot(qx[h], state[h].astype(BF)) for h in hs]
        kv = [_dg(kz[h], vb[h], TN) for h in hs]
        for h in hs:
            sp_ref[0, h] = state[h]
            r_ref[:, RDV * h:RDV * (h + 1)] = out[h]
            st[h] = state[h] * gd_ref[h] + kv[h]

    return pl.pallas_call(
        body, name="ret_fwd", grid=(NCH,),
        in_specs=_ret_specs(lambda n: n),
        out_specs=[pl.BlockSpec((C, RH * RDV), lambda n: (n, 0)),
                   pl.BlockSpec((1, RH, RDK, RDV), lambda n: (n, 0, 0, 0))],
        out_shape=[jax.ShapeDtypeStruct((T, RH * RDV), F32), jax.ShapeDtypeStruct((NCH, RH, RDK, RDV), F32)],
        scratch_shapes=[pltpu.VMEM((RH, RDK, RDV), F32)],
        compiler_params=_params(("arbitrary",)),
    )(z, z, z, cst["cos2"], cst["sin2"], cst["dmask"], cst["xi"], cst["zeta"], cst["gdec"])


def _ret_bwd(z, cst, sprev, dr):
    def body(q_ref, k_ref, v_ref, cos_ref, sin_ref, dm_ref, xi_ref, zt_ref, gd_ref, sp_ref, dr_ref,
             dq_ref, dk_ref, dv_ref, gst):
        i = pl.program_id(0)

        @pl.when(i == 0)
        def _():
            gst[...] = jnp.zeros_like(gst)

        hs = range(RH)
        cos, sin = cos_ref[...], sin_ref[...]
        qr, kr, qb, kb, vb = _ret_heads(q_ref, k_ref, v_ref, cos, sin)
        dm = [dm_ref[h] for h in hs]
        xi = [xi_ref[h] for h in hs]
        zt = [zt_ref[h] for h in hs]
        sd = [(_dg(qb[h], kb[h], NT) * dm[h]).astype(BF) for h in hs]
        qx = [(qr[h] * xi[h]).astype(BF) for h in hs]
        kz = [(kr[h] * zt[h]).astype(BF) for h in hs]
        drb = [dr_ref[:, RDV * h:RDV * (h + 1)] for h in hs]
        sb = [sp_ref[0, h].astype(BF) for h in hs]
        g = [gst[h] for h in hs]
        gb = [t.astype(BF) for t in g]
        ds = [(_dg(drb[h], vb[h], NT) * dm[h]).astype(BF) for h in hs]
        dq = [_dot(ds[h], kb[h]) + _dg(drb[h], sb[h], NT) * xi[h] for h in hs]
        dk = [(_dg(ds[h], qb[h], TN) + _dg(vb[h], gb[h], NT) * zt[h]) * RSCALE for h in hs]
        dv = [_dg(sd[h], drb[h], TN) + _dot(kz[h], gb[h]) for h in hs]
        gn = [g[h] * gd_ref[h] + _dg(qx[h], drb[h], TN) for h in hs]
        for h in hs:
            gst[h] = gn[h]
            dq_ref[:, RDK * h:RDK * (h + 1)] = (dq[h] * cos + pltpu.roll(dq[h] * sin, 64, 1)).astype(BF)
            dk_ref[:, RDK * h:RDK * (h + 1)] = (dk[h] * cos + pltpu.roll(dk[h] * sin, 64, 1)).astype(BF)
            dv_ref[:, RDV * h:RDV * (h + 1)] = dv[h].astype(BF)

    rev = lambda n: NCH - 1 - n
    return pl.pallas_call(
        body, name="ret_bwd", grid=(NCH,),
        in_specs=_ret_specs(rev) + [
            pl.BlockSpec((1, RH, RDK, RDV), lambda n: (rev(n), 0, 0, 0)),
            pl.BlockSpec((C, RH * RDV), lambda n: (rev(n), 0)),
        ],
        out_specs=[pl.BlockSpec((C, RH * RDK), lambda n: (rev(n), 0)),
                   pl.BlockSpec((C, RH * RDK), lambda n: (rev(n), 0)),
                   pl.BlockSpec((C, RH * RDV), lambda n: (rev(n), 0))],
        out_shape=[jax.ShapeDtypeStruct((T, RH * RDK), BF), jax.ShapeDtypeStruct((T, RH * RDK), BF),
                   jax.ShapeDtypeStruct((T, RH * RDV), BF)],
        scratch_shapes=[pltpu.VMEM((RH, RDK, RDV), F32)],
        compiler_params=_params(("arbitrary",)),
    )(z, z, z, cst["cos2"], cst["sin2"], cst["dmask"], cst["xi"], cst["zeta"], cst["gdec"], sprev, dr)


def _place():
    x, y, c = lax.axis_index("x"), lax.axis_index("y"), lax.axis_index("c")
    return x, y, c


def _other_chips(x, y):
    return [(1 - x, y, 2 * (1 - x) + y), (x, 1 - y, 2 * x + (1 - y)), (1 - x, 1 - y, 2 * (1 - x) + (1 - y))]


def _chip_copies(srcs, lands, send_sems, recv_sems, by_dest):
    x, y, c = _place()
    me_s = 2 * x + y
    return [pltpu.make_async_remote_copy(
        src_ref=src.at[cs] if by_dest else src, dst_ref=land.at[me_s],
        send_sem=send_sems.at[3 * a + j], recv_sem=recv_sems.at[3 * a + j],
        device_id=(cx, cy, c), device_id_type=MESH)
        for a, (src, land) in enumerate(zip(srcs, lands)) for j, (cx, cy, cs) in enumerate(_other_chips(x, y))]


def _split_dot(x, mat01, dims=NN_DIMS, x_first=True):
    acc, rest = None, x
    for _ in range(3):
        piece = rest.astype(BF)
        part = _dg(piece, mat01, dims) if x_first else _dg(mat01, piece, dims)
        acc = part if acc is None else acc + part
        rest = rest - piece.astype(F32)
    return acc


def _log_sigmoid(x):
    return -(jnp.maximum(-x, 0.0) + jnp.log1p(jnp.exp(-jnp.abs(x))))


def _fox_prep(zf, bf_pad, cst):
    def body(zf_ref, b_ref, tri_ref, ct_ref, carry):
        n = pl.program_id(0)

        @pl.when(n == 0)
        def _():
            carry[...] = jnp.zeros_like(carry)

        ls = _log_sigmoid(zf_ref[...] + b_ref[...])
        row = n * C + lax.broadcasted_iota(jnp.int32, (C, C), 0)
        lf = jnp.where(row >= PAD, ls, 0.0)
        cc = _split_dot(lf, tri_ref[...], x_first=False) + carry[0:1, :]
        carry[...] = jnp.broadcast_to(cc[C - 1:C, :], carry.shape)
        pos = n * C + lax.broadcasted_iota(jnp.int32, (FH, C), 1)
        ct_ref[0] = jnp.where(pos >= PAD, cc.T[:FH, :], -NEG)

    return pl.pallas_call(
        body, name="fox_prep", grid=(NCH,),
        in_specs=[pl.BlockSpec((C, C), lambda n: (n, 0)), pl.BlockSpec((1, C), lambda n: (0, 0)),
                  pl.BlockSpec((C, C), lambda n: (0, 0))],
        out_specs=pl.BlockSpec((1, FH, C), lambda n: (n, 0, 0)),
        out_shape=jax.ShapeDtypeStruct((NCH, FH, C), F32),
        scratch_shapes=[pltpu.VMEM((8, C), F32)],
        compiler_params=_params(("arbitrary",)),
    )(zf, bf_pad, cst["tri"])


def _lo_lanes(shape):
    return lax.broadcasted_iota(jnp.int32, shape, 1) < FD


def _split_heads(x):
    lo = _lo_lanes(x.shape)
    zero = jnp.zeros_like(x)
    return jnp.concatenate([jnp.where(lo, x, zero), jnp.where(lo, zero, x)], axis=0)


def _spread2(x):
    lo = _lo_lanes(x.shape)
    r = pltpu.roll(x, FD, 1)
    return jnp.concatenate([jnp.where(lo, x, r), jnp.where(lo, r, x)], axis=1)


NSTEP = (NCH + 1) // 2
NTILE = NCH + 1
TROWS = T + C


def _fox_tile(s, t):
    second = t > s
    return second.astype(jnp.int32), jnp.where(second, t - s - 1, s - t)


def _fox_pos(i):
    return jnp.where(i < NSTEP, 2 * i, 2 * (NCH - 1 - i) + 1)


def _fox_pair_columns():
    return pl.BlockSpec((TROWS, C), lambda p, s: (0, p))


def _fox_key_bias(ct_ref, p, j):
    return jnp.concatenate([ct_ref[j, pl.ds(2 * p, 1), :], ct_ref[j, pl.ds(2 * p + 1, 1), :]], axis=1)


def _fox_columns(cols, sems, p):
    def copies(pair, slot):
        return [pltpu.make_async_copy(
            src.at[pl.ds(0, buf.shape[1]), pl.ds(pl.multiple_of((first + pair) * C, C), C)], buf.at[slot],
            sems.at[i, slot]) for i, (src, first, buf) in enumerate(cols)]

    @pl.when(p == 0)
    def _():
        for cp in copies(0, 0):
            cp.start()

    for cp in copies(p, p % 2):
        cp.wait()

    @pl.when(p + 1 < NPAIR)
    def _():
        for cp in copies(p + 1, 1 - p % 2):
            cp.start()


def _rows(block, size=C):
    return pl.ds(pl.multiple_of(block * size, size), size)


def _fox_fwd(z, ct, cst, share):
    n = 0 if share is None else 1

    def body(z_ref, ct_ref, ones_ref, mb_ref, *rest):
        share_refs, (a_ref, g_ref), land_refs = rest[:n], rest[n:n + 2], rest[n + 2:2 * n + 2]
        kks, vvs, q2, m2, sbuf, qbuf, kbuf, vbuf, col_sems = rest[2 * n + 2:2 * n + 11]
        p, s = pl.program_id(0), pl.program_id(1)
        slot = p % 2
        if n:
            copies = _chip_copies(share_refs, land_refs, *rest[2 * n + 11:], by_dest=False)

            @pl.when((p == 0) & (s == 0))
            def _():
                for cp in copies:
                    cp.start()

            @pl.when((p == NPAIR - 1) & (s == NSTEP - 1))
            def _():
                for cp in copies:
                    cp.wait()

        @pl.when(s == 0)
        def _():
            ones = ones_ref[...]
            _fox_columns([(z_ref, QB_F, qbuf), (z_ref, KB_F, kbuf), (z_ref, VB_F, vbuf)], col_sems, p)

            def prep(j, carry):
                kks[j] = _split_heads(kbuf[slot, _rows(j), :]).astype(BF)
                vvs[j] = jnp.concatenate([_split_heads(vbuf[slot, _rows(j), :]).astype(BF), ones], axis=1)
                return carry

            lax.fori_loop(0, NCH, prep, 0)

        q2[0] = (qbuf[slot, _rows(s), :] * FSCALE).astype(BF)
        q2[1] = (qbuf[slot, _rows(NCH - 1 - s), :] * FSCALE).astype(BF)

        tiles = [_fox_tile(s, t) for t in range(NTILE)]
        causal = mb_ref[1]
        neg = jnp.full((C, 2 * C), NEG, F32)
        run, first = neg, neg
        for t, (sel, j) in enumerate(tiles):
            st = _dg(q2[sel], kks[j], NT) - _fox_key_bias(ct_ref, p, j)
            if t in (0, NTILE - 1):
                st = st + causal
            sbuf[t] = st
            run = jnp.maximum(jnp.where(t == s + 1, neg, run), st)
            first = jnp.where(t == s, run, first)
        for w, mx in enumerate((first, run)):
            m2[w] = jnp.concatenate(
                [jnp.broadcast_to(jnp.max(mx[:, :C], axis=1, keepdims=True), (C, C)),
                 jnp.broadcast_to(jnp.max(mx[:, C:], axis=1, keepdims=True), (C, C))], axis=1)

        zero = jnp.zeros((C, 2 * C), F32)
        run, first = zero, zero
        for t, (sel, j) in enumerate(tiles):
            run = jnp.where(t == s + 1, zero, run) + _dot(jnp.exp(sbuf[t] - m2[sel]).astype(BF), vvs[j])
            first = jnp.where(t == s, run, first)
        lo = _lo_lanes((C, C))
        for w, res in enumerate((first, run)):
            l = res[:, C:]
            a_ref[_rows(2 * s + w), :] = res[:, :C] / l
            mw = m2[w]
            g_ref[_rows(2 * s + w), :] = -(jnp.where(lo, mw[:, :C], mw[:, C:]) + jnp.log(l))

    col = _fox_pair_columns()
    return pl.pallas_call(
        body, name="fox_fwd", grid=(NPAIR, NSTEP),
        in_specs=[ANY,
                  pl.BlockSpec((NCH, FH, C), lambda p, s: (0, 0, 0)),
                  pl.BlockSpec((2 * C, C), lambda p, s: (0, 0)),
                  pl.BlockSpec((2, C, 2 * C), lambda p, s: (0, 0, 0))] + [ANY] * n,
        out_specs=[col, col] + [ANY] * n,
        out_shape=[jax.ShapeDtypeStruct((TROWS, FH * FD), F32)] * 2
        + ([jax.ShapeDtypeStruct((4,) + share.shape, share.dtype)] if n else []),
        scratch_shapes=[pltpu.VMEM((NCH, 2 * C, C), BF), pltpu.VMEM((NCH, 2 * C, 2 * C), BF),
                        pltpu.VMEM((2, C, C), BF), pltpu.VMEM((2, C, 2 * C), F32),
                        pltpu.VMEM((NTILE, C, 2 * C), F32),
                        pltpu.VMEM((2, T, C), F32), pltpu.VMEM((2, T, C), F32), pltpu.VMEM((2, T, C), F32),
                        pltpu.SemaphoreType.DMA((3, 2))]
        + [pltpu.SemaphoreType.DMA((3,)), pltpu.SemaphoreType.DMA((3,))] * n,
        compiler_params=_params(("arbitrary", "arbitrary")),
    )(z, ct, cst["ones_aug"], cst["mask_bias"], *([share] * n))


def _fox_bwd(z, da, g, delta, ct, cst, parts=()):
    grp = 9

    n = len(parts)

    def body(z_ref, da_ref, g_ref, dl_ref, ct_ref, ones_ref, mb_ref, *rest):
        part_refs, (dq_ref, dr_ref, dk_ref, dv_ref, dcs_ref), land_refs = rest[:n], rest[n:n + 5], rest[n + 5:2 * n + 5]
        (kks, vvs, q2, qq2, dd2, da2, gi2, dl2, dq2, dvb, dkb, dkacc, dvacc, csacc, qbuf, kbuf, vbuf, dabuf, gbuf,
         dlbuf, col_sems) = rest[2 * n + 5:2 * n + 26]
        p, s = pl.program_id(0), pl.program_id(1)
        slot = p % 2
        ones = ones_ref[...]
        if n:
            copies = _chip_copies(part_refs, land_refs, *rest[2 * n + 26:], by_dest=True)

            @pl.when((p == 0) & (s == 0))
            def _():
                for cp in copies:
                    cp.start()

            @pl.when((p == NPAIR - 1) & (s == NSTEP - 1))
            def _():
                for cp in copies:
                    cp.wait()

        @pl.when(s == 0)
        def _():
            dkacc[...] = jnp.zeros_like(dkacc)
            dvacc[...] = jnp.zeros_like(dvacc)
            csacc[...] = jnp.zeros_like(csacc)
            _fox_columns([(z_ref, QB_F, qbuf), (z_ref, KB_F, kbuf), (z_ref, VB_F, vbuf), (da_ref, 0, dabuf),
                          (g_ref, 0, gbuf), (dl_ref, 0, dlbuf)], col_sems, p)

            def prep(j, carry):
                kks[j] = _split_heads(kbuf[slot, _rows(j), :]).astype(BF)
                vvs[j] = _split_heads(vbuf[slot, _rows(j), :]).astype(BF)
                return carry

            lax.fori_loop(0, NCH, prep, 0)

        for w, (chunk, blk) in enumerate(((s, 2 * s), (NCH - 1 - s, jnp.where(s == NSTEP - 1, 2 * s, 2 * s + 1)))):
            qf = qbuf[slot, _rows(chunk), :]
            q2[w] = (qf * FSCALE).astype(BF)
            qq2[w] = jnp.concatenate([_split_heads(qf).astype(BF), ones], axis=1)
            da2[w] = dabuf[slot, _rows(blk), :]
            dd2[w] = _split_heads(da2[w].astype(F32)).astype(BF)
            gi2[w] = _spread2(gbuf[slot, _rows(blk), :])
            dl2[w] = _spread2(dlbuf[slot, _rows(blk), :])
        dq2[...] = jnp.zeros_like(dq2)
        zero = jnp.zeros((C, 2 * C), F32)

        def group(gi, carry):
            ts = [gi * grp + u for u in range(grp)]
            tiles = [_fox_tile(s, t) for t in ts]
            kk = [kks[j] for _, j in tiles]
            ss = [_dg(q2[sel], kj, NT) + (gi2[sel] - _fox_key_bias(ct_ref, p, j)) for kj, (sel, j) in zip(kk, tiles)]
            ss[0] = ss[0] + mb_ref[(gi == 0).astype(jnp.int32)]
            ss[-1] = ss[-1] + mb_ref[(gi == 1).astype(jnp.int32)]
            dps = [_dg(da2[sel], vvs[j], NT) for sel, j in tiles]
            pes = [jnp.exp(st) for st in ss]
            dss = [pe * (dp - dl2[sel]) * FSCALE for pe, dp, (sel, _) in zip(pes, dps, tiles)]
            pts = [jnp.concatenate([pe[:, :C].T, pe[:, C:].T], axis=1).astype(BF) for pe in pes]
            dsts = [jnp.concatenate([ds[:, :C].T, ds[:, C:].T], axis=1).astype(BF) for ds in dss]
            dvs = [_dot(pt, dd2[sel]) for pt, (sel, _) in zip(pts, tiles)]
            rs = [_dot(dst, qq2[sel]) for dst, (sel, _) in zip(dsts, tiles)]
            parts = [_dot(ds.astype(BF), jnp.concatenate([kj, ones], axis=1)) for ds, kj in zip(dss, kk)]
            for t, dv, rr in zip(ts, dvs, rs):
                dvb[t] = dv
                dkb[t] = rr
            pa, pb = zero, zero
            for t, part in zip(ts, parts):
                pa = pa + jnp.where(t <= s, part, zero)
                pb = pb + jnp.where(t <= s, zero, part)
            dq2[0] += pa
            dq2[1] += pb
            return carry

        ntile = jnp.where(s == NSTEP - 1, grp, NTILE)
        lax.fori_loop(0, ntile // grp, group, 0)

        def scatter(t, carry):
            _, j = _fox_tile(s, t)
            r = pl.ds(pl.multiple_of(j * C, C), C)
            dvacc[r, :] += dvb[t]
            dkacc[r, :] += dkb[t, :, :C]
            csacc[r, :] += dkb[t, :, C:]
            return carry

        lax.fori_loop(0, ntile, scatter, 0)
        for w, chunk in ((1, NCH - 1 - s), (0, s)):
            res = dq2[w]
            dq_ref[_rows(chunk), :] = res[:, :C].astype(BF)
            dr_ref[_rows(2 * s + w), :] = res[:, C:]

        @pl.when(s == NSTEP - 1)
        def _():
            dk_ref[...] = dkacc[...].astype(BF)
            dv_ref[...] = dvacc[...].astype(BF)
            dcs_ref[...] = csacc[...]

    both = _fox_pair_columns()
    col = pl.BlockSpec((T, C), lambda p, s: (0, p))
    return pl.pallas_call(
        body, name="fox_bwd", grid=(NPAIR, NSTEP),
        in_specs=[ANY] * 4
        + [pl.BlockSpec((NCH, FH, C), lambda p, s: (0, 0, 0)),
           pl.BlockSpec((2 * C, C), lambda p, s: (0, 0)),
           pl.BlockSpec((2, C, 2 * C), lambda p, s: (0, 0, 0))] + [ANY] * n,
        out_specs=[col, both, col, col, col] + [ANY] * n,
        out_shape=[jax.ShapeDtypeStruct((T, FH * FD), BF), jax.ShapeDtypeStruct((TROWS, FH * FD), F32),
                   jax.ShapeDtypeStruct((T, FH * FD), BF), jax.ShapeDtypeStruct((T, FH * FD), BF),
                   jax.ShapeDtypeStruct((T, FH * FD), F32)]
        + [jax.ShapeDtypeStruct(p.shape, p.dtype) for p in parts],
        scratch_shapes=[pltpu.VMEM((NCH, 2 * C, C), BF), pltpu.VMEM((NCH, 2 * C, C), BF),
                        pltpu.VMEM((2, C, C), BF), pltpu.VMEM((2, 2 * C, 2 * C), BF), pltpu.VMEM((2, 2 * C, C), BF),
                        pltpu.VMEM((2, C, C), BF), pltpu.VMEM((2, C, 2 * C), F32), pltpu.VMEM((2, C, 2 * C), F32),
                        pltpu.VMEM((2, C, 2 * C), F32),
                        pltpu.VMEM((NTILE, C, C), F32), pltpu.VMEM((NTILE, C, 2 * C), F32),
                        pltpu.VMEM((T, C), F32), pltpu.VMEM((T, C), F32), pltpu.VMEM((T, C), F32),
                        pltpu.VMEM((2, T, C), F32), pltpu.VMEM((2, T, C), F32), pltpu.VMEM((2, T, C), F32),
                        pltpu.VMEM((2, T, C), BF), pltpu.VMEM((2, T, C), F32), pltpu.VMEM((2, T, C), F32),
                        pltpu.SemaphoreType.DMA((6, 2))]
        + ([pltpu.SemaphoreType.DMA((3 * n,)), pltpu.SemaphoreType.DMA((3 * n,))] if n else []),
        compiler_params=_params(("arbitrary", "arbitrary")),
    )(z, da, g, delta, ct, cst["ones_aug"], cst["mask_bias"], *parts)


def _fox_gate_bwd(drow, dcol, zf, bf_pad, cst):
    def body(dr_ref, dc_ref, zf_ref, b_ref, tri_ref, pick_ref, dff_ref, db_ref, carry):
        s = pl.program_id(0)
        n = NCH - 1 - s

        @pl.when(s == 0)
        def _():
            carry[...] = jnp.zeros_like(carry)
            db_ref[...] = jnp.zeros_like(db_ref)

        dcb = _split_dot((dr_ref[...] - dc_ref[...]) * (1.0 / FSCALE), pick_ref[...])
        suf = _split_dot(dcb, tri_ref[...], TN, x_first=False) + carry[0:1, :]
        carry[...] = jnp.broadcast_to(suf[0:1, :], carry.shape)
        x = zf_ref[...] + b_ref[...]
        row = n * C + lax.broadcasted_iota(jnp.int32, (C, C), 0)
        dff = jnp.where(row >= PAD, suf * (1.0 - jax.nn.sigmoid(x)), 0.0)
        dff_ref[...] = dff.astype(BF)
        db_ref[...] += jnp.sum(dff, axis=0, keepdims=True)

    rev = lambda s: (NCH - 1 - s, 0)
    return pl.pallas_call(
        body, name="fox_gate_bwd", grid=(NCH,),
        in_specs=[pl.BlockSpec((C, FH * FD), lambda s: (_fox_pos(NCH - 1 - s), 0)),
                  pl.BlockSpec((C, FH * FD), rev), pl.BlockSpec((C, C), rev),
                  pl.BlockSpec((1, C), lambda s: (0, 0)), pl.BlockSpec((C, C), lambda s: (0, 0)),
                  pl.BlockSpec((FH * FD, C), lambda s: (0, 0))],
        out_specs=[pl.BlockSpec((C, C), rev), pl.BlockSpec((1, C), lambda s: (0, 0))],
        out_shape=[jax.ShapeDtypeStruct((T, C), BF), jax.ShapeDtypeStruct((1, C), F32)],
        scratch_shapes=[pltpu.VMEM((8, C), F32)],
        compiler_params=_params(("arbitrary",)),
    )(drow, dcol, zf, bf_pad, cst["tri"], cst["pick"])


def _head_norm(r):
    rn, rs = [], []
    for h in range(RH):
        rh = r[:, RDV * h:RDV * (h + 1)]
        s = lax.rsqrt(jnp.mean(rh * rh, axis=1, keepdims=True) + EPS)
        rn.append(rh * s)
        rs.append(s)
    return jnp.concatenate(rn, axis=1), rs


def _gated(r, rg, a, fg):
    rn, _ = _head_norm(r)
    return jnp.concatenate([rn * (rg * jax.nn.sigmoid(rg)), a * (fg * jax.nn.sigmoid(fg))], axis=1)


def _out_loss(r, z, a, wout, x, tgt, fgain):
    def body(r_ref, rg_ref, a_ref, fg_ref, w_ref, x_ref, t_ref, g_ref, yt_ref, do_ref, dob_ref, loss_ref, dg_ref):
        i = pl.program_id(0)

        @pl.when(i == 0)
        def _():
            yt_ref[...] = jnp.zeros_like(yt_ref)
            do_ref[...] = jnp.zeros_like(do_ref)
            dob_ref[...] = jnp.zeros_like(dob_ref)
            loss_ref[...] = jnp.zeros_like(loss_ref)
            dg_ref[...] = jnp.zeros_like(dg_ref)

        @pl.when(i > 0)
        def _():
            y = _gated(r_ref[...], rg_ref[...], a_ref[...], fg_ref[...])
            yt_ref[...] = y.T.astype(BF)
            o = x_ref[...] + _dot(y.astype(BF), w_ref[...])
            rs = lax.rsqrt(jnp.mean(o * o, axis=1, keepdims=True) + EPS)
            on = o * rs
            g = g_ref[...]
            e = on * g - t_ref[...]
            loss_ref[...] += 0.5 * jnp.sum(jnp.mean(e * e, axis=1, keepdims=True))
            dyh = e * (1.0 / D)
            dg_ref[...] += jnp.sum(dyh * on, axis=0, keepdims=True)
            don = dyh * g
            do = rs * (don - on * jnp.mean(don * on, axis=1, keepdims=True))
            do_ref[...] = do
            dob_ref[...] = do.astype(BF)

    tok = lambda i: (jnp.maximum(i - 1, 0), 0)
    return pl.pallas_call(
        body, name="out_loss", grid=(NCH,),
        in_specs=[pl.BlockSpec((C, D), lambda i: (i, 0)), pl.BlockSpec((C, D), lambda i: (i, GB_R)),
                  pl.BlockSpec((C, D), lambda i: (_fox_pos(i), 0)), pl.BlockSpec((C, D), lambda i: (i, GB_F)),
                  pl.BlockSpec((DMIX, D), lambda i: (0, 0)),
                  pl.BlockSpec((C, D), tok), pl.BlockSpec((C, D), tok), pl.BlockSpec((1, D), lambda i: (0, 0))],
        out_specs=[pl.BlockSpec((DMIX, C), lambda i: (0, i)), pl.BlockSpec((C, D), lambda i: (i, 0)),
                   pl.BlockSpec((C, D), lambda i: (i, 0)), pl.BlockSpec((8, C), lambda i: (0, 0)),
                   pl.BlockSpec((1, D), lambda i: (0, 0))],
        out_shape=[jax.ShapeDtypeStruct((DMIX, T), BF), jax.ShapeDtypeStruct((T, D), F32),
                   jax.ShapeDtypeStruct((T, D), BF), jax.ShapeDtypeStruct((8, C), F32),
                   jax.ShapeDtypeStruct((1, D), F32)],
        compiler_params=_params(("arbitrary",)),
    )(r, z, a, z, wout, x, tgt, fgain)


def _silu_and_grad(x):
    s = jax.nn.sigmoid(x)
    return x * s, s * (1.0 + x * (1.0 - s))


def _dy_gate_bwd(dob, wout, r, z, a, seg, swap=()):
    n = len(swap)

    def body(do_ref, w_ref, r_ref, rg_ref, a_ref, fg_ref, seg_ref, *rest):
        (dr_ref, da_ref, drg_ref, dfg_ref, dl_ref) = rest[n:n + 5]
        if n:
            copies = _pair_copies(rest[:n], rest[n + 5:2 * n + 5], *rest[2 * n + 5:], n)

            @pl.when(pl.program_id(0) == 0)
            def _():
                for cp in copies:
                    cp.start()

            @pl.when(pl.program_id(0) == NCH - 1)
            def _():
                for cp in copies:
                    cp.wait()

        dy = _dg(do_ref[...], w_ref[...], NT)
        a_ = a_ref[...]
        rn, rs = _head_norm(r_ref[...])
        silu_rg, dsilu_rg = _silu_and_grad(rg_ref[...])
        silu_fg, dsilu_fg = _silu_and_grad(fg_ref[...])
        dyr, dyf = dy[:, :D], dy[:, D:]
        drn = dyr * silu_rg
        drg_ref[...] = (dyr * rn * dsilu_rg).astype(BF)
        for h in range(RH):
            sl = slice(RDV * h, RDV * (h + 1))
            dh, nh = drn[:, sl], rn[:, sl]
            dr_ref[:, sl] = (rs[h] * (dh - nh * jnp.mean(dh * nh, axis=1, keepdims=True))).astype(BF)
        dab = (dyf * silu_fg).astype(BF)
        da_ref[...] = dab
        dfg_ref[...] = (dyf * a_ * dsilu_fg).astype(BF)
        prod = dab.astype(F32) * a_
        segm = seg_ref[...]
        for p in range(NPAIR):
            sl = slice(C * p, C * (p + 1))
            hi = prod[:, sl].astype(BF)
            lo = (prod[:, sl] - hi.astype(F32)).astype(BF)
            dl_ref[:, sl] = _dot(hi, segm) + _dot(lo, segm)

    row = pl.BlockSpec((C, D), lambda i: (i, 0))
    fox = pl.BlockSpec((C, D), lambda i: (_fox_pos(i), 0))
    return pl.pallas_call(
        body, name="dy_gate_bwd", grid=(NCH,),
        in_specs=[row, pl.BlockSpec((DMIX, D), lambda i: (0, 0)),
                  row, pl.BlockSpec((C, D), lambda i: (i, GB_R)),
                  fox, pl.BlockSpec((C, D), lambda i: (i, GB_F)),
                  pl.BlockSpec((C, C), lambda i: (0, 0))] + [ANY] * n,
        out_specs=[row, fox, row, row, fox] + [ANY] * n,
        out_shape=[jax.ShapeDtypeStruct((T, D), BF), jax.ShapeDtypeStruct((TROWS, D), BF),
                   jax.ShapeDtypeStruct((T, D), BF), jax.ShapeDtypeStruct((T, D), BF),
                   jax.ShapeDtypeStruct((TROWS, D), F32)]
        + [jax.ShapeDtypeStruct((4, s.shape[1] // 2, s.shape[2]), s.dtype) for s in swap],
        scratch_shapes=[pltpu.SemaphoreType.DMA((n,)), pltpu.SemaphoreType.DMA((n,))] if n else [],
        compiler_params=_params(("arbitrary",)),
    )(dob, wout, r, z, a, z, seg, *swap)


DZ_WIDTHS = (512, 512, 1024, 1024, 1024, 1024, 1024, 1024)


def _du_norm_bwd(dzs, dzf, wt, wft, hpad, g, dopad, parts=()):
    tm, tk = 544, 1024
    nk = WMAIN // tk
    ni = T // tm
    n = len(parts)

    def body(rq_ref, rk_ref, rv_ref, rg_ref, fq_ref, fk_ref, fv_ref, fg_ref, dzf_ref, w_ref, wf_ref, h_ref, g_ref,
             do_ref, *rest):
        part_refs, (gh_ref, dg_ref), land_refs = rest[:n], rest[n:n + 2], rest[n + 2:2 * n + 2]
        acc = rest[2 * n + 2]
        i, k = pl.program_id(0), pl.program_id(1)

        if n:
            send_sems, recv_sems = rest[2 * n + 3:]
            copies = _chip_copies(part_refs, land_refs, send_sems, recv_sems, by_dest=True)

            @pl.when((i == 0) & (k == 0))
            def _():
                for cp in copies:
                    cp.start()

            @pl.when((i == ni - 1) & (k == nk - 1))
            def _():
                for cp in copies:
                    cp.wait()

        @pl.when(k == 0)
        def _():
            acc[...] = (_dot(dzf_ref[...], wf_ref[...]) + _dot(rq_ref[...], w_ref[:512, :])
                        + _dot(rk_ref[...], w_ref[512:, :]))

        for kk, piece in enumerate((rv_ref, rg_ref, fq_ref, fk_ref, fv_ref, fg_ref), start=1):
            @pl.when(k == kk)
            def _(piece=piece):
                acc[...] += _dot(piece[...], w_ref[...])

        @pl.when(k == nk - 1)
        def _():
            du = acc[...]
            h = h_ref[...]
            gg = g_ref[...]
            rs = lax.rsqrt(jnp.mean(h * h, axis=1, keepdims=True) + EPS)
            hn = h * rs
            part = jnp.sum(du * hn, axis=0, keepdims=True)

            @pl.when(i == 0)
            def _():
                dg_ref[...] = part

            @pl.when(i > 0)
            def _():
                dg_ref[...] += part

            dhn = du * gg
            gh_ref[...] = rs * (dhn - hn * jnp.mean(dhn * hn, axis=1, keepdims=True)) + do_ref[...]

    sems = [pltpu.SemaphoreType.DMA((3 * n,)), pltpu.SemaphoreType.DMA((3 * n,))] if n else []
    return pl.pallas_call(
        body, name="du_norm_bwd", grid=(ni, nk),
        in_specs=[pl.BlockSpec((tm, w), lambda i, k: (i, 0)) for w in DZ_WIDTHS]
        + [pl.BlockSpec((tm, C), lambda i, k: (i, 0)),
           pl.BlockSpec((tk, D), lambda i, k: (k, 0)), pl.BlockSpec((C, D), lambda i, k: (0, 0)),
           pl.BlockSpec((tm, D), lambda i, k: (i, 0)), pl.BlockSpec((1, D), lambda i, k: (0, 0)),
           pl.BlockSpec((tm, D), lambda i, k: (i, 0))] + [ANY] * n,
        out_specs=[pl.BlockSpec((tm, D), lambda i, k: (i, 0)), pl.BlockSpec((1, D), lambda i, k: (0, 0))] + [ANY] * n,
        out_shape=[jax.ShapeDtypeStruct((T, D), F32), jax.ShapeDtypeStruct((1, D), F32)]
        + [jax.ShapeDtypeStruct(p.shape, p.dtype) for p in parts],
        scratch_shapes=[pltpu.VMEM((tm, D), F32)] + sems,
        compiler_params=_params(("arbitrary", "arbitrary")),
    )(*dzs, dzf, wt, wft, hpad, g, dopad, *parts)


GROWS = 7680


def _dw_in(dzs, dzf, ut):
    tn = 512
    nmain = WMAIN // tn
    first, blocks = [], []
    for w in DZ_WIDTHS:
        first.append(sum(blocks))
        blocks.append(w // tn)

    def body(rq_ref, rk_ref, rv_ref, rg_ref, fq_ref, fk_ref, fv_ref, fg_ref, dzf_ref, ut_ref, o_ref):
        gidx = pl.program_id(0)
        for piece, g0, nb in zip((rq_ref, rk_ref, rv_ref, rg_ref, fq_ref, fk_ref, fv_ref, fg_ref), first, blocks):
            @pl.when((gidx >= g0) & (gidx < g0 + nb))
            def _(piece=piece):
                o_ref[...] = _dot(ut_ref[...], piece[...]).T.astype(BF)

        @pl.when(gidx == nmain)
        def _():
            o_ref[:C, :] = _dot(ut_ref[...], dzf_ref[...]).T.astype(BF)
            o_ref[C:, :] = jnp.zeros((tn - C, D), BF)

    def piece_spec(g0, nb):
        return pl.BlockSpec((T, tn), lambda gidx: (0, jnp.clip(gidx - g0, 0, nb - 1)))

    return pl.pallas_call(
        body, name="dw_in", grid=(nmain + 1,),
        in_specs=[piece_spec(g0, nb) for g0, nb in zip(first, blocks)]
        + [pl.BlockSpec((T, C), lambda gidx: (0, 0)), pl.BlockSpec((D, T), lambda gidx: (0, 0))],
        out_specs=pl.BlockSpec((tn, D), lambda gidx: (gidx, 0)),
        out_shape=jax.ShapeDtypeStruct((GROWS, D), BF),
        compiler_params=pltpu.CompilerParams(dimension_semantics=("arbitrary",), vmem_limit_bytes=DW_VMEM_LIMIT),
    )(*dzs, dzf, ut)


def _local_step(x, tgt, normed, norm_g, wt, wft, b_f, wout, final_g, reduce_scatter=False, wout_full=None):
    cst = _constants()
    hpad, u, ut = normed
    bf_pad = jnp.pad(b_f, ((0, 0), (0, C - NFF)))
    z = _mm_nt(u, wt, WMAIN, T // 2, 1024, "in_proj")
    zf = _mm_nt(u, wft, C, T // 2, C, "in_proj_ff")
    r, sprev = _ret_fwd(z, cst)
    ct = _fox_prep(zf, bf_pad, cst)
    if wout_full is None:
        a, g = _fox_fwd(z, ct, cst, None)
    else:
        a, g, landed_wout = _fox_fwd(z, ct, cst, wout)
        wout = wout_full(landed_wout)
    yt, dopad, dob, loss8, dfg = _out_loss(r, z, a, wout, x, tgt, final_g)
    dwout = _mm_nn(yt, dob, 512, D, "dw_out", BF)
    g_out = [dwout.reshape(4, DMIX // 4, D)] if reduce_scatter else []
    dr, da, dzrg, dzfg, delta, *r_out = _dy_gate_bwd(dob, wout, r, z, a, cst["seg"], g_out)
    p_out = [_add_halves(g_out[0], r_out[0], "pair_add_out", BF)] if reduce_scatter else []
    dzq_r, dzk_r, dzv_r = _ret_bwd(z, cst, sprev, dr)
    dzq_f, drow, dzk_f, dzv_f, dcol, *e_out = _fox_bwd(z, da, g, delta, ct, cst, p_out)
    dzf, dbf = _fox_gate_bwd(drow, dcol, zf, bf_pad, cst)
    dzs = [dzq_r, dzk_r, dzv_r, dzrg, dzq_f, dzk_f, dzv_f, dzfg]
    gwt = _dw_in(dzs, dzf, ut)
    p_in = [_add_windows(gwt, *_pair_swap(gwt, [], "rs_pair_swap_in"))] if reduce_scatter else []
    gh, dng, *e_in = _du_norm_bwd(dzs, dzf, wt, wft, hpad, norm_g, dopad, p_in)
    return (loss8[0, 0], gh[C:], gh[PAD:C], dng, gwt, dbf[:, :NFF], dwout, dfg, p_in + p_out, e_in + e_out)


WOFF, WLEN = 1792, 2048
WHALF = WLEN // 2
LAP = WPADROWS - WOFF


def _own_window(w3):
    rows, sub, lanes = w3.shape
    pad = WPADROWS - rows
    tb = 96
    nb = WPADROWS // tb
    half = rows // 2

    def body(w_ref, o_ref, buf, sems):
        x, y, _ = _place()
        shift = 4 * (2 * x + y)
        buf[pl.ds(0, pad)] = jnp.zeros((pad, sub, lanes), F32)
        buf[pl.ds(rows, pad)] = jnp.zeros((pad, sub, lanes), F32)
        cps = [pltpu.make_async_copy(w_ref.at[pl.ds(half * h, half)], buf.at[pl.ds(shift + half * h, half)],
                                     sems.at[h]) for h in range(2)]
        for cp in cps:
            cp.start()

        def block(i, carry):
            r0 = pl.multiple_of(i * tb, tb)
            o_ref[pl.ds(r0, tb), :] = buf[pl.ds(r0, tb)].reshape(tb, sub * lanes).astype(BF)
            return carry

        cps[0].wait()
        lax.fori_loop(0, half // tb, block, 0)
        cps[1].wait()
        lax.fori_loop(half // tb, nb, block, 0)

    return pl.pallas_call(
        body, name="own_window",
        in_specs=[ANY], out_shape=jax.ShapeDtypeStruct((WPADROWS, sub * lanes), BF),
        scratch_shapes=[pltpu.VMEM((WPADROWS, sub, lanes), F32), pltpu.SemaphoreType.DMA((2,))],
        compiler_params=pltpu.CompilerParams(vmem_limit_bytes=VMEM_LIMIT),
    )(w3)


def _gather_weights(own_win, meta, x, norm_g):
    half_main, half_lap, half_meta = WOFF // 2, LAP // 2, meta.shape[0] // 2
    last = NCH - 1

    def body(win_ref, meta_ref, x_ref, g_ref, w_ref, laps_ref, gm_ref, h_ref, u_ref, ut_ref,
             send_sems, recv_sems, local_sems, stage, lapbuf, headbuf, metabuf):
        step = pl.program_id(0)
        x, y, c = _place()
        me_s = 2 * x + y
        sib = (x, y, 1 - c)
        chips = _other_chips(x, y)

        def emit(h):
            u = _norm_rows(h, g_ref[...])
            h_ref[...] = h
            u_ref[...] = u.astype(BF)
            ut_ref[...] = u.T.astype(BF)

        kinds = [
            (lambda h: win_ref.at[pl.ds(half_main * h, half_main)],
             lambda s, h: w_ref.at[pl.ds(WOFF * s + half_main * h, half_main)]),
            (lambda h: win_ref.at[pl.ds(WOFF + half_lap * h, half_lap)],
             lambda s, h: laps_ref.at[s, pl.ds(half_lap * h, half_lap)]),
            (lambda h: meta_ref.at[pl.ds(half_meta * h, half_meta)],
             lambda s, h: gm_ref.at[s, pl.ds(half_meta * h, half_meta)]),
        ]
        own_in = pltpu.make_async_copy(win_ref.at[pl.ds(0, WOFF)], stage, local_sems.at[0])
        own_lap_in = pltpu.make_async_copy(win_ref.at[pl.ds(WOFF, LAP)], lapbuf.at[0], local_sems.at[1])
        own_out = pltpu.make_async_copy(stage, w_ref.at[pl.ds(WOFF * me_s, WOFF)], local_sems.at[0])
        own_lap_out = pltpu.make_async_copy(lapbuf.at[0], laps_ref.at[me_s], local_sems.at[1])
        sends, arrivals, forwards, forwarded = [], [], [], []
        for a, (src, dst) in enumerate(kinds):
            for k, (cx, cy, cs) in enumerate(chips):
                there = dict(send_sem=send_sems.at[6 * a + k], recv_sem=recv_sems.at[6 * a + k],
                             device_id=(cx, cy, c), device_id_type=MESH)
                across = dict(send_sem=send_sems.at[6 * a + 3 + k], recv_sem=recv_sems.at[6 * a + 3 + k],
                              device_id=sib, device_id_type=MESH)
                sends.append(pltpu.make_async_remote_copy(src_ref=src(c), dst_ref=dst(me_s, c), **there))
                arrivals.append(pltpu.make_async_remote_copy(src_ref=dst(cs, c), dst_ref=dst(cs, c), **there))
                forwards.append(pltpu.make_async_remote_copy(src_ref=dst(cs, c), dst_ref=dst(cs, c), **across))
                forwarded.append(pltpu.make_async_remote_copy(
                    src_ref=dst(cs, 1 - c), dst_ref=dst(cs, 1 - c), **across))

        @pl.when(step == 0)
        def _():
            own_in.start()
            own_lap_in.start()
            for cp in sends:
                cp.start()
            own_in.wait()
            own_out.start()
            own_lap_in.wait()
            own_lap_out.start()

        @pl.when(step < last)
        def _():
            emit(x_ref[...])

        @pl.when(step == last)
        def _():
            for cp, fwd in zip(arrivals, forwards):
                cp.wait_recv()
                fwd.start()
            for cp in forwarded:
                cp.wait_recv()
            for cp in sends + forwards:
                cp.wait_send()
            own_out.wait()
            own_lap_out.wait()
            for s in range(1, 4):
                head = w_ref.at[pl.ds(WOFF * s, LAP)]
                loads = [pltpu.make_async_copy(laps_ref.at[s - 1], lapbuf.at[1], local_sems.at[2]),
                         pltpu.make_async_copy(head, headbuf, local_sems.at[3])]
                for cp in loads:
                    cp.start()
                for cp in loads:
                    cp.wait()
                headbuf[...] = (headbuf[...].astype(F32) + lapbuf[1].astype(F32)).astype(BF)
                store = pltpu.make_async_copy(headbuf, head, local_sems.at[3])
                store.start()
                store.wait()
            loads = [pltpu.make_async_copy(meta_ref, metabuf.at[me_s], local_sems.at[0])]
            loads += [pltpu.make_async_copy(gm_ref.at[cs], metabuf.at[cs], local_sems.at[1 + k])
                      for k, (_, _, cs) in enumerate(chips)]
            for cp in loads:
                cp.start()
            for cp in loads:
                cp.wait()
            tokens = jnp.concatenate([metabuf[s] for s in range(4)], axis=1)
            emit(jnp.concatenate([jnp.zeros((PAD, D), F32), tokens], axis=0))

    def chunk(i):
        return (i + 1) % NCH

    return pl.pallas_call(
        body, name="all_gather_w", grid=(NCH,),
        in_specs=[ANY, ANY, pl.BlockSpec((C, D), lambda i: (jnp.minimum(i, last - 1), 0)),
                  pl.BlockSpec((1, D), lambda i: (0, 0))],
        out_specs=[ANY] * 3 + [pl.BlockSpec((C, D), lambda i: (chunk(i), 0))] * 2
        + [pl.BlockSpec((D, C), lambda i: (0, chunk(i)))],
        out_shape=[jax.ShapeDtypeStruct((WMAIN, D), own_win.dtype), jax.ShapeDtypeStruct((4, LAP, D), own_win.dtype),
                   jax.ShapeDtypeStruct((4,) + meta.shape, meta.dtype),
                   jax.ShapeDtypeStruct((T, D), F32), jax.ShapeDtypeStruct((T, D), BF),
                   jax.ShapeDtypeStruct((D, T), BF)],
        scratch_shapes=[pltpu.SemaphoreType.DMA((18,)), pltpu.SemaphoreType.DMA((18,)), pltpu.SemaphoreType.DMA((4,)),
                        pltpu.VMEM((WOFF, D), own_win.dtype), pltpu.VMEM((2, LAP, D), own_win.dtype),
                        pltpu.VMEM((LAP, D), own_win.dtype), pltpu.VMEM((4,) + meta.shape, meta.dtype)],
        compiler_params=_params(("arbitrary",)),
    )(own_win, meta, x, norm_g)


def _pair_copies(ins, outs, send_sems, recv_sems, n):
    x, y, c = _place()
    sib = dict(device_id=(x, y, 1 - c), device_id_type=MESH)
    cps = []
    for a in range(n):
        rows = ins[a].shape[1] // 2
        cps.append(pltpu.make_async_remote_copy(
            src_ref=ins[a].at[:, pl.ds((1 - c) * rows, rows)], dst_ref=outs[a],
            send_sem=send_sems.at[a], recv_sem=recv_sems.at[a], **sib))
    for k in range(4 * (len(ins) - n)):
        cps.append(pltpu.make_async_remote_copy(
            src_ref=ins[n].at[pl.ds(WOFF * k + (1 - c) * WHALF, WHALF)], dst_ref=outs[n].at[k],
            send_sem=send_sems.at[n + k], recv_sem=recv_sems.at[n + k], **sib))
    return cps


def _pair_swap(gwt, arrs, name):
    n = len(arrs)
    wins = [] if gwt is None else [gwt]
    m = n + len(wins)
    nsem = n + 4 * len(wins)

    def body(*refs):
        cps = _pair_copies(refs[:m], refs[m:2 * m], *refs[2 * m:], n)
        for cp in cps:
            cp.start()
        for cp in cps:
            cp.wait()

    return pl.pallas_call(
        body, name=name,
        in_specs=[ANY] * m, out_specs=[ANY] * m,
        out_shape=[jax.ShapeDtypeStruct((4, a.shape[1] // 2, a.shape[2]), a.dtype) for a in arrs]
        + [jax.ShapeDtypeStruct((4, WHALF, D), w.dtype) for w in wins],
        scratch_shapes=[pltpu.SemaphoreType.DMA((nsem,)), pltpu.SemaphoreType.DMA((nsem,))],
    )(*arrs, *wins)


def _add_windows(gwt, recv):
    tb = 256
    nb = WHALF // tb
    c = lax.axis_index("c")

    def body(c_ref, a_ref, b_ref, o_ref):
        o_ref[0] = (a_ref[...].astype(F32) + b_ref[0].astype(F32)).astype(BF)

    return pl.pallas_call(
        body, name="pair_add_in",
        grid_spec=pltpu.PrefetchScalarGridSpec(
            num_scalar_prefetch=1, grid=(4, nb),
            in_specs=[pl.BlockSpec((tb, D), lambda k, i, cr: ((WOFF // tb) * k + nb * cr[0] + i, 0)),
                      pl.BlockSpec((1, tb, D), lambda k, i, cr: (k, i, 0))],
            out_specs=pl.BlockSpec((1, tb, D), lambda k, i, cr: (k, i, 0))),
        out_shape=jax.ShapeDtypeStruct(recv.shape, BF),
        compiler_params=_params(("parallel", "parallel")),
    )(jnp.reshape(c, (1,)).astype(jnp.int32), gwt, recv)


def _chip_exchange(parts, small):
    n = len(parts)

    def body(*refs):
        ins, sm = refs[:n], refs[n]
        outs, smo = refs[n + 1:2 * n + 1], refs[2 * n + 1]
        send_sems, recv_sems = refs[2 * n + 2:]
        cps = _chip_copies(ins, outs, send_sems, recv_sems, by_dest=True)
        cps += _chip_copies([sm], [smo], send_sems.at[pl.ds(3 * n, 3)], recv_sems.at[pl.ds(3 * n, 3)], by_dest=False)
        for cp in cps:
            cp.start()
        for cp in cps:
            cp.wait()

    return pl.pallas_call(
        body, name="rs_chip_exchange",
        in_specs=[ANY] * (n + 1), out_specs=[ANY] * (n + 1),
        out_shape=[jax.ShapeDtypeStruct(p.shape, p.dtype) for p in parts]
        + [jax.ShapeDtypeStruct((4,) + small.shape, small.dtype)],
        scratch_shapes=[pltpu.SemaphoreType.DMA((3 * (n + 1),)), pltpu.SemaphoreType.DMA((3 * (n + 1),))],
    )(*parts, small)


def _pair_send(halves):
    n = len(halves)

    def body(*refs):
        ins, outs = refs[:n], refs[n:2 * n]
        send_sems, recv_sems = refs[2 * n:]
        x, y, c = _place()
        cps = [pltpu.make_async_remote_copy(
            src_ref=ins[a], dst_ref=outs[a], send_sem=send_sems.at[a], recv_sem=recv_sems.at[a],
            device_id=(x, y, 1 - c), device_id_type=MESH) for a in range(n)]
        for cp in cps:
            cp.start()
        for cp in cps:
            cp.wait()

    return pl.pallas_call(
        body, name="rs_pair_send",
        in_specs=[ANY] * n, out_specs=[ANY] * n,
        out_shape=[jax.ShapeDtypeStruct(h.shape, h.dtype) for h in halves],
        scratch_shapes=[pltpu.SemaphoreType.DMA((n,)), pltpu.SemaphoreType.DMA((n,))],
    )(*halves)


def _row_block(rows):
    for tb in (256, 128, 64, 32, 16, 8):
        if rows % tb == 0:
            return tb
    return rows


def _add_halves(full, recv, name, out_dtype):
    _, r2, w = recv.shape
    tb = _row_block(r2)
    nb = r2 // tb
    c = lax.axis_index("c")

    def body(c_ref, a_ref, b_ref, o_ref):
        o_ref[...] = (a_ref[...].astype(F32) + b_ref[...].astype(F32)).astype(o_ref.dtype)

    return pl.pallas_call(
        body, name=name,
        grid_spec=pltpu.PrefetchScalarGridSpec(
            num_scalar_prefetch=1, grid=(4, nb),
            in_specs=[pl.BlockSpec((1, tb, w), lambda s, i, cr: (s, cr[0] * nb + i, 0)),
                      pl.BlockSpec((1, tb, w), lambda s, i, cr: (s, i, 0))],
            out_specs=pl.BlockSpec((1, tb, w), lambda s, i, cr: (s, i, 0))),
        out_shape=jax.ShapeDtypeStruct(recv.shape, out_dtype),
        compiler_params=_params(("parallel", "parallel")),
    )(jnp.reshape(c, (1,)).astype(jnp.int32), full, recv)


def _add2(a, b, name):
    def body(a_ref, b_ref, o_ref):
        o_ref[...] = a_ref[...] + b_ref[...]

    return pl.pallas_call(body, name=name, out_shape=jax.ShapeDtypeStruct(a.shape, a.dtype))(a, b)


def _sum4(buf, own, name):
    _, r, w = buf.shape
    tb = _row_block(r)
    me_s = 2 * lax.axis_index("x") + lax.axis_index("y")
    by_dest = own.ndim == 3

    def body(s_ref, b_ref, own_ref, o_ref):
        mine = (own_ref[0] if by_dest else own_ref[...]).astype(F32)
        terms = [jnp.where(s_ref[0] == t, mine, b_ref[t].astype(F32)) for t in range(4)]
        o_ref[...] = ((terms[0] + terms[1]) + terms[2]) + terms[3]

    own_spec = (pl.BlockSpec((1, tb, w), lambda i, sr: (sr[0], i, 0)) if by_dest
                else pl.BlockSpec((tb, w), lambda i, sr: (i, 0)))
    return pl.pallas_call(
        body, name=name,
        grid_spec=pltpu.PrefetchScalarGridSpec(
            num_scalar_prefetch=1, grid=(r // tb,),
            in_specs=[pl.BlockSpec((4, tb, w), lambda i, sr: (0, i, 0)), own_spec],
            out_specs=pl.BlockSpec((tb, w), lambda i, sr: (i, 0))),
        out_shape=jax.ShapeDtypeStruct((r, w), F32),
        compiler_params=_params(("parallel",)),
    )(jnp.reshape(me_s, (1,)).astype(jnp.int32), buf, own)


def _adamw_math(w, g, m, v):
    mn = B1 * m + (1.0 - B1) * g
    vn = B2 * v + (1.0 - B2) * (g * g)
    m_hat = mn / (1.0 - B1 ** STEP)
    v_hat = vn / (1.0 - B2 ** STEP)
    return -LR * (m_hat / (jnp.sqrt(v_hat) + AEPS) + WD * w), mn, vn


def _adamw(w, g, m, v, name):
    r, c_ = w.shape
    tb = _row_block(r)
    if tb == r and r > 512:
        tb = 256

    def body(w_ref, g_ref, m_ref, v_ref, d_ref, mo_ref, vo_ref):
        d_ref[...], mo_ref[...], vo_ref[...] = _adamw_math(w_ref[...], g_ref[...], m_ref[...], v_ref[...])

    spec = pl.BlockSpec((tb, c_), lambda i: (i, 0))
    return pl.pallas_call(
        body, name=name, grid=(pl.cdiv(r, tb),),
        in_specs=[spec] * 4, out_specs=[spec] * 3,
        out_shape=[jax.ShapeDtypeStruct(w.shape, F32)] * 3,
        compiler_params=_params(("parallel",)),
    )(w, g, m, v)


def _adamw_rows(w, g_mine, g_sib, m, v, name):
    r = w.shape[0]
    tb = 256
    sub, lanes = w.shape[1:]
    nh = g_mine.shape[0] // tb
    nsteps = pl.cdiv(r, tb)
    assert nsteps <= 2 * nh and 4 * 3 + r <= 2 * nh * tb
    x, y, c = _place()
    place = jnp.stack([c, 4 * (2 * x + y)]).astype(jnp.int32)

    def body(p_ref, w_ref, mc_ref, sc_ref, mn_ref, sn_ref, m_ref, v_ref, go_ref, d_ref, mo_ref, vo_ref, buf):
        i = pl.program_id(0)
        for at, blk, mine_ref, sib_ref in ((0, i, mc_ref, sc_ref), (1, jnp.minimum(i + 1, 2 * nh - 1), mn_ref, sn_ref)):
            rows = jnp.where(blk // nh == p_ref[0], mine_ref[...], sib_ref[...])
            buf[tb * at:tb * (at + 1)] = rows.reshape(tb, sub, lanes)
        g = buf[pl.ds(p_ref[1], tb)]
        go_ref[...] = g
        d_ref[...], mo_ref[...], vo_ref[...] = _adamw_math(w_ref[...], g, m_ref[...], v_ref[...])

    def half_spec(ahead, sibling):
        def index(i, pr):
            half = (1 - pr[0]) if sibling else pr[0]
            return (jnp.clip(jnp.minimum(i + ahead, 2 * nh - 1) - nh * half, 0, nh - 1), 0)
        return pl.BlockSpec((tb, sub * lanes), index)

    spec = pl.BlockSpec((tb, sub, lanes), lambda i, pr: (i, 0, 0))
    return pl.pallas_call(
        body, name=name,
        grid_spec=pltpu.PrefetchScalarGridSpec(
            num_scalar_prefetch=1, grid=(nsteps,),
            in_specs=[spec, half_spec(0, False), half_spec(0, True), half_spec(1, False), half_spec(1, True),
                      spec, spec],
            out_specs=[spec] * 4,
            scratch_shapes=[pltpu.VMEM((2 * tb, sub, lanes), F32)]),
        out_shape=[jax.ShapeDtypeStruct(w.shape, F32)] * 4,
        compiler_params=_params(("parallel",)),
    )(place, w, g_mine, g_sib, g_mine, g_sib, m, v)


def _adamw_halves(w, g_mine, g_sib, m, v, name):
    r, c_ = w.shape
    r2 = g_mine.shape[0]
    tb = _row_block(r2)
    nb = r2 // tb
    c = lax.axis_index("c")

    def body(c_ref, w_ref, gm_ref, gs_ref, m_ref, v_ref, g_ref, d_ref, mo_ref, vo_ref):
        g = jnp.where(pl.program_id(0) == c_ref[0], gm_ref[...], gs_ref[...])
        g_ref[...] = g
        d_ref[...], mo_ref[...], vo_ref[...] = _adamw_math(w_ref[...], g, m_ref[...], v_ref[...])

    full = pl.BlockSpec((tb, c_), lambda h, i, cr: (h * nb + i, 0))
    half = pl.BlockSpec((tb, c_), lambda h, i, cr: (i, 0))
    return pl.pallas_call(
        body, name=name,
        grid_spec=pltpu.PrefetchScalarGridSpec(
            num_scalar_prefetch=1, grid=(2, nb),
            in_specs=[full, half, half, full, full], out_specs=[full] * 4),
        out_shape=[jax.ShapeDtypeStruct(w.shape, F32)] * 4,
        compiler_params=_params(("parallel", "parallel")),
    )(jnp.reshape(c, (1,)).astype(jnp.int32), w, g_mine, g_sib, m, v)


def kernel(x, meta_tokens, norm_g, w_in, b_f, w_out, final_g, loss_target, m_meta_tokens, m_norm_g, m_w_in, m_b_f, m_w_out, m_final_g, v_meta_tokens, v_norm_g, v_w_in, v_b_f, v_w_out, v_final_g):
    me_s = 2 * lax.axis_index("x") + lax.axis_index("y")
    w3, m3, v3 = [jnp.transpose(jnp.reshape(t[0], (D // C, C, WSH)), (2, 0, 1)) for t in (w_in, m_w_in, v_w_in)]

    wt_main, laps, _, *normed = _gather_weights(_own_window(w3), meta_tokens, x[0], norm_g)
    wft = jnp.pad(laps[3, :NFF], ((0, C - NFF), (0, 0)))
    mine = (jnp.arange(4) == me_s)[:, None, None]
    wout_own = w_out[0].astype(BF)

    def wout_full(landed):
        return jnp.where(mine, wout_own[None], landed).reshape(DMIX, D)

    loss, gx, dmeta, dng, gwt, dbf, dwout, dfg, (p_in, p_out), (e_in, e_out) = _local_step(
        x[0], loss_target[0], normed, norm_g, wt_main, wft, b_f, wout_own, final_g.reshape(1, D), True, wout_full)

    g_meta = jnp.stack([dmeta[:, 256 * s:256 * (s + 1)] for s in range(4)])
    small = jnp.concatenate([dng, dfg, jnp.pad(dbf, ((0, 0), (0, D - NFF))),
                             jnp.pad(jnp.reshape(loss, (1, 1)), ((0, 0), (0, D - 1))),
                             jnp.zeros((4, D), F32)], axis=0)
    e_meta, e_small = _chip_exchange([g_meta], small)
    h_in, h_out = _sum4(e_in, p_in, "sum_in"), _sum4(e_out, p_out, "sum_out")
    h_meta, h_small = _sum4(e_meta, g_meta, "sum_meta"), _sum4(e_small, small, "sum_small")
    s_in, s_out, s_meta, s_small = _pair_send([h_in, h_out, h_meta, h_small])
    gw_meta = _add2(h_meta, s_meta, "pair_add_meta")
    tot = _add2(h_small, s_small, "pair_add_small")
    g_norm, g_final, g_bf, loss_all = tot[0:1], tot[1], tot[2:3, :NFF], tot[3, 0]

    d_meta, nm_meta, nv_meta = _adamw(meta_tokens, gw_meta, m_meta_tokens, v_meta_tokens, "adamw_meta")
    d_norm, nm_norm, nv_norm = _adamw(norm_g, g_norm, m_norm_g, v_norm_g, "adamw_norm")
    outs_in = _adamw_rows(w3, h_in, s_in, m3, v3, "adamw_in")
    gw_in, d_in, nm_in, nv_in = [jnp.reshape(jnp.transpose(t, (1, 2, 0)), (1, D, WSH)) for t in outs_in]
    d_bf, nm_bf, nv_bf = _adamw(b_f, g_bf, m_b_f, v_b_f, "adamw_bf")
    gw_out, d_out, nm_out, nv_out = _adamw_halves(w_out[0], h_out, s_out, m_w_out[0], v_w_out[0], "adamw_out")
    d_fin, nm_fin, nv_fin = _adamw(final_g.reshape(1, D), g_final.reshape(1, D), m_final_g.reshape(1, D),
                                   v_final_g.reshape(1, D), "adamw_final")
    return (loss_all, gx[None], gw_meta, g_norm, gw_in, g_bf, gw_out[None], g_final,
            d_meta, d_norm, d_in, d_bf, d_out[None], d_fin.reshape(D),
            nm_meta, nm_norm, nm_in, nm_bf, nm_out[None], nm_fin.reshape(D),
            nv_meta, nv_norm, nv_in, nv_bf, nv_out[None], nv_fin.reshape(D))
```

```python
import numpy as np
import jax
import jax.numpy as jnp
from jax import lax
from jax.experimental import pallas as pl
from jax.experimental.pallas import tpu as pltpu

D = 1024
SEQ = 2048
NMETA = 16
C = 128
PAD = C - NMETA
T = PAD + NMETA + SEQ
NCH = T // C
RH, RDK, RDV = 4, 128, 256
FH, FD = 16, 64
NPAIR = FH // 2
WMAIN = 7168
NFF = 16
WIN = WMAIN + NFF
WSH = WIN // 4
WPADROWS = 1824
DMIX = 2048
EPS = 1e-6
NEG = -1e30
RSCALE = RDK ** -0.5
FSCALE = FD ** -0.5
ROPE_BASE = 10000.0
LR, B1, B2, AEPS, WD, STEP = 0.001, 0.9, 0.999, 1e-08, 0.01, 10

BF = jnp.bfloat16
F32 = jnp.float32
NT = (((1,), (1,)), ((), ()))
TN = (((0,), (0,)), ((), ()))
NN_DIMS = (((1,), (0,)), ((), ()))
MESH = pl.DeviceIdType.MESH
ANY = pl.BlockSpec(memory_space=pl.ANY)
VMEM_LIMIT = 48 * 1024 * 1024
DW_VMEM_LIMIT = 56 * 1024 * 1024

GB_R, GB_F = 2, 6
QB_F, KB_F, VB_F = 24, 32, 40


def _dot(a, b):
    return jnp.dot(a, b, preferred_element_type=F32)


def _dg(a, b, dims):
    return lax.dot_general(a, b, dims, preferred_element_type=F32)


def _params(sem=None):
    return pltpu.CompilerParams(dimension_semantics=sem, vmem_limit_bytes=VMEM_LIMIT)


def _constants():
    pos = jnp.arange(T, dtype=F32) - PAD
    inv = ROPE_BASE ** (-jnp.arange(0, RDK, 2, dtype=F32) / RDK)
    ang = pos[:, None] * inv[None, :]
    cos, sin = jnp.cos(ang), jnp.sin(ang)
    cos2 = jnp.concatenate([cos, cos], axis=1)
    sin2 = jnp.concatenate([-sin, sin], axis=1)
    log_gamma = jnp.log1p(-jnp.exp2(-5.0 - jnp.arange(RH, dtype=F32)))
    idx = jnp.arange(C, dtype=F32)
    diff = idx[:, None] - idx[None, :]
    dmask = jnp.where(diff[None] >= 0, jnp.exp(log_gamma[:, None, None] * jnp.maximum(diff, 0.0)[None]), 0.0)
    zeta = jnp.exp(log_gamma[:, None] * (C - 1.0 - idx)[None, :])
    xi = jnp.exp(log_gamma[:, None] * (idx + 1.0)[None, :])
    gdec = jnp.exp(log_gamma * C)
    zeta_b = jnp.broadcast_to(zeta[:, :, None], (RH, C, RDK))
    xi_b = jnp.broadcast_to(xi[:, :, None], (RH, C, RDK))
    gdec_b = jnp.broadcast_to(gdec[:, None, None], (RH, RDK, RDV))
    tri = jnp.asarray(np.tril(np.ones((C, C), np.float32)), dtype=BF)
    head_of_lane = np.arange(FH * FD) // FD
    pick = ((np.arange(FH * FD)[:, None] % FD == 0)
            & (head_of_lane[:, None] == np.arange(C)[None, :])).astype(np.float32)
    seg = (np.arange(C)[:, None] // FD == np.arange(C)[None, :] // FD).astype(np.float32)
    ones_aug = np.concatenate([np.tile((np.arange(C) < FD)[None, :], (C, 1)),
                               np.tile((np.arange(C) >= FD)[None, :], (C, 1))], axis=0).astype(np.float32)
    lane = np.arange(2 * C) % C
    causal = np.where(lane[None, :] <= np.arange(C)[:, None], 0.0, NEG).astype(np.float32)
    mask_bias = np.stack([np.zeros((C, 2 * C), np.float32), causal])
    return dict(cos2=cos2, sin2=sin2, dmask=dmask, zeta=zeta_b, xi=xi_b, gdec=gdec_b, tri=tri,
                mask_bias=jnp.asarray(mask_bias), pick=jnp.asarray(pick, dtype=BF), seg=jnp.asarray(seg, dtype=BF),
                ones_aug=jnp.asarray(ones_aug, dtype=BF))


def _norm_rows(h, g):
    return h * lax.rsqrt(jnp.mean(h * h, axis=1, keepdims=True) + EPS) * g


def _mm_nt(a, b, n, tm, tn, name):
    m, k = a.shape

    def body(a_ref, b_ref, o_ref):
        o_ref[...] = _dg(a_ref[...], b_ref[...], NT)

    return pl.pallas_call(
        body, name=name, grid=(m // tm, n // tn),
        in_specs=[pl.BlockSpec((tm, k), lambda i, j: (i, 0)), pl.BlockSpec((tn, k), lambda i, j: (j, 0))],
        out_specs=pl.BlockSpec((tm, tn), lambda i, j: (i, j)),
        out_shape=jax.ShapeDtypeStruct((m, n), F32),
        compiler_params=_params(("parallel", "parallel")),
    )(a, b)


def _mm_nn(a, b, tm, tn, name, out_dtype=F32):
    m, k = a.shape
    _, n = b.shape

    def body(a_ref, b_ref, o_ref):
        o_ref[...] = _dot(a_ref[...], b_ref[...]).astype(out_dtype)

    return pl.pallas_call(
        body, name=name, grid=(m // tm, n // tn),
        in_specs=[pl.BlockSpec((tm, k), lambda i, j: (i, 0)), pl.BlockSpec((k, tn), lambda i, j: (0, j))],
        out_specs=pl.BlockSpec((tm, tn), lambda i, j: (i, j)),
        out_shape=jax.ShapeDtypeStruct((m, n), out_dtype),
        compiler_params=_params(("parallel", "parallel")),
    )(a, b)


def _rot(x, cos2, sin2):
    return x * cos2 + pltpu.roll(x, 64, 1) * sin2


def _ret_specs(chunk):
    whole = lambda shape: pl.BlockSpec(shape, lambda n: (0,) * len(shape))
    return [
        pl.BlockSpec((C, RH * RDK), lambda n: (chunk(n), 0)),
        pl.BlockSpec((C, RH * RDK), lambda n: (chunk(n), 1)),
        pl.BlockSpec((C, RH * RDV), lambda n: (chunk(n), 1)),
        pl.BlockSpec((C, RDK), lambda n: (chunk(n), 0)),
        pl.BlockSpec((C, RDK), lambda n: (chunk(n), 0)),
        whole((RH, C, C)), whole((RH, C, RDK)), whole((RH, C, RDK)), whole((RH, RDK, RDV)),
    ]


def _ret_heads(q_ref, k_ref, v_ref, cos, sin):
    qr = [_rot(q_ref[:, RDK * h:RDK * (h + 1)], cos, sin) for h in range(RH)]
    kr = [_rot(k_ref[:, RDK * h:RDK * (h + 1)], cos, sin) * RSCALE for h in range(RH)]
    vb = [v_ref[:, RDV * h:RDV * (h + 1)].astype(BF) for h in range(RH)]
    return qr, kr, [t.astype(BF) for t in qr], [t.astype(BF) for t in kr], vb


def _ret_fwd(z, cst):
    def body(q_ref, k_ref, v_ref, cos_ref, sin_ref, dm_ref, xi_ref, zt_ref, gd_ref, r_ref, sp_ref, st):
        n = pl.program_id(0)

        @pl.when(n == 0)
        def _():
            st[...] = jnp.zeros_like(st)

        hs = range(RH)
        qr, kr, qb, kb, vb = _ret_heads(q_ref, k_ref, v_ref, cos_ref[...], sin_ref[...])
        sd = [(_dg(qb[h], kb[h], NT) * dm_ref[h]).astype(BF) for h in hs]
        state = [st[h] for h in hs]
        qx = [(qr[h] * xi_ref[h]).astype(BF) for h in hs]
        kz = [(kr[h] * zt_ref[h]).astype(BF) for h in hs]
        out = [_dot(sd[h], vb[h]) + _dot(qx[h], state[h].astype(BF)) for h in hs]
        kv = [_dg(kz[h], vb[h], TN) for h in hs]
        for h in hs:
            sp_ref[0, h] = state[h]
            r_ref[:, RDV * h:RDV * (h + 1)] = out[h]
            st[h] = state[h] * gd_ref[h] + kv[h]

    return pl.pallas_call(
        body, name="ret_fwd", grid=(NCH,),
        in_specs=_ret_specs(lambda n: n),
        out_specs=[pl.BlockSpec((C, RH * RDV), lambda n: (n, 0)),
                   pl.BlockSpec((1, RH, RDK, RDV), lambda n: (n, 0, 0, 0))],
        out_shape=[jax.ShapeDtypeStruct((T, RH * RDV), F32), jax.ShapeDtypeStruct((NCH, RH, RDK, RDV), F32)],
        scratch_shapes=[pltpu.VMEM((RH, RDK, RDV), F32)],
        compiler_params=_params(("arbitrary",)),
    )(z, z, z, cst["cos2"], cst["sin2"], cst["dmask"], cst["xi"], cst["zeta"], cst["gdec"])


def _ret_bwd(z, cst, sprev, dr):
    def body(q_ref, k_ref, v_ref, cos_ref, sin_ref, dm_ref, xi_ref, zt_ref, gd_ref, sp_ref, dr_ref,
             dq_ref, dk_ref, dv_ref, gst):
        i = pl.program_id(0)

        @pl.when(i == 0)
        def _():
            gst[...] = jnp.zeros_like(gst)

        hs = range(RH)
        cos, sin = cos_ref[...], sin_ref[...]
        qr, kr, qb, kb, vb = _ret_heads(q_ref, k_ref, v_ref, cos, sin)
        dm = [dm_ref[h] for h in hs]
        xi = [xi_ref[h] for h in hs]
        zt = [zt_ref[h] for h in hs]
        sd = [(_dg(qb[h], kb[h], NT) * dm[h]).astype(BF) for h in hs]
        qx = [(qr[h] * xi[h]).astype(BF) for h in hs]
        kz = [(kr[h] * zt[h]).astype(BF) for h in hs]
        drb = [dr_ref[:, RDV * h:RDV * (h + 1)] for h in hs]
        sb = [sp_ref[0, h].astype(BF) for h in hs]
        g = [gst[h] for h in hs]
        gb = [t.astype(BF) for t in g]
        ds = [(_dg(drb[h], vb[h], NT) * dm[h]).astype(BF) for h in hs]
        dq = [_dot(ds[h], kb[h]) + _dg(drb[h], sb[h], NT) * xi[h] for h in hs]
        dk = [(_dg(ds[h], qb[h], TN) + _dg(vb[h], gb[h], NT) * zt[h]) * RSCALE for h in hs]
        dv = [_dg(sd[h], drb[h], TN) + _dot(kz[h], gb[h]) for h in hs]
        gn = [g[h] * gd_ref[h] + _dg(qx[h], drb[h], TN) for h in hs]
        for h in hs:
            gst[h] = gn[h]
            dq_ref[:, RDK * h:RDK * (h + 1)] = (dq[h] * cos + pltpu.roll(dq[h] * sin, 64, 1)).astype(BF)
            dk_ref[:, RDK * h:RDK * (h + 1)] = (dk[h] * cos + pltpu.roll(dk[h] * sin, 64, 1)).astype(BF)
            dv_ref[:, RDV * h:RDV * (h + 1)] = dv[h].astype(BF)

    rev = lambda n: NCH - 1 - n
    return pl.pallas_call(
        body, name="ret_bwd", grid=(NCH,),
        in_specs=_ret_specs(rev) + [
            pl.BlockSpec((1, RH, RDK, RDV), lambda n: (rev(n), 0, 0, 0)),
            pl.BlockSpec((C, RH * RDV), lambda n: (rev(n), 0)),
        ],
        out_specs=[pl.BlockSpec((C, RH * RDK), lambda n: (rev(n), 0)),
                   pl.BlockSpec((C, RH * RDK), lambda n: (rev(n), 0)),
                   pl.BlockSpec((C, RH * RDV), lambda n: (rev(n), 0))],
        out_shape=[jax.ShapeDtypeStruct((T, RH * RDK), BF), jax.ShapeDtypeStruct((T, RH * RDK), BF),
                   jax.ShapeDtypeStruct((T, RH * RDV), BF)],
        scratch_shapes=[pltpu.VMEM((RH, RDK, RDV), F32)],
        compiler_params=_params(("arbitrary",)),
    )(z, z, z, cst["cos2"], cst["sin2"], cst["dmask"], cst["xi"], cst["zeta"], cst["gdec"], sprev, dr)


def _place():
    x, y, c = lax.axis_index("x"), lax.axis_index("y"), lax.axis_index("c")
    return x, y, c


def _other_chips(x, y):
    return [(1 - x, y, 2 * (1 - x) + y), (x, 1 - y, 2 * x + (1 - y)), (1 - x, 1 - y, 2 * (1 - x) + (1 - y))]


def _chip_copies(srcs, lands, send_sems, recv_sems, by_dest):
    x, y, c = _place()
    me_s = 2 * x + y
    return [pltpu.make_async_remote_copy(
        src_ref=src.at[cs] if by_dest else src, dst_ref=land.at[me_s],
        send_sem=send_sems.at[3 * a + j], recv_sem=recv_sems.at[3 * a + j],
        device_id=(cx, cy, c), device_id_type=MESH)
        for a, (src, land) in enumerate(zip(srcs, lands)) for j, (cx, cy, cs) in enumerate(_other_chips(x, y))]


def _split_dot(x, mat01, dims=NN_DIMS, x_first=True):
    acc, rest = None, x
    for _ in range(3):
        piece = rest.astype(BF)
        part = _dg(piece, mat01, dims) if x_first else _dg(mat01, piece, dims)
        acc = part if acc is None else acc + part
        rest = rest - piece.astype(F32)
    return acc


def _log_sigmoid(x):
    return -(jnp.maximum(-x, 0.0) + jnp.log1p(jnp.exp(-jnp.abs(x))))


def _fox_prep(zf, bf_pad, cst):
    def body(zf_ref, b_ref, tri_ref, ct_ref, carry):
        n = pl.program_id(0)

        @pl.when(n == 0)
        def _():
            carry[...] = jnp.zeros_like(carry)

        ls = _log_sigmoid(zf_ref[...] + b_ref[...])
        row = n * C + lax.broadcasted_iota(jnp.int32, (C, C), 0)
        lf = jnp.where(row >= PAD, ls, 0.0)
        cc = _split_dot(lf, tri_ref[...], x_first=False) + carry[0:1, :]
        carry[...] = jnp.broadcast_to(cc[C - 1:C, :], carry.shape)
        pos = n * C + lax.broadcasted_iota(jnp.int32, (FH, C), 1)
        ct_ref[0] = jnp.where(pos >= PAD, cc.T[:FH, :], -NEG)

    return pl.pallas_call(
        body, name="fox_prep", grid=(NCH,),
        in_specs=[pl.BlockSpec((C, C), lambda n: (n, 0)), pl.BlockSpec((1, C), lambda n: (0, 0)),
                  pl.BlockSpec((C, C), lambda n: (0, 0))],
        out_specs=pl.BlockSpec((1, FH, C), lambda n: (n, 0, 0)),
        out_shape=jax.ShapeDtypeStruct((NCH, FH, C), F32),
        scratch_shapes=[pltpu.VMEM((8, C), F32)],
        compiler_params=_params(("arbitrary",)),
    )(zf, bf_pad, cst["tri"])


def _lo_lanes(shape):
    return lax.broadcasted_iota(jnp.int32, shape, 1) < FD


def _split_heads(x):
    lo = _lo_lanes(x.shape)
    zero = jnp.zeros_like(x)
    return jnp.concatenate([jnp.where(lo, x, zero), jnp.where(lo, zero, x)], axis=0)


def _spread2(x):
    lo = _lo_lanes(x.shape)
    r = pltpu.roll(x, FD, 1)
    return jnp.concatenate([jnp.where(lo, x, r), jnp.where(lo, r, x)], axis=1)


NSTEP = (NCH + 1) // 2
NTILE = NCH + 1
TROWS = T + C


def _fox_tile(s, t):
    second = t > s
    return second.astype(jnp.int32), jnp.where(second, t - s - 1, s - t)


def _fox_pos(i):
    return jnp.where(i < NSTEP, 2 * i, 2 * (NCH - 1 - i) + 1)


def _fox_pair_columns():
    return pl.BlockSpec((TROWS, C), lambda p, s: (0, p))


def _fox_key_bias(ct_ref, p, j):
    return jnp.concatenate([ct_ref[j, pl.ds(2 * p, 1), :], ct_ref[j, pl.ds(2 * p + 1, 1), :]], axis=1)


def _fox_columns(cols, sems, p):
    def copies(pair, slot):
        return [pltpu.make_async_copy(
            src.at[pl.ds(0, buf.shape[1]), pl.ds(pl.multiple_of((first + pair) * C, C), C)], buf.at[slot],
            sems.at[i, slot]) for i, (src, first, buf) in enumerate(cols)]

    @pl.when(p == 0)
    def _():
        for cp in copies(0, 0):
            cp.start()

    for cp in copies(p, p % 2):
        cp.wait()

    @pl.when(p + 1 < NPAIR)
    def _():
        for cp in copies(p + 1, 1 - p % 2):
            cp.start()


def _rows(block, size=C):
    return pl.ds(pl.multiple_of(block * size, size), size)


def _fox_fwd(z, ct, cst, share):
    n = 0 if share is None else 1

    def body(z_ref, ct_ref, ones_ref, mb_ref, *rest):
        share_refs, (a_ref, g_ref), land_refs = rest[:n], rest[n:n + 2], rest[n + 2:2 * n + 2]
        kks, vvs, q2, m2, sbuf, qbuf, kbuf, vbuf, col_sems = rest[2 * n + 2:2 * n + 11]
        p, s = pl.program_id(0), pl.program_id(1)
        slot = p % 2
        if n:
            copies = _chip_copies(share_refs, land_refs, *rest[2 * n + 11:], by_dest=False)

            @pl.when((p == 0) & (s == 0))
            def _():
                for cp in copies:
                    cp.start()

            @pl.when((p == NPAIR - 1) & (s == NSTEP - 1))
            def _():
                for cp in copies:
                    cp.wait()

        @pl.when(s == 0)
        def _():
            ones = ones_ref[...]
            _fox_columns([(z_ref, QB_F, qbuf), (z_ref, KB_F, kbuf), (z_ref, VB_F, vbuf)], col_sems, p)

            def prep(j, carry):
                kks[j] = _split_heads(kbuf[slot, _rows(j), :]).astype(BF)
                vvs[j] = jnp.concatenate([_split_heads(vbuf[slot, _rows(j), :]).astype(BF), ones], axis=1)
                return carry

            lax.fori_loop(0, NCH, prep, 0)

        q2[0] = (qbuf[slot, _rows(s), :] * FSCALE).astype(BF)
        q2[1] = (qbuf[slot, _rows(NCH - 1 - s), :] * FSCALE).astype(BF)

        tiles = [_fox_tile(s, t) for t in range(NTILE)]
        causal = mb_ref[1]
        neg = jnp.full((C, 2 * C), NEG, F32)
        run, first = neg, neg
        for t, (sel, j) in enumerate(tiles):
            st = _dg(q2[sel], kks[j], NT) - _fox_key_bias(ct_ref, p, j)
            if t in (0, NTILE - 1):
                st = st + causal
            sbuf[t] = st
            run = jnp.maximum(jnp.where(t == s + 1, neg, run), st)
            first = jnp.where(t == s, run, first)
        for w, mx in enumerate((first, run)):
            m2[w] = jnp.concatenate(
                [jnp.broadcast_to(jnp.max(mx[:, :C], axis=1, keepdims=True), (C, C)),
                 jnp.broadcast_to(jnp.max(mx[:, C:], axis=1, keepdims=True), (C, C))], axis=1)

        zero = jnp.zeros((C, 2 * C), F32)
        run, first = zero, zero
        for t, (sel, j) in enumerate(tiles):
            run = jnp.where(t == s + 1, zero, run) + _dot(jnp.exp(sbuf[t] - m2[sel]).astype(BF), vvs[j])
            first = jnp.where(t == s, run, first)
        lo = _lo_lanes((C, C))
        for w, res in enumerate((first, run)):
            l = res[:, C:]
            a_ref[_rows(2 * s + w), :] = res[:, :C] / l
            mw = m2[w]
            g_ref[_rows(2 * s + w), :] = -(jnp.where(lo, mw[:, :C], mw[:, C:]) + jnp.log(l))

    col = _fox_pair_columns()
    return pl.pallas_call(
        body, name="fox_fwd", grid=(NPAIR, NSTEP),
        in_specs=[ANY,
                  pl.BlockSpec((NCH, FH, C), lambda p, s: (0, 0, 0)),
                  pl.BlockSpec((2 * C, C), lambda p, s: (0, 0)),
                  pl.BlockSpec((2, C, 2 * C), lambda p, s: (0, 0, 0))] + [ANY] * n,
        out_specs=[col, col] + [ANY] * n,
        out_shape=[jax.ShapeDtypeStruct((TROWS, FH * FD), F32)] * 2
        + ([jax.ShapeDtypeStruct((4,) + share.shape, share.dtype)] if n else []),
        scratch_shapes=[pltpu.VMEM((NCH, 2 * C, C), BF), pltpu.VMEM((NCH, 2 * C, 2 * C), BF),
                        pltpu.VMEM((2, C, C), BF), pltpu.VMEM((2, C, 2 * C), F32),
                        pltpu.VMEM((NTILE, C, 2 * C), F32),
                        pltpu.VMEM((2, T, C), F32), pltpu.VMEM((2, T, C), F32), pltpu.VMEM((2, T, C), F32),
                        pltpu.SemaphoreType.DMA((3, 2))]
        + [pltpu.SemaphoreType.DMA((3,)), pltpu.SemaphoreType.DMA((3,))] * n,
        compiler_params=_params(("arbitrary", "arbitrary")),
    )(z, ct, cst["ones_aug"], cst["mask_bias"], *([share] * n))


def _fox_bwd(z, da, g, delta, ct, cst, parts=()):
    grp = 9

    n = len(parts)

    def body(z_ref, da_ref, g_ref, dl_ref, ct_ref, ones_ref, mb_ref, *rest):
        part_refs, (dq_ref, dr_ref, dk_ref, dv_ref, dcs_ref), land_refs = rest[:n], rest[n:n + 5], rest[n + 5:2 * n + 5]
        (kks, vvs, q2, qq2, dd2, da2, gi2, dl2, dq2, dvb, dkb, dkacc, dvacc, csacc, qbuf, kbuf, vbuf, dabuf, gbuf,
         dlbuf, col_sems) = rest[2 * n + 5:2 * n + 26]
        p, s = pl.program_id(0), pl.program_id(1)
        slot = p % 2
        ones = ones_ref[...]
        if n:
            copies = _chip_copies(part_refs, land_refs, *rest[2 * n + 26:], by_dest=True)

            @pl.when((p == 0) & (s == 0))
            def _():
                for cp in copies:
                    cp.start()

            @pl.when((p == NPAIR - 1) & (s == NSTEP - 1))
            def _():
                for cp in copies:
                    cp.wait()

        @pl.when(s == 0)
        def _():
            dkacc[...] = jnp.zeros_like(dkacc)
            dvacc[...] = jnp.zeros_like(dvacc)
            csacc[...] = jnp.zeros_like(csacc)
            _fox_columns([(z_ref, QB_F, qbuf), (z_ref, KB_F, kbuf), (z_ref, VB_F, vbuf), (da_ref, 0, dabuf),
                          (g_ref, 0, gbuf), (dl_ref, 0, dlbuf)], col_sems, p)

            def prep(j, carry):
                kks[j] = _split_heads(kbuf[slot, _rows(j), :]).astype(BF)
                vvs[j] = _split_heads(vbuf[slot, _rows(j), :]).astype(BF)
                return carry

            lax.fori_loop(0, NCH, prep, 0)

        for w, (chunk, blk) in enumerate(((s, 2 * s), (NCH - 1 - s, jnp.where(s == NSTEP - 1, 2 * s, 2 * s + 1)))):
            qf = qbuf[slot, _rows(chunk), :]
            q2[w] = (qf * FSCALE).astype(BF)
            qq2[w] = jnp.concatenate([_split_heads(qf).astype(BF), ones], axis=1)
            da2[w] = dabuf[slot, _rows(blk), :]
            dd2[w] = _split_heads(da2[w].astype(F32)).astype(BF)
            gi2[w] = _spread2(gbuf[slot, _rows(blk), :])
            dl2[w] = _spread2(dlbuf[slot, _rows(blk), :])
        dq2[...] = jnp.zeros_like(dq2)
        zero = jnp.zeros((C, 2 * C), F32)

        def group(gi, carry):
            ts = [gi * grp + u for u in range(grp)]
            tiles = [_fox_tile(s, t) for t in ts]
            kk = [kks[j] for _, j in tiles]
            ss = [_dg(q2[sel], kj, NT) + (gi2[sel] - _fox_key_bias(ct_ref, p, j)) for kj, (sel, j) in zip(kk, tiles)]
            ss[0] = ss[0] + mb_ref[(gi == 0).astype(jnp.int32)]
            ss[-1] = ss[-1] + mb_ref[(gi == 1).astype(jnp.int32)]
            dps = [_dg(da2[sel], vvs[j], NT) for sel, j in tiles]
            pes = [jnp.exp(st) for st in ss]
            dss = [pe * (dp - dl2[sel]) * FSCALE for pe, dp, (sel, _) in zip(pes, dps, tiles)]
            pts = [jnp.concatenate([pe[:, :C].T, pe[:, C:].T], axis=1).astype(BF) for pe in pes]
            dsts = [jnp.concatenate([ds[:, :C].T, ds[:, C:].T], axis=1).astype(BF) for ds in dss]
            dvs = [_dot(pt, dd2[sel]) for pt, (sel, _) in zip(pts, tiles)]
            rs = [_dot(dst, qq2[sel]) for dst, (sel, _) in zip(dsts, tiles)]
            parts = [_dot(ds.astype(BF), jnp.concatenate([kj, ones], axis=1)) for ds, kj in zip(dss, kk)]
            for t, dv, rr in zip(ts, dvs, rs):
                dvb[t] = dv
                dkb[t] = rr
            pa, pb = zero, zero
            for t, part in zip(ts, parts):
                pa = pa + jnp.where(t <= s, part, zero)
                pb = pb + jnp.where(t <= s, zero, part)
            dq2[0] += pa
            dq2[1] += pb
            return carry

        ntile = jnp.where(s == NSTEP - 1, grp, NTILE)
        lax.fori_loop(0, ntile // grp, group, 0)

        def scatter(t, carry):
            _, j = _fox_tile(s, t)
            r = pl.ds(pl.multiple_of(j * C, C), C)
            dvacc[r, :] += dvb[t]
            dkacc[r, :] += dkb[t, :, :C]
            csacc[r, :] += dkb[t, :, C:]
            return carry

        lax.fori_loop(0, ntile, scatter, 0)
        for w, chunk in ((1, NCH - 1 - s), (0, s)):
            res = dq2[w]
            dq_ref[_rows(chunk), :] = res[:, :C].astype(BF)
            dr_ref[_rows(2 * s + w), :] = res[:, C:]

        @pl.when(s == NSTEP - 1)
        def _():
            dk_ref[...] = dkacc[...].astype(BF)
            dv_ref[...] = dvacc[...].astype(BF)
            dcs_ref[...] = csacc[...]

    both = _fox_pair_columns()
    col = pl.BlockSpec((T, C), lambda p, s: (0, p))
    return pl.pallas_call(
        body, name="fox_bwd", grid=(NPAIR, NSTEP),
        in_specs=[ANY] * 4
        + [pl.BlockSpec((NCH, FH, C), lambda p, s: (0, 0, 0)),
           pl.BlockSpec((2 * C, C), lambda p, s: (0, 0)),
           pl.BlockSpec((2, C, 2 * C), lambda p, s: (0, 0, 0))] + [ANY] * n,
        out_specs=[col, both, col, col, col] + [ANY] * n,
        out_shape=[jax.ShapeDtypeStruct((T, FH * FD), BF), jax.ShapeDtypeStruct((TROWS, FH * FD), F32),
                   jax.ShapeDtypeStruct((T, FH * FD), BF), jax.ShapeDtypeStruct((T, FH * FD), BF),
                   jax.ShapeDtypeStruct((T, FH * FD), F32)]
        + [jax.ShapeDtypeStruct(p.shape, p.dtype) for p in parts],
        scratch_shapes=[pltpu.VMEM((NCH, 2 * C, C), BF), pltpu.VMEM((NCH, 2 * C, C), BF),
                        pltpu.VMEM((2, C, C), BF), pltpu.VMEM((2, 2 * C, 2 * C), BF), pltpu.VMEM((2, 2 * C, C), BF),
                        pltpu.VMEM((2, C, C), BF), pltpu.VMEM((2, C, 2 * C), F32), pltpu.VMEM((2, C, 2 * C), F32),
                        pltpu.VMEM((2, C, 2 * C), F32),
                        pltpu.VMEM((NTILE, C, C), F32), pltpu.VMEM((NTILE, C, 2 * C), F32),
                        pltpu.VMEM((T, C), F32), pltpu.VMEM((T, C), F32), pltpu.VMEM((T, C), F32),
                        pltpu.VMEM((2, T, C), F32), pltpu.VMEM((2, T, C), F32), pltpu.VMEM((2, T, C), F32),
                        pltpu.VMEM((2, T, C), BF), pltpu.VMEM((2, T, C), F32), pltpu.VMEM((2, T, C), F32),
                        pltpu.SemaphoreType.DMA((6, 2))]
        + ([pltpu.SemaphoreType.DMA((3 * n,)), pltpu.SemaphoreType.DMA((3 * n,))] if n else []),
        compiler_params=_params(("arbitrary", "arbitrary")),
    )(z, da, g, delta, ct, cst["ones_aug"], cst["mask_bias"], *parts)


def _fox_gate_bwd(drow, dcol, zf, bf_pad, cst):
    def body(dr_ref, dc_ref, zf_ref, b_ref, tri_ref, pick_ref, dff_ref, db_ref, carry):
        s = pl.program_id(0)
        n = NCH - 1 - s

        @pl.when(s == 0)
        def _():
            carry[...] = jnp.zeros_like(carry)
            db_ref[...] = jnp.zeros_like(db_ref)

        dcb = _split_dot((dr_ref[...] - dc_ref[...]) * (1.0 / FSCALE), pick_ref[...])
        suf = _split_dot(dcb, tri_ref[...], TN, x_first=False) + carry[0:1, :]
        carry[...] = jnp.broadcast_to(suf[0:1, :], carry.shape)
        x = zf_ref[...] + b_ref[...]
        row = n * C + lax.broadcasted_iota(jnp.int32, (C, C), 0)
        dff = jnp.where(row >= PAD, suf * (1.0 - jax.nn.sigmoid(x)), 0.0)
        dff_ref[...] = dff.astype(BF)
        db_ref[...] += jnp.sum(dff, axis=0, keepdims=True)

    rev = lambda s: (NCH - 1 - s, 0)
    return pl.pallas_call(
        body, name="fox_gate_bwd", grid=(NCH,),
        in_specs=[pl.BlockSpec((C, FH * FD), lambda s: (_fox_pos(NCH - 1 - s), 0)),
                  pl.BlockSpec((C, FH * FD), rev), pl.BlockSpec((C, C), rev),
                  pl.BlockSpec((1, C), lambda s: (0, 0)), pl.BlockSpec((C, C), lambda s: (0, 0)),
                  pl.BlockSpec((FH * FD, C), lambda s: (0, 0))],
        out_specs=[pl.BlockSpec((C, C), rev), pl.BlockSpec((1, C), lambda s: (0, 0))],
        out_shape=[jax.ShapeDtypeStruct((T, C), BF), jax.ShapeDtypeStruct((1, C), F32)],
        scratch_shapes=[pltpu.VMEM((8, C), F32)],
        compiler_params=_params(("arbitrary",)),
    )(drow, dcol, zf, bf_pad, cst["tri"], cst["pick"])


def _head_norm(r):
    rn, rs = [], []
    for h in range(RH):
        rh = r[:, RDV * h:RDV * (h + 1)]
        s = lax.rsqrt(jnp.mean(rh * rh, axis=1, keepdims=True) + EPS)
        rn.append(rh * s)
        rs.append(s)
    return jnp.concatenate(rn, axis=1), rs


def _gated(r, rg, a, fg):
    rn, _ = _head_norm(r)
    return jnp.concatenate([rn * (rg * jax.nn.sigmoid(rg)), a * (fg * jax.nn.sigmoid(fg))], axis=1)


def _out_loss(r, z, a, wout, x, tgt, fgain):
    def body(r_ref, rg_ref, a_ref, fg_ref, w_ref, x_ref, t_ref, g_ref, yt_ref, do_ref, dob_ref, loss_ref, dg_ref):
        i = pl.program_id(0)

        @pl.when(i == 0)
        def _():
            yt_ref[...] = jnp.zeros_like(yt_ref)
            do_ref[...] = jnp.zeros_like(do_ref)
            dob_ref[...] = jnp.zeros_like(dob_ref)
            loss_ref[...] = jnp.zeros_like(loss_ref)
            dg_ref[...] = jnp.zeros_like(dg_ref)

        @pl.when(i > 0)
        def _():
            y = _gated(r_ref[...], rg_ref[...], a_ref[...], fg_ref[...])
            yt_ref[...] = y.T.astype(BF)
            o = x_ref[...] + _dot(y.astype(BF), w_ref[...])
            rs = lax.rsqrt(jnp.mean(o * o, axis=1, keepdims=True) + EPS)
            on = o * rs
            g = g_ref[...]
            e = on * g - t_ref[...]
            loss_ref[...] += 0.5 * jnp.sum(jnp.mean(e * e, axis=1, keepdims=True))
            dyh = e * (1.0 / D)
            dg_ref[...] += jnp.sum(dyh * on, axis=0, keepdims=True)
            don = dyh * g
            do = rs * (don - on * jnp.mean(don * on, axis=1, keepdims=True))
            do_ref[...] = do
            dob_ref[...] = do.astype(BF)

    tok = lambda i: (jnp.maximum(i - 1, 0), 0)
    return pl.pallas_call(
        body, name="out_loss", grid=(NCH,),
        in_specs=[pl.BlockSpec((C, D), lambda i: (i, 0)), pl.BlockSpec((C, D), lambda i: (i, GB_R)),
                  pl.BlockSpec((C, D), lambda i: (_fox_pos(i), 0)), pl.BlockSpec((C, D), lambda i: (i, GB_F)),
                  pl.BlockSpec((DMIX, D), lambda i: (0, 0)),
                  pl.BlockSpec((C, D), tok), pl.BlockSpec((C, D), tok), pl.BlockSpec((1, D), lambda i: (0, 0))],
        out_specs=[pl.BlockSpec((DMIX, C), lambda i: (0, i)), pl.BlockSpec((C, D), lambda i: (i, 0)),
                   pl.BlockSpec((C, D), lambda i: (i, 0)), pl.BlockSpec((8, C), lambda i: (0, 0)),
                   pl.BlockSpec((1, D), lambda i: (0, 0))],
        out_shape=[jax.ShapeDtypeStruct((DMIX, T), BF), jax.ShapeDtypeStruct((T, D), F32),
                   jax.ShapeDtypeStruct((T, D), BF), jax.ShapeDtypeStruct((8, C), F32),
                   jax.ShapeDtypeStruct((1, D), F32)],
        compiler_params=_params(("arbitrary",)),
    )(r, z, a, z, wout, x, tgt, fgain)


def _silu_and_grad(x):
    s = jax.nn.sigmoid(x)
    return x * s, s * (1.0 + x * (1.0 - s))


def _dy_gate_bwd(dob, wout, r, z, a, seg, swap=()):
    n = len(swap)

    def body(do_ref, w_ref, r_ref, rg_ref, a_ref, fg_ref, seg_ref, *rest):
        (dr_ref, da_ref, drg_ref, dfg_ref, dl_ref) = rest[n:n + 5]
        if n:
            copies = _pair_copies(rest[:n], rest[n + 5:2 * n + 5], *rest[2 * n + 5:], n)

            @pl.when(pl.program_id(0) == 0)
            def _():
                for cp in copies:
                    cp.start()

            @pl.when(pl.program_id(0) == NCH - 1)
            def _():
                for cp in copies:
                    cp.wait()

        dy = _dg(do_ref[...], w_ref[...], NT)
        a_ = a_ref[...]
        rn, rs = _head_norm(r_ref[...])
        silu_rg, dsilu_rg = _silu_and_grad(rg_ref[...])
        silu_fg, dsilu_fg = _silu_and_grad(fg_ref[...])
        dyr, dyf = dy[:, :D], dy[:, D:]
        drn = dyr * silu_rg
        drg_ref[...] = (dyr * rn * dsilu_rg).astype(BF)
        for h in range(RH):
            sl = slice(RDV * h, RDV * (h + 1))
            dh, nh = drn[:, sl], rn[:, sl]
            dr_ref[:, sl] = (rs[h] * (dh - nh * jnp.mean(dh * nh, axis=1, keepdims=True))).astype(BF)
        dab = (dyf * silu_fg).astype(BF)
        da_ref[...] = dab
        dfg_ref[...] = (dyf * a_ * dsilu_fg).astype(BF)
        prod = dab.astype(F32) * a_
        segm = seg_ref[...]
        for p in range(NPAIR):
            sl = slice(C * p, C * (p + 1))
            hi = prod[:, sl].astype(BF)
            lo = (prod[:, sl] - hi.astype(F32)).astype(BF)
            dl_ref[:, sl] = _dot(hi, segm) + _dot(lo, segm)

    row = pl.BlockSpec((C, D), lambda i: (i, 0))
    fox = pl.BlockSpec((C, D), lambda i: (_fox_pos(i), 0))
    return pl.pallas_call(
        body, name="dy_gate_bwd", grid=(NCH,),
        in_specs=[row, pl.BlockSpec((DMIX, D), lambda i: (0, 0)),
                  row, pl.BlockSpec((C, D), lambda i: (i, GB_R)),
                  fox, pl.BlockSpec((C, D), lambda i: (i, GB_F)),
                  pl.BlockSpec((C, C), lambda i: (0, 0))] + [ANY] * n,
        out_specs=[row, fox, row, row, fox] + [ANY] * n,
        out_shape=[jax.ShapeDtypeStruct((T, D), BF), jax.ShapeDtypeStruct((TROWS, D), BF),
                   jax.ShapeDtypeStruct((T, D), BF), jax.ShapeDtypeStruct((T, D), BF),
                   jax.ShapeDtypeStruct((TROWS, D), F32)]
        + [jax.ShapeDtypeStruct((4, s.shape[1] // 2, s.shape[2]), s.dtype) for s in swap],
        scratch_shapes=[pltpu.SemaphoreType.DMA((n,)), pltpu.SemaphoreType.DMA((n,))] if n else [],
        compiler_params=_params(("arbitrary",)),
    )(dob, wout, r, z, a, z, seg, *swap)


DZ_WIDTHS = (512, 512, 1024, 1024, 1024, 1024, 1024, 1024)


def _du_norm_bwd(dzs, dzf, wt, wft, hpad, g, dopad, parts=()):
    tm, tk = 544, 1024
    nk = WMAIN // tk
    ni = T // tm
    n = len(parts)

    def body(rq_ref, rk_ref, rv_ref, rg_ref, fq_ref, fk_ref, fv_ref, fg_ref, dzf_ref, w_ref, wf_ref, h_ref, g_ref,
             do_ref, *rest):
        part_refs, (gh_ref, dg_ref), land_refs = rest[:n], rest[n:n + 2], rest[n + 2:2 * n + 2]
        acc = rest[2 * n + 2]
        i, k = pl.program_id(0), pl.program_id(1)

        if n:
            send_sems, recv_sems = rest[2 * n + 3:]
            copies = _chip_copies(part_refs, land_refs, send_sems, recv_sems, by_dest=True)

            @pl.when((i == 0) & (k == 0))
            def _():
                for cp in copies:
                    cp.start()

            @pl.when((i == ni - 1) & (k == nk - 1))
            def _():
                for cp in copies:
                    cp.wait()

        @pl.when(k == 0)
        def _():
            acc[...] = (_dot(dzf_ref[...], wf_ref[...]) + _dot(rq_ref[...], w_ref[:512, :])
                        + _dot(rk_ref[...], w_ref[512:, :]))

        for kk, piece in enumerate((rv_ref, rg_ref, fq_ref, fk_ref, fv_ref, fg_ref), start=1):
            @pl.when(k == kk)
            def _(piece=piece):
                acc[...] += _dot(piece[...], w_ref[...])

        @pl.when(k == nk - 1)
        def _():
            du = acc[...]
            h = h_ref[...]
            gg = g_ref[...]
            rs = lax.rsqrt(jnp.mean(h * h, axis=1, keepdims=True) + EPS)
            hn = h * rs
            part = jnp.sum(du * hn, axis=0, keepdims=True)

            @pl.when(i == 0)
            def _():
                dg_ref[...] = part

            @pl.when(i > 0)
            def _():
                dg_ref[...] += part

            dhn = du * gg
            gh_ref[...] = rs * (dhn - hn * jnp.mean(dhn * hn, axis=1, keepdims=True)) + do_ref[...]

    sems = [pltpu.SemaphoreType.DMA((3 * n,)), pltpu.SemaphoreType.DMA((3 * n,))] if n else []
    return pl.pallas_call(
        body, name="du_norm_bwd", grid=(ni, nk),
        in_specs=[pl.BlockSpec((tm, w), lambda i, k: (i, 0)) for w in DZ_WIDTHS]
        + [pl.BlockSpec((tm, C), lambda i, k: (i, 0)),
           pl.BlockSpec((tk, D), lambda i, k: (k, 0)), pl.BlockSpec((C, D), lambda i, k: (0, 0)),
           pl.BlockSpec((tm, D), lambda i, k: (i, 0)), pl.BlockSpec((1, D), lambda i, k: (0, 0)),
           pl.BlockSpec((tm, D), lambda i, k: (i, 0))] + [ANY] * n,
        out_specs=[pl.BlockSpec((tm, D), lambda i, k: (i, 0)), pl.BlockSpec((1, D), lambda i, k: (0, 0))] + [ANY] * n,
        out_shape=[jax.ShapeDtypeStruct((T, D), F32), jax.ShapeDtypeStruct((1, D), F32)]
        + [jax.ShapeDtypeStruct(p.shape, p.dtype) for p in parts],
        scratch_shapes=[pltpu.VMEM((tm, D), F32)] + sems,
        compiler_params=_params(("arbitrary", "arbitrary")),
    )(*dzs, dzf, wt, wft, hpad, g, dopad, *parts)


GROWS = 7680


def _dw_in(dzs, dzf, ut):
    tn = 512
    nmain = WMAIN // tn
    first, blocks = [], []
    for w in DZ_WIDTHS:
        first.append(sum(blocks))
        blocks.append(w // tn)

    def body(rq_ref, rk_ref, rv_ref, rg_ref, fq_ref, fk_ref, fv_ref, fg_ref, dzf_ref, ut_ref, o_ref):
        gidx = pl.program_id(0)
        for piece, g0, nb in zip((rq_ref, rk_ref, rv_ref, rg_ref, fq_ref, fk_ref, fv_ref, fg_ref), first, blocks):
            @pl.when((gidx >= g0) & (gidx < g0 + nb))
            def _(piece=piece):
                o_ref[...] = _dot(ut_ref[...], piece[...]).T.astype(BF)

        @pl.when(gidx == nmain)
        def _():
            o_ref[:C, :] = _dot(ut_ref[...], dzf_ref[...]).T.astype(BF)
            o_ref[C:, :] = jnp.zeros((tn - C, D), BF)

    def piece_spec(g0, nb):
        return pl.BlockSpec((T, tn), lambda gidx: (0, jnp.clip(gidx - g0, 0, nb - 1)))

    return pl.pallas_call(
        body, name="dw_in", grid=(nmain + 1,),
        in_specs=[piece_spec(g0, nb) for g0, nb in zip(first, blocks)]
        + [pl.BlockSpec((T, C), lambda gidx: (0, 0)), pl.BlockSpec((D, T), lambda gidx: (0, 0))],
        out_specs=pl.BlockSpec((tn, D), lambda gidx: (gidx, 0)),
        out_shape=jax.ShapeDtypeStruct((GROWS, D), BF),
        compiler_params=pltpu.CompilerParams(dimension_semantics=("arbitrary",), vmem_limit_bytes=DW_VMEM_LIMIT),
    )(*dzs, dzf, ut)


def _local_step(x, tgt, normed, norm_g, wt, wft, b_f, wout, final_g, reduce_scatter=False, wout_full=None):
    cst = _constants()
    hpad, u, ut = normed
    bf_pad = jnp.pad(b_f, ((0, 0), (0, C - NFF)))
    z = _mm_nt(u, wt, WMAIN, T // 2, 1024, "in_proj")
    zf = _mm_nt(u, wft, C, T // 2, C, "in_proj_ff")
    r, sprev = _ret_fwd(z, cst)
    ct = _fox_prep(zf, bf_pad, cst)
    if wout_full is None:
        a, g = _fox_fwd(z, ct, cst, None)
    else:
        a, g, landed_wout = _fox_fwd(z, ct, cst, wout)
        wout = wout_full(landed_wout)
    yt, dopad, dob, loss8, dfg = _out_loss(r, z, a, wout, x, tgt, final_g)
    dwout = _mm_nn(yt, dob, 512, D, "dw_out", BF)
    g_out = [dwout.reshape(4, DMIX // 4, D)] if reduce_scatter else []
    dr, da, dzrg, dzfg, delta, *r_out = _dy_gate_bwd(dob, wout, r, z, a, cst["seg"], g_out)
    p_out = [_add_halves(g_out[0], r_out[0], "pair_add_out", BF)] if reduce_scatter else []
    dzq_r, dzk_r, dzv_r = _ret_bwd(z, cst, sprev, dr)
    dzq_f, drow, dzk_f, dzv_f, dcol, *e_out = _fox_bwd(z, da, g, delta, ct, cst, p_out)
    dzf, dbf = _fox_gate_bwd(drow, dcol, zf, bf_pad, cst)
    dzs = [dzq_r, dzk_r, dzv_r, dzrg, dzq_f, dzk_f, dzv_f, dzfg]
    gwt = _dw_in(dzs, dzf, ut)
    p_in = [_swap_add_windows(gwt)[1]] if reduce_scatter else []
    gh, dng, *e_in = _du_norm_bwd(dzs, dzf, wt, wft, hpad, norm_g, dopad, p_in)
    return (loss8[0, 0], gh[C:], gh[PAD:C], dng, gwt, dbf[:, :NFF], dwout, dfg, p_in + p_out, e_in + e_out)


WOFF, WLEN = 1792, 2048
WHALF = WLEN // 2
LAP = WPADROWS - WOFF


def _own_window(w3):
    rows, sub, lanes = w3.shape
    pad = WPADROWS - rows
    tb = 96
    nb = WPADROWS // tb
    half = rows // 2

    def body(w_ref, o_ref, buf, sems):
        x, y, _ = _place()
        shift = 4 * (2 * x + y)
        buf[pl.ds(0, pad)] = jnp.zeros((pad, sub, lanes), F32)
        buf[pl.ds(rows, pad)] = jnp.zeros((pad, sub, lanes), F32)
        cps = [pltpu.make_async_copy(w_ref.at[pl.ds(half * h, half)], buf.at[pl.ds(shift + half * h, half)],
                                     sems.at[h]) for h in range(2)]
        for cp in cps:
            cp.start()

        def block(i, carry):
            r0 = pl.multiple_of(i * tb, tb)
            o_ref[pl.ds(r0, tb), :] = buf[pl.ds(r0, tb)].reshape(tb, sub * lanes).astype(BF)
            return carry

        cps[0].wait()
        lax.fori_loop(0, half // tb, block, 0)
        cps[1].wait()
        lax.fori_loop(half // tb, nb, block, 0)

    return pl.pallas_call(
        body, name="own_window",
        in_specs=[ANY], out_shape=jax.ShapeDtypeStruct((WPADROWS, sub * lanes), BF),
        scratch_shapes=[pltpu.VMEM((WPADROWS, sub, lanes), F32), pltpu.SemaphoreType.DMA((2,))],
        compiler_params=pltpu.CompilerParams(vmem_limit_bytes=VMEM_LIMIT),
    )(w3)


def _gather_weights(own_win, meta, x, norm_g):
    half_main, half_lap, half_meta = WOFF // 2, LAP // 2, meta.shape[0] // 2
    last = NCH - 1

    def body(win_ref, meta_ref, x_ref, g_ref, w_ref, laps_ref, gm_ref, h_ref, u_ref, ut_ref,
             send_sems, recv_sems, local_sems, stage, lapbuf, headbuf, metabuf):
        step = pl.program_id(0)
        x, y, c = _place()
        me_s = 2 * x + y
        sib = (x, y, 1 - c)
        chips = _other_chips(x, y)

        def emit(h):
            u = _norm_rows(h, g_ref[...])
            h_ref[...] = h
            u_ref[...] = u.astype(BF)
            ut_ref[...] = u.T.astype(BF)

        kinds = [
            (lambda h: win_ref.at[pl.ds(half_main * h, half_main)],
             lambda s, h: w_ref.at[pl.ds(WOFF * s + half_main * h, half_main)]),
            (lambda h: win_ref.at[pl.ds(WOFF + half_lap * h, half_lap)],
             lambda s, h: laps_ref.at[s, pl.ds(half_lap * h, half_lap)]),
            (lambda h: meta_ref.at[pl.ds(half_meta * h, half_meta)],
             lambda s, h: gm_ref.at[s, pl.ds(half_meta * h, half_meta)]),
        ]
        own_in = pltpu.make_async_copy(win_ref.at[pl.ds(0, WOFF)], stage, local_sems.at[0])
        own_lap_in = pltpu.make_async_copy(win_ref.at[pl.ds(WOFF, LAP)], lapbuf.at[0], local_sems.at[1])
        own_out = pltpu.make_async_copy(stage, w_ref.at[pl.ds(WOFF * me_s, WOFF)], local_sems.at[0])
        own_lap_out = pltpu.make_async_copy(lapbuf.at[0], laps_ref.at[me_s], local_sems.at[1])
        sends, arrivals, forwards, forwarded = [], [], [], []
        for a, (src, dst) in enumerate(kinds):
            for k, (cx, cy, cs) in enumerate(chips):
                there = dict(send_sem=send_sems.at[6 * a + k], recv_sem=recv_sems.at[6 * a + k],
                             device_id=(cx, cy, c), device_id_type=MESH)
                across = dict(send_sem=send_sems.at[6 * a + 3 + k], recv_sem=recv_sems.at[6 * a + 3 + k],
                              device_id=sib, device_id_type=MESH)
                sends.append(pltpu.make_async_remote_copy(src_ref=src(c), dst_ref=dst(me_s, c), **there))
                arrivals.append(pltpu.make_async_remote_copy(src_ref=dst(cs, c), dst_ref=dst(cs, c), **there))
                forwards.append(pltpu.make_async_remote_copy(src_ref=dst(cs, c), dst_ref=dst(cs, c), **across))
                forwarded.append(pltpu.make_async_remote_copy(
                    src_ref=dst(cs, 1 - c), dst_ref=dst(cs, 1 - c), **across))

        @pl.when(step == 0)
        def _():
            own_in.start()
            own_lap_in.start()
            for cp in sends:
                cp.start()
            own_in.wait()
            own_out.start()
            own_lap_in.wait()
            own_lap_out.start()

        @pl.when(step < last)
        def _():
            emit(x_ref[...])

        @pl.when(step == last)
        def _():
            for cp, fwd in zip(arrivals, forwards):
                cp.wait_recv()
                fwd.start()
            for cp in forwarded:
                cp.wait_recv()
            for cp in sends + forwards:
                cp.wait_send()
            own_out.wait()
            own_lap_out.wait()
            for s in range(1, 4):
                head = w_ref.at[pl.ds(WOFF * s, LAP)]
                loads = [pltpu.make_async_copy(laps_ref.at[s - 1], lapbuf.at[1], local_sems.at[2]),
                         pltpu.make_async_copy(head, headbuf, local_sems.at[3])]
                for cp in loads:
                    cp.start()
                for cp in loads:
                    cp.wait()
                headbuf[...] = (headbuf[...].astype(F32) + lapbuf[1].astype(F32)).astype(BF)
                store = pltpu.make_async_copy(headbuf, head, local_sems.at[3])
                store.start()
                store.wait()
            loads = [pltpu.make_async_copy(meta_ref, metabuf.at[me_s], local_sems.at[0])]
            loads += [pltpu.make_async_copy(gm_ref.at[cs], metabuf.at[cs], local_sems.at[1 + k])
                      for k, (_, _, cs) in enumerate(chips)]
            for cp in loads:
                cp.start()
            for cp in loads:
                cp.wait()
            tokens = jnp.concatenate([metabuf[s] for s in range(4)], axis=1)
            emit(jnp.concatenate([jnp.zeros((PAD, D), F32), tokens], axis=0))

    def chunk(i):
        return (i + 1) % NCH

    return pl.pallas_call(
        body, name="all_gather_w", grid=(NCH,),
        in_specs=[ANY, ANY, pl.BlockSpec((C, D), lambda i: (jnp.minimum(i, last - 1), 0)),
                  pl.BlockSpec((1, D), lambda i: (0, 0))],
        out_specs=[ANY] * 3 + [pl.BlockSpec((C, D), lambda i: (chunk(i), 0))] * 2
        + [pl.BlockSpec((D, C), lambda i: (0, chunk(i)))],
        out_shape=[jax.ShapeDtypeStruct((WMAIN, D), own_win.dtype), jax.ShapeDtypeStruct((4, LAP, D), own_win.dtype),
                   jax.ShapeDtypeStruct((4,) + meta.shape, meta.dtype),
                   jax.ShapeDtypeStruct((T, D), F32), jax.ShapeDtypeStruct((T, D), BF),
                   jax.ShapeDtypeStruct((D, T), BF)],
        scratch_shapes=[pltpu.SemaphoreType.DMA((18,)), pltpu.SemaphoreType.DMA((18,)), pltpu.SemaphoreType.DMA((4,)),
                        pltpu.VMEM((WOFF, D), own_win.dtype), pltpu.VMEM((2, LAP, D), own_win.dtype),
                        pltpu.VMEM((LAP, D), own_win.dtype), pltpu.VMEM((4,) + meta.shape, meta.dtype)],
        compiler_params=_params(("arbitrary",)),
    )(own_win, meta, x, norm_g)


def _pair_copies(ins, outs, send_sems, recv_sems, n):
    x, y, c = _place()
    sib = dict(device_id=(x, y, 1 - c), device_id_type=MESH)
    cps = []
    for a in range(n):
        rows = ins[a].shape[1] // 2
        cps.append(pltpu.make_async_remote_copy(
            src_ref=ins[a].at[:, pl.ds((1 - c) * rows, rows)], dst_ref=outs[a],
            send_sem=send_sems.at[a], recv_sem=recv_sems.at[a], **sib))
    for k in range(4 * (len(ins) - n)):
        cps.append(pltpu.make_async_remote_copy(
            src_ref=ins[n].at[pl.ds(WOFF * k + (1 - c) * WHALF, WHALF)], dst_ref=outs[n].at[k],
            send_sem=send_sems.at[n + k], recv_sem=recv_sems.at[n + k], **sib))
    return cps


def _swap_add_windows(gwt):
    nchunk = 4
    rows = WHALF // nchunk

    def body(gw_ref, land_ref, out_ref, send_sems, recv_sems, local_sems, own, theirs):
        _, _, c = _place()
        swaps = _pair_copies([gw_ref], [land_ref], send_sems, recv_sems, 0)
        loads = [pltpu.make_async_copy(gw_ref.at[pl.ds(WOFF * k + c * WHALF, WHALF)], own.at[k], local_sems.at[k])
                 for k in range(4)]
        stores = [pltpu.make_async_copy(own.at[k], out_ref.at[k], local_sems.at[k]) for k in range(4)]
        for cp in loads:
            cp.start()
        swaps[0].start()
        for k in range(4):
            swaps[k].wait_send()
            if k + 1 < 4:
                swaps[k + 1].start()
            swaps[k].wait_recv()
            fetch = pltpu.make_async_copy(land_ref.at[k], theirs, local_sems.at[4])
            fetch.start()
            loads[k].wait()
            fetch.wait()

            def add(i, carry, k=k):
                r = _rows(i, rows)
                own[k, r, :] = (own[k, r, :].astype(F32) + theirs[r, :].astype(F32)).astype(BF)
                return carry

            lax.fori_loop(0, nchunk, add, 0)
            stores[k].start()
        for cp in stores:
            cp.wait()

    return pl.pallas_call(
        body, name="rs_pair_swap_add",
        in_specs=[ANY], out_specs=[ANY, ANY],
        out_shape=[jax.ShapeDtypeStruct((4, WHALF, D), gwt.dtype), jax.ShapeDtypeStruct((4, WHALF, D), BF)],
        scratch_shapes=[pltpu.SemaphoreType.DMA((4,)), pltpu.SemaphoreType.DMA((4,)), pltpu.SemaphoreType.DMA((5,)),
                        pltpu.VMEM((4, WHALF, D), gwt.dtype), pltpu.VMEM((WHALF, D), gwt.dtype)],
        compiler_params=pltpu.CompilerParams(vmem_limit_bytes=VMEM_LIMIT),
    )(gwt)


def _chip_exchange(parts, small):
    n = len(parts)

    def body(*refs):
        ins, sm = refs[:n], refs[n]
        outs, smo = refs[n + 1:2 * n + 1], refs[2 * n + 1]
        send_sems, recv_sems = refs[2 * n + 2:]
        cps = _chip_copies(ins, outs, send_sems, recv_sems, by_dest=True)
        cps += _chip_copies([sm], [smo], send_sems.at[pl.ds(3 * n, 3)], recv_sems.at[pl.ds(3 * n, 3)], by_dest=False)
        for cp in cps:
            cp.start()
        for cp in cps:
            cp.wait()

    return pl.pallas_call(
        body, name="rs_chip_exchange",
        in_specs=[ANY] * (n + 1), out_specs=[ANY] * (n + 1),
        out_shape=[jax.ShapeDtypeStruct(p.shape, p.dtype) for p in parts]
        + [jax.ShapeDtypeStruct((4,) + small.shape, small.dtype)],
        scratch_shapes=[pltpu.SemaphoreType.DMA((3 * (n + 1),)), pltpu.SemaphoreType.DMA((3 * (n + 1),))],
    )(*parts, small)


def _pair_send(halves):
    n = len(halves)

    def body(*refs):
        ins, outs = refs[:n], refs[n:2 * n]
        send_sems, recv_sems = refs[2 * n:]
        x, y, c = _place()
        cps = [pltpu.make_async_remote_copy(
            src_ref=ins[a], dst_ref=outs[a], send_sem=send_sems.at[a], recv_sem=recv_sems.at[a],
            device_id=(x, y, 1 - c), device_id_type=MESH) for a in range(n)]
        for cp in cps:
            cp.start()
        for cp in cps:
            cp.wait()

    return pl.pallas_call(
        body, name="rs_pair_send",
        in_specs=[ANY] * n, out_specs=[ANY] * n,
        out_shape=[jax.ShapeDtypeStruct(h.shape, h.dtype) for h in halves],
        scratch_shapes=[pltpu.SemaphoreType.DMA((n,)), pltpu.SemaphoreType.DMA((n,))],
    )(*halves)


def _row_block(rows):
    for tb in (256, 128, 64, 32, 16, 8):
        if rows % tb == 0:
            return tb
    return rows


def _add_halves(full, recv, name, out_dtype):
    _, r2, w = recv.shape
    tb = _row_block(r2)
    nb = r2 // tb
    c = lax.axis_index("c")

    def body(c_ref, a_ref, b_ref, o_ref):
        o_ref[...] = (a_ref[...].astype(F32) + b_ref[...].astype(F32)).astype(o_ref.dtype)

    return pl.pallas_call(
        body, name=name,
        grid_spec=pltpu.PrefetchScalarGridSpec(
            num_scalar_prefetch=1, grid=(4, nb),
            in_specs=[pl.BlockSpec((1, tb, w), lambda s, i, cr: (s, cr[0] * nb + i, 0)),
                      pl.BlockSpec((1, tb, w), lambda s, i, cr: (s, i, 0))],
            out_specs=pl.BlockSpec((1, tb, w), lambda s, i, cr: (s, i, 0))),
        out_shape=jax.ShapeDtypeStruct(recv.shape, out_dtype),
        compiler_params=_params(("parallel", "parallel")),
    )(jnp.reshape(c, (1,)).astype(jnp.int32), full, recv)


def _add2(a, b, name):
    def body(a_ref, b_ref, o_ref):
        o_ref[...] = a_ref[...] + b_ref[...]

    return pl.pallas_call(body, name=name, out_shape=jax.ShapeDtypeStruct(a.shape, a.dtype))(a, b)


def _sum4(buf, own, name):
    _, r, w = buf.shape
    tb = _row_block(r)
    me_s = 2 * lax.axis_index("x") + lax.axis_index("y")
    by_dest = own.ndim == 3

    def body(s_ref, b_ref, own_ref, o_ref):
        mine = (own_ref[0] if by_dest else own_ref[...]).astype(F32)
        terms = [jnp.where(s_ref[0] == t, mine, b_ref[t].astype(F32)) for t in range(4)]
        o_ref[...] = ((terms[0] + terms[1]) + terms[2]) + terms[3]

    own_spec = (pl.BlockSpec((1, tb, w), lambda i, sr: (sr[0], i, 0)) if by_dest
                else pl.BlockSpec((tb, w), lambda i, sr: (i, 0)))
    return pl.pallas_call(
        body, name=name,
        grid_spec=pltpu.PrefetchScalarGridSpec(
            num_scalar_prefetch=1, grid=(r // tb,),
            in_specs=[pl.BlockSpec((4, tb, w), lambda i, sr: (0, i, 0)), own_spec],
            out_specs=pl.BlockSpec((tb, w), lambda i, sr: (i, 0))),
        out_shape=jax.ShapeDtypeStruct((r, w), F32),
        compiler_params=_params(("parallel",)),
    )(jnp.reshape(me_s, (1,)).astype(jnp.int32), buf, own)


def _adamw_math(w, g, m, v):
    mn = B1 * m + (1.0 - B1) * g
    vn = B2 * v + (1.0 - B2) * (g * g)
    m_hat = mn / (1.0 - B1 ** STEP)
    v_hat = vn / (1.0 - B2 ** STEP)
    return -LR * (m_hat / (jnp.sqrt(v_hat) + AEPS) + WD * w), mn, vn


def _adamw(w, g, m, v, name):
    r, c_ = w.shape
    tb = _row_block(r)
    if tb == r and r > 512:
        tb = 256

    def body(w_ref, g_ref, m_ref, v_ref, d_ref, mo_ref, vo_ref):
        d_ref[...], mo_ref[...], vo_ref[...] = _adamw_math(w_ref[...], g_ref[...], m_ref[...], v_ref[...])

    spec = pl.BlockSpec((tb, c_), lambda i: (i, 0))
    return pl.pallas_call(
        body, name=name, grid=(pl.cdiv(r, tb),),
        in_specs=[spec] * 4, out_specs=[spec] * 3,
        out_shape=[jax.ShapeDtypeStruct(w.shape, F32)] * 3,
        compiler_params=_params(("parallel",)),
    )(w, g, m, v)


def _adamw_rows(w, g_mine, g_sib, m, v, name):
    r = w.shape[0]
    tb = 256
    sub, lanes = w.shape[1:]
    nh = g_mine.shape[0] // tb
    nsteps = pl.cdiv(r, tb)
    assert nsteps <= 2 * nh and 4 * 3 + r <= 2 * nh * tb
    x, y, c = _place()
    place = jnp.stack([c, 4 * (2 * x + y)]).astype(jnp.int32)

    def body(p_ref, w_ref, mc_ref, sc_ref, mn_ref, sn_ref, m_ref, v_ref, go_ref, d_ref, mo_ref, vo_ref, buf):
        i = pl.program_id(0)
        for at, blk, mine_ref, sib_ref in ((0, i, mc_ref, sc_ref), (1, jnp.minimum(i + 1, 2 * nh - 1), mn_ref, sn_ref)):
            rows = jnp.where(blk // nh == p_ref[0], mine_ref[...], sib_ref[...])
            buf[tb * at:tb * (at + 1)] = rows.reshape(tb, sub, lanes)
        g = buf[pl.ds(p_ref[1], tb)]
        go_ref[...] = g
        d_ref[...], mo_ref[...], vo_ref[...] = _adamw_math(w_ref[...], g, m_ref[...], v_ref[...])

    def half_spec(ahead, sibling):
        def index(i, pr):
            half = (1 - pr[0]) if sibling else pr[0]
            return (jnp.clip(jnp.minimum(i + ahead, 2 * nh - 1) - nh * half, 0, nh - 1), 0)
        return pl.BlockSpec((tb, sub * lanes), index)

    spec = pl.BlockSpec((tb, sub, lanes), lambda i, pr: (i, 0, 0))
    return pl.pallas_call(
        body, name=name,
        grid_spec=pltpu.PrefetchScalarGridSpec(
            num_scalar_prefetch=1, grid=(nsteps,),
            in_specs=[spec, half_spec(0, False), half_spec(0, True), half_spec(1, False), half_spec(1, True),
                      spec, spec],
            out_specs=[spec] * 4,
            scratch_shapes=[pltpu.VMEM((2 * tb, sub, lanes), F32)]),
        out_shape=[jax.ShapeDtypeStruct(w.shape, F32)] * 4,
        compiler_params=_params(("parallel",)),
    )(place, w, g_mine, g_sib, g_mine, g_sib, m, v)


def _adamw_halves(w, g_mine, g_sib, m, v, name):
    r, c_ = w.shape
    r2 = g_mine.shape[0]
    tb = _row_block(r2)
    nb = r2 // tb
    c = lax.axis_index("c")

    def body(c_ref, w_ref, gm_ref, gs_ref, m_ref, v_ref, g_ref, d_ref, mo_ref, vo_ref):
        g = jnp.where(pl.program_id(0) == c_ref[0], gm_ref[...], gs_ref[...])
        g_ref[...] = g
        d_ref[...], mo_ref[...], vo_ref[...] = _adamw_math(w_ref[...], g, m_ref[...], v_ref[...])

    full = pl.BlockSpec((tb, c_), lambda h, i, cr: (h * nb + i, 0))
    half = pl.BlockSpec((tb, c_), lambda h, i, cr: (i, 0))
    return pl.pallas_call(
        body, name=name,
        grid_spec=pltpu.PrefetchScalarGridSpec(
            num_scalar_prefetch=1, grid=(2, nb),
            in_specs=[full, half, half, full, full], out_specs=[full] * 4),
        out_shape=[jax.ShapeDtypeStruct(w.shape, F32)] * 4,
        compiler_params=_params(("parallel", "parallel")),
    )(jnp.reshape(c, (1,)).astype(jnp.int32), w, g_mine, g_sib, m, v)


def kernel(x, meta_tokens, norm_g, w_in, b_f, w_out, final_g, loss_target, m_meta_tokens, m_norm_g, m_w_in, m_b_f, m_w_out, m_final_g, v_meta_tokens, v_norm_g, v_w_in, v_b_f, v_w_out, v_final_g):
    me_s = 2 * lax.axis_index("x") + lax.axis_index("y")
    w3, m3, v3 = [jnp.transpose(jnp.reshape(t[0], (D // C, C, WSH)), (2, 0, 1)) for t in (w_in, m_w_in, v_w_in)]

    wt_main, laps, _, *normed = _gather_weights(_own_window(w3), meta_tokens, x[0], norm_g)
    wft = jnp.pad(laps[3, :NFF], ((0, C - NFF), (0, 0)))
    mine = (jnp.arange(4) == me_s)[:, None, None]
    wout_own = w_out[0].astype(BF)

    def wout_full(landed):
        return jnp.where(mine, wout_own[None], landed).reshape(DMIX, D)

    loss, gx, dmeta, dng, gwt, dbf, dwout, dfg, (p_in, p_out), (e_in, e_out) = _local_step(
        x[0], loss_target[0], normed, norm_g, wt_main, wft, b_f, wout_own, final_g.reshape(1, D), True, wout_full)

    g_meta = jnp.stack([dmeta[:, 256 * s:256 * (s + 1)] for s in range(4)])
    small = jnp.concatenate([dng, dfg, jnp.pad(dbf, ((0, 0), (0, D - NFF))),
                             jnp.pad(jnp.reshape(loss, (1, 1)), ((0, 0), (0, D - 1))),
                             jnp.zeros((4, D), F32)], axis=0)
    e_meta, e_small = _chip_exchange([g_meta], small)
    h_in, h_out = _sum4(e_in, p_in, "sum_in"), _sum4(e_out, p_out, "sum_out")
    h_meta, h_small = _sum4(e_meta, g_meta, "sum_meta"), _sum4(e_small, small, "sum_small")
    s_in, s_out, s_meta, s_small = _pair_send([h_in, h_out, h_meta, h_small])
    gw_meta = _add2(h_meta, s_meta, "pair_add_meta")
    tot = _add2(h_small, s_small, "pair_add_small")
    g_norm, g_final, g_bf, loss_all = tot[0:1], tot[1], tot[2:3, :NFF], tot[3, 0]

    d_meta, nm_meta, nv_meta = _adamw(meta_tokens, gw_meta, m_meta_tokens, v_meta_tokens, "adamw_meta")
    d_norm, nm_norm, nv_norm = _adamw(norm_g, g_norm, m_norm_g, v_norm_g, "adamw_norm")
    outs_in = _adamw_rows(w3, h_in, s_in, m3, v3, "adamw_in")
    gw_in, d_in, nm_in, nv_in = [jnp.reshape(jnp.transpose(t, (1, 2, 0)), (1, D, WSH)) for t in outs_in]
    d_bf, nm_bf, nv_bf = _adamw(b_f, g_bf, m_b_f, v_b_f, "adamw_bf")
    gw_out, d_out, nm_out, nv_out = _adamw_halves(w_out[0], h_out, s_out, m_w_out[0], v_w_out[0], "adamw_out")
    d_fin, nm_fin, nv_fin = _adamw(final_g.reshape(1, D), g_final.reshape(1, D), m_final_g.reshape(1, D),
                                   v_final_g.reshape(1, D), "adamw_final")
    return (loss_all, gx[None], gw_meta, g_norm, gw_in, g_bf, gw_out[None], g_final,
            d_meta, d_norm, d_in, d_bf, d_out[None], d_fin.reshape(D),
            nm_meta, nm_norm, nm_in, nm_bf, nm_out[None], nm_fin.reshape(D),
            nv_meta, nv_norm, nv_in, nv_bf, nv_out[None], nv_fin.reshape(D))
```

```python
import numpy as np
import jax
import jax.numpy as jnp
from jax import lax
from jax.experimental import pallas as pl
from jax.experimental.pallas import tpu as pltpu

D = 1024
SEQ = 2048
NMETA = 16
C = 128
PAD = C - NMETA
T = PAD + NMETA + SEQ
NCH = T // C
RH, RDK, RDV = 4, 128, 256
FH, FD = 16, 64
NPAIR = FH // 2
WMAIN = 7168
NFF = 16
WIN = WMAIN + NFF
WSH = WIN // 4
WPADROWS = 1824
DMIX = 2048
EPS = 1e-6
NEG = -1e30
RSCALE = RDK ** -0.5
FSCALE = FD ** -0.5
ROPE_BASE = 10000.0
LR, B1, B2, AEPS, WD, STEP = 0.001, 0.9, 0.999, 1e-08, 0.01, 10

BF = jnp.bfloat16
F32 = jnp.float32
NT = (((1,), (1,)), ((), ()))
TN = (((0,), (0,)), ((), ()))
NN_DIMS = (((1,), (0,)), ((), ()))
MESH = pl.DeviceIdType.MESH
ANY = pl.BlockSpec(memory_space=pl.ANY)
VMEM_LIMIT = 48 * 1024 * 1024
DW_VMEM_LIMIT = 56 * 1024 * 1024

GB_R, GB_F = 2, 6
QB_F, KB_F, VB_F = 24, 32, 40


def _dot(a, b):
    return jnp.dot(a, b, preferred_element_type=F32)


def _dg(a, b, dims):
    return lax.dot_general(a, b, dims, preferred_element_type=F32)


def _params(sem=None):
    return pltpu.CompilerParams(dimension_semantics=sem, vmem_limit_bytes=VMEM_LIMIT)


def _constants():
    pos = jnp.arange(T, dtype=F32) - PAD
    inv = ROPE_BASE ** (-jnp.arange(0, RDK, 2, dtype=F32) / RDK)
    ang = pos[:, None] * inv[None, :]
    cos, sin = jnp.cos(ang), jnp.sin(ang)
    cos2 = jnp.concatenate([cos, cos], axis=1)
    sin2 = jnp.concatenate([-sin, sin], axis=1)
    log_gamma = jnp.log1p(-jnp.exp2(-5.0 - jnp.arange(RH, dtype=F32)))
    idx = jnp.arange(C, dtype=F32)
    diff = idx[:, None] - idx[None, :]
    dmask = jnp.where(diff[None] >= 0, jnp.exp(log_gamma[:, None, None] * jnp.maximum(diff, 0.0)[None]), 0.0)
    zeta = jnp.exp(log_gamma[:, None] * (C - 1.0 - idx)[None, :])
    xi = jnp.exp(log_gamma[:, None] * (idx + 1.0)[None, :])
    gdec = jnp.exp(log_gamma * C)
    zeta_b = jnp.broadcast_to(zeta[:, :, None], (RH, C, RDK))
    xi_b = jnp.broadcast_to(xi[:, :, None], (RH, C, RDK))
    gdec_b = jnp.broadcast_to(gdec[:, None, None], (RH, RDK, RDV))
    tri = jnp.asarray(np.tril(np.ones((C, C), np.float32)), dtype=BF)
    head_of_lane = np.arange(FH * FD) // FD
    pick = ((np.arange(FH * FD)[:, None] % FD == 0)
            & (head_of_lane[:, None] == np.arange(C)[None, :])).astype(np.float32)
    seg = (np.arange(C)[:, None] // FD == np.arange(C)[None, :] // FD).astype(np.float32)
    ones_aug = np.concatenate([np.tile((np.arange(C) < FD)[None, :], (C, 1)),
                               np.tile((np.arange(C) >= FD)[None, :], (C, 1))], axis=0).astype(np.float32)
    lane = np.arange(2 * C) % C
    causal = np.where(lane[None, :] <= np.arange(C)[:, None], 0.0, NEG).astype(np.float32)
    mask_bias = np.stack([np.zeros((C, 2 * C), np.float32), causal])
    return dict(cos2=cos2, sin2=sin2, dmask=dmask, zeta=zeta_b, xi=xi_b, gdec=gdec_b, tri=tri,
                mask_bias=jnp.asarray(mask_bias), pick=jnp.asarray(pick, dtype=BF), seg=jnp.asarray(seg, dtype=BF),
                ones_aug=jnp.asarray(ones_aug, dtype=BF))


def _norm_rows(h, g):
    return h * lax.rsqrt(jnp.mean(h * h, axis=1, keepdims=True) + EPS) * g


def _mm_nt(a, b, n, tm, tn, name):
    m, k = a.shape

    def body(a_ref, b_ref, o_ref):
        o_ref[...] = _dg(a_ref[...], b_ref[...], NT)

    return pl.pallas_call(
        body, name=name, grid=(m // tm, n // tn),
        in_specs=[pl.BlockSpec((tm, k), lambda i, j: (i, 0)), pl.BlockSpec((tn, k), lambda i, j: (j, 0))],
        out_specs=pl.BlockSpec((tm, tn), lambda i, j: (i, j)),
        out_shape=jax.ShapeDtypeStruct((m, n), F32),
        compiler_params=_params(("parallel", "parallel")),
    )(a, b)


def _mm_nn(a, b, tm, tn, name, out_dtype=F32):
    m, k = a.shape
    _, n = b.shape

    def body(a_ref, b_ref, o_ref):
        o_ref[...] = _dot(a_ref[...], b_ref[...]).astype(out_dtype)

    return pl.pallas_call(
        body, name=name, grid=(m // tm, n // tn),
        in_specs=[pl.BlockSpec((tm, k), lambda i, j: (i, 0)), pl.BlockSpec((k, tn), lambda i, j: (0, j))],
        out_specs=pl.BlockSpec((tm, tn), lambda i, j: (i, j)),
        out_shape=jax.ShapeDtypeStruct((m, n), out_dtype),
        compiler_params=_params(("parallel", "parallel")),
    )(a, b)


def _rot(x, cos2, sin2):
    return x * cos2 + pltpu.roll(x, 64, 1) * sin2


def _ret_specs(chunk):
    whole = lambda shape: pl.BlockSpec(shape, lambda n: (0,) * len(shape))
    return [
        pl.BlockSpec((C, RH * RDK), lambda n: (chunk(n), 0)),
        pl.BlockSpec((C, RH * RDK), lambda n: (chunk(n), 1)),
        pl.BlockSpec((C, RH * RDV), lambda n: (chunk(n), 1)),
        pl.BlockSpec((C, RDK), lambda n: (chunk(n), 0)),
        pl.BlockSpec((C, RDK), lambda n: (chunk(n), 0)),
        whole((RH, C, C)), whole((RH, C, RDK)), whole((RH, C, RDK)), whole((RH, RDK, RDV)),
    ]


def _ret_heads(q_ref, k_ref, v_ref, cos, sin):
    qr = [_rot(q_ref[:, RDK * h:RDK * (h + 1)], cos, sin) for h in range(RH)]
    kr = [_rot(k_ref[:, RDK * h:RDK * (h + 1)], cos, sin) * RSCALE for h in range(RH)]
    vb = [v_ref[:, RDV * h:RDV * (h + 1)].astype(BF) for h in range(RH)]
    return qr, kr, [t.astype(BF) for t in qr], [t.astype(BF) for t in kr], vb


def _ret_fwd(z, cst):
    def body(q_ref, k_ref, v_ref, cos_ref, sin_ref, dm_ref, xi_ref, zt_ref, gd_ref, r_ref, sp_ref, st):
        n = pl.program_id(0)

        @pl.when(n == 0)
        def _():
            st[...] = jnp.zeros_like(st)

        hs = range(RH)
        qr, kr, qb, kb, vb = _ret_heads(q_ref, k_ref, v_ref, cos_ref[...], sin_ref[...])
        sd = [(_dg(qb[h], kb[h], NT) * dm_ref[h]).astype(BF) for h in hs]
        state = [st[h] for h in hs]
        qx = [(qr[h] * xi_ref[h]).astype(BF) for h in hs]
        kz = [(kr[h] * zt_ref[h]).astype(BF) for h in hs]
        out = [_dot(sd[h], vb[h]) + _dot(qx[h], state[h].astype(BF)) for h in hs]
        kv = [_dg(kz[h], vb[h], TN) for h in hs]
        for h in hs:
            sp_ref[0, h] = state[h]
            r_ref[:, RDV * h:RDV * (h + 1)] = out[h]
            st[h] = state[h] * gd_ref[h] + kv[h]

    return pl.pallas_call(
        body, name="ret_fwd", grid=(NCH,),
        in_specs=_ret_specs(lambda n: n),
        out_specs=[pl.BlockSpec((C, RH * RDV), lambda n: (n, 0)),
                   pl.BlockSpec((1, RH, RDK, RDV), lambda n: (n, 0, 0, 0))],
        out_shape=[jax.ShapeDtypeStruct((T, RH * RDV), F32), jax.ShapeDtypeStruct((NCH, RH, RDK, RDV), F32)],
        scratch_shapes=[pltpu.VMEM((RH, RDK, RDV), F32)],
        compiler_params=_params(("arbitrary",)),
    )(z, z, z, cst["cos2"], cst["sin2"], cst["dmask"], cst["xi"], cst["zeta"], cst["gdec"])


def _ret_bwd(z, cst, sprev, dr):
    def body(q_ref, k_ref, v_ref, cos_ref, sin_ref, dm_ref, xi_ref, zt_ref, gd_ref, sp_ref, dr_ref,
             dq_ref, dk_ref, dv_ref, gst):
        i = pl.program_id(0)

        @pl.when(i == 0)
        def _():
            gst[...] = jnp.zeros_like(gst)

        hs = range(RH)
        cos, sin = cos_ref[...], sin_ref[...]
        qr, kr, qb, kb, vb = _ret_heads(q_ref, k_ref, v_ref, cos, sin)
        dm = [dm_ref[h] for h in hs]
        xi = [xi_ref[h] for h in hs]
        zt = [zt_ref[h] for h in hs]
        sd = [(_dg(qb[h], kb[h], NT) * dm[h]).astype(BF) for h in hs]
        qx = [(qr[h] * xi[h]).astype(BF) for h in hs]
        kz = [(kr[h] * zt[h]).astype(BF) for h in hs]
        drb = [dr_ref[:, RDV * h:RDV * (h + 1)] for h in hs]
        sb = [sp_ref[0, h].astype(BF) for h in hs]
        g = [gst[h] for h in hs]
        gb = [t.astype(BF) for t in g]
        ds = [(_dg(drb[h], vb[h], NT) * dm[h]).astype(BF) for h in hs]
        dq = [_dot(ds[h], kb[h]) + _dg(drb[h], sb[h], NT) * xi[h] for h in hs]
        dk = [(_dg(ds[h], qb[h], TN) + _dg(vb[h], gb[h], NT) * zt[h]) * RSCALE for h in hs]
        dv = [_dg(sd[h], drb[h], TN) + _dot(kz[h], gb[h]) for h in hs]
        gn = [g[h] * gd_ref[h] + _dg(qx[h], drb[h], TN) for h in hs]
        for h in hs:
            gst[h] = gn[h]
            dq_ref[:, RDK * h:RDK * (h + 1)] = (dq[h] * cos + pltpu.roll(dq[h] * sin, 64, 1)).astype(BF)
            dk_ref[:, RDK * h:RDK * (h + 1)] = (dk[h] * cos + pltpu.roll(dk[h] * sin, 64, 1)).astype(BF)
            dv_ref[:, RDV * h:RDV * (h + 1)] = dv[h].astype(BF)

    rev = lambda n: NCH - 1 - n
    return pl.pallas_call(
        body, name="ret_bwd", grid=(NCH,),
        in_specs=_ret_specs(rev) + [
            pl.BlockSpec((1, RH, RDK, RDV), lambda n: (rev(n), 0, 0, 0)),
            pl.BlockSpec((C, RH * RDV), lambda n: (rev(n), 0)),
        ],
        out_specs=[pl.BlockSpec((C, RH * RDK), lambda n: (rev(n), 0)),
                   pl.BlockSpec((C, RH * RDK), lambda n: (rev(n), 0)),
                   pl.BlockSpec((C, RH * RDV), lambda n: (rev(n), 0))],
        out_shape=[jax.ShapeDtypeStruct((T, RH * RDK), BF), jax.ShapeDtypeStruct((T, RH * RDK), BF),
                   jax.ShapeDtypeStruct((T, RH * RDV), BF)],
        scratch_shapes=[pltpu.VMEM((RH, RDK, RDV), F32)],
        compiler_params=_params(("arbitrary",)),
    )(z, z, z, cst["cos2"], cst["sin2"], cst["dmask"], cst["xi"], cst["zeta"], cst["gdec"], sprev, dr)


def _place():
    x, y, c = lax.axis_index("x"), lax.axis_index("y"), lax.axis_index("c")
    return x, y, c


def _other_chips(x, y):
    return [(1 - x, y, 2 * (1 - x) + y), (x, 1 - y, 2 * x + (1 - y)), (1 - x, 1 - y, 2 * (1 - x) + (1 - y))]


def _chip_copies(srcs, lands, send_sems, recv_sems, by_dest):
    x, y, c = _place()
    me_s = 2 * x + y
    return [pltpu.make_async_remote_copy(
        src_ref=src.at[cs] if by_dest else src, dst_ref=land.at[me_s],
        send_sem=send_sems.at[3 * a + j], recv_sem=recv_sems.at[3 * a + j],
        device_id=(cx, cy, c), device_id_type=MESH)
        for a, (src, land) in enumerate(zip(srcs, lands)) for j, (cx, cy, cs) in enumerate(_other_chips(x, y))]


def _split_dot(x, mat01, dims=NN_DIMS, x_first=True):
    acc, rest = None, x
    for _ in range(3):
        piece = rest.astype(BF)
        part = _dg(piece, mat01, dims) if x_first else _dg(mat01, piece, dims)
        acc = part if acc is None else acc + part
        rest = rest - piece.astype(F32)
    return acc


def _log_sigmoid(x):
    return -(jnp.maximum(-x, 0.0) + jnp.log1p(jnp.exp(-jnp.abs(x))))


def _fox_prep(zf, bf_pad, cst):
    def body(zf_ref, b_ref, tri_ref, ct_ref, carry):
        n = pl.program_id(0)

        @pl.when(n == 0)
        def _():
            carry[...] = jnp.zeros_like(carry)

        ls = _log_sigmoid(zf_ref[...] + b_ref[...])
        row = n * C + lax.broadcasted_iota(jnp.int32, (C, C), 0)
        lf = jnp.where(row >= PAD, ls, 0.0)
        cc = _split_dot(lf, tri_ref[...], x_first=False) + carry[0:1, :]
        carry[...] = jnp.broadcast_to(cc[C - 1:C, :], carry.shape)
        pos = n * C + lax.broadcasted_iota(jnp.int32, (FH, C), 1)
        ct_ref[0] = jnp.where(pos >= PAD, cc.T[:FH, :], -NEG)

    return pl.pallas_call(
        body, name="fox_prep", grid=(NCH,),
        in_specs=[pl.BlockSpec((C, C), lambda n: (n, 0)), pl.BlockSpec((1, C), lambda n: (0, 0)),
                  pl.BlockSpec((C, C), lambda n: (0, 0))],
        out_specs=pl.BlockSpec((1, FH, C), lambda n: (n, 0, 0)),
        out_shape=jax.ShapeDtypeStruct((NCH, FH, C), F32),
        scratch_shapes=[pltpu.VMEM((8, C), F32)],
        compiler_params=_params(("arbitrary",)),
    )(zf, bf_pad, cst["tri"])


def _lo_lanes(shape):
    return lax.broadcasted_iota(jnp.int32, shape, 1) < FD


def _split_heads(x):
    lo = _lo_lanes(x.shape)
    zero = jnp.zeros_like(x)
    return jnp.concatenate([jnp.where(lo, x, zero), jnp.where(lo, zero, x)], axis=0)


def _spread2(x):
    lo = _lo_lanes(x.shape)
    r = pltpu.roll(x, FD, 1)
    return jnp.concatenate([jnp.where(lo, x, r), jnp.where(lo, r, x)], axis=1)


NSTEP = (NCH + 1) // 2
NTILE = NCH + 1
TROWS = T + C


def _fox_tile(s, t):
    second = t > s
    return second.astype(jnp.int32), jnp.where(second, t - s - 1, s - t)


def _fox_pos(i):
    return jnp.where(i < NSTEP, 2 * i, 2 * (NCH - 1 - i) + 1)


def _fox_pair_columns():
    return pl.BlockSpec((TROWS, C), lambda p, s: (0, p))


def _fox_key_bias(ct_ref, p, j):
    return jnp.concatenate([ct_ref[j, pl.ds(2 * p, 1), :], ct_ref[j, pl.ds(2 * p + 1, 1), :]], axis=1)


def _fox_columns(cols, sems, p):
    def copies(pair, slot):
        return [pltpu.make_async_copy(
            src.at[pl.ds(0, buf.shape[1]), pl.ds(pl.multiple_of((first + pair) * C, C), C)], buf.at[slot],
            sems.at[i, slot]) for i, (src, first, buf) in enumerate(cols)]

    @pl.when(p == 0)
    def _():
        for cp in copies(0, 0):
            cp.start()

    for cp in copies(p, p % 2):
        cp.wait()

    @pl.when(p + 1 < NPAIR)
    def _():
        for cp in copies(p + 1, 1 - p % 2):
            cp.start()


def _rows(block, size=C):
    return pl.ds(pl.multiple_of(block * size, size), size)


def _fox_fwd(z, ct, cst, share):
    n = 0 if share is None else 1

    def body(z_ref, ct_ref, ones_ref, mb_ref, *rest):
        share_refs, (a_ref, g_ref), land_refs = rest[:n], rest[n:n + 2], rest[n + 2:2 * n + 2]
        kks, vvs, q2, m2, sbuf, qbuf, kbuf, vbuf, col_sems = rest[2 * n + 2:2 * n + 11]
        p, s = pl.program_id(0), pl.program_id(1)
        slot = p % 2
        if n:
            copies = _chip_copies(share_refs, land_refs, *rest[2 * n + 11:], by_dest=False)

            @pl.when((p == 0) & (s == 0))
            def _():
                for cp in copies:
                    cp.start()

            @pl.when((p == NPAIR - 1) & (s == NSTEP - 1))
            def _():
                for cp in copies:
                    cp.wait()

        @pl.when(s == 0)
        def _():
            ones = ones_ref[...]
            _fox_columns([(z_ref, QB_F, qbuf), (z_ref, KB_F, kbuf), (z_ref, VB_F, vbuf)], col_sems, p)

            def prep(j, carry):
                kks[j] = _split_heads(kbuf[slot, _rows(j), :]).astype(BF)
                vvs[j] = jnp.concatenate([_split_heads(vbuf[slot, _rows(j), :]).astype(BF), ones], axis=1)
                return carry

            lax.fori_loop(0, NCH, prep, 0)

        q2[0] = (qbuf[slot, _rows(s), :] * FSCALE).astype(BF)
        q2[1] = (qbuf[slot, _rows(NCH - 1 - s), :] * FSCALE).astype(BF)

        tiles = [_fox_tile(s, t) for t in range(NTILE)]
        causal = mb_ref[1]
        neg = jnp.full((C, 2 * C), NEG, F32)
        run, first = neg, neg
        for t, (sel, j) in enumerate(tiles):
            st = _dg(q2[sel], kks[j], NT) - _fox_key_bias(ct_ref, p, j)
            if t in (0, NTILE - 1):
                st = st + causal
            sbuf[t] = st
            run = jnp.maximum(jnp.where(t == s + 1, neg, run), st)
            first = jnp.where(t == s, run, first)
        for w, mx in enumerate((first, run)):
            m2[w] = jnp.concatenate(
                [jnp.broadcast_to(jnp.max(mx[:, :C], axis=1, keepdims=True), (C, C)),
                 jnp.broadcast_to(jnp.max(mx[:, C:], axis=1, keepdims=True), (C, C))], axis=1)

        zero = jnp.zeros((C, 2 * C), F32)
        run, first = zero, zero
        for t, (sel, j) in enumerate(tiles):
            run = jnp.where(t == s + 1, zero, run) + _dot(jnp.exp(sbuf[t] - m2[sel]).astype(BF), vvs[j])
            first = jnp.where(t == s, run, first)
        lo = _lo_lanes((C, C))
        for w, res in enumerate((first, run)):
            l = res[:, C:]
            a_ref[_rows(2 * s + w), :] = res[:, :C] / l
            mw = m2[w]
            g_ref[_rows(2 * s + w), :] = -(jnp.where(lo, mw[:, :C], mw[:, C:]) + jnp.log(l))

    col = _fox_pair_columns()
    return pl.pallas_call(
        body, name="fox_fwd", grid=(NPAIR, NSTEP),
        in_specs=[ANY,
                  pl.BlockSpec((NCH, FH, C), lambda p, s: (0, 0, 0)),
                  pl.BlockSpec((2 * C, C), lambda p, s: (0, 0)),
                  pl.BlockSpec((2, C, 2 * C), lambda p, s: (0, 0, 0))] + [ANY] * n,
        out_specs=[col, col] + [ANY] * n,
        out_shape=[jax.ShapeDtypeStruct((TROWS, FH * FD), F32)] * 2
        + ([jax.ShapeDtypeStruct((4,) + share.shape, share.dtype)] if n else []),
        scratch_shapes=[pltpu.VMEM((NCH, 2 * C, C), BF), pltpu.VMEM((NCH, 2 * C, 2 * C), BF),
                        pltpu.VMEM((2, C, C), BF), pltpu.VMEM((2, C, 2 * C), F32),
                        pltpu.VMEM((NTILE, C, 2 * C), F32),
                        pltpu.VMEM((2, T, C), F32), pltpu.VMEM((2, T, C), F32), pltpu.VMEM((2, T, C), F32),
                        pltpu.SemaphoreType.DMA((3, 2))]
        + [pltpu.SemaphoreType.DMA((3,)), pltpu.SemaphoreType.DMA((3,))] * n,
        compiler_params=_params(("arbitrary", "arbitrary")),
    )(z, ct, cst["ones_aug"], cst["mask_bias"], *([share] * n))


def _fox_bwd(z, da, g, delta, ct, cst, parts=()):
    grp = 9

    n = len(parts)

    def body(z_ref, da_ref, g_ref, dl_ref, ct_ref, ones_ref, mb_ref, *rest):
        part_refs, (dq_ref, dr_ref, dk_ref, dv_ref, dcs_ref), land_refs = rest[:n], rest[n:n + 5], rest[n + 5:2 * n + 5]
        (kks, vvs, q2, qq2, dd2, da2, gi2, dl2, dq2, dvb, dkb, dkacc, dvacc, csacc, qbuf, kbuf, vbuf, dabuf, gbuf,
         dlbuf, col_sems) = rest[2 * n + 5:2 * n + 26]
        p, s = pl.program_id(0), pl.program_id(1)
        slot = p % 2
        ones = ones_ref[...]
        if n:
            copies = _chip_copies(part_refs, land_refs, *rest[2 * n + 26:], by_dest=True)

            @pl.when((p == 0) & (s == 0))
            def _():
                for cp in copies:
                    cp.start()

            @pl.when((p == NPAIR - 1) & (s == NSTEP - 1))
            def _():
                for cp in copies:
                    cp.wait()

        @pl.when(s == 0)
        def _():
            dkacc[...] = jnp.zeros_like(dkacc)
            dvacc[...] = jnp.zeros_like(dvacc)
            csacc[...] = jnp.zeros_like(csacc)
            _fox_columns([(z_ref, QB_F, qbuf), (z_ref, KB_F, kbuf), (z_ref, VB_F, vbuf), (da_ref, 0, dabuf),
                          (g_ref, 0, gbuf), (dl_ref, 0, dlbuf)], col_sems, p)

            def prep(j, carry):
                kks[j] = _split_heads(kbuf[slot, _rows(j), :]).astype(BF)
                vvs[j] = _split_heads(vbuf[slot, _rows(j), :]).astype(BF)
                return carry

            lax.fori_loop(0, NCH, prep, 0)

        for w, (chunk, blk) in enumerate(((s, 2 * s), (NCH - 1 - s, jnp.where(s == NSTEP - 1, 2 * s, 2 * s + 1)))):
            qf = qbuf[slot, _rows(chunk), :]
            q2[w] = (qf * FSCALE).astype(BF)
            qq2[w] = jnp.concatenate([_split_heads(qf).astype(BF), ones], axis=1)
            da2[w] = dabuf[slot, _rows(blk), :]
            dd2[w] = _split_heads(da2[w].astype(F32)).astype(BF)
            gi2[w] = _spread2(gbuf[slot, _rows(blk), :])
            dl2[w] = _spread2(dlbuf[slot, _rows(blk), :])
        dq2[...] = jnp.zeros_like(dq2)
        zero = jnp.zeros((C, 2 * C), F32)

        def group(gi, carry):
            ts = [gi * grp + u for u in range(grp)]
            tiles = [_fox_tile(s, t) for t in ts]
            kk = [kks[j] for _, j in tiles]
            ss = [_dg(q2[sel], kj, NT) + (gi2[sel] - _fox_key_bias(ct_ref, p, j)) for kj, (sel, j) in zip(kk, tiles)]
            ss[0] = ss[0] + mb_ref[(gi == 0).astype(jnp.int32)]
            ss[-1] = ss[-1] + mb_ref[(gi == 1).astype(jnp.int32)]
            dps = [_dg(da2[sel], vvs[j], NT) for sel, j in tiles]
            pes = [jnp.exp(st) for st in ss]
            dss = [pe * (dp - dl2[sel]) * FSCALE for pe, dp, (sel, _) in zip(pes, dps, tiles)]
            pts = [jnp.concatenate([pe[:, :C].T, pe[:, C:].T], axis=1).astype(BF) for pe in pes]
            dsts = [jnp.concatenate([ds[:, :C].T, ds[:, C:].T], axis=1).astype(BF) for ds in dss]
            dvs = [_dot(pt, dd2[sel]) for pt, (sel, _) in zip(pts, tiles)]
            rs = [_dot(dst, qq2[sel]) for dst, (sel, _) in zip(dsts, tiles)]
            parts = [_dot(ds.astype(BF), jnp.concatenate([kj, ones], axis=1)) for ds, kj in zip(dss, kk)]
            for t, dv, rr in zip(ts, dvs, rs):
                dvb[t] = dv
                dkb[t] = rr
            pa, pb = zero, zero
            for t, part in zip(ts, parts):
                pa = pa + jnp.where(t <= s, part, zero)
                pb = pb + jnp.where(t <= s, zero, part)
            dq2[0] += pa
            dq2[1] += pb
            return carry

        ntile = jnp.where(s == NSTEP - 1, grp, NTILE)
        lax.fori_loop(0, ntile // grp, group, 0)

        def scatter(t, carry):
            _, j = _fox_tile(s, t)
            r = pl.ds(pl.multiple_of(j * C, C), C)
            dvacc[r, :] += dvb[t]
            dkacc[r, :] += dkb[t, :, :C]
            csacc[r, :] += dkb[t, :, C:]
            return carry

        lax.fori_loop(0, ntile, scatter, 0)
        for w, chunk in ((1, NCH - 1 - s), (0, s)):
            res = dq2[w]
            dq_ref[_rows(chunk), :] = res[:, :C].astype(BF)
            dr_ref[_rows(2 * s + w), :] = res[:, C:]

        @pl.when(s == NSTEP - 1)
        def _():
            dk_ref[...] = dkacc[...].astype(BF)
            dv_ref[...] = dvacc[...].astype(BF)
            dcs_ref[...] = csacc[...]

    both = _fox_pair_columns()
    col = pl.BlockSpec((T, C), lambda p, s: (0, p))
    return pl.pallas_call(
        body, name="fox_bwd", grid=(NPAIR, NSTEP),
        in_specs=[ANY] * 4
        + [pl.BlockSpec((NCH, FH, C), lambda p, s: (0, 0, 0)),
           pl.BlockSpec((2 * C, C), lambda p, s: (0, 0)),
           pl.BlockSpec((2, C, 2 * C), lambda p, s: (0, 0, 0))] + [ANY] * n,
        out_specs=[col, both, col, col, col] + [ANY] * n,
        out_shape=[jax.ShapeDtypeStruct((T, FH * FD), BF), jax.ShapeDtypeStruct((TROWS, FH * FD), F32),
                   jax.ShapeDtypeStruct((T, FH * FD), BF), jax.ShapeDtypeStruct((T, FH * FD), BF),
                   jax.ShapeDtypeStruct((T, FH * FD), F32)]
        + [jax.ShapeDtypeStruct(p.shape, p.dtype) for p in parts],
        scratch_shapes=[pltpu.VMEM((NCH, 2 * C, C), BF), pltpu.VMEM((NCH, 2 * C, C), BF),
                        pltpu.VMEM((2, C, C), BF), pltpu.VMEM((2, 2 * C, 2 * C), BF), pltpu.VMEM((2, 2 * C, C), BF),
                        pltpu.VMEM((2, C, C), BF), pltpu.VMEM((2, C, 2 * C), F32), pltpu.VMEM((2, C, 2 * C), F32),
                        pltpu.VMEM((2, C, 2 * C), F32),
                        pltpu.VMEM((NTILE, C, C), F32), pltpu.VMEM((NTILE, C, 2 * C), F32),
                        pltpu.VMEM((T, C), F32), pltpu.VMEM((T, C), F32), pltpu.VMEM((T, C), F32),
                        pltpu.VMEM((2, T, C), F32), pltpu.VMEM((2, T, C), F32), pltpu.VMEM((2, T, C), F32),
                        pltpu.VMEM((2, T, C), BF), pltpu.VMEM((2, T, C), F32), pltpu.VMEM((2, T, C), F32),
                        pltpu.SemaphoreType.DMA((6, 2))]
        + ([pltpu.SemaphoreType.DMA((3 * n,)), pltpu.SemaphoreType.DMA((3 * n,))] if n else []),
        compiler_params=_params(("arbitrary", "arbitrary")),
    )(z, da, g, delta, ct, cst["ones_aug"], cst["mask_bias"], *parts)


def _fox_gate_bwd(drow, dcol, zf, bf_pad, cst):
    def body(dr_ref, dc_ref, zf_ref, b_ref, tri_ref, pick_ref, dff_ref, db_ref, carry):
        s = pl.program_id(0)
        n = NCH - 1 - s

        @pl.when(s == 0)
        def _():
            carry[...] = jnp.zeros_like(carry)
            db_ref[...] = jnp.zeros_like(db_ref)

        dcb = _split_dot((dr_ref[...] - dc_ref[...]) * (1.0 / FSCALE), pick_ref[...])
        suf = _split_dot(dcb, tri_ref[...], TN, x_first=False) + carry[0:1, :]
        carry[...] = jnp.broadcast_to(suf[0:1, :], carry.shape)
        x = zf_ref[...] + b_ref[...]
        row = n * C + lax.broadcasted_iota(jnp.int32, (C, C), 0)
        dff = jnp.where(row >= PAD, suf * (1.0 - jax.nn.sigmoid(x)), 0.0)
        dff_ref[...] = dff.astype(BF)
        db_ref[...] += jnp.sum(dff, axis=0, keepdims=True)

    rev = lambda s: (NCH - 1 - s, 0)
    return pl.pallas_call(
        body, name="fox_gate_bwd", grid=(NCH,),
        in_specs=[pl.BlockSpec((C, FH * FD), lambda s: (_fox_pos(NCH - 1 - s), 0)),
                  pl.BlockSpec((C, FH * FD), rev), pl.BlockSpec((C, C), rev),
                  pl.BlockSpec((1, C), lambda s: (0, 0)), pl.BlockSpec((C, C), lambda s: (0, 0)),
                  pl.BlockSpec((FH * FD, C), lambda s: (0, 0))],
        out_specs=[pl.BlockSpec((C, C), rev), pl.BlockSpec((1, C), lambda s: (0, 0))],
        out_shape=[jax.ShapeDtypeStruct((T, C), BF), jax.ShapeDtypeStruct((1, C), F32)],
        scratch_shapes=[pltpu.VMEM((8, C), F32)],
        compiler_params=_params(("arbitrary",)),
    )(drow, dcol, zf, bf_pad, cst["tri"], cst["pick"])


def _head_norm(r):
    rn, rs = [], []
    for h in range(RH):
        rh = r[:, RDV * h:RDV * (h + 1)]
        s = lax.rsqrt(jnp.mean(rh * rh, axis=1, keepdims=True) + EPS)
        rn.append(rh * s)
        rs.append(s)
    return jnp.concatenate(rn, axis=1), rs


def _gated(r, rg, a, fg):
    rn, _ = _head_norm(r)
    return jnp.concatenate([rn * (rg * jax.nn.sigmoid(rg)), a * (fg * jax.nn.sigmoid(fg))], axis=1)


def _out_loss(r, z, a, wout, x, tgt, fgain):
    def body(r_ref, rg_ref, a_ref, fg_ref, w_ref, x_ref, t_ref, g_ref, yt_ref, do_ref, dob_ref, loss_ref, dg_ref):
        i = pl.program_id(0)

        @pl.when(i == 0)
        def _():
            yt_ref[...] = jnp.zeros_like(yt_ref)
            do_ref[...] = jnp.zeros_like(do_ref)
            dob_ref[...] = jnp.zeros_like(dob_ref)
            loss_ref[...] = jnp.zeros_like(loss_ref)
            dg_ref[...] = jnp.zeros_like(dg_ref)

        @pl.when(i > 0)
        def _():
            y = _gated(r_ref[...], rg_ref[...], a_ref[...], fg_ref[...])
            yt_ref[...] = y.T.astype(BF)
            o = x_ref[...] + _dot(y.astype(BF), w_ref[...])
            rs = lax.rsqrt(jnp.mean(o * o, axis=1, keepdims=True) + EPS)
            on = o * rs
            g = g_ref[...]
            e = on * g - t_ref[...]
            loss_ref[...] += 0.5 * jnp.sum(jnp.mean(e * e, axis=1, keepdims=True))
            dyh = e * (1.0 / D)
            dg_ref[...] += jnp.sum(dyh * on, axis=0, keepdims=True)
            don = dyh * g
            do = rs * (don - on * jnp.mean(don * on, axis=1, keepdims=True))
            do_ref[...] = do
            dob_ref[...] = do.astype(BF)

    tok = lambda i: (jnp.maximum(i - 1, 0), 0)
    return pl.pallas_call(
        body, name="out_loss", grid=(NCH,),
        in_specs=[pl.BlockSpec((C, D), lambda i: (i, 0)), pl.BlockSpec((C, D), lambda i: (i, GB_R)),
                  pl.BlockSpec((C, D), lambda i: (_fox_pos(i), 0)), pl.BlockSpec((C, D), lambda i: (i, GB_F)),
                  pl.BlockSpec((DMIX, D), lambda i: (0, 0)),
                  pl.BlockSpec((C, D), tok), pl.BlockSpec((C, D), tok), pl.BlockSpec((1, D), lambda i: (0, 0))],
        out_specs=[pl.BlockSpec((DMIX, C), lambda i: (0, i)), pl.BlockSpec((C, D), lambda i: (i, 0)),
                   pl.BlockSpec((C, D), lambda i: (i, 0)), pl.BlockSpec((8, C), lambda i: (0, 0)),
                   pl.BlockSpec((1, D), lambda i: (0, 0))],
        out_shape=[jax.ShapeDtypeStruct((DMIX, T), BF), jax.ShapeDtypeStruct((T, D), F32),
                   jax.ShapeDtypeStruct((T, D), BF), jax.ShapeDtypeStruct((8, C), F32),
                   jax.ShapeDtypeStruct((1, D), F32)],
        compiler_params=_params(("arbitrary",)),
    )(r, z, a, z, wout, x, tgt, fgain)


def _silu_and_grad(x):
    s = jax.nn.sigmoid(x)
    return x * s, s * (1.0 + x * (1.0 - s))


def _dy_gate_bwd(dob, wout, r, z, a, seg, swap=()):
    n = len(swap)

    def body(do_ref, w_ref, r_ref, rg_ref, a_ref, fg_ref, seg_ref, *rest):
        (dr_ref, da_ref, drg_ref, dfg_ref, dl_ref) = rest[n:n + 5]
        if n:
            copies = _pair_copies(rest[:n], rest[n + 5:2 * n + 5], *rest[2 * n + 5:], n)

            @pl.when(pl.program_id(0) == 0)
            def _():
                for cp in copies:
                    cp.start()

            @pl.when(pl.program_id(0) == NCH - 1)
            def _():
                for cp in copies:
                    cp.wait()

        dy = _dg(do_ref[...], w_ref[...], NT)
        a_ = a_ref[...]
        rn, rs = _head_norm(r_ref[...])
        silu_rg, dsilu_rg = _silu_and_grad(rg_ref[...])
        silu_fg, dsilu_fg = _silu_and_grad(fg_ref[...])
        dyr, dyf = dy[:, :D], dy[:, D:]
        drn = dyr * silu_rg
        drg_ref[...] = (dyr * rn * dsilu_rg).astype(BF)
        for h in range(RH):
            sl = slice(RDV * h, RDV * (h + 1))
            dh, nh = drn[:, sl], rn[:, sl]
            dr_ref[:, sl] = (rs[h] * (dh - nh * jnp.mean(dh * nh, axis=1, keepdims=True))).astype(BF)
        dab = (dyf * silu_fg).astype(BF)
        da_ref[...] = dab
        dfg_ref[...] = (dyf * a_ * dsilu_fg).astype(BF)
        prod = dab.astype(F32) * a_
        segm = seg_ref[...]
        for p in range(NPAIR):
            sl = slice(C * p, C * (p + 1))
            hi = prod[:, sl].astype(BF)
            lo = (prod[:, sl] - hi.astype(F32)).astype(BF)
            dl_ref[:, sl] = _dot(hi, segm) + _dot(lo, segm)

    row = pl.BlockSpec((C, D), lambda i: (i, 0))
    fox = pl.BlockSpec((C, D), lambda i: (_fox_pos(i), 0))
    return pl.pallas_call(
        body, name="dy_gate_bwd", grid=(NCH,),
        in_specs=[row, pl.BlockSpec((DMIX, D), lambda i: (0, 0)),
                  row, pl.BlockSpec((C, D), lambda i: (i, GB_R)),
                  fox, pl.BlockSpec((C, D), lambda i: (i, GB_F)),
                  pl.BlockSpec((C, C), lambda i: (0, 0))] + [ANY] * n,
        out_specs=[row, fox, row, row, fox] + [ANY] * n,
        out_shape=[jax.ShapeDtypeStruct((T, D), BF), jax.ShapeDtypeStruct((TROWS, D), BF),
                   jax.ShapeDtypeStruct((T, D), BF), jax.ShapeDtypeStruct((T, D), BF),
                   jax.ShapeDtypeStruct((TROWS, D), F32)]
        + [jax.ShapeDtypeStruct((4, s.shape[1] // 2, s.shape[2]), s.dtype) for s in swap],
        scratch_shapes=[pltpu.SemaphoreType.DMA((n,)), pltpu.SemaphoreType.DMA((n,))] if n else [],
        compiler_params=_params(("arbitrary",)),
    )(dob, wout, r, z, a, z, seg, *swap)


DZ_WIDTHS = (512, 512, 1024, 1024, 1024, 1024, 1024, 1024)


def _du_norm_bwd(dzs, dzf, wt, wft, hpad, g, dopad, parts=()):
    tm, tk = 544, 1024
    nk = WMAIN // tk
    ni = T // tm
    n = len(parts)

    def body(rq_ref, rk_ref, rv_ref, rg_ref, fq_ref, fk_ref, fv_ref, fg_ref, dzf_ref, w_ref, wf_ref, h_ref, g_ref,
             do_ref, *rest):
        part_refs, (gx_ref, lead_ref, dg_ref), land_refs = rest[:n], rest[n:n + 3], rest[n + 3:2 * n + 3]
        acc, obuf, out_sems = rest[2 * n + 3:2 * n + 6]
        i, k = pl.program_id(0), pl.program_id(1)

        def out_copies(b):
            if b == 0:
                return [pltpu.make_async_copy(obuf.at[0, pl.ds(0, C)], lead_ref, out_sems.at[0, 0]),
                        pltpu.make_async_copy(obuf.at[0, pl.ds(C, tm - C)], gx_ref.at[pl.ds(0, tm - C)],
                                              out_sems.at[0, 1])]
            return [pltpu.make_async_copy(obuf.at[b % 2], gx_ref.at[pl.ds(tm * b - C, tm)], out_sems.at[b % 2, 0])]

        if n:
            send_sems, recv_sems = rest[2 * n + 6:]
            copies = _chip_copies(part_refs, land_refs, send_sems, recv_sems, by_dest=True)

            @pl.when((i == 0) & (k == 0))
            def _():
                for cp in copies:
                    cp.start()

            @pl.when((i == ni - 1) & (k == nk - 1))
            def _():
                for cp in copies:
                    cp.wait()

        @pl.when(k == 0)
        def _():
            acc[...] = (_dot(dzf_ref[...], wf_ref[...]) + _dot(rq_ref[...], w_ref[:512, :])
                        + _dot(rk_ref[...], w_ref[512:, :]))

        for kk, piece in enumerate((rv_ref, rg_ref, fq_ref, fk_ref, fv_ref, fg_ref), start=1):
            @pl.when(k == kk)
            def _(piece=piece):
                acc[...] += _dot(piece[...], w_ref[...])

        @pl.when(k == nk - 1)
        def _():
            du = acc[...]
            h = h_ref[...]
            gg = g_ref[...]
            rs = lax.rsqrt(jnp.mean(h * h, axis=1, keepdims=True) + EPS)
            hn = h * rs
            part = jnp.sum(du * hn, axis=0, keepdims=True)

            @pl.when(i == 0)
            def _():
                dg_ref[...] = part

            @pl.when(i > 0)
            def _():
                dg_ref[...] += part

            dhn = du * gg
            gh = rs * (dhn - hn * jnp.mean(dhn * hn, axis=1, keepdims=True)) + do_ref[...]
            for b in range(ni):
                @pl.when(i == b)
                def _(b=b):
                    for cp in (out_copies(b - 2) if b >= 2 else []):
                        cp.wait()
                    obuf[b % 2] = gh
                    for cp in out_copies(b):
                        cp.start()
                    if b == ni - 1:
                        for cp in out_copies(b - 1) + out_copies(b):
                            cp.wait()

    sems = [pltpu.SemaphoreType.DMA((3 * n,)), pltpu.SemaphoreType.DMA((3 * n,))] if n else []
    return pl.pallas_call(
        body, name="du_norm_bwd", grid=(ni, nk),
        in_specs=[pl.BlockSpec((tm, w), lambda i, k: (i, 0)) for w in DZ_WIDTHS]
        + [pl.BlockSpec((tm, C), lambda i, k: (i, 0)),
           pl.BlockSpec((tk, D), lambda i, k: (k, 0)), pl.BlockSpec((C, D), lambda i, k: (0, 0)),
           pl.BlockSpec((tm, D), lambda i, k: (i, 0)), pl.BlockSpec((1, D), lambda i, k: (0, 0)),
           pl.BlockSpec((tm, D), lambda i, k: (i, 0))] + [ANY] * n,
        out_specs=[ANY, ANY, pl.BlockSpec((1, D), lambda i, k: (0, 0))] + [ANY] * n,
        out_shape=[jax.ShapeDtypeStruct((T - C, D), F32), jax.ShapeDtypeStruct((C, D), F32),
                   jax.ShapeDtypeStruct((1, D), F32)]
        + [jax.ShapeDtypeStruct(p.shape, p.dtype) for p in parts],
        scratch_shapes=[pltpu.VMEM((tm, D), F32), pltpu.VMEM((2, tm, D), F32), pltpu.SemaphoreType.DMA((2, 2))] + sems,
        compiler_params=_params(("arbitrary", "arbitrary")),
    )(*dzs, dzf, wt, wft, hpad, g, dopad, *parts)


GROWS = 7680


def _dw_in(dzs, dzf, ut):
    tn = 512
    nmain = WMAIN // tn
    first, blocks = [], []
    for w in DZ_WIDTHS:
        first.append(sum(blocks))
        blocks.append(w // tn)

    def body(rq_ref, rk_ref, rv_ref, rg_ref, fq_ref, fk_ref, fv_ref, fg_ref, dzf_ref, ut_ref, o_ref):
        gidx = pl.program_id(0)
        for piece, g0, nb in zip((rq_ref, rk_ref, rv_ref, rg_ref, fq_ref, fk_ref, fv_ref, fg_ref), first, blocks):
            @pl.when((gidx >= g0) & (gidx < g0 + nb))
            def _(piece=piece):
                o_ref[...] = _dot(ut_ref[...], piece[...]).T.astype(BF)

        @pl.when(gidx == nmain)
        def _():
            o_ref[:C, :] = _dot(ut_ref[...], dzf_ref[...]).T.astype(BF)
            o_ref[C:, :] = jnp.zeros((tn - C, D), BF)

    def piece_spec(g0, nb):
        return pl.BlockSpec((T, tn), lambda gidx: (0, jnp.clip(gidx - g0, 0, nb - 1)))

    return pl.pallas_call(
        body, name="dw_in", grid=(nmain + 1,),
        in_specs=[piece_spec(g0, nb) for g0, nb in zip(first, blocks)]
        + [pl.BlockSpec((T, C), lambda gidx: (0, 0)), pl.BlockSpec((D, T), lambda gidx: (0, 0))],
        out_specs=pl.BlockSpec((tn, D), lambda gidx: (gidx, 0)),
        out_shape=jax.ShapeDtypeStruct((GROWS, D), BF),
        compiler_params=pltpu.CompilerParams(dimension_semantics=("arbitrary",), vmem_limit_bytes=DW_VMEM_LIMIT),
    )(*dzs, dzf, ut)


def _local_step(x, tgt, normed, norm_g, wt, wft, b_f, wout, final_g, reduce_scatter=False, wout_full=None):
    cst = _constants()
    hpad, u, ut = normed
    bf_pad = jnp.pad(b_f, ((0, 0), (0, C - NFF)))
    z = _mm_nt(u, wt, WMAIN, T // 2, 1024, "in_proj")
    zf = _mm_nt(u, wft, C, T // 2, C, "in_proj_ff")
    r, sprev = _ret_fwd(z, cst)
    ct = _fox_prep(zf, bf_pad, cst)
    if wout_full is None:
        a, g = _fox_fwd(z, ct, cst, None)
    else:
        a, g, landed_wout = _fox_fwd(z, ct, cst, wout)
        wout = wout_full(landed_wout)
    yt, dopad, dob, loss8, dfg = _out_loss(r, z, a, wout, x, tgt, final_g)
    dwout = _mm_nn(yt, dob, 512, D, "dw_out", BF)
    g_out = [dwout.reshape(4, DMIX // 4, D)] if reduce_scatter else []
    dr, da, dzrg, dzfg, delta, *r_out = _dy_gate_bwd(dob, wout, r, z, a, cst["seg"], g_out)
    p_out = [_add_halves(g_out[0], r_out[0], "pair_add_out", BF)] if reduce_scatter else []
    dzq_r, dzk_r, dzv_r = _ret_bwd(z, cst, sprev, dr)
    dzq_f, drow, dzk_f, dzv_f, dcol, *e_out = _fox_bwd(z, da, g, delta, ct, cst, p_out)
    dzf, dbf = _fox_gate_bwd(drow, dcol, zf, bf_pad, cst)
    dzs = [dzq_r, dzk_r, dzv_r, dzrg, dzq_f, dzk_f, dzv_f, dzfg]
    gwt = _dw_in(dzs, dzf, ut)
    p_in = [_swap_add_windows(gwt)[1]] if reduce_scatter else []
    gx, lead, dng, *e_in = _du_norm_bwd(dzs, dzf, wt, wft, hpad, norm_g, dopad, p_in)
    return (loss8[0, 0], gx, lead[PAD:], dng, gwt, dbf[:, :NFF], dwout, dfg, p_in + p_out, e_in + e_out)


WOFF, WLEN = 1792, 2048
WHALF = WLEN // 2
LAP = WPADROWS - WOFF


def _own_window(w3):
    rows, sub, lanes = w3.shape
    pad = WPADROWS - rows
    tb = 96
    nb = WPADROWS // tb
    half = rows // 2

    def body(w_ref, o_ref, buf, sems):
        x, y, _ = _place()
        shift = 4 * (2 * x + y)
        buf[pl.ds(0, pad)] = jnp.zeros((pad, sub, lanes), F32)
        buf[pl.ds(rows, pad)] = jnp.zeros((pad, sub, lanes), F32)
        cps = [pltpu.make_async_copy(w_ref.at[pl.ds(half * h, half)], buf.at[pl.ds(shift + half * h, half)],
                                     sems.at[h]) for h in range(2)]
        for cp in cps:
            cp.start()

        def block(i, carry):
            r0 = pl.multiple_of(i * tb, tb)
            o_ref[pl.ds(r0, tb), :] = buf[pl.ds(r0, tb)].reshape(tb, sub * lanes).astype(BF)
            return carry

        cps[0].wait()
        lax.fori_loop(0, half // tb, block, 0)
        cps[1].wait()
        lax.fori_loop(half // tb, nb, block, 0)

    return pl.pallas_call(
        body, name="own_window",
        in_specs=[ANY], out_shape=jax.ShapeDtypeStruct((WPADROWS, sub * lanes), BF),
        scratch_shapes=[pltpu.VMEM((WPADROWS, sub, lanes), F32), pltpu.SemaphoreType.DMA((2,))],
        compiler_params=pltpu.CompilerParams(vmem_limit_bytes=VMEM_LIMIT),
    )(w3)


def _gather_weights(own_win, meta, x, norm_g):
    half_main, half_lap, half_meta = WOFF // 2, LAP // 2, meta.shape[0] // 2
    last = NCH - 1

    def body(win_ref, meta_ref, x_ref, g_ref, w_ref, laps_ref, gm_ref, h_ref, u_ref, ut_ref,
             send_sems, recv_sems, local_sems, stage, lapbuf, headbuf, metabuf):
        step = pl.program_id(0)
        x, y, c = _place()
        me_s = 2 * x + y
        sib = (x, y, 1 - c)
        chips = _other_chips(x, y)

        def emit(h):
            u = _norm_rows(h, g_ref[...])
            h_ref[...] = h
            u_ref[...] = u.astype(BF)
            ut_ref[...] = u.T.astype(BF)

        kinds = [
            (lambda h: win_ref.at[pl.ds(half_main * h, half_main)],
             lambda s, h: w_ref.at[pl.ds(WOFF * s + half_main * h, half_main)]),
            (lambda h: win_ref.at[pl.ds(WOFF + half_lap * h, half_lap)],
             lambda s, h: laps_ref.at[s, pl.ds(half_lap * h, half_lap)]),
            (lambda h: meta_ref.at[pl.ds(half_meta * h, half_meta)],
             lambda s, h: gm_ref.at[s, pl.ds(half_meta * h, half_meta)]),
        ]
        own_in = pltpu.make_async_copy(win_ref.at[pl.ds(0, WOFF)], stage, local_sems.at[0])
        own_lap_in = pltpu.make_async_copy(win_ref.at[pl.ds(WOFF, LAP)], lapbuf.at[0], local_sems.at[1])
        own_out = pltpu.make_async_copy(stage, w_ref.at[pl.ds(WOFF * me_s, WOFF)], local_sems.at[0])
        own_lap_out = pltpu.make_async_copy(lapbuf.at[0], laps_ref.at[me_s], local_sems.at[1])
        sends, arrivals, forwards, forwarded = [], [], [], []
        for a, (src, dst) in enumerate(kinds):
            for k, (cx, cy, cs) in enumerate(chips):
                there = dict(send_sem=send_sems.at[6 * a + k], recv_sem=recv_sems.at[6 * a + k],
                             device_id=(cx, cy, c), device_id_type=MESH)
                across = dict(send_sem=send_sems.at[6 * a + 3 + k], recv_sem=recv_sems.at[6 * a + 3 + k],
                              device_id=sib, device_id_type=MESH)
                sends.append(pltpu.make_async_remote_copy(src_ref=src(c), dst_ref=dst(me_s, c), **there))
                arrivals.append(pltpu.make_async_remote_copy(src_ref=dst(cs, c), dst_ref=dst(cs, c), **there))
                forwards.append(pltpu.make_async_remote_copy(src_ref=dst(cs, c), dst_ref=dst(cs, c), **across))
                forwarded.append(pltpu.make_async_remote_copy(
                    src_ref=dst(cs, 1 - c), dst_ref=dst(cs, 1 - c), **across))

        @pl.when(step == 0)
        def _():
            own_in.start()
            own_lap_in.start()
            for cp in sends:
                cp.start()
            own_in.wait()
            own_out.start()
            own_lap_in.wait()
            own_lap_out.start()

        @pl.when(step < last)
        def _():
            emit(x_ref[...])

        @pl.when(step == last)
        def _():
            for cp, fwd in zip(arrivals, forwards):
                cp.wait_recv()
                fwd.start()
            for cp in forwarded:
                cp.wait_recv()
            for cp in sends + forwards:
                cp.wait_send()
            own_out.wait()
            own_lap_out.wait()
            for s in range(1, 4):
                head = w_ref.at[pl.ds(WOFF * s, LAP)]
                loads = [pltpu.make_async_copy(laps_ref.at[s - 1], lapbuf.at[1], local_sems.at[2]),
                         pltpu.make_async_copy(head, headbuf, local_sems.at[3])]
                for cp in loads:
                    cp.start()
                for cp in loads:
                    cp.wait()
                headbuf[...] = (headbuf[...].astype(F32) + lapbuf[1].astype(F32)).astype(BF)
                store = pltpu.make_async_copy(headbuf, head, local_sems.at[3])
                store.start()
                store.wait()
            loads = [pltpu.make_async_copy(meta_ref, metabuf.at[me_s], local_sems.at[0])]
            loads += [pltpu.make_async_copy(gm_ref.at[cs], metabuf.at[cs], local_sems.at[1 + k])
                      for k, (_, _, cs) in enumerate(chips)]
            for cp in loads:
                cp.start()
            for cp in loads:
                cp.wait()
            tokens = jnp.concatenate([metabuf[s] for s in range(4)], axis=1)
            emit(jnp.concatenate([jnp.zeros((PAD, D), F32), tokens], axis=0))

    def chunk(i):
        return (i + 1) % NCH

    return pl.pallas_call(
        body, name="all_gather_w", grid=(NCH,),
        in_specs=[ANY, ANY, pl.BlockSpec((C, D), lambda i: (jnp.minimum(i, last - 1), 0)),
                  pl.BlockSpec((1, D), lambda i: (0, 0))],
        out_specs=[ANY] * 3 + [pl.BlockSpec((C, D), lambda i: (chunk(i), 0))] * 2
        + [pl.BlockSpec((D, C), lambda i: (0, chunk(i)))],
        out_shape=[jax.ShapeDtypeStruct((WMAIN, D), own_win.dtype), jax.ShapeDtypeStruct((4, LAP, D), own_win.dtype),
                   jax.ShapeDtypeStruct((4,) + meta.shape, meta.dtype),
                   jax.ShapeDtypeStruct((T, D), F32), jax.ShapeDtypeStruct((T, D), BF),
                   jax.ShapeDtypeStruct((D, T), BF)],
        scratch_shapes=[pltpu.SemaphoreType.DMA((18,)), pltpu.SemaphoreType.DMA((18,)), pltpu.SemaphoreType.DMA((4,)),
                        pltpu.VMEM((WOFF, D), own_win.dtype), pltpu.VMEM((2, LAP, D), own_win.dtype),
                        pltpu.VMEM((LAP, D), own_win.dtype), pltpu.VMEM((4,) + meta.shape, meta.dtype)],
        compiler_params=_params(("arbitrary",)),
    )(own_win, meta, x, norm_g)


def _pair_copies(ins, outs, send_sems, recv_sems, n):
    x, y, c = _place()
    sib = dict(device_id=(x, y, 1 - c), device_id_type=MESH)
    cps = []
    for a in range(n):
        rows = ins[a].shape[1] // 2
        cps.append(pltpu.make_async_remote_copy(
            src_ref=ins[a].at[:, pl.ds((1 - c) * rows, rows)], dst_ref=outs[a],
            send_sem=send_sems.at[a], recv_sem=recv_sems.at[a], **sib))
    for k in range(4 * (len(ins) - n)):
        cps.append(pltpu.make_async_remote_copy(
            src_ref=ins[n].at[pl.ds(WOFF * k + (1 - c) * WHALF, WHALF)], dst_ref=outs[n].at[k],
            send_sem=send_sems.at[n + k], recv_sem=recv_sems.at[n + k], **sib))
    return cps


def _swap_add_windows(gwt):
    nchunk = 4
    rows = WHALF // nchunk

    def body(gw_ref, land_ref, out_ref, send_sems, recv_sems, local_sems, own, theirs):
        _, _, c = _place()
        swaps = _pair_copies([gw_ref], [land_ref], send_sems, recv_sems, 0)
        loads = [pltpu.make_async_copy(gw_ref.at[pl.ds(WOFF * k + c * WHALF, WHALF)], own.at[k], local_sems.at[k])
                 for k in range(4)]
        stores = [pltpu.make_async_copy(own.at[k], out_ref.at[k], local_sems.at[k]) for k in range(4)]
        for cp in loads:
            cp.start()
        swaps[0].start()
        for k in range(4):
            swaps[k].wait_send()
            if k + 1 < 4:
                swaps[k + 1].start()
            swaps[k].wait_recv()
            fetch = pltpu.make_async_copy(land_ref.at[k], theirs, local_sems.at[4])
            fetch.start()
            loads[k].wait()
            fetch.wait()

            def add(i, carry, k=k):
                r = _rows(i, rows)
                own[k, r, :] = (own[k, r, :].astype(F32) + theirs[r, :].astype(F32)).astype(BF)
                return carry

            lax.fori_loop(0, nchunk, add, 0)
            stores[k].start()
        for cp in stores:
            cp.wait()

    return pl.pallas_call(
        body, name="rs_pair_swap_add",
        in_specs=[ANY], out_specs=[ANY, ANY],
        out_shape=[jax.ShapeDtypeStruct((4, WHALF, D), gwt.dtype), jax.ShapeDtypeStruct((4, WHALF, D), BF)],
        scratch_shapes=[pltpu.SemaphoreType.DMA((4,)), pltpu.SemaphoreType.DMA((4,)), pltpu.SemaphoreType.DMA((5,)),
                        pltpu.VMEM((4, WHALF, D), gwt.dtype), pltpu.VMEM((WHALF, D), gwt.dtype)],
        compiler_params=pltpu.CompilerParams(vmem_limit_bytes=VMEM_LIMIT),
    )(gwt)


def _chip_exchange(parts, small):
    n = len(parts)

    def body(*refs):
        ins, sm = refs[:n], refs[n]
        outs, smo = refs[n + 1:2 * n + 1], refs[2 * n + 1]
        send_sems, recv_sems = refs[2 * n + 2:]
        cps = _chip_copies(ins, outs, send_sems, recv_sems, by_dest=True)
        cps += _chip_copies([sm], [smo], send_sems.at[pl.ds(3 * n, 3)], recv_sems.at[pl.ds(3 * n, 3)], by_dest=False)
        for cp in cps:
            cp.start()
        for cp in cps:
            cp.wait()

    return pl.pallas_call(
        body, name="rs_chip_exchange",
        in_specs=[ANY] * (n + 1), out_specs=[ANY] * (n + 1),
        out_shape=[jax.ShapeDtypeStruct(p.shape, p.dtype) for p in parts]
        + [jax.ShapeDtypeStruct((4,) + small.shape, small.dtype)],
        scratch_shapes=[pltpu.SemaphoreType.DMA((3 * (n + 1),)), pltpu.SemaphoreType.DMA((3 * (n + 1),))],
    )(*parts, small)


def _pair_send(halves):
    n = len(halves)

    def body(*refs):
        ins, outs = refs[:n], refs[n:2 * n]
        send_sems, recv_sems = refs[2 * n:]
        x, y, c = _place()
        cps = [pltpu.make_async_remote_copy(
            src_ref=ins[a], dst_ref=outs[a], send_sem=send_sems.at[a], recv_sem=recv_sems.at[a],
            device_id=(x, y, 1 - c), device_id_type=MESH) for a in range(n)]
        for cp in cps:
            cp.start()
        for cp in cps:
            cp.wait()

    return pl.pallas_call(
        body, name="rs_pair_send",
        in_specs=[ANY] * n, out_specs=[ANY] * n,
        out_shape=[jax.ShapeDtypeStruct(h.shape, h.dtype) for h in halves],
        scratch_shapes=[pltpu.SemaphoreType.DMA((n,)), pltpu.SemaphoreType.DMA((n,))],
    )(*halves)


def _row_block(rows):
    for tb in (256, 128, 64, 32, 16, 8):
        if rows % tb == 0:
            return tb
    return rows


def _add_halves(full, recv, name, out_dtype):
    _, r2, w = recv.shape
    tb = _row_block(r2)
    nb = r2 // tb
    c = lax.axis_index("c")

    def body(c_ref, a_ref, b_ref, o_ref):
        o_ref[...] = (a_ref[...].astype(F32) + b_ref[...].astype(F32)).astype(o_ref.dtype)

    return pl.pallas_call(
        body, name=name,
        grid_spec=pltpu.PrefetchScalarGridSpec(
            num_scalar_prefetch=1, grid=(4, nb),
            in_specs=[pl.BlockSpec((1, tb, w), lambda s, i, cr: (s, cr[0] * nb + i, 0)),
                      pl.BlockSpec((1, tb, w), lambda s, i, cr: (s, i, 0))],
            out_specs=pl.BlockSpec((1, tb, w), lambda s, i, cr: (s, i, 0))),
        out_shape=jax.ShapeDtypeStruct(recv.shape, out_dtype),
        compiler_params=_params(("parallel", "parallel")),
    )(jnp.reshape(c, (1,)).astype(jnp.int32), full, recv)


def _add2(a, b, name):
    def body(a_ref, b_ref, o_ref):
        o_ref[...] = a_ref[...] + b_ref[...]

    return pl.pallas_call(body, name=name, out_shape=jax.ShapeDtypeStruct(a.shape, a.dtype))(a, b)


def _sum4(buf, own, name):
    _, r, w = buf.shape
    tb = _row_block(r)
    me_s = 2 * lax.axis_index("x") + lax.axis_index("y")
    by_dest = own.ndim == 3

    def body(s_ref, b_ref, own_ref, o_ref):
        mine = (own_ref[0] if by_dest else own_ref[...]).astype(F32)
        terms = [jnp.where(s_ref[0] == t, mine, b_ref[t].astype(F32)) for t in range(4)]
        o_ref[...] = ((terms[0] + terms[1]) + terms[2]) + terms[3]

    own_spec = (pl.BlockSpec((1, tb, w), lambda i, sr: (sr[0], i, 0)) if by_dest
                else pl.BlockSpec((tb, w), lambda i, sr: (i, 0)))
    return pl.pallas_call(
        body, name=name,
        grid_spec=pltpu.PrefetchScalarGridSpec(
            num_scalar_prefetch=1, grid=(r // tb,),
            in_specs=[pl.BlockSpec((4, tb, w), lambda i, sr: (0, i, 0)), own_spec],
            out_specs=pl.BlockSpec((tb, w), lambda i, sr: (i, 0))),
        out_shape=jax.ShapeDtypeStruct((r, w), F32),
        compiler_params=_params(("parallel",)),
    )(jnp.reshape(me_s, (1,)).astype(jnp.int32), buf, own)


def _adamw_math(w, g, m, v):
    mn = B1 * m + (1.0 - B1) * g
    vn = B2 * v + (1.0 - B2) * (g * g)
    m_hat = mn / (1.0 - B1 ** STEP)
    v_hat = vn / (1.0 - B2 ** STEP)
    return -LR * (m_hat / (jnp.sqrt(v_hat) + AEPS) + WD * w), mn, vn


def _adamw(w, g, m, v, name):
    r, c_ = w.shape
    tb = _row_block(r)
    if tb == r and r > 512:
        tb = 256

    def body(w_ref, g_ref, m_ref, v_ref, d_ref, mo_ref, vo_ref):
        d_ref[...], mo_ref[...], vo_ref[...] = _adamw_math(w_ref[...], g_ref[...], m_ref[...], v_ref[...])

    spec = pl.BlockSpec((tb, c_), lambda i: (i, 0))
    return pl.pallas_call(
        body, name=name, grid=(pl.cdiv(r, tb),),
        in_specs=[spec] * 4, out_specs=[spec] * 3,
        out_shape=[jax.ShapeDtypeStruct(w.shape, F32)] * 3,
        compiler_params=_params(("parallel",)),
    )(w, g, m, v)


def _adamw_rows(w, g_mine, g_sib, m, v, name):
    r = w.shape[0]
    tb = 256
    sub, lanes = w.shape[1:]
    nh = g_mine.shape[0] // tb
    nsteps = pl.cdiv(r, tb)
    assert nsteps <= 2 * nh and 4 * 3 + r <= 2 * nh * tb
    x, y, c = _place()
    place = jnp.stack([c, 4 * (2 * x + y)]).astype(jnp.int32)

    def body(p_ref, w_ref, mc_ref, sc_ref, mn_ref, sn_ref, m_ref, v_ref, go_ref, d_ref, mo_ref, vo_ref, buf):
        i = pl.program_id(0)
        for at, blk, mine_ref, sib_ref in ((0, i, mc_ref, sc_ref), (1, jnp.minimum(i + 1, 2 * nh - 1), mn_ref, sn_ref)):
            rows = jnp.where(blk // nh == p_ref[0], mine_ref[...], sib_ref[...])
            buf[tb * at:tb * (at + 1)] = rows.reshape(tb, sub, lanes)
        g = buf[pl.ds(p_ref[1], tb)]
        go_ref[...] = g
        d_ref[...], mo_ref[...], vo_ref[...] = _adamw_math(w_ref[...], g, m_ref[...], v_ref[...])

    def half_spec(ahead, sibling):
        def index(i, pr):
            half = (1 - pr[0]) if sibling else pr[0]
            return (jnp.clip(jnp.minimum(i + ahead, 2 * nh - 1) - nh * half, 0, nh - 1), 0)
        return pl.BlockSpec((tb, sub * lanes), index)

    spec = pl.BlockSpec((tb, sub, lanes), lambda i, pr: (i, 0, 0))
    return pl.pallas_call(
        body, name=name,
        grid_spec=pltpu.PrefetchScalarGridSpec(
            num_scalar_prefetch=1, grid=(nsteps,),
            in_specs=[spec, half_spec(0, False), half_spec(0, True), half_spec(1, False), half_spec(1, True),
                      spec, spec],
            out_specs=[spec] * 4,
            scratch_shapes=[pltpu.VMEM((2 * tb, sub, lanes), F32)]),
        out_shape=[jax.ShapeDtypeStruct(w.shape, F32)] * 4,
        compiler_params=_params(("parallel",)),
    )(place, w, g_mine, g_sib, g_mine, g_sib, m, v)


def _adamw_halves(w, g_mine, g_sib, m, v, name):
    r, c_ = w.shape
    r2 = g_mine.shape[0]
    tb = _row_block(r2)
    nb = r2 // tb
    c = lax.axis_index("c")

    def body(c_ref, w_ref, gm_ref, gs_ref, m_ref, v_ref, g_ref, d_ref, mo_ref, vo_ref):
        g = jnp.where(pl.program_id(0) == c_ref[0], gm_ref[...], gs_ref[...])
        g_ref[...] = g
        d_ref[...], mo_ref[...], vo_ref[...] = _adamw_math(w_ref[...], g, m_ref[...], v_ref[...])

    full = pl.BlockSpec((tb, c_), lambda h, i, cr: (h * nb + i, 0))
    half = pl.BlockSpec((tb, c_), lambda h, i, cr: (i, 0))
    return pl.pallas_call(
        body, name=name,
        grid_spec=pltpu.PrefetchScalarGridSpec(
            num_scalar_prefetch=1, grid=(2, nb),
            in_specs=[full, half, half, full, full], out_specs=[full] * 4),
        out_shape=[jax.ShapeDtypeStruct(w.shape, F32)] * 4,
        compiler_params=_params(("parallel", "parallel")),
    )(jnp.reshape(c, (1,)).astype(jnp.int32), w, g_mine, g_sib, m, v)


def kernel(x, meta_tokens, norm_g, w_in, b_f, w_out, final_g, loss_target, m_meta_tokens, m_norm_g, m_w_in, m_b_f, m_w_out, m_final_g, v_meta_tokens, v_norm_g, v_w_in, v_b_f, v_w_out, v_final_g):
    me_s = 2 * lax.axis_index("x") + lax.axis_index("y")
    w3, m3, v3 = [jnp.transpose(jnp.reshape(t[0], (D // C, C, WSH)), (2, 0, 1)) for t in (w_in, m_w_in, v_w_in)]

    wt_main, laps, _, *normed = _gather_weights(_own_window(w3), meta_tokens, x[0], norm_g)
    wft = jnp.pad(laps[3, :NFF], ((0, C - NFF), (0, 0)))
    mine = (jnp.arange(4) == me_s)[:, None, None]
    wout_own = w_out[0].astype(BF)

    def wout_full(landed):
        return jnp.where(mine, wout_own[None], landed).reshape(DMIX, D)

    loss, gx, dmeta, dng, gwt, dbf, dwout, dfg, (p_in, p_out), (e_in, e_out) = _local_step(
        x[0], loss_target[0], normed, norm_g, wt_main, wft, b_f, wout_own, final_g.reshape(1, D), True, wout_full)

    g_meta = jnp.stack([dmeta[:, 256 * s:256 * (s + 1)] for s in range(4)])
    small = jnp.concatenate([dng, dfg, jnp.pad(dbf, ((0, 0), (0, D - NFF))),
                             jnp.pad(jnp.reshape(loss, (1, 1)), ((0, 0), (0, D - 1))),
                             jnp.zeros((4, D), F32)], axis=0)
    e_meta, e_small = _chip_exchange([g_meta], small)
    h_in, h_out = _sum4(e_in, p_in, "sum_in"), _sum4(e_out, p_out, "sum_out")
    h_meta, h_small = _sum4(e_meta, g_meta, "sum_meta"), _sum4(e_small, small, "sum_small")
    s_in, s_out, s_meta, s_small = _pair_send([h_in, h_out, h_meta, h_small])
    gw_meta = _add2(h_meta, s_meta, "pair_add_meta")
    tot = _add2(h_small, s_small, "pair_add_small")
    g_norm, g_final, g_bf, loss_all = tot[0:1], tot[1], tot[2:3, :NFF], tot[3, 0]

    d_meta, nm_meta, nv_meta = _adamw(meta_tokens, gw_meta, m_meta_tokens, v_meta_tokens, "adamw_meta")
    d_norm, nm_norm, nv_norm = _adamw(norm_g, g_norm, m_norm_g, v_norm_g, "adamw_norm")
    outs_in = _adamw_rows(w3, h_in, s_in, m3, v3, "adamw_in")
    gw_in, d_in, nm_in, nv_in = [jnp.reshape(jnp.transpose(t, (1, 2, 0)), (1, D, WSH)) for t in outs_in]
    d_bf, nm_bf, nv_bf = _adamw(b_f, g_bf, m_b_f, v_b_f, "adamw_bf")
    gw_out, d_out, nm_out, nv_out = _adamw_halves(w_out[0], h_out, s_out, m_w_out[0], v_w_out[0], "adamw_out")
    d_fin, nm_fin, nv_fin = _adamw(final_g.reshape(1, D), g_final.reshape(1, D), m_final_g.reshape(1, D),
                                   v_final_g.reshape(1, D), "adamw_final")
    return (loss_all, gx[None], gw_meta, g_norm, gw_in, g_bf, gw_out[None], g_final,
            d_meta, d_norm, d_in, d_bf, d_out[None], d_fin.reshape(D),
            nm_meta, nm_norm, nm_in, nm_bf, nm_out[None], nm_fin.reshape(D),
            nv_meta, nv_norm, nv_in, nv_bf, nv_out[None], nv_fin.reshape(D))
```

```python
import numpy as np
import jax
import jax.numpy as jnp
from jax import lax
from jax.experimental import pallas as pl
from jax.experimental.pallas import tpu as pltpu

D = 1024
SEQ = 2048
NMETA = 16
C = 128
PAD = C - NMETA
T = PAD + NMETA + SEQ
NCH = T // C
RH, RDK, RDV = 4, 128, 256
FH, FD = 16, 64
NPAIR = FH // 2
WMAIN = 7168
NFF = 16
WIN = WMAIN + NFF
WSH = WIN // 4
WPADROWS = 1824
DMIX = 2048
EPS = 1e-6
NEG = -1e30
RSCALE = RDK ** -0.5
FSCALE = FD ** -0.5
ROPE_BASE = 10000.0
LR, B1, B2, AEPS, WD, STEP = 0.001, 0.9, 0.999, 1e-08, 0.01, 10

BF = jnp.bfloat16
F32 = jnp.float32
NT = (((1,), (1,)), ((), ()))
TN = (((0,), (0,)), ((), ()))
NN_DIMS = (((1,), (0,)), ((), ()))
MESH = pl.DeviceIdType.MESH
ANY = pl.BlockSpec(memory_space=pl.ANY)
VMEM_LIMIT = 48 * 1024 * 1024
DW_VMEM_LIMIT = 56 * 1024 * 1024

GB_R, GB_F = 2, 6
QB_F, KB_F, VB_F = 24, 32, 40


def _dot(a, b):
    return jnp.dot(a, b, preferred_element_type=F32)


def _dg(a, b, dims):
    return lax.dot_general(a, b, dims, preferred_element_type=F32)


def _params(sem=None):
    return pltpu.CompilerParams(dimension_semantics=sem, vmem_limit_bytes=VMEM_LIMIT)


def _constants():
    pos = jnp.arange(T, dtype=F32) - PAD
    inv = ROPE_BASE ** (-jnp.arange(0, RDK, 2, dtype=F32) / RDK)
    ang = pos[:, None] * inv[None, :]
    cos, sin = jnp.cos(ang), jnp.sin(ang)
    cos2 = jnp.concatenate([cos, cos], axis=1)
    sin2 = jnp.concatenate([-sin, sin], axis=1)
    log_gamma = jnp.log1p(-jnp.exp2(-5.0 - jnp.arange(RH, dtype=F32)))
    idx = jnp.arange(C, dtype=F32)
    diff = idx[:, None] - idx[None, :]
    dmask = jnp.where(diff[None] >= 0, jnp.exp(log_gamma[:, None, None] * jnp.maximum(diff, 0.0)[None]), 0.0)
    zeta = jnp.exp(log_gamma[:, None] * (C - 1.0 - idx)[None, :])
    xi = jnp.exp(log_gamma[:, None] * (idx + 1.0)[None, :])
    gdec = jnp.exp(log_gamma * C)
    zeta_b = jnp.broadcast_to(zeta[:, :, None], (RH, C, RDK))
    xi_b = jnp.broadcast_to(xi[:, :, None], (RH, C, RDK))
    gdec_b = jnp.broadcast_to(gdec[:, None, None], (RH, RDK, RDV))
    tri = jnp.asarray(np.tril(np.ones((C, C), np.float32)), dtype=BF)
    head_of_lane = np.arange(FH * FD) // FD
    pick = ((np.arange(FH * FD)[:, None] % FD == 0)
            & (head_of_lane[:, None] == np.arange(C)[None, :])).astype(np.float32)
    seg = (np.arange(C)[:, None] // FD == np.arange(C)[None, :] // FD).astype(np.float32)
    ones_aug = np.concatenate([np.tile((np.arange(C) < FD)[None, :], (C, 1)),
                               np.tile((np.arange(C) >= FD)[None, :], (C, 1))], axis=0).astype(np.float32)
    lane = np.arange(2 * C) % C
    causal = np.where(lane[None, :] <= np.arange(C)[:, None], 0.0, NEG).astype(np.float32)
    mask_bias = np.stack([np.zeros((C, 2 * C), np.float32), causal])
    return dict(cos2=cos2, sin2=sin2, dmask=dmask, zeta=zeta_b, xi=xi_b, gdec=gdec_b, tri=tri,
                mask_bias=jnp.asarray(mask_bias), pick=jnp.asarray(pick, dtype=BF), seg=jnp.asarray(seg, dtype=BF),
                ones_aug=jnp.asarray(ones_aug, dtype=BF))


def _norm_rows(h, g):
    return h * lax.rsqrt(jnp.mean(h * h, axis=1, keepdims=True) + EPS) * g


def _mm_nt(a, b, n, tm, tn, name):
    m, k = a.shape

    def body(a_ref, b_ref, o_ref):
        o_ref[...] = _dg(a_ref[...], b_ref[...], NT)

    return pl.pallas_call(
        body, name=name, grid=(m // tm, n // tn),
        in_specs=[pl.BlockSpec((tm, k), lambda i, j: (i, 0)), pl.BlockSpec((tn, k), lambda i, j: (j, 0))],
        out_specs=pl.BlockSpec((tm, tn), lambda i, j: (i, j)),
        out_shape=jax.ShapeDtypeStruct((m, n), F32),
        compiler_params=_params(("parallel", "parallel")),
    )(a, b)


def _mm_nn(a, b, tm, tn, name, out_dtype=F32):
    m, k = a.shape
    _, n = b.shape

    def body(a_ref, b_ref, o_ref):
        o_ref[...] = _dot(a_ref[...], b_ref[...]).astype(out_dtype)

    return pl.pallas_call(
        body, name=name, grid=(m // tm, n // tn),
        in_specs=[pl.BlockSpec((tm, k), lambda i, j: (i, 0)), pl.BlockSpec((k, tn), lambda i, j: (0, j))],
        out_specs=pl.BlockSpec((tm, tn), lambda i, j: (i, j)),
        out_shape=jax.ShapeDtypeStruct((m, n), out_dtype),
        compiler_params=_params(("parallel", "parallel")),
    )(a, b)


def _rot(x, cos2, sin2):
    return x * cos2 + pltpu.roll(x, 64, 1) * sin2


def _ret_specs(chunk):
    whole = lambda shape: pl.BlockSpec(shape, lambda n: (0,) * len(shape))
    return [
        pl.BlockSpec((C, RH * RDK), lambda n: (chunk(n), 0)),
        pl.BlockSpec((C, RH * RDK), lambda n: (chunk(n), 1)),
        pl.BlockSpec((C, RH * RDV), lambda n: (chunk(n), 1)),
        pl.BlockSpec((C, RDK), lambda n: (chunk(n), 0)),
        pl.BlockSpec((C, RDK), lambda n: (chunk(n), 0)),
        whole((RH, C, C)), whole((RH, C, RDK)), whole((RH, C, RDK)), whole((RH, RDK, RDV)),
    ]


def _ret_heads(q_ref, k_ref, v_ref, cos, sin):
    qr = [_rot(q_ref[:, RDK * h:RDK * (h + 1)], cos, sin) for h in range(RH)]
    kr = [_rot(k_ref[:, RDK * h:RDK * (h + 1)], cos, sin) * RSCALE for h in range(RH)]
    vb = [v_ref[:, RDV * h:RDV * (h + 1)].astype(BF) for h in range(RH)]
    return qr, kr, [t.astype(BF) for t in qr], [t.astype(BF) for t in kr], vb


def _ret_fwd(z, cst):
    def body(q_ref, k_ref, v_ref, cos_ref, sin_ref, dm_ref, xi_ref, zt_ref, gd_ref, r_ref, sp_ref, st):
        n = pl.program_id(0)

        @pl.when(n == 0)
        def _():
            st[...] = jnp.zeros_like(st)

        hs = range(RH)
        qr, kr, qb, kb, vb = _ret_heads(q_ref, k_ref, v_ref, cos_ref[...], sin_ref[...])
        sd = [(_dg(qb[h], kb[h], NT) * dm_ref[h]).astype(BF) for h in hs]
        state = [st[h] for h in hs]
        qx = [(qr[h] * xi_ref[h]).astype(BF) for h in hs]
        kz = [(kr[h] * zt_ref[h]).astype(BF) for h in hs]
        out = [_dot(sd[h], vb[h]) + _dot(qx[h], state[h].astype(BF)) for h in hs]
        kv = [_dg(kz[h], vb[h], TN) for h in hs]
        for h in hs:
            sp_ref[0, h] = state[h]
            r_ref[:, RDV * h:RDV * (h + 1)] = out[h]
            st[h] = state[h] * gd_ref[h] + kv[h]

    return pl.pallas_call(
        body, name="ret_fwd", grid=(NCH,),
        in_specs=_ret_specs(lambda n: n),
        out_specs=[pl.BlockSpec((C, RH * RDV), lambda n: (n, 0)),
                   pl.BlockSpec((1, RH, RDK, RDV), lambda n: (n, 0, 0, 0))],
        out_shape=[jax.ShapeDtypeStruct((T, RH * RDV), F32), jax.ShapeDtypeStruct((NCH, RH, RDK, RDV), F32)],
        scratch_shapes=[pltpu.VMEM((RH, RDK, RDV), F32)],
        compiler_params=_params(("arbitrary",)),
    )(z, z, z, cst["cos2"], cst["sin2"], cst["dmask"], cst["xi"], cst["zeta"], cst["gdec"])


def _ret_bwd(z, cst, sprev, dr):
    def body(q_ref, k_ref, v_ref, cos_ref, sin_ref, dm_ref, xi_ref, zt_ref, gd_ref, sp_ref, dr_ref,
             dq_ref, dk_ref, dv_ref, gst):
        i = pl.program_id(0)

        @pl.when(i == 0)
        def _():
            gst[...] = jnp.zeros_like(gst)

        hs = range(RH)
        cos, sin = cos_ref[...], sin_ref[...]
        qr, kr, qb, kb, vb = _ret_heads(q_ref, k_ref, v_ref, cos, sin)
        dm = [dm_ref[h] for h in hs]
        xi = [xi_ref[h] for h in hs]
        zt = [zt_ref[h] for h in hs]
        sd = [(_dg(qb[h], kb[h], NT) * dm[h]).astype(BF) for h in hs]
        qx = [(qr[h] * xi[h]).astype(BF) for h in hs]
        kz = [(kr[h] * zt[h]).astype(BF) for h in hs]
        drb = [dr_ref[:, RDV * h:RDV * (h + 1)] for h in hs]
        sb = [sp_ref[0, h].astype(BF) for h in hs]
        g = [gst[h] for h in hs]
        gb = [t.astype(BF) for t in g]
        ds = [(_dg(drb[h], vb[h], NT) * dm[h]).astype(BF) for h in hs]
        dq = [_dot(ds[h], kb[h]) + _dg(drb[h], sb[h], NT) * xi[h] for h in hs]
        dk = [(_dg(ds[h], qb[h], TN) + _dg(vb[h], gb[h], NT) * zt[h]) * RSCALE for h in hs]
        dv = [_dg(sd[h], drb[h], TN) + _dot(kz[h], gb[h]) for h in hs]
        gn = [g[h] * gd_ref[h] + _dg(qx[h], drb[h], TN) for h in hs]
        for h in hs:
            gst[h] = gn[h]
            dq_ref[:, RDK * h:RDK * (h + 1)] = (dq[h] * cos + pltpu.roll(dq[h] * sin, 64, 1)).astype(BF)
            dk_ref[:, RDK * h:RDK * (h + 1)] = (dk[h] * cos + pltpu.roll(dk[h] * sin, 64, 1)).astype(BF)
            dv_ref[:, RDV * h:RDV * (h + 1)] = dv[h].astype(BF)

    rev = lambda n: NCH - 1 - n
    return pl.pallas_call(
        body, name="ret_bwd", grid=(NCH,),
        in_specs=_ret_specs(rev) + [
            pl.BlockSpec((1, RH, RDK, RDV), lambda n: (rev(n), 0, 0, 0)),
            pl.BlockSpec((C, RH * RDV), lambda n: (rev(n), 0)),
        ],
        out_specs=[pl.BlockSpec((C, RH * RDK), lambda n: (rev(n), 0)),
                   pl.BlockSpec((C, RH * RDK), lambda n: (rev(n), 0)),
                   pl.BlockSpec((C, RH * RDV), lambda n: (rev(n), 0))],
        out_shape=[jax.ShapeDtypeStruct((T, RH * RDK), BF), jax.ShapeDtypeStruct((T, RH * RDK), BF),
                   jax.ShapeDtypeStruct((T, RH * RDV), BF)],
        scratch_shapes=[pltpu.VMEM((RH, RDK, RDV), F32)],
        compiler_params=_params(("arbitrary",)),
    )(z, z, z, cst["cos2"], cst["sin2"], cst["dmask"], cst["xi"], cst["zeta"], cst["gdec"], sprev, dr)


def _place():
    x, y, c = lax.axis_index("x"), lax.axis_index("y"), lax.axis_index("c")
    return x, y, c


def _other_chips(x, y):
    return [(1 - x, y, 2 * (1 - x) + y), (x, 1 - y, 2 * x + (1 - y)), (1 - x, 1 - y, 2 * (1 - x) + (1 - y))]


def _chip_copies(srcs, lands, send_sems, recv_sems, by_dest):
    x, y, c = _place()
    me_s = 2 * x + y
    return [pltpu.make_async_remote_copy(
        src_ref=src.at[cs] if by_dest else src, dst_ref=land.at[me_s],
        send_sem=send_sems.at[3 * a + j], recv_sem=recv_sems.at[3 * a + j],
        device_id=(cx, cy, c), device_id_type=MESH)
        for a, (src, land) in enumerate(zip(srcs, lands)) for j, (cx, cy, cs) in enumerate(_other_chips(x, y))]


def _split_dot(x, mat01, dims=NN_DIMS, x_first=True):
    acc, rest = None, x
    for _ in range(3):
        piece = rest.astype(BF)
        part = _dg(piece, mat01, dims) if x_first else _dg(mat01, piece, dims)
        acc = part if acc is None else acc + part
        rest = rest - piece.astype(F32)
    return acc


def _log_sigmoid(x):
    return -(jnp.maximum(-x, 0.0) + jnp.log1p(jnp.exp(-jnp.abs(x))))


def _fox_prep(zf, bf_pad, cst):
    def body(zf_ref, b_ref, tri_ref, ct_ref, carry):
        n = pl.program_id(0)

        @pl.when(n == 0)
        def _():
            carry[...] = jnp.zeros_like(carry)

        ls = _log_sigmoid(zf_ref[...] + b_ref[...])
        row = n * C + lax.broadcasted_iota(jnp.int32, (C, C), 0)
        lf = jnp.where(row >= PAD, ls, 0.0)
        cc = _split_dot(lf, tri_ref[...], x_first=False) + carry[0:1, :]
        carry[...] = jnp.broadcast_to(cc[C - 1:C, :], carry.shape)
        pos = n * C + lax.broadcasted_iota(jnp.int32, (FH, C), 1)
        ct_ref[0] = jnp.where(pos >= PAD, cc.T[:FH, :], -NEG)

    return pl.pallas_call(
        body, name="fox_prep", grid=(NCH,),
        in_specs=[pl.BlockSpec((C, C), lambda n: (n, 0)), pl.BlockSpec((1, C), lambda n: (0, 0)),
                  pl.BlockSpec((C, C), lambda n: (0, 0))],
        out_specs=pl.BlockSpec((1, FH, C), lambda n: (n, 0, 0)),
        out_shape=jax.ShapeDtypeStruct((NCH, FH, C), F32),
        scratch_shapes=[pltpu.VMEM((8, C), F32)],
        compiler_params=_params(("arbitrary",)),
    )(zf, bf_pad, cst["tri"])


def _lo_lanes(shape):
    return lax.broadcasted_iota(jnp.int32, shape, 1) < FD


def _split_heads(x):
    lo = _lo_lanes(x.shape)
    zero = jnp.zeros_like(x)
    return jnp.concatenate([jnp.where(lo, x, zero), jnp.where(lo, zero, x)], axis=0)


def _spread2(x):
    lo = _lo_lanes(x.shape)
    r = pltpu.roll(x, FD, 1)
    return jnp.concatenate([jnp.where(lo, x, r), jnp.where(lo, r, x)], axis=1)


NSTEP = (NCH + 1) // 2
NTILE = NCH + 1
TROWS = T + C


def _fox_tile(s, t):
    second = t > s
    return second.astype(jnp.int32), jnp.where(second, t - s - 1, s - t)


def _fox_pos(i):
    return jnp.where(i < NSTEP, 2 * i, 2 * (NCH - 1 - i) + 1)


def _fox_pair_columns():
    return pl.BlockSpec((TROWS, C), lambda p, s: (0, p))


def _fox_key_bias(ct_ref, p, j):
    return jnp.concatenate([ct_ref[j, pl.ds(2 * p, 1), :], ct_ref[j, pl.ds(2 * p + 1, 1), :]], axis=1)


def _fox_columns(cols, sems, p):
    def copies(pair, slot):
        return [pltpu.make_async_copy(
            src.at[pl.ds(0, buf.shape[1]), pl.ds(pl.multiple_of((first + pair) * C, C), C)], buf.at[slot],
            sems.at[i, slot]) for i, (src, first, buf) in enumerate(cols)]

    @pl.when(p == 0)
    def _():
        for cp in copies(0, 0):
            cp.start()

    for cp in copies(p, p % 2):
        cp.wait()

    @pl.when(p + 1 < NPAIR)
    def _():
        for cp in copies(p + 1, 1 - p % 2):
            cp.start()


def _rows(block, size=C):
    return pl.ds(pl.multiple_of(block * size, size), size)


def _fox_fwd(z, ct, cst, share):
    n = 0 if share is None else 1

    def body(z_ref, ct_ref, ones_ref, mb_ref, *rest):
        share_refs, (a_ref, g_ref), land_refs = rest[:n], rest[n:n + 2], rest[n + 2:2 * n + 2]
        kks, vvs, q2, m2, sbuf, qbuf, kbuf, vbuf, col_sems = rest[2 * n + 2:2 * n + 11]
        p, s = pl.program_id(0), pl.program_id(1)
        slot = p % 2
        if n:
            send_sems, recv_sems, own_sem = rest[2 * n + 11:]
            x, y, _ = _place()
            copies = _chip_copies(share_refs, land_refs, send_sems, recv_sems, by_dest=False)
            copies.append(pltpu.make_async_copy(share_refs[0], land_refs[0].at[2 * x + y], own_sem.at[0]))

            @pl.when((p == 0) & (s == 0))
            def _():
                for cp in copies:
                    cp.start()

            @pl.when((p == NPAIR - 1) & (s == NSTEP - 1))
            def _():
                for cp in copies:
                    cp.wait()

        @pl.when(s == 0)
        def _():
            ones = ones_ref[...]
            _fox_columns([(z_ref, QB_F, qbuf), (z_ref, KB_F, kbuf), (z_ref, VB_F, vbuf)], col_sems, p)

            def prep(j, carry):
                kks[j] = _split_heads(kbuf[slot, _rows(j), :]).astype(BF)
                vvs[j] = jnp.concatenate([_split_heads(vbuf[slot, _rows(j), :]).astype(BF), ones], axis=1)
                return carry

            lax.fori_loop(0, NCH, prep, 0)

        q2[0] = (qbuf[slot, _rows(s), :] * FSCALE).astype(BF)
        q2[1] = (qbuf[slot, _rows(NCH - 1 - s), :] * FSCALE).astype(BF)

        tiles = [_fox_tile(s, t) for t in range(NTILE)]
        causal = mb_ref[1]
        neg = jnp.full((C, 2 * C), NEG, F32)
        run, first = neg, neg
        for t, (sel, j) in enumerate(tiles):
            st = _dg(q2[sel], kks[j], NT) - _fox_key_bias(ct_ref, p, j)
            if t in (0, NTILE - 1):
                st = st + causal
            sbuf[t] = st
            run = jnp.maximum(jnp.where(t == s + 1, neg, run), st)
            first = jnp.where(t == s, run, first)
        for w, mx in enumerate((first, run)):
            m2[w] = jnp.concatenate(
                [jnp.broadcast_to(jnp.max(mx[:, :C], axis=1, keepdims=True), (C, C)),
                 jnp.broadcast_to(jnp.max(mx[:, C:], axis=1, keepdims=True), (C, C))], axis=1)

        zero = jnp.zeros((C, 2 * C), F32)
        run, first = zero, zero
        for t, (sel, j) in enumerate(tiles):
            run = jnp.where(t == s + 1, zero, run) + _dot(jnp.exp(sbuf[t] - m2[sel]).astype(BF), vvs[j])
            first = jnp.where(t == s, run, first)
        lo = _lo_lanes((C, C))
        for w, res in enumerate((first, run)):
            l = res[:, C:]
            a_ref[_rows(2 * s + w), :] = res[:, :C] / l
            mw = m2[w]
            g_ref[_rows(2 * s + w), :] = -(jnp.where(lo, mw[:, :C], mw[:, C:]) + jnp.log(l))

    col = _fox_pair_columns()
    return pl.pallas_call(
        body, name="fox_fwd", grid=(NPAIR, NSTEP),
        in_specs=[ANY,
                  pl.BlockSpec((NCH, FH, C), lambda p, s: (0, 0, 0)),
                  pl.BlockSpec((2 * C, C), lambda p, s: (0, 0)),
                  pl.BlockSpec((2, C, 2 * C), lambda p, s: (0, 0, 0))] + [ANY] * n,
        out_specs=[col, col] + [ANY] * n,
        out_shape=[jax.ShapeDtypeStruct((TROWS, FH * FD), F32)] * 2
        + ([jax.ShapeDtypeStruct((4,) + share.shape, share.dtype)] if n else []),
        scratch_shapes=[pltpu.VMEM((NCH, 2 * C, C), BF), pltpu.VMEM((NCH, 2 * C, 2 * C), BF),
                        pltpu.VMEM((2, C, C), BF), pltpu.VMEM((2, C, 2 * C), F32),
                        pltpu.VMEM((NTILE, C, 2 * C), F32),
                        pltpu.VMEM((2, T, C), F32), pltpu.VMEM((2, T, C), F32), pltpu.VMEM((2, T, C), F32),
                        pltpu.SemaphoreType.DMA((3, 2))]
        + [pltpu.SemaphoreType.DMA((3,)), pltpu.SemaphoreType.DMA((3,)), pltpu.SemaphoreType.DMA((1,))] * n,
        compiler_params=_params(("arbitrary", "arbitrary")),
    )(z, ct, cst["ones_aug"], cst["mask_bias"], *([share] * n))


def _fox_bwd(z, da, g, delta, ct, cst, parts=()):
    grp = 9

    n = len(parts)

    def body(z_ref, da_ref, g_ref, dl_ref, ct_ref, ones_ref, mb_ref, *rest):
        part_refs, (dq_ref, dr_ref, dk_ref, dv_ref, dcs_ref), land_refs = rest[:n], rest[n:n + 5], rest[n + 5:2 * n + 5]
        (kks, vvs, q2, qq2, dd2, da2, gi2, dl2, dq2, dvb, dkb, dkacc, dvacc, csacc, qbuf, kbuf, vbuf, dabuf, gbuf,
         dlbuf, col_sems) = rest[2 * n + 5:2 * n + 26]
        p, s = pl.program_id(0), pl.program_id(1)
        slot = p % 2
        ones = ones_ref[...]
        if n:
            copies = _chip_copies(part_refs, land_refs, *rest[2 * n + 26:], by_dest=True)

            @pl.when((p == 0) & (s == 0))
            def _():
                for cp in copies:
                    cp.start()

            @pl.when((p == NPAIR - 1) & (s == NSTEP - 1))
            def _():
                for cp in copies:
                    cp.wait()

        @pl.when(s == 0)
        def _():
            dkacc[...] = jnp.zeros_like(dkacc)
            dvacc[...] = jnp.zeros_like(dvacc)
            csacc[...] = jnp.zeros_like(csacc)
            _fox_columns([(z_ref, QB_F, qbuf), (z_ref, KB_F, kbuf), (z_ref, VB_F, vbuf), (da_ref, 0, dabuf),
                          (g_ref, 0, gbuf), (dl_ref, 0, dlbuf)], col_sems, p)

            def prep(j, carry):
                kks[j] = _split_heads(kbuf[slot, _rows(j), :]).astype(BF)
                vvs[j] = _split_heads(vbuf[slot, _rows(j), :]).astype(BF)
                return carry

            lax.fori_loop(0, NCH, prep, 0)

        for w, (chunk, blk) in enumerate(((s, 2 * s), (NCH - 1 - s, jnp.where(s == NSTEP - 1, 2 * s, 2 * s + 1)))):
            qf = qbuf[slot, _rows(chunk), :]
            q2[w] = (qf * FSCALE).astype(BF)
            qq2[w] = jnp.concatenate([_split_heads(qf).astype(BF), ones], axis=1)
            da2[w] = dabuf[slot, _rows(blk), :]
            dd2[w] = _split_heads(da2[w].astype(F32)).astype(BF)
            gi2[w] = _spread2(gbuf[slot, _rows(blk), :])
            dl2[w] = _spread2(dlbuf[slot, _rows(blk), :])
        dq2[...] = jnp.zeros_like(dq2)
        zero = jnp.zeros((C, 2 * C), F32)

        def group(gi, carry):
            ts = [gi * grp + u for u in range(grp)]
            tiles = [_fox_tile(s, t) for t in ts]
            kk = [kks[j] for _, j in tiles]
            ss = [_dg(q2[sel], kj, NT) + (gi2[sel] - _fox_key_bias(ct_ref, p, j)) for kj, (sel, j) in zip(kk, tiles)]
            ss[0] = ss[0] + mb_ref[(gi == 0).astype(jnp.int32)]
            ss[-1] = ss[-1] + mb_ref[(gi == 1).astype(jnp.int32)]
            dps = [_dg(da2[sel], vvs[j], NT) for sel, j in tiles]
            pes = [jnp.exp(st) for st in ss]
            dss = [pe * (dp - dl2[sel]) * FSCALE for pe, dp, (sel, _) in zip(pes, dps, tiles)]
            pts = [jnp.concatenate([pe[:, :C].T, pe[:, C:].T], axis=1).astype(BF) for pe in pes]
            dsts = [jnp.concatenate([ds[:, :C].T, ds[:, C:].T], axis=1).astype(BF) for ds in dss]
            dvs = [_dot(pt, dd2[sel]) for pt, (sel, _) in zip(pts, tiles)]
            rs = [_dot(dst, qq2[sel]) for dst, (sel, _) in zip(dsts, tiles)]
            parts = [_dot(ds.astype(BF), jnp.concatenate([kj, ones], axis=1)) for ds, kj in zip(dss, kk)]
            for t, dv, rr in zip(ts, dvs, rs):
                dvb[t] = dv
                dkb[t] = rr
            pa, pb = zero, zero
            for t, part in zip(ts, parts):
                pa = pa + jnp.where(t <= s, part, zero)
                pb = pb + jnp.where(t <= s, zero, part)
            dq2[0] += pa
            dq2[1] += pb
            return carry

        ntile = jnp.where(s == NSTEP - 1, grp, NTILE)
        lax.fori_loop(0, ntile // grp, group, 0)

        def scatter(t, carry):
            _, j = _fox_tile(s, t)
            r = pl.ds(pl.multiple_of(j * C, C), C)
            dvacc[r, :] += dvb[t]
            dkacc[r, :] += dkb[t, :, :C]
            csacc[r, :] += dkb[t, :, C:]
            return carry

        lax.fori_loop(0, ntile, scatter, 0)
        for w, chunk in ((1, NCH - 1 - s), (0, s)):
            res = dq2[w]
            dq_ref[_rows(chunk), :] = res[:, :C].astype(BF)
            dr_ref[_rows(2 * s + w), :] = res[:, C:]

        @pl.when(s == NSTEP - 1)
        def _():
            dk_ref[...] = dkacc[...].astype(BF)
            dv_ref[...] = dvacc[...].astype(BF)
            dcs_ref[...] = csacc[...]

    both = _fox_pair_columns()
    col = pl.BlockSpec((T, C), lambda p, s: (0, p))
    return pl.pallas_call(
        body, name="fox_bwd", grid=(NPAIR, NSTEP),
        in_specs=[ANY] * 4
        + [pl.BlockSpec((NCH, FH, C), lambda p, s: (0, 0, 0)),
           pl.BlockSpec((2 * C, C), lambda p, s: (0, 0)),
           pl.BlockSpec((2, C, 2 * C), lambda p, s: (0, 0, 0))] + [ANY] * n,
        out_specs=[col, both, col, col, col] + [ANY] * n,
        out_shape=[jax.ShapeDtypeStruct((T, FH * FD), BF), jax.ShapeDtypeStruct((TROWS, FH * FD), F32),
                   jax.ShapeDtypeStruct((T, FH * FD), BF), jax.ShapeDtypeStruct((T, FH * FD), BF),
                   jax.ShapeDtypeStruct((T, FH * FD), F32)]
        + [jax.ShapeDtypeStruct(p.shape, p.dtype) for p in parts],
        scratch_shapes=[pltpu.VMEM((NCH, 2 * C, C), BF), pltpu.VMEM((NCH, 2 * C, C), BF),
                        pltpu.VMEM((2, C, C), BF), pltpu.VMEM((2, 2 * C, 2 * C), BF), pltpu.VMEM((2, 2 * C, C), BF),
                        pltpu.VMEM((2, C, C), BF), pltpu.VMEM((2, C, 2 * C), F32), pltpu.VMEM((2, C, 2 * C), F32),
                        pltpu.VMEM((2, C, 2 * C), F32),
                        pltpu.VMEM((NTILE, C, C), F32), pltpu.VMEM((NTILE, C, 2 * C), F32),
                        pltpu.VMEM((T, C), F32), pltpu.VMEM((T, C), F32), pltpu.VMEM((T, C), F32),
                        pltpu.VMEM((2, T, C), F32), pltpu.VMEM((2, T, C), F32), pltpu.VMEM((2, T, C), F32),
                        pltpu.VMEM((2, T, C), BF), pltpu.VMEM((2, T, C), F32), pltpu.VMEM((2, T, C), F32),
                        pltpu.SemaphoreType.DMA((6, 2))]
        + ([pltpu.SemaphoreType.DMA((3 * n,)), pltpu.SemaphoreType.DMA((3 * n,))] if n else []),
        compiler_params=_params(("arbitrary", "arbitrary")),
    )(z, da, g, delta, ct, cst["ones_aug"], cst["mask_bias"], *parts)


def _fox_gate_bwd(drow, dcol, zf, bf_pad, cst):
    def body(dr_ref, dc_ref, zf_ref, b_ref, tri_ref, pick_ref, dff_ref, db_ref, carry):
        s = pl.program_id(0)
        n = NCH - 1 - s

        @pl.when(s == 0)
        def _():
            carry[...] = jnp.zeros_like(carry)
            db_ref[...] = jnp.zeros_like(db_ref)

        dcb = _split_dot((dr_ref[...] - dc_ref[...]) * (1.0 / FSCALE), pick_ref[...])
        suf = _split_dot(dcb, tri_ref[...], TN, x_first=False) + carry[0:1, :]
        carry[...] = jnp.broadcast_to(suf[0:1, :], carry.shape)
        x = zf_ref[...] + b_ref[...]
        row = n * C + lax.broadcasted_iota(jnp.int32, (C, C), 0)
        dff = jnp.where(row >= PAD, suf * (1.0 - jax.nn.sigmoid(x)), 0.0)
        dff_ref[...] = dff.astype(BF)
        db_ref[...] += jnp.sum(dff, axis=0, keepdims=True)

    rev = lambda s: (NCH - 1 - s, 0)
    return pl.pallas_call(
        body, name="fox_gate_bwd", grid=(NCH,),
        in_specs=[pl.BlockSpec((C, FH * FD), lambda s: (_fox_pos(NCH - 1 - s), 0)),
                  pl.BlockSpec((C, FH * FD), rev), pl.BlockSpec((C, C), rev),
                  pl.BlockSpec((1, C), lambda s: (0, 0)), pl.BlockSpec((C, C), lambda s: (0, 0)),
                  pl.BlockSpec((FH * FD, C), lambda s: (0, 0))],
        out_specs=[pl.BlockSpec((C, C), rev), pl.BlockSpec((1, C), lambda s: (0, 0))],
        out_shape=[jax.ShapeDtypeStruct((T, C), BF), jax.ShapeDtypeStruct((1, C), F32)],
        scratch_shapes=[pltpu.VMEM((8, C), F32)],
        compiler_params=_params(("arbitrary",)),
    )(drow, dcol, zf, bf_pad, cst["tri"], cst["pick"])


def _head_norm(r):
    rn, rs = [], []
    for h in range(RH):
        rh = r[:, RDV * h:RDV * (h + 1)]
        s = lax.rsqrt(jnp.mean(rh * rh, axis=1, keepdims=True) + EPS)
        rn.append(rh * s)
        rs.append(s)
    return jnp.concatenate(rn, axis=1), rs


def _gated(r, rg, a, fg):
    rn, _ = _head_norm(r)
    return jnp.concatenate([rn * (rg * jax.nn.sigmoid(rg)), a * (fg * jax.nn.sigmoid(fg))], axis=1)


def _out_loss(r, z, a, wout, x, tgt, fgain):
    def body(r_ref, rg_ref, a_ref, fg_ref, w_ref, x_ref, t_ref, g_ref, yt_ref, do_ref, dob_ref, loss_ref, dg_ref):
        i = pl.program_id(0)

        @pl.when(i == 0)
        def _():
            yt_ref[...] = jnp.zeros_like(yt_ref)
            do_ref[...] = jnp.zeros_like(do_ref)
            dob_ref[...] = jnp.zeros_like(dob_ref)
            loss_ref[...] = jnp.zeros_like(loss_ref)
            dg_ref[...] = jnp.zeros_like(dg_ref)

        @pl.when(i > 0)
        def _():
            y = _gated(r_ref[...], rg_ref[...], a_ref[...], fg_ref[...])
            yt_ref[...] = y.T.astype(BF)
            o = x_ref[...] + _dot(y.astype(BF), w_ref[...])
            rs = lax.rsqrt(jnp.mean(o * o, axis=1, keepdims=True) + EPS)
            on = o * rs
            g = g_ref[...]
            e = on * g - t_ref[...]
            loss_ref[...] += 0.5 * jnp.sum(jnp.mean(e * e, axis=1, keepdims=True))
            dyh = e * (1.0 / D)
            dg_ref[...] += jnp.sum(dyh * on, axis=0, keepdims=True)
            don = dyh * g
            do = rs * (don - on * jnp.mean(don * on, axis=1, keepdims=True))
            do_ref[...] = do
            dob_ref[...] = do.astype(BF)

    tok = lambda i: (jnp.maximum(i - 1, 0), 0)
    return pl.pallas_call(
        body, name="out_loss", grid=(NCH,),
        in_specs=[pl.BlockSpec((C, D), lambda i: (i, 0)), pl.BlockSpec((C, D), lambda i: (i, GB_R)),
                  pl.BlockSpec((C, D), lambda i: (_fox_pos(i), 0)), pl.BlockSpec((C, D), lambda i: (i, GB_F)),
                  pl.BlockSpec((DMIX, D), lambda i: (0, 0)),
                  pl.BlockSpec((C, D), tok), pl.BlockSpec((C, D), tok), pl.BlockSpec((1, D), lambda i: (0, 0))],
        out_specs=[pl.BlockSpec((DMIX, C), lambda i: (0, i)), pl.BlockSpec((C, D), lambda i: (i, 0)),
                   pl.BlockSpec((C, D), lambda i: (i, 0)), pl.BlockSpec((8, C), lambda i: (0, 0)),
                   pl.BlockSpec((1, D), lambda i: (0, 0))],
        out_shape=[jax.ShapeDtypeStruct((DMIX, T), BF), jax.ShapeDtypeStruct((T, D), F32),
                   jax.ShapeDtypeStruct((T, D), BF), jax.ShapeDtypeStruct((8, C), F32),
                   jax.ShapeDtypeStruct((1, D), F32)],
        compiler_params=_params(("arbitrary",)),
    )(r, z, a, z, wout, x, tgt, fgain)


def _silu_and_grad(x):
    s = jax.nn.sigmoid(x)
    return x * s, s * (1.0 + x * (1.0 - s))


def _dy_gate_bwd(dob, wout, r, z, a, seg, swap=()):
    n = len(swap)

    def body(do_ref, w_ref, r_ref, rg_ref, a_ref, fg_ref, seg_ref, *rest):
        (dr_ref, da_ref, drg_ref, dfg_ref, dl_ref) = rest[n:n + 5]
        if n:
            copies = _pair_copies(rest[:n], rest[n + 5:2 * n + 5], *rest[2 * n + 5:], n)

            @pl.when(pl.program_id(0) == 0)
            def _():
                for cp in copies:
                    cp.start()

            @pl.when(pl.program_id(0) == NCH - 1)
            def _():
                for cp in copies:
                    cp.wait()

        dy = _dg(do_ref[...], w_ref[...], NT)
        a_ = a_ref[...]
        rn, rs = _head_norm(r_ref[...])
        silu_rg, dsilu_rg = _silu_and_grad(rg_ref[...])
        silu_fg, dsilu_fg = _silu_and_grad(fg_ref[...])
        dyr, dyf = dy[:, :D], dy[:, D:]
        drn = dyr * silu_rg
        drg_ref[...] = (dyr * rn * dsilu_rg).astype(BF)
        for h in range(RH):
            sl = slice(RDV * h, RDV * (h + 1))
            dh, nh = drn[:, sl], rn[:, sl]
            dr_ref[:, sl] = (rs[h] * (dh - nh * jnp.mean(dh * nh, axis=1, keepdims=True))).astype(BF)
        dab = (dyf * silu_fg).astype(BF)
        da_ref[...] = dab
        dfg_ref[...] = (dyf * a_ * dsilu_fg).astype(BF)
        prod = dab.astype(F32) * a_
        segm = seg_ref[...]
        for p in range(NPAIR):
            sl = slice(C * p, C * (p + 1))
            hi = prod[:, sl].astype(BF)
            lo = (prod[:, sl] - hi.astype(F32)).astype(BF)
            dl_ref[:, sl] = _dot(hi, segm) + _dot(lo, segm)

    row = pl.BlockSpec((C, D), lambda i: (i, 0))
    fox = pl.BlockSpec((C, D), lambda i: (_fox_pos(i), 0))
    return pl.pallas_call(
        body, name="dy_gate_bwd", grid=(NCH,),
        in_specs=[row, pl.BlockSpec((DMIX, D), lambda i: (0, 0)),
                  row, pl.BlockSpec((C, D), lambda i: (i, GB_R)),
                  fox, pl.BlockSpec((C, D), lambda i: (i, GB_F)),
                  pl.BlockSpec((C, C), lambda i: (0, 0))] + [ANY] * n,
        out_specs=[row, fox, row, row, fox] + [ANY] * n,
        out_shape=[jax.ShapeDtypeStruct((T, D), BF), jax.ShapeDtypeStruct((TROWS, D), BF),
                   jax.ShapeDtypeStruct((T, D), BF), jax.ShapeDtypeStruct((T, D), BF),
                   jax.ShapeDtypeStruct((TROWS, D), F32)]
        + [jax.ShapeDtypeStruct((4, s.shape[1] // 2, s.shape[2]), s.dtype) for s in swap],
        scratch_shapes=[pltpu.SemaphoreType.DMA((n,)), pltpu.SemaphoreType.DMA((n,))] if n else [],
        compiler_params=_params(("arbitrary",)),
    )(dob, wout, r, z, a, z, seg, *swap)


DZ_WIDTHS = (512, 512, 1024, 1024, 1024, 1024, 1024, 1024)


def _du_norm_bwd(dzs, dzf, wt, wft, hpad, g, dopad, parts=()):
    tm, tk = 544, 1024
    nk = WMAIN // tk
    ni = T // tm
    n = len(parts)

    def body(rq_ref, rk_ref, rv_ref, rg_ref, fq_ref, fk_ref, fv_ref, fg_ref, dzf_ref, w_ref, wf_ref, h_ref, g_ref,
             do_ref, *rest):
        part_refs, (gh_ref, dg_ref), land_refs = rest[:n], rest[n:n + 2], rest[n + 2:2 * n + 2]
        acc = rest[2 * n + 2]
        i, k = pl.program_id(0), pl.program_id(1)

        if n:
            send_sems, recv_sems = rest[2 * n + 3:]
            copies = _chip_copies(part_refs, land_refs, send_sems, recv_sems, by_dest=True)

            @pl.when((i == 0) & (k == 0))
            def _():
                for cp in copies:
                    cp.start()

            @pl.when((i == ni - 1) & (k == nk - 1))
            def _():
                for cp in copies:
                    cp.wait()

        @pl.when(k == 0)
        def _():
            acc[...] = (_dot(dzf_ref[...], wf_ref[...]) + _dot(rq_ref[...], w_ref[:512, :])
                        + _dot(rk_ref[...], w_ref[512:, :]))

        for kk, piece in enumerate((rv_ref, rg_ref, fq_ref, fk_ref, fv_ref, fg_ref), start=1):
            @pl.when(k == kk)
            def _(piece=piece):
                acc[...] += _dot(piece[...], w_ref[...])

        @pl.when(k == nk - 1)
        def _():
            du = acc[...]
            h = h_ref[...]
            gg = g_ref[...]
            rs = lax.rsqrt(jnp.mean(h * h, axis=1, keepdims=True) + EPS)
            hn = h * rs
            part = jnp.sum(du * hn, axis=0, keepdims=True)

            @pl.when(i == 0)
            def _():
                dg_ref[...] = part

            @pl.when(i > 0)
            def _():
                dg_ref[...] += part

            dhn = du * gg
            gh_ref[...] = rs * (dhn - hn * jnp.mean(dhn * hn, axis=1, keepdims=True)) + do_ref[...]

    sems = [pltpu.SemaphoreType.DMA((3 * n,)), pltpu.SemaphoreType.DMA((3 * n,))] if n else []
    return pl.pallas_call(
        body, name="du_norm_bwd", grid=(ni, nk),
        in_specs=[pl.BlockSpec((tm, w), lambda i, k: (i, 0)) for w in DZ_WIDTHS]
        + [pl.BlockSpec((tm, C), lambda i, k: (i, 0)),
           pl.BlockSpec((tk, D), lambda i, k: (k, 0)), pl.BlockSpec((C, D), lambda i, k: (0, 0)),
           pl.BlockSpec((tm, D), lambda i, k: (i, 0)), pl.BlockSpec((1, D), lambda i, k: (0, 0)),
           pl.BlockSpec((tm, D), lambda i, k: (i, 0))] + [ANY] * n,
        out_specs=[pl.BlockSpec((tm, D), lambda i, k: (i, 0)), pl.BlockSpec((1, D), lambda i, k: (0, 0))] + [ANY] * n,
        out_shape=[jax.ShapeDtypeStruct((T, D), F32), jax.ShapeDtypeStruct((1, D), F32)]
        + [jax.ShapeDtypeStruct(p.shape, p.dtype) for p in parts],
        scratch_shapes=[pltpu.VMEM((tm, D), F32)] + sems,
        compiler_params=_params(("arbitrary", "arbitrary")),
    )(*dzs, dzf, wt, wft, hpad, g, dopad, *parts)


GROWS = 7680


def _dw_in(dzs, dzf, ut):
    tn = 512
    nmain = WMAIN // tn
    first, blocks = [], []
    for w in DZ_WIDTHS:
        first.append(sum(blocks))
        blocks.append(w // tn)

    def body(rq_ref, rk_ref, rv_ref, rg_ref, fq_ref, fk_ref, fv_ref, fg_ref, dzf_ref, ut_ref, o_ref):
        gidx = pl.program_id(0)
        for piece, g0, nb in zip((rq_ref, rk_ref, rv_ref, rg_ref, fq_ref, fk_ref, fv_ref, fg_ref), first, blocks):
            @pl.when((gidx >= g0) & (gidx < g0 + nb))
            def _(piece=piece):
                o_ref[...] = _dot(ut_ref[...], piece[...]).T.astype(BF)

        @pl.when(gidx == nmain)
        def _():
            o_ref[:C, :] = _dot(ut_ref[...], dzf_ref[...]).T.astype(BF)
            o_ref[C:, :] = jnp.zeros((tn - C, D), BF)

    def piece_spec(g0, nb):
        return pl.BlockSpec((T, tn), lambda gidx: (0, jnp.clip(gidx - g0, 0, nb - 1)))

    return pl.pallas_call(
        body, name="dw_in", grid=(nmain + 1,),
        in_specs=[piece_spec(g0, nb) for g0, nb in zip(first, blocks)]
        + [pl.BlockSpec((T, C), lambda gidx: (0, 0)), pl.BlockSpec((D, T), lambda gidx: (0, 0))],
        out_specs=pl.BlockSpec((tn, D), lambda gidx: (gidx, 0)),
        out_shape=jax.ShapeDtypeStruct((GROWS, D), BF),
        compiler_params=pltpu.CompilerParams(dimension_semantics=("arbitrary",), vmem_limit_bytes=DW_VMEM_LIMIT),
    )(*dzs, dzf, ut)


def _local_step(x, tgt, normed, norm_g, wt, wft, b_f, wout, final_g, reduce_scatter=False, gather_wout=False):
    cst = _constants()
    hpad, u, ut = normed
    bf_pad = jnp.pad(b_f, ((0, 0), (0, C - NFF)))
    z = _mm_nt(u, wt, WMAIN, T // 2, 1024, "in_proj")
    zf = _mm_nt(u, wft, C, T // 2, C, "in_proj_ff")
    r, sprev = _ret_fwd(z, cst)
    ct = _fox_prep(zf, bf_pad, cst)
    if not gather_wout:
        a, g = _fox_fwd(z, ct, cst, None)
    else:
        a, g, landed_wout = _fox_fwd(z, ct, cst, wout)
        wout = landed_wout.reshape(DMIX, D)
    yt, dopad, dob, loss8, dfg = _out_loss(r, z, a, wout, x, tgt, final_g)
    dwout = _mm_nn(yt, dob, 512, D, "dw_out", BF)
    g_out = [dwout.reshape(4, DMIX // 4, D)] if reduce_scatter else []
    dr, da, dzrg, dzfg, delta, *r_out = _dy_gate_bwd(dob, wout, r, z, a, cst["seg"], g_out)
    p_out = [_add_halves(g_out[0], r_out[0], "pair_add_out", BF)] if reduce_scatter else []
    dzq_r, dzk_r, dzv_r = _ret_bwd(z, cst, sprev, dr)
    dzq_f, drow, dzk_f, dzv_f, dcol, *e_out = _fox_bwd(z, da, g, delta, ct, cst, p_out)
    dzf, dbf = _fox_gate_bwd(drow, dcol, zf, bf_pad, cst)
    dzs = [dzq_r, dzk_r, dzv_r, dzrg, dzq_f, dzk_f, dzv_f, dzfg]
    gwt = _dw_in(dzs, dzf, ut)
    p_in = [_swap_add_windows(gwt)[1]] if reduce_scatter else []
    gh, dng, *e_in = _du_norm_bwd(dzs, dzf, wt, wft, hpad, norm_g, dopad, p_in)
    return (loss8[0, 0], gh[C:], gh[PAD:C], dng, gwt, dbf[:, :NFF], dwout, dfg, p_in + p_out, e_in + e_out)


WOFF, WLEN = 1792, 2048
WHALF = WLEN // 2
LAP = WPADROWS - WOFF


def _own_window(w3):
    rows, sub, lanes = w3.shape
    pad = WPADROWS - rows
    tb = 96
    nb = WPADROWS // tb
    half = rows // 2

    def body(w_ref, o_ref, buf, sems):
        x, y, _ = _place()
        shift = 4 * (2 * x + y)
        buf[pl.ds(0, pad)] = jnp.zeros((pad, sub, lanes), F32)
        buf[pl.ds(rows, pad)] = jnp.zeros((pad, sub, lanes), F32)
        cps = [pltpu.make_async_copy(w_ref.at[pl.ds(half * h, half)], buf.at[pl.ds(shift + half * h, half)],
                                     sems.at[h]) for h in range(2)]
        for cp in cps:
            cp.start()

        def block(i, carry):
            r0 = pl.multiple_of(i * tb, tb)
            o_ref[pl.ds(r0, tb), :] = buf[pl.ds(r0, tb)].reshape(tb, sub * lanes).astype(BF)
            return carry

        cps[0].wait()
        lax.fori_loop(0, half // tb, block, 0)
        cps[1].wait()
        lax.fori_loop(half // tb, nb, block, 0)

    return pl.pallas_call(
        body, name="own_window",
        in_specs=[ANY], out_shape=jax.ShapeDtypeStruct((WPADROWS, sub * lanes), BF),
        scratch_shapes=[pltpu.VMEM((WPADROWS, sub, lanes), F32), pltpu.SemaphoreType.DMA((2,))],
        compiler_params=pltpu.CompilerParams(vmem_limit_bytes=VMEM_LIMIT),
    )(w3)


def _gather_weights(own_win, meta, x, norm_g):
    half_main, half_lap, half_meta = WOFF // 2, LAP // 2, meta.shape[0] // 2
    last = NCH - 1

    def body(win_ref, meta_ref, x_ref, g_ref, w_ref, laps_ref, gm_ref, h_ref, u_ref, ut_ref,
             send_sems, recv_sems, local_sems, stage, lapbuf, headbuf, metabuf):
        step = pl.program_id(0)
        x, y, c = _place()
        me_s = 2 * x + y
        sib = (x, y, 1 - c)
        chips = _other_chips(x, y)

        def emit(h):
            u = _norm_rows(h, g_ref[...])
            h_ref[...] = h
            u_ref[...] = u.astype(BF)
            ut_ref[...] = u.T.astype(BF)

        kinds = [
            (lambda h: win_ref.at[pl.ds(half_main * h, half_main)],
             lambda s, h: w_ref.at[pl.ds(WOFF * s + half_main * h, half_main)]),
            (lambda h: win_ref.at[pl.ds(WOFF + half_lap * h, half_lap)],
             lambda s, h: laps_ref.at[s, pl.ds(half_lap * h, half_lap)]),
            (lambda h: meta_ref.at[pl.ds(half_meta * h, half_meta)],
             lambda s, h: gm_ref.at[s, pl.ds(half_meta * h, half_meta)]),
        ]
        own_in = pltpu.make_async_copy(win_ref.at[pl.ds(0, WOFF)], stage, local_sems.at[0])
        own_lap_in = pltpu.make_async_copy(win_ref.at[pl.ds(WOFF, LAP)], lapbuf.at[0], local_sems.at[1])
        own_out = pltpu.make_async_copy(stage, w_ref.at[pl.ds(WOFF * me_s, WOFF)], local_sems.at[0])
        own_lap_out = pltpu.make_async_copy(lapbuf.at[0], laps_ref.at[me_s], local_sems.at[1])
        sends, arrivals, forwards, forwarded = [], [], [], []
        for a, (src, dst) in enumerate(kinds):
            for k, (cx, cy, cs) in enumerate(chips):
                there = dict(send_sem=send_sems.at[6 * a + k], recv_sem=recv_sems.at[6 * a + k],
                             device_id=(cx, cy, c), device_id_type=MESH)
                across = dict(send_sem=send_sems.at[6 * a + 3 + k], recv_sem=recv_sems.at[6 * a + 3 + k],
                              device_id=sib, device_id_type=MESH)
                sends.append(pltpu.make_async_remote_copy(src_ref=src(c), dst_ref=dst(me_s, c), **there))
                arrivals.append(pltpu.make_async_remote_copy(src_ref=dst(cs, c), dst_ref=dst(cs, c), **there))
                forwards.append(pltpu.make_async_remote_copy(src_ref=dst(cs, c), dst_ref=dst(cs, c), **across))
                forwarded.append(pltpu.make_async_remote_copy(
                    src_ref=dst(cs, 1 - c), dst_ref=dst(cs, 1 - c), **across))

        @pl.when(step == 0)
        def _():
            own_in.start()
            own_lap_in.start()
            for cp in sends:
                cp.start()
            own_in.wait()
            own_out.start()
            own_lap_in.wait()
            own_lap_out.start()

        @pl.when(step < last)
        def _():
            emit(x_ref[...])

        @pl.when(step == last)
        def _():
            for cp, fwd in zip(arrivals, forwards):
                cp.wait_recv()
                fwd.start()
            for cp in forwarded:
                cp.wait_recv()
            for cp in sends + forwards:
                cp.wait_send()
            own_out.wait()
            own_lap_out.wait()
            for s in range(1, 4):
                head = w_ref.at[pl.ds(WOFF * s, LAP)]
                loads = [pltpu.make_async_copy(laps_ref.at[s - 1], lapbuf.at[1], local_sems.at[2]),
                         pltpu.make_async_copy(head, headbuf, local_sems.at[3])]
                for cp in loads:
                    cp.start()
                for cp in loads:
                    cp.wait()
                headbuf[...] = (headbuf[...].astype(F32) + lapbuf[1].astype(F32)).astype(BF)
                store = pltpu.make_async_copy(headbuf, head, local_sems.at[3])
                store.start()
                store.wait()
            loads = [pltpu.make_async_copy(meta_ref, metabuf.at[me_s], local_sems.at[0])]
            loads += [pltpu.make_async_copy(gm_ref.at[cs], metabuf.at[cs], local_sems.at[1 + k])
                      for k, (_, _, cs) in enumerate(chips)]
            for cp in loads:
                cp.start()
            for cp in loads:
                cp.wait()
            tokens = jnp.concatenate([metabuf[s] for s in range(4)], axis=1)
            emit(jnp.concatenate([jnp.zeros((PAD, D), F32), tokens], axis=0))

    def chunk(i):
        return (i + 1) % NCH

    return pl.pallas_call(
        body, name="all_gather_w", grid=(NCH,),
        in_specs=[ANY, ANY, pl.BlockSpec((C, D), lambda i: (jnp.minimum(i, last - 1), 0)),
                  pl.BlockSpec((1, D), lambda i: (0, 0))],
        out_specs=[ANY] * 3 + [pl.BlockSpec((C, D), lambda i: (chunk(i), 0))] * 2
        + [pl.BlockSpec((D, C), lambda i: (0, chunk(i)))],
        out_shape=[jax.ShapeDtypeStruct((WMAIN, D), own_win.dtype), jax.ShapeDtypeStruct((4, LAP, D), own_win.dtype),
                   jax.ShapeDtypeStruct((4,) + meta.shape, meta.dtype),
                   jax.ShapeDtypeStruct((T, D), F32), jax.ShapeDtypeStruct((T, D), BF),
                   jax.ShapeDtypeStruct((D, T), BF)],
        scratch_shapes=[pltpu.SemaphoreType.DMA((18,)), pltpu.SemaphoreType.DMA((18,)), pltpu.SemaphoreType.DMA((4,)),
                        pltpu.VMEM((WOFF, D), own_win.dtype), pltpu.VMEM((2, LAP, D), own_win.dtype),
                        pltpu.VMEM((LAP, D), own_win.dtype), pltpu.VMEM((4,) + meta.shape, meta.dtype)],
        compiler_params=_params(("arbitrary",)),
    )(own_win, meta, x, norm_g)


def _pair_copies(ins, outs, send_sems, recv_sems, n):
    x, y, c = _place()
    sib = dict(device_id=(x, y, 1 - c), device_id_type=MESH)
    cps = []
    for a in range(n):
        rows = ins[a].shape[1] // 2
        cps.append(pltpu.make_async_remote_copy(
            src_ref=ins[a].at[:, pl.ds((1 - c) * rows, rows)], dst_ref=outs[a],
            send_sem=send_sems.at[a], recv_sem=recv_sems.at[a], **sib))
    for k in range(4 * (len(ins) - n)):
        cps.append(pltpu.make_async_remote_copy(
            src_ref=ins[n].at[pl.ds(WOFF * k + (1 - c) * WHALF, WHALF)], dst_ref=outs[n].at[k],
            send_sem=send_sems.at[n + k], recv_sem=recv_sems.at[n + k], **sib))
    return cps


def _swap_add_windows(gwt):
    nchunk = 4
    rows = WHALF // nchunk

    def body(gw_ref, land_ref, out_ref, send_sems, recv_sems, local_sems, own, theirs):
        _, _, c = _place()
        swaps = _pair_copies([gw_ref], [land_ref], send_sems, recv_sems, 0)
        loads = [pltpu.make_async_copy(gw_ref.at[pl.ds(WOFF * k + c * WHALF, WHALF)], own.at[k], local_sems.at[k])
                 for k in range(4)]
        stores = [pltpu.make_async_copy(own.at[k], out_ref.at[k], local_sems.at[k]) for k in range(4)]
        for cp in loads + swaps:
            cp.start()
        for k in range(4):
            swaps[k].wait()
            fetch = pltpu.make_async_copy(land_ref.at[k], theirs, local_sems.at[4])
            fetch.start()
            loads[k].wait()
            fetch.wait()

            def add(i, carry, k=k):
                r = _rows(i, rows)
                own[k, r, :] = (own[k, r, :].astype(F32) + theirs[r, :].astype(F32)).astype(BF)
                return carry

            lax.fori_loop(0, nchunk, add, 0)
            stores[k].start()
        for cp in stores:
            cp.wait()

    return pl.pallas_call(
        body, name="rs_pair_swap_add",
        in_specs=[ANY], out_specs=[ANY, ANY],
        out_shape=[jax.ShapeDtypeStruct((4, WHALF, D), gwt.dtype), jax.ShapeDtypeStruct((4, WHALF, D), BF)],
        scratch_shapes=[pltpu.SemaphoreType.DMA((4,)), pltpu.SemaphoreType.DMA((4,)), pltpu.SemaphoreType.DMA((5,)),
                        pltpu.VMEM((4, WHALF, D), gwt.dtype), pltpu.VMEM((WHALF, D), gwt.dtype)],
        compiler_params=pltpu.CompilerParams(vmem_limit_bytes=VMEM_LIMIT),
    )(gwt)


def _chip_exchange(parts, small):
    n = len(parts)

    def body(*refs):
        ins, sm = refs[:n], refs[n]
        outs, smo = refs[n + 1:2 * n + 1], refs[2 * n + 1]
        send_sems, recv_sems = refs[2 * n + 2:]
        cps = _chip_copies(ins, outs, send_sems, recv_sems, by_dest=True)
        cps += _chip_copies([sm], [smo], send_sems.at[pl.ds(3 * n, 3)], recv_sems.at[pl.ds(3 * n, 3)], by_dest=False)
        for cp in cps:
            cp.start()
        for cp in cps:
            cp.wait()

    return pl.pallas_call(
        body, name="rs_chip_exchange",
        in_specs=[ANY] * (n + 1), out_specs=[ANY] * (n + 1),
        out_shape=[jax.ShapeDtypeStruct(p.shape, p.dtype) for p in parts]
        + [jax.ShapeDtypeStruct((4,) + small.shape, small.dtype)],
        scratch_shapes=[pltpu.SemaphoreType.DMA((3 * (n + 1),)), pltpu.SemaphoreType.DMA((3 * (n + 1),))],
    )(*parts, small)


def _pair_send(halves):
    n = len(halves)

    def body(*refs):
        ins, outs = refs[:n], refs[n:2 * n]
        send_sems, recv_sems = refs[2 * n:]
        x, y, c = _place()
        cps = [pltpu.make_async_remote_copy(
            src_ref=ins[a], dst_ref=outs[a], send_sem=send_sems.at[a], recv_sem=recv_sems.at[a],
            device_id=(x, y, 1 - c), device_id_type=MESH) for a in range(n)]
        for cp in cps:
            cp.start()
        for cp in cps:
            cp.wait()

    return pl.pallas_call(
        body, name="rs_pair_send",
        in_specs=[ANY] * n, out_specs=[ANY] * n,
        out_shape=[jax.ShapeDtypeStruct(h.shape, h.dtype) for h in halves],
        scratch_shapes=[pltpu.SemaphoreType.DMA((n,)), pltpu.SemaphoreType.DMA((n,))],
    )(*halves)


def _row_block(rows):
    for tb in (256, 128, 64, 32, 16, 8):
        if rows % tb == 0:
            return tb
    return rows


def _add_halves(full, recv, name, out_dtype):
    _, r2, w = recv.shape
    tb = _row_block(r2)
    nb = r2 // tb
    c = lax.axis_index("c")

    def body(c_ref, a_ref, b_ref, o_ref):
        o_ref[...] = (a_ref[...].astype(F32) + b_ref[...].astype(F32)).astype(o_ref.dtype)

    return pl.pallas_call(
        body, name=name,
        grid_spec=pltpu.PrefetchScalarGridSpec(
            num_scalar_prefetch=1, grid=(4, nb),
            in_specs=[pl.BlockSpec((1, tb, w), lambda s, i, cr: (s, cr[0] * nb + i, 0)),
                      pl.BlockSpec((1, tb, w), lambda s, i, cr: (s, i, 0))],
            out_specs=pl.BlockSpec((1, tb, w), lambda s, i, cr: (s, i, 0))),
        out_shape=jax.ShapeDtypeStruct(recv.shape, out_dtype),
        compiler_params=_params(("parallel", "parallel")),
    )(jnp.reshape(c, (1,)).astype(jnp.int32), full, recv)


def _add2(a, b, name):
    def body(a_ref, b_ref, o_ref):
        o_ref[...] = a_ref[...] + b_ref[...]

    return pl.pallas_call(body, name=name, out_shape=jax.ShapeDtypeStruct(a.shape, a.dtype))(a, b)


def _sum4(buf, own, name):
    _, r, w = buf.shape
    tb = _row_block(r)
    me_s = 2 * lax.axis_index("x") + lax.axis_index("y")
    by_dest = own.ndim == 3

    def body(s_ref, b_ref, own_ref, o_ref):
        mine = (own_ref[0] if by_dest else own_ref[...]).astype(F32)
        terms = [jnp.where(s_ref[0] == t, mine, b_ref[t].astype(F32)) for t in range(4)]
        o_ref[...] = ((terms[0] + terms[1]) + terms[2]) + terms[3]

    own_spec = (pl.BlockSpec((1, tb, w), lambda i, sr: (sr[0], i, 0)) if by_dest
                else pl.BlockSpec((tb, w), lambda i, sr: (i, 0)))
    return pl.pallas_call(
        body, name=name,
        grid_spec=pltpu.PrefetchScalarGridSpec(
            num_scalar_prefetch=1, grid=(r // tb,),
            in_specs=[pl.BlockSpec((4, tb, w), lambda i, sr: (0, i, 0)), own_spec],
            out_specs=pl.BlockSpec((tb, w), lambda i, sr: (i, 0))),
        out_shape=jax.ShapeDtypeStruct((r, w), F32),
        compiler_params=_params(("parallel",)),
    )(jnp.reshape(me_s, (1,)).astype(jnp.int32), buf, own)


def _adamw_math(w, g, m, v):
    mn = B1 * m + (1.0 - B1) * g
    vn = B2 * v + (1.0 - B2) * (g * g)
    m_hat = mn / (1.0 - B1 ** STEP)
    v_hat = vn / (1.0 - B2 ** STEP)
    return -LR * (m_hat / (jnp.sqrt(v_hat) + AEPS) + WD * w), mn, vn


def _adamw(w, g, m, v, name):
    r, c_ = w.shape
    tb = _row_block(r)
    if tb == r and r > 512:
        tb = 256

    def body(w_ref, g_ref, m_ref, v_ref, d_ref, mo_ref, vo_ref):
        d_ref[...], mo_ref[...], vo_ref[...] = _adamw_math(w_ref[...], g_ref[...], m_ref[...], v_ref[...])

    spec = pl.BlockSpec((tb, c_), lambda i: (i, 0))
    return pl.pallas_call(
        body, name=name, grid=(pl.cdiv(r, tb),),
        in_specs=[spec] * 4, out_specs=[spec] * 3,
        out_shape=[jax.ShapeDtypeStruct(w.shape, F32)] * 3,
        compiler_params=_params(("parallel",)),
    )(w, g, m, v)


def _adamw_rows(w, g_mine, g_sib, m, v, name):
    r = w.shape[0]
    tb = 256
    sub, lanes = w.shape[1:]
    nh = g_mine.shape[0] // tb
    nsteps = pl.cdiv(r, tb)
    assert nsteps <= 2 * nh and 4 * 3 + r <= 2 * nh * tb
    x, y, c = _place()
    place = jnp.stack([c, 4 * (2 * x + y)]).astype(jnp.int32)

    def body(p_ref, w_ref, mc_ref, sc_ref, mn_ref, sn_ref, m_ref, v_ref, go_ref, d_ref, mo_ref, vo_ref, buf):
        i = pl.program_id(0)
        for at, blk, mine_ref, sib_ref in ((0, i, mc_ref, sc_ref), (1, jnp.minimum(i + 1, 2 * nh - 1), mn_ref, sn_ref)):
            rows = jnp.where(blk // nh == p_ref[0], mine_ref[...], sib_ref[...])
            buf[tb * at:tb * (at + 1)] = rows.reshape(tb, sub, lanes)
        g = buf[pl.ds(p_ref[1], tb)]
        go_ref[...] = g
        d_ref[...], mo_ref[...], vo_ref[...] = _adamw_math(w_ref[...], g, m_ref[...], v_ref[...])

    def half_spec(ahead, sibling):
        def index(i, pr):
            half = (1 - pr[0]) if sibling else pr[0]
            return (jnp.clip(jnp.minimum(i + ahead, 2 * nh - 1) - nh * half, 0, nh - 1), 0)
        return pl.BlockSpec((tb, sub * lanes), index)

    spec = pl.BlockSpec((tb, sub, lanes), lambda i, pr: (i, 0, 0))
    return pl.pallas_call(
        body, name=name,
        grid_spec=pltpu.PrefetchScalarGridSpec(
            num_scalar_prefetch=1, grid=(nsteps,),
            in_specs=[spec, half_spec(0, False), half_spec(0, True), half_spec(1, False), half_spec(1, True),
                      spec, spec],
            out_specs=[spec] * 4,
            scratch_shapes=[pltpu.VMEM((2 * tb, sub, lanes), F32)]),
        out_shape=[jax.ShapeDtypeStruct(w.shape, F32)] * 4,
        compiler_params=_params(("parallel",)),
    )(place, w, g_mine, g_sib, g_mine, g_sib, m, v)


def _adamw_halves(w, g_mine, g_sib, m, v, name):
    r, c_ = w.shape
    r2 = g_mine.shape[0]
    tb = _row_block(r2)
    nb = r2 // tb
    c = lax.axis_index("c")

    def body(c_ref, w_ref, gm_ref, gs_ref, m_ref, v_ref, g_ref, d_ref, mo_ref, vo_ref):
        g = jnp.where(pl.program_id(0) == c_ref[0], gm_ref[...], gs_ref[...])
        g_ref[...] = g
        d_ref[...], mo_ref[...], vo_ref[...] = _adamw_math(w_ref[...], g, m_ref[...], v_ref[...])

    full = pl.BlockSpec((tb, c_), lambda h, i, cr: (h * nb + i, 0))
    half = pl.BlockSpec((tb, c_), lambda h, i, cr: (i, 0))
    return pl.pallas_call(
        body, name=name,
        grid_spec=pltpu.PrefetchScalarGridSpec(
            num_scalar_prefetch=1, grid=(2, nb),
            in_specs=[full, half, half, full, full], out_specs=[full] * 4),
        out_shape=[jax.ShapeDtypeStruct(w.shape, F32)] * 4,
        compiler_params=_params(("parallel", "parallel")),
    )(jnp.reshape(c, (1,)).astype(jnp.int32), w, g_mine, g_sib, m, v)


def kernel(x, meta_tokens, norm_g, w_in, b_f, w_out, final_g, loss_target, m_meta_tokens, m_norm_g, m_w_in, m_b_f, m_w_out, m_final_g, v_meta_tokens, v_norm_g, v_w_in, v_b_f, v_w_out, v_final_g):
    w3, m3, v3 = [jnp.transpose(jnp.reshape(t[0], (D // C, C, WSH)), (2, 0, 1)) for t in (w_in, m_w_in, v_w_in)]

    wt_main, laps, _, *normed = _gather_weights(_own_window(w3), meta_tokens, x[0], norm_g)
    wft = jnp.pad(laps[3, :NFF], ((0, C - NFF), (0, 0)))
    wout_own = w_out[0].astype(BF)

    loss, gx, dmeta, dng, gwt, dbf, dwout, dfg, (p_in, p_out), (e_in, e_out) = _local_step(
        x[0], loss_target[0], normed, norm_g, wt_main, wft, b_f, wout_own, final_g.reshape(1, D), True, True)

    g_meta = jnp.stack([dmeta[:, 256 * s:256 * (s + 1)] for s in range(4)])
    small = jnp.concatenate([dng, dfg, jnp.pad(dbf, ((0, 0), (0, D - NFF))),
                             jnp.pad(jnp.reshape(loss, (1, 1)), ((0, 0), (0, D - 1))),
                             jnp.zeros((4, D), F32)], axis=0)
    e_meta, e_small = _chip_exchange([g_meta], small)
    h_in, h_out = _sum4(e_in, p_in, "sum_in"), _sum4(e_out, p_out, "sum_out")
    h_meta, h_small = _sum4(e_meta, g_meta, "sum_meta"), _sum4(e_small, small, "sum_small")
    s_in, s_out, s_meta, s_small = _pair_send([h_in, h_out, h_meta, h_small])
    gw_meta = _add2(h_meta, s_meta, "pair_add_meta")
    tot = _add2(h_small, s_small, "pair_add_small")
    g_norm, g_final, g_bf, loss_all = tot[0:1], tot[1], tot[2:3, :NFF], tot[3, 0]

    d_meta, nm_meta, nv_meta = _adamw(meta_tokens, gw_meta, m_meta_tokens, v_meta_tokens, "adamw_meta")
    d_norm, nm_norm, nv_norm = _adamw(norm_g, g_norm, m_norm_g, v_norm_g, "adamw_norm")
    outs_in = _adamw_rows(w3, h_in, s_in, m3, v3, "adamw_in")
    gw_in, d_in, nm_in, nv_in = [jnp.reshape(jnp.transpose(t, (1, 2, 0)), (1, D, WSH)) for t in outs_in]
    d_bf, nm_bf, nv_bf = _adamw(b_f, g_bf, m_b_f, v_b_f, "adamw_bf")
    gw_out, d_out, nm_out, nv_out = _adamw_halves(w_out[0], h_out, s_out, m_w_out[0], v_w_out[0], "adamw_out")
    d_fin, nm_fin, nv_fin = _adamw(final_g.reshape(1, D), g_final.reshape(1, D), m_final_g.reshape(1, D),
                                   v_final_g.reshape(1, D), "adamw_final")
    return (loss_all, gx[None], gw_meta, g_norm, gw_in, g_bf, gw_out[None], g_final,
            d_meta, d_norm, d_in, d_bf, d_out[None], d_fin.reshape(D),
            nm_meta, nm_norm, nm_in, nm_bf, nm_out[None], nm_fin.reshape(D),
            nv_meta, nv_norm, nv_in, nv_bf, nv_out[None], nv_fin.reshape(D))
```

```python
import numpy as np
import jax
import jax.numpy as jnp
from jax import lax
from jax.experimental import pallas as pl
from jax.experimental.pallas import tpu as pltpu

D = 1024
SEQ = 2048
NMETA = 16
C = 128
PAD = C - NMETA
T = PAD + NMETA + SEQ
NCH = T // C
RH, RDK, RDV = 4, 128, 256
FH, FD = 16, 64
NPAIR = FH // 2
WMAIN = 7168
NFF = 16
WIN = WMAIN + NFF
WSH = WIN // 4
WPADROWS = 1824
DMIX = 2048
EPS = 1e-6
NEG = -1e30
RSCALE = RDK ** -0.5
FSCALE = FD ** -0.5
ROPE_BASE = 10000.0
LR, B1, B2, AEPS, WD, STEP = 0.001, 0.9, 0.999, 1e-08, 0.01, 10

BF = jnp.bfloat16
F32 = jnp.float32
NT = (((1,), (1,)), ((), ()))
TN = (((0,), (0,)), ((), ()))
NN_DIMS = (((1,), (0,)), ((), ()))
MESH = pl.DeviceIdType.MESH
ANY = pl.BlockSpec(memory_space=pl.ANY)
VMEM_LIMIT = 48 * 1024 * 1024
DW_VMEM_LIMIT = 56 * 1024 * 1024

GB_R, GB_F = 2, 6
QB_F, KB_F, VB_F = 24, 32, 40


def _dot(a, b):
    return jnp.dot(a, b, preferred_element_type=F32)


def _dg(a, b, dims):
    return lax.dot_general(a, b, dims, preferred_element_type=F32)


def _params(sem=None):
    return pltpu.CompilerParams(dimension_semantics=sem, vmem_limit_bytes=VMEM_LIMIT)


def _constants():
    pos = jnp.arange(T, dtype=F32) - PAD
    inv = ROPE_BASE ** (-jnp.arange(0, RDK, 2, dtype=F32) / RDK)
    ang = pos[:, None] * inv[None, :]
    cos, sin = jnp.cos(ang), jnp.sin(ang)
    cos2 = jnp.concatenate([cos, cos], axis=1)
    sin2 = jnp.concatenate([-sin, sin], axis=1)
    log_gamma = jnp.log1p(-jnp.exp2(-5.0 - jnp.arange(RH, dtype=F32)))
    idx = jnp.arange(C, dtype=F32)
    diff = idx[:, None] - idx[None, :]
    dmask = jnp.where(diff[None] >= 0, jnp.exp(log_gamma[:, None, None] * jnp.maximum(diff, 0.0)[None]), 0.0)
    zeta = jnp.exp(log_gamma[:, None] * (C - 1.0 - idx)[None, :])
    xi = jnp.exp(log_gamma[:, None] * (idx + 1.0)[None, :])
    gdec = jnp.exp(log_gamma * C)
    zeta_b = jnp.broadcast_to(zeta[:, :, None], (RH, C, RDK))
    xi_b = jnp.broadcast_to(xi[:, :, None], (RH, C, RDK))
    gdec_b = jnp.broadcast_to(gdec[:, None, None], (RH, RDK, RDV))
    tri = jnp.asarray(np.tril(np.ones((C, C), np.float32)), dtype=BF)
    head_of_lane = np.arange(FH * FD) // FD
    pick = ((np.arange(FH * FD)[:, None] % FD == 0)
            & (head_of_lane[:, None] == np.arange(C)[None, :])).astype(np.float32)
    seg = (np.arange(C)[:, None] // FD == np.arange(C)[None, :] // FD).astype(np.float32)
    ones_aug = np.concatenate([np.tile((np.arange(C) < FD)[None, :], (C, 1)),
                               np.tile((np.arange(C) >= FD)[None, :], (C, 1))], axis=0).astype(np.float32)
    lane = np.arange(2 * C) % C
    causal = np.where(lane[None, :] <= np.arange(C)[:, None], 0.0, NEG).astype(np.float32)
    mask_bias = np.stack([np.zeros((C, 2 * C), np.float32), causal])
    return dict(cos2=cos2, sin2=sin2, dmask=dmask, zeta=zeta_b, xi=xi_b, gdec=gdec_b, tri=tri,
                mask_bias=jnp.asarray(mask_bias), pick=jnp.asarray(pick, dtype=BF), seg=jnp.asarray(seg, dtype=BF),
                ones_aug=jnp.asarray(ones_aug, dtype=BF))


def _norm_rows(h, g):
    return h * lax.rsqrt(jnp.mean(h * h, axis=1, keepdims=True) + EPS) * g


def _mm_nt(a, b, n, tm, tn, name):
    m, k = a.shape

    def body(a_ref, b_ref, o_ref):
        o_ref[...] = _dg(a_ref[...], b_ref[...], NT)

    return pl.pallas_call(
        body, name=name, grid=(m // tm, n // tn),
        in_specs=[pl.BlockSpec((tm, k), lambda i, j: (i, 0)), pl.BlockSpec((tn, k), lambda i, j: (j, 0))],
        out_specs=pl.BlockSpec((tm, tn), lambda i, j: (i, j)),
        out_shape=jax.ShapeDtypeStruct((m, n), F32),
        compiler_params=_params(("parallel", "parallel")),
    )(a, b)


def _mm_nn(a, b, tm, tn, name, out_dtype=F32):
    m, k = a.shape
    _, n = b.shape

    def body(a_ref, b_ref, o_ref):
        o_ref[...] = _dot(a_ref[...], b_ref[...]).astype(out_dtype)

    return pl.pallas_call(
        body, name=name, grid=(m // tm, n // tn),
        in_specs=[pl.BlockSpec((tm, k), lambda i, j: (i, 0)), pl.BlockSpec((k, tn), lambda i, j: (0, j))],
        out_specs=pl.BlockSpec((tm, tn), lambda i, j: (i, j)),
        out_shape=jax.ShapeDtypeStruct((m, n), out_dtype),
        compiler_params=_params(("parallel", "parallel")),
    )(a, b)


def _rot(x, cos2, sin2):
    return x * cos2 + pltpu.roll(x, 64, 1) * sin2


def _ret_specs(chunk):
    whole = lambda shape: pl.BlockSpec(shape, lambda n: (0,) * len(shape))
    return [
        pl.BlockSpec((C, RH * RDK), lambda n: (chunk(n), 0)),
        pl.BlockSpec((C, RH * RDK), lambda n: (chunk(n), 1)),
        pl.BlockSpec((C, RH * RDV), lambda n: (chunk(n), 1)),
        pl.BlockSpec((C, RDK), lambda n: (chunk(n), 0)),
        pl.BlockSpec((C, RDK), lambda n: (chunk(n), 0)),
        whole((RH, C, C)), whole((RH, C, RDK)), whole((RH, C, RDK)), whole((RH, RDK, RDV)),
    ]


def _ret_heads(q_ref, k_ref, v_ref, cos, sin):
    qr = [_rot(q_ref[:, RDK * h:RDK * (h + 1)], cos, sin) for h in range(RH)]
    kr = [_rot(k_ref[:, RDK * h:RDK * (h + 1)], cos, sin) * RSCALE for h in range(RH)]
    vb = [v_ref[:, RDV * h:RDV * (h + 1)].astype(BF) for h in range(RH)]
    return qr, kr, [t.astype(BF) for t in qr], [t.astype(BF) for t in kr], vb


def _ret_fwd(z, cst):
    def body(q_ref, k_ref, v_ref, cos_ref, sin_ref, dm_ref, xi_ref, zt_ref, gd_ref, r_ref, sp_ref, st):
        n = pl.program_id(0)

        @pl.when(n == 0)
        def _():
            st[...] = jnp.zeros_like(st)

        hs = range(RH)
        qr, kr, qb, kb, vb = _ret_heads(q_ref, k_ref, v_ref, cos_ref[...], sin_ref[...])
        sd = [(_dg(qb[h], kb[h], NT) * dm_ref[h]).astype(BF) for h in hs]
        state = [st[h] for h in hs]
        qx = [(qr[h] * xi_ref[h]).astype(BF) for h in hs]
        kz = [(kr[h] * zt_ref[h]).astype(BF) for h in hs]
        out = [_dot(sd[h], vb[h]) + _dot(qx[h], state[h].astype(BF)) for h in hs]
        kv = [_dg(kz[h], vb[h], TN) for h in hs]
        for h in hs:
            sp_ref[0, h] = state[h]
            r_ref[:, RDV * h:RDV * (h + 1)] = out[h]
            st[h] = state[h] * gd_ref[h] + kv[h]

    return pl.pallas_call(
        body, name="ret_fwd", grid=(NCH,),
        in_specs=_ret_specs(lambda n: n),
        out_specs=[pl.BlockSpec((C, RH * RDV), lambda n: (n, 0)),
                   pl.BlockSpec((1, RH, RDK, RDV), lambda n: (n, 0, 0, 0))],
        out_shape=[jax.ShapeDtypeStruct((T, RH * RDV), F32), jax.ShapeDtypeStruct((NCH, RH, RDK, RDV), F32)],
        scratch_shapes=[pltpu.VMEM((RH, RDK, RDV), F32)],
        compiler_params=_params(("arbitrary",)),
    )(z, z, z, cst["cos2"], cst["sin2"], cst["dmask"], cst["xi"], cst["zeta"], cst["gdec"])


def _ret_bwd(z, cst, sprev, dr):
    def body(q_ref, k_ref, v_ref, cos_ref, sin_ref, dm_ref, xi_ref, zt_ref, gd_ref, sp_ref, dr_ref,
             dq_ref, dk_ref, dv_ref, gst):
        i = pl.program_id(0)

        @pl.when(i == 0)
        def _():
            gst[...] = jnp.zeros_like(gst)

        hs = range(RH)
        cos, sin = cos_ref[...], sin_ref[...]
        qr, kr, qb, kb, vb = _ret_heads(q_ref, k_ref, v_ref, cos, sin)
        dm = [dm_ref[h] for h in hs]
        xi = [xi_ref[h] for h in hs]
        zt = [zt_ref[h] for h in hs]
        sd = [(_dg(qb[h], kb[h], NT) * dm[h]).astype(BF) for h in hs]
        qx = [(qr[h] * xi[h]).astype(BF) for h in hs]
        kz = [(kr[h] * zt[h]).astype(BF) for h in hs]
        drb = [dr_ref[:, RDV * h:RDV * (h + 1)] for h in hs]
        sb = [sp_ref[0, h].astype(BF) for h in hs]
        g = [gst[h] for h in hs]
        gb = [t.astype(BF) for t in g]
        ds = [(_dg(drb[h], vb[h], NT) * dm[h]).astype(BF) for h in hs]
        dq = [_dot(ds[h], kb[h]) + _dg(drb[h], sb[h], NT) * xi[h] for h in hs]
        dk = [(_dg(ds[h], qb[h], TN) + _dg(vb[h], gb[h], NT) * zt[h]) * RSCALE for h in hs]
        dv = [_dg(sd[h], drb[h], TN) + _dot(kz[h], gb[h]) for h in hs]
        gn = [g[h] * gd_ref[h] + _dg(qx[h], drb[h], TN) for h in hs]
        for h in hs:
            gst[h] = gn[h]
            dq_ref[:, RDK * h:RDK * (h + 1)] = (dq[h] * cos + pltpu.roll(dq[h] * sin, 64, 1)).astype(BF)
            dk_ref[:, RDK * h:RDK * (h + 1)] = (dk[h] * cos + pltpu.roll(dk[h] * sin, 64, 1)).astype(BF)
            dv_ref[:, RDV * h:RDV * (h + 1)] = dv[h].astype(BF)

    rev = lambda n: NCH - 1 - n
    return pl.pallas_call(
        body, name="ret_bwd", grid=(NCH,),
        in_specs=_ret_specs(rev) + [
            pl.BlockSpec((1, RH, RDK, RDV), lambda n: (rev(n), 0, 0, 0)),
            pl.BlockSpec((C, RH * RDV), lambda n: (rev(n), 0)),
        ],
        out_specs=[pl.BlockSpec((C, RH * RDK), lambda n: (rev(n), 0)),
                   pl.BlockSpec((C, RH * RDK), lambda n: (rev(n), 0)),
                   pl.BlockSpec((C, RH * RDV), lambda n: (rev(n), 0))],
        out_shape=[jax.ShapeDtypeStruct((T, RH * RDK), BF), jax.ShapeDtypeStruct((T, RH * RDK), BF),
                   jax.ShapeDtypeStruct((T, RH * RDV), BF)],
        scratch_shapes=[pltpu.VMEM((RH, RDK, RDV), F32)],
        compiler_params=_params(("arbitrary",)),
    )(z, z, z, cst["cos2"], cst["sin2"], cst["dmask"], cst["xi"], cst["zeta"], cst["gdec"], sprev, dr)


def _place():
    x, y, c = lax.axis_index("x"), lax.axis_index("y"), lax.axis_index("c")
    return x, y, c


def _other_chips(x, y):
    return [(1 - x, y, 2 * (1 - x) + y), (x, 1 - y, 2 * x + (1 - y)), (1 - x, 1 - y, 2 * (1 - x) + (1 - y))]


def _chip_copies(srcs, lands, send_sems, recv_sems, by_dest):
    x, y, c = _place()
    me_s = 2 * x + y
    return [pltpu.make_async_remote_copy(
        src_ref=src.at[cs] if by_dest else src, dst_ref=land.at[me_s],
        send_sem=send_sems.at[3 * a + j], recv_sem=recv_sems.at[3 * a + j],
        device_id=(cx, cy, c), device_id_type=MESH)
        for a, (src, land) in enumerate(zip(srcs, lands)) for j, (cx, cy, cs) in enumerate(_other_chips(x, y))]


def _split_dot(x, mat01, dims=NN_DIMS, x_first=True):
    acc, rest = None, x
    for _ in range(3):
        piece = rest.astype(BF)
        part = _dg(piece, mat01, dims) if x_first else _dg(mat01, piece, dims)
        acc = part if acc is None else acc + part
        rest = rest - piece.astype(F32)
    return acc


def _log_sigmoid(x):
    return -(jnp.maximum(-x, 0.0) + jnp.log1p(jnp.exp(-jnp.abs(x))))


def _fox_prep(zf, bf_pad, cst):
    def body(zf_ref, b_ref, tri_ref, ct_ref, carry):
        n = pl.program_id(0)

        @pl.when(n == 0)
        def _():
            carry[...] = jnp.zeros_like(carry)

        ls = _log_sigmoid(zf_ref[...] + b_ref[...])
        row = n * C + lax.broadcasted_iota(jnp.int32, (C, C), 0)
        lf = jnp.where(row >= PAD, ls, 0.0)
        cc = _split_dot(lf, tri_ref[...], x_first=False) + carry[0:1, :]
        carry[...] = jnp.broadcast_to(cc[C - 1:C, :], carry.shape)
        pos = n * C + lax.broadcasted_iota(jnp.int32, (FH, C), 1)
        ct_ref[0] = jnp.where(pos >= PAD, cc.T[:FH, :], -NEG)

    return pl.pallas_call(
        body, name="fox_prep", grid=(NCH,),
        in_specs=[pl.BlockSpec((C, C), lambda n: (n, 0)), pl.BlockSpec((1, C), lambda n: (0, 0)),
                  pl.BlockSpec((C, C), lambda n: (0, 0))],
        out_specs=pl.BlockSpec((1, FH, C), lambda n: (n, 0, 0)),
        out_shape=jax.ShapeDtypeStruct((NCH, FH, C), F32),
        scratch_shapes=[pltpu.VMEM((8, C), F32)],
        compiler_params=_params(("arbitrary",)),
    )(zf, bf_pad, cst["tri"])


def _lo_lanes(shape):
    return lax.broadcasted_iota(jnp.int32, shape, 1) < FD


def _split_heads(x):
    lo = _lo_lanes(x.shape)
    zero = jnp.zeros_like(x)
    return jnp.concatenate([jnp.where(lo, x, zero), jnp.where(lo, zero, x)], axis=0)


def _spread2(x):
    lo = _lo_lanes(x.shape)
    r = pltpu.roll(x, FD, 1)
    return jnp.concatenate([jnp.where(lo, x, r), jnp.where(lo, r, x)], axis=1)


NSTEP = (NCH + 1) // 2
NTILE = NCH + 1
TROWS = T + C


def _fox_tile(s, t):
    second = t > s
    return second.astype(jnp.int32), jnp.where(second, t - s - 1, s - t)


def _fox_pos(i):
    return jnp.where(i < NSTEP, 2 * i, 2 * (NCH - 1 - i) + 1)


def _fox_pair_columns():
    return pl.BlockSpec((TROWS, C), lambda p, s: (0, p))


def _fox_key_bias(ct_ref, p, j):
    return jnp.concatenate([ct_ref[j, pl.ds(2 * p, 1), :], ct_ref[j, pl.ds(2 * p + 1, 1), :]], axis=1)


def _fox_columns(cols, sems, p):
    def copies(pair, slot):
        return [pltpu.make_async_copy(
            src.at[pl.ds(0, buf.shape[1]), pl.ds(pl.multiple_of((first + pair) * C, C), C)], buf.at[slot],
            sems.at[i, slot]) for i, (src, first, buf) in enumerate(cols)]

    @pl.when(p == 0)
    def _():
        for cp in copies(0, 0):
            cp.start()

    for cp in copies(p, p % 2):
        cp.wait()

    @pl.when(p + 1 < NPAIR)
    def _():
        for cp in copies(p + 1, 1 - p % 2):
            cp.start()


def _rows(block, size=C):
    return pl.ds(pl.multiple_of(block * size, size), size)


def _fox_fwd(z, ct, cst, share):
    n = 0 if share is None else 1

    def body(z_ref, ct_ref, ones_ref, mb_ref, *rest):
        share_refs, (a_ref, g_ref), land_refs = rest[:n], rest[n:n + 2], rest[n + 2:2 * n + 2]
        kks, vvs, q2, m2, sbuf, qbuf, kbuf, vbuf, col_sems = rest[2 * n + 2:2 * n + 11]
        p, s = pl.program_id(0), pl.program_id(1)
        slot = p % 2
        if n:
            send_sems, recv_sems, own_sem = rest[2 * n + 11:]
            x, y, _ = _place()
            copies = _chip_copies(share_refs, land_refs, send_sems, recv_sems, by_dest=False)
            copies.append(pltpu.make_async_copy(share_refs[0], land_refs[0].at[2 * x + y], own_sem.at[0]))

            @pl.when((p == 0) & (s == 0))
            def _():
                for cp in copies:
                    cp.start()

            @pl.when((p == NPAIR - 1) & (s == NSTEP - 1))
            def _():
                for cp in copies:
                    cp.wait()

        @pl.when(s == 0)
        def _():
            ones = ones_ref[...]
            _fox_columns([(z_ref, QB_F, qbuf), (z_ref, KB_F, kbuf), (z_ref, VB_F, vbuf)], col_sems, p)

            def prep(j, carry):
                kks[j] = _split_heads(kbuf[slot, _rows(j), :]).astype(BF)
                vvs[j] = jnp.concatenate([_split_heads(vbuf[slot, _rows(j), :]).astype(BF), ones], axis=1)
                return carry

            lax.fori_loop(0, NCH, prep, 0)

        q2[0] = (qbuf[slot, _rows(s), :] * FSCALE).astype(BF)
        q2[1] = (qbuf[slot, _rows(NCH - 1 - s), :] * FSCALE).astype(BF)

        tiles = [_fox_tile(s, t) for t in range(NTILE)]
        causal = mb_ref[1]
        neg = jnp.full((C, 2 * C), NEG, F32)
        run, first = neg, neg
        for t, (sel, j) in enumerate(tiles):
            st = _dg(q2[sel], kks[j], NT) - _fox_key_bias(ct_ref, p, j)
            if t in (0, NTILE - 1):
                st = st + causal
            sbuf[t] = st
            run = jnp.maximum(jnp.where(t == s + 1, neg, run), st)
            first = jnp.where(t == s, run, first)
        for w, mx in enumerate((first, run)):
            m2[w] = jnp.concatenate(
                [jnp.broadcast_to(jnp.max(mx[:, :C], axis=1, keepdims=True), (C, C)),
                 jnp.broadcast_to(jnp.max(mx[:, C:], axis=1, keepdims=True), (C, C))], axis=1)

        zero = jnp.zeros((C, 2 * C), F32)
        run, first = zero, zero
        for t, (sel, j) in enumerate(tiles):
            run = jnp.where(t == s + 1, zero, run) + _dot(jnp.exp(sbuf[t] - m2[sel]).astype(BF), vvs[j])
            first = jnp.where(t == s, run, first)
        lo = _lo_lanes((C, C))
        for w, res in enumerate((first, run)):
            l = res[:, C:]
            a_ref[_rows(2 * s + w), :] = res[:, :C] / l
            mw = m2[w]
            g_ref[_rows(2 * s + w), :] = -(jnp.where(lo, mw[:, :C], mw[:, C:]) + jnp.log(l))

    col = _fox_pair_columns()
    return pl.pallas_call(
        body, name="fox_fwd", grid=(NPAIR, NSTEP),
        in_specs=[ANY,
                  pl.BlockSpec((NCH, FH, C), lambda p, s: (0, 0, 0)),
                  pl.BlockSpec((2 * C, C), lambda p, s: (0, 0)),
                  pl.BlockSpec((2, C, 2 * C), lambda p, s: (0, 0, 0))] + [ANY] * n,
        out_specs=[col, col] + [ANY] * n,
        out_shape=[jax.ShapeDtypeStruct((TROWS, FH * FD), F32)] * 2
        + ([jax.ShapeDtypeStruct((4,) + share.shape, share.dtype)] if n else []),
        scratch_shapes=[pltpu.VMEM((NCH, 2 * C, C), BF), pltpu.VMEM((NCH, 2 * C, 2 * C), BF),
                        pltpu.VMEM((2, C, C), BF), pltpu.VMEM((2, C, 2 * C), F32),
                        pltpu.VMEM((NTILE, C, 2 * C), F32),
                        pltpu.VMEM((2, T, C), F32), pltpu.VMEM((2, T, C), F32), pltpu.VMEM((2, T, C), F32),
                        pltpu.SemaphoreType.DMA((3, 2))]
        + [pltpu.SemaphoreType.DMA((3,)), pltpu.SemaphoreType.DMA((3,)), pltpu.SemaphoreType.DMA((1,))] * n,
        compiler_params=_params(("arbitrary", "arbitrary")),
    )(z, ct, cst["ones_aug"], cst["mask_bias"], *([share] * n))


def _fox_bwd(z, da, g, delta, ct, cst, parts=()):
    grp = 9

    n = len(parts)

    def body(z_ref, da_ref, g_ref, dl_ref, ct_ref, ones_ref, mb_ref, *rest):
        part_refs, (dq_ref, dr_ref, dk_ref, dv_ref, dcs_ref), land_refs = rest[:n], rest[n:n + 5], rest[n + 5:2 * n + 5]
        (kks, vvs, q2, qq2, dd2, da2, gi2, dl2, dq2, dvb, dkb, dkacc, dvacc, csacc, qbuf, kbuf, vbuf, dabuf, gbuf,
         dlbuf, col_sems) = rest[2 * n + 5:2 * n + 26]
        p, s = pl.program_id(0), pl.program_id(1)
        slot = p % 2
        ones = ones_ref[...]
        if n:
            copies = _chip_copies(part_refs, land_refs, *rest[2 * n + 26:], by_dest=True)

            @pl.when((p == 0) & (s == 0))
            def _():
                for cp in copies:
                    cp.start()

            @pl.when((p == NPAIR - 1) & (s == NSTEP - 1))
            def _():
                for cp in copies:
                    cp.wait()

        @pl.when(s == 0)
        def _():
            dkacc[...] = jnp.zeros_like(dkacc)
            dvacc[...] = jnp.zeros_like(dvacc)
            csacc[...] = jnp.zeros_like(csacc)
            _fox_columns([(z_ref, QB_F, qbuf), (z_ref, KB_F, kbuf), (z_ref, VB_F, vbuf), (da_ref, 0, dabuf),
                          (g_ref, 0, gbuf), (dl_ref, 0, dlbuf)], col_sems, p)

            def prep(j, carry):
                kks[j] = _split_heads(kbuf[slot, _rows(j), :]).astype(BF)
                vvs[j] = _split_heads(vbuf[slot, _rows(j), :]).astype(BF)
                return carry

            lax.fori_loop(0, NCH, prep, 0)

        for w, (chunk, blk) in enumerate(((s, 2 * s), (NCH - 1 - s, jnp.where(s == NSTEP - 1, 2 * s, 2 * s + 1)))):
            qf = qbuf[slot, _rows(chunk), :]
            q2[w] = (qf * FSCALE).astype(BF)
            qq2[w] = jnp.concatenate([_split_heads(qf).astype(BF), ones], axis=1)
            da2[w] = dabuf[slot, _rows(blk), :]
            dd2[w] = _split_heads(da2[w].astype(F32)).astype(BF)
            gi2[w] = _spread2(gbuf[slot, _rows(blk), :])
            dl2[w] = _spread2(dlbuf[slot, _rows(blk), :])
        dq2[...] = jnp.zeros_like(dq2)
        zero = jnp.zeros((C, 2 * C), F32)

        def group(gi, carry):
            ts = [gi * grp + u for u in range(grp)]
            tiles = [_fox_tile(s, t) for t in ts]
            kk = [kks[j] for _, j in tiles]
            ss = [_dg(q2[sel], kj, NT) + (gi2[sel] - _fox_key_bias(ct_ref, p, j)) for kj, (sel, j) in zip(kk, tiles)]
            ss[0] = ss[0] + mb_ref[(gi == 0).astype(jnp.int32)]
            ss[-1] = ss[-1] + mb_ref[(gi == 1).astype(jnp.int32)]
            dps = [_dg(da2[sel], vvs[j], NT) for sel, j in tiles]
            pes = [jnp.exp(st) for st in ss]
            dss = [pe * (dp - dl2[sel]) * FSCALE for pe, dp, (sel, _) in zip(pes, dps, tiles)]
            pts = [jnp.concatenate([pe[:, :C].T, pe[:, C:].T], axis=1).astype(BF) for pe in pes]
            dsts = [jnp.concatenate([ds[:, :C].T, ds[:, C:].T], axis=1).astype(BF) for ds in dss]
            dvs = [_dot(pt, dd2[sel]) for pt, (sel, _) in zip(pts, tiles)]
            rs = [_dot(dst, qq2[sel]) for dst, (sel, _) in zip(dsts, tiles)]
            parts = [_dot(ds.astype(BF), jnp.concatenate([kj, ones], axis=1)) for ds, kj in zip(dss, kk)]
            for t, dv, rr in zip(ts, dvs, rs):
                dvb[t] = dv
                dkb[t] = rr
            pa, pb = zero, zero
            for t, part in zip(ts, parts):
                pa = pa + jnp.where(t <= s, part, zero)
                pb = pb + jnp.where(t <= s, zero, part)
            dq2[0] += pa
            dq2[1] += pb
            return carry

        ntile = jnp.where(s == NSTEP - 1, grp, NTILE)
        lax.fori_loop(0, ntile // grp, group, 0)

        def scatter(t, carry):
            _, j = _fox_tile(s, t)
            r = pl.ds(pl.multiple_of(j * C, C), C)
            dvacc[r, :] += dvb[t]
            dkacc[r, :] += dkb[t, :, :C]
            csacc[r, :] += dkb[t, :, C:]
            return carry

        lax.fori_loop(0, ntile, scatter, 0)
        for w, chunk in ((1, NCH - 1 - s), (0, s)):
            res = dq2[w]
            dq_ref[_rows(chunk), :] = res[:, :C].astype(BF)
            dr_ref[_rows(2 * s + w), :] = res[:, C:]

        @pl.when(s == NSTEP - 1)
        def _():
            dk_ref[...] = dkacc[...].astype(BF)
            dv_ref[...] = dvacc[...].astype(BF)
            dcs_ref[...] = csacc[...]

    both = _fox_pair_columns()
    col = pl.BlockSpec((T, C), lambda p, s: (0, p))
    return pl.pallas_call(
        body, name="fox_bwd", grid=(NPAIR, NSTEP),
        in_specs=[ANY] * 4
        + [pl.BlockSpec((NCH, FH, C), lambda p, s: (0, 0, 0)),
           pl.BlockSpec((2 * C, C), lambda p, s: (0, 0)),
           pl.BlockSpec((2, C, 2 * C), lambda p, s: (0, 0, 0))] + [ANY] * n,
        out_specs=[col, both, col, col, col] + [ANY] * n,
        out_shape=[jax.ShapeDtypeStruct((T, FH * FD), BF), jax.ShapeDtypeStruct((TROWS, FH * FD), F32),
                   jax.ShapeDtypeStruct((T, FH * FD), BF), jax.ShapeDtypeStruct((T, FH * FD), BF),
                   jax.ShapeDtypeStruct((T, FH * FD), F32)]
        + [jax.ShapeDtypeStruct(p.shape, p.dtype) for p in parts],
        scratch_shapes=[pltpu.VMEM((NCH, 2 * C, C), BF), pltpu.VMEM((NCH, 2 * C, C), BF),
                        pltpu.VMEM((2, C, C), BF), pltpu.VMEM((2, 2 * C, 2 * C), BF), pltpu.VMEM((2, 2 * C, C), BF),
                        pltpu.VMEM((2, C, C), BF), pltpu.VMEM((2, C, 2 * C), F32), pltpu.VMEM((2, C, 2 * C), F32),
                        pltpu.VMEM((2, C, 2 * C), F32),
                        pltpu.VMEM((NTILE, C, C), F32), pltpu.VMEM((NTILE, C, 2 * C), F32),
                        pltpu.VMEM((T, C), F32), pltpu.VMEM((T, C), F32), pltpu.VMEM((T, C), F32),
                        pltpu.VMEM((2, T, C), F32), pltpu.VMEM((2, T, C), F32), pltpu.VMEM((2, T, C), F32),
                        pltpu.VMEM((2, T, C), BF), pltpu.VMEM((2, T, C), F32), pltpu.VMEM((2, T, C), F32),
                        pltpu.SemaphoreType.DMA((6, 2))]
        + ([pltpu.SemaphoreType.DMA((3 * n,)), pltpu.SemaphoreType.DMA((3 * n,))] if n else []),
        compiler_params=_params(("arbitrary", "arbitrary")),
    )(z, da, g, delta, ct, cst["ones_aug"], cst["mask_bias"], *parts)


def _fox_gate_bwd(drow, dcol, zf, bf_pad, cst):
    def body(dr_ref, dc_ref, zf_ref, b_ref, tri_ref, pick_ref, dff_ref, db_ref, carry):
        s = pl.program_id(0)
        n = NCH - 1 - s

        @pl.when(s == 0)
        def _():
            carry[...] = jnp.zeros_like(carry)
            db_ref[...] = jnp.zeros_like(db_ref)

        dcb = _split_dot((dr_ref[...] - dc_ref[...]) * (1.0 / FSCALE), pick_ref[...])
        suf = _split_dot(dcb, tri_ref[...], TN, x_first=False) + carry[0:1, :]
        carry[...] = jnp.broadcast_to(suf[0:1, :], carry.shape)
        x = zf_ref[...] + b_ref[...]
        row = n * C + lax.broadcasted_iota(jnp.int32, (C, C), 0)
        dff = jnp.where(row >= PAD, suf * (1.0 - jax.nn.sigmoid(x)), 0.0)
        dff_ref[...] = dff.astype(BF)
        db_ref[...] += jnp.sum(dff, axis=0, keepdims=True)

    rev = lambda s: (NCH - 1 - s, 0)
    return pl.pallas_call(
        body, name="fox_gate_bwd", grid=(NCH,),
        in_specs=[pl.BlockSpec((C, FH * FD), lambda s: (_fox_pos(NCH - 1 - s), 0)),
                  pl.BlockSpec((C, FH * FD), rev), pl.BlockSpec((C, C), rev),
                  pl.BlockSpec((1, C), lambda s: (0, 0)), pl.BlockSpec((C, C), lambda s: (0, 0)),
                  pl.BlockSpec((FH * FD, C), lambda s: (0, 0))],
        out_specs=[pl.BlockSpec((C, C), rev), pl.BlockSpec((1, C), lambda s: (0, 0))],
        out_shape=[jax.ShapeDtypeStruct((T, C), BF), jax.ShapeDtypeStruct((1, C), F32)],
        scratch_shapes=[pltpu.VMEM((8, C), F32)],
        compiler_params=_params(("arbitrary",)),
    )(drow, dcol, zf, bf_pad, cst["tri"], cst["pick"])


def _head_norm(r):
    rn, rs = [], []
    for h in range(RH):
        rh = r[:, RDV * h:RDV * (h + 1)]
        s = lax.rsqrt(jnp.mean(rh * rh, axis=1, keepdims=True) + EPS)
        rn.append(rh * s)
        rs.append(s)
    return jnp.concatenate(rn, axis=1), rs


def _gated(r, rg, a, fg):
    rn, _ = _head_norm(r)
    return jnp.concatenate([rn * (rg * jax.nn.sigmoid(rg)), a * (fg * jax.nn.sigmoid(fg))], axis=1)


def _out_loss(r, z, a, wout, x, tgt, fgain):
    def body(r_ref, rg_ref, a_ref, fg_ref, w_ref, x_ref, t_ref, g_ref, yt_ref, do_ref, dob_ref, loss_ref, dg_ref):
        i = pl.program_id(0)

        @pl.when(i == 0)
        def _():
            yt_ref[...] = jnp.zeros_like(yt_ref)
            do_ref[...] = jnp.zeros_like(do_ref)
            dob_ref[...] = jnp.zeros_like(dob_ref)
            loss_ref[...] = jnp.zeros_like(loss_ref)
            dg_ref[...] = jnp.zeros_like(dg_ref)

        @pl.when(i > 0)
        def _():
            y = _gated(r_ref[...], rg_ref[...], a_ref[...], fg_ref[...])
            yt_ref[...] = y.T.astype(BF)
            o = x_ref[...] + _dot(y.astype(BF), w_ref[...])
            rs = lax.rsqrt(jnp.mean(o * o, axis=1, keepdims=True) + EPS)
            on = o * rs
            g = g_ref[...]
            e = on * g - t_ref[...]
            loss_ref[...] += 0.5 * jnp.sum(jnp.mean(e * e, axis=1, keepdims=True))
            dyh = e * (1.0 / D)
            dg_ref[...] += jnp.sum(dyh * on, axis=0, keepdims=True)
            don = dyh * g
            do = rs * (don - on * jnp.mean(don * on, axis=1, keepdims=True))
            do_ref[...] = do
            dob_ref[...] = do.astype(BF)

    tok = lambda i: (jnp.maximum(i - 1, 0), 0)
    return pl.pallas_call(
        body, name="out_loss", grid=(NCH,),
        in_specs=[pl.BlockSpec((C, D), lambda i: (i, 0)), pl.BlockSpec((C, D), lambda i: (i, GB_R)),
                  pl.BlockSpec((C, D), lambda i: (_fox_pos(i), 0)), pl.BlockSpec((C, D), lambda i: (i, GB_F)),
                  pl.BlockSpec((DMIX, D), lambda i: (0, 0)),
                  pl.BlockSpec((C, D), tok), pl.BlockSpec((C, D), tok), pl.BlockSpec((1, D), lambda i: (0, 0))],
        out_specs=[pl.BlockSpec((DMIX, C), lambda i: (0, i)), pl.BlockSpec((C, D), lambda i: (i, 0)),
                   pl.BlockSpec((C, D), lambda i: (i, 0)), pl.BlockSpec((8, C), lambda i: (0, 0)),
                   pl.BlockSpec((1, D), lambda i: (0, 0))],
        out_shape=[jax.ShapeDtypeStruct((DMIX, T), BF), jax.ShapeDtypeStruct((T, D), F32),
                   jax.ShapeDtypeStruct((T, D), BF), jax.ShapeDtypeStruct((8, C), F32),
                   jax.ShapeDtypeStruct((1, D), F32)],
        compiler_params=_params(("arbitrary",)),
    )(r, z, a, z, wout, x, tgt, fgain)


def _silu_and_grad(x):
    s = jax.nn.sigmoid(x)
    return x * s, s * (1.0 + x * (1.0 - s))


def _dy_gate_bwd(dob, wout, r, z, a, seg, swap=()):
    n = len(swap)

    def body(do_ref, w_ref, r_ref, rg_ref, a_ref, fg_ref, seg_ref, *rest):
        (dr_ref, da_ref, drg_ref, dfg_ref, dl_ref) = rest[n:n + 5]
        if n:
            copies = _pair_copies(rest[:n], rest[n + 5:2 * n + 5], *rest[2 * n + 5:], n)

            @pl.when(pl.program_id(0) == 0)
            def _():
                for cp in copies:
                    cp.start()

            @pl.when(pl.program_id(0) == NCH - 1)
            def _():
                for cp in copies:
                    cp.wait()

        dy = _dg(do_ref[...], w_ref[...], NT)
        a_ = a_ref[...]
        rn, rs = _head_norm(r_ref[...])
        silu_rg, dsilu_rg = _silu_and_grad(rg_ref[...])
        silu_fg, dsilu_fg = _silu_and_grad(fg_ref[...])
        dyr, dyf = dy[:, :D], dy[:, D:]
        drn = dyr * silu_rg
        drg_ref[...] = (dyr * rn * dsilu_rg).astype(BF)
        for h in range(RH):
            sl = slice(RDV * h, RDV * (h + 1))
            dh, nh = drn[:, sl], rn[:, sl]
            dr_ref[:, sl] = (rs[h] * (dh - nh * jnp.mean(dh * nh, axis=1, keepdims=True))).astype(BF)
        dab = (dyf * silu_fg).astype(BF)
        da_ref[...] = dab
        dfg_ref[...] = (dyf * a_ * dsilu_fg).astype(BF)
        prod = dab.astype(F32) * a_
        segm = seg_ref[...]
        for p in range(NPAIR):
            sl = slice(C * p, C * (p + 1))
            hi = prod[:, sl].astype(BF)
            lo = (prod[:, sl] - hi.astype(F32)).astype(BF)
            dl_ref[:, sl] = _dot(hi, segm) + _dot(lo, segm)

    row = pl.BlockSpec((C, D), lambda i: (i, 0))
    fox = pl.BlockSpec((C, D), lambda i: (_fox_pos(i), 0))
    return pl.pallas_call(
        body, name="dy_gate_bwd", grid=(NCH,),
        in_specs=[row, pl.BlockSpec((DMIX, D), lambda i: (0, 0)),
                  row, pl.BlockSpec((C, D), lambda i: (i, GB_R)),
                  fox, pl.BlockSpec((C, D), lambda i: (i, GB_F)),
                  pl.BlockSpec((C, C), lambda i: (0, 0))] + [ANY] * n,
        out_specs=[row, fox, row, row, fox] + [ANY] * n,
        out_shape=[jax.ShapeDtypeStruct((T, D), BF), jax.ShapeDtypeStruct((TROWS, D), BF),
                   jax.ShapeDtypeStruct((T, D), BF), jax.ShapeDtypeStruct((T, D), BF),
                   jax.ShapeDtypeStruct((TROWS, D), F32)]
        + [jax.ShapeDtypeStruct((4, s.shape[1] // 2, s.shape[2]), s.dtype) for s in swap],
        scratch_shapes=[pltpu.SemaphoreType.DMA((n,)), pltpu.SemaphoreType.DMA((n,))] if n else [],
        compiler_params=_params(("arbitrary",)),
    )(dob, wout, r, z, a, z, seg, *swap)


DZ_WIDTHS = (512, 512, 1024, 1024, 1024, 1024, 1024, 1024)


def _du_norm_bwd(dzs, dzf, wt, wft, hpad, g, dopad, parts=()):
    tm, tk = 544, 1024
    nk = WMAIN // tk
    ni = T // tm
    n = len(parts)

    def body(rq_ref, rk_ref, rv_ref, rg_ref, fq_ref, fk_ref, fv_ref, fg_ref, dzf_ref, w_ref, wf_ref, h_ref, g_ref,
             do_ref, *rest):
        part_refs, (gh_ref, dg_ref), land_refs = rest[:n], rest[n:n + 2], rest[n + 2:2 * n + 2]
        acc = rest[2 * n + 2]
        i, k = pl.program_id(0), pl.program_id(1)

        if n:
            send_sems, recv_sems = rest[2 * n + 3:]
            copies = _chip_copies(part_refs, land_refs, send_sems, recv_sems, by_dest=True)

            @pl.when((i == 0) & (k == 0))
            def _():
                for cp in copies:
                    cp.start()

            @pl.when((i == ni - 1) & (k == nk - 1))
            def _():
                for cp in copies:
                    cp.wait()

        @pl.when(k == 0)
        def _():
            acc[...] = (_dot(dzf_ref[...], wf_ref[...]) + _dot(rq_ref[...], w_ref[:512, :])
                        + _dot(rk_ref[...], w_ref[512:, :]))

        for kk, piece in enumerate((rv_ref, rg_ref, fq_ref, fk_ref, fv_ref, fg_ref), start=1):
            @pl.when(k == kk)
            def _(piece=piece):
                acc[...] += _dot(piece[...], w_ref[...])

        @pl.when(k == nk - 1)
        def _():
            du = acc[...]
            h = h_ref[...]
            gg = g_ref[...]
            rs = lax.rsqrt(jnp.mean(h * h, axis=1, keepdims=True) + EPS)
            hn = h * rs
            part = jnp.sum(du * hn, axis=0, keepdims=True)

            @pl.when(i == 0)
            def _():
                dg_ref[...] = part

            @pl.when(i > 0)
            def _():
                dg_ref[...] += part

            dhn = du * gg
            gh_ref[...] = rs * (dhn - hn * jnp.mean(dhn * hn, axis=1, keepdims=True)) + do_ref[...]

    sems = [pltpu.SemaphoreType.DMA((3 * n,)), pltpu.SemaphoreType.DMA((3 * n,))] if n else []
    return pl.pallas_call(
        body, name="du_norm_bwd", grid=(ni, nk),
        in_specs=[pl.BlockSpec((tm, w), lambda i, k: (i, 0)) for w in DZ_WIDTHS]
        + [pl.BlockSpec((tm, C), lambda i, k: (i, 0)),
           pl.BlockSpec((tk, D), lambda i, k: (k, 0)), pl.BlockSpec((C, D), lambda i, k: (0, 0)),
           pl.BlockSpec((tm, D), lambda i, k: (i, 0)), pl.BlockSpec((1, D), lambda i, k: (0, 0)),
           pl.BlockSpec((tm, D), lambda i, k: (i, 0))] + [ANY] * n,
        out_specs=[pl.BlockSpec((tm, D), lambda i, k: (i, 0)), pl.BlockSpec((1, D), lambda i, k: (0, 0))] + [ANY] * n,
        out_shape=[jax.ShapeDtypeStruct((T, D), F32), jax.ShapeDtypeStruct((1, D), F32)]
        + [jax.ShapeDtypeStruct(p.shape, p.dtype) for p in parts],
        scratch_shapes=[pltpu.VMEM((tm, D), F32)] + sems,
        compiler_params=_params(("arbitrary", "arbitrary")),
    )(*dzs, dzf, wt, wft, hpad, g, dopad, *parts)


GROWS = 7680


def _dw_in(dzs, dzf, ut):
    tn = 512
    nmain = WMAIN // tn
    first, blocks = [], []
    for w in DZ_WIDTHS:
        first.append(sum(blocks))
        blocks.append(w // tn)

    def body(rq_ref, rk_ref, rv_ref, rg_ref, fq_ref, fk_ref, fv_ref, fg_ref, dzf_ref, ut_ref, o_ref):
        gidx = pl.program_id(0)
        for piece, g0, nb in zip((rq_ref, rk_ref, rv_ref, rg_ref, fq_ref, fk_ref, fv_ref, fg_ref), first, blocks):
            @pl.when((gidx >= g0) & (gidx < g0 + nb))
            def _(piece=piece):
                o_ref[...] = _dot(ut_ref[...], piece[...]).T.astype(BF)

        @pl.when(gidx == nmain)
        def _():
            o_ref[:C, :] = _dot(ut_ref[...], dzf_ref[...]).T.astype(BF)
            o_ref[C:, :] = jnp.zeros((tn - C, D), BF)

    def piece_spec(g0, nb):
        return pl.BlockSpec((T, tn), lambda gidx: (0, jnp.clip(gidx - g0, 0, nb - 1)))

    return pl.pallas_call(
        body, name="dw_in", grid=(nmain + 1,),
        in_specs=[piece_spec(g0, nb) for g0, nb in zip(first, blocks)]
        + [pl.BlockSpec((T, C), lambda gidx: (0, 0)), pl.BlockSpec((D, T), lambda gidx: (0, 0))],
        out_specs=pl.BlockSpec((tn, D), lambda gidx: (gidx, 0)),
        out_shape=jax.ShapeDtypeStruct((GROWS, D), BF),
        compiler_params=pltpu.CompilerParams(dimension_semantics=("arbitrary",), vmem_limit_bytes=DW_VMEM_LIMIT),
    )(*dzs, dzf, ut)


def _local_step(x, tgt, normed, norm_g, wt, wft, b_f, wout, final_g, reduce_scatter=False, gather_wout=False):
    cst = _constants()
    hpad, u, ut = normed
    bf_pad = jnp.pad(b_f, ((0, 0), (0, C - NFF)))
    z = _mm_nt(u, wt, WMAIN, T // 2, 1024, "in_proj")
    zf = _mm_nt(u, wft, C, T // 2, C, "in_proj_ff")
    r, sprev = _ret_fwd(z, cst)
    ct = _fox_prep(zf, bf_pad, cst)
    if not gather_wout:
        a, g = _fox_fwd(z, ct, cst, None)
    else:
        a, g, landed_wout = _fox_fwd(z, ct, cst, wout)
        wout = landed_wout.reshape(DMIX, D)
    yt, dopad, dob, loss8, dfg = _out_loss(r, z, a, wout, x, tgt, final_g)
    dwout = _mm_nn(yt, dob, 512, D, "dw_out", BF)
    g_out = [dwout.reshape(4, DMIX // 4, D)] if reduce_scatter else []
    dr, da, dzrg, dzfg, delta, *r_out = _dy_gate_bwd(dob, wout, r, z, a, cst["seg"], g_out)
    p_out = [_add_halves(g_out[0], r_out[0], "pair_add_out", BF)] if reduce_scatter else []
    dzq_r, dzk_r, dzv_r = _ret_bwd(z, cst, sprev, dr)
    dzq_f, drow, dzk_f, dzv_f, dcol, *e_out = _fox_bwd(z, da, g, delta, ct, cst, p_out)
    dzf, dbf = _fox_gate_bwd(drow, dcol, zf, bf_pad, cst)
    dzs = [dzq_r, dzk_r, dzv_r, dzrg, dzq_f, dzk_f, dzv_f, dzfg]
    gwt = _dw_in(dzs, dzf, ut)
    p_in = [_swap_add_windows(gwt)[1]] if reduce_scatter else []
    gh, dng, *e_in = _du_norm_bwd(dzs, dzf, wt, wft, hpad, norm_g, dopad, p_in)
    return (loss8[0, 0], gh[C:], gh[PAD:C], dng, gwt, dbf[:, :NFF], dwout, dfg, p_in + p_out, e_in + e_out)


WOFF, WLEN = 1792, 2048
WHALF = WLEN // 2
LAP = WPADROWS - WOFF


def _own_window(w3):
    rows, sub, lanes = w3.shape
    pad = WPADROWS - rows
    tb = 96
    nb = WPADROWS // tb
    half = rows // 2

    def body(w_ref, o_ref, buf, sems):
        x, y, _ = _place()
        shift = 4 * (2 * x + y)
        buf[pl.ds(0, pad)] = jnp.zeros((pad, sub, lanes), F32)
        buf[pl.ds(rows, pad)] = jnp.zeros((pad, sub, lanes), F32)
        cps = [pltpu.make_async_copy(w_ref.at[pl.ds(half * h, half)], buf.at[pl.ds(shift + half * h, half)],
                                     sems.at[h]) for h in range(2)]
        for cp in cps:
            cp.start()

        def block(i, carry):
            r0 = pl.multiple_of(i * tb, tb)
            o_ref[pl.ds(r0, tb), :] = buf[pl.ds(r0, tb)].reshape(tb, sub * lanes).astype(BF)
            return carry

        cps[0].wait()
        lax.fori_loop(0, half // tb, block, 0)
        cps[1].wait()
        lax.fori_loop(half // tb, nb, block, 0)

    return pl.pallas_call(
        body, name="own_window",
        in_specs=[ANY], out_shape=jax.ShapeDtypeStruct((WPADROWS, sub * lanes), BF),
        scratch_shapes=[pltpu.VMEM((WPADROWS, sub, lanes), F32), pltpu.SemaphoreType.DMA((2,))],
        compiler_params=pltpu.CompilerParams(vmem_limit_bytes=VMEM_LIMIT),
    )(w3)


def _gather_weights(own_win, meta, x, norm_g):
    half_main, half_lap, half_meta = WOFF // 2, LAP // 2, meta.shape[0] // 2
    last = NCH - 1

    def body(win_ref, meta_ref, x_ref, g_ref, w_ref, laps_ref, gm_ref, h_ref, u_ref, ut_ref,
             send_sems, recv_sems, local_sems, stage, lapbuf, headbuf, metabuf):
        step = pl.program_id(0)
        x, y, c = _place()
        me_s = 2 * x + y
        sib = (x, y, 1 - c)
        chips = _other_chips(x, y)

        def emit(h):
            u = _norm_rows(h, g_ref[...])
            h_ref[...] = h
            u_ref[...] = u.astype(BF)
            ut_ref[...] = u.T.astype(BF)

        kinds = [
            (lambda h: win_ref.at[pl.ds(half_main * h, half_main)],
             lambda s, h: w_ref.at[pl.ds(WOFF * s + half_main * h, half_main)]),
            (lambda h: win_ref.at[pl.ds(WOFF + half_lap * h, half_lap)],
             lambda s, h: laps_ref.at[s, pl.ds(half_lap * h, half_lap)]),
            (lambda h: meta_ref.at[pl.ds(half_meta * h, half_meta)],
             lambda s, h: gm_ref.at[s, pl.ds(half_meta * h, half_meta)]),
        ]
        own_in = pltpu.make_async_copy(win_ref.at[pl.ds(0, WOFF)], stage, local_sems.at[0])
        own_lap_in = pltpu.make_async_copy(win_ref.at[pl.ds(WOFF, LAP)], lapbuf.at[0], local_sems.at[1])
        own_out = pltpu.make_async_copy(stage, w_ref.at[pl.ds(WOFF * me_s, WOFF)], local_sems.at[0])
        own_lap_out = pltpu.make_async_copy(lapbuf.at[0], laps_ref.at[me_s], local_sems.at[1])
        sends, arrivals, forwards, forwarded = [], [], [], []
        for a, (src, dst) in enumerate(kinds):
            for k, (cx, cy, cs) in enumerate(chips):
                there = dict(send_sem=send_sems.at[6 * a + k], recv_sem=recv_sems.at[6 * a + k],
                             device_id=(cx, cy, c), device_id_type=MESH)
                across = dict(send_sem=send_sems.at[6 * a + 3 + k], recv_sem=recv_sems.at[6 * a + 3 + k],
                              device_id=sib, device_id_type=MESH)
                sends.append(pltpu.make_async_remote_copy(src_ref=src(c), dst_ref=dst(me_s, c), **there))
                arrivals.append(pltpu.make_async_remote_copy(src_ref=dst(cs, c), dst_ref=dst(cs, c), **there))
                forwards.append(pltpu.make_async_remote_copy(src_ref=dst(cs, c), dst_ref=dst(cs, c), **across))
                forwarded.append(pltpu.make_async_remote_copy(
                    src_ref=dst(cs, 1 - c), dst_ref=dst(cs, 1 - c), **across))

        @pl.when(step == 0)
        def _():
            own_in.start()
            own_lap_in.start()
            for cp in sends:
                cp.start()
            own_in.wait()
            own_out.start()
            own_lap_in.wait()
            own_lap_out.start()

        @pl.when(step < last)
        def _():
            emit(x_ref[...])

        @pl.when(step == last)
        def _():
            for cp, fwd in zip(arrivals, forwards):
                cp.wait_recv()
                fwd.start()
            for cp in forwarded:
                cp.wait_recv()
            for cp in sends + forwards:
                cp.wait_send()
            own_out.wait()
            own_lap_out.wait()
            for s in range(1, 4):
                head = w_ref.at[pl.ds(WOFF * s, LAP)]
                loads = [pltpu.make_async_copy(laps_ref.at[s - 1], lapbuf.at[1], local_sems.at[2]),
                         pltpu.make_async_copy(head, headbuf, local_sems.at[3])]
                for cp in loads:
                    cp.start()
                for cp in loads:
                    cp.wait()
                headbuf[...] = (headbuf[...].astype(F32) + lapbuf[1].astype(F32)).astype(BF)
                store = pltpu.make_async_copy(headbuf, head, local_sems.at[3])
                store.start()
                store.wait()
            loads = [pltpu.make_async_copy(meta_ref, metabuf.at[me_s], local_sems.at[0])]
            loads += [pltpu.make_async_copy(gm_ref.at[cs], metabuf.at[cs], local_sems.at[1 + k])
                      for k, (_, _, cs) in enumerate(chips)]
            for cp in loads:
                cp.start()
            for cp in loads:
                cp.wait()
            tokens = jnp.concatenate([metabuf[s] for s in range(4)], axis=1)
            emit(jnp.concatenate([jnp.zeros((PAD, D), F32), tokens], axis=0))

    def chunk(i):
        return (i + 1) % NCH

    return pl.pallas_call(
        body, name="all_gather_w", grid=(NCH,),
        in_specs=[ANY, ANY, pl.BlockSpec((C, D), lambda i: (jnp.minimum(i, last - 1), 0)),
                  pl.BlockSpec((1, D), lambda i: (0, 0))],
        out_specs=[ANY] * 3 + [pl.BlockSpec((C, D), lambda i: (chunk(i), 0))] * 2
        + [pl.BlockSpec((D, C), lambda i: (0, chunk(i)))],
        out_shape=[jax.ShapeDtypeStruct((WMAIN, D), own_win.dtype), jax.ShapeDtypeStruct((4, LAP, D), own_win.dtype),
                   jax.ShapeDtypeStruct((4,) + meta.shape, meta.dtype),
                   jax.ShapeDtypeStruct((T, D), F32), jax.ShapeDtypeStruct((T, D), BF),
                   jax.ShapeDtypeStruct((D, T), BF)],
        scratch_shapes=[pltpu.SemaphoreType.DMA((18,)), pltpu.SemaphoreType.DMA((18,)), pltpu.SemaphoreType.DMA((4,)),
                        pltpu.VMEM((WOFF, D), own_win.dtype), pltpu.VMEM((2, LAP, D), own_win.dtype),
                        pltpu.VMEM((LAP, D), own_win.dtype), pltpu.VMEM((4,) + meta.shape, meta.dtype)],
        compiler_params=_params(("arbitrary",)),
    )(own_win, meta, x, norm_g)


def _pair_copies(ins, outs, send_sems, recv_sems, n):
    x, y, c = _place()
    sib = dict(device_id=(x, y, 1 - c), device_id_type=MESH)
    cps = []
    for a in range(n):
        rows = ins[a].shape[1] // 2
        cps.append(pltpu.make_async_remote_copy(
            src_ref=ins[a].at[:, pl.ds((1 - c) * rows, rows)], dst_ref=outs[a],
            send_sem=send_sems.at[a], recv_sem=recv_sems.at[a], **sib))
    for k in range(4 * (len(ins) - n)):
        cps.append(pltpu.make_async_remote_copy(
            src_ref=ins[n].at[pl.ds(WOFF * k + (1 - c) * WHALF, WHALF)], dst_ref=outs[n].at[k],
            send_sem=send_sems.at[n + k], recv_sem=recv_sems.at[n + k], **sib))
    return cps


def _swap_add_windows(gwt):
    nchunk = 4
    rows = WHALF // nchunk

    def body(gw_ref, land_ref, out_ref, send_sems, recv_sems, local_sems, own, theirs):
        _, _, c = _place()
        swaps = _pair_copies([gw_ref], [land_ref], send_sems, recv_sems, 0)
        loads = [pltpu.make_async_copy(gw_ref.at[pl.ds(WOFF * k + c * WHALF, WHALF)], own.at[k], local_sems.at[k])
                 for k in range(4)]
        stores = [pltpu.make_async_copy(own.at[k], out_ref.at[k], local_sems.at[k]) for k in range(4)]
        for cp in loads + swaps:
            cp.start()
        for k in range(4):
            swaps[k].wait()
            fetch = pltpu.make_async_copy(land_ref.at[k], theirs, local_sems.at[4])
            fetch.start()
            loads[k].wait()
            fetch.wait()

            def add(i, carry, k=k):
                r = _rows(i, rows)
                own[k, r, :] = (own[k, r, :].astype(F32) + theirs[r, :].astype(F32)).astype(BF)
                return carry

            lax.fori_loop(0, nchunk, add, 0)
            stores[k].start()
        for cp in stores:
            cp.wait()

    return pl.pallas_call(
        body, name="rs_pair_swap_add",
        in_specs=[ANY], out_specs=[ANY, ANY],
        out_shape=[jax.ShapeDtypeStruct((4, WHALF, D), gwt.dtype), jax.ShapeDtypeStruct((4, WHALF, D), BF)],
        scratch_shapes=[pltpu.SemaphoreType.DMA((4,)), pltpu.SemaphoreType.DMA((4,)), pltpu.SemaphoreType.DMA((5,)),
                        pltpu.VMEM((4, WHALF, D), gwt.dtype), pltpu.VMEM((WHALF, D), gwt.dtype)],
        compiler_params=pltpu.CompilerParams(vmem_limit_bytes=VMEM_LIMIT),
    )(gwt)


def _pair_send(halves):
    n = len(halves)

    def body(*refs):
        ins, outs = refs[:n], refs[n:2 * n]
        send_sems, recv_sems = refs[2 * n:]
        x, y, c = _place()
        cps = [pltpu.make_async_remote_copy(
            src_ref=ins[a], dst_ref=outs[a], send_sem=send_sems.at[a], recv_sem=recv_sems.at[a],
            device_id=(x, y, 1 - c), device_id_type=MESH) for a in range(n)]
        for cp in cps:
            cp.start()
        for cp in cps:
            cp.wait()

    return pl.pallas_call(
        body, name="rs_pair_send",
        in_specs=[ANY] * n, out_specs=[ANY] * n,
        out_shape=[jax.ShapeDtypeStruct(h.shape, h.dtype) for h in halves],
        scratch_shapes=[pltpu.SemaphoreType.DMA((n,)), pltpu.SemaphoreType.DMA((n,))],
    )(*halves)


def _row_block(rows):
    for tb in (256, 128, 64, 32, 16, 8):
        if rows % tb == 0:
            return tb
    return rows


def _add_halves(full, recv, name, out_dtype):
    _, r2, w = recv.shape
    tb = _row_block(r2)
    nb = r2 // tb
    c = lax.axis_index("c")

    def body(c_ref, a_ref, b_ref, o_ref):
        o_ref[...] = (a_ref[...].astype(F32) + b_ref[...].astype(F32)).astype(o_ref.dtype)

    return pl.pallas_call(
        body, name=name,
        grid_spec=pltpu.PrefetchScalarGridSpec(
            num_scalar_prefetch=1, grid=(4, nb),
            in_specs=[pl.BlockSpec((1, tb, w), lambda s, i, cr: (s, cr[0] * nb + i, 0)),
                      pl.BlockSpec((1, tb, w), lambda s, i, cr: (s, i, 0))],
            out_specs=pl.BlockSpec((1, tb, w), lambda s, i, cr: (s, i, 0))),
        out_shape=jax.ShapeDtypeStruct(recv.shape, out_dtype),
        compiler_params=_params(("parallel", "parallel")),
    )(jnp.reshape(c, (1,)).astype(jnp.int32), full, recv)


def _add2(a, b, name):
    def body(a_ref, b_ref, o_ref):
        o_ref[...] = a_ref[...] + b_ref[...]

    return pl.pallas_call(body, name=name, out_shape=jax.ShapeDtypeStruct(a.shape, a.dtype))(a, b)


def _sum4(buf, own, name, exchange=None):
    _, r, w = buf.shape
    tb = _row_block(r)
    nsteps = r // tb
    me_s = 2 * lax.axis_index("x") + lax.axis_index("y")
    by_dest = own.ndim == 3
    carried = [] if exchange is None else [*exchange[0], exchange[1]]
    m = len(carried)

    def body(s_ref, b_ref, own_ref, *rest):
        o_ref = rest[m]
        if m:
            ins, outs, (send_sems, recv_sems) = rest[:m], rest[m + 1:2 * m + 1], rest[2 * m + 1:]
            cps = _chip_copies(ins[:-1], outs[:-1], send_sems, recv_sems, by_dest=True)
            cps += _chip_copies(ins[-1:], outs[-1:], send_sems.at[pl.ds(3 * (m - 1), 3)],
                                recv_sems.at[pl.ds(3 * (m - 1), 3)], by_dest=False)

            @pl.when(pl.program_id(0) == 0)
            def _():
                for cp in cps:
                    cp.start()

            @pl.when(pl.program_id(0) == nsteps - 1)
            def _():
                for cp in cps:
                    cp.wait()

        mine = (own_ref[0] if by_dest else own_ref[...]).astype(F32)
        terms = [jnp.where(s_ref[0] == t, mine, b_ref[t].astype(F32)) for t in range(4)]
        o_ref[...] = ((terms[0] + terms[1]) + terms[2]) + terms[3]

    own_spec = (pl.BlockSpec((1, tb, w), lambda i, sr: (sr[0], i, 0)) if by_dest
                else pl.BlockSpec((tb, w), lambda i, sr: (i, 0)))
    landing = [jax.ShapeDtypeStruct(p.shape, p.dtype) for p in carried[:-1]]
    landing += [jax.ShapeDtypeStruct((4,) + s.shape, s.dtype) for s in carried[-1:]]
    return pl.pallas_call(
        body, name=name,
        grid_spec=pltpu.PrefetchScalarGridSpec(
            num_scalar_prefetch=1, grid=(nsteps,),
            in_specs=[pl.BlockSpec((4, tb, w), lambda i, sr: (0, i, 0)), own_spec] + [ANY] * m,
            out_specs=[pl.BlockSpec((tb, w), lambda i, sr: (i, 0))] + [ANY] * m,
            scratch_shapes=[pltpu.SemaphoreType.DMA((3 * m,)), pltpu.SemaphoreType.DMA((3 * m,))] if m else []),
        out_shape=[jax.ShapeDtypeStruct((r, w), F32)] + landing,
        compiler_params=_params(("arbitrary" if m else "parallel",)),
    )(jnp.reshape(me_s, (1,)).astype(jnp.int32), buf, own, *carried)


def _adamw_math(w, g, m, v):
    mn = B1 * m + (1.0 - B1) * g
    vn = B2 * v + (1.0 - B2) * (g * g)
    m_hat = mn / (1.0 - B1 ** STEP)
    v_hat = vn / (1.0 - B2 ** STEP)
    return -LR * (m_hat / (jnp.sqrt(v_hat) + AEPS) + WD * w), mn, vn


def _adamw(w, g, m, v, name):
    r, c_ = w.shape
    tb = _row_block(r)
    if tb == r and r > 512:
        tb = 256

    def body(w_ref, g_ref, m_ref, v_ref, d_ref, mo_ref, vo_ref):
        d_ref[...], mo_ref[...], vo_ref[...] = _adamw_math(w_ref[...], g_ref[...], m_ref[...], v_ref[...])

    spec = pl.BlockSpec((tb, c_), lambda i: (i, 0))
    return pl.pallas_call(
        body, name=name, grid=(pl.cdiv(r, tb),),
        in_specs=[spec] * 4, out_specs=[spec] * 3,
        out_shape=[jax.ShapeDtypeStruct(w.shape, F32)] * 3,
        compiler_params=_params(("parallel",)),
    )(w, g, m, v)


def _adamw_rows(w, g_mine, g_sib, m, v, name):
    r = w.shape[0]
    tb = 256
    sub, lanes = w.shape[1:]
    nh = g_mine.shape[0] // tb
    nsteps = pl.cdiv(r, tb)
    assert nsteps <= 2 * nh and 4 * 3 + r <= 2 * nh * tb
    x, y, c = _place()
    place = jnp.stack([c, 4 * (2 * x + y)]).astype(jnp.int32)

    def body(p_ref, w_ref, mc_ref, sc_ref, mn_ref, sn_ref, m_ref, v_ref, go_ref, d_ref, mo_ref, vo_ref, buf):
        i = pl.program_id(0)
        for at, blk, mine_ref, sib_ref in ((0, i, mc_ref, sc_ref), (1, jnp.minimum(i + 1, 2 * nh - 1), mn_ref, sn_ref)):
            rows = jnp.where(blk // nh == p_ref[0], mine_ref[...], sib_ref[...])
            buf[tb * at:tb * (at + 1)] = rows.reshape(tb, sub, lanes)
        g = buf[pl.ds(p_ref[1], tb)]
        go_ref[...] = g
        d_ref[...], mo_ref[...], vo_ref[...] = _adamw_math(w_ref[...], g, m_ref[...], v_ref[...])

    def half_spec(ahead, sibling):
        def index(i, pr):
            half = (1 - pr[0]) if sibling else pr[0]
            return (jnp.clip(jnp.minimum(i + ahead, 2 * nh - 1) - nh * half, 0, nh - 1), 0)
        return pl.BlockSpec((tb, sub * lanes), index)

    spec = pl.BlockSpec((tb, sub, lanes), lambda i, pr: (i, 0, 0))
    return pl.pallas_call(
        body, name=name,
        grid_spec=pltpu.PrefetchScalarGridSpec(
            num_scalar_prefetch=1, grid=(nsteps,),
            in_specs=[spec, half_spec(0, False), half_spec(0, True), half_spec(1, False), half_spec(1, True),
                      spec, spec],
            out_specs=[spec] * 4,
            scratch_shapes=[pltpu.VMEM((2 * tb, sub, lanes), F32)]),
        out_shape=[jax.ShapeDtypeStruct(w.shape, F32)] * 4,
        compiler_params=_params(("parallel",)),
    )(place, w, g_mine, g_sib, g_mine, g_sib, m, v)


def _adamw_halves(w, g_mine, g_sib, m, v, name):
    r, c_ = w.shape
    r2 = g_mine.shape[0]
    tb = _row_block(r2)
    nb = r2 // tb
    c = lax.axis_index("c")

    def body(c_ref, w_ref, gm_ref, gs_ref, m_ref, v_ref, g_ref, d_ref, mo_ref, vo_ref):
        g = jnp.where(pl.program_id(0) == c_ref[0], gm_ref[...], gs_ref[...])
        g_ref[...] = g
        d_ref[...], mo_ref[...], vo_ref[...] = _adamw_math(w_ref[...], g, m_ref[...], v_ref[...])

    full = pl.BlockSpec((tb, c_), lambda h, i, cr: (h * nb + i, 0))
    half = pl.BlockSpec((tb, c_), lambda h, i, cr: (i, 0))
    return pl.pallas_call(
        body, name=name,
        grid_spec=pltpu.PrefetchScalarGridSpec(
            num_scalar_prefetch=1, grid=(2, nb),
            in_specs=[full, half, half, full, full], out_specs=[full] * 4),
        out_shape=[jax.ShapeDtypeStruct(w.shape, F32)] * 4,
        compiler_params=_params(("parallel", "parallel")),
    )(jnp.reshape(c, (1,)).astype(jnp.int32), w, g_mine, g_sib, m, v)


def kernel(x, meta_tokens, norm_g, w_in, b_f, w_out, final_g, loss_target, m_meta_tokens, m_norm_g, m_w_in, m_b_f, m_w_out, m_final_g, v_meta_tokens, v_norm_g, v_w_in, v_b_f, v_w_out, v_final_g):
    w3, m3, v3 = [jnp.transpose(jnp.reshape(t[0], (D // C, C, WSH)), (2, 0, 1)) for t in (w_in, m_w_in, v_w_in)]

    wt_main, laps, _, *normed = _gather_weights(_own_window(w3), meta_tokens, x[0], norm_g)
    wft = jnp.pad(laps[3, :NFF], ((0, C - NFF), (0, 0)))
    wout_own = w_out[0].astype(BF)

    loss, gx, dmeta, dng, gwt, dbf, dwout, dfg, (p_in, p_out), (e_in, e_out) = _local_step(
        x[0], loss_target[0], normed, norm_g, wt_main, wft, b_f, wout_own, final_g.reshape(1, D), True, True)

    g_meta = jnp.stack([dmeta[:, 256 * s:256 * (s + 1)] for s in range(4)])
    small = jnp.concatenate([dng, dfg, jnp.pad(dbf, ((0, 0), (0, D - NFF))),
                             jnp.pad(jnp.reshape(loss, (1, 1)), ((0, 0), (0, D - 1))),
                             jnp.zeros((4, D), F32)], axis=0)
    h_in, e_meta, e_small = _sum4(e_in, p_in, "sum_in", exchange=([g_meta], small))
    (h_out,), (h_meta,), (h_small,) = (_sum4(e_out, p_out, "sum_out"), _sum4(e_meta, g_meta, "sum_meta"),
                                       _sum4(e_small, small, "sum_small"))
    s_in, s_out, s_meta, s_small = _pair_send([h_in, h_out, h_meta, h_small])
    gw_meta = _add2(h_meta, s_meta, "pair_add_meta")
    tot = _add2(h_small, s_small, "pair_add_small")
    g_norm, g_final, g_bf, loss_all = tot[0:1], tot[1], tot[2:3, :NFF], tot[3, 0]

    d_meta, nm_meta, nv_meta = _adamw(meta_tokens, gw_meta, m_meta_tokens, v_meta_tokens, "adamw_meta")
    d_norm, nm_norm, nv_norm = _adamw(norm_g, g_norm, m_norm_g, v_norm_g, "adamw_norm")
    outs_in = _adamw_rows(w3, h_in, s_in, m3, v3, "adamw_in")
    gw_in, d_in, nm_in, nv_in = [jnp.reshape(jnp.transpose(t, (1, 2, 0)), (1, D, WSH)) for t in outs_in]
    d_bf, nm_bf, nv_bf = _adamw(b_f, g_bf, m_b_f, v_b_f, "adamw_bf")
    gw_out, d_out, nm_out, nv_out = _adamw_halves(w_out[0], h_out, s_out, m_w_out[0], v_w_out[0], "adamw_out")
    d_fin, nm_fin, nv_fin = _adamw(final_g.reshape(1, D), g_final.reshape(1, D), m_final_g.reshape(1, D),
                                   v_final_g.reshape(1, D), "adamw_final")
    return (loss_all, gx[None], gw_meta, g_norm, gw_in, g_bf, gw_out[None], g_final,
            d_meta, d_norm, d_in, d_bf, d_out[None], d_fin.reshape(D),
            nm_meta, nm_norm, nm_in, nm_bf, nm_out[None], nm_fin.reshape(D),
            nv_meta, nv_norm, nv_in, nv_bf, nv_out[None], nv_fin.reshape(D))
```

```python
import numpy as np
import jax
import jax.numpy as jnp
from jax import lax
from jax.experimental import pallas as pl
from jax.experimental.pallas import tpu as pltpu

D = 1024
SEQ = 2048
NMETA = 16
C = 128
PAD = C - NMETA
T = PAD + NMETA + SEQ
NCH = T // C
RH, RDK, RDV = 4, 128, 256
FH, FD = 16, 64
NPAIR = FH // 2
WMAIN = 7168
NFF = 16
WIN = WMAIN + NFF
WSH = WIN // 4
WPADROWS = 1824
DMIX = 2048
EPS = 1e-6
NEG = -1e30
RSCALE = RDK ** -0.5
FSCALE = FD ** -0.5
ROPE_BASE = 10000.0
LR, B1, B2, AEPS, WD, STEP = 0.001, 0.9, 0.999, 1e-08, 0.01, 10

BF = jnp.bfloat16
F32 = jnp.float32
NT = (((1,), (1,)), ((), ()))
TN = (((0,), (0,)), ((), ()))
NN_DIMS = (((1,), (0,)), ((), ()))
MESH = pl.DeviceIdType.MESH
ANY = pl.BlockSpec(memory_space=pl.ANY)
VMEM_LIMIT = 48 * 1024 * 1024
DW_VMEM_LIMIT = 56 * 1024 * 1024

GB_R, GB_F = 2, 6
QB_F, KB_F, VB_F = 24, 32, 40


def _dot(a, b):
    return jnp.dot(a, b, preferred_element_type=F32)


def _dg(a, b, dims):
    return lax.dot_general(a, b, dims, preferred_element_type=F32)


def _params(sem=None):
    return pltpu.CompilerParams(dimension_semantics=sem, vmem_limit_bytes=VMEM_LIMIT)


def _constants():
    pos = jnp.arange(T, dtype=F32) - PAD
    inv = ROPE_BASE ** (-jnp.arange(0, RDK, 2, dtype=F32) / RDK)
    ang = pos[:, None] * inv[None, :]
    cos, sin = jnp.cos(ang), jnp.sin(ang)
    cos2 = jnp.concatenate([cos, cos], axis=1)
    sin2 = jnp.concatenate([-sin, sin], axis=1)
    log_gamma = jnp.log1p(-jnp.exp2(-5.0 - jnp.arange(RH, dtype=F32)))
    idx = jnp.arange(C, dtype=F32)
    diff = idx[:, None] - idx[None, :]
    dmask = jnp.where(diff[None] >= 0, jnp.exp(log_gamma[:, None, None] * jnp.maximum(diff, 0.0)[None]), 0.0)
    zeta = jnp.exp(log_gamma[:, None] * (C - 1.0 - idx)[None, :])
    xi = jnp.exp(log_gamma[:, None] * (idx + 1.0)[None, :])
    gdec = jnp.exp(log_gamma * C)
    zeta_b = jnp.broadcast_to(zeta[:, :, None], (RH, C, RDK))
    xi_b = jnp.broadcast_to(xi[:, :, None], (RH, C, RDK))
    gdec_b = jnp.broadcast_to(gdec[:, None, None], (RH, RDK, RDV))
    tri = jnp.asarray(np.tril(np.ones((C, C), np.float32)), dtype=BF)
    head_of_lane = np.arange(FH * FD) // FD
    pick = ((np.arange(FH * FD)[:, None] % FD == 0)
            & (head_of_lane[:, None] == np.arange(C)[None, :])).astype(np.float32)
    seg = (np.arange(C)[:, None] // FD == np.arange(C)[None, :] // FD).astype(np.float32)
    ones_aug = np.concatenate([np.tile((np.arange(C) < FD)[None, :], (C, 1)),
                               np.tile((np.arange(C) >= FD)[None, :], (C, 1))], axis=0).astype(np.float32)
    lane = np.arange(2 * C) % C
    causal = np.where(lane[None, :] <= np.arange(C)[:, None], 0.0, NEG).astype(np.float32)
    mask_bias = np.stack([np.zeros((C, 2 * C), np.float32), causal])
    return dict(cos2=cos2, sin2=sin2, dmask=dmask, zeta=zeta_b, xi=xi_b, gdec=gdec_b, tri=tri,
                mask_bias=jnp.asarray(mask_bias), pick=jnp.asarray(pick, dtype=BF), seg=jnp.asarray(seg, dtype=BF),
                ones_aug=jnp.asarray(ones_aug, dtype=BF))


def _norm_rows(h, g):
    return h * lax.rsqrt(jnp.mean(h * h, axis=1, keepdims=True) + EPS) * g


def _mm_nt(a, b, n, tm, tn, name):
    m, k = a.shape

    def body(a_ref, b_ref, o_ref):
        o_ref[...] = _dg(a_ref[...], b_ref[...], NT)

    return pl.pallas_call(
        body, name=name, grid=(m // tm, n // tn),
        in_specs=[pl.BlockSpec((tm, k), lambda i, j: (i, 0)), pl.BlockSpec((tn, k), lambda i, j: (j, 0))],
        out_specs=pl.BlockSpec((tm, tn), lambda i, j: (i, j)),
        out_shape=jax.ShapeDtypeStruct((m, n), F32),
        compiler_params=_params(("parallel", "parallel")),
    )(a, b)


def _mm_nn(a, b, tm, tn, name, out_dtype=F32):
    m, k = a.shape
    _, n = b.shape

    def body(a_ref, b_ref, o_ref):
        o_ref[...] = _dot(a_ref[...], b_ref[...]).astype(out_dtype)

    return pl.pallas_call(
        body, name=name, grid=(m // tm, n // tn),
        in_specs=[pl.BlockSpec((tm, k), lambda i, j: (i, 0)), pl.BlockSpec((k, tn), lambda i, j: (0, j))],
        out_specs=pl.BlockSpec((tm, tn), lambda i, j: (i, j)),
        out_shape=jax.ShapeDtypeStruct((m, n), out_dtype),
        compiler_params=_params(("parallel", "parallel")),
    )(a, b)


def _rot(x, cos2, sin2):
    return x * cos2 + pltpu.roll(x, 64, 1) * sin2


def _ret_specs(chunk):
    whole = lambda shape: pl.BlockSpec(shape, lambda n: (0,) * len(shape))
    return [
        pl.BlockSpec((C, RH * RDK), lambda n: (chunk(n), 0)),
        pl.BlockSpec((C, RH * RDK), lambda n: (chunk(n), 1)),
        pl.BlockSpec((C, RH * RDV), lambda n: (chunk(n), 1)),
        pl.BlockSpec((C, RDK), lambda n: (chunk(n), 0)),
        pl.BlockSpec((C, RDK), lambda n: (chunk(n), 0)),
        whole((RH, C, C)), whole((RH, C, RDK)), whole((RH, C, RDK)), whole((RH, RDK, RDV)),
    ]


def _ret_heads(q_ref, k_ref, v_ref, cos, sin):
    qr = [_rot(q_ref[:, RDK * h:RDK * (h + 1)], cos, sin) for h in range(RH)]
    kr = [_rot(k_ref[:, RDK * h:RDK * (h + 1)], cos, sin) * RSCALE for h in range(RH)]
    vb = [v_ref[:, RDV * h:RDV * (h + 1)].astype(BF) for h in range(RH)]
    return qr, kr, [t.astype(BF) for t in qr], [t.astype(BF) for t in kr], vb


def _ret_fwd(z, cst):
    def body(q_ref, k_ref, v_ref, cos_ref, sin_ref, dm_ref, xi_ref, zt_ref, gd_ref, r_ref, sp_ref, st):
        n = pl.program_id(0)

        @pl.when(n == 0)
        def _():
            st[...] = jnp.zeros_like(st)

        hs = range(RH)
        qr, kr, qb, kb, vb = _ret_heads(q_ref, k_ref, v_ref, cos_ref[...], sin_ref[...])
        sd = [(_dg(qb[h], kb[h], NT) * dm_ref[h]).astype(BF) for h in hs]
        state = [st[h] for h in hs]
        qx = [(qr[h] * xi_ref[h]).astype(BF) for h in hs]
        kz = [(kr[h] * zt_ref[h]).astype(BF) for h in hs]
        out = [_dot(sd[h], vb[h]) + _dot(qx[h], state[h].astype(BF)) for h in hs]
        kv = [_dg(kz[h], vb[h], TN) for h in hs]
        for h in hs:
            sp_ref[0, h] = state[h]
            r_ref[:, RDV * h:RDV * (h + 1)] = out[h]
            st[h] = state[h] * gd_ref[h] + kv[h]

    return pl.pallas_call(
        body, name="ret_fwd", grid=(NCH,),
        in_specs=_ret_specs(lambda n: n),
        out_specs=[pl.BlockSpec((C, RH * RDV), lambda n: (n, 0)),
                   pl.BlockSpec((1, RH, RDK, RDV), lambda n: (n, 0, 0, 0))],
        out_shape=[jax.ShapeDtypeStruct((T, RH * RDV), F32), jax.ShapeDtypeStruct((NCH, RH, RDK, RDV), F32)],
        scratch_shapes=[pltpu.VMEM((RH, RDK, RDV), F32)],
        compiler_params=_params(("arbitrary",)),
    )(z, z, z, cst["cos2"], cst["sin2"], cst["dmask"], cst["xi"], cst["zeta"], cst["gdec"])


def _ret_bwd(z, cst, sprev, dr):
    def body(q_ref, k_ref, v_ref, cos_ref, sin_ref, dm_ref, xi_ref, zt_ref, gd_ref, sp_ref, dr_ref,
             dq_ref, dk_ref, dv_ref, gst):
        i = pl.program_id(0)

        @pl.when(i == 0)
        def _():
            gst[...] = jnp.zeros_like(gst)

        hs = range(RH)
        cos, sin = cos_ref[...], sin_ref[...]
        qr, kr, qb, kb, vb = _ret_heads(q_ref, k_ref, v_ref, cos, sin)
        dm = [dm_ref[h] for h in hs]
        xi = [xi_ref[h] for h in hs]
        zt = [zt_ref[h] for h in hs]
        sd = [(_dg(qb[h], kb[h], NT) * dm[h]).astype(BF) for h in hs]
        qx = [(qr[h] * xi[h]).astype(BF) for h in hs]
        kz = [(kr[h] * zt[h]).astype(BF) for h in hs]
        drb = [dr_ref[:, RDV * h:RDV * (h + 1)] for h in hs]
        sb = [sp_ref[0, h].astype(BF) for h in hs]
        g = [gst[h] for h in hs]
        gb = [t.astype(BF) for t in g]
        ds = [(_dg(drb[h], vb[h], NT) * dm[h]).astype(BF) for h in hs]
        dq = [_dot(ds[h], kb[h]) + _dg(drb[h], sb[h], NT) * xi[h] for h in hs]
        dk = [(_dg(ds[h], qb[h], TN) + _dg(vb[h], gb[h], NT) * zt[h]) * RSCALE for h in hs]
        dv = [_dg(sd[h], drb[h], TN) + _dot(kz[h], gb[h]) for h in hs]
        gn = [g[h] * gd_ref[h] + _dg(qx[h], drb[h], TN) for h in hs]
        for h in hs:
            gst[h] = gn[h]
            dq_ref[:, RDK * h:RDK * (h + 1)] = (dq[h] * cos + pltpu.roll(dq[h] * sin, 64, 1)).astype(BF)
            dk_ref[:, RDK * h:RDK * (h + 1)] = (dk[h] * cos + pltpu.roll(dk[h] * sin, 64, 1)).astype(BF)
            dv_ref[:, RDV * h:RDV * (h + 1)] = dv[h].astype(BF)

    rev = lambda n: NCH - 1 - n
    return pl.pallas_call(
        body, name="ret_bwd", grid=(NCH,),
        in_specs=_ret_specs(rev) + [
            pl.BlockSpec((1, RH, RDK, RDV), lambda n: (rev(n), 0, 0, 0)),
            pl.BlockSpec((C, RH * RDV), lambda n: (rev(n), 0)),
        ],
        out_specs=[pl.BlockSpec((C, RH * RDK), lambda n: (rev(n), 0)),
                   pl.BlockSpec((C, RH * RDK), lambda n: (rev(n), 0)),
                   pl.BlockSpec((C, RH * RDV), lambda n: (rev(n), 0))],
        out_shape=[jax.ShapeDtypeStruct((T, RH * RDK), BF), jax.ShapeDtypeStruct((T, RH * RDK), BF),
                   jax.ShapeDtypeStruct((T, RH * RDV), BF)],
        scratch_shapes=[pltpu.VMEM((RH, RDK, RDV), F32)],
        compiler_params=_params(("arbitrary",)),
    )(z, z, z, cst["cos2"], cst["sin2"], cst["dmask"], cst["xi"], cst["zeta"], cst["gdec"], sprev, dr)


def _place():
    x, y, c = lax.axis_index("x"), lax.axis_index("y"), lax.axis_index("c")
    return x, y, c


def _other_chips(x, y):
    return [(1 - x, y, 2 * (1 - x) + y), (x, 1 - y, 2 * x + (1 - y)), (1 - x, 1 - y, 2 * (1 - x) + (1 - y))]


def _chip_copies(srcs, lands, send_sems, recv_sems, by_dest):
    x, y, c = _place()
    me_s = 2 * x + y
    return [pltpu.make_async_remote_copy(
        src_ref=src.at[cs] if by_dest else src, dst_ref=land.at[me_s],
        send_sem=send_sems.at[3 * a + j], recv_sem=recv_sems.at[3 * a + j],
        device_id=(cx, cy, c), device_id_type=MESH)
        for a, (src, land) in enumerate(zip(srcs, lands)) for j, (cx, cy, cs) in enumerate(_other_chips(x, y))]


def _split_dot(x, mat01, dims=NN_DIMS, x_first=True):
    acc, rest = None, x
    for _ in range(3):
        piece = rest.astype(BF)
        part = _dg(piece, mat01, dims) if x_first else _dg(mat01, piece, dims)
        acc = part if acc is None else acc + part
        rest = rest - piece.astype(F32)
    return acc


def _log_sigmoid(x):
    return -(jnp.maximum(-x, 0.0) + jnp.log1p(jnp.exp(-jnp.abs(x))))


def _fox_prep(zf, bf_pad, cst):
    def body(zf_ref, b_ref, tri_ref, ct_ref, carry):
        n = pl.program_id(0)

        @pl.when(n == 0)
        def _():
            carry[...] = jnp.zeros_like(carry)

        ls = _log_sigmoid(zf_ref[...] + b_ref[...])
        row = n * C + lax.broadcasted_iota(jnp.int32, (C, C), 0)
        lf = jnp.where(row >= PAD, ls, 0.0)
        cc = _split_dot(lf, tri_ref[...], x_first=False) + carry[0:1, :]
        carry[...] = jnp.broadcast_to(cc[C - 1:C, :], carry.shape)
        pos = n * C + lax.broadcasted_iota(jnp.int32, (FH, C), 1)
        ct_ref[0] = jnp.where(pos >= PAD, cc.T[:FH, :], -NEG)

    return pl.pallas_call(
        body, name="fox_prep", grid=(NCH,),
        in_specs=[pl.BlockSpec((C, C), lambda n: (n, 0)), pl.BlockSpec((1, C), lambda n: (0, 0)),
                  pl.BlockSpec((C, C), lambda n: (0, 0))],
        out_specs=pl.BlockSpec((1, FH, C), lambda n: (n, 0, 0)),
        out_shape=jax.ShapeDtypeStruct((NCH, FH, C), F32),
        scratch_shapes=[pltpu.VMEM((8, C), F32)],
        compiler_params=_params(("arbitrary",)),
    )(zf, bf_pad, cst["tri"])


def _lo_lanes(shape):
    return lax.broadcasted_iota(jnp.int32, shape, 1) < FD


def _split_heads(x):
    lo = _lo_lanes(x.shape)
    zero = jnp.zeros_like(x)
    return jnp.concatenate([jnp.where(lo, x, zero), jnp.where(lo, zero, x)], axis=0)


def _spread2(x):
    lo = _lo_lanes(x.shape)
    r = pltpu.roll(x, FD, 1)
    return jnp.concatenate([jnp.where(lo, x, r), jnp.where(lo, r, x)], axis=1)


NSTEP = (NCH + 1) // 2
NTILE = NCH + 1
TROWS = T + C


def _fox_tile(s, t):
    second = t > s
    return second.astype(jnp.int32), jnp.where(second, t - s - 1, s - t)


def _fox_pos(i):
    return jnp.where(i < NSTEP, 2 * i, 2 * (NCH - 1 - i) + 1)


def _fox_pair_columns():
    return pl.BlockSpec((TROWS, C), lambda p, s: (0, p))


def _fox_key_bias(ct_ref, p, j):
    return jnp.concatenate([ct_ref[j, pl.ds(2 * p, 1), :], ct_ref[j, pl.ds(2 * p + 1, 1), :]], axis=1)


def _fox_columns(cols, sems, p):
    def copies(pair, slot):
        return [pltpu.make_async_copy(
            src.at[pl.ds(0, buf.shape[1]), pl.ds(pl.multiple_of((first + pair) * C, C), C)], buf.at[slot],
            sems.at[i, slot]) for i, (src, first, buf) in enumerate(cols)]

    @pl.when(p == 0)
    def _():
        for cp in copies(0, 0):
            cp.start()

    for cp in copies(p, p % 2):
        cp.wait()

    @pl.when(p + 1 < NPAIR)
    def _():
        for cp in copies(p + 1, 1 - p % 2):
            cp.start()


def _rows(block, size=C):
    return pl.ds(pl.multiple_of(block * size, size), size)


def _fox_fwd(z, ct, cst, share):
    n = 0 if share is None else 1

    def body(z_ref, ct_ref, ones_ref, mb_ref, *rest):
        share_refs, (a_ref, g_ref), land_refs = rest[:n], rest[n:n + 2], rest[n + 2:2 * n + 2]
        kks, vvs, q2, m2, sbuf, qbuf, kbuf, vbuf, col_sems = rest[2 * n + 2:2 * n + 11]
        p, s = pl.program_id(0), pl.program_id(1)
        slot = p % 2
        if n:
            send_sems, recv_sems, own_sem = rest[2 * n + 11:]
            x, y, _ = _place()
            copies = _chip_copies(share_refs, land_refs, send_sems, recv_sems, by_dest=False)
            copies.append(pltpu.make_async_copy(share_refs[0], land_refs[0].at[2 * x + y], own_sem.at[0]))

            @pl.when((p == 0) & (s == 0))
            def _():
                for cp in copies:
                    cp.start()

            @pl.when((p == NPAIR - 1) & (s == NSTEP - 1))
            def _():
                for cp in copies:
                    cp.wait()

        @pl.when(s == 0)
        def _():
            ones = ones_ref[...]
            _fox_columns([(z_ref, QB_F, qbuf), (z_ref, KB_F, kbuf), (z_ref, VB_F, vbuf)], col_sems, p)

            def prep(j, carry):
                kks[j] = _split_heads(kbuf[slot, _rows(j), :]).astype(BF)
                vvs[j] = jnp.concatenate([_split_heads(vbuf[slot, _rows(j), :]).astype(BF), ones], axis=1)
                return carry

            lax.fori_loop(0, NCH, prep, 0)

        q2[0] = (qbuf[slot, _rows(s), :] * FSCALE).astype(BF)
        q2[1] = (qbuf[slot, _rows(NCH - 1 - s), :] * FSCALE).astype(BF)

        tiles = [_fox_tile(s, t) for t in range(NTILE)]
        causal = mb_ref[1]
        neg = jnp.full((C, 2 * C), NEG, F32)
        run, first = neg, neg
        for t, (sel, j) in enumerate(tiles):
            st = _dg(q2[sel], kks[j], NT) - _fox_key_bias(ct_ref, p, j)
            if t in (0, NTILE - 1):
                st = st + causal
            sbuf[t] = st
            run = jnp.maximum(jnp.where(t == s + 1, neg, run), st)
            first = jnp.where(t == s, run, first)
        for w, mx in enumerate((first, run)):
            m2[w] = jnp.concatenate(
                [jnp.broadcast_to(jnp.max(mx[:, :C], axis=1, keepdims=True), (C, C)),
                 jnp.broadcast_to(jnp.max(mx[:, C:], axis=1, keepdims=True), (C, C))], axis=1)

        zero = jnp.zeros((C, 2 * C), F32)
        run, first = zero, zero
        for t, (sel, j) in enumerate(tiles):
            run = jnp.where(t == s + 1, zero, run) + _dot(jnp.exp(sbuf[t] - m2[sel]).astype(BF), vvs[j])
            first = jnp.where(t == s, run, first)
        lo = _lo_lanes((C, C))
        for w, res in enumerate((first, run)):
            l = res[:, C:]
            a_ref[_rows(2 * s + w), :] = res[:, :C] / l
            mw = m2[w]
            g_ref[_rows(2 * s + w), :] = -(jnp.where(lo, mw[:, :C], mw[:, C:]) + jnp.log(l))

    col = _fox_pair_columns()
    return pl.pallas_call(
        body, name="fox_fwd", grid=(NPAIR, NSTEP),
        in_specs=[ANY,
                  pl.BlockSpec((NCH, FH, C), lambda p, s: (0, 0, 0)),
                  pl.BlockSpec((2 * C, C), lambda p, s: (0, 0)),
                  pl.BlockSpec((2, C, 2 * C), lambda p, s: (0, 0, 0))] + [ANY] * n,
        out_specs=[col, col] + [ANY] * n,
        out_shape=[jax.ShapeDtypeStruct((TROWS, FH * FD), F32)] * 2
        + ([jax.ShapeDtypeStruct((4,) + share.shape, share.dtype)] if n else []),
        scratch_shapes=[pltpu.VMEM((NCH, 2 * C, C), BF), pltpu.VMEM((NCH, 2 * C, 2 * C), BF),
                        pltpu.VMEM((2, C, C), BF), pltpu.VMEM((2, C, 2 * C), F32),
                        pltpu.VMEM((NTILE, C, 2 * C), F32),
                        pltpu.VMEM((2, T, C), F32), pltpu.VMEM((2, T, C), F32), pltpu.VMEM((2, T, C), F32),
                        pltpu.SemaphoreType.DMA((3, 2))]
        + [pltpu.SemaphoreType.DMA((3,)), pltpu.SemaphoreType.DMA((3,)), pltpu.SemaphoreType.DMA((1,))] * n,
        compiler_params=_params(("arbitrary", "arbitrary")),
    )(z, ct, cst["ones_aug"], cst["mask_bias"], *([share] * n))


def _fox_bwd(z, da, g, delta, ct, cst, parts=()):
    grp = 9

    n = len(parts)

    def body(z_ref, da_ref, g_ref, dl_ref, ct_ref, ones_ref, mb_ref, *rest):
        part_refs, (dq_ref, dr_ref, dk_ref, dv_ref, dcs_ref), land_refs = rest[:n], rest[n:n + 5], rest[n + 5:2 * n + 5]
        (kks, vvs, q2, qq2, dd2, da2, gi2, dl2, dq2, dkacc, dvacc, csacc, qbuf, kbuf, vbuf, dabuf, gbuf,
         dlbuf, col_sems) = rest[2 * n + 5:2 * n + 24]
        p, s = pl.program_id(0), pl.program_id(1)
        slot = p % 2
        ones = ones_ref[...]
        if n:
            copies = _chip_copies(part_refs, land_refs, *rest[2 * n + 24:], by_dest=True)

            @pl.when((p == 0) & (s == 0))
            def _():
                for cp in copies:
                    cp.start()

            @pl.when((p == NPAIR - 1) & (s == NSTEP - 1))
            def _():
                for cp in copies:
                    cp.wait()

        @pl.when(s == 0)
        def _():
            dkacc[...] = jnp.zeros_like(dkacc)
            dvacc[...] = jnp.zeros_like(dvacc)
            csacc[...] = jnp.zeros_like(csacc)
            _fox_columns([(z_ref, QB_F, qbuf), (z_ref, KB_F, kbuf), (z_ref, VB_F, vbuf), (da_ref, 0, dabuf),
                          (g_ref, 0, gbuf), (dl_ref, 0, dlbuf)], col_sems, p)

            def prep(j, carry):
                kks[j] = _split_heads(kbuf[slot, _rows(j), :]).astype(BF)
                vvs[j] = _split_heads(vbuf[slot, _rows(j), :]).astype(BF)
                return carry

            lax.fori_loop(0, NCH, prep, 0)

        for w, (chunk, blk) in enumerate(((s, 2 * s), (NCH - 1 - s, jnp.where(s == NSTEP - 1, 2 * s, 2 * s + 1)))):
            qf = qbuf[slot, _rows(chunk), :]
            q2[w] = (qf * FSCALE).astype(BF)
            qq2[w] = jnp.concatenate([_split_heads(qf).astype(BF), ones], axis=1)
            da2[w] = dabuf[slot, _rows(blk), :]
            dd2[w] = _split_heads(da2[w].astype(F32)).astype(BF)
            gi2[w] = _spread2(gbuf[slot, _rows(blk), :])
            dl2[w] = _spread2(dlbuf[slot, _rows(blk), :])
        dq2[...] = jnp.zeros_like(dq2)
        zero = jnp.zeros((C, 2 * C), F32)

        def group(gi, carry):
            ts = [gi * grp + u for u in range(grp)]
            tiles = [_fox_tile(s, t) for t in ts]
            kk = [kks[j] for _, j in tiles]
            ss = [_dg(q2[sel], kj, NT) + (gi2[sel] - _fox_key_bias(ct_ref, p, j)) for kj, (sel, j) in zip(kk, tiles)]
            ss[0] = ss[0] + mb_ref[(gi == 0).astype(jnp.int32)]
            ss[-1] = ss[-1] + mb_ref[(gi == 1).astype(jnp.int32)]
            dps = [_dg(da2[sel], vvs[j], NT) for sel, j in tiles]
            pes = [jnp.exp(st) for st in ss]
            dss = [pe * (dp - dl2[sel]) * FSCALE for pe, dp, (sel, _) in zip(pes, dps, tiles)]
            pts = [jnp.concatenate([pe[:, :C].T, pe[:, C:].T], axis=1).astype(BF) for pe in pes]
            dsts = [jnp.concatenate([ds[:, :C].T, ds[:, C:].T], axis=1).astype(BF) for ds in dss]
            dvs = [_dot(pt, dd2[sel]) for pt, (sel, _) in zip(pts, tiles)]
            rs = [_dot(dst, qq2[sel]) for dst, (sel, _) in zip(dsts, tiles)]
            parts = [_dot(ds.astype(BF), jnp.concatenate([kj, ones], axis=1)) for ds, kj in zip(dss, kk)]
            for (_, j), dv, rr in zip(tiles, dvs, rs):
                dvacc[_rows(j), :] += dv
                dkacc[_rows(j), :] += rr[:, :C]
                csacc[_rows(j), :] += rr[:, C:]
            pa, pb = zero, zero
            for t, part in zip(ts, parts):
                pa = pa + jnp.where(t <= s, part, zero)
                pb = pb + jnp.where(t <= s, zero, part)
            dq2[0] += pa
            dq2[1] += pb
            return carry

        ntile = jnp.where(s == NSTEP - 1, grp, NTILE)
        lax.fori_loop(0, ntile // grp, group, 0)
        for w, chunk in ((1, NCH - 1 - s), (0, s)):
            res = dq2[w]
            dq_ref[_rows(chunk), :] = res[:, :C].astype(BF)
            dr_ref[_rows(2 * s + w), :] = res[:, C:]

        @pl.when(s == NSTEP - 1)
        def _():
            dk_ref[...] = dkacc[...].astype(BF)
            dv_ref[...] = dvacc[...].astype(BF)
            dcs_ref[...] = csacc[...]

    both = _fox_pair_columns()
    col = pl.BlockSpec((T, C), lambda p, s: (0, p))
    return pl.pallas_call(
        body, name="fox_bwd", grid=(NPAIR, NSTEP),
        in_specs=[ANY] * 4
        + [pl.BlockSpec((NCH, FH, C), lambda p, s: (0, 0, 0)),
           pl.BlockSpec((2 * C, C), lambda p, s: (0, 0)),
           pl.BlockSpec((2, C, 2 * C), lambda p, s: (0, 0, 0))] + [ANY] * n,
        out_specs=[col, both, col, col, col] + [ANY] * n,
        out_shape=[jax.ShapeDtypeStruct((T, FH * FD), BF), jax.ShapeDtypeStruct((TROWS, FH * FD), F32),
                   jax.ShapeDtypeStruct((T, FH * FD), BF), jax.ShapeDtypeStruct((T, FH * FD), BF),
                   jax.ShapeDtypeStruct((T, FH * FD), F32)]
        + [jax.ShapeDtypeStruct(p.shape, p.dtype) for p in parts],
        scratch_shapes=[pltpu.VMEM((NCH, 2 * C, C), BF), pltpu.VMEM((NCH, 2 * C, C), BF),
                        pltpu.VMEM((2, C, C), BF), pltpu.VMEM((2, 2 * C, 2 * C), BF), pltpu.VMEM((2, 2 * C, C), BF),
                        pltpu.VMEM((2, C, C), BF), pltpu.VMEM((2, C, 2 * C), F32), pltpu.VMEM((2, C, 2 * C), F32),
                        pltpu.VMEM((2, C, 2 * C), F32),
                        pltpu.VMEM((T, C), F32), pltpu.VMEM((T, C), F32), pltpu.VMEM((T, C), F32),
                        pltpu.VMEM((2, T, C), F32), pltpu.VMEM((2, T, C), F32), pltpu.VMEM((2, T, C), F32),
                        pltpu.VMEM((2, T, C), BF), pltpu.VMEM((2, T, C), F32), pltpu.VMEM((2, T, C), F32),
                        pltpu.SemaphoreType.DMA((6, 2))]
        + ([pltpu.SemaphoreType.DMA((3 * n,)), pltpu.SemaphoreType.DMA((3 * n,))] if n else []),
        compiler_params=_params(("arbitrary", "arbitrary")),
    )(z, da, g, delta, ct, cst["ones_aug"], cst["mask_bias"], *parts)


def _fox_gate_bwd(drow, dcol, zf, bf_pad, cst):
    def body(dr_ref, dc_ref, zf_ref, b_ref, tri_ref, pick_ref, dff_ref, db_ref, carry):
        s = pl.program_id(0)
        n = NCH - 1 - s

        @pl.when(s == 0)
        def _():
            carry[...] = jnp.zeros_like(carry)
            db_ref[...] = jnp.zeros_like(db_ref)

        dcb = _split_dot((dr_ref[...] - dc_ref[...]) * (1.0 / FSCALE), pick_ref[...])
        suf = _split_dot(dcb, tri_ref[...], TN, x_first=False) + carry[0:1, :]
        carry[...] = jnp.broadcast_to(suf[0:1, :], carry.shape)
        x = zf_ref[...] + b_ref[...]
        row = n * C + lax.broadcasted_iota(jnp.int32, (C, C), 0)
        dff = jnp.where(row >= PAD, suf * (1.0 - jax.nn.sigmoid(x)), 0.0)
        dff_ref[...] = dff.astype(BF)
        db_ref[...] += jnp.sum(dff, axis=0, keepdims=True)

    rev = lambda s: (NCH - 1 - s, 0)
    return pl.pallas_call(
        body, name="fox_gate_bwd", grid=(NCH,),
        in_specs=[pl.BlockSpec((C, FH * FD), lambda s: (_fox_pos(NCH - 1 - s), 0)),
                  pl.BlockSpec((C, FH * FD), rev), pl.BlockSpec((C, C), rev),
                  pl.BlockSpec((1, C), lambda s: (0, 0)), pl.BlockSpec((C, C), lambda s: (0, 0)),
                  pl.BlockSpec((FH * FD, C), lambda s: (0, 0))],
        out_specs=[pl.BlockSpec((C, C), rev), pl.BlockSpec((1, C), lambda s: (0, 0))],
        out_shape=[jax.ShapeDtypeStruct((T, C), BF), jax.ShapeDtypeStruct((1, C), F32)],
        scratch_shapes=[pltpu.VMEM((8, C), F32)],
        compiler_params=_params(("arbitrary",)),
    )(drow, dcol, zf, bf_pad, cst["tri"], cst["pick"])


def _head_norm(r):
    rn, rs = [], []
    for h in range(RH):
        rh = r[:, RDV * h:RDV * (h + 1)]
        s = lax.rsqrt(jnp.mean(rh * rh, axis=1, keepdims=True) + EPS)
        rn.append(rh * s)
        rs.append(s)
    return jnp.concatenate(rn, axis=1), rs


def _gated(r, rg, a, fg):
    rn, _ = _head_norm(r)
    return jnp.concatenate([rn * (rg * jax.nn.sigmoid(rg)), a * (fg * jax.nn.sigmoid(fg))], axis=1)


def _out_loss(r, z, a, wout, x, tgt, fgain):
    def body(r_ref, rg_ref, a_ref, fg_ref, w_ref, x_ref, t_ref, g_ref, yt_ref, do_ref, dob_ref, loss_ref, dg_ref):
        i = pl.program_id(0)

        @pl.when(i == 0)
        def _():
            yt_ref[...] = jnp.zeros_like(yt_ref)
            do_ref[...] = jnp.zeros_like(do_ref)
            dob_ref[...] = jnp.zeros_like(dob_ref)
            loss_ref[...] = jnp.zeros_like(loss_ref)
            dg_ref[...] = jnp.zeros_like(dg_ref)

        @pl.when(i > 0)
        def _():
            y = _gated(r_ref[...], rg_ref[...], a_ref[...], fg_ref[...])
            yt_ref[...] = y.T.astype(BF)
            o = x_ref[...] + _dot(y.astype(BF), w_ref[...])
            rs = lax.rsqrt(jnp.mean(o * o, axis=1, keepdims=True) + EPS)
            on = o * rs
            g = g_ref[...]
            e = on * g - t_ref[...]
            loss_ref[...] += 0.5 * jnp.sum(jnp.mean(e * e, axis=1, keepdims=True))
            dyh = e * (1.0 / D)
            dg_ref[...] += jnp.sum(dyh * on, axis=0, keepdims=True)
            don = dyh * g
            do = rs * (don - on * jnp.mean(don * on, axis=1, keepdims=True))
            do_ref[...] = do
            dob_ref[...] = do.astype(BF)

    tok = lambda i: (jnp.maximum(i - 1, 0), 0)
    return pl.pallas_call(
        body, name="out_loss", grid=(NCH,),
        in_specs=[pl.BlockSpec((C, D), lambda i: (i, 0)), pl.BlockSpec((C, D), lambda i: (i, GB_R)),
                  pl.BlockSpec((C, D), lambda i: (_fox_pos(i), 0)), pl.BlockSpec((C, D), lambda i: (i, GB_F)),
                  pl.BlockSpec((DMIX, D), lambda i: (0, 0)),
                  pl.BlockSpec((C, D), tok), pl.BlockSpec((C, D), tok), pl.BlockSpec((1, D), lambda i: (0, 0))],
        out_specs=[pl.BlockSpec((DMIX, C), lambda i: (0, i)), pl.BlockSpec((C, D), lambda i: (i, 0)),
                   pl.BlockSpec((C, D), lambda i: (i, 0)), pl.BlockSpec((8, C), lambda i: (0, 0)),
                   pl.BlockSpec((1, D), lambda i: (0, 0))],
        out_shape=[jax.ShapeDtypeStruct((DMIX, T), BF), jax.ShapeDtypeStruct((T, D), F32),
                   jax.ShapeDtypeStruct((T, D), BF), jax.ShapeDtypeStruct((8, C), F32),
                   jax.ShapeDtypeStruct((1, D), F32)],
        compiler_params=_params(("arbitrary",)),
    )(r, z, a, z, wout, x, tgt, fgain)


def _silu_and_grad(x):
    s = jax.nn.sigmoid(x)
    return x * s, s * (1.0 + x * (1.0 - s))


def _dy_gate_bwd(dob, wout, r, z, a, seg, swap=()):
    n = len(swap)

    def body(do_ref, w_ref, r_ref, rg_ref, a_ref, fg_ref, seg_ref, *rest):
        (dr_ref, da_ref, drg_ref, dfg_ref, dl_ref) = rest[n:n + 5]
        if n:
            copies = _pair_copies(rest[:n], rest[n + 5:2 * n + 5], *rest[2 * n + 5:], n)

            @pl.when(pl.program_id(0) == 0)
            def _():
                for cp in copies:
                    cp.start()

            @pl.when(pl.program_id(0) == NCH - 1)
            def _():
                for cp in copies:
                    cp.wait()

        dy = _dg(do_ref[...], w_ref[...], NT)
        a_ = a_ref[...]
        rn, rs = _head_norm(r_ref[...])
        silu_rg, dsilu_rg = _silu_and_grad(rg_ref[...])
        silu_fg, dsilu_fg = _silu_and_grad(fg_ref[...])
        dyr, dyf = dy[:, :D], dy[:, D:]
        drn = dyr * silu_rg
        drg_ref[...] = (dyr * rn * dsilu_rg).astype(BF)
        for h in range(RH):
            sl = slice(RDV * h, RDV * (h + 1))
            dh, nh = drn[:, sl], rn[:, sl]
            dr_ref[:, sl] = (rs[h] * (dh - nh * jnp.mean(dh * nh, axis=1, keepdims=True))).astype(BF)
        dab = (dyf * silu_fg).astype(BF)
        da_ref[...] = dab
        dfg_ref[...] = (dyf * a_ * dsilu_fg).astype(BF)
        prod = dab.astype(F32) * a_
        segm = seg_ref[...]
        for p in range(NPAIR):
            sl = slice(C * p, C * (p + 1))
            hi = prod[:, sl].astype(BF)
            lo = (prod[:, sl] - hi.astype(F32)).astype(BF)
            dl_ref[:, sl] = _dot(hi, segm) + _dot(lo, segm)

    row = pl.BlockSpec((C, D), lambda i: (i, 0))
    fox = pl.BlockSpec((C, D), lambda i: (_fox_pos(i), 0))
    return pl.pallas_call(
        body, name="dy_gate_bwd", grid=(NCH,),
        in_specs=[row, pl.BlockSpec((DMIX, D), lambda i: (0, 0)),
                  row, pl.BlockSpec((C, D), lambda i: (i, GB_R)),
                  fox, pl.BlockSpec((C, D), lambda i: (i, GB_F)),
                  pl.BlockSpec((C, C), lambda i: (0, 0))] + [ANY] * n,
        out_specs=[row, fox, row, row, fox] + [ANY] * n,
        out_shape=[jax.ShapeDtypeStruct((T, D), BF), jax.ShapeDtypeStruct((TROWS, D), BF),
                   jax.ShapeDtypeStruct((T, D), BF), jax.ShapeDtypeStruct((T, D), BF),
                   jax.ShapeDtypeStruct((TROWS, D), F32)]
        + [jax.ShapeDtypeStruct((4, s.shape[1] // 2, s.shape[2]), s.dtype) for s in swap],
        scratch_shapes=[pltpu.SemaphoreType.DMA((n,)), pltpu.SemaphoreType.DMA((n,))] if n else [],
        compiler_params=_params(("arbitrary",)),
    )(dob, wout, r, z, a, z, seg, *swap)


DZ_WIDTHS = (512, 512, 1024, 1024, 1024, 1024, 1024, 1024)


def _du_norm_bwd(dzs, dzf, wt, wft, hpad, g, dopad, parts=()):
    tm, tk = 544, 1024
    nk = WMAIN // tk
    ni = T // tm
    n = len(parts)

    def body(rq_ref, rk_ref, rv_ref, rg_ref, fq_ref, fk_ref, fv_ref, fg_ref, dzf_ref, w_ref, wf_ref, h_ref, g_ref,
             do_ref, *rest):
        part_refs, (gh_ref, dg_ref), land_refs = rest[:n], rest[n:n + 2], rest[n + 2:2 * n + 2]
        acc = rest[2 * n + 2]
        i, k = pl.program_id(0), pl.program_id(1)

        if n:
            send_sems, recv_sems = rest[2 * n + 3:]
            copies = _chip_copies(part_refs, land_refs, send_sems, recv_sems, by_dest=True)

            @pl.when((i == 0) & (k == 0))
            def _():
                for cp in copies:
                    cp.start()

            @pl.when((i == ni - 1) & (k == nk - 1))
            def _():
                for cp in copies:
                    cp.wait()

        @pl.when(k == 0)
        def _():
            acc[...] = (_dot(dzf_ref[...], wf_ref[...]) + _dot(rq_ref[...], w_ref[:512, :])
                        + _dot(rk_ref[...], w_ref[512:, :]))

        for kk, piece in enumerate((rv_ref, rg_ref, fq_ref, fk_ref, fv_ref, fg_ref), start=1):
            @pl.when(k == kk)
            def _(piece=piece):
                acc[...] += _dot(piece[...], w_ref[...])

        @pl.when(k == nk - 1)
        def _():
            du = acc[...]
            h = h_ref[...]
            gg = g_ref[...]
            rs = lax.rsqrt(jnp.mean(h * h, axis=1, keepdims=True) + EPS)
            hn = h * rs
            part = jnp.sum(du * hn, axis=0, keepdims=True)

            @pl.when(i == 0)
            def _():
                dg_ref[...] = part

            @pl.when(i > 0)
            def _():
                dg_ref[...] += part

            dhn = du * gg
            gh_ref[...] = rs * (dhn - hn * jnp.mean(dhn * hn, axis=1, keepdims=True)) + do_ref[...]

    sems = [pltpu.SemaphoreType.DMA((3 * n,)), pltpu.SemaphoreType.DMA((3 * n,))] if n else []
    return pl.pallas_call(
        body, name="du_norm_bwd", grid=(ni, nk),
        in_specs=[pl.BlockSpec((tm, w), lambda i, k: (i, 0)) for w in DZ_WIDTHS]
        + [pl.BlockSpec((tm, C), lambda i, k: (i, 0)),
           pl.BlockSpec((tk, D), lambda i, k: (k, 0)), pl.BlockSpec((C, D), lambda i, k: (0, 0)),
           pl.BlockSpec((tm, D), lambda i, k: (i, 0)), pl.BlockSpec((1, D), lambda i, k: (0, 0)),
           pl.BlockSpec((tm, D), lambda i, k: (i, 0))] + [ANY] * n,
        out_specs=[pl.BlockSpec((tm, D), lambda i, k: (i, 0)), pl.BlockSpec((1, D), lambda i, k: (0, 0))] + [ANY] * n,
        out_shape=[jax.ShapeDtypeStruct((T, D), F32), jax.ShapeDtypeStruct((1, D), F32)]
        + [jax.ShapeDtypeStruct(p.shape, p.dtype) for p in parts],
        scratch_shapes=[pltpu.VMEM((tm, D), F32)] + sems,
        compiler_params=_params(("arbitrary", "arbitrary")),
    )(*dzs, dzf, wt, wft, hpad, g, dopad, *parts)


GROWS = 7680


def _dw_in(dzs, dzf, ut):
    tn = 512
    nmain = WMAIN // tn
    first, blocks = [], []
    for w in DZ_WIDTHS:
        first.append(sum(blocks))
        blocks.append(w // tn)

    def body(rq_ref, rk_ref, rv_ref, rg_ref, fq_ref, fk_ref, fv_ref, fg_ref, dzf_ref, ut_ref, o_ref):
        gidx = pl.program_id(0)
        for piece, g0, nb in zip((rq_ref, rk_ref, rv_ref, rg_ref, fq_ref, fk_ref, fv_ref, fg_ref), first, blocks):
            @pl.when((gidx >= g0) & (gidx < g0 + nb))
            def _(piece=piece):
                o_ref[...] = _dot(ut_ref[...], piece[...]).T.astype(BF)

        @pl.when(gidx == nmain)
        def _():
            o_ref[:C, :] = _dot(ut_ref[...], dzf_ref[...]).T.astype(BF)
            o_ref[C:, :] = jnp.zeros((tn - C, D), BF)

    def piece_spec(g0, nb):
        return pl.BlockSpec((T, tn), lambda gidx: (0, jnp.clip(gidx - g0, 0, nb - 1)))

    return pl.pallas_call(
        body, name="dw_in", grid=(nmain + 1,),
        in_specs=[piece_spec(g0, nb) for g0, nb in zip(first, blocks)]
        + [pl.BlockSpec((T, C), lambda gidx: (0, 0)), pl.BlockSpec((D, T), lambda gidx: (0, 0))],
        out_specs=pl.BlockSpec((tn, D), lambda gidx: (gidx, 0)),
        out_shape=jax.ShapeDtypeStruct((GROWS, D), BF),
        compiler_params=pltpu.CompilerParams(dimension_semantics=("arbitrary",), vmem_limit_bytes=DW_VMEM_LIMIT),
    )(*dzs, dzf, ut)


def _local_step(x, tgt, normed, norm_g, wt, wft, b_f, wout, final_g, reduce_scatter=False, gather_wout=False):
    cst = _constants()
    hpad, u, ut = normed
    bf_pad = jnp.pad(b_f, ((0, 0), (0, C - NFF)))
    z = _mm_nt(u, wt, WMAIN, T // 2, 1024, "in_proj")
    zf = _mm_nt(u, wft, C, T // 2, C, "in_proj_ff")
    r, sprev = _ret_fwd(z, cst)
    ct = _fox_prep(zf, bf_pad, cst)
    if not gather_wout:
        a, g = _fox_fwd(z, ct, cst, None)
    else:
        a, g, landed_wout = _fox_fwd(z, ct, cst, wout)
        wout = landed_wout.reshape(DMIX, D)
    yt, dopad, dob, loss8, dfg = _out_loss(r, z, a, wout, x, tgt, final_g)
    dwout = _mm_nn(yt, dob, 512, D, "dw_out", BF)
    g_out = [dwout.reshape(4, DMIX // 4, D)] if reduce_scatter else []
    dr, da, dzrg, dzfg, delta, *r_out = _dy_gate_bwd(dob, wout, r, z, a, cst["seg"], g_out)
    p_out = [_add_halves(g_out[0], r_out[0], "pair_add_out", BF)] if reduce_scatter else []
    dzq_r, dzk_r, dzv_r = _ret_bwd(z, cst, sprev, dr)
    dzq_f, drow, dzk_f, dzv_f, dcol, *e_out = _fox_bwd(z, da, g, delta, ct, cst, p_out)
    dzf, dbf = _fox_gate_bwd(drow, dcol, zf, bf_pad, cst)
    dzs = [dzq_r, dzk_r, dzv_r, dzrg, dzq_f, dzk_f, dzv_f, dzfg]
    gwt = _dw_in(dzs, dzf, ut)
    p_in = [_swap_add_windows(gwt)[1]] if reduce_scatter else []
    gh, dng, *e_in = _du_norm_bwd(dzs, dzf, wt, wft, hpad, norm_g, dopad, p_in)
    return (loss8[0, 0], gh[C:], gh[PAD:C], dng, gwt, dbf[:, :NFF], dwout, dfg, p_in + p_out, e_in + e_out)


WOFF, WLEN = 1792, 2048
WHALF = WLEN // 2
LAP = WPADROWS - WOFF


def _own_window(w3):
    rows, sub, lanes = w3.shape
    pad = WPADROWS - rows
    tb = 96
    nb = WPADROWS // tb
    half = rows // 2

    def body(w_ref, o_ref, buf, sems):
        x, y, _ = _place()
        shift = 4 * (2 * x + y)
        buf[pl.ds(0, pad)] = jnp.zeros((pad, sub, lanes), F32)
        buf[pl.ds(rows, pad)] = jnp.zeros((pad, sub, lanes), F32)
        cps = [pltpu.make_async_copy(w_ref.at[pl.ds(half * h, half)], buf.at[pl.ds(shift + half * h, half)],
                                     sems.at[h]) for h in range(2)]
        for cp in cps:
            cp.start()

        def block(i, carry):
            r0 = pl.multiple_of(i * tb, tb)
            o_ref[pl.ds(r0, tb), :] = buf[pl.ds(r0, tb)].reshape(tb, sub * lanes).astype(BF)
            return carry

        cps[0].wait()
        lax.fori_loop(0, half // tb, block, 0)
        cps[1].wait()
        lax.fori_loop(half // tb, nb, block, 0)

    return pl.pallas_call(
        body, name="own_window",
        in_specs=[ANY], out_shape=jax.ShapeDtypeStruct((WPADROWS, sub * lanes), BF),
        scratch_shapes=[pltpu.VMEM((WPADROWS, sub, lanes), F32), pltpu.SemaphoreType.DMA((2,))],
        compiler_params=pltpu.CompilerParams(vmem_limit_bytes=VMEM_LIMIT),
    )(w3)


def _gather_weights(own_win, meta, x, norm_g):
    half_main, half_lap, half_meta = WOFF // 2, LAP // 2, meta.shape[0] // 2
    last = NCH - 1

    def body(win_ref, meta_ref, x_ref, g_ref, w_ref, laps_ref, gm_ref, h_ref, u_ref, ut_ref,
             send_sems, recv_sems, local_sems, stage, lapbuf, headbuf, metabuf):
        step = pl.program_id(0)
        x, y, c = _place()
        me_s = 2 * x + y
        sib = (x, y, 1 - c)
        chips = _other_chips(x, y)

        def emit(h):
            u = _norm_rows(h, g_ref[...])
            h_ref[...] = h
            u_ref[...] = u.astype(BF)
            ut_ref[...] = u.T.astype(BF)

        kinds = [
            (lambda h: win_ref.at[pl.ds(half_main * h, half_main)],
             lambda s, h: w_ref.at[pl.ds(WOFF * s + half_main * h, half_main)]),
            (lambda h: win_ref.at[pl.ds(WOFF + half_lap * h, half_lap)],
             lambda s, h: laps_ref.at[s, pl.ds(half_lap * h, half_lap)]),
            (lambda h: meta_ref.at[pl.ds(half_meta * h, half_meta)],
             lambda s, h: gm_ref.at[s, pl.ds(half_meta * h, half_meta)]),
        ]
        own_in = pltpu.make_async_copy(win_ref.at[pl.ds(0, WOFF)], stage, local_sems.at[0])
        own_lap_in = pltpu.make_async_copy(win_ref.at[pl.ds(WOFF, LAP)], lapbuf.at[0], local_sems.at[1])
        own_out = pltpu.make_async_copy(stage, w_ref.at[pl.ds(WOFF * me_s, WOFF)], local_sems.at[0])
        own_lap_out = pltpu.make_async_copy(lapbuf.at[0], laps_ref.at[me_s], local_sems.at[1])
        sends, arrivals, forwards, forwarded = [], [], [], []
        for a, (src, dst) in enumerate(kinds):
            for k, (cx, cy, cs) in enumerate(chips):
                there = dict(send_sem=send_sems.at[6 * a + k], recv_sem=recv_sems.at[6 * a + k],
                             device_id=(cx, cy, c), device_id_type=MESH)
                across = dict(send_sem=send_sems.at[6 * a + 3 + k], recv_sem=recv_sems.at[6 * a + 3 + k],
                              device_id=sib, device_id_type=MESH)
                sends.append(pltpu.make_async_remote_copy(src_ref=src(c), dst_ref=dst(me_s, c), **there))
                arrivals.append(pltpu.make_async_remote_copy(src_ref=dst(cs, c), dst_ref=dst(cs, c), **there))
                forwards.append(pltpu.make_async_remote_copy(src_ref=dst(cs, c), dst_ref=dst(cs, c), **across))
                forwarded.append(pltpu.make_async_remote_copy(
                    src_ref=dst(cs, 1 - c), dst_ref=dst(cs, 1 - c), **across))

        @pl.when(step == 0)
        def _():
            own_in.start()
            own_lap_in.start()
            for cp in sends:
                cp.start()
            own_in.wait()
            own_out.start()
            own_lap_in.wait()
            own_lap_out.start()

        @pl.when(step < last)
        def _():
            emit(x_ref[...])

        @pl.when(step == last)
        def _():
            for cp, fwd in zip(arrivals, forwards):
                cp.wait_recv()
                fwd.start()
            for cp in forwarded:
                cp.wait_recv()
            for cp in sends + forwards:
                cp.wait_send()
            own_out.wait()
            own_lap_out.wait()
            for s in range(1, 4):
                head = w_ref.at[pl.ds(WOFF * s, LAP)]
                loads = [pltpu.make_async_copy(laps_ref.at[s - 1], lapbuf.at[1], local_sems.at[2]),
                         pltpu.make_async_copy(head, headbuf, local_sems.at[3])]
                for cp in loads:
                    cp.start()
                for cp in loads:
                    cp.wait()
                headbuf[...] = (headbuf[...].astype(F32) + lapbuf[1].astype(F32)).astype(BF)
                store = pltpu.make_async_copy(headbuf, head, local_sems.at[3])
                store.start()
                store.wait()
            loads = [pltpu.make_async_copy(meta_ref, metabuf.at[me_s], local_sems.at[0])]
            loads += [pltpu.make_async_copy(gm_ref.at[cs], metabuf.at[cs], local_sems.at[1 + k])
                      for k, (_, _, cs) in enumerate(chips)]
            for cp in loads:
                cp.start()
            for cp in loads:
                cp.wait()
            tokens = jnp.concatenate([metabuf[s] for s in range(4)], axis=1)
            emit(jnp.concatenate([jnp.zeros((PAD, D), F32), tokens], axis=0))

    def chunk(i):
        return (i + 1) % NCH

    return pl.pallas_call(
        body, name="all_gather_w", grid=(NCH,),
        in_specs=[ANY, ANY, pl.BlockSpec((C, D), lambda i: (jnp.minimum(i, last - 1), 0)),
                  pl.BlockSpec((1, D), lambda i: (0, 0))],
        out_specs=[ANY] * 3 + [pl.BlockSpec((C, D), lambda i: (chunk(i), 0))] * 2
        + [pl.BlockSpec((D, C), lambda i: (0, chunk(i)))],
        out_shape=[jax.ShapeDtypeStruct((WMAIN, D), own_win.dtype), jax.ShapeDtypeStruct((4, LAP, D), own_win.dtype),
                   jax.ShapeDtypeStruct((4,) + meta.shape, meta.dtype),
                   jax.ShapeDtypeStruct((T, D), F32), jax.ShapeDtypeStruct((T, D), BF),
                   jax.ShapeDtypeStruct((D, T), BF)],
        scratch_shapes=[pltpu.SemaphoreType.DMA((18,)), pltpu.SemaphoreType.DMA((18,)), pltpu.SemaphoreType.DMA((4,)),
                        pltpu.VMEM((WOFF, D), own_win.dtype), pltpu.VMEM((2, LAP, D), own_win.dtype),
                        pltpu.VMEM((LAP, D), own_win.dtype), pltpu.VMEM((4,) + meta.shape, meta.dtype)],
        compiler_params=_params(("arbitrary",)),
    )(own_win, meta, x, norm_g)


def _pair_copies(ins, outs, send_sems, recv_sems, n):
    x, y, c = _place()
    sib = dict(device_id=(x, y, 1 - c), device_id_type=MESH)
    cps = []
    for a in range(n):
        rows = ins[a].shape[1] // 2
        cps.append(pltpu.make_async_remote_copy(
            src_ref=ins[a].at[:, pl.ds((1 - c) * rows, rows)], dst_ref=outs[a],
            send_sem=send_sems.at[a], recv_sem=recv_sems.at[a], **sib))
    for k in range(4 * (len(ins) - n)):
        cps.append(pltpu.make_async_remote_copy(
            src_ref=ins[n].at[pl.ds(WOFF * k + (1 - c) * WHALF, WHALF)], dst_ref=outs[n].at[k],
            send_sem=send_sems.at[n + k], recv_sem=recv_sems.at[n + k], **sib))
    return cps


def _swap_add_windows(gwt):
    nchunk = 4
    rows = WHALF // nchunk

    def body(gw_ref, land_ref, out_ref, send_sems, recv_sems, local_sems, own, theirs):
        _, _, c = _place()
        swaps = _pair_copies([gw_ref], [land_ref], send_sems, recv_sems, 0)
        loads = [pltpu.make_async_copy(gw_ref.at[pl.ds(WOFF * k + c * WHALF, WHALF)], own.at[k], local_sems.at[k])
                 for k in range(4)]
        stores = [pltpu.make_async_copy(own.at[k], out_ref.at[k], local_sems.at[k]) for k in range(4)]
        for cp in loads + swaps:
            cp.start()
        for k in range(4):
            swaps[k].wait()
            fetch = pltpu.make_async_copy(land_ref.at[k], theirs, local_sems.at[4])
            fetch.start()
            loads[k].wait()
            fetch.wait()

            def add(i, carry, k=k):
                r = _rows(i, rows)
                own[k, r, :] = (own[k, r, :].astype(F32) + theirs[r, :].astype(F32)).astype(BF)
                return carry

            lax.fori_loop(0, nchunk, add, 0)
            stores[k].start()
        for cp in stores:
            cp.wait()

    return pl.pallas_call(
        body, name="rs_pair_swap_add",
        in_specs=[ANY], out_specs=[ANY, ANY],
        out_shape=[jax.ShapeDtypeStruct((4, WHALF, D), gwt.dtype), jax.ShapeDtypeStruct((4, WHALF, D), BF)],
        scratch_shapes=[pltpu.SemaphoreType.DMA((4,)), pltpu.SemaphoreType.DMA((4,)), pltpu.SemaphoreType.DMA((5,)),
                        pltpu.VMEM((4, WHALF, D), gwt.dtype), pltpu.VMEM((WHALF, D), gwt.dtype)],
        compiler_params=pltpu.CompilerParams(vmem_limit_bytes=VMEM_LIMIT),
    )(gwt)


def _pair_send(halves):
    n = len(halves)

    def body(*refs):
        ins, outs = refs[:n], refs[n:2 * n]
        send_sems, recv_sems = refs[2 * n:]
        x, y, c = _place()
        cps = [pltpu.make_async_remote_copy(
            src_ref=ins[a], dst_ref=outs[a], send_sem=send_sems.at[a], recv_sem=recv_sems.at[a],
            device_id=(x, y, 1 - c), device_id_type=MESH) for a in range(n)]
        for cp in cps:
            cp.start()
        for cp in cps:
            cp.wait()

    return pl.pallas_call(
        body, name="rs_pair_send",
        in_specs=[ANY] * n, out_specs=[ANY] * n,
        out_shape=[jax.ShapeDtypeStruct(h.shape, h.dtype) for h in halves],
        scratch_shapes=[pltpu.SemaphoreType.DMA((n,)), pltpu.SemaphoreType.DMA((n,))],
    )(*halves)


def _row_block(rows):
    for tb in (256, 128, 64, 32, 16, 8):
        if rows % tb == 0:
            return tb
    return rows


def _add_halves(full, recv, name, out_dtype):
    _, r2, w = recv.shape
    tb = _row_block(r2)
    nb = r2 // tb
    c = lax.axis_index("c")

    def body(c_ref, a_ref, b_ref, o_ref):
        o_ref[...] = (a_ref[...].astype(F32) + b_ref[...].astype(F32)).astype(o_ref.dtype)

    return pl.pallas_call(
        body, name=name,
        grid_spec=pltpu.PrefetchScalarGridSpec(
            num_scalar_prefetch=1, grid=(4, nb),
            in_specs=[pl.BlockSpec((1, tb, w), lambda s, i, cr: (s, cr[0] * nb + i, 0)),
                      pl.BlockSpec((1, tb, w), lambda s, i, cr: (s, i, 0))],
            out_specs=pl.BlockSpec((1, tb, w), lambda s, i, cr: (s, i, 0))),
        out_shape=jax.ShapeDtypeStruct(recv.shape, out_dtype),
        compiler_params=_params(("parallel", "parallel")),
    )(jnp.reshape(c, (1,)).astype(jnp.int32), full, recv)


def _add2(a, b, name):
    def body(a_ref, b_ref, o_ref):
        o_ref[...] = a_ref[...] + b_ref[...]

    return pl.pallas_call(body, name=name, out_shape=jax.ShapeDtypeStruct(a.shape, a.dtype))(a, b)


def _sum4(buf, own, name, exchange=None):
    _, r, w = buf.shape
    tb = _row_block(r)
    nsteps = r // tb
    me_s = 2 * lax.axis_index("x") + lax.axis_index("y")
    by_dest = own.ndim == 3
    carried = [] if exchange is None else [*exchange[0], exchange[1]]
    m = len(carried)

    def body(s_ref, b_ref, own_ref, *rest):
        o_ref = rest[m]
        if m:
            ins, outs, (send_sems, recv_sems) = rest[:m], rest[m + 1:2 * m + 1], rest[2 * m + 1:]
            cps = _chip_copies(ins[:-1], outs[:-1], send_sems, recv_sems, by_dest=True)
            cps += _chip_copies(ins[-1:], outs[-1:], send_sems.at[pl.ds(3 * (m - 1), 3)],
                                recv_sems.at[pl.ds(3 * (m - 1), 3)], by_dest=False)

            @pl.when(pl.program_id(0) == 0)
            def _():
                for cp in cps:
                    cp.start()

            @pl.when(pl.program_id(0) == nsteps - 1)
            def _():
                for cp in cps:
                    cp.wait()

        mine = (own_ref[0] if by_dest else own_ref[...]).astype(F32)
        terms = [jnp.where(s_ref[0] == t, mine, b_ref[t].astype(F32)) for t in range(4)]
        o_ref[...] = ((terms[0] + terms[1]) + terms[2]) + terms[3]

    own_spec = (pl.BlockSpec((1, tb, w), lambda i, sr: (sr[0], i, 0)) if by_dest
                else pl.BlockSpec((tb, w), lambda i, sr: (i, 0)))
    landing = [jax.ShapeDtypeStruct(p.shape, p.dtype) for p in carried[:-1]]
    landing += [jax.ShapeDtypeStruct((4,) + s.shape, s.dtype) for s in carried[-1:]]
    return pl.pallas_call(
        body, name=name,
        grid_spec=pltpu.PrefetchScalarGridSpec(
            num_scalar_prefetch=1, grid=(nsteps,),
            in_specs=[pl.BlockSpec((4, tb, w), lambda i, sr: (0, i, 0)), own_spec] + [ANY] * m,
            out_specs=[pl.BlockSpec((tb, w), lambda i, sr: (i, 0))] + [ANY] * m,
            scratch_shapes=[pltpu.SemaphoreType.DMA((3 * m,)), pltpu.SemaphoreType.DMA((3 * m,))] if m else []),
        out_shape=[jax.ShapeDtypeStruct((r, w), F32)] + landing,
        compiler_params=_params(("arbitrary" if m else "parallel",)),
    )(jnp.reshape(me_s, (1,)).astype(jnp.int32), buf, own, *carried)


def _adamw_math(w, g, m, v):
    mn = B1 * m + (1.0 - B1) * g
    vn = B2 * v + (1.0 - B2) * (g * g)
    m_hat = mn / (1.0 - B1 ** STEP)
    v_hat = vn / (1.0 - B2 ** STEP)
    return -LR * (m_hat / (jnp.sqrt(v_hat) + AEPS) + WD * w), mn, vn


def _adamw(w, g, m, v, name):
    r, c_ = w.shape
    tb = _row_block(r)
    if tb == r and r > 512:
        tb = 256

    def body(w_ref, g_ref, m_ref, v_ref, d_ref, mo_ref, vo_ref):
        d_ref[...], mo_ref[...], vo_ref[...] = _adamw_math(w_ref[...], g_ref[...], m_ref[...], v_ref[...])

    spec = pl.BlockSpec((tb, c_), lambda i: (i, 0))
    return pl.pallas_call(
        body, name=name, grid=(pl.cdiv(r, tb),),
        in_specs=[spec] * 4, out_specs=[spec] * 3,
        out_shape=[jax.ShapeDtypeStruct(w.shape, F32)] * 3,
        compiler_params=_params(("parallel",)),
    )(w, g, m, v)


def _adamw_rows(w, g_mine, g_sib, m, v, name):
    r = w.shape[0]
    tb = 256
    sub, lanes = w.shape[1:]
    nh = g_mine.shape[0] // tb
    nsteps = pl.cdiv(r, tb)
    assert nsteps <= 2 * nh and 4 * 3 + r <= 2 * nh * tb
    x, y, c = _place()
    place = jnp.stack([c, 4 * (2 * x + y)]).astype(jnp.int32)

    def body(p_ref, w_ref, mc_ref, sc_ref, mn_ref, sn_ref, m_ref, v_ref, go_ref, d_ref, mo_ref, vo_ref, buf):
        i = pl.program_id(0)
        for at, blk, mine_ref, sib_ref in ((0, i, mc_ref, sc_ref), (1, jnp.minimum(i + 1, 2 * nh - 1), mn_ref, sn_ref)):
            rows = jnp.where(blk // nh == p_ref[0], mine_ref[...], sib_ref[...])
            buf[tb * at:tb * (at + 1)] = rows.reshape(tb, sub, lanes)
        g = buf[pl.ds(p_ref[1], tb)]
        go_ref[...] = g
        d_ref[...], mo_ref[...], vo_ref[...] = _adamw_math(w_ref[...], g, m_ref[...], v_ref[...])

    def half_spec(ahead, sibling):
        def index(i, pr):
            half = (1 - pr[0]) if sibling else pr[0]
            return (jnp.clip(jnp.minimum(i + ahead, 2 * nh - 1) - nh * half, 0, nh - 1), 0)
        return pl.BlockSpec((tb, sub * lanes), index)

    spec = pl.BlockSpec((tb, sub, lanes), lambda i, pr: (i, 0, 0))
    return pl.pallas_call(
        body, name=name,
        grid_spec=pltpu.PrefetchScalarGridSpec(
            num_scalar_prefetch=1, grid=(nsteps,),
            in_specs=[spec, half_spec(0, False), half_spec(0, True), half_spec(1, False), half_spec(1, True),
                      spec, spec],
            out_specs=[spec] * 4,
            scratch_shapes=[pltpu.VMEM((2 * tb, sub, lanes), F32)]),
        out_shape=[jax.ShapeDtypeStruct(w.shape, F32)] * 4,
        compiler_params=_params(("parallel",)),
    )(place, w, g_mine, g_sib, g_mine, g_sib, m, v)


def _adamw_halves(w, g_mine, g_sib, m, v, name):
    r, c_ = w.shape
    r2 = g_mine.shape[0]
    tb = _row_block(r2)
    nb = r2 // tb
    c = lax.axis_index("c")

    def body(c_ref, w_ref, gm_ref, gs_ref, m_ref, v_ref, g_ref, d_ref, mo_ref, vo_ref):
        g = jnp.where(pl.program_id(0) == c_ref[0], gm_ref[...], gs_ref[...])
        g_ref[...] = g
        d_ref[...], mo_ref[...], vo_ref[...] = _adamw_math(w_ref[...], g, m_ref[...], v_ref[...])

    full = pl.BlockSpec((tb, c_), lambda h, i, cr: (h * nb + i, 0))
    half = pl.BlockSpec((tb, c_), lambda h, i, cr: (i, 0))
    return pl.pallas_call(
        body, name=name,
        grid_spec=pltpu.PrefetchScalarGridSpec(
            num_scalar_prefetch=1, grid=(2, nb),
            in_specs=[full, half, half, full, full], out_specs=[full] * 4),
        out_shape=[jax.ShapeDtypeStruct(w.shape, F32)] * 4,
        compiler_params=_params(("parallel", "parallel")),
    )(jnp.reshape(c, (1,)).astype(jnp.int32), w, g_mine, g_sib, m, v)


def kernel(x, meta_tokens, norm_g, w_in, b_f, w_out, final_g, loss_target, m_meta_tokens, m_norm_g, m_w_in, m_b_f, m_w_out, m_final_g, v_meta_tokens, v_norm_g, v_w_in, v_b_f, v_w_out, v_final_g):
    w3, m3, v3 = [jnp.transpose(jnp.reshape(t[0], (D // C, C, WSH)), (2, 0, 1)) for t in (w_in, m_w_in, v_w_in)]

    wt_main, laps, _, *normed = _gather_weights(_own_window(w3), meta_tokens, x[0], norm_g)
    wft = jnp.pad(laps[3, :NFF], ((0, C - NFF), (0, 0)))
    wout_own = w_out[0].astype(BF)

    loss, gx, dmeta, dng, gwt, dbf, dwout, dfg, (p_in, p_out), (e_in, e_out) = _local_step(
        x[0], loss_target[0], normed, norm_g, wt_main, wft, b_f, wout_own, final_g.reshape(1, D), True, True)

    g_meta = jnp.stack([dmeta[:, 256 * s:256 * (s + 1)] for s in range(4)])
    small = jnp.concatenate([dng, dfg, jnp.pad(dbf, ((0, 0), (0, D - NFF))),
                             jnp.pad(jnp.reshape(loss, (1, 1)), ((0, 0), (0, D - 1))),
                             jnp.zeros((4, D), F32)], axis=0)
    h_in, e_meta, e_small = _sum4(e_in, p_in, "sum_in", exchange=([g_meta], small))
    (h_out,), (h_meta,), (h_small,) = (_sum4(e_out, p_out, "sum_out"), _sum4(e_meta, g_meta, "sum_meta"),
                                       _sum4(e_small, small, "sum_small"))
    s_in, s_out, s_meta, s_small = _pair_send([h_in, h_out, h_meta, h_small])
    gw_meta = _add2(h_meta, s_meta, "pair_add_meta")
    tot = _add2(h_small, s_small, "pair_add_small")
    g_norm, g_final, g_bf, loss_all = tot[0:1], tot[1], tot[2:3, :NFF], tot[3, 0]

    d_meta, nm_meta, nv_meta = _adamw(meta_tokens, gw_meta, m_meta_tokens, v_meta_tokens, "adamw_meta")
    d_norm, nm_norm, nv_norm = _adamw(norm_g, g_norm, m_norm_g, v_norm_g, "adamw_norm")
    outs_in = _adamw_rows(w3, h_in, s_in, m3, v3, "adamw_in")
    gw_in, d_in, nm_in, nv_in = [jnp.reshape(jnp.transpose(t, (1, 2, 0)), (1, D, WSH)) for t in outs_in]
    d_bf, nm_bf, nv_bf = _adamw(b_f, g_bf, m_b_f, v_b_f, "adamw_bf")
    gw_out, d_out, nm_out, nv_out = _adamw_halves(w_out[0], h_out, s_out, m_w_out[0], v_w_out[0], "adamw_out")
    d_fin, nm_fin, nv_fin = _adamw(final_g.reshape(1, D), g_final.reshape(1, D), m_final_g.reshape(1, D),
                                   v_final_g.reshape(1, D), "adamw_final")
    return (loss_all, gx[None], gw_meta, g_norm, gw_in, g_bf, gw_out[None], g_final,
            d_meta, d_norm, d_in, d_bf, d_out[None], d_fin.reshape(D),
            nm_meta, nm_norm, nm_in, nm_bf, nm_out[None], nm_fin.reshape(D),
            nv_meta, nv_norm, nv_in, nv_bf, nv_out[None], nv_fin.reshape(D))
```

```python
import numpy as np
import jax
import jax.numpy as jnp
from jax import lax
from jax.experimental import pallas as pl
from jax.experimental.pallas import tpu as pltpu

D = 1024
SEQ = 2048
NMETA = 16
C = 128
PAD = C - NMETA
T = PAD + NMETA + SEQ
NCH = T // C
RH, RDK, RDV = 4, 128, 256
FH, FD = 16, 64
NPAIR = FH // 2
WMAIN = 7168
NFF = 16
WIN = WMAIN + NFF
WSH = WIN // 4
WPADROWS = 1824
DMIX = 2048
EPS = 1e-6
NEG = -1e30
RSCALE = RDK ** -0.5
FSCALE = FD ** -0.5
ROPE_BASE = 10000.0
LR, B1, B2, AEPS, WD, STEP = 0.001, 0.9, 0.999, 1e-08, 0.01, 10

BF = jnp.bfloat16
F32 = jnp.float32
NT = (((1,), (1,)), ((), ()))
TN = (((0,), (0,)), ((), ()))
NN_DIMS = (((1,), (0,)), ((), ()))
MESH = pl.DeviceIdType.MESH
ANY = pl.BlockSpec(memory_space=pl.ANY)
VMEM_LIMIT = 48 * 1024 * 1024
DW_VMEM_LIMIT = 56 * 1024 * 1024

GB_R, GB_F = 2, 6
QB_F, KB_F, VB_F = 24, 32, 40


def _dot(a, b):
    return jnp.dot(a, b, preferred_element_type=F32)


def _dg(a, b, dims):
    return lax.dot_general(a, b, dims, preferred_element_type=F32)


def _params(sem=None):
    return pltpu.CompilerParams(dimension_semantics=sem, vmem_limit_bytes=VMEM_LIMIT)


def _constants():
    pos = jnp.arange(T, dtype=F32) - PAD
    inv = ROPE_BASE ** (-jnp.arange(0, RDK, 2, dtype=F32) / RDK)
    ang = pos[:, None] * inv[None, :]
    cos, sin = jnp.cos(ang), jnp.sin(ang)
    cos2 = jnp.concatenate([cos, cos], axis=1)
    sin2 = jnp.concatenate([-sin, sin], axis=1)
    log_gamma = jnp.log1p(-jnp.exp2(-5.0 - jnp.arange(RH, dtype=F32)))
    idx = jnp.arange(C, dtype=F32)
    diff = idx[:, None] - idx[None, :]
    dmask = jnp.where(diff[None] >= 0, jnp.exp(log_gamma[:, None, None] * jnp.maximum(diff, 0.0)[None]), 0.0)
    zeta = jnp.exp(log_gamma[:, None] * (C - 1.0 - idx)[None, :])
    xi = jnp.exp(log_gamma[:, None] * (idx + 1.0)[None, :])
    gdec = jnp.exp(log_gamma * C)
    zeta_b = jnp.broadcast_to(zeta[:, :, None], (RH, C, RDK))
    xi_b = jnp.broadcast_to(xi[:, :, None], (RH, C, RDK))
    gdec_b = jnp.broadcast_to(gdec[:, None, None], (RH, RDK, RDV))
    tri = jnp.asarray(np.tril(np.ones((C, C), np.float32)), dtype=BF)
    head_of_lane = np.arange(FH * FD) // FD
    pick = ((np.arange(FH * FD)[:, None] % FD == 0)
            & (head_of_lane[:, None] == np.arange(C)[None, :])).astype(np.float32)
    seg = (np.arange(C)[:, None] // FD == np.arange(C)[None, :] // FD).astype(np.float32)
    ones_aug = np.concatenate([np.tile((np.arange(C) < FD)[None, :], (C, 1)),
                               np.tile((np.arange(C) >= FD)[None, :], (C, 1))], axis=0).astype(np.float32)
    lane = np.arange(2 * C) % C
    causal = np.where(lane[None, :] <= np.arange(C)[:, None], 0.0, NEG).astype(np.float32)
    mask_bias = np.stack([np.zeros((C, 2 * C), np.float32), causal])
    return dict(cos2=cos2, sin2=sin2, dmask=dmask, zeta=zeta_b, xi=xi_b, gdec=gdec_b, tri=tri,
                mask_bias=jnp.asarray(mask_bias), pick=jnp.asarray(pick, dtype=BF), seg=jnp.asarray(seg, dtype=BF),
                ones_aug=jnp.asarray(ones_aug, dtype=BF))


def _norm_rows(h, g):
    return h * lax.rsqrt(jnp.mean(h * h, axis=1, keepdims=True) + EPS) * g


def _mm_nt(a, b, n, tm, tn, name):
    m, k = a.shape

    def body(a_ref, b_ref, o_ref):
        o_ref[...] = _dg(a_ref[...], b_ref[...], NT)

    return pl.pallas_call(
        body, name=name, grid=(m // tm, n // tn),
        in_specs=[pl.BlockSpec((tm, k), lambda i, j: (i, 0)), pl.BlockSpec((tn, k), lambda i, j: (j, 0))],
        out_specs=pl.BlockSpec((tm, tn), lambda i, j: (i, j)),
        out_shape=jax.ShapeDtypeStruct((m, n), F32),
        compiler_params=_params(("parallel", "parallel")),
    )(a, b)


def _mm_nn(a, b, tm, tn, name, out_dtype=F32):
    m, k = a.shape
    _, n = b.shape

    def body(a_ref, b_ref, o_ref):
        o_ref[...] = _dot(a_ref[...], b_ref[...]).astype(out_dtype)

    return pl.pallas_call(
        body, name=name, grid=(m // tm, n // tn),
        in_specs=[pl.BlockSpec((tm, k), lambda i, j: (i, 0)), pl.BlockSpec((k, tn), lambda i, j: (0, j))],
        out_specs=pl.BlockSpec((tm, tn), lambda i, j: (i, j)),
        out_shape=jax.ShapeDtypeStruct((m, n), out_dtype),
        compiler_params=_params(("parallel", "parallel")),
    )(a, b)


def _rot(x, cos2, sin2):
    return x * cos2 + pltpu.roll(x, 64, 1) * sin2


def _ret_specs(chunk):
    whole = lambda shape: pl.BlockSpec(shape, lambda n: (0,) * len(shape))
    return [
        pl.BlockSpec((C, RH * RDK), lambda n: (chunk(n), 0)),
        pl.BlockSpec((C, RH * RDK), lambda n: (chunk(n), 1)),
        pl.BlockSpec((C, RH * RDV), lambda n: (chunk(n), 1)),
        pl.BlockSpec((C, RDK), lambda n: (chunk(n), 0)),
        pl.BlockSpec((C, RDK), lambda n: (chunk(n), 0)),
        whole((RH, C, C)), whole((RH, C, RDK)), whole((RH, C, RDK)), whole((RH, RDK, RDV)),
    ]


def _ret_heads(q_ref, k_ref, v_ref, cos, sin):
    qr = [_rot(q_ref[:, RDK * h:RDK * (h + 1)], cos, sin) for h in range(RH)]
    kr = [_rot(k_ref[:, RDK * h:RDK * (h + 1)], cos, sin) * RSCALE for h in range(RH)]
    vb = [v_ref[:, RDV * h:RDV * (h + 1)].astype(BF) for h in range(RH)]
    return qr, kr, [t.astype(BF) for t in qr], [t.astype(BF) for t in kr], vb


def _ret_fwd(z, cst):
    def body(q_ref, k_ref, v_ref, cos_ref, sin_ref, dm_ref, xi_ref, zt_ref, gd_ref, r_ref, sp_ref, st):
        n = pl.program_id(0)

        @pl.when(n == 0)
        def _():
            st[...] = jnp.zeros_like(st)

        hs = range(RH)
        qr, kr, qb, kb, vb = _ret_heads(q_ref, k_ref, v_ref, cos_ref[...], sin_ref[...])
        sd = [(_dg(qb[h], kb[h], NT) * dm_ref[h]).astype(BF) for h in hs]
        state = [st[h] for h in hs]
        qx = [(qr[h] * xi_ref[h]).astype(BF) for h in hs]
        kz = [(kr[h] * zt_ref[h]).astype(BF) for h in hs]
        out = [_dot(sd[h], vb[h]) + _dot(qx[h], state[h].astype(BF)) for h in hs]
        kv = [_dg(kz[h], vb[h], TN) for h in hs]
        for h in hs:
            sp_ref[0, h] = state[h]
            r_ref[:, RDV * h:RDV * (h + 1)] = out[h]
            st[h] = state[h] * gd_ref[h] + kv[h]

    return pl.pallas_call(
        body, name="ret_fwd", grid=(NCH,),
        in_specs=_ret_specs(lambda n: n),
        out_specs=[pl.BlockSpec((C, RH * RDV), lambda n: (n, 0)),
                   pl.BlockSpec((1, RH, RDK, RDV), lambda n: (n, 0, 0, 0))],
        out_shape=[jax.ShapeDtypeStruct((T, RH * RDV), F32), jax.ShapeDtypeStruct((NCH, RH, RDK, RDV), F32)],
        scratch_shapes=[pltpu.VMEM((RH, RDK, RDV), F32)],
        compiler_params=_params(("arbitrary",)),
    )(z, z, z, cst["cos2"], cst["sin2"], cst["dmask"], cst["xi"], cst["zeta"], cst["gdec"])


def _ret_bwd(z, cst, sprev, dr):
    def body(q_ref, k_ref, v_ref, cos_ref, sin_ref, dm_ref, xi_ref, zt_ref, gd_ref, sp_ref, dr_ref,
             dq_ref, dk_ref, dv_ref, gst):
        i = pl.program_id(0)

        @pl.when(i == 0)
        def _():
            gst[...] = jnp.zeros_like(gst)

        hs = range(RH)
        cos, sin = cos_ref[...], sin_ref[...]
        qr, kr, qb, kb, vb = _ret_heads(q_ref, k_ref, v_ref, cos, sin)
        dm = [dm_ref[h] for h in hs]
        xi = [xi_ref[h] for h in hs]
        zt = [zt_ref[h] for h in hs]
        sd = [(_dg(qb[h], kb[h], NT) * dm[h]).astype(BF) for h in hs]
        qx = [(qr[h] * xi[h]).astype(BF) for h in hs]
        kz = [(kr[h] * zt[h]).astype(BF) for h in hs]
        drb = [dr_ref[:, RDV * h:RDV * (h + 1)] for h in hs]
        sb = [sp_ref[0, h].astype(BF) for h in hs]
        g = [gst[h] for h in hs]
        gb = [t.astype(BF) for t in g]
        ds = [(_dg(drb[h], vb[h], NT) * dm[h]).astype(BF) for h in hs]
        dq = [_dot(ds[h], kb[h]) + _dg(drb[h], sb[h], NT) * xi[h] for h in hs]
        dk = [(_dg(ds[h], qb[h], TN) + _dg(vb[h], gb[h], NT) * zt[h]) * RSCALE for h in hs]
        dv = [_dg(sd[h], drb[h], TN) + _dot(kz[h], gb[h]) for h in hs]
        gn = [g[h] * gd_ref[h] + _dg(qx[h], drb[h], TN) for h in hs]
        for h in hs:
            gst[h] = gn[h]
            dq_ref[:, RDK * h:RDK * (h + 1)] = (dq[h] * cos + pltpu.roll(dq[h] * sin, 64, 1)).astype(BF)
            dk_ref[:, RDK * h:RDK * (h + 1)] = (dk[h] * cos + pltpu.roll(dk[h] * sin, 64, 1)).astype(BF)
            dv_ref[:, RDV * h:RDV * (h + 1)] = dv[h].astype(BF)

    rev = lambda n: NCH - 1 - n
    return pl.pallas_call(
        body, name="ret_bwd", grid=(NCH,),
        in_specs=_ret_specs(rev) + [
            pl.BlockSpec((1, RH, RDK, RDV), lambda n: (rev(n), 0, 0, 0)),
            pl.BlockSpec((C, RH * RDV), lambda n: (rev(n), 0)),
        ],
        out_specs=[pl.BlockSpec((C, RH * RDK), lambda n: (rev(n), 0)),
                   pl.BlockSpec((C, RH * RDK), lambda n: (rev(n), 0)),
                   pl.BlockSpec((C, RH * RDV), lambda n: (rev(n), 0))],
        out_shape=[jax.ShapeDtypeStruct((T, RH * RDK), BF), jax.ShapeDtypeStruct((T, RH * RDK), BF),
                   jax.ShapeDtypeStruct((T, RH * RDV), BF)],
        scratch_shapes=[pltpu.VMEM((RH, RDK, RDV), F32)],
        compiler_params=_params(("arbitrary",)),
    )(z, z, z, cst["cos2"], cst["sin2"], cst["dmask"], cst["xi"], cst["zeta"], cst["gdec"], sprev, dr)


def _place():
    x, y, c = lax.axis_index("x"), lax.axis_index("y"), lax.axis_index("c")
    return x, y, c


def _other_chips(x, y):
    return [(1 - x, y, 2 * (1 - x) + y), (x, 1 - y, 2 * x + (1 - y)), (1 - x, 1 - y, 2 * (1 - x) + (1 - y))]


def _chip_copies(srcs, lands, send_sems, recv_sems, by_dest):
    x, y, c = _place()
    me_s = 2 * x + y
    return [pltpu.make_async_remote_copy(
        src_ref=src.at[cs] if by_dest else src, dst_ref=land.at[me_s],
        send_sem=send_sems.at[3 * a + j], recv_sem=recv_sems.at[3 * a + j],
        device_id=(cx, cy, c), device_id_type=MESH)
        for a, (src, land) in enumerate(zip(srcs, lands)) for j, (cx, cy, cs) in enumerate(_other_chips(x, y))]


def _split_dot(x, mat01, dims=NN_DIMS, x_first=True):
    acc, rest = None, x
    for _ in range(3):
        piece = rest.astype(BF)
        part = _dg(piece, mat01, dims) if x_first else _dg(mat01, piece, dims)
        acc = part if acc is None else acc + part
        rest = rest - piece.astype(F32)
    return acc


def _log_sigmoid(x):
    return -(jnp.maximum(-x, 0.0) + jnp.log1p(jnp.exp(-jnp.abs(x))))


def _fox_prep(zf, bf_pad, cst):
    def body(zf_ref, b_ref, tri_ref, ct_ref, carry):
        n = pl.program_id(0)

        @pl.when(n == 0)
        def _():
            carry[...] = jnp.zeros_like(carry)

        ls = _log_sigmoid(zf_ref[...] + b_ref[...])
        row = n * C + lax.broadcasted_iota(jnp.int32, (C, C), 0)
        lf = jnp.where(row >= PAD, ls, 0.0)
        cc = _split_dot(lf, tri_ref[...], x_first=False) + carry[0:1, :]
        carry[...] = jnp.broadcast_to(cc[C - 1:C, :], carry.shape)
        pos = n * C + lax.broadcasted_iota(jnp.int32, (FH, C), 1)
        ct_ref[0] = jnp.where(pos >= PAD, cc.T[:FH, :], -NEG)

    return pl.pallas_call(
        body, name="fox_prep", grid=(NCH,),
        in_specs=[pl.BlockSpec((C, C), lambda n: (n, 0)), pl.BlockSpec((1, C), lambda n: (0, 0)),
                  pl.BlockSpec((C, C), lambda n: (0, 0))],
        out_specs=pl.BlockSpec((1, FH, C), lambda n: (n, 0, 0)),
        out_shape=jax.ShapeDtypeStruct((NCH, FH, C), F32),
        scratch_shapes=[pltpu.VMEM((8, C), F32)],
        compiler_params=_params(("arbitrary",)),
    )(zf, bf_pad, cst["tri"])


def _lo_lanes(shape):
    return lax.broadcasted_iota(jnp.int32, shape, 1) < FD


def _split_heads(x):
    lo = _lo_lanes(x.shape)
    zero = jnp.zeros_like(x)
    return jnp.concatenate([jnp.where(lo, x, zero), jnp.where(lo, zero, x)], axis=0)


def _spread2(x):
    lo = _lo_lanes(x.shape)
    r = pltpu.roll(x, FD, 1)
    return jnp.concatenate([jnp.where(lo, x, r), jnp.where(lo, r, x)], axis=1)


NSTEP = (NCH + 1) // 2
NTILE = NCH + 1
TROWS = T + C


def _fox_tile(s, t):
    second = t > s
    return second.astype(jnp.int32), jnp.where(second, t - s - 1, s - t)


def _fox_pos(i):
    return jnp.where(i < NSTEP, 2 * i, 2 * (NCH - 1 - i) + 1)


def _fox_pair_columns():
    return pl.BlockSpec((TROWS, C), lambda p, s: (0, p))


def _fox_key_bias(ct_ref, p, j):
    return jnp.concatenate([ct_ref[j, pl.ds(2 * p, 1), :], ct_ref[j, pl.ds(2 * p + 1, 1), :]], axis=1)


def _fox_columns(cols, sems, p):
    def copies(pair, slot):
        return [pltpu.make_async_copy(
            src.at[pl.ds(0, buf.shape[1]), pl.ds(pl.multiple_of((first + pair) * C, C), C)], buf.at[slot],
            sems.at[i, slot]) for i, (src, first, buf) in enumerate(cols)]

    @pl.when(p == 0)
    def _():
        for cp in copies(0, 0):
            cp.start()

    for cp in copies(p, p % 2):
        cp.wait()

    @pl.when(p + 1 < NPAIR)
    def _():
        for cp in copies(p + 1, 1 - p % 2):
            cp.start()


def _rows(block, size=C):
    return pl.ds(pl.multiple_of(block * size, size), size)


def _fox_fwd(z, ct, cst, share):
    n = 0 if share is None else 1

    def body(z_ref, ct_ref, ones_ref, mb_ref, *rest):
        share_refs, (a_ref, g_ref), land_refs = rest[:n], rest[n:n + 2], rest[n + 2:2 * n + 2]
        kks, vvs, q2, m2, sbuf, qbuf, kbuf, vbuf, col_sems = rest[2 * n + 2:2 * n + 11]
        p, s = pl.program_id(0), pl.program_id(1)
        slot = p % 2
        if n:
            send_sems, recv_sems, own_sem = rest[2 * n + 11:]
            x, y, _ = _place()
            copies = _chip_copies(share_refs, land_refs, send_sems, recv_sems, by_dest=False)
            copies.append(pltpu.make_async_copy(share_refs[0], land_refs[0].at[2 * x + y], own_sem.at[0]))

            @pl.when((p == 0) & (s == 0))
            def _():
                for cp in copies:
                    cp.start()

            @pl.when((p == NPAIR - 1) & (s == NSTEP - 1))
            def _():
                for cp in copies:
                    cp.wait()

        @pl.when(s == 0)
        def _():
            ones = ones_ref[...]
            _fox_columns([(z_ref, QB_F, qbuf), (z_ref, KB_F, kbuf), (z_ref, VB_F, vbuf)], col_sems, p)

            def prep(j, carry):
                kks[j] = _split_heads(kbuf[slot, _rows(j), :]).astype(BF)
                vvs[j] = jnp.concatenate([_split_heads(vbuf[slot, _rows(j), :]).astype(BF), ones], axis=1)
                return carry

            lax.fori_loop(0, NCH, prep, 0)

        q2[0] = (qbuf[slot, _rows(s), :] * FSCALE).astype(BF)
        q2[1] = (qbuf[slot, _rows(NCH - 1 - s), :] * FSCALE).astype(BF)

        tiles = [_fox_tile(s, t) for t in range(NTILE)]
        causal = mb_ref[1]
        neg = jnp.full((C, 2 * C), NEG, F32)
        run, first = neg, neg
        for t, (sel, j) in enumerate(tiles):
            st = _dg(q2[sel], kks[j], NT) - _fox_key_bias(ct_ref, p, j)
            if t in (0, NTILE - 1):
                st = st + causal
            sbuf[t] = st
            run = jnp.maximum(jnp.where(t == s + 1, neg, run), st)
            first = jnp.where(t == s, run, first)
        for w, mx in enumerate((first, run)):
            m2[w] = jnp.concatenate(
                [jnp.broadcast_to(jnp.max(mx[:, :C], axis=1, keepdims=True), (C, C)),
                 jnp.broadcast_to(jnp.max(mx[:, C:], axis=1, keepdims=True), (C, C))], axis=1)

        zero = jnp.zeros((C, 2 * C), F32)
        run, first = zero, zero
        for t, (sel, j) in enumerate(tiles):
            run = jnp.where(t == s + 1, zero, run) + _dot(jnp.exp(sbuf[t] - m2[sel]).astype(BF), vvs[j])
            first = jnp.where(t == s, run, first)
        lo = _lo_lanes((C, C))
        for w, res in enumerate((first, run)):
            l = res[:, C:]
            a_ref[_rows(2 * s + w), :] = res[:, :C] / l
            mw = m2[w]
            g_ref[_rows(2 * s + w), :] = -(jnp.where(lo, mw[:, :C], mw[:, C:]) + jnp.log(l))

    col = _fox_pair_columns()
    return pl.pallas_call(
        body, name="fox_fwd", grid=(NPAIR, NSTEP),
        in_specs=[ANY,
                  pl.BlockSpec((NCH, FH, C), lambda p, s: (0, 0, 0)),
                  pl.BlockSpec((2 * C, C), lambda p, s: (0, 0)),
                  pl.BlockSpec((2, C, 2 * C), lambda p, s: (0, 0, 0))] + [ANY] * n,
        out_specs=[col, col] + [ANY] * n,
        out_shape=[jax.ShapeDtypeStruct((TROWS, FH * FD), F32)] * 2
        + ([jax.ShapeDtypeStruct((4,) + share.shape, share.dtype)] if n else []),
        scratch_shapes=[pltpu.VMEM((NCH, 2 * C, C), BF), pltpu.VMEM((NCH, 2 * C, 2 * C), BF),
                        pltpu.VMEM((2, C, C), BF), pltpu.VMEM((2, C, 2 * C), F32),
                        pltpu.VMEM((NTILE, C, 2 * C), F32),
                        pltpu.VMEM((2, T, C), F32), pltpu.VMEM((2, T, C), F32), pltpu.VMEM((2, T, C), F32),
                        pltpu.SemaphoreType.DMA((3, 2))]
        + [pltpu.SemaphoreType.DMA((3,)), pltpu.SemaphoreType.DMA((3,)), pltpu.SemaphoreType.DMA((1,))] * n,
        compiler_params=_params(("arbitrary", "arbitrary")),
    )(z, ct, cst["ones_aug"], cst["mask_bias"], *([share] * n))


def _fox_bwd(z, da, g, delta, ct, cst, parts=()):
    grp = 9

    n = len(parts)

    def body(z_ref, da_ref, g_ref, dl_ref, ct_ref, ones_ref, mb_ref, *rest):
        part_refs, (dq_ref, dr_ref, dk_ref, dv_ref, dcs_ref), land_refs = rest[:n], rest[n:n + 5], rest[n + 5:2 * n + 5]
        (kks, vvs, q2, qq2, dd2, da2, gi2, dl2, dq2, dkacc, dvacc, csacc, qbuf, kbuf, vbuf, dabuf, gbuf,
         dlbuf, col_sems) = rest[2 * n + 5:2 * n + 24]
        p, s = pl.program_id(0), pl.program_id(1)
        slot = p % 2
        ones = ones_ref[...]
        if n:
            copies = _chip_copies(part_refs, land_refs, *rest[2 * n + 24:], by_dest=True)

            @pl.when((p == 0) & (s == 0))
            def _():
                for cp in copies:
                    cp.start()

            @pl.when((p == NPAIR - 1) & (s == NSTEP - 1))
            def _():
                for cp in copies:
                    cp.wait()

        @pl.when(s == 0)
        def _():
            dkacc[...] = jnp.zeros_like(dkacc)
            dvacc[...] = jnp.zeros_like(dvacc)
            csacc[...] = jnp.zeros_like(csacc)
            _fox_columns([(z_ref, QB_F, qbuf), (z_ref, KB_F, kbuf), (z_ref, VB_F, vbuf), (da_ref, 0, dabuf),
                          (g_ref, 0, gbuf), (dl_ref, 0, dlbuf)], col_sems, p)

            def prep(j, carry):
                kks[j] = _split_heads(kbuf[slot, _rows(j), :]).astype(BF)
                vvs[j] = _split_heads(vbuf[slot, _rows(j), :]).astype(BF)
                return carry

            lax.fori_loop(0, NCH, prep, 0)

        for w, (chunk, blk) in enumerate(((s, 2 * s), (NCH - 1 - s, jnp.where(s == NSTEP - 1, 2 * s, 2 * s + 1)))):
            qf = qbuf[slot, _rows(chunk), :]
            q2[w] = (qf * FSCALE).astype(BF)
            qq2[w] = jnp.concatenate([_split_heads(qf).astype(BF), ones], axis=1)
            da2[w] = dabuf[slot, _rows(blk), :]
            dd2[w] = _split_heads(da2[w].astype(F32)).astype(BF)
            gi2[w] = _spread2(gbuf[slot, _rows(blk), :])
            dl2[w] = _spread2(dlbuf[slot, _rows(blk), :])
        dq2[...] = jnp.zeros_like(dq2)
        zero = jnp.zeros((C, 2 * C), F32)

        def group(gi, carry):
            ts = [gi * grp + u for u in range(grp)]
            tiles = [_fox_tile(s, t) for t in ts]
            kk = [kks[j] for _, j in tiles]
            ss = [_dg(q2[sel], kj, NT) + (gi2[sel] - _fox_key_bias(ct_ref, p, j)) for kj, (sel, j) in zip(kk, tiles)]
            ss[0] = ss[0] + mb_ref[(gi == 0).astype(jnp.int32)]
            ss[-1] = ss[-1] + mb_ref[(gi == 1).astype(jnp.int32)]
            dps = [_dg(da2[sel], vvs[j], NT) for sel, j in tiles]
            pes = [jnp.exp(st) for st in ss]
            dss = [pe * (dp - dl2[sel]) * FSCALE for pe, dp, (sel, _) in zip(pes, dps, tiles)]
            pts = [jnp.concatenate([pe[:, :C].T, pe[:, C:].T], axis=1).astype(BF) for pe in pes]
            dsts = [jnp.concatenate([ds[:, :C].T, ds[:, C:].T], axis=1).astype(BF) for ds in dss]
            dvs = [_dot(pt, dd2[sel]) for pt, (sel, _) in zip(pts, tiles)]
            rs = [_dot(dst, qq2[sel]) for dst, (sel, _) in zip(dsts, tiles)]
            parts = [_dot(ds.astype(BF), jnp.concatenate([kj, ones], axis=1)) for ds, kj in zip(dss, kk)]
            for (_, j), dv, rr in zip(tiles, dvs, rs):
                dvacc[_rows(j), :] += dv
                dkacc[_rows(j), :] += rr[:, :C]
                csacc[_rows(j), :] += rr[:, C:]
            pa, pb = zero, zero
            for t, part in zip(ts, parts):
                pa = pa + jnp.where(t <= s, part, zero)
                pb = pb + jnp.where(t <= s, zero, part)
            dq2[0] += pa
            dq2[1] += pb
            return carry

        ntile = jnp.where(s == NSTEP - 1, grp, NTILE)
        lax.fori_loop(0, ntile // grp, group, 0)
        for w, chunk in ((1, NCH - 1 - s), (0, s)):
            res = dq2[w]
            dq_ref[_rows(chunk), :] = res[:, :C].astype(BF)
            dr_ref[_rows(2 * s + w), :] = res[:, C:]

        @pl.when(s == NSTEP - 1)
        def _():
            dk_ref[...] = dkacc[...].astype(BF)
            dv_ref[...] = dvacc[...].astype(BF)
            dcs_ref[...] = csacc[...]

    both = _fox_pair_columns()
    col = pl.BlockSpec((T, C), lambda p, s: (0, p))
    return pl.pallas_call(
        body, name="fox_bwd", grid=(NPAIR, NSTEP),
        in_specs=[ANY] * 4
        + [pl.BlockSpec((NCH, FH, C), lambda p, s: (0, 0, 0)),
           pl.BlockSpec((2 * C, C), lambda p, s: (0, 0)),
           pl.BlockSpec((2, C, 2 * C), lambda p, s: (0, 0, 0))] + [ANY] * n,
        out_specs=[col, both, col, col, col] + [ANY] * n,
        out_shape=[jax.ShapeDtypeStruct((T, FH * FD), BF), jax.ShapeDtypeStruct((TROWS, FH * FD), F32),
                   jax.ShapeDtypeStruct((T, FH * FD), BF), jax.ShapeDtypeStruct((T, FH * FD), BF),
                   jax.ShapeDtypeStruct((T, FH * FD), F32)]
        + [jax.ShapeDtypeStruct(p.shape, p.dtype) for p in parts],
        scratch_shapes=[pltpu.VMEM((NCH, 2 * C, C), BF), pltpu.VMEM((NCH, 2 * C, C), BF),
                        pltpu.VMEM((2, C, C), BF), pltpu.VMEM((2, 2 * C, 2 * C), BF), pltpu.VMEM((2, 2 * C, C), BF),
                        pltpu.VMEM((2, C, C), BF), pltpu.VMEM((2, C, 2 * C), F32), pltpu.VMEM((2, C, 2 * C), F32),
                        pltpu.VMEM((2, C, 2 * C), F32),
                        pltpu.VMEM((T, C), F32), pltpu.VMEM((T, C), F32), pltpu.VMEM((T, C), F32),
                        pltpu.VMEM((2, T, C), F32), pltpu.VMEM((2, T, C), F32), pltpu.VMEM((2, T, C), F32),
                        pltpu.VMEM((2, T, C), BF), pltpu.VMEM((2, T, C), F32), pltpu.VMEM((2, T, C), F32),
                        pltpu.SemaphoreType.DMA((6, 2))]
        + ([pltpu.SemaphoreType.DMA((3 * n,)), pltpu.SemaphoreType.DMA((3 * n,))] if n else []),
        compiler_params=_params(("arbitrary", "arbitrary")),
    )(z, da, g, delta, ct, cst["ones_aug"], cst["mask_bias"], *parts)


def _fox_gate_bwd(drow, dcol, zf, bf_pad, cst):
    def body(dr_ref, dc_ref, zf_ref, b_ref, tri_ref, pick_ref, dff_ref, db_ref, carry):
        s = pl.program_id(0)
        n = NCH - 1 - s

        @pl.when(s == 0)
        def _():
            carry[...] = jnp.zeros_like(carry)
            db_ref[...] = jnp.zeros_like(db_ref)

        dcb = _split_dot((dr_ref[...] - dc_ref[...]) * (1.0 / FSCALE), pick_ref[...])
        suf = _split_dot(dcb, tri_ref[...], TN, x_first=False) + carry[0:1, :]
        carry[...] = jnp.broadcast_to(suf[0:1, :], carry.shape)
        x = zf_ref[...] + b_ref[...]
        row = n * C + lax.broadcasted_iota(jnp.int32, (C, C), 0)
        dff = jnp.where(row >= PAD, suf * (1.0 - jax.nn.sigmoid(x)), 0.0)
        dff_ref[...] = dff.astype(BF)
        db_ref[...] += jnp.sum(dff, axis=0, keepdims=True)

    rev = lambda s: (NCH - 1 - s, 0)
    return pl.pallas_call(
        body, name="fox_gate_bwd", grid=(NCH,),
        in_specs=[pl.BlockSpec((C, FH * FD), lambda s: (_fox_pos(NCH - 1 - s), 0)),
                  pl.BlockSpec((C, FH * FD), rev), pl.BlockSpec((C, C), rev),
                  pl.BlockSpec((1, C), lambda s: (0, 0)), pl.BlockSpec((C, C), lambda s: (0, 0)),
                  pl.BlockSpec((FH * FD, C), lambda s: (0, 0))],
        out_specs=[pl.BlockSpec((C, C), rev), pl.BlockSpec((1, C), lambda s: (0, 0))],
        out_shape=[jax.ShapeDtypeStruct((T, C), BF), jax.ShapeDtypeStruct((1, C), F32)],
        scratch_shapes=[pltpu.VMEM((8, C), F32)],
        compiler_params=_params(("arbitrary",)),
    )(drow, dcol, zf, bf_pad, cst["tri"], cst["pick"])


def _head_norm(r):
    rn, rs = [], []
    for h in range(RH):
        rh = r[:, RDV * h:RDV * (h + 1)]
        s = lax.rsqrt(jnp.mean(rh * rh, axis=1, keepdims=True) + EPS)
        rn.append(rh * s)
        rs.append(s)
    return jnp.concatenate(rn, axis=1), rs


def _gated(r, rg, a, fg):
    rn, _ = _head_norm(r)
    return jnp.concatenate([rn * (rg * jax.nn.sigmoid(rg)), a * (fg * jax.nn.sigmoid(fg))], axis=1)


def _out_loss(r, z, a, wout, x, tgt, fgain):
    def body(r_ref, rg_ref, a_ref, fg_ref, w_ref, x_ref, t_ref, g_ref, yt_ref, do_ref, dob_ref, loss_ref, dg_ref):
        i = pl.program_id(0)

        @pl.when(i == 0)
        def _():
            yt_ref[...] = jnp.zeros_like(yt_ref)
            do_ref[...] = jnp.zeros_like(do_ref)
            dob_ref[...] = jnp.zeros_like(dob_ref)
            loss_ref[...] = jnp.zeros_like(loss_ref)
            dg_ref[...] = jnp.zeros_like(dg_ref)

        @pl.when(i > 0)
        def _():
            y = _gated(r_ref[...], rg_ref[...], a_ref[...], fg_ref[...])
            yt_ref[...] = y.T.astype(BF)
            o = x_ref[...] + _dot(y.astype(BF), w_ref[...])
            rs = lax.rsqrt(jnp.mean(o * o, axis=1, keepdims=True) + EPS)
            on = o * rs
            g = g_ref[...]
            e = on * g - t_ref[...]
            loss_ref[...] += 0.5 * jnp.sum(jnp.mean(e * e, axis=1, keepdims=True))
            dyh = e * (1.0 / D)
            dg_ref[...] += jnp.sum(dyh * on, axis=0, keepdims=True)
            don = dyh * g
            do = rs * (don - on * jnp.mean(don * on, axis=1, keepdims=True))
            do_ref[...] = do
            dob_ref[...] = do.astype(BF)

    tok = lambda i: (jnp.maximum(i - 1, 0), 0)
    return pl.pallas_call(
        body, name="out_loss", grid=(NCH,),
        in_specs=[pl.BlockSpec((C, D), lambda i: (i, 0)), pl.BlockSpec((C, D), lambda i: (i, GB_R)),
                  pl.BlockSpec((C, D), lambda i: (_fox_pos(i), 0)), pl.BlockSpec((C, D), lambda i: (i, GB_F)),
                  pl.BlockSpec((DMIX, D), lambda i: (0, 0)),
                  pl.BlockSpec((C, D), tok), pl.BlockSpec((C, D), tok), pl.BlockSpec((1, D), lambda i: (0, 0))],
        out_specs=[pl.BlockSpec((DMIX, C), lambda i: (0, i)), pl.BlockSpec((C, D), lambda i: (i, 0)),
                   pl.BlockSpec((C, D), lambda i: (i, 0)), pl.BlockSpec((8, C), lambda i: (0, 0)),
                   pl.BlockSpec((1, D), lambda i: (0, 0))],
        out_shape=[jax.ShapeDtypeStruct((DMIX, T), BF), jax.ShapeDtypeStruct((T, D), F32),
                   jax.ShapeDtypeStruct((T, D), BF), jax.ShapeDtypeStruct((8, C), F32),
                   jax.ShapeDtypeStruct((1, D), F32)],
        compiler_params=_params(("arbitrary",)),
    )(r, z, a, z, wout, x, tgt, fgain)


def _silu_and_grad(x):
    s = jax.nn.sigmoid(x)
    return x * s, s * (1.0 + x * (1.0 - s))


def _dy_gate_bwd(dob, wout, r, z, a, seg, swap=()):
    n = len(swap)

    def body(do_ref, w_ref, r_ref, rg_ref, a_ref, fg_ref, seg_ref, *rest):
        (dr_ref, da_ref, drg_ref, dfg_ref, dl_ref) = rest[n:n + 5]
        if n:
            copies = _pair_copies(rest[:n], rest[n + 5:2 * n + 5], *rest[2 * n + 5:], n)

            @pl.when(pl.program_id(0) == 0)
            def _():
                for cp in copies:
                    cp.start()

            @pl.when(pl.program_id(0) == NCH - 1)
            def _():
                for cp in copies:
                    cp.wait()

        dy = _dg(do_ref[...], w_ref[...], NT)
        a_ = a_ref[...]
        rn, rs = _head_norm(r_ref[...])
        silu_rg, dsilu_rg = _silu_and_grad(rg_ref[...])
        silu_fg, dsilu_fg = _silu_and_grad(fg_ref[...])
        dyr, dyf = dy[:, :D], dy[:, D:]
        drn = dyr * silu_rg
        drg_ref[...] = (dyr * rn * dsilu_rg).astype(BF)
        for h in range(RH):
            sl = slice(RDV * h, RDV * (h + 1))
            dh, nh = drn[:, sl], rn[:, sl]
            dr_ref[:, sl] = (rs[h] * (dh - nh * jnp.mean(dh * nh, axis=1, keepdims=True))).astype(BF)
        dab = (dyf * silu_fg).astype(BF)
        da_ref[...] = dab
        dfg_ref[...] = (dyf * a_ * dsilu_fg).astype(BF)
        prod = dab.astype(F32) * a_
        segm = seg_ref[...]
        for p in range(NPAIR):
            sl = slice(C * p, C * (p + 1))
            hi = prod[:, sl].astype(BF)
            lo = (prod[:, sl] - hi.astype(F32)).astype(BF)
            dl_ref[:, sl] = _dot(hi, segm) + _dot(lo, segm)

    row = pl.BlockSpec((C, D), lambda i: (i, 0))
    fox = pl.BlockSpec((C, D), lambda i: (_fox_pos(i), 0))
    return pl.pallas_call(
        body, name="dy_gate_bwd", grid=(NCH,),
        in_specs=[row, pl.BlockSpec((DMIX, D), lambda i: (0, 0)),
                  row, pl.BlockSpec((C, D), lambda i: (i, GB_R)),
                  fox, pl.BlockSpec((C, D), lambda i: (i, GB_F)),
                  pl.BlockSpec((C, C), lambda i: (0, 0))] + [ANY] * n,
        out_specs=[row, fox, row, row, fox] + [ANY] * n,
        out_shape=[jax.ShapeDtypeStruct((T, D), BF), jax.ShapeDtypeStruct((TROWS, D), BF),
                   jax.ShapeDtypeStruct((T, D), BF), jax.ShapeDtypeStruct((T, D), BF),
                   jax.ShapeDtypeStruct((TROWS, D), F32)]
        + [jax.ShapeDtypeStruct((4, s.shape[1] // 2, s.shape[2]), s.dtype) for s in swap],
        scratch_shapes=[pltpu.SemaphoreType.DMA((n,)), pltpu.SemaphoreType.DMA((n,))] if n else [],
        compiler_params=_params(("arbitrary",)),
    )(dob, wout, r, z, a, z, seg, *swap)


DZ_WIDTHS = (512, 512, 1024, 1024, 1024, 1024, 1024, 1024)


def _du_norm_bwd(dzs, dzf, wt, wft, hpad, g, dopad, parts=()):
    tm, tk = 544, 1024
    nk = WMAIN // tk
    ni = T // tm
    n = len(parts)

    def body(rq_ref, rk_ref, rv_ref, rg_ref, fq_ref, fk_ref, fv_ref, fg_ref, dzf_ref, w_ref, wf_ref, h_ref, g_ref,
             do_ref, *rest):
        part_refs, (gh_ref, dg_ref), land_refs = rest[:n], rest[n:n + 2], rest[n + 2:2 * n + 2]
        acc = rest[2 * n + 2]
        i, k = pl.program_id(0), pl.program_id(1)

        if n:
            send_sems, recv_sems = rest[2 * n + 3:]
            copies = _chip_copies(part_refs, land_refs, send_sems, recv_sems, by_dest=True)

            @pl.when((i == 0) & (k == 0))
            def _():
                for cp in copies:
                    cp.start()

            @pl.when((i == ni - 1) & (k == nk - 1))
            def _():
                for cp in copies:
                    cp.wait()

        @pl.when(k == 0)
        def _():
            acc[...] = (_dot(dzf_ref[...], wf_ref[...]) + _dot(rq_ref[...], w_ref[:512, :])
                        + _dot(rk_ref[...], w_ref[512:, :]))

        for kk, piece in enumerate((rv_ref, rg_ref, fq_ref, fk_ref, fv_ref, fg_ref), start=1):
            @pl.when(k == kk)
            def _(piece=piece):
                acc[...] += _dot(piece[...], w_ref[...])

        @pl.when(k == nk - 1)
        def _():
            du = acc[...]
            h = h_ref[...]
            gg = g_ref[...]
            rs = lax.rsqrt(jnp.mean(h * h, axis=1, keepdims=True) + EPS)
            hn = h * rs
            part = jnp.sum(du * hn, axis=0, keepdims=True)

            @pl.when(i == 0)
            def _():
                dg_ref[...] = part

            @pl.when(i > 0)
            def _():
                dg_ref[...] += part

            dhn = du * gg
            gh_ref[...] = rs * (dhn - hn * jnp.mean(dhn * hn, axis=1, keepdims=True)) + do_ref[...]

    sems = [pltpu.SemaphoreType.DMA((3 * n,)), pltpu.SemaphoreType.DMA((3 * n,))] if n else []
    return pl.pallas_call(
        body, name="du_norm_bwd", grid=(ni, nk),
        in_specs=[pl.BlockSpec((tm, w), lambda i, k: (i, 0)) for w in DZ_WIDTHS]
        + [pl.BlockSpec((tm, C), lambda i, k: (i, 0)),
           pl.BlockSpec((tk, D), lambda i, k: (k, 0)), pl.BlockSpec((C, D), lambda i, k: (0, 0)),
           pl.BlockSpec((tm, D), lambda i, k: (i, 0)), pl.BlockSpec((1, D), lambda i, k: (0, 0)),
           pl.BlockSpec((tm, D), lambda i, k: (i, 0))] + [ANY] * n,
        out_specs=[pl.BlockSpec((tm, D), lambda i, k: (i, 0)), pl.BlockSpec((1, D), lambda i, k: (0, 0))] + [ANY] * n,
        out_shape=[jax.ShapeDtypeStruct((T, D), F32), jax.ShapeDtypeStruct((1, D), F32)]
        + [jax.ShapeDtypeStruct(p.shape, p.dtype) for p in parts],
        scratch_shapes=[pltpu.VMEM((tm, D), F32)] + sems,
        compiler_params=_params(("arbitrary", "arbitrary")),
    )(*dzs, dzf, wt, wft, hpad, g, dopad, *parts)


GROWS = 7680


def _dw_in(dzs, dzf, ut):
    tn = 512
    nmain = WMAIN // tn
    first, blocks = [], []
    for w in DZ_WIDTHS:
        first.append(sum(blocks))
        blocks.append(w // tn)

    def body(rq_ref, rk_ref, rv_ref, rg_ref, fq_ref, fk_ref, fv_ref, fg_ref, dzf_ref, ut_ref, o_ref):
        gidx = pl.program_id(0)
        for piece, g0, nb in zip((rq_ref, rk_ref, rv_ref, rg_ref, fq_ref, fk_ref, fv_ref, fg_ref), first, blocks):
            @pl.when((gidx >= g0) & (gidx < g0 + nb))
            def _(piece=piece):
                o_ref[...] = _dot(ut_ref[...], piece[...]).T.astype(BF)

        @pl.when(gidx == nmain)
        def _():
            o_ref[:C, :] = _dot(ut_ref[...], dzf_ref[...]).T.astype(BF)
            o_ref[C:, :] = jnp.zeros((tn - C, D), BF)

    def piece_spec(g0, nb):
        return pl.BlockSpec((T, tn), lambda gidx: (0, jnp.clip(gidx - g0, 0, nb - 1)))

    return pl.pallas_call(
        body, name="dw_in", grid=(nmain + 1,),
        in_specs=[piece_spec(g0, nb) for g0, nb in zip(first, blocks)]
        + [pl.BlockSpec((T, C), lambda gidx: (0, 0)), pl.BlockSpec((D, T), lambda gidx: (0, 0))],
        out_specs=pl.BlockSpec((tn, D), lambda gidx: (gidx, 0)),
        out_shape=jax.ShapeDtypeStruct((GROWS, D), BF),
        compiler_params=pltpu.CompilerParams(dimension_semantics=("arbitrary",), vmem_limit_bytes=DW_VMEM_LIMIT),
    )(*dzs, dzf, ut)


def _local_step(x, tgt, normed, norm_g, wt, wft, b_f, wout, final_g, reduce_scatter=False, gather_wout=False):
    cst = _constants()
    hpad, u, ut = normed
    bf_pad = jnp.pad(b_f, ((0, 0), (0, C - NFF)))
    z = _mm_nt(u, wt, WMAIN, T // 2, 1024, "in_proj")
    zf = _mm_nt(u, wft, C, T // 2, C, "in_proj_ff")
    r, sprev = _ret_fwd(z, cst)
    ct = _fox_prep(zf, bf_pad, cst)
    if not gather_wout:
        a, g = _fox_fwd(z, ct, cst, None)
    else:
        a, g, landed_wout = _fox_fwd(z, ct, cst, wout)
        wout = landed_wout.reshape(DMIX, D)
    yt, dopad, dob, loss8, dfg = _out_loss(r, z, a, wout, x, tgt, final_g)
    dwout = _mm_nn(yt, dob, 512, D, "dw_out", BF)
    g_out = [dwout.reshape(4, DMIX // 4, D)] if reduce_scatter else []
    dr, da, dzrg, dzfg, delta, *r_out = _dy_gate_bwd(dob, wout, r, z, a, cst["seg"], g_out)
    p_out = [_add_halves(g_out[0], r_out[0], "pair_add_out", BF)] if reduce_scatter else []
    dzq_r, dzk_r, dzv_r = _ret_bwd(z, cst, sprev, dr)
    dzq_f, drow, dzk_f, dzv_f, dcol, *e_out = _fox_bwd(z, da, g, delta, ct, cst, p_out)
    dzf, dbf = _fox_gate_bwd(drow, dcol, zf, bf_pad, cst)
    dzs = [dzq_r, dzk_r, dzv_r, dzrg, dzq_f, dzk_f, dzv_f, dzfg]
    gwt = _dw_in(dzs, dzf, ut)
    p_in = [_swap_add_windows(gwt)[1]] if reduce_scatter else []
    gh, dng, *e_in = _du_norm_bwd(dzs, dzf, wt, wft, hpad, norm_g, dopad, p_in)
    return (loss8[0, 0], gh[C:], gh[PAD:C], dng, gwt, dbf[:, :NFF], dwout, dfg, p_in + p_out, e_in + e_out)


WOFF, WLEN = 1792, 2048
WHALF = WLEN // 2
LAP = WPADROWS - WOFF


def _own_window(w3):
    rows, sub, lanes = w3.shape
    pad = WPADROWS - rows
    tb = 96
    nb = WPADROWS // tb
    half = rows // 2

    def body(w_ref, o_ref, buf, sems):
        x, y, _ = _place()
        shift = 4 * (2 * x + y)
        buf[pl.ds(0, pad)] = jnp.zeros((pad, sub, lanes), F32)
        buf[pl.ds(rows, pad)] = jnp.zeros((pad, sub, lanes), F32)
        cps = [pltpu.make_async_copy(w_ref.at[pl.ds(half * h, half)], buf.at[pl.ds(shift + half * h, half)],
                                     sems.at[h]) for h in range(2)]
        for cp in cps:
            cp.start()

        def block(i, carry):
            r0 = pl.multiple_of(i * tb, tb)
            o_ref[pl.ds(r0, tb), :] = buf[pl.ds(r0, tb)].reshape(tb, sub * lanes).astype(BF)
            return carry

        cps[0].wait()
        lax.fori_loop(0, half // tb, block, 0)
        cps[1].wait()
        lax.fori_loop(half // tb, nb, block, 0)

    return pl.pallas_call(
        body, name="own_window",
        in_specs=[ANY], out_shape=jax.ShapeDtypeStruct((WPADROWS, sub * lanes), BF),
        scratch_shapes=[pltpu.VMEM((WPADROWS, sub, lanes), F32), pltpu.SemaphoreType.DMA((2,))],
        compiler_params=pltpu.CompilerParams(vmem_limit_bytes=VMEM_LIMIT),
    )(w3)


def _gather_weights(own_win, meta, x, norm_g):
    half_main, half_lap, half_meta = WOFF // 2, LAP // 2, meta.shape[0] // 2
    last = NCH - 1

    def body(win_ref, meta_ref, x_ref, g_ref, w_ref, laps_ref, gm_ref, h_ref, u_ref, ut_ref,
             send_sems, recv_sems, local_sems, stage, lapbuf, headbuf, metabuf):
        step = pl.program_id(0)
        x, y, c = _place()
        me_s = 2 * x + y
        sib = (x, y, 1 - c)
        chips = _other_chips(x, y)

        def emit(h):
            u = _norm_rows(h, g_ref[...])
            h_ref[...] = h
            u_ref[...] = u.astype(BF)
            ut_ref[...] = u.T.astype(BF)

        kinds = [
            (lambda h: win_ref.at[pl.ds(half_main * h, half_main)],
             lambda s, h: w_ref.at[pl.ds(WOFF * s + half_main * h, half_main)]),
            (lambda h: win_ref.at[pl.ds(WOFF + half_lap * h, half_lap)],
             lambda s, h: laps_ref.at[s, pl.ds(half_lap * h, half_lap)]),
            (lambda h: meta_ref.at[pl.ds(half_meta * h, half_meta)],
             lambda s, h: gm_ref.at[s, pl.ds(half_meta * h, half_meta)]),
        ]
        own_in = pltpu.make_async_copy(win_ref.at[pl.ds(0, WOFF)], stage, local_sems.at[0])
        own_lap_in = pltpu.make_async_copy(win_ref.at[pl.ds(WOFF, LAP)], lapbuf.at[0], local_sems.at[1])
        own_out = pltpu.make_async_copy(stage, w_ref.at[pl.ds(WOFF * me_s, WOFF)], local_sems.at[0])
        own_lap_out = pltpu.make_async_copy(lapbuf.at[0], laps_ref.at[me_s], local_sems.at[1])
        sends, arrivals, forwards, forwarded = [], [], [], []
        for a, (src, dst) in enumerate(kinds):
            for k, (cx, cy, cs) in enumerate(chips):
                there = dict(send_sem=send_sems.at[6 * a + k], recv_sem=recv_sems.at[6 * a + k],
                             device_id=(cx, cy, c), device_id_type=MESH)
                across = dict(send_sem=send_sems.at[6 * a + 3 + k], recv_sem=recv_sems.at[6 * a + 3 + k],
                              device_id=sib, device_id_type=MESH)
                sends.append(pltpu.make_async_remote_copy(src_ref=src(c), dst_ref=dst(me_s, c), **there))
                arrivals.append(pltpu.make_async_remote_copy(src_ref=dst(cs, c), dst_ref=dst(cs, c), **there))
                forwards.append(pltpu.make_async_remote_copy(src_ref=dst(cs, c), dst_ref=dst(cs, c), **across))
                forwarded.append(pltpu.make_async_remote_copy(
                    src_ref=dst(cs, 1 - c), dst_ref=dst(cs, 1 - c), **across))

        @pl.when(step == 0)
        def _():
            own_in.start()
            own_lap_in.start()
            for cp in sends:
                cp.start()
            own_in.wait()
            own_out.start()
            own_lap_in.wait()
            own_lap_out.start()

        @pl.when(step < last)
        def _():
            emit(x_ref[...])

        @pl.when(step == last)
        def _():
            for cp, fwd in zip(arrivals, forwards):
                cp.wait_recv()
                fwd.start()
            for cp in forwarded:
                cp.wait_recv()
            for cp in sends + forwards:
                cp.wait_send()
            own_out.wait()
            own_lap_out.wait()
            heads = [w_ref.at[pl.ds(WOFF * s, LAP)] for s in range(1, 4)]
            lap_loads = [pltpu.make_async_copy(laps_ref.at[i], lapbuf.at[1 + i], local_sems.at[4 + 2 * i])
                         for i in range(3)]
            head_loads = [pltpu.make_async_copy(heads[i], headbuf.at[i], local_sems.at[5 + 2 * i]) for i in range(3)]
            head_stores = [pltpu.make_async_copy(headbuf.at[i], heads[i], local_sems.at[5 + 2 * i]) for i in range(3)]
            loads = [pltpu.make_async_copy(meta_ref, metabuf.at[me_s], local_sems.at[0])]
            loads += [pltpu.make_async_copy(gm_ref.at[cs], metabuf.at[cs], local_sems.at[1 + k])
                      for k, (_, _, cs) in enumerate(chips)]
            for cp in lap_loads + head_loads + loads:
                cp.start()
            for i in range(3):
                lap_loads[i].wait()
                head_loads[i].wait()
                headbuf[i] = (headbuf[i].astype(F32) + lapbuf[1 + i].astype(F32)).astype(BF)
                head_stores[i].start()
            for cp in loads:
                cp.wait()
            tokens = jnp.concatenate([metabuf[s] for s in range(4)], axis=1)
            emit(jnp.concatenate([jnp.zeros((PAD, D), F32), tokens], axis=0))
            for cp in head_stores:
                cp.wait()

    def chunk(i):
        return (i + 1) % NCH

    return pl.pallas_call(
        body, name="all_gather_w", grid=(NCH,),
        in_specs=[ANY, ANY, pl.BlockSpec((C, D), lambda i: (jnp.minimum(i, last - 1), 0)),
                  pl.BlockSpec((1, D), lambda i: (0, 0))],
        out_specs=[ANY] * 3 + [pl.BlockSpec((C, D), lambda i: (chunk(i), 0))] * 2
        + [pl.BlockSpec((D, C), lambda i: (0, chunk(i)))],
        out_shape=[jax.ShapeDtypeStruct((WMAIN, D), own_win.dtype), jax.ShapeDtypeStruct((4, LAP, D), own_win.dtype),
                   jax.ShapeDtypeStruct((4,) + meta.shape, meta.dtype),
                   jax.ShapeDtypeStruct((T, D), F32), jax.ShapeDtypeStruct((T, D), BF),
                   jax.ShapeDtypeStruct((D, T), BF)],
        scratch_shapes=[pltpu.SemaphoreType.DMA((18,)), pltpu.SemaphoreType.DMA((18,)), pltpu.SemaphoreType.DMA((10,)),
                        pltpu.VMEM((WOFF, D), own_win.dtype), pltpu.VMEM((4, LAP, D), own_win.dtype),
                        pltpu.VMEM((3, LAP, D), own_win.dtype), pltpu.VMEM((4,) + meta.shape, meta.dtype)],
        compiler_params=_params(("arbitrary",)),
    )(own_win, meta, x, norm_g)


def _pair_copies(ins, outs, send_sems, recv_sems, n):
    x, y, c = _place()
    sib = dict(device_id=(x, y, 1 - c), device_id_type=MESH)
    cps = []
    for a in range(n):
        rows = ins[a].shape[1] // 2
        cps.append(pltpu.make_async_remote_copy(
            src_ref=ins[a].at[:, pl.ds((1 - c) * rows, rows)], dst_ref=outs[a],
            send_sem=send_sems.at[a], recv_sem=recv_sems.at[a], **sib))
    for k in range(4 * (len(ins) - n)):
        cps.append(pltpu.make_async_remote_copy(
            src_ref=ins[n].at[pl.ds(WOFF * k + (1 - c) * WHALF, WHALF)], dst_ref=outs[n].at[k],
            send_sem=send_sems.at[n + k], recv_sem=recv_sems.at[n + k], **sib))
    return cps


def _swap_add_windows(gwt):
    nchunk = 4
    rows = WHALF // nchunk

    def body(gw_ref, land_ref, out_ref, send_sems, recv_sems, local_sems, own, theirs):
        _, _, c = _place()
        swaps = _pair_copies([gw_ref], [land_ref], send_sems, recv_sems, 0)
        loads = [pltpu.make_async_copy(gw_ref.at[pl.ds(WOFF * k + c * WHALF, WHALF)], own.at[k], local_sems.at[k])
                 for k in range(4)]
        stores = [pltpu.make_async_copy(own.at[k], out_ref.at[k], local_sems.at[k]) for k in range(4)]
        for cp in loads + swaps:
            cp.start()
        for k in range(4):
            swaps[k].wait()
            fetch = pltpu.make_async_copy(land_ref.at[k], theirs, local_sems.at[4])
            fetch.start()
            loads[k].wait()
            fetch.wait()

            def add(i, carry, k=k):
                r = _rows(i, rows)
                own[k, r, :] = (own[k, r, :].astype(F32) + theirs[r, :].astype(F32)).astype(BF)
                return carry

            lax.fori_loop(0, nchunk, add, 0)
            stores[k].start()
        for cp in stores:
            cp.wait()

    return pl.pallas_call(
        body, name="rs_pair_swap_add",
        in_specs=[ANY], out_specs=[ANY, ANY],
        out_shape=[jax.ShapeDtypeStruct((4, WHALF, D), gwt.dtype), jax.ShapeDtypeStruct((4, WHALF, D), BF)],
        scratch_shapes=[pltpu.SemaphoreType.DMA((4,)), pltpu.SemaphoreType.DMA((4,)), pltpu.SemaphoreType.DMA((5,)),
                        pltpu.VMEM((4, WHALF, D), gwt.dtype), pltpu.VMEM((WHALF, D), gwt.dtype)],
        compiler_params=pltpu.CompilerParams(vmem_limit_bytes=VMEM_LIMIT),
    )(gwt)


def _pair_send(halves):
    n = len(halves)

    def body(*refs):
        ins, outs = refs[:n], refs[n:2 * n]
        send_sems, recv_sems = refs[2 * n:]
        x, y, c = _place()
        cps = [pltpu.make_async_remote_copy(
            src_ref=ins[a], dst_ref=outs[a], send_sem=send_sems.at[a], recv_sem=recv_sems.at[a],
            device_id=(x, y, 1 - c), device_id_type=MESH) for a in range(n)]
        for cp in cps:
            cp.start()
        for cp in cps:
            cp.wait()

    return pl.pallas_call(
        body, name="rs_pair_send",
        in_specs=[ANY] * n, out_specs=[ANY] * n,
        out_shape=[jax.ShapeDtypeStruct(h.shape, h.dtype) for h in halves],
        scratch_shapes=[pltpu.SemaphoreType.DMA((n,)), pltpu.SemaphoreType.DMA((n,))],
    )(*halves)


def _row_block(rows):
    for tb in (256, 128, 64, 32, 16, 8):
        if rows % tb == 0:
            return tb
    return rows


def _add_halves(full, recv, name, out_dtype):
    _, r2, w = recv.shape
    tb = _row_block(r2)
    nb = r2 // tb
    c = lax.axis_index("c")

    def body(c_ref, a_ref, b_ref, o_ref):
        o_ref[...] = (a_ref[...].astype(F32) + b_ref[...].astype(F32)).astype(o_ref.dtype)

    return pl.pallas_call(
        body, name=name,
        grid_spec=pltpu.PrefetchScalarGridSpec(
            num_scalar_prefetch=1, grid=(4, nb),
            in_specs=[pl.BlockSpec((1, tb, w), lambda s, i, cr: (s, cr[0] * nb + i, 0)),
                      pl.BlockSpec((1, tb, w), lambda s, i, cr: (s, i, 0))],
            out_specs=pl.BlockSpec((1, tb, w), lambda s, i, cr: (s, i, 0))),
        out_shape=jax.ShapeDtypeStruct(recv.shape, out_dtype),
        compiler_params=_params(("parallel", "parallel")),
    )(jnp.reshape(c, (1,)).astype(jnp.int32), full, recv)


def _add2(a, b, name):
    def body(a_ref, b_ref, o_ref):
        o_ref[...] = a_ref[...] + b_ref[...]

    return pl.pallas_call(body, name=name, out_shape=jax.ShapeDtypeStruct(a.shape, a.dtype))(a, b)


def _sum4(buf, own, name, exchange=None):
    _, r, w = buf.shape
    tb = _row_block(r)
    nsteps = r // tb
    me_s = 2 * lax.axis_index("x") + lax.axis_index("y")
    by_dest = own.ndim == 3
    carried = [] if exchange is None else [*exchange[0], exchange[1]]
    m = len(carried)

    def body(s_ref, b_ref, own_ref, *rest):
        o_ref = rest[m]
        if m:
            ins, outs, (send_sems, recv_sems) = rest[:m], rest[m + 1:2 * m + 1], rest[2 * m + 1:]
            cps = _chip_copies(ins[:-1], outs[:-1], send_sems, recv_sems, by_dest=True)
            cps += _chip_copies(ins[-1:], outs[-1:], send_sems.at[pl.ds(3 * (m - 1), 3)],
                                recv_sems.at[pl.ds(3 * (m - 1), 3)], by_dest=False)

            @pl.when(pl.program_id(0) == 0)
            def _():
                for cp in cps:
                    cp.start()

            @pl.when(pl.program_id(0) == nsteps - 1)
            def _():
                for cp in cps:
                    cp.wait()

        mine = (own_ref[0] if by_dest else own_ref[...]).astype(F32)
        terms = [jnp.where(s_ref[0] == t, mine, b_ref[t].astype(F32)) for t in range(4)]
        o_ref[...] = ((terms[0] + terms[1]) + terms[2]) + terms[3]

    own_spec = (pl.BlockSpec((1, tb, w), lambda i, sr: (sr[0], i, 0)) if by_dest
                else pl.BlockSpec((tb, w), lambda i, sr: (i, 0)))
    landing = [jax.ShapeDtypeStruct(p.shape, p.dtype) for p in carried[:-1]]
    landing += [jax.ShapeDtypeStruct((4,) + s.shape, s.dtype) for s in carried[-1:]]
    return pl.pallas_call(
        body, name=name,
        grid_spec=pltpu.PrefetchScalarGridSpec(
            num_scalar_prefetch=1, grid=(nsteps,),
            in_specs=[pl.BlockSpec((4, tb, w), lambda i, sr: (0, i, 0)), own_spec] + [ANY] * m,
            out_specs=[pl.BlockSpec((tb, w), lambda i, sr: (i, 0))] + [ANY] * m,
            scratch_shapes=[pltpu.SemaphoreType.DMA((3 * m,)), pltpu.SemaphoreType.DMA((3 * m,))] if m else []),
        out_shape=[jax.ShapeDtypeStruct((r, w), F32)] + landing,
        compiler_params=_params(("arbitrary" if m else "parallel",)),
    )(jnp.reshape(me_s, (1,)).astype(jnp.int32), buf, own, *carried)


def _adamw_math(w, g, m, v):
    mn = B1 * m + (1.0 - B1) * g
    vn = B2 * v + (1.0 - B2) * (g * g)
    m_hat = mn / (1.0 - B1 ** STEP)
    v_hat = vn / (1.0 - B2 ** STEP)
    return -LR * (m_hat / (jnp.sqrt(v_hat) + AEPS) + WD * w), mn, vn


def _adamw(w, g, m, v, name):
    r, c_ = w.shape
    tb = _row_block(r)
    if tb == r and r > 512:
        tb = 256

    def body(w_ref, g_ref, m_ref, v_ref, d_ref, mo_ref, vo_ref):
        d_ref[...], mo_ref[...], vo_ref[...] = _adamw_math(w_ref[...], g_ref[...], m_ref[...], v_ref[...])

    spec = pl.BlockSpec((tb, c_), lambda i: (i, 0))
    return pl.pallas_call(
        body, name=name, grid=(pl.cdiv(r, tb),),
        in_specs=[spec] * 4, out_specs=[spec] * 3,
        out_shape=[jax.ShapeDtypeStruct(w.shape, F32)] * 3,
        compiler_params=_params(("parallel",)),
    )(w, g, m, v)


def _adamw_rows(w, g_mine, g_sib, m, v, name):
    r = w.shape[0]
    tb = 256
    sub, lanes = w.shape[1:]
    nh = g_mine.shape[0] // tb
    nsteps = pl.cdiv(r, tb)
    assert nsteps <= 2 * nh and 4 * 3 + r <= 2 * nh * tb
    x, y, c = _place()
    place = jnp.stack([c, 4 * (2 * x + y)]).astype(jnp.int32)

    def body(p_ref, w_ref, mc_ref, sc_ref, mn_ref, sn_ref, m_ref, v_ref, go_ref, d_ref, mo_ref, vo_ref, buf):
        i = pl.program_id(0)
        for at, blk, mine_ref, sib_ref in ((0, i, mc_ref, sc_ref), (1, jnp.minimum(i + 1, 2 * nh - 1), mn_ref, sn_ref)):
            rows = jnp.where(blk // nh == p_ref[0], mine_ref[...], sib_ref[...])
            buf[tb * at:tb * (at + 1)] = rows.reshape(tb, sub, lanes)
        g = buf[pl.ds(p_ref[1], tb)]
        go_ref[...] = g
        d_ref[...], mo_ref[...], vo_ref[...] = _adamw_math(w_ref[...], g, m_ref[...], v_ref[...])

    def half_spec(ahead, sibling):
        def index(i, pr):
            half = (1 - pr[0]) if sibling else pr[0]
            return (jnp.clip(jnp.minimum(i + ahead, 2 * nh - 1) - nh * half, 0, nh - 1), 0)
        return pl.BlockSpec((tb, sub * lanes), index)

    spec = pl.BlockSpec((tb, sub, lanes), lambda i, pr: (i, 0, 0))
    return pl.pallas_call(
        body, name=name,
        grid_spec=pltpu.PrefetchScalarGridSpec(
            num_scalar_prefetch=1, grid=(nsteps,),
            in_specs=[spec, half_spec(0, False), half_spec(0, True), half_spec(1, False), half_spec(1, True),
                      spec, spec],
            out_specs=[spec] * 4,
            scratch_shapes=[pltpu.VMEM((2 * tb, sub, lanes), F32)]),
        out_shape=[jax.ShapeDtypeStruct(w.shape, F32)] * 4,
        compiler_params=_params(("parallel",)),
    )(place, w, g_mine, g_sib, g_mine, g_sib, m, v)


def _adamw_halves(w, g_mine, g_sib, m, v, name):
    r, c_ = w.shape
    r2 = g_mine.shape[0]
    tb = _row_block(r2)
    nb = r2 // tb
    c = lax.axis_index("c")

    def body(c_ref, w_ref, gm_ref, gs_ref, m_ref, v_ref, g_ref, d_ref, mo_ref, vo_ref):
        g = jnp.where(pl.program_id(0) == c_ref[0], gm_ref[...], gs_ref[...])
        g_ref[...] = g
        d_ref[...], mo_ref[...], vo_ref[...] = _adamw_math(w_ref[...], g, m_ref[...], v_ref[...])

    full = pl.BlockSpec((tb, c_), lambda h, i, cr: (h * nb + i, 0))
    half = pl.BlockSpec((tb, c_), lambda h, i, cr: (i, 0))
    return pl.pallas_call(
        body, name=name,
        grid_spec=pltpu.PrefetchScalarGridSpec(
            num_scalar_prefetch=1, grid=(2, nb),
            in_specs=[full, half, half, full, full], out_specs=[full] * 4),
        out_shape=[jax.ShapeDtypeStruct(w.shape, F32)] * 4,
        compiler_params=_params(("parallel", "parallel")),
    )(jnp.reshape(c, (1,)).astype(jnp.int32), w, g_mine, g_sib, m, v)


def kernel(x, meta_tokens, norm_g, w_in, b_f, w_out, final_g, loss_target, m_meta_tokens, m_norm_g, m_w_in, m_b_f, m_w_out, m_final_g, v_meta_tokens, v_norm_g, v_w_in, v_b_f, v_w_out, v_final_g):
    w3, m3, v3 = [jnp.transpose(jnp.reshape(t[0], (D // C, C, WSH)), (2, 0, 1)) for t in (w_in, m_w_in, v_w_in)]

    wt_main, laps, _, *normed = _gather_weights(_own_window(w3), meta_tokens, x[0], norm_g)
    wft = jnp.pad(laps[3, :NFF], ((0, C - NFF), (0, 0)))
    wout_own = w_out[0].astype(BF)

    loss, gx, dmeta, dng, gwt, dbf, dwout, dfg, (p_in, p_out), (e_in, e_out) = _local_step(
        x[0], loss_target[0], normed, norm_g, wt_main, wft, b_f, wout_own, final_g.reshape(1, D), True, True)

    g_meta = jnp.stack([dmeta[:, 256 * s:256 * (s + 1)] for s in range(4)])
    small = jnp.concatenate([dng, dfg, jnp.pad(dbf, ((0, 0), (0, D - NFF))),
                             jnp.pad(jnp.reshape(loss, (1, 1)), ((0, 0), (0, D - 1))),
                             jnp.zeros((4, D), F32)], axis=0)
    h_in, e_meta, e_small = _sum4(e_in, p_in, "sum_in", exchange=([g_meta], small))
    (h_out,), (h_meta,), (h_small,) = (_sum4(e_out, p_out, "sum_out"), _sum4(e_meta, g_meta, "sum_meta"),
                                       _sum4(e_small, small, "sum_small"))
    s_in, s_out, s_meta, s_small = _pair_send([h_in, h_out, h_meta, h_small])
    gw_meta = _add2(h_meta, s_meta, "pair_add_meta")
    tot = _add2(h_small, s_small, "pair_add_small")
    g_norm, g_final, g_bf, loss_all = tot[0:1], tot[1], tot[2:3, :NFF], tot[3, 0]

    d_meta, nm_meta, nv_meta = _adamw(meta_tokens, gw_meta, m_meta_tokens, v_meta_tokens, "adamw_meta")
    d_norm, nm_norm, nv_norm = _adamw(norm_g, g_norm, m_norm_g, v_norm_g, "adamw_norm")
    outs_in = _adamw_rows(w3, h_in, s_in, m3, v3, "adamw_in")
    gw_in, d_in, nm_in, nv_in = [jnp.reshape(jnp.transpose(t, (1, 2, 0)), (1, D, WSH)) for t in outs_in]
    d_bf, nm_bf, nv_bf = _adamw(b_f, g_bf, m_b_f, v_b_f, "adamw_bf")
    gw_out, d_out, nm_out, nv_out = _adamw_halves(w_out[0], h_out, s_out, m_w_out[0], v_w_out[0], "adamw_out")
    d_fin, nm_fin, nv_fin = _adamw(final_g.reshape(1, D), g_final.reshape(1, D), m_final_g.reshape(1, D),
                                   v_final_g.reshape(1, D), "adamw_final")
    return (loss_all, gx[None], gw_meta, g_norm, gw_in, g_bf, gw_out[None], g_final,
            d_meta, d_norm, d_in, d_bf, d_out[None], d_fin.reshape(D),
            nm_meta, nm_norm, nm_in, nm_bf, nm_out[None], nm_fin.reshape(D),
            nv_meta, nv_norm, nv_in, nv_bf, nv_out[None], nv_fin.reshape(D))
```

```python
import numpy as np
import jax
import jax.numpy as jnp
from jax import lax
from jax.experimental import pallas as pl
from jax.experimental.pallas import tpu as pltpu

D = 1024
SEQ = 2048
NMETA = 16
C = 128
PAD = C - NMETA
T = PAD + NMETA + SEQ
NCH = T // C
RH, RDK, RDV = 4, 128, 256
FH, FD = 16, 64
NPAIR = FH // 2
WMAIN = 7168
NFF = 16
WIN = WMAIN + NFF
WSH = WIN // 4
WPADROWS = 1824
DMIX = 2048
EPS = 1e-6
NEG = -1e30
RSCALE = RDK ** -0.5
FSCALE = FD ** -0.5
ROPE_BASE = 10000.0
LR, B1, B2, AEPS, WD, STEP = 0.001, 0.9, 0.999, 1e-08, 0.01, 10

BF = jnp.bfloat16
F32 = jnp.float32
NT = (((1,), (1,)), ((), ()))
TN = (((0,), (0,)), ((), ()))
NN_DIMS = (((1,), (0,)), ((), ()))
MESH = pl.DeviceIdType.MESH
ANY = pl.BlockSpec(memory_space=pl.ANY)
VMEM_LIMIT = 48 * 1024 * 1024
DW_VMEM_LIMIT = 56 * 1024 * 1024

GB_R, GB_F = 2, 6
QB_F, KB_F, VB_F = 24, 32, 40


def _dot(a, b):
    return jnp.dot(a, b, preferred_element_type=F32)


def _dg(a, b, dims):
    return lax.dot_general(a, b, dims, preferred_element_type=F32)


def _params(sem=None):
    return pltpu.CompilerParams(dimension_semantics=sem, vmem_limit_bytes=VMEM_LIMIT)


def _constants():
    pos = jnp.arange(T, dtype=F32) - PAD
    inv = ROPE_BASE ** (-jnp.arange(0, RDK, 2, dtype=F32) / RDK)
    ang = pos[:, None] * inv[None, :]
    cos, sin = jnp.cos(ang), jnp.sin(ang)
    cos2 = jnp.concatenate([cos, cos], axis=1)
    sin2 = jnp.concatenate([-sin, sin], axis=1)
    log_gamma = jnp.log1p(-jnp.exp2(-5.0 - jnp.arange(RH, dtype=F32)))
    idx = jnp.arange(C, dtype=F32)
    diff = idx[:, None] - idx[None, :]
    dmask = jnp.where(diff[None] >= 0, jnp.exp(log_gamma[:, None, None] * jnp.maximum(diff, 0.0)[None]), 0.0)
    zeta = jnp.exp(log_gamma[:, None] * (C - 1.0 - idx)[None, :])
    xi = jnp.exp(log_gamma[:, None] * (idx + 1.0)[None, :])
    gdec = jnp.exp(log_gamma * C)
    zeta_b = jnp.broadcast_to(zeta[:, :, None], (RH, C, RDK))
    xi_b = jnp.broadcast_to(xi[:, :, None], (RH, C, RDK))
    gdec_b = jnp.broadcast_to(gdec[:, None, None], (RH, RDK, RDV))
    tri = jnp.asarray(np.tril(np.ones((C, C), np.float32)), dtype=BF)
    head_of_lane = np.arange(FH * FD) // FD
    pick = ((np.arange(FH * FD)[:, None] % FD == 0)
            & (head_of_lane[:, None] == np.arange(C)[None, :])).astype(np.float32)
    seg = (np.arange(C)[:, None] // FD == np.arange(C)[None, :] // FD).astype(np.float32)
    ones_aug = np.concatenate([np.tile((np.arange(C) < FD)[None, :], (C, 1)),
                               np.tile((np.arange(C) >= FD)[None, :], (C, 1))], axis=0).astype(np.float32)
    lane = np.arange(2 * C) % C
    causal = np.where(lane[None, :] <= np.arange(C)[:, None], 0.0, NEG).astype(np.float32)
    mask_bias = np.stack([np.zeros((C, 2 * C), np.float32), causal])
    return dict(cos2=cos2, sin2=sin2, dmask=dmask, zeta=zeta_b, xi=xi_b, gdec=gdec_b, tri=tri,
                mask_bias=jnp.asarray(mask_bias), pick=jnp.asarray(pick, dtype=BF), seg=jnp.asarray(seg, dtype=BF),
                ones_aug=jnp.asarray(ones_aug, dtype=BF))


def _norm_rows(h, g):
    return h * lax.rsqrt(jnp.mean(h * h, axis=1, keepdims=True) + EPS) * g


def _mm_nt(a, b, n, tm, tn, name):
    m, k = a.shape

    def body(a_ref, b_ref, o_ref):
        o_ref[...] = _dg(a_ref[...], b_ref[...], NT)

    return pl.pallas_call(
        body, name=name, grid=(m // tm, n // tn),
        in_specs=[pl.BlockSpec((tm, k), lambda i, j: (i, 0)), pl.BlockSpec((tn, k), lambda i, j: (j, 0))],
        out_specs=pl.BlockSpec((tm, tn), lambda i, j: (i, j)),
        out_shape=jax.ShapeDtypeStruct((m, n), F32),
        compiler_params=_params(("parallel", "parallel")),
    )(a, b)


def _mm_nn(a, b, tm, tn, name, out_dtype=F32):
    m, k = a.shape
    _, n = b.shape

    def body(a_ref, b_ref, o_ref):
        o_ref[...] = _dot(a_ref[...], b_ref[...]).astype(out_dtype)

    return pl.pallas_call(
        body, name=name, grid=(m // tm, n // tn),
        in_specs=[pl.BlockSpec((tm, k), lambda i, j: (i, 0)), pl.BlockSpec((k, tn), lambda i, j: (0, j))],
        out_specs=pl.BlockSpec((tm, tn), lambda i, j: (i, j)),
        out_shape=jax.ShapeDtypeStruct((m, n), out_dtype),
        compiler_params=_params(("parallel", "parallel")),
    )(a, b)


def _rot(x, cos2, sin2):
    return x * cos2 + pltpu.roll(x, 64, 1) * sin2


def _ret_specs(chunk):
    whole = lambda shape: pl.BlockSpec(shape, lambda n: (0,) * len(shape))
    return [
        pl.BlockSpec((C, RH * RDK), lambda n: (chunk(n), 0)),
        pl.BlockSpec((C, RH * RDK), lambda n: (chunk(n), 1)),
        pl.BlockSpec((C, RH * RDV), lambda n: (chunk(n), 1)),
        pl.BlockSpec((C, RDK), lambda n: (chunk(n), 0)),
        pl.BlockSpec((C, RDK), lambda n: (chunk(n), 0)),
        whole((RH, C, C)), whole((RH, C, RDK)), whole((RH, C, RDK)), whole((RH, RDK, RDV)),
    ]


def _ret_heads(q_ref, k_ref, v_ref, cos, sin):
    qr = [_rot(q_ref[:, RDK * h:RDK * (h + 1)], cos, sin) for h in range(RH)]
    kr = [_rot(k_ref[:, RDK * h:RDK * (h + 1)], cos, sin) * RSCALE for h in range(RH)]
    vb = [v_ref[:, RDV * h:RDV * (h + 1)].astype(BF) for h in range(RH)]
    return qr, kr, [t.astype(BF) for t in qr], [t.astype(BF) for t in kr], vb


def _ret_fwd(z, cst):
    def body(q_ref, k_ref, v_ref, cos_ref, sin_ref, dm_ref, xi_ref, zt_ref, gd_ref, r_ref, sp_ref, st):
        n = pl.program_id(0)

        @pl.when(n == 0)
        def _():
            st[...] = jnp.zeros_like(st)

        hs = range(RH)
        qr, kr, qb, kb, vb = _ret_heads(q_ref, k_ref, v_ref, cos_ref[...], sin_ref[...])
        sd = [(_dg(qb[h], kb[h], NT) * dm_ref[h]).astype(BF) for h in hs]
        state = [st[h] for h in hs]
        qx = [(qr[h] * xi_ref[h]).astype(BF) for h in hs]
        kz = [(kr[h] * zt_ref[h]).astype(BF) for h in hs]
        out = [_dot(sd[h], vb[h]) + _dot(qx[h], state[h].astype(BF)) for h in hs]
        kv = [_dg(kz[h], vb[h], TN) for h in hs]
        for h in hs:
            sp_ref[0, h] = state[h]
            r_ref[:, RDV * h:RDV * (h + 1)] = out[h]
            st[h] = state[h] * gd_ref[h] + kv[h]

    return pl.pallas_call(
        body, name="ret_fwd", grid=(NCH,),
        in_specs=_ret_specs(lambda n: n),
        out_specs=[pl.BlockSpec((C, RH * RDV), lambda n: (n, 0)),
                   pl.BlockSpec((1, RH, RDK, RDV), lambda n: (n, 0, 0, 0))],
        out_shape=[jax.ShapeDtypeStruct((T, RH * RDV), F32), jax.ShapeDtypeStruct((NCH, RH, RDK, RDV), F32)],
        scratch_shapes=[pltpu.VMEM((RH, RDK, RDV), F32)],
        compiler_params=_params(("arbitrary",)),
    )(z, z, z, cst["cos2"], cst["sin2"], cst["dmask"], cst["xi"], cst["zeta"], cst["gdec"])


def _ret_bwd(z, cst, sprev, dr):
    def body(q_ref, k_ref, v_ref, cos_ref, sin_ref, dm_ref, xi_ref, zt_ref, gd_ref, sp_ref, dr_ref,
             dq_ref, dk_ref, dv_ref, gst):
        i = pl.program_id(0)

        @pl.when(i == 0)
        def _():
            gst[...] = jnp.zeros_like(gst)

        hs = range(RH)
        cos, sin = cos_ref[...], sin_ref[...]
        qr, kr, qb, kb, vb = _ret_heads(q_ref, k_ref, v_ref, cos, sin)
        dm = [dm_ref[h] for h in hs]
        xi = [xi_ref[h] for h in hs]
        zt = [zt_ref[h] for h in hs]
        sd = [(_dg(qb[h], kb[h], NT) * dm[h]).astype(BF) for h in hs]
        qx = [(qr[h] * xi[h]).astype(BF) for h in hs]
        kz = [(kr[h] * zt[h]).astype(BF) for h in hs]
        drb = [dr_ref[:, RDV * h:RDV * (h + 1)] for h in hs]
        sb = [sp_ref[0, h].astype(BF) for h in hs]
        g = [gst[h] for h in hs]
        gb = [t.astype(BF) for t in g]
        ds = [(_dg(drb[h], vb[h], NT) * dm[h]).astype(BF) for h in hs]
        dq = [_dot(ds[h], kb[h]) + _dg(drb[h], sb[h], NT) * xi[h] for h in hs]
        dk = [(_dg(ds[h], qb[h], TN) + _dg(vb[h], gb[h], NT) * zt[h]) * RSCALE for h in hs]
        dv = [_dg(sd[h], drb[h], TN) + _dot(kz[h], gb[h]) for h in hs]
        gn = [g[h] * gd_ref[h] + _dg(qx[h], drb[h], TN) for h in hs]
        for h in hs:
            gst[h] = gn[h]
            dq_ref[:, RDK * h:RDK * (h + 1)] = (dq[h] * cos + pltpu.roll(dq[h] * sin, 64, 1)).astype(BF)
            dk_ref[:, RDK * h:RDK * (h + 1)] = (dk[h] * cos + pltpu.roll(dk[h] * sin, 64, 1)).astype(BF)
            dv_ref[:, RDV * h:RDV * (h + 1)] = dv[h].astype(BF)

    rev = lambda n: NCH - 1 - n
    return pl.pallas_call(
        body, name="ret_bwd", grid=(NCH,),
        in_specs=_ret_specs(rev) + [
            pl.BlockSpec((1, RH, RDK, RDV), lambda n: (rev(n), 0, 0, 0)),
            pl.BlockSpec((C, RH * RDV), lambda n: (rev(n), 0)),
        ],
        out_specs=[pl.BlockSpec((C, RH * RDK), lambda n: (rev(n), 0)),
                   pl.BlockSpec((C, RH * RDK), lambda n: (rev(n), 0)),
                   pl.BlockSpec((C, RH * RDV), lambda n: (rev(n), 0))],
        out_shape=[jax.ShapeDtypeStruct((T, RH * RDK), BF), jax.ShapeDtypeStruct((T, RH * RDK), BF),
                   jax.ShapeDtypeStruct((T, RH * RDV), BF)],
        scratch_shapes=[pltpu.VMEM((RH, RDK, RDV), F32)],
        compiler_params=_params(("arbitrary",)),
    )(z, z, z, cst["cos2"], cst["sin2"], cst["dmask"], cst["xi"], cst["zeta"], cst["gdec"], sprev, dr)


def _place():
    x, y, c = lax.axis_index("x"), lax.axis_index("y"), lax.axis_index("c")
    return x, y, c


def _other_chips(x, y):
    return [(1 - x, y, 2 * (1 - x) + y), (x, 1 - y, 2 * x + (1 - y)), (1 - x, 1 - y, 2 * (1 - x) + (1 - y))]


def _chip_copies(srcs, lands, send_sems, recv_sems, by_dest):
    x, y, c = _place()
    me_s = 2 * x + y
    return [pltpu.make_async_remote_copy(
        src_ref=src.at[cs] if by_dest else src, dst_ref=land.at[me_s],
        send_sem=send_sems.at[3 * a + j], recv_sem=recv_sems.at[3 * a + j],
        device_id=(cx, cy, c), device_id_type=MESH)
        for a, (src, land) in enumerate(zip(srcs, lands)) for j, (cx, cy, cs) in enumerate(_other_chips(x, y))]


def _split_dot(x, mat01, dims=NN_DIMS, x_first=True):
    acc, rest = None, x
    for _ in range(3):
        piece = rest.astype(BF)
        part = _dg(piece, mat01, dims) if x_first else _dg(mat01, piece, dims)
        acc = part if acc is None else acc + part
        rest = rest - piece.astype(F32)
    return acc


def _log_sigmoid(x):
    return -(jnp.maximum(-x, 0.0) + jnp.log1p(jnp.exp(-jnp.abs(x))))


def _fox_prep(zf, bf_pad, cst):
    def body(zf_ref, b_ref, tri_ref, ct_ref):
        def chunk(n, carry):
            ls = _log_sigmoid(zf_ref[_rows(n), :] + b_ref[...])
            row = n * C + lax.broadcasted_iota(jnp.int32, (C, C), 0)
            lf = jnp.where(row >= PAD, ls, 0.0)
            cc = _split_dot(lf, tri_ref[...], x_first=False) + carry
            pos = n * C + lax.broadcasted_iota(jnp.int32, (FH, C), 1)
            ct_ref[n] = jnp.where(pos >= PAD, cc.T[:FH, :], -NEG)
            return cc[C - 1:C, :]

        lax.fori_loop(0, NCH, chunk, jnp.zeros((1, C), F32))

    return pl.pallas_call(
        body, name="fox_prep",
        out_shape=jax.ShapeDtypeStruct((NCH, FH, C), F32),
    )(zf, bf_pad, cst["tri"])


def _lo_lanes(shape):
    return lax.broadcasted_iota(jnp.int32, shape, 1) < FD


def _split_heads(x):
    lo = _lo_lanes(x.shape)
    zero = jnp.zeros_like(x)
    return jnp.concatenate([jnp.where(lo, x, zero), jnp.where(lo, zero, x)], axis=0)


def _spread2(x):
    lo = _lo_lanes(x.shape)
    r = pltpu.roll(x, FD, 1)
    return jnp.concatenate([jnp.where(lo, x, r), jnp.where(lo, r, x)], axis=1)


NSTEP = (NCH + 1) // 2
NTILE = NCH + 1
TROWS = T + C


def _fox_tile(s, t):
    second = t > s
    return second.astype(jnp.int32), jnp.where(second, t - s - 1, s - t)


def _fox_pos(i):
    return jnp.where(i < NSTEP, 2 * i, 2 * (NCH - 1 - i) + 1)


def _fox_pair_columns():
    return pl.BlockSpec((TROWS, C), lambda p, s: (0, p))


def _fox_key_bias(ct_ref, p, j):
    return jnp.concatenate([ct_ref[j, pl.ds(2 * p, 1), :], ct_ref[j, pl.ds(2 * p + 1, 1), :]], axis=1)


def _fox_columns(cols, sems, p):
    def copies(pair, slot):
        return [pltpu.make_async_copy(
            src.at[pl.ds(0, buf.shape[1]), pl.ds(pl.multiple_of((first + pair) * C, C), C)], buf.at[slot],
            sems.at[i, slot]) for i, (src, first, buf) in enumerate(cols)]

    @pl.when(p == 0)
    def _():
        for cp in copies(0, 0):
            cp.start()

    for cp in copies(p, p % 2):
        cp.wait()

    @pl.when(p + 1 < NPAIR)
    def _():
        for cp in copies(p + 1, 1 - p % 2):
            cp.start()


def _rows(block, size=C):
    return pl.ds(pl.multiple_of(block * size, size), size)


def _fox_fwd(z, ct, cst, share):
    n = 0 if share is None else 1

    def body(z_ref, ct_ref, ones_ref, mb_ref, *rest):
        share_refs, (a_ref, g_ref), land_refs = rest[:n], rest[n:n + 2], rest[n + 2:2 * n + 2]
        kks, vvs, q2, m2, sbuf, qbuf, kbuf, vbuf, col_sems = rest[2 * n + 2:2 * n + 11]
        p, s = pl.program_id(0), pl.program_id(1)
        slot = p % 2
        if n:
            send_sems, recv_sems, own_sem = rest[2 * n + 11:]
            x, y, _ = _place()
            copies = _chip_copies(share_refs, land_refs, send_sems, recv_sems, by_dest=False)
            copies.append(pltpu.make_async_copy(share_refs[0], land_refs[0].at[2 * x + y], own_sem.at[0]))

            @pl.when((p == 0) & (s == 0))
            def _():
                for cp in copies:
                    cp.start()

            @pl.when((p == NPAIR - 1) & (s == NSTEP - 1))
            def _():
                for cp in copies:
                    cp.wait()

        @pl.when(s == 0)
        def _():
            ones = ones_ref[...]
            _fox_columns([(z_ref, QB_F, qbuf), (z_ref, KB_F, kbuf), (z_ref, VB_F, vbuf)], col_sems, p)

            def prep(j, carry):
                kks[j] = _split_heads(kbuf[slot, _rows(j), :]).astype(BF)
                vvs[j] = jnp.concatenate([_split_heads(vbuf[slot, _rows(j), :]).astype(BF), ones], axis=1)
                return carry

            lax.fori_loop(0, NCH, prep, 0)

        q2[0] = (qbuf[slot, _rows(s), :] * FSCALE).astype(BF)
        q2[1] = (qbuf[slot, _rows(NCH - 1 - s), :] * FSCALE).astype(BF)

        tiles = [_fox_tile(s, t) for t in range(NTILE)]
        causal = mb_ref[1]
        neg = jnp.full((C, 2 * C), NEG, F32)
        run, first = neg, neg
        for t, (sel, j) in enumerate(tiles):
            st = _dg(q2[sel], kks[j], NT) - _fox_key_bias(ct_ref, p, j)
            if t in (0, NTILE - 1):
                st = st + causal
            sbuf[t] = st
            run = jnp.maximum(jnp.where(t == s + 1, neg, run), st)
            first = jnp.where(t == s, run, first)
        for w, mx in enumerate((first, run)):
            m2[w] = jnp.concatenate(
                [jnp.broadcast_to(jnp.max(mx[:, :C], axis=1, keepdims=True), (C, C)),
                 jnp.broadcast_to(jnp.max(mx[:, C:], axis=1, keepdims=True), (C, C))], axis=1)

        zero = jnp.zeros((C, 2 * C), F32)
        run, first = zero, zero
        for t, (sel, j) in enumerate(tiles):
            run = jnp.where(t == s + 1, zero, run) + _dot(jnp.exp(sbuf[t] - m2[sel]).astype(BF), vvs[j])
            first = jnp.where(t == s, run, first)
        lo = _lo_lanes((C, C))
        for w, res in enumerate((first, run)):
            l = res[:, C:]
            a_ref[_rows(2 * s + w), :] = res[:, :C] / l
            mw = m2[w]
            g_ref[_rows(2 * s + w), :] = -(jnp.where(lo, mw[:, :C], mw[:, C:]) + jnp.log(l))

    col = _fox_pair_columns()
    return pl.pallas_call(
        body, name="fox_fwd", grid=(NPAIR, NSTEP),
        in_specs=[ANY,
                  pl.BlockSpec((NCH, FH, C), lambda p, s: (0, 0, 0)),
                  pl.BlockSpec((2 * C, C), lambda p, s: (0, 0)),
                  pl.BlockSpec((2, C, 2 * C), lambda p, s: (0, 0, 0))] + [ANY] * n,
        out_specs=[col, col] + [ANY] * n,
        out_shape=[jax.ShapeDtypeStruct((TROWS, FH * FD), F32)] * 2
        + ([jax.ShapeDtypeStruct((4,) + share.shape, share.dtype)] if n else []),
        scratch_shapes=[pltpu.VMEM((NCH, 2 * C, C), BF), pltpu.VMEM((NCH, 2 * C, 2 * C), BF),
                        pltpu.VMEM((2, C, C), BF), pltpu.VMEM((2, C, 2 * C), F32),
                        pltpu.VMEM((NTILE, C, 2 * C), F32),
                        pltpu.VMEM((2, T, C), F32), pltpu.VMEM((2, T, C), F32), pltpu.VMEM((2, T, C), F32),
                        pltpu.SemaphoreType.DMA((3, 2))]
        + [pltpu.SemaphoreType.DMA((3,)), pltpu.SemaphoreType.DMA((3,)), pltpu.SemaphoreType.DMA((1,))] * n,
        compiler_params=_params(("arbitrary", "arbitrary")),
    )(z, ct, cst["ones_aug"], cst["mask_bias"], *([share] * n))


def _fox_bwd(z, da, g, delta, ct, cst, parts=()):
    grp = 9

    n = len(parts)

    def body(z_ref, da_ref, g_ref, dl_ref, ct_ref, ones_ref, mb_ref, *rest):
        part_refs, (dq_ref, dr_ref, dk_ref, dv_ref, dcs_ref), land_refs = rest[:n], rest[n:n + 5], rest[n + 5:2 * n + 5]
        (kks, vvs, q2, qq2, dd2, da2, gi2, dl2, dq2, dkacc, dvacc, csacc, qbuf, kbuf, vbuf, dabuf, gbuf,
         dlbuf, col_sems) = rest[2 * n + 5:2 * n + 24]
        p, s = pl.program_id(0), pl.program_id(1)
        slot = p % 2
        ones = ones_ref[...]
        if n:
            copies = _chip_copies(part_refs, land_refs, *rest[2 * n + 24:], by_dest=True)

            @pl.when((p == 0) & (s == 0))
            def _():
                for cp in copies:
                    cp.start()

            @pl.when((p == NPAIR - 1) & (s == NSTEP - 1))
            def _():
                for cp in copies:
                    cp.wait()

        @pl.when(s == 0)
        def _():
            dkacc[...] = jnp.zeros_like(dkacc)
            dvacc[...] = jnp.zeros_like(dvacc)
            csacc[...] = jnp.zeros_like(csacc)
            _fox_columns([(z_ref, QB_F, qbuf), (z_ref, KB_F, kbuf), (z_ref, VB_F, vbuf), (da_ref, 0, dabuf),
                          (g_ref, 0, gbuf), (dl_ref, 0, dlbuf)], col_sems, p)

            def prep(j, carry):
                kks[j] = _split_heads(kbuf[slot, _rows(j), :]).astype(BF)
                vvs[j] = _split_heads(vbuf[slot, _rows(j), :]).astype(BF)
                return carry

            lax.fori_loop(0, NCH, prep, 0)

        for w, (chunk, blk) in enumerate(((s, 2 * s), (NCH - 1 - s, jnp.where(s == NSTEP - 1, 2 * s, 2 * s + 1)))):
            qf = qbuf[slot, _rows(chunk), :]
            q2[w] = (qf * FSCALE).astype(BF)
            qq2[w] = jnp.concatenate([_split_heads(qf).astype(BF), ones], axis=1)
            da2[w] = dabuf[slot, _rows(blk), :]
            dd2[w] = _split_heads(da2[w].astype(F32)).astype(BF)
            gi2[w] = _spread2(gbuf[slot, _rows(blk), :])
            dl2[w] = _spread2(dlbuf[slot, _rows(blk), :])
        dq2[...] = jnp.zeros_like(dq2)
        zero = jnp.zeros((C, 2 * C), F32)

        def group(gi, carry):
            ts = [gi * grp + u for u in range(grp)]
            tiles = [_fox_tile(s, t) for t in ts]
            kk = [kks[j] for _, j in tiles]
            ss = [_dg(q2[sel], kj, NT) + (gi2[sel] - _fox_key_bias(ct_ref, p, j)) for kj, (sel, j) in zip(kk, tiles)]
            ss[0] = ss[0] + mb_ref[(gi == 0).astype(jnp.int32)]
            ss[-1] = ss[-1] + mb_ref[(gi == 1).astype(jnp.int32)]
            dps = [_dg(da2[sel], vvs[j], NT) for sel, j in tiles]
            pes = [jnp.exp(st) for st in ss]
            dss = [pe * (dp - dl2[sel]) * FSCALE for pe, dp, (sel, _) in zip(pes, dps, tiles)]
            pts = [jnp.concatenate([pe[:, :C].T, pe[:, C:].T], axis=1).astype(BF) for pe in pes]
            dsts = [jnp.concatenate([ds[:, :C].T, ds[:, C:].T], axis=1).astype(BF) for ds in dss]
            dvs = [_dot(pt, dd2[sel]) for pt, (sel, _) in zip(pts, tiles)]
            rs = [_dot(dst, qq2[sel]) for dst, (sel, _) in zip(dsts, tiles)]
            parts = [_dot(ds.astype(BF), jnp.concatenate([kj, ones], axis=1)) for ds, kj in zip(dss, kk)]
            for (_, j), dv, rr in zip(tiles, dvs, rs):
                dvacc[_rows(j), :] += dv
                dkacc[_rows(j), :] += rr[:, :C]
                csacc[_rows(j), :] += rr[:, C:]
            pa, pb = zero, zero
            for t, part in zip(ts, parts):
                pa = pa + jnp.where(t <= s, part, zero)
                pb = pb + jnp.where(t <= s, zero, part)
            dq2[0] += pa
            dq2[1] += pb
            return carry

        ntile = jnp.where(s == NSTEP - 1, grp, NTILE)
        lax.fori_loop(0, ntile // grp, group, 0)
        for w, chunk in ((1, NCH - 1 - s), (0, s)):
            res = dq2[w]
            dq_ref[_rows(chunk), :] = res[:, :C].astype(BF)
            dr_ref[_rows(2 * s + w), :] = res[:, C:]

        @pl.when(s == NSTEP - 1)
        def _():
            dk_ref[...] = dkacc[...].astype(BF)
            dv_ref[...] = dvacc[...].astype(BF)
            dcs_ref[...] = csacc[...]

    both = _fox_pair_columns()
    col = pl.BlockSpec((T, C), lambda p, s: (0, p))
    return pl.pallas_call(
        body, name="fox_bwd", grid=(NPAIR, NSTEP),
        in_specs=[ANY] * 4
        + [pl.BlockSpec((NCH, FH, C), lambda p, s: (0, 0, 0)),
           pl.BlockSpec((2 * C, C), lambda p, s: (0, 0)),
           pl.BlockSpec((2, C, 2 * C), lambda p, s: (0, 0, 0))] + [ANY] * n,
        out_specs=[col, both, col, col, col] + [ANY] * n,
        out_shape=[jax.ShapeDtypeStruct((T, FH * FD), BF), jax.ShapeDtypeStruct((TROWS, FH * FD), F32),
                   jax.ShapeDtypeStruct((T, FH * FD), BF), jax.ShapeDtypeStruct((T, FH * FD), BF),
                   jax.ShapeDtypeStruct((T, FH * FD), F32)]
        + [jax.ShapeDtypeStruct(p.shape, p.dtype) for p in parts],
        scratch_shapes=[pltpu.VMEM((NCH, 2 * C, C), BF), pltpu.VMEM((NCH, 2 * C, C), BF),
                        pltpu.VMEM((2, C, C), BF), pltpu.VMEM((2, 2 * C, 2 * C), BF), pltpu.VMEM((2, 2 * C, C), BF),
                        pltpu.VMEM((2, C, C), BF), pltpu.VMEM((2, C, 2 * C), F32), pltpu.VMEM((2, C, 2 * C), F32),
                        pltpu.VMEM((2, C, 2 * C), F32),
                        pltpu.VMEM((T, C), F32), pltpu.VMEM((T, C), F32), pltpu.VMEM((T, C), F32),
                        pltpu.VMEM((2, T, C), F32), pltpu.VMEM((2, T, C), F32), pltpu.VMEM((2, T, C), F32),
                        pltpu.VMEM((2, T, C), BF), pltpu.VMEM((2, T, C), F32), pltpu.VMEM((2, T, C), F32),
                        pltpu.SemaphoreType.DMA((6, 2))]
        + ([pltpu.SemaphoreType.DMA((3 * n,)), pltpu.SemaphoreType.DMA((3 * n,))] if n else []),
        compiler_params=_params(("arbitrary", "arbitrary")),
    )(z, da, g, delta, ct, cst["ones_aug"], cst["mask_bias"], *parts)


def _fox_gate_bwd(drow, dcol, zf, bf_pad, cst):
    def body(dr_ref, dc_ref, zf_ref, b_ref, tri_ref, pick_ref, dff_ref, db_ref, carry):
        s = pl.program_id(0)
        n = NCH - 1 - s

        @pl.when(s == 0)
        def _():
            carry[...] = jnp.zeros_like(carry)
            db_ref[...] = jnp.zeros_like(db_ref)

        dcb = _split_dot((dr_ref[...] - dc_ref[...]) * (1.0 / FSCALE), pick_ref[...])
        suf = _split_dot(dcb, tri_ref[...], TN, x_first=False) + carry[0:1, :]
        carry[...] = jnp.broadcast_to(suf[0:1, :], carry.shape)
        x = zf_ref[...] + b_ref[...]
        row = n * C + lax.broadcasted_iota(jnp.int32, (C, C), 0)
        dff = jnp.where(row >= PAD, suf * (1.0 - jax.nn.sigmoid(x)), 0.0)
        dff_ref[...] = dff.astype(BF)
        db_ref[...] += jnp.sum(dff, axis=0, keepdims=True)

    rev = lambda s: (NCH - 1 - s, 0)
    return pl.pallas_call(
        body, name="fox_gate_bwd", grid=(NCH,),
        in_specs=[pl.BlockSpec((C, FH * FD), lambda s: (_fox_pos(NCH - 1 - s), 0)),
                  pl.BlockSpec((C, FH * FD), rev), pl.BlockSpec((C, C), rev),
                  pl.BlockSpec((1, C), lambda s: (0, 0)), pl.BlockSpec((C, C), lambda s: (0, 0)),
                  pl.BlockSpec((FH * FD, C), lambda s: (0, 0))],
        out_specs=[pl.BlockSpec((C, C), rev), pl.BlockSpec((1, C), lambda s: (0, 0))],
        out_shape=[jax.ShapeDtypeStruct((T, C), BF), jax.ShapeDtypeStruct((1, C), F32)],
        scratch_shapes=[pltpu.VMEM((8, C), F32)],
        compiler_params=_params(("arbitrary",)),
    )(drow, dcol, zf, bf_pad, cst["tri"], cst["pick"])


def _head_norm(r):
    rn, rs = [], []
    for h in range(RH):
        rh = r[:, RDV * h:RDV * (h + 1)]
        s = lax.rsqrt(jnp.mean(rh * rh, axis=1, keepdims=True) + EPS)
        rn.append(rh * s)
        rs.append(s)
    return jnp.concatenate(rn, axis=1), rs


def _gated(r, rg, a, fg):
    rn, _ = _head_norm(r)
    return jnp.concatenate([rn * (rg * jax.nn.sigmoid(rg)), a * (fg * jax.nn.sigmoid(fg))], axis=1)


def _out_loss(r, z, a, wout, x, tgt, fgain):
    def body(r_ref, rg_ref, a_ref, fg_ref, w_ref, x_ref, t_ref, g_ref, yt_ref, do_ref, dob_ref, loss_ref, dg_ref):
        i = pl.program_id(0)

        @pl.when(i == 0)
        def _():
            yt_ref[...] = jnp.zeros_like(yt_ref)
            do_ref[...] = jnp.zeros_like(do_ref)
            dob_ref[...] = jnp.zeros_like(dob_ref)
            loss_ref[...] = jnp.zeros_like(loss_ref)
            dg_ref[...] = jnp.zeros_like(dg_ref)

        @pl.when(i > 0)
        def _():
            y = _gated(r_ref[...], rg_ref[...], a_ref[...], fg_ref[...])
            yt_ref[...] = y.T.astype(BF)
            o = x_ref[...] + _dot(y.astype(BF), w_ref[...])
            rs = lax.rsqrt(jnp.mean(o * o, axis=1, keepdims=True) + EPS)
            on = o * rs
            g = g_ref[...]
            e = on * g - t_ref[...]
            loss_ref[...] += 0.5 * jnp.sum(jnp.mean(e * e, axis=1, keepdims=True))
            dyh = e * (1.0 / D)
            dg_ref[...] += jnp.sum(dyh * on, axis=0, keepdims=True)
            don = dyh * g
            do = rs * (don - on * jnp.mean(don * on, axis=1, keepdims=True))
            do_ref[...] = do
            dob_ref[...] = do.astype(BF)

    tok = lambda i: (jnp.maximum(i - 1, 0), 0)
    return pl.pallas_call(
        body, name="out_loss", grid=(NCH,),
        in_specs=[pl.BlockSpec((C, D), lambda i: (i, 0)), pl.BlockSpec((C, D), lambda i: (i, GB_R)),
                  pl.BlockSpec((C, D), lambda i: (_fox_pos(i), 0)), pl.BlockSpec((C, D), lambda i: (i, GB_F)),
                  pl.BlockSpec((DMIX, D), lambda i: (0, 0)),
                  pl.BlockSpec((C, D), tok), pl.BlockSpec((C, D), tok), pl.BlockSpec((1, D), lambda i: (0, 0))],
        out_specs=[pl.BlockSpec((DMIX, C), lambda i: (0, i)), pl.BlockSpec((C, D), lambda i: (i, 0)),
                   pl.BlockSpec((C, D), lambda i: (i, 0)), pl.BlockSpec((8, C), lambda i: (0, 0)),
                   pl.BlockSpec((1, D), lambda i: (0, 0))],
        out_shape=[jax.ShapeDtypeStruct((DMIX, T), BF), jax.ShapeDtypeStruct((T, D), F32),
                   jax.ShapeDtypeStruct((T, D), BF), jax.ShapeDtypeStruct((8, C), F32),
                   jax.ShapeDtypeStruct((1, D), F32)],
        compiler_params=_params(("arbitrary",)),
    )(r, z, a, z, wout, x, tgt, fgain)


def _silu_and_grad(x):
    s = jax.nn.sigmoid(x)
    return x * s, s * (1.0 + x * (1.0 - s))


def _dy_gate_bwd(dob, wout, r, z, a, seg, swap=()):
    n = len(swap)

    def body(do_ref, w_ref, r_ref, rg_ref, a_ref, fg_ref, seg_ref, *rest):
        (dr_ref, da_ref, drg_ref, dfg_ref, dl_ref) = rest[n:n + 5]
        if n:
            copies = _pair_copies(rest[:n], rest[n + 5:2 * n + 5], *rest[2 * n + 5:], n)

            @pl.when(pl.program_id(0) == 0)
            def _():
                for cp in copies:
                    cp.start()

            @pl.when(pl.program_id(0) == NCH - 1)
            def _():
                for cp in copies:
                    cp.wait()

        dy = _dg(do_ref[...], w_ref[...], NT)
        a_ = a_ref[...]
        rn, rs = _head_norm(r_ref[...])
        silu_rg, dsilu_rg = _silu_and_grad(rg_ref[...])
        silu_fg, dsilu_fg = _silu_and_grad(fg_ref[...])
        dyr, dyf = dy[:, :D], dy[:, D:]
        drn = dyr * silu_rg
        drg_ref[...] = (dyr * rn * dsilu_rg).astype(BF)
        for h in range(RH):
            sl = slice(RDV * h, RDV * (h + 1))
            dh, nh = drn[:, sl], rn[:, sl]
            dr_ref[:, sl] = (rs[h] * (dh - nh * jnp.mean(dh * nh, axis=1, keepdims=True))).astype(BF)
        dab = (dyf * silu_fg).astype(BF)
        da_ref[...] = dab
        dfg_ref[...] = (dyf * a_ * dsilu_fg).astype(BF)
        prod = dab.astype(F32) * a_
        segm = seg_ref[...]
        for p in range(NPAIR):
            sl = slice(C * p, C * (p + 1))
            hi = prod[:, sl].astype(BF)
            lo = (prod[:, sl] - hi.astype(F32)).astype(BF)
            dl_ref[:, sl] = _dot(hi, segm) + _dot(lo, segm)

    row = pl.BlockSpec((C, D), lambda i: (i, 0))
    fox = pl.BlockSpec((C, D), lambda i: (_fox_pos(i), 0))
    return pl.pallas_call(
        body, name="dy_gate_bwd", grid=(NCH,),
        in_specs=[row, pl.BlockSpec((DMIX, D), lambda i: (0, 0)),
                  row, pl.BlockSpec((C, D), lambda i: (i, GB_R)),
                  fox, pl.BlockSpec((C, D), lambda i: (i, GB_F)),
                  pl.BlockSpec((C, C), lambda i: (0, 0))] + [ANY] * n,
        out_specs=[row, fox, row, row, fox] + [ANY] * n,
        out_shape=[jax.ShapeDtypeStruct((T, D), BF), jax.ShapeDtypeStruct((TROWS, D), BF),
                   jax.ShapeDtypeStruct((T, D), BF), jax.ShapeDtypeStruct((T, D), BF),
                   jax.ShapeDtypeStruct((TROWS, D), F32)]
        + [jax.ShapeDtypeStruct((4, s.shape[1] // 2, s.shape[2]), s.dtype) for s in swap],
        scratch_shapes=[pltpu.SemaphoreType.DMA((n,)), pltpu.SemaphoreType.DMA((n,))] if n else [],
        compiler_params=_params(("arbitrary",)),
    )(dob, wout, r, z, a, z, seg, *swap)


DZ_WIDTHS = (512, 512, 1024, 1024, 1024, 1024, 1024, 1024)


def _du_norm_bwd(dzs, dzf, wt, wft, hpad, g, dopad, parts=()):
    tm, tk = 544, 1024
    nk = WMAIN // tk
    ni = T // tm
    n = len(parts)

    def body(rq_ref, rk_ref, rv_ref, rg_ref, fq_ref, fk_ref, fv_ref, fg_ref, dzf_ref, w_ref, wf_ref, h_ref, g_ref,
             do_ref, *rest):
        part_refs, (gh_ref, dg_ref), land_refs = rest[:n], rest[n:n + 2], rest[n + 2:2 * n + 2]
        acc = rest[2 * n + 2]
        i, k = pl.program_id(0), pl.program_id(1)

        if n:
            send_sems, recv_sems = rest[2 * n + 3:]
            copies = _chip_copies(part_refs, land_refs, send_sems, recv_sems, by_dest=True)

            @pl.when((i == 0) & (k == 0))
            def _():
                for cp in copies:
                    cp.start()

            @pl.when((i == ni - 1) & (k == nk - 1))
            def _():
                for cp in copies:
                    cp.wait()

        @pl.when(k == 0)
        def _():
            acc[...] = (_dot(dzf_ref[...], wf_ref[...]) + _dot(rq_ref[...], w_ref[:512, :])
                        + _dot(rk_ref[...], w_ref[512:, :]))

        for kk, piece in enumerate((rv_ref, rg_ref, fq_ref, fk_ref, fv_ref, fg_ref), start=1):
            @pl.when(k == kk)
            def _(piece=piece):
                acc[...] += _dot(piece[...], w_ref[...])

        @pl.when(k == nk - 1)
        def _():
            du = acc[...]
            h = h_ref[...]
            gg = g_ref[...]
            rs = lax.rsqrt(jnp.mean(h * h, axis=1, keepdims=True) + EPS)
            hn = h * rs
            part = jnp.sum(du * hn, axis=0, keepdims=True)

            @pl.when(i == 0)
            def _():
                dg_ref[...] = part

            @pl.when(i > 0)
            def _():
                dg_ref[...] += part

            dhn = du * gg
            gh_ref[...] = rs * (dhn - hn * jnp.mean(dhn * hn, axis=1, keepdims=True)) + do_ref[...]

    sems = [pltpu.SemaphoreType.DMA((3 * n,)), pltpu.SemaphoreType.DMA((3 * n,))] if n else []
    return pl.pallas_call(
        body, name="du_norm_bwd", grid=(ni, nk),
        in_specs=[pl.BlockSpec((tm, w), lambda i, k: (i, 0)) for w in DZ_WIDTHS]
        + [pl.BlockSpec((tm, C), lambda i, k: (i, 0)),
           pl.BlockSpec((tk, D), lambda i, k: (k, 0)), pl.BlockSpec((C, D), lambda i, k: (0, 0)),
           pl.BlockSpec((tm, D), lambda i, k: (i, 0)), pl.BlockSpec((1, D), lambda i, k: (0, 0)),
           pl.BlockSpec((tm, D), lambda i, k: (i, 0))] + [ANY] * n,
        out_specs=[pl.BlockSpec((tm, D), lambda i, k: (i, 0)), pl.BlockSpec((1, D), lambda i, k: (0, 0))] + [ANY] * n,
        out_shape=[jax.ShapeDtypeStruct((T, D), F32), jax.ShapeDtypeStruct((1, D), F32)]
        + [jax.ShapeDtypeStruct(p.shape, p.dtype) for p in parts],
        scratch_shapes=[pltpu.VMEM((tm, D), F32)] + sems,
        compiler_params=_params(("arbitrary", "arbitrary")),
    )(*dzs, dzf, wt, wft, hpad, g, dopad, *parts)


GROWS = 7680


def _dw_in(dzs, dzf, ut):
    tn = 512
    nmain = WMAIN // tn
    first, blocks = [], []
    for w in DZ_WIDTHS:
        first.append(sum(blocks))
        blocks.append(w // tn)

    def body(rq_ref, rk_ref, rv_ref, rg_ref, fq_ref, fk_ref, fv_ref, fg_ref, dzf_ref, ut_ref, o_ref):
        gidx = pl.program_id(0)
        for piece, g0, nb in zip((rq_ref, rk_ref, rv_ref, rg_ref, fq_ref, fk_ref, fv_ref, fg_ref), first, blocks):
            @pl.when((gidx >= g0) & (gidx < g0 + nb))
            def _(piece=piece):
                o_ref[...] = _dot(ut_ref[...], piece[...]).T.astype(BF)

        @pl.when(gidx == nmain)
        def _():
            o_ref[:C, :] = _dot(ut_ref[...], dzf_ref[...]).T.astype(BF)
            o_ref[C:, :] = jnp.zeros((tn - C, D), BF)

    def piece_spec(g0, nb):
        return pl.BlockSpec((T, tn), lambda gidx: (0, jnp.clip(gidx - g0, 0, nb - 1)))

    return pl.pallas_call(
        body, name="dw_in", grid=(nmain + 1,),
        in_specs=[piece_spec(g0, nb) for g0, nb in zip(first, blocks)]
        + [pl.BlockSpec((T, C), lambda gidx: (0, 0)), pl.BlockSpec((D, T), lambda gidx: (0, 0))],
        out_specs=pl.BlockSpec((tn, D), lambda gidx: (gidx, 0)),
        out_shape=jax.ShapeDtypeStruct((GROWS, D), BF),
        compiler_params=pltpu.CompilerParams(dimension_semantics=("arbitrary",), vmem_limit_bytes=DW_VMEM_LIMIT),
    )(*dzs, dzf, ut)


def _local_step(x, tgt, normed, norm_g, wt, wft, b_f, wout, final_g, reduce_scatter=False, gather_wout=False):
    cst = _constants()
    hpad, u, ut = normed
    bf_pad = jnp.pad(b_f, ((0, 0), (0, C - NFF)))
    z = _mm_nt(u, wt, WMAIN, T // 2, 1024, "in_proj")
    zf = _mm_nt(u, wft, C, T // 2, C, "in_proj_ff")
    r, sprev = _ret_fwd(z, cst)
    ct = _fox_prep(zf, bf_pad, cst)
    if not gather_wout:
        a, g = _fox_fwd(z, ct, cst, None)
    else:
        a, g, landed_wout = _fox_fwd(z, ct, cst, wout)
        wout = landed_wout.reshape(DMIX, D)
    yt, dopad, dob, loss8, dfg = _out_loss(r, z, a, wout, x, tgt, final_g)
    dwout = _mm_nn(yt, dob, 512, D, "dw_out", BF)
    g_out = [dwout.reshape(4, DMIX // 4, D)] if reduce_scatter else []
    dr, da, dzrg, dzfg, delta, *r_out = _dy_gate_bwd(dob, wout, r, z, a, cst["seg"], g_out)
    p_out = [_add_halves(g_out[0], r_out[0], "pair_add_out", BF)] if reduce_scatter else []
    dzq_r, dzk_r, dzv_r = _ret_bwd(z, cst, sprev, dr)
    dzq_f, drow, dzk_f, dzv_f, dcol, *e_out = _fox_bwd(z, da, g, delta, ct, cst, p_out)
    dzf, dbf = _fox_gate_bwd(drow, dcol, zf, bf_pad, cst)
    dzs = [dzq_r, dzk_r, dzv_r, dzrg, dzq_f, dzk_f, dzv_f, dzfg]
    gwt = _dw_in(dzs, dzf, ut)
    p_in = [_swap_add_windows(gwt)[1]] if reduce_scatter else []
    gh, dng, *e_in = _du_norm_bwd(dzs, dzf, wt, wft, hpad, norm_g, dopad, p_in)
    return (loss8[0, 0], gh[C:], gh[PAD:C], dng, gwt, dbf[:, :NFF], dwout, dfg, p_in + p_out, e_in + e_out)


WOFF, WLEN = 1792, 2048
WHALF = WLEN // 2
LAP = WPADROWS - WOFF


def _own_window(w3):
    rows, sub, lanes = w3.shape
    pad = WPADROWS - rows
    tb = 96
    nb = WPADROWS // tb
    half = rows // 2

    def body(w_ref, o_ref, buf, sems):
        x, y, _ = _place()
        shift = 4 * (2 * x + y)
        buf[pl.ds(0, pad)] = jnp.zeros((pad, sub, lanes), F32)
        buf[pl.ds(rows, pad)] = jnp.zeros((pad, sub, lanes), F32)
        cps = [pltpu.make_async_copy(w_ref.at[pl.ds(half * h, half)], buf.at[pl.ds(shift + half * h, half)],
                                     sems.at[h]) for h in range(2)]
        for cp in cps:
            cp.start()

        def block(i, carry):
            r0 = pl.multiple_of(i * tb, tb)
            o_ref[pl.ds(r0, tb), :] = buf[pl.ds(r0, tb)].reshape(tb, sub * lanes).astype(BF)
            return carry

        cps[0].wait()
        lax.fori_loop(0, half // tb, block, 0)
        cps[1].wait()
        lax.fori_loop(half // tb, nb, block, 0)

    return pl.pallas_call(
        body, name="own_window",
        in_specs=[ANY], out_shape=jax.ShapeDtypeStruct((WPADROWS, sub * lanes), BF),
        scratch_shapes=[pltpu.VMEM((WPADROWS, sub, lanes), F32), pltpu.SemaphoreType.DMA((2,))],
        compiler_params=pltpu.CompilerParams(vmem_limit_bytes=VMEM_LIMIT),
    )(w3)


def _gather_weights(own_win, meta, x, norm_g):
    half_main, half_lap, half_meta = WOFF // 2, LAP // 2, meta.shape[0] // 2
    last = NCH - 1

    def body(win_ref, meta_ref, x_ref, g_ref, w_ref, laps_ref, gm_ref, h_ref, u_ref, ut_ref,
             send_sems, recv_sems, local_sems, stage, lapbuf, headbuf, metabuf):
        step = pl.program_id(0)
        x, y, c = _place()
        me_s = 2 * x + y
        sib = (x, y, 1 - c)
        chips = _other_chips(x, y)

        def emit(h):
            u = _norm_rows(h, g_ref[...])
            h_ref[...] = h
            u_ref[...] = u.astype(BF)
            ut_ref[...] = u.T.astype(BF)

        kinds = [
            (lambda h: win_ref.at[pl.ds(half_main * h, half_main)],
             lambda s, h: w_ref.at[pl.ds(WOFF * s + half_main * h, half_main)]),
            (lambda h: win_ref.at[pl.ds(WOFF + half_lap * h, half_lap)],
             lambda s, h: laps_ref.at[s, pl.ds(half_lap * h, half_lap)]),
            (lambda h: meta_ref.at[pl.ds(half_meta * h, half_meta)],
             lambda s, h: gm_ref.at[s, pl.ds(half_meta * h, half_meta)]),
        ]
        own_in = pltpu.make_async_copy(win_ref.at[pl.ds(0, WOFF)], stage, local_sems.at[0])
        own_lap_in = pltpu.make_async_copy(win_ref.at[pl.ds(WOFF, LAP)], lapbuf.at[0], local_sems.at[1])
        own_out = pltpu.make_async_copy(stage, w_ref.at[pl.ds(WOFF * me_s, WOFF)], local_sems.at[0])
        own_lap_out = pltpu.make_async_copy(lapbuf.at[0], laps_ref.at[me_s], local_sems.at[1])
        sends, arrivals, forwards, forwarded = [], [], [], []
        for a, (src, dst) in enumerate(kinds):
            for k, (cx, cy, cs) in enumerate(chips):
                there = dict(send_sem=send_sems.at[6 * a + k], recv_sem=recv_sems.at[6 * a + k],
                             device_id=(cx, cy, c), device_id_type=MESH)
                across = dict(send_sem=send_sems.at[6 * a + 3 + k], recv_sem=recv_sems.at[6 * a + 3 + k],
                              device_id=sib, device_id_type=MESH)
                sends.append(pltpu.make_async_remote_copy(src_ref=src(c), dst_ref=dst(me_s, c), **there))
                arrivals.append(pltpu.make_async_remote_copy(src_ref=dst(cs, c), dst_ref=dst(cs, c), **there))
                forwards.append(pltpu.make_async_remote_copy(src_ref=dst(cs, c), dst_ref=dst(cs, c), **across))
                forwarded.append(pltpu.make_async_remote_copy(
                    src_ref=dst(cs, 1 - c), dst_ref=dst(cs, 1 - c), **across))

        @pl.when(step == 0)
        def _():
            own_in.start()
            own_lap_in.start()
            for cp in sends:
                cp.start()
            own_in.wait()
            own_out.start()
            own_lap_in.wait()
            own_lap_out.start()

        @pl.when(step < last)
        def _():
            emit(x_ref[...])

        @pl.when(step == last)
        def _():
            for cp, fwd in zip(arrivals, forwards):
                cp.wait_recv()
                fwd.start()
            for cp in forwarded:
                cp.wait_recv()
            for cp in sends + forwards:
                cp.wait_send()
            own_out.wait()
            own_lap_out.wait()
            heads = [w_ref.at[pl.ds(WOFF * s, LAP)] for s in range(1, 4)]
            lap_loads = [pltpu.make_async_copy(laps_ref.at[i], lapbuf.at[1 + i], local_sems.at[4 + 2 * i])
                         for i in range(3)]
            head_loads = [pltpu.make_async_copy(heads[i], headbuf.at[i], local_sems.at[5 + 2 * i]) for i in range(3)]
            head_stores = [pltpu.make_async_copy(headbuf.at[i], heads[i], local_sems.at[5 + 2 * i]) for i in range(3)]
            loads = [pltpu.make_async_copy(meta_ref, metabuf.at[me_s], local_sems.at[0])]
            loads += [pltpu.make_async_copy(gm_ref.at[cs], metabuf.at[cs], local_sems.at[1 + k])
                      for k, (_, _, cs) in enumerate(chips)]
            for cp in lap_loads + head_loads + loads:
                cp.start()
            for i in range(3):
                lap_loads[i].wait()
                head_loads[i].wait()
                headbuf[i] = (headbuf[i].astype(F32) + lapbuf[1 + i].astype(F32)).astype(BF)
                head_stores[i].start()
            for cp in loads:
                cp.wait()
            tokens = jnp.concatenate([metabuf[s] for s in range(4)], axis=1)
            emit(jnp.concatenate([jnp.zeros((PAD, D), F32), tokens], axis=0))
            for cp in head_stores:
                cp.wait()

    def chunk(i):
        return (i + 1) % NCH

    return pl.pallas_call(
        body, name="all_gather_w", grid=(NCH,),
        in_specs=[ANY, ANY, pl.BlockSpec((C, D), lambda i: (jnp.minimum(i, last - 1), 0)),
                  pl.BlockSpec((1, D), lambda i: (0, 0))],
        out_specs=[ANY] * 3 + [pl.BlockSpec((C, D), lambda i: (chunk(i), 0))] * 2
        + [pl.BlockSpec((D, C), lambda i: (0, chunk(i)))],
        out_shape=[jax.ShapeDtypeStruct((WMAIN, D), own_win.dtype), jax.ShapeDtypeStruct((4, LAP, D), own_win.dtype),
                   jax.ShapeDtypeStruct((4,) + meta.shape, meta.dtype),
                   jax.ShapeDtypeStruct((T, D), F32), jax.ShapeDtypeStruct((T, D), BF),
                   jax.ShapeDtypeStruct((D, T), BF)],
        scratch_shapes=[pltpu.SemaphoreType.DMA((18,)), pltpu.SemaphoreType.DMA((18,)), pltpu.SemaphoreType.DMA((10,)),
                        pltpu.VMEM((WOFF, D), own_win.dtype), pltpu.VMEM((4, LAP, D), own_win.dtype),
                        pltpu.VMEM((3, LAP, D), own_win.dtype), pltpu.VMEM((4,) + meta.shape, meta.dtype)],
        compiler_params=_params(("arbitrary",)),
    )(own_win, meta, x, norm_g)


def _pair_copies(ins, outs, send_sems, recv_sems, n):
    x, y, c = _place()
    sib = dict(device_id=(x, y, 1 - c), device_id_type=MESH)
    cps = []
    for a in range(n):
        rows = ins[a].shape[1] // 2
        cps.append(pltpu.make_async_remote_copy(
            src_ref=ins[a].at[:, pl.ds((1 - c) * rows, rows)], dst_ref=outs[a],
            send_sem=send_sems.at[a], recv_sem=recv_sems.at[a], **sib))
    for k in range(4 * (len(ins) - n)):
        cps.append(pltpu.make_async_remote_copy(
            src_ref=ins[n].at[pl.ds(WOFF * k + (1 - c) * WHALF, WHALF)], dst_ref=outs[n].at[k],
            send_sem=send_sems.at[n + k], recv_sem=recv_sems.at[n + k], **sib))
    return cps


def _swap_add_windows(gwt):
    nchunk = 4
    rows = WHALF // nchunk

    def body(gw_ref, land_ref, out_ref, send_sems, recv_sems, local_sems, own, theirs):
        _, _, c = _place()
        swaps = _pair_copies([gw_ref], [land_ref], send_sems, recv_sems, 0)
        loads = [pltpu.make_async_copy(gw_ref.at[pl.ds(WOFF * k + c * WHALF, WHALF)], own.at[k], local_sems.at[k])
                 for k in range(4)]
        stores = [pltpu.make_async_copy(own.at[k], out_ref.at[k], local_sems.at[k]) for k in range(4)]
        for cp in loads + swaps:
            cp.start()
        for k in range(4):
            swaps[k].wait()
            fetch = pltpu.make_async_copy(land_ref.at[k], theirs, local_sems.at[4])
            fetch.start()
            loads[k].wait()
            fetch.wait()

            def add(i, carry, k=k):
                r = _rows(i, rows)
                own[k, r, :] = (own[k, r, :].astype(F32) + theirs[r, :].astype(F32)).astype(BF)
                return carry

            lax.fori_loop(0, nchunk, add, 0)
            stores[k].start()
        for cp in stores:
            cp.wait()

    return pl.pallas_call(
        body, name="rs_pair_swap_add",
        in_specs=[ANY], out_specs=[ANY, ANY],
        out_shape=[jax.ShapeDtypeStruct((4, WHALF, D), gwt.dtype), jax.ShapeDtypeStruct((4, WHALF, D), BF)],
        scratch_shapes=[pltpu.SemaphoreType.DMA((4,)), pltpu.SemaphoreType.DMA((4,)), pltpu.SemaphoreType.DMA((5,)),
                        pltpu.VMEM((4, WHALF, D), gwt.dtype), pltpu.VMEM((WHALF, D), gwt.dtype)],
        compiler_params=pltpu.CompilerParams(vmem_limit_bytes=VMEM_LIMIT),
    )(gwt)


def _pair_send(halves):
    n = len(halves)

    def body(*refs):
        ins, outs = refs[:n], refs[n:2 * n]
        send_sems, recv_sems = refs[2 * n:]
        x, y, c = _place()
        cps = [pltpu.make_async_remote_copy(
            src_ref=ins[a], dst_ref=outs[a], send_sem=send_sems.at[a], recv_sem=recv_sems.at[a],
            device_id=(x, y, 1 - c), device_id_type=MESH) for a in range(n)]
        for cp in cps:
            cp.start()
        for cp in cps:
            cp.wait()

    return pl.pallas_call(
        body, name="rs_pair_send",
        in_specs=[ANY] * n, out_specs=[ANY] * n,
        out_shape=[jax.ShapeDtypeStruct(h.shape, h.dtype) for h in halves],
        scratch_shapes=[pltpu.SemaphoreType.DMA((n,)), pltpu.SemaphoreType.DMA((n,))],
    )(*halves)


def _row_block(rows):
    for tb in (256, 128, 64, 32, 16, 8):
        if rows % tb == 0:
            return tb
    return rows


def _add_halves(full, recv, name, out_dtype):
    _, r2, w = recv.shape
    tb = _row_block(r2)
    nb = r2 // tb
    c = lax.axis_index("c")

    def body(c_ref, a_ref, b_ref, o_ref):
        o_ref[...] = (a_ref[...].astype(F32) + b_ref[...].astype(F32)).astype(o_ref.dtype)

    return pl.pallas_call(
        body, name=name,
        grid_spec=pltpu.PrefetchScalarGridSpec(
            num_scalar_prefetch=1, grid=(4, nb),
            in_specs=[pl.BlockSpec((1, tb, w), lambda s, i, cr: (s, cr[0] * nb + i, 0)),
                      pl.BlockSpec((1, tb, w), lambda s, i, cr: (s, i, 0))],
            out_specs=pl.BlockSpec((1, tb, w), lambda s, i, cr: (s, i, 0))),
        out_shape=jax.ShapeDtypeStruct(recv.shape, out_dtype),
        compiler_params=_params(("parallel", "parallel")),
    )(jnp.reshape(c, (1,)).astype(jnp.int32), full, recv)


def _add2(a, b, name):
    def body(a_ref, b_ref, o_ref):
        o_ref[...] = a_ref[...] + b_ref[...]

    return pl.pallas_call(body, name=name, out_shape=jax.ShapeDtypeStruct(a.shape, a.dtype))(a, b)


def _sum4(buf, own, name, exchange=None):
    _, r, w = buf.shape
    tb = _row_block(r)
    nsteps = r // tb
    me_s = 2 * lax.axis_index("x") + lax.axis_index("y")
    by_dest = own.ndim == 3
    carried = [] if exchange is None else [*exchange[0], exchange[1]]
    m = len(carried)

    def body(s_ref, b_ref, own_ref, *rest):
        o_ref = rest[m]
        if m:
            ins, outs, (send_sems, recv_sems) = rest[:m], rest[m + 1:2 * m + 1], rest[2 * m + 1:]
            cps = _chip_copies(ins[:-1], outs[:-1], send_sems, recv_sems, by_dest=True)
            cps += _chip_copies(ins[-1:], outs[-1:], send_sems.at[pl.ds(3 * (m - 1), 3)],
                                recv_sems.at[pl.ds(3 * (m - 1), 3)], by_dest=False)

            @pl.when(pl.program_id(0) == 0)
            def _():
                for cp in cps:
                    cp.start()

            @pl.when(pl.program_id(0) == nsteps - 1)
            def _():
                for cp in cps:
                    cp.wait()

        mine = (own_ref[0] if by_dest else own_ref[...]).astype(F32)
        terms = [jnp.where(s_ref[0] == t, mine, b_ref[t].astype(F32)) for t in range(4)]
        o_ref[...] = ((terms[0] + terms[1]) + terms[2]) + terms[3]

    own_spec = (pl.BlockSpec((1, tb, w), lambda i, sr: (sr[0], i, 0)) if by_dest
                else pl.BlockSpec((tb, w), lambda i, sr: (i, 0)))
    landing = [jax.ShapeDtypeStruct(p.shape, p.dtype) for p in carried[:-1]]
    landing += [jax.ShapeDtypeStruct((4,) + s.shape, s.dtype) for s in carried[-1:]]
    return pl.pallas_call(
        body, name=name,
        grid_spec=pltpu.PrefetchScalarGridSpec(
            num_scalar_prefetch=1, grid=(nsteps,),
            in_specs=[pl.BlockSpec((4, tb, w), lambda i, sr: (0, i, 0)), own_spec] + [ANY] * m,
            out_specs=[pl.BlockSpec((tb, w), lambda i, sr: (i, 0))] + [ANY] * m,
            scratch_shapes=[pltpu.SemaphoreType.DMA((3 * m,)), pltpu.SemaphoreType.DMA((3 * m,))] if m else []),
        out_shape=[jax.ShapeDtypeStruct((r, w), F32)] + landing,
        compiler_params=_params(("arbitrary" if m else "parallel",)),
    )(jnp.reshape(me_s, (1,)).astype(jnp.int32), buf, own, *carried)


def _adamw_math(w, g, m, v):
    mn = B1 * m + (1.0 - B1) * g
    vn = B2 * v + (1.0 - B2) * (g * g)
    m_hat = mn / (1.0 - B1 ** STEP)
    v_hat = vn / (1.0 - B2 ** STEP)
    return -LR * (m_hat / (jnp.sqrt(v_hat) + AEPS) + WD * w), mn, vn


def _adamw(w, g, m, v, name):
    r, c_ = w.shape
    tb = _row_block(r)
    if tb == r and r > 512:
        tb = 256

    def body(w_ref, g_ref, m_ref, v_ref, d_ref, mo_ref, vo_ref):
        d_ref[...], mo_ref[...], vo_ref[...] = _adamw_math(w_ref[...], g_ref[...], m_ref[...], v_ref[...])

    spec = pl.BlockSpec((tb, c_), lambda i: (i, 0))
    return pl.pallas_call(
        body, name=name, grid=(pl.cdiv(r, tb),),
        in_specs=[spec] * 4, out_specs=[spec] * 3,
        out_shape=[jax.ShapeDtypeStruct(w.shape, F32)] * 3,
        compiler_params=_params(("parallel",)),
    )(w, g, m, v)


def _adamw_rows(w, g_mine, g_sib, m, v, name):
    r = w.shape[0]
    tb = 256
    sub, lanes = w.shape[1:]
    nh = g_mine.shape[0] // tb
    nsteps = pl.cdiv(r, tb)
    assert nsteps <= 2 * nh and 4 * 3 + r <= 2 * nh * tb
    x, y, c = _place()
    place = jnp.stack([c, 4 * (2 * x + y)]).astype(jnp.int32)

    def body(p_ref, w_ref, mc_ref, sc_ref, mn_ref, sn_ref, m_ref, v_ref, go_ref, d_ref, mo_ref, vo_ref, buf):
        i = pl.program_id(0)
        for at, blk, mine_ref, sib_ref in ((0, i, mc_ref, sc_ref), (1, jnp.minimum(i + 1, 2 * nh - 1), mn_ref, sn_ref)):
            rows = jnp.where(blk // nh == p_ref[0], mine_ref[...], sib_ref[...])
            buf[tb * at:tb * (at + 1)] = rows.reshape(tb, sub, lanes)
        g = buf[pl.ds(p_ref[1], tb)]
        go_ref[...] = g
        d_ref[...], mo_ref[...], vo_ref[...] = _adamw_math(w_ref[...], g, m_ref[...], v_ref[...])

    def half_spec(ahead, sibling):
        def index(i, pr):
            half = (1 - pr[0]) if sibling else pr[0]
            return (jnp.clip(jnp.minimum(i + ahead, 2 * nh - 1) - nh * half, 0, nh - 1), 0)
        return pl.BlockSpec((tb, sub * lanes), index)

    spec = pl.BlockSpec((tb, sub, lanes), lambda i, pr: (i, 0, 0))
    return pl.pallas_call(
        body, name=name,
        grid_spec=pltpu.PrefetchScalarGridSpec(
            num_scalar_prefetch=1, grid=(nsteps,),
            in_specs=[spec, half_spec(0, False), half_spec(0, True), half_spec(1, False), half_spec(1, True),
                      spec, spec],
            out_specs=[spec] * 4,
            scratch_shapes=[pltpu.VMEM((2 * tb, sub, lanes), F32)]),
        out_shape=[jax.ShapeDtypeStruct(w.shape, F32)] * 4,
        compiler_params=_params(("parallel",)),
    )(place, w, g_mine, g_sib, g_mine, g_sib, m, v)


def _adamw_halves(w, g_mine, g_sib, m, v, name):
    r, c_ = w.shape
    r2 = g_mine.shape[0]
    tb = _row_block(r2)
    nb = r2 // tb
    c = lax.axis_index("c")

    def body(c_ref, w_ref, gm_ref, gs_ref, m_ref, v_ref, g_ref, d_ref, mo_ref, vo_ref):
        g = jnp.where(pl.program_id(0) == c_ref[0], gm_ref[...], gs_ref[...])
        g_ref[...] = g
        d_ref[...], mo_ref[...], vo_ref[...] = _adamw_math(w_ref[...], g, m_ref[...], v_ref[...])

    full = pl.BlockSpec((tb, c_), lambda h, i, cr: (h * nb + i, 0))
    half = pl.BlockSpec((tb, c_), lambda h, i, cr: (i, 0))
    return pl.pallas_call(
        body, name=name,
        grid_spec=pltpu.PrefetchScalarGridSpec(
            num_scalar_prefetch=1, grid=(2, nb),
            in_specs=[full, half, half, full, full], out_specs=[full] * 4),
        out_shape=[jax.ShapeDtypeStruct(w.shape, F32)] * 4,
        compiler_params=_params(("parallel", "parallel")),
    )(jnp.reshape(c, (1,)).astype(jnp.int32), w, g_mine, g_sib, m, v)


def kernel(x, meta_tokens, norm_g, w_in, b_f, w_out, final_g, loss_target, m_meta_tokens, m_norm_g, m_w_in, m_b_f, m_w_out, m_final_g, v_meta_tokens, v_norm_g, v_w_in, v_b_f, v_w_out, v_final_g):
    w3, m3, v3 = [jnp.transpose(jnp.reshape(t[0], (D // C, C, WSH)), (2, 0, 1)) for t in (w_in, m_w_in, v_w_in)]

    wt_main, laps, _, *normed = _gather_weights(_own_window(w3), meta_tokens, x[0], norm_g)
    wft = jnp.pad(laps[3, :NFF], ((0, C - NFF), (0, 0)))
    wout_own = w_out[0].astype(BF)

    loss, gx, dmeta, dng, gwt, dbf, dwout, dfg, (p_in, p_out), (e_in, e_out) = _local_step(
        x[0], loss_target[0], normed, norm_g, wt_main, wft, b_f, wout_own, final_g.reshape(1, D), True, True)

    g_meta = jnp.stack([dmeta[:, 256 * s:256 * (s + 1)] for s in range(4)])
    small = jnp.concatenate([dng, dfg, jnp.pad(dbf, ((0, 0), (0, D - NFF))),
                             jnp.pad(jnp.reshape(loss, (1, 1)), ((0, 0), (0, D - 1))),
                             jnp.zeros((4, D), F32)], axis=0)
    h_in, e_meta, e_small = _sum4(e_in, p_in, "sum_in", exchange=([g_meta], small))
    (h_out,), (h_meta,), (h_small,) = (_sum4(e_out, p_out, "sum_out"), _sum4(e_meta, g_meta, "sum_meta"),
                                       _sum4(e_small, small, "sum_small"))
    s_in, s_out, s_meta, s_small = _pair_send([h_in, h_out, h_meta, h_small])
    gw_meta = _add2(h_meta, s_meta, "pair_add_meta")
    tot = _add2(h_small, s_small, "pair_add_small")
    g_norm, g_final, g_bf, loss_all = tot[0:1], tot[1], tot[2:3, :NFF], tot[3, 0]

    d_meta, nm_meta, nv_meta = _adamw(meta_tokens, gw_meta, m_meta_tokens, v_meta_tokens, "adamw_meta")
    d_norm, nm_norm, nv_norm = _adamw(norm_g, g_norm, m_norm_g, v_norm_g, "adamw_norm")
    outs_in = _adamw_rows(w3, h_in, s_in, m3, v3, "adamw_in")
    gw_in, d_in, nm_in, nv_in = [jnp.reshape(jnp.transpose(t, (1, 2, 0)), (1, D, WSH)) for t in outs_in]
    d_bf, nm_bf, nv_bf = _adamw(b_f, g_bf, m_b_f, v_b_f, "adamw_bf")
    gw_out, d_out, nm_out, nv_out = _adamw_halves(w_out[0], h_out, s_out, m_w_out[0], v_w_out[0], "adamw_out")
    d_fin, nm_fin, nv_fin = _adamw(final_g.reshape(1, D), g_final.reshape(1, D), m_final_g.reshape(1, D),
                                   v_final_g.reshape(1, D), "adamw_final")
    return (loss_all, gx[None], gw_meta, g_norm, gw_in, g_bf, gw_out[None], g_final,
            d_meta, d_norm, d_in, d_bf, d_out[None], d_fin.reshape(D),
            nm_meta, nm_norm, nm_in, nm_bf, nm_out[None], nm_fin.reshape(D),
            nv_meta, nv_norm, nv_in, nv_bf, nv_out[None], nv_fin.reshape(D))
```

```python
import numpy as np
import jax
import jax.numpy as jnp
from jax import lax
from jax.experimental import pallas as pl
from jax.experimental.pallas import tpu as pltpu

D = 1024
SEQ = 2048
NMETA = 16
C = 128
PAD = C - NMETA
T = PAD + NMETA + SEQ
NCH = T // C
RH, RDK, RDV = 4, 128, 256
FH, FD = 16, 64
NPAIR = FH // 2
WMAIN = 7168
NFF = 16
WIN = WMAIN + NFF
WSH = WIN // 4
WPADROWS = 1824
DMIX = 2048
EPS = 1e-6
NEG = -1e30
RSCALE = RDK ** -0.5
FSCALE = FD ** -0.5
ROPE_BASE = 10000.0
LR, B1, B2, AEPS, WD, STEP = 0.001, 0.9, 0.999, 1e-08, 0.01, 10

BF = jnp.bfloat16
F32 = jnp.float32
NT = (((1,), (1,)), ((), ()))
TN = (((0,), (0,)), ((), ()))
NN_DIMS = (((1,), (0,)), ((), ()))
MESH = pl.DeviceIdType.MESH
ANY = pl.BlockSpec(memory_space=pl.ANY)
VMEM_LIMIT = 48 * 1024 * 1024
DW_VMEM_LIMIT = 56 * 1024 * 1024

GB_R, GB_F = 2, 6
QB_F, KB_F, VB_F = 24, 32, 40


def _dot(a, b):
    return jnp.dot(a, b, preferred_element_type=F32)


def _dg(a, b, dims):
    return lax.dot_general(a, b, dims, preferred_element_type=F32)


def _params(sem=None):
    return pltpu.CompilerParams(dimension_semantics=sem, vmem_limit_bytes=VMEM_LIMIT)


def _constants():
    pos = jnp.arange(T, dtype=F32) - PAD
    inv = ROPE_BASE ** (-jnp.arange(0, RDK, 2, dtype=F32) / RDK)
    ang = pos[:, None] * inv[None, :]
    cos, sin = jnp.cos(ang), jnp.sin(ang)
    cos2 = jnp.concatenate([cos, cos], axis=1)
    sin2 = jnp.concatenate([-sin, sin], axis=1)
    log_gamma = jnp.log1p(-jnp.exp2(-5.0 - jnp.arange(RH, dtype=F32)))
    idx = jnp.arange(C, dtype=F32)
    diff = idx[:, None] - idx[None, :]
    dmask = jnp.where(diff[None] >= 0, jnp.exp(log_gamma[:, None, None] * jnp.maximum(diff, 0.0)[None]), 0.0)
    zeta = jnp.exp(log_gamma[:, None] * (C - 1.0 - idx)[None, :])
    xi = jnp.exp(log_gamma[:, None] * (idx + 1.0)[None, :])
    gdec = jnp.exp(log_gamma * C)
    zeta_b = jnp.broadcast_to(zeta[:, :, None], (RH, C, RDK))
    xi_b = jnp.broadcast_to(xi[:, :, None], (RH, C, RDK))
    gdec_b = jnp.broadcast_to(gdec[:, None, None], (RH, RDK, RDV))
    tri = jnp.asarray(np.tril(np.ones((C, C), np.float32)), dtype=BF)
    head_of_lane = np.arange(FH * FD) // FD
    pick = ((np.arange(FH * FD)[:, None] % FD == 0)
            & (head_of_lane[:, None] == np.arange(C)[None, :])).astype(np.float32)
    seg = (np.arange(C)[:, None] // FD == np.arange(C)[None, :] // FD).astype(np.float32)
    ones_aug = np.concatenate([np.tile((np.arange(C) < FD)[None, :], (C, 1)),
                               np.tile((np.arange(C) >= FD)[None, :], (C, 1))], axis=0).astype(np.float32)
    lane = np.arange(2 * C) % C
    causal = np.where(lane[None, :] <= np.arange(C)[:, None], 0.0, NEG).astype(np.float32)
    mask_bias = np.stack([np.zeros((C, 2 * C), np.float32), causal])
    return dict(cos2=cos2, sin2=sin2, dmask=dmask, zeta=zeta_b, xi=xi_b, gdec=gdec_b, tri=tri,
                mask_bias=jnp.asarray(mask_bias), pick=jnp.asarray(pick, dtype=BF), seg=jnp.asarray(seg, dtype=BF),
                ones_aug=jnp.asarray(ones_aug, dtype=BF))


def _norm_rows(h, g):
    return h * lax.rsqrt(jnp.mean(h * h, axis=1, keepdims=True) + EPS) * g


def _mm_nt(a, b, n, tm, tn, name):
    m, k = a.shape

    def body(a_ref, b_ref, o_ref):
        o_ref[...] = _dg(a_ref[...], b_ref[...], NT)

    return pl.pallas_call(
        body, name=name, grid=(m // tm, n // tn),
        in_specs=[pl.BlockSpec((tm, k), lambda i, j: (i, 0)), pl.BlockSpec((tn, k), lambda i, j: (j, 0))],
        out_specs=pl.BlockSpec((tm, tn), lambda i, j: (i, j)),
        out_shape=jax.ShapeDtypeStruct((m, n), F32),
        compiler_params=_params(("parallel", "parallel")),
    )(a, b)


def _mm_nn(a, b, tm, tn, name, out_dtype=F32):
    m, k = a.shape
    _, n = b.shape

    def body(a_ref, b_ref, o_ref):
        o_ref[...] = _dot(a_ref[...], b_ref[...]).astype(out_dtype)

    return pl.pallas_call(
        body, name=name, grid=(m // tm, n // tn),
        in_specs=[pl.BlockSpec((tm, k), lambda i, j: (i, 0)), pl.BlockSpec((k, tn), lambda i, j: (0, j))],
        out_specs=pl.BlockSpec((tm, tn), lambda i, j: (i, j)),
        out_shape=jax.ShapeDtypeStruct((m, n), out_dtype),
        compiler_params=_params(("parallel", "parallel")),
    )(a, b)


def _rot(x, cos2, sin2):
    return x * cos2 + pltpu.roll(x, 64, 1) * sin2


def _ret_specs(chunk):
    whole = lambda shape: pl.BlockSpec(shape, lambda n: (0,) * len(shape))
    return [
        pl.BlockSpec((C, RH * RDK), lambda n: (chunk(n), 0)),
        pl.BlockSpec((C, RH * RDK), lambda n: (chunk(n), 1)),
        pl.BlockSpec((C, RH * RDV), lambda n: (chunk(n), 1)),
        pl.BlockSpec((C, RDK), lambda n: (chunk(n), 0)),
        pl.BlockSpec((C, RDK), lambda n: (chunk(n), 0)),
        whole((RH, C, C)), whole((RH, C, RDK)), whole((RH, C, RDK)), whole((RH, RDK, RDV)),
    ]


def _ret_heads(q_ref, k_ref, v_ref, cos, sin):
    qr = [_rot(q_ref[:, RDK * h:RDK * (h + 1)], cos, sin) for h in range(RH)]
    kr = [_rot(k_ref[:, RDK * h:RDK * (h + 1)], cos, sin) * RSCALE for h in range(RH)]
    vb = [v_ref[:, RDV * h:RDV * (h + 1)].astype(BF) for h in range(RH)]
    return qr, kr, [t.astype(BF) for t in qr], [t.astype(BF) for t in kr], vb


def _ret_fwd(z, cst):
    def body(q_ref, k_ref, v_ref, cos_ref, sin_ref, dm_ref, xi_ref, zt_ref, gd_ref, r_ref, sp_ref, st):
        n = pl.program_id(0)

        @pl.when(n == 0)
        def _():
            st[...] = jnp.zeros_like(st)

        hs = range(RH)
        qr, kr, qb, kb, vb = _ret_heads(q_ref, k_ref, v_ref, cos_ref[...], sin_ref[...])
        sd = [(_dg(qb[h], kb[h], NT) * dm_ref[h]).astype(BF) for h in hs]
        state = [st[h] for h in hs]
        qx = [(qr[h] * xi_ref[h]).astype(BF) for h in hs]
        kz = [(kr[h] * zt_ref[h]).astype(BF) for h in hs]
        out = [_dot(sd[h], vb[h]) + _dot(qx[h], state[h].astype(BF)) for h in hs]
        kv = [_dg(kz[h], vb[h], TN) for h in hs]
        for h in hs:
            sp_ref[0, h] = state[h]
            r_ref[:, RDV * h:RDV * (h + 1)] = out[h]
            st[h] = state[h] * gd_ref[h] + kv[h]

    return pl.pallas_call(
        body, name="ret_fwd", grid=(NCH,),
        in_specs=_ret_specs(lambda n: n),
        out_specs=[pl.BlockSpec((C, RH * RDV), lambda n: (n, 0)),
                   pl.BlockSpec((1, RH, RDK, RDV), lambda n: (n, 0, 0, 0))],
        out_shape=[jax.ShapeDtypeStruct((T, RH * RDV), F32), jax.ShapeDtypeStruct((NCH, RH, RDK, RDV), F32)],
        scratch_shapes=[pltpu.VMEM((RH, RDK, RDV), F32)],
        compiler_params=_params(("arbitrary",)),
    )(z, z, z, cst["cos2"], cst["sin2"], cst["dmask"], cst["xi"], cst["zeta"], cst["gdec"])


def _ret_bwd(z, cst, sprev, dr):
    def body(q_ref, k_ref, v_ref, cos_ref, sin_ref, dm_ref, xi_ref, zt_ref, gd_ref, sp_ref, dr_ref,
             dq_ref, dk_ref, dv_ref, gst):
        i = pl.program_id(0)

        @pl.when(i == 0)
        def _():
            gst[...] = jnp.zeros_like(gst)

        hs = range(RH)
        cos, sin = cos_ref[...], sin_ref[...]
        qr, kr, qb, kb, vb = _ret_heads(q_ref, k_ref, v_ref, cos, sin)
        dm = [dm_ref[h] for h in hs]
        xi = [xi_ref[h] for h in hs]
        zt = [zt_ref[h] for h in hs]
        sd = [(_dg(qb[h], kb[h], NT) * dm[h]).astype(BF) for h in hs]
        qx = [(qr[h] * xi[h]).astype(BF) for h in hs]
        kz = [(kr[h] * zt[h]).astype(BF) for h in hs]
        drb = [dr_ref[:, RDV * h:RDV * (h + 1)] for h in hs]
        sb = [sp_ref[0, h].astype(BF) for h in hs]
        g = [gst[h] for h in hs]
        gb = [t.astype(BF) for t in g]
        ds = [(_dg(drb[h], vb[h], NT) * dm[h]).astype(BF) for h in hs]
        dq = [_dot(ds[h], kb[h]) + _dg(drb[h], sb[h], NT) * xi[h] for h in hs]
        dk = [(_dg(ds[h], qb[h], TN) + _dg(vb[h], gb[h], NT) * zt[h]) * RSCALE for h in hs]
        dv = [_dg(sd[h], drb[h], TN) + _dot(kz[h], gb[h]) for h in hs]
        gn = [g[h] * gd_ref[h] + _dg(qx[h], drb[h], TN) for h in hs]
        for h in hs:
            gst[h] = gn[h]
            dq_ref[:, RDK * h:RDK * (h + 1)] = (dq[h] * cos + pltpu.roll(dq[h] * sin, 64, 1)).astype(BF)
            dk_ref[:, RDK * h:RDK * (h + 1)] = (dk[h] * cos + pltpu.roll(dk[h] * sin, 64, 1)).astype(BF)
            dv_ref[:, RDV * h:RDV * (h + 1)] = dv[h].astype(BF)

    rev = lambda n: NCH - 1 - n
    return pl.pallas_call(
        body, name="ret_bwd", grid=(NCH,),
        in_specs=_ret_specs(rev) + [
            pl.BlockSpec((1, RH, RDK, RDV), lambda n: (rev(n), 0, 0, 0)),
            pl.BlockSpec((C, RH * RDV), lambda n: (rev(n), 0)),
        ],
        out_specs=[pl.BlockSpec((C, RH * RDK), lambda n: (rev(n), 0)),
                   pl.BlockSpec((C, RH * RDK), lambda n: (rev(n), 0)),
                   pl.BlockSpec((C, RH * RDV), lambda n: (rev(n), 0))],
        out_shape=[jax.ShapeDtypeStruct((T, RH * RDK), BF), jax.ShapeDtypeStruct((T, RH * RDK), BF),
                   jax.ShapeDtypeStruct((T, RH * RDV), BF)],
        scratch_shapes=[pltpu.VMEM((RH, RDK, RDV), F32)],
        compiler_params=_params(("arbitrary",)),
    )(z, z, z, cst["cos2"], cst["sin2"], cst["dmask"], cst["xi"], cst["zeta"], cst["gdec"], sprev, dr)


def _place():
    x, y, c = lax.axis_index("x"), lax.axis_index("y"), lax.axis_index("c")
    return x, y, c


def _other_chips(x, y):
    return [(1 - x, y, 2 * (1 - x) + y), (x, 1 - y, 2 * x + (1 - y)), (1 - x, 1 - y, 2 * (1 - x) + (1 - y))]


def _chip_copies(srcs, lands, send_sems, recv_sems, by_dest):
    x, y, c = _place()
    me_s = 2 * x + y
    return [pltpu.make_async_remote_copy(
        src_ref=src.at[cs] if by_dest else src, dst_ref=land.at[me_s],
        send_sem=send_sems.at[3 * a + j], recv_sem=recv_sems.at[3 * a + j],
        device_id=(cx, cy, c), device_id_type=MESH)
        for a, (src, land) in enumerate(zip(srcs, lands)) for j, (cx, cy, cs) in enumerate(_other_chips(x, y))]


def _split_dot(x, mat01, dims=NN_DIMS, x_first=True):
    acc, rest = None, x
    for _ in range(3):
        piece = rest.astype(BF)
        part = _dg(piece, mat01, dims) if x_first else _dg(mat01, piece, dims)
        acc = part if acc is None else acc + part
        rest = rest - piece.astype(F32)
    return acc


def _log_sigmoid(x):
    return -(jnp.maximum(-x, 0.0) + jnp.log1p(jnp.exp(-jnp.abs(x))))


def _fox_prep(zf, bf_pad, cst):
    def body(zf_ref, b_ref, tri_ref, ct_ref, carry):
        n = pl.program_id(0)

        @pl.when(n == 0)
        def _():
            carry[...] = jnp.zeros_like(carry)

        ls = _log_sigmoid(zf_ref[...] + b_ref[...])
        row = n * C + lax.broadcasted_iota(jnp.int32, (C, C), 0)
        lf = jnp.where(row >= PAD, ls, 0.0)
        cc = _split_dot(lf, tri_ref[...], x_first=False) + carry[0:1, :]
        carry[...] = jnp.broadcast_to(cc[C - 1:C, :], carry.shape)
        pos = n * C + lax.broadcasted_iota(jnp.int32, (FH, C), 1)
        ct_ref[0] = jnp.where(pos >= PAD, cc.T[:FH, :], -NEG)

    return pl.pallas_call(
        body, name="fox_prep", grid=(NCH,),
        in_specs=[pl.BlockSpec((C, C), lambda n: (n, 0)), pl.BlockSpec((1, C), lambda n: (0, 0)),
                  pl.BlockSpec((C, C), lambda n: (0, 0))],
        out_specs=pl.BlockSpec((1, FH, C), lambda n: (n, 0, 0)),
        out_shape=jax.ShapeDtypeStruct((NCH, FH, C), F32),
        scratch_shapes=[pltpu.VMEM((8, C), F32)],
        compiler_params=_params(("arbitrary",)),
    )(zf, bf_pad, cst["tri"])


def _lo_lanes(shape):
    return lax.broadcasted_iota(jnp.int32, shape, 1) < FD


def _split_heads(x):
    lo = _lo_lanes(x.shape)
    zero = jnp.zeros_like(x)
    return jnp.concatenate([jnp.where(lo, x, zero), jnp.where(lo, zero, x)], axis=0)


def _spread2(x):
    lo = _lo_lanes(x.shape)
    r = pltpu.roll(x, FD, 1)
    return jnp.concatenate([jnp.where(lo, x, r), jnp.where(lo, r, x)], axis=1)


NSTEP = (NCH + 1) // 2
NTILE = NCH + 1
TROWS = T + C


def _fox_tile(s, t):
    second = t > s
    return second.astype(jnp.int32), jnp.where(second, t - s - 1, s - t)


def _fox_pos(i):
    return jnp.where(i < NSTEP, 2 * i, 2 * (NCH - 1 - i) + 1)


def _fox_pair_columns():
    return pl.BlockSpec((TROWS, C), lambda p, s: (0, p))


def _fox_key_bias(ct_ref, p, j):
    return jnp.concatenate([ct_ref[j, pl.ds(2 * p, 1), :], ct_ref[j, pl.ds(2 * p + 1, 1), :]], axis=1)


def _fox_columns(cols, sems, p):
    def copies(pair, slot):
        return [pltpu.make_async_copy(
            src.at[pl.ds(0, buf.shape[1]), pl.ds(pl.multiple_of((first + pair) * C, C), C)], buf.at[slot],
            sems.at[i, slot]) for i, (src, first, buf) in enumerate(cols)]

    @pl.when(p == 0)
    def _():
        for cp in copies(0, 0):
            cp.start()

    for cp in copies(p, p % 2):
        cp.wait()

    @pl.when(p + 1 < NPAIR)
    def _():
        for cp in copies(p + 1, 1 - p % 2):
            cp.start()


def _rows(block, size=C):
    return pl.ds(pl.multiple_of(block * size, size), size)


def _fox_fwd(z, ct, cst, share):
    n = 0 if share is None else 1

    def body(z_ref, ct_ref, ones_ref, mb_ref, *rest):
        share_refs, (a_ref, g_ref), land_refs = rest[:n], rest[n:n + 2], rest[n + 2:2 * n + 2]
        kks, vvs, q2, m2, sbuf, qbuf, kbuf, vbuf, col_sems = rest[2 * n + 2:2 * n + 11]
        p, s = pl.program_id(0), pl.program_id(1)
        slot = p % 2
        if n:
            send_sems, recv_sems, own_sem = rest[2 * n + 11:]
            x, y, _ = _place()
            copies = _chip_copies(share_refs, land_refs, send_sems, recv_sems, by_dest=False)
            copies.append(pltpu.make_async_copy(share_refs[0], land_refs[0].at[2 * x + y], own_sem.at[0]))

            @pl.when((p == 0) & (s == 0))
            def _():
                for cp in copies:
                    cp.start()

            @pl.when((p == NPAIR - 1) & (s == NSTEP - 1))
            def _():
                for cp in copies:
                    cp.wait()

        @pl.when(s == 0)
        def _():
            ones = ones_ref[...]
            _fox_columns([(z_ref, QB_F, qbuf), (z_ref, KB_F, kbuf), (z_ref, VB_F, vbuf)], col_sems, p)

            def prep(j, carry):
                kks[j] = _split_heads(kbuf[slot, _rows(j), :]).astype(BF)
                vvs[j] = jnp.concatenate([_split_heads(vbuf[slot, _rows(j), :]).astype(BF), ones], axis=1)
                return carry

            lax.fori_loop(0, NCH, prep, 0)

        q2[0] = (qbuf[slot, _rows(s), :] * FSCALE).astype(BF)
        q2[1] = (qbuf[slot, _rows(NCH - 1 - s), :] * FSCALE).astype(BF)

        tiles = [_fox_tile(s, t) for t in range(NTILE)]
        causal = mb_ref[1]
        neg = jnp.full((C, 2 * C), NEG, F32)
        run, first = neg, neg
        for t, (sel, j) in enumerate(tiles):
            st = _dg(q2[sel], kks[j], NT) - _fox_key_bias(ct_ref, p, j)
            if t in (0, NTILE - 1):
                st = st + causal
            sbuf[t] = st
            run = jnp.maximum(jnp.where(t == s + 1, neg, run), st)
            first = jnp.where(t == s, run, first)
        for w, mx in enumerate((first, run)):
            m2[w] = jnp.concatenate(
                [jnp.broadcast_to(jnp.max(mx[:, :C], axis=1, keepdims=True), (C, C)),
                 jnp.broadcast_to(jnp.max(mx[:, C:], axis=1, keepdims=True), (C, C))], axis=1)

        zero = jnp.zeros((C, 2 * C), F32)
        run, first = zero, zero
        for t, (sel, j) in enumerate(tiles):
            run = jnp.where(t == s + 1, zero, run) + _dot(jnp.exp(sbuf[t] - m2[sel]).astype(BF), vvs[j])
            first = jnp.where(t == s, run, first)
        lo = _lo_lanes((C, C))
        for w, res in enumerate((first, run)):
            l = res[:, C:]
            a_ref[_rows(2 * s + w), :] = res[:, :C] / l
            mw = m2[w]
            g_ref[_rows(2 * s + w), :] = -(jnp.where(lo, mw[:, :C], mw[:, C:]) + jnp.log(l))

    col = _fox_pair_columns()
    return pl.pallas_call(
        body, name="fox_fwd", grid=(NPAIR, NSTEP),
        in_specs=[ANY,
                  pl.BlockSpec((NCH, FH, C), lambda p, s: (0, 0, 0)),
                  pl.BlockSpec((2 * C, C), lambda p, s: (0, 0)),
                  pl.BlockSpec((2, C, 2 * C), lambda p, s: (0, 0, 0))] + [ANY] * n,
        out_specs=[col, col] + [ANY] * n,
        out_shape=[jax.ShapeDtypeStruct((TROWS, FH * FD), F32)] * 2
        + ([jax.ShapeDtypeStruct((4,) + share.shape, share.dtype)] if n else []),
        scratch_shapes=[pltpu.VMEM((NCH, 2 * C, C), BF), pltpu.VMEM((NCH, 2 * C, 2 * C), BF),
                        pltpu.VMEM((2, C, C), BF), pltpu.VMEM((2, C, 2 * C), F32),
                        pltpu.VMEM((NTILE, C, 2 * C), F32),
                        pltpu.VMEM((2, T, C), F32), pltpu.VMEM((2, T, C), F32), pltpu.VMEM((2, T, C), F32),
                        pltpu.SemaphoreType.DMA((3, 2))]
        + [pltpu.SemaphoreType.DMA((3,)), pltpu.SemaphoreType.DMA((3,)), pltpu.SemaphoreType.DMA((1,))] * n,
        compiler_params=_params(("arbitrary", "arbitrary")),
    )(z, ct, cst["ones_aug"], cst["mask_bias"], *([share] * n))


def _fox_bwd(z, da, g, delta, ct, cst, parts=()):
    grp = 9

    n = len(parts)

    def body(z_ref, da_ref, g_ref, dl_ref, ct_ref, ones_ref, mb_ref, *rest):
        part_refs, (dq_ref, dr_ref, dk_ref, dv_ref, dcs_ref), land_refs = rest[:n], rest[n:n + 5], rest[n + 5:2 * n + 5]
        (kks, vvs, q2, qq2, dd2, da2, gi2, dl2, dq2, dkacc, dvacc, csacc, qbuf, kbuf, vbuf, dabuf, gbuf,
         dlbuf, col_sems) = rest[2 * n + 5:2 * n + 24]
        p, s = pl.program_id(0), pl.program_id(1)
        slot = p % 2
        ones = ones_ref[...]
        if n:
            copies = _chip_copies(part_refs, land_refs, *rest[2 * n + 24:], by_dest=True)

            @pl.when((p == 0) & (s == 0))
            def _():
                for cp in copies:
                    cp.start()

            @pl.when((p == NPAIR - 1) & (s == NSTEP - 1))
            def _():
                for cp in copies:
                    cp.wait()

        @pl.when(s == 0)
        def _():
            dkacc[...] = jnp.zeros_like(dkacc)
            dvacc[...] = jnp.zeros_like(dvacc)
            csacc[...] = jnp.zeros_like(csacc)
            _fox_columns([(z_ref, QB_F, qbuf), (z_ref, KB_F, kbuf), (z_ref, VB_F, vbuf), (da_ref, 0, dabuf),
                          (g_ref, 0, gbuf), (dl_ref, 0, dlbuf)], col_sems, p)

            def prep(j, carry):
                kks[j] = _split_heads(kbuf[slot, _rows(j), :]).astype(BF)
                vvs[j] = _split_heads(vbuf[slot, _rows(j), :]).astype(BF)
                return carry

            lax.fori_loop(0, NCH, prep, 0)

        for w, (chunk, blk) in enumerate(((s, 2 * s), (NCH - 1 - s, jnp.where(s == NSTEP - 1, 2 * s, 2 * s + 1)))):
            qf = qbuf[slot, _rows(chunk), :]
            q2[w] = (qf * FSCALE).astype(BF)
            qq2[w] = jnp.concatenate([_split_heads(qf).astype(BF), ones], axis=1)
            da2[w] = dabuf[slot, _rows(blk), :]
            dd2[w] = _split_heads(da2[w].astype(F32)).astype(BF)
            gi2[w] = _spread2(gbuf[slot, _rows(blk), :])
            dl2[w] = _spread2(dlbuf[slot, _rows(blk), :])
        dq2[...] = jnp.zeros_like(dq2)
        zero = jnp.zeros((C, 2 * C), F32)

        def group(gi, carry):
            ts = [gi * grp + u for u in range(grp)]
            tiles = [_fox_tile(s, t) for t in ts]
            kk = [kks[j] for _, j in tiles]
            ss = [_dg(q2[sel], kj, NT) + (gi2[sel] - _fox_key_bias(ct_ref, p, j)) for kj, (sel, j) in zip(kk, tiles)]
            ss[0] = ss[0] + mb_ref[(gi == 0).astype(jnp.int32)]
            ss[-1] = ss[-1] + mb_ref[(gi == 1).astype(jnp.int32)]
            dps = [_dg(da2[sel], vvs[j], NT) for sel, j in tiles]
            pes = [jnp.exp(st) for st in ss]
            dss = [pe * (dp - dl2[sel]) * FSCALE for pe, dp, (sel, _) in zip(pes, dps, tiles)]
            pts = [jnp.concatenate([pe[:, :C].T, pe[:, C:].T], axis=1).astype(BF) for pe in pes]
            dsts = [jnp.concatenate([ds[:, :C].T, ds[:, C:].T], axis=1).astype(BF) for ds in dss]
            dvs = [_dot(pt, dd2[sel]) for pt, (sel, _) in zip(pts, tiles)]
            rs = [_dot(dst, qq2[sel]) for dst, (sel, _) in zip(dsts, tiles)]
            parts = [_dot(ds.astype(BF), jnp.concatenate([kj, ones], axis=1)) for ds, kj in zip(dss, kk)]
            for (_, j), dv, rr in zip(tiles, dvs, rs):
                dvacc[_rows(j), :] += dv
                dkacc[_rows(j), :] += rr[:, :C]
                csacc[_rows(j), :] += rr[:, C:]
            pa, pb = zero, zero
            for t, part in zip(ts, parts):
                pa = pa + jnp.where(t <= s, part, zero)
                pb = pb + jnp.where(t <= s, zero, part)
            dq2[0] += pa
            dq2[1] += pb
            return carry

        ntile = jnp.where(s == NSTEP - 1, grp, NTILE)
        lax.fori_loop(0, ntile // grp, group, 0)
        for w, chunk in ((1, NCH - 1 - s), (0, s)):
            res = dq2[w]
            dq_ref[_rows(chunk), :] = res[:, :C].astype(BF)
            dr_ref[_rows(2 * s + w), :] = res[:, C:]

        @pl.when(s == NSTEP - 1)
        def _():
            dk_ref[...] = dkacc[...].astype(BF)
            dv_ref[...] = dvacc[...].astype(BF)
            dcs_ref[...] = csacc[...]

    both = _fox_pair_columns()
    col = pl.BlockSpec((T, C), lambda p, s: (0, p))
    return pl.pallas_call(
        body, name="fox_bwd", grid=(NPAIR, NSTEP),
        in_specs=[ANY] * 4
        + [pl.BlockSpec((NCH, FH, C), lambda p, s: (0, 0, 0)),
           pl.BlockSpec((2 * C, C), lambda p, s: (0, 0)),
           pl.BlockSpec((2, C, 2 * C), lambda p, s: (0, 0, 0))] + [ANY] * n,
        out_specs=[col, both, col, col, col] + [ANY] * n,
        out_shape=[jax.ShapeDtypeStruct((T, FH * FD), BF), jax.ShapeDtypeStruct((TROWS, FH * FD), F32),
                   jax.ShapeDtypeStruct((T, FH * FD), BF), jax.ShapeDtypeStruct((T, FH * FD), BF),
                   jax.ShapeDtypeStruct((T, FH * FD), F32)]
        + [jax.ShapeDtypeStruct(p.shape, p.dtype) for p in parts],
        scratch_shapes=[pltpu.VMEM((NCH, 2 * C, C), BF), pltpu.VMEM((NCH, 2 * C, C), BF),
                        pltpu.VMEM((2, C, C), BF), pltpu.VMEM((2, 2 * C, 2 * C), BF), pltpu.VMEM((2, 2 * C, C), BF),
                        pltpu.VMEM((2, C, C), BF), pltpu.VMEM((2, C, 2 * C), F32), pltpu.VMEM((2, C, 2 * C), F32),
                        pltpu.VMEM((2, C, 2 * C), F32),
                        pltpu.VMEM((T, C), F32), pltpu.VMEM((T, C), F32), pltpu.VMEM((T, C), F32),
                        pltpu.VMEM((2, T, C), F32), pltpu.VMEM((2, T, C), F32), pltpu.VMEM((2, T, C), F32),
                        pltpu.VMEM((2, T, C), BF), pltpu.VMEM((2, T, C), F32), pltpu.VMEM((2, T, C), F32),
                        pltpu.SemaphoreType.DMA((6, 2))]
        + ([pltpu.SemaphoreType.DMA((3 * n,)), pltpu.SemaphoreType.DMA((3 * n,))] if n else []),
        compiler_params=_params(("arbitrary", "arbitrary")),
    )(z, da, g, delta, ct, cst["ones_aug"], cst["mask_bias"], *parts)


def _fox_gate_bwd(drow, dcol, zf, bf_pad, cst):
    def body(dr_ref, dc_ref, zf_ref, b_ref, tri_ref, pick_ref, dff_ref, db_ref, carry):
        s = pl.program_id(0)
        n = NCH - 1 - s

        @pl.when(s == 0)
        def _():
            carry[...] = jnp.zeros_like(carry)
            db_ref[...] = jnp.zeros_like(db_ref)

        dcb = _split_dot((dr_ref[...] - dc_ref[...]) * (1.0 / FSCALE), pick_ref[...])
        suf = _split_dot(dcb, tri_ref[...], TN, x_first=False) + carry[0:1, :]
        carry[...] = jnp.broadcast_to(suf[0:1, :], carry.shape)
        x = zf_ref[...] + b_ref[...]
        row = n * C + lax.broadcasted_iota(jnp.int32, (C, C), 0)
        dff = jnp.where(row >= PAD, suf * (1.0 - jax.nn.sigmoid(x)), 0.0)
        dff_ref[...] = dff.astype(BF)
        db_ref[...] += jnp.sum(dff, axis=0, keepdims=True)

    rev = lambda s: (NCH - 1 - s, 0)
    return pl.pallas_call(
        body, name="fox_gate_bwd", grid=(NCH,),
        in_specs=[pl.BlockSpec((C, FH * FD), lambda s: (_fox_pos(NCH - 1 - s), 0)),
                  pl.BlockSpec((C, FH * FD), rev), pl.BlockSpec((C, C), rev),
                  pl.BlockSpec((1, C), lambda s: (0, 0)), pl.BlockSpec((C, C), lambda s: (0, 0)),
                  pl.BlockSpec((FH * FD, C), lambda s: (0, 0))],
        out_specs=[pl.BlockSpec((C, C), rev), pl.BlockSpec((1, C), lambda s: (0, 0))],
        out_shape=[jax.ShapeDtypeStruct((T, C), BF), jax.ShapeDtypeStruct((1, C), F32)],
        scratch_shapes=[pltpu.VMEM((8, C), F32)],
        compiler_params=_params(("arbitrary",)),
    )(drow, dcol, zf, bf_pad, cst["tri"], cst["pick"])


def _head_norm(r):
    rn, rs = [], []
    for h in range(RH):
        rh = r[:, RDV * h:RDV * (h + 1)]
        s = lax.rsqrt(jnp.mean(rh * rh, axis=1, keepdims=True) + EPS)
        rn.append(rh * s)
        rs.append(s)
    return jnp.concatenate(rn, axis=1), rs


def _gated(r, rg, a, fg):
    rn, _ = _head_norm(r)
    return jnp.concatenate([rn * (rg * jax.nn.sigmoid(rg)), a * (fg * jax.nn.sigmoid(fg))], axis=1)


def _out_loss(r, z, a, wout, x, tgt, fgain):
    def body(r_ref, rg_ref, a_ref, fg_ref, w_ref, x_ref, t_ref, g_ref, yt_ref, do_ref, dob_ref, loss_ref, dg_ref):
        i = pl.program_id(0)

        @pl.when(i == 0)
        def _():
            yt_ref[...] = jnp.zeros_like(yt_ref)
            do_ref[...] = jnp.zeros_like(do_ref)
            dob_ref[...] = jnp.zeros_like(dob_ref)
            loss_ref[...] = jnp.zeros_like(loss_ref)
            dg_ref[...] = jnp.zeros_like(dg_ref)

        @pl.when(i > 0)
        def _():
            y = _gated(r_ref[...], rg_ref[...], a_ref[...], fg_ref[...])
            yt_ref[...] = y.T.astype(BF)
            o = x_ref[...] + _dot(y.astype(BF), w_ref[...])
            rs = lax.rsqrt(jnp.mean(o * o, axis=1, keepdims=True) + EPS)
            on = o * rs
            g = g_ref[...]
            e = on * g - t_ref[...]
            loss_ref[...] += 0.5 * jnp.sum(jnp.mean(e * e, axis=1, keepdims=True))
            dyh = e * (1.0 / D)
            dg_ref[...] += jnp.sum(dyh * on, axis=0, keepdims=True)
            don = dyh * g
            do = rs * (don - on * jnp.mean(don * on, axis=1, keepdims=True))
            do_ref[...] = do
            dob_ref[...] = do.astype(BF)

    tok = lambda i: (jnp.maximum(i - 1, 0), 0)
    return pl.pallas_call(
        body, name="out_loss", grid=(NCH,),
        in_specs=[pl.BlockSpec((C, D), lambda i: (i, 0)), pl.BlockSpec((C, D), lambda i: (i, GB_R)),
                  pl.BlockSpec((C, D), lambda i: (_fox_pos(i), 0)), pl.BlockSpec((C, D), lambda i: (i, GB_F)),
                  pl.BlockSpec((DMIX, D), lambda i: (0, 0)),
                  pl.BlockSpec((C, D), tok), pl.BlockSpec((C, D), tok), pl.BlockSpec((1, D), lambda i: (0, 0))],
        out_specs=[pl.BlockSpec((DMIX, C), lambda i: (0, i)), pl.BlockSpec((C, D), lambda i: (i, 0)),
                   pl.BlockSpec((C, D), lambda i: (i, 0)), pl.BlockSpec((8, C), lambda i: (0, 0)),
                   pl.BlockSpec((1, D), lambda i: (0, 0))],
        out_shape=[jax.ShapeDtypeStruct((DMIX, T), BF), jax.ShapeDtypeStruct((T, D), F32),
                   jax.ShapeDtypeStruct((T, D), BF), jax.ShapeDtypeStruct((8, C), F32),
                   jax.ShapeDtypeStruct((1, D), F32)],
        compiler_params=_params(("arbitrary",)),
    )(r, z, a, z, wout, x, tgt, fgain)


def _silu_and_grad(x):
    s = jax.nn.sigmoid(x)
    return x * s, s * (1.0 + x * (1.0 - s))


def _dy_gate_bwd(dob, wout, r, z, a, seg, swap=()):
    n = len(swap)

    def body(do_ref, w_ref, r_ref, rg_ref, a_ref, fg_ref, seg_ref, *rest):
        (dr_ref, da_ref, drg_ref, dfg_ref, dl_ref) = rest[n:n + 5]
        if n:
            copies = _pair_copies(rest[:n], rest[n + 5:2 * n + 5], *rest[2 * n + 5:], n)

            @pl.when(pl.program_id(0) == 0)
            def _():
                for cp in copies:
                    cp.start()

            @pl.when(pl.program_id(0) == NCH - 1)
            def _():
                for cp in copies:
                    cp.wait()

        dy = _dg(do_ref[...], w_ref[...], NT)
        a_ = a_ref[...]
        rn, rs = _head_norm(r_ref[...])
        silu_rg, dsilu_rg = _silu_and_grad(rg_ref[...])
        silu_fg, dsilu_fg = _silu_and_grad(fg_ref[...])
        dyr, dyf = dy[:, :D], dy[:, D:]
        drn = dyr * silu_rg
        drg_ref[...] = (dyr * rn * dsilu_rg).astype(BF)
        for h in range(RH):
            sl = slice(RDV * h, RDV * (h + 1))
            dh, nh = drn[:, sl], rn[:, sl]
            dr_ref[:, sl] = (rs[h] * (dh - nh * jnp.mean(dh * nh, axis=1, keepdims=True))).astype(BF)
        dab = (dyf * silu_fg).astype(BF)
        da_ref[...] = dab
        dfg_ref[...] = (dyf * a_ * dsilu_fg).astype(BF)
        prod = dab.astype(F32) * a_
        segm = seg_ref[...]
        for p in range(NPAIR):
            sl = slice(C * p, C * (p + 1))
            hi = prod[:, sl].astype(BF)
            lo = (prod[:, sl] - hi.astype(F32)).astype(BF)
            dl_ref[:, sl] = _dot(hi, segm) + _dot(lo, segm)

    row = pl.BlockSpec((C, D), lambda i: (i, 0))
    fox = pl.BlockSpec((C, D), lambda i: (_fox_pos(i), 0))
    return pl.pallas_call(
        body, name="dy_gate_bwd", grid=(NCH,),
        in_specs=[row, pl.BlockSpec((DMIX, D), lambda i: (0, 0)),
                  row, pl.BlockSpec((C, D), lambda i: (i, GB_R)),
                  fox, pl.BlockSpec((C, D), lambda i: (i, GB_F)),
                  pl.BlockSpec((C, C), lambda i: (0, 0))] + [ANY] * n,
        out_specs=[row, fox, row, row, fox] + [ANY] * n,
        out_shape=[jax.ShapeDtypeStruct((T, D), BF), jax.ShapeDtypeStruct((TROWS, D), BF),
                   jax.ShapeDtypeStruct((T, D), BF), jax.ShapeDtypeStruct((T, D), BF),
                   jax.ShapeDtypeStruct((TROWS, D), F32)]
        + [jax.ShapeDtypeStruct((4, s.shape[1] // 2, s.shape[2]), s.dtype) for s in swap],
        scratch_shapes=[pltpu.SemaphoreType.DMA((n,)), pltpu.SemaphoreType.DMA((n,))] if n else [],
        compiler_params=_params(("arbitrary",)),
    )(dob, wout, r, z, a, z, seg, *swap)


DZ_WIDTHS = (512, 512, 1024, 1024, 1024, 1024, 1024, 1024)


def _du_norm_bwd(dzs, dzf, wt, wft, hpad, g, dopad, parts=()):
    tm, tk = 544, 1024
    nk = WMAIN // tk
    ni = T // tm
    n = len(parts)

    def body(rq_ref, rk_ref, rv_ref, rg_ref, fq_ref, fk_ref, fv_ref, fg_ref, dzf_ref, w_ref, wf_ref, h_ref, g_ref,
             do_ref, *rest):
        part_refs, (gh_ref, dg_ref), land_refs = rest[:n], rest[n:n + 2], rest[n + 2:2 * n + 2]
        acc = rest[2 * n + 2]
        i, k = pl.program_id(0), pl.program_id(1)

        if n:
            send_sems, recv_sems = rest[2 * n + 3:]
            copies = _chip_copies(part_refs, land_refs, send_sems, recv_sems, by_dest=True)

            @pl.when((i == 0) & (k == 0))
            def _():
                for cp in copies:
                    cp.start()

            @pl.when((i == ni - 1) & (k == nk - 1))
            def _():
                for cp in copies:
                    cp.wait()

        @pl.when(k == 0)
        def _():
            acc[...] = (_dot(dzf_ref[...], wf_ref[...]) + _dot(rq_ref[...], w_ref[:512, :])
                        + _dot(rk_ref[...], w_ref[512:, :]))

        for kk, piece in enumerate((rv_ref, rg_ref, fq_ref, fk_ref, fv_ref, fg_ref), start=1):
            @pl.when(k == kk)
            def _(piece=piece):
                acc[...] += _dot(piece[...], w_ref[...])

        @pl.when(k == nk - 1)
        def _():
            du = acc[...]
            h = h_ref[...]
            gg = g_ref[...]
            rs = lax.rsqrt(jnp.mean(h * h, axis=1, keepdims=True) + EPS)
            hn = h * rs
            part = jnp.sum(du * hn, axis=0, keepdims=True)

            @pl.when(i == 0)
            def _():
                dg_ref[...] = part

            @pl.when(i > 0)
            def _():
                dg_ref[...] += part

            dhn = du * gg
            gh_ref[...] = rs * (dhn - hn * jnp.mean(dhn * hn, axis=1, keepdims=True)) + do_ref[...]

    sems = [pltpu.SemaphoreType.DMA((3 * n,)), pltpu.SemaphoreType.DMA((3 * n,))] if n else []
    return pl.pallas_call(
        body, name="du_norm_bwd", grid=(ni, nk),
        in_specs=[pl.BlockSpec((tm, w), lambda i, k: (i, 0)) for w in DZ_WIDTHS]
        + [pl.BlockSpec((tm, C), lambda i, k: (i, 0)),
           pl.BlockSpec((tk, D), lambda i, k: (k, 0)), pl.BlockSpec((C, D), lambda i, k: (0, 0)),
           pl.BlockSpec((tm, D), lambda i, k: (i, 0)), pl.BlockSpec((1, D), lambda i, k: (0, 0)),
           pl.BlockSpec((tm, D), lambda i, k: (i, 0))] + [ANY] * n,
        out_specs=[pl.BlockSpec((tm, D), lambda i, k: (i, 0)), pl.BlockSpec((1, D), lambda i, k: (0, 0))] + [ANY] * n,
        out_shape=[jax.ShapeDtypeStruct((T, D), F32), jax.ShapeDtypeStruct((1, D), F32)]
        + [jax.ShapeDtypeStruct(p.shape, p.dtype) for p in parts],
        scratch_shapes=[pltpu.VMEM((tm, D), F32)] + sems,
        compiler_params=_params(("arbitrary", "arbitrary")),
    )(*dzs, dzf, wt, wft, hpad, g, dopad, *parts)


GROWS = 7680


def _dw_in(dzs, dzf, ut):
    tn = 512
    nmain = WMAIN // tn
    first, blocks = [], []
    for w in DZ_WIDTHS:
        first.append(sum(blocks))
        blocks.append(w // tn)

    def body(rq_ref, rk_ref, rv_ref, rg_ref, fq_ref, fk_ref, fv_ref, fg_ref, dzf_ref, ut_ref, o_ref):
        gidx = pl.program_id(0)
        for piece, g0, nb in zip((rq_ref, rk_ref, rv_ref, rg_ref, fq_ref, fk_ref, fv_ref, fg_ref), first, blocks):
            @pl.when((gidx >= g0) & (gidx < g0 + nb))
            def _(piece=piece):
                o_ref[...] = _dot(ut_ref[...], piece[...]).T.astype(BF)

        @pl.when(gidx == nmain)
        def _():
            o_ref[:C, :] = _dot(ut_ref[...], dzf_ref[...]).T.astype(BF)
            o_ref[C:, :] = jnp.zeros((tn - C, D), BF)

    def piece_spec(g0, nb):
        return pl.BlockSpec((T, tn), lambda gidx: (0, jnp.clip(gidx - g0, 0, nb - 1)))

    return pl.pallas_call(
        body, name="dw_in", grid=(nmain + 1,),
        in_specs=[piece_spec(g0, nb) for g0, nb in zip(first, blocks)]
        + [pl.BlockSpec((T, C), lambda gidx: (0, 0)), pl.BlockSpec((D, T), lambda gidx: (0, 0))],
        out_specs=pl.BlockSpec((tn, D), lambda gidx: (gidx, 0)),
        out_shape=jax.ShapeDtypeStruct((GROWS, D), BF),
        compiler_params=pltpu.CompilerParams(dimension_semantics=("arbitrary",), vmem_limit_bytes=DW_VMEM_LIMIT),
    )(*dzs, dzf, ut)


def _local_step(x, tgt, normed, norm_g, wt, wft, b_f, wout, final_g, reduce_scatter=False, gather_wout=False):
    cst = _constants()
    hpad, u, ut = normed
    bf_pad = jnp.pad(b_f, ((0, 0), (0, C - NFF)))
    z = _mm_nt(u, wt, WMAIN, T, 1024, "in_proj")
    zf = _mm_nt(u, wft, C, T // 2, C, "in_proj_ff")
    r, sprev = _ret_fwd(z, cst)
    ct = _fox_prep(zf, bf_pad, cst)
    if not gather_wout:
        a, g = _fox_fwd(z, ct, cst, None)
    else:
        a, g, landed_wout = _fox_fwd(z, ct, cst, wout)
        wout = landed_wout.reshape(DMIX, D)
    yt, dopad, dob, loss8, dfg = _out_loss(r, z, a, wout, x, tgt, final_g)
    dwout = _mm_nn(yt, dob, 512, D, "dw_out", BF)
    g_out = [dwout.reshape(4, DMIX // 4, D)] if reduce_scatter else []
    dr, da, dzrg, dzfg, delta, *r_out = _dy_gate_bwd(dob, wout, r, z, a, cst["seg"], g_out)
    p_out = [_add_halves(g_out[0], r_out[0], "pair_add_out", BF)] if reduce_scatter else []
    dzq_r, dzk_r, dzv_r = _ret_bwd(z, cst, sprev, dr)
    dzq_f, drow, dzk_f, dzv_f, dcol, *e_out = _fox_bwd(z, da, g, delta, ct, cst, p_out)
    dzf, dbf = _fox_gate_bwd(drow, dcol, zf, bf_pad, cst)
    dzs = [dzq_r, dzk_r, dzv_r, dzrg, dzq_f, dzk_f, dzv_f, dzfg]
    gwt = _dw_in(dzs, dzf, ut)
    p_in = [_swap_add_windows(gwt)[1]] if reduce_scatter else []
    gh, dng, *e_in = _du_norm_bwd(dzs, dzf, wt, wft, hpad, norm_g, dopad, p_in)
    return (loss8[0, 0], gh[C:], gh[PAD:C], dng, gwt, dbf[:, :NFF], dwout, dfg, p_in + p_out, e_in + e_out)


WOFF, WLEN = 1792, 2048
WHALF = WLEN // 2
LAP = WPADROWS - WOFF


def _own_window(w3):
    rows, sub, lanes = w3.shape
    pad = WPADROWS - rows
    tb = 96
    nb = WPADROWS // tb
    half = rows // 2

    def body(w_ref, o_ref, buf, sems):
        x, y, _ = _place()
        shift = 4 * (2 * x + y)
        buf[pl.ds(0, pad)] = jnp.zeros((pad, sub, lanes), F32)
        buf[pl.ds(rows, pad)] = jnp.zeros((pad, sub, lanes), F32)
        cps = [pltpu.make_async_copy(w_ref.at[pl.ds(half * h, half)], buf.at[pl.ds(shift + half * h, half)],
                                     sems.at[h]) for h in range(2)]
        for cp in cps:
            cp.start()

        def block(i, carry):
            r0 = pl.multiple_of(i * tb, tb)
            o_ref[pl.ds(r0, tb), :] = buf[pl.ds(r0, tb)].reshape(tb, sub * lanes).astype(BF)
            return carry

        cps[0].wait()
        lax.fori_loop(0, half // tb, block, 0)
        cps[1].wait()
        lax.fori_loop(half // tb, nb, block, 0)

    return pl.pallas_call(
        body, name="own_window",
        in_specs=[ANY], out_shape=jax.ShapeDtypeStruct((WPADROWS, sub * lanes), BF),
        scratch_shapes=[pltpu.VMEM((WPADROWS, sub, lanes), F32), pltpu.SemaphoreType.DMA((2,))],
        compiler_params=pltpu.CompilerParams(vmem_limit_bytes=VMEM_LIMIT),
    )(w3)


def _gather_weights(own_win, meta, x, norm_g):
    half_main, half_lap, half_meta = WOFF // 2, LAP // 2, meta.shape[0] // 2
    last = NCH - 1

    def body(win_ref, meta_ref, x_ref, g_ref, w_ref, laps_ref, gm_ref, h_ref, u_ref, ut_ref,
             send_sems, recv_sems, local_sems, stage, lapbuf, headbuf, metabuf):
        step = pl.program_id(0)
        x, y, c = _place()
        me_s = 2 * x + y
        sib = (x, y, 1 - c)
        chips = _other_chips(x, y)

        def emit(h):
            u = _norm_rows(h, g_ref[...])
            h_ref[...] = h
            u_ref[...] = u.astype(BF)
            ut_ref[...] = u.T.astype(BF)

        kinds = [
            (lambda h: win_ref.at[pl.ds(half_main * h, half_main)],
             lambda s, h: w_ref.at[pl.ds(WOFF * s + half_main * h, half_main)]),
            (lambda h: win_ref.at[pl.ds(WOFF + half_lap * h, half_lap)],
             lambda s, h: laps_ref.at[s, pl.ds(half_lap * h, half_lap)]),
            (lambda h: meta_ref.at[pl.ds(half_meta * h, half_meta)],
             lambda s, h: gm_ref.at[s, pl.ds(half_meta * h, half_meta)]),
        ]
        own_in = pltpu.make_async_copy(win_ref.at[pl.ds(0, WOFF)], stage, local_sems.at[0])
        own_lap_in = pltpu.make_async_copy(win_ref.at[pl.ds(WOFF, LAP)], lapbuf.at[0], local_sems.at[1])
        own_out = pltpu.make_async_copy(stage, w_ref.at[pl.ds(WOFF * me_s, WOFF)], local_sems.at[0])
        own_lap_out = pltpu.make_async_copy(lapbuf.at[0], laps_ref.at[me_s], local_sems.at[1])
        sends, arrivals, forwards, forwarded = [], [], [], []
        for a, (src, dst) in enumerate(kinds):
            for k, (cx, cy, cs) in enumerate(chips):
                there = dict(send_sem=send_sems.at[6 * a + k], recv_sem=recv_sems.at[6 * a + k],
                             device_id=(cx, cy, c), device_id_type=MESH)
                across = dict(send_sem=send_sems.at[6 * a + 3 + k], recv_sem=recv_sems.at[6 * a + 3 + k],
                              device_id=sib, device_id_type=MESH)
                sends.append(pltpu.make_async_remote_copy(src_ref=src(c), dst_ref=dst(me_s, c), **there))
                arrivals.append(pltpu.make_async_remote_copy(src_ref=dst(cs, c), dst_ref=dst(cs, c), **there))
                forwards.append(pltpu.make_async_remote_copy(src_ref=dst(cs, c), dst_ref=dst(cs, c), **across))
                forwarded.append(pltpu.make_async_remote_copy(
                    src_ref=dst(cs, 1 - c), dst_ref=dst(cs, 1 - c), **across))

        @pl.when(step == 0)
        def _():
            own_in.start()
            own_lap_in.start()
            for cp in sends:
                cp.start()
            own_in.wait()
            own_out.start()
            own_lap_in.wait()
            own_lap_out.start()

        @pl.when(step < last)
        def _():
            emit(x_ref[...])

        @pl.when(step == last)
        def _():
            for cp, fwd in zip(arrivals, forwards):
                cp.wait_recv()
                fwd.start()
            for cp in forwarded:
                cp.wait_recv()
            for cp in sends + forwards:
                cp.wait_send()
            own_out.wait()
            own_lap_out.wait()
            heads = [w_ref.at[pl.ds(WOFF * s, LAP)] for s in range(1, 4)]
            lap_loads = [pltpu.make_async_copy(laps_ref.at[i], lapbuf.at[1 + i], local_sems.at[4 + 2 * i])
                         for i in range(3)]
            head_loads = [pltpu.make_async_copy(heads[i], headbuf.at[i], local_sems.at[5 + 2 * i]) for i in range(3)]
            head_stores = [pltpu.make_async_copy(headbuf.at[i], heads[i], local_sems.at[5 + 2 * i]) for i in range(3)]
            loads = [pltpu.make_async_copy(meta_ref, metabuf.at[me_s], local_sems.at[0])]
            loads += [pltpu.make_async_copy(gm_ref.at[cs], metabuf.at[cs], local_sems.at[1 + k])
                      for k, (_, _, cs) in enumerate(chips)]
            for cp in lap_loads + head_loads + loads:
                cp.start()
            for i in range(3):
                lap_loads[i].wait()
                head_loads[i].wait()
                headbuf[i] = (headbuf[i].astype(F32) + lapbuf[1 + i].astype(F32)).astype(BF)
                head_stores[i].start()
            for cp in loads:
                cp.wait()
            tokens = jnp.concatenate([metabuf[s] for s in range(4)], axis=1)
            emit(jnp.concatenate([jnp.zeros((PAD, D), F32), tokens], axis=0))
            for cp in head_stores:
                cp.wait()

    def chunk(i):
        return (i + 1) % NCH

    return pl.pallas_call(
        body, name="all_gather_w", grid=(NCH,),
        in_specs=[ANY, ANY, pl.BlockSpec((C, D), lambda i: (jnp.minimum(i, last - 1), 0)),
                  pl.BlockSpec((1, D), lambda i: (0, 0))],
        out_specs=[ANY] * 3 + [pl.BlockSpec((C, D), lambda i: (chunk(i), 0))] * 2
        + [pl.BlockSpec((D, C), lambda i: (0, chunk(i)))],
        out_shape=[jax.ShapeDtypeStruct((WMAIN, D), own_win.dtype), jax.ShapeDtypeStruct((4, LAP, D), own_win.dtype),
                   jax.ShapeDtypeStruct((4,) + meta.shape, meta.dtype),
                   jax.ShapeDtypeStruct((T, D), F32), jax.ShapeDtypeStruct((T, D), BF),
                   jax.ShapeDtypeStruct((D, T), BF)],
        scratch_shapes=[pltpu.SemaphoreType.DMA((18,)), pltpu.SemaphoreType.DMA((18,)), pltpu.SemaphoreType.DMA((10,)),
                        pltpu.VMEM((WOFF, D), own_win.dtype), pltpu.VMEM((4, LAP, D), own_win.dtype),
                        pltpu.VMEM((3, LAP, D), own_win.dtype), pltpu.VMEM((4,) + meta.shape, meta.dtype)],
        compiler_params=_params(("arbitrary",)),
    )(own_win, meta, x, norm_g)


def _pair_copies(ins, outs, send_sems, recv_sems, n):
    x, y, c = _place()
    sib = dict(device_id=(x, y, 1 - c), device_id_type=MESH)
    cps = []
    for a in range(n):
        rows = ins[a].shape[1] // 2
        cps.append(pltpu.make_async_remote_copy(
            src_ref=ins[a].at[:, pl.ds((1 - c) * rows, rows)], dst_ref=outs[a],
            send_sem=send_sems.at[a], recv_sem=recv_sems.at[a], **sib))
    for k in range(4 * (len(ins) - n)):
        cps.append(pltpu.make_async_remote_copy(
            src_ref=ins[n].at[pl.ds(WOFF * k + (1 - c) * WHALF, WHALF)], dst_ref=outs[n].at[k],
            send_sem=send_sems.at[n + k], recv_sem=recv_sems.at[n + k], **sib))
    return cps


def _swap_add_windows(gwt):
    nchunk = 4
    rows = WHALF // nchunk

    def body(gw_ref, land_ref, out_ref, send_sems, recv_sems, local_sems, own, theirs):
        _, _, c = _place()
        swaps = _pair_copies([gw_ref], [land_ref], send_sems, recv_sems, 0)
        loads = [pltpu.make_async_copy(gw_ref.at[pl.ds(WOFF * k + c * WHALF, WHALF)], own.at[k], local_sems.at[k])
                 for k in range(4)]
        stores = [pltpu.make_async_copy(own.at[k], out_ref.at[k], local_sems.at[k]) for k in range(4)]
        for cp in loads + swaps:
            cp.start()
        for k in range(4):
            swaps[k].wait()
            fetch = pltpu.make_async_copy(land_ref.at[k], theirs, local_sems.at[4])
            fetch.start()
            loads[k].wait()
            fetch.wait()

            def add(i, carry, k=k):
                r = _rows(i, rows)
                own[k, r, :] = (own[k, r, :].astype(F32) + theirs[r, :].astype(F32)).astype(BF)
                return carry

            lax.fori_loop(0, nchunk, add, 0)
            stores[k].start()
        for cp in stores:
            cp.wait()

    return pl.pallas_call(
        body, name="rs_pair_swap_add",
        in_specs=[ANY], out_specs=[ANY, ANY],
        out_shape=[jax.ShapeDtypeStruct((4, WHALF, D), gwt.dtype), jax.ShapeDtypeStruct((4, WHALF, D), BF)],
        scratch_shapes=[pltpu.SemaphoreType.DMA((4,)), pltpu.SemaphoreType.DMA((4,)), pltpu.SemaphoreType.DMA((5,)),
                        pltpu.VMEM((4, WHALF, D), gwt.dtype), pltpu.VMEM((WHALF, D), gwt.dtype)],
        compiler_params=pltpu.CompilerParams(vmem_limit_bytes=VMEM_LIMIT),
    )(gwt)


def _pair_send(halves):
    n = len(halves)

    def body(*refs):
        ins, outs = refs[:n], refs[n:2 * n]
        send_sems, recv_sems = refs[2 * n:]
        x, y, c = _place()
        cps = [pltpu.make_async_remote_copy(
            src_ref=ins[a], dst_ref=outs[a], send_sem=send_sems.at[a], recv_sem=recv_sems.at[a],
            device_id=(x, y, 1 - c), device_id_type=MESH) for a in range(n)]
        for cp in cps:
            cp.start()
        for cp in cps:
            cp.wait()

    return pl.pallas_call(
        body, name="rs_pair_send",
        in_specs=[ANY] * n, out_specs=[ANY] * n,
        out_shape=[jax.ShapeDtypeStruct(h.shape, h.dtype) for h in halves],
        scratch_shapes=[pltpu.SemaphoreType.DMA((n,)), pltpu.SemaphoreType.DMA((n,))],
    )(*halves)


def _row_block(rows):
    for tb in (256, 128, 64, 32, 16, 8):
        if rows % tb == 0:
            return tb
    return rows


def _add_halves(full, recv, name, out_dtype):
    _, r2, w = recv.shape
    tb = _row_block(r2)
    nb = r2 // tb
    c = lax.axis_index("c")

    def body(c_ref, a_ref, b_ref, o_ref):
        o_ref[...] = (a_ref[...].astype(F32) + b_ref[...].astype(F32)).astype(o_ref.dtype)

    return pl.pallas_call(
        body, name=name,
        grid_spec=pltpu.PrefetchScalarGridSpec(
            num_scalar_prefetch=1, grid=(4, nb),
            in_specs=[pl.BlockSpec((1, tb, w), lambda s, i, cr: (s, cr[0] * nb + i, 0)),
                      pl.BlockSpec((1, tb, w), lambda s, i, cr: (s, i, 0))],
            out_specs=pl.BlockSpec((1, tb, w), lambda s, i, cr: (s, i, 0))),
        out_shape=jax.ShapeDtypeStruct(recv.shape, out_dtype),
        compiler_params=_params(("parallel", "parallel")),
    )(jnp.reshape(c, (1,)).astype(jnp.int32), full, recv)


def _add2(a, b, name):
    def body(a_ref, b_ref, o_ref):
        o_ref[...] = a_ref[...] + b_ref[...]

    return pl.pallas_call(body, name=name, out_shape=jax.ShapeDtypeStruct(a.shape, a.dtype))(a, b)


def _sum4(buf, own, name, exchange=None):
    _, r, w = buf.shape
    tb = _row_block(r)
    nsteps = r // tb
    me_s = 2 * lax.axis_index("x") + lax.axis_index("y")
    by_dest = own.ndim == 3
    carried = [] if exchange is None else [*exchange[0], exchange[1]]
    m = len(carried)

    def body(s_ref, b_ref, own_ref, *rest):
        o_ref = rest[m]
        if m:
            ins, outs, (send_sems, recv_sems) = rest[:m], rest[m + 1:2 * m + 1], rest[2 * m + 1:]
            cps = _chip_copies(ins[:-1], outs[:-1], send_sems, recv_sems, by_dest=True)
            cps += _chip_copies(ins[-1:], outs[-1:], send_sems.at[pl.ds(3 * (m - 1), 3)],
                                recv_sems.at[pl.ds(3 * (m - 1), 3)], by_dest=False)

            @pl.when(pl.program_id(0) == 0)
            def _():
                for cp in cps:
                    cp.start()

            @pl.when(pl.program_id(0) == nsteps - 1)
            def _():
                for cp in cps:
                    cp.wait()

        mine = (own_ref[0] if by_dest else own_ref[...]).astype(F32)
        terms = [jnp.where(s_ref[0] == t, mine, b_ref[t].astype(F32)) for t in range(4)]
        o_ref[...] = ((terms[0] + terms[1]) + terms[2]) + terms[3]

    own_spec = (pl.BlockSpec((1, tb, w), lambda i, sr: (sr[0], i, 0)) if by_dest
                else pl.BlockSpec((tb, w), lambda i, sr: (i, 0)))
    landing = [jax.ShapeDtypeStruct(p.shape, p.dtype) for p in carried[:-1]]
    landing += [jax.ShapeDtypeStruct((4,) + s.shape, s.dtype) for s in carried[-1:]]
    return pl.pallas_call(
        body, name=name,
        grid_spec=pltpu.PrefetchScalarGridSpec(
            num_scalar_prefetch=1, grid=(nsteps,),
            in_specs=[pl.BlockSpec((4, tb, w), lambda i, sr: (0, i, 0)), own_spec] + [ANY] * m,
            out_specs=[pl.BlockSpec((tb, w), lambda i, sr: (i, 0))] + [ANY] * m,
            scratch_shapes=[pltpu.SemaphoreType.DMA((3 * m,)), pltpu.SemaphoreType.DMA((3 * m,))] if m else []),
        out_shape=[jax.ShapeDtypeStruct((r, w), F32)] + landing,
        compiler_params=_params(("arbitrary" if m else "parallel",)),
    )(jnp.reshape(me_s, (1,)).astype(jnp.int32), buf, own, *carried)


def _adamw_math(w, g, m, v):
    mn = B1 * m + (1.0 - B1) * g
    vn = B2 * v + (1.0 - B2) * (g * g)
    m_hat = mn / (1.0 - B1 ** STEP)
    v_hat = vn / (1.0 - B2 ** STEP)
    return -LR * (m_hat / (jnp.sqrt(v_hat) + AEPS) + WD * w), mn, vn


def _adamw(w, g, m, v, name):
    r, c_ = w.shape
    tb = _row_block(r)
    if tb == r and r > 512:
        tb = 256

    def body(w_ref, g_ref, m_ref, v_ref, d_ref, mo_ref, vo_ref):
        d_ref[...], mo_ref[...], vo_ref[...] = _adamw_math(w_ref[...], g_ref[...], m_ref[...], v_ref[...])

    spec = pl.BlockSpec((tb, c_), lambda i: (i, 0))
    return pl.pallas_call(
        body, name=name, grid=(pl.cdiv(r, tb),),
        in_specs=[spec] * 4, out_specs=[spec] * 3,
        out_shape=[jax.ShapeDtypeStruct(w.shape, F32)] * 3,
        compiler_params=_params(("parallel",)),
    )(w, g, m, v)


def _adamw_rows(w, g_mine, g_sib, m, v, name):
    r = w.shape[0]
    tb = 256
    sub, lanes = w.shape[1:]
    nh = g_mine.shape[0] // tb
    nsteps = pl.cdiv(r, tb)
    assert nsteps <= 2 * nh and 4 * 3 + r <= 2 * nh * tb
    x, y, c = _place()
    place = jnp.stack([c, 4 * (2 * x + y)]).astype(jnp.int32)

    def body(p_ref, w_ref, mc_ref, sc_ref, mn_ref, sn_ref, m_ref, v_ref, go_ref, d_ref, mo_ref, vo_ref, buf):
        i = pl.program_id(0)
        for at, blk, mine_ref, sib_ref in ((0, i, mc_ref, sc_ref), (1, jnp.minimum(i + 1, 2 * nh - 1), mn_ref, sn_ref)):
            rows = jnp.where(blk // nh == p_ref[0], mine_ref[...], sib_ref[...])
            buf[tb * at:tb * (at + 1)] = rows.reshape(tb, sub, lanes)
        g = buf[pl.ds(p_ref[1], tb)]
        go_ref[...] = g
        d_ref[...], mo_ref[...], vo_ref[...] = _adamw_math(w_ref[...], g, m_ref[...], v_ref[...])

    def half_spec(ahead, sibling):
        def index(i, pr):
            half = (1 - pr[0]) if sibling else pr[0]
            return (jnp.clip(jnp.minimum(i + ahead, 2 * nh - 1) - nh * half, 0, nh - 1), 0)
        return pl.BlockSpec((tb, sub * lanes), index)

    spec = pl.BlockSpec((tb, sub, lanes), lambda i, pr: (i, 0, 0))
    return pl.pallas_call(
        body, name=name,
        grid_spec=pltpu.PrefetchScalarGridSpec(
            num_scalar_prefetch=1, grid=(nsteps,),
            in_specs=[spec, half_spec(0, False), half_spec(0, True), half_spec(1, False), half_spec(1, True),
                      spec, spec],
            out_specs=[spec] * 4,
            scratch_shapes=[pltpu.VMEM((2 * tb, sub, lanes), F32)]),
        out_shape=[jax.ShapeDtypeStruct(w.shape, F32)] * 4,
        compiler_params=_params(("parallel",)),
    )(place, w, g_mine, g_sib, g_mine, g_sib, m, v)


def _adamw_halves(w, g_mine, g_sib, m, v, name):
    r, c_ = w.shape
    r2 = g_mine.shape[0]
    tb = _row_block(r2)
    nb = r2 // tb
    c = lax.axis_index("c")

    def body(c_ref, w_ref, gm_ref, gs_ref, m_ref, v_ref, g_ref, d_ref, mo_ref, vo_ref):
        g = jnp.where(pl.program_id(0) == c_ref[0], gm_ref[...], gs_ref[...])
        g_ref[...] = g
        d_ref[...], mo_ref[...], vo_ref[...] = _adamw_math(w_ref[...], g, m_ref[...], v_ref[...])

    full = pl.BlockSpec((tb, c_), lambda h, i, cr: (h * nb + i, 0))
    half = pl.BlockSpec((tb, c_), lambda h, i, cr: (i, 0))
    return pl.pallas_call(
        body, name=name,
        grid_spec=pltpu.PrefetchScalarGridSpec(
            num_scalar_prefetch=1, grid=(2, nb),
            in_specs=[full, half, half, full, full], out_specs=[full] * 4),
        out_shape=[jax.ShapeDtypeStruct(w.shape, F32)] * 4,
        compiler_params=_params(("parallel", "parallel")),
    )(jnp.reshape(c, (1,)).astype(jnp.int32), w, g_mine, g_sib, m, v)


def kernel(x, meta_tokens, norm_g, w_in, b_f, w_out, final_g, loss_target, m_meta_tokens, m_norm_g, m_w_in, m_b_f, m_w_out, m_final_g, v_meta_tokens, v_norm_g, v_w_in, v_b_f, v_w_out, v_final_g):
    w3, m3, v3 = [jnp.transpose(jnp.reshape(t[0], (D // C, C, WSH)), (2, 0, 1)) for t in (w_in, m_w_in, v_w_in)]

    wt_main, laps, _, *normed = _gather_weights(_own_window(w3), meta_tokens, x[0], norm_g)
    wft = jnp.pad(laps[3, :NFF], ((0, C - NFF), (0, 0)))
    wout_own = w_out[0].astype(BF)

    loss, gx, dmeta, dng, gwt, dbf, dwout, dfg, (p_in, p_out), (e_in, e_out) = _local_step(
        x[0], loss_target[0], normed, norm_g, wt_main, wft, b_f, wout_own, final_g.reshape(1, D), True, True)

    g_meta = jnp.stack([dmeta[:, 256 * s:256 * (s + 1)] for s in range(4)])
    small = jnp.concatenate([dng, dfg, jnp.pad(dbf, ((0, 0), (0, D - NFF))),
                             jnp.pad(jnp.reshape(loss, (1, 1)), ((0, 0), (0, D - 1))),
                             jnp.zeros((4, D), F32)], axis=0)
    h_in, e_meta, e_small = _sum4(e_in, p_in, "sum_in", exchange=([g_meta], small))
    (h_out,), (h_meta,), (h_small,) = (_sum4(e_out, p_out, "sum_out"), _sum4(e_meta, g_meta, "sum_meta"),
                                       _sum4(e_small, small, "sum_small"))
    s_in, s_out, s_meta, s_small = _pair_send([h_in, h_out, h_meta, h_small])
    gw_meta = _add2(h_meta, s_meta, "pair_add_meta")
    tot = _add2(h_small, s_small, "pair_add_small")
    g_norm, g_final, g_bf, loss_all = tot[0:1], tot[1], tot[2:3, :NFF], tot[3, 0]

    d_meta, nm_meta, nv_meta = _adamw(meta_tokens, gw_meta, m_meta_tokens, v_meta_tokens, "adamw_meta")
    d_norm, nm_norm, nv_norm = _adamw(norm_g, g_norm, m_norm_g, v_norm_g, "adamw_norm")
    outs_in = _adamw_rows(w3, h_in, s_in, m3, v3, "adamw_in")
    gw_in, d_in, nm_in, nv_in = [jnp.reshape(jnp.transpose(t, (1, 2, 0)), (1, D, WSH)) for t in outs_in]
    d_bf, nm_bf, nv_bf = _adamw(b_f, g_bf, m_b_f, v_b_f, "adamw_bf")
    gw_out, d_out, nm_out, nv_out = _adamw_halves(w_out[0], h_out, s_out, m_w_out[0], v_w_out[0], "adamw_out")
    d_fin, nm_fin, nv_fin = _adamw(final_g.reshape(1, D), g_final.reshape(1, D), m_final_g.reshape(1, D),
                                   v_final_g.reshape(1, D), "adamw_final")
    return (loss_all, gx[None], gw_meta, g_norm, gw_in, g_bf, gw_out[None], g_final,
            d_meta, d_norm, d_in, d_bf, d_out[None], d_fin.reshape(D),
            nm_meta, nm_norm, nm_in, nm_bf, nm_out[None], nm_fin.reshape(D),
            nv_meta, nv_norm, nv_in, nv_bf, nv_out[None], nv_fin.reshape(D))
```
